```python
import jax, jax.numpy as jnp
from jax import lax
import numpy as np

D_MODEL = 1024
BATCH = 8
SEQ = 4096
DEPTH = 1

CHUNK = 64
A_HEADS = 8
A_EXPAND = 128
A_FDIM = A_HEADS * A_EXPAND
A_IDIM = D_MODEL
A_HEAD_I = A_IDIM // A_HEADS
B_EXPAND = 2
B_INNER = B_EXPAND * D_MODEL
B_HEADDIM = 64
B_HEADS = B_INNER // B_HEADDIM
B_GROUPS = 4
B_HG = B_HEADS // B_GROUPS
B_STATE = 128
B_CONV = 4
B_CONV_DIM = B_INNER + 2 * B_GROUPS * B_STATE
D_FF = -(-8 * D_MODEL // (3 * 256)) * 256
ALPHA = (2.0 * DEPTH) ** 0.25
BETA = (8.0 * DEPTH) ** -0.25
LN_EPS = 1e-5
RMS_EPS = 1e-6
IN_SPLITS = (A_FDIM, A_FDIM, A_IDIM, A_IDIM, B_INNER, B_CONV_DIM, B_HEADS, D_MODEL, D_MODEL)
IN_DIM = sum(IN_SPLITS)

kernel_name = "hybrid_hgrn2_mamba2_deepnorm_adaln"

F32 = jnp.float32


def layer_norm(x, g=None, b=None):
    x32 = x.astype(F32)
    mu = jnp.mean(x32, axis=-1, keepdims=True)
    xc = x32 - mu
    y = xc * lax.rsqrt(jnp.mean(xc * xc, axis=-1, keepdims=True) + LN_EPS)
    if g is not None:
        y = y * g.astype(F32) + b.astype(F32)
    return y.astype(x.dtype)


def rms_norm(x, w=None):
    x32 = x.astype(F32)
    y = x32 * lax.rsqrt(jnp.mean(x32 * x32, axis=-1, keepdims=True) + RMS_EPS)
    if w is not None:
        y = y * w.astype(F32)
    return y


def to_chunks(t):
    b, s = t.shape[:2]
    return jnp.moveaxis(t.reshape(b, s // CHUNK, CHUNK, *t.shape[2:]), 1, 0)


def from_chunks(t):
    t = jnp.moveaxis(t, 0, 1)
    return t.reshape(t.shape[0], t.shape[1] * t.shape[2], *t.shape[3:])


def causal_dwconv(x, w, b):
    k, ch = w.shape
    y = lax.conv_general_dilated(x, w[:, None, :], window_strides=(1,), padding=[(k - 1, 0)],
                                 dimension_numbers=("NWC", "WIO", "NWC"), feature_group_count=ch)
    return y + b


def hgrn2_mixer(q, f_logit, i, g, lb, w_gnorm):
    bsz, s, _ = q.shape
    f = lb + (1.0 - lb) * jax.nn.sigmoid(f_logit.astype(F32))
    log_f = jnp.log(f)
    k = 1.0 - f
    qf = jax.nn.silu(q.astype(F32)) * (A_EXPAND ** -0.5)
    hk = (bsz, s, A_HEADS, A_EXPAND)
    qh, kh, gh = qf.reshape(hk), k.reshape(hk), log_f.reshape(hk)
    vh = i.astype(F32).reshape(bsz, s, A_HEADS, A_HEAD_I)
    mask = jnp.tril(jnp.ones((CHUNK, CHUNK), bool))[None, :, :, None, None]

    def step(state, inp):
        qc, kc, gc, vc = inp
        bc = jnp.cumsum(gc, axis=1)
        diff = bc[:, :, None] - bc[:, None, :]
        decay = jnp.exp(jnp.where(mask, diff, -jnp.inf))
        scores = jnp.einsum("bthk,bshk,btshk->bhts", qc, kc, decay)
        o = jnp.einsum("bhts,bshv->bthv", scores, vc)
        o = o + jnp.einsum("bthk,bhkv->bthv", qc * jnp.exp(bc), state)
        b_last = bc[:, -1]
        state = state * jnp.exp(b_last)[..., None] + jnp.einsum(
            "bshk,bshv->bhkv", kc * jnp.exp(b_last[:, None] - bc), vc)
        return state, o

    s0 = jnp.zeros((bsz, A_HEADS, A_EXPAND, A_HEAD_I), F32)
    _, o = lax.scan(step, s0, (to_chunks(qh), to_chunks(kh), to_chunks(gh), to_chunks(vh)))
    o = from_chunks(o)
    o = rms_norm(o, w_gnorm) * jax.nn.silu(g.astype(F32)).reshape(o.shape)
    return o.reshape(bsz, s, A_IDIM)


def mamba2_mixer(z, xbc, dt, conv_w, conv_b, dt_bias, a_log, d_skip, w_norm):
    bsz, s, _ = z.shape
    xbc = jax.nn.silu(causal_dwconv(xbc, conv_w, conv_b)).astype(F32)
    xs = xbc[..., :B_INNER].reshape(bsz, s, B_GROUPS, B_HG, B_HEADDIM)
    bm = xbc[..., B_INNER:B_INNER + B_GROUPS * B_STATE].reshape(bsz, s, B_GROUPS, B_STATE)
    cm = xbc[..., B_INNER + B_GROUPS * B_STATE:].reshape(bsz, s, B_GROUPS, B_STATE)
    delta = jax.nn.softplus(dt.astype(F32) + dt_bias.astype(F32)).reshape(bsz, s, B_GROUPS, B_HG)
    a = -jnp.exp(a_log.astype(F32)).reshape(B_GROUPS, B_HG) * delta
    xdt = xs * delta[..., None]
    mask = jnp.tril(jnp.ones((CHUNK, CHUNK), bool))[None, :, :, None, None]

    def step(state, inp):
        xc, ac, bc, cc = inp
        acum = jnp.cumsum(ac, axis=1)
        seg = acum[:, :, None] - acum[:, None, :]
        decay = jnp.exp(jnp.where(mask, seg, -jnp.inf))
        cb = jnp.einsum("btgn,bsgn->btsg", cc, bc)
        y = jnp.einsum("btsg,btsgh,bsghp->btghp", cb, decay, xc)
        y = y + jnp.einsum("btgn,bghpn->btghp", cc, state) * jnp.exp(acum)[..., None]
        a_last = acum[:, -1]
        state = state * jnp.exp(a_last)[..., None, None] + jnp.einsum(
            "bsgn,bsgh,bsghp->bghpn", bc, jnp.exp(a_last[:, None] - acum), xc)
        return state, y

    s0 = jnp.zeros((bsz, B_GROUPS, B_HG, B_HEADDIM, B_STATE), F32)
    _, y = lax.scan(step, s0, (to_chunks(xdt), to_chunks(a), to_chunks(bm), to_chunks(cm)))
    y = from_chunks(y) + xs * d_skip.astype(F32).reshape(B_GROUPS, B_HG)[..., None]
    y = y.reshape(bsz, s, B_INNER) * jax.nn.silu(z.astype(F32))
    y = rms_norm(y.reshape(bsz, s, B_GROUPS, B_INNER // B_GROUPS)).reshape(bsz, s, B_INNER)
    return y * w_norm.astype(F32)


def token_mixer(u, w_in, lb, gnorm, conv_w, conv_b, dt_bias, a_log, d_skip, ssm_norm,
                w_branch_a, w_branch_b, w_o):
    proj = u @ w_in
    offs = np.cumsum(IN_SPLITS)[:-1].tolist()
    q, f, i, g, z, xbc, dt, gate_a, gate_b = jnp.split(proj, offs, axis=-1)
    y_a = hgrn2_mixer(q, f, i, g, lb, gnorm).astype(u.dtype) @ w_branch_a
    y_b = mamba2_mixer(z, xbc, dt, conv_w, conv_b, dt_bias, a_log, d_skip,
                       ssm_norm).astype(u.dtype) @ w_branch_b
    merged = jax.nn.sigmoid(gate_a) * y_a + jax.nn.sigmoid(gate_b) * y_b
    return merged @ w_o


def swiglu(u, w_gate, w_up, w_down):
    return (jax.nn.silu(u @ w_gate) * (u @ w_up)) @ w_down


def _fwd_setup_inputs(seed: int = 0) -> dict:
    key = jax.random.key(seed)
    ks = jax.random.split(key, 24)
    n = lambda k, shape, s: jax.random.normal(k, shape, F32) * s
    L, D = DEPTH, D_MODEL
    dt0 = jnp.exp(jax.random.uniform(ks[8], (L, B_HEADS), F32, np.log(1e-3), np.log(1e-1)))
    return {
        "x": jax.random.normal(ks[0], (BATCH, SEQ, D), F32),
        "c": jax.random.normal(ks[1], (BATCH, D), F32),
        "w_ada": n(ks[2], (L, D, 6 * D), D ** -0.5),
        "b_ada": n(ks[3], (L, 6 * D), 0.02),
        "w_in": n(ks[4], (L, D, IN_DIM), D ** -0.5),
        "hgrn_lb": n(ks[5], (DEPTH + 1, A_FDIM), 0.1),
        "hgrn_gnorm": 1.0 + n(ks[6], (L, A_HEAD_I), 0.02),
        "ssm_conv_w": n(ks[7], (L, B_CONV, B_CONV_DIM), B_CONV ** -0.5),
        "ssm_conv_b": n(ks[9], (L, B_CONV_DIM), 0.02),
        "ssm_dt_bias": dt0 + jnp.log(-jnp.expm1(-dt0)),
        "ssm_a_log": jnp.log(jax.random.uniform(ks[10], (L, B_HEADS), F32, 1.0, 16.0)),
        "ssm_d": 1.0 + n(ks[11], (L, B_HEADS), 0.02),
        "ssm_norm": 1.0 + n(ks[12], (L, B_INNER), 0.02),
        "w_branch_a": n(ks[13], (L, A_IDIM, D), A_IDIM ** -0.5),
        "w_branch_b": n(ks[14], (L, B_INNER, D), B_INNER ** -0.5),
        "w_o": n(ks[15], (L, D, D), BETA * D ** -0.5),
        "ln1_g": 1.0 + n(ks[16], (L, D), 0.02),
        "ln1_b": n(ks[17], (L, D), 0.02),
        "w_ffn_gate": n(ks[18], (L, D, D_FF), D ** -0.5),
        "w_ffn_up": n(ks[19], (L, D, D_FF), D ** -0.5),
        "w_ffn_down": n(ks[20], (L, D_FF, D), BETA * D_FF ** -0.5),
        "ln2_g": 1.0 + n(ks[21], (L, D), 0.02),
        "ln2_b": n(ks[22], (L, D), 0.02),
    }


def _fwd_reference(x, c, w_ada, b_ada, w_in, hgrn_lb, hgrn_gnorm, ssm_conv_w, ssm_conv_b,
              ssm_dt_bias, ssm_a_log, ssm_d, ssm_norm, w_branch_a, w_branch_b, w_o,
              ln1_g, ln1_b, w_ffn_gate, w_ffn_up, w_ffn_down, ln2_g, ln2_b):
    cond = jax.nn.silu(c)
    lb_table = jnp.cumsum(jax.nn.softmax(hgrn_lb.astype(F32), axis=0), axis=0)
    for l in range(DEPTH):
        mod = (cond @ w_ada[l] + b_ada[l])[:, None, :]
        sh1, sc1, g1, sh2, sc2, g2 = jnp.split(mod, 6, axis=-1)
        u = layer_norm(x) * (1.0 + sc1) + sh1
        h = token_mixer(u, w_in[l], lb_table[l], hgrn_gnorm[l], ssm_conv_w[l], ssm_conv_b[l],
                        ssm_dt_bias[l], ssm_a_log[l], ssm_d[l], ssm_norm[l],
                        w_branch_a[l], w_branch_b[l], w_o[l])
        x = layer_norm(ALPHA * x + g1 * h, ln1_g[l], ln1_b[l])
        u = layer_norm(x) * (1.0 + sc2) + sh2
        h = swiglu(u, w_ffn_gate[l], w_ffn_up[l], w_ffn_down[l])
        x = layer_norm(ALPHA * x + g2 * h, ln2_g[l], ln2_b[l])
    return x


import jax as _jax
import jax.numpy as _jnp

TWIN_FORMAT = 'train_step'
FWD_PARAMS = ['x', 'c', 'w_ada', 'b_ada', 'w_in', 'hgrn_lb', 'hgrn_gnorm', 'ssm_conv_w', 'ssm_conv_b', 'ssm_dt_bias', 'ssm_a_log', 'ssm_d', 'ssm_norm', 'w_branch_a', 'w_branch_b', 'w_o', 'ln1_g', 'ln1_b', 'w_ffn_gate', 'w_ffn_up', 'w_ffn_down', 'ln2_g', 'ln2_b']
TWIN_WEIGHTS = ['w_ada', 'b_ada', 'w_in', 'hgrn_lb', 'hgrn_gnorm', 'ssm_conv_w', 'ssm_conv_b', 'ssm_dt_bias', 'ssm_a_log', 'ssm_d', 'ssm_norm', 'w_branch_a', 'w_branch_b', 'w_o', 'ln1_g', 'ln1_b', 'w_ffn_gate', 'w_ffn_up', 'w_ffn_down', 'ln2_g', 'ln2_b']
TWIN_DIFF_INPUT = 'x'
TWIN_INPUTS = ['x', 'c', 'w_ada', 'b_ada', 'w_in', 'hgrn_lb', 'hgrn_gnorm', 'ssm_conv_w', 'ssm_conv_b', 'ssm_dt_bias', 'ssm_a_log', 'ssm_d', 'ssm_norm', 'w_branch_a', 'w_branch_b', 'w_o', 'ln1_g', 'ln1_b', 'w_ffn_gate', 'w_ffn_up', 'w_ffn_down', 'ln2_g', 'ln2_b', 'loss_target', 'm_w_ada', 'm_b_ada', 'm_w_in', 'm_hgrn_lb', 'm_hgrn_gnorm', 'm_ssm_conv_w', 'm_ssm_conv_b', 'm_ssm_dt_bias', 'm_ssm_a_log', 'm_ssm_d', 'm_ssm_norm', 'm_w_branch_a', 'm_w_branch_b', 'm_w_o', 'm_ln1_g', 'm_ln1_b', 'm_w_ffn_gate', 'm_w_ffn_up', 'm_w_ffn_down', 'm_ln2_g', 'm_ln2_b', 'v_w_ada', 'v_b_ada', 'v_w_in', 'v_hgrn_lb', 'v_hgrn_gnorm', 'v_ssm_conv_w', 'v_ssm_conv_b', 'v_ssm_dt_bias', 'v_ssm_a_log', 'v_ssm_d', 'v_ssm_norm', 'v_w_branch_a', 'v_w_branch_b', 'v_w_o', 'v_ln1_g', 'v_ln1_b', 'v_w_ffn_gate', 'v_w_ffn_up', 'v_w_ffn_down', 'v_ln2_g', 'v_ln2_b']
TWIN_OUTPUTS = ['loss', 'grad_x', 'grad_w_ada', 'grad_b_ada', 'grad_w_in', 'grad_hgrn_lb', 'grad_hgrn_gnorm', 'grad_ssm_conv_w', 'grad_ssm_conv_b', 'grad_ssm_dt_bias', 'grad_ssm_a_log', 'grad_ssm_d', 'grad_ssm_norm', 'grad_w_branch_a', 'grad_w_branch_b', 'grad_w_o', 'grad_ln1_g', 'grad_ln1_b', 'grad_w_ffn_gate', 'grad_w_ffn_up', 'grad_w_ffn_down', 'grad_ln2_g', 'grad_ln2_b', 'delta_w_ada', 'delta_b_ada', 'delta_w_in', 'delta_hgrn_lb', 'delta_hgrn_gnorm', 'delta_ssm_conv_w', 'delta_ssm_conv_b', 'delta_ssm_dt_bias', 'delta_ssm_a_log', 'delta_ssm_d', 'delta_ssm_norm', 'delta_w_branch_a', 'delta_w_branch_b', 'delta_w_o', 'delta_ln1_g', 'delta_ln1_b', 'delta_w_ffn_gate', 'delta_w_ffn_up', 'delta_w_ffn_down', 'delta_ln2_g', 'delta_ln2_b', 'new_m_w_ada', 'new_m_b_ada', 'new_m_w_in', 'new_m_hgrn_lb', 'new_m_hgrn_gnorm', 'new_m_ssm_conv_w', 'new_m_ssm_conv_b', 'new_m_ssm_dt_bias', 'new_m_ssm_a_log', 'new_m_ssm_d', 'new_m_ssm_norm', 'new_m_w_branch_a', 'new_m_w_branch_b', 'new_m_w_o', 'new_m_ln1_g', 'new_m_ln1_b', 'new_m_w_ffn_gate', 'new_m_w_ffn_up', 'new_m_w_ffn_down', 'new_m_ln2_g', 'new_m_ln2_b', 'new_v_w_ada', 'new_v_b_ada', 'new_v_w_in', 'new_v_hgrn_lb', 'new_v_hgrn_gnorm', 'new_v_ssm_conv_w', 'new_v_ssm_conv_b', 'new_v_ssm_dt_bias', 'new_v_ssm_a_log', 'new_v_ssm_d', 'new_v_ssm_norm', 'new_v_w_branch_a', 'new_v_w_branch_b', 'new_v_w_o', 'new_v_ln1_g', 'new_v_ln1_b', 'new_v_w_ffn_gate', 'new_v_w_ffn_up', 'new_v_w_ffn_down', 'new_v_ln2_g', 'new_v_ln2_b']
TWIN_LEAF_KINDS = {'loss': 'loss', 'grad_x': 'grad_x', 'grad_w_ada': 'grad_w', 'grad_b_ada': 'grad_w', 'grad_w_in': 'grad_w', 'grad_hgrn_lb': 'grad_w', 'grad_hgrn_gnorm': 'grad_w', 'grad_ssm_conv_w': 'grad_w', 'grad_ssm_conv_b': 'grad_w', 'grad_ssm_dt_bias': 'grad_w', 'grad_ssm_a_log': 'grad_w', 'grad_ssm_d': 'grad_w', 'grad_ssm_norm': 'grad_w', 'grad_w_branch_a': 'grad_w', 'grad_w_branch_b': 'grad_w', 'grad_w_o': 'grad_w', 'grad_ln1_g': 'grad_w', 'grad_ln1_b': 'grad_w', 'grad_w_ffn_gate': 'grad_w', 'grad_w_ffn_up': 'grad_w', 'grad_w_ffn_down': 'grad_w', 'grad_ln2_g': 'grad_w', 'grad_ln2_b': 'grad_w', 'delta_w_ada': 'delta_w', 'delta_b_ada': 'delta_w', 'delta_w_in': 'delta_w', 'delta_hgrn_lb': 'delta_w', 'delta_hgrn_gnorm': 'delta_w', 'delta_ssm_conv_w': 'delta_w', 'delta_ssm_conv_b': 'delta_w', 'delta_ssm_dt_bias': 'delta_w', 'delta_ssm_a_log': 'delta_w', 'delta_ssm_d': 'delta_w', 'delta_ssm_norm': 'delta_w', 'delta_w_branch_a': 'delta_w', 'delta_w_branch_b': 'delta_w', 'delta_w_o': 'delta_w', 'delta_ln1_g': 'delta_w', 'delta_ln1_b': 'delta_w', 'delta_w_ffn_gate': 'delta_w', 'delta_w_ffn_up': 'delta_w', 'delta_w_ffn_down': 'delta_w', 'delta_ln2_g': 'delta_w', 'delta_ln2_b': 'delta_w', 'new_m_w_ada': 'new_m', 'new_m_b_ada': 'new_m', 'new_m_w_in': 'new_m', 'new_m_hgrn_lb': 'new_m', 'new_m_hgrn_gnorm': 'new_m', 'new_m_ssm_conv_w': 'new_m', 'new_m_ssm_conv_b': 'new_m', 'new_m_ssm_dt_bias': 'new_m', 'new_m_ssm_a_log': 'new_m', 'new_m_ssm_d': 'new_m', 'new_m_ssm_norm': 'new_m', 'new_m_w_branch_a': 'new_m', 'new_m_w_branch_b': 'new_m', 'new_m_w_o': 'new_m', 'new_m_ln1_g': 'new_m', 'new_m_ln1_b': 'new_m', 'new_m_w_ffn_gate': 'new_m', 'new_m_w_ffn_up': 'new_m', 'new_m_w_ffn_down': 'new_m', 'new_m_ln2_g': 'new_m', 'new_m_ln2_b': 'new_m', 'new_v_w_ada': 'new_v', 'new_v_b_ada': 'new_v', 'new_v_w_in': 'new_v', 'new_v_hgrn_lb': 'new_v', 'new_v_hgrn_gnorm': 'new_v', 'new_v_ssm_conv_w': 'new_v', 'new_v_ssm_conv_b': 'new_v', 'new_v_ssm_dt_bias': 'new_v', 'new_v_ssm_a_log': 'new_v', 'new_v_ssm_d': 'new_v', 'new_v_ssm_norm': 'new_v', 'new_v_w_branch_a': 'new_v', 'new_v_w_branch_b': 'new_v', 'new_v_w_o': 'new_v', 'new_v_ln1_g': 'new_v', 'new_v_ln1_b': 'new_v', 'new_v_w_ffn_gate': 'new_v', 'new_v_w_ffn_up': 'new_v', 'new_v_w_ffn_down': 'new_v', 'new_v_ln2_g': 'new_v', 'new_v_ln2_b': 'new_v'}


def _forward(args):
    return _fwd_reference(*[args[k] for k in FWD_PARAMS])


def _output_shape():
    out = _jax.eval_shape(lambda: _forward(_fwd_setup_inputs(0)))
    return out.shape, out.dtype

N_MICROBATCH = 1
ADAM_LR = 0.001
ADAM_B1 = 0.9
ADAM_B2 = 0.999
ADAM_EPS = 1e-08
ADAM_WD = 0.01
ADAM_STEP = 10
PER_EXAMPLE_BATCH_AXIS = {'x': 0, 'c': 0, 'loss_target': 0}
SHARED_INPUTS = []
_WEIGHT_DTYPES = {'w_ada': _jnp.float32, 'b_ada': _jnp.float32, 'w_in': _jnp.float32, 'hgrn_lb': _jnp.float32, 'hgrn_gnorm': _jnp.float32, 'ssm_conv_w': _jnp.float32, 'ssm_conv_b': _jnp.float32, 'ssm_dt_bias': _jnp.float32, 'ssm_a_log': _jnp.float32, 'ssm_d': _jnp.float32, 'ssm_norm': _jnp.float32, 'w_branch_a': _jnp.float32, 'w_branch_b': _jnp.float32, 'w_o': _jnp.float32, 'ln1_g': _jnp.float32, 'ln1_b': _jnp.float32, 'w_ffn_gate': _jnp.float32, 'w_ffn_up': _jnp.float32, 'w_ffn_down': _jnp.float32, 'ln2_g': _jnp.float32, 'ln2_b': _jnp.float32}
MOMENT_SCALE = {'w_ada': 4.491001e-02, 'b_ada': 7.676749e-02, 'w_in': 1.890841e-02, 'hgrn_lb': 1.502948e-03, 'hgrn_gnorm': 6.980811e-02, 'ssm_conv_w': 1.908713e-02, 'ssm_conv_b': 2.105746e-02, 'ssm_dt_bias': 6.277553e-02, 'ssm_a_log': 1.062508e-01, 'ssm_d': 1.092046e-01, 'ssm_norm': 2.368337e-02, 'w_branch_a': 2.700304e-02, 'w_branch_b': 3.346041e-02, 'w_o': 7.241297e-02, 'ln1_g': 9.162105e-01, 'ln1_b': 4.187229e-01, 'w_ffn_gate': 3.554040e-02, 'w_ffn_up': 3.479975e-02, 'w_ffn_down': 9.685206e-02, 'ln2_g': 3.212499e+01, 'ln2_b': 1.916390e+00}


def _to_microbatches(a, axis):
    t = _jnp.moveaxis(a, axis, 0)
    t = t.reshape((N_MICROBATCH, t.shape[0] // N_MICROBATCH) + t.shape[1:])
    return _jnp.moveaxis(t, 1, axis + 1)


def setup_inputs(seed: int = 0) -> dict:
    inp = _fwd_setup_inputs(seed)
    key = _jax.random.fold_in(_jax.random.key(seed), 7919)
    shape, _ = _output_shape()
    out = dict(inp)
    out["loss_target"] = _jax.random.normal(_jax.random.fold_in(key, 0), shape, _jnp.float32)
    for i, name in enumerate(TWIN_WEIGHTS):
        w = inp[name].astype(_jnp.float32)
        if MOMENT_SCALE is None:
            s = _jnp.sqrt(_jnp.mean(_jnp.square(w)) + 1e-30)
        else:
            s = MOMENT_SCALE[name]
        km, kv = _jax.random.split(_jax.random.fold_in(key, i + 1))
        out[name] = w
        out["m_" + name] = s * _jax.random.normal(km, w.shape, _jnp.float32)
        out["v_" + name] = (s * s) * _jax.random.uniform(kv, w.shape, _jnp.float32, 0.5, 1.5)
    if N_MICROBATCH > 1:
        for name, axis in PER_EXAMPLE_BATCH_AXIS.items():
            out[name] = _to_microbatches(out[name], axis)
    return {'x': out['x'], 'c': out['c'], 'w_ada': out['w_ada'], 'b_ada': out['b_ada'], 'w_in': out['w_in'], 'hgrn_lb': out['hgrn_lb'], 'hgrn_gnorm': out['hgrn_gnorm'], 'ssm_conv_w': out['ssm_conv_w'], 'ssm_conv_b': out['ssm_conv_b'], 'ssm_dt_bias': out['ssm_dt_bias'], 'ssm_a_log': out['ssm_a_log'], 'ssm_d': out['ssm_d'], 'ssm_norm': out['ssm_norm'], 'w_branch_a': out['w_branch_a'], 'w_branch_b': out['w_branch_b'], 'w_o': out['w_o'], 'ln1_g': out['ln1_g'], 'ln1_b': out['ln1_b'], 'w_ffn_gate': out['w_ffn_gate'], 'w_ffn_up': out['w_ffn_up'], 'w_ffn_down': out['w_ffn_down'], 'ln2_g': out['ln2_g'], 'ln2_b': out['ln2_b'], 'loss_target': out['loss_target'], 'm_w_ada': out['m_w_ada'], 'm_b_ada': out['m_b_ada'], 'm_w_in': out['m_w_in'], 'm_hgrn_lb': out['m_hgrn_lb'], 'm_hgrn_gnorm': out['m_hgrn_gnorm'], 'm_ssm_conv_w': out['m_ssm_conv_w'], 'm_ssm_conv_b': out['m_ssm_conv_b'], 'm_ssm_dt_bias': out['m_ssm_dt_bias'], 'm_ssm_a_log': out['m_ssm_a_log'], 'm_ssm_d': out['m_ssm_d'], 'm_ssm_norm': out['m_ssm_norm'], 'm_w_branch_a': out['m_w_branch_a'], 'm_w_branch_b': out['m_w_branch_b'], 'm_w_o': out['m_w_o'], 'm_ln1_g': out['m_ln1_g'], 'm_ln1_b': out['m_ln1_b'], 'm_w_ffn_gate': out['m_w_ffn_gate'], 'm_w_ffn_up': out['m_w_ffn_up'], 'm_w_ffn_down': out['m_w_ffn_down'], 'm_ln2_g': out['m_ln2_g'], 'm_ln2_b': out['m_ln2_b'], 'v_w_ada': out['v_w_ada'], 'v_b_ada': out['v_b_ada'], 'v_w_in': out['v_w_in'], 'v_hgrn_lb': out['v_hgrn_lb'], 'v_hgrn_gnorm': out['v_hgrn_gnorm'], 'v_ssm_conv_w': out['v_ssm_conv_w'], 'v_ssm_conv_b': out['v_ssm_conv_b'], 'v_ssm_dt_bias': out['v_ssm_dt_bias'], 'v_ssm_a_log': out['v_ssm_a_log'], 'v_ssm_d': out['v_ssm_d'], 'v_ssm_norm': out['v_ssm_norm'], 'v_w_branch_a': out['v_w_branch_a'], 'v_w_branch_b': out['v_w_branch_b'], 'v_w_o': out['v_w_o'], 'v_ln1_g': out['v_ln1_g'], 'v_ln1_b': out['v_ln1_b'], 'v_w_ffn_gate': out['v_w_ffn_gate'], 'v_w_ffn_up': out['v_w_ffn_up'], 'v_w_ffn_down': out['v_w_ffn_down'], 'v_ln2_g': out['v_ln2_g'], 'v_ln2_b': out['v_ln2_b']}


def _loss(weights, diff, rest, loss_target):
    with _jax.named_scope("forward"):
        args = {**rest, TWIN_DIFF_INPUT: diff, **{k: w.astype(_WEIGHT_DTYPES[k]) for k, w in weights.items()}}
        y = _forward(args)
    with _jax.named_scope("loss_head"):
        err = _jnp.square(y.astype(_jnp.float32) - loss_target)
        return 0.5 * _jnp.sum(_jnp.mean(err, axis=-1)) if err.ndim else 0.5 * err


def _adamw(w, g, m, v):
    m = ADAM_B1 * m + (1.0 - ADAM_B1) * g
    v = ADAM_B2 * v + (1.0 - ADAM_B2) * _jnp.square(g)
    m_hat = m / (1.0 - ADAM_B1 ** ADAM_STEP)
    v_hat = v / (1.0 - ADAM_B2 ** ADAM_STEP)
    delta = -ADAM_LR * (m_hat / (_jnp.sqrt(v_hat) + ADAM_EPS) + ADAM_WD * w)
    return delta, m, v


def reference(x, c, w_ada, b_ada, w_in, hgrn_lb, hgrn_gnorm, ssm_conv_w, ssm_conv_b, ssm_dt_bias, ssm_a_log, ssm_d, ssm_norm, w_branch_a, w_branch_b, w_o, ln1_g, ln1_b, w_ffn_gate, w_ffn_up, w_ffn_down, ln2_g, ln2_b, loss_target, m_w_ada, m_b_ada, m_w_in, m_hgrn_lb, m_hgrn_gnorm, m_ssm_conv_w, m_ssm_conv_b, m_ssm_dt_bias, m_ssm_a_log, m_ssm_d, m_ssm_norm, m_w_branch_a, m_w_branch_b, m_w_o, m_ln1_g, m_ln1_b, m_w_ffn_gate, m_w_ffn_up, m_w_ffn_down, m_ln2_g, m_ln2_b, v_w_ada, v_b_ada, v_w_in, v_hgrn_lb, v_hgrn_gnorm, v_ssm_conv_w, v_ssm_conv_b, v_ssm_dt_bias, v_ssm_a_log, v_ssm_d, v_ssm_norm, v_w_branch_a, v_w_branch_b, v_w_o, v_ln1_g, v_ln1_b, v_w_ffn_gate, v_w_ffn_up, v_w_ffn_down, v_ln2_g, v_ln2_b):
    given = dict(x=x, c=c, w_ada=w_ada, b_ada=b_ada, w_in=w_in, hgrn_lb=hgrn_lb, hgrn_gnorm=hgrn_gnorm, ssm_conv_w=ssm_conv_w, ssm_conv_b=ssm_conv_b, ssm_dt_bias=ssm_dt_bias, ssm_a_log=ssm_a_log, ssm_d=ssm_d, ssm_norm=ssm_norm, w_branch_a=w_branch_a, w_branch_b=w_branch_b, w_o=w_o, ln1_g=ln1_g, ln1_b=ln1_b, w_ffn_gate=w_ffn_gate, w_ffn_up=w_ffn_up, w_ffn_down=w_ffn_down, ln2_g=ln2_g, ln2_b=ln2_b, loss_target=loss_target, m_w_ada=m_w_ada, m_b_ada=m_b_ada, m_w_in=m_w_in, m_hgrn_lb=m_hgrn_lb, m_hgrn_gnorm=m_hgrn_gnorm, m_ssm_conv_w=m_ssm_conv_w, m_ssm_conv_b=m_ssm_conv_b, m_ssm_dt_bias=m_ssm_dt_bias, m_ssm_a_log=m_ssm_a_log, m_ssm_d=m_ssm_d, m_ssm_norm=m_ssm_norm, m_w_branch_a=m_w_branch_a, m_w_branch_b=m_w_branch_b, m_w_o=m_w_o, m_ln1_g=m_ln1_g, m_ln1_b=m_ln1_b, m_w_ffn_gate=m_w_ffn_gate, m_w_ffn_up=m_w_ffn_up, m_w_ffn_down=m_w_ffn_down, m_ln2_g=m_ln2_g, m_ln2_b=m_ln2_b, v_w_ada=v_w_ada, v_b_ada=v_b_ada, v_w_in=v_w_in, v_hgrn_lb=v_hgrn_lb, v_hgrn_gnorm=v_hgrn_gnorm, v_ssm_conv_w=v_ssm_conv_w, v_ssm_conv_b=v_ssm_conv_b, v_ssm_dt_bias=v_ssm_dt_bias, v_ssm_a_log=v_ssm_a_log, v_ssm_d=v_ssm_d, v_ssm_norm=v_ssm_norm, v_w_branch_a=v_w_branch_a, v_w_branch_b=v_w_branch_b, v_w_o=v_w_o, v_ln1_g=v_ln1_g, v_ln1_b=v_ln1_b, v_w_ffn_gate=v_w_ffn_gate, v_w_ffn_up=v_w_ffn_up, v_w_ffn_down=v_w_ffn_down, v_ln2_g=v_ln2_g, v_ln2_b=v_ln2_b)
    weights = {n: given[n] for n in TWIN_WEIGHTS}
    shared = {n: given[n] for n in SHARED_INPUTS}
    per_example = {n: given[n] for n in ['x', 'c']}
    grad_fn = _jax.value_and_grad(_loss, argnums=(0, 1))

    def one_microbatch(ex, loss_target):
        ex = dict(ex)
        diff = ex.pop(TWIN_DIFF_INPUT)
        return grad_fn(weights, diff, {**shared, **ex}, loss_target)

    if N_MICROBATCH == 1:
        loss, (grad_w, grad_x) = one_microbatch(per_example, given["loss_target"])
    else:
        def body(carry, xs):
            loss_sum, grad_sum = carry
            l_k, (gw_k, gx_k) = one_microbatch(xs[0], xs[1])
            with _jax.named_scope("update"):
                return (loss_sum + l_k, _jax.tree.map(_jnp.add, grad_sum, gw_k)), gx_k

        init = (_jnp.zeros((), _jnp.float32), _jax.tree.map(_jnp.zeros_like, weights))
        (loss, grad_w), grad_x = _jax.lax.scan(body, init, (per_example, given["loss_target"]))
    with _jax.named_scope("update"):
        delta_w, new_m, new_v = {}, {}, {}
        for n in TWIN_WEIGHTS:
            delta_w[n], new_m[n], new_v[n] = _adamw(weights[n], grad_w[n], given["m_" + n], given["v_" + n])
    return (loss, grad_x, *[grad_w[n] for n in TWIN_WEIGHTS], *[delta_w[n] for n in TWIN_WEIGHTS],
            *[new_m[n] for n in TWIN_WEIGHTS], *[new_v[n] for n in TWIN_WEIGHTS])
```

```python
import functools

import numpy as np
import jax
import jax.numpy as jnp
from jax import lax
from jax.experimental import pallas as pl
from jax.experimental.pallas import tpu as pltpu

F32 = jnp.float32
BF16 = jnp.bfloat16
HI = lax.Precision.HIGHEST

N_DEV = 8
D = 1024
N_HEADS_A = 8
HK = 128
CHUNK = 64
N_GROUPS = 4
HEADS_PER_GROUP = 8
HEAD_P = 64
N_STATE = 128
GROUP_W = HEADS_PER_GROUP * HEAD_P
B_INNER = 2048
CONV_DIM = 3072
D_FF = 2816
D_FF_PAD = 3072
IN_DIM = 11296
N_PROJ = 12288
ALPHA = 2.0 ** 0.25
LN_EPS = 1e-5
RMS_EPS = 1e-6
Q_SCALE = 128 ** -0.5
EXP_CLIP = 80.0
ADAM_LR, ADAM_B1, ADAM_B2, ADAM_EPS, ADAM_WD, ADAM_STEP = 0.001, 0.9, 0.999, 1e-8, 0.01, 10
VMEM_LIMIT = 48 * 1024 * 1024
TOKEN_BLOCK = 512
ROW_TILE = 256
FFN_ROW_TILE = 128
MM_ROW_TILE = 1024
MM_TOKEN_TILE = 512
MESH_ID = pl.DeviceIdType.MESH

NT_DIMS = (((1,), (1,)), ((), ()))
TN_DIMS = (((0,), (0,)), ((), ()))


def _cparams(sem=None):
    return pltpu.CompilerParams(dimension_semantics=sem, vmem_limit_bytes=VMEM_LIMIT)


def _sigmoid(x):
    return 1.0 / (1.0 + jnp.exp(-x))


def _dsilu(x, s):
    return s * (1.0 + x * (1.0 - s))


def _nt(a, b, precision=None):
    return lax.dot_general(a, b, NT_DIMS, precision=precision, preferred_element_type=F32)


def _tn(a, b, precision=None):
    return lax.dot_general(a, b, TN_DIMS, precision=precision, preferred_element_type=F32)


def _nn(a, b, precision=None):
    return jnp.dot(a, b, precision=precision, preferred_element_type=F32)


def _ln(x):
    mu = jnp.mean(x, axis=-1, keepdims=True)
    xc = x - mu
    rstd = lax.rsqrt(jnp.mean(xc * xc, axis=-1, keepdims=True) + LN_EPS)
    return xc * rstd, rstd


def _ln_bwd(dxh, xh, rstd):
    return rstd * (dxh - jnp.mean(dxh, axis=-1, keepdims=True) - xh * jnp.mean(dxh * xh, axis=-1, keepdims=True))


def _colsum(x):
    return jnp.sum(x, axis=0, keepdims=True)


def _tri(n, upper=False):
    r = lax.broadcasted_iota(jnp.int32, (n, n), 0)
    c = lax.broadcasted_iota(jnp.int32, (n, n), 1)
    return (c >= r) if upper else (r >= c)


def _my_pos():
    return lax.axis_index("x"), lax.axis_index("y"), lax.axis_index("c")


def _peer(pos, k):
    x, y, c = pos
    return (x ^ ((k >> 2) & 1), y ^ ((k >> 1) & 1), c ^ (k & 1))


def _flat(pos):
    return 4 * pos[0] + 2 * pos[1] + pos[2]


def allgather_vmem(v, name):
    n = v.shape[1]

    def body(v_ref, o_ref, send_sems, recv_sems, local_sem):
        me = _my_pos()
        mine = pltpu.make_async_copy(v_ref, o_ref.at[_flat(me)], local_sem)
        mine.start()
        sends = []
        for k in range(1, N_DEV):
            peer = _peer(me, k)
            cp = pltpu.make_async_remote_copy(v_ref, o_ref.at[_flat(me)], send_sems.at[k - 1], recv_sems.at[k - 1],
                                              device_id=peer, device_id_type=MESH_ID)
            cp.start()
            sends.append(cp)
        for k in range(1, N_DEV):
            peer = _peer(me, k)
            pltpu.make_async_remote_copy(v_ref, o_ref.at[_flat(peer)], send_sems.at[k - 1], recv_sems.at[k - 1],
                                         device_id=peer, device_id_type=MESH_ID).wait_recv()
        for cp in sends:
            cp.wait_send()
        mine.wait()

    return pl.pallas_call(
        body, name=name,
        out_shape=jax.ShapeDtypeStruct((N_DEV, 1, n), F32),
        in_specs=[pl.BlockSpec(memory_space=pltpu.VMEM)],
        out_specs=pl.BlockSpec(memory_space=pltpu.VMEM),
        scratch_shapes=[pltpu.SemaphoreType.DMA((N_DEV - 1,)), pltpu.SemaphoreType.DMA((N_DEV - 1,)),
                        pltpu.SemaphoreType.DMA],
        compiler_params=_cparams(),
    )(v)


def ada_modulation(c_all, w_ada_s, b_ada_r):
    ncol = w_ada_s.shape[1]

    def body(c_ref, w_ref, b_ref, o_ref, part_ref, send_sems, recv_sems):
        me = _my_pos()
        cval = c_ref[...]
        cond = cval * _sigmoid(cval)
        part = _nn(cond, w_ref[...], HI)
        for r in range(N_DEV):
            part_ref[r] = part[r:r + 1, :]
        sends = []
        for k in range(1, N_DEV):
            peer = _peer(me, k)
            cp = pltpu.make_async_remote_copy(part_ref.at[_flat(peer)], o_ref.at[_flat(me)], send_sems.at[k - 1],
                                              recv_sems.at[k - 1], device_id=peer, device_id_type=MESH_ID)
            cp.start()
            sends.append(cp)
        o_ref[_flat(me)] = part_ref[_flat(me)]
        for k in range(1, N_DEV):
            peer = _peer(me, k)
            pltpu.make_async_remote_copy(part_ref.at[_flat(peer)], o_ref.at[_flat(peer)], send_sems.at[k - 1],
                                         recv_sems.at[k - 1], device_id=peer, device_id_type=MESH_ID).wait_recv()
        for cp in sends:
            cp.wait_send()
        o_ref[...] = o_ref[...] + b_ref[...]

    return pl.pallas_call(
        body, name="ada_modulation",
        out_shape=jax.ShapeDtypeStruct((N_DEV, 1, ncol), F32),
        in_specs=[pl.BlockSpec(memory_space=pltpu.VMEM)] * 3,
        out_specs=pl.BlockSpec(memory_space=pltpu.VMEM),
        scratch_shapes=[pltpu.VMEM((N_DEV, 1, ncol), F32), pltpu.SemaphoreType.DMA((N_DEV - 1,)),
                        pltpu.SemaphoreType.DMA((N_DEV - 1,))],
        compiler_params=_cparams(),
    )(c_all, w_ada_s, b_ada_r)


def allgather_hbm(shards, name):
    nw = len(shards)

    def body(*refs):
        ins, outs = refs[:nw], refs[nw:2 * nw]
        send_sems, recv_sems, local_sems = refs[2 * nw:]
        me = _my_pos()
        local = []
        for i in range(nw):
            cp = pltpu.make_async_copy(ins[i], outs[i].at[_flat(me)], local_sems.at[i])
            cp.start()
            local.append(cp)
        sends = []
        for k in range(1, N_DEV):
            peer = _peer(me, k)
            for i in range(nw):
                cp = pltpu.make_async_remote_copy(ins[i], outs[i].at[_flat(me)], send_sems.at[i, k - 1],
                                                  recv_sems.at[i, k - 1], device_id=peer, device_id_type=MESH_ID)
                cp.start()
                sends.append(cp)
        for k in range(1, N_DEV):
            peer = _peer(me, k)
            for i in range(nw):
                pltpu.make_async_remote_copy(ins[i], outs[i].at[_flat(peer)], send_sems.at[i, k - 1],
                                             recv_sems.at[i, k - 1], device_id=peer, device_id_type=MESH_ID).wait_recv()
        for cp in sends:
            cp.wait_send()
        for cp in local:
            cp.wait()

    return pl.pallas_call(
        body, name=name,
        out_shape=[jax.ShapeDtypeStruct((N_DEV,) + s.shape, s.dtype) for s in shards],
        in_specs=[pl.BlockSpec(memory_space=pl.ANY)] * nw,
        out_specs=[pl.BlockSpec(memory_space=pl.ANY)] * nw,
        scratch_shapes=[pltpu.SemaphoreType.DMA((nw, N_DEV - 1)), pltpu.SemaphoreType.DMA((nw, N_DEV - 1)),
                        pltpu.SemaphoreType.DMA((nw,))],
        compiler_params=_cparams(),
    )(*shards)


def exchange_hbm(blocks, name):
    nw = len(blocks)

    def body(*refs):
        ins, outs = refs[:nw], refs[nw:2 * nw]
        send_sems, recv_sems, local_sems = refs[2 * nw:]
        me = _my_pos()
        local = []
        for i in range(nw):
            cp = pltpu.make_async_copy(ins[i].at[_flat(me)], outs[i].at[_flat(me)], local_sems.at[i])
            cp.start()
            local.append(cp)
        sends = []
        for k in range(1, N_DEV):
            peer = _peer(me, k)
            for i in range(nw):
                cp = pltpu.make_async_remote_copy(ins[i].at[_flat(peer)], outs[i].at[_flat(me)], send_sems.at[i, k - 1],
                                                  recv_sems.at[i, k - 1], device_id=peer, device_id_type=MESH_ID)
                cp.start()
                sends.append(cp)
        for k in range(1, N_DEV):
            peer = _peer(me, k)
            for i in range(nw):
                pltpu.make_async_remote_copy(ins[i].at[_flat(peer)], outs[i].at[_flat(peer)], send_sems.at[i, k - 1],
                                             recv_sems.at[i, k - 1], device_id=peer, device_id_type=MESH_ID).wait_recv()
        for cp in sends:
            cp.wait_send()
        for cp in local:
            cp.wait()

    return pl.pallas_call(
        body, name=name,
        out_shape=[jax.ShapeDtypeStruct(b.shape, b.dtype) for b in blocks],
        in_specs=[pl.BlockSpec(memory_space=pl.ANY)] * nw,
        out_specs=[pl.BlockSpec(memory_space=pl.ANY)] * nw,
        scratch_shapes=[pltpu.SemaphoreType.DMA((nw, N_DEV - 1)), pltpu.SemaphoreType.DMA((nw, N_DEV - 1)),
                        pltpu.SemaphoreType.DMA((nw,))],
        compiler_params=_cparams(),
    )(*blocks)


def mm_nn(a, b, out_dtype, name):
    m, kdim = a.shape
    n = b.shape[1]
    tm, tn, tk = min(MM_ROW_TILE, m), 1024, 1024
    nk = kdim // tk

    def body(a_ref, b_ref, o_ref, acc_ref):
        p = _nn(a_ref[...], b_ref[...])
        if nk == 1:
            o_ref[...] = p.astype(o_ref.dtype)
        else:
            k = pl.program_id(2)

            @pl.when(k == 0)
            def _():
                acc_ref[...] = p

            @pl.when(k > 0)
            def _():
                acc_ref[...] += p

            @pl.when(k == nk - 1)
            def _():
                o_ref[...] = acc_ref[...].astype(o_ref.dtype)

    return pl.pallas_call(
        body, name=name, grid=(n // tn, m // tm, nk),
        out_shape=jax.ShapeDtypeStruct((m, n), out_dtype),
        in_specs=[pl.BlockSpec((tm, tk), lambda j, i, k: (i, k)), pl.BlockSpec((tk, tn), lambda j, i, k: (k, j))],
        out_specs=pl.BlockSpec((tm, tn), lambda j, i, k: (i, j)),
        scratch_shapes=[pltpu.VMEM((tm, tn), F32)],
        compiler_params=_cparams(("parallel", "parallel", "arbitrary")),
    )(a, b)


def mm_tn(a, b, name):
    t, ka = a.shape
    n = b.shape[1]
    tt, tka, tn = min(MM_TOKEN_TILE, t), 1024, 1024

    def body(a_ref, b_ref, o_ref):
        p = _tn(a_ref[...], b_ref[...])
        s = pl.program_id(2)

        @pl.when(s == 0)
        def _():
            o_ref[...] = p

        @pl.when(s > 0)
        def _():
            o_ref[...] += p

    return pl.pallas_call(
        body, name=name, grid=(ka // tka, n // tn, t // tt),
        out_shape=jax.ShapeDtypeStruct((ka, n), F32),
        in_specs=[pl.BlockSpec((tt, tka), lambda i, j, s: (s, i)), pl.BlockSpec((tt, tn), lambda i, j, s: (s, j))],
        out_specs=pl.BlockSpec((tka, tn), lambda i, j, s: (i, j)),
        compiler_params=_cparams(("parallel", "parallel", "arbitrary")),
    )(a, b)


def _tile(t, cap):
    return min(cap, t)


def ln_modulate(x, mod6, shift_row, scale_row, name):
    t = x.shape[0]
    tm = _tile(t, ROW_TILE)

    def body(x_ref, mod_ref, o_ref):
        xh, _ = _ln(x_ref[...])
        sc = mod_ref[scale_row:scale_row + 1, :]
        sh = mod_ref[shift_row:shift_row + 1, :]
        o_ref[...] = (xh * (1.0 + sc) + sh).astype(BF16)

    return pl.pallas_call(
        body, name=name, grid=(t // tm,),
        out_shape=jax.ShapeDtypeStruct((t, D), BF16),
        in_specs=[pl.BlockSpec((tm, D), lambda i: (i, 0)), pl.BlockSpec((6, D), lambda i: (0, 0))],
        out_specs=pl.BlockSpec((tm, D), lambda i: (i, 0)),
        compiler_params=_cparams(("parallel",)),
    )(x, mod6)


def resid_ln(x, h, mod6, gate_row, ln_g, ln_b, name):
    t = x.shape[0]
    tm = _tile(t, ROW_TILE)

    def body(x_ref, h_ref, mod_ref, g_ref, b_ref, o_ref):
        r = ALPHA * x_ref[...] + mod_ref[gate_row:gate_row + 1, :] * h_ref[...]
        rh, _ = _ln(r)
        o_ref[...] = rh * g_ref[...] + b_ref[...]

    row = pl.BlockSpec((tm, D), lambda i: (i, 0))
    vec = pl.BlockSpec((1, D), lambda i: (0, 0))
    return pl.pallas_call(
        body, name=name, grid=(t // tm,),
        out_shape=jax.ShapeDtypeStruct((t, D), F32),
        in_specs=[row, row, pl.BlockSpec((6, D), lambda i: (0, 0)), vec, vec],
        out_specs=row,
        compiler_params=_cparams(("parallel",)),
    )(x, h, mod6, ln_g, ln_b)


def resid_ln_bwd(x, h, mod6, gate_row, ln_g, ln_b, cot, with_loss, name):
    t = x.shape[0]
    tm = _tile(t, ROW_TILE)

    def body(x_ref, h_ref, mod_ref, g_ref, b_ref, c_ref, dh_ref, dx_ref, acc_ref):
        @pl.when(pl.program_id(0) == 0)
        def _():
            acc_ref[...] = jnp.zeros_like(acc_ref)

        gate = mod_ref[gate_row:gate_row + 1, :]
        hv = h_ref[...]
        r = ALPHA * x_ref[...] + gate * hv
        rh, rstd = _ln(r)
        lng = g_ref[...]
        if with_loss:
            diff = rh * lng + b_ref[...] - c_ref[...]
            dxo = diff * (1.0 / D)
            lsum = jnp.sum(_colsum(diff * diff), axis=-1, keepdims=True) * (0.5 / D)
            acc_ref[3:4, :] += jnp.broadcast_to(lsum, (1, D))
        else:
            dxo = c_ref[...]
        acc_ref[1:2, :] += _colsum(dxo * rh)
        acc_ref[2:3, :] += _colsum(dxo)
        dr = _ln_bwd(dxo * lng, rh, rstd)
        acc_ref[0:1, :] += _colsum(dr * hv)
        dh_ref[...] = (gate * dr).astype(BF16)
        dx_ref[...] = ALPHA * dr

    row = pl.BlockSpec((tm, D), lambda i: (i, 0))
    vec = pl.BlockSpec((1, D), lambda i: (0, 0))
    return pl.pallas_call(
        body, name=name, grid=(t // tm,),
        out_shape=[jax.ShapeDtypeStruct((t, D), BF16), jax.ShapeDtypeStruct((t, D), F32),
                   jax.ShapeDtypeStruct((8, D), F32)],
        in_specs=[row, row, pl.BlockSpec((6, D), lambda i: (0, 0)), vec, vec, row],
        out_specs=[row, row, pl.BlockSpec((8, D), lambda i: (0, 0))],
        compiler_params=_cparams(("arbitrary",)),
    )(x, h, mod6, ln_g, ln_b, cot)


def ln_modulate_bwd(x, du, mod6, scale_row, dx_part, name):
    t = x.shape[0]
    tm = _tile(t, ROW_TILE)

    def body(x_ref, du_ref, mod_ref, dp_ref, dx_ref, acc_ref):
        @pl.when(pl.program_id(0) == 0)
        def _():
            acc_ref[...] = jnp.zeros_like(acc_ref)

        xh, rstd = _ln(x_ref[...])
        du_v = du_ref[...]
        sc = mod_ref[scale_row:scale_row + 1, :]
        acc_ref[0:1, :] += _colsum(du_v * xh)
        acc_ref[1:2, :] += _colsum(du_v)
        dx_ref[...] = dp_ref[...] + _ln_bwd(du_v * (1.0 + sc), xh, rstd)

    row = pl.BlockSpec((tm, D), lambda i: (i, 0))
    return pl.pallas_call(
        body, name=name, grid=(t // tm,),
        out_shape=[jax.ShapeDtypeStruct((t, D), F32), jax.ShapeDtypeStruct((8, D), F32)],
        in_specs=[row, row, pl.BlockSpec((6, D), lambda i: (0, 0)), row],
        out_specs=[row, pl.BlockSpec((8, D), lambda i: (0, 0))],
        compiler_params=_cparams(("arbitrary",)),
    )(x, du, mod6, dx_part)


def merge_gates(ya, yb, proj):
    t = ya.shape[0]
    tm = _tile(t, ROW_TILE)

    def body(ya_ref, yb_ref, ga_ref, gb_ref, o_ref):
        o_ref[...] = (_sigmoid(ga_ref[...]) * ya_ref[...] + _sigmoid(gb_ref[...]) * yb_ref[...]).astype(BF16)

    row = pl.BlockSpec((tm, D), lambda i: (i, 0))
    return pl.pallas_call(
        body, name="merge_gates", grid=(t // tm,),
        out_shape=jax.ShapeDtypeStruct((t, D), BF16),
        in_specs=[row, row, pl.BlockSpec((tm, D), lambda i: (i, 9)), pl.BlockSpec((tm, D), lambda i: (i, 10))],
        out_specs=row,
        compiler_params=_cparams(("parallel",)),
    )(ya, yb, proj, proj)


def merge_gates_bwd(dm, ya, yb, proj):
    t = ya.shape[0]
    tm = _tile(t, ROW_TILE)

    def body(dm_ref, ya_ref, yb_ref, ga_ref, gb_ref, dya_ref, dyb_ref, dga_ref, dgb_ref):
        dmv = dm_ref[...]
        sa = _sigmoid(ga_ref[...])
        sb = _sigmoid(gb_ref[...])
        dya_ref[...] = (dmv * sa).astype(BF16)
        dyb_ref[...] = (dmv * sb).astype(BF16)
        dga_ref[...] = (dmv * ya_ref[...] * sa * (1.0 - sa)).astype(BF16)
        dgb_ref[...] = (dmv * yb_ref[...] * sb * (1.0 - sb)).astype(BF16)

    row = pl.BlockSpec((tm, D), lambda i: (i, 0))
    return pl.pallas_call(
        body, name="merge_gates_bwd", grid=(t // tm,),
        out_shape=[jax.ShapeDtypeStruct((t, D), BF16)] * 4,
        in_specs=[row, row, row, pl.BlockSpec((tm, D), lambda i: (i, 9)), pl.BlockSpec((tm, D), lambda i: (i, 10))],
        out_specs=[row] * 4,
        compiler_params=_cparams(("parallel",)),
    )(dm, ya, yb, proj, proj)


def swiglu_act(gu):
    t = gu.shape[0]
    tm = _tile(t, FFN_ROW_TILE)

    def body(gu_ref, o_ref):
        for j in range(D_FF_PAD // D):
            g = gu_ref[:, j * D:(j + 1) * D]
            u = gu_ref[:, D_FF_PAD + j * D:D_FF_PAD + (j + 1) * D]
            o_ref[:, j * D:(j + 1) * D] = (g * _sigmoid(g) * u).astype(BF16)

    return pl.pallas_call(
        body, name="swiglu_act", grid=(t // tm,),
        out_shape=jax.ShapeDtypeStruct((t, D_FF_PAD), BF16),
        in_specs=[pl.BlockSpec((tm, 2 * D_FF_PAD), lambda i: (i, 0))],
        out_specs=pl.BlockSpec((tm, D_FF_PAD), lambda i: (i, 0)),
        compiler_params=_cparams(("parallel",)),
    )(gu)


def swiglu_act_bwd(gu, dact):
    t = gu.shape[0]
    tm = _tile(t, FFN_ROW_TILE)

    def body(gu_ref, da_ref, o_ref):
        for j in range(D_FF_PAD // D):
            g = gu_ref[:, j * D:(j + 1) * D]
            u = gu_ref[:, D_FF_PAD + j * D:D_FF_PAD + (j + 1) * D]
            da = da_ref[:, j * D:(j + 1) * D]
            s = _sigmoid(g)
            o_ref[:, j * D:(j + 1) * D] = (da * u * _dsilu(g, s)).astype(BF16)
            o_ref[:, D_FF_PAD + j * D:D_FF_PAD + (j + 1) * D] = (da * g * s).astype(BF16)

    return pl.pallas_call(
        body, name="swiglu_act_bwd", grid=(t // tm,),
        out_shape=jax.ShapeDtypeStruct((t, 2 * D_FF_PAD), BF16),
        in_specs=[pl.BlockSpec((tm, 2 * D_FF_PAD), lambda i: (i, 0)), pl.BlockSpec((tm, D_FF_PAD), lambda i: (i, 0))],
        out_specs=pl.BlockSpec((tm, 2 * D_FF_PAD), lambda i: (i, 0)),
        compiler_params=_cparams(("parallel",)),
    )(gu, dact)


def _hgrn_chunk_terms(q, fl, lbv, tril_f):
    sig = _sigmoid(fl)
    f = lbv + (1.0 - lbv) * sig
    lam = jnp.log(f)
    k = 1.0 - f
    sq = _sigmoid(q)
    qt = q * sq * Q_SCALE
    bc = _nn(tril_f, lam, HI)
    bmid = bc[CHUNK // 2 - 1:CHUNK // 2, :]
    bl = bc[CHUNK - 1:CHUNK, :]
    eq = jnp.exp(jnp.minimum(bc - bmid, EXP_CLIP))
    ek = jnp.exp(jnp.minimum(bmid - bc, EXP_CLIP))
    eb = jnp.exp(bc)
    ekl = jnp.exp(bl - bc)
    ebl = jnp.exp(bl)
    return sig, f, k, sq, qt, eq, ek, eb, ekl, ebl


def hgrn_fwd(proj, lb, gnorm):
    t = proj.shape[0]
    tb = _tile(t, TOKEN_BLOCK)
    ncb = tb // CHUNK

    def body(q_ref, f_ref, i_ref, g_ref, lb_ref, gn_ref, oa_ref, oraw_ref, st_ref, state):
        @pl.when(pl.program_id(1) == 0)
        def _():
            state[...] = jnp.zeros_like(state)

        lbv = lb_ref[...]
        gn = gn_ref[...]
        mask = _tri(CHUNK)
        tril_f = mask.astype(F32)

        def chunk(c, carry):
            sl = pl.ds(pl.multiple_of(c * CHUNK, CHUNK), CHUNK)
            q, fl, v, g = q_ref[sl, :], f_ref[sl, :], i_ref[sl, :], g_ref[sl, :]
            sig, f, k, sq, qt, eq, ek, eb, ekl, ebl = _hgrn_chunk_terms(q, fl, lbv, tril_f)
            a = jnp.where(mask, _nt((qt * eq).astype(BF16), (k * ek).astype(BF16)), 0.0)
            st = state[...]
            st_ref[0, c] = st
            vb = v.astype(BF16)
            o = _nn(a.astype(BF16), vb) + _nt((qt * eb).astype(BF16), st.astype(BF16))
            state[...] = st * ebl + _tn(vb, (k * ekl).astype(BF16))
            oraw_ref[sl, :] = o
            rn = o * lax.rsqrt(jnp.mean(o * o, axis=-1, keepdims=True) + RMS_EPS)
            oa_ref[sl, :] = (rn * gn * g * _sigmoid(g)).astype(BF16)
            return carry

        lax.fori_loop(0, ncb, chunk, 0)

    def col(block):
        return pl.BlockSpec((tb, HK), lambda h, j: (j, block * N_HEADS_A + h))

    return pl.pallas_call(
        body, name="hgrn_fwd", grid=(N_HEADS_A, t // tb),
        out_shape=[jax.ShapeDtypeStruct((t, D), BF16), jax.ShapeDtypeStruct((t, D), F32),
                   jax.ShapeDtypeStruct((N_HEADS_A, t // CHUNK, HK, HK), F32)],
        in_specs=[col(0), col(1), col(2), col(3), pl.BlockSpec((1, HK), lambda h, j: (0, h)),
                  pl.BlockSpec((1, HK), lambda h, j: (0, 0))],
        out_specs=[pl.BlockSpec((tb, HK), lambda h, j: (j, h)), pl.BlockSpec((tb, HK), lambda h, j: (j, h)),
                   pl.BlockSpec((1, ncb, HK, HK), lambda h, j: (h, j, 0, 0))],
        scratch_shapes=[pltpu.VMEM((HK, HK), F32)],
        compiler_params=_cparams(("parallel", "arbitrary")),
    )(proj, proj, proj, proj, lb, gnorm)


def hgrn_bwd(proj, lb, gnorm, o_raw, doa, states):
    t = proj.shape[0]
    tb = _tile(t, TOKEN_BLOCK)
    ncb = tb // CHUNK
    nb = t // tb

    def body(q_ref, f_ref, i_ref, g_ref, lb_ref, gn_ref, oraw_ref, doa_ref, st_ref,
             dq_ref, df_ref, di_ref, dg_ref, dlb_ref, dgn_ref, dstate):
        h, j = pl.program_id(0), pl.program_id(1)

        @pl.when(j == 0)
        def _():
            dstate[...] = jnp.zeros_like(dstate)
            dlb_ref[...] = jnp.zeros_like(dlb_ref)

        @pl.when((j == 0) & (h == 0))
        def _():
            dgn_ref[...] = jnp.zeros_like(dgn_ref)

        lbv = lb_ref[...]
        gn = gn_ref[...]
        mask = _tri(CHUNK)
        mask_t = _tri(CHUNK, upper=True)
        tril_f = mask.astype(F32)
        triu_f = mask_t.astype(F32)

        def chunk(i, c0):
            c = ncb - 1 - i
            sl = pl.ds(pl.multiple_of(c * CHUNK, CHUNK), CHUNK)
            q, fl, v, g = q_ref[sl, :], f_ref[sl, :], i_ref[sl, :], g_ref[sl, :]
            sig, f, k, sq, qt, eq, ek, eb, ekl, ebl = _hgrn_chunk_terms(q, fl, lbv, tril_f)
            qe = (qt * eq).astype(BF16)
            ke = (k * ek).astype(BF16)
            st32 = st_ref[0, c]
            st = st32.astype(BF16)
            dst = dstate[...]
            dstb = dst.astype(BF16)
            o = oraw_ref[sl, :]
            rstd = lax.rsqrt(jnp.mean(o * o, axis=-1, keepdims=True) + RMS_EPS)
            rn = o * rstd
            sgm = _sigmoid(g)
            sg = g * sgm
            doa_v = doa_ref[sl, :]
            drn = doa_v * gn * sg
            dgn_ref[...] += _colsum(doa_v * rn * sg)
            dg_ref[sl, :] = (doa_v * rn * gn * _dsilu(g, sgm)).astype(BF16)
            do = rstd * (drn - rn * jnp.mean(drn * rn, axis=-1, keepdims=True))
            dob = do.astype(BF16)
            vb = v.astype(BF16)
            da = jnp.where(mask, _nt(dob, vb), 0.0).astype(BF16)
            da_t = jnp.where(mask_t, _nt(vb, dob), 0.0).astype(BF16)
            a_t = jnp.where(mask_t, _nt(ke, qe), 0.0).astype(BF16)
            kl = (k * ekl).astype(BF16)
            qb = (qt * eb).astype(BF16)
            dq_in = _nn(da, ke)
            dk_in = _nn(da_t, qe)
            dq_out = eb * _nn(dob, st)
            dk_out = ekl * _nn(vb, dstb)
            dqt = eq * dq_in + dq_out
            dk = ek * dk_in + dk_out
            dv = _nn(a_t, dob) + _nt(kl, dstb)
            dstate[...] = dst * ebl + _tn(dob, qb)
            dbig = qe.astype(F32) * dq_in - ke.astype(F32) * dk_in + qt * dq_out - k * dk_out
            beyond = _colsum(k * dk_out) + ebl * _colsum(dst * st32)
            dlam = _nn(triu_f, dbig, HI) + beyond
            df = dlam / f - dk
            df_ref[sl, :] = (df * (1.0 - lbv) * sig * (1.0 - sig)).astype(BF16)
            dlb_ref[...] += _colsum(df * (1.0 - sig))
            dq_ref[sl, :] = (dqt * Q_SCALE * _dsilu(q, sq)).astype(BF16)
            di_ref[sl, :] = dv.astype(BF16)
            return c0

        lax.fori_loop(0, ncb, chunk, 0)

    def col(block):
        return pl.BlockSpec((tb, HK), lambda h, j: (nb - 1 - j, block * N_HEADS_A + h))

    hcol = pl.BlockSpec((tb, HK), lambda h, j: (nb - 1 - j, h))
    return pl.pallas_call(
        body, name="hgrn_bwd", grid=(N_HEADS_A, nb),
        out_shape=[jax.ShapeDtypeStruct((t, D), BF16)] * 4 + [jax.ShapeDtypeStruct((1, D), F32),
                                                                jax.ShapeDtypeStruct((1, HK), F32)],
        in_specs=[col(0), col(1), col(2), col(3), pl.BlockSpec((1, HK), lambda h, j: (0, h)),
                  pl.BlockSpec((1, HK), lambda h, j: (0, 0)), hcol, hcol,
                  pl.BlockSpec((1, ncb, HK, HK), lambda h, j: (h, nb - 1 - j, 0, 0))],
        out_specs=[hcol] * 4 + [pl.BlockSpec((1, HK), lambda h, j: (0, h)), pl.BlockSpec((1, HK), lambda h, j: (0, 0))],
        scratch_shapes=[pltpu.VMEM((HK, HK), F32)],
        compiler_params=_cparams(("arbitrary", "arbitrary")),
    )(proj, proj, proj, proj, lb, gnorm, o_raw, doa, states)


CONV_BLOCK0 = 6
CONV_TAPS = 4
HALO = 8


def conv_fwd(proj, conv_w, conv_b):
    t = proj.shape[0]
    tm = _tile(t, ROW_TILE)
    r = tm // HALO

    def body(x_ref, halo_ref, w_ref, b_ref, o_ref):
        i = pl.program_id(1)
        halo = jnp.where(i > 0, halo_ref[...], 0.0)
        ext = jnp.concatenate([halo, x_ref[...]], axis=0)
        pre = b_ref[...] + w_ref[CONV_TAPS - 1:CONV_TAPS, :] * ext[HALO:, :]
        for tap in range(CONV_TAPS - 1):
            pre = pre + w_ref[tap:tap + 1, :] * pltpu.roll(ext, CONV_TAPS - 1 - tap, axis=0)[HALO:, :]
        o_ref[...] = pre * _sigmoid(pre)

    return pl.pallas_call(
        body, name="conv_fwd", grid=(CONV_DIM // D, t // tm),
        out_shape=jax.ShapeDtypeStruct((t, CONV_DIM), F32),
        in_specs=[pl.BlockSpec((tm, D), lambda cb, i: (i, CONV_BLOCK0 + cb)),
                  pl.BlockSpec((HALO, D), lambda cb, i: (jnp.maximum(i * r - 1, 0), CONV_BLOCK0 + cb)),
                  pl.BlockSpec((CONV_TAPS, D), lambda cb, i: (0, cb)), pl.BlockSpec((1, D), lambda cb, i: (0, cb))],
        out_specs=pl.BlockSpec((tm, D), lambda cb, i: (i, cb)),
        compiler_params=_cparams(("parallel", "parallel")),
    )(proj, proj, conv_w, conv_b)


def conv_bwd(proj, dxc, conv_w, conv_b):
    t = proj.shape[0]
    tm = _tile(t, ROW_TILE)
    r = tm // HALO
    n = t // tm
    last_halo = t // HALO - 1

    def body(x_ref, prev_ref, next_ref, d_ref, dnext_ref, w_ref, b_ref, dx_ref, dw_ref, db_ref):
        i = pl.program_id(1)

        @pl.when(i == 0)
        def _():
            dw_ref[...] = jnp.zeros_like(dw_ref)
            db_ref[...] = jnp.zeros_like(db_ref)

        prev = jnp.where(i > 0, prev_ref[...], 0.0)
        ext = jnp.concatenate([prev, x_ref[...], next_ref[...]], axis=0)
        shifted = [pltpu.roll(ext, CONV_TAPS - 1 - tap, axis=0)[HALO:, :] for tap in range(CONV_TAPS - 1)]
        shifted.append(ext[HALO:, :])
        pre = b_ref[...]
        for tap in range(CONV_TAPS):
            pre = pre + w_ref[tap:tap + 1, :] * shifted[tap]
        s = _sigmoid(pre)
        d_ext = jnp.concatenate([d_ref[...], jnp.where(i < n - 1, dnext_ref[...], 0.0)], axis=0)
        dpre = d_ext * _dsilu(pre, s)
        dx = w_ref[CONV_TAPS - 1:CONV_TAPS, :] * dpre[:tm, :]
        for tap in range(CONV_TAPS - 1):
            back = CONV_TAPS - 1 - tap
            dx = dx + w_ref[tap:tap + 1, :] * pltpu.roll(dpre, tm + HALO - back, axis=0)[:tm, :]
        dx_ref[...] = dx.astype(BF16)
        dp = dpre[:tm, :]
        db_ref[...] += _colsum(dp)
        for tap in range(CONV_TAPS):
            dw_ref[tap:tap + 1, :] += _colsum(dp * shifted[tap][:tm, :])

    return pl.pallas_call(
        body, name="conv_bwd", grid=(CONV_DIM // D, n),
        out_shape=[jax.ShapeDtypeStruct((t, CONV_DIM), BF16), jax.ShapeDtypeStruct((8, CONV_DIM), F32),
                   jax.ShapeDtypeStruct((1, CONV_DIM), F32)],
        in_specs=[pl.BlockSpec((tm, D), lambda cb, i: (i, CONV_BLOCK0 + cb)),
                  pl.BlockSpec((HALO, D), lambda cb, i: (jnp.maximum(i * r - 1, 0), CONV_BLOCK0 + cb)),
                  pl.BlockSpec((HALO, D), lambda cb, i: (jnp.minimum((i + 1) * r, last_halo), CONV_BLOCK0 + cb)),
                  pl.BlockSpec((tm, D), lambda cb, i: (i, cb)),
                  pl.BlockSpec((HALO, D), lambda cb, i: (jnp.minimum((i + 1) * r, last_halo), cb)),
                  pl.BlockSpec((CONV_TAPS, D), lambda cb, i: (0, cb)), pl.BlockSpec((1, D), lambda cb, i: (0, cb))],
        out_specs=[pl.BlockSpec((tm, D), lambda cb, i: (i, cb)), pl.BlockSpec((8, D), lambda cb, i: (0, cb)),
                   pl.BlockSpec((1, D), lambda cb, i: (0, cb))],
        compiler_params=_cparams(("parallel", "arbitrary")),
    )(proj, proj, proj, dxc, dxc, conv_w, conv_b)


Z_BLOCK0 = 8
DT_BLOCK0 = 88
B_BLOCK0 = 16
C_BLOCK0 = 20


def _head_expand():
    e = np.zeros((N_STATE, GROUP_W), np.float32)
    for hh in range(HEADS_PER_GROUP):
        e[hh, hh * HEAD_P:(hh + 1) * HEAD_P] = 1.0
    return jnp.asarray(e)


def _ssd_chunk_terms(dt, bias, alog, expand, tril_f, eye):
    dtb = dt + bias
    delta = jnp.maximum(dtb, 0.0) + jnp.log(1.0 + jnp.exp(-jnp.abs(dtb)))
    ea = jnp.exp(alog)
    a = -ea * delta
    acum = _nn(tril_f, a, HI)
    delta_e = _nn(delta, expand, HI)
    acum_e = _nn(acum, expand, HI)
    acum_t = _nt(eye, acum, HI)
    return dtb, delta, ea, a, acum, delta_e, acum_e, acum_t


def ssd_fwd(proj, xc, alog4, bias4, dskip4, wnorm, expand):
    t = proj.shape[0]
    tb = _tile(t, TOKEN_BLOCK)
    ncb = tb // CHUNK

    def body(xs_ref, b_ref, c_ref, dt_ref, z_ref, alog_ref, bias_ref, dsk_ref, wn_ref, e_ref, ob_ref, st_ref, state):
        @pl.when(pl.program_id(1) == 0)
        def _():
            state[...] = jnp.zeros_like(state)

        expand = e_ref[...]
        mask = _tri(CHUNK)
        tril_f = mask.astype(F32)
        eye = (lax.broadcasted_iota(jnp.int32, (N_STATE, N_STATE), 0) ==
               lax.broadcasted_iota(jnp.int32, (N_STATE, N_STATE), 1)).astype(F32)
        alog, bias = alog_ref[0], bias_ref[0]
        d_e = _nn(jnp.broadcast_to(dsk_ref[0], (8, N_STATE)), expand, HI)[0:1, :]
        wn = wn_ref[...]

        def chunk(c, carry):
            sl = pl.ds(pl.multiple_of(c * CHUNK, CHUNK), CHUNK)
            xs, bm, cm, dt, z = xs_ref[sl, :], b_ref[sl, :], c_ref[sl, :], dt_ref[sl, :], z_ref[sl, :]
            dtb, delta, ea, a, acum, delta_e, acum_e, acum_t = _ssd_chunk_terms(dt, bias, alog, expand, tril_f, eye)
            alast_e = acum_e[CHUNK - 1:CHUNK, :]
            xd = xs * delta_e
            xdb = xd.astype(BF16)
            cb_, bb_ = cm.astype(BF16), bm.astype(BF16)
            cbm = _nt(cb_, bb_)
            ys = []
            for hh in range(HEADS_PER_GROUP):
                lh = jnp.where(mask, jnp.exp(jnp.minimum(acum[:, hh:hh + 1] - acum_t[hh:hh + 1, :], 0.0)), 0.0)
                ys.append(_nn((cbm * lh).astype(BF16), xdb[:, hh * HEAD_P:(hh + 1) * HEAD_P]))
            st = state[...]
            st_ref[0, c] = st
            y = jnp.concatenate(ys, axis=1) + _nn(cb_, st.astype(BF16)) * jnp.exp(acum_e) + xs * d_e
            state[...] = st * jnp.exp(alast_e) + _tn(bb_, (xd * jnp.exp(alast_e - acum_e)).astype(BF16))
            yg = y * z * _sigmoid(z)
            ob_ref[sl, :] = (yg * lax.rsqrt(jnp.mean(yg * yg, axis=-1, keepdims=True) + RMS_EPS) * wn).astype(BF16)
            return carry

        lax.fori_loop(0, ncb, chunk, 0)

    small = pl.BlockSpec((1, 1, N_STATE), lambda g, j: (g, 0, 0))
    return pl.pallas_call(
        body, name="ssd_fwd", grid=(N_GROUPS, t // tb),
        out_shape=[jax.ShapeDtypeStruct((t, B_INNER), BF16),
                   jax.ShapeDtypeStruct((N_GROUPS, t // CHUNK, N_STATE, GROUP_W), F32)],
        in_specs=[pl.BlockSpec((tb, GROUP_W), lambda g, j: (j, g)),
                  pl.BlockSpec((tb, N_STATE), lambda g, j: (j, B_BLOCK0 + g)),
                  pl.BlockSpec((tb, N_STATE), lambda g, j: (j, C_BLOCK0 + g)),
                  pl.BlockSpec((tb, N_STATE), lambda g, j: (j, DT_BLOCK0 + g)),
                  pl.BlockSpec((tb, GROUP_W), lambda g, j: (j, Z_BLOCK0 + g)),
                  small, small, small, pl.BlockSpec((1, GROUP_W), lambda g, j: (0, g)),
                  pl.BlockSpec((N_STATE, GROUP_W), lambda g, j: (0, 0))],
        out_specs=[pl.BlockSpec((tb, GROUP_W), lambda g, j: (j, g)),
                   pl.BlockSpec((1, ncb, N_STATE, GROUP_W), lambda g, j: (g, j, 0, 0))],
        scratch_shapes=[pltpu.VMEM((N_STATE, GROUP_W), F32)],
        compiler_params=_cparams(("parallel", "arbitrary")),
    )(xc, xc, xc, proj, proj, alog4, bias4, dskip4, wnorm, expand)


def ssd_bwd(proj, xc, alog4, bias4, dskip4, wnorm, expand, dob, states):
    t = proj.shape[0]
    tb = _tile(t, TOKEN_BLOCK)
    ncb = tb // CHUNK
    nb = t // tb

    def body(xs_ref, b_ref, c_ref, dt_ref, z_ref, alog_ref, bias_ref, dsk_ref, wn_ref, e_ref, dob_ref, st_ref,
             dxs_ref, db_ref, dc_ref, dz_ref, ddt_ref, dwn_ref, dalog_ref, dbias_ref, ddsk_ref, dstate):
        @pl.when(pl.program_id(1) == 0)
        def _():
            dstate[...] = jnp.zeros_like(dstate)
            dwn_ref[...] = jnp.zeros_like(dwn_ref)
            dalog_ref[...] = jnp.zeros_like(dalog_ref)
            dbias_ref[...] = jnp.zeros_like(dbias_ref)
            ddsk_ref[...] = jnp.zeros_like(ddsk_ref)

        expand = e_ref[...]
        mask = _tri(CHUNK)
        mask_t = _tri(CHUNK, upper=True)
        tril_f = mask.astype(F32)
        triu_f = mask_t.astype(F32)
        eye = (lax.broadcasted_iota(jnp.int32, (N_STATE, N_STATE), 0) ==
               lax.broadcasted_iota(jnp.int32, (N_STATE, N_STATE), 1)).astype(F32)
        alog, bias = alog_ref[0], bias_ref[0]
        d_e = _nn(jnp.broadcast_to(dsk_ref[0], (8, N_STATE)), expand, HI)[0:1, :]
        wn = wn_ref[...]

        def chunk(i, c0):
            c = ncb - 1 - i
            sl = pl.ds(pl.multiple_of(c * CHUNK, CHUNK), CHUNK)
            xs, bm, cm, dt, z = xs_ref[sl, :], b_ref[sl, :], c_ref[sl, :], dt_ref[sl, :], z_ref[sl, :]
            dtb, delta, ea, a, acum, delta_e, acum_e, acum_t = _ssd_chunk_terms(dt, bias, alog, expand, tril_f, eye)
            alast_e = acum_e[CHUNK - 1:CHUNK, :]
            eacum = jnp.exp(acum_e)
            wl = jnp.exp(alast_e - acum_e)
            xd = xs * delta_e
            xdb = xd.astype(BF16)
            cb_, bb_ = cm.astype(BF16), bm.astype(BF16)
            cbm = _nt(cb_, bb_)
            cbm_t = _nt(bb_, cb_)
            st32 = st_ref[0, c]
            stb = st32.astype(BF16)
            dst = dstate[...]
            dstb = dst.astype(BF16)
            lhs, lhts, ys = [], [], []
            for hh in range(HEADS_PER_GROUP):
                col, row = acum[:, hh:hh + 1], acum_t[hh:hh + 1, :]
                lh = jnp.where(mask, jnp.exp(jnp.minimum(col - row, 0.0)), 0.0)
                lht = jnp.where(mask_t, jnp.exp(jnp.minimum(row - col, 0.0)), 0.0)
                lhs.append(lh)
                lhts.append(lht)
                ys.append(_nn((cbm * lh).astype(BF16), xdb[:, hh * HEAD_P:(hh + 1) * HEAD_P]))
            y_in = jnp.concatenate(ys, axis=1)
            y_out = _nn(cb_, stb) * eacum
            y = y_in + y_out + xs * d_e
            sgz = _sigmoid(z)
            sz = z * sgz
            yg = y * sz
            rstd = lax.rsqrt(jnp.mean(yg * yg, axis=-1, keepdims=True) + RMS_EPS)
            nrm = yg * rstd
            dob_v = dob_ref[sl, :]
            dn = dob_v * wn
            dwn_ref[...] += _colsum(dob_v * nrm)
            dyg = rstd * (dn - nrm * jnp.mean(dn * nrm, axis=-1, keepdims=True))
            dy = dyg * sz
            dz_ref[sl, :] = (dyg * y * _dsilu(z, sgz)).astype(BF16)
            dyb = dy.astype(BF16)
            dxds = []
            dcb = jnp.zeros((CHUNK, CHUNK), F32)
            dcb_t = jnp.zeros((CHUNK, CHUNK), F32)
            for hh in range(HEADS_PER_GROUP):
                hs = slice(hh * HEAD_P, (hh + 1) * HEAD_P)
                dy_h, x_h = dyb[:, hs], xdb[:, hs]
                dxds.append(_nn((cbm_t * lhts[hh]).astype(BF16), dy_h))
                dcb = dcb + _nt(dy_h, x_h) * lhs[hh]
                dcb_t = dcb_t + _nt(x_h, dy_h) * lhts[hh]
            dye = (dy * eacum).astype(BF16)
            xw = (xd * wl).astype(BF16)
            dxd_in = jnp.concatenate(dxds, axis=1)
            dxd_out = wl * _nn(bb_, dstb)
            dxd = dxd_in + dxd_out
            dc_ref[sl, :] = _nn(dcb.astype(BF16), bb_) + _nt(dye, stb)
            db_ref[sl, :] = _nn(dcb_t.astype(BF16), cb_) + _nt(xw, dstb)
            dstate[...] = dst * jnp.exp(alast_e) + _tn(cb_, dye)
            col_out = xd * dxd_out
            dac = _nt(dyb.astype(F32) * y_in - xdb.astype(F32) * dxd_in + dy * y_out - col_out, expand, HI)
            beyond = _colsum(col_out) + jnp.exp(alast_e) * _colsum(dst * st32)
            da = _nn(triu_f, dac, HI) + _nt(jnp.broadcast_to(beyond, (8, GROUP_W)), expand, HI)[0:1, :]
            ddelta = _nt(dxd * xs, expand, HI) - da * ea
            dalog_ref[0] += _colsum(da * a)
            ddtb = ddelta * _sigmoid(dtb)
            dbias_ref[0] += _colsum(ddtb)
            ddt_ref[sl, :] = ddtb.astype(BF16)
            ddsk_ref[0] += _colsum(_nt(dy * xs, expand, HI))
            dxs_ref[sl, :] = dxd * delta_e + dy * d_e
            return c0

        lax.fori_loop(0, ncb, chunk, 0)

    small = pl.BlockSpec((1, 1, N_STATE), lambda g, j: (g, 0, 0))
    wide = pl.BlockSpec((tb, GROUP_W), lambda g, j: (nb - 1 - j, g))
    narrow = pl.BlockSpec((tb, N_STATE), lambda g, j: (nb - 1 - j, g))
    return pl.pallas_call(
        body, name="ssd_bwd", grid=(N_GROUPS, nb),
        out_shape=[jax.ShapeDtypeStruct((t, B_INNER), F32), jax.ShapeDtypeStruct((t, GROUP_W), F32),
                   jax.ShapeDtypeStruct((t, GROUP_W), F32), jax.ShapeDtypeStruct((t, B_INNER), BF16),
                   jax.ShapeDtypeStruct((t, GROUP_W), BF16), jax.ShapeDtypeStruct((1, B_INNER), F32),
                   jax.ShapeDtypeStruct((N_GROUPS, 1, N_STATE), F32), jax.ShapeDtypeStruct((N_GROUPS, 1, N_STATE), F32),
                   jax.ShapeDtypeStruct((N_GROUPS, 1, N_STATE), F32)],
        in_specs=[wide,
                  pl.BlockSpec((tb, N_STATE), lambda g, j: (nb - 1 - j, B_BLOCK0 + g)),
                  pl.BlockSpec((tb, N_STATE), lambda g, j: (nb - 1 - j, C_BLOCK0 + g)),
                  pl.BlockSpec((tb, N_STATE), lambda g, j: (nb - 1 - j, DT_BLOCK0 + g)),
                  pl.BlockSpec((tb, GROUP_W), lambda g, j: (nb - 1 - j, Z_BLOCK0 + g)),
                  small, small, small, pl.BlockSpec((1, GROUP_W), lambda g, j: (0, g)),
                  pl.BlockSpec((N_STATE, GROUP_W), lambda g, j: (0, 0)), wide,
                  pl.BlockSpec((1, ncb, N_STATE, GROUP_W), lambda g, j: (g, nb - 1 - j, 0, 0))],
        out_specs=[wide, narrow, narrow, wide, narrow, pl.BlockSpec((1, GROUP_W), lambda g, j: (0, g)),
                   small, small, small],
        scratch_shapes=[pltpu.VMEM((N_STATE, GROUP_W), F32)],
        compiler_params=_cparams(("parallel", "arbitrary")),
    )(xc, xc, xc, proj, proj, alog4, bias4, dskip4, wnorm, expand, dob, states)


def lower_bound_fwd(hgrn_lb):
    def body(a_ref, o_ref):
        a0, a1 = a_ref[0:1, :], a_ref[1:2, :]
        m = jnp.maximum(a0, a1)
        e0, e1 = jnp.exp(a0 - m), jnp.exp(a1 - m)
        o_ref[...] = e0 / (e0 + e1)

    return pl.pallas_call(body, name="lower_bound_fwd", out_shape=jax.ShapeDtypeStruct((1, D), F32))(hgrn_lb)


def ada_weight_grad(c_all, dmod_cols):
    def body(c_ref, d_ref, o_ref):
        cval = c_ref[...]
        o_ref[...] = _tn(cval * _sigmoid(cval), d_ref[...], HI)

    return pl.pallas_call(body, name="ada_weight_grad",
                          out_shape=jax.ShapeDtypeStruct((D, dmod_cols.shape[1]), F32))(c_all, dmod_cols)


def reduce_small(gathered, hgrn_lb, dlb_off):
    n = gathered.shape[2]

    def body(g_ref, a_ref, o_ref, glb_ref):
        s = g_ref[0]
        for d in range(1, N_DEV):
            s = s + g_ref[d]
        o_ref[...] = s
        a0, a1 = a_ref[0:1, :], a_ref[1:2, :]
        m = jnp.maximum(a0, a1)
        e0, e1 = jnp.exp(a0 - m), jnp.exp(a1 - m)
        p0 = e0 / (e0 + e1)
        tq = s[:, dlb_off:dlb_off + D] * p0 * (1.0 - p0)
        glb_ref[0:1, :] = tq
        glb_ref[1:2, :] = -tq

    return pl.pallas_call(body, name="reduce_small",
                          out_shape=[jax.ShapeDtypeStruct((1, n), F32), jax.ShapeDtypeStruct((2, D), F32)])(gathered, hgrn_lb)


def _adam_math(w, g, m, v):
    m2 = ADAM_B1 * m + (1.0 - ADAM_B1) * g
    v2 = ADAM_B2 * v + (1.0 - ADAM_B2) * (g * g)
    m_hat = m2 / (1.0 - ADAM_B1 ** ADAM_STEP)
    v_hat = v2 / (1.0 - ADAM_B2 ** ADAM_STEP)
    delta = -ADAM_LR * (m_hat / (jnp.sqrt(v_hat) + ADAM_EPS) + ADAM_WD * w)
    return delta, m2, v2


def _row_tile(rows):
    for cand in range(min(rows, 128), 0, -8):
        if rows % cand == 0 and cand % 8 == 0:
            return cand
    return rows


def sum_parts(parts, name):
    _, rows, cols = parts.shape
    tr = _row_tile(rows)

    def body(p_ref, o_ref):
        s = p_ref[0]
        for d in range(1, N_DEV):
            s = s + p_ref[d]
        o_ref[...] = s

    return pl.pallas_call(
        body, name=name, grid=(rows // tr,),
        out_shape=jax.ShapeDtypeStruct((rows, cols), F32),
        in_specs=[pl.BlockSpec((N_DEV, tr, cols), lambda i: (0, i, 0))],
        out_specs=pl.BlockSpec((tr, cols), lambda i: (i, 0)),
        compiler_params=_cparams(("parallel",)),
    )(parts)


def adamw(w, g, m, v, name):
    rows, cols = w.shape
    tr = _row_tile(rows)

    def body(w_ref, g_ref, m_ref, v_ref, d_ref, m2_ref, v2_ref):
        delta, m2, v2 = _adam_math(w_ref[...], g_ref[...], m_ref[...], v_ref[...])
        d_ref[...] = delta
        m2_ref[...] = m2
        v2_ref[...] = v2

    blk = pl.BlockSpec((tr, cols), lambda i: (i, 0))
    return pl.pallas_call(
        body, name=name, grid=(rows // tr,),
        out_shape=[jax.ShapeDtypeStruct((rows, cols), F32)] * 3,
        in_specs=[blk] * 4, out_specs=[blk] * 3,
        compiler_params=_cparams(("parallel",)),
    )(w, g, m, v)


def _pad128(n):
    return -(-n // 128) * 128


def _pack(arrays):
    offs, parts, off = [], [], 0
    for a in arrays:
        flat = a.reshape(1, -1)
        n = flat.shape[1]
        offs.append(off)
        parts.append(jnp.pad(flat, ((0, 0), (0, _pad128(n) - n))))
        off += _pad128(n)
    return jnp.concatenate(parts, axis=1), offs


def _unpack(vec, offs, shapes):
    out = []
    for off, shp in zip(offs, shapes):
        n = int(np.prod(shp))
        out.append(vec[0, off:off + n].reshape(shp))
    return out


def _permute_in_rows(w_t):
    dt = w_t[9216:9248].reshape(N_GROUPS, HEADS_PER_GROUP, D)
    dt = jnp.pad(dt, ((0, 0), (0, N_STATE - HEADS_PER_GROUP), (0, 0))).reshape(N_GROUPS * N_STATE, D)
    dt = jnp.pad(dt, ((0, D - N_GROUPS * N_STATE), (0, 0)))
    return jnp.concatenate([w_t[:9216], w_t[9248:], dt], axis=0)


def _unpermute_in_cols(g):
    dt = g[:, 11264:11264 + N_GROUPS * N_STATE].reshape(D, N_GROUPS, N_STATE)[:, :, :HEADS_PER_GROUP].reshape(D, 32)
    return jnp.concatenate([g[:, :9216], dt, g[:, 9216:11264]], axis=1)


def _column_blocks(g):
    rows = g.shape[0]
    return g.reshape(rows, N_DEV, -1).transpose(1, 0, 2)


def kernel(x, c, w_ada, b_ada, w_in, hgrn_lb, hgrn_gnorm, ssm_conv_w, ssm_conv_b, ssm_dt_bias, ssm_a_log, ssm_d, ssm_norm, w_branch_a, w_branch_b, w_o, ln1_g, ln1_b, w_ffn_gate, w_ffn_up, w_ffn_down, ln2_g, ln2_b, loss_target, m_w_ada, m_b_ada, m_w_in, m_hgrn_lb, m_hgrn_gnorm, m_ssm_conv_w, m_ssm_conv_b, m_ssm_dt_bias, m_ssm_a_log, m_ssm_d, m_ssm_norm, m_w_branch_a, m_w_branch_b, m_w_o, m_ln1_g, m_ln1_b, m_w_ffn_gate, m_w_ffn_up, m_w_ffn_down, m_ln2_g, m_ln2_b, v_w_ada, v_b_ada, v_w_in, v_hgrn_lb, v_hgrn_gnorm, v_ssm_conv_w, v_ssm_conv_b, v_ssm_dt_bias, v_ssm_a_log, v_ssm_d, v_ssm_norm, v_w_branch_a, v_w_branch_b, v_w_o, v_ln1_g, v_ln1_b, v_w_ffn_gate, v_w_ffn_up, v_w_ffn_down, v_ln2_g, v_ln2_b):
    me = 4 * lax.axis_index("x") + 2 * lax.axis_index("y") + lax.axis_index("c")
    xt = x[0]
    tgt = loss_target[0]
    t = xt.shape[0]
    ada_cols = w_ada.shape[2]
    conv_cols = ssm_conv_w.shape[2]

    small_in, _ = _pack([c, ssm_conv_w[0]])
    small_all = allgather_vmem(small_in, "allgather_small_inputs")
    c_all = small_all[:, 0, :D]
    conv_w = small_all[:, 0, D:D + CONV_TAPS * conv_cols].reshape(N_DEV, CONV_TAPS, conv_cols)
    conv_w = conv_w.transpose(1, 0, 2).reshape(CONV_TAPS, CONV_DIM)
    mod = ada_modulation(c_all, w_ada[0], b_ada.reshape(N_DEV, 1, ada_cols))
    mod6 = mod.reshape(6, D)

    shards = [w_in[0].T, w_branch_a[0], w_branch_b[0], w_o[0], w_ffn_gate[0].T, w_ffn_up[0].T, w_ffn_down[0]]
    g_in, g_ba, g_bb, g_o, g_fg, g_fu, g_fd = allgather_hbm([s.astype(BF16) for s in shards], "allgather_weights")
    w_in_t = _permute_in_rows(g_in.reshape(IN_DIM, D))
    w_in_p = w_in_t.T
    w_ba = g_ba.reshape(D, D)
    w_bb = g_bb.reshape(B_INNER, D)
    w_oo = g_o.reshape(D, D)
    ffpad = ((0, D_FF_PAD - D_FF), (0, 0))
    w_gu_t = jnp.concatenate([jnp.pad(g_fg.reshape(D_FF, D), ffpad), jnp.pad(g_fu.reshape(D_FF, D), ffpad)], axis=0)
    w_gu = w_gu_t.T
    w_dn = jnp.pad(g_fd.reshape(D_FF, D), ffpad)

    lb = lower_bound_fwd(hgrn_lb)
    u1 = ln_modulate(xt, mod6, 0, 1, "ln_modulate_1")
    proj = mm_nn(u1, w_in_p, F32, "mm_in_proj")
    o_a, o_raw, st_a = hgrn_fwd(proj, lb, hgrn_gnorm)
    xc = conv_fwd(proj, conv_w, ssm_conv_b)
    pad3 = ((0, 0), (0, 0), (0, N_STATE - HEADS_PER_GROUP))
    alog4 = jnp.pad(ssm_a_log.reshape(N_GROUPS, 1, HEADS_PER_GROUP), pad3)
    bias4 = jnp.pad(ssm_dt_bias.reshape(N_GROUPS, 1, HEADS_PER_GROUP), pad3)
    dskip4 = jnp.pad(ssm_d.reshape(N_GROUPS, 1, HEADS_PER_GROUP), pad3)
    expand = _head_expand()
    o_b, st_b = ssd_fwd(proj, xc, alog4, bias4, dskip4, ssm_norm, expand)
    ya = mm_nn(o_a, w_ba, F32, "mm_branch_a")
    yb = mm_nn(o_b, w_bb, F32, "mm_branch_b")
    merged = merge_gates(ya, yb, proj)
    h1 = mm_nn(merged, w_oo, F32, "mm_out_proj")
    x1 = resid_ln(xt, h1, mod6, 2, ln1_g, ln1_b, "resid_ln_1")
    u2 = ln_modulate(x1, mod6, 3, 4, "ln_modulate_2")
    gu = mm_nn(u2, w_gu, F32, "mm_ffn_in")
    act = swiglu_act(gu)
    h2 = mm_nn(act, w_dn, F32, "mm_ffn_out")

    dh2, dx1_part, acc4 = resid_ln_bwd(x1, h2, mod6, 5, ln2_g, ln2_b, tgt, True, "resid_ln_2_bwd")
    g_dn = mm_tn(act, dh2, "mm_grad_ffn_down")
    dact = mm_nn(dh2, w_dn.T, F32, "mm_dact")
    dgu = swiglu_act_bwd(gu, dact)
    g_gu = mm_tn(u2, dgu, "mm_grad_ffn_in")
    du2 = mm_nn(dgu, w_gu_t, F32, "mm_du2")
    dx1, acc3 = ln_modulate_bwd(x1, du2, mod6, 4, dx1_part, "ln_modulate_2_bwd")
    dh1, dx_part, acc2 = resid_ln_bwd(xt, h1, mod6, 2, ln1_g, ln1_b, dx1, False, "resid_ln_1_bwd")
    g_o = mm_tn(merged, dh1, "mm_grad_out_proj")
    dmerged = mm_nn(dh1, w_oo.T, F32, "mm_dmerged")
    dya, dyb, dga, dgb = merge_gates_bwd(dmerged, ya, yb, proj)
    g_ba_full = mm_tn(o_a, dya, "mm_grad_branch_a")
    g_bb_full = mm_tn(o_b, dyb, "mm_grad_branch_b")
    doa = mm_nn(dya, w_ba.T, F32, "mm_doa")
    dob = mm_nn(dyb, w_bb.T, F32, "mm_dob")
    dq, dfl, di, dg, dlb, dgn = hgrn_bwd(proj, lb, hgrn_gnorm, o_raw, doa, st_a)
    dxs, dbm, dcm, dz, ddt, dwn, dalog, dbias, ddsk = ssd_bwd(proj, xc, alog4, bias4, dskip4, ssm_norm, expand, dob, st_b)
    dxc = jnp.concatenate([dxs, dbm, dcm], axis=1)
    dxbc, dcw, dcb = conv_bwd(proj, dxc, conv_w, ssm_conv_b)
    dproj = jnp.concatenate([dq, dfl, di, dg, dz, dxbc, dga, dgb, ddt, jnp.zeros((t, D - N_GROUPS * N_STATE), BF16)], axis=1)
    g_in = mm_tn(u1, dproj, "mm_grad_in_proj")
    du1 = mm_nn(dproj, w_in_t, F32, "mm_du1")
    dx, acc1 = ln_modulate_bwd(xt, du1, mod6, 1, dx_part, "ln_modulate_1_bwd")

    blocks = [_column_blocks(_unpermute_in_cols(g_in)), g_ba_full.reshape(N_DEV, -1, D), g_bb_full.reshape(N_DEV, -1, D),
              g_o.reshape(N_DEV, -1, D), _column_blocks(g_gu[:, :D_FF]),
              _column_blocks(g_gu[:, D_FF_PAD:D_FF_PAD + D_FF]), g_dn[:D_FF].reshape(N_DEV, -1, D)]
    recv = exchange_hbm(blocks, "exchange_weight_grads")
    names = ["in", "branch_a", "branch_b", "o", "ffn_gate", "ffn_up", "ffn_down"]
    sums = [sum_parts(r, "sum_grad_" + n) for r, n in zip(recv, names)]
    gw_in, gw_ba, gw_bb, gw_o, gw_fg, gw_fu, gw_fd = sums

    dmod = jnp.concatenate([acc1[1:2], acc1[0:1], acc2[0:1], acc3[1:2], acc3[0:1], acc4[0:1]], axis=1)
    small_fields = [dmod, acc4[3:4, :128], dlb, dgn, dcw[:CONV_TAPS], dcb, dbias, dalog, ddsk, dwn,
                    acc2[1:2], acc2[2:3], acc4[1:2], acc4[2:3]]
    small_out, offs = _pack(small_fields)
    small_sum_in = allgather_vmem(small_out, "allgather_small_grads")
    gsum, g_lb = reduce_small(small_sum_in, hgrn_lb, offs[2])
    (g_bada, loss_row, _, g_gn, g_cw_full, g_cb, g_bias4, g_alog4, g_dsk4, g_wn, g_l1g, g_l1b, g_l2g, g_l2b) = _unpack(
        gsum, offs, [(1, 6 * D), (1, 128), (1, D), (1, HK), (CONV_TAPS, CONV_DIM), (1, CONV_DIM),
                     (N_GROUPS, N_STATE), (N_GROUPS, N_STATE), (N_GROUPS, N_STATE), (1, B_INNER),
                     (1, D), (1, D), (1, D), (1, D)])
    loss = loss_row[0, 0]
    g_cw = lax.dynamic_slice(g_cw_full, (0, me * conv_cols), (CONV_TAPS, conv_cols))[None]
    g_dtb = g_bias4[:, :HEADS_PER_GROUP].reshape(1, 32)
    g_alog = g_alog4[:, :HEADS_PER_GROUP].reshape(1, 32)
    g_dsk = g_dsk4[:, :HEADS_PER_GROUP].reshape(1, 32)

    dmod_all = small_sum_in[:, 0, offs[0]:offs[0] + 6 * D]
    dmod_cols = lax.dynamic_slice(dmod_all, (0, me * ada_cols), (N_DEV, ada_cols))
    gw_ada = ada_weight_grad(c_all, dmod_cols)

    big = [("ada", w_ada[0], gw_ada, m_w_ada[0], v_w_ada[0]), ("in", w_in[0], gw_in, m_w_in[0], v_w_in[0]),
           ("branch_a", w_branch_a[0], gw_ba, m_w_branch_a[0], v_w_branch_a[0]),
           ("branch_b", w_branch_b[0], gw_bb, m_w_branch_b[0], v_w_branch_b[0]),
           ("o", w_o[0], gw_o, m_w_o[0], v_w_o[0]),
           ("ffn_gate", w_ffn_gate[0], gw_fg, m_w_ffn_gate[0], v_w_ffn_gate[0]),
           ("ffn_up", w_ffn_up[0], gw_fu, m_w_ffn_up[0], v_w_ffn_up[0]),
           ("ffn_down", w_ffn_down[0], gw_fd, m_w_ffn_down[0], v_w_ffn_down[0])]
    big_out = {}
    for nm, w_, g_, m_, v_ in big:
        d_, m2_, v2_ = adamw(w_, g_, m_, v_, "adamw_" + nm)
        big_out[nm] = (g_[None], d_[None], m2_[None], v2_[None])

    small_w = [b_ada, hgrn_lb, hgrn_gnorm, ssm_conv_w, ssm_conv_b, ssm_dt_bias, ssm_a_log, ssm_d, ssm_norm,
               ln1_g, ln1_b, ln2_g, ln2_b]
    small_g = [g_bada, g_lb, g_gn, g_cw, g_cb, g_dtb, g_alog, g_dsk, g_wn, g_l1g, g_l1b, g_l2g, g_l2b]
    small_m = [m_b_ada, m_hgrn_lb, m_hgrn_gnorm, m_ssm_conv_w, m_ssm_conv_b, m_ssm_dt_bias, m_ssm_a_log, m_ssm_d,
               m_ssm_norm, m_ln1_g, m_ln1_b, m_ln2_g, m_ln2_b]
    small_v = [v_b_ada, v_hgrn_lb, v_hgrn_gnorm, v_ssm_conv_w, v_ssm_conv_b, v_ssm_dt_bias, v_ssm_a_log, v_ssm_d,
               v_ssm_norm, v_ln1_g, v_ln1_b, v_ln2_g, v_ln2_b]
    shapes = [a.shape for a in small_w]
    small_g = [g_.reshape(s) for g_, s in zip(small_g, shapes)]
    pw, poffs = _pack(small_w)
    pg, _ = _pack(small_g)
    pm, _ = _pack(small_m)
    pv, _ = _pack(small_v)
    pd, pm2, pv2 = adamw(pw, pg, pm, pv, "adamw_small")
    s_d, s_m, s_v = (_unpack(p, poffs, shapes) for p in (pd, pm2, pv2))
    (sn_bada, sn_lb, sn_gn, sn_cw, sn_cb, sn_dtb, sn_alog, sn_dsk, sn_wn, sn_l1g, sn_l1b, sn_l2g, sn_l2b) = range(13)

    def order(kind):
        sm = [small_g, s_d, s_m, s_v][kind]
        bg = lambda nm: big_out[nm][kind]
        return [bg("ada"), sm[sn_bada], bg("in"), sm[sn_lb], sm[sn_gn], sm[sn_cw], sm[sn_cb], sm[sn_dtb], sm[sn_alog],
                sm[sn_dsk], sm[sn_wn], bg("branch_a"), bg("branch_b"), bg("o"), sm[sn_l1g], sm[sn_l1b],
                bg("ffn_gate"), bg("ffn_up"), bg("ffn_down"), sm[sn_l2g], sm[sn_l2b]]

    return (loss, dx[None], *order(0), *order(1), *order(2), *order(3))
```

```python
import functools

import numpy as np
import jax
import jax.numpy as jnp
from jax import lax
from jax.experimental import pallas as pl
from jax.experimental.pallas import tpu as pltpu

F32 = jnp.float32
BF16 = jnp.bfloat16
HI = lax.Precision.HIGHEST

N_DEV = 8
D = 1024
N_HEADS_A = 8
HK = 128
CHUNK = 64
N_GROUPS = 4
HEADS_PER_GROUP = 8
HEAD_P = 64
N_STATE = 128
GROUP_W = HEADS_PER_GROUP * HEAD_P
B_INNER = 2048
CONV_DIM = 3072
D_FF = 2816
D_FF_PAD = 3072
IN_DIM = 11296
N_PROJ = 12288
ALPHA = 2.0 ** 0.25
LN_EPS = 1e-5
RMS_EPS = 1e-6
Q_SCALE = 128 ** -0.5
EXP_CLIP = 80.0
ADAM_LR, ADAM_B1, ADAM_B2, ADAM_EPS, ADAM_WD, ADAM_STEP = 0.001, 0.9, 0.999, 1e-8, 0.01, 10
VMEM_LIMIT = 48 * 1024 * 1024
TOKEN_BLOCK = 512
ROW_TILE = 256
FFN_ROW_TILE = 128
MM_ROW_TILE = 1024
MM_TOKEN_TILE = 512
MESH_ID = pl.DeviceIdType.MESH

NT_DIMS = (((1,), (1,)), ((), ()))
TN_DIMS = (((0,), (0,)), ((), ()))


def _cparams(sem=None):
    return pltpu.CompilerParams(dimension_semantics=sem, vmem_limit_bytes=VMEM_LIMIT)


def _sigmoid(x):
    return 1.0 / (1.0 + jnp.exp(-x))


def _dsilu(x, s):
    return s * (1.0 + x * (1.0 - s))


def _nt(a, b, precision=None):
    return lax.dot_general(a, b, NT_DIMS, precision=precision, preferred_element_type=F32)


def _tn(a, b, precision=None):
    return lax.dot_general(a, b, TN_DIMS, precision=precision, preferred_element_type=F32)


def _nn(a, b, precision=None):
    return jnp.dot(a, b, precision=precision, preferred_element_type=F32)


def _ln(x):
    mu = jnp.mean(x, axis=-1, keepdims=True)
    xc = x - mu
    rstd = lax.rsqrt(jnp.mean(xc * xc, axis=-1, keepdims=True) + LN_EPS)
    return xc * rstd, rstd


def _ln_bwd(dxh, xh, rstd):
    return rstd * (dxh - jnp.mean(dxh, axis=-1, keepdims=True) - xh * jnp.mean(dxh * xh, axis=-1, keepdims=True))


def _colsum(x):
    return jnp.sum(x, axis=0, keepdims=True)


def _tri(n, upper=False):
    r = lax.broadcasted_iota(jnp.int32, (n, n), 0)
    c = lax.broadcasted_iota(jnp.int32, (n, n), 1)
    return (c >= r) if upper else (r >= c)


def _my_pos():
    return lax.axis_index("x"), lax.axis_index("y"), lax.axis_index("c")


def _peer(pos, k):
    x, y, c = pos
    return (x ^ ((k >> 2) & 1), y ^ ((k >> 1) & 1), c ^ (k & 1))


def _flat(pos):
    return 4 * pos[0] + 2 * pos[1] + pos[2]


def allgather_vmem(v, name):
    n = v.shape[1]

    def body(v_ref, o_ref, send_sems, recv_sems, local_sem):
        me = _my_pos()
        mine = pltpu.make_async_copy(v_ref, o_ref.at[_flat(me)], local_sem)
        mine.start()
        sends = []
        for k in range(1, N_DEV):
            peer = _peer(me, k)
            cp = pltpu.make_async_remote_copy(v_ref, o_ref.at[_flat(me)], send_sems.at[k - 1], recv_sems.at[k - 1],
                                              device_id=peer, device_id_type=MESH_ID)
            cp.start()
            sends.append(cp)
        for k in range(1, N_DEV):
            peer = _peer(me, k)
            pltpu.make_async_remote_copy(v_ref, o_ref.at[_flat(peer)], send_sems.at[k - 1], recv_sems.at[k - 1],
                                         device_id=peer, device_id_type=MESH_ID).wait_recv()
        for cp in sends:
            cp.wait_send()
        mine.wait()

    return pl.pallas_call(
        body, name=name,
        out_shape=jax.ShapeDtypeStruct((N_DEV, 1, n), F32),
        in_specs=[pl.BlockSpec(memory_space=pltpu.VMEM)],
        out_specs=pl.BlockSpec(memory_space=pltpu.VMEM),
        scratch_shapes=[pltpu.SemaphoreType.DMA((N_DEV - 1,)), pltpu.SemaphoreType.DMA((N_DEV - 1,)),
                        pltpu.SemaphoreType.DMA],
        compiler_params=_cparams(),
    )(v)


def ada_modulation(c_all, w_ada_s, b_ada_r):
    ncol = w_ada_s.shape[1]

    def body(c_ref, w_ref, b_ref, o_ref, part_ref, send_sems, recv_sems):
        me = _my_pos()
        cval = c_ref[...]
        cond = cval * _sigmoid(cval)
        part = _nn(cond, w_ref[...], HI)
        for r in range(N_DEV):
            part_ref[r] = part[r:r + 1, :]
        sends = []
        for k in range(1, N_DEV):
            peer = _peer(me, k)
            cp = pltpu.make_async_remote_copy(part_ref.at[_flat(peer)], o_ref.at[_flat(me)], send_sems.at[k - 1],
                                              recv_sems.at[k - 1], device_id=peer, device_id_type=MESH_ID)
            cp.start()
            sends.append(cp)
        o_ref[_flat(me)] = part_ref[_flat(me)]
        for k in range(1, N_DEV):
            peer = _peer(me, k)
            pltpu.make_async_remote_copy(part_ref.at[_flat(peer)], o_ref.at[_flat(peer)], send_sems.at[k - 1],
                                         recv_sems.at[k - 1], device_id=peer, device_id_type=MESH_ID).wait_recv()
        for cp in sends:
            cp.wait_send()
        o_ref[...] = o_ref[...] + b_ref[...]

    return pl.pallas_call(
        body, name="ada_modulation",
        out_shape=jax.ShapeDtypeStruct((N_DEV, 1, ncol), F32),
        in_specs=[pl.BlockSpec(memory_space=pltpu.VMEM)] * 3,
        out_specs=pl.BlockSpec(memory_space=pltpu.VMEM),
        scratch_shapes=[pltpu.VMEM((N_DEV, 1, ncol), F32), pltpu.SemaphoreType.DMA((N_DEV - 1,)),
                        pltpu.SemaphoreType.DMA((N_DEV - 1,))],
        compiler_params=_cparams(),
    )(c_all, w_ada_s, b_ada_r)


def allgather_hbm(shard, name):
    def body(x_ref, out_ref, send_sems, recv_sems, local_sem):
        x, y, c = _my_pos()
        me, sibling = (x, y, c), (x, y, 1 - c)
        chips = [(1 - x, y), (x, 1 - y), (1 - x, 1 - y)]

        def slot(pos):
            return out_ref.at[_flat(pos)]

        def copy(k, block, to, src=None):
            return pltpu.make_async_remote_copy(slot(block) if src is None else src, slot(block), send_sems.at[k],
                                                recv_sems.at[k], device_id=to, device_id_type=MESH_ID)

        mine = pltpu.make_async_copy(x_ref, slot(me), local_sem)
        mine.start()
        first = [copy(0, me, sibling, src=x_ref)]
        first += [copy(1 + j, me, (*chip, c), src=x_ref) for j, chip in enumerate(chips)]
        for cp in first:
            cp.start()
        passed = [copy(4 + j, (*chip, c), sibling) for j, chip in enumerate(chips)]
        for j, chip in enumerate(chips):
            copy(1 + j, (*chip, c), me).wait_recv()
            passed[j].start()
        copy(0, sibling, me).wait_recv()
        for j, chip in enumerate(chips):
            copy(4 + j, (*chip, 1 - c), me).wait_recv()
        for cp in first + passed:
            cp.wait_send()
        mine.wait()

    return pl.pallas_call(
        body, name=name,
        out_shape=jax.ShapeDtypeStruct((N_DEV,) + shard.shape, shard.dtype),
        in_specs=[pl.BlockSpec(memory_space=pl.ANY)],
        out_specs=pl.BlockSpec(memory_space=pl.ANY),
        scratch_shapes=[pltpu.SemaphoreType.DMA((N_DEV - 1,)), pltpu.SemaphoreType.DMA((N_DEV - 1,)),
                        pltpu.SemaphoreType.DMA],
        compiler_params=_cparams(),
    )(shard)


def exchange_sibling(send, name):
    def body(s_ref, o_ref, send_sem, recv_sem):
        x, y, c = _my_pos()
        cp = pltpu.make_async_remote_copy(s_ref, o_ref, send_sem, recv_sem, device_id=(x, y, 1 - c),
                                          device_id_type=MESH_ID)
        cp.start()
        cp.wait()

    return pl.pallas_call(
        body, name=name,
        out_shape=jax.ShapeDtypeStruct(send.shape, send.dtype),
        in_specs=[pl.BlockSpec(memory_space=pl.ANY)],
        out_specs=pl.BlockSpec(memory_space=pl.ANY),
        scratch_shapes=[pltpu.SemaphoreType.DMA, pltpu.SemaphoreType.DMA],
        compiler_params=_cparams(),
    )(send)


def exchange_chips(part, name):
    n_chip = N_DEV // 2

    def body(p_ref, o_ref, send_sems, recv_sems, local_sem):
        x, y, c = _my_pos()
        my_chip = 2 * x + y
        mine = pltpu.make_async_copy(p_ref.at[my_chip], o_ref.at[my_chip], local_sem)
        mine.start()
        sends = []
        for k in range(1, n_chip):
            px, py = x ^ (k >> 1), y ^ (k & 1)
            cp = pltpu.make_async_remote_copy(p_ref.at[2 * px + py], o_ref.at[my_chip], send_sems.at[k - 1],
                                              recv_sems.at[k - 1], device_id=(px, py, c), device_id_type=MESH_ID)
            cp.start()
            sends.append(cp)
        for k in range(1, n_chip):
            px, py = x ^ (k >> 1), y ^ (k & 1)
            pltpu.make_async_remote_copy(p_ref.at[2 * px + py], o_ref.at[2 * px + py], send_sems.at[k - 1],
                                         recv_sems.at[k - 1], device_id=(px, py, c), device_id_type=MESH_ID).wait_recv()
        for cp in sends:
            cp.wait_send()
        mine.wait()

    return pl.pallas_call(
        body, name=name,
        out_shape=jax.ShapeDtypeStruct(part.shape, part.dtype),
        in_specs=[pl.BlockSpec(memory_space=pl.ANY)],
        out_specs=pl.BlockSpec(memory_space=pl.ANY),
        scratch_shapes=[pltpu.SemaphoreType.DMA((n_chip - 1,)), pltpu.SemaphoreType.DMA((n_chip - 1,)),
                        pltpu.SemaphoreType.DMA],
        compiler_params=_cparams(),
    )(part)


def mm_nn(a, b, out_dtype, name):
    m, kdim = a.shape
    n = b.shape[1]
    tm, tn, tk = min(MM_ROW_TILE, m), 1024, 1024
    nk = kdim // tk

    def body(a_ref, b_ref, o_ref, acc_ref):
        p = _nn(a_ref[...], b_ref[...])
        if nk == 1:
            o_ref[...] = p.astype(o_ref.dtype)
        else:
            k = pl.program_id(2)

            @pl.when(k == 0)
            def _():
                acc_ref[...] = p

            @pl.when(k > 0)
            def _():
                acc_ref[...] += p

            @pl.when(k == nk - 1)
            def _():
                o_ref[...] = acc_ref[...].astype(o_ref.dtype)

    return pl.pallas_call(
        body, name=name, grid=(n // tn, m // tm, nk),
        out_shape=jax.ShapeDtypeStruct((m, n), out_dtype),
        in_specs=[pl.BlockSpec((tm, tk), lambda j, i, k: (i, k)), pl.BlockSpec((tk, tn), lambda j, i, k: (k, j))],
        out_specs=pl.BlockSpec((tm, tn), lambda j, i, k: (i, j)),
        scratch_shapes=[pltpu.VMEM((tm, tn), F32)],
        compiler_params=_cparams(("parallel", "parallel", "arbitrary")),
    )(a, b)


def mm_tn(a, b, name):
    t, ka = a.shape
    n = b.shape[1]
    tt, tka, tn = min(MM_TOKEN_TILE, t), 1024, 1024
    nt = t // tt

    def body(a_ref, b_ref, o_ref, acc_ref):
        p = _tn(a_ref[...], b_ref[...])
        s = pl.program_id(2)

        @pl.when(s == 0)
        def _():
            acc_ref[...] = p

        @pl.when(s > 0)
        def _():
            acc_ref[...] += p

        @pl.when(s == nt - 1)
        def _():
            o_ref[...] = acc_ref[...].astype(o_ref.dtype)

    return pl.pallas_call(
        body, name=name, grid=(ka // tka, n // tn, nt),
        out_shape=jax.ShapeDtypeStruct((ka, n), BF16),
        in_specs=[pl.BlockSpec((tt, tka), lambda i, j, s: (s, i)), pl.BlockSpec((tt, tn), lambda i, j, s: (s, j))],
        out_specs=pl.BlockSpec((tka, tn), lambda i, j, s: (i, j)),
        scratch_shapes=[pltpu.VMEM((tka, tn), F32)],
        compiler_params=_cparams(("parallel", "parallel", "arbitrary")),
    )(a, b)


def _tile(t, cap):
    return min(cap, t)


def ln_modulate(x, mod6, shift_row, scale_row, name):
    t = x.shape[0]
    tm = _tile(t, ROW_TILE)

    def body(x_ref, mod_ref, o_ref):
        xh, _ = _ln(x_ref[...])
        sc = mod_ref[scale_row:scale_row + 1, :]
        sh = mod_ref[shift_row:shift_row + 1, :]
        o_ref[...] = (xh * (1.0 + sc) + sh).astype(BF16)

    return pl.pallas_call(
        body, name=name, grid=(t // tm,),
        out_shape=jax.ShapeDtypeStruct((t, D), BF16),
        in_specs=[pl.BlockSpec((tm, D), lambda i: (i, 0)), pl.BlockSpec((6, D), lambda i: (0, 0))],
        out_specs=pl.BlockSpec((tm, D), lambda i: (i, 0)),
        compiler_params=_cparams(("parallel",)),
    )(x, mod6)


def resid_ln(x, h, mod6, gate_row, ln_g, ln_b, name):
    t = x.shape[0]
    tm = _tile(t, ROW_TILE)

    def body(x_ref, h_ref, mod_ref, g_ref, b_ref, o_ref):
        r = ALPHA * x_ref[...] + mod_ref[gate_row:gate_row + 1, :] * h_ref[...]
        rh, _ = _ln(r)
        o_ref[...] = rh * g_ref[...] + b_ref[...]

    row = pl.BlockSpec((tm, D), lambda i: (i, 0))
    vec = pl.BlockSpec((1, D), lambda i: (0, 0))
    return pl.pallas_call(
        body, name=name, grid=(t // tm,),
        out_shape=jax.ShapeDtypeStruct((t, D), F32),
        in_specs=[row, row, pl.BlockSpec((6, D), lambda i: (0, 0)), vec, vec],
        out_specs=row,
        compiler_params=_cparams(("parallel",)),
    )(x, h, mod6, ln_g, ln_b)


def resid_ln_bwd(x, h, mod6, gate_row, ln_g, ln_b, cot, with_loss, name):
    t = x.shape[0]
    tm = _tile(t, ROW_TILE)

    def body(x_ref, h_ref, mod_ref, g_ref, b_ref, c_ref, dh_ref, dx_ref, acc_ref):
        @pl.when(pl.program_id(0) == 0)
        def _():
            acc_ref[...] = jnp.zeros_like(acc_ref)

        gate = mod_ref[gate_row:gate_row + 1, :]
        hv = h_ref[...]
        r = ALPHA * x_ref[...] + gate * hv
        rh, rstd = _ln(r)
        lng = g_ref[...]
        if with_loss:
            diff = rh * lng + b_ref[...] - c_ref[...]
            dxo = diff * (1.0 / D)
            lsum = jnp.sum(_colsum(diff * diff), axis=-1, keepdims=True) * (0.5 / D)
            acc_ref[3:4, :] += jnp.broadcast_to(lsum, (1, D))
        else:
            dxo = c_ref[...]
        acc_ref[1:2, :] += _colsum(dxo * rh)
        acc_ref[2:3, :] += _colsum(dxo)
        dr = _ln_bwd(dxo * lng, rh, rstd)
        acc_ref[0:1, :] += _colsum(dr * hv)
        dh_ref[...] = (gate * dr).astype(BF16)
        dx_ref[...] = ALPHA * dr

    row = pl.BlockSpec((tm, D), lambda i: (i, 0))
    vec = pl.BlockSpec((1, D), lambda i: (0, 0))
    return pl.pallas_call(
        body, name=name, grid=(t // tm,),
        out_shape=[jax.ShapeDtypeStruct((t, D), BF16), jax.ShapeDtypeStruct((t, D), F32),
                   jax.ShapeDtypeStruct((8, D), F32)],
        in_specs=[row, row, pl.BlockSpec((6, D), lambda i: (0, 0)), vec, vec, row],
        out_specs=[row, row, pl.BlockSpec((8, D), lambda i: (0, 0))],
        compiler_params=_cparams(("arbitrary",)),
    )(x, h, mod6, ln_g, ln_b, cot)


def ln_modulate_bwd(x, du, mod6, scale_row, dx_part, name):
    t = x.shape[0]
    tm = _tile(t, ROW_TILE)

    def body(x_ref, du_ref, mod_ref, dp_ref, dx_ref, acc_ref):
        @pl.when(pl.program_id(0) == 0)
        def _():
            acc_ref[...] = jnp.zeros_like(acc_ref)

        xh, rstd = _ln(x_ref[...])
        du_v = du_ref[...]
        sc = mod_ref[scale_row:scale_row + 1, :]
        acc_ref[0:1, :] += _colsum(du_v * xh)
        acc_ref[1:2, :] += _colsum(du_v)
        dx_ref[...] = dp_ref[...] + _ln_bwd(du_v * (1.0 + sc), xh, rstd)

    row = pl.BlockSpec((tm, D), lambda i: (i, 0))
    return pl.pallas_call(
        body, name=name, grid=(t // tm,),
        out_shape=[jax.ShapeDtypeStruct((t, D), F32), jax.ShapeDtypeStruct((8, D), F32)],
        in_specs=[row, row, pl.BlockSpec((6, D), lambda i: (0, 0)), row],
        out_specs=[row, pl.BlockSpec((8, D), lambda i: (0, 0))],
        compiler_params=_cparams(("arbitrary",)),
    )(x, du, mod6, dx_part)


def merge_gates(ya, yb, proj):
    t = ya.shape[0]
    tm = _tile(t, ROW_TILE)

    def body(ya_ref, yb_ref, ga_ref, gb_ref, o_ref):
        o_ref[...] = (_sigmoid(ga_ref[...]) * ya_ref[...] + _sigmoid(gb_ref[...]) * yb_ref[...]).astype(BF16)

    row = pl.BlockSpec((tm, D), lambda i: (i, 0))
    return pl.pallas_call(
        body, name="merge_gates", grid=(t // tm,),
        out_shape=jax.ShapeDtypeStruct((t, D), BF16),
        in_specs=[row, row, pl.BlockSpec((tm, D), lambda i: (i, 9)), pl.BlockSpec((tm, D), lambda i: (i, 10))],
        out_specs=row,
        compiler_params=_cparams(("parallel",)),
    )(ya, yb, proj, proj)


def merge_gates_bwd(dm, ya, yb, proj):
    t = ya.shape[0]
    tm = _tile(t, ROW_TILE)

    def body(dm_ref, ya_ref, yb_ref, ga_ref, gb_ref, dya_ref, dyb_ref, dga_ref, dgb_ref):
        dmv = dm_ref[...]
        sa = _sigmoid(ga_ref[...])
        sb = _sigmoid(gb_ref[...])
        dya_ref[...] = (dmv * sa).astype(BF16)
        dyb_ref[...] = (dmv * sb).astype(BF16)
        dga_ref[...] = (dmv * ya_ref[...] * sa * (1.0 - sa)).astype(BF16)
        dgb_ref[...] = (dmv * yb_ref[...] * sb * (1.0 - sb)).astype(BF16)

    row = pl.BlockSpec((tm, D), lambda i: (i, 0))
    return pl.pallas_call(
        body, name="merge_gates_bwd", grid=(t // tm,),
        out_shape=[jax.ShapeDtypeStruct((t, D), BF16)] * 4,
        in_specs=[row, row, row, pl.BlockSpec((tm, D), lambda i: (i, 9)), pl.BlockSpec((tm, D), lambda i: (i, 10))],
        out_specs=[row] * 4,
        compiler_params=_cparams(("parallel",)),
    )(dm, ya, yb, proj, proj)


def swiglu_act(gu):
    t = gu.shape[0]
    tm = _tile(t, FFN_ROW_TILE)

    def body(gu_ref, o_ref):
        for j in range(D_FF_PAD // D):
            g = gu_ref[:, j * D:(j + 1) * D]
            u = gu_ref[:, D_FF_PAD + j * D:D_FF_PAD + (j + 1) * D]
            o_ref[:, j * D:(j + 1) * D] = (g * _sigmoid(g) * u).astype(BF16)

    return pl.pallas_call(
        body, name="swiglu_act", grid=(t // tm,),
        out_shape=jax.ShapeDtypeStruct((t, D_FF_PAD), BF16),
        in_specs=[pl.BlockSpec((tm, 2 * D_FF_PAD), lambda i: (i, 0))],
        out_specs=pl.BlockSpec((tm, D_FF_PAD), lambda i: (i, 0)),
        compiler_params=_cparams(("parallel",)),
    )(gu)


def swiglu_act_bwd(gu, dact):
    t = gu.shape[0]
    tm = _tile(t, FFN_ROW_TILE)

    def body(gu_ref, da_ref, o_ref):
        for j in range(D_FF_PAD // D):
            g = gu_ref[:, j * D:(j + 1) * D]
            u = gu_ref[:, D_FF_PAD + j * D:D_FF_PAD + (j + 1) * D]
            da = da_ref[:, j * D:(j + 1) * D]
            s = _sigmoid(g)
            o_ref[:, j * D:(j + 1) * D] = (da * u * _dsilu(g, s)).astype(BF16)
            o_ref[:, D_FF_PAD + j * D:D_FF_PAD + (j + 1) * D] = (da * g * s).astype(BF16)

    return pl.pallas_call(
        body, name="swiglu_act_bwd", grid=(t // tm,),
        out_shape=jax.ShapeDtypeStruct((t, 2 * D_FF_PAD), BF16),
        in_specs=[pl.BlockSpec((tm, 2 * D_FF_PAD), lambda i: (i, 0)), pl.BlockSpec((tm, D_FF_PAD), lambda i: (i, 0))],
        out_specs=pl.BlockSpec((tm, 2 * D_FF_PAD), lambda i: (i, 0)),
        compiler_params=_cparams(("parallel",)),
    )(gu, dact)


def _hgrn_chunk_terms(q, fl, lbv, tril_f):
    sig = _sigmoid(fl)
    f = lbv + (1.0 - lbv) * sig
    lam = jnp.log(f)
    k = 1.0 - f
    sq = _sigmoid(q)
    qt = q * sq * Q_SCALE
    bc = _nn(tril_f, lam, HI)
    bmid = bc[CHUNK // 2 - 1:CHUNK // 2, :]
    bl = bc[CHUNK - 1:CHUNK, :]
    eq = jnp.exp(jnp.minimum(bc - bmid, EXP_CLIP))
    ek = jnp.exp(jnp.minimum(bmid - bc, EXP_CLIP))
    eb = jnp.exp(bc)
    ekl = jnp.exp(bl - bc)
    ebl = jnp.exp(bl)
    return sig, f, k, sq, qt, eq, ek, eb, ekl, ebl


def hgrn_fwd(proj, lb, gnorm):
    t = proj.shape[0]
    tb = _tile(t, TOKEN_BLOCK)
    ncb = tb // CHUNK

    def body(q_ref, f_ref, i_ref, g_ref, lb_ref, gn_ref, oa_ref, oraw_ref, st_ref, state):
        @pl.when(pl.program_id(1) == 0)
        def _():
            state[...] = jnp.zeros_like(state)

        lbv = lb_ref[...]
        gn = gn_ref[...]
        mask = _tri(CHUNK)
        tril_f = mask.astype(F32)

        def chunk(c, carry):
            sl = pl.ds(pl.multiple_of(c * CHUNK, CHUNK), CHUNK)
            q, fl, v, g = q_ref[sl, :], f_ref[sl, :], i_ref[sl, :], g_ref[sl, :]
            sig, f, k, sq, qt, eq, ek, eb, ekl, ebl = _hgrn_chunk_terms(q, fl, lbv, tril_f)
            a = jnp.where(mask, _nt((qt * eq).astype(BF16), (k * ek).astype(BF16)), 0.0)
            st = state[...]
            st_ref[0, c] = st
            vb = v.astype(BF16)
            o = _nn(a.astype(BF16), vb) + _nt((qt * eb).astype(BF16), st.astype(BF16))
            state[...] = st * ebl + _tn(vb, (k * ekl).astype(BF16))
            oraw_ref[sl, :] = o
            rn = o * lax.rsqrt(jnp.mean(o * o, axis=-1, keepdims=True) + RMS_EPS)
            oa_ref[sl, :] = (rn * gn * g * _sigmoid(g)).astype(BF16)
            return carry

        lax.fori_loop(0, ncb, chunk, 0)

    def col(block):
        return pl.BlockSpec((tb, HK), lambda h, j: (j, block * N_HEADS_A + h))

    return pl.pallas_call(
        body, name="hgrn_fwd", grid=(N_HEADS_A, t // tb),
        out_shape=[jax.ShapeDtypeStruct((t, D), BF16), jax.ShapeDtypeStruct((t, D), F32),
                   jax.ShapeDtypeStruct((N_HEADS_A, t // CHUNK, HK, HK), F32)],
        in_specs=[col(0), col(1), col(2), col(3), pl.BlockSpec((1, HK), lambda h, j: (0, h)),
                  pl.BlockSpec((1, HK), lambda h, j: (0, 0))],
        out_specs=[pl.BlockSpec((tb, HK), lambda h, j: (j, h)), pl.BlockSpec((tb, HK), lambda h, j: (j, h)),
                   pl.BlockSpec((1, ncb, HK, HK), lambda h, j: (h, j, 0, 0))],
        scratch_shapes=[pltpu.VMEM((HK, HK), F32)],
        compiler_params=_cparams(("parallel", "arbitrary")),
    )(proj, proj, proj, proj, lb, gnorm)


def hgrn_bwd(proj, lb, gnorm, o_raw, doa, states):
    t = proj.shape[0]
    tb = _tile(t, TOKEN_BLOCK)
    ncb = tb // CHUNK
    nb = t // tb

    def body(q_ref, f_ref, i_ref, g_ref, lb_ref, gn_ref, oraw_ref, doa_ref, st_ref,
             dq_ref, df_ref, di_ref, dg_ref, dlb_ref, dgn_ref, dstate):
        h, j = pl.program_id(0), pl.program_id(1)

        @pl.when(j == 0)
        def _():
            dstate[...] = jnp.zeros_like(dstate)
            dlb_ref[...] = jnp.zeros_like(dlb_ref)

        @pl.when((j == 0) & (h == 0))
        def _():
            dgn_ref[...] = jnp.zeros_like(dgn_ref)

        lbv = lb_ref[...]
        gn = gn_ref[...]
        mask = _tri(CHUNK)
        mask_t = _tri(CHUNK, upper=True)
        tril_f = mask.astype(F32)
        triu_f = mask_t.astype(F32)

        def chunk(i, c0):
            c = ncb - 1 - i
            sl = pl.ds(pl.multiple_of(c * CHUNK, CHUNK), CHUNK)
            q, fl, v, g = q_ref[sl, :], f_ref[sl, :], i_ref[sl, :], g_ref[sl, :]
            sig, f, k, sq, qt, eq, ek, eb, ekl, ebl = _hgrn_chunk_terms(q, fl, lbv, tril_f)
            qe = (qt * eq).astype(BF16)
            ke = (k * ek).astype(BF16)
            st32 = st_ref[0, c]
            st = st32.astype(BF16)
            dst = dstate[...]
            dstb = dst.astype(BF16)
            o = oraw_ref[sl, :]
            rstd = lax.rsqrt(jnp.mean(o * o, axis=-1, keepdims=True) + RMS_EPS)
            rn = o * rstd
            sgm = _sigmoid(g)
            sg = g * sgm
            doa_v = doa_ref[sl, :]
            drn = doa_v * gn * sg
            dgn_ref[...] += _colsum(doa_v * rn * sg)
            dg_ref[sl, :] = (doa_v * rn * gn * _dsilu(g, sgm)).astype(BF16)
            do = rstd * (drn - rn * jnp.mean(drn * rn, axis=-1, keepdims=True))
            dob = do.astype(BF16)
            vb = v.astype(BF16)
            da = jnp.where(mask, _nt(dob, vb), 0.0).astype(BF16)
            da_t = jnp.where(mask_t, _nt(vb, dob), 0.0).astype(BF16)
            a_t = jnp.where(mask_t, _nt(ke, qe), 0.0).astype(BF16)
            kl = (k * ekl).astype(BF16)
            qb = (qt * eb).astype(BF16)
            dq_in = _nn(da, ke)
            dk_in = _nn(da_t, qe)
            dq_out = eb * _nn(dob, st)
            dk_out = ekl * _nn(vb, dstb)
            dqt = eq * dq_in + dq_out
            dk = ek * dk_in + dk_out
            dv = _nn(a_t, dob) + _nt(kl, dstb)
            dstate[...] = dst * ebl + _tn(dob, qb)
            dbig = qe.astype(F32) * dq_in - ke.astype(F32) * dk_in + qt * dq_out - k * dk_out
            beyond = _colsum(k * dk_out) + ebl * _colsum(dst * st32)
            dlam = _nn(triu_f, dbig, HI) + beyond
            df = dlam / f - dk
            df_ref[sl, :] = (df * (1.0 - lbv) * sig * (1.0 - sig)).astype(BF16)
            dlb_ref[...] += _colsum(df * (1.0 - sig))
            dq_ref[sl, :] = (dqt * Q_SCALE * _dsilu(q, sq)).astype(BF16)
            di_ref[sl, :] = dv.astype(BF16)
            return c0

        lax.fori_loop(0, ncb, chunk, 0)

    def col(block):
        return pl.BlockSpec((tb, HK), lambda h, j: (nb - 1 - j, block * N_HEADS_A + h))

    hcol = pl.BlockSpec((tb, HK), lambda h, j: (nb - 1 - j, h))
    return pl.pallas_call(
        body, name="hgrn_bwd", grid=(N_HEADS_A, nb),
        out_shape=[jax.ShapeDtypeStruct((t, D), BF16)] * 4 + [jax.ShapeDtypeStruct((1, D), F32),
                                                                jax.ShapeDtypeStruct((1, HK), F32)],
        in_specs=[col(0), col(1), col(2), col(3), pl.BlockSpec((1, HK), lambda h, j: (0, h)),
                  pl.BlockSpec((1, HK), lambda h, j: (0, 0)), hcol, hcol,
                  pl.BlockSpec((1, ncb, HK, HK), lambda h, j: (h, nb - 1 - j, 0, 0))],
        out_specs=[hcol] * 4 + [pl.BlockSpec((1, HK), lambda h, j: (0, h)), pl.BlockSpec((1, HK), lambda h, j: (0, 0))],
        scratch_shapes=[pltpu.VMEM((HK, HK), F32)],
        compiler_params=_cparams(("arbitrary", "arbitrary")),
    )(proj, proj, proj, proj, lb, gnorm, o_raw, doa, states)


CONV_BLOCK0 = 6
CONV_TAPS = 4
HALO = 8


def conv_fwd(proj, conv_w, conv_b):
    t = proj.shape[0]
    tm = _tile(t, ROW_TILE)
    r = tm // HALO

    def body(x_ref, halo_ref, w_ref, b_ref, o_ref):
        i = pl.program_id(1)
        halo = jnp.where(i > 0, halo_ref[...], 0.0)
        ext = jnp.concatenate([halo, x_ref[...]], axis=0)
        pre = b_ref[...] + w_ref[CONV_TAPS - 1:CONV_TAPS, :] * ext[HALO:, :]
        for tap in range(CONV_TAPS - 1):
            pre = pre + w_ref[tap:tap + 1, :] * pltpu.roll(ext, CONV_TAPS - 1 - tap, axis=0)[HALO:, :]
        o_ref[...] = pre * _sigmoid(pre)

    return pl.pallas_call(
        body, name="conv_fwd", grid=(CONV_DIM // D, t // tm),
        out_shape=jax.ShapeDtypeStruct((t, CONV_DIM), F32),
        in_specs=[pl.BlockSpec((tm, D), lambda cb, i: (i, CONV_BLOCK0 + cb)),
                  pl.BlockSpec((HALO, D), lambda cb, i: (jnp.maximum(i * r - 1, 0), CONV_BLOCK0 + cb)),
                  pl.BlockSpec((CONV_TAPS, D), lambda cb, i: (0, cb)), pl.BlockSpec((1, D), lambda cb, i: (0, cb))],
        out_specs=pl.BlockSpec((tm, D), lambda cb, i: (i, cb)),
        compiler_params=_cparams(("parallel", "parallel")),
    )(proj, proj, conv_w, conv_b)


def conv_bwd(proj, dxc, conv_w, conv_b):
    t = proj.shape[0]
    tm = _tile(t, ROW_TILE)
    r = tm // HALO
    n = t // tm
    last_halo = t // HALO - 1

    def body(x_ref, prev_ref, next_ref, d_ref, dnext_ref, w_ref, b_ref, dx_ref, dw_ref, db_ref):
        i = pl.program_id(1)

        @pl.when(i == 0)
        def _():
            dw_ref[...] = jnp.zeros_like(dw_ref)
            db_ref[...] = jnp.zeros_like(db_ref)

        prev = jnp.where(i > 0, prev_ref[...], 0.0)
        ext = jnp.concatenate([prev, x_ref[...], next_ref[...]], axis=0)
        shifted = [pltpu.roll(ext, CONV_TAPS - 1 - tap, axis=0)[HALO:, :] for tap in range(CONV_TAPS - 1)]
        shifted.append(ext[HALO:, :])
        pre = b_ref[...]
        for tap in range(CONV_TAPS):
            pre = pre + w_ref[tap:tap + 1, :] * shifted[tap]
        s = _sigmoid(pre)
        d_ext = jnp.concatenate([d_ref[...], jnp.where(i < n - 1, dnext_ref[...], 0.0)], axis=0)
        dpre = d_ext * _dsilu(pre, s)
        dx = w_ref[CONV_TAPS - 1:CONV_TAPS, :] * dpre[:tm, :]
        for tap in range(CONV_TAPS - 1):
            back = CONV_TAPS - 1 - tap
            dx = dx + w_ref[tap:tap + 1, :] * pltpu.roll(dpre, tm + HALO - back, axis=0)[:tm, :]
        dx_ref[...] = dx.astype(BF16)
        dp = dpre[:tm, :]
        db_ref[...] += _colsum(dp)
        for tap in range(CONV_TAPS):
            dw_ref[tap:tap + 1, :] += _colsum(dp * shifted[tap][:tm, :])

    return pl.pallas_call(
        body, name="conv_bwd", grid=(CONV_DIM // D, n),
        out_shape=[jax.ShapeDtypeStruct((t, CONV_DIM), BF16), jax.ShapeDtypeStruct((8, CONV_DIM), F32),
                   jax.ShapeDtypeStruct((1, CONV_DIM), F32)],
        in_specs=[pl.BlockSpec((tm, D), lambda cb, i: (i, CONV_BLOCK0 + cb)),
                  pl.BlockSpec((HALO, D), lambda cb, i: (jnp.maximum(i * r - 1, 0), CONV_BLOCK0 + cb)),
                  pl.BlockSpec((HALO, D), lambda cb, i: (jnp.minimum((i + 1) * r, last_halo), CONV_BLOCK0 + cb)),
                  pl.BlockSpec((tm, D), lambda cb, i: (i, cb)),
                  pl.BlockSpec((HALO, D), lambda cb, i: (jnp.minimum((i + 1) * r, last_halo), cb)),
                  pl.BlockSpec((CONV_TAPS, D), lambda cb, i: (0, cb)), pl.BlockSpec((1, D), lambda cb, i: (0, cb))],
        out_specs=[pl.BlockSpec((tm, D), lambda cb, i: (i, cb)), pl.BlockSpec((8, D), lambda cb, i: (0, cb)),
                   pl.BlockSpec((1, D), lambda cb, i: (0, cb))],
        compiler_params=_cparams(("parallel", "arbitrary")),
    )(proj, proj, proj, dxc, dxc, conv_w, conv_b)


Z_BLOCK0 = 8
DT_BLOCK0 = 88
B_BLOCK0 = 16
C_BLOCK0 = 20


def _head_expand():
    e = np.zeros((N_STATE, GROUP_W), np.float32)
    for hh in range(HEADS_PER_GROUP):
        e[hh, hh * HEAD_P:(hh + 1) * HEAD_P] = 1.0
    return jnp.asarray(e)


def _ssd_chunk_terms(dt, bias, alog, expand, tril_f, eye):
    dtb = dt + bias
    delta = jnp.maximum(dtb, 0.0) + jnp.log(1.0 + jnp.exp(-jnp.abs(dtb)))
    ea = jnp.exp(alog)
    a = -ea * delta
    acum = _nn(tril_f, a, HI)
    delta_e = _nn(delta, expand, HI)
    acum_e = _nn(acum, expand, HI)
    acum_t = _nt(eye, acum, HI)
    return dtb, delta, ea, a, acum, delta_e, acum_e, acum_t


def ssd_fwd(proj, xc, alog4, bias4, dskip4, wnorm, expand):
    t = proj.shape[0]
    tb = _tile(t, TOKEN_BLOCK)
    ncb = tb // CHUNK

    def body(xs_ref, b_ref, c_ref, dt_ref, z_ref, alog_ref, bias_ref, dsk_ref, wn_ref, e_ref, ob_ref, st_ref, state):
        @pl.when(pl.program_id(1) == 0)
        def _():
            state[...] = jnp.zeros_like(state)

        expand = e_ref[...]
        mask = _tri(CHUNK)
        tril_f = mask.astype(F32)
        eye = (lax.broadcasted_iota(jnp.int32, (N_STATE, N_STATE), 0) ==
               lax.broadcasted_iota(jnp.int32, (N_STATE, N_STATE), 1)).astype(F32)
        alog, bias = alog_ref[0], bias_ref[0]
        d_e = _nn(jnp.broadcast_to(dsk_ref[0], (8, N_STATE)), expand, HI)[0:1, :]
        wn = wn_ref[...]

        def chunk(c, carry):
            sl = pl.ds(pl.multiple_of(c * CHUNK, CHUNK), CHUNK)
            xs, bm, cm, dt, z = xs_ref[sl, :], b_ref[sl, :], c_ref[sl, :], dt_ref[sl, :], z_ref[sl, :]
            dtb, delta, ea, a, acum, delta_e, acum_e, acum_t = _ssd_chunk_terms(dt, bias, alog, expand, tril_f, eye)
            alast_e = acum_e[CHUNK - 1:CHUNK, :]
            xd = xs * delta_e
            xdb = xd.astype(BF16)
            cb_, bb_ = cm.astype(BF16), bm.astype(BF16)
            cbm = _nt(cb_, bb_)
            ys = []
            for hh in range(HEADS_PER_GROUP):
                lh = jnp.where(mask, jnp.exp(jnp.minimum(acum[:, hh:hh + 1] - acum_t[hh:hh + 1, :], 0.0)), 0.0)
                ys.append(_nn((cbm * lh).astype(BF16), xdb[:, hh * HEAD_P:(hh + 1) * HEAD_P]))
            st = state[...]
            st_ref[0, c] = st
            y = jnp.concatenate(ys, axis=1) + _nn(cb_, st.astype(BF16)) * jnp.exp(acum_e) + xs * d_e
            state[...] = st * jnp.exp(alast_e) + _tn(bb_, (xd * jnp.exp(alast_e - acum_e)).astype(BF16))
            yg = y * z * _sigmoid(z)
            ob_ref[sl, :] = (yg * lax.rsqrt(jnp.mean(yg * yg, axis=-1, keepdims=True) + RMS_EPS) * wn).astype(BF16)
            return carry

        lax.fori_loop(0, ncb, chunk, 0)

    small = pl.BlockSpec((1, 1, N_STATE), lambda g, j: (g, 0, 0))
    return pl.pallas_call(
        body, name="ssd_fwd", grid=(N_GROUPS, t // tb),
        out_shape=[jax.ShapeDtypeStruct((t, B_INNER), BF16),
                   jax.ShapeDtypeStruct((N_GROUPS, t // CHUNK, N_STATE, GROUP_W), F32)],
        in_specs=[pl.BlockSpec((tb, GROUP_W), lambda g, j: (j, g)),
                  pl.BlockSpec((tb, N_STATE), lambda g, j: (j, B_BLOCK0 + g)),
                  pl.BlockSpec((tb, N_STATE), lambda g, j: (j, C_BLOCK0 + g)),
                  pl.BlockSpec((tb, N_STATE), lambda g, j: (j, DT_BLOCK0 + g)),
                  pl.BlockSpec((tb, GROUP_W), lambda g, j: (j, Z_BLOCK0 + g)),
                  small, small, small, pl.BlockSpec((1, GROUP_W), lambda g, j: (0, g)),
                  pl.BlockSpec((N_STATE, GROUP_W), lambda g, j: (0, 0))],
        out_specs=[pl.BlockSpec((tb, GROUP_W), lambda g, j: (j, g)),
                   pl.BlockSpec((1, ncb, N_STATE, GROUP_W), lambda g, j: (g, j, 0, 0))],
        scratch_shapes=[pltpu.VMEM((N_STATE, GROUP_W), F32)],
        compiler_params=_cparams(("parallel", "arbitrary")),
    )(xc, xc, xc, proj, proj, alog4, bias4, dskip4, wnorm, expand)


def ssd_bwd(proj, xc, alog4, bias4, dskip4, wnorm, expand, dob, states):
    t = proj.shape[0]
    tb = _tile(t, TOKEN_BLOCK)
    ncb = tb // CHUNK
    nb = t // tb

    def body(xs_ref, b_ref, c_ref, dt_ref, z_ref, alog_ref, bias_ref, dsk_ref, wn_ref, e_ref, dob_ref, st_ref,
             dxs_ref, db_ref, dc_ref, dz_ref, ddt_ref, dwn_ref, dalog_ref, dbias_ref, ddsk_ref, dstate):
        @pl.when(pl.program_id(1) == 0)
        def _():
            dstate[...] = jnp.zeros_like(dstate)
            dwn_ref[...] = jnp.zeros_like(dwn_ref)
            dalog_ref[...] = jnp.zeros_like(dalog_ref)
            dbias_ref[...] = jnp.zeros_like(dbias_ref)
            ddsk_ref[...] = jnp.zeros_like(ddsk_ref)

        expand = e_ref[...]
        mask = _tri(CHUNK)
        mask_t = _tri(CHUNK, upper=True)
        tril_f = mask.astype(F32)
        triu_f = mask_t.astype(F32)
        eye = (lax.broadcasted_iota(jnp.int32, (N_STATE, N_STATE), 0) ==
               lax.broadcasted_iota(jnp.int32, (N_STATE, N_STATE), 1)).astype(F32)
        alog, bias = alog_ref[0], bias_ref[0]
        d_e = _nn(jnp.broadcast_to(dsk_ref[0], (8, N_STATE)), expand, HI)[0:1, :]
        wn = wn_ref[...]

        def chunk(i, c0):
            c = ncb - 1 - i
            sl = pl.ds(pl.multiple_of(c * CHUNK, CHUNK), CHUNK)
            xs, bm, cm, dt, z = xs_ref[sl, :], b_ref[sl, :], c_ref[sl, :], dt_ref[sl, :], z_ref[sl, :]
            dtb, delta, ea, a, acum, delta_e, acum_e, acum_t = _ssd_chunk_terms(dt, bias, alog, expand, tril_f, eye)
            alast_e = acum_e[CHUNK - 1:CHUNK, :]
            eacum = jnp.exp(acum_e)
            wl = jnp.exp(alast_e - acum_e)
            xd = xs * delta_e
            xdb = xd.astype(BF16)
            cb_, bb_ = cm.astype(BF16), bm.astype(BF16)
            cbm = _nt(cb_, bb_)
            cbm_t = _nt(bb_, cb_)
            st32 = st_ref[0, c]
            stb = st32.astype(BF16)
            dst = dstate[...]
            dstb = dst.astype(BF16)
            lhs, lhts, ys = [], [], []
            for hh in range(HEADS_PER_GROUP):
                col, row = acum[:, hh:hh + 1], acum_t[hh:hh + 1, :]
                lh = jnp.where(mask, jnp.exp(jnp.minimum(col - row, 0.0)), 0.0)
                lht = jnp.where(mask_t, jnp.exp(jnp.minimum(row - col, 0.0)), 0.0)
                lhs.append(lh)
                lhts.append(lht)
                ys.append(_nn((cbm * lh).astype(BF16), xdb[:, hh * HEAD_P:(hh + 1) * HEAD_P]))
            y_in = jnp.concatenate(ys, axis=1)
            y_out = _nn(cb_, stb) * eacum
            y = y_in + y_out + xs * d_e
            sgz = _sigmoid(z)
            sz = z * sgz
            yg = y * sz
            rstd = lax.rsqrt(jnp.mean(yg * yg, axis=-1, keepdims=True) + RMS_EPS)
            nrm = yg * rstd
            dob_v = dob_ref[sl, :]
            dn = dob_v * wn
            dwn_ref[...] += _colsum(dob_v * nrm)
            dyg = rstd * (dn - nrm * jnp.mean(dn * nrm, axis=-1, keepdims=True))
            dy = dyg * sz
            dz_ref[sl, :] = (dyg * y * _dsilu(z, sgz)).astype(BF16)
            dyb = dy.astype(BF16)
            dxds = []
            dcb = jnp.zeros((CHUNK, CHUNK), F32)
            dcb_t = jnp.zeros((CHUNK, CHUNK), F32)
            for hh in range(HEADS_PER_GROUP):
                hs = slice(hh * HEAD_P, (hh + 1) * HEAD_P)
                dy_h, x_h = dyb[:, hs], xdb[:, hs]
                dxds.append(_nn((cbm_t * lhts[hh]).astype(BF16), dy_h))
                dcb = dcb + _nt(dy_h, x_h) * lhs[hh]
                dcb_t = dcb_t + _nt(x_h, dy_h) * lhts[hh]
            dye = (dy * eacum).astype(BF16)
            xw = (xd * wl).astype(BF16)
            dxd_in = jnp.concatenate(dxds, axis=1)
            dxd_out = wl * _nn(bb_, dstb)
            dxd = dxd_in + dxd_out
            dc_ref[sl, :] = _nn(dcb.astype(BF16), bb_) + _nt(dye, stb)
            db_ref[sl, :] = _nn(dcb_t.astype(BF16), cb_) + _nt(xw, dstb)
            dstate[...] = dst * jnp.exp(alast_e) + _tn(cb_, dye)
            col_out = xd * dxd_out
            dac = _nt(dyb.astype(F32) * y_in - xdb.astype(F32) * dxd_in + dy * y_out - col_out, expand, HI)
            beyond = _colsum(col_out) + jnp.exp(alast_e) * _colsum(dst * st32)
            da = _nn(triu_f, dac, HI) + _nt(jnp.broadcast_to(beyond, (8, GROUP_W)), expand, HI)[0:1, :]
            ddelta = _nt(dxd * xs, expand, HI) - da * ea
            dalog_ref[0] += _colsum(da * a)
            ddtb = ddelta * _sigmoid(dtb)
            dbias_ref[0] += _colsum(ddtb)
            ddt_ref[sl, :] = ddtb.astype(BF16)
            ddsk_ref[0] += _colsum(_nt(dy * xs, expand, HI))
            dxs_ref[sl, :] = dxd * delta_e + dy * d_e
            return c0

        lax.fori_loop(0, ncb, chunk, 0)

    small = pl.BlockSpec((1, 1, N_STATE), lambda g, j: (g, 0, 0))
    wide = pl.BlockSpec((tb, GROUP_W), lambda g, j: (nb - 1 - j, g))
    narrow = pl.BlockSpec((tb, N_STATE), lambda g, j: (nb - 1 - j, g))
    return pl.pallas_call(
        body, name="ssd_bwd", grid=(N_GROUPS, nb),
        out_shape=[jax.ShapeDtypeStruct((t, B_INNER), F32), jax.ShapeDtypeStruct((t, GROUP_W), F32),
                   jax.ShapeDtypeStruct((t, GROUP_W), F32), jax.ShapeDtypeStruct((t, B_INNER), BF16),
                   jax.ShapeDtypeStruct((t, GROUP_W), BF16), jax.ShapeDtypeStruct((1, B_INNER), F32),
                   jax.ShapeDtypeStruct((N_GROUPS, 1, N_STATE), F32), jax.ShapeDtypeStruct((N_GROUPS, 1, N_STATE), F32),
                   jax.ShapeDtypeStruct((N_GROUPS, 1, N_STATE), F32)],
        in_specs=[wide,
                  pl.BlockSpec((tb, N_STATE), lambda g, j: (nb - 1 - j, B_BLOCK0 + g)),
                  pl.BlockSpec((tb, N_STATE), lambda g, j: (nb - 1 - j, C_BLOCK0 + g)),
                  pl.BlockSpec((tb, N_STATE), lambda g, j: (nb - 1 - j, DT_BLOCK0 + g)),
                  pl.BlockSpec((tb, GROUP_W), lambda g, j: (nb - 1 - j, Z_BLOCK0 + g)),
                  small, small, small, pl.BlockSpec((1, GROUP_W), lambda g, j: (0, g)),
                  pl.BlockSpec((N_STATE, GROUP_W), lambda g, j: (0, 0)), wide,
                  pl.BlockSpec((1, ncb, N_STATE, GROUP_W), lambda g, j: (g, nb - 1 - j, 0, 0))],
        out_specs=[wide, narrow, narrow, wide, narrow, pl.BlockSpec((1, GROUP_W), lambda g, j: (0, g)),
                   small, small, small],
        scratch_shapes=[pltpu.VMEM((N_STATE, GROUP_W), F32)],
        compiler_params=_cparams(("parallel", "arbitrary")),
    )(xc, xc, xc, proj, proj, alog4, bias4, dskip4, wnorm, expand, dob, states)


def lower_bound_fwd(hgrn_lb):
    def body(a_ref, o_ref):
        a0, a1 = a_ref[0:1, :], a_ref[1:2, :]
        m = jnp.maximum(a0, a1)
        e0, e1 = jnp.exp(a0 - m), jnp.exp(a1 - m)
        o_ref[...] = e0 / (e0 + e1)

    return pl.pallas_call(body, name="lower_bound_fwd", out_shape=jax.ShapeDtypeStruct((1, D), F32))(hgrn_lb)


def ada_weight_grad(c_all, dmod_cols):
    def body(c_ref, d_ref, o_ref):
        cval = c_ref[...]
        o_ref[...] = _tn(cval * _sigmoid(cval), d_ref[...], HI)

    return pl.pallas_call(body, name="ada_weight_grad",
                          out_shape=jax.ShapeDtypeStruct((D, dmod_cols.shape[1]), F32))(c_all, dmod_cols)


def reduce_small(gathered, hgrn_lb, dlb_off):
    n = gathered.shape[2]

    def body(g_ref, a_ref, o_ref, glb_ref):
        s = g_ref[0]
        for d in range(1, N_DEV):
            s = s + g_ref[d]
        o_ref[...] = s
        a0, a1 = a_ref[0:1, :], a_ref[1:2, :]
        m = jnp.maximum(a0, a1)
        e0, e1 = jnp.exp(a0 - m), jnp.exp(a1 - m)
        p0 = e0 / (e0 + e1)
        tq = s[:, dlb_off:dlb_off + D] * p0 * (1.0 - p0)
        glb_ref[0:1, :] = tq
        glb_ref[1:2, :] = -tq

    return pl.pallas_call(body, name="reduce_small",
                          out_shape=[jax.ShapeDtypeStruct((1, n), F32), jax.ShapeDtypeStruct((2, D), F32)])(gathered, hgrn_lb)


def _adam_math(w, g, m, v):
    m2 = ADAM_B1 * m + (1.0 - ADAM_B1) * g
    v2 = ADAM_B2 * v + (1.0 - ADAM_B2) * (g * g)
    m_hat = m2 / (1.0 - ADAM_B1 ** ADAM_STEP)
    v_hat = v2 / (1.0 - ADAM_B2 ** ADAM_STEP)
    delta = -ADAM_LR * (m_hat / (jnp.sqrt(v_hat) + ADAM_EPS) + ADAM_WD * w)
    return delta, m2, v2


def _row_tile(rows, mult=8, cap=128):
    for cand in range(cap - cap % mult, 0, -mult):
        if rows % cand == 0:
            return cand
    return rows


def sum_parts(parts, name):
    n, rows, cols = parts.shape
    tr = _row_tile(rows, 16, 256)

    def body(p_ref, o_ref):
        s = p_ref[0].astype(F32)
        for d in range(1, n):
            s = s + p_ref[d].astype(F32)
        o_ref[...] = s

    return pl.pallas_call(
        body, name=name, grid=(rows // tr,),
        out_shape=jax.ShapeDtypeStruct((rows, cols), F32),
        in_specs=[pl.BlockSpec((n, tr, cols), lambda i: (0, i, 0))],
        out_specs=pl.BlockSpec((tr, cols), lambda i: (i, 0)),
        compiler_params=_cparams(("parallel",)),
    )(parts)


def sum_pair(a, b, name):
    rows, cols = a.shape
    tr = _row_tile(rows, 16, 256)

    def body(a_ref, b_ref, o_ref):
        o_ref[...] = (a_ref[...].astype(F32) + b_ref[...].astype(F32)).astype(o_ref.dtype)

    blk = pl.BlockSpec((tr, cols), lambda i: (i, 0))
    return pl.pallas_call(
        body, name=name, grid=(rows // tr,),
        out_shape=jax.ShapeDtypeStruct((rows, cols), a.dtype),
        in_specs=[blk, blk], out_specs=blk,
        compiler_params=_cparams(("parallel",)),
    )(a, b)


def adamw(w, g, m, v, name):
    rows, cols = w.shape
    tr = _row_tile(rows)

    def body(w_ref, g_ref, m_ref, v_ref, d_ref, m2_ref, v2_ref):
        delta, m2, v2 = _adam_math(w_ref[...], g_ref[...], m_ref[...], v_ref[...])
        d_ref[...] = delta
        m2_ref[...] = m2
        v2_ref[...] = v2

    blk = pl.BlockSpec((tr, cols), lambda i: (i, 0))
    return pl.pallas_call(
        body, name=name, grid=(rows // tr,),
        out_shape=[jax.ShapeDtypeStruct((rows, cols), F32)] * 3,
        in_specs=[blk] * 4, out_specs=[blk] * 3,
        compiler_params=_cparams(("parallel",)),
    )(w, g, m, v)


def _pad128(n):
    return -(-n // 128) * 128


def _pack(arrays):
    offs, parts, off = [], [], 0
    for a in arrays:
        flat = a.reshape(1, -1)
        n = flat.shape[1]
        offs.append(off)
        parts.append(jnp.pad(flat, ((0, 0), (0, _pad128(n) - n))))
        off += _pad128(n)
    return jnp.concatenate(parts, axis=1), offs


def _unpack(vec, offs, shapes):
    out = []
    for off, shp in zip(offs, shapes):
        n = int(np.prod(shp))
        out.append(vec[0, off:off + n].reshape(shp))
    return out


def _permute_in_rows(w_t):
    dt = w_t[9216:9248].reshape(N_GROUPS, HEADS_PER_GROUP, D)
    dt = jnp.pad(dt, ((0, 0), (0, N_STATE - HEADS_PER_GROUP), (0, 0))).reshape(N_GROUPS * N_STATE, D)
    dt = jnp.pad(dt, ((0, D - N_GROUPS * N_STATE), (0, 0)))
    return jnp.concatenate([w_t[:9216], w_t[9248:], dt], axis=0)


def _unpermute_in_cols(g):
    dt = g[:, 11264:11264 + N_GROUPS * N_STATE].reshape(D, N_GROUPS, N_STATE)[:, :, :HEADS_PER_GROUP].reshape(D, 32)
    return jnp.concatenate([g[:, :9216], dt, g[:, 9216:11264]], axis=1)


def _column_blocks(g):
    rows = g.shape[0]
    return g.reshape(rows, N_DEV, -1).transpose(1, 0, 2)


def kernel(x, c, w_ada, b_ada, w_in, hgrn_lb, hgrn_gnorm, ssm_conv_w, ssm_conv_b, ssm_dt_bias, ssm_a_log, ssm_d, ssm_norm, w_branch_a, w_branch_b, w_o, ln1_g, ln1_b, w_ffn_gate, w_ffn_up, w_ffn_down, ln2_g, ln2_b, loss_target, m_w_ada, m_b_ada, m_w_in, m_hgrn_lb, m_hgrn_gnorm, m_ssm_conv_w, m_ssm_conv_b, m_ssm_dt_bias, m_ssm_a_log, m_ssm_d, m_ssm_norm, m_w_branch_a, m_w_branch_b, m_w_o, m_ln1_g, m_ln1_b, m_w_ffn_gate, m_w_ffn_up, m_w_ffn_down, m_ln2_g, m_ln2_b, v_w_ada, v_b_ada, v_w_in, v_hgrn_lb, v_hgrn_gnorm, v_ssm_conv_w, v_ssm_conv_b, v_ssm_dt_bias, v_ssm_a_log, v_ssm_d, v_ssm_norm, v_w_branch_a, v_w_branch_b, v_w_o, v_ln1_g, v_ln1_b, v_w_ffn_gate, v_w_ffn_up, v_w_ffn_down, v_ln2_g, v_ln2_b):
    me = 4 * lax.axis_index("x") + 2 * lax.axis_index("y") + lax.axis_index("c")
    xt = x[0]
    tgt = loss_target[0]
    t = xt.shape[0]
    ada_cols = w_ada.shape[2]
    conv_cols = ssm_conv_w.shape[2]

    small_in, _ = _pack([c, ssm_conv_w[0]])
    small_all = allgather_vmem(small_in, "allgather_small_inputs")
    c_all = small_all[:, 0, :D]
    conv_w = small_all[:, 0, D:D + CONV_TAPS * conv_cols].reshape(N_DEV, CONV_TAPS, conv_cols)
    conv_w = conv_w.transpose(1, 0, 2).reshape(CONV_TAPS, CONV_DIM)
    mod = ada_modulation(c_all, w_ada[0], b_ada.reshape(N_DEV, 1, ada_cols))
    mod6 = mod.reshape(6, D)

    shards = [w_in[0].T, w_branch_a[0], w_branch_b[0], w_o[0], w_ffn_gate[0].T, w_ffn_up[0].T, w_ffn_down[0]]
    shard_rows = [s.shape[0] for s in shards]
    slot_rows = [-(-r // 16) * 16 for r in shard_rows]
    row_offs = [sum(slot_rows[:i]) for i in range(len(shards))]
    rows_pad = sum(slot_rows)
    stacked = jnp.concatenate([jnp.pad(s.astype(BF16), ((0, p - r), (0, 0)))
                               for s, r, p in zip(shards, shard_rows, slot_rows)], axis=0)
    gathered = allgather_hbm(stacked, "allgather_weights")
    g_in, g_ba, g_bb, g_o, g_fg, g_fu, g_fd = (gathered[:, o:o + r] for o, r in zip(row_offs, shard_rows))
    w_in_t = _permute_in_rows(g_in.reshape(IN_DIM, D))
    w_in_p = w_in_t.T
    w_ba = g_ba.reshape(D, D)
    w_bb = g_bb.reshape(B_INNER, D)
    w_oo = g_o.reshape(D, D)
    ffpad = ((0, D_FF_PAD - D_FF), (0, 0))
    w_gu_t = jnp.concatenate([jnp.pad(g_fg.reshape(D_FF, D), ffpad), jnp.pad(g_fu.reshape(D_FF, D), ffpad)], axis=0)
    w_gu = w_gu_t.T
    w_dn = jnp.pad(g_fd.reshape(D_FF, D), ffpad)

    lb = lower_bound_fwd(hgrn_lb)
    u1 = ln_modulate(xt, mod6, 0, 1, "ln_modulate_1")
    proj = mm_nn(u1, w_in_p, F32, "mm_in_proj")
    o_a, o_raw, st_a = hgrn_fwd(proj, lb, hgrn_gnorm)
    xc = conv_fwd(proj, conv_w, ssm_conv_b)
    pad3 = ((0, 0), (0, 0), (0, N_STATE - HEADS_PER_GROUP))
    alog4 = jnp.pad(ssm_a_log.reshape(N_GROUPS, 1, HEADS_PER_GROUP), pad3)
    bias4 = jnp.pad(ssm_dt_bias.reshape(N_GROUPS, 1, HEADS_PER_GROUP), pad3)
    dskip4 = jnp.pad(ssm_d.reshape(N_GROUPS, 1, HEADS_PER_GROUP), pad3)
    expand = _head_expand()
    o_b, st_b = ssd_fwd(proj, xc, alog4, bias4, dskip4, ssm_norm, expand)
    ya = mm_nn(o_a, w_ba, F32, "mm_branch_a")
    yb = mm_nn(o_b, w_bb, F32, "mm_branch_b")
    merged = merge_gates(ya, yb, proj)
    h1 = mm_nn(merged, w_oo, F32, "mm_out_proj")
    x1 = resid_ln(xt, h1, mod6, 2, ln1_g, ln1_b, "resid_ln_1")
    u2 = ln_modulate(x1, mod6, 3, 4, "ln_modulate_2")
    gu = mm_nn(u2, w_gu, F32, "mm_ffn_in")
    act = swiglu_act(gu)
    h2 = mm_nn(act, w_dn, F32, "mm_ffn_out")

    dh2, dx1_part, acc4 = resid_ln_bwd(x1, h2, mod6, 5, ln2_g, ln2_b, tgt, True, "resid_ln_2_bwd")
    g_dn = mm_tn(act, dh2, "mm_grad_ffn_down")
    dact = mm_nn(dh2, w_dn.T, F32, "mm_dact")
    dgu = swiglu_act_bwd(gu, dact)
    g_gu = mm_tn(u2, dgu, "mm_grad_ffn_in")
    du2 = mm_nn(dgu, w_gu_t, F32, "mm_du2")
    dx1, acc3 = ln_modulate_bwd(x1, du2, mod6, 4, dx1_part, "ln_modulate_2_bwd")
    dh1, dx_part, acc2 = resid_ln_bwd(xt, h1, mod6, 2, ln1_g, ln1_b, dx1, False, "resid_ln_1_bwd")
    g_o = mm_tn(merged, dh1, "mm_grad_out_proj")
    dmerged = mm_nn(dh1, w_oo.T, F32, "mm_dmerged")
    dya, dyb, dga, dgb = merge_gates_bwd(dmerged, ya, yb, proj)
    g_ba_full = mm_tn(o_a, dya, "mm_grad_branch_a")
    g_bb_full = mm_tn(o_b, dyb, "mm_grad_branch_b")
    doa = mm_nn(dya, w_ba.T, F32, "mm_doa")
    dob = mm_nn(dyb, w_bb.T, F32, "mm_dob")
    dq, dfl, di, dg, dlb, dgn = hgrn_bwd(proj, lb, hgrn_gnorm, o_raw, doa, st_a)
    dxs, dbm, dcm, dz, ddt, dwn, dalog, dbias, ddsk = ssd_bwd(proj, xc, alog4, bias4, dskip4, ssm_norm, expand, dob, st_b)
    dxc = jnp.concatenate([dxs, dbm, dcm], axis=1)
    dxbc, dcw, dcb = conv_bwd(proj, dxc, conv_w, ssm_conv_b)
    dproj = jnp.concatenate([dq, dfl, di, dg, dz, dxbc, dga, dgb, ddt, jnp.zeros((t, D - N_GROUPS * N_STATE), BF16)], axis=1)
    g_in = mm_tn(u1, dproj, "mm_grad_in_proj")
    du1 = mm_nn(dproj, w_in_t, F32, "mm_du1")
    dx, acc1 = ln_modulate_bwd(xt, du1, mod6, 1, dx_part, "ln_modulate_1_bwd")

    blocks = [_column_blocks(_unpermute_in_cols(g_in)), g_ba_full.reshape(N_DEV, -1, D), g_bb_full.reshape(N_DEV, -1, D),
              g_o.reshape(N_DEV, -1, D), _column_blocks(g_gu[:, :D_FF]),
              _column_blocks(g_gu[:, D_FF_PAD:D_FF_PAD + D_FF]), g_dn[:D_FF].reshape(N_DEV, -1, D)]
    contrib = jnp.concatenate([jnp.pad(b.reshape(N_DEV, -1, D), ((0, 0), (0, p - r), (0, 0)))
                               for b, r, p in zip(blocks, shard_rows, slot_rows)], axis=1)
    by_core = contrib.reshape(N_DEV // 2, 2, rows_pad, D).transpose(1, 0, 2, 3)
    my_core = lax.axis_index("c")
    keep = lax.dynamic_index_in_dim(by_core, my_core, 0, keepdims=False)
    give = lax.dynamic_index_in_dim(by_core, 1 - my_core, 0, keepdims=False)
    got = exchange_sibling(give, "exchange_grads_sibling")
    chip_part = sum_pair(keep.reshape(-1, D), got.reshape(-1, D), "sum_grads_chip").reshape(keep.shape)
    parts = exchange_chips(chip_part, "exchange_grads_chips")
    g_rows = sum_parts(parts, "sum_grads_all")
    shapes_big = [w_in.shape[1:], w_branch_a.shape[1:], w_branch_b.shape[1:], w_o.shape[1:], w_ffn_gate.shape[1:],
                  w_ffn_up.shape[1:], w_ffn_down.shape[1:]]
    gw_in, gw_ba, gw_bb, gw_o, gw_fg, gw_fu, gw_fd = (
        g_rows[o:o + r].reshape(shp) for o, r, shp in zip(row_offs, shard_rows, shapes_big))

    dmod = jnp.concatenate([acc1[1:2], acc1[0:1], acc2[0:1], acc3[1:2], acc3[0:1], acc4[0:1]], axis=1)
    small_fields = [dmod, acc4[3:4, :128], dlb, dgn, dcw[:CONV_TAPS], dcb, dbias, dalog, ddsk, dwn,
                    acc2[1:2], acc2[2:3], acc4[1:2], acc4[2:3]]
    small_out, offs = _pack(small_fields)
    small_sum_in = allgather_vmem(small_out, "allgather_small_grads")
    gsum, g_lb = reduce_small(small_sum_in, hgrn_lb, offs[2])
    (g_bada, loss_row, _, g_gn, g_cw_full, g_cb, g_bias4, g_alog4, g_dsk4, g_wn, g_l1g, g_l1b, g_l2g, g_l2b) = _unpack(
        gsum, offs, [(1, 6 * D), (1, 128), (1, D), (1, HK), (CONV_TAPS, CONV_DIM), (1, CONV_DIM),
                     (N_GROUPS, N_STATE), (N_GROUPS, N_STATE), (N_GROUPS, N_STATE), (1, B_INNER),
                     (1, D), (1, D), (1, D), (1, D)])
    loss = loss_row[0, 0]
    g_cw = lax.dynamic_slice(g_cw_full, (0, me * conv_cols), (CONV_TAPS, conv_cols))[None]
    g_dtb = g_bias4[:, :HEADS_PER_GROUP].reshape(1, 32)
    g_alog = g_alog4[:, :HEADS_PER_GROUP].reshape(1, 32)
    g_dsk = g_dsk4[:, :HEADS_PER_GROUP].reshape(1, 32)

    dmod_all = small_sum_in[:, 0, offs[0]:offs[0] + 6 * D]
    dmod_cols = lax.dynamic_slice(dmod_all, (0, me * ada_cols), (N_DEV, ada_cols))
    gw_ada = ada_weight_grad(c_all, dmod_cols)

    big = [("ada", w_ada[0], gw_ada, m_w_ada[0], v_w_ada[0]), ("in", w_in[0], gw_in, m_w_in[0], v_w_in[0]),
           ("branch_a", w_branch_a[0], gw_ba, m_w_branch_a[0], v_w_branch_a[0]),
           ("branch_b", w_branch_b[0], gw_bb, m_w_branch_b[0], v_w_branch_b[0]),
           ("o", w_o[0], gw_o, m_w_o[0], v_w_o[0]),
           ("ffn_gate", w_ffn_gate[0], gw_fg, m_w_ffn_gate[0], v_w_ffn_gate[0]),
           ("ffn_up", w_ffn_up[0], gw_fu, m_w_ffn_up[0], v_w_ffn_up[0]),
           ("ffn_down", w_ffn_down[0], gw_fd, m_w_ffn_down[0], v_w_ffn_down[0])]
    big_out = {}
    for nm, w_, g_, m_, v_ in big:
        d_, m2_, v2_ = adamw(w_, g_, m_, v_, "adamw_" + nm)
        big_out[nm] = (g_[None], d_[None], m2_[None], v2_[None])

    small_w = [b_ada, hgrn_lb, hgrn_gnorm, ssm_conv_w, ssm_conv_b, ssm_dt_bias, ssm_a_log, ssm_d, ssm_norm,
               ln1_g, ln1_b, ln2_g, ln2_b]
    small_g = [g_bada, g_lb, g_gn, g_cw, g_cb, g_dtb, g_alog, g_dsk, g_wn, g_l1g, g_l1b, g_l2g, g_l2b]
    small_m = [m_b_ada, m_hgrn_lb, m_hgrn_gnorm, m_ssm_conv_w, m_ssm_conv_b, m_ssm_dt_bias, m_ssm_a_log, m_ssm_d,
               m_ssm_norm, m_ln1_g, m_ln1_b, m_ln2_g, m_ln2_b]
    small_v = [v_b_ada, v_hgrn_lb, v_hgrn_gnorm, v_ssm_conv_w, v_ssm_conv_b, v_ssm_dt_bias, v_ssm_a_log, v_ssm_d,
               v_ssm_norm, v_ln1_g, v_ln1_b, v_ln2_g, v_ln2_b]
    shapes = [a.shape for a in small_w]
    small_g = [g_.reshape(s) for g_, s in zip(small_g, shapes)]
    pw, poffs = _pack(small_w)
    pg, _ = _pack(small_g)
    pm, _ = _pack(small_m)
    pv, _ = _pack(small_v)
    pd, pm2, pv2 = adamw(pw, pg, pm, pv, "adamw_small")
    s_d, s_m, s_v = (_unpack(p, poffs, shapes) for p in (pd, pm2, pv2))
    (sn_bada, sn_lb, sn_gn, sn_cw, sn_cb, sn_dtb, sn_alog, sn_dsk, sn_wn, sn_l1g, sn_l1b, sn_l2g, sn_l2b) = range(13)

    def order(kind):
        sm = [small_g, s_d, s_m, s_v][kind]
        bg = lambda nm: big_out[nm][kind]
        return [bg("ada"), sm[sn_bada], bg("in"), sm[sn_lb], sm[sn_gn], sm[sn_cw], sm[sn_cb], sm[sn_dtb], sm[sn_alog],
                sm[sn_dsk], sm[sn_wn], bg("branch_a"), bg("branch_b"), bg("o"), sm[sn_l1g], sm[sn_l1b],
                bg("ffn_gate"), bg("ffn_up"), bg("ffn_down"), sm[sn_l2g], sm[sn_l2b]]

    return (loss, dx[None], *order(0), *order(1), *order(2), *order(3))
```

```python
import functools

import numpy as np
import jax
import jax.numpy as jnp
from jax import lax
from jax.experimental import pallas as pl
from jax.experimental.pallas import tpu as pltpu

F32 = jnp.float32
BF16 = jnp.bfloat16
HI = lax.Precision.HIGHEST

N_DEV = 8
D = 1024
N_HEADS_A = 8
HK = 128
CHUNK = 64
N_GROUPS = 4
HEADS_PER_GROUP = 8
HEAD_P = 64
N_STATE = 128
GROUP_W = HEADS_PER_GROUP * HEAD_P
B_INNER = 2048
CONV_DIM = 3072
D_FF = 2816
D_FF_PAD = 3072
IN_DIM = 11296
N_PROJ = 12288
ALPHA = 2.0 ** 0.25
LN_EPS = 1e-5
RMS_EPS = 1e-6
Q_SCALE = 128 ** -0.5
EXP_CLIP = 80.0
ADAM_LR, ADAM_B1, ADAM_B2, ADAM_EPS, ADAM_WD, ADAM_STEP = 0.001, 0.9, 0.999, 1e-8, 0.01, 10
VMEM_LIMIT = 48 * 1024 * 1024
TOKEN_BLOCK = 512
ROW_TILE = 256
FFN_ROW_TILE = 128
MM_ROW_TILE = 1024
MM_TOKEN_TILE = 512
CHUNK_UNROLL = 8
MESH_ID = pl.DeviceIdType.MESH

NT_DIMS = (((1,), (1,)), ((), ()))
TN_DIMS = (((0,), (0,)), ((), ()))


def _cparams(sem=None):
    return pltpu.CompilerParams(dimension_semantics=sem, vmem_limit_bytes=VMEM_LIMIT)


def _sigmoid(x):
    return 1.0 / (1.0 + jnp.exp(-x))


def _dsilu(x, s):
    return s * (1.0 + x * (1.0 - s))


def _nt(a, b, precision=None):
    return lax.dot_general(a, b, NT_DIMS, precision=precision, preferred_element_type=F32)


def _tn(a, b, precision=None):
    return lax.dot_general(a, b, TN_DIMS, precision=precision, preferred_element_type=F32)


def _nn(a, b, precision=None):
    return jnp.dot(a, b, precision=precision, preferred_element_type=F32)


def _ln(x):
    mu = jnp.mean(x, axis=-1, keepdims=True)
    xc = x - mu
    rstd = lax.rsqrt(jnp.mean(xc * xc, axis=-1, keepdims=True) + LN_EPS)
    return xc * rstd, rstd


def _ln_bwd(dxh, xh, rstd):
    return rstd * (dxh - jnp.mean(dxh, axis=-1, keepdims=True) - xh * jnp.mean(dxh * xh, axis=-1, keepdims=True))


def _colsum(x):
    return jnp.sum(x, axis=0, keepdims=True)


def _tri(n, upper=False):
    r = lax.broadcasted_iota(jnp.int32, (n, n), 0)
    c = lax.broadcasted_iota(jnp.int32, (n, n), 1)
    return (c >= r) if upper else (r >= c)


def _my_pos():
    return lax.axis_index("x"), lax.axis_index("y"), lax.axis_index("c")


def _peer(pos, k):
    x, y, c = pos
    return (x ^ ((k >> 2) & 1), y ^ ((k >> 1) & 1), c ^ (k & 1))


def _flat(pos):
    return 4 * pos[0] + 2 * pos[1] + pos[2]


def allgather_vmem(v, name):
    n = v.shape[1]

    def body(v_ref, o_ref, send_sems, recv_sems, local_sem):
        me = _my_pos()
        mine = pltpu.make_async_copy(v_ref, o_ref.at[_flat(me)], local_sem)
        mine.start()
        sends = []
        for k in range(1, N_DEV):
            peer = _peer(me, k)
            cp = pltpu.make_async_remote_copy(v_ref, o_ref.at[_flat(me)], send_sems.at[k - 1], recv_sems.at[k - 1],
                                              device_id=peer, device_id_type=MESH_ID)
            cp.start()
            sends.append(cp)
        for k in range(1, N_DEV):
            peer = _peer(me, k)
            pltpu.make_async_remote_copy(v_ref, o_ref.at[_flat(peer)], send_sems.at[k - 1], recv_sems.at[k - 1],
                                         device_id=peer, device_id_type=MESH_ID).wait_recv()
        for cp in sends:
            cp.wait_send()
        mine.wait()

    return pl.pallas_call(
        body, name=name,
        out_shape=jax.ShapeDtypeStruct((N_DEV, 1, n), F32),
        in_specs=[pl.BlockSpec(memory_space=pltpu.VMEM)],
        out_specs=pl.BlockSpec(memory_space=pltpu.VMEM),
        scratch_shapes=[pltpu.SemaphoreType.DMA((N_DEV - 1,)), pltpu.SemaphoreType.DMA((N_DEV - 1,)),
                        pltpu.SemaphoreType.DMA],
        compiler_params=_cparams(),
    )(v)


def ada_modulation(c_all, w_ada_s, b_ada_r):
    ncol = w_ada_s.shape[1]

    def body(c_ref, w_ref, b_ref, o_ref, part_ref, send_sems, recv_sems):
        me = _my_pos()
        cval = c_ref[...]
        cond = cval * _sigmoid(cval)
        part = _nn(cond, w_ref[...], HI)
        for r in range(N_DEV):
            part_ref[r] = part[r:r + 1, :]
        sends = []
        for k in range(1, N_DEV):
            peer = _peer(me, k)
            cp = pltpu.make_async_remote_copy(part_ref.at[_flat(peer)], o_ref.at[_flat(me)], send_sems.at[k - 1],
                                              recv_sems.at[k - 1], device_id=peer, device_id_type=MESH_ID)
            cp.start()
            sends.append(cp)
        o_ref[_flat(me)] = part_ref[_flat(me)]
        for k in range(1, N_DEV):
            peer = _peer(me, k)
            pltpu.make_async_remote_copy(part_ref.at[_flat(peer)], o_ref.at[_flat(peer)], send_sems.at[k - 1],
                                         recv_sems.at[k - 1], device_id=peer, device_id_type=MESH_ID).wait_recv()
        for cp in sends:
            cp.wait_send()
        o_ref[...] = o_ref[...] + b_ref[...]

    return pl.pallas_call(
        body, name="ada_modulation",
        out_shape=jax.ShapeDtypeStruct((N_DEV, 1, ncol), F32),
        in_specs=[pl.BlockSpec(memory_space=pltpu.VMEM)] * 3,
        out_specs=pl.BlockSpec(memory_space=pltpu.VMEM),
        scratch_shapes=[pltpu.VMEM((N_DEV, 1, ncol), F32), pltpu.SemaphoreType.DMA((N_DEV - 1,)),
                        pltpu.SemaphoreType.DMA((N_DEV - 1,))],
        compiler_params=_cparams(),
    )(c_all, w_ada_s, b_ada_r)


def allgather_hbm(shard, name):
    def body(x_ref, out_ref, send_sems, recv_sems, local_sem):
        x, y, c = _my_pos()
        me, sibling = (x, y, c), (x, y, 1 - c)
        chips = [(1 - x, y), (x, 1 - y), (1 - x, 1 - y)]

        def slot(pos):
            return out_ref.at[_flat(pos)]

        def copy(k, block, to, src=None):
            return pltpu.make_async_remote_copy(slot(block) if src is None else src, slot(block), send_sems.at[k],
                                                recv_sems.at[k], device_id=to, device_id_type=MESH_ID)

        mine = pltpu.make_async_copy(x_ref, slot(me), local_sem)
        mine.start()
        first = [copy(0, me, sibling, src=x_ref)]
        first += [copy(1 + j, me, (*chip, c), src=x_ref) for j, chip in enumerate(chips)]
        for cp in first:
            cp.start()
        passed = [copy(4 + j, (*chip, c), sibling) for j, chip in enumerate(chips)]
        for j, chip in enumerate(chips):
            copy(1 + j, (*chip, c), me).wait_recv()
            passed[j].start()
        copy(0, sibling, me).wait_recv()
        for j, chip in enumerate(chips):
            copy(4 + j, (*chip, 1 - c), me).wait_recv()
        for cp in first + passed:
            cp.wait_send()
        mine.wait()

    return pl.pallas_call(
        body, name=name,
        out_shape=jax.ShapeDtypeStruct((N_DEV,) + shard.shape, shard.dtype),
        in_specs=[pl.BlockSpec(memory_space=pl.ANY)],
        out_specs=pl.BlockSpec(memory_space=pl.ANY),
        scratch_shapes=[pltpu.SemaphoreType.DMA((N_DEV - 1,)), pltpu.SemaphoreType.DMA((N_DEV - 1,)),
                        pltpu.SemaphoreType.DMA],
        compiler_params=_cparams(),
    )(shard)


def exchange_sibling(send, name):
    def body(s_ref, o_ref, send_sem, recv_sem):
        x, y, c = _my_pos()
        cp = pltpu.make_async_remote_copy(s_ref, o_ref, send_sem, recv_sem, device_id=(x, y, 1 - c),
                                          device_id_type=MESH_ID)
        cp.start()
        cp.wait()

    return pl.pallas_call(
        body, name=name,
        out_shape=jax.ShapeDtypeStruct(send.shape, send.dtype),
        in_specs=[pl.BlockSpec(memory_space=pl.ANY)],
        out_specs=pl.BlockSpec(memory_space=pl.ANY),
        scratch_shapes=[pltpu.SemaphoreType.DMA, pltpu.SemaphoreType.DMA],
        compiler_params=_cparams(),
    )(send)


def exchange_chips(part, name):
    n_chip = N_DEV // 2

    def body(p_ref, o_ref, send_sems, recv_sems, local_sem):
        x, y, c = _my_pos()
        my_chip = 2 * x + y
        mine = pltpu.make_async_copy(p_ref.at[my_chip], o_ref.at[my_chip], local_sem)
        mine.start()
        sends = []
        for k in range(1, n_chip):
            px, py = x ^ (k >> 1), y ^ (k & 1)
            cp = pltpu.make_async_remote_copy(p_ref.at[2 * px + py], o_ref.at[my_chip], send_sems.at[k - 1],
                                              recv_sems.at[k - 1], device_id=(px, py, c), device_id_type=MESH_ID)
            cp.start()
            sends.append(cp)
        for k in range(1, n_chip):
            px, py = x ^ (k >> 1), y ^ (k & 1)
            pltpu.make_async_remote_copy(p_ref.at[2 * px + py], o_ref.at[2 * px + py], send_sems.at[k - 1],
                                         recv_sems.at[k - 1], device_id=(px, py, c), device_id_type=MESH_ID).wait_recv()
        for cp in sends:
            cp.wait_send()
        mine.wait()

    return pl.pallas_call(
        body, name=name,
        out_shape=jax.ShapeDtypeStruct(part.shape, part.dtype),
        in_specs=[pl.BlockSpec(memory_space=pl.ANY)],
        out_specs=pl.BlockSpec(memory_space=pl.ANY),
        scratch_shapes=[pltpu.SemaphoreType.DMA((n_chip - 1,)), pltpu.SemaphoreType.DMA((n_chip - 1,)),
                        pltpu.SemaphoreType.DMA],
        compiler_params=_cparams(),
    )(part)


def mm_nn(a, b, out_dtype, name):
    m, kdim = a.shape
    n = b.shape[1]
    tm, tn, tk = min(MM_ROW_TILE, m), 1024, 1024
    nk = kdim // tk

    def body(a_ref, b_ref, o_ref, acc_ref):
        p = _nn(a_ref[...], b_ref[...])
        if nk == 1:
            o_ref[...] = p.astype(o_ref.dtype)
        else:
            k = pl.program_id(2)

            @pl.when(k == 0)
            def _():
                acc_ref[...] = p

            @pl.when(k > 0)
            def _():
                acc_ref[...] += p

            @pl.when(k == nk - 1)
            def _():
                o_ref[...] = acc_ref[...].astype(o_ref.dtype)

    return pl.pallas_call(
        body, name=name, grid=(n // tn, m // tm, nk),
        out_shape=jax.ShapeDtypeStruct((m, n), out_dtype),
        in_specs=[pl.BlockSpec((tm, tk), lambda j, i, k: (i, k)), pl.BlockSpec((tk, tn), lambda j, i, k: (k, j))],
        out_specs=pl.BlockSpec((tm, tn), lambda j, i, k: (i, j)),
        scratch_shapes=[pltpu.VMEM((tm, tn), F32)],
        compiler_params=_cparams(("parallel", "parallel", "arbitrary")),
    )(a, b)


def mm_nt(a, b, out_dtype, name):
    m, kdim = a.shape
    n = b.shape[0]
    tm, tn, tk = min(MM_ROW_TILE, m), 1024, 1024
    nk = kdim // tk

    def body(a_ref, b_ref, o_ref, acc_ref):
        p = _nt(a_ref[...], b_ref[...])
        if nk == 1:
            o_ref[...] = p.astype(o_ref.dtype)
        else:
            k = pl.program_id(2)

            @pl.when(k == 0)
            def _():
                acc_ref[...] = p

            @pl.when(k > 0)
            def _():
                acc_ref[...] += p

            @pl.when(k == nk - 1)
            def _():
                o_ref[...] = acc_ref[...].astype(o_ref.dtype)

    return pl.pallas_call(
        body, name=name, grid=(n // tn, m // tm, nk),
        out_shape=jax.ShapeDtypeStruct((m, n), out_dtype),
        in_specs=[pl.BlockSpec((tm, tk), lambda j, i, k: (i, k)), pl.BlockSpec((tn, tk), lambda j, i, k: (j, k))],
        out_specs=pl.BlockSpec((tm, tn), lambda j, i, k: (i, j)),
        scratch_shapes=[pltpu.VMEM((tm, tn), F32)],
        compiler_params=_cparams(("parallel", "parallel", "arbitrary")),
    )(a, b)


def mm_tn(a, b, name):
    t, ka = a.shape
    n = b.shape[1]
    tt, tka, tn = min(MM_TOKEN_TILE, t), 1024, 1024
    nt = t // tt

    def body(a_ref, b_ref, o_ref, acc_ref):
        p = _tn(a_ref[...], b_ref[...])
        s = pl.program_id(2)

        @pl.when(s == 0)
        def _():
            acc_ref[...] = p

        @pl.when(s > 0)
        def _():
            acc_ref[...] += p

        @pl.when(s == nt - 1)
        def _():
            o_ref[...] = acc_ref[...].astype(o_ref.dtype)

    return pl.pallas_call(
        body, name=name, grid=(ka // tka, n // tn, nt),
        out_shape=jax.ShapeDtypeStruct((ka, n), BF16),
        in_specs=[pl.BlockSpec((tt, tka), lambda i, j, s: (s, i)), pl.BlockSpec((tt, tn), lambda i, j, s: (s, j))],
        out_specs=pl.BlockSpec((tka, tn), lambda i, j, s: (i, j)),
        scratch_shapes=[pltpu.VMEM((tka, tn), F32)],
        compiler_params=_cparams(("parallel", "parallel", "arbitrary")),
    )(a, b)


def _tile(t, cap):
    return min(cap, t)


def ln_modulate(x, mod6, shift_row, scale_row, name):
    t = x.shape[0]
    tm = _tile(t, ROW_TILE)

    def body(x_ref, mod_ref, o_ref):
        xh, _ = _ln(x_ref[...])
        sc = mod_ref[scale_row:scale_row + 1, :]
        sh = mod_ref[shift_row:shift_row + 1, :]
        o_ref[...] = (xh * (1.0 + sc) + sh).astype(BF16)

    return pl.pallas_call(
        body, name=name, grid=(t // tm,),
        out_shape=jax.ShapeDtypeStruct((t, D), BF16),
        in_specs=[pl.BlockSpec((tm, D), lambda i: (i, 0)), pl.BlockSpec((6, D), lambda i: (0, 0))],
        out_specs=pl.BlockSpec((tm, D), lambda i: (i, 0)),
        compiler_params=_cparams(("parallel",)),
    )(x, mod6)


def resid_ln(x, h, mod6, gate_row, ln_g, ln_b, name):
    t = x.shape[0]
    tm = _tile(t, ROW_TILE)

    def body(x_ref, h_ref, mod_ref, g_ref, b_ref, o_ref):
        r = ALPHA * x_ref[...] + mod_ref[gate_row:gate_row + 1, :] * h_ref[...]
        rh, _ = _ln(r)
        o_ref[...] = rh * g_ref[...] + b_ref[...]

    row = pl.BlockSpec((tm, D), lambda i: (i, 0))
    vec = pl.BlockSpec((1, D), lambda i: (0, 0))
    return pl.pallas_call(
        body, name=name, grid=(t // tm,),
        out_shape=jax.ShapeDtypeStruct((t, D), F32),
        in_specs=[row, row, pl.BlockSpec((6, D), lambda i: (0, 0)), vec, vec],
        out_specs=row,
        compiler_params=_cparams(("parallel",)),
    )(x, h, mod6, ln_g, ln_b)


def resid_ln_bwd(x, h, mod6, gate_row, ln_g, ln_b, cot, with_loss, name):
    t = x.shape[0]
    tm = _tile(t, ROW_TILE)

    def body(x_ref, h_ref, mod_ref, g_ref, b_ref, c_ref, dh_ref, dx_ref, acc_ref):
        @pl.when(pl.program_id(0) == 0)
        def _():
            acc_ref[...] = jnp.zeros_like(acc_ref)

        gate = mod_ref[gate_row:gate_row + 1, :]
        hv = h_ref[...]
        r = ALPHA * x_ref[...] + gate * hv
        rh, rstd = _ln(r)
        lng = g_ref[...]
        if with_loss:
            diff = rh * lng + b_ref[...] - c_ref[...]
            dxo = diff * (1.0 / D)
            lsum = jnp.sum(_colsum(diff * diff), axis=-1, keepdims=True) * (0.5 / D)
            acc_ref[3:4, :] += jnp.broadcast_to(lsum, (1, D))
        else:
            dxo = c_ref[...]
        acc_ref[1:2, :] += _colsum(dxo * rh)
        acc_ref[2:3, :] += _colsum(dxo)
        dr = _ln_bwd(dxo * lng, rh, rstd)
        acc_ref[0:1, :] += _colsum(dr * hv)
        dh_ref[...] = (gate * dr).astype(BF16)
        dx_ref[...] = ALPHA * dr

    row = pl.BlockSpec((tm, D), lambda i: (i, 0))
    vec = pl.BlockSpec((1, D), lambda i: (0, 0))
    return pl.pallas_call(
        body, name=name, grid=(t // tm,),
        out_shape=[jax.ShapeDtypeStruct((t, D), BF16), jax.ShapeDtypeStruct((t, D), F32),
                   jax.ShapeDtypeStruct((8, D), F32)],
        in_specs=[row, row, pl.BlockSpec((6, D), lambda i: (0, 0)), vec, vec, row],
        out_specs=[row, row, pl.BlockSpec((8, D), lambda i: (0, 0))],
        compiler_params=_cparams(("arbitrary",)),
    )(x, h, mod6, ln_g, ln_b, cot)


def ln_modulate_bwd(x, du, mod6, scale_row, dx_part, name):
    t = x.shape[0]
    tm = _tile(t, ROW_TILE)

    def body(x_ref, du_ref, mod_ref, dp_ref, dx_ref, acc_ref):
        @pl.when(pl.program_id(0) == 0)
        def _():
            acc_ref[...] = jnp.zeros_like(acc_ref)

        xh, rstd = _ln(x_ref[...])
        du_v = du_ref[...]
        sc = mod_ref[scale_row:scale_row + 1, :]
        acc_ref[0:1, :] += _colsum(du_v * xh)
        acc_ref[1:2, :] += _colsum(du_v)
        dx_ref[...] = dp_ref[...] + _ln_bwd(du_v * (1.0 + sc), xh, rstd)

    row = pl.BlockSpec((tm, D), lambda i: (i, 0))
    return pl.pallas_call(
        body, name=name, grid=(t // tm,),
        out_shape=[jax.ShapeDtypeStruct((t, D), F32), jax.ShapeDtypeStruct((8, D), F32)],
        in_specs=[row, row, pl.BlockSpec((6, D), lambda i: (0, 0)), row],
        out_specs=[row, pl.BlockSpec((8, D), lambda i: (0, 0))],
        compiler_params=_cparams(("arbitrary",)),
    )(x, du, mod6, dx_part)


def merge_gates(ya, yb, proj):
    t = ya.shape[0]
    tm = _tile(t, ROW_TILE)

    def body(ya_ref, yb_ref, ga_ref, gb_ref, o_ref):
        o_ref[...] = (_sigmoid(ga_ref[...]) * ya_ref[...] + _sigmoid(gb_ref[...]) * yb_ref[...]).astype(BF16)

    row = pl.BlockSpec((tm, D), lambda i: (i, 0))
    return pl.pallas_call(
        body, name="merge_gates", grid=(t // tm,),
        out_shape=jax.ShapeDtypeStruct((t, D), BF16),
        in_specs=[row, row, pl.BlockSpec((tm, D), lambda i: (i, 9)), pl.BlockSpec((tm, D), lambda i: (i, 10))],
        out_specs=row,
        compiler_params=_cparams(("parallel",)),
    )(ya, yb, proj, proj)


def merge_gates_bwd(dm, ya, yb, proj):
    t = ya.shape[0]
    tm = _tile(t, ROW_TILE)

    def body(dm_ref, ya_ref, yb_ref, ga_ref, gb_ref, dya_ref, dyb_ref, dga_ref, dgb_ref):
        dmv = dm_ref[...]
        sa = _sigmoid(ga_ref[...])
        sb = _sigmoid(gb_ref[...])
        dya_ref[...] = (dmv * sa).astype(BF16)
        dyb_ref[...] = (dmv * sb).astype(BF16)
        dga_ref[...] = (dmv * ya_ref[...] * sa * (1.0 - sa)).astype(BF16)
        dgb_ref[...] = (dmv * yb_ref[...] * sb * (1.0 - sb)).astype(BF16)

    row = pl.BlockSpec((tm, D), lambda i: (i, 0))
    return pl.pallas_call(
        body, name="merge_gates_bwd", grid=(t // tm,),
        out_shape=[jax.ShapeDtypeStruct((t, D), BF16)] * 4,
        in_specs=[row, row, row, pl.BlockSpec((tm, D), lambda i: (i, 9)), pl.BlockSpec((tm, D), lambda i: (i, 10))],
        out_specs=[row] * 4,
        compiler_params=_cparams(("parallel",)),
    )(dm, ya, yb, proj, proj)


def swiglu_act(gu):
    t = gu.shape[0]
    tm = _tile(t, FFN_ROW_TILE)

    def body(gu_ref, o_ref):
        for j in range(D_FF_PAD // D):
            g = gu_ref[:, j * D:(j + 1) * D]
            u = gu_ref[:, D_FF_PAD + j * D:D_FF_PAD + (j + 1) * D]
            o_ref[:, j * D:(j + 1) * D] = (g * _sigmoid(g) * u).astype(BF16)

    return pl.pallas_call(
        body, name="swiglu_act", grid=(t // tm,),
        out_shape=jax.ShapeDtypeStruct((t, D_FF_PAD), BF16),
        in_specs=[pl.BlockSpec((tm, 2 * D_FF_PAD), lambda i: (i, 0))],
        out_specs=pl.BlockSpec((tm, D_FF_PAD), lambda i: (i, 0)),
        compiler_params=_cparams(("parallel",)),
    )(gu)


def swiglu_act_bwd(gu, dact):
    t = gu.shape[0]
    tm = _tile(t, FFN_ROW_TILE)

    def body(gu_ref, da_ref, o_ref):
        for j in range(D_FF_PAD // D):
            g = gu_ref[:, j * D:(j + 1) * D]
            u = gu_ref[:, D_FF_PAD + j * D:D_FF_PAD + (j + 1) * D]
            da = da_ref[:, j * D:(j + 1) * D]
            s = _sigmoid(g)
            o_ref[:, j * D:(j + 1) * D] = (da * u * _dsilu(g, s)).astype(BF16)
            o_ref[:, D_FF_PAD + j * D:D_FF_PAD + (j + 1) * D] = (da * g * s).astype(BF16)

    return pl.pallas_call(
        body, name="swiglu_act_bwd", grid=(t // tm,),
        out_shape=jax.ShapeDtypeStruct((t, 2 * D_FF_PAD), BF16),
        in_specs=[pl.BlockSpec((tm, 2 * D_FF_PAD), lambda i: (i, 0)), pl.BlockSpec((tm, D_FF_PAD), lambda i: (i, 0))],
        out_specs=pl.BlockSpec((tm, 2 * D_FF_PAD), lambda i: (i, 0)),
        compiler_params=_cparams(("parallel",)),
    )(gu, dact)


def _hgrn_chunk_terms(q, fl, lbv, tril_f):
    sig = _sigmoid(fl)
    f = lbv + (1.0 - lbv) * sig
    lam = jnp.log(f)
    k = 1.0 - f
    sq = _sigmoid(q)
    qt = q * sq * Q_SCALE
    bc = _nn(tril_f, lam, HI)
    bmid = bc[CHUNK // 2 - 1:CHUNK // 2, :]
    bl = bc[CHUNK - 1:CHUNK, :]
    eq = jnp.exp(jnp.minimum(bc - bmid, EXP_CLIP))
    ek = jnp.exp(jnp.minimum(bmid - bc, EXP_CLIP))
    eb = jnp.exp(bc)
    ekl = jnp.exp(bl - bc)
    ebl = jnp.exp(bl)
    return sig, f, k, sq, qt, eq, ek, eb, ekl, ebl


def hgrn_fwd(proj, lb, gnorm):
    t = proj.shape[0]
    tb = _tile(t, TOKEN_BLOCK)
    ncb = tb // CHUNK

    def body(q_ref, f_ref, i_ref, g_ref, lb_ref, gn_ref, oa_ref, oraw_ref, st_ref, state):
        @pl.when(pl.program_id(1) == 0)
        def _():
            state[...] = jnp.zeros_like(state)

        lbv = lb_ref[...]
        gn = gn_ref[...]
        mask = _tri(CHUNK)
        tril_f = mask.astype(F32)

        def chunk(c, carry):
            sl = pl.ds(pl.multiple_of(c * CHUNK, CHUNK), CHUNK)
            q, fl, v, g = q_ref[sl, :], f_ref[sl, :], i_ref[sl, :], g_ref[sl, :]
            sig, f, k, sq, qt, eq, ek, eb, ekl, ebl = _hgrn_chunk_terms(q, fl, lbv, tril_f)
            a = jnp.where(mask, _nt((qt * eq).astype(BF16), (k * ek).astype(BF16)), 0.0)
            st = state[...]
            st_ref[0, c] = st
            vb = v.astype(BF16)
            o = _nn(a.astype(BF16), vb) + _nt((qt * eb).astype(BF16), st.astype(BF16))
            state[...] = st * ebl + _tn(vb, (k * ekl).astype(BF16))
            oraw_ref[sl, :] = o
            rn = o * lax.rsqrt(jnp.mean(o * o, axis=-1, keepdims=True) + RMS_EPS)
            oa_ref[sl, :] = (rn * gn * g * _sigmoid(g)).astype(BF16)
            return carry

        lax.fori_loop(0, ncb, chunk, 0, unroll=CHUNK_UNROLL)

    def col(block):
        return pl.BlockSpec((tb, HK), lambda h, j: (j, block * N_HEADS_A + h))

    return pl.pallas_call(
        body, name="hgrn_fwd", grid=(N_HEADS_A, t // tb),
        out_shape=[jax.ShapeDtypeStruct((t, D), BF16), jax.ShapeDtypeStruct((t, D), F32),
                   jax.ShapeDtypeStruct((N_HEADS_A, t // CHUNK, HK, HK), F32)],
        in_specs=[col(0), col(1), col(2), col(3), pl.BlockSpec((1, HK), lambda h, j: (0, h)),
                  pl.BlockSpec((1, HK), lambda h, j: (0, 0))],
        out_specs=[pl.BlockSpec((tb, HK), lambda h, j: (j, h)), pl.BlockSpec((tb, HK), lambda h, j: (j, h)),
                   pl.BlockSpec((1, ncb, HK, HK), lambda h, j: (h, j, 0, 0))],
        scratch_shapes=[pltpu.VMEM((HK, HK), F32)],
        compiler_params=_cparams(("parallel", "arbitrary")),
    )(proj, proj, proj, proj, lb, gnorm)


def hgrn_bwd(proj, lb, gnorm, o_raw, doa, states):
    t = proj.shape[0]
    tb = _tile(t, TOKEN_BLOCK)
    ncb = tb // CHUNK
    nb = t // tb

    def body(q_ref, f_ref, i_ref, g_ref, lb_ref, gn_ref, oraw_ref, doa_ref, st_ref,
             dq_ref, df_ref, di_ref, dg_ref, dlb_ref, dgn_ref, dstate):
        h, j = pl.program_id(0), pl.program_id(1)

        @pl.when(j == 0)
        def _():
            dstate[...] = jnp.zeros_like(dstate)
            dlb_ref[...] = jnp.zeros_like(dlb_ref)

        @pl.when((j == 0) & (h == 0))
        def _():
            dgn_ref[...] = jnp.zeros_like(dgn_ref)

        lbv = lb_ref[...]
        gn = gn_ref[...]
        mask = _tri(CHUNK)
        mask_t = _tri(CHUNK, upper=True)
        tril_f = mask.astype(F32)
        triu_f = mask_t.astype(F32)

        def chunk(i, c0):
            c = ncb - 1 - i
            sl = pl.ds(pl.multiple_of(c * CHUNK, CHUNK), CHUNK)
            q, fl, v, g = q_ref[sl, :], f_ref[sl, :], i_ref[sl, :], g_ref[sl, :]
            sig, f, k, sq, qt, eq, ek, eb, ekl, ebl = _hgrn_chunk_terms(q, fl, lbv, tril_f)
            qe = (qt * eq).astype(BF16)
            ke = (k * ek).astype(BF16)
            st32 = st_ref[0, c]
            st = st32.astype(BF16)
            dst = dstate[...]
            dstb = dst.astype(BF16)
            o = oraw_ref[sl, :]
            rstd = lax.rsqrt(jnp.mean(o * o, axis=-1, keepdims=True) + RMS_EPS)
            rn = o * rstd
            sgm = _sigmoid(g)
            sg = g * sgm
            doa_v = doa_ref[sl, :]
            drn = doa_v * gn * sg
            dgn_ref[...] += _colsum(doa_v * rn * sg)
            dg_ref[sl, :] = (doa_v * rn * gn * _dsilu(g, sgm)).astype(BF16)
            do = rstd * (drn - rn * jnp.mean(drn * rn, axis=-1, keepdims=True))
            dob = do.astype(BF16)
            vb = v.astype(BF16)
            da = jnp.where(mask, _nt(dob, vb), 0.0).astype(BF16)
            da_t = jnp.where(mask_t, _nt(vb, dob), 0.0).astype(BF16)
            a_t = jnp.where(mask_t, _nt(ke, qe), 0.0).astype(BF16)
            kl = (k * ekl).astype(BF16)
            qb = (qt * eb).astype(BF16)
            dq_in = _nn(da, ke)
            dk_in = _nn(da_t, qe)
            dq_out = eb * _nn(dob, st)
            dk_out = ekl * _nn(vb, dstb)
            dqt = eq * dq_in + dq_out
            dk = ek * dk_in + dk_out
            dv = _nn(a_t, dob) + _nt(kl, dstb)
            dstate[...] = dst * ebl + _tn(dob, qb)
            dbig = qe.astype(F32) * dq_in - ke.astype(F32) * dk_in + qt * dq_out - k * dk_out
            beyond = _colsum(k * dk_out) + ebl * _colsum(dst * st32)
            dlam = _nn(triu_f, dbig, HI) + beyond
            df = dlam / f - dk
            df_ref[sl, :] = (df * (1.0 - lbv) * sig * (1.0 - sig)).astype(BF16)
            dlb_ref[...] += _colsum(df * (1.0 - sig))
            dq_ref[sl, :] = (dqt * Q_SCALE * _dsilu(q, sq)).astype(BF16)
            di_ref[sl, :] = dv.astype(BF16)
            return c0

        lax.fori_loop(0, ncb, chunk, 0, unroll=CHUNK_UNROLL)

    def col(block):
        return pl.BlockSpec((tb, HK), lambda h, j: (nb - 1 - j, block * N_HEADS_A + h))

    hcol = pl.BlockSpec((tb, HK), lambda h, j: (nb - 1 - j, h))
    return pl.pallas_call(
        body, name="hgrn_bwd", grid=(N_HEADS_A, nb),
        out_shape=[jax.ShapeDtypeStruct((t, D), BF16)] * 4 + [jax.ShapeDtypeStruct((1, D), F32),
                                                                jax.ShapeDtypeStruct((1, HK), F32)],
        in_specs=[col(0), col(1), col(2), col(3), pl.BlockSpec((1, HK), lambda h, j: (0, h)),
                  pl.BlockSpec((1, HK), lambda h, j: (0, 0)), hcol, hcol,
                  pl.BlockSpec((1, ncb, HK, HK), lambda h, j: (h, nb - 1 - j, 0, 0))],
        out_specs=[hcol] * 4 + [pl.BlockSpec((1, HK), lambda h, j: (0, h)), pl.BlockSpec((1, HK), lambda h, j: (0, 0))],
        scratch_shapes=[pltpu.VMEM((HK, HK), F32)],
        compiler_params=_cparams(("arbitrary", "arbitrary")),
    )(proj, proj, proj, proj, lb, gnorm, o_raw, doa, states)


CONV_BLOCK0 = 6
CONV_TAPS = 4
HALO = 8


def conv_fwd(proj, conv_w, conv_b):
    t = proj.shape[0]
    tm = _tile(t, ROW_TILE)
    r = tm // HALO

    def body(x_ref, halo_ref, w_ref, b_ref, o_ref):
        i = pl.program_id(1)
        halo = jnp.where(i > 0, halo_ref[...], 0.0)
        ext = jnp.concatenate([halo, x_ref[...]], axis=0)
        pre = b_ref[...] + w_ref[CONV_TAPS - 1:CONV_TAPS, :] * ext[HALO:, :]
        for tap in range(CONV_TAPS - 1):
            pre = pre + w_ref[tap:tap + 1, :] * pltpu.roll(ext, CONV_TAPS - 1 - tap, axis=0)[HALO:, :]
        o_ref[...] = pre * _sigmoid(pre)

    return pl.pallas_call(
        body, name="conv_fwd", grid=(CONV_DIM // D, t // tm),
        out_shape=jax.ShapeDtypeStruct((t, CONV_DIM), F32),
        in_specs=[pl.BlockSpec((tm, D), lambda cb, i: (i, CONV_BLOCK0 + cb)),
                  pl.BlockSpec((HALO, D), lambda cb, i: (jnp.maximum(i * r - 1, 0), CONV_BLOCK0 + cb)),
                  pl.BlockSpec((CONV_TAPS, D), lambda cb, i: (0, cb)), pl.BlockSpec((1, D), lambda cb, i: (0, cb))],
        out_specs=pl.BlockSpec((tm, D), lambda cb, i: (i, cb)),
        compiler_params=_cparams(("parallel", "parallel")),
    )(proj, proj, conv_w, conv_b)


def conv_bwd(proj, dxc, conv_w, conv_b):
    t = proj.shape[0]
    tm = _tile(t, ROW_TILE)
    r = tm // HALO
    n = t // tm
    last_halo = t // HALO - 1

    def body(x_ref, prev_ref, next_ref, d_ref, dnext_ref, w_ref, b_ref, dx_ref, dw_ref, db_ref):
        i = pl.program_id(1)

        @pl.when(i == 0)
        def _():
            dw_ref[...] = jnp.zeros_like(dw_ref)
            db_ref[...] = jnp.zeros_like(db_ref)

        prev = jnp.where(i > 0, prev_ref[...], 0.0)
        ext = jnp.concatenate([prev, x_ref[...], next_ref[...]], axis=0)
        shifted = [pltpu.roll(ext, CONV_TAPS - 1 - tap, axis=0)[HALO:, :] for tap in range(CONV_TAPS - 1)]
        shifted.append(ext[HALO:, :])
        pre = b_ref[...]
        for tap in range(CONV_TAPS):
            pre = pre + w_ref[tap:tap + 1, :] * shifted[tap]
        s = _sigmoid(pre)
        d_ext = jnp.concatenate([d_ref[...], jnp.where(i < n - 1, dnext_ref[...], 0.0)], axis=0)
        dpre = d_ext * _dsilu(pre, s)
        dx = w_ref[CONV_TAPS - 1:CONV_TAPS, :] * dpre[:tm, :]
        for tap in range(CONV_TAPS - 1):
            back = CONV_TAPS - 1 - tap
            dx = dx + w_ref[tap:tap + 1, :] * pltpu.roll(dpre, tm + HALO - back, axis=0)[:tm, :]
        dx_ref[...] = dx.astype(BF16)
        dp = dpre[:tm, :]
        db_ref[...] += _colsum(dp)
        for tap in range(CONV_TAPS):
            dw_ref[tap:tap + 1, :] += _colsum(dp * shifted[tap][:tm, :])

    return pl.pallas_call(
        body, name="conv_bwd", grid=(CONV_DIM // D, n),
        out_shape=[jax.ShapeDtypeStruct((t, CONV_DIM), BF16), jax.ShapeDtypeStruct((8, CONV_DIM), F32),
                   jax.ShapeDtypeStruct((1, CONV_DIM), F32)],
        in_specs=[pl.BlockSpec((tm, D), lambda cb, i: (i, CONV_BLOCK0 + cb)),
                  pl.BlockSpec((HALO, D), lambda cb, i: (jnp.maximum(i * r - 1, 0), CONV_BLOCK0 + cb)),
                  pl.BlockSpec((HALO, D), lambda cb, i: (jnp.minimum((i + 1) * r, last_halo), CONV_BLOCK0 + cb)),
                  pl.BlockSpec((tm, D), lambda cb, i: (i, cb)),
                  pl.BlockSpec((HALO, D), lambda cb, i: (jnp.minimum((i + 1) * r, last_halo), cb)),
                  pl.BlockSpec((CONV_TAPS, D), lambda cb, i: (0, cb)), pl.BlockSpec((1, D), lambda cb, i: (0, cb))],
        out_specs=[pl.BlockSpec((tm, D), lambda cb, i: (i, cb)), pl.BlockSpec((8, D), lambda cb, i: (0, cb)),
                   pl.BlockSpec((1, D), lambda cb, i: (0, cb))],
        compiler_params=_cparams(("parallel", "arbitrary")),
    )(proj, proj, proj, dxc, dxc, conv_w, conv_b)


Z_BLOCK0 = 8
DT_BLOCK0 = 88
B_BLOCK0 = 16
C_BLOCK0 = 20


def _head_expand():
    e = np.zeros((N_STATE, GROUP_W), np.float32)
    for hh in range(HEADS_PER_GROUP):
        e[hh, hh * HEAD_P:(hh + 1) * HEAD_P] = 1.0
    return jnp.asarray(e)


def _ssd_chunk_terms(dt, bias, alog, expand, tril_f, eye):
    dtb = dt + bias
    delta = jnp.maximum(dtb, 0.0) + jnp.log(1.0 + jnp.exp(-jnp.abs(dtb)))
    ea = jnp.exp(alog)
    a = -ea * delta
    acum = _nn(tril_f, a, HI)
    delta_e = _nn(delta, expand, HI)
    acum_e = _nn(acum, expand, HI)
    acum_t = _nt(eye, acum, HI)
    return dtb, delta, ea, a, acum, delta_e, acum_e, acum_t


def ssd_fwd(proj, xc, alog4, bias4, dskip4, wnorm, expand):
    t = proj.shape[0]
    tb = _tile(t, TOKEN_BLOCK)
    ncb = tb // CHUNK

    def body(xs_ref, b_ref, c_ref, dt_ref, z_ref, alog_ref, bias_ref, dsk_ref, wn_ref, e_ref, ob_ref, st_ref, state):
        @pl.when(pl.program_id(1) == 0)
        def _():
            state[...] = jnp.zeros_like(state)

        expand = e_ref[...]
        mask = _tri(CHUNK)
        tril_f = mask.astype(F32)
        eye = (lax.broadcasted_iota(jnp.int32, (N_STATE, N_STATE), 0) ==
               lax.broadcasted_iota(jnp.int32, (N_STATE, N_STATE), 1)).astype(F32)
        alog, bias = alog_ref[0], bias_ref[0]
        d_e = _nn(jnp.broadcast_to(dsk_ref[0], (8, N_STATE)), expand, HI)[0:1, :]
        wn = wn_ref[...]

        def chunk(c, carry):
            sl = pl.ds(pl.multiple_of(c * CHUNK, CHUNK), CHUNK)
            xs, bm, cm, dt, z = xs_ref[sl, :], b_ref[sl, :], c_ref[sl, :], dt_ref[sl, :], z_ref[sl, :]
            dtb, delta, ea, a, acum, delta_e, acum_e, acum_t = _ssd_chunk_terms(dt, bias, alog, expand, tril_f, eye)
            alast_e = acum_e[CHUNK - 1:CHUNK, :]
            xd = xs * delta_e
            xdb = xd.astype(BF16)
            cb_, bb_ = cm.astype(BF16), bm.astype(BF16)
            cbm = _nt(cb_, bb_)
            ys = []
            for hh in range(HEADS_PER_GROUP):
                lh = jnp.where(mask, jnp.exp(jnp.minimum(acum[:, hh:hh + 1] - acum_t[hh:hh + 1, :], 0.0)), 0.0)
                ys.append(_nn((cbm * lh).astype(BF16), xdb[:, hh * HEAD_P:(hh + 1) * HEAD_P]))
            st = state[...]
            st_ref[0, c] = st
            y = jnp.concatenate(ys, axis=1) + _nn(cb_, st.astype(BF16)) * jnp.exp(acum_e) + xs * d_e
            state[...] = st * jnp.exp(alast_e) + _tn(bb_, (xd * jnp.exp(alast_e - acum_e)).astype(BF16))
            yg = y * z * _sigmoid(z)
            ob_ref[sl, :] = (yg * lax.rsqrt(jnp.mean(yg * yg, axis=-1, keepdims=True) + RMS_EPS) * wn).astype(BF16)
            return carry

        lax.fori_loop(0, ncb, chunk, 0, unroll=CHUNK_UNROLL)

    small = pl.BlockSpec((1, 1, N_STATE), lambda g, j: (g, 0, 0))
    return pl.pallas_call(
        body, name="ssd_fwd", grid=(N_GROUPS, t // tb),
        out_shape=[jax.ShapeDtypeStruct((t, B_INNER), BF16),
                   jax.ShapeDtypeStruct((N_GROUPS, t // CHUNK, N_STATE, GROUP_W), F32)],
        in_specs=[pl.BlockSpec((tb, GROUP_W), lambda g, j: (j, g)),
                  pl.BlockSpec((tb, N_STATE), lambda g, j: (j, B_BLOCK0 + g)),
                  pl.BlockSpec((tb, N_STATE), lambda g, j: (j, C_BLOCK0 + g)),
                  pl.BlockSpec((tb, N_STATE), lambda g, j: (j, DT_BLOCK0 + g)),
                  pl.BlockSpec((tb, GROUP_W), lambda g, j: (j, Z_BLOCK0 + g)),
                  small, small, small, pl.BlockSpec((1, GROUP_W), lambda g, j: (0, g)),
                  pl.BlockSpec((N_STATE, GROUP_W), lambda g, j: (0, 0))],
        out_specs=[pl.BlockSpec((tb, GROUP_W), lambda g, j: (j, g)),
                   pl.BlockSpec((1, ncb, N_STATE, GROUP_W), lambda g, j: (g, j, 0, 0))],
        scratch_shapes=[pltpu.VMEM((N_STATE, GROUP_W), F32)],
        compiler_params=_cparams(("parallel", "arbitrary")),
    )(xc, xc, xc, proj, proj, alog4, bias4, dskip4, wnorm, expand)


def ssd_bwd(proj, xc, alog4, bias4, dskip4, wnorm, expand, dob, states):
    t = proj.shape[0]
    tb = _tile(t, TOKEN_BLOCK)
    ncb = tb // CHUNK
    nb = t // tb

    def body(xs_ref, b_ref, c_ref, dt_ref, z_ref, alog_ref, bias_ref, dsk_ref, wn_ref, e_ref, dob_ref, st_ref,
             dxs_ref, db_ref, dc_ref, dz_ref, ddt_ref, dwn_ref, dalog_ref, dbias_ref, ddsk_ref, dstate):
        @pl.when(pl.program_id(1) == 0)
        def _():
            dstate[...] = jnp.zeros_like(dstate)
            dwn_ref[...] = jnp.zeros_like(dwn_ref)
            dalog_ref[...] = jnp.zeros_like(dalog_ref)
            dbias_ref[...] = jnp.zeros_like(dbias_ref)
            ddsk_ref[...] = jnp.zeros_like(ddsk_ref)

        expand = e_ref[...]
        mask = _tri(CHUNK)
        mask_t = _tri(CHUNK, upper=True)
        tril_f = mask.astype(F32)
        triu_f = mask_t.astype(F32)
        eye = (lax.broadcasted_iota(jnp.int32, (N_STATE, N_STATE), 0) ==
               lax.broadcasted_iota(jnp.int32, (N_STATE, N_STATE), 1)).astype(F32)
        alog, bias = alog_ref[0], bias_ref[0]
        d_e = _nn(jnp.broadcast_to(dsk_ref[0], (8, N_STATE)), expand, HI)[0:1, :]
        wn = wn_ref[...]

        def chunk(i, c0):
            c = ncb - 1 - i
            sl = pl.ds(pl.multiple_of(c * CHUNK, CHUNK), CHUNK)
            xs, bm, cm, dt, z = xs_ref[sl, :], b_ref[sl, :], c_ref[sl, :], dt_ref[sl, :], z_ref[sl, :]
            dtb, delta, ea, a, acum, delta_e, acum_e, acum_t = _ssd_chunk_terms(dt, bias, alog, expand, tril_f, eye)
            alast_e = acum_e[CHUNK - 1:CHUNK, :]
            eacum = jnp.exp(acum_e)
            wl = jnp.exp(alast_e - acum_e)
            xd = xs * delta_e
            xdb = xd.astype(BF16)
            cb_, bb_ = cm.astype(BF16), bm.astype(BF16)
            cbm = _nt(cb_, bb_)
            cbm_t = _nt(bb_, cb_)
            st32 = st_ref[0, c]
            stb = st32.astype(BF16)
            dst = dstate[...]
            dstb = dst.astype(BF16)
            lhs, lhts, ys = [], [], []
            for hh in range(HEADS_PER_GROUP):
                col, row = acum[:, hh:hh + 1], acum_t[hh:hh + 1, :]
                lh = jnp.where(mask, jnp.exp(jnp.minimum(col - row, 0.0)), 0.0)
                lht = jnp.where(mask_t, jnp.exp(jnp.minimum(row - col, 0.0)), 0.0)
                lhs.append(lh)
                lhts.append(lht)
                ys.append(_nn((cbm * lh).astype(BF16), xdb[:, hh * HEAD_P:(hh + 1) * HEAD_P]))
            y_in = jnp.concatenate(ys, axis=1)
            y_out = _nn(cb_, stb) * eacum
            y = y_in + y_out + xs * d_e
            sgz = _sigmoid(z)
            sz = z * sgz
            yg = y * sz
            rstd = lax.rsqrt(jnp.mean(yg * yg, axis=-1, keepdims=True) + RMS_EPS)
            nrm = yg * rstd
            dob_v = dob_ref[sl, :]
            dn = dob_v * wn
            dwn_ref[...] += _colsum(dob_v * nrm)
            dyg = rstd * (dn - nrm * jnp.mean(dn * nrm, axis=-1, keepdims=True))
            dy = dyg * sz
            dz_ref[sl, :] = (dyg * y * _dsilu(z, sgz)).astype(BF16)
            dyb = dy.astype(BF16)
            dxds = []
            dcb = jnp.zeros((CHUNK, CHUNK), F32)
            dcb_t = jnp.zeros((CHUNK, CHUNK), F32)
            for hh in range(HEADS_PER_GROUP):
                hs = slice(hh * HEAD_P, (hh + 1) * HEAD_P)
                dy_h, x_h = dyb[:, hs], xdb[:, hs]
                dxds.append(_nn((cbm_t * lhts[hh]).astype(BF16), dy_h))
                dcb = dcb + _nt(dy_h, x_h) * lhs[hh]
                dcb_t = dcb_t + _nt(x_h, dy_h) * lhts[hh]
            dye = (dy * eacum).astype(BF16)
            xw = (xd * wl).astype(BF16)
            dxd_in = jnp.concatenate(dxds, axis=1)
            dxd_out = wl * _nn(bb_, dstb)
            dxd = dxd_in + dxd_out
            dc_ref[sl, :] = _nn(dcb.astype(BF16), bb_) + _nt(dye, stb)
            db_ref[sl, :] = _nn(dcb_t.astype(BF16), cb_) + _nt(xw, dstb)
            dstate[...] = dst * jnp.exp(alast_e) + _tn(cb_, dye)
            col_out = xd * dxd_out
            dac = _nt(dyb.astype(F32) * y_in - xdb.astype(F32) * dxd_in + dy * y_out - col_out, expand, HI)
            beyond = _colsum(col_out) + jnp.exp(alast_e) * _colsum(dst * st32)
            da = _nn(triu_f, dac, HI) + _nt(jnp.broadcast_to(beyond, (8, GROUP_W)), expand, HI)[0:1, :]
            ddelta = _nt(dxd * xs, expand, HI) - da * ea
            dalog_ref[0] += _colsum(da * a)
            ddtb = ddelta * _sigmoid(dtb)
            dbias_ref[0] += _colsum(ddtb)
            ddt_ref[sl, :] = ddtb.astype(BF16)
            ddsk_ref[0] += _colsum(_nt(dy * xs, expand, HI))
            dxs_ref[sl, :] = dxd * delta_e + dy * d_e
            return c0

        lax.fori_loop(0, ncb, chunk, 0, unroll=CHUNK_UNROLL)

    small = pl.BlockSpec((1, 1, N_STATE), lambda g, j: (g, 0, 0))
    wide = pl.BlockSpec((tb, GROUP_W), lambda g, j: (nb - 1 - j, g))
    narrow = pl.BlockSpec((tb, N_STATE), lambda g, j: (nb - 1 - j, g))
    return pl.pallas_call(
        body, name="ssd_bwd", grid=(N_GROUPS, nb),
        out_shape=[jax.ShapeDtypeStruct((t, B_INNER), F32), jax.ShapeDtypeStruct((t, GROUP_W), F32),
                   jax.ShapeDtypeStruct((t, GROUP_W), F32), jax.ShapeDtypeStruct((t, B_INNER), BF16),
                   jax.ShapeDtypeStruct((t, GROUP_W), BF16), jax.ShapeDtypeStruct((1, B_INNER), F32),
                   jax.ShapeDtypeStruct((N_GROUPS, 1, N_STATE), F32), jax.ShapeDtypeStruct((N_GROUPS, 1, N_STATE), F32),
                   jax.ShapeDtypeStruct((N_GROUPS, 1, N_STATE), F32)],
        in_specs=[wide,
                  pl.BlockSpec((tb, N_STATE), lambda g, j: (nb - 1 - j, B_BLOCK0 + g)),
                  pl.BlockSpec((tb, N_STATE), lambda g, j: (nb - 1 - j, C_BLOCK0 + g)),
                  pl.BlockSpec((tb, N_STATE), lambda g, j: (nb - 1 - j, DT_BLOCK0 + g)),
                  pl.BlockSpec((tb, GROUP_W), lambda g, j: (nb - 1 - j, Z_BLOCK0 + g)),
                  small, small, small, pl.BlockSpec((1, GROUP_W), lambda g, j: (0, g)),
                  pl.BlockSpec((N_STATE, GROUP_W), lambda g, j: (0, 0)), wide,
                  pl.BlockSpec((1, ncb, N_STATE, GROUP_W), lambda g, j: (g, nb - 1 - j, 0, 0))],
        out_specs=[wide, narrow, narrow, wide, narrow, pl.BlockSpec((1, GROUP_W), lambda g, j: (0, g)),
                   small, small, small],
        scratch_shapes=[pltpu.VMEM((N_STATE, GROUP_W), F32)],
        compiler_params=_cparams(("parallel", "arbitrary")),
    )(xc, xc, xc, proj, proj, alog4, bias4, dskip4, wnorm, expand, dob, states)


def lower_bound_fwd(hgrn_lb):
    def body(a_ref, o_ref):
        a0, a1 = a_ref[0:1, :], a_ref[1:2, :]
        m = jnp.maximum(a0, a1)
        e0, e1 = jnp.exp(a0 - m), jnp.exp(a1 - m)
        o_ref[...] = e0 / (e0 + e1)

    return pl.pallas_call(body, name="lower_bound_fwd", out_shape=jax.ShapeDtypeStruct((1, D), F32))(hgrn_lb)


def ada_weight_grad(c_all, dmod_cols):
    def body(c_ref, d_ref, o_ref):
        cval = c_ref[...]
        o_ref[...] = _tn(cval * _sigmoid(cval), d_ref[...], HI)

    return pl.pallas_call(body, name="ada_weight_grad",
                          out_shape=jax.ShapeDtypeStruct((D, dmod_cols.shape[1]), F32))(c_all, dmod_cols)


def reduce_small(gathered, hgrn_lb, dlb_off):
    n = gathered.shape[2]

    def body(g_ref, a_ref, o_ref, glb_ref):
        s = g_ref[0]
        for d in range(1, N_DEV):
            s = s + g_ref[d]
        o_ref[...] = s
        a0, a1 = a_ref[0:1, :], a_ref[1:2, :]
        m = jnp.maximum(a0, a1)
        e0, e1 = jnp.exp(a0 - m), jnp.exp(a1 - m)
        p0 = e0 / (e0 + e1)
        tq = s[:, dlb_off:dlb_off + D] * p0 * (1.0 - p0)
        glb_ref[0:1, :] = tq
        glb_ref[1:2, :] = -tq

    return pl.pallas_call(body, name="reduce_small",
                          out_shape=[jax.ShapeDtypeStruct((1, n), F32), jax.ShapeDtypeStruct((2, D), F32)])(gathered, hgrn_lb)


def _adam_math(w, g, m, v):
    m2 = ADAM_B1 * m + (1.0 - ADAM_B1) * g
    v2 = ADAM_B2 * v + (1.0 - ADAM_B2) * (g * g)
    m_hat = m2 / (1.0 - ADAM_B1 ** ADAM_STEP)
    v_hat = v2 / (1.0 - ADAM_B2 ** ADAM_STEP)
    delta = -ADAM_LR * (m_hat / (jnp.sqrt(v_hat) + ADAM_EPS) + ADAM_WD * w)
    return delta, m2, v2


def _row_tile(rows, mult=8, cap=128):
    for cand in range(cap - cap % mult, 0, -mult):
        if rows % cand == 0:
            return cand
    return rows


def sum_parts(parts, name):
    n, rows, cols = parts.shape
    tr = _row_tile(rows, 16, 256)

    def body(p_ref, o_ref):
        s = p_ref[0].astype(F32)
        for d in range(1, n):
            s = s + p_ref[d].astype(F32)
        o_ref[...] = s

    return pl.pallas_call(
        body, name=name, grid=(rows // tr,),
        out_shape=jax.ShapeDtypeStruct((rows, cols), F32),
        in_specs=[pl.BlockSpec((n, tr, cols), lambda i: (0, i, 0))],
        out_specs=pl.BlockSpec((tr, cols), lambda i: (i, 0)),
        compiler_params=_cparams(("parallel",)),
    )(parts)


def sum_pair(a, b, name):
    rows, cols = a.shape
    tr = _row_tile(rows, 16, 256)

    def body(a_ref, b_ref, o_ref):
        o_ref[...] = (a_ref[...].astype(F32) + b_ref[...].astype(F32)).astype(o_ref.dtype)

    blk = pl.BlockSpec((tr, cols), lambda i: (i, 0))
    return pl.pallas_call(
        body, name=name, grid=(rows // tr,),
        out_shape=jax.ShapeDtypeStruct((rows, cols), a.dtype),
        in_specs=[blk, blk], out_specs=blk,
        compiler_params=_cparams(("parallel",)),
    )(a, b)


def adamw(w, g, m, v, name):
    rows, cols = w.shape
    tr = _row_tile(rows)

    def body(w_ref, g_ref, m_ref, v_ref, d_ref, m2_ref, v2_ref):
        delta, m2, v2 = _adam_math(w_ref[...], g_ref[...], m_ref[...], v_ref[...])
        d_ref[...] = delta
        m2_ref[...] = m2
        v2_ref[...] = v2

    blk = pl.BlockSpec((tr, cols), lambda i: (i, 0))
    return pl.pallas_call(
        body, name=name, grid=(rows // tr,),
        out_shape=[jax.ShapeDtypeStruct((rows, cols), F32)] * 3,
        in_specs=[blk] * 4, out_specs=[blk] * 3,
        compiler_params=_cparams(("parallel",)),
    )(w, g, m, v)


def _pad128(n):
    return -(-n // 128) * 128


def _pack(arrays):
    offs, parts, off = [], [], 0
    for a in arrays:
        flat = a.reshape(1, -1)
        n = flat.shape[1]
        offs.append(off)
        parts.append(jnp.pad(flat, ((0, 0), (0, _pad128(n) - n))))
        off += _pad128(n)
    return jnp.concatenate(parts, axis=1), offs


def _unpack(vec, offs, shapes):
    out = []
    for off, shp in zip(offs, shapes):
        n = int(np.prod(shp))
        out.append(vec[0, off:off + n].reshape(shp))
    return out


def _permute_in_rows(w_t):
    dt = w_t[9216:9248].reshape(N_GROUPS, HEADS_PER_GROUP, D)
    dt = jnp.pad(dt, ((0, 0), (0, N_STATE - HEADS_PER_GROUP), (0, 0))).reshape(N_GROUPS * N_STATE, D)
    dt = jnp.pad(dt, ((0, D - N_GROUPS * N_STATE), (0, 0)))
    return jnp.concatenate([w_t[:9216], w_t[9248:], dt], axis=0)


def _unpermute_in_rows(g_t):
    dt = g_t[11264:11264 + N_GROUPS * N_STATE].reshape(N_GROUPS, N_STATE, D)[:, :HEADS_PER_GROUP].reshape(32, D)
    return jnp.concatenate([g_t[:9216], dt, g_t[9216:11264]], axis=0)


def kernel(x, c, w_ada, b_ada, w_in, hgrn_lb, hgrn_gnorm, ssm_conv_w, ssm_conv_b, ssm_dt_bias, ssm_a_log, ssm_d, ssm_norm, w_branch_a, w_branch_b, w_o, ln1_g, ln1_b, w_ffn_gate, w_ffn_up, w_ffn_down, ln2_g, ln2_b, loss_target, m_w_ada, m_b_ada, m_w_in, m_hgrn_lb, m_hgrn_gnorm, m_ssm_conv_w, m_ssm_conv_b, m_ssm_dt_bias, m_ssm_a_log, m_ssm_d, m_ssm_norm, m_w_branch_a, m_w_branch_b, m_w_o, m_ln1_g, m_ln1_b, m_w_ffn_gate, m_w_ffn_up, m_w_ffn_down, m_ln2_g, m_ln2_b, v_w_ada, v_b_ada, v_w_in, v_hgrn_lb, v_hgrn_gnorm, v_ssm_conv_w, v_ssm_conv_b, v_ssm_dt_bias, v_ssm_a_log, v_ssm_d, v_ssm_norm, v_w_branch_a, v_w_branch_b, v_w_o, v_ln1_g, v_ln1_b, v_w_ffn_gate, v_w_ffn_up, v_w_ffn_down, v_ln2_g, v_ln2_b):
    me = 4 * lax.axis_index("x") + 2 * lax.axis_index("y") + lax.axis_index("c")
    xt = x[0]
    tgt = loss_target[0]
    t = xt.shape[0]
    ada_cols = w_ada.shape[2]
    conv_cols = ssm_conv_w.shape[2]

    small_in, _ = _pack([c, ssm_conv_w[0]])
    small_all = allgather_vmem(small_in, "allgather_small_inputs")
    c_all = small_all[:, 0, :D]
    conv_w = small_all[:, 0, D:D + CONV_TAPS * conv_cols].reshape(N_DEV, CONV_TAPS, conv_cols)
    conv_w = conv_w.transpose(1, 0, 2).reshape(CONV_TAPS, CONV_DIM)
    mod = ada_modulation(c_all, w_ada[0], b_ada.reshape(N_DEV, 1, ada_cols))
    mod6 = mod.reshape(6, D)

    shards = [w_in[0].T, w_branch_a[0], w_branch_b[0], w_o[0], w_ffn_gate[0].T, w_ffn_up[0].T, w_ffn_down[0]]
    shard_rows = [s.shape[0] for s in shards]
    slot_rows = [-(-r // 16) * 16 for r in shard_rows]
    row_offs = [sum(slot_rows[:i]) for i in range(len(shards))]
    rows_pad = sum(slot_rows)
    stacked = jnp.concatenate([jnp.pad(s.astype(BF16), ((0, p - r), (0, 0)))
                               for s, r, p in zip(shards, shard_rows, slot_rows)], axis=0)
    gathered = allgather_hbm(stacked, "allgather_weights")
    g_in, g_ba, g_bb, g_o, g_fg, g_fu, g_fd = (gathered[:, o:o + r] for o, r in zip(row_offs, shard_rows))
    w_in_t = _permute_in_rows(g_in.reshape(IN_DIM, D))
    w_ba = g_ba.reshape(D, D)
    w_bb = g_bb.reshape(B_INNER, D)
    w_oo = g_o.reshape(D, D)
    ffpad = ((0, D_FF_PAD - D_FF), (0, 0))
    w_gu_t = jnp.concatenate([jnp.pad(g_fg.reshape(D_FF, D), ffpad), jnp.pad(g_fu.reshape(D_FF, D), ffpad)], axis=0)
    w_dn = jnp.pad(g_fd.reshape(D_FF, D), ffpad)

    lb = lower_bound_fwd(hgrn_lb)
    u1 = ln_modulate(xt, mod6, 0, 1, "ln_modulate_1")
    proj = mm_nt(u1, w_in_t, F32, "mm_in_proj")
    o_a, o_raw, st_a = hgrn_fwd(proj, lb, hgrn_gnorm)
    xc = conv_fwd(proj, conv_w, ssm_conv_b)
    pad3 = ((0, 0), (0, 0), (0, N_STATE - HEADS_PER_GROUP))
    alog4 = jnp.pad(ssm_a_log.reshape(N_GROUPS, 1, HEADS_PER_GROUP), pad3)
    bias4 = jnp.pad(ssm_dt_bias.reshape(N_GROUPS, 1, HEADS_PER_GROUP), pad3)
    dskip4 = jnp.pad(ssm_d.reshape(N_GROUPS, 1, HEADS_PER_GROUP), pad3)
    expand = _head_expand()
    o_b, st_b = ssd_fwd(proj, xc, alog4, bias4, dskip4, ssm_norm, expand)
    ya = mm_nn(o_a, w_ba, F32, "mm_branch_a")
    yb = mm_nn(o_b, w_bb, F32, "mm_branch_b")
    merged = merge_gates(ya, yb, proj)
    h1 = mm_nn(merged, w_oo, F32, "mm_out_proj")
    x1 = resid_ln(xt, h1, mod6, 2, ln1_g, ln1_b, "resid_ln_1")
    u2 = ln_modulate(x1, mod6, 3, 4, "ln_modulate_2")
    gu = mm_nt(u2, w_gu_t, F32, "mm_ffn_in")
    act = swiglu_act(gu)
    h2 = mm_nn(act, w_dn, F32, "mm_ffn_out")

    dh2, dx1_part, acc4 = resid_ln_bwd(x1, h2, mod6, 5, ln2_g, ln2_b, tgt, True, "resid_ln_2_bwd")
    g_dn = mm_tn(act, dh2, "mm_grad_ffn_down")
    dact = mm_nt(dh2, w_dn, F32, "mm_dact")
    dgu = swiglu_act_bwd(gu, dact)
    g_gu_t = mm_tn(dgu, u2, "mm_grad_ffn_in")
    du2 = mm_nn(dgu, w_gu_t, F32, "mm_du2")
    dx1, acc3 = ln_modulate_bwd(x1, du2, mod6, 4, dx1_part, "ln_modulate_2_bwd")
    dh1, dx_part, acc2 = resid_ln_bwd(xt, h1, mod6, 2, ln1_g, ln1_b, dx1, False, "resid_ln_1_bwd")
    g_o = mm_tn(merged, dh1, "mm_grad_out_proj")
    dmerged = mm_nt(dh1, w_oo, F32, "mm_dmerged")
    dya, dyb, dga, dgb = merge_gates_bwd(dmerged, ya, yb, proj)
    g_ba_full = mm_tn(o_a, dya, "mm_grad_branch_a")
    g_bb_full = mm_tn(o_b, dyb, "mm_grad_branch_b")
    doa = mm_nt(dya, w_ba, F32, "mm_doa")
    dob = mm_nt(dyb, w_bb, F32, "mm_dob")
    dq, dfl, di, dg, dlb, dgn = hgrn_bwd(proj, lb, hgrn_gnorm, o_raw, doa, st_a)
    dxs, dbm, dcm, dz, ddt, dwn, dalog, dbias, ddsk = ssd_bwd(proj, xc, alog4, bias4, dskip4, ssm_norm, expand, dob, st_b)
    dxc = jnp.concatenate([dxs, dbm, dcm], axis=1)
    dxbc, dcw, dcb = conv_bwd(proj, dxc, conv_w, ssm_conv_b)
    dproj = jnp.concatenate([dq, dfl, di, dg, dz, dxbc, dga, dgb, ddt, jnp.zeros((t, D - N_GROUPS * N_STATE), BF16)], axis=1)
    g_in_t = mm_tn(dproj, u1, "mm_grad_in_proj")
    du1 = mm_nn(dproj, w_in_t, F32, "mm_du1")
    dx, acc1 = ln_modulate_bwd(xt, du1, mod6, 1, dx_part, "ln_modulate_1_bwd")

    blocks = [_unpermute_in_rows(g_in_t), g_ba_full, g_bb_full, g_o, g_gu_t[:D_FF], g_gu_t[D_FF_PAD:D_FF_PAD + D_FF],
              g_dn[:D_FF]]
    contrib = jnp.concatenate([jnp.pad(b.reshape(N_DEV, -1, D), ((0, 0), (0, p - r), (0, 0)))
                               for b, r, p in zip(blocks, shard_rows, slot_rows)], axis=1)
    by_core = contrib.reshape(N_DEV // 2, 2, rows_pad, D).transpose(1, 0, 2, 3)
    my_core = lax.axis_index("c")
    keep = lax.dynamic_index_in_dim(by_core, my_core, 0, keepdims=False)
    give = lax.dynamic_index_in_dim(by_core, 1 - my_core, 0, keepdims=False)
    got = exchange_sibling(give, "exchange_grads_sibling")
    chip_part = sum_pair(keep.reshape(-1, D), got.reshape(-1, D), "sum_grads_chip").reshape(keep.shape)
    parts = exchange_chips(chip_part, "exchange_grads_chips")
    g_rows = sum_parts(parts, "sum_grads_all")
    gw_in, gw_ba, gw_bb, gw_o, gw_fg, gw_fu, gw_fd = (g_rows[o:o + r] for o, r in zip(row_offs, shard_rows))
    gw_in, gw_fg, gw_fu = gw_in.T, gw_fg.T, gw_fu.T

    dmod = jnp.concatenate([acc1[1:2], acc1[0:1], acc2[0:1], acc3[1:2], acc3[0:1], acc4[0:1]], axis=1)
    small_fields = [dmod, acc4[3:4, :128], dlb, dgn, dcw[:CONV_TAPS], dcb, dbias, dalog, ddsk, dwn,
                    acc2[1:2], acc2[2:3], acc4[1:2], acc4[2:3]]
    small_out, offs = _pack(small_fields)
    small_sum_in = allgather_vmem(small_out, "allgather_small_grads")
    gsum, g_lb = reduce_small(small_sum_in, hgrn_lb, offs[2])
    (g_bada, loss_row, _, g_gn, g_cw_full, g_cb, g_bias4, g_alog4, g_dsk4, g_wn, g_l1g, g_l1b, g_l2g, g_l2b) = _unpack(
        gsum, offs, [(1, 6 * D), (1, 128), (1, D), (1, HK), (CONV_TAPS, CONV_DIM), (1, CONV_DIM),
                     (N_GROUPS, N_STATE), (N_GROUPS, N_STATE), (N_GROUPS, N_STATE), (1, B_INNER),
                     (1, D), (1, D), (1, D), (1, D)])
    loss = loss_row[0, 0]
    g_cw = lax.dynamic_slice(g_cw_full, (0, me * conv_cols), (CONV_TAPS, conv_cols))[None]
    g_dtb = g_bias4[:, :HEADS_PER_GROUP].reshape(1, 32)
    g_alog = g_alog4[:, :HEADS_PER_GROUP].reshape(1, 32)
    g_dsk = g_dsk4[:, :HEADS_PER_GROUP].reshape(1, 32)

    dmod_all = small_sum_in[:, 0, offs[0]:offs[0] + 6 * D]
    dmod_cols = lax.dynamic_slice(dmod_all, (0, me * ada_cols), (N_DEV, ada_cols))
    gw_ada = ada_weight_grad(c_all, dmod_cols)

    big = [("ada", w_ada[0], gw_ada, m_w_ada[0], v_w_ada[0]), ("in", w_in[0], gw_in, m_w_in[0], v_w_in[0]),
           ("branch_a", w_branch_a[0], gw_ba, m_w_branch_a[0], v_w_branch_a[0]),
           ("branch_b", w_branch_b[0], gw_bb, m_w_branch_b[0], v_w_branch_b[0]),
           ("o", w_o[0], gw_o, m_w_o[0], v_w_o[0]),
           ("ffn_gate", w_ffn_gate[0], gw_fg, m_w_ffn_gate[0], v_w_ffn_gate[0]),
           ("ffn_up", w_ffn_up[0], gw_fu, m_w_ffn_up[0], v_w_ffn_up[0]),
           ("ffn_down", w_ffn_down[0], gw_fd, m_w_ffn_down[0], v_w_ffn_down[0])]
    big_out = {}
    for nm, w_, g_, m_, v_ in big:
        d_, m2_, v2_ = adamw(w_, g_, m_, v_, "adamw_" + nm)
        big_out[nm] = (g_[None], d_[None], m2_[None], v2_[None])

    small_w = [b_ada, hgrn_lb, hgrn_gnorm, ssm_conv_w, ssm_conv_b, ssm_dt_bias, ssm_a_log, ssm_d, ssm_norm,
               ln1_g, ln1_b, ln2_g, ln2_b]
    small_g = [g_bada, g_lb, g_gn, g_cw, g_cb, g_dtb, g_alog, g_dsk, g_wn, g_l1g, g_l1b, g_l2g, g_l2b]
    small_m = [m_b_ada, m_hgrn_lb, m_hgrn_gnorm, m_ssm_conv_w, m_ssm_conv_b, m_ssm_dt_bias, m_ssm_a_log, m_ssm_d,
               m_ssm_norm, m_ln1_g, m_ln1_b, m_ln2_g, m_ln2_b]
    small_v = [v_b_ada, v_hgrn_lb, v_hgrn_gnorm, v_ssm_conv_w, v_ssm_conv_b, v_ssm_dt_bias, v_ssm_a_log, v_ssm_d,
               v_ssm_norm, v_ln1_g, v_ln1_b, v_ln2_g, v_ln2_b]
    shapes = [a.shape for a in small_w]
    small_g = [g_.reshape(s) for g_, s in zip(small_g, shapes)]
    pw, poffs = _pack(small_w)
    pg, _ = _pack(small_g)
    pm, _ = _pack(small_m)
    pv, _ = _pack(small_v)
    pd, pm2, pv2 = adamw(pw, pg, pm, pv, "adamw_small")
    s_d, s_m, s_v = (_unpack(p, poffs, shapes) for p in (pd, pm2, pv2))
    (sn_bada, sn_lb, sn_gn, sn_cw, sn_cb, sn_dtb, sn_alog, sn_dsk, sn_wn, sn_l1g, sn_l1b, sn_l2g, sn_l2b) = range(13)

    def order(kind):
        sm = [small_g, s_d, s_m, s_v][kind]
        bg = lambda nm: big_out[nm][kind]
        return [bg("ada"), sm[sn_bada], bg("in"), sm[sn_lb], sm[sn_gn], sm[sn_cw], sm[sn_cb], sm[sn_dtb], sm[sn_alog],
                sm[sn_dsk], sm[sn_wn], bg("branch_a"), bg("branch_b"), bg("o"), sm[sn_l1g], sm[sn_l1b],
                bg("ffn_gate"), bg("ffn_up"), bg("ffn_down"), sm[sn_l2g], sm[sn_l2b]]

    return (loss, dx[None], *order(0), *order(1), *order(2), *order(3))
```

```python
import functools

import numpy as np
import jax
import jax.numpy as jnp
from jax import lax
from jax.experimental import pallas as pl
from jax.experimental.pallas import tpu as pltpu

F32 = jnp.float32
BF16 = jnp.bfloat16
HI = lax.Precision.HIGHEST

N_DEV = 8
D = 1024
N_HEADS_A = 8
HK = 128
CHUNK = 64
SSD_CHUNK = 128
N_GROUPS = 4
HEADS_PER_GROUP = 8
HEAD_P = 64
N_STATE = 128
GROUP_W = HEADS_PER_GROUP * HEAD_P
B_INNER = 2048
CONV_DIM = 3072
D_FF = 2816
D_FF_PAD = 3072
IN_DIM = 11296
N_PROJ = 12288
ALPHA = 2.0 ** 0.25
LN_EPS = 1e-5
RMS_EPS = 1e-6
Q_SCALE = 128 ** -0.5
EXP_CLIP = 80.0
ADAM_LR, ADAM_B1, ADAM_B2, ADAM_EPS, ADAM_WD, ADAM_STEP = 0.001, 0.9, 0.999, 1e-8, 0.01, 10
VMEM_LIMIT = 48 * 1024 * 1024
TOKEN_BLOCK = 512
ROW_TILE = 256
FFN_ROW_TILE = 128
MM_ROW_TILE = 1024
MM_TOKEN_TILE = 512
CHUNK_UNROLL = 8
MESH_ID = pl.DeviceIdType.MESH

NT_DIMS = (((1,), (1,)), ((), ()))
TN_DIMS = (((0,), (0,)), ((), ()))


def _cparams(sem=None):
    return pltpu.CompilerParams(dimension_semantics=sem, vmem_limit_bytes=VMEM_LIMIT)


def _sigmoid(x):
    return 1.0 / (1.0 + jnp.exp(-x))


def _dsilu(x, s):
    return s * (1.0 + x * (1.0 - s))


def _nt(a, b, precision=None):
    return lax.dot_general(a, b, NT_DIMS, precision=precision, preferred_element_type=F32)


def _tn(a, b, precision=None):
    return lax.dot_general(a, b, TN_DIMS, precision=precision, preferred_element_type=F32)


def _nn(a, b, precision=None):
    return jnp.dot(a, b, precision=precision, preferred_element_type=F32)


def _split(x, pieces):
    out = []
    for i in range(pieces):
        p = x.astype(BF16)
        out.append(p)
        if i + 1 < pieces:
            x = x - p.astype(F32)
    return out


def _sel(dot, x, sel01, pieces, x_first=True):
    acc = None
    for p in _split(x, pieces):
        term = dot(p, sel01) if x_first else dot(sel01, p)
        acc = term if acc is None else acc + term
    return acc


def _ln(x):
    mu = jnp.mean(x, axis=-1, keepdims=True)
    xc = x - mu
    rstd = lax.rsqrt(jnp.mean(xc * xc, axis=-1, keepdims=True) + LN_EPS)
    return xc * rstd, rstd


def _ln_bwd(dxh, xh, rstd):
    return rstd * (dxh - jnp.mean(dxh, axis=-1, keepdims=True) - xh * jnp.mean(dxh * xh, axis=-1, keepdims=True))


def _colsum(x):
    return jnp.sum(x, axis=0, keepdims=True)


def _tri(n, upper=False):
    r = lax.broadcasted_iota(jnp.int32, (n, n), 0)
    c = lax.broadcasted_iota(jnp.int32, (n, n), 1)
    return (c >= r) if upper else (r >= c)


def _my_pos():
    return lax.axis_index("x"), lax.axis_index("y"), lax.axis_index("c")


def _peer(pos, k):
    x, y, c = pos
    return (x ^ ((k >> 2) & 1), y ^ ((k >> 1) & 1), c ^ (k & 1))


def _flat(pos):
    return 4 * pos[0] + 2 * pos[1] + pos[2]


def allgather_vmem(v, name):
    n = v.shape[1]

    def body(v_ref, o_ref, send_sems, recv_sems, local_sem):
        me = _my_pos()
        mine = pltpu.make_async_copy(v_ref, o_ref.at[_flat(me)], local_sem)
        mine.start()
        sends = []
        for k in range(1, N_DEV):
            peer = _peer(me, k)
            cp = pltpu.make_async_remote_copy(v_ref, o_ref.at[_flat(me)], send_sems.at[k - 1], recv_sems.at[k - 1],
                                              device_id=peer, device_id_type=MESH_ID)
            cp.start()
            sends.append(cp)
        for k in range(1, N_DEV):
            peer = _peer(me, k)
            pltpu.make_async_remote_copy(v_ref, o_ref.at[_flat(peer)], send_sems.at[k - 1], recv_sems.at[k - 1],
                                         device_id=peer, device_id_type=MESH_ID).wait_recv()
        for cp in sends:
            cp.wait_send()
        mine.wait()

    return pl.pallas_call(
        body, name=name,
        out_shape=jax.ShapeDtypeStruct((N_DEV, 1, n), F32),
        in_specs=[pl.BlockSpec(memory_space=pltpu.VMEM)],
        out_specs=pl.BlockSpec(memory_space=pltpu.VMEM),
        scratch_shapes=[pltpu.SemaphoreType.DMA((N_DEV - 1,)), pltpu.SemaphoreType.DMA((N_DEV - 1,)),
                        pltpu.SemaphoreType.DMA],
        compiler_params=_cparams(),
    )(v)


def ada_modulation(c_all, w_ada_s, b_ada_r):
    ncol = w_ada_s.shape[1]

    def body(c_ref, w_ref, b_ref, o_ref, part_ref, send_sems, recv_sems):
        me = _my_pos()
        cval = c_ref[...]
        cond = cval * _sigmoid(cval)
        part = _nn(cond, w_ref[...], HI)
        for r in range(N_DEV):
            part_ref[r] = part[r:r + 1, :]
        sends = []
        for k in range(1, N_DEV):
            peer = _peer(me, k)
            cp = pltpu.make_async_remote_copy(part_ref.at[_flat(peer)], o_ref.at[_flat(me)], send_sems.at[k - 1],
                                              recv_sems.at[k - 1], device_id=peer, device_id_type=MESH_ID)
            cp.start()
            sends.append(cp)
        o_ref[_flat(me)] = part_ref[_flat(me)]
        for k in range(1, N_DEV):
            peer = _peer(me, k)
            pltpu.make_async_remote_copy(part_ref.at[_flat(peer)], o_ref.at[_flat(peer)], send_sems.at[k - 1],
                                         recv_sems.at[k - 1], device_id=peer, device_id_type=MESH_ID).wait_recv()
        for cp in sends:
            cp.wait_send()
        o_ref[...] = o_ref[...] + b_ref[...]

    return pl.pallas_call(
        body, name="ada_modulation",
        out_shape=jax.ShapeDtypeStruct((N_DEV, 1, ncol), F32),
        in_specs=[pl.BlockSpec(memory_space=pltpu.VMEM)] * 3,
        out_specs=pl.BlockSpec(memory_space=pltpu.VMEM),
        scratch_shapes=[pltpu.VMEM((N_DEV, 1, ncol), F32), pltpu.SemaphoreType.DMA((N_DEV - 1,)),
                        pltpu.SemaphoreType.DMA((N_DEV - 1,))],
        compiler_params=_cparams(),
    )(c_all, w_ada_s, b_ada_r)


def allgather_hbm(shard, name):
    def body(x_ref, out_ref, send_sems, recv_sems, local_sem):
        x, y, c = _my_pos()
        me, sibling = (x, y, c), (x, y, 1 - c)
        chips = [(1 - x, y), (x, 1 - y), (1 - x, 1 - y)]

        def slot(pos):
            return out_ref.at[_flat(pos)]

        def copy(k, block, to, src=None):
            return pltpu.make_async_remote_copy(slot(block) if src is None else src, slot(block), send_sems.at[k],
                                                recv_sems.at[k], device_id=to, device_id_type=MESH_ID)

        mine = pltpu.make_async_copy(x_ref, slot(me), local_sem)
        mine.start()
        first = [copy(0, me, sibling, src=x_ref)]
        first += [copy(1 + j, me, (*chip, c), src=x_ref) for j, chip in enumerate(chips)]
        for cp in first:
            cp.start()
        passed = [copy(4 + j, (*chip, c), sibling) for j, chip in enumerate(chips)]
        for j, chip in enumerate(chips):
            copy(1 + j, (*chip, c), me).wait_recv()
            passed[j].start()
        copy(0, sibling, me).wait_recv()
        for j, chip in enumerate(chips):
            copy(4 + j, (*chip, 1 - c), me).wait_recv()
        for cp in first + passed:
            cp.wait_send()
        mine.wait()

    return pl.pallas_call(
        body, name=name,
        out_shape=jax.ShapeDtypeStruct((N_DEV,) + shard.shape, shard.dtype),
        in_specs=[pl.BlockSpec(memory_space=pl.ANY)],
        out_specs=pl.BlockSpec(memory_space=pl.ANY),
        scratch_shapes=[pltpu.SemaphoreType.DMA((N_DEV - 1,)), pltpu.SemaphoreType.DMA((N_DEV - 1,)),
                        pltpu.SemaphoreType.DMA],
        compiler_params=_cparams(),
    )(shard)


def exchange_sibling(send, name):
    def body(s_ref, o_ref, send_sem, recv_sem):
        x, y, c = _my_pos()
        cp = pltpu.make_async_remote_copy(s_ref, o_ref, send_sem, recv_sem, device_id=(x, y, 1 - c),
                                          device_id_type=MESH_ID)
        cp.start()
        cp.wait()

    return pl.pallas_call(
        body, name=name,
        out_shape=jax.ShapeDtypeStruct(send.shape, send.dtype),
        in_specs=[pl.BlockSpec(memory_space=pl.ANY)],
        out_specs=pl.BlockSpec(memory_space=pl.ANY),
        scratch_shapes=[pltpu.SemaphoreType.DMA, pltpu.SemaphoreType.DMA],
        compiler_params=_cparams(),
    )(send)


def exchange_chips(part, name):
    n_chip = N_DEV // 2

    def body(p_ref, o_ref, send_sems, recv_sems, local_sem):
        x, y, c = _my_pos()
        my_chip = 2 * x + y
        mine = pltpu.make_async_copy(p_ref.at[my_chip], o_ref.at[my_chip], local_sem)
        mine.start()
        sends = []
        for k in range(1, n_chip):
            px, py = x ^ (k >> 1), y ^ (k & 1)
            cp = pltpu.make_async_remote_copy(p_ref.at[2 * px + py], o_ref.at[my_chip], send_sems.at[k - 1],
                                              recv_sems.at[k - 1], device_id=(px, py, c), device_id_type=MESH_ID)
            cp.start()
            sends.append(cp)
        for k in range(1, n_chip):
            px, py = x ^ (k >> 1), y ^ (k & 1)
            pltpu.make_async_remote_copy(p_ref.at[2 * px + py], o_ref.at[2 * px + py], send_sems.at[k - 1],
                                         recv_sems.at[k - 1], device_id=(px, py, c), device_id_type=MESH_ID).wait_recv()
        for cp in sends:
            cp.wait_send()
        mine.wait()

    return pl.pallas_call(
        body, name=name,
        out_shape=jax.ShapeDtypeStruct(part.shape, part.dtype),
        in_specs=[pl.BlockSpec(memory_space=pl.ANY)],
        out_specs=pl.BlockSpec(memory_space=pl.ANY),
        scratch_shapes=[pltpu.SemaphoreType.DMA((n_chip - 1,)), pltpu.SemaphoreType.DMA((n_chip - 1,)),
                        pltpu.SemaphoreType.DMA],
        compiler_params=_cparams(),
    )(part)


def mm_nn(a, b, out_dtype, name):
    m, kdim = a.shape
    n = b.shape[1]
    tm, tn, tk = min(MM_ROW_TILE, m), 1024, 1024
    nk = kdim // tk

    def body(a_ref, b_ref, o_ref, acc_ref):
        p = _nn(a_ref[...], b_ref[...])
        if nk == 1:
            o_ref[...] = p.astype(o_ref.dtype)
        else:
            k = pl.program_id(2)

            @pl.when(k == 0)
            def _():
                acc_ref[...] = p

            @pl.when(k > 0)
            def _():
                acc_ref[...] += p

            @pl.when(k == nk - 1)
            def _():
                o_ref[...] = acc_ref[...].astype(o_ref.dtype)

    return pl.pallas_call(
        body, name=name, grid=(n // tn, m // tm, nk),
        out_shape=jax.ShapeDtypeStruct((m, n), out_dtype),
        in_specs=[pl.BlockSpec((tm, tk), lambda j, i, k: (i, k)), pl.BlockSpec((tk, tn), lambda j, i, k: (k, j))],
        out_specs=pl.BlockSpec((tm, tn), lambda j, i, k: (i, j)),
        scratch_shapes=[pltpu.VMEM((tm, tn), F32)],
        compiler_params=_cparams(("parallel", "parallel", "arbitrary")),
    )(a, b)


def mm_nt(a, b, out_dtype, name):
    m, kdim = a.shape
    n = b.shape[0]
    tm, tn, tk = min(MM_ROW_TILE, m), 1024, 1024
    nk = kdim // tk

    def body(a_ref, b_ref, o_ref, acc_ref):
        p = _nt(a_ref[...], b_ref[...])
        if nk == 1:
            o_ref[...] = p.astype(o_ref.dtype)
        else:
            k = pl.program_id(2)

            @pl.when(k == 0)
            def _():
                acc_ref[...] = p

            @pl.when(k > 0)
            def _():
                acc_ref[...] += p

            @pl.when(k == nk - 1)
            def _():
                o_ref[...] = acc_ref[...].astype(o_ref.dtype)

    return pl.pallas_call(
        body, name=name, grid=(n // tn, m // tm, nk),
        out_shape=jax.ShapeDtypeStruct((m, n), out_dtype),
        in_specs=[pl.BlockSpec((tm, tk), lambda j, i, k: (i, k)), pl.BlockSpec((tn, tk), lambda j, i, k: (j, k))],
        out_specs=pl.BlockSpec((tm, tn), lambda j, i, k: (i, j)),
        scratch_shapes=[pltpu.VMEM((tm, tn), F32)],
        compiler_params=_cparams(("parallel", "parallel", "arbitrary")),
    )(a, b)


def mm_tn(a, b, name):
    t, ka = a.shape
    n = b.shape[1]
    tt, tka, tn = min(MM_TOKEN_TILE, t), 1024, 1024
    nt = t // tt

    def body(a_ref, b_ref, o_ref, acc_ref):
        p = _tn(a_ref[...], b_ref[...])
        s = pl.program_id(2)

        @pl.when(s == 0)
        def _():
            acc_ref[...] = p

        @pl.when(s > 0)
        def _():
            acc_ref[...] += p

        @pl.when(s == nt - 1)
        def _():
            o_ref[...] = acc_ref[...].astype(o_ref.dtype)

    return pl.pallas_call(
        body, name=name, grid=(ka // tka, n // tn, nt),
        out_shape=jax.ShapeDtypeStruct((ka, n), BF16),
        in_specs=[pl.BlockSpec((tt, tka), lambda i, j, s: (s, i)), pl.BlockSpec((tt, tn), lambda i, j, s: (s, j))],
        out_specs=pl.BlockSpec((tka, tn), lambda i, j, s: (i, j)),
        scratch_shapes=[pltpu.VMEM((tka, tn), F32)],
        compiler_params=_cparams(("parallel", "parallel", "arbitrary")),
    )(a, b)


def _tile(t, cap):
    return min(cap, t)


def ln_modulate(x, mod6, shift_row, scale_row, name):
    t = x.shape[0]
    tm = _tile(t, ROW_TILE)

    def body(x_ref, mod_ref, o_ref):
        xh, _ = _ln(x_ref[...])
        sc = mod_ref[scale_row:scale_row + 1, :]
        sh = mod_ref[shift_row:shift_row + 1, :]
        o_ref[...] = (xh * (1.0 + sc) + sh).astype(BF16)

    return pl.pallas_call(
        body, name=name, grid=(t // tm,),
        out_shape=jax.ShapeDtypeStruct((t, D), BF16),
        in_specs=[pl.BlockSpec((tm, D), lambda i: (i, 0)), pl.BlockSpec((6, D), lambda i: (0, 0))],
        out_specs=pl.BlockSpec((tm, D), lambda i: (i, 0)),
        compiler_params=_cparams(("parallel",)),
    )(x, mod6)


def resid_ln(x, h, mod6, gate_row, ln_g, ln_b, name):
    t = x.shape[0]
    tm = _tile(t, ROW_TILE)

    def body(x_ref, h_ref, mod_ref, g_ref, b_ref, o_ref):
        r = ALPHA * x_ref[...] + mod_ref[gate_row:gate_row + 1, :] * h_ref[...]
        rh, _ = _ln(r)
        o_ref[...] = rh * g_ref[...] + b_ref[...]

    row = pl.BlockSpec((tm, D), lambda i: (i, 0))
    vec = pl.BlockSpec((1, D), lambda i: (0, 0))
    return pl.pallas_call(
        body, name=name, grid=(t // tm,),
        out_shape=jax.ShapeDtypeStruct((t, D), F32),
        in_specs=[row, row, pl.BlockSpec((6, D), lambda i: (0, 0)), vec, vec],
        out_specs=row,
        compiler_params=_cparams(("parallel",)),
    )(x, h, mod6, ln_g, ln_b)


def resid_ln_bwd(x, h, mod6, gate_row, ln_g, ln_b, cot, with_loss, name):
    t = x.shape[0]
    tm = _tile(t, ROW_TILE)

    def body(x_ref, h_ref, mod_ref, g_ref, b_ref, c_ref, dh_ref, dx_ref, acc_ref):
        @pl.when(pl.program_id(0) == 0)
        def _():
            acc_ref[...] = jnp.zeros_like(acc_ref)

        gate = mod_ref[gate_row:gate_row + 1, :]
        hv = h_ref[...]
        r = ALPHA * x_ref[...] + gate * hv
        rh, rstd = _ln(r)
        lng = g_ref[...]
        if with_loss:
            diff = rh * lng + b_ref[...] - c_ref[...]
            dxo = diff * (1.0 / D)
            lsum = jnp.sum(_colsum(diff * diff), axis=-1, keepdims=True) * (0.5 / D)
            acc_ref[3:4, :] += jnp.broadcast_to(lsum, (1, D))
        else:
            dxo = c_ref[...]
        acc_ref[1:2, :] += _colsum(dxo * rh)
        acc_ref[2:3, :] += _colsum(dxo)
        dr = _ln_bwd(dxo * lng, rh, rstd)
        acc_ref[0:1, :] += _colsum(dr * hv)
        dh_ref[...] = (gate * dr).astype(BF16)
        dx_ref[...] = ALPHA * dr

    row = pl.BlockSpec((tm, D), lambda i: (i, 0))
    vec = pl.BlockSpec((1, D), lambda i: (0, 0))
    return pl.pallas_call(
        body, name=name, grid=(t // tm,),
        out_shape=[jax.ShapeDtypeStruct((t, D), BF16), jax.ShapeDtypeStruct((t, D), F32),
                   jax.ShapeDtypeStruct((8, D), F32)],
        in_specs=[row, row, pl.BlockSpec((6, D), lambda i: (0, 0)), vec, vec, row],
        out_specs=[row, row, pl.BlockSpec((8, D), lambda i: (0, 0))],
        compiler_params=_cparams(("arbitrary",)),
    )(x, h, mod6, ln_g, ln_b, cot)


def ln_modulate_bwd(x, du, mod6, scale_row, dx_part, name):
    t = x.shape[0]
    tm = _tile(t, ROW_TILE)

    def body(x_ref, du_ref, mod_ref, dp_ref, dx_ref, acc_ref):
        @pl.when(pl.program_id(0) == 0)
        def _():
            acc_ref[...] = jnp.zeros_like(acc_ref)

        xh, rstd = _ln(x_ref[...])
        du_v = du_ref[...]
        sc = mod_ref[scale_row:scale_row + 1, :]
        acc_ref[0:1, :] += _colsum(du_v * xh)
        acc_ref[1:2, :] += _colsum(du_v)
        dx_ref[...] = dp_ref[...] + _ln_bwd(du_v * (1.0 + sc), xh, rstd)

    row = pl.BlockSpec((tm, D), lambda i: (i, 0))
    return pl.pallas_call(
        body, name=name, grid=(t // tm,),
        out_shape=[jax.ShapeDtypeStruct((t, D), F32), jax.ShapeDtypeStruct((8, D), F32)],
        in_specs=[row, row, pl.BlockSpec((6, D), lambda i: (0, 0)), row],
        out_specs=[row, pl.BlockSpec((8, D), lambda i: (0, 0))],
        compiler_params=_cparams(("arbitrary",)),
    )(x, du, mod6, dx_part)


def merge_gates(ya, yb, proj):
    t = ya.shape[0]
    tm = _tile(t, ROW_TILE)

    def body(ya_ref, yb_ref, ga_ref, gb_ref, o_ref):
        o_ref[...] = (_sigmoid(ga_ref[...]) * ya_ref[...] + _sigmoid(gb_ref[...]) * yb_ref[...]).astype(BF16)

    row = pl.BlockSpec((tm, D), lambda i: (i, 0))
    return pl.pallas_call(
        body, name="merge_gates", grid=(t // tm,),
        out_shape=jax.ShapeDtypeStruct((t, D), BF16),
        in_specs=[row, row, pl.BlockSpec((tm, D), lambda i: (i, 9)), pl.BlockSpec((tm, D), lambda i: (i, 10))],
        out_specs=row,
        compiler_params=_cparams(("parallel",)),
    )(ya, yb, proj, proj)


def merge_gates_bwd(dm, ya, yb, proj):
    t = ya.shape[0]
    tm = _tile(t, ROW_TILE)

    def body(dm_ref, ya_ref, yb_ref, ga_ref, gb_ref, dya_ref, dyb_ref, dga_ref, dgb_ref):
        dmv = dm_ref[...]
        sa = _sigmoid(ga_ref[...])
        sb = _sigmoid(gb_ref[...])
        dya_ref[...] = (dmv * sa).astype(BF16)
        dyb_ref[...] = (dmv * sb).astype(BF16)
        dga_ref[...] = (dmv * ya_ref[...] * sa * (1.0 - sa)).astype(BF16)
        dgb_ref[...] = (dmv * yb_ref[...] * sb * (1.0 - sb)).astype(BF16)

    row = pl.BlockSpec((tm, D), lambda i: (i, 0))
    return pl.pallas_call(
        body, name="merge_gates_bwd", grid=(t // tm,),
        out_shape=[jax.ShapeDtypeStruct((t, D), BF16)] * 4,
        in_specs=[row, row, row, pl.BlockSpec((tm, D), lambda i: (i, 9)), pl.BlockSpec((tm, D), lambda i: (i, 10))],
        out_specs=[row] * 4,
        compiler_params=_cparams(("parallel",)),
    )(dm, ya, yb, proj, proj)


def swiglu_act(gu):
    t = gu.shape[0]
    tm = _tile(t, FFN_ROW_TILE)

    def body(gu_ref, o_ref):
        for j in range(D_FF_PAD // D):
            g = gu_ref[:, j * D:(j + 1) * D]
            u = gu_ref[:, D_FF_PAD + j * D:D_FF_PAD + (j + 1) * D]
            o_ref[:, j * D:(j + 1) * D] = (g * _sigmoid(g) * u).astype(BF16)

    return pl.pallas_call(
        body, name="swiglu_act", grid=(t // tm,),
        out_shape=jax.ShapeDtypeStruct((t, D_FF_PAD), BF16),
        in_specs=[pl.BlockSpec((tm, 2 * D_FF_PAD), lambda i: (i, 0))],
        out_specs=pl.BlockSpec((tm, D_FF_PAD), lambda i: (i, 0)),
        compiler_params=_cparams(("parallel",)),
    )(gu)


def swiglu_act_bwd(gu, dact):
    t = gu.shape[0]
    tm = _tile(t, FFN_ROW_TILE)

    def body(gu_ref, da_ref, o_ref):
        for j in range(D_FF_PAD // D):
            g = gu_ref[:, j * D:(j + 1) * D]
            u = gu_ref[:, D_FF_PAD + j * D:D_FF_PAD + (j + 1) * D]
            da = da_ref[:, j * D:(j + 1) * D]
            s = _sigmoid(g)
            o_ref[:, j * D:(j + 1) * D] = (da * u * _dsilu(g, s)).astype(BF16)
            o_ref[:, D_FF_PAD + j * D:D_FF_PAD + (j + 1) * D] = (da * g * s).astype(BF16)

    return pl.pallas_call(
        body, name="swiglu_act_bwd", grid=(t // tm,),
        out_shape=jax.ShapeDtypeStruct((t, 2 * D_FF_PAD), BF16),
        in_specs=[pl.BlockSpec((tm, 2 * D_FF_PAD), lambda i: (i, 0)), pl.BlockSpec((tm, D_FF_PAD), lambda i: (i, 0))],
        out_specs=pl.BlockSpec((tm, 2 * D_FF_PAD), lambda i: (i, 0)),
        compiler_params=_cparams(("parallel",)),
    )(gu, dact)


def _hgrn_chunk_terms(q, fl, lbv, tril_f):
    sig = _sigmoid(fl)
    f = lbv + (1.0 - lbv) * sig
    lam = jnp.log(f)
    k = 1.0 - f
    sq = _sigmoid(q)
    qt = q * sq * Q_SCALE
    bc = _sel(_nn, lam, tril_f, 3, x_first=False)
    bmid = bc[CHUNK // 2 - 1:CHUNK // 2, :]
    bl = bc[CHUNK - 1:CHUNK, :]
    eq = jnp.exp(jnp.minimum(bc - bmid, EXP_CLIP))
    ek = jnp.exp(jnp.minimum(bmid - bc, EXP_CLIP))
    eb = jnp.exp(bc)
    ekl = jnp.exp(bl - bc)
    ebl = jnp.exp(bl)
    return sig, f, k, sq, qt, eq, ek, eb, ekl, ebl


def hgrn_fwd(proj, lb, gnorm):
    t = proj.shape[0]
    tb = _tile(t, TOKEN_BLOCK)
    ncb = tb // CHUNK

    def body(q_ref, f_ref, i_ref, g_ref, lb_ref, gn_ref, oa_ref, oraw_ref, st_ref, state):
        @pl.when(pl.program_id(1) == 0)
        def _():
            state[...] = jnp.zeros_like(state)

        lbv = lb_ref[...]
        gn = gn_ref[...]
        mask = _tri(CHUNK)
        tril_f = mask.astype(BF16)

        def chunk(c, carry):
            sl = pl.ds(pl.multiple_of(c * CHUNK, CHUNK), CHUNK)
            q, fl, v, g = q_ref[sl, :], f_ref[sl, :], i_ref[sl, :], g_ref[sl, :]
            sig, f, k, sq, qt, eq, ek, eb, ekl, ebl = _hgrn_chunk_terms(q, fl, lbv, tril_f)
            a = jnp.where(mask, _nt((qt * eq).astype(BF16), (k * ek).astype(BF16)), 0.0)
            st = state[...]
            st_ref[0, c] = st
            vb = v.astype(BF16)
            o = _nn(a.astype(BF16), vb) + _nt((qt * eb).astype(BF16), st.astype(BF16))
            state[...] = st * ebl + _tn(vb, (k * ekl).astype(BF16))
            oraw_ref[sl, :] = o
            rn = o * lax.rsqrt(jnp.mean(o * o, axis=-1, keepdims=True) + RMS_EPS)
            oa_ref[sl, :] = (rn * gn * g * _sigmoid(g)).astype(BF16)
            return carry

        lax.fori_loop(0, ncb, chunk, 0, unroll=min(CHUNK_UNROLL, ncb))

    def col(block):
        return pl.BlockSpec((tb, HK), lambda h, j: (j, block * N_HEADS_A + h))

    return pl.pallas_call(
        body, name="hgrn_fwd", grid=(N_HEADS_A, t // tb),
        out_shape=[jax.ShapeDtypeStruct((t, D), BF16), jax.ShapeDtypeStruct((t, D), F32),
                   jax.ShapeDtypeStruct((N_HEADS_A, t // CHUNK, HK, HK), F32)],
        in_specs=[col(0), col(1), col(2), col(3), pl.BlockSpec((1, HK), lambda h, j: (0, h)),
                  pl.BlockSpec((1, HK), lambda h, j: (0, 0))],
        out_specs=[pl.BlockSpec((tb, HK), lambda h, j: (j, h)), pl.BlockSpec((tb, HK), lambda h, j: (j, h)),
                   pl.BlockSpec((1, ncb, HK, HK), lambda h, j: (h, j, 0, 0))],
        scratch_shapes=[pltpu.VMEM((HK, HK), F32)],
        compiler_params=_cparams(("parallel", "arbitrary")),
    )(proj, proj, proj, proj, lb, gnorm)


def hgrn_bwd(proj, lb, gnorm, o_raw, doa, states):
    t = proj.shape[0]
    tb = _tile(t, TOKEN_BLOCK)
    ncb = tb // CHUNK
    nb = t // tb

    def body(q_ref, f_ref, i_ref, g_ref, lb_ref, gn_ref, oraw_ref, doa_ref, st_ref,
             dq_ref, df_ref, di_ref, dg_ref, dlb_ref, dgn_ref, dstate):
        h, j = pl.program_id(0), pl.program_id(1)

        @pl.when(j == 0)
        def _():
            dstate[...] = jnp.zeros_like(dstate)
            dlb_ref[...] = jnp.zeros_like(dlb_ref)

        @pl.when((j == 0) & (h == 0))
        def _():
            dgn_ref[...] = jnp.zeros_like(dgn_ref)

        lbv = lb_ref[...]
        gn = gn_ref[...]
        mask = _tri(CHUNK)
        mask_t = _tri(CHUNK, upper=True)
        tril_f = mask.astype(BF16)
        triu_f = mask_t.astype(BF16)

        def chunk(i, c0):
            c = ncb - 1 - i
            sl = pl.ds(pl.multiple_of(c * CHUNK, CHUNK), CHUNK)
            q, fl, v, g = q_ref[sl, :], f_ref[sl, :], i_ref[sl, :], g_ref[sl, :]
            sig, f, k, sq, qt, eq, ek, eb, ekl, ebl = _hgrn_chunk_terms(q, fl, lbv, tril_f)
            qe = (qt * eq).astype(BF16)
            ke = (k * ek).astype(BF16)
            st32 = st_ref[0, c]
            st = st32.astype(BF16)
            dst = dstate[...]
            dstb = dst.astype(BF16)
            o = oraw_ref[sl, :]
            rstd = lax.rsqrt(jnp.mean(o * o, axis=-1, keepdims=True) + RMS_EPS)
            rn = o * rstd
            sgm = _sigmoid(g)
            sg = g * sgm
            doa_v = doa_ref[sl, :]
            drn = doa_v * gn * sg
            dgn_ref[...] += _colsum(doa_v * rn * sg)
            dg_ref[sl, :] = (doa_v * rn * gn * _dsilu(g, sgm)).astype(BF16)
            do = rstd * (drn - rn * jnp.mean(drn * rn, axis=-1, keepdims=True))
            dob = do.astype(BF16)
            vb = v.astype(BF16)
            da = jnp.where(mask, _nt(dob, vb), 0.0).astype(BF16)
            da_t = jnp.where(mask_t, _nt(vb, dob), 0.0).astype(BF16)
            a_t = jnp.where(mask_t, _nt(ke, qe), 0.0).astype(BF16)
            kl = (k * ekl).astype(BF16)
            qb = (qt * eb).astype(BF16)
            dq_in = _nn(da, ke)
            dk_in = _nn(da_t, qe)
            dq_out = eb * _nn(dob, st)
            dk_out = ekl * _nn(vb, dstb)
            dqt = eq * dq_in + dq_out
            dk = ek * dk_in + dk_out
            dv = _nn(a_t, dob) + _nt(kl, dstb)
            dstate[...] = dst * ebl + _tn(dob, qb)
            dbig = qe.astype(F32) * dq_in - ke.astype(F32) * dk_in + qt * dq_out - k * dk_out
            beyond = _colsum(k * dk_out) + ebl * _colsum(dst * st32)
            dlam = _sel(_nn, dbig, triu_f, 3, x_first=False) + beyond
            df = dlam / f - dk
            df_ref[sl, :] = (df * (1.0 - lbv) * sig * (1.0 - sig)).astype(BF16)
            dlb_ref[...] += _colsum(df * (1.0 - sig))
            dq_ref[sl, :] = (dqt * Q_SCALE * _dsilu(q, sq)).astype(BF16)
            di_ref[sl, :] = dv.astype(BF16)
            return c0

        lax.fori_loop(0, ncb, chunk, 0, unroll=min(CHUNK_UNROLL, ncb))

    def col(block):
        return pl.BlockSpec((tb, HK), lambda h, j: (nb - 1 - j, block * N_HEADS_A + h))

    hcol = pl.BlockSpec((tb, HK), lambda h, j: (nb - 1 - j, h))
    return pl.pallas_call(
        body, name="hgrn_bwd", grid=(N_HEADS_A, nb),
        out_shape=[jax.ShapeDtypeStruct((t, D), BF16)] * 4 + [jax.ShapeDtypeStruct((1, D), F32),
                                                                jax.ShapeDtypeStruct((1, HK), F32)],
        in_specs=[col(0), col(1), col(2), col(3), pl.BlockSpec((1, HK), lambda h, j: (0, h)),
                  pl.BlockSpec((1, HK), lambda h, j: (0, 0)), hcol, hcol,
                  pl.BlockSpec((1, ncb, HK, HK), lambda h, j: (h, nb - 1 - j, 0, 0))],
        out_specs=[hcol] * 4 + [pl.BlockSpec((1, HK), lambda h, j: (0, h)), pl.BlockSpec((1, HK), lambda h, j: (0, 0))],
        scratch_shapes=[pltpu.VMEM((HK, HK), F32)],
        compiler_params=_cparams(("arbitrary", "arbitrary")),
    )(proj, proj, proj, proj, lb, gnorm, o_raw, doa, states)


CONV_BLOCK0 = 6
CONV_TAPS = 4
HALO = 8


def conv_fwd(proj, conv_w, conv_b):
    t = proj.shape[0]
    tm = _tile(t, ROW_TILE)
    r = tm // HALO

    def body(x_ref, halo_ref, w_ref, b_ref, o_ref):
        i = pl.program_id(1)
        halo = jnp.where(i > 0, halo_ref[...], 0.0)
        ext = jnp.concatenate([halo, x_ref[...]], axis=0)
        pre = b_ref[...] + w_ref[CONV_TAPS - 1:CONV_TAPS, :] * ext[HALO:, :]
        for tap in range(CONV_TAPS - 1):
            pre = pre + w_ref[tap:tap + 1, :] * pltpu.roll(ext, CONV_TAPS - 1 - tap, axis=0)[HALO:, :]
        o_ref[...] = pre * _sigmoid(pre)

    return pl.pallas_call(
        body, name="conv_fwd", grid=(CONV_DIM // D, t // tm),
        out_shape=jax.ShapeDtypeStruct((t, CONV_DIM), F32),
        in_specs=[pl.BlockSpec((tm, D), lambda cb, i: (i, CONV_BLOCK0 + cb)),
                  pl.BlockSpec((HALO, D), lambda cb, i: (jnp.maximum(i * r - 1, 0), CONV_BLOCK0 + cb)),
                  pl.BlockSpec((CONV_TAPS, D), lambda cb, i: (0, cb)), pl.BlockSpec((1, D), lambda cb, i: (0, cb))],
        out_specs=pl.BlockSpec((tm, D), lambda cb, i: (i, cb)),
        compiler_params=_cparams(("parallel", "parallel")),
    )(proj, proj, conv_w, conv_b)


def conv_bwd(proj, dxc, conv_w, conv_b):
    t = proj.shape[0]
    tm = _tile(t, ROW_TILE)
    r = tm // HALO
    n = t // tm
    last_halo = t // HALO - 1

    def body(x_ref, prev_ref, next_ref, d_ref, dnext_ref, w_ref, b_ref, dx_ref, dw_ref, db_ref):
        i = pl.program_id(1)

        @pl.when(i == 0)
        def _():
            dw_ref[...] = jnp.zeros_like(dw_ref)
            db_ref[...] = jnp.zeros_like(db_ref)

        prev = jnp.where(i > 0, prev_ref[...], 0.0)
        ext = jnp.concatenate([prev, x_ref[...], next_ref[...]], axis=0)
        shifted = [pltpu.roll(ext, CONV_TAPS - 1 - tap, axis=0)[HALO:, :] for tap in range(CONV_TAPS - 1)]
        shifted.append(ext[HALO:, :])
        pre = b_ref[...]
        for tap in range(CONV_TAPS):
            pre = pre + w_ref[tap:tap + 1, :] * shifted[tap]
        s = _sigmoid(pre)
        d_ext = jnp.concatenate([d_ref[...], jnp.where(i < n - 1, dnext_ref[...], 0.0)], axis=0)
        dpre = d_ext * _dsilu(pre, s)
        dx = w_ref[CONV_TAPS - 1:CONV_TAPS, :] * dpre[:tm, :]
        for tap in range(CONV_TAPS - 1):
            back = CONV_TAPS - 1 - tap
            dx = dx + w_ref[tap:tap + 1, :] * pltpu.roll(dpre, tm + HALO - back, axis=0)[:tm, :]
        dx_ref[...] = dx.astype(BF16)
        dp = dpre[:tm, :]
        db_ref[...] += _colsum(dp)
        for tap in range(CONV_TAPS):
            dw_ref[tap:tap + 1, :] += _colsum(dp * shifted[tap][:tm, :])

    return pl.pallas_call(
        body, name="conv_bwd", grid=(CONV_DIM // D, n),
        out_shape=[jax.ShapeDtypeStruct((t, CONV_DIM), BF16), jax.ShapeDtypeStruct((8, CONV_DIM), F32),
                   jax.ShapeDtypeStruct((1, CONV_DIM), F32)],
        in_specs=[pl.BlockSpec((tm, D), lambda cb, i: (i, CONV_BLOCK0 + cb)),
                  pl.BlockSpec((HALO, D), lambda cb, i: (jnp.maximum(i * r - 1, 0), CONV_BLOCK0 + cb)),
                  pl.BlockSpec((HALO, D), lambda cb, i: (jnp.minimum((i + 1) * r, last_halo), CONV_BLOCK0 + cb)),
                  pl.BlockSpec((tm, D), lambda cb, i: (i, cb)),
                  pl.BlockSpec((HALO, D), lambda cb, i: (jnp.minimum((i + 1) * r, last_halo), cb)),
                  pl.BlockSpec((CONV_TAPS, D), lambda cb, i: (0, cb)), pl.BlockSpec((1, D), lambda cb, i: (0, cb))],
        out_specs=[pl.BlockSpec((tm, D), lambda cb, i: (i, cb)), pl.BlockSpec((8, D), lambda cb, i: (0, cb)),
                   pl.BlockSpec((1, D), lambda cb, i: (0, cb))],
        compiler_params=_cparams(("parallel", "arbitrary")),
    )(proj, proj, proj, dxc, dxc, conv_w, conv_b)


Z_BLOCK0 = 8
DT_BLOCK0 = 88
B_BLOCK0 = 16
C_BLOCK0 = 20


def _head_expand():
    e = np.zeros((N_STATE, GROUP_W), np.float32)
    for hh in range(HEADS_PER_GROUP):
        e[hh, hh * HEAD_P:(hh + 1) * HEAD_P] = 1.0
    return jnp.asarray(e, BF16)


def _ssd_chunk_terms(dt, bias, alog, expand, tril_f, eye):
    dtb = dt + bias
    delta = jnp.maximum(dtb, 0.0) + jnp.log(1.0 + jnp.exp(-jnp.abs(dtb)))
    ea = jnp.exp(alog)
    a = -ea * delta
    acum = _sel(_nn, a, tril_f, 3, x_first=False)
    delta_e = _sel(_nn, delta, expand, 2)
    acum_e = _sel(_nn, acum, expand, 3)
    acum_t = _sel(_nt, acum, eye, 3, x_first=False)
    return dtb, delta, ea, a, acum, delta_e, acum_e, acum_t


def ssd_fwd(proj, xc, alog4, bias4, dskip4, wnorm, expand):
    t = proj.shape[0]
    tb = _tile(t, TOKEN_BLOCK)
    ncb = tb // SSD_CHUNK

    def body(xs_ref, b_ref, c_ref, dt_ref, z_ref, alog_ref, bias_ref, dsk_ref, wn_ref, e_ref, ob_ref, st_ref, state):
        @pl.when(pl.program_id(1) == 0)
        def _():
            state[...] = jnp.zeros_like(state)

        expand = e_ref[...]
        mask = _tri(SSD_CHUNK)
        tril_f = mask.astype(BF16)
        eye = (lax.broadcasted_iota(jnp.int32, (N_STATE, N_STATE), 0) ==
               lax.broadcasted_iota(jnp.int32, (N_STATE, N_STATE), 1)).astype(BF16)
        alog, bias = alog_ref[0], bias_ref[0]
        d_e = _sel(_nn, jnp.broadcast_to(dsk_ref[0], (8, N_STATE)), expand, 3)[0:1, :]
        wn = wn_ref[...]

        def chunk(c, carry):
            sl = pl.ds(pl.multiple_of(c * SSD_CHUNK, SSD_CHUNK), SSD_CHUNK)
            xs, bm, cm, dt, z = xs_ref[sl, :], b_ref[sl, :], c_ref[sl, :], dt_ref[sl, :], z_ref[sl, :]
            dtb, delta, ea, a, acum, delta_e, acum_e, acum_t = _ssd_chunk_terms(dt, bias, alog, expand, tril_f, eye)
            alast_e = acum_e[SSD_CHUNK - 1:SSD_CHUNK, :]
            xd = xs * delta_e
            xdb = xd.astype(BF16)
            cb_, bb_ = cm.astype(BF16), bm.astype(BF16)
            cbm = _nt(cb_, bb_)
            ys = []
            for hh in range(HEADS_PER_GROUP):
                lh = jnp.where(mask, jnp.exp(jnp.minimum(acum[:, hh:hh + 1] - acum_t[hh:hh + 1, :], 0.0)), 0.0)
                ys.append(_nn((cbm * lh).astype(BF16), xdb[:, hh * HEAD_P:(hh + 1) * HEAD_P]))
            st = state[...]
            st_ref[0, c] = st
            y = jnp.concatenate(ys, axis=1) + _nn(cb_, st.astype(BF16)) * jnp.exp(acum_e) + xs * d_e
            state[...] = st * jnp.exp(alast_e) + _tn(bb_, (xd * jnp.exp(alast_e - acum_e)).astype(BF16))
            yg = y * z * _sigmoid(z)
            ob_ref[sl, :] = (yg * lax.rsqrt(jnp.mean(yg * yg, axis=-1, keepdims=True) + RMS_EPS) * wn).astype(BF16)
            return carry

        lax.fori_loop(0, ncb, chunk, 0, unroll=min(CHUNK_UNROLL, ncb))

    small = pl.BlockSpec((1, 1, N_STATE), lambda g, j: (g, 0, 0))
    return pl.pallas_call(
        body, name="ssd_fwd", grid=(N_GROUPS, t // tb),
        out_shape=[jax.ShapeDtypeStruct((t, B_INNER), BF16),
                   jax.ShapeDtypeStruct((N_GROUPS, t // SSD_CHUNK, N_STATE, GROUP_W), F32)],
        in_specs=[pl.BlockSpec((tb, GROUP_W), lambda g, j: (j, g)),
                  pl.BlockSpec((tb, N_STATE), lambda g, j: (j, B_BLOCK0 + g)),
                  pl.BlockSpec((tb, N_STATE), lambda g, j: (j, C_BLOCK0 + g)),
                  pl.BlockSpec((tb, N_STATE), lambda g, j: (j, DT_BLOCK0 + g)),
                  pl.BlockSpec((tb, GROUP_W), lambda g, j: (j, Z_BLOCK0 + g)),
                  small, small, small, pl.BlockSpec((1, GROUP_W), lambda g, j: (0, g)),
                  pl.BlockSpec((N_STATE, GROUP_W), lambda g, j: (0, 0))],
        out_specs=[pl.BlockSpec((tb, GROUP_W), lambda g, j: (j, g)),
                   pl.BlockSpec((1, ncb, N_STATE, GROUP_W), lambda g, j: (g, j, 0, 0))],
        scratch_shapes=[pltpu.VMEM((N_STATE, GROUP_W), F32)],
        compiler_params=_cparams(("parallel", "arbitrary")),
    )(xc, xc, xc, proj, proj, alog4, bias4, dskip4, wnorm, expand)


def ssd_bwd(proj, xc, alog4, bias4, dskip4, wnorm, expand, dob, states):
    t = proj.shape[0]
    tb = _tile(t, TOKEN_BLOCK)
    ncb = tb // SSD_CHUNK
    nb = t // tb

    def body(xs_ref, b_ref, c_ref, dt_ref, z_ref, alog_ref, bias_ref, dsk_ref, wn_ref, e_ref, dob_ref, st_ref,
             dxs_ref, db_ref, dc_ref, dz_ref, ddt_ref, dwn_ref, dalog_ref, dbias_ref, ddsk_ref, dstate):
        @pl.when(pl.program_id(1) == 0)
        def _():
            dstate[...] = jnp.zeros_like(dstate)
            dwn_ref[...] = jnp.zeros_like(dwn_ref)
            dalog_ref[...] = jnp.zeros_like(dalog_ref)
            dbias_ref[...] = jnp.zeros_like(dbias_ref)
            ddsk_ref[...] = jnp.zeros_like(ddsk_ref)

        expand = e_ref[...]
        mask = _tri(SSD_CHUNK)
        mask_t = _tri(SSD_CHUNK, upper=True)
        tril_f = mask.astype(BF16)
        triu_f = mask_t.astype(BF16)
        eye = (lax.broadcasted_iota(jnp.int32, (N_STATE, N_STATE), 0) ==
               lax.broadcasted_iota(jnp.int32, (N_STATE, N_STATE), 1)).astype(BF16)
        alog, bias = alog_ref[0], bias_ref[0]
        d_e = _sel(_nn, jnp.broadcast_to(dsk_ref[0], (8, N_STATE)), expand, 3)[0:1, :]
        wn = wn_ref[...]

        def chunk(i, c0):
            c = ncb - 1 - i
            sl = pl.ds(pl.multiple_of(c * SSD_CHUNK, SSD_CHUNK), SSD_CHUNK)
            xs, bm, cm, dt, z = xs_ref[sl, :], b_ref[sl, :], c_ref[sl, :], dt_ref[sl, :], z_ref[sl, :]
            dtb, delta, ea, a, acum, delta_e, acum_e, acum_t = _ssd_chunk_terms(dt, bias, alog, expand, tril_f, eye)
            alast_e = acum_e[SSD_CHUNK - 1:SSD_CHUNK, :]
            eacum = jnp.exp(acum_e)
            wl = jnp.exp(alast_e - acum_e)
            xd = xs * delta_e
            xdb = xd.astype(BF16)
            cb_, bb_ = cm.astype(BF16), bm.astype(BF16)
            cbm = _nt(cb_, bb_)
            cbm_t = _nt(bb_, cb_)
            st32 = st_ref[0, c]
            stb = st32.astype(BF16)
            dst = dstate[...]
            dstb = dst.astype(BF16)
            lhs, lhts, ys = [], [], []
            for hh in range(HEADS_PER_GROUP):
                col, row = acum[:, hh:hh + 1], acum_t[hh:hh + 1, :]
                lh = jnp.where(mask, jnp.exp(jnp.minimum(col - row, 0.0)), 0.0)
                lht = jnp.where(mask_t, jnp.exp(jnp.minimum(row - col, 0.0)), 0.0)
                lhs.append(lh)
                lhts.append(lht)
                ys.append(_nn((cbm * lh).astype(BF16), xdb[:, hh * HEAD_P:(hh + 1) * HEAD_P]))
            y_in = jnp.concatenate(ys, axis=1)
            y_out = _nn(cb_, stb) * eacum
            y = y_in + y_out + xs * d_e
            sgz = _sigmoid(z)
            sz = z * sgz
            yg = y * sz
            rstd = lax.rsqrt(jnp.mean(yg * yg, axis=-1, keepdims=True) + RMS_EPS)
            nrm = yg * rstd
            dob_v = dob_ref[sl, :]
            dn = dob_v * wn
            dwn_ref[...] += _colsum(dob_v * nrm)
            dyg = rstd * (dn - nrm * jnp.mean(dn * nrm, axis=-1, keepdims=True))
            dy = dyg * sz
            dz_ref[sl, :] = (dyg * y * _dsilu(z, sgz)).astype(BF16)
            dyb = dy.astype(BF16)
            dxds = []
            dcb = jnp.zeros((SSD_CHUNK, SSD_CHUNK), F32)
            dcb_t = jnp.zeros((SSD_CHUNK, SSD_CHUNK), F32)
            for hh in range(HEADS_PER_GROUP):
                hs = slice(hh * HEAD_P, (hh + 1) * HEAD_P)
                dy_h, x_h = dyb[:, hs], xdb[:, hs]
                dxds.append(_nn((cbm_t * lhts[hh]).astype(BF16), dy_h))
                dcb = dcb + _nt(dy_h, x_h) * lhs[hh]
                dcb_t = dcb_t + _nt(x_h, dy_h) * lhts[hh]
            dye = (dy * eacum).astype(BF16)
            xw = (xd * wl).astype(BF16)
            dxd_in = jnp.concatenate(dxds, axis=1)
            dxd_out = wl * _nn(bb_, dstb)
            dxd = dxd_in + dxd_out
            dc_ref[sl, :] = _nn(dcb.astype(BF16), bb_) + _nt(dye, stb)
            db_ref[sl, :] = _nn(dcb_t.astype(BF16), cb_) + _nt(xw, dstb)
            dstate[...] = dst * jnp.exp(alast_e) + _tn(cb_, dye)
            col_out = xd * dxd_out
            dac = _sel(_nt, dyb.astype(F32) * y_in - xdb.astype(F32) * dxd_in + dy * y_out - col_out, expand, 3)
            beyond = _colsum(col_out) + jnp.exp(alast_e) * _colsum(dst * st32)
            da = (_sel(_nn, dac, triu_f, 3, x_first=False) +
                  _sel(_nt, jnp.broadcast_to(beyond, (8, GROUP_W)), expand, 3)[0:1, :])
            ddelta = _sel(_nt, dxd * xs, expand, 2) - da * ea
            dalog_ref[0] += _colsum(da * a)
            ddtb = ddelta * _sigmoid(dtb)
            dbias_ref[0] += _colsum(ddtb)
            ddt_ref[sl, :] = ddtb.astype(BF16)
            ddsk_ref[0] += _sel(_nt, jnp.broadcast_to(_colsum(dy * xs), (8, GROUP_W)), expand, 3)[0:1, :]
            dxs_ref[sl, :] = dxd * delta_e + dy * d_e
            return c0

        lax.fori_loop(0, ncb, chunk, 0, unroll=min(CHUNK_UNROLL, ncb))

    small = pl.BlockSpec((1, 1, N_STATE), lambda g, j: (g, 0, 0))
    wide = pl.BlockSpec((tb, GROUP_W), lambda g, j: (nb - 1 - j, g))
    narrow = pl.BlockSpec((tb, N_STATE), lambda g, j: (nb - 1 - j, g))
    return pl.pallas_call(
        body, name="ssd_bwd", grid=(N_GROUPS, nb),
        out_shape=[jax.ShapeDtypeStruct((t, B_INNER), F32), jax.ShapeDtypeStruct((t, GROUP_W), F32),
                   jax.ShapeDtypeStruct((t, GROUP_W), F32), jax.ShapeDtypeStruct((t, B_INNER), BF16),
                   jax.ShapeDtypeStruct((t, GROUP_W), BF16), jax.ShapeDtypeStruct((1, B_INNER), F32),
                   jax.ShapeDtypeStruct((N_GROUPS, 1, N_STATE), F32), jax.ShapeDtypeStruct((N_GROUPS, 1, N_STATE), F32),
                   jax.ShapeDtypeStruct((N_GROUPS, 1, N_STATE), F32)],
        in_specs=[wide,
                  pl.BlockSpec((tb, N_STATE), lambda g, j: (nb - 1 - j, B_BLOCK0 + g)),
                  pl.BlockSpec((tb, N_STATE), lambda g, j: (nb - 1 - j, C_BLOCK0 + g)),
                  pl.BlockSpec((tb, N_STATE), lambda g, j: (nb - 1 - j, DT_BLOCK0 + g)),
                  pl.BlockSpec((tb, GROUP_W), lambda g, j: (nb - 1 - j, Z_BLOCK0 + g)),
                  small, small, small, pl.BlockSpec((1, GROUP_W), lambda g, j: (0, g)),
                  pl.BlockSpec((N_STATE, GROUP_W), lambda g, j: (0, 0)), wide,
                  pl.BlockSpec((1, ncb, N_STATE, GROUP_W), lambda g, j: (g, nb - 1 - j, 0, 0))],
        out_specs=[wide, narrow, narrow, wide, narrow, pl.BlockSpec((1, GROUP_W), lambda g, j: (0, g)),
                   small, small, small],
        scratch_shapes=[pltpu.VMEM((N_STATE, GROUP_W), F32)],
        compiler_params=_cparams(("parallel", "arbitrary")),
    )(xc, xc, xc, proj, proj, alog4, bias4, dskip4, wnorm, expand, dob, states)


def lower_bound_fwd(hgrn_lb):
    def body(a_ref, o_ref):
        a0, a1 = a_ref[0:1, :], a_ref[1:2, :]
        m = jnp.maximum(a0, a1)
        e0, e1 = jnp.exp(a0 - m), jnp.exp(a1 - m)
        o_ref[...] = e0 / (e0 + e1)

    return pl.pallas_call(body, name="lower_bound_fwd", out_shape=jax.ShapeDtypeStruct((1, D), F32))(hgrn_lb)


def ada_weight_grad(c_all, dmod_cols):
    def body(c_ref, d_ref, o_ref):
        cval = c_ref[...]
        o_ref[...] = _tn(cval * _sigmoid(cval), d_ref[...], HI)

    return pl.pallas_call(body, name="ada_weight_grad",
                          out_shape=jax.ShapeDtypeStruct((D, dmod_cols.shape[1]), F32))(c_all, dmod_cols)


def reduce_small(gathered, hgrn_lb, dlb_off):
    n = gathered.shape[2]

    def body(g_ref, a_ref, o_ref, glb_ref):
        s = g_ref[0]
        for d in range(1, N_DEV):
            s = s + g_ref[d]
        o_ref[...] = s
        a0, a1 = a_ref[0:1, :], a_ref[1:2, :]
        m = jnp.maximum(a0, a1)
        e0, e1 = jnp.exp(a0 - m), jnp.exp(a1 - m)
        p0 = e0 / (e0 + e1)
        tq = s[:, dlb_off:dlb_off + D] * p0 * (1.0 - p0)
        glb_ref[0:1, :] = tq
        glb_ref[1:2, :] = -tq

    return pl.pallas_call(body, name="reduce_small",
                          out_shape=[jax.ShapeDtypeStruct((1, n), F32), jax.ShapeDtypeStruct((2, D), F32)])(gathered, hgrn_lb)


def _adam_math(w, g, m, v):
    m2 = ADAM_B1 * m + (1.0 - ADAM_B1) * g
    v2 = ADAM_B2 * v + (1.0 - ADAM_B2) * (g * g)
    m_hat = m2 / (1.0 - ADAM_B1 ** ADAM_STEP)
    v_hat = v2 / (1.0 - ADAM_B2 ** ADAM_STEP)
    delta = -ADAM_LR * (m_hat / (jnp.sqrt(v_hat) + ADAM_EPS) + ADAM_WD * w)
    return delta, m2, v2


def _row_tile(rows, mult=8, cap=128):
    for cand in range(cap - cap % mult, 0, -mult):
        if rows % cand == 0:
            return cand
    return rows


def sum_parts(parts, name):
    n, rows, cols = parts.shape
    tr = _row_tile(rows, 16, 256)

    def body(p_ref, o_ref):
        s = p_ref[0].astype(F32)
        for d in range(1, n):
            s = s + p_ref[d].astype(F32)
        o_ref[...] = s

    return pl.pallas_call(
        body, name=name, grid=(rows // tr,),
        out_shape=jax.ShapeDtypeStruct((rows, cols), F32),
        in_specs=[pl.BlockSpec((n, tr, cols), lambda i: (0, i, 0))],
        out_specs=pl.BlockSpec((tr, cols), lambda i: (i, 0)),
        compiler_params=_cparams(("parallel",)),
    )(parts)


def sum_pair(a, b, name):
    rows, cols = a.shape
    tr = _row_tile(rows, 16, 256)

    def body(a_ref, b_ref, o_ref):
        o_ref[...] = (a_ref[...].astype(F32) + b_ref[...].astype(F32)).astype(o_ref.dtype)

    blk = pl.BlockSpec((tr, cols), lambda i: (i, 0))
    return pl.pallas_call(
        body, name=name, grid=(rows // tr,),
        out_shape=jax.ShapeDtypeStruct((rows, cols), a.dtype),
        in_specs=[blk, blk], out_specs=blk,
        compiler_params=_cparams(("parallel",)),
    )(a, b)


def adamw(w, g, m, v, name):
    rows, cols = w.shape
    tr = _row_tile(rows)

    def body(w_ref, g_ref, m_ref, v_ref, d_ref, m2_ref, v2_ref):
        delta, m2, v2 = _adam_math(w_ref[...], g_ref[...], m_ref[...], v_ref[...])
        d_ref[...] = delta
        m2_ref[...] = m2
        v2_ref[...] = v2

    blk = pl.BlockSpec((tr, cols), lambda i: (i, 0))
    return pl.pallas_call(
        body, name=name, grid=(rows // tr,),
        out_shape=[jax.ShapeDtypeStruct((rows, cols), F32)] * 3,
        in_specs=[blk] * 4, out_specs=[blk] * 3,
        compiler_params=_cparams(("parallel",)),
    )(w, g, m, v)


def _pad128(n):
    return -(-n // 128) * 128


def _pack(arrays):
    offs, parts, off = [], [], 0
    for a in arrays:
        flat = a.reshape(1, -1)
        n = flat.shape[1]
        offs.append(off)
        parts.append(jnp.pad(flat, ((0, 0), (0, _pad128(n) - n))))
        off += _pad128(n)
    return jnp.concatenate(parts, axis=1), offs


def _unpack(vec, offs, shapes):
    out = []
    for off, shp in zip(offs, shapes):
        n = int(np.prod(shp))
        out.append(vec[0, off:off + n].reshape(shp))
    return out


def _permute_in_rows(w_t):
    dt = w_t[9216:9248].reshape(N_GROUPS, HEADS_PER_GROUP, D)
    dt = jnp.pad(dt, ((0, 0), (0, N_STATE - HEADS_PER_GROUP), (0, 0))).reshape(N_GROUPS * N_STATE, D)
    dt = jnp.pad(dt, ((0, D - N_GROUPS * N_STATE), (0, 0)))
    return jnp.concatenate([w_t[:9216], w_t[9248:], dt], axis=0)


def _unpermute_in_rows(g_t):
    dt = g_t[11264:11264 + N_GROUPS * N_STATE].reshape(N_GROUPS, N_STATE, D)[:, :HEADS_PER_GROUP].reshape(32, D)
    return jnp.concatenate([g_t[:9216], dt, g_t[9216:11264]], axis=0)


def kernel(x, c, w_ada, b_ada, w_in, hgrn_lb, hgrn_gnorm, ssm_conv_w, ssm_conv_b, ssm_dt_bias, ssm_a_log, ssm_d, ssm_norm, w_branch_a, w_branch_b, w_o, ln1_g, ln1_b, w_ffn_gate, w_ffn_up, w_ffn_down, ln2_g, ln2_b, loss_target, m_w_ada, m_b_ada, m_w_in, m_hgrn_lb, m_hgrn_gnorm, m_ssm_conv_w, m_ssm_conv_b, m_ssm_dt_bias, m_ssm_a_log, m_ssm_d, m_ssm_norm, m_w_branch_a, m_w_branch_b, m_w_o, m_ln1_g, m_ln1_b, m_w_ffn_gate, m_w_ffn_up, m_w_ffn_down, m_ln2_g, m_ln2_b, v_w_ada, v_b_ada, v_w_in, v_hgrn_lb, v_hgrn_gnorm, v_ssm_conv_w, v_ssm_conv_b, v_ssm_dt_bias, v_ssm_a_log, v_ssm_d, v_ssm_norm, v_w_branch_a, v_w_branch_b, v_w_o, v_ln1_g, v_ln1_b, v_w_ffn_gate, v_w_ffn_up, v_w_ffn_down, v_ln2_g, v_ln2_b):
    me = 4 * lax.axis_index("x") + 2 * lax.axis_index("y") + lax.axis_index("c")
    xt = x[0]
    tgt = loss_target[0]
    t = xt.shape[0]
    ada_cols = w_ada.shape[2]
    conv_cols = ssm_conv_w.shape[2]

    small_in, _ = _pack([c, ssm_conv_w[0]])
    small_all = allgather_vmem(small_in, "allgather_small_inputs")
    c_all = small_all[:, 0, :D]
    conv_w = small_all[:, 0, D:D + CONV_TAPS * conv_cols].reshape(N_DEV, CONV_TAPS, conv_cols)
    conv_w = conv_w.transpose(1, 0, 2).reshape(CONV_TAPS, CONV_DIM)
    mod = ada_modulation(c_all, w_ada[0], b_ada.reshape(N_DEV, 1, ada_cols))
    mod6 = mod.reshape(6, D)

    shards = [w_in[0].T, w_branch_a[0], w_branch_b[0], w_o[0], w_ffn_gate[0].T, w_ffn_up[0].T, w_ffn_down[0]]
    shard_rows = [s.shape[0] for s in shards]
    slot_rows = [-(-r // 16) * 16 for r in shard_rows]
    row_offs = [sum(slot_rows[:i]) for i in range(len(shards))]
    rows_pad = sum(slot_rows)
    stacked = jnp.concatenate([jnp.pad(s.astype(BF16), ((0, p - r), (0, 0)))
                               for s, r, p in zip(shards, shard_rows, slot_rows)], axis=0)
    gathered = allgather_hbm(stacked, "allgather_weights")
    g_in, g_ba, g_bb, g_o, g_fg, g_fu, g_fd = (gathered[:, o:o + r] for o, r in zip(row_offs, shard_rows))
    w_in_t = _permute_in_rows(g_in.reshape(IN_DIM, D))
    w_ba = g_ba.reshape(D, D)
    w_bb = g_bb.reshape(B_INNER, D)
    w_oo = g_o.reshape(D, D)
    ffpad = ((0, D_FF_PAD - D_FF), (0, 0))
    w_gu_t = jnp.concatenate([jnp.pad(g_fg.reshape(D_FF, D), ffpad), jnp.pad(g_fu.reshape(D_FF, D), ffpad)], axis=0)
    w_dn = jnp.pad(g_fd.reshape(D_FF, D), ffpad)

    lb = lower_bound_fwd(hgrn_lb)
    u1 = ln_modulate(xt, mod6, 0, 1, "ln_modulate_1")
    proj = mm_nt(u1, w_in_t, F32, "mm_in_proj")
    o_a, o_raw, st_a = hgrn_fwd(proj, lb, hgrn_gnorm)
    xc = conv_fwd(proj, conv_w, ssm_conv_b)
    pad3 = ((0, 0), (0, 0), (0, N_STATE - HEADS_PER_GROUP))
    alog4 = jnp.pad(ssm_a_log.reshape(N_GROUPS, 1, HEADS_PER_GROUP), pad3)
    bias4 = jnp.pad(ssm_dt_bias.reshape(N_GROUPS, 1, HEADS_PER_GROUP), pad3)
    dskip4 = jnp.pad(ssm_d.reshape(N_GROUPS, 1, HEADS_PER_GROUP), pad3)
    expand = _head_expand()
    o_b, st_b = ssd_fwd(proj, xc, alog4, bias4, dskip4, ssm_norm, expand)
    ya = mm_nn(o_a, w_ba, F32, "mm_branch_a")
    yb = mm_nn(o_b, w_bb, F32, "mm_branch_b")
    merged = merge_gates(ya, yb, proj)
    h1 = mm_nn(merged, w_oo, F32, "mm_out_proj")
    x1 = resid_ln(xt, h1, mod6, 2, ln1_g, ln1_b, "resid_ln_1")
    u2 = ln_modulate(x1, mod6, 3, 4, "ln_modulate_2")
    gu = mm_nt(u2, w_gu_t, F32, "mm_ffn_in")
    act = swiglu_act(gu)
    h2 = mm_nn(act, w_dn, F32, "mm_ffn_out")

    dh2, dx1_part, acc4 = resid_ln_bwd(x1, h2, mod6, 5, ln2_g, ln2_b, tgt, True, "resid_ln_2_bwd")
    g_dn = mm_tn(act, dh2, "mm_grad_ffn_down")
    dact = mm_nt(dh2, w_dn, F32, "mm_dact")
    dgu = swiglu_act_bwd(gu, dact)
    g_gu_t = mm_tn(dgu, u2, "mm_grad_ffn_in")
    du2 = mm_nn(dgu, w_gu_t, F32, "mm_du2")
    dx1, acc3 = ln_modulate_bwd(x1, du2, mod6, 4, dx1_part, "ln_modulate_2_bwd")
    dh1, dx_part, acc2 = resid_ln_bwd(xt, h1, mod6, 2, ln1_g, ln1_b, dx1, False, "resid_ln_1_bwd")
    g_o = mm_tn(merged, dh1, "mm_grad_out_proj")
    dmerged = mm_nt(dh1, w_oo, F32, "mm_dmerged")
    dya, dyb, dga, dgb = merge_gates_bwd(dmerged, ya, yb, proj)
    g_ba_full = mm_tn(o_a, dya, "mm_grad_branch_a")
    g_bb_full = mm_tn(o_b, dyb, "mm_grad_branch_b")
    doa = mm_nt(dya, w_ba, F32, "mm_doa")
    dob = mm_nt(dyb, w_bb, F32, "mm_dob")
    dq, dfl, di, dg, dlb, dgn = hgrn_bwd(proj, lb, hgrn_gnorm, o_raw, doa, st_a)
    dxs, dbm, dcm, dz, ddt, dwn, dalog, dbias, ddsk = ssd_bwd(proj, xc, alog4, bias4, dskip4, ssm_norm, expand, dob, st_b)
    dxc = jnp.concatenate([dxs, dbm, dcm], axis=1)
    dxbc, dcw, dcb = conv_bwd(proj, dxc, conv_w, ssm_conv_b)
    dproj = jnp.concatenate([dq, dfl, di, dg, dz, dxbc, dga, dgb, ddt, jnp.zeros((t, D - N_GROUPS * N_STATE), BF16)], axis=1)
    g_in_t = mm_tn(dproj, u1, "mm_grad_in_proj")
    du1 = mm_nn(dproj, w_in_t, F32, "mm_du1")
    dx, acc1 = ln_modulate_bwd(xt, du1, mod6, 1, dx_part, "ln_modulate_1_bwd")

    blocks = [_unpermute_in_rows(g_in_t), g_ba_full, g_bb_full, g_o, g_gu_t[:D_FF], g_gu_t[D_FF_PAD:D_FF_PAD + D_FF],
              g_dn[:D_FF]]
    contrib = jnp.concatenate([jnp.pad(b.reshape(N_DEV, -1, D), ((0, 0), (0, p - r), (0, 0)))
                               for b, r, p in zip(blocks, shard_rows, slot_rows)], axis=1)
    by_core = contrib.reshape(N_DEV // 2, 2, rows_pad, D).transpose(1, 0, 2, 3)
    my_core = lax.axis_index("c")
    keep = lax.dynamic_index_in_dim(by_core, my_core, 0, keepdims=False)
    give = lax.dynamic_index_in_dim(by_core, 1 - my_core, 0, keepdims=False)
    got = exchange_sibling(give, "exchange_grads_sibling")
    chip_part = sum_pair(keep.reshape(-1, D), got.reshape(-1, D), "sum_grads_chip").reshape(keep.shape)
    parts = exchange_chips(chip_part, "exchange_grads_chips")
    g_rows = sum_parts(parts, "sum_grads_all")
    gw_in, gw_ba, gw_bb, gw_o, gw_fg, gw_fu, gw_fd = (g_rows[o:o + r] for o, r in zip(row_offs, shard_rows))
    gw_in, gw_fg, gw_fu = gw_in.T, gw_fg.T, gw_fu.T

    dmod = jnp.concatenate([acc1[1:2], acc1[0:1], acc2[0:1], acc3[1:2], acc3[0:1], acc4[0:1]], axis=1)
    small_fields = [dmod, acc4[3:4, :128], dlb, dgn, dcw[:CONV_TAPS], dcb, dbias, dalog, ddsk, dwn,
                    acc2[1:2], acc2[2:3], acc4[1:2], acc4[2:3]]
    small_out, offs = _pack(small_fields)
    small_sum_in = allgather_vmem(small_out, "allgather_small_grads")
    gsum, g_lb = reduce_small(small_sum_in, hgrn_lb, offs[2])
    (g_bada, loss_row, _, g_gn, g_cw_full, g_cb, g_bias4, g_alog4, g_dsk4, g_wn, g_l1g, g_l1b, g_l2g, g_l2b) = _unpack(
        gsum, offs, [(1, 6 * D), (1, 128), (1, D), (1, HK), (CONV_TAPS, CONV_DIM), (1, CONV_DIM),
                     (N_GROUPS, N_STATE), (N_GROUPS, N_STATE), (N_GROUPS, N_STATE), (1, B_INNER),
                     (1, D), (1, D), (1, D), (1, D)])
    loss = loss_row[0, 0]
    g_cw = lax.dynamic_slice(g_cw_full, (0, me * conv_cols), (CONV_TAPS, conv_cols))[None]
    g_dtb = g_bias4[:, :HEADS_PER_GROUP].reshape(1, 32)
    g_alog = g_alog4[:, :HEADS_PER_GROUP].reshape(1, 32)
    g_dsk = g_dsk4[:, :HEADS_PER_GROUP].reshape(1, 32)

    dmod_all = small_sum_in[:, 0, offs[0]:offs[0] + 6 * D]
    dmod_cols = lax.dynamic_slice(dmod_all, (0, me * ada_cols), (N_DEV, ada_cols))
    gw_ada = ada_weight_grad(c_all, dmod_cols)

    big = [("ada", w_ada[0], gw_ada, m_w_ada[0], v_w_ada[0]), ("in", w_in[0], gw_in, m_w_in[0], v_w_in[0]),
           ("branch_a", w_branch_a[0], gw_ba, m_w_branch_a[0], v_w_branch_a[0]),
           ("branch_b", w_branch_b[0], gw_bb, m_w_branch_b[0], v_w_branch_b[0]),
           ("o", w_o[0], gw_o, m_w_o[0], v_w_o[0]),
           ("ffn_gate", w_ffn_gate[0], gw_fg, m_w_ffn_gate[0], v_w_ffn_gate[0]),
           ("ffn_up", w_ffn_up[0], gw_fu, m_w_ffn_up[0], v_w_ffn_up[0]),
           ("ffn_down", w_ffn_down[0], gw_fd, m_w_ffn_down[0], v_w_ffn_down[0])]
    big_out = {}
    for nm, w_, g_, m_, v_ in big:
        d_, m2_, v2_ = adamw(w_, g_, m_, v_, "adamw_" + nm)
        big_out[nm] = (g_[None], d_[None], m2_[None], v2_[None])

    small_w = [b_ada, hgrn_lb, hgrn_gnorm, ssm_conv_w, ssm_conv_b, ssm_dt_bias, ssm_a_log, ssm_d, ssm_norm,
               ln1_g, ln1_b, ln2_g, ln2_b]
    small_g = [g_bada, g_lb, g_gn, g_cw, g_cb, g_dtb, g_alog, g_dsk, g_wn, g_l1g, g_l1b, g_l2g, g_l2b]
    small_m = [m_b_ada, m_hgrn_lb, m_hgrn_gnorm, m_ssm_conv_w, m_ssm_conv_b, m_ssm_dt_bias, m_ssm_a_log, m_ssm_d,
               m_ssm_norm, m_ln1_g, m_ln1_b, m_ln2_g, m_ln2_b]
    small_v = [v_b_ada, v_hgrn_lb, v_hgrn_gnorm, v_ssm_conv_w, v_ssm_conv_b, v_ssm_dt_bias, v_ssm_a_log, v_ssm_d,
               v_ssm_norm, v_ln1_g, v_ln1_b, v_ln2_g, v_ln2_b]
    shapes = [a.shape for a in small_w]
    small_g = [g_.reshape(s) for g_, s in zip(small_g, shapes)]
    pw, poffs = _pack(small_w)
    pg, _ = _pack(small_g)
    pm, _ = _pack(small_m)
    pv, _ = _pack(small_v)
    pd, pm2, pv2 = adamw(pw, pg, pm, pv, "adamw_small")
    s_d, s_m, s_v = (_unpack(p, poffs, shapes) for p in (pd, pm2, pv2))
    (sn_bada, sn_lb, sn_gn, sn_cw, sn_cb, sn_dtb, sn_alog, sn_dsk, sn_wn, sn_l1g, sn_l1b, sn_l2g, sn_l2b) = range(13)

    def order(kind):
        sm = [small_g, s_d, s_m, s_v][kind]
        bg = lambda nm: big_out[nm][kind]
        return [bg("ada"), sm[sn_bada], bg("in"), sm[sn_lb], sm[sn_gn], sm[sn_cw], sm[sn_cb], sm[sn_dtb], sm[sn_alog],
                sm[sn_dsk], sm[sn_wn], bg("branch_a"), bg("branch_b"), bg("o"), sm[sn_l1g], sm[sn_l1b],
                bg("ffn_gate"), bg("ffn_up"), bg("ffn_down"), sm[sn_l2g], sm[sn_l2b]]

    return (loss, dx[None], *order(0), *order(1), *order(2), *order(3))
```

```python
import functools

import numpy as np
import jax
import jax.numpy as jnp
from jax import lax
from jax.experimental import pallas as pl
from jax.experimental.pallas import tpu as pltpu

F32 = jnp.float32
BF16 = jnp.bfloat16
HI = lax.Precision.HIGHEST

N_DEV = 8
D = 1024
N_HEADS_A = 8
HK = 128
CHUNK = 64
SSD_CHUNK = 128
N_GROUPS = 4
HEADS_PER_GROUP = 8
HEAD_P = 64
N_STATE = 128
GROUP_W = HEADS_PER_GROUP * HEAD_P
B_INNER = 2048
CONV_DIM = 3072
D_FF = 2816
D_FF_PAD = 3072
IN_DIM = 11296
N_PROJ = 12288
ALPHA = 2.0 ** 0.25
LN_EPS = 1e-5
RMS_EPS = 1e-6
Q_SCALE = 128 ** -0.5
EXP_CLIP = 80.0
ADAM_LR, ADAM_B1, ADAM_B2, ADAM_EPS, ADAM_WD, ADAM_STEP = 0.001, 0.9, 0.999, 1e-8, 0.01, 10
VMEM_LIMIT = 48 * 1024 * 1024
TOKEN_BLOCK = 512
ROW_TILE = 256
FFN_ROW_TILE = 128
MM_ROW_TILE = 1024
MM_TOKEN_TILE = 512
CHUNK_UNROLL = 8
MESH_ID = pl.DeviceIdType.MESH

NT_DIMS = (((1,), (1,)), ((), ()))
TN_DIMS = (((0,), (0,)), ((), ()))


def _cparams(sem=None):
    return pltpu.CompilerParams(dimension_semantics=sem, vmem_limit_bytes=VMEM_LIMIT)


def _sigmoid(x):
    return 1.0 / (1.0 + jnp.exp(-x))


def _dsilu(x, s):
    return s * (1.0 + x * (1.0 - s))


def _nt(a, b, precision=None):
    return lax.dot_general(a, b, NT_DIMS, precision=precision, preferred_element_type=F32)


def _tn(a, b, precision=None):
    return lax.dot_general(a, b, TN_DIMS, precision=precision, preferred_element_type=F32)


def _nn(a, b, precision=None):
    return jnp.dot(a, b, precision=precision, preferred_element_type=F32)


def _split(x, pieces):
    out = []
    for i in range(pieces):
        p = x.astype(BF16)
        out.append(p)
        if i + 1 < pieces:
            x = x - p.astype(F32)
    return out


def _sel(dot, x, sel01, pieces, x_first=True):
    acc = None
    for p in _split(x, pieces):
        term = dot(p, sel01) if x_first else dot(sel01, p)
        acc = term if acc is None else acc + term
    return acc


def _ln(x):
    mu = jnp.mean(x, axis=-1, keepdims=True)
    xc = x - mu
    rstd = lax.rsqrt(jnp.mean(xc * xc, axis=-1, keepdims=True) + LN_EPS)
    return xc * rstd, rstd


def _ln_bwd(dxh, xh, rstd):
    return rstd * (dxh - jnp.mean(dxh, axis=-1, keepdims=True) - xh * jnp.mean(dxh * xh, axis=-1, keepdims=True))


def _colsum(x):
    return jnp.sum(x, axis=0, keepdims=True)


def _tri(n, upper=False):
    r = lax.broadcasted_iota(jnp.int32, (n, n), 0)
    c = lax.broadcasted_iota(jnp.int32, (n, n), 1)
    return (c >= r) if upper else (r >= c)


def _my_pos():
    return lax.axis_index("x"), lax.axis_index("y"), lax.axis_index("c")


def _peer(pos, k):
    x, y, c = pos
    return (x ^ ((k >> 2) & 1), y ^ ((k >> 1) & 1), c ^ (k & 1))


def _flat(pos):
    return 4 * pos[0] + 2 * pos[1] + pos[2]


def allgather_vmem(v, name):
    n = v.shape[1]

    def body(v_ref, o_ref, send_sems, recv_sems, local_sem):
        me = _my_pos()
        mine = pltpu.make_async_copy(v_ref, o_ref.at[_flat(me)], local_sem)
        mine.start()
        sends = []
        for k in range(1, N_DEV):
            peer = _peer(me, k)
            cp = pltpu.make_async_remote_copy(v_ref, o_ref.at[_flat(me)], send_sems.at[k - 1], recv_sems.at[k - 1],
                                              device_id=peer, device_id_type=MESH_ID)
            cp.start()
            sends.append(cp)
        for k in range(1, N_DEV):
            peer = _peer(me, k)
            pltpu.make_async_remote_copy(v_ref, o_ref.at[_flat(peer)], send_sems.at[k - 1], recv_sems.at[k - 1],
                                         device_id=peer, device_id_type=MESH_ID).wait_recv()
        for cp in sends:
            cp.wait_send()
        mine.wait()

    return pl.pallas_call(
        body, name=name,
        out_shape=jax.ShapeDtypeStruct((N_DEV, 1, n), F32),
        in_specs=[pl.BlockSpec(memory_space=pltpu.VMEM)],
        out_specs=pl.BlockSpec(memory_space=pltpu.VMEM),
        scratch_shapes=[pltpu.SemaphoreType.DMA((N_DEV - 1,)), pltpu.SemaphoreType.DMA((N_DEV - 1,)),
                        pltpu.SemaphoreType.DMA],
        compiler_params=_cparams(),
    )(v)


def ada_modulation(c_all, w_ada_s, b_ada_r):
    ncol = w_ada_s.shape[1]

    def body(c_ref, w_ref, b_ref, o_ref, part_ref, send_sems, recv_sems):
        me = _my_pos()
        cval = c_ref[...]
        cond = cval * _sigmoid(cval)
        part = _nn(cond, w_ref[...], HI)
        for r in range(N_DEV):
            part_ref[r] = part[r:r + 1, :]
        sends = []
        for k in range(1, N_DEV):
            peer = _peer(me, k)
            cp = pltpu.make_async_remote_copy(part_ref.at[_flat(peer)], o_ref.at[_flat(me)], send_sems.at[k - 1],
                                              recv_sems.at[k - 1], device_id=peer, device_id_type=MESH_ID)
            cp.start()
            sends.append(cp)
        o_ref[_flat(me)] = part_ref[_flat(me)]
        for k in range(1, N_DEV):
            peer = _peer(me, k)
            pltpu.make_async_remote_copy(part_ref.at[_flat(peer)], o_ref.at[_flat(peer)], send_sems.at[k - 1],
                                         recv_sems.at[k - 1], device_id=peer, device_id_type=MESH_ID).wait_recv()
        for cp in sends:
            cp.wait_send()
        o_ref[...] = o_ref[...] + b_ref[...]

    return pl.pallas_call(
        body, name="ada_modulation",
        out_shape=jax.ShapeDtypeStruct((N_DEV, 1, ncol), F32),
        in_specs=[pl.BlockSpec(memory_space=pltpu.VMEM)] * 3,
        out_specs=pl.BlockSpec(memory_space=pltpu.VMEM),
        scratch_shapes=[pltpu.VMEM((N_DEV, 1, ncol), F32), pltpu.SemaphoreType.DMA((N_DEV - 1,)),
                        pltpu.SemaphoreType.DMA((N_DEV - 1,))],
        compiler_params=_cparams(),
    )(c_all, w_ada_s, b_ada_r)


def allgather_hbm(shard, name):
    def body(x_ref, out_ref, send_sems, recv_sems, local_sem):
        x, y, c = _my_pos()
        me, sibling = (x, y, c), (x, y, 1 - c)
        chips = [(1 - x, y), (x, 1 - y), (1 - x, 1 - y)]

        def slot(pos):
            return out_ref.at[_flat(pos)]

        def copy(k, block, to, src=None):
            return pltpu.make_async_remote_copy(slot(block) if src is None else src, slot(block), send_sems.at[k],
                                                recv_sems.at[k], device_id=to, device_id_type=MESH_ID)

        mine = pltpu.make_async_copy(x_ref, slot(me), local_sem)
        mine.start()
        first = [copy(0, me, sibling, src=x_ref)]
        first += [copy(1 + j, me, (*chip, c), src=x_ref) for j, chip in enumerate(chips)]
        for cp in first:
            cp.start()
        passed = [copy(4 + j, (*chip, c), sibling) for j, chip in enumerate(chips)]
        for j, chip in enumerate(chips):
            copy(1 + j, (*chip, c), me).wait_recv()
            passed[j].start()
        copy(0, sibling, me).wait_recv()
        for j, chip in enumerate(chips):
            copy(4 + j, (*chip, 1 - c), me).wait_recv()
        for cp in first + passed:
            cp.wait_send()
        mine.wait()

    return pl.pallas_call(
        body, name=name,
        out_shape=jax.ShapeDtypeStruct((N_DEV,) + shard.shape, shard.dtype),
        in_specs=[pl.BlockSpec(memory_space=pl.ANY)],
        out_specs=pl.BlockSpec(memory_space=pl.ANY),
        scratch_shapes=[pltpu.SemaphoreType.DMA((N_DEV - 1,)), pltpu.SemaphoreType.DMA((N_DEV - 1,)),
                        pltpu.SemaphoreType.DMA],
        compiler_params=_cparams(),
    )(shard)


N_CHIP = N_DEV // 2
SIBLING_SEMS = [pltpu.SemaphoreType.DMA, pltpu.SemaphoreType.DMA]
CHIP_SEMS = [pltpu.SemaphoreType.DMA((N_CHIP - 1,)), pltpu.SemaphoreType.DMA((N_CHIP - 1,)), pltpu.SemaphoreType.DMA]


def _sibling_exchange(s_ref, o_ref, send_sem, recv_sem):
    x, y, c = _my_pos()
    cp = pltpu.make_async_remote_copy(s_ref, o_ref, send_sem, recv_sem, device_id=(x, y, 1 - c), device_id_type=MESH_ID)
    return cp.start, cp.wait


def _chip_exchange(p_ref, o_ref, send_sems, recv_sems, local_sem):
    x, y, c = _my_pos()
    my_chip = 2 * x + y
    mine = pltpu.make_async_copy(p_ref.at[my_chip], o_ref.at[my_chip], local_sem)
    peers = [(x ^ (k >> 1), y ^ (k & 1)) for k in range(1, N_CHIP)]
    sends = [pltpu.make_async_remote_copy(p_ref.at[2 * px + py], o_ref.at[my_chip], send_sems.at[k], recv_sems.at[k],
                                          device_id=(px, py, c), device_id_type=MESH_ID)
             for k, (px, py) in enumerate(peers)]
    recvs = [pltpu.make_async_remote_copy(p_ref.at[2 * px + py], o_ref.at[2 * px + py], send_sems.at[k], recv_sems.at[k],
                                          device_id=(px, py, c), device_id_type=MESH_ID)
             for k, (px, py) in enumerate(peers)]

    def start():
        mine.start()
        for cp in sends:
            cp.start()

    def wait():
        for cp in recvs:
            cp.wait_recv()
        for cp in sends:
            cp.wait_send()
        mine.wait()

    return start, wait


def exchange_sibling(send, name):
    def body(s_ref, o_ref, send_sem, recv_sem):
        start, wait = _sibling_exchange(s_ref, o_ref, send_sem, recv_sem)
        start()
        wait()

    return pl.pallas_call(
        body, name=name,
        out_shape=jax.ShapeDtypeStruct(send.shape, send.dtype),
        in_specs=[pl.BlockSpec(memory_space=pl.ANY)],
        out_specs=pl.BlockSpec(memory_space=pl.ANY),
        scratch_shapes=SIBLING_SEMS,
        compiler_params=_cparams(),
    )(send)


def exchange_chips(part, name):
    def body(p_ref, o_ref, send_sems, recv_sems, local_sem):
        start, wait = _chip_exchange(p_ref, o_ref, send_sems, recv_sems, local_sem)
        start()
        wait()

    return pl.pallas_call(
        body, name=name,
        out_shape=jax.ShapeDtypeStruct(part.shape, part.dtype),
        in_specs=[pl.BlockSpec(memory_space=pl.ANY)],
        out_specs=pl.BlockSpec(memory_space=pl.ANY),
        scratch_shapes=CHIP_SEMS,
        compiler_params=_cparams(),
    )(part)


def mm_nn(a, b, out_dtype, name):
    m, kdim = a.shape
    n = b.shape[1]
    tm, tn, tk = min(MM_ROW_TILE, m), 1024, 1024
    nk = kdim // tk

    def body(a_ref, b_ref, o_ref, acc_ref):
        p = _nn(a_ref[...], b_ref[...])
        if nk == 1:
            o_ref[...] = p.astype(o_ref.dtype)
        else:
            k = pl.program_id(2)

            @pl.when(k == 0)
            def _():
                acc_ref[...] = p

            @pl.when(k > 0)
            def _():
                acc_ref[...] += p

            @pl.when(k == nk - 1)
            def _():
                o_ref[...] = acc_ref[...].astype(o_ref.dtype)

    return pl.pallas_call(
        body, name=name, grid=(n // tn, m // tm, nk),
        out_shape=jax.ShapeDtypeStruct((m, n), out_dtype),
        in_specs=[pl.BlockSpec((tm, tk), lambda j, i, k: (i, k)), pl.BlockSpec((tk, tn), lambda j, i, k: (k, j))],
        out_specs=pl.BlockSpec((tm, tn), lambda j, i, k: (i, j)),
        scratch_shapes=[pltpu.VMEM((tm, tn), F32)],
        compiler_params=_cparams(("parallel", "parallel", "arbitrary")),
    )(a, b)


def mm_nt(a, b, out_dtype, name):
    m, kdim = a.shape
    n = b.shape[0]
    tm, tn, tk = min(MM_ROW_TILE, m), 1024, 1024
    nk = kdim // tk

    def body(a_ref, b_ref, o_ref, acc_ref):
        p = _nt(a_ref[...], b_ref[...])
        if nk == 1:
            o_ref[...] = p.astype(o_ref.dtype)
        else:
            k = pl.program_id(2)

            @pl.when(k == 0)
            def _():
                acc_ref[...] = p

            @pl.when(k > 0)
            def _():
                acc_ref[...] += p

            @pl.when(k == nk - 1)
            def _():
                o_ref[...] = acc_ref[...].astype(o_ref.dtype)

    return pl.pallas_call(
        body, name=name, grid=(n // tn, m // tm, nk),
        out_shape=jax.ShapeDtypeStruct((m, n), out_dtype),
        in_specs=[pl.BlockSpec((tm, tk), lambda j, i, k: (i, k)), pl.BlockSpec((tn, tk), lambda j, i, k: (j, k))],
        out_specs=pl.BlockSpec((tm, tn), lambda j, i, k: (i, j)),
        scratch_shapes=[pltpu.VMEM((tm, tn), F32)],
        compiler_params=_cparams(("parallel", "parallel", "arbitrary")),
    )(a, b)


def mm_nt_gather(a, b, out_dtype, shard, name):
    m, kdim = a.shape
    n = b.shape[0]
    tm, tn = min(MM_ROW_TILE, m), 1024
    assert kdim == 1024
    gj = m // tm
    nsteps = (n // tn) * gj
    forward_step = max(nsteps - 3, 0)

    def body(a_ref, b_ref, x_ref, o_ref, g_ref, send_sems, recv_sems, local_sem):
        step = pl.program_id(0) * gj + pl.program_id(1)
        x, y, c = _my_pos()
        me, sibling = (x, y, c), (x, y, 1 - c)
        chips = [(1 - x, y), (x, 1 - y), (1 - x, 1 - y)]

        def slot(pos):
            return g_ref.at[_flat(pos)]

        def copy(k, block, to, src=None):
            return pltpu.make_async_remote_copy(slot(block) if src is None else src, slot(block), send_sems.at[k],
                                                recv_sems.at[k], device_id=to, device_id_type=MESH_ID)

        mine = pltpu.make_async_copy(x_ref, slot(me), local_sem)
        first = [copy(0, me, sibling, src=x_ref)]
        first += [copy(1 + j, me, (*chip, c), src=x_ref) for j, chip in enumerate(chips)]
        passed = [copy(4 + j, (*chip, c), sibling) for j, chip in enumerate(chips)]

        @pl.when(step == 0)
        def _():
            mine.start()
            for cp in first:
                cp.start()

        o_ref[...] = _nt(a_ref[...], b_ref[...]).astype(o_ref.dtype)

        @pl.when(step == forward_step)
        def _():
            for j, chip in enumerate(chips):
                copy(1 + j, (*chip, c), me).wait_recv()
                passed[j].start()

        @pl.when(step == nsteps - 1)
        def _():
            copy(0, sibling, me).wait_recv()
            for j, chip in enumerate(chips):
                copy(4 + j, (*chip, 1 - c), me).wait_recv()
            for cp in first + passed:
                cp.wait_send()
            mine.wait()

    return pl.pallas_call(
        body, name=name, grid=(n // tn, gj),
        out_shape=[jax.ShapeDtypeStruct((m, n), out_dtype), jax.ShapeDtypeStruct((N_DEV,) + shard.shape, shard.dtype)],
        in_specs=[pl.BlockSpec((tm, kdim), lambda j, i: (i, 0)), pl.BlockSpec((tn, kdim), lambda j, i: (j, 0)),
                  pl.BlockSpec(memory_space=pl.ANY)],
        out_specs=[pl.BlockSpec((tm, tn), lambda j, i: (i, j)), pl.BlockSpec(memory_space=pl.ANY)],
        scratch_shapes=[pltpu.SemaphoreType.DMA((N_DEV - 1,)), pltpu.SemaphoreType.DMA((N_DEV - 1,)),
                        pltpu.SemaphoreType.DMA],
        compiler_params=_cparams(("arbitrary", "arbitrary")),
    )(a, b, shard)


def mm_tn(a, b, name):
    t, ka = a.shape
    n = b.shape[1]
    tt, tka, tn = min(MM_TOKEN_TILE, t), 1024, 1024
    nt = t // tt

    def body(a_ref, b_ref, o_ref, acc_ref):
        p = _tn(a_ref[...], b_ref[...])
        s = pl.program_id(2)

        @pl.when(s == 0)
        def _():
            acc_ref[...] = p

        @pl.when(s > 0)
        def _():
            acc_ref[...] += p

        @pl.when(s == nt - 1)
        def _():
            o_ref[...] = acc_ref[...].astype(o_ref.dtype)

    return pl.pallas_call(
        body, name=name, grid=(ka // tka, n // tn, nt),
        out_shape=jax.ShapeDtypeStruct((ka, n), BF16),
        in_specs=[pl.BlockSpec((tt, tka), lambda i, j, s: (s, i)), pl.BlockSpec((tt, tn), lambda i, j, s: (s, j))],
        out_specs=pl.BlockSpec((tka, tn), lambda i, j, s: (i, j)),
        scratch_shapes=[pltpu.VMEM((tka, tn), F32)],
        compiler_params=_cparams(("parallel", "parallel", "arbitrary")),
    )(a, b)


def _tile(t, cap):
    return min(cap, t)


def ln_modulate(x, mod6, shift_row, scale_row, name):
    t = x.shape[0]
    tm = _tile(t, ROW_TILE)

    def body(x_ref, mod_ref, o_ref):
        xh, _ = _ln(x_ref[...])
        sc = mod_ref[scale_row:scale_row + 1, :]
        sh = mod_ref[shift_row:shift_row + 1, :]
        o_ref[...] = (xh * (1.0 + sc) + sh).astype(BF16)

    return pl.pallas_call(
        body, name=name, grid=(t // tm,),
        out_shape=jax.ShapeDtypeStruct((t, D), BF16),
        in_specs=[pl.BlockSpec((tm, D), lambda i: (i, 0)), pl.BlockSpec((6, D), lambda i: (0, 0))],
        out_specs=pl.BlockSpec((tm, D), lambda i: (i, 0)),
        compiler_params=_cparams(("parallel",)),
    )(x, mod6)


def resid_ln(x, h, mod6, gate_row, ln_g, ln_b, name):
    t = x.shape[0]
    tm = _tile(t, ROW_TILE)

    def body(x_ref, h_ref, mod_ref, g_ref, b_ref, o_ref):
        r = ALPHA * x_ref[...] + mod_ref[gate_row:gate_row + 1, :] * h_ref[...]
        rh, _ = _ln(r)
        o_ref[...] = rh * g_ref[...] + b_ref[...]

    row = pl.BlockSpec((tm, D), lambda i: (i, 0))
    vec = pl.BlockSpec((1, D), lambda i: (0, 0))
    return pl.pallas_call(
        body, name=name, grid=(t // tm,),
        out_shape=jax.ShapeDtypeStruct((t, D), F32),
        in_specs=[row, row, pl.BlockSpec((6, D), lambda i: (0, 0)), vec, vec],
        out_specs=row,
        compiler_params=_cparams(("parallel",)),
    )(x, h, mod6, ln_g, ln_b)


def resid_ln_bwd(x, h, mod6, gate_row, ln_g, ln_b, cot, with_loss, name):
    t = x.shape[0]
    tm = _tile(t, ROW_TILE)

    def body(x_ref, h_ref, mod_ref, g_ref, b_ref, c_ref, dh_ref, dx_ref, acc_ref):
        @pl.when(pl.program_id(0) == 0)
        def _():
            acc_ref[...] = jnp.zeros_like(acc_ref)

        gate = mod_ref[gate_row:gate_row + 1, :]
        hv = h_ref[...]
        r = ALPHA * x_ref[...] + gate * hv
        rh, rstd = _ln(r)
        lng = g_ref[...]
        if with_loss:
            diff = rh * lng + b_ref[...] - c_ref[...]
            dxo = diff * (1.0 / D)
            lsum = jnp.sum(_colsum(diff * diff), axis=-1, keepdims=True) * (0.5 / D)
            acc_ref[3:4, :] += jnp.broadcast_to(lsum, (1, D))
        else:
            dxo = c_ref[...]
        acc_ref[1:2, :] += _colsum(dxo * rh)
        acc_ref[2:3, :] += _colsum(dxo)
        dr = _ln_bwd(dxo * lng, rh, rstd)
        acc_ref[0:1, :] += _colsum(dr * hv)
        dh_ref[...] = (gate * dr).astype(BF16)
        dx_ref[...] = ALPHA * dr

    row = pl.BlockSpec((tm, D), lambda i: (i, 0))
    vec = pl.BlockSpec((1, D), lambda i: (0, 0))
    return pl.pallas_call(
        body, name=name, grid=(t // tm,),
        out_shape=[jax.ShapeDtypeStruct((t, D), BF16), jax.ShapeDtypeStruct((t, D), F32),
                   jax.ShapeDtypeStruct((8, D), F32)],
        in_specs=[row, row, pl.BlockSpec((6, D), lambda i: (0, 0)), vec, vec, row],
        out_specs=[row, row, pl.BlockSpec((8, D), lambda i: (0, 0))],
        compiler_params=_cparams(("arbitrary",)),
    )(x, h, mod6, ln_g, ln_b, cot)


def ln_modulate_bwd(x, du, mod6, scale_row, dx_part, name):
    t = x.shape[0]
    tm = _tile(t, ROW_TILE)

    def body(x_ref, du_ref, mod_ref, dp_ref, dx_ref, acc_ref):
        @pl.when(pl.program_id(0) == 0)
        def _():
            acc_ref[...] = jnp.zeros_like(acc_ref)

        xh, rstd = _ln(x_ref[...])
        du_v = du_ref[...]
        sc = mod_ref[scale_row:scale_row + 1, :]
        acc_ref[0:1, :] += _colsum(du_v * xh)
        acc_ref[1:2, :] += _colsum(du_v)
        dx_ref[...] = dp_ref[...] + _ln_bwd(du_v * (1.0 + sc), xh, rstd)

    row = pl.BlockSpec((tm, D), lambda i: (i, 0))
    return pl.pallas_call(
        body, name=name, grid=(t // tm,),
        out_shape=[jax.ShapeDtypeStruct((t, D), F32), jax.ShapeDtypeStruct((8, D), F32)],
        in_specs=[row, row, pl.BlockSpec((6, D), lambda i: (0, 0)), row],
        out_specs=[row, pl.BlockSpec((8, D), lambda i: (0, 0))],
        compiler_params=_cparams(("arbitrary",)),
    )(x, du, mod6, dx_part)


def merge_gates(ya, yb, proj):
    t = ya.shape[0]
    tm = _tile(t, ROW_TILE)

    def body(ya_ref, yb_ref, ga_ref, gb_ref, o_ref):
        o_ref[...] = (_sigmoid(ga_ref[...]) * ya_ref[...] + _sigmoid(gb_ref[...]) * yb_ref[...]).astype(BF16)

    row = pl.BlockSpec((tm, D), lambda i: (i, 0))
    return pl.pallas_call(
        body, name="merge_gates", grid=(t // tm,),
        out_shape=jax.ShapeDtypeStruct((t, D), BF16),
        in_specs=[row, row, pl.BlockSpec((tm, D), lambda i: (i, 9)), pl.BlockSpec((tm, D), lambda i: (i, 10))],
        out_specs=row,
        compiler_params=_cparams(("parallel",)),
    )(ya, yb, proj, proj)


def merge_gates_bwd(dm, ya, yb, proj):
    t = ya.shape[0]
    tm = _tile(t, ROW_TILE)

    def body(dm_ref, ya_ref, yb_ref, ga_ref, gb_ref, dya_ref, dyb_ref, dga_ref, dgb_ref):
        dmv = dm_ref[...]
        sa = _sigmoid(ga_ref[...])
        sb = _sigmoid(gb_ref[...])
        dya_ref[...] = (dmv * sa).astype(BF16)
        dyb_ref[...] = (dmv * sb).astype(BF16)
        dga_ref[...] = (dmv * ya_ref[...] * sa * (1.0 - sa)).astype(BF16)
        dgb_ref[...] = (dmv * yb_ref[...] * sb * (1.0 - sb)).astype(BF16)

    row = pl.BlockSpec((tm, D), lambda i: (i, 0))
    return pl.pallas_call(
        body, name="merge_gates_bwd", grid=(t // tm,),
        out_shape=[jax.ShapeDtypeStruct((t, D), BF16)] * 4,
        in_specs=[row, row, row, pl.BlockSpec((tm, D), lambda i: (i, 9)), pl.BlockSpec((tm, D), lambda i: (i, 10))],
        out_specs=[row] * 4,
        compiler_params=_cparams(("parallel",)),
    )(dm, ya, yb, proj, proj)


def swiglu_act(gu):
    t = gu.shape[0]
    tm = _tile(t, FFN_ROW_TILE)

    def body(gu_ref, o_ref):
        for j in range(D_FF_PAD // D):
            g = gu_ref[:, j * D:(j + 1) * D]
            u = gu_ref[:, D_FF_PAD + j * D:D_FF_PAD + (j + 1) * D]
            o_ref[:, j * D:(j + 1) * D] = (g * _sigmoid(g) * u).astype(BF16)

    return pl.pallas_call(
        body, name="swiglu_act", grid=(t // tm,),
        out_shape=jax.ShapeDtypeStruct((t, D_FF_PAD), BF16),
        in_specs=[pl.BlockSpec((tm, 2 * D_FF_PAD), lambda i: (i, 0))],
        out_specs=pl.BlockSpec((tm, D_FF_PAD), lambda i: (i, 0)),
        compiler_params=_cparams(("parallel",)),
    )(gu)


def swiglu_act_bwd(gu, dact):
    t = gu.shape[0]
    tm = _tile(t, FFN_ROW_TILE)

    def body(gu_ref, da_ref, o_ref):
        for j in range(D_FF_PAD // D):
            g = gu_ref[:, j * D:(j + 1) * D]
            u = gu_ref[:, D_FF_PAD + j * D:D_FF_PAD + (j + 1) * D]
            da = da_ref[:, j * D:(j + 1) * D]
            s = _sigmoid(g)
            o_ref[:, j * D:(j + 1) * D] = (da * u * _dsilu(g, s)).astype(BF16)
            o_ref[:, D_FF_PAD + j * D:D_FF_PAD + (j + 1) * D] = (da * g * s).astype(BF16)

    return pl.pallas_call(
        body, name="swiglu_act_bwd", grid=(t // tm,),
        out_shape=jax.ShapeDtypeStruct((t, 2 * D_FF_PAD), BF16),
        in_specs=[pl.BlockSpec((tm, 2 * D_FF_PAD), lambda i: (i, 0)), pl.BlockSpec((tm, D_FF_PAD), lambda i: (i, 0))],
        out_specs=pl.BlockSpec((tm, 2 * D_FF_PAD), lambda i: (i, 0)),
        compiler_params=_cparams(("parallel",)),
    )(gu, dact)


def _hgrn_chunk_terms(q, fl, lbv, tril_f):
    sig = _sigmoid(fl)
    f = lbv + (1.0 - lbv) * sig
    lam = jnp.log(f)
    k = 1.0 - f
    sq = _sigmoid(q)
    qt = q * sq * Q_SCALE
    bc = _sel(_nn, lam, tril_f, 3, x_first=False)
    bmid = bc[CHUNK // 2 - 1:CHUNK // 2, :]
    bl = bc[CHUNK - 1:CHUNK, :]
    eq = jnp.exp(jnp.minimum(bc - bmid, EXP_CLIP))
    ek = jnp.exp(jnp.minimum(bmid - bc, EXP_CLIP))
    eb = jnp.exp(bc)
    ekl = jnp.exp(bl - bc)
    ebl = jnp.exp(bl)
    return sig, f, k, sq, qt, eq, ek, eb, ekl, ebl


def hgrn_fwd(proj, lb, gnorm):
    t = proj.shape[0]
    tb = _tile(t, TOKEN_BLOCK)
    ncb = tb // CHUNK

    def body(q_ref, f_ref, i_ref, g_ref, lb_ref, gn_ref, oa_ref, oraw_ref, st_ref, state):
        @pl.when(pl.program_id(1) == 0)
        def _():
            state[...] = jnp.zeros_like(state)

        lbv = lb_ref[...]
        gn = gn_ref[...]
        mask = _tri(CHUNK)
        tril_f = mask.astype(BF16)

        def chunk(c, carry):
            sl = pl.ds(pl.multiple_of(c * CHUNK, CHUNK), CHUNK)
            q, fl, v, g = q_ref[sl, :], f_ref[sl, :], i_ref[sl, :], g_ref[sl, :]
            sig, f, k, sq, qt, eq, ek, eb, ekl, ebl = _hgrn_chunk_terms(q, fl, lbv, tril_f)
            a = jnp.where(mask, _nt((qt * eq).astype(BF16), (k * ek).astype(BF16)), 0.0)
            st = state[...]
            st_ref[0, c] = st
            vb = v.astype(BF16)
            o = _nn(a.astype(BF16), vb) + _nt((qt * eb).astype(BF16), st.astype(BF16))
            state[...] = st * ebl + _tn(vb, (k * ekl).astype(BF16))
            oraw_ref[sl, :] = o
            rn = o * lax.rsqrt(jnp.mean(o * o, axis=-1, keepdims=True) + RMS_EPS)
            oa_ref[sl, :] = (rn * gn * g * _sigmoid(g)).astype(BF16)
            return carry

        lax.fori_loop(0, ncb, chunk, 0, unroll=min(CHUNK_UNROLL, ncb))

    def col(block):
        return pl.BlockSpec((tb, HK), lambda h, j: (j, block * N_HEADS_A + h))

    return pl.pallas_call(
        body, name="hgrn_fwd", grid=(N_HEADS_A, t // tb),
        out_shape=[jax.ShapeDtypeStruct((t, D), BF16), jax.ShapeDtypeStruct((t, D), F32),
                   jax.ShapeDtypeStruct((N_HEADS_A, t // CHUNK, HK, HK), F32)],
        in_specs=[col(0), col(1), col(2), col(3), pl.BlockSpec((1, HK), lambda h, j: (0, h)),
                  pl.BlockSpec((1, HK), lambda h, j: (0, 0))],
        out_specs=[pl.BlockSpec((tb, HK), lambda h, j: (j, h)), pl.BlockSpec((tb, HK), lambda h, j: (j, h)),
                   pl.BlockSpec((1, ncb, HK, HK), lambda h, j: (h, j, 0, 0))],
        scratch_shapes=[pltpu.VMEM((HK, HK), F32)],
        compiler_params=_cparams(("parallel", "arbitrary")),
    )(proj, proj, proj, proj, lb, gnorm)


def hgrn_bwd(proj, lb, gnorm, o_raw, doa, states, give):
    t = proj.shape[0]
    tb = _tile(t, TOKEN_BLOCK)
    ncb = tb // CHUNK
    nb = t // tb

    def body(q_ref, f_ref, i_ref, g_ref, lb_ref, gn_ref, oraw_ref, doa_ref, st_ref, give_ref,
             dq_ref, df_ref, di_ref, dg_ref, dlb_ref, dgn_ref, got_ref, dstate, send_sem, recv_sem):
        h, j = pl.program_id(0), pl.program_id(1)
        swap_start, swap_wait = _sibling_exchange(give_ref, got_ref, send_sem, recv_sem)

        @pl.when((h == 0) & (j == 0))
        def _():
            swap_start()

        @pl.when(j == 0)
        def _():
            dstate[...] = jnp.zeros_like(dstate)
            dlb_ref[...] = jnp.zeros_like(dlb_ref)

        @pl.when((j == 0) & (h == 0))
        def _():
            dgn_ref[...] = jnp.zeros_like(dgn_ref)

        lbv = lb_ref[...]
        gn = gn_ref[...]
        mask = _tri(CHUNK)
        mask_t = _tri(CHUNK, upper=True)
        tril_f = mask.astype(BF16)
        triu_f = mask_t.astype(BF16)

        def chunk(i, c0):
            c = ncb - 1 - i
            sl = pl.ds(pl.multiple_of(c * CHUNK, CHUNK), CHUNK)
            q, fl, v, g = q_ref[sl, :], f_ref[sl, :], i_ref[sl, :], g_ref[sl, :]
            sig, f, k, sq, qt, eq, ek, eb, ekl, ebl = _hgrn_chunk_terms(q, fl, lbv, tril_f)
            qe = (qt * eq).astype(BF16)
            ke = (k * ek).astype(BF16)
            st32 = st_ref[0, c]
            st = st32.astype(BF16)
            dst = dstate[...]
            dstb = dst.astype(BF16)
            o = oraw_ref[sl, :]
            rstd = lax.rsqrt(jnp.mean(o * o, axis=-1, keepdims=True) + RMS_EPS)
            rn = o * rstd
            sgm = _sigmoid(g)
            sg = g * sgm
            doa_v = doa_ref[sl, :]
            drn = doa_v * gn * sg
            dgn_ref[...] += _colsum(doa_v * rn * sg)
            dg_ref[sl, :] = (doa_v * rn * gn * _dsilu(g, sgm)).astype(BF16)
            do = rstd * (drn - rn * jnp.mean(drn * rn, axis=-1, keepdims=True))
            dob = do.astype(BF16)
            vb = v.astype(BF16)
            da = jnp.where(mask, _nt(dob, vb), 0.0).astype(BF16)
            da_t = jnp.where(mask_t, _nt(vb, dob), 0.0).astype(BF16)
            a_t = jnp.where(mask_t, _nt(ke, qe), 0.0).astype(BF16)
            kl = (k * ekl).astype(BF16)
            qb = (qt * eb).astype(BF16)
            dq_in = _nn(da, ke)
            dk_in = _nn(da_t, qe)
            dq_out = eb * _nn(dob, st)
            dk_out = ekl * _nn(vb, dstb)
            dqt = eq * dq_in + dq_out
            dk = ek * dk_in + dk_out
            dv = _nn(a_t, dob) + _nt(kl, dstb)
            dstate[...] = dst * ebl + _tn(dob, qb)
            dbig = qe.astype(F32) * dq_in - ke.astype(F32) * dk_in + qt * dq_out - k * dk_out
            beyond = _colsum(k * dk_out) + ebl * _colsum(dst * st32)
            dlam = _sel(_nn, dbig, triu_f, 3, x_first=False) + beyond
            df = dlam / f - dk
            df_ref[sl, :] = (df * (1.0 - lbv) * sig * (1.0 - sig)).astype(BF16)
            dlb_ref[...] += _colsum(df * (1.0 - sig))
            dq_ref[sl, :] = (dqt * Q_SCALE * _dsilu(q, sq)).astype(BF16)
            di_ref[sl, :] = dv.astype(BF16)
            return c0

        lax.fori_loop(0, ncb, chunk, 0, unroll=min(CHUNK_UNROLL, ncb))

        @pl.when((h == N_HEADS_A - 1) & (j == nb - 1))
        def _():
            swap_wait()

    def col(block):
        return pl.BlockSpec((tb, HK), lambda h, j: (nb - 1 - j, block * N_HEADS_A + h))

    hcol = pl.BlockSpec((tb, HK), lambda h, j: (nb - 1 - j, h))
    hbm = pl.BlockSpec(memory_space=pl.ANY)
    return pl.pallas_call(
        body, name="hgrn_bwd", grid=(N_HEADS_A, nb),
        out_shape=[jax.ShapeDtypeStruct((t, D), BF16)] * 4 + [jax.ShapeDtypeStruct((1, D), F32),
                                                                jax.ShapeDtypeStruct((1, HK), F32),
                                                                jax.ShapeDtypeStruct(give.shape, give.dtype)],
        in_specs=[col(0), col(1), col(2), col(3), pl.BlockSpec((1, HK), lambda h, j: (0, h)),
                  pl.BlockSpec((1, HK), lambda h, j: (0, 0)), hcol, hcol,
                  pl.BlockSpec((1, ncb, HK, HK), lambda h, j: (h, nb - 1 - j, 0, 0)), hbm],
        out_specs=[hcol] * 4 + [pl.BlockSpec((1, HK), lambda h, j: (0, h)), pl.BlockSpec((1, HK), lambda h, j: (0, 0)),
                                hbm],
        scratch_shapes=[pltpu.VMEM((HK, HK), F32)] + SIBLING_SEMS,
        compiler_params=_cparams(("arbitrary", "arbitrary")),
    )(proj, proj, proj, proj, lb, gnorm, o_raw, doa, states, give)


CONV_BLOCK0 = 6
CONV_TAPS = 4
HALO = 8


def conv_fwd(proj, conv_w, conv_b):
    t = proj.shape[0]
    tm = _tile(t, ROW_TILE)
    r = tm // HALO

    def body(x_ref, halo_ref, w_ref, b_ref, o_ref):
        i = pl.program_id(1)
        halo = jnp.where(i > 0, halo_ref[...], 0.0)
        ext = jnp.concatenate([halo, x_ref[...]], axis=0)
        pre = b_ref[...] + w_ref[CONV_TAPS - 1:CONV_TAPS, :] * ext[HALO:, :]
        for tap in range(CONV_TAPS - 1):
            pre = pre + w_ref[tap:tap + 1, :] * pltpu.roll(ext, CONV_TAPS - 1 - tap, axis=0)[HALO:, :]
        o_ref[...] = pre * _sigmoid(pre)

    return pl.pallas_call(
        body, name="conv_fwd", grid=(CONV_DIM // D, t // tm),
        out_shape=jax.ShapeDtypeStruct((t, CONV_DIM), F32),
        in_specs=[pl.BlockSpec((tm, D), lambda cb, i: (i, CONV_BLOCK0 + cb)),
                  pl.BlockSpec((HALO, D), lambda cb, i: (jnp.maximum(i * r - 1, 0), CONV_BLOCK0 + cb)),
                  pl.BlockSpec((CONV_TAPS, D), lambda cb, i: (0, cb)), pl.BlockSpec((1, D), lambda cb, i: (0, cb))],
        out_specs=pl.BlockSpec((tm, D), lambda cb, i: (i, cb)),
        compiler_params=_cparams(("parallel", "parallel")),
    )(proj, proj, conv_w, conv_b)


def conv_bwd(proj, dxc, conv_w, conv_b):
    t = proj.shape[0]
    tm = _tile(t, ROW_TILE)
    r = tm // HALO
    n = t // tm
    last_halo = t // HALO - 1

    def body(x_ref, prev_ref, next_ref, d_ref, dnext_ref, w_ref, b_ref, dx_ref, dw_ref, db_ref):
        i = pl.program_id(1)

        @pl.when(i == 0)
        def _():
            dw_ref[...] = jnp.zeros_like(dw_ref)
            db_ref[...] = jnp.zeros_like(db_ref)

        prev = jnp.where(i > 0, prev_ref[...], 0.0)
        ext = jnp.concatenate([prev, x_ref[...], next_ref[...]], axis=0)
        shifted = [pltpu.roll(ext, CONV_TAPS - 1 - tap, axis=0)[HALO:, :] for tap in range(CONV_TAPS - 1)]
        shifted.append(ext[HALO:, :])
        pre = b_ref[...]
        for tap in range(CONV_TAPS):
            pre = pre + w_ref[tap:tap + 1, :] * shifted[tap]
        s = _sigmoid(pre)
        d_ext = jnp.concatenate([d_ref[...], jnp.where(i < n - 1, dnext_ref[...], 0.0)], axis=0)
        dpre = d_ext * _dsilu(pre, s)
        dx = w_ref[CONV_TAPS - 1:CONV_TAPS, :] * dpre[:tm, :]
        for tap in range(CONV_TAPS - 1):
            back = CONV_TAPS - 1 - tap
            dx = dx + w_ref[tap:tap + 1, :] * pltpu.roll(dpre, tm + HALO - back, axis=0)[:tm, :]
        dx_ref[...] = dx.astype(BF16)
        dp = dpre[:tm, :]
        db_ref[...] += _colsum(dp)
        for tap in range(CONV_TAPS):
            dw_ref[tap:tap + 1, :] += _colsum(dp * shifted[tap][:tm, :])

    return pl.pallas_call(
        body, name="conv_bwd", grid=(CONV_DIM // D, n),
        out_shape=[jax.ShapeDtypeStruct((t, CONV_DIM), BF16), jax.ShapeDtypeStruct((8, CONV_DIM), F32),
                   jax.ShapeDtypeStruct((1, CONV_DIM), F32)],
        in_specs=[pl.BlockSpec((tm, D), lambda cb, i: (i, CONV_BLOCK0 + cb)),
                  pl.BlockSpec((HALO, D), lambda cb, i: (jnp.maximum(i * r - 1, 0), CONV_BLOCK0 + cb)),
                  pl.BlockSpec((HALO, D), lambda cb, i: (jnp.minimum((i + 1) * r, last_halo), CONV_BLOCK0 + cb)),
                  pl.BlockSpec((tm, D), lambda cb, i: (i, cb)),
                  pl.BlockSpec((HALO, D), lambda cb, i: (jnp.minimum((i + 1) * r, last_halo), cb)),
                  pl.BlockSpec((CONV_TAPS, D), lambda cb, i: (0, cb)), pl.BlockSpec((1, D), lambda cb, i: (0, cb))],
        out_specs=[pl.BlockSpec((tm, D), lambda cb, i: (i, cb)), pl.BlockSpec((8, D), lambda cb, i: (0, cb)),
                   pl.BlockSpec((1, D), lambda cb, i: (0, cb))],
        compiler_params=_cparams(("parallel", "arbitrary")),
    )(proj, proj, proj, dxc, dxc, conv_w, conv_b)


Z_BLOCK0 = 8
DT_BLOCK0 = 88
B_BLOCK0 = 16
C_BLOCK0 = 20


def _head_expand():
    e = np.zeros((N_STATE, GROUP_W), np.float32)
    for hh in range(HEADS_PER_GROUP):
        e[hh, hh * HEAD_P:(hh + 1) * HEAD_P] = 1.0
    return jnp.asarray(e, BF16)


def _ssd_chunk_terms(dt, bias, alog, expand, tril_f, eye):
    dtb = dt + bias
    delta = jnp.maximum(dtb, 0.0) + jnp.log(1.0 + jnp.exp(-jnp.abs(dtb)))
    ea = jnp.exp(alog)
    a = -ea * delta
    acum = _sel(_nn, a, tril_f, 3, x_first=False)
    delta_e = _sel(_nn, delta, expand, 2)
    acum_e = _sel(_nn, acum, expand, 3)
    acum_t = _sel(_nt, acum, eye, 3, x_first=False)
    return dtb, delta, ea, a, acum, delta_e, acum_e, acum_t


def ssd_fwd(proj, xc, alog4, bias4, dskip4, wnorm, expand):
    t = proj.shape[0]
    tb = _tile(t, TOKEN_BLOCK)
    ncb = tb // SSD_CHUNK

    def body(xs_ref, b_ref, c_ref, dt_ref, z_ref, alog_ref, bias_ref, dsk_ref, wn_ref, e_ref, ob_ref, st_ref, state):
        @pl.when(pl.program_id(1) == 0)
        def _():
            state[...] = jnp.zeros_like(state)

        expand = e_ref[...]
        mask = _tri(SSD_CHUNK)
        tril_f = mask.astype(BF16)
        eye = (lax.broadcasted_iota(jnp.int32, (N_STATE, N_STATE), 0) ==
               lax.broadcasted_iota(jnp.int32, (N_STATE, N_STATE), 1)).astype(BF16)
        alog, bias = alog_ref[0], bias_ref[0]
        d_e = _sel(_nn, jnp.broadcast_to(dsk_ref[0], (8, N_STATE)), expand, 3)[0:1, :]
        wn = wn_ref[...]

        def chunk(c, carry):
            sl = pl.ds(pl.multiple_of(c * SSD_CHUNK, SSD_CHUNK), SSD_CHUNK)
            xs, bm, cm, dt, z = xs_ref[sl, :], b_ref[sl, :], c_ref[sl, :], dt_ref[sl, :], z_ref[sl, :]
            dtb, delta, ea, a, acum, delta_e, acum_e, acum_t = _ssd_chunk_terms(dt, bias, alog, expand, tril_f, eye)
            alast_e = acum_e[SSD_CHUNK - 1:SSD_CHUNK, :]
            xd = xs * delta_e
            xdb = xd.astype(BF16)
            cb_, bb_ = cm.astype(BF16), bm.astype(BF16)
            cbm = _nt(cb_, bb_)
            ys = []
            for hh in range(HEADS_PER_GROUP):
                lh = jnp.where(mask, jnp.exp(jnp.minimum(acum[:, hh:hh + 1] - acum_t[hh:hh + 1, :], 0.0)), 0.0)
                ys.append(_nn((cbm * lh).astype(BF16), xdb[:, hh * HEAD_P:(hh + 1) * HEAD_P]))
            st = state[...]
            st_ref[0, c] = st
            y = jnp.concatenate(ys, axis=1) + _nn(cb_, st.astype(BF16)) * jnp.exp(acum_e) + xs * d_e
            state[...] = st * jnp.exp(alast_e) + _tn(bb_, (xd * jnp.exp(alast_e - acum_e)).astype(BF16))
            yg = y * z * _sigmoid(z)
            ob_ref[sl, :] = (yg * lax.rsqrt(jnp.mean(yg * yg, axis=-1, keepdims=True) + RMS_EPS) * wn).astype(BF16)
            return carry

        lax.fori_loop(0, ncb, chunk, 0, unroll=min(CHUNK_UNROLL, ncb))

    small = pl.BlockSpec((1, 1, N_STATE), lambda g, j: (g, 0, 0))
    return pl.pallas_call(
        body, name="ssd_fwd", grid=(N_GROUPS, t // tb),
        out_shape=[jax.ShapeDtypeStruct((t, B_INNER), BF16),
                   jax.ShapeDtypeStruct((N_GROUPS, t // SSD_CHUNK, N_STATE, GROUP_W), F32)],
        in_specs=[pl.BlockSpec((tb, GROUP_W), lambda g, j: (j, g)),
                  pl.BlockSpec((tb, N_STATE), lambda g, j: (j, B_BLOCK0 + g)),
                  pl.BlockSpec((tb, N_STATE), lambda g, j: (j, C_BLOCK0 + g)),
                  pl.BlockSpec((tb, N_STATE), lambda g, j: (j, DT_BLOCK0 + g)),
                  pl.BlockSpec((tb, GROUP_W), lambda g, j: (j, Z_BLOCK0 + g)),
                  small, small, small, pl.BlockSpec((1, GROUP_W), lambda g, j: (0, g)),
                  pl.BlockSpec((N_STATE, GROUP_W), lambda g, j: (0, 0))],
        out_specs=[pl.BlockSpec((tb, GROUP_W), lambda g, j: (j, g)),
                   pl.BlockSpec((1, ncb, N_STATE, GROUP_W), lambda g, j: (g, j, 0, 0))],
        scratch_shapes=[pltpu.VMEM((N_STATE, GROUP_W), F32)],
        compiler_params=_cparams(("parallel", "arbitrary")),
    )(xc, xc, xc, proj, proj, alog4, bias4, dskip4, wnorm, expand)


def ssd_bwd(proj, xc, alog4, bias4, dskip4, wnorm, expand, dob, states, part):
    t = proj.shape[0]
    tb = _tile(t, TOKEN_BLOCK)
    ncb = tb // SSD_CHUNK
    nb = t // tb

    def body(xs_ref, b_ref, c_ref, dt_ref, z_ref, alog_ref, bias_ref, dsk_ref, wn_ref, e_ref, dob_ref, st_ref, part_ref,
             dxs_ref, db_ref, dc_ref, dz_ref, ddt_ref, dwn_ref, dalog_ref, dbias_ref, ddsk_ref, parts_ref, dstate,
             send_sems, recv_sems, local_sem):
        xchg_start, xchg_wait = _chip_exchange(part_ref, parts_ref, send_sems, recv_sems, local_sem)

        @pl.when((pl.program_id(0) == 0) & (pl.program_id(1) == 0))
        def _():
            xchg_start()

        @pl.when(pl.program_id(1) == 0)
        def _():
            dstate[...] = jnp.zeros_like(dstate)
            dwn_ref[...] = jnp.zeros_like(dwn_ref)
            dalog_ref[...] = jnp.zeros_like(dalog_ref)
            dbias_ref[...] = jnp.zeros_like(dbias_ref)
            ddsk_ref[...] = jnp.zeros_like(ddsk_ref)

        expand = e_ref[...]
        mask = _tri(SSD_CHUNK)
        mask_t = _tri(SSD_CHUNK, upper=True)
        tril_f = mask.astype(BF16)
        triu_f = mask_t.astype(BF16)
        eye = (lax.broadcasted_iota(jnp.int32, (N_STATE, N_STATE), 0) ==
               lax.broadcasted_iota(jnp.int32, (N_STATE, N_STATE), 1)).astype(BF16)
        alog, bias = alog_ref[0], bias_ref[0]
        d_e = _sel(_nn, jnp.broadcast_to(dsk_ref[0], (8, N_STATE)), expand, 3)[0:1, :]
        wn = wn_ref[...]

        def chunk(i, c0):
            c = ncb - 1 - i
            sl = pl.ds(pl.multiple_of(c * SSD_CHUNK, SSD_CHUNK), SSD_CHUNK)
            xs, bm, cm, dt, z = xs_ref[sl, :], b_ref[sl, :], c_ref[sl, :], dt_ref[sl, :], z_ref[sl, :]
            dtb, delta, ea, a, acum, delta_e, acum_e, acum_t = _ssd_chunk_terms(dt, bias, alog, expand, tril_f, eye)
            alast_e = acum_e[SSD_CHUNK - 1:SSD_CHUNK, :]
            eacum = jnp.exp(acum_e)
            wl = jnp.exp(alast_e - acum_e)
            xd = xs * delta_e
            xdb = xd.astype(BF16)
            cb_, bb_ = cm.astype(BF16), bm.astype(BF16)
            cbm = _nt(cb_, bb_)
            cbm_t = _nt(bb_, cb_)
            st32 = st_ref[0, c]
            stb = st32.astype(BF16)
            dst = dstate[...]
            dstb = dst.astype(BF16)
            lhs, lhts, ys = [], [], []
            for hh in range(HEADS_PER_GROUP):
                col, row = acum[:, hh:hh + 1], acum_t[hh:hh + 1, :]
                lh = jnp.where(mask, jnp.exp(jnp.minimum(col - row, 0.0)), 0.0)
                lht = jnp.where(mask_t, jnp.exp(jnp.minimum(row - col, 0.0)), 0.0)
                lhs.append(lh)
                lhts.append(lht)
                ys.append(_nn((cbm * lh).astype(BF16), xdb[:, hh * HEAD_P:(hh + 1) * HEAD_P]))
            y_in = jnp.concatenate(ys, axis=1)
            y_out = _nn(cb_, stb) * eacum
            y = y_in + y_out + xs * d_e
            sgz = _sigmoid(z)
            sz = z * sgz
            yg = y * sz
            rstd = lax.rsqrt(jnp.mean(yg * yg, axis=-1, keepdims=True) + RMS_EPS)
            nrm = yg * rstd
            dob_v = dob_ref[sl, :]
            dn = dob_v * wn
            dwn_ref[...] += _colsum(dob_v * nrm)
            dyg = rstd * (dn - nrm * jnp.mean(dn * nrm, axis=-1, keepdims=True))
            dy = dyg * sz
            dz_ref[sl, :] = (dyg * y * _dsilu(z, sgz)).astype(BF16)
            dyb = dy.astype(BF16)
            dxds = []
            dcb = jnp.zeros((SSD_CHUNK, SSD_CHUNK), F32)
            dcb_t = jnp.zeros((SSD_CHUNK, SSD_CHUNK), F32)
            for hh in range(HEADS_PER_GROUP):
                hs = slice(hh * HEAD_P, (hh + 1) * HEAD_P)
                dy_h, x_h = dyb[:, hs], xdb[:, hs]
                dxds.append(_nn((cbm_t * lhts[hh]).astype(BF16), dy_h))
                dcb = dcb + _nt(dy_h, x_h) * lhs[hh]
                dcb_t = dcb_t + _nt(x_h, dy_h) * lhts[hh]
            dye = (dy * eacum).astype(BF16)
            xw = (xd * wl).astype(BF16)
            dxd_in = jnp.concatenate(dxds, axis=1)
            dxd_out = wl * _nn(bb_, dstb)
            dxd = dxd_in + dxd_out
            dc_ref[sl, :] = _nn(dcb.astype(BF16), bb_) + _nt(dye, stb)
            db_ref[sl, :] = _nn(dcb_t.astype(BF16), cb_) + _nt(xw, dstb)
            dstate[...] = dst * jnp.exp(alast_e) + _tn(cb_, dye)
            col_out = xd * dxd_out
            dac = _sel(_nt, dyb.astype(F32) * y_in - xdb.astype(F32) * dxd_in + dy * y_out - col_out, expand, 3)
            beyond = _colsum(col_out) + jnp.exp(alast_e) * _colsum(dst * st32)
            da = (_sel(_nn, dac, triu_f, 3, x_first=False) +
                  _sel(_nt, jnp.broadcast_to(beyond, (8, GROUP_W)), expand, 3)[0:1, :])
            ddelta = _sel(_nt, dxd * xs, expand, 2) - da * ea
            dalog_ref[0] += _colsum(da * a)
            ddtb = ddelta * _sigmoid(dtb)
            dbias_ref[0] += _colsum(ddtb)
            ddt_ref[sl, :] = ddtb.astype(BF16)
            ddsk_ref[0] += _sel(_nt, jnp.broadcast_to(_colsum(dy * xs), (8, GROUP_W)), expand, 3)[0:1, :]
            dxs_ref[sl, :] = dxd * delta_e + dy * d_e
            return c0

        lax.fori_loop(0, ncb, chunk, 0, unroll=min(CHUNK_UNROLL, ncb))

        @pl.when((pl.program_id(0) == N_GROUPS - 1) & (pl.program_id(1) == nb - 1))
        def _():
            xchg_wait()

    small = pl.BlockSpec((1, 1, N_STATE), lambda g, j: (g, 0, 0))
    wide = pl.BlockSpec((tb, GROUP_W), lambda g, j: (nb - 1 - j, g))
    narrow = pl.BlockSpec((tb, N_STATE), lambda g, j: (nb - 1 - j, g))
    hbm = pl.BlockSpec(memory_space=pl.ANY)
    return pl.pallas_call(
        body, name="ssd_bwd", grid=(N_GROUPS, nb),
        out_shape=[jax.ShapeDtypeStruct((t, B_INNER), F32), jax.ShapeDtypeStruct((t, GROUP_W), F32),
                   jax.ShapeDtypeStruct((t, GROUP_W), F32), jax.ShapeDtypeStruct((t, B_INNER), BF16),
                   jax.ShapeDtypeStruct((t, GROUP_W), BF16), jax.ShapeDtypeStruct((1, B_INNER), F32),
                   jax.ShapeDtypeStruct((N_GROUPS, 1, N_STATE), F32), jax.ShapeDtypeStruct((N_GROUPS, 1, N_STATE), F32),
                   jax.ShapeDtypeStruct((N_GROUPS, 1, N_STATE), F32), jax.ShapeDtypeStruct(part.shape, part.dtype)],
        in_specs=[wide,
                  pl.BlockSpec((tb, N_STATE), lambda g, j: (nb - 1 - j, B_BLOCK0 + g)),
                  pl.BlockSpec((tb, N_STATE), lambda g, j: (nb - 1 - j, C_BLOCK0 + g)),
                  pl.BlockSpec((tb, N_STATE), lambda g, j: (nb - 1 - j, DT_BLOCK0 + g)),
                  pl.BlockSpec((tb, GROUP_W), lambda g, j: (nb - 1 - j, Z_BLOCK0 + g)),
                  small, small, small, pl.BlockSpec((1, GROUP_W), lambda g, j: (0, g)),
                  pl.BlockSpec((N_STATE, GROUP_W), lambda g, j: (0, 0)), wide,
                  pl.BlockSpec((1, ncb, N_STATE, GROUP_W), lambda g, j: (g, nb - 1 - j, 0, 0)), hbm],
        out_specs=[wide, narrow, narrow, wide, narrow, pl.BlockSpec((1, GROUP_W), lambda g, j: (0, g)),
                   small, small, small, hbm],
        scratch_shapes=[pltpu.VMEM((N_STATE, GROUP_W), F32)] + CHIP_SEMS,
        compiler_params=_cparams(("arbitrary", "arbitrary")),
    )(xc, xc, xc, proj, proj, alog4, bias4, dskip4, wnorm, expand, dob, states, part)


def lower_bound_fwd(hgrn_lb):
    def body(a_ref, o_ref):
        a0, a1 = a_ref[0:1, :], a_ref[1:2, :]
        m = jnp.maximum(a0, a1)
        e0, e1 = jnp.exp(a0 - m), jnp.exp(a1 - m)
        o_ref[...] = e0 / (e0 + e1)

    return pl.pallas_call(body, name="lower_bound_fwd", out_shape=jax.ShapeDtypeStruct((1, D), F32))(hgrn_lb)


def ada_weight_grad(c_all, dmod_cols):
    def body(c_ref, d_ref, o_ref):
        cval = c_ref[...]
        o_ref[...] = _tn(cval * _sigmoid(cval), d_ref[...], HI)

    return pl.pallas_call(body, name="ada_weight_grad",
                          out_shape=jax.ShapeDtypeStruct((D, dmod_cols.shape[1]), F32))(c_all, dmod_cols)


def reduce_small(gathered, hgrn_lb, dlb_off):
    n = gathered.shape[2]

    def body(g_ref, a_ref, o_ref, glb_ref):
        s = g_ref[0]
        for d in range(1, N_DEV):
            s = s + g_ref[d]
        o_ref[...] = s
        a0, a1 = a_ref[0:1, :], a_ref[1:2, :]
        m = jnp.maximum(a0, a1)
        e0, e1 = jnp.exp(a0 - m), jnp.exp(a1 - m)
        p0 = e0 / (e0 + e1)
        tq = s[:, dlb_off:dlb_off + D] * p0 * (1.0 - p0)
        glb_ref[0:1, :] = tq
        glb_ref[1:2, :] = -tq

    return pl.pallas_call(body, name="reduce_small",
                          out_shape=[jax.ShapeDtypeStruct((1, n), F32), jax.ShapeDtypeStruct((2, D), F32)])(gathered, hgrn_lb)


def _adam_math(w, g, m, v):
    m2 = ADAM_B1 * m + (1.0 - ADAM_B1) * g
    v2 = ADAM_B2 * v + (1.0 - ADAM_B2) * (g * g)
    m_hat = m2 / (1.0 - ADAM_B1 ** ADAM_STEP)
    v_hat = v2 / (1.0 - ADAM_B2 ** ADAM_STEP)
    delta = -ADAM_LR * (m_hat / (jnp.sqrt(v_hat) + ADAM_EPS) + ADAM_WD * w)
    return delta, m2, v2


def _row_tile(rows, mult=8, cap=128):
    for cand in range(cap - cap % mult, 0, -mult):
        if rows % cand == 0:
            return cand
    return rows


def sum_parts(parts, name):
    n, rows, cols = parts.shape
    tr = _row_tile(rows, 16, 256)

    def body(p_ref, o_ref):
        s = p_ref[0].astype(F32)
        for d in range(1, n):
            s = s + p_ref[d].astype(F32)
        o_ref[...] = s

    return pl.pallas_call(
        body, name=name, grid=(rows // tr,),
        out_shape=jax.ShapeDtypeStruct((rows, cols), F32),
        in_specs=[pl.BlockSpec((n, tr, cols), lambda i: (0, i, 0))],
        out_specs=pl.BlockSpec((tr, cols), lambda i: (i, 0)),
        compiler_params=_cparams(("parallel",)),
    )(parts)


def sum_pair(a, b, name):
    rows, cols = a.shape
    tr = _row_tile(rows, 16, 256)

    def body(a_ref, b_ref, o_ref):
        o_ref[...] = (a_ref[...].astype(F32) + b_ref[...].astype(F32)).astype(o_ref.dtype)

    blk = pl.BlockSpec((tr, cols), lambda i: (i, 0))
    return pl.pallas_call(
        body, name=name, grid=(rows // tr,),
        out_shape=jax.ShapeDtypeStruct((rows, cols), a.dtype),
        in_specs=[blk, blk], out_specs=blk,
        compiler_params=_cparams(("parallel",)),
    )(a, b)


def adamw(w, g, m, v, name):
    rows, cols = w.shape
    tr = _row_tile(rows)

    def body(w_ref, g_ref, m_ref, v_ref, d_ref, m2_ref, v2_ref):
        delta, m2, v2 = _adam_math(w_ref[...], g_ref[...], m_ref[...], v_ref[...])
        d_ref[...] = delta
        m2_ref[...] = m2
        v2_ref[...] = v2

    blk = pl.BlockSpec((tr, cols), lambda i: (i, 0))
    return pl.pallas_call(
        body, name=name, grid=(rows // tr,),
        out_shape=[jax.ShapeDtypeStruct((rows, cols), F32)] * 3,
        in_specs=[blk] * 4, out_specs=[blk] * 3,
        compiler_params=_cparams(("parallel",)),
    )(w, g, m, v)


def _pad128(n):
    return -(-n // 128) * 128


def _pack(arrays):
    offs, parts, off = [], [], 0
    for a in arrays:
        flat = a.reshape(1, -1)
        n = flat.shape[1]
        offs.append(off)
        parts.append(jnp.pad(flat, ((0, 0), (0, _pad128(n) - n))))
        off += _pad128(n)
    return jnp.concatenate(parts, axis=1), offs


def _unpack(vec, offs, shapes):
    out = []
    for off, shp in zip(offs, shapes):
        n = int(np.prod(shp))
        out.append(vec[0, off:off + n].reshape(shp))
    return out


def _permute_in_rows(w_t):
    dt = w_t[9216:9248].reshape(N_GROUPS, HEADS_PER_GROUP, D)
    dt = jnp.pad(dt, ((0, 0), (0, N_STATE - HEADS_PER_GROUP), (0, 0))).reshape(N_GROUPS * N_STATE, D)
    dt = jnp.pad(dt, ((0, D - N_GROUPS * N_STATE), (0, 0)))
    return jnp.concatenate([w_t[:9216], w_t[9248:], dt], axis=0)


def _unpermute_in_rows(g_t):
    dt = g_t[11264:11264 + N_GROUPS * N_STATE].reshape(N_GROUPS, N_STATE, D)[:, :HEADS_PER_GROUP].reshape(32, D)
    return jnp.concatenate([g_t[:9216], dt, g_t[9216:11264]], axis=0)


def kernel(x, c, w_ada, b_ada, w_in, hgrn_lb, hgrn_gnorm, ssm_conv_w, ssm_conv_b, ssm_dt_bias, ssm_a_log, ssm_d, ssm_norm, w_branch_a, w_branch_b, w_o, ln1_g, ln1_b, w_ffn_gate, w_ffn_up, w_ffn_down, ln2_g, ln2_b, loss_target, m_w_ada, m_b_ada, m_w_in, m_hgrn_lb, m_hgrn_gnorm, m_ssm_conv_w, m_ssm_conv_b, m_ssm_dt_bias, m_ssm_a_log, m_ssm_d, m_ssm_norm, m_w_branch_a, m_w_branch_b, m_w_o, m_ln1_g, m_ln1_b, m_w_ffn_gate, m_w_ffn_up, m_w_ffn_down, m_ln2_g, m_ln2_b, v_w_ada, v_b_ada, v_w_in, v_hgrn_lb, v_hgrn_gnorm, v_ssm_conv_w, v_ssm_conv_b, v_ssm_dt_bias, v_ssm_a_log, v_ssm_d, v_ssm_norm, v_w_branch_a, v_w_branch_b, v_w_o, v_ln1_g, v_ln1_b, v_w_ffn_gate, v_w_ffn_up, v_w_ffn_down, v_ln2_g, v_ln2_b):
    me = 4 * lax.axis_index("x") + 2 * lax.axis_index("y") + lax.axis_index("c")
    xt = x[0]
    tgt = loss_target[0]
    t = xt.shape[0]
    ada_cols = w_ada.shape[2]
    conv_cols = ssm_conv_w.shape[2]

    small_in, _ = _pack([c, ssm_conv_w[0]])
    small_all = allgather_vmem(small_in, "allgather_small_inputs")
    c_all = small_all[:, 0, :D]
    conv_w = small_all[:, 0, D:D + CONV_TAPS * conv_cols].reshape(N_DEV, CONV_TAPS, conv_cols)
    conv_w = conv_w.transpose(1, 0, 2).reshape(CONV_TAPS, CONV_DIM)
    mod = ada_modulation(c_all, w_ada[0], b_ada.reshape(N_DEV, 1, ada_cols))
    mod6 = mod.reshape(6, D)

    shards = [w_in[0].T, w_branch_a[0], w_branch_b[0], w_o[0], w_ffn_gate[0].T, w_ffn_up[0].T, w_ffn_down[0]]
    shard_rows = [s.shape[0] for s in shards]
    slot_rows = [-(-r // 32) * 32 for r in shard_rows]
    row_offs = [sum(slot_rows[:i]) for i in range(len(shards))]
    padded = [jnp.pad(s.astype(BF16), ((0, p - r), (0, 0))) for s, r, p in zip(shards, shard_rows, slot_rows)]
    g_in = allgather_hbm(padded[0], "allgather_w_in")[:, :shard_rows[0]]
    w_in_t = _permute_in_rows(g_in.reshape(IN_DIM, D))

    lb = lower_bound_fwd(hgrn_lb)
    u1 = ln_modulate(xt, mod6, 0, 1, "ln_modulate_1")
    proj, g_rest = mm_nt_gather(u1, w_in_t, F32, jnp.concatenate(padded[1:], axis=0), "mm_in_proj")
    g_ba, g_bb, g_o, g_fg, g_fu, g_fd = (g_rest[:, o - slot_rows[0]:o - slot_rows[0] + r]
                                         for o, r in zip(row_offs[1:], shard_rows[1:]))
    w_ba = g_ba.reshape(D, D)
    w_bb = g_bb.reshape(B_INNER, D)
    w_oo = g_o.reshape(D, D)
    ffpad = ((0, D_FF_PAD - D_FF), (0, 0))
    w_gu_t = jnp.concatenate([jnp.pad(g_fg.reshape(D_FF, D), ffpad), jnp.pad(g_fu.reshape(D_FF, D), ffpad)], axis=0)
    w_dn = jnp.pad(g_fd.reshape(D_FF, D), ffpad)
    o_a, o_raw, st_a = hgrn_fwd(proj, lb, hgrn_gnorm)
    xc = conv_fwd(proj, conv_w, ssm_conv_b)
    pad3 = ((0, 0), (0, 0), (0, N_STATE - HEADS_PER_GROUP))
    alog4 = jnp.pad(ssm_a_log.reshape(N_GROUPS, 1, HEADS_PER_GROUP), pad3)
    bias4 = jnp.pad(ssm_dt_bias.reshape(N_GROUPS, 1, HEADS_PER_GROUP), pad3)
    dskip4 = jnp.pad(ssm_d.reshape(N_GROUPS, 1, HEADS_PER_GROUP), pad3)
    expand = _head_expand()
    o_b, st_b = ssd_fwd(proj, xc, alog4, bias4, dskip4, ssm_norm, expand)
    ya = mm_nn(o_a, w_ba, F32, "mm_branch_a")
    yb = mm_nn(o_b, w_bb, F32, "mm_branch_b")
    merged = merge_gates(ya, yb, proj)
    h1 = mm_nn(merged, w_oo, F32, "mm_out_proj")
    x1 = resid_ln(xt, h1, mod6, 2, ln1_g, ln1_b, "resid_ln_1")
    u2 = ln_modulate(x1, mod6, 3, 4, "ln_modulate_2")
    gu = mm_nt(u2, w_gu_t, F32, "mm_ffn_in")
    act = swiglu_act(gu)
    h2 = mm_nn(act, w_dn, F32, "mm_ffn_out")

    dh2, dx1_part, acc4 = resid_ln_bwd(x1, h2, mod6, 5, ln2_g, ln2_b, tgt, True, "resid_ln_2_bwd")
    g_dn = mm_tn(act, dh2, "mm_grad_ffn_down")
    dact = mm_nt(dh2, w_dn, F32, "mm_dact")
    dgu = swiglu_act_bwd(gu, dact)
    g_gu_t = mm_tn(dgu, u2, "mm_grad_ffn_in")
    du2 = mm_nn(dgu, w_gu_t, F32, "mm_du2")
    dx1, acc3 = ln_modulate_bwd(x1, du2, mod6, 4, dx1_part, "ln_modulate_2_bwd")
    dh1, dx_part, acc2 = resid_ln_bwd(xt, h1, mod6, 2, ln1_g, ln1_b, dx1, False, "resid_ln_1_bwd")
    g_o = mm_tn(merged, dh1, "mm_grad_out_proj")
    dmerged = mm_nt(dh1, w_oo, F32, "mm_dmerged")
    dya, dyb, dga, dgb = merge_gates_bwd(dmerged, ya, yb, proj)
    g_ba_full = mm_tn(o_a, dya, "mm_grad_branch_a")
    g_bb_full = mm_tn(o_b, dyb, "mm_grad_branch_b")
    doa = mm_nt(dya, w_ba, F32, "mm_doa")
    dob = mm_nt(dyb, w_bb, F32, "mm_dob")
    my_core = lax.axis_index("c")

    def by_core(blocks, rows, slots):
        contrib = jnp.concatenate([jnp.pad(b.reshape(N_DEV, -1, D), ((0, 0), (0, p - r), (0, 0)))
                                   for b, r, p in zip(blocks, rows, slots)], axis=1)
        split = contrib.reshape(N_CHIP, 2, contrib.shape[1], D).transpose(1, 0, 2, 3)
        return (lax.dynamic_index_in_dim(split, my_core, 0, keepdims=False),
                lax.dynamic_index_in_dim(split, 1 - my_core, 0, keepdims=False))

    keep_e, give_e = by_core([g_ba_full, g_bb_full, g_o, g_gu_t[:D_FF], g_gu_t[D_FF_PAD:D_FF_PAD + D_FF], g_dn[:D_FF]],
                             shard_rows[1:], slot_rows[1:])
    dq, dfl, di, dg, dlb, dgn, got_e = hgrn_bwd(proj, lb, hgrn_gnorm, o_raw, doa, st_a, give_e)
    chip_e = sum_pair(keep_e.reshape(-1, D), got_e.reshape(-1, D), "sum_grads_rest_chip").reshape(keep_e.shape)
    dxs, dbm, dcm, dz, ddt, dwn, dalog, dbias, ddsk, parts_e = ssd_bwd(proj, xc, alog4, bias4, dskip4, ssm_norm, expand,
                                                                       dob, st_b, chip_e)
    dxc = jnp.concatenate([dxs, dbm, dcm], axis=1)
    dxbc, dcw, dcb = conv_bwd(proj, dxc, conv_w, ssm_conv_b)
    dproj = jnp.concatenate([dq, dfl, di, dg, dz, dxbc, dga, dgb, ddt, jnp.zeros((t, D - N_GROUPS * N_STATE), BF16)], axis=1)
    g_in_t = mm_tn(dproj, u1, "mm_grad_in_proj")
    du1 = mm_nn(dproj, w_in_t, F32, "mm_du1")
    dx, acc1 = ln_modulate_bwd(xt, du1, mod6, 1, dx_part, "ln_modulate_1_bwd")

    keep_l, give_l = by_core([_unpermute_in_rows(g_in_t)], shard_rows[:1], slot_rows[:1])
    got_l = exchange_sibling(give_l, "exchange_grad_in_sibling")
    chip_l = sum_pair(keep_l.reshape(-1, D), got_l.reshape(-1, D), "sum_grad_in_chip").reshape(keep_l.shape)
    parts_l = exchange_chips(chip_l, "exchange_grad_in_chips")
    gw_in = sum_parts(parts_l, "sum_grad_in")[:shard_rows[0]].T
    g_rows = sum_parts(parts_e, "sum_grads_rest")
    gw_ba, gw_bb, gw_o, gw_fg, gw_fu, gw_fd = (g_rows[o - slot_rows[0]:o - slot_rows[0] + r]
                                               for o, r in zip(row_offs[1:], shard_rows[1:]))
    gw_fg, gw_fu = gw_fg.T, gw_fu.T

    dmod = jnp.concatenate([acc1[1:2], acc1[0:1], acc2[0:1], acc3[1:2], acc3[0:1], acc4[0:1]], axis=1)
    small_fields = [dmod, acc4[3:4, :128], dlb, dgn, dcw[:CONV_TAPS], dcb, dbias, dalog, ddsk, dwn,
                    acc2[1:2], acc2[2:3], acc4[1:2], acc4[2:3]]
    small_out, offs = _pack(small_fields)
    small_sum_in = allgather_vmem(small_out, "allgather_small_grads")
    gsum, g_lb = reduce_small(small_sum_in, hgrn_lb, offs[2])
    (g_bada, loss_row, _, g_gn, g_cw_full, g_cb, g_bias4, g_alog4, g_dsk4, g_wn, g_l1g, g_l1b, g_l2g, g_l2b) = _unpack(
        gsum, offs, [(1, 6 * D), (1, 128), (1, D), (1, HK), (CONV_TAPS, CONV_DIM), (1, CONV_DIM),
                     (N_GROUPS, N_STATE), (N_GROUPS, N_STATE), (N_GROUPS, N_STATE), (1, B_INNER),
                     (1, D), (1, D), (1, D), (1, D)])
    loss = loss_row[0, 0]
    g_cw = lax.dynamic_slice(g_cw_full, (0, me * conv_cols), (CONV_TAPS, conv_cols))[None]
    g_dtb = g_bias4[:, :HEADS_PER_GROUP].reshape(1, 32)
    g_alog = g_alog4[:, :HEADS_PER_GROUP].reshape(1, 32)
    g_dsk = g_dsk4[:, :HEADS_PER_GROUP].reshape(1, 32)

    dmod_all = small_sum_in[:, 0, offs[0]:offs[0] + 6 * D]
    dmod_cols = lax.dynamic_slice(dmod_all, (0, me * ada_cols), (N_DEV, ada_cols))
    gw_ada = ada_weight_grad(c_all, dmod_cols)

    big = [("ada", w_ada[0], gw_ada, m_w_ada[0], v_w_ada[0]), ("in", w_in[0], gw_in, m_w_in[0], v_w_in[0]),
           ("branch_a", w_branch_a[0], gw_ba, m_w_branch_a[0], v_w_branch_a[0]),
           ("branch_b", w_branch_b[0], gw_bb, m_w_branch_b[0], v_w_branch_b[0]),
           ("o", w_o[0], gw_o, m_w_o[0], v_w_o[0]),
           ("ffn_gate", w_ffn_gate[0], gw_fg, m_w_ffn_gate[0], v_w_ffn_gate[0]),
           ("ffn_up", w_ffn_up[0], gw_fu, m_w_ffn_up[0], v_w_ffn_up[0]),
           ("ffn_down", w_ffn_down[0], gw_fd, m_w_ffn_down[0], v_w_ffn_down[0])]
    big_out = {}
    for nm, w_, g_, m_, v_ in big:
        d_, m2_, v2_ = adamw(w_, g_, m_, v_, "adamw_" + nm)
        big_out[nm] = (g_[None], d_[None], m2_[None], v2_[None])

    small_w = [b_ada, hgrn_lb, hgrn_gnorm, ssm_conv_w, ssm_conv_b, ssm_dt_bias, ssm_a_log, ssm_d, ssm_norm,
               ln1_g, ln1_b, ln2_g, ln2_b]
    small_g = [g_bada, g_lb, g_gn, g_cw, g_cb, g_dtb, g_alog, g_dsk, g_wn, g_l1g, g_l1b, g_l2g, g_l2b]
    small_m = [m_b_ada, m_hgrn_lb, m_hgrn_gnorm, m_ssm_conv_w, m_ssm_conv_b, m_ssm_dt_bias, m_ssm_a_log, m_ssm_d,
               m_ssm_norm, m_ln1_g, m_ln1_b, m_ln2_g, m_ln2_b]
    small_v = [v_b_ada, v_hgrn_lb, v_hgrn_gnorm, v_ssm_conv_w, v_ssm_conv_b, v_ssm_dt_bias, v_ssm_a_log, v_ssm_d,
               v_ssm_norm, v_ln1_g, v_ln1_b, v_ln2_g, v_ln2_b]
    shapes = [a.shape for a in small_w]
    small_g = [g_.reshape(s) for g_, s in zip(small_g, shapes)]
    pw, poffs = _pack(small_w)
    pg, _ = _pack(small_g)
    pm, _ = _pack(small_m)
    pv, _ = _pack(small_v)
    pd, pm2, pv2 = adamw(pw, pg, pm, pv, "adamw_small")
    s_d, s_m, s_v = (_unpack(p, poffs, shapes) for p in (pd, pm2, pv2))
    (sn_bada, sn_lb, sn_gn, sn_cw, sn_cb, sn_dtb, sn_alog, sn_dsk, sn_wn, sn_l1g, sn_l1b, sn_l2g, sn_l2b) = range(13)

    def order(kind):
        sm = [small_g, s_d, s_m, s_v][kind]
        bg = lambda nm: big_out[nm][kind]
        return [bg("ada"), sm[sn_bada], bg("in"), sm[sn_lb], sm[sn_gn], sm[sn_cw], sm[sn_cb], sm[sn_dtb], sm[sn_alog],
                sm[sn_dsk], sm[sn_wn], bg("branch_a"), bg("branch_b"), bg("o"), sm[sn_l1g], sm[sn_l1b],
                bg("ffn_gate"), bg("ffn_up"), bg("ffn_down"), sm[sn_l2g], sm[sn_l2b]]

    return (loss, dx[None], *order(0), *order(1), *order(2), *order(3))
```

```python
import numpy as np
import jax
import jax.numpy as jnp
from jax import lax
from jax.experimental import pallas as pl
from jax.experimental.pallas import tpu as pltpu

F32 = jnp.float32
BF16 = jnp.bfloat16
HI = lax.Precision.HIGHEST

N_DEV = 8
D = 1024
N_HEADS_A = 8
HK = 128
CHUNK = 64
SSD_CHUNK = 128
N_GROUPS = 4
HEADS_PER_GROUP = 8
HEAD_P = 64
N_STATE = 128
GROUP_W = HEADS_PER_GROUP * HEAD_P
B_INNER = 2048
CONV_DIM = 3072
D_FF = 2816
D_FF_PAD = 3072
IN_DIM = 11296
N_PROJ = 12288
ALPHA = 2.0 ** 0.25
LN_EPS = 1e-5
RMS_EPS = 1e-6
Q_SCALE = 128 ** -0.5
EXP_CLIP = 80.0
ADAM_LR, ADAM_B1, ADAM_B2, ADAM_EPS, ADAM_WD, ADAM_STEP = 0.001, 0.9, 0.999, 1e-8, 0.01, 10
VMEM_LIMIT = 48 * 1024 * 1024
TOKEN_BLOCK = 512
ROW_TILE = 256
FFN_ROW_TILE = 128
MM_ROW_TILE = 1024
MM_TOKEN_TILE = 4096
MM_K_TILE = 3072
CHUNK_UNROLL = 8
MESH_ID = pl.DeviceIdType.MESH

NT_DIMS = (((1,), (1,)), ((), ()))
TN_DIMS = (((0,), (0,)), ((), ()))


def _cparams(sem=None):
    return pltpu.CompilerParams(dimension_semantics=sem, vmem_limit_bytes=VMEM_LIMIT)


def _sigmoid(x):
    return 1.0 / (1.0 + jnp.exp(-x))


def _dsilu(x, s):
    return s * (1.0 + x * (1.0 - s))


def _nt(a, b, precision=None):
    return lax.dot_general(a, b, NT_DIMS, precision=precision, preferred_element_type=F32)


def _tn(a, b, precision=None):
    return lax.dot_general(a, b, TN_DIMS, precision=precision, preferred_element_type=F32)


def _nn(a, b, precision=None):
    return jnp.dot(a, b, precision=precision, preferred_element_type=F32)


def _split(x, pieces):
    out = []
    for i in range(pieces):
        p = x.astype(BF16)
        out.append(p)
        if i + 1 < pieces:
            x = x - p.astype(F32)
    return out


def _sel(dot, x, sel01, pieces, x_first=True):
    acc = None
    for p in _split(x, pieces):
        term = dot(p, sel01) if x_first else dot(sel01, p)
        acc = term if acc is None else acc + term
    return acc


def _ln(x):
    mu = jnp.mean(x, axis=-1, keepdims=True)
    xc = x - mu
    rstd = lax.rsqrt(jnp.mean(xc * xc, axis=-1, keepdims=True) + LN_EPS)
    return xc * rstd, rstd


def _ln_bwd(dxh, xh, rstd):
    return rstd * (dxh - jnp.mean(dxh, axis=-1, keepdims=True) - xh * jnp.mean(dxh * xh, axis=-1, keepdims=True))


def _colsum(x):
    return jnp.sum(x, axis=0, keepdims=True)


def _tri(n, upper=False):
    r = lax.broadcasted_iota(jnp.int32, (n, n), 0)
    c = lax.broadcasted_iota(jnp.int32, (n, n), 1)
    return (c >= r) if upper else (r >= c)


def _my_pos():
    return lax.axis_index("x"), lax.axis_index("y"), lax.axis_index("c")


def _peer(pos, k):
    x, y, c = pos
    return (x ^ ((k >> 2) & 1), y ^ ((k >> 1) & 1), c ^ (k & 1))


def _flat(pos):
    return 4 * pos[0] + 2 * pos[1] + pos[2]


def allgather_vmem(v, name):
    n = v.shape[1]

    def body(v_ref, o_ref, send_sems, recv_sems, local_sem):
        me = _my_pos()
        mine = pltpu.make_async_copy(v_ref, o_ref.at[_flat(me)], local_sem)
        mine.start()
        sends = []
        for k in range(1, N_DEV):
            peer = _peer(me, k)
            cp = pltpu.make_async_remote_copy(v_ref, o_ref.at[_flat(me)], send_sems.at[k - 1], recv_sems.at[k - 1],
                                              device_id=peer, device_id_type=MESH_ID)
            cp.start()
            sends.append(cp)
        for k in range(1, N_DEV):
            peer = _peer(me, k)
            pltpu.make_async_remote_copy(v_ref, o_ref.at[_flat(peer)], send_sems.at[k - 1], recv_sems.at[k - 1],
                                         device_id=peer, device_id_type=MESH_ID).wait_recv()
        for cp in sends:
            cp.wait_send()
        mine.wait()

    return pl.pallas_call(
        body, name=name,
        out_shape=jax.ShapeDtypeStruct((N_DEV, 1, n), F32),
        in_specs=[pl.BlockSpec(memory_space=pltpu.VMEM)],
        out_specs=pl.BlockSpec(memory_space=pltpu.VMEM),
        scratch_shapes=[pltpu.SemaphoreType.DMA((N_DEV - 1,)), pltpu.SemaphoreType.DMA((N_DEV - 1,)),
                        pltpu.SemaphoreType.DMA],
        compiler_params=_cparams(),
    )(v)


def ada_modulation(c_all, w_ada_s, b_ada_r):
    ncol = w_ada_s.shape[1]

    def body(c_ref, w_ref, b_ref, o_ref, part_ref, send_sems, recv_sems):
        me = _my_pos()
        cval = c_ref[...]
        cond = cval * _sigmoid(cval)
        part = _nn(cond, w_ref[...], HI)
        for r in range(N_DEV):
            part_ref[r] = part[r:r + 1, :]
        sends = []
        for k in range(1, N_DEV):
            peer = _peer(me, k)
            cp = pltpu.make_async_remote_copy(part_ref.at[_flat(peer)], o_ref.at[_flat(me)], send_sems.at[k - 1],
                                              recv_sems.at[k - 1], device_id=peer, device_id_type=MESH_ID)
            cp.start()
            sends.append(cp)
        o_ref[_flat(me)] = part_ref[_flat(me)]
        for k in range(1, N_DEV):
            peer = _peer(me, k)
            pltpu.make_async_remote_copy(part_ref.at[_flat(peer)], o_ref.at[_flat(peer)], send_sems.at[k - 1],
                                         recv_sems.at[k - 1], device_id=peer, device_id_type=MESH_ID).wait_recv()
        for cp in sends:
            cp.wait_send()
        o_ref[...] = o_ref[...] + b_ref[...]

    return pl.pallas_call(
        body, name="ada_modulation",
        out_shape=jax.ShapeDtypeStruct((N_DEV, 1, ncol), F32),
        in_specs=[pl.BlockSpec(memory_space=pltpu.VMEM)] * 3,
        out_specs=pl.BlockSpec(memory_space=pltpu.VMEM),
        scratch_shapes=[pltpu.VMEM((N_DEV, 1, ncol), F32), pltpu.SemaphoreType.DMA((N_DEV - 1,)),
                        pltpu.SemaphoreType.DMA((N_DEV - 1,))],
        compiler_params=_cparams(),
    )(c_all, w_ada_s, b_ada_r)


def allgather_hbm(shard, name):
    def body(x_ref, out_ref, send_sems, recv_sems, local_sem):
        x, y, c = _my_pos()
        me, sibling = (x, y, c), (x, y, 1 - c)
        chips = [(1 - x, y), (x, 1 - y), (1 - x, 1 - y)]

        def slot(pos):
            return out_ref.at[_flat(pos)]

        def copy(k, block, to, src=None):
            return pltpu.make_async_remote_copy(slot(block) if src is None else src, slot(block), send_sems.at[k],
                                                recv_sems.at[k], device_id=to, device_id_type=MESH_ID)

        mine = pltpu.make_async_copy(x_ref, slot(me), local_sem)
        mine.start()
        first = [copy(0, me, sibling, src=x_ref)]
        first += [copy(1 + j, me, (*chip, c), src=x_ref) for j, chip in enumerate(chips)]
        for cp in first:
            cp.start()
        passed = [copy(4 + j, (*chip, c), sibling) for j, chip in enumerate(chips)]
        for j, chip in enumerate(chips):
            copy(1 + j, (*chip, c), me).wait_recv()
            passed[j].start()
        copy(0, sibling, me).wait_recv()
        for j, chip in enumerate(chips):
            copy(4 + j, (*chip, 1 - c), me).wait_recv()
        for cp in first + passed:
            cp.wait_send()
        mine.wait()

    return pl.pallas_call(
        body, name=name,
        out_shape=jax.ShapeDtypeStruct((N_DEV,) + shard.shape, shard.dtype),
        in_specs=[pl.BlockSpec(memory_space=pl.ANY)],
        out_specs=pl.BlockSpec(memory_space=pl.ANY),
        scratch_shapes=[pltpu.SemaphoreType.DMA((N_DEV - 1,)), pltpu.SemaphoreType.DMA((N_DEV - 1,)),
                        pltpu.SemaphoreType.DMA],
        compiler_params=_cparams(),
    )(shard)


N_CHIP = N_DEV // 2
SIBLING_SEMS = [pltpu.SemaphoreType.DMA, pltpu.SemaphoreType.DMA]
CHIP_SEMS = [pltpu.SemaphoreType.DMA((N_CHIP - 1,)), pltpu.SemaphoreType.DMA((N_CHIP - 1,)), pltpu.SemaphoreType.DMA]


def _sibling_exchange(s_ref, o_ref, send_sem, recv_sem):
    x, y, c = _my_pos()
    cp = pltpu.make_async_remote_copy(s_ref, o_ref, send_sem, recv_sem, device_id=(x, y, 1 - c), device_id_type=MESH_ID)
    return cp.start, cp.wait


def _chip_exchange(p_ref, o_ref, send_sems, recv_sems, local_sem):
    x, y, c = _my_pos()
    my_chip = 2 * x + y
    mine = pltpu.make_async_copy(p_ref.at[my_chip], o_ref.at[my_chip], local_sem)
    peers = [(x ^ (k >> 1), y ^ (k & 1)) for k in range(1, N_CHIP)]
    sends = [pltpu.make_async_remote_copy(p_ref.at[2 * px + py], o_ref.at[my_chip], send_sems.at[k], recv_sems.at[k],
                                          device_id=(px, py, c), device_id_type=MESH_ID)
             for k, (px, py) in enumerate(peers)]
    recvs = [pltpu.make_async_remote_copy(p_ref.at[2 * px + py], o_ref.at[2 * px + py], send_sems.at[k], recv_sems.at[k],
                                          device_id=(px, py, c), device_id_type=MESH_ID)
             for k, (px, py) in enumerate(peers)]

    def start():
        mine.start()
        for cp in sends:
            cp.start()

    def wait():
        for cp in recvs:
            cp.wait_recv()
        for cp in sends:
            cp.wait_send()
        mine.wait()

    return start, wait


def exchange_sibling(send, name):
    def body(s_ref, o_ref, send_sem, recv_sem):
        start, wait = _sibling_exchange(s_ref, o_ref, send_sem, recv_sem)
        start()
        wait()

    return pl.pallas_call(
        body, name=name,
        out_shape=jax.ShapeDtypeStruct(send.shape, send.dtype),
        in_specs=[pl.BlockSpec(memory_space=pl.ANY)],
        out_specs=pl.BlockSpec(memory_space=pl.ANY),
        scratch_shapes=SIBLING_SEMS,
        compiler_params=_cparams(),
    )(send)


def exchange_chips(part, name):
    def body(p_ref, o_ref, send_sems, recv_sems, local_sem):
        start, wait = _chip_exchange(p_ref, o_ref, send_sems, recv_sems, local_sem)
        start()
        wait()

    return pl.pallas_call(
        body, name=name,
        out_shape=jax.ShapeDtypeStruct(part.shape, part.dtype),
        in_specs=[pl.BlockSpec(memory_space=pl.ANY)],
        out_specs=pl.BlockSpec(memory_space=pl.ANY),
        scratch_shapes=CHIP_SEMS,
        compiler_params=_cparams(),
    )(part)


def _k_tile(kdim):
    for cand in range(MM_K_TILE, 0, -1024):
        if kdim % cand == 0:
            return cand
    return kdim


def mm_nn(a, b, out_dtype, name):
    m, kdim = a.shape
    n = b.shape[1]
    tm, tn, tk = min(MM_ROW_TILE, m), 1024, _k_tile(kdim)
    nk = kdim // tk

    def body(a_ref, b_ref, o_ref, acc_ref):
        p = _nn(a_ref[...], b_ref[...])
        if nk == 1:
            o_ref[...] = p.astype(o_ref.dtype)
        else:
            k = pl.program_id(2)

            @pl.when(k == 0)
            def _():
                acc_ref[...] = p

            @pl.when(k > 0)
            def _():
                acc_ref[...] += p

            @pl.when(k == nk - 1)
            def _():
                o_ref[...] = acc_ref[...].astype(o_ref.dtype)

    return pl.pallas_call(
        body, name=name, grid=(n // tn, m // tm, nk),
        out_shape=jax.ShapeDtypeStruct((m, n), out_dtype),
        in_specs=[pl.BlockSpec((tm, tk), lambda j, i, k: (i, k)), pl.BlockSpec((tk, tn), lambda j, i, k: (k, j))],
        out_specs=pl.BlockSpec((tm, tn), lambda j, i, k: (i, j)),
        scratch_shapes=[pltpu.VMEM((tm, tn), F32)],
        compiler_params=_cparams(("parallel", "parallel", "arbitrary")),
    )(a, b)


def mm_nt(a, b, out_dtype, name):
    m, kdim = a.shape
    n = b.shape[0]
    tm, tn, tk = min(MM_ROW_TILE, m), 1024, _k_tile(kdim)
    nk = kdim // tk

    def body(a_ref, b_ref, o_ref, acc_ref):
        p = _nt(a_ref[...], b_ref[...])
        if nk == 1:
            o_ref[...] = p.astype(o_ref.dtype)
        else:
            k = pl.program_id(2)

            @pl.when(k == 0)
            def _():
                acc_ref[...] = p

            @pl.when(k > 0)
            def _():
                acc_ref[...] += p

            @pl.when(k == nk - 1)
            def _():
                o_ref[...] = acc_ref[...].astype(o_ref.dtype)

    return pl.pallas_call(
        body, name=name, grid=(n // tn, m // tm, nk),
        out_shape=jax.ShapeDtypeStruct((m, n), out_dtype),
        in_specs=[pl.BlockSpec((tm, tk), lambda j, i, k: (i, k)), pl.BlockSpec((tn, tk), lambda j, i, k: (j, k))],
        out_specs=pl.BlockSpec((tm, tn), lambda j, i, k: (i, j)),
        scratch_shapes=[pltpu.VMEM((tm, tn), F32)],
        compiler_params=_cparams(("parallel", "parallel", "arbitrary")),
    )(a, b)


def mm_nt_gather(a, b, out_dtype, shard, name):
    m, kdim = a.shape
    n = b.shape[0]
    tm, tn = min(MM_ROW_TILE, m), 1024
    assert kdim == 1024
    gj = m // tm
    nsteps = (n // tn) * gj
    forward_step = max(nsteps - 3, 0)

    def body(a_ref, b_ref, x_ref, o_ref, g_ref, send_sems, recv_sems, local_sem):
        step = pl.program_id(0) * gj + pl.program_id(1)
        x, y, c = _my_pos()
        me, sibling = (x, y, c), (x, y, 1 - c)
        chips = [(1 - x, y), (x, 1 - y), (1 - x, 1 - y)]

        def slot(pos):
            return g_ref.at[_flat(pos)]

        def copy(k, block, to, src=None):
            return pltpu.make_async_remote_copy(slot(block) if src is None else src, slot(block), send_sems.at[k],
                                                recv_sems.at[k], device_id=to, device_id_type=MESH_ID)

        mine = pltpu.make_async_copy(x_ref, slot(me), local_sem)
        first = [copy(0, me, sibling, src=x_ref)]
        first += [copy(1 + j, me, (*chip, c), src=x_ref) for j, chip in enumerate(chips)]
        passed = [copy(4 + j, (*chip, c), sibling) for j, chip in enumerate(chips)]

        @pl.when(step == 0)
        def _():
            mine.start()
            for cp in first:
                cp.start()

        o_ref[...] = _nt(a_ref[...], b_ref[...]).astype(o_ref.dtype)

        @pl.when(step == forward_step)
        def _():
            for j, chip in enumerate(chips):
                copy(1 + j, (*chip, c), me).wait_recv()
                passed[j].start()

        @pl.when(step == nsteps - 1)
        def _():
            copy(0, sibling, me).wait_recv()
            for j, chip in enumerate(chips):
                copy(4 + j, (*chip, 1 - c), me).wait_recv()
            for cp in first + passed:
                cp.wait_send()
            mine.wait()

    return pl.pallas_call(
        body, name=name, grid=(n // tn, gj),
        out_shape=[jax.ShapeDtypeStruct((m, n), out_dtype), jax.ShapeDtypeStruct((N_DEV,) + shard.shape, shard.dtype)],
        in_specs=[pl.BlockSpec((tm, kdim), lambda j, i: (i, 0)), pl.BlockSpec((tn, kdim), lambda j, i: (j, 0)),
                  pl.BlockSpec(memory_space=pl.ANY)],
        out_specs=[pl.BlockSpec((tm, tn), lambda j, i: (i, j)), pl.BlockSpec(memory_space=pl.ANY)],
        scratch_shapes=[pltpu.SemaphoreType.DMA((N_DEV - 1,)), pltpu.SemaphoreType.DMA((N_DEV - 1,)),
                        pltpu.SemaphoreType.DMA],
        compiler_params=_cparams(("arbitrary", "arbitrary")),
    )(a, b, shard)


def mm_tn(a, b, name):
    t, ka = a.shape
    n = b.shape[1]
    tt, tka, tn = min(MM_TOKEN_TILE, t), 1024, 1024
    nt = t // tt

    def body(a_ref, b_ref, o_ref, *acc):
        p = _tn(a_ref[...], b_ref[...])
        if nt == 1:
            o_ref[...] = p.astype(o_ref.dtype)
        else:
            acc_ref, s = acc[0], pl.program_id(2)

            @pl.when(s == 0)
            def _():
                acc_ref[...] = p

            @pl.when(s > 0)
            def _():
                acc_ref[...] += p

            @pl.when(s == nt - 1)
            def _():
                o_ref[...] = acc_ref[...].astype(o_ref.dtype)

    return pl.pallas_call(
        body, name=name, grid=(ka // tka, n // tn, nt),
        out_shape=jax.ShapeDtypeStruct((ka, n), BF16),
        in_specs=[pl.BlockSpec((tt, tka), lambda i, j, s: (s, i)), pl.BlockSpec((tt, tn), lambda i, j, s: (s, j))],
        out_specs=pl.BlockSpec((tka, tn), lambda i, j, s: (i, j)),
        scratch_shapes=[] if nt == 1 else [pltpu.VMEM((tka, tn), F32)],
        compiler_params=_cparams(("parallel", "parallel", "arbitrary")),
    )(a, b)


def _tile(t, cap):
    return min(cap, t)


def ln_modulate(x, mod6, shift_row, scale_row, name):
    t = x.shape[0]
    tm = _tile(t, ROW_TILE)

    def body(x_ref, mod_ref, o_ref):
        xh, _ = _ln(x_ref[...])
        sc = mod_ref[scale_row:scale_row + 1, :]
        sh = mod_ref[shift_row:shift_row + 1, :]
        o_ref[...] = (xh * (1.0 + sc) + sh).astype(BF16)

    return pl.pallas_call(
        body, name=name, grid=(t // tm,),
        out_shape=jax.ShapeDtypeStruct((t, D), BF16),
        in_specs=[pl.BlockSpec((tm, D), lambda i: (i, 0)), pl.BlockSpec((6, D), lambda i: (0, 0))],
        out_specs=pl.BlockSpec((tm, D), lambda i: (i, 0)),
        compiler_params=_cparams(("parallel",)),
    )(x, mod6)


def resid_ln(x, h, mod6, gate_row, ln_g, ln_b, name):
    t = x.shape[0]
    tm = _tile(t, ROW_TILE)

    def body(x_ref, h_ref, mod_ref, g_ref, b_ref, o_ref):
        r = ALPHA * x_ref[...] + mod_ref[gate_row:gate_row + 1, :] * h_ref[...]
        rh, _ = _ln(r)
        o_ref[...] = rh * g_ref[...] + b_ref[...]

    row = pl.BlockSpec((tm, D), lambda i: (i, 0))
    vec = pl.BlockSpec((1, D), lambda i: (0, 0))
    return pl.pallas_call(
        body, name=name, grid=(t // tm,),
        out_shape=jax.ShapeDtypeStruct((t, D), F32),
        in_specs=[row, row, pl.BlockSpec((6, D), lambda i: (0, 0)), vec, vec],
        out_specs=row,
        compiler_params=_cparams(("parallel",)),
    )(x, h, mod6, ln_g, ln_b)


def resid_ln_bwd(x, h, mod6, gate_row, ln_g, ln_b, cot, with_loss, name):
    t = x.shape[0]
    tm = _tile(t, ROW_TILE)

    def body(x_ref, h_ref, mod_ref, g_ref, b_ref, c_ref, dh_ref, dx_ref, acc_ref):
        @pl.when(pl.program_id(0) == 0)
        def _():
            acc_ref[...] = jnp.zeros_like(acc_ref)

        gate = mod_ref[gate_row:gate_row + 1, :]
        hv = h_ref[...]
        r = ALPHA * x_ref[...] + gate * hv
        rh, rstd = _ln(r)
        lng = g_ref[...]
        if with_loss:
            diff = rh * lng + b_ref[...] - c_ref[...]
            dxo = diff * (1.0 / D)
            lsum = jnp.sum(_colsum(diff * diff), axis=-1, keepdims=True) * (0.5 / D)
            acc_ref[3:4, :] += jnp.broadcast_to(lsum, (1, D))
        else:
            dxo = c_ref[...]
        acc_ref[1:2, :] += _colsum(dxo * rh)
        acc_ref[2:3, :] += _colsum(dxo)
        dr = _ln_bwd(dxo * lng, rh, rstd)
        acc_ref[0:1, :] += _colsum(dr * hv)
        dh_ref[...] = (gate * dr).astype(BF16)
        dx_ref[...] = ALPHA * dr

    row = pl.BlockSpec((tm, D), lambda i: (i, 0))
    vec = pl.BlockSpec((1, D), lambda i: (0, 0))
    return pl.pallas_call(
        body, name=name, grid=(t // tm,),
        out_shape=[jax.ShapeDtypeStruct((t, D), BF16), jax.ShapeDtypeStruct((t, D), F32),
                   jax.ShapeDtypeStruct((8, D), F32)],
        in_specs=[row, row, pl.BlockSpec((6, D), lambda i: (0, 0)), vec, vec, row],
        out_specs=[row, row, pl.BlockSpec((8, D), lambda i: (0, 0))],
        compiler_params=_cparams(("arbitrary",)),
    )(x, h, mod6, ln_g, ln_b, cot)


def ln_modulate_bwd(x, du, mod6, scale_row, dx_part, name):
    t = x.shape[0]
    tm = _tile(t, ROW_TILE)

    def body(x_ref, du_ref, mod_ref, dp_ref, dx_ref, acc_ref):
        @pl.when(pl.program_id(0) == 0)
        def _():
            acc_ref[...] = jnp.zeros_like(acc_ref)

        xh, rstd = _ln(x_ref[...])
        du_v = du_ref[...]
        sc = mod_ref[scale_row:scale_row + 1, :]
        acc_ref[0:1, :] += _colsum(du_v * xh)
        acc_ref[1:2, :] += _colsum(du_v)
        dx_ref[...] = dp_ref[...] + _ln_bwd(du_v * (1.0 + sc), xh, rstd)

    row = pl.BlockSpec((tm, D), lambda i: (i, 0))
    return pl.pallas_call(
        body, name=name, grid=(t // tm,),
        out_shape=[jax.ShapeDtypeStruct((t, D), F32), jax.ShapeDtypeStruct((8, D), F32)],
        in_specs=[row, row, pl.BlockSpec((6, D), lambda i: (0, 0)), row],
        out_specs=[row, pl.BlockSpec((8, D), lambda i: (0, 0))],
        compiler_params=_cparams(("arbitrary",)),
    )(x, du, mod6, dx_part)


def merge_gates(ya, yb, proj):
    t = ya.shape[0]
    tm = _tile(t, ROW_TILE)

    def body(ya_ref, yb_ref, ga_ref, gb_ref, o_ref):
        o_ref[...] = (_sigmoid(ga_ref[...]) * ya_ref[...] + _sigmoid(gb_ref[...]) * yb_ref[...]).astype(BF16)

    row = pl.BlockSpec((tm, D), lambda i: (i, 0))
    return pl.pallas_call(
        body, name="merge_gates", grid=(t // tm,),
        out_shape=jax.ShapeDtypeStruct((t, D), BF16),
        in_specs=[row, row, pl.BlockSpec((tm, D), lambda i: (i, 9)), pl.BlockSpec((tm, D), lambda i: (i, 10))],
        out_specs=row,
        compiler_params=_cparams(("parallel",)),
    )(ya, yb, proj, proj)


def merge_gates_bwd(dm, ya, yb, proj):
    t = ya.shape[0]
    tm = _tile(t, ROW_TILE)

    def body(dm_ref, ya_ref, yb_ref, ga_ref, gb_ref, dya_ref, dyb_ref, dga_ref, dgb_ref):
        dmv = dm_ref[...]
        sa = _sigmoid(ga_ref[...])
        sb = _sigmoid(gb_ref[...])
        dya_ref[...] = (dmv * sa).astype(BF16)
        dyb_ref[...] = (dmv * sb).astype(BF16)
        dga_ref[...] = (dmv * ya_ref[...] * sa * (1.0 - sa)).astype(BF16)
        dgb_ref[...] = (dmv * yb_ref[...] * sb * (1.0 - sb)).astype(BF16)

    row = pl.BlockSpec((tm, D), lambda i: (i, 0))
    return pl.pallas_call(
        body, name="merge_gates_bwd", grid=(t // tm,),
        out_shape=[jax.ShapeDtypeStruct((t, D), BF16)] * 4,
        in_specs=[row, row, row, pl.BlockSpec((tm, D), lambda i: (i, 9)), pl.BlockSpec((tm, D), lambda i: (i, 10))],
        out_specs=[row] * 4,
        compiler_params=_cparams(("parallel",)),
    )(dm, ya, yb, proj, proj)


def swiglu_act(gu):
    t = gu.shape[0]
    tm = _tile(t, FFN_ROW_TILE)

    def body(gu_ref, o_ref):
        for j in range(D_FF_PAD // D):
            g = gu_ref[:, j * D:(j + 1) * D]
            u = gu_ref[:, D_FF_PAD + j * D:D_FF_PAD + (j + 1) * D]
            o_ref[:, j * D:(j + 1) * D] = (g * _sigmoid(g) * u).astype(BF16)

    return pl.pallas_call(
        body, name="swiglu_act", grid=(t // tm,),
        out_shape=jax.ShapeDtypeStruct((t, D_FF_PAD), BF16),
        in_specs=[pl.BlockSpec((tm, 2 * D_FF_PAD), lambda i: (i, 0))],
        out_specs=pl.BlockSpec((tm, D_FF_PAD), lambda i: (i, 0)),
        compiler_params=_cparams(("parallel",)),
    )(gu)


def swiglu_act_bwd(gu, dact):
    t = gu.shape[0]
    tm = _tile(t, FFN_ROW_TILE)

    def body(gu_ref, da_ref, o_ref):
        for j in range(D_FF_PAD // D):
            g = gu_ref[:, j * D:(j + 1) * D]
            u = gu_ref[:, D_FF_PAD + j * D:D_FF_PAD + (j + 1) * D]
            da = da_ref[:, j * D:(j + 1) * D]
            s = _sigmoid(g)
            o_ref[:, j * D:(j + 1) * D] = (da * u * _dsilu(g, s)).astype(BF16)
            o_ref[:, D_FF_PAD + j * D:D_FF_PAD + (j + 1) * D] = (da * g * s).astype(BF16)

    return pl.pallas_call(
        body, name="swiglu_act_bwd", grid=(t // tm,),
        out_shape=jax.ShapeDtypeStruct((t, 2 * D_FF_PAD), BF16),
        in_specs=[pl.BlockSpec((tm, 2 * D_FF_PAD), lambda i: (i, 0)), pl.BlockSpec((tm, D_FF_PAD), lambda i: (i, 0))],
        out_specs=pl.BlockSpec((tm, 2 * D_FF_PAD), lambda i: (i, 0)),
        compiler_params=_cparams(("parallel",)),
    )(gu, dact)


def _hgrn_chunk_terms(q, fl, lbv, tril_f):
    sig = _sigmoid(fl)
    f = lbv + (1.0 - lbv) * sig
    lam = jnp.log(f)
    k = 1.0 - f
    sq = _sigmoid(q)
    qt = q * sq * Q_SCALE
    bc = _sel(_nn, lam, tril_f, 3, x_first=False)
    bmid = bc[CHUNK // 2 - 1:CHUNK // 2, :]
    bl = bc[CHUNK - 1:CHUNK, :]
    eq = jnp.exp(jnp.minimum(bc - bmid, EXP_CLIP))
    ek = jnp.exp(jnp.minimum(bmid - bc, EXP_CLIP))
    eb = jnp.exp(bc)
    ekl = jnp.exp(bl - bc)
    ebl = jnp.exp(bl)
    return sig, f, k, sq, qt, eq, ek, eb, ekl, ebl


def hgrn_fwd(proj, lb, gnorm):
    t = proj.shape[0]
    tb = _tile(t, TOKEN_BLOCK)
    ncb = tb // CHUNK

    def body(q_ref, f_ref, i_ref, g_ref, lb_ref, gn_ref, oa_ref, oraw_ref, st_ref, state):
        @pl.when(pl.program_id(1) == 0)
        def _():
            state[...] = jnp.zeros_like(state)

        lbv = lb_ref[...]
        gn = gn_ref[...]
        mask = _tri(CHUNK)
        tril_f = mask.astype(BF16)

        def chunk(c, carry):
            sl = pl.ds(pl.multiple_of(c * CHUNK, CHUNK), CHUNK)
            q, fl, v, g = q_ref[sl, :], f_ref[sl, :], i_ref[sl, :], g_ref[sl, :]
            sig, f, k, sq, qt, eq, ek, eb, ekl, ebl = _hgrn_chunk_terms(q, fl, lbv, tril_f)
            a = jnp.where(mask, _nt((qt * eq).astype(BF16), (k * ek).astype(BF16)), 0.0)
            st = state[...]
            st_ref[0, c] = st
            vb = v.astype(BF16)
            o = _nn(a.astype(BF16), vb) + _nt((qt * eb).astype(BF16), st.astype(BF16))
            state[...] = st * ebl + _tn(vb, (k * ekl).astype(BF16))
            oraw_ref[sl, :] = o
            rn = o * lax.rsqrt(jnp.mean(o * o, axis=-1, keepdims=True) + RMS_EPS)
            oa_ref[sl, :] = (rn * gn * g * _sigmoid(g)).astype(BF16)
            return carry

        lax.fori_loop(0, ncb, chunk, 0, unroll=min(CHUNK_UNROLL, ncb))

    def col(block):
        return pl.BlockSpec((tb, HK), lambda h, j: (j, block * N_HEADS_A + h))

    return pl.pallas_call(
        body, name="hgrn_fwd", grid=(N_HEADS_A, t // tb),
        out_shape=[jax.ShapeDtypeStruct((t, D), BF16), jax.ShapeDtypeStruct((t, D), F32),
                   jax.ShapeDtypeStruct((N_HEADS_A, t // CHUNK, HK, HK), F32)],
        in_specs=[col(0), col(1), col(2), col(3), pl.BlockSpec((1, HK), lambda h, j: (0, h)),
                  pl.BlockSpec((1, HK), lambda h, j: (0, 0))],
        out_specs=[pl.BlockSpec((tb, HK), lambda h, j: (j, h)), pl.BlockSpec((tb, HK), lambda h, j: (j, h)),
                   pl.BlockSpec((1, ncb, HK, HK), lambda h, j: (h, j, 0, 0))],
        scratch_shapes=[pltpu.VMEM((HK, HK), F32)],
        compiler_params=_cparams(("parallel", "arbitrary")),
    )(proj, proj, proj, proj, lb, gnorm)


def hgrn_bwd(proj, lb, gnorm, o_raw, doa, states, give):
    t = proj.shape[0]
    tb = _tile(t, TOKEN_BLOCK)
    ncb = tb // CHUNK
    nb = t // tb

    def body(q_ref, f_ref, i_ref, g_ref, lb_ref, gn_ref, oraw_ref, doa_ref, st_ref, give_ref,
             dq_ref, df_ref, di_ref, dg_ref, dlb_ref, dgn_ref, got_ref, dstate, send_sem, recv_sem):
        h, j = pl.program_id(0), pl.program_id(1)
        swap_start, swap_wait = _sibling_exchange(give_ref, got_ref, send_sem, recv_sem)

        @pl.when((h == 0) & (j == 0))
        def _():
            swap_start()

        @pl.when(j == 0)
        def _():
            dstate[...] = jnp.zeros_like(dstate)
            dlb_ref[...] = jnp.zeros_like(dlb_ref)

        @pl.when((j == 0) & (h == 0))
        def _():
            dgn_ref[...] = jnp.zeros_like(dgn_ref)

        lbv = lb_ref[...]
        gn = gn_ref[...]
        mask = _tri(CHUNK)
        mask_t = _tri(CHUNK, upper=True)
        tril_f = mask.astype(BF16)
        triu_f = mask_t.astype(BF16)

        def chunk(i, c0):
            c = ncb - 1 - i
            sl = pl.ds(pl.multiple_of(c * CHUNK, CHUNK), CHUNK)
            q, fl, v, g = q_ref[sl, :], f_ref[sl, :], i_ref[sl, :], g_ref[sl, :]
            sig, f, k, sq, qt, eq, ek, eb, ekl, ebl = _hgrn_chunk_terms(q, fl, lbv, tril_f)
            qe = (qt * eq).astype(BF16)
            ke = (k * ek).astype(BF16)
            st32 = st_ref[0, c]
            st = st32.astype(BF16)
            dst = dstate[...]
            dstb = dst.astype(BF16)
            o = oraw_ref[sl, :]
            rstd = lax.rsqrt(jnp.mean(o * o, axis=-1, keepdims=True) + RMS_EPS)
            rn = o * rstd
            sgm = _sigmoid(g)
            sg = g * sgm
            doa_v = doa_ref[sl, :]
            drn = doa_v * gn * sg
            dgn_ref[...] += _colsum(doa_v * rn * sg)
            dg_ref[sl, :] = (doa_v * rn * gn * _dsilu(g, sgm)).astype(BF16)
            do = rstd * (drn - rn * jnp.mean(drn * rn, axis=-1, keepdims=True))
            dob = do.astype(BF16)
            vb = v.astype(BF16)
            da = jnp.where(mask, _nt(dob, vb), 0.0).astype(BF16)
            da_t = jnp.where(mask_t, _nt(vb, dob), 0.0).astype(BF16)
            a_t = jnp.where(mask_t, _nt(ke, qe), 0.0).astype(BF16)
            kl = (k * ekl).astype(BF16)
            qb = (qt * eb).astype(BF16)
            dq_in = _nn(da, ke)
            dk_in = _nn(da_t, qe)
            dq_out = eb * _nn(dob, st)
            dk_out = ekl * _nn(vb, dstb)
            dqt = eq * dq_in + dq_out
            dk = ek * dk_in + dk_out
            dv = _nn(a_t, dob) + _nt(kl, dstb)
            dstate[...] = dst * ebl + _tn(dob, qb)
            dbig = qe.astype(F32) * dq_in - ke.astype(F32) * dk_in + qt * dq_out - k * dk_out
            beyond = _colsum(k * dk_out) + ebl * _colsum(dst * st32)
            dlam = _sel(_nn, dbig, triu_f, 3, x_first=False) + beyond
            df = dlam / f - dk
            df_ref[sl, :] = (df * (1.0 - lbv) * sig * (1.0 - sig)).astype(BF16)
            dlb_ref[...] += _colsum(df * (1.0 - sig))
            dq_ref[sl, :] = (dqt * Q_SCALE * _dsilu(q, sq)).astype(BF16)
            di_ref[sl, :] = dv.astype(BF16)
            return c0

        lax.fori_loop(0, ncb, chunk, 0, unroll=min(CHUNK_UNROLL, ncb))

        @pl.when((h == N_HEADS_A - 1) & (j == nb - 1))
        def _():
            swap_wait()

    def col(block):
        return pl.BlockSpec((tb, HK), lambda h, j: (nb - 1 - j, block * N_HEADS_A + h))

    hcol = pl.BlockSpec((tb, HK), lambda h, j: (nb - 1 - j, h))
    hbm = pl.BlockSpec(memory_space=pl.ANY)
    return pl.pallas_call(
        body, name="hgrn_bwd", grid=(N_HEADS_A, nb),
        out_shape=[jax.ShapeDtypeStruct((t, D), BF16)] * 4 + [jax.ShapeDtypeStruct((1, D), F32),
                                                                jax.ShapeDtypeStruct((1, HK), F32),
                                                                jax.ShapeDtypeStruct(give.shape, give.dtype)],
        in_specs=[col(0), col(1), col(2), col(3), pl.BlockSpec((1, HK), lambda h, j: (0, h)),
                  pl.BlockSpec((1, HK), lambda h, j: (0, 0)), hcol, hcol,
                  pl.BlockSpec((1, ncb, HK, HK), lambda h, j: (h, nb - 1 - j, 0, 0)), hbm],
        out_specs=[hcol] * 4 + [pl.BlockSpec((1, HK), lambda h, j: (0, h)), pl.BlockSpec((1, HK), lambda h, j: (0, 0)),
                                hbm],
        scratch_shapes=[pltpu.VMEM((HK, HK), F32)] + SIBLING_SEMS,
        compiler_params=_cparams(("arbitrary", "arbitrary")),
    )(proj, proj, proj, proj, lb, gnorm, o_raw, doa, states, give)


CONV_BLOCK0 = 6
CONV_TAPS = 4
HALO = 8


def conv_fwd(proj, conv_w, conv_b):
    t = proj.shape[0]
    tm = _tile(t, ROW_TILE)
    r = tm // HALO

    def body(x_ref, halo_ref, w_ref, b_ref, o_ref):
        i = pl.program_id(1)
        halo = jnp.where(i > 0, halo_ref[...], 0.0)
        ext = jnp.concatenate([halo, x_ref[...]], axis=0)
        pre = b_ref[...] + w_ref[CONV_TAPS - 1:CONV_TAPS, :] * ext[HALO:, :]
        for tap in range(CONV_TAPS - 1):
            pre = pre + w_ref[tap:tap + 1, :] * pltpu.roll(ext, CONV_TAPS - 1 - tap, axis=0)[HALO:, :]
        o_ref[...] = pre * _sigmoid(pre)

    return pl.pallas_call(
        body, name="conv_fwd", grid=(CONV_DIM // D, t // tm),
        out_shape=jax.ShapeDtypeStruct((t, CONV_DIM), F32),
        in_specs=[pl.BlockSpec((tm, D), lambda cb, i: (i, CONV_BLOCK0 + cb)),
                  pl.BlockSpec((HALO, D), lambda cb, i: (jnp.maximum(i * r - 1, 0), CONV_BLOCK0 + cb)),
                  pl.BlockSpec((CONV_TAPS, D), lambda cb, i: (0, cb)), pl.BlockSpec((1, D), lambda cb, i: (0, cb))],
        out_specs=pl.BlockSpec((tm, D), lambda cb, i: (i, cb)),
        compiler_params=_cparams(("parallel", "parallel")),
    )(proj, proj, conv_w, conv_b)


def conv_bwd(proj, dxc, conv_w, conv_b):
    t = proj.shape[0]
    tm = _tile(t, ROW_TILE)
    r = tm // HALO
    n = t // tm
    last_halo = t // HALO - 1

    def body(x_ref, prev_ref, next_ref, d_ref, dnext_ref, w_ref, b_ref, dx_ref, dw_ref, db_ref):
        i = pl.program_id(1)

        @pl.when(i == 0)
        def _():
            dw_ref[...] = jnp.zeros_like(dw_ref)
            db_ref[...] = jnp.zeros_like(db_ref)

        prev = jnp.where(i > 0, prev_ref[...], 0.0)
        ext = jnp.concatenate([prev, x_ref[...], next_ref[...]], axis=0)
        shifted = [pltpu.roll(ext, CONV_TAPS - 1 - tap, axis=0)[HALO:, :] for tap in range(CONV_TAPS - 1)]
        shifted.append(ext[HALO:, :])
        pre = b_ref[...]
        for tap in range(CONV_TAPS):
            pre = pre + w_ref[tap:tap + 1, :] * shifted[tap]
        s = _sigmoid(pre)
        d_ext = jnp.concatenate([d_ref[...], jnp.where(i < n - 1, dnext_ref[...], 0.0)], axis=0)
        dpre = d_ext * _dsilu(pre, s)
        dx = w_ref[CONV_TAPS - 1:CONV_TAPS, :] * dpre[:tm, :]
        for tap in range(CONV_TAPS - 1):
            back = CONV_TAPS - 1 - tap
            dx = dx + w_ref[tap:tap + 1, :] * pltpu.roll(dpre, tm + HALO - back, axis=0)[:tm, :]
        dx_ref[...] = dx.astype(BF16)
        dp = dpre[:tm, :]
        db_ref[...] += _colsum(dp)
        for tap in range(CONV_TAPS):
            dw_ref[tap:tap + 1, :] += _colsum(dp * shifted[tap][:tm, :])

    return pl.pallas_call(
        body, name="conv_bwd", grid=(CONV_DIM // D, n),
        out_shape=[jax.ShapeDtypeStruct((t, CONV_DIM), BF16), jax.ShapeDtypeStruct((8, CONV_DIM), F32),
                   jax.ShapeDtypeStruct((1, CONV_DIM), F32)],
        in_specs=[pl.BlockSpec((tm, D), lambda cb, i: (i, CONV_BLOCK0 + cb)),
                  pl.BlockSpec((HALO, D), lambda cb, i: (jnp.maximum(i * r - 1, 0), CONV_BLOCK0 + cb)),
                  pl.BlockSpec((HALO, D), lambda cb, i: (jnp.minimum((i + 1) * r, last_halo), CONV_BLOCK0 + cb)),
                  pl.BlockSpec((tm, D), lambda cb, i: (i, cb)),
                  pl.BlockSpec((HALO, D), lambda cb, i: (jnp.minimum((i + 1) * r, last_halo), cb)),
                  pl.BlockSpec((CONV_TAPS, D), lambda cb, i: (0, cb)), pl.BlockSpec((1, D), lambda cb, i: (0, cb))],
        out_specs=[pl.BlockSpec((tm, D), lambda cb, i: (i, cb)), pl.BlockSpec((8, D), lambda cb, i: (0, cb)),
                   pl.BlockSpec((1, D), lambda cb, i: (0, cb))],
        compiler_params=_cparams(("parallel", "arbitrary")),
    )(proj, proj, proj, dxc, dxc, conv_w, conv_b)


Z_BLOCK0 = 8
DT_BLOCK0 = 88
B_BLOCK0 = 16
C_BLOCK0 = 20


def _head_expand():
    e = np.zeros((N_STATE, GROUP_W), np.float32)
    for hh in range(HEADS_PER_GROUP):
        e[hh, hh * HEAD_P:(hh + 1) * HEAD_P] = 1.0
    return jnp.asarray(e, BF16)


def _ssd_chunk_terms(dt, bias, alog, expand, tril_f, eye):
    dtb = dt + bias
    delta = jnp.maximum(dtb, 0.0) + jnp.log(1.0 + jnp.exp(-jnp.abs(dtb)))
    ea = jnp.exp(alog)
    a = -ea * delta
    acum = _sel(_nn, a, tril_f, 3, x_first=False)
    delta_e = _sel(_nn, delta, expand, 2)
    acum_e = _sel(_nn, acum, expand, 3)
    acum_t = _sel(_nt, acum, eye, 3, x_first=False)
    return dtb, delta, ea, a, acum, delta_e, acum_e, acum_t


def ssd_fwd(proj, xc, alog4, bias4, dskip4, wnorm, expand):
    t = proj.shape[0]
    tb = _tile(t, TOKEN_BLOCK)
    ncb = tb // SSD_CHUNK

    def body(xs_ref, b_ref, c_ref, dt_ref, z_ref, alog_ref, bias_ref, dsk_ref, wn_ref, e_ref, ob_ref, st_ref, state):
        @pl.when(pl.program_id(1) == 0)
        def _():
            state[...] = jnp.zeros_like(state)

        expand = e_ref[...]
        mask = _tri(SSD_CHUNK)
        tril_f = mask.astype(BF16)
        eye = (lax.broadcasted_iota(jnp.int32, (N_STATE, N_STATE), 0) ==
               lax.broadcasted_iota(jnp.int32, (N_STATE, N_STATE), 1)).astype(BF16)
        alog, bias = alog_ref[0], bias_ref[0]
        d_e = _sel(_nn, jnp.broadcast_to(dsk_ref[0], (8, N_STATE)), expand, 3)[0:1, :]
        wn = wn_ref[...]

        def chunk(c, carry):
            sl = pl.ds(pl.multiple_of(c * SSD_CHUNK, SSD_CHUNK), SSD_CHUNK)
            xs, bm, cm, dt, z = xs_ref[sl, :], b_ref[sl, :], c_ref[sl, :], dt_ref[sl, :], z_ref[sl, :]
            dtb, delta, ea, a, acum, delta_e, acum_e, acum_t = _ssd_chunk_terms(dt, bias, alog, expand, tril_f, eye)
            alast_e = acum_e[SSD_CHUNK - 1:SSD_CHUNK, :]
            xd = xs * delta_e
            xdb = xd.astype(BF16)
            cb_, bb_ = cm.astype(BF16), bm.astype(BF16)
            cbm = _nt(cb_, bb_)
            ys = []
            for hh in range(HEADS_PER_GROUP):
                lh = jnp.where(mask, jnp.exp(jnp.minimum(acum[:, hh:hh + 1] - acum_t[hh:hh + 1, :], 0.0)), 0.0)
                ys.append(_nn((cbm * lh).astype(BF16), xdb[:, hh * HEAD_P:(hh + 1) * HEAD_P]))
            st = state[...]
            st_ref[0, c] = st
            y = jnp.concatenate(ys, axis=1) + _nn(cb_, st.astype(BF16)) * jnp.exp(acum_e) + xs * d_e
            state[...] = st * jnp.exp(alast_e) + _tn(bb_, (xd * jnp.exp(alast_e - acum_e)).astype(BF16))
            yg = y * z * _sigmoid(z)
            ob_ref[sl, :] = (yg * lax.rsqrt(jnp.mean(yg * yg, axis=-1, keepdims=True) + RMS_EPS) * wn).astype(BF16)
            return carry

        lax.fori_loop(0, ncb, chunk, 0, unroll=min(CHUNK_UNROLL, ncb))

    small = pl.BlockSpec((1, 1, N_STATE), lambda g, j: (g, 0, 0))
    return pl.pallas_call(
        body, name="ssd_fwd", grid=(N_GROUPS, t // tb),
        out_shape=[jax.ShapeDtypeStruct((t, B_INNER), BF16),
                   jax.ShapeDtypeStruct((N_GROUPS, t // SSD_CHUNK, N_STATE, GROUP_W), F32)],
        in_specs=[pl.BlockSpec((tb, GROUP_W), lambda g, j: (j, g)),
                  pl.BlockSpec((tb, N_STATE), lambda g, j: (j, B_BLOCK0 + g)),
                  pl.BlockSpec((tb, N_STATE), lambda g, j: (j, C_BLOCK0 + g)),
                  pl.BlockSpec((tb, N_STATE), lambda g, j: (j, DT_BLOCK0 + g)),
                  pl.BlockSpec((tb, GROUP_W), lambda g, j: (j, Z_BLOCK0 + g)),
                  small, small, small, pl.BlockSpec((1, GROUP_W), lambda g, j: (0, g)),
                  pl.BlockSpec((N_STATE, GROUP_W), lambda g, j: (0, 0))],
        out_specs=[pl.BlockSpec((tb, GROUP_W), lambda g, j: (j, g)),
                   pl.BlockSpec((1, ncb, N_STATE, GROUP_W), lambda g, j: (g, j, 0, 0))],
        scratch_shapes=[pltpu.VMEM((N_STATE, GROUP_W), F32)],
        compiler_params=_cparams(("parallel", "arbitrary")),
    )(xc, xc, xc, proj, proj, alog4, bias4, dskip4, wnorm, expand)


def ssd_bwd(proj, xc, alog4, bias4, dskip4, wnorm, expand, dob, states, part):
    t = proj.shape[0]
    tb = _tile(t, TOKEN_BLOCK)
    ncb = tb // SSD_CHUNK
    nb = t // tb

    def body(xs_ref, b_ref, c_ref, dt_ref, z_ref, alog_ref, bias_ref, dsk_ref, wn_ref, e_ref, dob_ref, st_ref, part_ref,
             dxs_ref, db_ref, dc_ref, dz_ref, ddt_ref, dwn_ref, dalog_ref, dbias_ref, ddsk_ref, parts_ref, dstate,
             send_sems, recv_sems, local_sem):
        xchg_start, xchg_wait = _chip_exchange(part_ref, parts_ref, send_sems, recv_sems, local_sem)

        @pl.when((pl.program_id(0) == 0) & (pl.program_id(1) == 0))
        def _():
            xchg_start()

        @pl.when(pl.program_id(1) == 0)
        def _():
            dstate[...] = jnp.zeros_like(dstate)
            dwn_ref[...] = jnp.zeros_like(dwn_ref)
            dalog_ref[...] = jnp.zeros_like(dalog_ref)
            dbias_ref[...] = jnp.zeros_like(dbias_ref)
            ddsk_ref[...] = jnp.zeros_like(ddsk_ref)

        expand = e_ref[...]
        mask = _tri(SSD_CHUNK)
        mask_t = _tri(SSD_CHUNK, upper=True)
        tril_f = mask.astype(BF16)
        triu_f = mask_t.astype(BF16)
        eye = (lax.broadcasted_iota(jnp.int32, (N_STATE, N_STATE), 0) ==
               lax.broadcasted_iota(jnp.int32, (N_STATE, N_STATE), 1)).astype(BF16)
        alog, bias = alog_ref[0], bias_ref[0]
        d_e = _sel(_nn, jnp.broadcast_to(dsk_ref[0], (8, N_STATE)), expand, 3)[0:1, :]
        wn = wn_ref[...]

        def chunk(i, c0):
            c = ncb - 1 - i
            sl = pl.ds(pl.multiple_of(c * SSD_CHUNK, SSD_CHUNK), SSD_CHUNK)
            xs, bm, cm, dt, z = xs_ref[sl, :], b_ref[sl, :], c_ref[sl, :], dt_ref[sl, :], z_ref[sl, :]
            dtb, delta, ea, a, acum, delta_e, acum_e, acum_t = _ssd_chunk_terms(dt, bias, alog, expand, tril_f, eye)
            alast_e = acum_e[SSD_CHUNK - 1:SSD_CHUNK, :]
            eacum = jnp.exp(acum_e)
            wl = jnp.exp(alast_e - acum_e)
            xd = xs * delta_e
            xdb = xd.astype(BF16)
            cb_, bb_ = cm.astype(BF16), bm.astype(BF16)
            cbm = _nt(cb_, bb_)
            cbm_t = _nt(bb_, cb_)
            st32 = st_ref[0, c]
            stb = st32.astype(BF16)
            dst = dstate[...]
            dstb = dst.astype(BF16)
            lhs, lhts, ys = [], [], []
            for hh in range(HEADS_PER_GROUP):
                col, row = acum[:, hh:hh + 1], acum_t[hh:hh + 1, :]
                lh = jnp.where(mask, jnp.exp(jnp.minimum(col - row, 0.0)), 0.0)
                lht = jnp.where(mask_t, jnp.exp(jnp.minimum(row - col, 0.0)), 0.0)
                lhs.append(lh)
                lhts.append(lht)
                ys.append(_nn((cbm * lh).astype(BF16), xdb[:, hh * HEAD_P:(hh + 1) * HEAD_P]))
            y_in = jnp.concatenate(ys, axis=1)
            y_out = _nn(cb_, stb) * eacum
            y = y_in + y_out + xs * d_e
            sgz = _sigmoid(z)
            sz = z * sgz
            yg = y * sz
            rstd = lax.rsqrt(jnp.mean(yg * yg, axis=-1, keepdims=True) + RMS_EPS)
            nrm = yg * rstd
            dob_v = dob_ref[sl, :]
            dn = dob_v * wn
            dwn_ref[...] += _colsum(dob_v * nrm)
            dyg = rstd * (dn - nrm * jnp.mean(dn * nrm, axis=-1, keepdims=True))
            dy = dyg * sz
            dz_ref[sl, :] = (dyg * y * _dsilu(z, sgz)).astype(BF16)
            dyb = dy.astype(BF16)
            dxds = []
            dcb = jnp.zeros((SSD_CHUNK, SSD_CHUNK), F32)
            dcb_t = jnp.zeros((SSD_CHUNK, SSD_CHUNK), F32)
            for hh in range(HEADS_PER_GROUP):
                hs = slice(hh * HEAD_P, (hh + 1) * HEAD_P)
                dy_h, x_h = dyb[:, hs], xdb[:, hs]
                dxds.append(_nn((cbm_t * lhts[hh]).astype(BF16), dy_h))
                dcb = dcb + _nt(dy_h, x_h) * lhs[hh]
                dcb_t = dcb_t + _nt(x_h, dy_h) * lhts[hh]
            dye = (dy * eacum).astype(BF16)
            xw = (xd * wl).astype(BF16)
            dxd_in = jnp.concatenate(dxds, axis=1)
            dxd_out = wl * _nn(bb_, dstb)
            dxd = dxd_in + dxd_out
            dc_ref[sl, :] = _nn(dcb.astype(BF16), bb_) + _nt(dye, stb)
            db_ref[sl, :] = _nn(dcb_t.astype(BF16), cb_) + _nt(xw, dstb)
            dstate[...] = dst * jnp.exp(alast_e) + _tn(cb_, dye)
            col_out = xd * dxd_out
            dac = _sel(_nt, dyb.astype(F32) * y_in - xdb.astype(F32) * dxd_in + dy * y_out - col_out, expand, 3)
            beyond = _colsum(col_out) + jnp.exp(alast_e) * _colsum(dst * st32)
            da = (_sel(_nn, dac, triu_f, 3, x_first=False) +
                  _sel(_nt, jnp.broadcast_to(beyond, (8, GROUP_W)), expand, 3)[0:1, :])
            ddelta = _sel(_nt, dxd * xs, expand, 2) - da * ea
            dalog_ref[0] += _colsum(da * a)
            ddtb = ddelta * _sigmoid(dtb)
            dbias_ref[0] += _colsum(ddtb)
            ddt_ref[sl, :] = ddtb.astype(BF16)
            ddsk_ref[0] += _sel(_nt, jnp.broadcast_to(_colsum(dy * xs), (8, GROUP_W)), expand, 3)[0:1, :]
            dxs_ref[sl, :] = dxd * delta_e + dy * d_e
            return c0

        lax.fori_loop(0, ncb, chunk, 0, unroll=min(CHUNK_UNROLL, ncb))

        @pl.when((pl.program_id(0) == N_GROUPS - 1) & (pl.program_id(1) == nb - 1))
        def _():
            xchg_wait()

    small = pl.BlockSpec((1, 1, N_STATE), lambda g, j: (g, 0, 0))
    wide = pl.BlockSpec((tb, GROUP_W), lambda g, j: (nb - 1 - j, g))
    narrow = pl.BlockSpec((tb, N_STATE), lambda g, j: (nb - 1 - j, g))
    hbm = pl.BlockSpec(memory_space=pl.ANY)
    return pl.pallas_call(
        body, name="ssd_bwd", grid=(N_GROUPS, nb),
        out_shape=[jax.ShapeDtypeStruct((t, B_INNER), F32), jax.ShapeDtypeStruct((t, GROUP_W), F32),
                   jax.ShapeDtypeStruct((t, GROUP_W), F32), jax.ShapeDtypeStruct((t, B_INNER), BF16),
                   jax.ShapeDtypeStruct((t, GROUP_W), BF16), jax.ShapeDtypeStruct((1, B_INNER), F32),
                   jax.ShapeDtypeStruct((N_GROUPS, 1, N_STATE), F32), jax.ShapeDtypeStruct((N_GROUPS, 1, N_STATE), F32),
                   jax.ShapeDtypeStruct((N_GROUPS, 1, N_STATE), F32), jax.ShapeDtypeStruct(part.shape, part.dtype)],
        in_specs=[wide,
                  pl.BlockSpec((tb, N_STATE), lambda g, j: (nb - 1 - j, B_BLOCK0 + g)),
                  pl.BlockSpec((tb, N_STATE), lambda g, j: (nb - 1 - j, C_BLOCK0 + g)),
                  pl.BlockSpec((tb, N_STATE), lambda g, j: (nb - 1 - j, DT_BLOCK0 + g)),
                  pl.BlockSpec((tb, GROUP_W), lambda g, j: (nb - 1 - j, Z_BLOCK0 + g)),
                  small, small, small, pl.BlockSpec((1, GROUP_W), lambda g, j: (0, g)),
                  pl.BlockSpec((N_STATE, GROUP_W), lambda g, j: (0, 0)), wide,
                  pl.BlockSpec((1, ncb, N_STATE, GROUP_W), lambda g, j: (g, nb - 1 - j, 0, 0)), hbm],
        out_specs=[wide, narrow, narrow, wide, narrow, pl.BlockSpec((1, GROUP_W), lambda g, j: (0, g)),
                   small, small, small, hbm],
        scratch_shapes=[pltpu.VMEM((N_STATE, GROUP_W), F32)] + CHIP_SEMS,
        compiler_params=_cparams(("arbitrary", "arbitrary")),
    )(xc, xc, xc, proj, proj, alog4, bias4, dskip4, wnorm, expand, dob, states, part)


def lower_bound_fwd(hgrn_lb):
    def body(a_ref, o_ref):
        a0, a1 = a_ref[0:1, :], a_ref[1:2, :]
        m = jnp.maximum(a0, a1)
        e0, e1 = jnp.exp(a0 - m), jnp.exp(a1 - m)
        o_ref[...] = e0 / (e0 + e1)

    return pl.pallas_call(body, name="lower_bound_fwd", out_shape=jax.ShapeDtypeStruct((1, D), F32))(hgrn_lb)


def ada_weight_grad(c_all, dmod_cols):
    def body(c_ref, d_ref, o_ref):
        cval = c_ref[...]
        o_ref[...] = _tn(cval * _sigmoid(cval), d_ref[...], HI)

    return pl.pallas_call(body, name="ada_weight_grad",
                          out_shape=jax.ShapeDtypeStruct((D, dmod_cols.shape[1]), F32))(c_all, dmod_cols)


def reduce_small(gathered, hgrn_lb, dlb_off):
    n = gathered.shape[2]

    def body(g_ref, a_ref, o_ref, glb_ref):
        s = g_ref[0]
        for d in range(1, N_DEV):
            s = s + g_ref[d]
        o_ref[...] = s
        a0, a1 = a_ref[0:1, :], a_ref[1:2, :]
        m = jnp.maximum(a0, a1)
        e0, e1 = jnp.exp(a0 - m), jnp.exp(a1 - m)
        p0 = e0 / (e0 + e1)
        tq = s[:, dlb_off:dlb_off + D] * p0 * (1.0 - p0)
        glb_ref[0:1, :] = tq
        glb_ref[1:2, :] = -tq

    return pl.pallas_call(body, name="reduce_small",
                          out_shape=[jax.ShapeDtypeStruct((1, n), F32), jax.ShapeDtypeStruct((2, D), F32)])(gathered, hgrn_lb)


def _adam_math(w, g, m, v):
    m2 = ADAM_B1 * m + (1.0 - ADAM_B1) * g
    v2 = ADAM_B2 * v + (1.0 - ADAM_B2) * (g * g)
    m_hat = m2 / (1.0 - ADAM_B1 ** ADAM_STEP)
    v_hat = v2 / (1.0 - ADAM_B2 ** ADAM_STEP)
    delta = -ADAM_LR * (m_hat / (jnp.sqrt(v_hat) + ADAM_EPS) + ADAM_WD * w)
    return delta, m2, v2


def _row_tile(rows, mult=8, cap=128):
    for cand in range(cap - cap % mult, 0, -mult):
        if rows % cand == 0:
            return cand
    return rows


def sum_parts(parts, name):
    n, rows, cols = parts.shape
    tr = _row_tile(rows, 16, 256)

    def body(p_ref, o_ref):
        s = p_ref[0].astype(F32)
        for d in range(1, n):
            s = s + p_ref[d].astype(F32)
        o_ref[...] = s

    return pl.pallas_call(
        body, name=name, grid=(rows // tr,),
        out_shape=jax.ShapeDtypeStruct((rows, cols), F32),
        in_specs=[pl.BlockSpec((n, tr, cols), lambda i: (0, i, 0))],
        out_specs=pl.BlockSpec((tr, cols), lambda i: (i, 0)),
        compiler_params=_cparams(("parallel",)),
    )(parts)


def sum_pair(a, b, name):
    rows, cols = a.shape
    tr = _row_tile(rows, 16, 256)

    def body(a_ref, b_ref, o_ref):
        o_ref[...] = (a_ref[...].astype(F32) + b_ref[...].astype(F32)).astype(o_ref.dtype)

    blk = pl.BlockSpec((tr, cols), lambda i: (i, 0))
    return pl.pallas_call(
        body, name=name, grid=(rows // tr,),
        out_shape=jax.ShapeDtypeStruct((rows, cols), a.dtype),
        in_specs=[blk, blk], out_specs=blk,
        compiler_params=_cparams(("parallel",)),
    )(a, b)


def adamw(w, g, m, v, name):
    rows, cols = w.shape
    tr = _row_tile(rows)

    def body(w_ref, g_ref, m_ref, v_ref, d_ref, m2_ref, v2_ref):
        delta, m2, v2 = _adam_math(w_ref[...], g_ref[...], m_ref[...], v_ref[...])
        d_ref[...] = delta
        m2_ref[...] = m2
        v2_ref[...] = v2

    blk = pl.BlockSpec((tr, cols), lambda i: (i, 0))
    return pl.pallas_call(
        body, name=name, grid=(rows // tr,),
        out_shape=[jax.ShapeDtypeStruct((rows, cols), F32)] * 3,
        in_specs=[blk] * 4, out_specs=[blk] * 3,
        compiler_params=_cparams(("parallel",)),
    )(w, g, m, v)


def _pad128(n):
    return -(-n // 128) * 128


def _pack(arrays):
    offs, parts, off = [], [], 0
    for a in arrays:
        flat = a.reshape(1, -1)
        n = flat.shape[1]
        offs.append(off)
        parts.append(jnp.pad(flat, ((0, 0), (0, _pad128(n) - n))))
        off += _pad128(n)
    return jnp.concatenate(parts, axis=1), offs


def _unpack(vec, offs, shapes):
    out = []
    for off, shp in zip(offs, shapes):
        n = int(np.prod(shp))
        out.append(vec[0, off:off + n].reshape(shp))
    return out


IN_ROWS = IN_DIM // N_DEV
DT_ROW0 = 9216
DT_DEV, DT_LO = divmod(DT_ROW0, IN_ROWS)


def _gathered_in_rows(g_all):
    pieces = []
    for d in range(N_DEV):
        if d == DT_DEV:
            pieces += [g_all[d, :DT_LO], g_all[d, DT_LO + 32:IN_ROWS]]
        else:
            pieces.append(g_all[d, :IN_ROWS])
    dt = g_all[DT_DEV, DT_LO:DT_LO + 32].reshape(N_GROUPS, HEADS_PER_GROUP, D)
    dt = jnp.pad(dt, ((0, 0), (0, N_STATE - HEADS_PER_GROUP), (0, 0))).reshape(N_GROUPS * N_STATE, D)
    pieces.append(jnp.pad(dt, ((0, D - N_GROUPS * N_STATE), (0, 0))))
    return jnp.concatenate(pieces, axis=0)


def _grad_in_blocks(g_t, core, slot):
    dt = g_t[11264:11264 + N_GROUPS * N_STATE].reshape(N_GROUPS, N_STATE, D)[:, :HEADS_PER_GROUP].reshape(32, D)
    with_dt = jnp.concatenate([g_t[DT_DEV * IN_ROWS:DT_ROW0], dt, g_t[DT_ROW0:(DT_DEV + 1) * IN_ROWS - 32]], axis=0)
    blocks = []
    for q in range(N_CHIP):
        if 2 * q + 1 < DT_DEV:
            blk = lax.dynamic_slice_in_dim(g_t, IN_ROWS * (2 * q + core), IN_ROWS, axis=0)
        else:
            assert 2 * q == DT_DEV
            after = g_t[(DT_DEV + 1) * IN_ROWS - 32:(DT_DEV + 2) * IN_ROWS - 32]
            blk = jnp.where(core == 0, with_dt, after)
        blocks.append(jnp.pad(blk, ((0, slot - IN_ROWS), (0, 0))))
    return jnp.stack(blocks)


def kernel(x, c, w_ada, b_ada, w_in, hgrn_lb, hgrn_gnorm, ssm_conv_w, ssm_conv_b, ssm_dt_bias, ssm_a_log, ssm_d, ssm_norm, w_branch_a, w_branch_b, w_o, ln1_g, ln1_b, w_ffn_gate, w_ffn_up, w_ffn_down, ln2_g, ln2_b, loss_target, m_w_ada, m_b_ada, m_w_in, m_hgrn_lb, m_hgrn_gnorm, m_ssm_conv_w, m_ssm_conv_b, m_ssm_dt_bias, m_ssm_a_log, m_ssm_d, m_ssm_norm, m_w_branch_a, m_w_branch_b, m_w_o, m_ln1_g, m_ln1_b, m_w_ffn_gate, m_w_ffn_up, m_w_ffn_down, m_ln2_g, m_ln2_b, v_w_ada, v_b_ada, v_w_in, v_hgrn_lb, v_hgrn_gnorm, v_ssm_conv_w, v_ssm_conv_b, v_ssm_dt_bias, v_ssm_a_log, v_ssm_d, v_ssm_norm, v_w_branch_a, v_w_branch_b, v_w_o, v_ln1_g, v_ln1_b, v_w_ffn_gate, v_w_ffn_up, v_w_ffn_down, v_ln2_g, v_ln2_b):
    me = 4 * lax.axis_index("x") + 2 * lax.axis_index("y") + lax.axis_index("c")
    xt = x[0]
    tgt = loss_target[0]
    t = xt.shape[0]
    ada_cols = w_ada.shape[2]
    conv_cols = ssm_conv_w.shape[2]

    small_in, _ = _pack([c, ssm_conv_w[0]])
    small_all = allgather_vmem(small_in, "allgather_small_inputs")
    c_all = small_all[:, 0, :D]
    conv_w = small_all[:, 0, D:D + CONV_TAPS * conv_cols].reshape(N_DEV, CONV_TAPS, conv_cols)
    conv_w = conv_w.transpose(1, 0, 2).reshape(CONV_TAPS, CONV_DIM)
    mod = ada_modulation(c_all, w_ada[0], b_ada.reshape(N_DEV, 1, ada_cols))
    mod6 = mod.reshape(6, D)

    shards = [w_in[0].T, w_branch_a[0], w_branch_b[0], w_o[0], w_ffn_gate[0].T, w_ffn_up[0].T, w_ffn_down[0]]
    shard_rows = [s.shape[0] for s in shards]
    slot_rows = [-(-r // 32) * 32 for r in shard_rows]
    row_offs = [sum(slot_rows[:i]) for i in range(len(shards))]
    padded = [jnp.pad(s.astype(BF16), ((0, p - r), (0, 0))) for s, r, p in zip(shards, shard_rows, slot_rows)]
    w_in_t = _gathered_in_rows(allgather_hbm(padded[0], "allgather_w_in"))

    lb = lower_bound_fwd(hgrn_lb)
    u1 = ln_modulate(xt, mod6, 0, 1, "ln_modulate_1")
    proj, g_rest = mm_nt_gather(u1, w_in_t, F32, jnp.concatenate(padded[1:], axis=0), "mm_in_proj")
    g_ba, g_bb, g_o, g_fg, g_fu, g_fd = (g_rest[:, o - slot_rows[0]:o - slot_rows[0] + r]
                                         for o, r in zip(row_offs[1:], shard_rows[1:]))
    w_ba = g_ba.reshape(D, D)
    w_bb = g_bb.reshape(B_INNER, D)
    w_oo = g_o.reshape(D, D)
    ffpad = ((0, D_FF_PAD - D_FF), (0, 0))
    w_gu_t = jnp.concatenate([jnp.pad(g_fg.reshape(D_FF, D), ffpad), jnp.pad(g_fu.reshape(D_FF, D), ffpad)], axis=0)
    w_dn = jnp.pad(g_fd.reshape(D_FF, D), ffpad)
    o_a, o_raw, st_a = hgrn_fwd(proj, lb, hgrn_gnorm)
    xc = conv_fwd(proj, conv_w, ssm_conv_b)
    pad3 = ((0, 0), (0, 0), (0, N_STATE - HEADS_PER_GROUP))
    alog4 = jnp.pad(ssm_a_log.reshape(N_GROUPS, 1, HEADS_PER_GROUP), pad3)
    bias4 = jnp.pad(ssm_dt_bias.reshape(N_GROUPS, 1, HEADS_PER_GROUP), pad3)
    dskip4 = jnp.pad(ssm_d.reshape(N_GROUPS, 1, HEADS_PER_GROUP), pad3)
    expand = _head_expand()
    o_b, st_b = ssd_fwd(proj, xc, alog4, bias4, dskip4, ssm_norm, expand)
    ya = mm_nn(o_a, w_ba, F32, "mm_branch_a")
    yb = mm_nn(o_b, w_bb, F32, "mm_branch_b")
    merged = merge_gates(ya, yb, proj)
    h1 = mm_nn(merged, w_oo, F32, "mm_out_proj")
    x1 = resid_ln(xt, h1, mod6, 2, ln1_g, ln1_b, "resid_ln_1")
    u2 = ln_modulate(x1, mod6, 3, 4, "ln_modulate_2")
    gu = mm_nt(u2, w_gu_t, F32, "mm_ffn_in")
    act = swiglu_act(gu)
    h2 = mm_nn(act, w_dn, F32, "mm_ffn_out")

    dh2, dx1_part, acc4 = resid_ln_bwd(x1, h2, mod6, 5, ln2_g, ln2_b, tgt, True, "resid_ln_2_bwd")
    g_dn = mm_tn(act, dh2, "mm_grad_ffn_down")
    dact = mm_nt(dh2, w_dn, F32, "mm_dact")
    dgu = swiglu_act_bwd(gu, dact)
    g_gu_t = mm_tn(dgu, u2, "mm_grad_ffn_in")
    du2 = mm_nn(dgu, w_gu_t, F32, "mm_du2")
    dx1, acc3 = ln_modulate_bwd(x1, du2, mod6, 4, dx1_part, "ln_modulate_2_bwd")
    dh1, dx_part, acc2 = resid_ln_bwd(xt, h1, mod6, 2, ln1_g, ln1_b, dx1, False, "resid_ln_1_bwd")
    g_o = mm_tn(merged, dh1, "mm_grad_out_proj")
    dmerged = mm_nt(dh1, w_oo, F32, "mm_dmerged")
    dya, dyb, dga, dgb = merge_gates_bwd(dmerged, ya, yb, proj)
    g_ba_full = mm_tn(o_a, dya, "mm_grad_branch_a")
    g_bb_full = mm_tn(o_b, dyb, "mm_grad_branch_b")
    doa = mm_nt(dya, w_ba, F32, "mm_doa")
    dob = mm_nt(dyb, w_bb, F32, "mm_dob")
    my_core = lax.axis_index("c")

    def by_core(blocks, rows, slots):
        contrib = jnp.concatenate([jnp.pad(b.reshape(N_DEV, -1, D), ((0, 0), (0, p - r), (0, 0)))
                                   for b, r, p in zip(blocks, rows, slots)], axis=1)
        split = contrib.reshape(N_CHIP, 2, contrib.shape[1], D).transpose(1, 0, 2, 3)
        return (lax.dynamic_index_in_dim(split, my_core, 0, keepdims=False),
                lax.dynamic_index_in_dim(split, 1 - my_core, 0, keepdims=False))

    keep_e, give_e = by_core([g_ba_full, g_bb_full, g_o, g_gu_t[:D_FF], g_gu_t[D_FF_PAD:D_FF_PAD + D_FF], g_dn[:D_FF]],
                             shard_rows[1:], slot_rows[1:])
    dq, dfl, di, dg, dlb, dgn, got_e = hgrn_bwd(proj, lb, hgrn_gnorm, o_raw, doa, st_a, give_e)
    chip_e = sum_pair(keep_e.reshape(-1, D), got_e.reshape(-1, D), "sum_grads_rest_chip").reshape(keep_e.shape)
    dxs, dbm, dcm, dz, ddt, dwn, dalog, dbias, ddsk, parts_e = ssd_bwd(proj, xc, alog4, bias4, dskip4, ssm_norm, expand,
                                                                       dob, st_b, chip_e)
    dxc = jnp.concatenate([dxs, dbm, dcm], axis=1)
    dxbc, dcw, dcb = conv_bwd(proj, dxc, conv_w, ssm_conv_b)
    dproj = jnp.concatenate([dq, dfl, di, dg, dz, dxbc, dga, dgb, ddt, jnp.zeros((t, D - N_GROUPS * N_STATE), BF16)], axis=1)
    g_in_t = mm_tn(dproj, u1, "mm_grad_in_proj")
    du1 = mm_nn(dproj, w_in_t, F32, "mm_du1")
    dx, acc1 = ln_modulate_bwd(xt, du1, mod6, 1, dx_part, "ln_modulate_1_bwd")

    keep_l = _grad_in_blocks(g_in_t, my_core, slot_rows[0])
    give_l = _grad_in_blocks(g_in_t, 1 - my_core, slot_rows[0])
    got_l = exchange_sibling(give_l, "exchange_grad_in_sibling")
    chip_l = sum_pair(keep_l.reshape(-1, D), got_l.reshape(-1, D), "sum_grad_in_chip").reshape(keep_l.shape)
    parts_l = exchange_chips(chip_l, "exchange_grad_in_chips")
    gw_in = sum_parts(parts_l, "sum_grad_in")[:shard_rows[0]].T
    g_rows = sum_parts(parts_e, "sum_grads_rest")
    gw_ba, gw_bb, gw_o, gw_fg, gw_fu, gw_fd = (g_rows[o - slot_rows[0]:o - slot_rows[0] + r]
                                               for o, r in zip(row_offs[1:], shard_rows[1:]))
    gw_fg, gw_fu = gw_fg.T, gw_fu.T

    dmod = jnp.concatenate([acc1[1:2], acc1[0:1], acc2[0:1], acc3[1:2], acc3[0:1], acc4[0:1]], axis=1)
    small_fields = [dmod, acc4[3:4, :128], dlb, dgn, dcw[:CONV_TAPS], dcb, dbias, dalog, ddsk, dwn,
                    acc2[1:2], acc2[2:3], acc4[1:2], acc4[2:3]]
    small_out, offs = _pack(small_fields)
    small_sum_in = allgather_vmem(small_out, "allgather_small_grads")
    gsum, g_lb = reduce_small(small_sum_in, hgrn_lb, offs[2])
    (g_bada, loss_row, _, g_gn, g_cw_full, g_cb, g_bias4, g_alog4, g_dsk4, g_wn, g_l1g, g_l1b, g_l2g, g_l2b) = _unpack(
        gsum, offs, [(1, 6 * D), (1, 128), (1, D), (1, HK), (CONV_TAPS, CONV_DIM), (1, CONV_DIM),
                     (N_GROUPS, N_STATE), (N_GROUPS, N_STATE), (N_GROUPS, N_STATE), (1, B_INNER),
                     (1, D), (1, D), (1, D), (1, D)])
    loss = loss_row[0, 0]
    g_cw = lax.dynamic_slice(g_cw_full, (0, me * conv_cols), (CONV_TAPS, conv_cols))[None]
    g_dtb = g_bias4[:, :HEADS_PER_GROUP].reshape(1, 32)
    g_alog = g_alog4[:, :HEADS_PER_GROUP].reshape(1, 32)
    g_dsk = g_dsk4[:, :HEADS_PER_GROUP].reshape(1, 32)

    dmod_all = small_sum_in[:, 0, offs[0]:offs[0] + 6 * D]
    dmod_cols = lax.dynamic_slice(dmod_all, (0, me * ada_cols), (N_DEV, ada_cols))
    gw_ada = ada_weight_grad(c_all, dmod_cols)

    big = [("ada", w_ada[0], gw_ada, m_w_ada[0], v_w_ada[0]), ("in", w_in[0], gw_in, m_w_in[0], v_w_in[0]),
           ("branch_a", w_branch_a[0], gw_ba, m_w_branch_a[0], v_w_branch_a[0]),
           ("branch_b", w_branch_b[0], gw_bb, m_w_branch_b[0], v_w_branch_b[0]),
           ("o", w_o[0], gw_o, m_w_o[0], v_w_o[0]),
           ("ffn_gate", w_ffn_gate[0], gw_fg, m_w_ffn_gate[0], v_w_ffn_gate[0]),
           ("ffn_up", w_ffn_up[0], gw_fu, m_w_ffn_up[0], v_w_ffn_up[0]),
           ("ffn_down", w_ffn_down[0], gw_fd, m_w_ffn_down[0], v_w_ffn_down[0])]
    big_out = {}
    for nm, w_, g_, m_, v_ in big:
        d_, m2_, v2_ = adamw(w_, g_, m_, v_, "adamw_" + nm)
        big_out[nm] = (g_[None], d_[None], m2_[None], v2_[None])

    small_w = [b_ada, hgrn_lb, hgrn_gnorm, ssm_conv_w, ssm_conv_b, ssm_dt_bias, ssm_a_log, ssm_d, ssm_norm,
               ln1_g, ln1_b, ln2_g, ln2_b]
    small_g = [g_bada, g_lb, g_gn, g_cw, g_cb, g_dtb, g_alog, g_dsk, g_wn, g_l1g, g_l1b, g_l2g, g_l2b]
    small_m = [m_b_ada, m_hgrn_lb, m_hgrn_gnorm, m_ssm_conv_w, m_ssm_conv_b, m_ssm_dt_bias, m_ssm_a_log, m_ssm_d,
               m_ssm_norm, m_ln1_g, m_ln1_b, m_ln2_g, m_ln2_b]
    small_v = [v_b_ada, v_hgrn_lb, v_hgrn_gnorm, v_ssm_conv_w, v_ssm_conv_b, v_ssm_dt_bias, v_ssm_a_log, v_ssm_d,
               v_ssm_norm, v_ln1_g, v_ln1_b, v_ln2_g, v_ln2_b]
    shapes = [a.shape for a in small_w]
    small_g = [g_.reshape(s) for g_, s in zip(small_g, shapes)]
    pw, poffs = _pack(small_w)
    pg, _ = _pack(small_g)
    pm, _ = _pack(small_m)
    pv, _ = _pack(small_v)
    pd, pm2, pv2 = adamw(pw, pg, pm, pv, "adamw_small")
    s_d, s_m, s_v = (_unpack(p, poffs, shapes) for p in (pd, pm2, pv2))
    (sn_bada, sn_lb, sn_gn, sn_cw, sn_cb, sn_dtb, sn_alog, sn_dsk, sn_wn, sn_l1g, sn_l1b, sn_l2g, sn_l2b) = range(13)

    def order(kind):
        sm = [small_g, s_d, s_m, s_v][kind]
        bg = lambda nm: big_out[nm][kind]
        return [bg("ada"), sm[sn_bada], bg("in"), sm[sn_lb], sm[sn_gn], sm[sn_cw], sm[sn_cb], sm[sn_dtb], sm[sn_alog],
                sm[sn_dsk], sm[sn_wn], bg("branch_a"), bg("branch_b"), bg("o"), sm[sn_l1g], sm[sn_l1b],
                bg("ffn_gate"), bg("ffn_up"), bg("ffn_down"), sm[sn_l2g], sm[sn_l2b]]

    return (loss, dx[None], *order(0), *order(1), *order(2), *order(3))
```

```python
import numpy as np
import jax
import jax.numpy as jnp
from jax import lax
from jax.experimental import pallas as pl
from jax.experimental.pallas import tpu as pltpu

F32 = jnp.float32
BF16 = jnp.bfloat16
HI = lax.Precision.HIGHEST

N_DEV = 8
D = 1024
N_HEADS_A = 8
HK = 128
CHUNK = 64
SSD_CHUNK = 128
N_GROUPS = 4
HEADS_PER_GROUP = 8
HEAD_P = 64
N_STATE = 128
GROUP_W = HEADS_PER_GROUP * HEAD_P
B_INNER = 2048
CONV_DIM = 3072
D_FF = 2816
D_FF_PAD = 3072
IN_DIM = 11296
N_PROJ = 12288
ALPHA = 2.0 ** 0.25
LN_EPS = 1e-5
RMS_EPS = 1e-6
Q_SCALE = 128 ** -0.5
EXP_CLIP = 80.0
ADAM_LR, ADAM_B1, ADAM_B2, ADAM_EPS, ADAM_WD, ADAM_STEP = 0.001, 0.9, 0.999, 1e-8, 0.01, 10
VMEM_LIMIT = 48 * 1024 * 1024
TOKEN_BLOCK = 512
ROW_TILE = 256
FFN_ROW_TILE = 128
MM_ROW_TILE = 1024
MM_TOKEN_TILE = 4096
MM_K_TILE = 3072
CHUNK_UNROLL = 8
MESH_ID = pl.DeviceIdType.MESH

NT_DIMS = (((1,), (1,)), ((), ()))
TN_DIMS = (((0,), (0,)), ((), ()))


def _cparams(sem=None):
    return pltpu.CompilerParams(dimension_semantics=sem, vmem_limit_bytes=VMEM_LIMIT)


def _sigmoid(x):
    return 1.0 / (1.0 + jnp.exp(-x))


def _dsilu(x, s):
    return s * (1.0 + x * (1.0 - s))


def _nt(a, b, precision=None):
    return lax.dot_general(a, b, NT_DIMS, precision=precision, preferred_element_type=F32)


def _tn(a, b, precision=None):
    return lax.dot_general(a, b, TN_DIMS, precision=precision, preferred_element_type=F32)


def _nn(a, b, precision=None):
    return jnp.dot(a, b, precision=precision, preferred_element_type=F32)


def _split(x, pieces):
    out = []
    for i in range(pieces):
        p = x.astype(BF16)
        out.append(p)
        if i + 1 < pieces:
            x = x - p.astype(F32)
    return out


def _sel(dot, x, sel01, pieces, x_first=True):
    acc = None
    for p in _split(x, pieces):
        term = dot(p, sel01) if x_first else dot(sel01, p)
        acc = term if acc is None else acc + term
    return acc


def _ln(x):
    mu = jnp.mean(x, axis=-1, keepdims=True)
    xc = x - mu
    rstd = lax.rsqrt(jnp.mean(xc * xc, axis=-1, keepdims=True) + LN_EPS)
    return xc * rstd, rstd


def _ln_bwd(dxh, xh, rstd):
    return rstd * (dxh - jnp.mean(dxh, axis=-1, keepdims=True) - xh * jnp.mean(dxh * xh, axis=-1, keepdims=True))


def _colsum(x):
    return jnp.sum(x, axis=0, keepdims=True)


def _tri(n, upper=False):
    r = lax.broadcasted_iota(jnp.int32, (n, n), 0)
    c = lax.broadcasted_iota(jnp.int32, (n, n), 1)
    return (c >= r) if upper else (r >= c)


def _my_pos():
    return lax.axis_index("x"), lax.axis_index("y"), lax.axis_index("c")


def _peer(pos, k):
    x, y, c = pos
    return (x ^ ((k >> 2) & 1), y ^ ((k >> 1) & 1), c ^ (k & 1))


def _flat(pos):
    return 4 * pos[0] + 2 * pos[1] + pos[2]


def allgather_vmem(v, name):
    n = v.shape[1]

    def body(v_ref, o_ref, send_sems, recv_sems, local_sem):
        me = _my_pos()
        mine = pltpu.make_async_copy(v_ref, o_ref.at[_flat(me)], local_sem)
        mine.start()
        sends = []
        for k in range(1, N_DEV):
            peer = _peer(me, k)
            cp = pltpu.make_async_remote_copy(v_ref, o_ref.at[_flat(me)], send_sems.at[k - 1], recv_sems.at[k - 1],
                                              device_id=peer, device_id_type=MESH_ID)
            cp.start()
            sends.append(cp)
        for k in range(1, N_DEV):
            peer = _peer(me, k)
            pltpu.make_async_remote_copy(v_ref, o_ref.at[_flat(peer)], send_sems.at[k - 1], recv_sems.at[k - 1],
                                         device_id=peer, device_id_type=MESH_ID).wait_recv()
        for cp in sends:
            cp.wait_send()
        mine.wait()

    return pl.pallas_call(
        body, name=name,
        out_shape=jax.ShapeDtypeStruct((N_DEV, 1, n), F32),
        in_specs=[pl.BlockSpec(memory_space=pltpu.VMEM)],
        out_specs=pl.BlockSpec(memory_space=pltpu.VMEM),
        scratch_shapes=[pltpu.SemaphoreType.DMA((N_DEV - 1,)), pltpu.SemaphoreType.DMA((N_DEV - 1,)),
                        pltpu.SemaphoreType.DMA],
        compiler_params=_cparams(),
    )(v)


def ada_modulation(c_all, w_ada_s, b_ada_r):
    ncol = w_ada_s.shape[1]

    def body(c_ref, w_ref, b_ref, o_ref, part_ref, send_sems, recv_sems):
        me = _my_pos()
        cval = c_ref[...]
        cond = cval * _sigmoid(cval)
        part = _nn(cond, w_ref[...], HI)
        for r in range(N_DEV):
            part_ref[r] = part[r:r + 1, :]
        sends = []
        for k in range(1, N_DEV):
            peer = _peer(me, k)
            cp = pltpu.make_async_remote_copy(part_ref.at[_flat(peer)], o_ref.at[_flat(me)], send_sems.at[k - 1],
                                              recv_sems.at[k - 1], device_id=peer, device_id_type=MESH_ID)
            cp.start()
            sends.append(cp)
        o_ref[_flat(me)] = part_ref[_flat(me)]
        for k in range(1, N_DEV):
            peer = _peer(me, k)
            pltpu.make_async_remote_copy(part_ref.at[_flat(peer)], o_ref.at[_flat(peer)], send_sems.at[k - 1],
                                         recv_sems.at[k - 1], device_id=peer, device_id_type=MESH_ID).wait_recv()
        for cp in sends:
            cp.wait_send()
        o_ref[...] = o_ref[...] + b_ref[...]

    return pl.pallas_call(
        body, name="ada_modulation",
        out_shape=jax.ShapeDtypeStruct((N_DEV, 1, ncol), F32),
        in_specs=[pl.BlockSpec(memory_space=pltpu.VMEM)] * 3,
        out_specs=pl.BlockSpec(memory_space=pltpu.VMEM),
        scratch_shapes=[pltpu.VMEM((N_DEV, 1, ncol), F32), pltpu.SemaphoreType.DMA((N_DEV - 1,)),
                        pltpu.SemaphoreType.DMA((N_DEV - 1,))],
        compiler_params=_cparams(),
    )(c_all, w_ada_s, b_ada_r)


def allgather_hbm(shard, name):
    def body(x_ref, out_ref, send_sems, recv_sems, local_sem):
        x, y, c = _my_pos()
        me, sibling = (x, y, c), (x, y, 1 - c)
        chips = [(1 - x, y), (x, 1 - y), (1 - x, 1 - y)]

        def slot(pos):
            return out_ref.at[_flat(pos)]

        def copy(k, block, to, src=None):
            return pltpu.make_async_remote_copy(slot(block) if src is None else src, slot(block), send_sems.at[k],
                                                recv_sems.at[k], device_id=to, device_id_type=MESH_ID)

        mine = pltpu.make_async_copy(x_ref, slot(me), local_sem)
        mine.start()
        first = [copy(0, me, sibling, src=x_ref)]
        first += [copy(1 + j, me, (*chip, c), src=x_ref) for j, chip in enumerate(chips)]
        for cp in first:
            cp.start()
        passed = [copy(4 + j, (*chip, c), sibling) for j, chip in enumerate(chips)]
        for j, chip in enumerate(chips):
            copy(1 + j, (*chip, c), me).wait_recv()
            passed[j].start()
        copy(0, sibling, me).wait_recv()
        for j, chip in enumerate(chips):
            copy(4 + j, (*chip, 1 - c), me).wait_recv()
        for cp in first + passed:
            cp.wait_send()
        mine.wait()

    return pl.pallas_call(
        body, name=name,
        out_shape=jax.ShapeDtypeStruct((N_DEV,) + shard.shape, shard.dtype),
        in_specs=[pl.BlockSpec(memory_space=pl.ANY)],
        out_specs=pl.BlockSpec(memory_space=pl.ANY),
        scratch_shapes=[pltpu.SemaphoreType.DMA((N_DEV - 1,)), pltpu.SemaphoreType.DMA((N_DEV - 1,)),
                        pltpu.SemaphoreType.DMA],
        compiler_params=_cparams(),
    )(shard)


N_CHIP = N_DEV // 2
SIBLING_SEMS = [pltpu.SemaphoreType.DMA, pltpu.SemaphoreType.DMA]
CHIP_SEMS = [pltpu.SemaphoreType.DMA((N_CHIP - 1,)), pltpu.SemaphoreType.DMA((N_CHIP - 1,)), pltpu.SemaphoreType.DMA]


def _sibling_exchange(s_ref, o_ref, send_sem, recv_sem):
    x, y, c = _my_pos()
    cp = pltpu.make_async_remote_copy(s_ref, o_ref, send_sem, recv_sem, device_id=(x, y, 1 - c), device_id_type=MESH_ID)
    return cp.start, cp.wait


def _chip_exchange(p_ref, o_ref, send_sems, recv_sems, local_sem):
    x, y, c = _my_pos()
    my_chip = 2 * x + y
    mine = pltpu.make_async_copy(p_ref.at[my_chip], o_ref.at[my_chip], local_sem)
    peers = [(x ^ (k >> 1), y ^ (k & 1)) for k in range(1, N_CHIP)]
    sends = [pltpu.make_async_remote_copy(p_ref.at[2 * px + py], o_ref.at[my_chip], send_sems.at[k], recv_sems.at[k],
                                          device_id=(px, py, c), device_id_type=MESH_ID)
             for k, (px, py) in enumerate(peers)]
    recvs = [pltpu.make_async_remote_copy(p_ref.at[2 * px + py], o_ref.at[2 * px + py], send_sems.at[k], recv_sems.at[k],
                                          device_id=(px, py, c), device_id_type=MESH_ID)
             for k, (px, py) in enumerate(peers)]

    def start():
        mine.start()
        for cp in sends:
            cp.start()

    def wait():
        for cp in recvs:
            cp.wait_recv()
        for cp in sends:
            cp.wait_send()
        mine.wait()

    return start, wait


def exchange_sibling(send, name):
    def body(s_ref, o_ref, send_sem, recv_sem):
        start, wait = _sibling_exchange(s_ref, o_ref, send_sem, recv_sem)
        start()
        wait()

    return pl.pallas_call(
        body, name=name,
        out_shape=jax.ShapeDtypeStruct(send.shape, send.dtype),
        in_specs=[pl.BlockSpec(memory_space=pl.ANY)],
        out_specs=pl.BlockSpec(memory_space=pl.ANY),
        scratch_shapes=SIBLING_SEMS,
        compiler_params=_cparams(),
    )(send)


def _k_tile(kdim):
    for cand in range(MM_K_TILE, 0, -1024):
        if kdim % cand == 0:
            return cand
    return kdim


def mm_nn(a, b, out_dtype, name):
    m, kdim = a.shape
    n = b.shape[1]
    tm, tn, tk = min(MM_ROW_TILE, m), 1024, _k_tile(kdim)
    nk = kdim // tk

    def body(a_ref, b_ref, o_ref, acc_ref):
        p = _nn(a_ref[...], b_ref[...])
        if nk == 1:
            o_ref[...] = p.astype(o_ref.dtype)
        else:
            k = pl.program_id(2)

            @pl.when(k == 0)
            def _():
                acc_ref[...] = p

            @pl.when(k > 0)
            def _():
                acc_ref[...] += p

            @pl.when(k == nk - 1)
            def _():
                o_ref[...] = acc_ref[...].astype(o_ref.dtype)

    return pl.pallas_call(
        body, name=name, grid=(n // tn, m // tm, nk),
        out_shape=jax.ShapeDtypeStruct((m, n), out_dtype),
        in_specs=[pl.BlockSpec((tm, tk), lambda j, i, k: (i, k)), pl.BlockSpec((tk, tn), lambda j, i, k: (k, j))],
        out_specs=pl.BlockSpec((tm, tn), lambda j, i, k: (i, j)),
        scratch_shapes=[pltpu.VMEM((tm, tn), F32)],
        compiler_params=_cparams(("parallel", "parallel", "arbitrary")),
    )(a, b)


def mm_nt(a, b, out_dtype, name):
    m, kdim = a.shape
    n = b.shape[0]
    tm, tn, tk = min(MM_ROW_TILE, m), 1024, _k_tile(kdim)
    nk = kdim // tk

    def body(a_ref, b_ref, o_ref, acc_ref):
        p = _nt(a_ref[...], b_ref[...])
        if nk == 1:
            o_ref[...] = p.astype(o_ref.dtype)
        else:
            k = pl.program_id(2)

            @pl.when(k == 0)
            def _():
                acc_ref[...] = p

            @pl.when(k > 0)
            def _():
                acc_ref[...] += p

            @pl.when(k == nk - 1)
            def _():
                o_ref[...] = acc_ref[...].astype(o_ref.dtype)

    return pl.pallas_call(
        body, name=name, grid=(n // tn, m // tm, nk),
        out_shape=jax.ShapeDtypeStruct((m, n), out_dtype),
        in_specs=[pl.BlockSpec((tm, tk), lambda j, i, k: (i, k)), pl.BlockSpec((tn, tk), lambda j, i, k: (j, k))],
        out_specs=pl.BlockSpec((tm, tn), lambda j, i, k: (i, j)),
        scratch_shapes=[pltpu.VMEM((tm, tn), F32)],
        compiler_params=_cparams(("parallel", "parallel", "arbitrary")),
    )(a, b)


def mm_nn_exchange(a, b, out_dtype, part, name):
    m, kdim = a.shape
    n = b.shape[1]
    tm, tn, tk = min(MM_ROW_TILE, m), 1024, _k_tile(kdim)
    gn, gm, nk = n // tn, m // tm, kdim // tk

    def body(a_ref, b_ref, part_ref, o_ref, parts_ref, acc_ref, send_sems, recv_sems, local_sem):
        j, i, k = pl.program_id(0), pl.program_id(1), pl.program_id(2)
        xchg_start, xchg_wait = _chip_exchange(part_ref, parts_ref, send_sems, recv_sems, local_sem)

        @pl.when((j == 0) & (i == 0) & (k == 0))
        def _():
            xchg_start()

        p = _nn(a_ref[...], b_ref[...])

        @pl.when(k == 0)
        def _():
            acc_ref[...] = p

        @pl.when(k > 0)
        def _():
            acc_ref[...] += p

        @pl.when(k == nk - 1)
        def _():
            o_ref[...] = acc_ref[...].astype(o_ref.dtype)

        @pl.when((j == gn - 1) & (i == gm - 1) & (k == nk - 1))
        def _():
            xchg_wait()

    hbm = pl.BlockSpec(memory_space=pl.ANY)
    return pl.pallas_call(
        body, name=name, grid=(gn, gm, nk),
        out_shape=[jax.ShapeDtypeStruct((m, n), out_dtype), jax.ShapeDtypeStruct(part.shape, part.dtype)],
        in_specs=[pl.BlockSpec((tm, tk), lambda j, i, k: (i, k)), pl.BlockSpec((tk, tn), lambda j, i, k: (k, j)), hbm],
        out_specs=[pl.BlockSpec((tm, tn), lambda j, i, k: (i, j)), hbm],
        scratch_shapes=[pltpu.VMEM((tm, tn), F32)] + CHIP_SEMS,
        compiler_params=_cparams(("arbitrary", "arbitrary", "arbitrary")),
    )(a, b, part)


def mm_nt_gather(a, b, out_dtype, shard, name):
    m, kdim = a.shape
    n = b.shape[0]
    tm, tn = min(MM_ROW_TILE, m), 1024
    assert kdim == 1024
    gj = m // tm
    nsteps = (n // tn) * gj
    forward_step = max(nsteps - 3, 0)

    def body(a_ref, b_ref, x_ref, o_ref, g_ref, send_sems, recv_sems, local_sem):
        step = pl.program_id(0) * gj + pl.program_id(1)
        x, y, c = _my_pos()
        me, sibling = (x, y, c), (x, y, 1 - c)
        chips = [(1 - x, y), (x, 1 - y), (1 - x, 1 - y)]

        def slot(pos):
            return g_ref.at[_flat(pos)]

        def copy(k, block, to, src=None):
            return pltpu.make_async_remote_copy(slot(block) if src is None else src, slot(block), send_sems.at[k],
                                                recv_sems.at[k], device_id=to, device_id_type=MESH_ID)

        mine = pltpu.make_async_copy(x_ref, slot(me), local_sem)
        first = [copy(0, me, sibling, src=x_ref)]
        first += [copy(1 + j, me, (*chip, c), src=x_ref) for j, chip in enumerate(chips)]
        passed = [copy(4 + j, (*chip, c), sibling) for j, chip in enumerate(chips)]

        @pl.when(step == 0)
        def _():
            mine.start()
            for cp in first:
                cp.start()

        o_ref[...] = _nt(a_ref[...], b_ref[...]).astype(o_ref.dtype)

        @pl.when(step == forward_step)
        def _():
            for j, chip in enumerate(chips):
                copy(1 + j, (*chip, c), me).wait_recv()
                passed[j].start()

        @pl.when(step == nsteps - 1)
        def _():
            copy(0, sibling, me).wait_recv()
            for j, chip in enumerate(chips):
                copy(4 + j, (*chip, 1 - c), me).wait_recv()
            for cp in first + passed:
                cp.wait_send()
            mine.wait()

    return pl.pallas_call(
        body, name=name, grid=(n // tn, gj),
        out_shape=[jax.ShapeDtypeStruct((m, n), out_dtype), jax.ShapeDtypeStruct((N_DEV,) + shard.shape, shard.dtype)],
        in_specs=[pl.BlockSpec((tm, kdim), lambda j, i: (i, 0)), pl.BlockSpec((tn, kdim), lambda j, i: (j, 0)),
                  pl.BlockSpec(memory_space=pl.ANY)],
        out_specs=[pl.BlockSpec((tm, tn), lambda j, i: (i, j)), pl.BlockSpec(memory_space=pl.ANY)],
        scratch_shapes=[pltpu.SemaphoreType.DMA((N_DEV - 1,)), pltpu.SemaphoreType.DMA((N_DEV - 1,)),
                        pltpu.SemaphoreType.DMA],
        compiler_params=_cparams(("arbitrary", "arbitrary")),
    )(a, b, shard)


def mm_tn(a, b, name):
    t, ka = a.shape
    n = b.shape[1]
    tt, tka, tn = min(MM_TOKEN_TILE, t), 1024, 1024
    nt = t // tt

    def body(a_ref, b_ref, o_ref, *acc):
        p = _tn(a_ref[...], b_ref[...])
        if nt == 1:
            o_ref[...] = p.astype(o_ref.dtype)
        else:
            acc_ref, s = acc[0], pl.program_id(2)

            @pl.when(s == 0)
            def _():
                acc_ref[...] = p

            @pl.when(s > 0)
            def _():
                acc_ref[...] += p

            @pl.when(s == nt - 1)
            def _():
                o_ref[...] = acc_ref[...].astype(o_ref.dtype)

    return pl.pallas_call(
        body, name=name, grid=(ka // tka, n // tn, nt),
        out_shape=jax.ShapeDtypeStruct((ka, n), BF16),
        in_specs=[pl.BlockSpec((tt, tka), lambda i, j, s: (s, i)), pl.BlockSpec((tt, tn), lambda i, j, s: (s, j))],
        out_specs=pl.BlockSpec((tka, tn), lambda i, j, s: (i, j)),
        scratch_shapes=[] if nt == 1 else [pltpu.VMEM((tka, tn), F32)],
        compiler_params=_cparams(("parallel", "parallel", "arbitrary")),
    )(a, b)


def _tile(t, cap):
    return min(cap, t)


def ln_modulate(x, mod6, shift_row, scale_row, name):
    t = x.shape[0]
    tm = _tile(t, ROW_TILE)

    def body(x_ref, mod_ref, o_ref):
        xh, _ = _ln(x_ref[...])
        sc = mod_ref[scale_row:scale_row + 1, :]
        sh = mod_ref[shift_row:shift_row + 1, :]
        o_ref[...] = (xh * (1.0 + sc) + sh).astype(BF16)

    return pl.pallas_call(
        body, name=name, grid=(t // tm,),
        out_shape=jax.ShapeDtypeStruct((t, D), BF16),
        in_specs=[pl.BlockSpec((tm, D), lambda i: (i, 0)), pl.BlockSpec((6, D), lambda i: (0, 0))],
        out_specs=pl.BlockSpec((tm, D), lambda i: (i, 0)),
        compiler_params=_cparams(("parallel",)),
    )(x, mod6)


def resid_ln(x, h, mod6, gate_row, ln_g, ln_b, name):
    t = x.shape[0]
    tm = _tile(t, ROW_TILE)

    def body(x_ref, h_ref, mod_ref, g_ref, b_ref, o_ref):
        r = ALPHA * x_ref[...] + mod_ref[gate_row:gate_row + 1, :] * h_ref[...]
        rh, _ = _ln(r)
        o_ref[...] = rh * g_ref[...] + b_ref[...]

    row = pl.BlockSpec((tm, D), lambda i: (i, 0))
    vec = pl.BlockSpec((1, D), lambda i: (0, 0))
    return pl.pallas_call(
        body, name=name, grid=(t // tm,),
        out_shape=jax.ShapeDtypeStruct((t, D), F32),
        in_specs=[row, row, pl.BlockSpec((6, D), lambda i: (0, 0)), vec, vec],
        out_specs=row,
        compiler_params=_cparams(("parallel",)),
    )(x, h, mod6, ln_g, ln_b)


def resid_ln_bwd(x, h, mod6, gate_row, ln_g, ln_b, cot, with_loss, name):
    t = x.shape[0]
    tm = _tile(t, ROW_TILE)

    def body(x_ref, h_ref, mod_ref, g_ref, b_ref, c_ref, dh_ref, dx_ref, acc_ref):
        @pl.when(pl.program_id(0) == 0)
        def _():
            acc_ref[...] = jnp.zeros_like(acc_ref)

        gate = mod_ref[gate_row:gate_row + 1, :]
        hv = h_ref[...]
        r = ALPHA * x_ref[...] + gate * hv
        rh, rstd = _ln(r)
        lng = g_ref[...]
        if with_loss:
            diff = rh * lng + b_ref[...] - c_ref[...]
            dxo = diff * (1.0 / D)
            lsum = jnp.sum(_colsum(diff * diff), axis=-1, keepdims=True) * (0.5 / D)
            acc_ref[3:4, :] += jnp.broadcast_to(lsum, (1, D))
        else:
            dxo = c_ref[...]
        acc_ref[1:2, :] += _colsum(dxo * rh)
        acc_ref[2:3, :] += _colsum(dxo)
        dr = _ln_bwd(dxo * lng, rh, rstd)
        acc_ref[0:1, :] += _colsum(dr * hv)
        dh_ref[...] = (gate * dr).astype(BF16)
        dx_ref[...] = ALPHA * dr

    row = pl.BlockSpec((tm, D), lambda i: (i, 0))
    vec = pl.BlockSpec((1, D), lambda i: (0, 0))
    return pl.pallas_call(
        body, name=name, grid=(t // tm,),
        out_shape=[jax.ShapeDtypeStruct((t, D), BF16), jax.ShapeDtypeStruct((t, D), F32),
                   jax.ShapeDtypeStruct((8, D), F32)],
        in_specs=[row, row, pl.BlockSpec((6, D), lambda i: (0, 0)), vec, vec, row],
        out_specs=[row, row, pl.BlockSpec((8, D), lambda i: (0, 0))],
        compiler_params=_cparams(("arbitrary",)),
    )(x, h, mod6, ln_g, ln_b, cot)


def ln_modulate_bwd(x, du, mod6, scale_row, dx_part, name):
    t = x.shape[0]
    tm = _tile(t, ROW_TILE)

    def body(x_ref, du_ref, mod_ref, dp_ref, dx_ref, acc_ref):
        @pl.when(pl.program_id(0) == 0)
        def _():
            acc_ref[...] = jnp.zeros_like(acc_ref)

        xh, rstd = _ln(x_ref[...])
        du_v = du_ref[...]
        sc = mod_ref[scale_row:scale_row + 1, :]
        acc_ref[0:1, :] += _colsum(du_v * xh)
        acc_ref[1:2, :] += _colsum(du_v)
        dx_ref[...] = dp_ref[...] + _ln_bwd(du_v * (1.0 + sc), xh, rstd)

    row = pl.BlockSpec((tm, D), lambda i: (i, 0))
    return pl.pallas_call(
        body, name=name, grid=(t // tm,),
        out_shape=[jax.ShapeDtypeStruct((t, D), F32), jax.ShapeDtypeStruct((8, D), F32)],
        in_specs=[row, row, pl.BlockSpec((6, D), lambda i: (0, 0)), row],
        out_specs=[row, pl.BlockSpec((8, D), lambda i: (0, 0))],
        compiler_params=_cparams(("arbitrary",)),
    )(x, du, mod6, dx_part)


def merge_gates(ya, yb, proj):
    t = ya.shape[0]
    tm = _tile(t, ROW_TILE)

    def body(ya_ref, yb_ref, ga_ref, gb_ref, o_ref):
        o_ref[...] = (_sigmoid(ga_ref[...]) * ya_ref[...] + _sigmoid(gb_ref[...]) * yb_ref[...]).astype(BF16)

    row = pl.BlockSpec((tm, D), lambda i: (i, 0))
    return pl.pallas_call(
        body, name="merge_gates", grid=(t // tm,),
        out_shape=jax.ShapeDtypeStruct((t, D), BF16),
        in_specs=[row, row, pl.BlockSpec((tm, D), lambda i: (i, 9)), pl.BlockSpec((tm, D), lambda i: (i, 10))],
        out_specs=row,
        compiler_params=_cparams(("parallel",)),
    )(ya, yb, proj, proj)


def merge_gates_bwd(dm, ya, yb, proj):
    t = ya.shape[0]
    tm = _tile(t, ROW_TILE)

    def body(dm_ref, ya_ref, yb_ref, ga_ref, gb_ref, dya_ref, dyb_ref, dga_ref, dgb_ref):
        dmv = dm_ref[...]
        sa = _sigmoid(ga_ref[...])
        sb = _sigmoid(gb_ref[...])
        dya_ref[...] = (dmv * sa).astype(BF16)
        dyb_ref[...] = (dmv * sb).astype(BF16)
        dga_ref[...] = (dmv * ya_ref[...] * sa * (1.0 - sa)).astype(BF16)
        dgb_ref[...] = (dmv * yb_ref[...] * sb * (1.0 - sb)).astype(BF16)

    row = pl.BlockSpec((tm, D), lambda i: (i, 0))
    return pl.pallas_call(
        body, name="merge_gates_bwd", grid=(t // tm,),
        out_shape=[jax.ShapeDtypeStruct((t, D), BF16)] * 4,
        in_specs=[row, row, row, pl.BlockSpec((tm, D), lambda i: (i, 9)), pl.BlockSpec((tm, D), lambda i: (i, 10))],
        out_specs=[row] * 4,
        compiler_params=_cparams(("parallel",)),
    )(dm, ya, yb, proj, proj)


def swiglu_act(gu):
    t = gu.shape[0]
    tm = _tile(t, FFN_ROW_TILE)

    def body(gu_ref, o_ref):
        for j in range(D_FF_PAD // D):
            g = gu_ref[:, j * D:(j + 1) * D]
            u = gu_ref[:, D_FF_PAD + j * D:D_FF_PAD + (j + 1) * D]
            o_ref[:, j * D:(j + 1) * D] = (g * _sigmoid(g) * u).astype(BF16)

    return pl.pallas_call(
        body, name="swiglu_act", grid=(t // tm,),
        out_shape=jax.ShapeDtypeStruct((t, D_FF_PAD), BF16),
        in_specs=[pl.BlockSpec((tm, 2 * D_FF_PAD), lambda i: (i, 0))],
        out_specs=pl.BlockSpec((tm, D_FF_PAD), lambda i: (i, 0)),
        compiler_params=_cparams(("parallel",)),
    )(gu)


def swiglu_act_bwd(gu, dact):
    t = gu.shape[0]
    tm = _tile(t, FFN_ROW_TILE)

    def body(gu_ref, da_ref, o_ref):
        for j in range(D_FF_PAD // D):
            g = gu_ref[:, j * D:(j + 1) * D]
            u = gu_ref[:, D_FF_PAD + j * D:D_FF_PAD + (j + 1) * D]
            da = da_ref[:, j * D:(j + 1) * D]
            s = _sigmoid(g)
            o_ref[:, j * D:(j + 1) * D] = (da * u * _dsilu(g, s)).astype(BF16)
            o_ref[:, D_FF_PAD + j * D:D_FF_PAD + (j + 1) * D] = (da * g * s).astype(BF16)

    return pl.pallas_call(
        body, name="swiglu_act_bwd", grid=(t // tm,),
        out_shape=jax.ShapeDtypeStruct((t, 2 * D_FF_PAD), BF16),
        in_specs=[pl.BlockSpec((tm, 2 * D_FF_PAD), lambda i: (i, 0)), pl.BlockSpec((tm, D_FF_PAD), lambda i: (i, 0))],
        out_specs=pl.BlockSpec((tm, 2 * D_FF_PAD), lambda i: (i, 0)),
        compiler_params=_cparams(("parallel",)),
    )(gu, dact)


def _hgrn_chunk_terms(q, fl, lbv, tril_f):
    sig = _sigmoid(fl)
    f = lbv + (1.0 - lbv) * sig
    lam = jnp.log(f)
    k = 1.0 - f
    sq = _sigmoid(q)
    qt = q * sq * Q_SCALE
    bc = _sel(_nn, lam, tril_f, 3, x_first=False)
    bmid = bc[CHUNK // 2 - 1:CHUNK // 2, :]
    bl = bc[CHUNK - 1:CHUNK, :]
    eq = jnp.exp(jnp.minimum(bc - bmid, EXP_CLIP))
    ek = jnp.exp(jnp.minimum(bmid - bc, EXP_CLIP))
    eb = jnp.exp(bc)
    ekl = jnp.exp(bl - bc)
    ebl = jnp.exp(bl)
    return sig, f, k, sq, qt, eq, ek, eb, ekl, ebl


def hgrn_fwd(proj, lb, gnorm):
    t = proj.shape[0]
    tb = _tile(t, TOKEN_BLOCK)
    ncb = tb // CHUNK

    def body(q_ref, f_ref, i_ref, g_ref, lb_ref, gn_ref, oa_ref, oraw_ref, st_ref, state):
        @pl.when(pl.program_id(1) == 0)
        def _():
            state[...] = jnp.zeros_like(state)

        lbv = lb_ref[...]
        gn = gn_ref[...]
        mask = _tri(CHUNK)
        tril_f = mask.astype(BF16)

        def chunk(c, carry):
            sl = pl.ds(pl.multiple_of(c * CHUNK, CHUNK), CHUNK)
            q, fl, v, g = q_ref[sl, :], f_ref[sl, :], i_ref[sl, :], g_ref[sl, :]
            sig, f, k, sq, qt, eq, ek, eb, ekl, ebl = _hgrn_chunk_terms(q, fl, lbv, tril_f)
            a = jnp.where(mask, _nt((qt * eq).astype(BF16), (k * ek).astype(BF16)), 0.0)
            st = state[...]
            st_ref[0, c] = st
            vb = v.astype(BF16)
            o = _nn(a.astype(BF16), vb) + _nt((qt * eb).astype(BF16), st.astype(BF16))
            state[...] = st * ebl + _tn(vb, (k * ekl).astype(BF16))
            oraw_ref[sl, :] = o
            rn = o * lax.rsqrt(jnp.mean(o * o, axis=-1, keepdims=True) + RMS_EPS)
            oa_ref[sl, :] = (rn * gn * g * _sigmoid(g)).astype(BF16)
            return carry

        lax.fori_loop(0, ncb, chunk, 0, unroll=min(CHUNK_UNROLL, ncb))

    def col(block):
        return pl.BlockSpec((tb, HK), lambda h, j: (j, block * N_HEADS_A + h))

    return pl.pallas_call(
        body, name="hgrn_fwd", grid=(N_HEADS_A, t // tb),
        out_shape=[jax.ShapeDtypeStruct((t, D), BF16), jax.ShapeDtypeStruct((t, D), F32),
                   jax.ShapeDtypeStruct((N_HEADS_A, t // CHUNK, HK, HK), F32)],
        in_specs=[col(0), col(1), col(2), col(3), pl.BlockSpec((1, HK), lambda h, j: (0, h)),
                  pl.BlockSpec((1, HK), lambda h, j: (0, 0))],
        out_specs=[pl.BlockSpec((tb, HK), lambda h, j: (j, h)), pl.BlockSpec((tb, HK), lambda h, j: (j, h)),
                   pl.BlockSpec((1, ncb, HK, HK), lambda h, j: (h, j, 0, 0))],
        scratch_shapes=[pltpu.VMEM((HK, HK), F32)],
        compiler_params=_cparams(("parallel", "arbitrary")),
    )(proj, proj, proj, proj, lb, gnorm)


def hgrn_bwd(proj, lb, gnorm, o_raw, doa, states, give):
    t = proj.shape[0]
    tb = _tile(t, TOKEN_BLOCK)
    ncb = tb // CHUNK
    nb = t // tb

    def body(q_ref, f_ref, i_ref, g_ref, lb_ref, gn_ref, oraw_ref, doa_ref, st_ref, give_ref,
             dq_ref, df_ref, di_ref, dg_ref, dlb_ref, dgn_ref, got_ref, dstate, send_sem, recv_sem):
        h, j = pl.program_id(0), pl.program_id(1)
        swap_start, swap_wait = _sibling_exchange(give_ref, got_ref, send_sem, recv_sem)

        @pl.when((h == 0) & (j == 0))
        def _():
            swap_start()

        @pl.when(j == 0)
        def _():
            dstate[...] = jnp.zeros_like(dstate)
            dlb_ref[...] = jnp.zeros_like(dlb_ref)

        @pl.when((j == 0) & (h == 0))
        def _():
            dgn_ref[...] = jnp.zeros_like(dgn_ref)

        lbv = lb_ref[...]
        gn = gn_ref[...]
        mask = _tri(CHUNK)
        mask_t = _tri(CHUNK, upper=True)
        tril_f = mask.astype(BF16)
        triu_f = mask_t.astype(BF16)

        def chunk(i, c0):
            c = ncb - 1 - i
            sl = pl.ds(pl.multiple_of(c * CHUNK, CHUNK), CHUNK)
            q, fl, v, g = q_ref[sl, :], f_ref[sl, :], i_ref[sl, :], g_ref[sl, :]
            sig, f, k, sq, qt, eq, ek, eb, ekl, ebl = _hgrn_chunk_terms(q, fl, lbv, tril_f)
            qe = (qt * eq).astype(BF16)
            ke = (k * ek).astype(BF16)
            st32 = st_ref[0, c]
            st = st32.astype(BF16)
            dst = dstate[...]
            dstb = dst.astype(BF16)
            o = oraw_ref[sl, :]
            rstd = lax.rsqrt(jnp.mean(o * o, axis=-1, keepdims=True) + RMS_EPS)
            rn = o * rstd
            sgm = _sigmoid(g)
            sg = g * sgm
            doa_v = doa_ref[sl, :]
            drn = doa_v * gn * sg
            dgn_ref[...] += _colsum(doa_v * rn * sg)
            dg_ref[sl, :] = (doa_v * rn * gn * _dsilu(g, sgm)).astype(BF16)
            do = rstd * (drn - rn * jnp.mean(drn * rn, axis=-1, keepdims=True))
            dob = do.astype(BF16)
            vb = v.astype(BF16)
            da = jnp.where(mask, _nt(dob, vb), 0.0).astype(BF16)
            da_t = jnp.where(mask_t, _nt(vb, dob), 0.0).astype(BF16)
            a_t = jnp.where(mask_t, _nt(ke, qe), 0.0).astype(BF16)
            kl = (k * ekl).astype(BF16)
            qb = (qt * eb).astype(BF16)
            dq_in = _nn(da, ke)
            dk_in = _nn(da_t, qe)
            dq_out = eb * _nn(dob, st)
            dk_out = ekl * _nn(vb, dstb)
            dqt = eq * dq_in + dq_out
            dk = ek * dk_in + dk_out
            dv = _nn(a_t, dob) + _nt(kl, dstb)
            dstate[...] = dst * ebl + _tn(dob, qb)
            dbig = qe.astype(F32) * dq_in - ke.astype(F32) * dk_in + qt * dq_out - k * dk_out
            beyond = _colsum(k * dk_out) + ebl * _colsum(dst * st32)
            dlam = _sel(_nn, dbig, triu_f, 3, x_first=False) + beyond
            df = dlam / f - dk
            df_ref[sl, :] = (df * (1.0 - lbv) * sig * (1.0 - sig)).astype(BF16)
            dlb_ref[...] += _colsum(df * (1.0 - sig))
            dq_ref[sl, :] = (dqt * Q_SCALE * _dsilu(q, sq)).astype(BF16)
            di_ref[sl, :] = dv.astype(BF16)
            return c0

        lax.fori_loop(0, ncb, chunk, 0, unroll=min(CHUNK_UNROLL, ncb))

        @pl.when((h == N_HEADS_A - 1) & (j == nb - 1))
        def _():
            swap_wait()

    def col(block):
        return pl.BlockSpec((tb, HK), lambda h, j: (nb - 1 - j, block * N_HEADS_A + h))

    hcol = pl.BlockSpec((tb, HK), lambda h, j: (nb - 1 - j, h))
    hbm = pl.BlockSpec(memory_space=pl.ANY)
    return pl.pallas_call(
        body, name="hgrn_bwd", grid=(N_HEADS_A, nb),
        out_shape=[jax.ShapeDtypeStruct((t, D), BF16)] * 4 + [jax.ShapeDtypeStruct((1, D), F32),
                                                                jax.ShapeDtypeStruct((1, HK), F32),
                                                                jax.ShapeDtypeStruct(give.shape, give.dtype)],
        in_specs=[col(0), col(1), col(2), col(3), pl.BlockSpec((1, HK), lambda h, j: (0, h)),
                  pl.BlockSpec((1, HK), lambda h, j: (0, 0)), hcol, hcol,
                  pl.BlockSpec((1, ncb, HK, HK), lambda h, j: (h, nb - 1 - j, 0, 0)), hbm],
        out_specs=[hcol] * 4 + [pl.BlockSpec((1, HK), lambda h, j: (0, h)), pl.BlockSpec((1, HK), lambda h, j: (0, 0)),
                                hbm],
        scratch_shapes=[pltpu.VMEM((HK, HK), F32)] + SIBLING_SEMS,
        compiler_params=_cparams(("arbitrary", "arbitrary")),
    )(proj, proj, proj, proj, lb, gnorm, o_raw, doa, states, give)


CONV_BLOCK0 = 6
CONV_TAPS = 4
HALO = 8


def conv_fwd(proj, conv_w, conv_b):
    t = proj.shape[0]
    tm = _tile(t, ROW_TILE)
    r = tm // HALO

    def body(x_ref, halo_ref, w_ref, b_ref, o_ref):
        i = pl.program_id(1)
        halo = jnp.where(i > 0, halo_ref[...], 0.0)
        ext = jnp.concatenate([halo, x_ref[...]], axis=0)
        pre = b_ref[...] + w_ref[CONV_TAPS - 1:CONV_TAPS, :] * ext[HALO:, :]
        for tap in range(CONV_TAPS - 1):
            pre = pre + w_ref[tap:tap + 1, :] * pltpu.roll(ext, CONV_TAPS - 1 - tap, axis=0)[HALO:, :]
        o_ref[...] = pre * _sigmoid(pre)

    return pl.pallas_call(
        body, name="conv_fwd", grid=(CONV_DIM // D, t // tm),
        out_shape=jax.ShapeDtypeStruct((t, CONV_DIM), F32),
        in_specs=[pl.BlockSpec((tm, D), lambda cb, i: (i, CONV_BLOCK0 + cb)),
                  pl.BlockSpec((HALO, D), lambda cb, i: (jnp.maximum(i * r - 1, 0), CONV_BLOCK0 + cb)),
                  pl.BlockSpec((CONV_TAPS, D), lambda cb, i: (0, cb)), pl.BlockSpec((1, D), lambda cb, i: (0, cb))],
        out_specs=pl.BlockSpec((tm, D), lambda cb, i: (i, cb)),
        compiler_params=_cparams(("parallel", "parallel")),
    )(proj, proj, conv_w, conv_b)


def conv_bwd(proj, dxc, conv_w, conv_b):
    t = proj.shape[0]
    tm = _tile(t, ROW_TILE)
    r = tm // HALO
    n = t // tm
    last_halo = t // HALO - 1

    def body(x_ref, prev_ref, next_ref, d_ref, dnext_ref, w_ref, b_ref, dx_ref, dw_ref, db_ref):
        i = pl.program_id(1)

        @pl.when(i == 0)
        def _():
            dw_ref[...] = jnp.zeros_like(dw_ref)
            db_ref[...] = jnp.zeros_like(db_ref)

        prev = jnp.where(i > 0, prev_ref[...], 0.0)
        ext = jnp.concatenate([prev, x_ref[...], next_ref[...]], axis=0)
        shifted = [pltpu.roll(ext, CONV_TAPS - 1 - tap, axis=0)[HALO:, :] for tap in range(CONV_TAPS - 1)]
        shifted.append(ext[HALO:, :])
        pre = b_ref[...]
        for tap in range(CONV_TAPS):
            pre = pre + w_ref[tap:tap + 1, :] * shifted[tap]
        s = _sigmoid(pre)
        d_ext = jnp.concatenate([d_ref[...], jnp.where(i < n - 1, dnext_ref[...], 0.0)], axis=0)
        dpre = d_ext * _dsilu(pre, s)
        dx = w_ref[CONV_TAPS - 1:CONV_TAPS, :] * dpre[:tm, :]
        for tap in range(CONV_TAPS - 1):
            back = CONV_TAPS - 1 - tap
            dx = dx + w_ref[tap:tap + 1, :] * pltpu.roll(dpre, tm + HALO - back, axis=0)[:tm, :]
        dx_ref[...] = dx.astype(BF16)
        dp = dpre[:tm, :]
        db_ref[...] += _colsum(dp)
        for tap in range(CONV_TAPS):
            dw_ref[tap:tap + 1, :] += _colsum(dp * shifted[tap][:tm, :])

    return pl.pallas_call(
        body, name="conv_bwd", grid=(CONV_DIM // D, n),
        out_shape=[jax.ShapeDtypeStruct((t, CONV_DIM), BF16), jax.ShapeDtypeStruct((8, CONV_DIM), F32),
                   jax.ShapeDtypeStruct((1, CONV_DIM), F32)],
        in_specs=[pl.BlockSpec((tm, D), lambda cb, i: (i, CONV_BLOCK0 + cb)),
                  pl.BlockSpec((HALO, D), lambda cb, i: (jnp.maximum(i * r - 1, 0), CONV_BLOCK0 + cb)),
                  pl.BlockSpec((HALO, D), lambda cb, i: (jnp.minimum((i + 1) * r, last_halo), CONV_BLOCK0 + cb)),
                  pl.BlockSpec((tm, D), lambda cb, i: (i, cb)),
                  pl.BlockSpec((HALO, D), lambda cb, i: (jnp.minimum((i + 1) * r, last_halo), cb)),
                  pl.BlockSpec((CONV_TAPS, D), lambda cb, i: (0, cb)), pl.BlockSpec((1, D), lambda cb, i: (0, cb))],
        out_specs=[pl.BlockSpec((tm, D), lambda cb, i: (i, cb)), pl.BlockSpec((8, D), lambda cb, i: (0, cb)),
                   pl.BlockSpec((1, D), lambda cb, i: (0, cb))],
        compiler_params=_cparams(("parallel", "arbitrary")),
    )(proj, proj, proj, dxc, dxc, conv_w, conv_b)


Z_BLOCK0 = 8
DT_BLOCK0 = 88
B_BLOCK0 = 16
C_BLOCK0 = 20


def _head_expand():
    e = np.zeros((N_STATE, GROUP_W), np.float32)
    for hh in range(HEADS_PER_GROUP):
        e[hh, hh * HEAD_P:(hh + 1) * HEAD_P] = 1.0
    return jnp.asarray(e, BF16)


def _ssd_chunk_terms(dt, bias, alog, expand, tril_f, eye):
    dtb = dt + bias
    delta = jnp.maximum(dtb, 0.0) + jnp.log(1.0 + jnp.exp(-jnp.abs(dtb)))
    ea = jnp.exp(alog)
    a = -ea * delta
    acum = _sel(_nn, a, tril_f, 3, x_first=False)
    delta_e = _sel(_nn, delta, expand, 2)
    acum_e = _sel(_nn, acum, expand, 3)
    acum_t = _sel(_nt, acum, eye, 3, x_first=False)
    return dtb, delta, ea, a, acum, delta_e, acum_e, acum_t


def ssd_fwd(proj, xc, alog4, bias4, dskip4, wnorm, expand):
    t = proj.shape[0]
    tb = _tile(t, TOKEN_BLOCK)
    ncb = tb // SSD_CHUNK

    def body(xs_ref, b_ref, c_ref, dt_ref, z_ref, alog_ref, bias_ref, dsk_ref, wn_ref, e_ref, ob_ref, st_ref, state):
        @pl.when(pl.program_id(1) == 0)
        def _():
            state[...] = jnp.zeros_like(state)

        expand = e_ref[...]
        mask = _tri(SSD_CHUNK)
        tril_f = mask.astype(BF16)
        eye = (lax.broadcasted_iota(jnp.int32, (N_STATE, N_STATE), 0) ==
               lax.broadcasted_iota(jnp.int32, (N_STATE, N_STATE), 1)).astype(BF16)
        alog, bias = alog_ref[0], bias_ref[0]
        d_e = _sel(_nn, jnp.broadcast_to(dsk_ref[0], (8, N_STATE)), expand, 3)[0:1, :]
        wn = wn_ref[...]

        def chunk(c, carry):
            sl = pl.ds(pl.multiple_of(c * SSD_CHUNK, SSD_CHUNK), SSD_CHUNK)
            xs, bm, cm, dt, z = xs_ref[sl, :], b_ref[sl, :], c_ref[sl, :], dt_ref[sl, :], z_ref[sl, :]
            dtb, delta, ea, a, acum, delta_e, acum_e, acum_t = _ssd_chunk_terms(dt, bias, alog, expand, tril_f, eye)
            alast_e = acum_e[SSD_CHUNK - 1:SSD_CHUNK, :]
            xd = xs * delta_e
            xdb = xd.astype(BF16)
            cb_, bb_ = cm.astype(BF16), bm.astype(BF16)
            cbm = _nt(cb_, bb_)
            ys = []
            for hh in range(HEADS_PER_GROUP):
                lh = jnp.where(mask, jnp.exp(jnp.minimum(acum[:, hh:hh + 1] - acum_t[hh:hh + 1, :], 0.0)), 0.0)
                ys.append(_nn((cbm * lh).astype(BF16), xdb[:, hh * HEAD_P:(hh + 1) * HEAD_P]))
            st = state[...]
            st_ref[0, c] = st
            y = jnp.concatenate(ys, axis=1) + _nn(cb_, st.astype(BF16)) * jnp.exp(acum_e) + xs * d_e
            state[...] = st * jnp.exp(alast_e) + _tn(bb_, (xd * jnp.exp(alast_e - acum_e)).astype(BF16))
            yg = y * z * _sigmoid(z)
            ob_ref[sl, :] = (yg * lax.rsqrt(jnp.mean(yg * yg, axis=-1, keepdims=True) + RMS_EPS) * wn).astype(BF16)
            return carry

        lax.fori_loop(0, ncb, chunk, 0, unroll=min(CHUNK_UNROLL, ncb))

    small = pl.BlockSpec((1, 1, N_STATE), lambda g, j: (g, 0, 0))
    return pl.pallas_call(
        body, name="ssd_fwd", grid=(N_GROUPS, t // tb),
        out_shape=[jax.ShapeDtypeStruct((t, B_INNER), BF16),
                   jax.ShapeDtypeStruct((N_GROUPS, t // SSD_CHUNK, N_STATE, GROUP_W), F32)],
        in_specs=[pl.BlockSpec((tb, GROUP_W), lambda g, j: (j, g)),
                  pl.BlockSpec((tb, N_STATE), lambda g, j: (j, B_BLOCK0 + g)),
                  pl.BlockSpec((tb, N_STATE), lambda g, j: (j, C_BLOCK0 + g)),
                  pl.BlockSpec((tb, N_STATE), lambda g, j: (j, DT_BLOCK0 + g)),
                  pl.BlockSpec((tb, GROUP_W), lambda g, j: (j, Z_BLOCK0 + g)),
                  small, small, small, pl.BlockSpec((1, GROUP_W), lambda g, j: (0, g)),
                  pl.BlockSpec((N_STATE, GROUP_W), lambda g, j: (0, 0))],
        out_specs=[pl.BlockSpec((tb, GROUP_W), lambda g, j: (j, g)),
                   pl.BlockSpec((1, ncb, N_STATE, GROUP_W), lambda g, j: (g, j, 0, 0))],
        scratch_shapes=[pltpu.VMEM((N_STATE, GROUP_W), F32)],
        compiler_params=_cparams(("parallel", "arbitrary")),
    )(xc, xc, xc, proj, proj, alog4, bias4, dskip4, wnorm, expand)


def ssd_bwd(proj, xc, alog4, bias4, dskip4, wnorm, expand, dob, states, part):
    t = proj.shape[0]
    tb = _tile(t, TOKEN_BLOCK)
    ncb = tb // SSD_CHUNK
    nb = t // tb

    def body(xs_ref, b_ref, c_ref, dt_ref, z_ref, alog_ref, bias_ref, dsk_ref, wn_ref, e_ref, dob_ref, st_ref, part_ref,
             dxs_ref, db_ref, dc_ref, dz_ref, ddt_ref, dwn_ref, dalog_ref, dbias_ref, ddsk_ref, parts_ref, dstate,
             send_sems, recv_sems, local_sem):
        xchg_start, xchg_wait = _chip_exchange(part_ref, parts_ref, send_sems, recv_sems, local_sem)

        @pl.when((pl.program_id(0) == 0) & (pl.program_id(1) == 0))
        def _():
            xchg_start()

        @pl.when(pl.program_id(1) == 0)
        def _():
            dstate[...] = jnp.zeros_like(dstate)
            dwn_ref[...] = jnp.zeros_like(dwn_ref)
            dalog_ref[...] = jnp.zeros_like(dalog_ref)
            dbias_ref[...] = jnp.zeros_like(dbias_ref)
            ddsk_ref[...] = jnp.zeros_like(ddsk_ref)

        expand = e_ref[...]
        mask = _tri(SSD_CHUNK)
        mask_t = _tri(SSD_CHUNK, upper=True)
        tril_f = mask.astype(BF16)
        triu_f = mask_t.astype(BF16)
        eye = (lax.broadcasted_iota(jnp.int32, (N_STATE, N_STATE), 0) ==
               lax.broadcasted_iota(jnp.int32, (N_STATE, N_STATE), 1)).astype(BF16)
        alog, bias = alog_ref[0], bias_ref[0]
        d_e = _sel(_nn, jnp.broadcast_to(dsk_ref[0], (8, N_STATE)), expand, 3)[0:1, :]
        wn = wn_ref[...]

        def chunk(i, c0):
            c = ncb - 1 - i
            sl = pl.ds(pl.multiple_of(c * SSD_CHUNK, SSD_CHUNK), SSD_CHUNK)
            xs, bm, cm, dt, z = xs_ref[sl, :], b_ref[sl, :], c_ref[sl, :], dt_ref[sl, :], z_ref[sl, :]
            dtb, delta, ea, a, acum, delta_e, acum_e, acum_t = _ssd_chunk_terms(dt, bias, alog, expand, tril_f, eye)
            alast_e = acum_e[SSD_CHUNK - 1:SSD_CHUNK, :]
            eacum = jnp.exp(acum_e)
            wl = jnp.exp(alast_e - acum_e)
            xd = xs * delta_e
            xdb = xd.astype(BF16)
            cb_, bb_ = cm.astype(BF16), bm.astype(BF16)
            cbm = _nt(cb_, bb_)
            cbm_t = _nt(bb_, cb_)
            st32 = st_ref[0, c]
            stb = st32.astype(BF16)
            dst = dstate[...]
            dstb = dst.astype(BF16)
            lhs, lhts, ys = [], [], []
            for hh in range(HEADS_PER_GROUP):
                col, row = acum[:, hh:hh + 1], acum_t[hh:hh + 1, :]
                lh = jnp.where(mask, jnp.exp(jnp.minimum(col - row, 0.0)), 0.0)
                lht = jnp.where(mask_t, jnp.exp(jnp.minimum(row - col, 0.0)), 0.0)
                lhs.append(lh)
                lhts.append(lht)
                ys.append(_nn((cbm * lh).astype(BF16), xdb[:, hh * HEAD_P:(hh + 1) * HEAD_P]))
            y_in = jnp.concatenate(ys, axis=1)
            y_out = _nn(cb_, stb) * eacum
            y = y_in + y_out + xs * d_e
            sgz = _sigmoid(z)
            sz = z * sgz
            yg = y * sz
            rstd = lax.rsqrt(jnp.mean(yg * yg, axis=-1, keepdims=True) + RMS_EPS)
            nrm = yg * rstd
            dob_v = dob_ref[sl, :]
            dn = dob_v * wn
            dwn_ref[...] += _colsum(dob_v * nrm)
            dyg = rstd * (dn - nrm * jnp.mean(dn * nrm, axis=-1, keepdims=True))
            dy = dyg * sz
            dz_ref[sl, :] = (dyg * y * _dsilu(z, sgz)).astype(BF16)
            dyb = dy.astype(BF16)
            dxds = []
            dcb = jnp.zeros((SSD_CHUNK, SSD_CHUNK), F32)
            dcb_t = jnp.zeros((SSD_CHUNK, SSD_CHUNK), F32)
            for hh in range(HEADS_PER_GROUP):
                hs = slice(hh * HEAD_P, (hh + 1) * HEAD_P)
                dy_h, x_h = dyb[:, hs], xdb[:, hs]
                dxds.append(_nn((cbm_t * lhts[hh]).astype(BF16), dy_h))
                dcb = dcb + _nt(dy_h, x_h) * lhs[hh]
                dcb_t = dcb_t + _nt(x_h, dy_h) * lhts[hh]
            dye = (dy * eacum).astype(BF16)
            xw = (xd * wl).astype(BF16)
            dxd_in = jnp.concatenate(dxds, axis=1)
            dxd_out = wl * _nn(bb_, dstb)
            dxd = dxd_in + dxd_out
            dc_ref[sl, :] = _nn(dcb.astype(BF16), bb_) + _nt(dye, stb)
            db_ref[sl, :] = _nn(dcb_t.astype(BF16), cb_) + _nt(xw, dstb)
            dstate[...] = dst * jnp.exp(alast_e) + _tn(cb_, dye)
            col_out = xd * dxd_out
            dac = _sel(_nt, dyb.astype(F32) * y_in - xdb.astype(F32) * dxd_in + dy * y_out - col_out, expand, 3)
            beyond = _colsum(col_out) + jnp.exp(alast_e) * _colsum(dst * st32)
            da = (_sel(_nn, dac, triu_f, 3, x_first=False) +
                  _sel(_nt, jnp.broadcast_to(beyond, (8, GROUP_W)), expand, 3)[0:1, :])
            ddelta = _sel(_nt, dxd * xs, expand, 2) - da * ea
            dalog_ref[0] += _colsum(da * a)
            ddtb = ddelta * _sigmoid(dtb)
            dbias_ref[0] += _colsum(ddtb)
            ddt_ref[sl, :] = ddtb.astype(BF16)
            ddsk_ref[0] += _sel(_nt, jnp.broadcast_to(_colsum(dy * xs), (8, GROUP_W)), expand, 3)[0:1, :]
            dxs_ref[sl, :] = dxd * delta_e + dy * d_e
            return c0

        lax.fori_loop(0, ncb, chunk, 0, unroll=min(CHUNK_UNROLL, ncb))

        @pl.when((pl.program_id(0) == N_GROUPS - 1) & (pl.program_id(1) == nb - 1))
        def _():
            xchg_wait()

    small = pl.BlockSpec((1, 1, N_STATE), lambda g, j: (g, 0, 0))
    wide = pl.BlockSpec((tb, GROUP_W), lambda g, j: (nb - 1 - j, g))
    narrow = pl.BlockSpec((tb, N_STATE), lambda g, j: (nb - 1 - j, g))
    hbm = pl.BlockSpec(memory_space=pl.ANY)
    return pl.pallas_call(
        body, name="ssd_bwd", grid=(N_GROUPS, nb),
        out_shape=[jax.ShapeDtypeStruct((t, B_INNER), F32), jax.ShapeDtypeStruct((t, GROUP_W), F32),
                   jax.ShapeDtypeStruct((t, GROUP_W), F32), jax.ShapeDtypeStruct((t, B_INNER), BF16),
                   jax.ShapeDtypeStruct((t, GROUP_W), BF16), jax.ShapeDtypeStruct((1, B_INNER), F32),
                   jax.ShapeDtypeStruct((N_GROUPS, 1, N_STATE), F32), jax.ShapeDtypeStruct((N_GROUPS, 1, N_STATE), F32),
                   jax.ShapeDtypeStruct((N_GROUPS, 1, N_STATE), F32), jax.ShapeDtypeStruct(part.shape, part.dtype)],
        in_specs=[wide,
                  pl.BlockSpec((tb, N_STATE), lambda g, j: (nb - 1 - j, B_BLOCK0 + g)),
                  pl.BlockSpec((tb, N_STATE), lambda g, j: (nb - 1 - j, C_BLOCK0 + g)),
                  pl.BlockSpec((tb, N_STATE), lambda g, j: (nb - 1 - j, DT_BLOCK0 + g)),
                  pl.BlockSpec((tb, GROUP_W), lambda g, j: (nb - 1 - j, Z_BLOCK0 + g)),
                  small, small, small, pl.BlockSpec((1, GROUP_W), lambda g, j: (0, g)),
                  pl.BlockSpec((N_STATE, GROUP_W), lambda g, j: (0, 0)), wide,
                  pl.BlockSpec((1, ncb, N_STATE, GROUP_W), lambda g, j: (g, nb - 1 - j, 0, 0)), hbm],
        out_specs=[wide, narrow, narrow, wide, narrow, pl.BlockSpec((1, GROUP_W), lambda g, j: (0, g)),
                   small, small, small, hbm],
        scratch_shapes=[pltpu.VMEM((N_STATE, GROUP_W), F32)] + CHIP_SEMS,
        compiler_params=_cparams(("arbitrary", "arbitrary")),
    )(xc, xc, xc, proj, proj, alog4, bias4, dskip4, wnorm, expand, dob, states, part)


def lower_bound_fwd(hgrn_lb):
    def body(a_ref, o_ref):
        a0, a1 = a_ref[0:1, :], a_ref[1:2, :]
        m = jnp.maximum(a0, a1)
        e0, e1 = jnp.exp(a0 - m), jnp.exp(a1 - m)
        o_ref[...] = e0 / (e0 + e1)

    return pl.pallas_call(body, name="lower_bound_fwd", out_shape=jax.ShapeDtypeStruct((1, D), F32))(hgrn_lb)


def ada_weight_grad(c_all, dmod_cols):
    def body(c_ref, d_ref, o_ref):
        cval = c_ref[...]
        o_ref[...] = _tn(cval * _sigmoid(cval), d_ref[...], HI)

    return pl.pallas_call(body, name="ada_weight_grad",
                          out_shape=jax.ShapeDtypeStruct((D, dmod_cols.shape[1]), F32))(c_all, dmod_cols)


def reduce_small(gathered, hgrn_lb, dlb_off):
    n = gathered.shape[2]

    def body(g_ref, a_ref, o_ref, glb_ref):
        s = g_ref[0]
        for d in range(1, N_DEV):
            s = s + g_ref[d]
        o_ref[...] = s
        a0, a1 = a_ref[0:1, :], a_ref[1:2, :]
        m = jnp.maximum(a0, a1)
        e0, e1 = jnp.exp(a0 - m), jnp.exp(a1 - m)
        p0 = e0 / (e0 + e1)
        tq = s[:, dlb_off:dlb_off + D] * p0 * (1.0 - p0)
        glb_ref[0:1, :] = tq
        glb_ref[1:2, :] = -tq

    return pl.pallas_call(body, name="reduce_small",
                          out_shape=[jax.ShapeDtypeStruct((1, n), F32), jax.ShapeDtypeStruct((2, D), F32)])(gathered, hgrn_lb)


def _adam_math(w, g, m, v):
    m2 = ADAM_B1 * m + (1.0 - ADAM_B1) * g
    v2 = ADAM_B2 * v + (1.0 - ADAM_B2) * (g * g)
    m_hat = m2 / (1.0 - ADAM_B1 ** ADAM_STEP)
    v_hat = v2 / (1.0 - ADAM_B2 ** ADAM_STEP)
    delta = -ADAM_LR * (m_hat / (jnp.sqrt(v_hat) + ADAM_EPS) + ADAM_WD * w)
    return delta, m2, v2


def _row_tile(rows, mult=8, cap=128):
    for cand in range(cap - cap % mult, 0, -mult):
        if rows % cand == 0:
            return cand
    return rows


def sum_parts(parts, name):
    n, rows, cols = parts.shape
    tr = _row_tile(rows, 16, 256)

    def body(p_ref, o_ref):
        s = p_ref[0].astype(F32)
        for d in range(1, n):
            s = s + p_ref[d].astype(F32)
        o_ref[...] = s

    return pl.pallas_call(
        body, name=name, grid=(rows // tr,),
        out_shape=jax.ShapeDtypeStruct((rows, cols), F32),
        in_specs=[pl.BlockSpec((n, tr, cols), lambda i: (0, i, 0))],
        out_specs=pl.BlockSpec((tr, cols), lambda i: (i, 0)),
        compiler_params=_cparams(("parallel",)),
    )(parts)


def sum_pair(a, b, name):
    rows, cols = a.shape
    tr = _row_tile(rows, 16, 256)

    def body(a_ref, b_ref, o_ref):
        o_ref[...] = (a_ref[...].astype(F32) + b_ref[...].astype(F32)).astype(o_ref.dtype)

    blk = pl.BlockSpec((tr, cols), lambda i: (i, 0))
    return pl.pallas_call(
        body, name=name, grid=(rows // tr,),
        out_shape=jax.ShapeDtypeStruct((rows, cols), a.dtype),
        in_specs=[blk, blk], out_specs=blk,
        compiler_params=_cparams(("parallel",)),
    )(a, b)


def adamw(w, g, m, v, name):
    rows, cols = w.shape
    tr = _row_tile(rows)

    def body(w_ref, g_ref, m_ref, v_ref, d_ref, m2_ref, v2_ref):
        delta, m2, v2 = _adam_math(w_ref[...], g_ref[...], m_ref[...], v_ref[...])
        d_ref[...] = delta
        m2_ref[...] = m2
        v2_ref[...] = v2

    blk = pl.BlockSpec((tr, cols), lambda i: (i, 0))
    return pl.pallas_call(
        body, name=name, grid=(rows // tr,),
        out_shape=[jax.ShapeDtypeStruct((rows, cols), F32)] * 3,
        in_specs=[blk] * 4, out_specs=[blk] * 3,
        compiler_params=_cparams(("parallel",)),
    )(w, g, m, v)


def _pad128(n):
    return -(-n // 128) * 128


def _pack(arrays):
    offs, parts, off = [], [], 0
    for a in arrays:
        flat = a.reshape(1, -1)
        n = flat.shape[1]
        offs.append(off)
        parts.append(jnp.pad(flat, ((0, 0), (0, _pad128(n) - n))))
        off += _pad128(n)
    return jnp.concatenate(parts, axis=1), offs


def _unpack(vec, offs, shapes):
    out = []
    for off, shp in zip(offs, shapes):
        n = int(np.prod(shp))
        out.append(vec[0, off:off + n].reshape(shp))
    return out


IN_ROWS = IN_DIM // N_DEV
DT_ROW0 = 9216
DT_DEV, DT_LO = divmod(DT_ROW0, IN_ROWS)


def _in_row_pieces(tile):
    pieces = []
    if tile == N_PROJ // D - 1:
        for g in range(N_GROUPS):
            o = DT_ROW0 + HEADS_PER_GROUP * g
            pieces.append((N_STATE * g, o // IN_ROWS, o % IN_ROWS, HEADS_PER_GROUP))
        return pieces
    r, end = tile * D, (tile + 1) * D
    while r < end:
        o = r if r < DT_ROW0 else r + 32
        dev, loc = divmod(o, IN_ROWS)
        n = min(end - r, IN_ROWS - loc)
        if r < DT_ROW0:
            n = min(n, DT_ROW0 - r)
        pieces.append((r - tile * D, dev, loc, n))
        r += n
    return pieces


def assemble_w_in(g_all):
    ntile = N_PROJ // D

    def body(g_ref, o_ref):
        j = pl.program_id(0)
        for tile in range(ntile):
            @pl.when(j == tile)
            def _(tile=tile):
                if tile == ntile - 1:
                    o_ref[...] = jnp.zeros_like(o_ref)
                for dst, dev, loc, n in _in_row_pieces(tile):
                    o_ref[pl.ds(dst, n), :] = g_ref[dev, pl.ds(loc, n), :]

    return pl.pallas_call(
        body, name="assemble_w_in", grid=(ntile,),
        out_shape=jax.ShapeDtypeStruct((N_PROJ, D), g_all.dtype),
        in_specs=[pl.BlockSpec(memory_space=pltpu.VMEM)],
        out_specs=pl.BlockSpec((D, D), lambda j: (j, 0)),
        compiler_params=_cparams(("arbitrary",)),
    )(g_all)


def _grad_in_blocks(g_t, core, slot):
    dt = g_t[11264:11264 + N_GROUPS * N_STATE].reshape(N_GROUPS, N_STATE, D)[:, :HEADS_PER_GROUP].reshape(32, D)
    with_dt = jnp.concatenate([g_t[DT_DEV * IN_ROWS:DT_ROW0], dt, g_t[DT_ROW0:(DT_DEV + 1) * IN_ROWS - 32]], axis=0)
    blocks = []
    for q in range(N_CHIP):
        if 2 * q + 1 < DT_DEV:
            blk = lax.dynamic_slice_in_dim(g_t, IN_ROWS * (2 * q + core), IN_ROWS, axis=0)
        else:
            assert 2 * q == DT_DEV
            after = g_t[(DT_DEV + 1) * IN_ROWS - 32:(DT_DEV + 2) * IN_ROWS - 32]
            blk = jnp.where(core == 0, with_dt, after)
        blocks.append(jnp.pad(blk, ((0, slot - IN_ROWS), (0, 0))))
    return jnp.stack(blocks)


def kernel(x, c, w_ada, b_ada, w_in, hgrn_lb, hgrn_gnorm, ssm_conv_w, ssm_conv_b, ssm_dt_bias, ssm_a_log, ssm_d, ssm_norm, w_branch_a, w_branch_b, w_o, ln1_g, ln1_b, w_ffn_gate, w_ffn_up, w_ffn_down, ln2_g, ln2_b, loss_target, m_w_ada, m_b_ada, m_w_in, m_hgrn_lb, m_hgrn_gnorm, m_ssm_conv_w, m_ssm_conv_b, m_ssm_dt_bias, m_ssm_a_log, m_ssm_d, m_ssm_norm, m_w_branch_a, m_w_branch_b, m_w_o, m_ln1_g, m_ln1_b, m_w_ffn_gate, m_w_ffn_up, m_w_ffn_down, m_ln2_g, m_ln2_b, v_w_ada, v_b_ada, v_w_in, v_hgrn_lb, v_hgrn_gnorm, v_ssm_conv_w, v_ssm_conv_b, v_ssm_dt_bias, v_ssm_a_log, v_ssm_d, v_ssm_norm, v_w_branch_a, v_w_branch_b, v_w_o, v_ln1_g, v_ln1_b, v_w_ffn_gate, v_w_ffn_up, v_w_ffn_down, v_ln2_g, v_ln2_b):
    me = 4 * lax.axis_index("x") + 2 * lax.axis_index("y") + lax.axis_index("c")
    xt = x[0]
    tgt = loss_target[0]
    t = xt.shape[0]
    ada_cols = w_ada.shape[2]
    conv_cols = ssm_conv_w.shape[2]

    small_in, _ = _pack([c, ssm_conv_w[0]])
    small_all = allgather_vmem(small_in, "allgather_small_inputs")
    c_all = small_all[:, 0, :D]
    conv_w = small_all[:, 0, D:D + CONV_TAPS * conv_cols].reshape(N_DEV, CONV_TAPS, conv_cols)
    conv_w = conv_w.transpose(1, 0, 2).reshape(CONV_TAPS, CONV_DIM)
    mod = ada_modulation(c_all, w_ada[0], b_ada.reshape(N_DEV, 1, ada_cols))
    mod6 = mod.reshape(6, D)

    shards = [w_in[0].T, w_branch_a[0], w_branch_b[0], w_o[0], w_ffn_gate[0].T, w_ffn_up[0].T, w_ffn_down[0]]
    shard_rows = [s.shape[0] for s in shards]
    slot_rows = [-(-r // 32) * 32 for r in shard_rows]
    row_offs = [sum(slot_rows[:i]) for i in range(len(shards))]
    padded = [jnp.pad(s.astype(BF16), ((0, p - r), (0, 0))) for s, r, p in zip(shards, shard_rows, slot_rows)]
    w_in_t = assemble_w_in(allgather_hbm(padded[0], "allgather_w_in"))

    lb = lower_bound_fwd(hgrn_lb)
    u1 = ln_modulate(xt, mod6, 0, 1, "ln_modulate_1")
    proj, g_rest = mm_nt_gather(u1, w_in_t, F32, jnp.concatenate(padded[1:], axis=0), "mm_in_proj")
    g_ba, g_bb, g_o, g_fg, g_fu, g_fd = (g_rest[:, o - slot_rows[0]:o - slot_rows[0] + r]
                                         for o, r in zip(row_offs[1:], shard_rows[1:]))
    w_ba = g_ba.reshape(D, D)
    w_bb = g_bb.reshape(B_INNER, D)
    w_oo = g_o.reshape(D, D)
    ffpad = ((0, D_FF_PAD - D_FF), (0, 0))
    w_gu_t = jnp.concatenate([jnp.pad(g_fg.reshape(D_FF, D), ffpad), jnp.pad(g_fu.reshape(D_FF, D), ffpad)], axis=0)
    w_dn = jnp.pad(g_fd.reshape(D_FF, D), ffpad)
    o_a, o_raw, st_a = hgrn_fwd(proj, lb, hgrn_gnorm)
    xc = conv_fwd(proj, conv_w, ssm_conv_b)
    pad3 = ((0, 0), (0, 0), (0, N_STATE - HEADS_PER_GROUP))
    alog4 = jnp.pad(ssm_a_log.reshape(N_GROUPS, 1, HEADS_PER_GROUP), pad3)
    bias4 = jnp.pad(ssm_dt_bias.reshape(N_GROUPS, 1, HEADS_PER_GROUP), pad3)
    dskip4 = jnp.pad(ssm_d.reshape(N_GROUPS, 1, HEADS_PER_GROUP), pad3)
    expand = _head_expand()
    o_b, st_b = ssd_fwd(proj, xc, alog4, bias4, dskip4, ssm_norm, expand)
    ya = mm_nn(o_a, w_ba, F32, "mm_branch_a")
    yb = mm_nn(o_b, w_bb, F32, "mm_branch_b")
    merged = merge_gates(ya, yb, proj)
    h1 = mm_nn(merged, w_oo, F32, "mm_out_proj")
    x1 = resid_ln(xt, h1, mod6, 2, ln1_g, ln1_b, "resid_ln_1")
    u2 = ln_modulate(x1, mod6, 3, 4, "ln_modulate_2")
    gu = mm_nt(u2, w_gu_t, F32, "mm_ffn_in")
    act = swiglu_act(gu)
    h2 = mm_nn(act, w_dn, F32, "mm_ffn_out")

    dh2, dx1_part, acc4 = resid_ln_bwd(x1, h2, mod6, 5, ln2_g, ln2_b, tgt, True, "resid_ln_2_bwd")
    g_dn = mm_tn(act, dh2, "mm_grad_ffn_down")
    dact = mm_nt(dh2, w_dn, F32, "mm_dact")
    dgu = swiglu_act_bwd(gu, dact)
    g_gu_t = mm_tn(dgu, u2, "mm_grad_ffn_in")
    du2 = mm_nn(dgu, w_gu_t, F32, "mm_du2")
    dx1, acc3 = ln_modulate_bwd(x1, du2, mod6, 4, dx1_part, "ln_modulate_2_bwd")
    dh1, dx_part, acc2 = resid_ln_bwd(xt, h1, mod6, 2, ln1_g, ln1_b, dx1, False, "resid_ln_1_bwd")
    g_o = mm_tn(merged, dh1, "mm_grad_out_proj")
    dmerged = mm_nt(dh1, w_oo, F32, "mm_dmerged")
    dya, dyb, dga, dgb = merge_gates_bwd(dmerged, ya, yb, proj)
    g_ba_full = mm_tn(o_a, dya, "mm_grad_branch_a")
    g_bb_full = mm_tn(o_b, dyb, "mm_grad_branch_b")
    doa = mm_nt(dya, w_ba, F32, "mm_doa")
    dob = mm_nt(dyb, w_bb, F32, "mm_dob")
    my_core = lax.axis_index("c")

    def by_core(blocks, rows, slots):
        contrib = jnp.concatenate([jnp.pad(b.reshape(N_DEV, -1, D), ((0, 0), (0, p - r), (0, 0)))
                                   for b, r, p in zip(blocks, rows, slots)], axis=1)
        split = contrib.reshape(N_CHIP, 2, contrib.shape[1], D).transpose(1, 0, 2, 3)
        return (lax.dynamic_index_in_dim(split, my_core, 0, keepdims=False),
                lax.dynamic_index_in_dim(split, 1 - my_core, 0, keepdims=False))

    keep_e, give_e = by_core([g_ba_full, g_bb_full, g_o, g_gu_t[:D_FF], g_gu_t[D_FF_PAD:D_FF_PAD + D_FF], g_dn[:D_FF]],
                             shard_rows[1:], slot_rows[1:])
    dq, dfl, di, dg, dlb, dgn, got_e = hgrn_bwd(proj, lb, hgrn_gnorm, o_raw, doa, st_a, give_e)
    chip_e = sum_pair(keep_e.reshape(-1, D), got_e.reshape(-1, D), "sum_grads_rest_chip").reshape(keep_e.shape)
    dxs, dbm, dcm, dz, ddt, dwn, dalog, dbias, ddsk, parts_e = ssd_bwd(proj, xc, alog4, bias4, dskip4, ssm_norm, expand,
                                                                       dob, st_b, chip_e)
    dxc = jnp.concatenate([dxs, dbm, dcm], axis=1)
    dxbc, dcw, dcb = conv_bwd(proj, dxc, conv_w, ssm_conv_b)
    dproj = jnp.concatenate([dq, dfl, di, dg, dz, dxbc, dga, dgb, ddt, jnp.zeros((t, D - N_GROUPS * N_STATE), BF16)], axis=1)
    g_in_t = mm_tn(dproj, u1, "mm_grad_in_proj")
    keep_l = _grad_in_blocks(g_in_t, my_core, slot_rows[0])
    give_l = _grad_in_blocks(g_in_t, 1 - my_core, slot_rows[0])
    got_l = exchange_sibling(give_l, "exchange_grad_in_sibling")
    chip_l = sum_pair(keep_l.reshape(-1, D), got_l.reshape(-1, D), "sum_grad_in_chip").reshape(keep_l.shape)
    du1, parts_l = mm_nn_exchange(dproj, w_in_t, F32, chip_l, "mm_du1")
    dx, acc1 = ln_modulate_bwd(xt, du1, mod6, 1, dx_part, "ln_modulate_1_bwd")
    gw_in = sum_parts(parts_l, "sum_grad_in")[:shard_rows[0]].T
    g_rows = sum_parts(parts_e, "sum_grads_rest")
    gw_ba, gw_bb, gw_o, gw_fg, gw_fu, gw_fd = (g_rows[o - slot_rows[0]:o - slot_rows[0] + r]
                                               for o, r in zip(row_offs[1:], shard_rows[1:]))
    gw_fg, gw_fu = gw_fg.T, gw_fu.T

    dmod = jnp.concatenate([acc1[1:2], acc1[0:1], acc2[0:1], acc3[1:2], acc3[0:1], acc4[0:1]], axis=1)
    small_fields = [dmod, acc4[3:4, :128], dlb, dgn, dcw[:CONV_TAPS], dcb, dbias, dalog, ddsk, dwn,
                    acc2[1:2], acc2[2:3], acc4[1:2], acc4[2:3]]
    small_out, offs = _pack(small_fields)
    small_sum_in = allgather_vmem(small_out, "allgather_small_grads")
    gsum, g_lb = reduce_small(small_sum_in, hgrn_lb, offs[2])
    (g_bada, loss_row, _, g_gn, g_cw_full, g_cb, g_bias4, g_alog4, g_dsk4, g_wn, g_l1g, g_l1b, g_l2g, g_l2b) = _unpack(
        gsum, offs, [(1, 6 * D), (1, 128), (1, D), (1, HK), (CONV_TAPS, CONV_DIM), (1, CONV_DIM),
                     (N_GROUPS, N_STATE), (N_GROUPS, N_STATE), (N_GROUPS, N_STATE), (1, B_INNER),
                     (1, D), (1, D), (1, D), (1, D)])
    loss = loss_row[0, 0]
    g_cw = lax.dynamic_slice(g_cw_full, (0, me * conv_cols), (CONV_TAPS, conv_cols))[None]
    g_dtb = g_bias4[:, :HEADS_PER_GROUP].reshape(1, 32)
    g_alog = g_alog4[:, :HEADS_PER_GROUP].reshape(1, 32)
    g_dsk = g_dsk4[:, :HEADS_PER_GROUP].reshape(1, 32)

    dmod_all = small_sum_in[:, 0, offs[0]:offs[0] + 6 * D]
    dmod_cols = lax.dynamic_slice(dmod_all, (0, me * ada_cols), (N_DEV, ada_cols))
    gw_ada = ada_weight_grad(c_all, dmod_cols)

    big = [("ada", w_ada[0], gw_ada, m_w_ada[0], v_w_ada[0]), ("in", w_in[0], gw_in, m_w_in[0], v_w_in[0]),
           ("branch_a", w_branch_a[0], gw_ba, m_w_branch_a[0], v_w_branch_a[0]),
           ("branch_b", w_branch_b[0], gw_bb, m_w_branch_b[0], v_w_branch_b[0]),
           ("o", w_o[0], gw_o, m_w_o[0], v_w_o[0]),
           ("ffn_gate", w_ffn_gate[0], gw_fg, m_w_ffn_gate[0], v_w_ffn_gate[0]),
           ("ffn_up", w_ffn_up[0], gw_fu, m_w_ffn_up[0], v_w_ffn_up[0]),
           ("ffn_down", w_ffn_down[0], gw_fd, m_w_ffn_down[0], v_w_ffn_down[0])]
    big_out = {}
    for nm, w_, g_, m_, v_ in big:
        d_, m2_, v2_ = adamw(w_, g_, m_, v_, "adamw_" + nm)
        big_out[nm] = (g_[None], d_[None], m2_[None], v2_[None])

    small_w = [b_ada, hgrn_lb, hgrn_gnorm, ssm_conv_w, ssm_conv_b, ssm_dt_bias, ssm_a_log, ssm_d, ssm_norm,
               ln1_g, ln1_b, ln2_g, ln2_b]
    small_g = [g_bada, g_lb, g_gn, g_cw, g_cb, g_dtb, g_alog, g_dsk, g_wn, g_l1g, g_l1b, g_l2g, g_l2b]
    small_m = [m_b_ada, m_hgrn_lb, m_hgrn_gnorm, m_ssm_conv_w, m_ssm_conv_b, m_ssm_dt_bias, m_ssm_a_log, m_ssm_d,
               m_ssm_norm, m_ln1_g, m_ln1_b, m_ln2_g, m_ln2_b]
    small_v = [v_b_ada, v_hgrn_lb, v_hgrn_gnorm, v_ssm_conv_w, v_ssm_conv_b, v_ssm_dt_bias, v_ssm_a_log, v_ssm_d,
               v_ssm_norm, v_ln1_g, v_ln1_b, v_ln2_g, v_ln2_b]
    shapes = [a.shape for a in small_w]
    small_g = [g_.reshape(s) for g_, s in zip(small_g, shapes)]
    pw, poffs = _pack(small_w)
    pg, _ = _pack(small_g)
    pm, _ = _pack(small_m)
    pv, _ = _pack(small_v)
    pd, pm2, pv2 = adamw(pw, pg, pm, pv, "adamw_small")
    s_d, s_m, s_v = (_unpack(p, poffs, shapes) for p in (pd, pm2, pv2))
    (sn_bada, sn_lb, sn_gn, sn_cw, sn_cb, sn_dtb, sn_alog, sn_dsk, sn_wn, sn_l1g, sn_l1b, sn_l2g, sn_l2b) = range(13)

    def order(kind):
        sm = [small_g, s_d, s_m, s_v][kind]
        bg = lambda nm: big_out[nm][kind]
        return [bg("ada"), sm[sn_bada], bg("in"), sm[sn_lb], sm[sn_gn], sm[sn_cw], sm[sn_cb], sm[sn_dtb], sm[sn_alog],
                sm[sn_dsk], sm[sn_wn], bg("branch_a"), bg("branch_b"), bg("o"), sm[sn_l1g], sm[sn_l1b],
                bg("ffn_gate"), bg("ffn_up"), bg("ffn_down"), sm[sn_l2g], sm[sn_l2b]]

    return (loss, dx[None], *order(0), *order(1), *order(2), *order(3))
```

```python
import numpy as np
import jax
import jax.numpy as jnp
from jax import lax
from jax.experimental import pallas as pl
from jax.experimental.pallas import tpu as pltpu

F32 = jnp.float32
BF16 = jnp.bfloat16
HI = lax.Precision.HIGHEST

N_DEV = 8
D = 1024
N_HEADS_A = 8
HK = 128
CHUNK = 64
SSD_CHUNK = 128
N_GROUPS = 4
HEADS_PER_GROUP = 8
HEAD_P = 64
N_STATE = 128
GROUP_W = HEADS_PER_GROUP * HEAD_P
B_INNER = 2048
CONV_DIM = 3072
D_FF = 2816
D_FF_PAD = 3072
IN_DIM = 11296
N_PROJ = 12288
ALPHA = 2.0 ** 0.25
LN_EPS = 1e-5
RMS_EPS = 1e-6
Q_SCALE = 128 ** -0.5
EXP_CLIP = 80.0
ADAM_LR, ADAM_B1, ADAM_B2, ADAM_EPS, ADAM_WD, ADAM_STEP = 0.001, 0.9, 0.999, 1e-8, 0.01, 10
VMEM_LIMIT = 48 * 1024 * 1024
TOKEN_BLOCK = 512
ROW_TILE = 256
FFN_ROW_TILE = 128
MM_ROW_TILE = 1024
MM_TOKEN_TILE = 4096
MM_K_TILE = 3072
CHUNK_UNROLL = 8
MESH_ID = pl.DeviceIdType.MESH

NT_DIMS = (((1,), (1,)), ((), ()))
TN_DIMS = (((0,), (0,)), ((), ()))


def _cparams(sem=None):
    return pltpu.CompilerParams(dimension_semantics=sem, vmem_limit_bytes=VMEM_LIMIT)


def _sigmoid(x):
    return 1.0 / (1.0 + jnp.exp(-x))


def _dsilu(x, s):
    return s * (1.0 + x * (1.0 - s))


def _nt(a, b, precision=None):
    return lax.dot_general(a, b, NT_DIMS, precision=precision, preferred_element_type=F32)


def _tn(a, b, precision=None):
    return lax.dot_general(a, b, TN_DIMS, precision=precision, preferred_element_type=F32)


def _nn(a, b, precision=None):
    return jnp.dot(a, b, precision=precision, preferred_element_type=F32)


def _split(x, pieces):
    out = []
    for i in range(pieces):
        p = x.astype(BF16)
        out.append(p)
        if i + 1 < pieces:
            x = x - p.astype(F32)
    return out


def _sel(dot, x, sel01, pieces, x_first=True):
    acc = None
    for p in _split(x, pieces):
        term = dot(p, sel01) if x_first else dot(sel01, p)
        acc = term if acc is None else acc + term
    return acc


def _ln(x):
    mu = jnp.mean(x, axis=-1, keepdims=True)
    xc = x - mu
    rstd = lax.rsqrt(jnp.mean(xc * xc, axis=-1, keepdims=True) + LN_EPS)
    return xc * rstd, rstd


def _ln_bwd(dxh, xh, rstd):
    return rstd * (dxh - jnp.mean(dxh, axis=-1, keepdims=True) - xh * jnp.mean(dxh * xh, axis=-1, keepdims=True))


def _colsum(x):
    return jnp.sum(x, axis=0, keepdims=True)


def _tri(n, upper=False):
    r = lax.broadcasted_iota(jnp.int32, (n, n), 0)
    c = lax.broadcasted_iota(jnp.int32, (n, n), 1)
    return (c >= r) if upper else (r >= c)


def _my_pos():
    return lax.axis_index("x"), lax.axis_index("y"), lax.axis_index("c")


def _peer(pos, k):
    x, y, c = pos
    return (x ^ ((k >> 2) & 1), y ^ ((k >> 1) & 1), c ^ (k & 1))


def _flat(pos):
    return 4 * pos[0] + 2 * pos[1] + pos[2]


def allgather_vmem(v, name):
    n = v.shape[1]

    def body(v_ref, o_ref, send_sems, recv_sems, local_sem):
        me = _my_pos()
        mine = pltpu.make_async_copy(v_ref, o_ref.at[_flat(me)], local_sem)
        mine.start()
        sends = []
        for k in range(1, N_DEV):
            peer = _peer(me, k)
            cp = pltpu.make_async_remote_copy(v_ref, o_ref.at[_flat(me)], send_sems.at[k - 1], recv_sems.at[k - 1],
                                              device_id=peer, device_id_type=MESH_ID)
            cp.start()
            sends.append(cp)
        for k in range(1, N_DEV):
            peer = _peer(me, k)
            pltpu.make_async_remote_copy(v_ref, o_ref.at[_flat(peer)], send_sems.at[k - 1], recv_sems.at[k - 1],
                                         device_id=peer, device_id_type=MESH_ID).wait_recv()
        for cp in sends:
            cp.wait_send()
        mine.wait()

    return pl.pallas_call(
        body, name=name,
        out_shape=jax.ShapeDtypeStruct((N_DEV, 1, n), F32),
        in_specs=[pl.BlockSpec(memory_space=pltpu.VMEM)],
        out_specs=pl.BlockSpec(memory_space=pltpu.VMEM),
        scratch_shapes=[pltpu.SemaphoreType.DMA((N_DEV - 1,)), pltpu.SemaphoreType.DMA((N_DEV - 1,)),
                        pltpu.SemaphoreType.DMA],
        compiler_params=_cparams(),
    )(v)


def ada_modulation(c_all, w_ada_s, b_ada_r):
    ncol = w_ada_s.shape[1]

    def body(c_ref, w_ref, b_ref, o_ref, part_ref, send_sems, recv_sems):
        me = _my_pos()
        cval = c_ref[...]
        cond = cval * _sigmoid(cval)
        part = _nn(cond, w_ref[...], HI)
        for r in range(N_DEV):
            part_ref[r] = part[r:r + 1, :]
        sends = []
        for k in range(1, N_DEV):
            peer = _peer(me, k)
            cp = pltpu.make_async_remote_copy(part_ref.at[_flat(peer)], o_ref.at[_flat(me)], send_sems.at[k - 1],
                                              recv_sems.at[k - 1], device_id=peer, device_id_type=MESH_ID)
            cp.start()
            sends.append(cp)
        o_ref[_flat(me)] = part_ref[_flat(me)]
        for k in range(1, N_DEV):
            peer = _peer(me, k)
            pltpu.make_async_remote_copy(part_ref.at[_flat(peer)], o_ref.at[_flat(peer)], send_sems.at[k - 1],
                                         recv_sems.at[k - 1], device_id=peer, device_id_type=MESH_ID).wait_recv()
        for cp in sends:
            cp.wait_send()
        o_ref[...] = o_ref[...] + b_ref[...]

    return pl.pallas_call(
        body, name="ada_modulation",
        out_shape=jax.ShapeDtypeStruct((N_DEV, 1, ncol), F32),
        in_specs=[pl.BlockSpec(memory_space=pltpu.VMEM)] * 3,
        out_specs=pl.BlockSpec(memory_space=pltpu.VMEM),
        scratch_shapes=[pltpu.VMEM((N_DEV, 1, ncol), F32), pltpu.SemaphoreType.DMA((N_DEV - 1,)),
                        pltpu.SemaphoreType.DMA((N_DEV - 1,))],
        compiler_params=_cparams(),
    )(c_all, w_ada_s, b_ada_r)


def allgather_hbm(shard, name):
    def body(x_ref, out_ref, send_sems, recv_sems, local_sem):
        x, y, c = _my_pos()
        me, sibling = (x, y, c), (x, y, 1 - c)
        chips = [(1 - x, y), (x, 1 - y), (1 - x, 1 - y)]

        def slot(pos):
            return out_ref.at[_flat(pos)]

        def copy(k, block, to, src=None):
            return pltpu.make_async_remote_copy(slot(block) if src is None else src, slot(block), send_sems.at[k],
                                                recv_sems.at[k], device_id=to, device_id_type=MESH_ID)

        mine = pltpu.make_async_copy(x_ref, slot(me), local_sem)
        mine.start()
        first = [copy(0, me, sibling, src=x_ref)]
        first += [copy(1 + j, me, (*chip, c), src=x_ref) for j, chip in enumerate(chips)]
        for cp in first:
            cp.start()
        passed = [copy(4 + j, (*chip, c), sibling) for j, chip in enumerate(chips)]
        for j, chip in enumerate(chips):
            copy(1 + j, (*chip, c), me).wait_recv()
            passed[j].start()
        copy(0, sibling, me).wait_recv()
        for j, chip in enumerate(chips):
            copy(4 + j, (*chip, 1 - c), me).wait_recv()
        for cp in first + passed:
            cp.wait_send()
        mine.wait()

    return pl.pallas_call(
        body, name=name,
        out_shape=jax.ShapeDtypeStruct((N_DEV,) + shard.shape, shard.dtype),
        in_specs=[pl.BlockSpec(memory_space=pl.ANY)],
        out_specs=pl.BlockSpec(memory_space=pl.ANY),
        scratch_shapes=[pltpu.SemaphoreType.DMA((N_DEV - 1,)), pltpu.SemaphoreType.DMA((N_DEV - 1,)),
                        pltpu.SemaphoreType.DMA],
        compiler_params=_cparams(),
    )(shard)


N_CHIP = N_DEV // 2
SIBLING_SEMS = [pltpu.SemaphoreType.DMA, pltpu.SemaphoreType.DMA]
CHIP_SEMS = [pltpu.SemaphoreType.DMA((N_CHIP - 1,)), pltpu.SemaphoreType.DMA((N_CHIP - 1,)), pltpu.SemaphoreType.DMA]


def _sibling_exchange(s_ref, o_ref, send_sem, recv_sem):
    x, y, c = _my_pos()
    cp = pltpu.make_async_remote_copy(s_ref, o_ref, send_sem, recv_sem, device_id=(x, y, 1 - c), device_id_type=MESH_ID)
    return cp.start, cp.wait


def _chip_exchange(p_ref, o_ref, send_sems, recv_sems, local_sem):
    x, y, c = _my_pos()
    my_chip = 2 * x + y
    mine = pltpu.make_async_copy(p_ref.at[my_chip], o_ref.at[my_chip], local_sem)
    peers = [(x ^ (k >> 1), y ^ (k & 1)) for k in range(1, N_CHIP)]
    sends = [pltpu.make_async_remote_copy(p_ref.at[2 * px + py], o_ref.at[my_chip], send_sems.at[k], recv_sems.at[k],
                                          device_id=(px, py, c), device_id_type=MESH_ID)
             for k, (px, py) in enumerate(peers)]
    recvs = [pltpu.make_async_remote_copy(p_ref.at[2 * px + py], o_ref.at[2 * px + py], send_sems.at[k], recv_sems.at[k],
                                          device_id=(px, py, c), device_id_type=MESH_ID)
             for k, (px, py) in enumerate(peers)]

    def start():
        mine.start()
        for cp in sends:
            cp.start()

    def wait():
        for cp in recvs:
            cp.wait_recv()
        for cp in sends:
            cp.wait_send()
        mine.wait()

    return start, wait


def exchange_sibling(send, name):
    def body(s_ref, o_ref, send_sem, recv_sem):
        start, wait = _sibling_exchange(s_ref, o_ref, send_sem, recv_sem)
        start()
        wait()

    return pl.pallas_call(
        body, name=name,
        out_shape=jax.ShapeDtypeStruct(send.shape, send.dtype),
        in_specs=[pl.BlockSpec(memory_space=pl.ANY)],
        out_specs=pl.BlockSpec(memory_space=pl.ANY),
        scratch_shapes=SIBLING_SEMS,
        compiler_params=_cparams(),
    )(send)


def _k_tile(kdim):
    for cand in range(MM_K_TILE, 0, -1024):
        if kdim % cand == 0:
            return cand
    return kdim


def mm_nn(a, b, out_dtype, name):
    m, kdim = a.shape
    n = b.shape[1]
    tm, tn, tk = min(MM_ROW_TILE, m), 1024, _k_tile(kdim)
    nk = kdim // tk

    def body(a_ref, b_ref, o_ref, acc_ref):
        p = _nn(a_ref[...], b_ref[...])
        if nk == 1:
            o_ref[...] = p.astype(o_ref.dtype)
        else:
            k = pl.program_id(2)

            @pl.when(k == 0)
            def _():
                acc_ref[...] = p

            @pl.when(k > 0)
            def _():
                acc_ref[...] += p

            @pl.when(k == nk - 1)
            def _():
                o_ref[...] = acc_ref[...].astype(o_ref.dtype)

    return pl.pallas_call(
        body, name=name, grid=(n // tn, m // tm, nk),
        out_shape=jax.ShapeDtypeStruct((m, n), out_dtype),
        in_specs=[pl.BlockSpec((tm, tk), lambda j, i, k: (i, k)), pl.BlockSpec((tk, tn), lambda j, i, k: (k, j))],
        out_specs=pl.BlockSpec((tm, tn), lambda j, i, k: (i, j)),
        scratch_shapes=[pltpu.VMEM((tm, tn), F32)],
        compiler_params=_cparams(("parallel", "parallel", "arbitrary")),
    )(a, b)


def mm_nt(a, b, out_dtype, name):
    m, kdim = a.shape
    n = b.shape[0]
    tm, tn, tk = min(MM_ROW_TILE, m), 1024, _k_tile(kdim)
    nk = kdim // tk

    def body(a_ref, b_ref, o_ref, acc_ref):
        p = _nt(a_ref[...], b_ref[...])
        if nk == 1:
            o_ref[...] = p.astype(o_ref.dtype)
        else:
            k = pl.program_id(2)

            @pl.when(k == 0)
            def _():
                acc_ref[...] = p

            @pl.when(k > 0)
            def _():
                acc_ref[...] += p

            @pl.when(k == nk - 1)
            def _():
                o_ref[...] = acc_ref[...].astype(o_ref.dtype)

    return pl.pallas_call(
        body, name=name, grid=(n // tn, m // tm, nk),
        out_shape=jax.ShapeDtypeStruct((m, n), out_dtype),
        in_specs=[pl.BlockSpec((tm, tk), lambda j, i, k: (i, k)), pl.BlockSpec((tn, tk), lambda j, i, k: (j, k))],
        out_specs=pl.BlockSpec((tm, tn), lambda j, i, k: (i, j)),
        scratch_shapes=[pltpu.VMEM((tm, tn), F32)],
        compiler_params=_cparams(("parallel", "parallel", "arbitrary")),
    )(a, b)


def mm_nn_exchange(a, b, out_dtype, part, name):
    kblocks, m, kb = a.shape
    kdim = kblocks * kb
    n = b.shape[1]
    tm, tn, tk = min(MM_ROW_TILE, m), 1024, _k_tile(kdim)
    gn, gm, nk = n // tn, m // tm, kdim // tk
    per_step = tk // kb

    def body(a_ref, b_ref, part_ref, o_ref, parts_ref, acc_ref, send_sems, recv_sems, local_sem):
        j, i, k = pl.program_id(0), pl.program_id(1), pl.program_id(2)
        xchg_start, xchg_wait = _chip_exchange(part_ref, parts_ref, send_sems, recv_sems, local_sem)

        @pl.when((j == 0) & (i == 0) & (k == 0))
        def _():
            xchg_start()

        p = _nn(a_ref[0], b_ref[0:kb, :])
        for c in range(1, per_step):
            p = p + _nn(a_ref[c], b_ref[c * kb:(c + 1) * kb, :])

        @pl.when(k == 0)
        def _():
            acc_ref[...] = p

        @pl.when(k > 0)
        def _():
            acc_ref[...] += p

        @pl.when(k == nk - 1)
        def _():
            o_ref[...] = acc_ref[...].astype(o_ref.dtype)

        @pl.when((j == gn - 1) & (i == gm - 1) & (k == nk - 1))
        def _():
            xchg_wait()

    hbm = pl.BlockSpec(memory_space=pl.ANY)
    return pl.pallas_call(
        body, name=name, grid=(gn, gm, nk),
        out_shape=[jax.ShapeDtypeStruct((m, n), out_dtype), jax.ShapeDtypeStruct(part.shape, part.dtype)],
        in_specs=[pl.BlockSpec((per_step, tm, kb), lambda j, i, k: (k, i, 0)),
                  pl.BlockSpec((tk, tn), lambda j, i, k: (k, j)), hbm],
        out_specs=[pl.BlockSpec((tm, tn), lambda j, i, k: (i, j)), hbm],
        scratch_shapes=[pltpu.VMEM((tm, tn), F32)] + CHIP_SEMS,
        compiler_params=_cparams(("arbitrary", "arbitrary", "arbitrary")),
    )(a, b, part)


def mm_nt_gather(a, b, out_dtype, shard, name):
    m, kdim = a.shape
    n = b.shape[0]
    tm, tn = min(MM_ROW_TILE, m), 1024
    assert kdim == 1024
    gj = m // tm
    nsteps = (n // tn) * gj
    forward_step = max(nsteps - 3, 0)

    def body(a_ref, b_ref, x_ref, o_ref, g_ref, send_sems, recv_sems, local_sem):
        step = pl.program_id(0) * gj + pl.program_id(1)
        x, y, c = _my_pos()
        me, sibling = (x, y, c), (x, y, 1 - c)
        chips = [(1 - x, y), (x, 1 - y), (1 - x, 1 - y)]

        def slot(pos):
            return g_ref.at[_flat(pos)]

        def copy(k, block, to, src=None):
            return pltpu.make_async_remote_copy(slot(block) if src is None else src, slot(block), send_sems.at[k],
                                                recv_sems.at[k], device_id=to, device_id_type=MESH_ID)

        mine = pltpu.make_async_copy(x_ref, slot(me), local_sem)
        first = [copy(0, me, sibling, src=x_ref)]
        first += [copy(1 + j, me, (*chip, c), src=x_ref) for j, chip in enumerate(chips)]
        passed = [copy(4 + j, (*chip, c), sibling) for j, chip in enumerate(chips)]

        @pl.when(step == 0)
        def _():
            mine.start()
            for cp in first:
                cp.start()

        o_ref[...] = _nt(a_ref[...], b_ref[...]).astype(o_ref.dtype)

        @pl.when(step == forward_step)
        def _():
            for j, chip in enumerate(chips):
                copy(1 + j, (*chip, c), me).wait_recv()
                passed[j].start()

        @pl.when(step == nsteps - 1)
        def _():
            copy(0, sibling, me).wait_recv()
            for j, chip in enumerate(chips):
                copy(4 + j, (*chip, 1 - c), me).wait_recv()
            for cp in first + passed:
                cp.wait_send()
            mine.wait()

    return pl.pallas_call(
        body, name=name, grid=(n // tn, gj),
        out_shape=[jax.ShapeDtypeStruct((m, n), out_dtype), jax.ShapeDtypeStruct((N_DEV,) + shard.shape, shard.dtype)],
        in_specs=[pl.BlockSpec((tm, kdim), lambda j, i: (i, 0)), pl.BlockSpec((tn, kdim), lambda j, i: (j, 0)),
                  pl.BlockSpec(memory_space=pl.ANY)],
        out_specs=[pl.BlockSpec((tm, tn), lambda j, i: (i, j)), pl.BlockSpec(memory_space=pl.ANY)],
        scratch_shapes=[pltpu.SemaphoreType.DMA((N_DEV - 1,)), pltpu.SemaphoreType.DMA((N_DEV - 1,)),
                        pltpu.SemaphoreType.DMA],
        compiler_params=_cparams(("arbitrary", "arbitrary")),
    )(a, b, shard)


def mm_tn(a, b, name):
    tt, tka, tn = min(MM_TOKEN_TILE, b.shape[0]), 1024, 1024
    if a.ndim == 3:
        t, ka = a.shape[1], a.shape[0] * a.shape[2]
        a_spec = pl.BlockSpec((None, tt, tka), lambda i, j, s: (i, s, 0))
    else:
        t, ka = a.shape
        a_spec = pl.BlockSpec((tt, tka), lambda i, j, s: (s, i))
    n = b.shape[1]
    nt = t // tt

    def body(a_ref, b_ref, o_ref, *acc):
        p = _tn(a_ref[...], b_ref[...])
        if nt == 1:
            o_ref[...] = p.astype(o_ref.dtype)
        else:
            acc_ref, s = acc[0], pl.program_id(2)

            @pl.when(s == 0)
            def _():
                acc_ref[...] = p

            @pl.when(s > 0)
            def _():
                acc_ref[...] += p

            @pl.when(s == nt - 1)
            def _():
                o_ref[...] = acc_ref[...].astype(o_ref.dtype)

    return pl.pallas_call(
        body, name=name, grid=(ka // tka, n // tn, nt),
        out_shape=jax.ShapeDtypeStruct((ka, n), BF16),
        in_specs=[a_spec, pl.BlockSpec((tt, tn), lambda i, j, s: (s, j))],
        out_specs=pl.BlockSpec((tka, tn), lambda i, j, s: (i, j)),
        scratch_shapes=[] if nt == 1 else [pltpu.VMEM((tka, tn), F32)],
        compiler_params=_cparams(("parallel", "parallel", "arbitrary")),
    )(a, b)


def _tile(t, cap):
    return min(cap, t)


def ln_modulate(x, mod6, shift_row, scale_row, name):
    t = x.shape[0]
    tm = _tile(t, ROW_TILE)

    def body(x_ref, mod_ref, o_ref):
        xh, _ = _ln(x_ref[...])
        sc = mod_ref[scale_row:scale_row + 1, :]
        sh = mod_ref[shift_row:shift_row + 1, :]
        o_ref[...] = (xh * (1.0 + sc) + sh).astype(BF16)

    return pl.pallas_call(
        body, name=name, grid=(t // tm,),
        out_shape=jax.ShapeDtypeStruct((t, D), BF16),
        in_specs=[pl.BlockSpec((tm, D), lambda i: (i, 0)), pl.BlockSpec((6, D), lambda i: (0, 0))],
        out_specs=pl.BlockSpec((tm, D), lambda i: (i, 0)),
        compiler_params=_cparams(("parallel",)),
    )(x, mod6)


def resid_ln(x, h, mod6, gate_row, ln_g, ln_b, name):
    t = x.shape[0]
    tm = _tile(t, ROW_TILE)

    def body(x_ref, h_ref, mod_ref, g_ref, b_ref, o_ref):
        r = ALPHA * x_ref[...] + mod_ref[gate_row:gate_row + 1, :] * h_ref[...]
        rh, _ = _ln(r)
        o_ref[...] = rh * g_ref[...] + b_ref[...]

    row = pl.BlockSpec((tm, D), lambda i: (i, 0))
    vec = pl.BlockSpec((1, D), lambda i: (0, 0))
    return pl.pallas_call(
        body, name=name, grid=(t // tm,),
        out_shape=jax.ShapeDtypeStruct((t, D), F32),
        in_specs=[row, row, pl.BlockSpec((6, D), lambda i: (0, 0)), vec, vec],
        out_specs=row,
        compiler_params=_cparams(("parallel",)),
    )(x, h, mod6, ln_g, ln_b)


def resid_ln_bwd(x, h, mod6, gate_row, ln_g, ln_b, cot, with_loss, name):
    t = x.shape[0]
    tm = _tile(t, ROW_TILE)

    def body(x_ref, h_ref, mod_ref, g_ref, b_ref, c_ref, dh_ref, dx_ref, acc_ref):
        @pl.when(pl.program_id(0) == 0)
        def _():
            acc_ref[...] = jnp.zeros_like(acc_ref)

        gate = mod_ref[gate_row:gate_row + 1, :]
        hv = h_ref[...]
        r = ALPHA * x_ref[...] + gate * hv
        rh, rstd = _ln(r)
        lng = g_ref[...]
        if with_loss:
            diff = rh * lng + b_ref[...] - c_ref[...]
            dxo = diff * (1.0 / D)
            lsum = jnp.sum(_colsum(diff * diff), axis=-1, keepdims=True) * (0.5 / D)
            acc_ref[3:4, :] += jnp.broadcast_to(lsum, (1, D))
        else:
            dxo = c_ref[...]
        acc_ref[1:2, :] += _colsum(dxo * rh)
        acc_ref[2:3, :] += _colsum(dxo)
        dr = _ln_bwd(dxo * lng, rh, rstd)
        acc_ref[0:1, :] += _colsum(dr * hv)
        dh_ref[...] = (gate * dr).astype(BF16)
        dx_ref[...] = ALPHA * dr

    row = pl.BlockSpec((tm, D), lambda i: (i, 0))
    vec = pl.BlockSpec((1, D), lambda i: (0, 0))
    return pl.pallas_call(
        body, name=name, grid=(t // tm,),
        out_shape=[jax.ShapeDtypeStruct((t, D), BF16), jax.ShapeDtypeStruct((t, D), F32),
                   jax.ShapeDtypeStruct((8, D), F32)],
        in_specs=[row, row, pl.BlockSpec((6, D), lambda i: (0, 0)), vec, vec, row],
        out_specs=[row, row, pl.BlockSpec((8, D), lambda i: (0, 0))],
        compiler_params=_cparams(("arbitrary",)),
    )(x, h, mod6, ln_g, ln_b, cot)


def ln_modulate_bwd(x, du, mod6, scale_row, dx_part, name):
    t = x.shape[0]
    tm = _tile(t, ROW_TILE)

    def body(x_ref, du_ref, mod_ref, dp_ref, dx_ref, acc_ref):
        @pl.when(pl.program_id(0) == 0)
        def _():
            acc_ref[...] = jnp.zeros_like(acc_ref)

        xh, rstd = _ln(x_ref[...])
        du_v = du_ref[...]
        sc = mod_ref[scale_row:scale_row + 1, :]
        acc_ref[0:1, :] += _colsum(du_v * xh)
        acc_ref[1:2, :] += _colsum(du_v)
        dx_ref[...] = dp_ref[...] + _ln_bwd(du_v * (1.0 + sc), xh, rstd)

    row = pl.BlockSpec((tm, D), lambda i: (i, 0))
    return pl.pallas_call(
        body, name=name, grid=(t // tm,),
        out_shape=[jax.ShapeDtypeStruct((t, D), F32), jax.ShapeDtypeStruct((8, D), F32)],
        in_specs=[row, row, pl.BlockSpec((6, D), lambda i: (0, 0)), row],
        out_specs=[row, pl.BlockSpec((8, D), lambda i: (0, 0))],
        compiler_params=_cparams(("arbitrary",)),
    )(x, du, mod6, dx_part)


def merge_gates(ya, yb, proj):
    t = ya.shape[0]
    tm = _tile(t, ROW_TILE)

    def body(ya_ref, yb_ref, ga_ref, gb_ref, o_ref):
        o_ref[...] = (_sigmoid(ga_ref[...]) * ya_ref[...].astype(F32) +
                      _sigmoid(gb_ref[...]) * yb_ref[...].astype(F32)).astype(BF16)

    row = pl.BlockSpec((tm, D), lambda i: (i, 0))
    return pl.pallas_call(
        body, name="merge_gates", grid=(t // tm,),
        out_shape=jax.ShapeDtypeStruct((t, D), BF16),
        in_specs=[row, row, pl.BlockSpec((tm, D), lambda i: (i, GATE_BLOCK0)),
                  pl.BlockSpec((tm, D), lambda i: (i, GATE_BLOCK0 + 1))],
        out_specs=row,
        compiler_params=_cparams(("parallel",)),
    )(ya, yb, proj, proj)


def merge_gates_bwd(dm, ya, yb, proj):
    t = ya.shape[0]
    tm = _tile(t, ROW_TILE)

    def body(dm_ref, ya_ref, yb_ref, ga_ref, gb_ref, dya_ref, dyb_ref, dp_ref):
        dmv = dm_ref[...].astype(F32)
        sa = _sigmoid(ga_ref[...])
        sb = _sigmoid(gb_ref[...])
        dya_ref[...] = (dmv * sa).astype(BF16)
        dyb_ref[...] = (dmv * sb).astype(BF16)
        dp_ref[0] = (dmv * ya_ref[...].astype(F32) * sa * (1.0 - sa)).astype(BF16)
        dp_ref[1] = (dmv * yb_ref[...].astype(F32) * sb * (1.0 - sb)).astype(BF16)

    row = pl.BlockSpec((tm, D), lambda i: (i, 0))
    return pl.pallas_call(
        body, name="merge_gates_bwd", grid=(t // tm,),
        out_shape=[jax.ShapeDtypeStruct((t, D), BF16)] * 2 + [jax.ShapeDtypeStruct((N_PROJ // D, t, D), BF16)],
        in_specs=[row, row, row, pl.BlockSpec((tm, D), lambda i: (i, GATE_BLOCK0)),
                  pl.BlockSpec((tm, D), lambda i: (i, GATE_BLOCK0 + 1))],
        out_specs=[row, row, pl.BlockSpec((2, tm, D), lambda i: (GATE_BLOCK0 // 2, i, 0))],
        compiler_params=_cparams(("parallel",)),
    )(dm, ya, yb, proj, proj)


def swiglu_act(gu):
    t = gu.shape[0]
    tm = _tile(t, FFN_ROW_TILE)

    def body(gu_ref, o_ref):
        for j in range(D_FF_PAD // D):
            g = gu_ref[:, j * D:(j + 1) * D].astype(F32)
            u = gu_ref[:, D_FF_PAD + j * D:D_FF_PAD + (j + 1) * D].astype(F32)
            o_ref[:, j * D:(j + 1) * D] = (g * _sigmoid(g) * u).astype(BF16)

    return pl.pallas_call(
        body, name="swiglu_act", grid=(t // tm,),
        out_shape=jax.ShapeDtypeStruct((t, D_FF_PAD), BF16),
        in_specs=[pl.BlockSpec((tm, 2 * D_FF_PAD), lambda i: (i, 0))],
        out_specs=pl.BlockSpec((tm, D_FF_PAD), lambda i: (i, 0)),
        compiler_params=_cparams(("parallel",)),
    )(gu)


def swiglu_act_bwd(gu, dact):
    t = gu.shape[0]
    tm = _tile(t, FFN_ROW_TILE)

    def body(gu_ref, da_ref, o_ref):
        for j in range(D_FF_PAD // D):
            g = gu_ref[:, j * D:(j + 1) * D].astype(F32)
            u = gu_ref[:, D_FF_PAD + j * D:D_FF_PAD + (j + 1) * D].astype(F32)
            da = da_ref[:, j * D:(j + 1) * D].astype(F32)
            s = _sigmoid(g)
            o_ref[:, j * D:(j + 1) * D] = (da * u * _dsilu(g, s)).astype(BF16)
            o_ref[:, D_FF_PAD + j * D:D_FF_PAD + (j + 1) * D] = (da * g * s).astype(BF16)

    return pl.pallas_call(
        body, name="swiglu_act_bwd", grid=(t // tm,),
        out_shape=jax.ShapeDtypeStruct((t, 2 * D_FF_PAD), BF16),
        in_specs=[pl.BlockSpec((tm, 2 * D_FF_PAD), lambda i: (i, 0)), pl.BlockSpec((tm, D_FF_PAD), lambda i: (i, 0))],
        out_specs=pl.BlockSpec((tm, 2 * D_FF_PAD), lambda i: (i, 0)),
        compiler_params=_cparams(("parallel",)),
    )(gu, dact)


def _hgrn_chunk_terms(q, fl, lbv, tril_f):
    sig = _sigmoid(fl)
    f = lbv + (1.0 - lbv) * sig
    lam = jnp.log(f)
    k = 1.0 - f
    sq = _sigmoid(q)
    qt = q * sq * Q_SCALE
    bc = _sel(_nn, lam, tril_f, 3, x_first=False)
    bmid = bc[CHUNK // 2 - 1:CHUNK // 2, :]
    bl = bc[CHUNK - 1:CHUNK, :]
    eq = jnp.exp(jnp.minimum(bc - bmid, EXP_CLIP))
    ek = jnp.exp(jnp.minimum(bmid - bc, EXP_CLIP))
    eb = jnp.exp(bc)
    ekl = jnp.exp(bl - bc)
    ebl = jnp.exp(bl)
    return sig, f, k, sq, qt, eq, ek, eb, ekl, ebl


def hgrn_fwd(proj, lb, gnorm):
    t = proj.shape[0]
    tb = _tile(t, TOKEN_BLOCK)
    ncb = tb // CHUNK

    def body(q_ref, f_ref, i_ref, g_ref, lb_ref, gn_ref, oa_ref, oraw_ref, st_ref, state):
        @pl.when(pl.program_id(1) == 0)
        def _():
            state[...] = jnp.zeros_like(state)

        lbv = lb_ref[...]
        gn = gn_ref[...]
        mask = _tri(CHUNK)
        tril_f = mask.astype(BF16)

        def chunk(c, carry):
            sl = pl.ds(pl.multiple_of(c * CHUNK, CHUNK), CHUNK)
            q, fl, v, g = q_ref[sl, :], f_ref[sl, :], i_ref[sl, :], g_ref[sl, :]
            sig, f, k, sq, qt, eq, ek, eb, ekl, ebl = _hgrn_chunk_terms(q, fl, lbv, tril_f)
            a = jnp.where(mask, _nt((qt * eq).astype(BF16), (k * ek).astype(BF16)), 0.0)
            st = state[...]
            st_ref[0, c] = st
            vb = v.astype(BF16)
            o = _nn(a.astype(BF16), vb) + _nt((qt * eb).astype(BF16), st.astype(BF16))
            state[...] = st * ebl + _tn(vb, (k * ekl).astype(BF16))
            oraw_ref[sl, :] = o
            rn = o * lax.rsqrt(jnp.mean(o * o, axis=-1, keepdims=True) + RMS_EPS)
            oa_ref[sl, :] = (rn * gn * g * _sigmoid(g)).astype(BF16)
            return carry

        lax.fori_loop(0, ncb, chunk, 0, unroll=min(CHUNK_UNROLL, ncb))

    def col(block):
        return pl.BlockSpec((tb, HK), lambda h, j: (j, block * N_HEADS_A + h))

    return pl.pallas_call(
        body, name="hgrn_fwd", grid=(N_HEADS_A, t // tb),
        out_shape=[jax.ShapeDtypeStruct((t, D), BF16), jax.ShapeDtypeStruct((t, D), F32),
                   jax.ShapeDtypeStruct((N_HEADS_A, t // CHUNK, HK, HK), F32)],
        in_specs=[col(0), col(1), col(2), col(3), pl.BlockSpec((1, HK), lambda h, j: (0, h)),
                  pl.BlockSpec((1, HK), lambda h, j: (0, 0))],
        out_specs=[pl.BlockSpec((tb, HK), lambda h, j: (j, h)), pl.BlockSpec((tb, HK), lambda h, j: (j, h)),
                   pl.BlockSpec((1, ncb, HK, HK), lambda h, j: (h, j, 0, 0))],
        scratch_shapes=[pltpu.VMEM((HK, HK), F32)],
        compiler_params=_cparams(("parallel", "arbitrary")),
    )(proj, proj, proj, proj, lb, gnorm)


def hgrn_bwd(proj, lb, gnorm, o_raw, doa, states, give, dproj):
    t = proj.shape[0]
    tb = _tile(t, TOKEN_BLOCK)
    ncb = tb // CHUNK
    nb = t // tb

    def body(q_ref, f_ref, i_ref, g_ref, lb_ref, gn_ref, oraw_ref, doa_ref, st_ref, give_ref, dp_in_ref,
             dp_ref, dlb_ref, dgn_ref, got_ref, dstate, send_sem, recv_sem):
        h, j = pl.program_id(0), pl.program_id(1)
        swap_start, swap_wait = _sibling_exchange(give_ref, got_ref, send_sem, recv_sem)

        @pl.when((h == 0) & (j == 0))
        def _():
            swap_start()

        @pl.when(j == 0)
        def _():
            dstate[...] = jnp.zeros_like(dstate)
            dlb_ref[...] = jnp.zeros_like(dlb_ref)

        @pl.when((j == 0) & (h == 0))
        def _():
            dgn_ref[...] = jnp.zeros_like(dgn_ref)

        lbv = lb_ref[...]
        gn = gn_ref[...]
        mask = _tri(CHUNK)
        mask_t = _tri(CHUNK, upper=True)
        tril_f = mask.astype(BF16)
        triu_f = mask_t.astype(BF16)

        def chunk(i, c0):
            c = ncb - 1 - i
            sl = pl.ds(pl.multiple_of(c * CHUNK, CHUNK), CHUNK)
            q, fl, v, g = q_ref[sl, :], f_ref[sl, :], i_ref[sl, :], g_ref[sl, :]
            sig, f, k, sq, qt, eq, ek, eb, ekl, ebl = _hgrn_chunk_terms(q, fl, lbv, tril_f)
            qe = (qt * eq).astype(BF16)
            ke = (k * ek).astype(BF16)
            st32 = st_ref[0, c]
            st = st32.astype(BF16)
            dst = dstate[...]
            dstb = dst.astype(BF16)
            o = oraw_ref[sl, :]
            rstd = lax.rsqrt(jnp.mean(o * o, axis=-1, keepdims=True) + RMS_EPS)
            rn = o * rstd
            sgm = _sigmoid(g)
            sg = g * sgm
            doa_v = doa_ref[sl, :]
            drn = doa_v * gn * sg
            dgn_ref[...] += _colsum(doa_v * rn * sg)
            dp_ref[3, sl, :] = (doa_v * rn * gn * _dsilu(g, sgm)).astype(BF16)
            do = rstd * (drn - rn * jnp.mean(drn * rn, axis=-1, keepdims=True))
            dob = do.astype(BF16)
            vb = v.astype(BF16)
            da = jnp.where(mask, _nt(dob, vb), 0.0).astype(BF16)
            da_t = jnp.where(mask_t, _nt(vb, dob), 0.0).astype(BF16)
            a_t = jnp.where(mask_t, _nt(ke, qe), 0.0).astype(BF16)
            kl = (k * ekl).astype(BF16)
            qb = (qt * eb).astype(BF16)
            dq_in = _nn(da, ke)
            dk_in = _nn(da_t, qe)
            dq_out = eb * _nn(dob, st)
            dk_out = ekl * _nn(vb, dstb)
            dqt = eq * dq_in + dq_out
            dk = ek * dk_in + dk_out
            dv = _nn(a_t, dob) + _nt(kl, dstb)
            dstate[...] = dst * ebl + _tn(dob, qb)
            dbig = qe.astype(F32) * dq_in - ke.astype(F32) * dk_in + qt * dq_out - k * dk_out
            beyond = _colsum(k * dk_out) + ebl * _colsum(dst * st32)
            dlam = _sel(_nn, dbig, triu_f, 3, x_first=False) + beyond
            df = dlam / f - dk
            dp_ref[1, sl, :] = (df * (1.0 - lbv) * sig * (1.0 - sig)).astype(BF16)
            dlb_ref[...] += _colsum(df * (1.0 - sig))
            dp_ref[0, sl, :] = (dqt * Q_SCALE * _dsilu(q, sq)).astype(BF16)
            dp_ref[2, sl, :] = dv.astype(BF16)
            return c0

        lax.fori_loop(0, ncb, chunk, 0, unroll=min(CHUNK_UNROLL, ncb))

        @pl.when((h == N_HEADS_A - 1) & (j == nb - 1))
        def _():
            swap_wait()

    def col(block):
        return pl.BlockSpec((tb, HK), lambda h, j: (nb - 1 - j, block * N_HEADS_A + h))

    hcol = pl.BlockSpec((tb, HK), lambda h, j: (nb - 1 - j, h))
    hbm = pl.BlockSpec(memory_space=pl.ANY)
    return pl.pallas_call(
        body, name="hgrn_bwd", grid=(N_HEADS_A, nb),
        out_shape=[jax.ShapeDtypeStruct(dproj.shape, dproj.dtype), jax.ShapeDtypeStruct((1, D), F32),
                   jax.ShapeDtypeStruct((1, HK), F32), jax.ShapeDtypeStruct(give.shape, give.dtype)],
        in_specs=[col(0), col(1), col(2), col(3), pl.BlockSpec((1, HK), lambda h, j: (0, h)),
                  pl.BlockSpec((1, HK), lambda h, j: (0, 0)), hcol, hcol,
                  pl.BlockSpec((1, ncb, HK, HK), lambda h, j: (h, nb - 1 - j, 0, 0)), hbm, hbm],
        out_specs=[pl.BlockSpec((4, tb, HK), lambda h, j: (0, nb - 1 - j, h)),
                   pl.BlockSpec((1, HK), lambda h, j: (0, h)), pl.BlockSpec((1, HK), lambda h, j: (0, 0)), hbm],
        input_output_aliases={10: 0},
        scratch_shapes=[pltpu.VMEM((HK, HK), F32)] + SIBLING_SEMS,
        compiler_params=_cparams(("arbitrary", "arbitrary")),
    )(proj, proj, proj, proj, lb, gnorm, o_raw, doa, states, give, dproj)


CONV_BLOCK0 = 6
CONV_TAPS = 4
HALO = 8


def conv_fwd(proj, conv_w, conv_b):
    t = proj.shape[0]
    tm = _tile(t, ROW_TILE)
    r = tm // HALO

    def body(x_ref, halo_ref, w_ref, b_ref, o_ref):
        i = pl.program_id(1)
        halo = jnp.where(i > 0, halo_ref[...], 0.0)
        ext = jnp.concatenate([halo, x_ref[...]], axis=0)
        pre = b_ref[...] + w_ref[CONV_TAPS - 1:CONV_TAPS, :] * ext[HALO:, :]
        for tap in range(CONV_TAPS - 1):
            pre = pre + w_ref[tap:tap + 1, :] * pltpu.roll(ext, CONV_TAPS - 1 - tap, axis=0)[HALO:, :]
        o_ref[...] = pre * _sigmoid(pre)

    return pl.pallas_call(
        body, name="conv_fwd", grid=(CONV_DIM // D, t // tm),
        out_shape=jax.ShapeDtypeStruct((t, CONV_DIM), F32),
        in_specs=[pl.BlockSpec((tm, D), lambda cb, i: (i, CONV_BLOCK0 + cb)),
                  pl.BlockSpec((HALO, D), lambda cb, i: (jnp.maximum(i * r - 1, 0), CONV_BLOCK0 + cb)),
                  pl.BlockSpec((CONV_TAPS, D), lambda cb, i: (0, cb)), pl.BlockSpec((1, D), lambda cb, i: (0, cb))],
        out_specs=pl.BlockSpec((tm, D), lambda cb, i: (i, cb)),
        compiler_params=_cparams(("parallel", "parallel")),
    )(proj, proj, conv_w, conv_b)


def conv_bwd(proj, dxc, conv_w, conv_b, dproj):
    t = proj.shape[0]
    tm = _tile(t, ROW_TILE)
    r = tm // HALO
    n = t // tm
    last_halo = t // HALO - 1

    def body(x_ref, prev_ref, next_ref, d_ref, dnext_ref, w_ref, b_ref, dp_in_ref, dx_ref, dw_ref, db_ref):
        i = pl.program_id(1)

        @pl.when(i == 0)
        def _():
            dw_ref[...] = jnp.zeros_like(dw_ref)
            db_ref[...] = jnp.zeros_like(db_ref)

        prev = jnp.where(i > 0, prev_ref[...], 0.0)
        ext = jnp.concatenate([prev, x_ref[...], next_ref[...]], axis=0)
        shifted = [pltpu.roll(ext, CONV_TAPS - 1 - tap, axis=0)[HALO:, :] for tap in range(CONV_TAPS - 1)]
        shifted.append(ext[HALO:, :])
        pre = b_ref[...]
        for tap in range(CONV_TAPS):
            pre = pre + w_ref[tap:tap + 1, :] * shifted[tap]
        s = _sigmoid(pre)
        d_ext = jnp.concatenate([d_ref[...], jnp.where(i < n - 1, dnext_ref[...], 0.0)], axis=0)
        dpre = d_ext * _dsilu(pre, s)
        dx = w_ref[CONV_TAPS - 1:CONV_TAPS, :] * dpre[:tm, :]
        for tap in range(CONV_TAPS - 1):
            back = CONV_TAPS - 1 - tap
            dx = dx + w_ref[tap:tap + 1, :] * pltpu.roll(dpre, tm + HALO - back, axis=0)[:tm, :]
        dx_ref[...] = dx.astype(BF16)
        dp = dpre[:tm, :]
        db_ref[...] += _colsum(dp)
        for tap in range(CONV_TAPS):
            dw_ref[tap:tap + 1, :] += _colsum(dp * shifted[tap][:tm, :])

    return pl.pallas_call(
        body, name="conv_bwd", grid=(CONV_DIM // D, n),
        out_shape=[jax.ShapeDtypeStruct(dproj.shape, dproj.dtype), jax.ShapeDtypeStruct((8, CONV_DIM), F32),
                   jax.ShapeDtypeStruct((1, CONV_DIM), F32)],
        in_specs=[pl.BlockSpec((tm, D), lambda cb, i: (i, CONV_BLOCK0 + cb)),
                  pl.BlockSpec((HALO, D), lambda cb, i: (jnp.maximum(i * r - 1, 0), CONV_BLOCK0 + cb)),
                  pl.BlockSpec((HALO, D), lambda cb, i: (jnp.minimum((i + 1) * r, last_halo), CONV_BLOCK0 + cb)),
                  pl.BlockSpec((tm, D), lambda cb, i: (i, cb)),
                  pl.BlockSpec((HALO, D), lambda cb, i: (jnp.minimum((i + 1) * r, last_halo), cb)),
                  pl.BlockSpec((CONV_TAPS, D), lambda cb, i: (0, cb)), pl.BlockSpec((1, D), lambda cb, i: (0, cb)),
                  pl.BlockSpec(memory_space=pl.ANY)],
        out_specs=[pl.BlockSpec((None, tm, D), lambda cb, i: (CONV_BLOCK0 + cb, i, 0)),
                   pl.BlockSpec((8, D), lambda cb, i: (0, cb)), pl.BlockSpec((1, D), lambda cb, i: (0, cb))],
        input_output_aliases={7: 0},
        compiler_params=_cparams(("parallel", "arbitrary")),
    )(proj, proj, proj, dxc, dxc, conv_w, conv_b, dproj)


def dt_fill(ddt, dproj):
    t = ddt.shape[0]
    tm = _tile(t, ROW_TILE)
    w = ddt.shape[1]

    def body(d_ref, dp_in_ref, o_ref):
        o_ref[:, :w] = d_ref[...]
        o_ref[:, w:] = jnp.zeros((tm, D - w), o_ref.dtype)

    return pl.pallas_call(
        body, name="dt_fill", grid=(t // tm,),
        out_shape=jax.ShapeDtypeStruct(dproj.shape, dproj.dtype),
        in_specs=[pl.BlockSpec((tm, w), lambda i: (i, 0)), pl.BlockSpec(memory_space=pl.ANY)],
        out_specs=pl.BlockSpec((None, tm, D), lambda i: (DT_COL_BLOCK, i, 0)),
        input_output_aliases={1: 0},
        compiler_params=_cparams(("parallel",)),
    )(ddt, dproj)


Z_BLOCK0 = 8
DT_COL_BLOCK = 9
DT_BLOCK0 = 8 * DT_COL_BLOCK
GATE_BLOCK0 = 10
B_BLOCK0 = 16
C_BLOCK0 = 20


def _head_expand():
    e = np.zeros((N_STATE, GROUP_W), np.float32)
    for hh in range(HEADS_PER_GROUP):
        e[hh, hh * HEAD_P:(hh + 1) * HEAD_P] = 1.0
    return jnp.asarray(e, BF16)


def _ssd_chunk_terms(dt, bias, alog, expand, tril_f, eye):
    dtb = dt + bias
    delta = jnp.maximum(dtb, 0.0) + jnp.log(1.0 + jnp.exp(-jnp.abs(dtb)))
    ea = jnp.exp(alog)
    a = -ea * delta
    acum = _sel(_nn, a, tril_f, 3, x_first=False)
    delta_e = _sel(_nn, delta, expand, 2)
    acum_e = _sel(_nn, acum, expand, 3)
    acum_t = _sel(_nt, acum, eye, 3, x_first=False)
    return dtb, delta, ea, a, acum, delta_e, acum_e, acum_t


def ssd_fwd(proj, xc, alog4, bias4, dskip4, wnorm, expand):
    t = proj.shape[0]
    tb = _tile(t, TOKEN_BLOCK)
    ncb = tb // SSD_CHUNK

    def body(xs_ref, b_ref, c_ref, dt_ref, z_ref, alog_ref, bias_ref, dsk_ref, wn_ref, e_ref, ob_ref, st_ref, state):
        @pl.when(pl.program_id(1) == 0)
        def _():
            state[...] = jnp.zeros_like(state)

        expand = e_ref[...]
        mask = _tri(SSD_CHUNK)
        tril_f = mask.astype(BF16)
        eye = (lax.broadcasted_iota(jnp.int32, (N_STATE, N_STATE), 0) ==
               lax.broadcasted_iota(jnp.int32, (N_STATE, N_STATE), 1)).astype(BF16)
        alog, bias = alog_ref[0], bias_ref[0]
        d_e = _sel(_nn, jnp.broadcast_to(dsk_ref[0], (8, N_STATE)), expand, 3)[0:1, :]
        wn = wn_ref[...]

        def chunk(c, carry):
            sl = pl.ds(pl.multiple_of(c * SSD_CHUNK, SSD_CHUNK), SSD_CHUNK)
            xs, bm, cm, dt, z = xs_ref[sl, :], b_ref[sl, :], c_ref[sl, :], dt_ref[sl, :], z_ref[sl, :]
            dtb, delta, ea, a, acum, delta_e, acum_e, acum_t = _ssd_chunk_terms(dt, bias, alog, expand, tril_f, eye)
            alast_e = acum_e[SSD_CHUNK - 1:SSD_CHUNK, :]
            xd = xs * delta_e
            xdb = xd.astype(BF16)
            cb_, bb_ = cm.astype(BF16), bm.astype(BF16)
            cbm = _nt(cb_, bb_)
            ys = []
            for hh in range(HEADS_PER_GROUP):
                lh = jnp.where(mask, jnp.exp(jnp.minimum(acum[:, hh:hh + 1] - acum_t[hh:hh + 1, :], 0.0)), 0.0)
                ys.append(_nn((cbm * lh).astype(BF16), xdb[:, hh * HEAD_P:(hh + 1) * HEAD_P]))
            st = state[...]
            st_ref[0, c] = st
            y = jnp.concatenate(ys, axis=1) + _nn(cb_, st.astype(BF16)) * jnp.exp(acum_e) + xs * d_e
            state[...] = st * jnp.exp(alast_e) + _tn(bb_, (xd * jnp.exp(alast_e - acum_e)).astype(BF16))
            yg = y * z * _sigmoid(z)
            ob_ref[sl, :] = (yg * lax.rsqrt(jnp.mean(yg * yg, axis=-1, keepdims=True) + RMS_EPS) * wn).astype(BF16)
            return carry

        lax.fori_loop(0, ncb, chunk, 0, unroll=min(CHUNK_UNROLL, ncb))

    small = pl.BlockSpec((1, 1, N_STATE), lambda g, j: (g, 0, 0))
    return pl.pallas_call(
        body, name="ssd_fwd", grid=(N_GROUPS, t // tb),
        out_shape=[jax.ShapeDtypeStruct((t, B_INNER), BF16),
                   jax.ShapeDtypeStruct((N_GROUPS, t // SSD_CHUNK, N_STATE, GROUP_W), F32)],
        in_specs=[pl.BlockSpec((tb, GROUP_W), lambda g, j: (j, g)),
                  pl.BlockSpec((tb, N_STATE), lambda g, j: (j, B_BLOCK0 + g)),
                  pl.BlockSpec((tb, N_STATE), lambda g, j: (j, C_BLOCK0 + g)),
                  pl.BlockSpec((tb, N_STATE), lambda g, j: (j, DT_BLOCK0 + g)),
                  pl.BlockSpec((tb, GROUP_W), lambda g, j: (j, Z_BLOCK0 + g)),
                  small, small, small, pl.BlockSpec((1, GROUP_W), lambda g, j: (0, g)),
                  pl.BlockSpec((N_STATE, GROUP_W), lambda g, j: (0, 0))],
        out_specs=[pl.BlockSpec((tb, GROUP_W), lambda g, j: (j, g)),
                   pl.BlockSpec((1, ncb, N_STATE, GROUP_W), lambda g, j: (g, j, 0, 0))],
        scratch_shapes=[pltpu.VMEM((N_STATE, GROUP_W), F32)],
        compiler_params=_cparams(("parallel", "arbitrary")),
    )(xc, xc, xc, proj, proj, alog4, bias4, dskip4, wnorm, expand)


def ssd_bwd(proj, xc, alog4, bias4, dskip4, wnorm, expand, dob, states, part, dproj):
    t = proj.shape[0]
    tb = _tile(t, TOKEN_BLOCK)
    ncb = tb // SSD_CHUNK
    nb = t // tb

    def body(xs_ref, b_ref, c_ref, dt_ref, z_ref, alog_ref, bias_ref, dsk_ref, wn_ref, e_ref, dob_ref, st_ref, part_ref,
             dp_in_ref, dxs_ref, db_ref, dc_ref, dz_ref, ddt_ref, dwn_ref, dalog_ref, dbias_ref, ddsk_ref, parts_ref, dstate,
             send_sems, recv_sems, local_sem):
        xchg_start, xchg_wait = _chip_exchange(part_ref, parts_ref, send_sems, recv_sems, local_sem)

        @pl.when((pl.program_id(0) == 0) & (pl.program_id(1) == 0))
        def _():
            xchg_start()

        @pl.when(pl.program_id(1) == 0)
        def _():
            dstate[...] = jnp.zeros_like(dstate)
            dwn_ref[...] = jnp.zeros_like(dwn_ref)
            dalog_ref[...] = jnp.zeros_like(dalog_ref)
            dbias_ref[...] = jnp.zeros_like(dbias_ref)
            ddsk_ref[...] = jnp.zeros_like(ddsk_ref)

        expand = e_ref[...]
        mask = _tri(SSD_CHUNK)
        mask_t = _tri(SSD_CHUNK, upper=True)
        tril_f = mask.astype(BF16)
        triu_f = mask_t.astype(BF16)
        eye = (lax.broadcasted_iota(jnp.int32, (N_STATE, N_STATE), 0) ==
               lax.broadcasted_iota(jnp.int32, (N_STATE, N_STATE), 1)).astype(BF16)
        alog, bias = alog_ref[0], bias_ref[0]
        d_e = _sel(_nn, jnp.broadcast_to(dsk_ref[0], (8, N_STATE)), expand, 3)[0:1, :]
        wn = wn_ref[...]

        def chunk(i, c0):
            c = ncb - 1 - i
            sl = pl.ds(pl.multiple_of(c * SSD_CHUNK, SSD_CHUNK), SSD_CHUNK)
            xs, bm, cm, dt, z = xs_ref[sl, :], b_ref[sl, :], c_ref[sl, :], dt_ref[sl, :], z_ref[sl, :]
            dtb, delta, ea, a, acum, delta_e, acum_e, acum_t = _ssd_chunk_terms(dt, bias, alog, expand, tril_f, eye)
            alast_e = acum_e[SSD_CHUNK - 1:SSD_CHUNK, :]
            eacum = jnp.exp(acum_e)
            wl = jnp.exp(alast_e - acum_e)
            xd = xs * delta_e
            xdb = xd.astype(BF16)
            cb_, bb_ = cm.astype(BF16), bm.astype(BF16)
            cbm = _nt(cb_, bb_)
            cbm_t = _nt(bb_, cb_)
            st32 = st_ref[0, c]
            stb = st32.astype(BF16)
            dst = dstate[...]
            dstb = dst.astype(BF16)
            lhs, lhts, ys = [], [], []
            for hh in range(HEADS_PER_GROUP):
                col, row = acum[:, hh:hh + 1], acum_t[hh:hh + 1, :]
                lh = jnp.where(mask, jnp.exp(jnp.minimum(col - row, 0.0)), 0.0)
                lht = jnp.where(mask_t, jnp.exp(jnp.minimum(row - col, 0.0)), 0.0)
                lhs.append(lh)
                lhts.append(lht)
                ys.append(_nn((cbm * lh).astype(BF16), xdb[:, hh * HEAD_P:(hh + 1) * HEAD_P]))
            y_in = jnp.concatenate(ys, axis=1)
            y_out = _nn(cb_, stb) * eacum
            y = y_in + y_out + xs * d_e
            sgz = _sigmoid(z)
            sz = z * sgz
            yg = y * sz
            rstd = lax.rsqrt(jnp.mean(yg * yg, axis=-1, keepdims=True) + RMS_EPS)
            nrm = yg * rstd
            dob_v = dob_ref[sl, :]
            dn = dob_v * wn
            dwn_ref[...] += _colsum(dob_v * nrm)
            dyg = rstd * (dn - nrm * jnp.mean(dn * nrm, axis=-1, keepdims=True))
            dy = dyg * sz
            dz_ref[sl, :] = (dyg * y * _dsilu(z, sgz)).astype(BF16)
            dyb = dy.astype(BF16)
            dxds = []
            dcb = jnp.zeros((SSD_CHUNK, SSD_CHUNK), F32)
            dcb_t = jnp.zeros((SSD_CHUNK, SSD_CHUNK), F32)
            for hh in range(HEADS_PER_GROUP):
                hs = slice(hh * HEAD_P, (hh + 1) * HEAD_P)
                dy_h, x_h = dyb[:, hs], xdb[:, hs]
                dxds.append(_nn((cbm_t * lhts[hh]).astype(BF16), dy_h))
                dcb = dcb + _nt(dy_h, x_h) * lhs[hh]
                dcb_t = dcb_t + _nt(x_h, dy_h) * lhts[hh]
            dye = (dy * eacum).astype(BF16)
            xw = (xd * wl).astype(BF16)
            dxd_in = jnp.concatenate(dxds, axis=1)
            dxd_out = wl * _nn(bb_, dstb)
            dxd = dxd_in + dxd_out
            dc_ref[sl, :] = _nn(dcb.astype(BF16), bb_) + _nt(dye, stb)
            db_ref[sl, :] = _nn(dcb_t.astype(BF16), cb_) + _nt(xw, dstb)
            dstate[...] = dst * jnp.exp(alast_e) + _tn(cb_, dye)
            col_out = xd * dxd_out
            dac = _sel(_nt, dyb.astype(F32) * y_in - xdb.astype(F32) * dxd_in + dy * y_out - col_out, expand, 3)
            beyond = _colsum(col_out) + jnp.exp(alast_e) * _colsum(dst * st32)
            da = (_sel(_nn, dac, triu_f, 3, x_first=False) +
                  _sel(_nt, jnp.broadcast_to(beyond, (8, GROUP_W)), expand, 3)[0:1, :])
            ddelta = _sel(_nt, dxd * xs, expand, 2) - da * ea
            dalog_ref[0] += _colsum(da * a)
            ddtb = ddelta * _sigmoid(dtb)
            dbias_ref[0] += _colsum(ddtb)
            ddt_ref[sl, :] = ddtb.astype(BF16)
            ddsk_ref[0] += _sel(_nt, jnp.broadcast_to(_colsum(dy * xs), (8, GROUP_W)), expand, 3)[0:1, :]
            dxs_ref[sl, :] = dxd * delta_e + dy * d_e
            return c0

        lax.fori_loop(0, ncb, chunk, 0, unroll=min(CHUNK_UNROLL, ncb))

        @pl.when((pl.program_id(0) == N_GROUPS - 1) & (pl.program_id(1) == nb - 1))
        def _():
            xchg_wait()

    small = pl.BlockSpec((1, 1, N_STATE), lambda g, j: (g, 0, 0))
    wide = pl.BlockSpec((tb, GROUP_W), lambda g, j: (nb - 1 - j, g))
    narrow = pl.BlockSpec((tb, N_STATE), lambda g, j: (nb - 1 - j, g))
    hbm = pl.BlockSpec(memory_space=pl.ANY)
    return pl.pallas_call(
        body, name="ssd_bwd", grid=(N_GROUPS, nb),
        out_shape=[jax.ShapeDtypeStruct((t, B_INNER), F32), jax.ShapeDtypeStruct((t, GROUP_W), F32),
                   jax.ShapeDtypeStruct((t, GROUP_W), F32), jax.ShapeDtypeStruct(dproj.shape, dproj.dtype),
                   jax.ShapeDtypeStruct((t, GROUP_W), BF16), jax.ShapeDtypeStruct((1, B_INNER), F32),
                   jax.ShapeDtypeStruct((N_GROUPS, 1, N_STATE), F32), jax.ShapeDtypeStruct((N_GROUPS, 1, N_STATE), F32),
                   jax.ShapeDtypeStruct((N_GROUPS, 1, N_STATE), F32), jax.ShapeDtypeStruct(part.shape, part.dtype)],
        in_specs=[wide,
                  pl.BlockSpec((tb, N_STATE), lambda g, j: (nb - 1 - j, B_BLOCK0 + g)),
                  pl.BlockSpec((tb, N_STATE), lambda g, j: (nb - 1 - j, C_BLOCK0 + g)),
                  pl.BlockSpec((tb, N_STATE), lambda g, j: (nb - 1 - j, DT_BLOCK0 + g)),
                  pl.BlockSpec((tb, GROUP_W), lambda g, j: (nb - 1 - j, Z_BLOCK0 + g)),
                  small, small, small, pl.BlockSpec((1, GROUP_W), lambda g, j: (0, g)),
                  pl.BlockSpec((N_STATE, GROUP_W), lambda g, j: (0, 0)), wide,
                  pl.BlockSpec((1, ncb, N_STATE, GROUP_W), lambda g, j: (g, nb - 1 - j, 0, 0)), hbm, hbm],
        out_specs=[wide, narrow, narrow,
                   pl.BlockSpec((None, tb, GROUP_W), lambda g, j: (Z_BLOCK0 // 2 + g // 2, nb - 1 - j, g % 2)),
                   narrow, pl.BlockSpec((1, GROUP_W), lambda g, j: (0, g)), small, small, small, hbm],
        input_output_aliases={13: 3},
        scratch_shapes=[pltpu.VMEM((N_STATE, GROUP_W), F32)] + CHIP_SEMS,
        compiler_params=_cparams(("arbitrary", "arbitrary")),
    )(xc, xc, xc, proj, proj, alog4, bias4, dskip4, wnorm, expand, dob, states, part, dproj)


def lower_bound_fwd(hgrn_lb):
    def body(a_ref, o_ref):
        a0, a1 = a_ref[0:1, :], a_ref[1:2, :]
        m = jnp.maximum(a0, a1)
        e0, e1 = jnp.exp(a0 - m), jnp.exp(a1 - m)
        o_ref[...] = e0 / (e0 + e1)

    return pl.pallas_call(body, name="lower_bound_fwd", out_shape=jax.ShapeDtypeStruct((1, D), F32))(hgrn_lb)


def ada_weight_grad(c_all, dmod_cols):
    def body(c_ref, d_ref, o_ref):
        cval = c_ref[...]
        o_ref[...] = _tn(cval * _sigmoid(cval), d_ref[...], HI)

    return pl.pallas_call(body, name="ada_weight_grad",
                          out_shape=jax.ShapeDtypeStruct((D, dmod_cols.shape[1]), F32))(c_all, dmod_cols)


def reduce_small(gathered, hgrn_lb, dlb_off):
    n = gathered.shape[2]

    def body(g_ref, a_ref, o_ref, glb_ref):
        s = g_ref[0]
        for d in range(1, N_DEV):
            s = s + g_ref[d]
        o_ref[...] = s
        a0, a1 = a_ref[0:1, :], a_ref[1:2, :]
        m = jnp.maximum(a0, a1)
        e0, e1 = jnp.exp(a0 - m), jnp.exp(a1 - m)
        p0 = e0 / (e0 + e1)
        tq = s[:, dlb_off:dlb_off + D] * p0 * (1.0 - p0)
        glb_ref[0:1, :] = tq
        glb_ref[1:2, :] = -tq

    return pl.pallas_call(body, name="reduce_small",
                          out_shape=[jax.ShapeDtypeStruct((1, n), F32), jax.ShapeDtypeStruct((2, D), F32)])(gathered, hgrn_lb)


def _adam_math(w, g, m, v):
    m2 = ADAM_B1 * m + (1.0 - ADAM_B1) * g
    v2 = ADAM_B2 * v + (1.0 - ADAM_B2) * (g * g)
    m_hat = m2 / (1.0 - ADAM_B1 ** ADAM_STEP)
    v_hat = v2 / (1.0 - ADAM_B2 ** ADAM_STEP)
    delta = -ADAM_LR * (m_hat / (jnp.sqrt(v_hat) + ADAM_EPS) + ADAM_WD * w)
    return delta, m2, v2


def _row_tile(rows, mult=8, cap=128):
    for cand in range(cap - cap % mult, 0, -mult):
        if rows % cand == 0:
            return cand
    return rows


def sum_parts(parts, name):
    n, rows, cols = parts.shape
    tr = _row_tile(rows, 16, 256)

    def body(p_ref, o_ref):
        s = p_ref[0].astype(F32)
        for d in range(1, n):
            s = s + p_ref[d].astype(F32)
        o_ref[...] = s

    return pl.pallas_call(
        body, name=name, grid=(rows // tr,),
        out_shape=jax.ShapeDtypeStruct((rows, cols), F32),
        in_specs=[pl.BlockSpec((n, tr, cols), lambda i: (0, i, 0))],
        out_specs=pl.BlockSpec((tr, cols), lambda i: (i, 0)),
        compiler_params=_cparams(("parallel",)),
    )(parts)


def sum_pair(a, b, name):
    rows, cols = a.shape
    tr = _row_tile(rows, 16, 256)

    def body(a_ref, b_ref, o_ref):
        o_ref[...] = (a_ref[...].astype(F32) + b_ref[...].astype(F32)).astype(o_ref.dtype)

    blk = pl.BlockSpec((tr, cols), lambda i: (i, 0))
    return pl.pallas_call(
        body, name=name, grid=(rows // tr,),
        out_shape=jax.ShapeDtypeStruct((rows, cols), a.dtype),
        in_specs=[blk, blk], out_specs=blk,
        compiler_params=_cparams(("parallel",)),
    )(a, b)


def adamw(w, g, m, v, name):
    rows, cols = w.shape
    tr = _row_tile(rows)

    def body(w_ref, g_ref, m_ref, v_ref, d_ref, m2_ref, v2_ref):
        delta, m2, v2 = _adam_math(w_ref[...], g_ref[...], m_ref[...], v_ref[...])
        d_ref[...] = delta
        m2_ref[...] = m2
        v2_ref[...] = v2

    blk = pl.BlockSpec((tr, cols), lambda i: (i, 0))
    return pl.pallas_call(
        body, name=name, grid=(rows // tr,),
        out_shape=[jax.ShapeDtypeStruct((rows, cols), F32)] * 3,
        in_specs=[blk] * 4, out_specs=[blk] * 3,
        compiler_params=_cparams(("parallel",)),
    )(w, g, m, v)


def _pad128(n):
    return -(-n // 128) * 128


def _pack(arrays):
    offs, parts, off = [], [], 0
    for a in arrays:
        flat = a.reshape(1, -1)
        n = flat.shape[1]
        offs.append(off)
        parts.append(jnp.pad(flat, ((0, 0), (0, _pad128(n) - n))))
        off += _pad128(n)
    return jnp.concatenate(parts, axis=1), offs


def _unpack(vec, offs, shapes):
    out = []
    for off, shp in zip(offs, shapes):
        n = int(np.prod(shp))
        out.append(vec[0, off:off + n].reshape(shp))
    return out


IN_ROWS = IN_DIM // N_DEV
DT_ROW0 = 9216
DT_DEV, DT_LO = divmod(DT_ROW0, IN_ROWS)


GATE_SHIFT = D - 32


def _in_row_pieces(tile):
    pieces = []
    if tile == DT_COL_BLOCK:
        for g in range(N_GROUPS):
            o = DT_ROW0 + HEADS_PER_GROUP * g
            pieces.append((N_STATE * g, o // IN_ROWS, o % IN_ROWS, HEADS_PER_GROUP))
        return pieces
    r, end = tile * D, (tile + 1) * D
    while r < end:
        o = r if r < DT_ROW0 else r - GATE_SHIFT
        dev, loc = divmod(o, IN_ROWS)
        n = min(end - r, IN_ROWS - loc)
        pieces.append((r - tile * D, dev, loc, n))
        r += n
    return pieces


def assemble_w_in(g_all):
    ntile = N_PROJ // D

    def body(g_ref, o_ref):
        j = pl.program_id(0)
        for tile in range(ntile):
            @pl.when(j == tile)
            def _(tile=tile):
                if tile == DT_COL_BLOCK:
                    o_ref[...] = jnp.zeros_like(o_ref)
                for dst, dev, loc, n in _in_row_pieces(tile):
                    o_ref[pl.ds(dst, n), :] = g_ref[dev, pl.ds(loc, n), :]

    return pl.pallas_call(
        body, name="assemble_w_in", grid=(ntile,),
        out_shape=jax.ShapeDtypeStruct((N_PROJ, D), g_all.dtype),
        in_specs=[pl.BlockSpec(memory_space=pltpu.VMEM)],
        out_specs=pl.BlockSpec((D, D), lambda j: (j, 0)),
        compiler_params=_cparams(("arbitrary",)),
    )(g_all)


def _grad_in_blocks(g_t, core, slot):
    dt0 = DT_COL_BLOCK * D
    dt = g_t[dt0:dt0 + N_GROUPS * N_STATE].reshape(N_GROUPS, N_STATE, D)[:, :HEADS_PER_GROUP].reshape(32, D)
    with_dt = jnp.concatenate([g_t[DT_DEV * IN_ROWS:DT_ROW0], dt,
                               g_t[DT_ROW0 + 32 + GATE_SHIFT:(DT_DEV + 1) * IN_ROWS + GATE_SHIFT]], axis=0)
    blocks = []
    for q in range(N_CHIP):
        if 2 * q + 1 < DT_DEV:
            blk = lax.dynamic_slice_in_dim(g_t, IN_ROWS * (2 * q + core), IN_ROWS, axis=0)
        else:
            assert 2 * q == DT_DEV
            after = g_t[(DT_DEV + 1) * IN_ROWS + GATE_SHIFT:(DT_DEV + 2) * IN_ROWS + GATE_SHIFT]
            blk = jnp.where(core == 0, with_dt, after)
        blocks.append(jnp.pad(blk, ((0, slot - IN_ROWS), (0, 0))))
    return jnp.stack(blocks)


def kernel(x, c, w_ada, b_ada, w_in, hgrn_lb, hgrn_gnorm, ssm_conv_w, ssm_conv_b, ssm_dt_bias, ssm_a_log, ssm_d, ssm_norm, w_branch_a, w_branch_b, w_o, ln1_g, ln1_b, w_ffn_gate, w_ffn_up, w_ffn_down, ln2_g, ln2_b, loss_target, m_w_ada, m_b_ada, m_w_in, m_hgrn_lb, m_hgrn_gnorm, m_ssm_conv_w, m_ssm_conv_b, m_ssm_dt_bias, m_ssm_a_log, m_ssm_d, m_ssm_norm, m_w_branch_a, m_w_branch_b, m_w_o, m_ln1_g, m_ln1_b, m_w_ffn_gate, m_w_ffn_up, m_w_ffn_down, m_ln2_g, m_ln2_b, v_w_ada, v_b_ada, v_w_in, v_hgrn_lb, v_hgrn_gnorm, v_ssm_conv_w, v_ssm_conv_b, v_ssm_dt_bias, v_ssm_a_log, v_ssm_d, v_ssm_norm, v_w_branch_a, v_w_branch_b, v_w_o, v_ln1_g, v_ln1_b, v_w_ffn_gate, v_w_ffn_up, v_w_ffn_down, v_ln2_g, v_ln2_b):
    me = 4 * lax.axis_index("x") + 2 * lax.axis_index("y") + lax.axis_index("c")
    xt = x[0]
    tgt = loss_target[0]
    t = xt.shape[0]
    ada_cols = w_ada.shape[2]
    conv_cols = ssm_conv_w.shape[2]

    small_in, _ = _pack([c, ssm_conv_w[0]])
    small_all = allgather_vmem(small_in, "allgather_small_inputs")
    c_all = small_all[:, 0, :D]
    conv_w = small_all[:, 0, D:D + CONV_TAPS * conv_cols].reshape(N_DEV, CONV_TAPS, conv_cols)
    conv_w = conv_w.transpose(1, 0, 2).reshape(CONV_TAPS, CONV_DIM)
    mod = ada_modulation(c_all, w_ada[0], b_ada.reshape(N_DEV, 1, ada_cols))
    mod6 = mod.reshape(6, D)

    shards = [w_in[0].T, w_branch_a[0], w_branch_b[0], w_o[0], w_ffn_gate[0].T, w_ffn_up[0].T, w_ffn_down[0]]
    shard_rows = [s.shape[0] for s in shards]
    slot_rows = [-(-r // 32) * 32 for r in shard_rows]
    row_offs = [sum(slot_rows[:i]) for i in range(len(shards))]
    padded = [jnp.pad(s.astype(BF16), ((0, p - r), (0, 0))) for s, r, p in zip(shards, shard_rows, slot_rows)]
    w_in_t = assemble_w_in(allgather_hbm(padded[0], "allgather_w_in"))

    lb = lower_bound_fwd(hgrn_lb)
    u1 = ln_modulate(xt, mod6, 0, 1, "ln_modulate_1")
    proj, g_rest = mm_nt_gather(u1, w_in_t, F32, jnp.concatenate(padded[1:], axis=0), "mm_in_proj")
    g_ba, g_bb, g_o, g_fg, g_fu, g_fd = (g_rest[:, o - slot_rows[0]:o - slot_rows[0] + r]
                                         for o, r in zip(row_offs[1:], shard_rows[1:]))
    w_ba = g_ba.reshape(D, D)
    w_bb = g_bb.reshape(B_INNER, D)
    w_oo = g_o.reshape(D, D)
    ffpad = ((0, D_FF_PAD - D_FF), (0, 0))
    w_gu_t = jnp.concatenate([jnp.pad(g_fg.reshape(D_FF, D), ffpad), jnp.pad(g_fu.reshape(D_FF, D), ffpad)], axis=0)
    w_dn = jnp.pad(g_fd.reshape(D_FF, D), ffpad)
    o_a, o_raw, st_a = hgrn_fwd(proj, lb, hgrn_gnorm)
    xc = conv_fwd(proj, conv_w, ssm_conv_b)
    pad3 = ((0, 0), (0, 0), (0, N_STATE - HEADS_PER_GROUP))
    alog4 = jnp.pad(ssm_a_log.reshape(N_GROUPS, 1, HEADS_PER_GROUP), pad3)
    bias4 = jnp.pad(ssm_dt_bias.reshape(N_GROUPS, 1, HEADS_PER_GROUP), pad3)
    dskip4 = jnp.pad(ssm_d.reshape(N_GROUPS, 1, HEADS_PER_GROUP), pad3)
    expand = _head_expand()
    o_b, st_b = ssd_fwd(proj, xc, alog4, bias4, dskip4, ssm_norm, expand)
    ya = mm_nn(o_a, w_ba, BF16, "mm_branch_a")
    yb = mm_nn(o_b, w_bb, BF16, "mm_branch_b")
    merged = merge_gates(ya, yb, proj)
    h1 = mm_nn(merged, w_oo, F32, "mm_out_proj")
    x1 = resid_ln(xt, h1, mod6, 2, ln1_g, ln1_b, "resid_ln_1")
    u2 = ln_modulate(x1, mod6, 3, 4, "ln_modulate_2")
    gu = mm_nt(u2, w_gu_t, BF16, "mm_ffn_in")
    act = swiglu_act(gu)
    h2 = mm_nn(act, w_dn, F32, "mm_ffn_out")

    dh2, dx1_part, acc4 = resid_ln_bwd(x1, h2, mod6, 5, ln2_g, ln2_b, tgt, True, "resid_ln_2_bwd")
    g_dn = mm_tn(act, dh2, "mm_grad_ffn_down")
    dact = mm_nt(dh2, w_dn, BF16, "mm_dact")
    dgu = swiglu_act_bwd(gu, dact)
    g_gu_t = mm_tn(dgu, u2, "mm_grad_ffn_in")
    du2 = mm_nn(dgu, w_gu_t, F32, "mm_du2")
    dx1, acc3 = ln_modulate_bwd(x1, du2, mod6, 4, dx1_part, "ln_modulate_2_bwd")
    dh1, dx_part, acc2 = resid_ln_bwd(xt, h1, mod6, 2, ln1_g, ln1_b, dx1, False, "resid_ln_1_bwd")
    g_o = mm_tn(merged, dh1, "mm_grad_out_proj")
    dmerged = mm_nt(dh1, w_oo, BF16, "mm_dmerged")
    dya, dyb, dproj = merge_gates_bwd(dmerged, ya, yb, proj)
    g_ba_full = mm_tn(o_a, dya, "mm_grad_branch_a")
    g_bb_full = mm_tn(o_b, dyb, "mm_grad_branch_b")
    doa = mm_nt(dya, w_ba, F32, "mm_doa")
    dob = mm_nt(dyb, w_bb, F32, "mm_dob")
    my_core = lax.axis_index("c")

    def by_core(blocks, rows, slots):
        contrib = jnp.concatenate([jnp.pad(b.reshape(N_DEV, -1, D), ((0, 0), (0, p - r), (0, 0)))
                                   for b, r, p in zip(blocks, rows, slots)], axis=1)
        split = contrib.reshape(N_CHIP, 2, contrib.shape[1], D).transpose(1, 0, 2, 3)
        return (lax.dynamic_index_in_dim(split, my_core, 0, keepdims=False),
                lax.dynamic_index_in_dim(split, 1 - my_core, 0, keepdims=False))

    keep_e, give_e = by_core([g_ba_full, g_bb_full, g_o, g_gu_t[:D_FF], g_gu_t[D_FF_PAD:D_FF_PAD + D_FF], g_dn[:D_FF]],
                             shard_rows[1:], slot_rows[1:])
    dproj, dlb, dgn, got_e = hgrn_bwd(proj, lb, hgrn_gnorm, o_raw, doa, st_a, give_e, dproj)
    chip_e = sum_pair(keep_e.reshape(-1, D), got_e.reshape(-1, D), "sum_grads_rest_chip").reshape(keep_e.shape)
    dxs, dbm, dcm, dproj, ddt, dwn, dalog, dbias, ddsk, parts_e = ssd_bwd(proj, xc, alog4, bias4, dskip4, ssm_norm,
                                                                          expand, dob, st_b, chip_e, dproj)
    dxc = jnp.concatenate([dxs, dbm, dcm], axis=1)
    dproj, dcw, dcb = conv_bwd(proj, dxc, conv_w, ssm_conv_b, dproj)
    dproj = dt_fill(ddt, dproj)
    g_in_t = mm_tn(dproj, u1, "mm_grad_in_proj")
    keep_l = _grad_in_blocks(g_in_t, my_core, slot_rows[0])
    give_l = _grad_in_blocks(g_in_t, 1 - my_core, slot_rows[0])
    got_l = exchange_sibling(give_l, "exchange_grad_in_sibling")
    chip_l = sum_pair(keep_l.reshape(-1, D), got_l.reshape(-1, D), "sum_grad_in_chip").reshape(keep_l.shape)
    du1, parts_l = mm_nn_exchange(dproj, w_in_t, F32, chip_l, "mm_du1")
    dx, acc1 = ln_modulate_bwd(xt, du1, mod6, 1, dx_part, "ln_modulate_1_bwd")
    gw_in = sum_parts(parts_l, "sum_grad_in")[:shard_rows[0]].T
    g_rows = sum_parts(parts_e, "sum_grads_rest")
    gw_ba, gw_bb, gw_o, gw_fg, gw_fu, gw_fd = (g_rows[o - slot_rows[0]:o - slot_rows[0] + r]
                                               for o, r in zip(row_offs[1:], shard_rows[1:]))
    gw_fg, gw_fu = gw_fg.T, gw_fu.T

    dmod = jnp.concatenate([acc1[1:2], acc1[0:1], acc2[0:1], acc3[1:2], acc3[0:1], acc4[0:1]], axis=1)
    small_fields = [dmod, acc4[3:4, :128], dlb, dgn, dcw[:CONV_TAPS], dcb, dbias, dalog, ddsk, dwn,
                    acc2[1:2], acc2[2:3], acc4[1:2], acc4[2:3]]
    small_out, offs = _pack(small_fields)
    small_sum_in = allgather_vmem(small_out, "allgather_small_grads")
    gsum, g_lb = reduce_small(small_sum_in, hgrn_lb, offs[2])
    (g_bada, loss_row, _, g_gn, g_cw_full, g_cb, g_bias4, g_alog4, g_dsk4, g_wn, g_l1g, g_l1b, g_l2g, g_l2b) = _unpack(
        gsum, offs, [(1, 6 * D), (1, 128), (1, D), (1, HK), (CONV_TAPS, CONV_DIM), (1, CONV_DIM),
                     (N_GROUPS, N_STATE), (N_GROUPS, N_STATE), (N_GROUPS, N_STATE), (1, B_INNER),
                     (1, D), (1, D), (1, D), (1, D)])
    loss = loss_row[0, 0]
    g_cw = lax.dynamic_slice(g_cw_full, (0, me * conv_cols), (CONV_TAPS, conv_cols))[None]
    g_dtb = g_bias4[:, :HEADS_PER_GROUP].reshape(1, 32)
    g_alog = g_alog4[:, :HEADS_PER_GROUP].reshape(1, 32)
    g_dsk = g_dsk4[:, :HEADS_PER_GROUP].reshape(1, 32)

    dmod_all = small_sum_in[:, 0, offs[0]:offs[0] + 6 * D]
    dmod_cols = lax.dynamic_slice(dmod_all, (0, me * ada_cols), (N_DEV, ada_cols))
    gw_ada = ada_weight_grad(c_all, dmod_cols)

    big = [("ada", w_ada[0], gw_ada, m_w_ada[0], v_w_ada[0]), ("in", w_in[0], gw_in, m_w_in[0], v_w_in[0]),
           ("branch_a", w_branch_a[0], gw_ba, m_w_branch_a[0], v_w_branch_a[0]),
           ("branch_b", w_branch_b[0], gw_bb, m_w_branch_b[0], v_w_branch_b[0]),
           ("o", w_o[0], gw_o, m_w_o[0], v_w_o[0]),
           ("ffn_gate", w_ffn_gate[0], gw_fg, m_w_ffn_gate[0], v_w_ffn_gate[0]),
           ("ffn_up", w_ffn_up[0], gw_fu, m_w_ffn_up[0], v_w_ffn_up[0]),
           ("ffn_down", w_ffn_down[0], gw_fd, m_w_ffn_down[0], v_w_ffn_down[0])]
    big_out = {}
    for nm, w_, g_, m_, v_ in big:
        d_, m2_, v2_ = adamw(w_, g_, m_, v_, "adamw_" + nm)
        big_out[nm] = (g_[None], d_[None], m2_[None], v2_[None])

    small_w = [b_ada, hgrn_lb, hgrn_gnorm, ssm_conv_w, ssm_conv_b, ssm_dt_bias, ssm_a_log, ssm_d, ssm_norm,
               ln1_g, ln1_b, ln2_g, ln2_b]
    small_g = [g_bada, g_lb, g_gn, g_cw, g_cb, g_dtb, g_alog, g_dsk, g_wn, g_l1g, g_l1b, g_l2g, g_l2b]
    small_m = [m_b_ada, m_hgrn_lb, m_hgrn_gnorm, m_ssm_conv_w, m_ssm_conv_b, m_ssm_dt_bias, m_ssm_a_log, m_ssm_d,
               m_ssm_norm, m_ln1_g, m_ln1_b, m_ln2_g, m_ln2_b]
    small_v = [v_b_ada, v_hgrn_lb, v_hgrn_gnorm, v_ssm_conv_w, v_ssm_conv_b, v_ssm_dt_bias, v_ssm_a_log, v_ssm_d,
               v_ssm_norm, v_ln1_g, v_ln1_b, v_ln2_g, v_ln2_b]
    shapes = [a.shape for a in small_w]
    small_g = [g_.reshape(s) for g_, s in zip(small_g, shapes)]
    pw, poffs = _pack(small_w)
    pg, _ = _pack(small_g)
    pm, _ = _pack(small_m)
    pv, _ = _pack(small_v)
    pd, pm2, pv2 = adamw(pw, pg, pm, pv, "adamw_small")
    s_d, s_m, s_v = (_unpack(p, poffs, shapes) for p in (pd, pm2, pv2))
    (sn_bada, sn_lb, sn_gn, sn_cw, sn_cb, sn_dtb, sn_alog, sn_dsk, sn_wn, sn_l1g, sn_l1b, sn_l2g, sn_l2b) = range(13)

    def order(kind):
        sm = [small_g, s_d, s_m, s_v][kind]
        bg = lambda nm: big_out[nm][kind]
        return [bg("ada"), sm[sn_bada], bg("in"), sm[sn_lb], sm[sn_gn], sm[sn_cw], sm[sn_cb], sm[sn_dtb], sm[sn_alog],
                sm[sn_dsk], sm[sn_wn], bg("branch_a"), bg("branch_b"), bg("o"), sm[sn_l1g], sm[sn_l1b],
                bg("ffn_gate"), bg("ffn_up"), bg("ffn_down"), sm[sn_l2g], sm[sn_l2b]]

    return (loss, dx[None], *order(0), *order(1), *order(2), *order(3))
```

```python
import numpy as np
import jax
import jax.numpy as jnp
from jax import lax
from jax.experimental import pallas as pl
from jax.experimental.pallas import tpu as pltpu

F32 = jnp.float32
BF16 = jnp.bfloat16
HI = lax.Precision.HIGHEST

N_DEV = 8
D = 1024
N_HEADS_A = 8
HK = 128
CHUNK = 64
SSD_CHUNK = 128
N_GROUPS = 4
HEADS_PER_GROUP = 8
HEAD_P = 64
N_STATE = 128
GROUP_W = HEADS_PER_GROUP * HEAD_P
B_INNER = 2048
CONV_DIM = 3072
D_FF = 2816
D_FF_PAD = 3072
IN_DIM = 11296
N_PROJ = 12288
ALPHA = 2.0 ** 0.25
LN_EPS = 1e-5
RMS_EPS = 1e-6
Q_SCALE = 128 ** -0.5
EXP_CLIP = 80.0
ADAM_LR, ADAM_B1, ADAM_B2, ADAM_EPS, ADAM_WD, ADAM_STEP = 0.001, 0.9, 0.999, 1e-8, 0.01, 10
VMEM_LIMIT = 48 * 1024 * 1024
TOKEN_BLOCK = 512
ROW_TILE = 256
FFN_ROW_TILE = 128
MM_ROW_TILE = 1024
MM_TOKEN_TILE = 4096
MM_K_TILE = 3072
CHUNK_UNROLL = 8
MESH_ID = pl.DeviceIdType.MESH

NT_DIMS = (((1,), (1,)), ((), ()))
TN_DIMS = (((0,), (0,)), ((), ()))


def _cparams(sem=None):
    return pltpu.CompilerParams(dimension_semantics=sem, vmem_limit_bytes=VMEM_LIMIT)


def _sigmoid(x):
    return 1.0 / (1.0 + jnp.exp(-x))


def _dsilu(x, s):
    return s * (1.0 + x * (1.0 - s))


def _nt(a, b, precision=None):
    return lax.dot_general(a, b, NT_DIMS, precision=precision, preferred_element_type=F32)


def _tn(a, b, precision=None):
    return lax.dot_general(a, b, TN_DIMS, precision=precision, preferred_element_type=F32)


def _nn(a, b, precision=None):
    return jnp.dot(a, b, precision=precision, preferred_element_type=F32)


def _split(x, pieces):
    out = []
    for i in range(pieces):
        p = x.astype(BF16)
        out.append(p)
        if i + 1 < pieces:
            x = x - p.astype(F32)
    return out


def _sel(dot, x, sel01, pieces, x_first=True):
    acc = None
    for p in _split(x, pieces):
        term = dot(p, sel01) if x_first else dot(sel01, p)
        acc = term if acc is None else acc + term
    return acc


def _ln(x):
    mu = jnp.mean(x, axis=-1, keepdims=True)
    xc = x - mu
    rstd = lax.rsqrt(jnp.mean(xc * xc, axis=-1, keepdims=True) + LN_EPS)
    return xc * rstd, rstd


def _ln_bwd(dxh, xh, rstd):
    return rstd * (dxh - jnp.mean(dxh, axis=-1, keepdims=True) - xh * jnp.mean(dxh * xh, axis=-1, keepdims=True))


def _colsum(x):
    return jnp.sum(x, axis=0, keepdims=True)


def _tri(n, upper=False):
    r = lax.broadcasted_iota(jnp.int32, (n, n), 0)
    c = lax.broadcasted_iota(jnp.int32, (n, n), 1)
    return (c >= r) if upper else (r >= c)


def _my_pos():
    return lax.axis_index("x"), lax.axis_index("y"), lax.axis_index("c")


def _peer(pos, k):
    x, y, c = pos
    return (x ^ ((k >> 2) & 1), y ^ ((k >> 1) & 1), c ^ (k & 1))


def _flat(pos):
    return 4 * pos[0] + 2 * pos[1] + pos[2]


def allgather_vmem(v, name):
    n = v.shape[1]

    def body(v_ref, o_ref, send_sems, recv_sems, local_sem):
        me = _my_pos()
        mine = pltpu.make_async_copy(v_ref, o_ref.at[_flat(me)], local_sem)
        mine.start()
        sends = []
        for k in range(1, N_DEV):
            peer = _peer(me, k)
            cp = pltpu.make_async_remote_copy(v_ref, o_ref.at[_flat(me)], send_sems.at[k - 1], recv_sems.at[k - 1],
                                              device_id=peer, device_id_type=MESH_ID)
            cp.start()
            sends.append(cp)
        for k in range(1, N_DEV):
            peer = _peer(me, k)
            pltpu.make_async_remote_copy(v_ref, o_ref.at[_flat(peer)], send_sems.at[k - 1], recv_sems.at[k - 1],
                                         device_id=peer, device_id_type=MESH_ID).wait_recv()
        for cp in sends:
            cp.wait_send()
        mine.wait()

    return pl.pallas_call(
        body, name=name,
        out_shape=jax.ShapeDtypeStruct((N_DEV, 1, n), F32),
        in_specs=[pl.BlockSpec(memory_space=pltpu.VMEM)],
        out_specs=pl.BlockSpec(memory_space=pltpu.VMEM),
        scratch_shapes=[pltpu.SemaphoreType.DMA((N_DEV - 1,)), pltpu.SemaphoreType.DMA((N_DEV - 1,)),
                        pltpu.SemaphoreType.DMA],
        compiler_params=_cparams(),
    )(v)


def ada_modulation(c_all, w_ada_s, b_ada_r):
    ncol = w_ada_s.shape[1]

    def body(c_ref, w_ref, b_ref, o_ref, part_ref, send_sems, recv_sems):
        me = _my_pos()
        cval = c_ref[...]
        cond = cval * _sigmoid(cval)
        part = _nn(cond, w_ref[...], HI)
        for r in range(N_DEV):
            part_ref[r] = part[r:r + 1, :]
        sends = []
        for k in range(1, N_DEV):
            peer = _peer(me, k)
            cp = pltpu.make_async_remote_copy(part_ref.at[_flat(peer)], o_ref.at[_flat(me)], send_sems.at[k - 1],
                                              recv_sems.at[k - 1], device_id=peer, device_id_type=MESH_ID)
            cp.start()
            sends.append(cp)
        o_ref[_flat(me)] = part_ref[_flat(me)]
        for k in range(1, N_DEV):
            peer = _peer(me, k)
            pltpu.make_async_remote_copy(part_ref.at[_flat(peer)], o_ref.at[_flat(peer)], send_sems.at[k - 1],
                                         recv_sems.at[k - 1], device_id=peer, device_id_type=MESH_ID).wait_recv()
        for cp in sends:
            cp.wait_send()
        o_ref[...] = o_ref[...] + b_ref[...]

    return pl.pallas_call(
        body, name="ada_modulation",
        out_shape=jax.ShapeDtypeStruct((N_DEV, 1, ncol), F32),
        in_specs=[pl.BlockSpec(memory_space=pltpu.VMEM)] * 3,
        out_specs=pl.BlockSpec(memory_space=pltpu.VMEM),
        scratch_shapes=[pltpu.VMEM((N_DEV, 1, ncol), F32), pltpu.SemaphoreType.DMA((N_DEV - 1,)),
                        pltpu.SemaphoreType.DMA((N_DEV - 1,))],
        compiler_params=_cparams(),
    )(c_all, w_ada_s, b_ada_r)


def allgather_hbm(shard, name):
    def body(x_ref, out_ref, send_sems, recv_sems, local_sem):
        x, y, c = _my_pos()
        me, sibling = (x, y, c), (x, y, 1 - c)
        chips = [(1 - x, y), (x, 1 - y), (1 - x, 1 - y)]

        def slot(pos):
            return out_ref.at[_flat(pos)]

        def copy(k, block, to, src=None):
            return pltpu.make_async_remote_copy(slot(block) if src is None else src, slot(block), send_sems.at[k],
                                                recv_sems.at[k], device_id=to, device_id_type=MESH_ID)

        mine = pltpu.make_async_copy(x_ref, slot(me), local_sem)
        mine.start()
        first = [copy(0, me, sibling, src=x_ref)]
        first += [copy(1 + j, me, (*chip, c), src=x_ref) for j, chip in enumerate(chips)]
        for cp in first:
            cp.start()
        passed = [copy(4 + j, (*chip, c), sibling) for j, chip in enumerate(chips)]
        for j, chip in enumerate(chips):
            copy(1 + j, (*chip, c), me).wait_recv()
            passed[j].start()
        copy(0, sibling, me).wait_recv()
        for j, chip in enumerate(chips):
            copy(4 + j, (*chip, 1 - c), me).wait_recv()
        for cp in first + passed:
            cp.wait_send()
        mine.wait()

    return pl.pallas_call(
        body, name=name,
        out_shape=jax.ShapeDtypeStruct((N_DEV,) + shard.shape, shard.dtype),
        in_specs=[pl.BlockSpec(memory_space=pl.ANY)],
        out_specs=pl.BlockSpec(memory_space=pl.ANY),
        scratch_shapes=[pltpu.SemaphoreType.DMA((N_DEV - 1,)), pltpu.SemaphoreType.DMA((N_DEV - 1,)),
                        pltpu.SemaphoreType.DMA],
        compiler_params=_cparams(),
    )(shard)


N_CHIP = N_DEV // 2
SIBLING_SEMS = [pltpu.SemaphoreType.DMA, pltpu.SemaphoreType.DMA]
CHIP_SEMS = [pltpu.SemaphoreType.DMA((N_CHIP - 1,)), pltpu.SemaphoreType.DMA((N_CHIP - 1,)), pltpu.SemaphoreType.DMA]


def _sibling_exchange(s_ref, o_ref, send_sem, recv_sem):
    x, y, c = _my_pos()
    cp = pltpu.make_async_remote_copy(s_ref, o_ref, send_sem, recv_sem, device_id=(x, y, 1 - c), device_id_type=MESH_ID)
    return cp.start, cp.wait


def _chip_exchange(p_ref, o_ref, send_sems, recv_sems, local_sem):
    x, y, c = _my_pos()
    my_chip = 2 * x + y
    mine = pltpu.make_async_copy(p_ref.at[my_chip], o_ref.at[my_chip], local_sem)
    peers = [(x ^ (k >> 1), y ^ (k & 1)) for k in range(1, N_CHIP)]
    sends = [pltpu.make_async_remote_copy(p_ref.at[2 * px + py], o_ref.at[my_chip], send_sems.at[k], recv_sems.at[k],
                                          device_id=(px, py, c), device_id_type=MESH_ID)
             for k, (px, py) in enumerate(peers)]
    recvs = [pltpu.make_async_remote_copy(p_ref.at[2 * px + py], o_ref.at[2 * px + py], send_sems.at[k], recv_sems.at[k],
                                          device_id=(px, py, c), device_id_type=MESH_ID)
             for k, (px, py) in enumerate(peers)]

    def start():
        mine.start()
        for cp in sends:
            cp.start()

    def wait():
        for cp in recvs:
            cp.wait_recv()
        for cp in sends:
            cp.wait_send()
        mine.wait()

    return start, wait


def exchange_sibling(send, name):
    def body(s_ref, o_ref, send_sem, recv_sem):
        start, wait = _sibling_exchange(s_ref, o_ref, send_sem, recv_sem)
        start()
        wait()

    return pl.pallas_call(
        body, name=name,
        out_shape=jax.ShapeDtypeStruct(send.shape, send.dtype),
        in_specs=[pl.BlockSpec(memory_space=pl.ANY)],
        out_specs=pl.BlockSpec(memory_space=pl.ANY),
        scratch_shapes=SIBLING_SEMS,
        compiler_params=_cparams(),
    )(send)


def _k_tile(kdim):
    for cand in range(MM_K_TILE, 0, -1024):
        if kdim % cand == 0:
            return cand
    return kdim


def mm_nn(a, b, out_dtype, name):
    m, kdim = a.shape
    n = b.shape[1]
    tm, tn, tk = min(MM_ROW_TILE, m), 1024, _k_tile(kdim)
    nk = kdim // tk

    def body(a_ref, b_ref, o_ref, acc_ref):
        p = _nn(a_ref[...], b_ref[...])
        if nk == 1:
            o_ref[...] = p.astype(o_ref.dtype)
        else:
            k = pl.program_id(2)

            @pl.when(k == 0)
            def _():
                acc_ref[...] = p

            @pl.when(k > 0)
            def _():
                acc_ref[...] += p

            @pl.when(k == nk - 1)
            def _():
                o_ref[...] = acc_ref[...].astype(o_ref.dtype)

    return pl.pallas_call(
        body, name=name, grid=(n // tn, m // tm, nk),
        out_shape=jax.ShapeDtypeStruct((m, n), out_dtype),
        in_specs=[pl.BlockSpec((tm, tk), lambda j, i, k: (i, k)), pl.BlockSpec((tk, tn), lambda j, i, k: (k, j))],
        out_specs=pl.BlockSpec((tm, tn), lambda j, i, k: (i, j)),
        scratch_shapes=[pltpu.VMEM((tm, tn), F32)],
        compiler_params=_cparams(("parallel", "parallel", "arbitrary")),
    )(a, b)


def mm_nt(a, b, out_dtype, name):
    m, kdim = a.shape
    n = b.shape[0]
    tm, tn, tk = min(MM_ROW_TILE, m), 1024, _k_tile(kdim)
    nk = kdim // tk

    def body(a_ref, b_ref, o_ref, acc_ref):
        p = _nt(a_ref[...], b_ref[...])
        if nk == 1:
            o_ref[...] = p.astype(o_ref.dtype)
        else:
            k = pl.program_id(2)

            @pl.when(k == 0)
            def _():
                acc_ref[...] = p

            @pl.when(k > 0)
            def _():
                acc_ref[...] += p

            @pl.when(k == nk - 1)
            def _():
                o_ref[...] = acc_ref[...].astype(o_ref.dtype)

    return pl.pallas_call(
        body, name=name, grid=(n // tn, m // tm, nk),
        out_shape=jax.ShapeDtypeStruct((m, n), out_dtype),
        in_specs=[pl.BlockSpec((tm, tk), lambda j, i, k: (i, k)), pl.BlockSpec((tn, tk), lambda j, i, k: (j, k))],
        out_specs=pl.BlockSpec((tm, tn), lambda j, i, k: (i, j)),
        scratch_shapes=[pltpu.VMEM((tm, tn), F32)],
        compiler_params=_cparams(("parallel", "parallel", "arbitrary")),
    )(a, b)


def mm_nn_exchange(a, b, out_dtype, part, name):
    kblocks, m, kb = a.shape
    kdim = kblocks * kb
    n = b.shape[1]
    tm, tn, tk = min(MM_ROW_TILE, m), 1024, _k_tile(kdim)
    gn, gm, nk = n // tn, m // tm, kdim // tk
    per_step = tk // kb

    def body(a_ref, b_ref, part_ref, o_ref, parts_ref, acc_ref, send_sems, recv_sems, local_sem):
        j, i, k = pl.program_id(0), pl.program_id(1), pl.program_id(2)
        xchg_start, xchg_wait = _chip_exchange(part_ref, parts_ref, send_sems, recv_sems, local_sem)

        @pl.when((j == 0) & (i == 0) & (k == 0))
        def _():
            xchg_start()

        p = _nn(a_ref[0], b_ref[0:kb, :])
        for c in range(1, per_step):
            p = p + _nn(a_ref[c], b_ref[c * kb:(c + 1) * kb, :])

        @pl.when(k == 0)
        def _():
            acc_ref[...] = p

        @pl.when(k > 0)
        def _():
            acc_ref[...] += p

        @pl.when(k == nk - 1)
        def _():
            o_ref[...] = acc_ref[...].astype(o_ref.dtype)

        @pl.when((j == gn - 1) & (i == gm - 1) & (k == nk - 1))
        def _():
            xchg_wait()

    hbm = pl.BlockSpec(memory_space=pl.ANY)
    return pl.pallas_call(
        body, name=name, grid=(gn, gm, nk),
        out_shape=[jax.ShapeDtypeStruct((m, n), out_dtype), jax.ShapeDtypeStruct(part.shape, part.dtype)],
        in_specs=[pl.BlockSpec((per_step, tm, kb), lambda j, i, k: (k, i, 0)),
                  pl.BlockSpec((tk, tn), lambda j, i, k: (k, j)), hbm],
        out_specs=[pl.BlockSpec((tm, tn), lambda j, i, k: (i, j)), hbm],
        scratch_shapes=[pltpu.VMEM((tm, tn), F32)] + CHIP_SEMS,
        compiler_params=_cparams(("arbitrary", "arbitrary", "arbitrary")),
    )(a, b, part)


def mm_nt_gather(a, b, out_dtype, shard, name):
    m, kdim = a.shape
    n = b.shape[0]
    tm, tn = min(MM_ROW_TILE, m), 1024
    assert kdim == 1024
    gj = m // tm
    nsteps = (n // tn) * gj
    forward_step = max(nsteps - 3, 0)

    def body(a_ref, b_ref, x_ref, o_ref, g_ref, send_sems, recv_sems, local_sem):
        step = pl.program_id(0) * gj + pl.program_id(1)
        x, y, c = _my_pos()
        me, sibling = (x, y, c), (x, y, 1 - c)
        chips = [(1 - x, y), (x, 1 - y), (1 - x, 1 - y)]

        def slot(pos):
            return g_ref.at[_flat(pos)]

        def copy(k, block, to, src=None):
            return pltpu.make_async_remote_copy(slot(block) if src is None else src, slot(block), send_sems.at[k],
                                                recv_sems.at[k], device_id=to, device_id_type=MESH_ID)

        mine = pltpu.make_async_copy(x_ref, slot(me), local_sem)
        first = [copy(0, me, sibling, src=x_ref)]
        first += [copy(1 + j, me, (*chip, c), src=x_ref) for j, chip in enumerate(chips)]
        passed = [copy(4 + j, (*chip, c), sibling) for j, chip in enumerate(chips)]

        @pl.when(step == 0)
        def _():
            mine.start()
            for cp in first:
                cp.start()

        o_ref[...] = _nt(a_ref[...], b_ref[...]).astype(o_ref.dtype)

        @pl.when(step == forward_step)
        def _():
            for j, chip in enumerate(chips):
                copy(1 + j, (*chip, c), me).wait_recv()
                passed[j].start()

        @pl.when(step == nsteps - 1)
        def _():
            copy(0, sibling, me).wait_recv()
            for j, chip in enumerate(chips):
                copy(4 + j, (*chip, 1 - c), me).wait_recv()
            for cp in first + passed:
                cp.wait_send()
            mine.wait()

    return pl.pallas_call(
        body, name=name, grid=(n // tn, gj),
        out_shape=[jax.ShapeDtypeStruct((m, n), out_dtype), jax.ShapeDtypeStruct((N_DEV,) + shard.shape, shard.dtype)],
        in_specs=[pl.BlockSpec((tm, kdim), lambda j, i: (i, 0)), pl.BlockSpec((tn, kdim), lambda j, i: (j, 0)),
                  pl.BlockSpec(memory_space=pl.ANY)],
        out_specs=[pl.BlockSpec((tm, tn), lambda j, i: (i, j)), pl.BlockSpec(memory_space=pl.ANY)],
        scratch_shapes=[pltpu.SemaphoreType.DMA((N_DEV - 1,)), pltpu.SemaphoreType.DMA((N_DEV - 1,)),
                        pltpu.SemaphoreType.DMA],
        compiler_params=_cparams(("arbitrary", "arbitrary")),
    )(a, b, shard)


def mm_tn(a, b, name):
    tt, tka, tn = min(MM_TOKEN_TILE, b.shape[0]), 1024, 1024
    if a.ndim == 3:
        t, ka = a.shape[1], a.shape[0] * a.shape[2]
        a_spec = pl.BlockSpec((None, tt, tka), lambda i, j, s: (i, s, 0))
    else:
        t, ka = a.shape
        a_spec = pl.BlockSpec((tt, tka), lambda i, j, s: (s, i))
    n = b.shape[1]
    nt = t // tt

    def body(a_ref, b_ref, o_ref, *acc):
        p = _tn(a_ref[...], b_ref[...])
        if nt == 1:
            o_ref[...] = p.astype(o_ref.dtype)
        else:
            acc_ref, s = acc[0], pl.program_id(2)

            @pl.when(s == 0)
            def _():
                acc_ref[...] = p

            @pl.when(s > 0)
            def _():
                acc_ref[...] += p

            @pl.when(s == nt - 1)
            def _():
                o_ref[...] = acc_ref[...].astype(o_ref.dtype)

    return pl.pallas_call(
        body, name=name, grid=(ka // tka, n // tn, nt),
        out_shape=jax.ShapeDtypeStruct((ka, n), BF16),
        in_specs=[a_spec, pl.BlockSpec((tt, tn), lambda i, j, s: (s, j))],
        out_specs=pl.BlockSpec((tka, tn), lambda i, j, s: (i, j)),
        scratch_shapes=[] if nt == 1 else [pltpu.VMEM((tka, tn), F32)],
        compiler_params=_cparams(("parallel", "parallel", "arbitrary")),
    )(a, b)


def _tile(t, cap):
    return min(cap, t)


def ln_modulate(x, mod6, shift_row, scale_row, name):
    t = x.shape[0]
    tm = _tile(t, ROW_TILE)

    def body(x_ref, mod_ref, o_ref):
        xh, _ = _ln(x_ref[...])
        sc = mod_ref[scale_row:scale_row + 1, :]
        sh = mod_ref[shift_row:shift_row + 1, :]
        o_ref[...] = (xh * (1.0 + sc) + sh).astype(BF16)

    return pl.pallas_call(
        body, name=name, grid=(t // tm,),
        out_shape=jax.ShapeDtypeStruct((t, D), BF16),
        in_specs=[pl.BlockSpec((tm, D), lambda i: (i, 0)), pl.BlockSpec((6, D), lambda i: (0, 0))],
        out_specs=pl.BlockSpec((tm, D), lambda i: (i, 0)),
        compiler_params=_cparams(("parallel",)),
    )(x, mod6)


def resid_ln(x, h, mod6, gate_row, ln_g, ln_b, name):
    t = x.shape[0]
    tm = _tile(t, ROW_TILE)

    def body(x_ref, h_ref, mod_ref, g_ref, b_ref, o_ref):
        r = ALPHA * x_ref[...] + mod_ref[gate_row:gate_row + 1, :] * h_ref[...]
        rh, _ = _ln(r)
        o_ref[...] = rh * g_ref[...] + b_ref[...]

    row = pl.BlockSpec((tm, D), lambda i: (i, 0))
    vec = pl.BlockSpec((1, D), lambda i: (0, 0))
    return pl.pallas_call(
        body, name=name, grid=(t // tm,),
        out_shape=jax.ShapeDtypeStruct((t, D), F32),
        in_specs=[row, row, pl.BlockSpec((6, D), lambda i: (0, 0)), vec, vec],
        out_specs=row,
        compiler_params=_cparams(("parallel",)),
    )(x, h, mod6, ln_g, ln_b)


def resid_ln_bwd(x, h, mod6, gate_row, ln_g, ln_b, cot, with_loss, name):
    t = x.shape[0]
    tm = _tile(t, ROW_TILE)

    def body(x_ref, h_ref, mod_ref, g_ref, b_ref, c_ref, dh_ref, dx_ref, acc_ref):
        @pl.when(pl.program_id(0) == 0)
        def _():
            acc_ref[...] = jnp.zeros_like(acc_ref)

        gate = mod_ref[gate_row:gate_row + 1, :]
        hv = h_ref[...]
        r = ALPHA * x_ref[...] + gate * hv
        rh, rstd = _ln(r)
        lng = g_ref[...]
        if with_loss:
            diff = rh * lng + b_ref[...] - c_ref[...]
            dxo = diff * (1.0 / D)
            lsum = jnp.sum(_colsum(diff * diff), axis=-1, keepdims=True) * (0.5 / D)
            acc_ref[3:4, :] += jnp.broadcast_to(lsum, (1, D))
        else:
            dxo = c_ref[...]
        acc_ref[1:2, :] += _colsum(dxo * rh)
        acc_ref[2:3, :] += _colsum(dxo)
        dr = _ln_bwd(dxo * lng, rh, rstd)
        acc_ref[0:1, :] += _colsum(dr * hv)
        dh_ref[...] = (gate * dr).astype(BF16)
        dx_ref[...] = ALPHA * dr

    row = pl.BlockSpec((tm, D), lambda i: (i, 0))
    vec = pl.BlockSpec((1, D), lambda i: (0, 0))
    return pl.pallas_call(
        body, name=name, grid=(t // tm,),
        out_shape=[jax.ShapeDtypeStruct((t, D), BF16), jax.ShapeDtypeStruct((t, D), F32),
                   jax.ShapeDtypeStruct((8, D), F32)],
        in_specs=[row, row, pl.BlockSpec((6, D), lambda i: (0, 0)), vec, vec, row],
        out_specs=[row, row, pl.BlockSpec((8, D), lambda i: (0, 0))],
        compiler_params=_cparams(("arbitrary",)),
    )(x, h, mod6, ln_g, ln_b, cot)


def ln_modulate_bwd(x, du, mod6, scale_row, dx_part, name):
    t = x.shape[0]
    tm = _tile(t, ROW_TILE)

    def body(x_ref, du_ref, mod_ref, dp_ref, dx_ref, acc_ref):
        @pl.when(pl.program_id(0) == 0)
        def _():
            acc_ref[...] = jnp.zeros_like(acc_ref)

        xh, rstd = _ln(x_ref[...])
        du_v = du_ref[...]
        sc = mod_ref[scale_row:scale_row + 1, :]
        acc_ref[0:1, :] += _colsum(du_v * xh)
        acc_ref[1:2, :] += _colsum(du_v)
        dx_ref[...] = dp_ref[...] + _ln_bwd(du_v * (1.0 + sc), xh, rstd)

    row = pl.BlockSpec((tm, D), lambda i: (i, 0))
    return pl.pallas_call(
        body, name=name, grid=(t // tm,),
        out_shape=[jax.ShapeDtypeStruct((t, D), F32), jax.ShapeDtypeStruct((8, D), F32)],
        in_specs=[row, row, pl.BlockSpec((6, D), lambda i: (0, 0)), row],
        out_specs=[row, pl.BlockSpec((8, D), lambda i: (0, 0))],
        compiler_params=_cparams(("arbitrary",)),
    )(x, du, mod6, dx_part)


def merge_gates(ya, yb, proj):
    t = ya.shape[0]
    tm = _tile(t, ROW_TILE)

    def body(ya_ref, yb_ref, ga_ref, gb_ref, o_ref):
        o_ref[...] = (_sigmoid(ga_ref[...]) * ya_ref[...].astype(F32) +
                      _sigmoid(gb_ref[...]) * yb_ref[...].astype(F32)).astype(BF16)

    row = pl.BlockSpec((tm, D), lambda i: (i, 0))
    return pl.pallas_call(
        body, name="merge_gates", grid=(t // tm,),
        out_shape=jax.ShapeDtypeStruct((t, D), BF16),
        in_specs=[row, row, pl.BlockSpec((tm, D), lambda i: (i, GATE_BLOCK0)),
                  pl.BlockSpec((tm, D), lambda i: (i, GATE_BLOCK0 + 1))],
        out_specs=row,
        compiler_params=_cparams(("parallel",)),
    )(ya, yb, proj, proj)


def merge_gates_bwd(dm, ya, yb, proj):
    t = ya.shape[0]
    tm = _tile(t, ROW_TILE)

    def body(dm_ref, ya_ref, yb_ref, ga_ref, gb_ref, dya_ref, dyb_ref, dp_ref):
        dmv = dm_ref[...].astype(F32)
        sa = _sigmoid(ga_ref[...])
        sb = _sigmoid(gb_ref[...])
        dya_ref[...] = (dmv * sa).astype(BF16)
        dyb_ref[...] = (dmv * sb).astype(BF16)
        dp_ref[0] = (dmv * ya_ref[...].astype(F32) * sa * (1.0 - sa)).astype(BF16)
        dp_ref[1] = (dmv * yb_ref[...].astype(F32) * sb * (1.0 - sb)).astype(BF16)

    row = pl.BlockSpec((tm, D), lambda i: (i, 0))
    return pl.pallas_call(
        body, name="merge_gates_bwd", grid=(t // tm,),
        out_shape=[jax.ShapeDtypeStruct((t, D), BF16)] * 2 + [jax.ShapeDtypeStruct((N_PROJ // D, t, D), BF16)],
        in_specs=[row, row, row, pl.BlockSpec((tm, D), lambda i: (i, GATE_BLOCK0)),
                  pl.BlockSpec((tm, D), lambda i: (i, GATE_BLOCK0 + 1))],
        out_specs=[row, row, pl.BlockSpec((2, tm, D), lambda i: (GATE_BLOCK0 // 2, i, 0))],
        compiler_params=_cparams(("parallel",)),
    )(dm, ya, yb, proj, proj)


def swiglu_act(gu):
    t = gu.shape[0]
    tm = _tile(t, FFN_ROW_TILE)

    def body(gu_ref, o_ref):
        for j in range(D_FF_PAD // D):
            g = gu_ref[:, j * D:(j + 1) * D].astype(F32)
            u = gu_ref[:, D_FF_PAD + j * D:D_FF_PAD + (j + 1) * D].astype(F32)
            o_ref[:, j * D:(j + 1) * D] = (g * _sigmoid(g) * u).astype(BF16)

    return pl.pallas_call(
        body, name="swiglu_act", grid=(t // tm,),
        out_shape=jax.ShapeDtypeStruct((t, D_FF_PAD), BF16),
        in_specs=[pl.BlockSpec((tm, 2 * D_FF_PAD), lambda i: (i, 0))],
        out_specs=pl.BlockSpec((tm, D_FF_PAD), lambda i: (i, 0)),
        compiler_params=_cparams(("parallel",)),
    )(gu)


def swiglu_act_bwd(gu, dact):
    t = gu.shape[0]
    tm = _tile(t, FFN_ROW_TILE)

    def body(gu_ref, da_ref, o_ref):
        for j in range(D_FF_PAD // D):
            g = gu_ref[:, j * D:(j + 1) * D].astype(F32)
            u = gu_ref[:, D_FF_PAD + j * D:D_FF_PAD + (j + 1) * D].astype(F32)
            da = da_ref[:, j * D:(j + 1) * D].astype(F32)
            s = _sigmoid(g)
            o_ref[:, j * D:(j + 1) * D] = (da * u * _dsilu(g, s)).astype(BF16)
            o_ref[:, D_FF_PAD + j * D:D_FF_PAD + (j + 1) * D] = (da * g * s).astype(BF16)

    return pl.pallas_call(
        body, name="swiglu_act_bwd", grid=(t // tm,),
        out_shape=jax.ShapeDtypeStruct((t, 2 * D_FF_PAD), BF16),
        in_specs=[pl.BlockSpec((tm, 2 * D_FF_PAD), lambda i: (i, 0)), pl.BlockSpec((tm, D_FF_PAD), lambda i: (i, 0))],
        out_specs=pl.BlockSpec((tm, 2 * D_FF_PAD), lambda i: (i, 0)),
        compiler_params=_cparams(("parallel",)),
    )(gu, dact)


def _hgrn_chunk_terms(q, fl, lbv, tril_f):
    sig = _sigmoid(fl)
    f = lbv + (1.0 - lbv) * sig
    lam = jnp.log(f)
    k = 1.0 - f
    sq = _sigmoid(q)
    qt = q * sq * Q_SCALE
    bc = _sel(_nn, lam, tril_f, 3, x_first=False)
    bmid = bc[CHUNK // 2 - 1:CHUNK // 2, :]
    bl = bc[CHUNK - 1:CHUNK, :]
    eq = jnp.exp(jnp.minimum(bc - bmid, EXP_CLIP))
    ek = jnp.exp(jnp.minimum(bmid - bc, EXP_CLIP))
    eb = jnp.exp(bc)
    ekl = jnp.exp(bl - bc)
    ebl = jnp.exp(bl)
    return sig, f, k, sq, qt, eq, ek, eb, ekl, ebl


def hgrn_fwd(proj, lb, gnorm):
    t = proj.shape[0]
    tb = _tile(t, TOKEN_BLOCK)
    ncb = tb // CHUNK

    def body(q_ref, f_ref, i_ref, g_ref, lb_ref, gn_ref, oa_ref, oraw_ref, st_ref, state):
        @pl.when(pl.program_id(1) == 0)
        def _():
            state[...] = jnp.zeros_like(state)

        lbv = lb_ref[...]
        gn = gn_ref[...]
        mask = _tri(CHUNK)
        tril_f = mask.astype(BF16)

        def chunk(c, carry):
            sl = pl.ds(pl.multiple_of(c * CHUNK, CHUNK), CHUNK)
            q, fl, v, g = q_ref[sl, :], f_ref[sl, :], i_ref[sl, :], g_ref[sl, :]
            sig, f, k, sq, qt, eq, ek, eb, ekl, ebl = _hgrn_chunk_terms(q, fl, lbv, tril_f)
            a = jnp.where(mask, _nt((qt * eq).astype(BF16), (k * ek).astype(BF16)), 0.0)
            st = state[...]
            st_ref[0, c] = st
            vb = v.astype(BF16)
            o = _nn(a.astype(BF16), vb) + _nt((qt * eb).astype(BF16), st.astype(BF16))
            state[...] = st * ebl + _tn(vb, (k * ekl).astype(BF16))
            oraw_ref[sl, :] = o
            rn = o * lax.rsqrt(jnp.mean(o * o, axis=-1, keepdims=True) + RMS_EPS)
            oa_ref[sl, :] = (rn * gn * g * _sigmoid(g)).astype(BF16)
            return carry

        lax.fori_loop(0, ncb, chunk, 0, unroll=min(CHUNK_UNROLL, ncb))

    def col(block):
        return pl.BlockSpec((tb, HK), lambda h, j: (j, block * N_HEADS_A + h))

    return pl.pallas_call(
        body, name="hgrn_fwd", grid=(N_HEADS_A, t // tb),
        out_shape=[jax.ShapeDtypeStruct((t, D), BF16), jax.ShapeDtypeStruct((t, D), F32),
                   jax.ShapeDtypeStruct((N_HEADS_A, t // CHUNK, HK, HK), F32)],
        in_specs=[col(0), col(1), col(2), col(3), pl.BlockSpec((1, HK), lambda h, j: (0, h)),
                  pl.BlockSpec((1, HK), lambda h, j: (0, 0))],
        out_specs=[pl.BlockSpec((tb, HK), lambda h, j: (j, h)), pl.BlockSpec((tb, HK), lambda h, j: (j, h)),
                   pl.BlockSpec((1, ncb, HK, HK), lambda h, j: (h, j, 0, 0))],
        scratch_shapes=[pltpu.VMEM((HK, HK), F32)],
        compiler_params=_cparams(("parallel", "arbitrary")),
    )(proj, proj, proj, proj, lb, gnorm)


def hgrn_bwd(proj, lb, gnorm, o_raw, doa, states, give, dproj):
    t = proj.shape[0]
    tb = _tile(t, TOKEN_BLOCK)
    ncb = tb // CHUNK
    nb = t // tb

    def body(q_ref, f_ref, i_ref, g_ref, lb_ref, gn_ref, oraw_ref, doa_ref, st_ref, give_ref, dp_in_ref,
             dp_ref, dlb_ref, dgn_ref, got_ref, dstate, send_sem, recv_sem):
        h, j = pl.program_id(0), pl.program_id(1)
        swap_start, swap_wait = _sibling_exchange(give_ref, got_ref, send_sem, recv_sem)

        @pl.when((h == 0) & (j == 0))
        def _():
            swap_start()

        @pl.when(j == 0)
        def _():
            dstate[...] = jnp.zeros_like(dstate)
            dlb_ref[...] = jnp.zeros_like(dlb_ref)

        @pl.when((j == 0) & (h == 0))
        def _():
            dgn_ref[...] = jnp.zeros_like(dgn_ref)

        lbv = lb_ref[...]
        gn = gn_ref[...]
        mask = _tri(CHUNK)
        mask_t = _tri(CHUNK, upper=True)
        tril_f = mask.astype(BF16)
        triu_f = mask_t.astype(BF16)

        def chunk(i, c0):
            c = ncb - 1 - i
            sl = pl.ds(pl.multiple_of(c * CHUNK, CHUNK), CHUNK)
            q, fl, v, g = q_ref[sl, :], f_ref[sl, :], i_ref[sl, :], g_ref[sl, :]
            sig, f, k, sq, qt, eq, ek, eb, ekl, ebl = _hgrn_chunk_terms(q, fl, lbv, tril_f)
            qe = (qt * eq).astype(BF16)
            ke = (k * ek).astype(BF16)
            st32 = st_ref[0, c]
            st = st32.astype(BF16)
            dst = dstate[...]
            dstb = dst.astype(BF16)
            o = oraw_ref[sl, :]
            rstd = lax.rsqrt(jnp.mean(o * o, axis=-1, keepdims=True) + RMS_EPS)
            rn = o * rstd
            sgm = _sigmoid(g)
            sg = g * sgm
            doa_v = doa_ref[sl, :]
            drn = doa_v * gn * sg
            dgn_ref[...] += _colsum(doa_v * rn * sg)
            dp_ref[3, sl, :] = (doa_v * rn * gn * _dsilu(g, sgm)).astype(BF16)
            do = rstd * (drn - rn * jnp.mean(drn * rn, axis=-1, keepdims=True))
            dob = do.astype(BF16)
            vb = v.astype(BF16)
            da = jnp.where(mask, _nt(dob, vb), 0.0).astype(BF16)
            da_t = jnp.where(mask_t, _nt(vb, dob), 0.0).astype(BF16)
            a_t = jnp.where(mask_t, _nt(ke, qe), 0.0).astype(BF16)
            kl = (k * ekl).astype(BF16)
            qb = (qt * eb).astype(BF16)
            dq_in = _nn(da, ke)
            dk_in = _nn(da_t, qe)
            dq_out = eb * _nn(dob, st)
            dk_out = ekl * _nn(vb, dstb)
            dqt = eq * dq_in + dq_out
            dk = ek * dk_in + dk_out
            dv = _nn(a_t, dob) + _nt(kl, dstb)
            dstate[...] = dst * ebl + _tn(dob, qb)
            dbig = qe.astype(F32) * dq_in - ke.astype(F32) * dk_in + qt * dq_out - k * dk_out
            beyond = _colsum(k * dk_out) + ebl * _colsum(dst * st32)
            dlam = _sel(_nn, dbig, triu_f, 3, x_first=False) + beyond
            df = dlam / f - dk
            dp_ref[1, sl, :] = (df * (1.0 - lbv) * sig * (1.0 - sig)).astype(BF16)
            dlb_ref[...] += _colsum(df * (1.0 - sig))
            dp_ref[0, sl, :] = (dqt * Q_SCALE * _dsilu(q, sq)).astype(BF16)
            dp_ref[2, sl, :] = dv.astype(BF16)
            return c0

        lax.fori_loop(0, ncb, chunk, 0, unroll=min(CHUNK_UNROLL, ncb))

        @pl.when((h == N_HEADS_A - 1) & (j == nb - 1))
        def _():
            swap_wait()

    def col(block):
        return pl.BlockSpec((tb, HK), lambda h, j: (nb - 1 - j, block * N_HEADS_A + h))

    hcol = pl.BlockSpec((tb, HK), lambda h, j: (nb - 1 - j, h))
    hbm = pl.BlockSpec(memory_space=pl.ANY)
    return pl.pallas_call(
        body, name="hgrn_bwd", grid=(N_HEADS_A, nb),
        out_shape=[jax.ShapeDtypeStruct(dproj.shape, dproj.dtype), jax.ShapeDtypeStruct((1, D), F32),
                   jax.ShapeDtypeStruct((1, HK), F32), jax.ShapeDtypeStruct(give.shape, give.dtype)],
        in_specs=[col(0), col(1), col(2), col(3), pl.BlockSpec((1, HK), lambda h, j: (0, h)),
                  pl.BlockSpec((1, HK), lambda h, j: (0, 0)), hcol, hcol,
                  pl.BlockSpec((1, ncb, HK, HK), lambda h, j: (h, nb - 1 - j, 0, 0)), hbm, hbm],
        out_specs=[pl.BlockSpec((4, tb, HK), lambda h, j: (0, nb - 1 - j, h)),
                   pl.BlockSpec((1, HK), lambda h, j: (0, h)), pl.BlockSpec((1, HK), lambda h, j: (0, 0)), hbm],
        input_output_aliases={10: 0},
        scratch_shapes=[pltpu.VMEM((HK, HK), F32)] + SIBLING_SEMS,
        compiler_params=_cparams(("arbitrary", "arbitrary")),
    )(proj, proj, proj, proj, lb, gnorm, o_raw, doa, states, give, dproj)


CONV_BLOCK0 = 6
CONV_TAPS = 4
HALO = 8


def conv_fwd(proj, conv_w, conv_b):
    t = proj.shape[0]
    tm = _tile(t, ROW_TILE)
    r = tm // HALO

    def body(x_ref, halo_ref, w_ref, b_ref, o_ref):
        i = pl.program_id(1)
        halo = jnp.where(i > 0, halo_ref[...], 0.0)
        ext = jnp.concatenate([halo, x_ref[...]], axis=0)
        pre = b_ref[...] + w_ref[CONV_TAPS - 1:CONV_TAPS, :] * ext[HALO:, :]
        for tap in range(CONV_TAPS - 1):
            pre = pre + w_ref[tap:tap + 1, :] * pltpu.roll(ext, CONV_TAPS - 1 - tap, axis=0)[HALO:, :]
        o_ref[...] = pre * _sigmoid(pre)

    return pl.pallas_call(
        body, name="conv_fwd", grid=(CONV_DIM // D, t // tm),
        out_shape=jax.ShapeDtypeStruct((t, CONV_DIM), F32),
        in_specs=[pl.BlockSpec((tm, D), lambda cb, i: (i, CONV_BLOCK0 + cb)),
                  pl.BlockSpec((HALO, D), lambda cb, i: (jnp.maximum(i * r - 1, 0), CONV_BLOCK0 + cb)),
                  pl.BlockSpec((CONV_TAPS, D), lambda cb, i: (0, cb)), pl.BlockSpec((1, D), lambda cb, i: (0, cb))],
        out_specs=pl.BlockSpec((tm, D), lambda cb, i: (i, cb)),
        compiler_params=_cparams(("parallel", "parallel")),
    )(proj, proj, conv_w, conv_b)


def conv_bwd(proj, dxc, conv_w, conv_b, dproj):
    t = proj.shape[0]
    tm = _tile(t, ROW_TILE)
    r = tm // HALO
    n = t // tm
    last_halo = t // HALO - 1

    def body(x_ref, prev_ref, next_ref, d_ref, dnext_ref, w_ref, b_ref, dp_in_ref, dx_ref, dw_ref, db_ref):
        i = pl.program_id(1)

        @pl.when(i == 0)
        def _():
            dw_ref[...] = jnp.zeros_like(dw_ref)
            db_ref[...] = jnp.zeros_like(db_ref)

        prev = jnp.where(i > 0, prev_ref[...], 0.0)
        ext = jnp.concatenate([prev, x_ref[...], next_ref[...]], axis=0)
        shifted = [pltpu.roll(ext, CONV_TAPS - 1 - tap, axis=0)[HALO:, :] for tap in range(CONV_TAPS - 1)]
        shifted.append(ext[HALO:, :])
        pre = b_ref[...]
        for tap in range(CONV_TAPS):
            pre = pre + w_ref[tap:tap + 1, :] * shifted[tap]
        s = _sigmoid(pre)
        d_ext = jnp.concatenate([d_ref[...].astype(F32),
                                 jnp.where(i < n - 1, dnext_ref[0:HALO, :].astype(F32), 0.0)], axis=0)
        dpre = d_ext * _dsilu(pre, s)
        dx = w_ref[CONV_TAPS - 1:CONV_TAPS, :] * dpre[:tm, :]
        for tap in range(CONV_TAPS - 1):
            back = CONV_TAPS - 1 - tap
            dx = dx + w_ref[tap:tap + 1, :] * pltpu.roll(dpre, tm + HALO - back, axis=0)[:tm, :]
        dx_ref[...] = dx.astype(BF16)
        dp = dpre[:tm, :]
        db_ref[...] += _colsum(dp)
        for tap in range(CONV_TAPS):
            dw_ref[tap:tap + 1, :] += _colsum(dp * shifted[tap][:tm, :])

    return pl.pallas_call(
        body, name="conv_bwd", grid=(CONV_DIM // D, n),
        out_shape=[jax.ShapeDtypeStruct(dproj.shape, dproj.dtype), jax.ShapeDtypeStruct((8, CONV_DIM), F32),
                   jax.ShapeDtypeStruct((1, CONV_DIM), F32)],
        in_specs=[pl.BlockSpec((tm, D), lambda cb, i: (i, CONV_BLOCK0 + cb)),
                  pl.BlockSpec((HALO, D), lambda cb, i: (jnp.maximum(i * r - 1, 0), CONV_BLOCK0 + cb)),
                  pl.BlockSpec((HALO, D), lambda cb, i: (jnp.minimum((i + 1) * r, last_halo), CONV_BLOCK0 + cb)),
                  pl.BlockSpec((tm, D), lambda cb, i: (i, cb)),
                  pl.BlockSpec((2 * HALO, D), lambda cb, i: (jnp.minimum((i + 1) * (r // 2), last_halo // 2), cb)),
                  pl.BlockSpec((CONV_TAPS, D), lambda cb, i: (0, cb)), pl.BlockSpec((1, D), lambda cb, i: (0, cb)),
                  pl.BlockSpec(memory_space=pl.ANY)],
        out_specs=[pl.BlockSpec((None, tm, D), lambda cb, i: (CONV_BLOCK0 + cb, i, 0)),
                   pl.BlockSpec((8, D), lambda cb, i: (0, cb)), pl.BlockSpec((1, D), lambda cb, i: (0, cb))],
        input_output_aliases={7: 0},
        compiler_params=_cparams(("parallel", "arbitrary")),
    )(proj, proj, proj, dxc, dxc, conv_w, conv_b, dproj)


def dt_fill(ddt, dproj):
    t = ddt.shape[0]
    tm = _tile(t, ROW_TILE)
    w = ddt.shape[1]

    def body(d_ref, dp_in_ref, o_ref):
        o_ref[:, :w] = d_ref[...]
        o_ref[:, w:] = jnp.zeros((tm, D - w), o_ref.dtype)

    return pl.pallas_call(
        body, name="dt_fill", grid=(t // tm,),
        out_shape=jax.ShapeDtypeStruct(dproj.shape, dproj.dtype),
        in_specs=[pl.BlockSpec((tm, w), lambda i: (i, 0)), pl.BlockSpec(memory_space=pl.ANY)],
        out_specs=pl.BlockSpec((None, tm, D), lambda i: (DT_COL_BLOCK, i, 0)),
        input_output_aliases={1: 0},
        compiler_params=_cparams(("parallel",)),
    )(ddt, dproj)


Z_BLOCK0 = 8
DT_COL_BLOCK = 9
DT_BLOCK0 = 8 * DT_COL_BLOCK
GATE_BLOCK0 = 10
B_BLOCK0 = 16
C_BLOCK0 = 20


def _head_expand():
    e = np.zeros((N_STATE, GROUP_W), np.float32)
    for hh in range(HEADS_PER_GROUP):
        e[hh, hh * HEAD_P:(hh + 1) * HEAD_P] = 1.0
    return jnp.asarray(e, BF16)


def _ssd_chunk_terms(dt, bias, alog, expand, tril_f, eye):
    dtb = dt + bias
    delta = jnp.maximum(dtb, 0.0) + jnp.log(1.0 + jnp.exp(-jnp.abs(dtb)))
    ea = jnp.exp(alog)
    a = -ea * delta
    acum = _sel(_nn, a, tril_f, 3, x_first=False)
    delta_e = _sel(_nn, delta, expand, 2)
    acum_e = _sel(_nn, acum, expand, 3)
    acum_t = _sel(_nt, acum, eye, 3, x_first=False)
    return dtb, delta, ea, a, acum, delta_e, acum_e, acum_t


def ssd_fwd(proj, xc, alog4, bias4, dskip4, wnorm, expand):
    t = proj.shape[0]
    tb = _tile(t, TOKEN_BLOCK)
    ncb = tb // SSD_CHUNK

    def body(xs_ref, b_ref, c_ref, dt_ref, z_ref, alog_ref, bias_ref, dsk_ref, wn_ref, e_ref, ob_ref, st_ref, state):
        @pl.when(pl.program_id(1) == 0)
        def _():
            state[...] = jnp.zeros_like(state)

        expand = e_ref[...]
        mask = _tri(SSD_CHUNK)
        tril_f = mask.astype(BF16)
        eye = (lax.broadcasted_iota(jnp.int32, (N_STATE, N_STATE), 0) ==
               lax.broadcasted_iota(jnp.int32, (N_STATE, N_STATE), 1)).astype(BF16)
        alog, bias = alog_ref[0], bias_ref[0]
        d_e = _sel(_nn, jnp.broadcast_to(dsk_ref[0], (8, N_STATE)), expand, 3)[0:1, :]
        wn = wn_ref[...]

        def chunk(c, carry):
            sl = pl.ds(pl.multiple_of(c * SSD_CHUNK, SSD_CHUNK), SSD_CHUNK)
            xs, bm, cm, dt, z = xs_ref[sl, :], b_ref[sl, :], c_ref[sl, :], dt_ref[sl, :], z_ref[sl, :]
            dtb, delta, ea, a, acum, delta_e, acum_e, acum_t = _ssd_chunk_terms(dt, bias, alog, expand, tril_f, eye)
            alast_e = acum_e[SSD_CHUNK - 1:SSD_CHUNK, :]
            xd = xs * delta_e
            xdb = xd.astype(BF16)
            cb_, bb_ = cm.astype(BF16), bm.astype(BF16)
            cbm = _nt(cb_, bb_)
            ys = []
            for hh in range(HEADS_PER_GROUP):
                lh = jnp.where(mask, jnp.exp(jnp.minimum(acum[:, hh:hh + 1] - acum_t[hh:hh + 1, :], 0.0)), 0.0)
                ys.append(_nn((cbm * lh).astype(BF16), xdb[:, hh * HEAD_P:(hh + 1) * HEAD_P]))
            st = state[...]
            st_ref[0, c] = st
            y = jnp.concatenate(ys, axis=1) + _nn(cb_, st.astype(BF16)) * jnp.exp(acum_e) + xs * d_e
            state[...] = st * jnp.exp(alast_e) + _tn(bb_, (xd * jnp.exp(alast_e - acum_e)).astype(BF16))
            yg = y * z * _sigmoid(z)
            ob_ref[sl, :] = (yg * lax.rsqrt(jnp.mean(yg * yg, axis=-1, keepdims=True) + RMS_EPS) * wn).astype(BF16)
            return carry

        lax.fori_loop(0, ncb, chunk, 0, unroll=min(CHUNK_UNROLL, ncb))

    small = pl.BlockSpec((1, 1, N_STATE), lambda g, j: (g, 0, 0))
    return pl.pallas_call(
        body, name="ssd_fwd", grid=(N_GROUPS, t // tb),
        out_shape=[jax.ShapeDtypeStruct((t, B_INNER), BF16),
                   jax.ShapeDtypeStruct((N_GROUPS, t // SSD_CHUNK, N_STATE, GROUP_W), F32)],
        in_specs=[pl.BlockSpec((tb, GROUP_W), lambda g, j: (j, g)),
                  pl.BlockSpec((tb, N_STATE), lambda g, j: (j, B_BLOCK0 + g)),
                  pl.BlockSpec((tb, N_STATE), lambda g, j: (j, C_BLOCK0 + g)),
                  pl.BlockSpec((tb, N_STATE), lambda g, j: (j, DT_BLOCK0 + g)),
                  pl.BlockSpec((tb, GROUP_W), lambda g, j: (j, Z_BLOCK0 + g)),
                  small, small, small, pl.BlockSpec((1, GROUP_W), lambda g, j: (0, g)),
                  pl.BlockSpec((N_STATE, GROUP_W), lambda g, j: (0, 0))],
        out_specs=[pl.BlockSpec((tb, GROUP_W), lambda g, j: (j, g)),
                   pl.BlockSpec((1, ncb, N_STATE, GROUP_W), lambda g, j: (g, j, 0, 0))],
        scratch_shapes=[pltpu.VMEM((N_STATE, GROUP_W), F32)],
        compiler_params=_cparams(("parallel", "arbitrary")),
    )(xc, xc, xc, proj, proj, alog4, bias4, dskip4, wnorm, expand)


def ssd_bwd(proj, xc, alog4, bias4, dskip4, wnorm, expand, dob, states, part, dproj):
    t = proj.shape[0]
    tb = _tile(t, TOKEN_BLOCK)
    ncb = tb // SSD_CHUNK
    nb = t // tb

    def body(xs_ref, b_ref, c_ref, dt_ref, z_ref, alog_ref, bias_ref, dsk_ref, wn_ref, e_ref, dob_ref, st_ref, part_ref,
             dp_in_ref, dxs_ref, db_ref, dc_ref, dz_ref, ddt_ref, dwn_ref, dalog_ref, dbias_ref, ddsk_ref, parts_ref, dstate,
             send_sems, recv_sems, local_sem):
        xchg_start, xchg_wait = _chip_exchange(part_ref, parts_ref, send_sems, recv_sems, local_sem)

        @pl.when((pl.program_id(0) == 0) & (pl.program_id(1) == 0))
        def _():
            xchg_start()

        @pl.when(pl.program_id(1) == 0)
        def _():
            dstate[...] = jnp.zeros_like(dstate)
            dwn_ref[...] = jnp.zeros_like(dwn_ref)
            dalog_ref[...] = jnp.zeros_like(dalog_ref)
            dbias_ref[...] = jnp.zeros_like(dbias_ref)
            ddsk_ref[...] = jnp.zeros_like(ddsk_ref)

        expand = e_ref[...]
        mask = _tri(SSD_CHUNK)
        mask_t = _tri(SSD_CHUNK, upper=True)
        tril_f = mask.astype(BF16)
        triu_f = mask_t.astype(BF16)
        eye = (lax.broadcasted_iota(jnp.int32, (N_STATE, N_STATE), 0) ==
               lax.broadcasted_iota(jnp.int32, (N_STATE, N_STATE), 1)).astype(BF16)
        alog, bias = alog_ref[0], bias_ref[0]
        d_e = _sel(_nn, jnp.broadcast_to(dsk_ref[0], (8, N_STATE)), expand, 3)[0:1, :]
        wn = wn_ref[...]

        def chunk(i, c0):
            c = ncb - 1 - i
            sl = pl.ds(pl.multiple_of(c * SSD_CHUNK, SSD_CHUNK), SSD_CHUNK)
            xs, bm, cm, dt, z = xs_ref[sl, :], b_ref[sl, :], c_ref[sl, :], dt_ref[sl, :], z_ref[sl, :]
            dtb, delta, ea, a, acum, delta_e, acum_e, acum_t = _ssd_chunk_terms(dt, bias, alog, expand, tril_f, eye)
            alast_e = acum_e[SSD_CHUNK - 1:SSD_CHUNK, :]
            eacum = jnp.exp(acum_e)
            wl = jnp.exp(alast_e - acum_e)
            xd = xs * delta_e
            xdb = xd.astype(BF16)
            cb_, bb_ = cm.astype(BF16), bm.astype(BF16)
            cbm = _nt(cb_, bb_)
            st32 = st_ref[0, c]
            stb = st32.astype(BF16)
            dst = dstate[...]
            dstb = dst.astype(BF16)
            lhs, mixes, ys = [], [], []
            for hh in range(HEADS_PER_GROUP):
                col, row = acum[:, hh:hh + 1], acum_t[hh:hh + 1, :]
                lh = jnp.where(mask, jnp.exp(jnp.minimum(col - row, 0.0)), 0.0)
                mix = (cbm * lh).astype(BF16)
                lhs.append(lh)
                mixes.append(mix)
                ys.append(_nn(mix, xdb[:, hh * HEAD_P:(hh + 1) * HEAD_P]))
            y_in = jnp.concatenate(ys, axis=1)
            y_out = _nn(cb_, stb) * eacum
            y = y_in + y_out + xs * d_e
            sgz = _sigmoid(z)
            sz = z * sgz
            yg = y * sz
            rstd = lax.rsqrt(jnp.mean(yg * yg, axis=-1, keepdims=True) + RMS_EPS)
            nrm = yg * rstd
            dob_v = dob_ref[sl, :]
            dn = dob_v * wn
            dwn_ref[...] += _colsum(dob_v * nrm)
            dyg = rstd * (dn - nrm * jnp.mean(dn * nrm, axis=-1, keepdims=True))
            dy = dyg * sz
            dz_ref[sl, :] = (dyg * y * _dsilu(z, sgz)).astype(BF16)
            dyb = dy.astype(BF16)
            dxds = []
            dcb = jnp.zeros((SSD_CHUNK, SSD_CHUNK), F32)
            for hh in range(HEADS_PER_GROUP):
                hs = slice(hh * HEAD_P, (hh + 1) * HEAD_P)
                dy_h, x_h = dyb[:, hs], xdb[:, hs]
                dxds.append(_tn(mixes[hh], dy_h))
                dcb = dcb + _nt(dy_h, x_h) * lhs[hh]
            dcbb = dcb.astype(BF16)
            dye = (dy * eacum).astype(BF16)
            xw = (xd * wl).astype(BF16)
            dxd_in = jnp.concatenate(dxds, axis=1)
            dxd_out = wl * _nn(bb_, dstb)
            dxd = dxd_in + dxd_out
            dc_ref[sl, :] = (_nn(dcbb, bb_) + _nt(dye, stb)).astype(dc_ref.dtype)
            db_ref[sl, :] = (_tn(dcbb, cb_) + _nt(xw, dstb)).astype(db_ref.dtype)
            dstate[...] = dst * jnp.exp(alast_e) + _tn(cb_, dye)
            col_out = xd * dxd_out
            dac = _sel(_nt, dyb.astype(F32) * y_in - xdb.astype(F32) * dxd_in + dy * y_out - col_out, expand, 3)
            beyond = _colsum(col_out) + jnp.exp(alast_e) * _colsum(dst * st32)
            da = (_sel(_nn, dac, triu_f, 3, x_first=False) +
                  _sel(_nt, jnp.broadcast_to(beyond, (8, GROUP_W)), expand, 3)[0:1, :])
            ddelta = _sel(_nt, dxd * xs, expand, 2) - da * ea
            dalog_ref[0] += _colsum(da * a)
            ddtb = ddelta * _sigmoid(dtb)
            dbias_ref[0] += _colsum(ddtb)
            ddt_ref[sl, :] = ddtb.astype(BF16)
            ddsk_ref[0] += _sel(_nt, jnp.broadcast_to(_colsum(dy * xs), (8, GROUP_W)), expand, 3)[0:1, :]
            dxs_ref[sl, :] = (dxd * delta_e + dy * d_e).astype(dxs_ref.dtype)
            return c0

        lax.fori_loop(0, ncb, chunk, 0, unroll=min(CHUNK_UNROLL, ncb))

        @pl.when((pl.program_id(0) == N_GROUPS - 1) & (pl.program_id(1) == nb - 1))
        def _():
            xchg_wait()

    small = pl.BlockSpec((1, 1, N_STATE), lambda g, j: (g, 0, 0))
    wide = pl.BlockSpec((tb, GROUP_W), lambda g, j: (nb - 1 - j, g))
    narrow = pl.BlockSpec((tb, N_STATE), lambda g, j: (nb - 1 - j, g))
    hbm = pl.BlockSpec(memory_space=pl.ANY)
    return pl.pallas_call(
        body, name="ssd_bwd", grid=(N_GROUPS, nb),
        out_shape=[jax.ShapeDtypeStruct((t, B_INNER), BF16), jax.ShapeDtypeStruct((t, GROUP_W), BF16),
                   jax.ShapeDtypeStruct((t, GROUP_W), BF16), jax.ShapeDtypeStruct(dproj.shape, dproj.dtype),
                   jax.ShapeDtypeStruct((t, GROUP_W), BF16), jax.ShapeDtypeStruct((1, B_INNER), F32),
                   jax.ShapeDtypeStruct((N_GROUPS, 1, N_STATE), F32), jax.ShapeDtypeStruct((N_GROUPS, 1, N_STATE), F32),
                   jax.ShapeDtypeStruct((N_GROUPS, 1, N_STATE), F32), jax.ShapeDtypeStruct(part.shape, part.dtype)],
        in_specs=[wide,
                  pl.BlockSpec((tb, N_STATE), lambda g, j: (nb - 1 - j, B_BLOCK0 + g)),
                  pl.BlockSpec((tb, N_STATE), lambda g, j: (nb - 1 - j, C_BLOCK0 + g)),
                  pl.BlockSpec((tb, N_STATE), lambda g, j: (nb - 1 - j, DT_BLOCK0 + g)),
                  pl.BlockSpec((tb, GROUP_W), lambda g, j: (nb - 1 - j, Z_BLOCK0 + g)),
                  small, small, small, pl.BlockSpec((1, GROUP_W), lambda g, j: (0, g)),
                  pl.BlockSpec((N_STATE, GROUP_W), lambda g, j: (0, 0)), wide,
                  pl.BlockSpec((1, ncb, N_STATE, GROUP_W), lambda g, j: (g, nb - 1 - j, 0, 0)), hbm, hbm],
        out_specs=[wide, narrow, narrow,
                   pl.BlockSpec((None, tb, GROUP_W), lambda g, j: (Z_BLOCK0 // 2 + g // 2, nb - 1 - j, g % 2)),
                   narrow, pl.BlockSpec((1, GROUP_W), lambda g, j: (0, g)), small, small, small, hbm],
        input_output_aliases={13: 3},
        scratch_shapes=[pltpu.VMEM((N_STATE, GROUP_W), F32)] + CHIP_SEMS,
        compiler_params=_cparams(("arbitrary", "arbitrary")),
    )(xc, xc, xc, proj, proj, alog4, bias4, dskip4, wnorm, expand, dob, states, part, dproj)


def lower_bound_fwd(hgrn_lb):
    def body(a_ref, o_ref):
        a0, a1 = a_ref[0:1, :], a_ref[1:2, :]
        m = jnp.maximum(a0, a1)
        e0, e1 = jnp.exp(a0 - m), jnp.exp(a1 - m)
        o_ref[...] = e0 / (e0 + e1)

    return pl.pallas_call(body, name="lower_bound_fwd", out_shape=jax.ShapeDtypeStruct((1, D), F32))(hgrn_lb)


def ada_weight_grad(c_all, dmod_cols):
    def body(c_ref, d_ref, o_ref):
        cval = c_ref[...]
        o_ref[...] = _tn(cval * _sigmoid(cval), d_ref[...], HI)

    return pl.pallas_call(body, name="ada_weight_grad",
                          out_shape=jax.ShapeDtypeStruct((D, dmod_cols.shape[1]), F32))(c_all, dmod_cols)


def reduce_small(gathered, hgrn_lb, dlb_off):
    n = gathered.shape[2]

    def body(g_ref, a_ref, o_ref, glb_ref):
        s = g_ref[0]
        for d in range(1, N_DEV):
            s = s + g_ref[d]
        o_ref[...] = s
        a0, a1 = a_ref[0:1, :], a_ref[1:2, :]
        m = jnp.maximum(a0, a1)
        e0, e1 = jnp.exp(a0 - m), jnp.exp(a1 - m)
        p0 = e0 / (e0 + e1)
        tq = s[:, dlb_off:dlb_off + D] * p0 * (1.0 - p0)
        glb_ref[0:1, :] = tq
        glb_ref[1:2, :] = -tq

    return pl.pallas_call(body, name="reduce_small",
                          out_shape=[jax.ShapeDtypeStruct((1, n), F32), jax.ShapeDtypeStruct((2, D), F32)])(gathered, hgrn_lb)


def _adam_math(w, g, m, v):
    m2 = ADAM_B1 * m + (1.0 - ADAM_B1) * g
    v2 = ADAM_B2 * v + (1.0 - ADAM_B2) * (g * g)
    m_hat = m2 / (1.0 - ADAM_B1 ** ADAM_STEP)
    v_hat = v2 / (1.0 - ADAM_B2 ** ADAM_STEP)
    delta = -ADAM_LR * (m_hat / (jnp.sqrt(v_hat) + ADAM_EPS) + ADAM_WD * w)
    return delta, m2, v2


def _row_tile(rows, mult=8, cap=128):
    for cand in range(cap - cap % mult, 0, -mult):
        if rows % cand == 0:
            return cand
    return rows


def sum_parts(parts, name):
    n, rows, cols = parts.shape
    tr = _row_tile(rows, 16, 256)

    def body(p_ref, o_ref):
        s = p_ref[0].astype(F32)
        for d in range(1, n):
            s = s + p_ref[d].astype(F32)
        o_ref[...] = s

    return pl.pallas_call(
        body, name=name, grid=(rows // tr,),
        out_shape=jax.ShapeDtypeStruct((rows, cols), F32),
        in_specs=[pl.BlockSpec((n, tr, cols), lambda i: (0, i, 0))],
        out_specs=pl.BlockSpec((tr, cols), lambda i: (i, 0)),
        compiler_params=_cparams(("parallel",)),
    )(parts)


def sum_pair(a, b, name):
    rows, cols = a.shape
    tr = _row_tile(rows, 16, 256)

    def body(a_ref, b_ref, o_ref):
        o_ref[...] = (a_ref[...].astype(F32) + b_ref[...].astype(F32)).astype(o_ref.dtype)

    blk = pl.BlockSpec((tr, cols), lambda i: (i, 0))
    return pl.pallas_call(
        body, name=name, grid=(rows // tr,),
        out_shape=jax.ShapeDtypeStruct((rows, cols), a.dtype),
        in_specs=[blk, blk], out_specs=blk,
        compiler_params=_cparams(("parallel",)),
    )(a, b)


def adamw(w, g, m, v, name):
    rows, cols = w.shape
    tr = _row_tile(rows)

    def body(w_ref, g_ref, m_ref, v_ref, d_ref, m2_ref, v2_ref):
        delta, m2, v2 = _adam_math(w_ref[...], g_ref[...], m_ref[...], v_ref[...])
        d_ref[...] = delta
        m2_ref[...] = m2
        v2_ref[...] = v2

    blk = pl.BlockSpec((tr, cols), lambda i: (i, 0))
    return pl.pallas_call(
        body, name=name, grid=(rows // tr,),
        out_shape=[jax.ShapeDtypeStruct((rows, cols), F32)] * 3,
        in_specs=[blk] * 4, out_specs=[blk] * 3,
        compiler_params=_cparams(("parallel",)),
    )(w, g, m, v)


def _pad128(n):
    return -(-n // 128) * 128


def _pack(arrays):
    offs, parts, off = [], [], 0
    for a in arrays:
        flat = a.reshape(1, -1)
        n = flat.shape[1]
        offs.append(off)
        parts.append(jnp.pad(flat, ((0, 0), (0, _pad128(n) - n))))
        off += _pad128(n)
    return jnp.concatenate(parts, axis=1), offs


def _unpack(vec, offs, shapes):
    out = []
    for off, shp in zip(offs, shapes):
        n = int(np.prod(shp))
        out.append(vec[0, off:off + n].reshape(shp))
    return out


IN_ROWS = IN_DIM // N_DEV
DT_ROW0 = 9216
DT_DEV, DT_LO = divmod(DT_ROW0, IN_ROWS)


GATE_SHIFT = D - 32


def _in_row_pieces(tile):
    pieces = []
    if tile == DT_COL_BLOCK:
        for g in range(N_GROUPS):
            o = DT_ROW0 + HEADS_PER_GROUP * g
            pieces.append((N_STATE * g, o // IN_ROWS, o % IN_ROWS, HEADS_PER_GROUP))
        return pieces
    r, end = tile * D, (tile + 1) * D
    while r < end:
        o = r if r < DT_ROW0 else r - GATE_SHIFT
        dev, loc = divmod(o, IN_ROWS)
        n = min(end - r, IN_ROWS - loc)
        pieces.append((r - tile * D, dev, loc, n))
        r += n
    return pieces


def assemble_w_in(g_all):
    ntile = N_PROJ // D

    def body(g_ref, o_ref):
        j = pl.program_id(0)
        for tile in range(ntile):
            @pl.when(j == tile)
            def _(tile=tile):
                if tile == DT_COL_BLOCK:
                    o_ref[...] = jnp.zeros_like(o_ref)
                for dst, dev, loc, n in _in_row_pieces(tile):
                    o_ref[pl.ds(dst, n), :] = g_ref[dev, pl.ds(loc, n), :]

    return pl.pallas_call(
        body, name="assemble_w_in", grid=(ntile,),
        out_shape=jax.ShapeDtypeStruct((N_PROJ, D), g_all.dtype),
        in_specs=[pl.BlockSpec(memory_space=pltpu.VMEM)],
        out_specs=pl.BlockSpec((D, D), lambda j: (j, 0)),
        compiler_params=_cparams(("arbitrary",)),
    )(g_all)


def _grad_in_blocks(g_t, core, slot):
    dt0 = DT_COL_BLOCK * D
    dt = g_t[dt0:dt0 + N_GROUPS * N_STATE].reshape(N_GROUPS, N_STATE, D)[:, :HEADS_PER_GROUP].reshape(32, D)
    with_dt = jnp.concatenate([g_t[DT_DEV * IN_ROWS:DT_ROW0], dt,
                               g_t[DT_ROW0 + 32 + GATE_SHIFT:(DT_DEV + 1) * IN_ROWS + GATE_SHIFT]], axis=0)
    blocks = []
    for q in range(N_CHIP):
        if 2 * q + 1 < DT_DEV:
            blk = lax.dynamic_slice_in_dim(g_t, IN_ROWS * (2 * q + core), IN_ROWS, axis=0)
        else:
            assert 2 * q == DT_DEV
            after = g_t[(DT_DEV + 1) * IN_ROWS + GATE_SHIFT:(DT_DEV + 2) * IN_ROWS + GATE_SHIFT]
            blk = jnp.where(core == 0, with_dt, after)
        blocks.append(jnp.pad(blk, ((0, slot - IN_ROWS), (0, 0))))
    return jnp.stack(blocks)


def kernel(x, c, w_ada, b_ada, w_in, hgrn_lb, hgrn_gnorm, ssm_conv_w, ssm_conv_b, ssm_dt_bias, ssm_a_log, ssm_d, ssm_norm, w_branch_a, w_branch_b, w_o, ln1_g, ln1_b, w_ffn_gate, w_ffn_up, w_ffn_down, ln2_g, ln2_b, loss_target, m_w_ada, m_b_ada, m_w_in, m_hgrn_lb, m_hgrn_gnorm, m_ssm_conv_w, m_ssm_conv_b, m_ssm_dt_bias, m_ssm_a_log, m_ssm_d, m_ssm_norm, m_w_branch_a, m_w_branch_b, m_w_o, m_ln1_g, m_ln1_b, m_w_ffn_gate, m_w_ffn_up, m_w_ffn_down, m_ln2_g, m_ln2_b, v_w_ada, v_b_ada, v_w_in, v_hgrn_lb, v_hgrn_gnorm, v_ssm_conv_w, v_ssm_conv_b, v_ssm_dt_bias, v_ssm_a_log, v_ssm_d, v_ssm_norm, v_w_branch_a, v_w_branch_b, v_w_o, v_ln1_g, v_ln1_b, v_w_ffn_gate, v_w_ffn_up, v_w_ffn_down, v_ln2_g, v_ln2_b):
    me = 4 * lax.axis_index("x") + 2 * lax.axis_index("y") + lax.axis_index("c")
    xt = x[0]
    tgt = loss_target[0]
    t = xt.shape[0]
    ada_cols = w_ada.shape[2]
    conv_cols = ssm_conv_w.shape[2]

    small_in, _ = _pack([c, ssm_conv_w[0]])
    small_all = allgather_vmem(small_in, "allgather_small_inputs")
    c_all = small_all[:, 0, :D]
    conv_w = small_all[:, 0, D:D + CONV_TAPS * conv_cols].reshape(N_DEV, CONV_TAPS, conv_cols)
    conv_w = conv_w.transpose(1, 0, 2).reshape(CONV_TAPS, CONV_DIM)
    mod = ada_modulation(c_all, w_ada[0], b_ada.reshape(N_DEV, 1, ada_cols))
    mod6 = mod.reshape(6, D)

    shards = [w_in[0].T, w_branch_a[0], w_branch_b[0], w_o[0], w_ffn_gate[0].T, w_ffn_up[0].T, w_ffn_down[0]]
    shard_rows = [s.shape[0] for s in shards]
    slot_rows = [-(-r // 32) * 32 for r in shard_rows]
    row_offs = [sum(slot_rows[:i]) for i in range(len(shards))]
    padded = [jnp.pad(s.astype(BF16), ((0, p - r), (0, 0))) for s, r, p in zip(shards, shard_rows, slot_rows)]
    w_in_t = assemble_w_in(allgather_hbm(padded[0], "allgather_w_in"))

    lb = lower_bound_fwd(hgrn_lb)
    u1 = ln_modulate(xt, mod6, 0, 1, "ln_modulate_1")
    proj, g_rest = mm_nt_gather(u1, w_in_t, F32, jnp.concatenate(padded[1:], axis=0), "mm_in_proj")
    g_ba, g_bb, g_o, g_fg, g_fu, g_fd = (g_rest[:, o - slot_rows[0]:o - slot_rows[0] + r]
                                         for o, r in zip(row_offs[1:], shard_rows[1:]))
    w_ba = g_ba.reshape(D, D)
    w_bb = g_bb.reshape(B_INNER, D)
    w_oo = g_o.reshape(D, D)
    ffpad = ((0, D_FF_PAD - D_FF), (0, 0))
    w_gu_t = jnp.concatenate([jnp.pad(g_fg.reshape(D_FF, D), ffpad), jnp.pad(g_fu.reshape(D_FF, D), ffpad)], axis=0)
    w_dn = jnp.pad(g_fd.reshape(D_FF, D), ffpad)
    o_a, o_raw, st_a = hgrn_fwd(proj, lb, hgrn_gnorm)
    xc = conv_fwd(proj, conv_w, ssm_conv_b)
    pad3 = ((0, 0), (0, 0), (0, N_STATE - HEADS_PER_GROUP))
    alog4 = jnp.pad(ssm_a_log.reshape(N_GROUPS, 1, HEADS_PER_GROUP), pad3)
    bias4 = jnp.pad(ssm_dt_bias.reshape(N_GROUPS, 1, HEADS_PER_GROUP), pad3)
    dskip4 = jnp.pad(ssm_d.reshape(N_GROUPS, 1, HEADS_PER_GROUP), pad3)
    expand = _head_expand()
    o_b, st_b = ssd_fwd(proj, xc, alog4, bias4, dskip4, ssm_norm, expand)
    ya = mm_nn(o_a, w_ba, BF16, "mm_branch_a")
    yb = mm_nn(o_b, w_bb, BF16, "mm_branch_b")
    merged = merge_gates(ya, yb, proj)
    h1 = mm_nn(merged, w_oo, F32, "mm_out_proj")
    x1 = resid_ln(xt, h1, mod6, 2, ln1_g, ln1_b, "resid_ln_1")
    u2 = ln_modulate(x1, mod6, 3, 4, "ln_modulate_2")
    gu = mm_nt(u2, w_gu_t, BF16, "mm_ffn_in")
    act = swiglu_act(gu)
    h2 = mm_nn(act, w_dn, F32, "mm_ffn_out")

    dh2, dx1_part, acc4 = resid_ln_bwd(x1, h2, mod6, 5, ln2_g, ln2_b, tgt, True, "resid_ln_2_bwd")
    g_dn = mm_tn(act, dh2, "mm_grad_ffn_down")
    dact = mm_nt(dh2, w_dn, BF16, "mm_dact")
    dgu = swiglu_act_bwd(gu, dact)
    g_gu_t = mm_tn(dgu, u2, "mm_grad_ffn_in")
    du2 = mm_nn(dgu, w_gu_t, F32, "mm_du2")
    dx1, acc3 = ln_modulate_bwd(x1, du2, mod6, 4, dx1_part, "ln_modulate_2_bwd")
    dh1, dx_part, acc2 = resid_ln_bwd(xt, h1, mod6, 2, ln1_g, ln1_b, dx1, False, "resid_ln_1_bwd")
    g_o = mm_tn(merged, dh1, "mm_grad_out_proj")
    dmerged = mm_nt(dh1, w_oo, BF16, "mm_dmerged")
    dya, dyb, dproj = merge_gates_bwd(dmerged, ya, yb, proj)
    g_ba_full = mm_tn(o_a, dya, "mm_grad_branch_a")
    g_bb_full = mm_tn(o_b, dyb, "mm_grad_branch_b")
    doa = mm_nt(dya, w_ba, F32, "mm_doa")
    dob = mm_nt(dyb, w_bb, F32, "mm_dob")
    my_core = lax.axis_index("c")

    def by_core(blocks, rows, slots):
        contrib = jnp.concatenate([jnp.pad(b.reshape(N_DEV, -1, D), ((0, 0), (0, p - r), (0, 0)))
                                   for b, r, p in zip(blocks, rows, slots)], axis=1)
        split = contrib.reshape(N_CHIP, 2, contrib.shape[1], D).transpose(1, 0, 2, 3)
        return (lax.dynamic_index_in_dim(split, my_core, 0, keepdims=False),
                lax.dynamic_index_in_dim(split, 1 - my_core, 0, keepdims=False))

    keep_e, give_e = by_core([g_ba_full, g_bb_full, g_o, g_gu_t[:D_FF], g_gu_t[D_FF_PAD:D_FF_PAD + D_FF], g_dn[:D_FF]],
                             shard_rows[1:], slot_rows[1:])
    dproj, dlb, dgn, got_e = hgrn_bwd(proj, lb, hgrn_gnorm, o_raw, doa, st_a, give_e, dproj)
    chip_e = sum_pair(keep_e.reshape(-1, D), got_e.reshape(-1, D), "sum_grads_rest_chip").reshape(keep_e.shape)
    dxs, dbm, dcm, dproj, ddt, dwn, dalog, dbias, ddsk, parts_e = ssd_bwd(proj, xc, alog4, bias4, dskip4, ssm_norm,
                                                                          expand, dob, st_b, chip_e, dproj)
    dxc = jnp.concatenate([dxs, dbm, dcm], axis=1)
    dproj, dcw, dcb = conv_bwd(proj, dxc, conv_w, ssm_conv_b, dproj)
    dproj = dt_fill(ddt, dproj)
    g_in_t = mm_tn(dproj, u1, "mm_grad_in_proj")
    keep_l = _grad_in_blocks(g_in_t, my_core, slot_rows[0])
    give_l = _grad_in_blocks(g_in_t, 1 - my_core, slot_rows[0])
    got_l = exchange_sibling(give_l, "exchange_grad_in_sibling")
    chip_l = sum_pair(keep_l.reshape(-1, D), got_l.reshape(-1, D), "sum_grad_in_chip").reshape(keep_l.shape)
    du1, parts_l = mm_nn_exchange(dproj, w_in_t, F32, chip_l, "mm_du1")
    dx, acc1 = ln_modulate_bwd(xt, du1, mod6, 1, dx_part, "ln_modulate_1_bwd")
    gw_in = sum_parts(parts_l, "sum_grad_in")[:shard_rows[0]].T
    g_rows = sum_parts(parts_e, "sum_grads_rest")
    gw_ba, gw_bb, gw_o, gw_fg, gw_fu, gw_fd = (g_rows[o - slot_rows[0]:o - slot_rows[0] + r]
                                               for o, r in zip(row_offs[1:], shard_rows[1:]))
    gw_fg, gw_fu = gw_fg.T, gw_fu.T

    dmod = jnp.concatenate([acc1[1:2], acc1[0:1], acc2[0:1], acc3[1:2], acc3[0:1], acc4[0:1]], axis=1)
    small_fields = [dmod, acc4[3:4, :128], dlb, dgn, dcw[:CONV_TAPS], dcb, dbias, dalog, ddsk, dwn,
                    acc2[1:2], acc2[2:3], acc4[1:2], acc4[2:3]]
    small_out, offs = _pack(small_fields)
    small_sum_in = allgather_vmem(small_out, "allgather_small_grads")
    gsum, g_lb = reduce_small(small_sum_in, hgrn_lb, offs[2])
    (g_bada, loss_row, _, g_gn, g_cw_full, g_cb, g_bias4, g_alog4, g_dsk4, g_wn, g_l1g, g_l1b, g_l2g, g_l2b) = _unpack(
        gsum, offs, [(1, 6 * D), (1, 128), (1, D), (1, HK), (CONV_TAPS, CONV_DIM), (1, CONV_DIM),
                     (N_GROUPS, N_STATE), (N_GROUPS, N_STATE), (N_GROUPS, N_STATE), (1, B_INNER),
                     (1, D), (1, D), (1, D), (1, D)])
    loss = loss_row[0, 0]
    g_cw = lax.dynamic_slice(g_cw_full, (0, me * conv_cols), (CONV_TAPS, conv_cols))[None]
    g_dtb = g_bias4[:, :HEADS_PER_GROUP].reshape(1, 32)
    g_alog = g_alog4[:, :HEADS_PER_GROUP].reshape(1, 32)
    g_dsk = g_dsk4[:, :HEADS_PER_GROUP].reshape(1, 32)

    dmod_all = small_sum_in[:, 0, offs[0]:offs[0] + 6 * D]
    dmod_cols = lax.dynamic_slice(dmod_all, (0, me * ada_cols), (N_DEV, ada_cols))
    gw_ada = ada_weight_grad(c_all, dmod_cols)

    big = [("ada", w_ada[0], gw_ada, m_w_ada[0], v_w_ada[0]), ("in", w_in[0], gw_in, m_w_in[0], v_w_in[0]),
           ("branch_a", w_branch_a[0], gw_ba, m_w_branch_a[0], v_w_branch_a[0]),
           ("branch_b", w_branch_b[0], gw_bb, m_w_branch_b[0], v_w_branch_b[0]),
           ("o", w_o[0], gw_o, m_w_o[0], v_w_o[0]),
           ("ffn_gate", w_ffn_gate[0], gw_fg, m_w_ffn_gate[0], v_w_ffn_gate[0]),
           ("ffn_up", w_ffn_up[0], gw_fu, m_w_ffn_up[0], v_w_ffn_up[0]),
           ("ffn_down", w_ffn_down[0], gw_fd, m_w_ffn_down[0], v_w_ffn_down[0])]
    big_out = {}
    for nm, w_, g_, m_, v_ in big:
        d_, m2_, v2_ = adamw(w_, g_, m_, v_, "adamw_" + nm)
        big_out[nm] = (g_[None], d_[None], m2_[None], v2_[None])

    small_w = [b_ada, hgrn_lb, hgrn_gnorm, ssm_conv_w, ssm_conv_b, ssm_dt_bias, ssm_a_log, ssm_d, ssm_norm,
               ln1_g, ln1_b, ln2_g, ln2_b]
    small_g = [g_bada, g_lb, g_gn, g_cw, g_cb, g_dtb, g_alog, g_dsk, g_wn, g_l1g, g_l1b, g_l2g, g_l2b]
    small_m = [m_b_ada, m_hgrn_lb, m_hgrn_gnorm, m_ssm_conv_w, m_ssm_conv_b, m_ssm_dt_bias, m_ssm_a_log, m_ssm_d,
               m_ssm_norm, m_ln1_g, m_ln1_b, m_ln2_g, m_ln2_b]
    small_v = [v_b_ada, v_hgrn_lb, v_hgrn_gnorm, v_ssm_conv_w, v_ssm_conv_b, v_ssm_dt_bias, v_ssm_a_log, v_ssm_d,
               v_ssm_norm, v_ln1_g, v_ln1_b, v_ln2_g, v_ln2_b]
    shapes = [a.shape for a in small_w]
    small_g = [g_.reshape(s) for g_, s in zip(small_g, shapes)]
    pw, poffs = _pack(small_w)
    pg, _ = _pack(small_g)
    pm, _ = _pack(small_m)
    pv, _ = _pack(small_v)
    pd, pm2, pv2 = adamw(pw, pg, pm, pv, "adamw_small")
    s_d, s_m, s_v = (_unpack(p, poffs, shapes) for p in (pd, pm2, pv2))
    (sn_bada, sn_lb, sn_gn, sn_cw, sn_cb, sn_dtb, sn_alog, sn_dsk, sn_wn, sn_l1g, sn_l1b, sn_l2g, sn_l2b) = range(13)

    def order(kind):
        sm = [small_g, s_d, s_m, s_v][kind]
        bg = lambda nm: big_out[nm][kind]
        return [bg("ada"), sm[sn_bada], bg("in"), sm[sn_lb], sm[sn_gn], sm[sn_cw], sm[sn_cb], sm[sn_dtb], sm[sn_alog],
                sm[sn_dsk], sm[sn_wn], bg("branch_a"), bg("branch_b"), bg("o"), sm[sn_l1g], sm[sn_l1b],
                bg("ffn_gate"), bg("ffn_up"), bg("ffn_down"), sm[sn_l2g], sm[sn_l2b]]

    return (loss, dx[None], *order(0), *order(1), *order(2), *order(3))
```

```python
import numpy as np
import jax
import jax.numpy as jnp
from jax import lax
from jax.experimental import pallas as pl
from jax.experimental.pallas import tpu as pltpu

F32 = jnp.float32
BF16 = jnp.bfloat16
HI = lax.Precision.HIGHEST

N_DEV = 8
D = 1024
N_HEADS_A = 8
HK = 128
CHUNK = 64
SSD_CHUNK = 128
N_GROUPS = 4
HEADS_PER_GROUP = 8
HEAD_P = 64
N_STATE = 128
GROUP_W = HEADS_PER_GROUP * HEAD_P
B_INNER = 2048
CONV_DIM = 3072
D_FF = 2816
D_FF_PAD = 3072
IN_DIM = 11296
N_PROJ = 12288
ALPHA = 2.0 ** 0.25
LN_EPS = 1e-5
RMS_EPS = 1e-6
Q_SCALE = 128 ** -0.5
EXP_CLIP = 80.0
ADAM_LR, ADAM_B1, ADAM_B2, ADAM_EPS, ADAM_WD, ADAM_STEP = 0.001, 0.9, 0.999, 1e-8, 0.01, 10
VMEM_LIMIT = 48 * 1024 * 1024
TOKEN_BLOCK = 512
ROW_TILE = 256
FFN_ROW_TILE = 128
MM_ROW_TILE = 1024
MM_TOKEN_TILE = 4096
MM_K_TILE = 3072
HGRN_HEADS_PER_STEP = 4
CHUNK_UNROLL = 8
MESH_ID = pl.DeviceIdType.MESH

NT_DIMS = (((1,), (1,)), ((), ()))
TN_DIMS = (((0,), (0,)), ((), ()))


def _cparams(sem=None):
    return pltpu.CompilerParams(dimension_semantics=sem, vmem_limit_bytes=VMEM_LIMIT)


def _sigmoid(x):
    return 1.0 / (1.0 + jnp.exp(-x))


def _dsilu(x, s):
    return s * (1.0 + x * (1.0 - s))


def _nt(a, b, precision=None):
    return lax.dot_general(a, b, NT_DIMS, precision=precision, preferred_element_type=F32)


def _tn(a, b, precision=None):
    return lax.dot_general(a, b, TN_DIMS, precision=precision, preferred_element_type=F32)


def _nn(a, b, precision=None):
    return jnp.dot(a, b, precision=precision, preferred_element_type=F32)


def _split(x, pieces):
    out = []
    for i in range(pieces):
        p = x.astype(BF16)
        out.append(p)
        if i + 1 < pieces:
            x = x - p.astype(F32)
    return out


def _sel(dot, x, sel01, pieces, x_first=True):
    acc = None
    for p in _split(x, pieces):
        term = dot(p, sel01) if x_first else dot(sel01, p)
        acc = term if acc is None else acc + term
    return acc


def _ln(x):
    mu = jnp.mean(x, axis=-1, keepdims=True)
    xc = x - mu
    rstd = lax.rsqrt(jnp.mean(xc * xc, axis=-1, keepdims=True) + LN_EPS)
    return xc * rstd, rstd


def _ln_bwd(dxh, xh, rstd):
    return rstd * (dxh - jnp.mean(dxh, axis=-1, keepdims=True) - xh * jnp.mean(dxh * xh, axis=-1, keepdims=True))


def _colsum(x):
    return jnp.sum(x, axis=0, keepdims=True)


def _tri(n, upper=False):
    r = lax.broadcasted_iota(jnp.int32, (n, n), 0)
    c = lax.broadcasted_iota(jnp.int32, (n, n), 1)
    return (c >= r) if upper else (r >= c)


def _my_pos():
    return lax.axis_index("x"), lax.axis_index("y"), lax.axis_index("c")


def _peer(pos, k):
    x, y, c = pos
    return (x ^ ((k >> 2) & 1), y ^ ((k >> 1) & 1), c ^ (k & 1))


def _flat(pos):
    return 4 * pos[0] + 2 * pos[1] + pos[2]


def allgather_vmem(v, name):
    n = v.shape[1]

    def body(v_ref, o_ref, send_sems, recv_sems, local_sem):
        me = _my_pos()
        mine = pltpu.make_async_copy(v_ref, o_ref.at[_flat(me)], local_sem)
        mine.start()
        sends = []
        for k in range(1, N_DEV):
            peer = _peer(me, k)
            cp = pltpu.make_async_remote_copy(v_ref, o_ref.at[_flat(me)], send_sems.at[k - 1], recv_sems.at[k - 1],
                                              device_id=peer, device_id_type=MESH_ID)
            cp.start()
            sends.append(cp)
        for k in range(1, N_DEV):
            peer = _peer(me, k)
            pltpu.make_async_remote_copy(v_ref, o_ref.at[_flat(peer)], send_sems.at[k - 1], recv_sems.at[k - 1],
                                         device_id=peer, device_id_type=MESH_ID).wait_recv()
        for cp in sends:
            cp.wait_send()
        mine.wait()

    return pl.pallas_call(
        body, name=name,
        out_shape=jax.ShapeDtypeStruct((N_DEV, 1, n), F32),
        in_specs=[pl.BlockSpec(memory_space=pltpu.VMEM)],
        out_specs=pl.BlockSpec(memory_space=pltpu.VMEM),
        scratch_shapes=[pltpu.SemaphoreType.DMA((N_DEV - 1,)), pltpu.SemaphoreType.DMA((N_DEV - 1,)),
                        pltpu.SemaphoreType.DMA],
        compiler_params=_cparams(),
    )(v)


def ada_modulation(c_all, w_ada_s, b_ada_r):
    ncol = w_ada_s.shape[1]

    def body(c_ref, w_ref, b_ref, o_ref, part_ref, send_sems, recv_sems):
        me = _my_pos()
        cval = c_ref[...]
        cond = cval * _sigmoid(cval)
        part = _nn(cond, w_ref[...], HI)
        for r in range(N_DEV):
            part_ref[r] = part[r:r + 1, :]
        sends = []
        for k in range(1, N_DEV):
            peer = _peer(me, k)
            cp = pltpu.make_async_remote_copy(part_ref.at[_flat(peer)], o_ref.at[_flat(me)], send_sems.at[k - 1],
                                              recv_sems.at[k - 1], device_id=peer, device_id_type=MESH_ID)
            cp.start()
            sends.append(cp)
        o_ref[_flat(me)] = part_ref[_flat(me)]
        for k in range(1, N_DEV):
            peer = _peer(me, k)
            pltpu.make_async_remote_copy(part_ref.at[_flat(peer)], o_ref.at[_flat(peer)], send_sems.at[k - 1],
                                         recv_sems.at[k - 1], device_id=peer, device_id_type=MESH_ID).wait_recv()
        for cp in sends:
            cp.wait_send()
        o_ref[...] = o_ref[...] + b_ref[...]

    return pl.pallas_call(
        body, name="ada_modulation",
        out_shape=jax.ShapeDtypeStruct((N_DEV, 1, ncol), F32),
        in_specs=[pl.BlockSpec(memory_space=pltpu.VMEM)] * 3,
        out_specs=pl.BlockSpec(memory_space=pltpu.VMEM),
        scratch_shapes=[pltpu.VMEM((N_DEV, 1, ncol), F32), pltpu.SemaphoreType.DMA((N_DEV - 1,)),
                        pltpu.SemaphoreType.DMA((N_DEV - 1,))],
        compiler_params=_cparams(),
    )(c_all, w_ada_s, b_ada_r)


def allgather_hbm(shard, name):
    def body(x_ref, out_ref, send_sems, recv_sems, local_sem):
        x, y, c = _my_pos()
        me, sibling = (x, y, c), (x, y, 1 - c)
        chips = [(1 - x, y), (x, 1 - y), (1 - x, 1 - y)]

        def slot(pos):
            return out_ref.at[_flat(pos)]

        def copy(k, block, to, src=None):
            return pltpu.make_async_remote_copy(slot(block) if src is None else src, slot(block), send_sems.at[k],
                                                recv_sems.at[k], device_id=to, device_id_type=MESH_ID)

        mine = pltpu.make_async_copy(x_ref, slot(me), local_sem)
        mine.start()
        first = [copy(0, me, sibling, src=x_ref)]
        first += [copy(1 + j, me, (*chip, c), src=x_ref) for j, chip in enumerate(chips)]
        for cp in first:
            cp.start()
        passed = [copy(4 + j, (*chip, c), sibling) for j, chip in enumerate(chips)]
        for j, chip in enumerate(chips):
            copy(1 + j, (*chip, c), me).wait_recv()
            passed[j].start()
        copy(0, sibling, me).wait_recv()
        for j, chip in enumerate(chips):
            copy(4 + j, (*chip, 1 - c), me).wait_recv()
        for cp in first + passed:
            cp.wait_send()
        mine.wait()

    return pl.pallas_call(
        body, name=name,
        out_shape=jax.ShapeDtypeStruct((N_DEV,) + shard.shape, shard.dtype),
        in_specs=[pl.BlockSpec(memory_space=pl.ANY)],
        out_specs=pl.BlockSpec(memory_space=pl.ANY),
        scratch_shapes=[pltpu.SemaphoreType.DMA((N_DEV - 1,)), pltpu.SemaphoreType.DMA((N_DEV - 1,)),
                        pltpu.SemaphoreType.DMA],
        compiler_params=_cparams(),
    )(shard)


N_CHIP = N_DEV // 2
SIBLING_SEMS = [pltpu.SemaphoreType.DMA, pltpu.SemaphoreType.DMA]
CHIP_SEMS = [pltpu.SemaphoreType.DMA((N_CHIP - 1,)), pltpu.SemaphoreType.DMA((N_CHIP - 1,)), pltpu.SemaphoreType.DMA]


def _sibling_exchange(s_ref, o_ref, send_sem, recv_sem):
    x, y, c = _my_pos()
    cp = pltpu.make_async_remote_copy(s_ref, o_ref, send_sem, recv_sem, device_id=(x, y, 1 - c), device_id_type=MESH_ID)
    return cp.start, cp.wait


def _chip_exchange(p_ref, o_ref, send_sems, recv_sems, local_sem):
    x, y, c = _my_pos()
    my_chip = 2 * x + y
    mine = pltpu.make_async_copy(p_ref.at[my_chip], o_ref.at[my_chip], local_sem)
    peers = [(x ^ (k >> 1), y ^ (k & 1)) for k in range(1, N_CHIP)]
    sends = [pltpu.make_async_remote_copy(p_ref.at[2 * px + py], o_ref.at[my_chip], send_sems.at[k], recv_sems.at[k],
                                          device_id=(px, py, c), device_id_type=MESH_ID)
             for k, (px, py) in enumerate(peers)]
    recvs = [pltpu.make_async_remote_copy(p_ref.at[2 * px + py], o_ref.at[2 * px + py], send_sems.at[k], recv_sems.at[k],
                                          device_id=(px, py, c), device_id_type=MESH_ID)
             for k, (px, py) in enumerate(peers)]

    def start():
        mine.start()
        for cp in sends:
            cp.start()

    def wait():
        for cp in recvs:
            cp.wait_recv()
        for cp in sends:
            cp.wait_send()
        mine.wait()

    return start, wait


def exchange_sibling(send, name):
    def body(s_ref, o_ref, send_sem, recv_sem):
        start, wait = _sibling_exchange(s_ref, o_ref, send_sem, recv_sem)
        start()
        wait()

    return pl.pallas_call(
        body, name=name,
        out_shape=jax.ShapeDtypeStruct(send.shape, send.dtype),
        in_specs=[pl.BlockSpec(memory_space=pl.ANY)],
        out_specs=pl.BlockSpec(memory_space=pl.ANY),
        scratch_shapes=SIBLING_SEMS,
        compiler_params=_cparams(),
    )(send)


def _k_tile(kdim):
    for cand in range(MM_K_TILE, 0, -1024):
        if kdim % cand == 0:
            return cand
    return kdim


def mm_nn(a, b, out_dtype, name):
    m, kdim = a.shape
    n = b.shape[1]
    tm, tn, tk = min(MM_ROW_TILE, m), 1024, _k_tile(kdim)
    nk = kdim // tk

    def body(a_ref, b_ref, o_ref, acc_ref):
        p = _nn(a_ref[...], b_ref[...])
        if nk == 1:
            o_ref[...] = p.astype(o_ref.dtype)
        else:
            k = pl.program_id(2)

            @pl.when(k == 0)
            def _():
                acc_ref[...] = p

            @pl.when(k > 0)
            def _():
                acc_ref[...] += p

            @pl.when(k == nk - 1)
            def _():
                o_ref[...] = acc_ref[...].astype(o_ref.dtype)

    return pl.pallas_call(
        body, name=name, grid=(n // tn, m // tm, nk),
        out_shape=jax.ShapeDtypeStruct((m, n), out_dtype),
        in_specs=[pl.BlockSpec((tm, tk), lambda j, i, k: (i, k)), pl.BlockSpec((tk, tn), lambda j, i, k: (k, j))],
        out_specs=pl.BlockSpec((tm, tn), lambda j, i, k: (i, j)),
        scratch_shapes=[pltpu.VMEM((tm, tn), F32)],
        compiler_params=_cparams(("parallel", "parallel", "arbitrary")),
    )(a, b)


def mm_nt(a, b, out_dtype, name):
    m, kdim = a.shape
    n = b.shape[0]
    tm, tn, tk = min(MM_ROW_TILE, m), 1024, _k_tile(kdim)
    nk = kdim // tk

    def body(a_ref, b_ref, o_ref, acc_ref):
        p = _nt(a_ref[...], b_ref[...])
        if nk == 1:
            o_ref[...] = p.astype(o_ref.dtype)
        else:
            k = pl.program_id(2)

            @pl.when(k == 0)
            def _():
                acc_ref[...] = p

            @pl.when(k > 0)
            def _():
                acc_ref[...] += p

            @pl.when(k == nk - 1)
            def _():
                o_ref[...] = acc_ref[...].astype(o_ref.dtype)

    return pl.pallas_call(
        body, name=name, grid=(n // tn, m // tm, nk),
        out_shape=jax.ShapeDtypeStruct((m, n), out_dtype),
        in_specs=[pl.BlockSpec((tm, tk), lambda j, i, k: (i, k)), pl.BlockSpec((tn, tk), lambda j, i, k: (j, k))],
        out_specs=pl.BlockSpec((tm, tn), lambda j, i, k: (i, j)),
        scratch_shapes=[pltpu.VMEM((tm, tn), F32)],
        compiler_params=_cparams(("parallel", "parallel", "arbitrary")),
    )(a, b)


def mm_nn_exchange(a, b, out_dtype, part, name):
    kblocks, m, kb = a.shape
    kdim = kblocks * kb
    n = b.shape[1]
    tm, tn, tk = min(MM_ROW_TILE, m), 1024, _k_tile(kdim)
    gn, gm, nk = n // tn, m // tm, kdim // tk
    per_step = tk // kb

    def body(a_ref, b_ref, part_ref, o_ref, parts_ref, acc_ref, send_sems, recv_sems, local_sem):
        j, i, k = pl.program_id(0), pl.program_id(1), pl.program_id(2)
        xchg_start, xchg_wait = _chip_exchange(part_ref, parts_ref, send_sems, recv_sems, local_sem)

        @pl.when((j == 0) & (i == 0) & (k == 0))
        def _():
            xchg_start()

        p = _nn(a_ref[0], b_ref[0:kb, :])
        for c in range(1, per_step):
            p = p + _nn(a_ref[c], b_ref[c * kb:(c + 1) * kb, :])

        @pl.when(k == 0)
        def _():
            acc_ref[...] = p

        @pl.when(k > 0)
        def _():
            acc_ref[...] += p

        @pl.when(k == nk - 1)
        def _():
            o_ref[...] = acc_ref[...].astype(o_ref.dtype)

        @pl.when((j == gn - 1) & (i == gm - 1) & (k == nk - 1))
        def _():
            xchg_wait()

    hbm = pl.BlockSpec(memory_space=pl.ANY)
    return pl.pallas_call(
        body, name=name, grid=(gn, gm, nk),
        out_shape=[jax.ShapeDtypeStruct((m, n), out_dtype), jax.ShapeDtypeStruct(part.shape, part.dtype)],
        in_specs=[pl.BlockSpec((per_step, tm, kb), lambda j, i, k: (k, i, 0)),
                  pl.BlockSpec((tk, tn), lambda j, i, k: (k, j)), hbm],
        out_specs=[pl.BlockSpec((tm, tn), lambda j, i, k: (i, j)), hbm],
        scratch_shapes=[pltpu.VMEM((tm, tn), F32)] + CHIP_SEMS,
        compiler_params=_cparams(("arbitrary", "arbitrary", "arbitrary")),
    )(a, b, part)


def mm_nt_gather(a, b, out_dtype, shard, name):
    m, kdim = a.shape
    n = b.shape[0]
    tm, tn = min(MM_ROW_TILE, m), 1024
    assert kdim == 1024
    gj = m // tm
    nsteps = (n // tn) * gj
    forward_step = max(nsteps - 3, 0)

    def body(a_ref, b_ref, x_ref, o_ref, g_ref, send_sems, recv_sems, local_sem):
        step = pl.program_id(0) * gj + pl.program_id(1)
        x, y, c = _my_pos()
        me, sibling = (x, y, c), (x, y, 1 - c)
        chips = [(1 - x, y), (x, 1 - y), (1 - x, 1 - y)]

        def slot(pos):
            return g_ref.at[_flat(pos)]

        def copy(k, block, to, src=None):
            return pltpu.make_async_remote_copy(slot(block) if src is None else src, slot(block), send_sems.at[k],
                                                recv_sems.at[k], device_id=to, device_id_type=MESH_ID)

        mine = pltpu.make_async_copy(x_ref, slot(me), local_sem)
        first = [copy(0, me, sibling, src=x_ref)]
        first += [copy(1 + j, me, (*chip, c), src=x_ref) for j, chip in enumerate(chips)]
        passed = [copy(4 + j, (*chip, c), sibling) for j, chip in enumerate(chips)]

        @pl.when(step == 0)
        def _():
            mine.start()
            for cp in first:
                cp.start()

        o_ref[...] = _nt(a_ref[...], b_ref[...]).astype(o_ref.dtype)

        @pl.when(step == forward_step)
        def _():
            for j, chip in enumerate(chips):
                copy(1 + j, (*chip, c), me).wait_recv()
                passed[j].start()

        @pl.when(step == nsteps - 1)
        def _():
            copy(0, sibling, me).wait_recv()
            for j, chip in enumerate(chips):
                copy(4 + j, (*chip, 1 - c), me).wait_recv()
            for cp in first + passed:
                cp.wait_send()
            mine.wait()

    return pl.pallas_call(
        body, name=name, grid=(n // tn, gj),
        out_shape=[jax.ShapeDtypeStruct((m, n), out_dtype), jax.ShapeDtypeStruct((N_DEV,) + shard.shape, shard.dtype)],
        in_specs=[pl.BlockSpec((tm, kdim), lambda j, i: (i, 0)), pl.BlockSpec((tn, kdim), lambda j, i: (j, 0)),
                  pl.BlockSpec(memory_space=pl.ANY)],
        out_specs=[pl.BlockSpec((tm, tn), lambda j, i: (i, j)), pl.BlockSpec(memory_space=pl.ANY)],
        scratch_shapes=[pltpu.SemaphoreType.DMA((N_DEV - 1,)), pltpu.SemaphoreType.DMA((N_DEV - 1,)),
                        pltpu.SemaphoreType.DMA],
        compiler_params=_cparams(("arbitrary", "arbitrary")),
    )(a, b, shard)


def mm_tn(a, b, name):
    tt, tka, tn = min(MM_TOKEN_TILE, b.shape[0]), 1024, 1024
    if a.ndim == 3:
        t, ka = a.shape[1], a.shape[0] * a.shape[2]
        a_spec = pl.BlockSpec((None, tt, tka), lambda i, j, s: (i, s, 0))
    else:
        t, ka = a.shape
        a_spec = pl.BlockSpec((tt, tka), lambda i, j, s: (s, i))
    n = b.shape[1]
    nt = t // tt

    def body(a_ref, b_ref, o_ref, *acc):
        p = _tn(a_ref[...], b_ref[...])
        if nt == 1:
            o_ref[...] = p.astype(o_ref.dtype)
        else:
            acc_ref, s = acc[0], pl.program_id(2)

            @pl.when(s == 0)
            def _():
                acc_ref[...] = p

            @pl.when(s > 0)
            def _():
                acc_ref[...] += p

            @pl.when(s == nt - 1)
            def _():
                o_ref[...] = acc_ref[...].astype(o_ref.dtype)

    return pl.pallas_call(
        body, name=name, grid=(ka // tka, n // tn, nt),
        out_shape=jax.ShapeDtypeStruct((ka, n), BF16),
        in_specs=[a_spec, pl.BlockSpec((tt, tn), lambda i, j, s: (s, j))],
        out_specs=pl.BlockSpec((tka, tn), lambda i, j, s: (i, j)),
        scratch_shapes=[] if nt == 1 else [pltpu.VMEM((tka, tn), F32)],
        compiler_params=_cparams(("parallel", "parallel", "arbitrary")),
    )(a, b)


def _tile(t, cap):
    return min(cap, t)


def ln_modulate(x, mod6, shift_row, scale_row, name):
    t = x.shape[0]
    tm = _tile(t, ROW_TILE)

    def body(x_ref, mod_ref, o_ref):
        xh, _ = _ln(x_ref[...])
        sc = mod_ref[scale_row:scale_row + 1, :]
        sh = mod_ref[shift_row:shift_row + 1, :]
        o_ref[...] = (xh * (1.0 + sc) + sh).astype(BF16)

    return pl.pallas_call(
        body, name=name, grid=(t // tm,),
        out_shape=jax.ShapeDtypeStruct((t, D), BF16),
        in_specs=[pl.BlockSpec((tm, D), lambda i: (i, 0)), pl.BlockSpec((6, D), lambda i: (0, 0))],
        out_specs=pl.BlockSpec((tm, D), lambda i: (i, 0)),
        compiler_params=_cparams(("parallel",)),
    )(x, mod6)


def resid_ln(x, h, mod6, gate_row, ln_g, ln_b, name):
    t = x.shape[0]
    tm = _tile(t, ROW_TILE)

    def body(x_ref, h_ref, mod_ref, g_ref, b_ref, o_ref):
        r = ALPHA * x_ref[...] + mod_ref[gate_row:gate_row + 1, :] * h_ref[...]
        rh, _ = _ln(r)
        o_ref[...] = rh * g_ref[...] + b_ref[...]

    row = pl.BlockSpec((tm, D), lambda i: (i, 0))
    vec = pl.BlockSpec((1, D), lambda i: (0, 0))
    return pl.pallas_call(
        body, name=name, grid=(t // tm,),
        out_shape=jax.ShapeDtypeStruct((t, D), F32),
        in_specs=[row, row, pl.BlockSpec((6, D), lambda i: (0, 0)), vec, vec],
        out_specs=row,
        compiler_params=_cparams(("parallel",)),
    )(x, h, mod6, ln_g, ln_b)


def resid_ln_bwd(x, h, mod6, gate_row, ln_g, ln_b, cot, with_loss, name):
    t = x.shape[0]
    tm = _tile(t, ROW_TILE)

    def body(x_ref, h_ref, mod_ref, g_ref, b_ref, c_ref, dh_ref, dx_ref, acc_ref):
        @pl.when(pl.program_id(0) == 0)
        def _():
            acc_ref[...] = jnp.zeros_like(acc_ref)

        gate = mod_ref[gate_row:gate_row + 1, :]
        hv = h_ref[...]
        r = ALPHA * x_ref[...] + gate * hv
        rh, rstd = _ln(r)
        lng = g_ref[...]
        if with_loss:
            diff = rh * lng + b_ref[...] - c_ref[...]
            dxo = diff * (1.0 / D)
            lsum = jnp.sum(_colsum(diff * diff), axis=-1, keepdims=True) * (0.5 / D)
            acc_ref[3:4, :] += jnp.broadcast_to(lsum, (1, D))
        else:
            dxo = c_ref[...]
        acc_ref[1:2, :] += _colsum(dxo * rh)
        acc_ref[2:3, :] += _colsum(dxo)
        dr = _ln_bwd(dxo * lng, rh, rstd)
        acc_ref[0:1, :] += _colsum(dr * hv)
        dh_ref[...] = (gate * dr).astype(BF16)
        dx_ref[...] = ALPHA * dr

    row = pl.BlockSpec((tm, D), lambda i: (i, 0))
    vec = pl.BlockSpec((1, D), lambda i: (0, 0))
    return pl.pallas_call(
        body, name=name, grid=(t // tm,),
        out_shape=[jax.ShapeDtypeStruct((t, D), BF16), jax.ShapeDtypeStruct((t, D), F32),
                   jax.ShapeDtypeStruct((8, D), F32)],
        in_specs=[row, row, pl.BlockSpec((6, D), lambda i: (0, 0)), vec, vec, row],
        out_specs=[row, row, pl.BlockSpec((8, D), lambda i: (0, 0))],
        compiler_params=_cparams(("arbitrary",)),
    )(x, h, mod6, ln_g, ln_b, cot)


def ln_modulate_bwd(x, du, mod6, scale_row, dx_part, name):
    t = x.shape[0]
    tm = _tile(t, ROW_TILE)

    def body(x_ref, du_ref, mod_ref, dp_ref, dx_ref, acc_ref):
        @pl.when(pl.program_id(0) == 0)
        def _():
            acc_ref[...] = jnp.zeros_like(acc_ref)

        xh, rstd = _ln(x_ref[...])
        du_v = du_ref[...]
        sc = mod_ref[scale_row:scale_row + 1, :]
        acc_ref[0:1, :] += _colsum(du_v * xh)
        acc_ref[1:2, :] += _colsum(du_v)
        dx_ref[...] = dp_ref[...] + _ln_bwd(du_v * (1.0 + sc), xh, rstd)

    row = pl.BlockSpec((tm, D), lambda i: (i, 0))
    return pl.pallas_call(
        body, name=name, grid=(t // tm,),
        out_shape=[jax.ShapeDtypeStruct((t, D), F32), jax.ShapeDtypeStruct((8, D), F32)],
        in_specs=[row, row, pl.BlockSpec((6, D), lambda i: (0, 0)), row],
        out_specs=[row, pl.BlockSpec((8, D), lambda i: (0, 0))],
        compiler_params=_cparams(("arbitrary",)),
    )(x, du, mod6, dx_part)


def merge_gates(ya, yb, proj):
    t = ya.shape[0]
    tm = _tile(t, ROW_TILE)

    def body(ya_ref, yb_ref, ga_ref, gb_ref, o_ref):
        o_ref[...] = (_sigmoid(ga_ref[...]) * ya_ref[...].astype(F32) +
                      _sigmoid(gb_ref[...]) * yb_ref[...].astype(F32)).astype(BF16)

    row = pl.BlockSpec((tm, D), lambda i: (i, 0))
    return pl.pallas_call(
        body, name="merge_gates", grid=(t // tm,),
        out_shape=jax.ShapeDtypeStruct((t, D), BF16),
        in_specs=[row, row, pl.BlockSpec((tm, D), lambda i: (i, GATE_BLOCK0)),
                  pl.BlockSpec((tm, D), lambda i: (i, GATE_BLOCK0 + 1))],
        out_specs=row,
        compiler_params=_cparams(("parallel",)),
    )(ya, yb, proj, proj)


def merge_gates_bwd(dm, ya, yb, proj):
    t = ya.shape[0]
    tm = _tile(t, ROW_TILE)

    def body(dm_ref, ya_ref, yb_ref, ga_ref, gb_ref, dya_ref, dyb_ref, dp_ref):
        dmv = dm_ref[...].astype(F32)
        sa = _sigmoid(ga_ref[...])
        sb = _sigmoid(gb_ref[...])
        dya_ref[...] = (dmv * sa).astype(BF16)
        dyb_ref[...] = (dmv * sb).astype(BF16)
        dp_ref[0] = (dmv * ya_ref[...].astype(F32) * sa * (1.0 - sa)).astype(BF16)
        dp_ref[1] = (dmv * yb_ref[...].astype(F32) * sb * (1.0 - sb)).astype(BF16)

    row = pl.BlockSpec((tm, D), lambda i: (i, 0))
    return pl.pallas_call(
        body, name="merge_gates_bwd", grid=(t // tm,),
        out_shape=[jax.ShapeDtypeStruct((t, D), BF16)] * 2 + [jax.ShapeDtypeStruct((N_PROJ // D, t, D), BF16)],
        in_specs=[row, row, row, pl.BlockSpec((tm, D), lambda i: (i, GATE_BLOCK0)),
                  pl.BlockSpec((tm, D), lambda i: (i, GATE_BLOCK0 + 1))],
        out_specs=[row, row, pl.BlockSpec((2, tm, D), lambda i: (GATE_BLOCK0 // 2, i, 0))],
        compiler_params=_cparams(("parallel",)),
    )(dm, ya, yb, proj, proj)


def swiglu_act(gu):
    t = gu.shape[0]
    tm = _tile(t, FFN_ROW_TILE)

    def body(gu_ref, o_ref):
        for j in range(D_FF_PAD // D):
            g = gu_ref[:, j * D:(j + 1) * D].astype(F32)
            u = gu_ref[:, D_FF_PAD + j * D:D_FF_PAD + (j + 1) * D].astype(F32)
            o_ref[:, j * D:(j + 1) * D] = (g * _sigmoid(g) * u).astype(BF16)

    return pl.pallas_call(
        body, name="swiglu_act", grid=(t // tm,),
        out_shape=jax.ShapeDtypeStruct((t, D_FF_PAD), BF16),
        in_specs=[pl.BlockSpec((tm, 2 * D_FF_PAD), lambda i: (i, 0))],
        out_specs=pl.BlockSpec((tm, D_FF_PAD), lambda i: (i, 0)),
        compiler_params=_cparams(("parallel",)),
    )(gu)


def swiglu_act_bwd(gu, dact):
    t = gu.shape[0]
    tm = _tile(t, FFN_ROW_TILE)

    def body(gu_ref, da_ref, o_ref):
        for j in range(D_FF_PAD // D):
            g = gu_ref[:, j * D:(j + 1) * D].astype(F32)
            u = gu_ref[:, D_FF_PAD + j * D:D_FF_PAD + (j + 1) * D].astype(F32)
            da = da_ref[:, j * D:(j + 1) * D].astype(F32)
            s = _sigmoid(g)
            o_ref[:, j * D:(j + 1) * D] = (da * u * _dsilu(g, s)).astype(BF16)
            o_ref[:, D_FF_PAD + j * D:D_FF_PAD + (j + 1) * D] = (da * g * s).astype(BF16)

    return pl.pallas_call(
        body, name="swiglu_act_bwd", grid=(t // tm,),
        out_shape=jax.ShapeDtypeStruct((t, 2 * D_FF_PAD), BF16),
        in_specs=[pl.BlockSpec((tm, 2 * D_FF_PAD), lambda i: (i, 0)), pl.BlockSpec((tm, D_FF_PAD), lambda i: (i, 0))],
        out_specs=pl.BlockSpec((tm, 2 * D_FF_PAD), lambda i: (i, 0)),
        compiler_params=_cparams(("parallel",)),
    )(gu, dact)


def _hgrn_chunk_terms(q, fl, lbv, tril_f):
    sig = _sigmoid(fl)
    f = lbv + (1.0 - lbv) * sig
    lam = jnp.log(f)
    k = 1.0 - f
    sq = _sigmoid(q)
    qt = q * sq * Q_SCALE
    bc = _sel(_nn, lam, tril_f, 3, x_first=False)
    bmid = bc[CHUNK // 2 - 1:CHUNK // 2, :]
    bl = bc[CHUNK - 1:CHUNK, :]
    eq = jnp.exp(jnp.minimum(bc - bmid, EXP_CLIP))
    ek = jnp.exp(jnp.minimum(bmid - bc, EXP_CLIP))
    eb = jnp.exp(bc)
    ekl = jnp.exp(bl - bc)
    ebl = jnp.exp(bl)
    return sig, f, k, sq, qt, eq, ek, eb, ekl, ebl


def hgrn_fwd(proj, lb, gnorm):
    t = proj.shape[0]
    tb = _tile(t, TOKEN_BLOCK)
    ncb = tb // CHUNK

    hps = HGRN_HEADS_PER_STEP
    wide = hps * HK

    def body(q_ref, f_ref, i_ref, g_ref, lb_ref, gn_ref, oa_ref, oraw_ref, st_ref, state):
        @pl.when(pl.program_id(1) == 0)
        def _():
            state[...] = jnp.zeros_like(state)

        gn = gn_ref[...]
        mask = _tri(CHUNK)
        tril_f = mask.astype(BF16)

        def chunk(c, carry):
            sl = pl.ds(pl.multiple_of(c * CHUNK, CHUNK), CHUNK)
            for hh in range(hps):
                ln = slice(hh * HK, (hh + 1) * HK)
                q, fl, v, g = q_ref[sl, ln], f_ref[sl, ln], i_ref[sl, ln], g_ref[sl, ln]
                sig, f, k, sq, qt, eq, ek, eb, ekl, ebl = _hgrn_chunk_terms(q, fl, lb_ref[:, ln], tril_f)
                a = jnp.where(mask, _nt((qt * eq).astype(BF16), (k * ek).astype(BF16)), 0.0)
                st = state[hh]
                st_ref[hh, c] = st
                vb = v.astype(BF16)
                o = _nn(a.astype(BF16), vb) + _nt((qt * eb).astype(BF16), st.astype(BF16))
                state[hh] = st * ebl + _tn(vb, (k * ekl).astype(BF16))
                oraw_ref[sl, ln] = o
                rn = o * lax.rsqrt(jnp.mean(o * o, axis=-1, keepdims=True) + RMS_EPS)
                oa_ref[sl, ln] = (rn * gn * g * _sigmoid(g)).astype(BF16)
            return carry

        lax.fori_loop(0, ncb, chunk, 0, unroll=min(CHUNK_UNROLL, ncb))

    def col(block):
        return pl.BlockSpec((tb, wide), lambda h, j: (j, block * (N_HEADS_A // hps) + h))

    return pl.pallas_call(
        body, name="hgrn_fwd", grid=(N_HEADS_A // hps, t // tb),
        out_shape=[jax.ShapeDtypeStruct((t, D), BF16), jax.ShapeDtypeStruct((t, D), F32),
                   jax.ShapeDtypeStruct((N_HEADS_A, t // CHUNK, HK, HK), F32)],
        in_specs=[col(0), col(1), col(2), col(3), pl.BlockSpec((1, wide), lambda h, j: (0, h)),
                  pl.BlockSpec((1, HK), lambda h, j: (0, 0))],
        out_specs=[pl.BlockSpec((tb, wide), lambda h, j: (j, h)), pl.BlockSpec((tb, wide), lambda h, j: (j, h)),
                   pl.BlockSpec((hps, ncb, HK, HK), lambda h, j: (h, j, 0, 0))],
        scratch_shapes=[pltpu.VMEM((hps, HK, HK), F32)],
        compiler_params=_cparams(("parallel", "arbitrary")),
    )(proj, proj, proj, proj, lb, gnorm)


def hgrn_bwd(proj, lb, gnorm, o_raw, doa, states, give, dproj):
    t = proj.shape[0]
    tb = _tile(t, TOKEN_BLOCK)
    ncb = tb // CHUNK
    nb = t // tb
    hps = HGRN_HEADS_PER_STEP
    wide = hps * HK

    def body(q_ref, f_ref, i_ref, g_ref, lb_ref, gn_ref, oraw_ref, doa_ref, st_ref, give_ref, dp_in_ref,
             dp_ref, dlb_ref, dgn_ref, got_ref, dstate, send_sem, recv_sem):
        h, j = pl.program_id(0), pl.program_id(1)
        swap_start, swap_wait = _sibling_exchange(give_ref, got_ref, send_sem, recv_sem)

        @pl.when((h == 0) & (j == 0))
        def _():
            swap_start()

        @pl.when(j == 0)
        def _():
            dstate[...] = jnp.zeros_like(dstate)
            dlb_ref[...] = jnp.zeros_like(dlb_ref)

        @pl.when((j == 0) & (h == 0))
        def _():
            dgn_ref[...] = jnp.zeros_like(dgn_ref)

        gn = gn_ref[...]
        mask = _tri(CHUNK)
        mask_t = _tri(CHUNK, upper=True)
        tril_f = mask.astype(BF16)
        triu_f = mask_t.astype(BF16)

        def chunk(i, c0):
            c = ncb - 1 - i
            sl = pl.ds(pl.multiple_of(c * CHUNK, CHUNK), CHUNK)
            for hh in range(hps):
                ln = slice(hh * HK, (hh + 1) * HK)
                q, fl, v, g = q_ref[sl, ln], f_ref[sl, ln], i_ref[sl, ln], g_ref[sl, ln]
                lbv = lb_ref[:, ln]
                sig, f, k, sq, qt, eq, ek, eb, ekl, ebl = _hgrn_chunk_terms(q, fl, lbv, tril_f)
                qe = (qt * eq).astype(BF16)
                ke = (k * ek).astype(BF16)
                st32 = st_ref[hh, c]
                st = st32.astype(BF16)
                dst = dstate[hh]
                dstb = dst.astype(BF16)
                o = oraw_ref[sl, ln]
                rstd = lax.rsqrt(jnp.mean(o * o, axis=-1, keepdims=True) + RMS_EPS)
                rn = o * rstd
                sgm = _sigmoid(g)
                sg = g * sgm
                doa_v = doa_ref[sl, ln]
                drn = doa_v * gn * sg
                dgn_ref[...] += _colsum(doa_v * rn * sg)
                dp_ref[3, sl, ln] = (doa_v * rn * gn * _dsilu(g, sgm)).astype(BF16)
                do = rstd * (drn - rn * jnp.mean(drn * rn, axis=-1, keepdims=True))
                dob = do.astype(BF16)
                vb = v.astype(BF16)
                da = jnp.where(mask, _nt(dob, vb), 0.0).astype(BF16)
                da_t = jnp.where(mask_t, _nt(vb, dob), 0.0).astype(BF16)
                a_t = jnp.where(mask_t, _nt(ke, qe), 0.0).astype(BF16)
                kl = (k * ekl).astype(BF16)
                qb = (qt * eb).astype(BF16)
                dq_in = _nn(da, ke)
                dk_in = _nn(da_t, qe)
                dq_out = eb * _nn(dob, st)
                dk_out = ekl * _nn(vb, dstb)
                dqt = eq * dq_in + dq_out
                dk = ek * dk_in + dk_out
                dv = _nn(a_t, dob) + _nt(kl, dstb)
                dstate[hh] = dst * ebl + _tn(dob, qb)
                dbig = qe.astype(F32) * dq_in - ke.astype(F32) * dk_in + qt * dq_out - k * dk_out
                beyond = _colsum(k * dk_out) + ebl * _colsum(dst * st32)
                dlam = _sel(_nn, dbig, triu_f, 3, x_first=False) + beyond
                df = dlam / f - dk
                dp_ref[1, sl, ln] = (df * (1.0 - lbv) * sig * (1.0 - sig)).astype(BF16)
                dlb_ref[:, ln] += _colsum(df * (1.0 - sig))
                dp_ref[0, sl, ln] = (dqt * Q_SCALE * _dsilu(q, sq)).astype(BF16)
                dp_ref[2, sl, ln] = dv.astype(BF16)
            return c0

        lax.fori_loop(0, ncb, chunk, 0, unroll=min(CHUNK_UNROLL, ncb))

        @pl.when((h == N_HEADS_A // hps - 1) & (j == nb - 1))
        def _():
            swap_wait()

    def col(block):
        return pl.BlockSpec((tb, wide), lambda h, j: (nb - 1 - j, block * (N_HEADS_A // hps) + h))

    hcol = pl.BlockSpec((tb, wide), lambda h, j: (nb - 1 - j, h))
    hbm = pl.BlockSpec(memory_space=pl.ANY)
    return pl.pallas_call(
        body, name="hgrn_bwd", grid=(N_HEADS_A // hps, nb),
        out_shape=[jax.ShapeDtypeStruct(dproj.shape, dproj.dtype), jax.ShapeDtypeStruct((1, D), F32),
                   jax.ShapeDtypeStruct((1, HK), F32), jax.ShapeDtypeStruct(give.shape, give.dtype)],
        in_specs=[col(0), col(1), col(2), col(3), pl.BlockSpec((1, wide), lambda h, j: (0, h)),
                  pl.BlockSpec((1, HK), lambda h, j: (0, 0)), hcol, hcol,
                  pl.BlockSpec((hps, ncb, HK, HK), lambda h, j: (h, nb - 1 - j, 0, 0)), hbm, hbm],
        out_specs=[pl.BlockSpec((4, tb, wide), lambda h, j: (0, nb - 1 - j, h)),
                   pl.BlockSpec((1, wide), lambda h, j: (0, h)), pl.BlockSpec((1, HK), lambda h, j: (0, 0)), hbm],
        input_output_aliases={10: 0},
        scratch_shapes=[pltpu.VMEM((hps, HK, HK), F32)] + SIBLING_SEMS,
        compiler_params=_cparams(("arbitrary", "arbitrary")),
    )(proj, proj, proj, proj, lb, gnorm, o_raw, doa, states, give, dproj)


CONV_BLOCK0 = 6
CONV_TAPS = 4
HALO = 8


def conv_fwd(proj, conv_w, conv_b):
    t = proj.shape[0]
    tm = _tile(t, ROW_TILE)
    r = tm // HALO

    def body(x_ref, halo_ref, w_ref, b_ref, o_ref):
        i = pl.program_id(1)
        halo = jnp.where(i > 0, halo_ref[...], 0.0)
        ext = jnp.concatenate([halo, x_ref[...]], axis=0)
        pre = b_ref[...] + w_ref[CONV_TAPS - 1:CONV_TAPS, :] * ext[HALO:, :]
        for tap in range(CONV_TAPS - 1):
            pre = pre + w_ref[tap:tap + 1, :] * pltpu.roll(ext, CONV_TAPS - 1 - tap, axis=0)[HALO:, :]
        o_ref[...] = pre * _sigmoid(pre)

    return pl.pallas_call(
        body, name="conv_fwd", grid=(CONV_DIM // D, t // tm),
        out_shape=jax.ShapeDtypeStruct((t, CONV_DIM), F32),
        in_specs=[pl.BlockSpec((tm, D), lambda cb, i: (i, CONV_BLOCK0 + cb)),
                  pl.BlockSpec((HALO, D), lambda cb, i: (jnp.maximum(i * r - 1, 0), CONV_BLOCK0 + cb)),
                  pl.BlockSpec((CONV_TAPS, D), lambda cb, i: (0, cb)), pl.BlockSpec((1, D), lambda cb, i: (0, cb))],
        out_specs=pl.BlockSpec((tm, D), lambda cb, i: (i, cb)),
        compiler_params=_cparams(("parallel", "parallel")),
    )(proj, proj, conv_w, conv_b)


def conv_bwd(proj, dxc, conv_w, conv_b, dproj):
    t = proj.shape[0]
    tm = _tile(t, ROW_TILE)
    r = tm // HALO
    n = t // tm
    last_halo = t // HALO - 1

    def body(x_ref, prev_ref, next_ref, d_ref, dnext_ref, w_ref, b_ref, dp_in_ref, dx_ref, dw_ref, db_ref):
        i = pl.program_id(1)

        @pl.when(i == 0)
        def _():
            dw_ref[...] = jnp.zeros_like(dw_ref)
            db_ref[...] = jnp.zeros_like(db_ref)

        prev = jnp.where(i > 0, prev_ref[...], 0.0)
        ext = jnp.concatenate([prev, x_ref[...], next_ref[...]], axis=0)
        shifted = [pltpu.roll(ext, CONV_TAPS - 1 - tap, axis=0)[HALO:, :] for tap in range(CONV_TAPS - 1)]
        shifted.append(ext[HALO:, :])
        pre = b_ref[...]
        for tap in range(CONV_TAPS):
            pre = pre + w_ref[tap:tap + 1, :] * shifted[tap]
        s = _sigmoid(pre)
        d_ext = jnp.concatenate([d_ref[...].astype(F32),
                                 jnp.where(i < n - 1, dnext_ref[0:HALO, :].astype(F32), 0.0)], axis=0)
        dpre = d_ext * _dsilu(pre, s)
        dx = w_ref[CONV_TAPS - 1:CONV_TAPS, :] * dpre[:tm, :]
        for tap in range(CONV_TAPS - 1):
            back = CONV_TAPS - 1 - tap
            dx = dx + w_ref[tap:tap + 1, :] * pltpu.roll(dpre, tm + HALO - back, axis=0)[:tm, :]
        dx_ref[...] = dx.astype(BF16)
        dp = dpre[:tm, :]
        db_ref[...] += _colsum(dp)
        for tap in range(CONV_TAPS):
            dw_ref[tap:tap + 1, :] += _colsum(dp * shifted[tap][:tm, :])

    return pl.pallas_call(
        body, name="conv_bwd", grid=(CONV_DIM // D, n),
        out_shape=[jax.ShapeDtypeStruct(dproj.shape, dproj.dtype), jax.ShapeDtypeStruct((8, CONV_DIM), F32),
                   jax.ShapeDtypeStruct((1, CONV_DIM), F32)],
        in_specs=[pl.BlockSpec((tm, D), lambda cb, i: (i, CONV_BLOCK0 + cb)),
                  pl.BlockSpec((HALO, D), lambda cb, i: (jnp.maximum(i * r - 1, 0), CONV_BLOCK0 + cb)),
                  pl.BlockSpec((HALO, D), lambda cb, i: (jnp.minimum((i + 1) * r, last_halo), CONV_BLOCK0 + cb)),
                  pl.BlockSpec((tm, D), lambda cb, i: (i, cb)),
                  pl.BlockSpec((2 * HALO, D), lambda cb, i: (jnp.minimum((i + 1) * (r // 2), last_halo // 2), cb)),
                  pl.BlockSpec((CONV_TAPS, D), lambda cb, i: (0, cb)), pl.BlockSpec((1, D), lambda cb, i: (0, cb)),
                  pl.BlockSpec(memory_space=pl.ANY)],
        out_specs=[pl.BlockSpec((None, tm, D), lambda cb, i: (CONV_BLOCK0 + cb, i, 0)),
                   pl.BlockSpec((8, D), lambda cb, i: (0, cb)), pl.BlockSpec((1, D), lambda cb, i: (0, cb))],
        input_output_aliases={7: 0},
        compiler_params=_cparams(("parallel", "arbitrary")),
    )(proj, proj, proj, dxc, dxc, conv_w, conv_b, dproj)


def dt_fill(ddt, dproj):
    t = ddt.shape[0]
    tm = _tile(t, ROW_TILE)
    w = ddt.shape[1]

    def body(d_ref, dp_in_ref, o_ref):
        o_ref[:, :w] = d_ref[...]
        o_ref[:, w:] = jnp.zeros((tm, D - w), o_ref.dtype)

    return pl.pallas_call(
        body, name="dt_fill", grid=(t // tm,),
        out_shape=jax.ShapeDtypeStruct(dproj.shape, dproj.dtype),
        in_specs=[pl.BlockSpec((tm, w), lambda i: (i, 0)), pl.BlockSpec(memory_space=pl.ANY)],
        out_specs=pl.BlockSpec((None, tm, D), lambda i: (DT_COL_BLOCK, i, 0)),
        input_output_aliases={1: 0},
        compiler_params=_cparams(("parallel",)),
    )(ddt, dproj)


Z_BLOCK0 = 8
DT_COL_BLOCK = 9
DT_BLOCK0 = 8 * DT_COL_BLOCK
GATE_BLOCK0 = 10
B_BLOCK0 = 16
C_BLOCK0 = 20


def _head_expand():
    e = np.zeros((N_STATE, GROUP_W), np.float32)
    for hh in range(HEADS_PER_GROUP):
        e[hh, hh * HEAD_P:(hh + 1) * HEAD_P] = 1.0
    return jnp.asarray(e, BF16)


def _ssd_chunk_terms(dt, bias, alog, expand, tril_f, eye):
    dtb = dt + bias
    delta = jnp.maximum(dtb, 0.0) + jnp.log(1.0 + jnp.exp(-jnp.abs(dtb)))
    ea = jnp.exp(alog)
    a = -ea * delta
    acum = _sel(_nn, a, tril_f, 3, x_first=False)
    delta_e = _sel(_nn, delta, expand, 2)
    acum_e = _sel(_nn, acum, expand, 3)
    acum_t = _sel(_nt, acum, eye, 3, x_first=False)
    return dtb, delta, ea, a, acum, delta_e, acum_e, acum_t


def ssd_fwd(proj, xc, alog4, bias4, dskip4, wnorm, expand):
    t = proj.shape[0]
    tb = _tile(t, TOKEN_BLOCK)
    ncb = tb // SSD_CHUNK

    def body(xs_ref, b_ref, c_ref, dt_ref, z_ref, alog_ref, bias_ref, dsk_ref, wn_ref, e_ref, ob_ref, st_ref, state):
        @pl.when(pl.program_id(1) == 0)
        def _():
            state[...] = jnp.zeros_like(state)

        expand = e_ref[...]
        mask = _tri(SSD_CHUNK)
        tril_f = mask.astype(BF16)
        eye = (lax.broadcasted_iota(jnp.int32, (N_STATE, N_STATE), 0) ==
               lax.broadcasted_iota(jnp.int32, (N_STATE, N_STATE), 1)).astype(BF16)
        alog, bias = alog_ref[0], bias_ref[0]
        d_e = _sel(_nn, jnp.broadcast_to(dsk_ref[0], (8, N_STATE)), expand, 3)[0:1, :]
        wn = wn_ref[...]

        def chunk(c, carry):
            sl = pl.ds(pl.multiple_of(c * SSD_CHUNK, SSD_CHUNK), SSD_CHUNK)
            xs, bm, cm, dt, z = xs_ref[sl, :], b_ref[sl, :], c_ref[sl, :], dt_ref[sl, :], z_ref[sl, :]
            dtb, delta, ea, a, acum, delta_e, acum_e, acum_t = _ssd_chunk_terms(dt, bias, alog, expand, tril_f, eye)
            alast_e = acum_e[SSD_CHUNK - 1:SSD_CHUNK, :]
            xd = xs * delta_e
            xdb = xd.astype(BF16)
            cb_, bb_ = cm.astype(BF16), bm.astype(BF16)
            cbm = _nt(cb_, bb_)
            ys = []
            for hh in range(HEADS_PER_GROUP):
                lh = jnp.where(mask, jnp.exp(jnp.minimum(acum[:, hh:hh + 1] - acum_t[hh:hh + 1, :], 0.0)), 0.0)
                ys.append(_nn((cbm * lh).astype(BF16), xdb[:, hh * HEAD_P:(hh + 1) * HEAD_P]))
            st = state[...]
            st_ref[0, c] = st
            y = jnp.concatenate(ys, axis=1) + _nn(cb_, st.astype(BF16)) * jnp.exp(acum_e) + xs * d_e
            state[...] = st * jnp.exp(alast_e) + _tn(bb_, (xd * jnp.exp(alast_e - acum_e)).astype(BF16))
            yg = y * z * _sigmoid(z)
            ob_ref[sl, :] = (yg * lax.rsqrt(jnp.mean(yg * yg, axis=-1, keepdims=True) + RMS_EPS) * wn).astype(BF16)
            return carry

        lax.fori_loop(0, ncb, chunk, 0, unroll=min(CHUNK_UNROLL, ncb))

    small = pl.BlockSpec((1, 1, N_STATE), lambda g, j: (g, 0, 0))
    return pl.pallas_call(
        body, name="ssd_fwd", grid=(N_GROUPS, t // tb),
        out_shape=[jax.ShapeDtypeStruct((t, B_INNER), BF16),
                   jax.ShapeDtypeStruct((N_GROUPS, t // SSD_CHUNK, N_STATE, GROUP_W), F32)],
        in_specs=[pl.BlockSpec((tb, GROUP_W), lambda g, j: (j, g)),
                  pl.BlockSpec((tb, N_STATE), lambda g, j: (j, B_BLOCK0 + g)),
                  pl.BlockSpec((tb, N_STATE), lambda g, j: (j, C_BLOCK0 + g)),
                  pl.BlockSpec((tb, N_STATE), lambda g, j: (j, DT_BLOCK0 + g)),
                  pl.BlockSpec((tb, GROUP_W), lambda g, j: (j, Z_BLOCK0 + g)),
                  small, small, small, pl.BlockSpec((1, GROUP_W), lambda g, j: (0, g)),
                  pl.BlockSpec((N_STATE, GROUP_W), lambda g, j: (0, 0))],
        out_specs=[pl.BlockSpec((tb, GROUP_W), lambda g, j: (j, g)),
                   pl.BlockSpec((1, ncb, N_STATE, GROUP_W), lambda g, j: (g, j, 0, 0))],
        scratch_shapes=[pltpu.VMEM((N_STATE, GROUP_W), F32)],
        compiler_params=_cparams(("parallel", "arbitrary")),
    )(xc, xc, xc, proj, proj, alog4, bias4, dskip4, wnorm, expand)


def ssd_bwd(proj, xc, alog4, bias4, dskip4, wnorm, expand, dob, states, part, dproj):
    t = proj.shape[0]
    tb = _tile(t, TOKEN_BLOCK)
    ncb = tb // SSD_CHUNK
    nb = t // tb

    def body(xs_ref, b_ref, c_ref, dt_ref, z_ref, alog_ref, bias_ref, dsk_ref, wn_ref, e_ref, dob_ref, st_ref, part_ref,
             dp_in_ref, dxs_ref, db_ref, dc_ref, dz_ref, ddt_ref, dwn_ref, dalog_ref, dbias_ref, ddsk_ref, parts_ref, dstate,
             send_sems, recv_sems, local_sem):
        xchg_start, xchg_wait = _chip_exchange(part_ref, parts_ref, send_sems, recv_sems, local_sem)

        @pl.when((pl.program_id(0) == 0) & (pl.program_id(1) == 0))
        def _():
            xchg_start()

        @pl.when(pl.program_id(1) == 0)
        def _():
            dstate[...] = jnp.zeros_like(dstate)
            dwn_ref[...] = jnp.zeros_like(dwn_ref)
            dalog_ref[...] = jnp.zeros_like(dalog_ref)
            dbias_ref[...] = jnp.zeros_like(dbias_ref)
            ddsk_ref[...] = jnp.zeros_like(ddsk_ref)

        expand = e_ref[...]
        mask = _tri(SSD_CHUNK)
        mask_t = _tri(SSD_CHUNK, upper=True)
        tril_f = mask.astype(BF16)
        triu_f = mask_t.astype(BF16)
        eye = (lax.broadcasted_iota(jnp.int32, (N_STATE, N_STATE), 0) ==
               lax.broadcasted_iota(jnp.int32, (N_STATE, N_STATE), 1)).astype(BF16)
        alog, bias = alog_ref[0], bias_ref[0]
        d_e = _sel(_nn, jnp.broadcast_to(dsk_ref[0], (8, N_STATE)), expand, 3)[0:1, :]
        wn = wn_ref[...]

        def chunk(i, c0):
            c = ncb - 1 - i
            sl = pl.ds(pl.multiple_of(c * SSD_CHUNK, SSD_CHUNK), SSD_CHUNK)
            xs, bm, cm, dt, z = xs_ref[sl, :], b_ref[sl, :], c_ref[sl, :], dt_ref[sl, :], z_ref[sl, :]
            dtb, delta, ea, a, acum, delta_e, acum_e, acum_t = _ssd_chunk_terms(dt, bias, alog, expand, tril_f, eye)
            alast_e = acum_e[SSD_CHUNK - 1:SSD_CHUNK, :]
            eacum = jnp.exp(acum_e)
            wl = jnp.exp(alast_e - acum_e)
            xd = xs * delta_e
            xdb = xd.astype(BF16)
            cb_, bb_ = cm.astype(BF16), bm.astype(BF16)
            cbm = _nt(cb_, bb_)
            st32 = st_ref[0, c]
            stb = st32.astype(BF16)
            dst = dstate[...]
            dstb = dst.astype(BF16)
            lhs, mixes, ys = [], [], []
            for hh in range(HEADS_PER_GROUP):
                col, row = acum[:, hh:hh + 1], acum_t[hh:hh + 1, :]
                lh = jnp.where(mask, jnp.exp(jnp.minimum(col - row, 0.0)), 0.0)
                mix = (cbm * lh).astype(BF16)
                lhs.append(lh)
                mixes.append(mix)
                ys.append(_nn(mix, xdb[:, hh * HEAD_P:(hh + 1) * HEAD_P]))
            y_in = jnp.concatenate(ys, axis=1)
            y_out = _nn(cb_, stb) * eacum
            y = y_in + y_out + xs * d_e
            sgz = _sigmoid(z)
            sz = z * sgz
            yg = y * sz
            rstd = lax.rsqrt(jnp.mean(yg * yg, axis=-1, keepdims=True) + RMS_EPS)
            nrm = yg * rstd
            dob_v = dob_ref[sl, :]
            dn = dob_v * wn
            dwn_ref[...] += _colsum(dob_v * nrm)
            dyg = rstd * (dn - nrm * jnp.mean(dn * nrm, axis=-1, keepdims=True))
            dy = dyg * sz
            dz_ref[sl, :] = (dyg * y * _dsilu(z, sgz)).astype(BF16)
            dyb = dy.astype(BF16)
            dxds = []
            dcb = jnp.zeros((SSD_CHUNK, SSD_CHUNK), F32)
            for hh in range(HEADS_PER_GROUP):
                hs = slice(hh * HEAD_P, (hh + 1) * HEAD_P)
                dy_h, x_h = dyb[:, hs], xdb[:, hs]
                dxds.append(_tn(mixes[hh], dy_h))
                dcb = dcb + _nt(dy_h, x_h) * lhs[hh]
            dcbb = dcb.astype(BF16)
            dye = (dy * eacum).astype(BF16)
            xw = (xd * wl).astype(BF16)
            dxd_in = jnp.concatenate(dxds, axis=1)
            dxd_out = wl * _nn(bb_, dstb)
            dxd = dxd_in + dxd_out
            dc_ref[sl, :] = (_nn(dcbb, bb_) + _nt(dye, stb)).astype(dc_ref.dtype)
            db_ref[sl, :] = (_tn(dcbb, cb_) + _nt(xw, dstb)).astype(db_ref.dtype)
            dstate[...] = dst * jnp.exp(alast_e) + _tn(cb_, dye)
            col_out = xd * dxd_out
            dac = _sel(_nt, dyb.astype(F32) * y_in - xdb.astype(F32) * dxd_in + dy * y_out - col_out, expand, 3)
            beyond = _colsum(col_out) + jnp.exp(alast_e) * _colsum(dst * st32)
            da = (_sel(_nn, dac, triu_f, 3, x_first=False) +
                  _sel(_nt, jnp.broadcast_to(beyond, (8, GROUP_W)), expand, 3)[0:1, :])
            ddelta = _sel(_nt, dxd * xs, expand, 2) - da * ea
            dalog_ref[0] += _colsum(da * a)
            ddtb = ddelta * _sigmoid(dtb)
            dbias_ref[0] += _colsum(ddtb)
            ddt_ref[sl, :] = ddtb.astype(BF16)
            ddsk_ref[0] += _sel(_nt, jnp.broadcast_to(_colsum(dy * xs), (8, GROUP_W)), expand, 3)[0:1, :]
            dxs_ref[sl, :] = (dxd * delta_e + dy * d_e).astype(dxs_ref.dtype)
            return c0

        lax.fori_loop(0, ncb, chunk, 0, unroll=min(CHUNK_UNROLL, ncb))

        @pl.when((pl.program_id(0) == N_GROUPS - 1) & (pl.program_id(1) == nb - 1))
        def _():
            xchg_wait()

    small = pl.BlockSpec((1, 1, N_STATE), lambda g, j: (g, 0, 0))
    wide = pl.BlockSpec((tb, GROUP_W), lambda g, j: (nb - 1 - j, g))
    narrow = pl.BlockSpec((tb, N_STATE), lambda g, j: (nb - 1 - j, g))
    hbm = pl.BlockSpec(memory_space=pl.ANY)
    return pl.pallas_call(
        body, name="ssd_bwd", grid=(N_GROUPS, nb),
        out_shape=[jax.ShapeDtypeStruct((t, B_INNER), BF16), jax.ShapeDtypeStruct((t, GROUP_W), BF16),
                   jax.ShapeDtypeStruct((t, GROUP_W), BF16), jax.ShapeDtypeStruct(dproj.shape, dproj.dtype),
                   jax.ShapeDtypeStruct((t, GROUP_W), BF16), jax.ShapeDtypeStruct((1, B_INNER), F32),
                   jax.ShapeDtypeStruct((N_GROUPS, 1, N_STATE), F32), jax.ShapeDtypeStruct((N_GROUPS, 1, N_STATE), F32),
                   jax.ShapeDtypeStruct((N_GROUPS, 1, N_STATE), F32), jax.ShapeDtypeStruct(part.shape, part.dtype)],
        in_specs=[wide,
                  pl.BlockSpec((tb, N_STATE), lambda g, j: (nb - 1 - j, B_BLOCK0 + g)),
                  pl.BlockSpec((tb, N_STATE), lambda g, j: (nb - 1 - j, C_BLOCK0 + g)),
                  pl.BlockSpec((tb, N_STATE), lambda g, j: (nb - 1 - j, DT_BLOCK0 + g)),
                  pl.BlockSpec((tb, GROUP_W), lambda g, j: (nb - 1 - j, Z_BLOCK0 + g)),
                  small, small, small, pl.BlockSpec((1, GROUP_W), lambda g, j: (0, g)),
                  pl.BlockSpec((N_STATE, GROUP_W), lambda g, j: (0, 0)), wide,
                  pl.BlockSpec((1, ncb, N_STATE, GROUP_W), lambda g, j: (g, nb - 1 - j, 0, 0)), hbm, hbm],
        out_specs=[wide, narrow, narrow,
                   pl.BlockSpec((None, tb, GROUP_W), lambda g, j: (Z_BLOCK0 // 2 + g // 2, nb - 1 - j, g % 2)),
                   narrow, pl.BlockSpec((1, GROUP_W), lambda g, j: (0, g)), small, small, small, hbm],
        input_output_aliases={13: 3},
        scratch_shapes=[pltpu.VMEM((N_STATE, GROUP_W), F32)] + CHIP_SEMS,
        compiler_params=_cparams(("arbitrary", "arbitrary")),
    )(xc, xc, xc, proj, proj, alog4, bias4, dskip4, wnorm, expand, dob, states, part, dproj)


def lower_bound_fwd(hgrn_lb):
    def body(a_ref, o_ref):
        a0, a1 = a_ref[0:1, :], a_ref[1:2, :]
        m = jnp.maximum(a0, a1)
        e0, e1 = jnp.exp(a0 - m), jnp.exp(a1 - m)
        o_ref[...] = e0 / (e0 + e1)

    return pl.pallas_call(body, name="lower_bound_fwd", out_shape=jax.ShapeDtypeStruct((1, D), F32))(hgrn_lb)


def ada_weight_grad(c_all, dmod_cols):
    def body(c_ref, d_ref, o_ref):
        cval = c_ref[...]
        o_ref[...] = _tn(cval * _sigmoid(cval), d_ref[...], HI)

    return pl.pallas_call(body, name="ada_weight_grad",
                          out_shape=jax.ShapeDtypeStruct((D, dmod_cols.shape[1]), F32))(c_all, dmod_cols)


def reduce_small(gathered, hgrn_lb, dlb_off):
    n = gathered.shape[2]

    def body(g_ref, a_ref, o_ref, glb_ref):
        s = g_ref[0]
        for d in range(1, N_DEV):
            s = s + g_ref[d]
        o_ref[...] = s
        a0, a1 = a_ref[0:1, :], a_ref[1:2, :]
        m = jnp.maximum(a0, a1)
        e0, e1 = jnp.exp(a0 - m), jnp.exp(a1 - m)
        p0 = e0 / (e0 + e1)
        tq = s[:, dlb_off:dlb_off + D] * p0 * (1.0 - p0)
        glb_ref[0:1, :] = tq
        glb_ref[1:2, :] = -tq

    return pl.pallas_call(body, name="reduce_small",
                          out_shape=[jax.ShapeDtypeStruct((1, n), F32), jax.ShapeDtypeStruct((2, D), F32)])(gathered, hgrn_lb)


def _adam_math(w, g, m, v):
    m2 = ADAM_B1 * m + (1.0 - ADAM_B1) * g
    v2 = ADAM_B2 * v + (1.0 - ADAM_B2) * (g * g)
    m_hat = m2 / (1.0 - ADAM_B1 ** ADAM_STEP)
    v_hat = v2 / (1.0 - ADAM_B2 ** ADAM_STEP)
    delta = -ADAM_LR * (m_hat / (jnp.sqrt(v_hat) + ADAM_EPS) + ADAM_WD * w)
    return delta, m2, v2


def _row_tile(rows, mult=8, cap=128):
    for cand in range(cap - cap % mult, 0, -mult):
        if rows % cand == 0:
            return cand
    return rows


def sum_parts(parts, name):
    n, rows, cols = parts.shape
    tr = _row_tile(rows, 16, 256)

    def body(p_ref, o_ref):
        s = p_ref[0].astype(F32)
        for d in range(1, n):
            s = s + p_ref[d].astype(F32)
        o_ref[...] = s

    return pl.pallas_call(
        body, name=name, grid=(rows // tr,),
        out_shape=jax.ShapeDtypeStruct((rows, cols), F32),
        in_specs=[pl.BlockSpec((n, tr, cols), lambda i: (0, i, 0))],
        out_specs=pl.BlockSpec((tr, cols), lambda i: (i, 0)),
        compiler_params=_cparams(("parallel",)),
    )(parts)


def sum_pair(a, b, name):
    rows, cols = a.shape
    tr = _row_tile(rows, 16, 256)

    def body(a_ref, b_ref, o_ref):
        o_ref[...] = (a_ref[...].astype(F32) + b_ref[...].astype(F32)).astype(o_ref.dtype)

    blk = pl.BlockSpec((tr, cols), lambda i: (i, 0))
    return pl.pallas_call(
        body, name=name, grid=(rows // tr,),
        out_shape=jax.ShapeDtypeStruct((rows, cols), a.dtype),
        in_specs=[blk, blk], out_specs=blk,
        compiler_params=_cparams(("parallel",)),
    )(a, b)


def adamw(w, g, m, v, name):
    rows, cols = w.shape
    tr = _row_tile(rows)

    def body(w_ref, g_ref, m_ref, v_ref, d_ref, m2_ref, v2_ref):
        delta, m2, v2 = _adam_math(w_ref[...], g_ref[...], m_ref[...], v_ref[...])
        d_ref[...] = delta
        m2_ref[...] = m2
        v2_ref[...] = v2

    blk = pl.BlockSpec((tr, cols), lambda i: (i, 0))
    return pl.pallas_call(
        body, name=name, grid=(rows // tr,),
        out_shape=[jax.ShapeDtypeStruct((rows, cols), F32)] * 3,
        in_specs=[blk] * 4, out_specs=[blk] * 3,
        compiler_params=_cparams(("parallel",)),
    )(w, g, m, v)


def _pad128(n):
    return -(-n // 128) * 128


def _pack(arrays):
    offs, parts, off = [], [], 0
    for a in arrays:
        flat = a.reshape(1, -1)
        n = flat.shape[1]
        offs.append(off)
        parts.append(jnp.pad(flat, ((0, 0), (0, _pad128(n) - n))))
        off += _pad128(n)
    return jnp.concatenate(parts, axis=1), offs


def _unpack(vec, offs, shapes):
    out = []
    for off, shp in zip(offs, shapes):
        n = int(np.prod(shp))
        out.append(vec[0, off:off + n].reshape(shp))
    return out


IN_ROWS = IN_DIM // N_DEV
DT_ROW0 = 9216
DT_DEV, DT_LO = divmod(DT_ROW0, IN_ROWS)


GATE_SHIFT = D - 32


def _in_row_pieces(tile):
    pieces = []
    if tile == DT_COL_BLOCK:
        for g in range(N_GROUPS):
            o = DT_ROW0 + HEADS_PER_GROUP * g
            pieces.append((N_STATE * g, o // IN_ROWS, o % IN_ROWS, HEADS_PER_GROUP))
        return pieces
    r, end = tile * D, (tile + 1) * D
    while r < end:
        o = r if r < DT_ROW0 else r - GATE_SHIFT
        dev, loc = divmod(o, IN_ROWS)
        n = min(end - r, IN_ROWS - loc)
        pieces.append((r - tile * D, dev, loc, n))
        r += n
    return pieces


def assemble_w_in(g_all):
    ntile = N_PROJ // D

    def body(g_ref, o_ref):
        j = pl.program_id(0)
        for tile in range(ntile):
            @pl.when(j == tile)
            def _(tile=tile):
                if tile == DT_COL_BLOCK:
                    o_ref[...] = jnp.zeros_like(o_ref)
                for dst, dev, loc, n in _in_row_pieces(tile):
                    o_ref[pl.ds(dst, n), :] = g_ref[dev, pl.ds(loc, n), :]

    return pl.pallas_call(
        body, name="assemble_w_in", grid=(ntile,),
        out_shape=jax.ShapeDtypeStruct((N_PROJ, D), g_all.dtype),
        in_specs=[pl.BlockSpec(memory_space=pltpu.VMEM)],
        out_specs=pl.BlockSpec((D, D), lambda j: (j, 0)),
        compiler_params=_cparams(("arbitrary",)),
    )(g_all)


def _grad_in_blocks(g_t, core, slot):
    dt0 = DT_COL_BLOCK * D
    dt = g_t[dt0:dt0 + N_GROUPS * N_STATE].reshape(N_GROUPS, N_STATE, D)[:, :HEADS_PER_GROUP].reshape(32, D)
    with_dt = jnp.concatenate([g_t[DT_DEV * IN_ROWS:DT_ROW0], dt,
                               g_t[DT_ROW0 + 32 + GATE_SHIFT:(DT_DEV + 1) * IN_ROWS + GATE_SHIFT]], axis=0)
    blocks = []
    for q in range(N_CHIP):
        if 2 * q + 1 < DT_DEV:
            blk = lax.dynamic_slice_in_dim(g_t, IN_ROWS * (2 * q + core), IN_ROWS, axis=0)
        else:
            assert 2 * q == DT_DEV
            after = g_t[(DT_DEV + 1) * IN_ROWS + GATE_SHIFT:(DT_DEV + 2) * IN_ROWS + GATE_SHIFT]
            blk = jnp.where(core == 0, with_dt, after)
        blocks.append(jnp.pad(blk, ((0, slot - IN_ROWS), (0, 0))))
    return jnp.stack(blocks)


def kernel(x, c, w_ada, b_ada, w_in, hgrn_lb, hgrn_gnorm, ssm_conv_w, ssm_conv_b, ssm_dt_bias, ssm_a_log, ssm_d, ssm_norm, w_branch_a, w_branch_b, w_o, ln1_g, ln1_b, w_ffn_gate, w_ffn_up, w_ffn_down, ln2_g, ln2_b, loss_target, m_w_ada, m_b_ada, m_w_in, m_hgrn_lb, m_hgrn_gnorm, m_ssm_conv_w, m_ssm_conv_b, m_ssm_dt_bias, m_ssm_a_log, m_ssm_d, m_ssm_norm, m_w_branch_a, m_w_branch_b, m_w_o, m_ln1_g, m_ln1_b, m_w_ffn_gate, m_w_ffn_up, m_w_ffn_down, m_ln2_g, m_ln2_b, v_w_ada, v_b_ada, v_w_in, v_hgrn_lb, v_hgrn_gnorm, v_ssm_conv_w, v_ssm_conv_b, v_ssm_dt_bias, v_ssm_a_log, v_ssm_d, v_ssm_norm, v_w_branch_a, v_w_branch_b, v_w_o, v_ln1_g, v_ln1_b, v_w_ffn_gate, v_w_ffn_up, v_w_ffn_down, v_ln2_g, v_ln2_b):
    me = 4 * lax.axis_index("x") + 2 * lax.axis_index("y") + lax.axis_index("c")
    xt = x[0]
    tgt = loss_target[0]
    t = xt.shape[0]
    ada_cols = w_ada.shape[2]
    conv_cols = ssm_conv_w.shape[2]

    small_in, _ = _pack([c, ssm_conv_w[0]])
    small_all = allgather_vmem(small_in, "allgather_small_inputs")
    c_all = small_all[:, 0, :D]
    conv_w = small_all[:, 0, D:D + CONV_TAPS * conv_cols].reshape(N_DEV, CONV_TAPS, conv_cols)
    conv_w = conv_w.transpose(1, 0, 2).reshape(CONV_TAPS, CONV_DIM)
    mod = ada_modulation(c_all, w_ada[0], b_ada.reshape(N_DEV, 1, ada_cols))
    mod6 = mod.reshape(6, D)

    shards = [w_in[0].T, w_branch_a[0], w_branch_b[0], w_o[0], w_ffn_gate[0].T, w_ffn_up[0].T, w_ffn_down[0]]
    shard_rows = [s.shape[0] for s in shards]
    slot_rows = [-(-r // 32) * 32 for r in shard_rows]
    row_offs = [sum(slot_rows[:i]) for i in range(len(shards))]
    padded = [jnp.pad(s.astype(BF16), ((0, p - r), (0, 0))) for s, r, p in zip(shards, shard_rows, slot_rows)]
    w_in_t = assemble_w_in(allgather_hbm(padded[0], "allgather_w_in"))

    lb = lower_bound_fwd(hgrn_lb)
    u1 = ln_modulate(xt, mod6, 0, 1, "ln_modulate_1")
    proj, g_rest = mm_nt_gather(u1, w_in_t, F32, jnp.concatenate(padded[1:], axis=0), "mm_in_proj")
    g_ba, g_bb, g_o, g_fg, g_fu, g_fd = (g_rest[:, o - slot_rows[0]:o - slot_rows[0] + r]
                                         for o, r in zip(row_offs[1:], shard_rows[1:]))
    w_ba = g_ba.reshape(D, D)
    w_bb = g_bb.reshape(B_INNER, D)
    w_oo = g_o.reshape(D, D)
    ffpad = ((0, D_FF_PAD - D_FF), (0, 0))
    w_gu_t = jnp.concatenate([jnp.pad(g_fg.reshape(D_FF, D), ffpad), jnp.pad(g_fu.reshape(D_FF, D), ffpad)], axis=0)
    w_dn = jnp.pad(g_fd.reshape(D_FF, D), ffpad)
    o_a, o_raw, st_a = hgrn_fwd(proj, lb, hgrn_gnorm)
    xc = conv_fwd(proj, conv_w, ssm_conv_b)
    pad3 = ((0, 0), (0, 0), (0, N_STATE - HEADS_PER_GROUP))
    alog4 = jnp.pad(ssm_a_log.reshape(N_GROUPS, 1, HEADS_PER_GROUP), pad3)
    bias4 = jnp.pad(ssm_dt_bias.reshape(N_GROUPS, 1, HEADS_PER_GROUP), pad3)
    dskip4 = jnp.pad(ssm_d.reshape(N_GROUPS, 1, HEADS_PER_GROUP), pad3)
    expand = _head_expand()
    o_b, st_b = ssd_fwd(proj, xc, alog4, bias4, dskip4, ssm_norm, expand)
    ya = mm_nn(o_a, w_ba, BF16, "mm_branch_a")
    yb = mm_nn(o_b, w_bb, BF16, "mm_branch_b")
    merged = merge_gates(ya, yb, proj)
    h1 = mm_nn(merged, w_oo, F32, "mm_out_proj")
    x1 = resid_ln(xt, h1, mod6, 2, ln1_g, ln1_b, "resid_ln_1")
    u2 = ln_modulate(x1, mod6, 3, 4, "ln_modulate_2")
    gu = mm_nt(u2, w_gu_t, BF16, "mm_ffn_in")
    act = swiglu_act(gu)
    h2 = mm_nn(act, w_dn, F32, "mm_ffn_out")

    dh2, dx1_part, acc4 = resid_ln_bwd(x1, h2, mod6, 5, ln2_g, ln2_b, tgt, True, "resid_ln_2_bwd")
    g_dn = mm_tn(act, dh2, "mm_grad_ffn_down")
    dact = mm_nt(dh2, w_dn, BF16, "mm_dact")
    dgu = swiglu_act_bwd(gu, dact)
    g_gu_t = mm_tn(dgu, u2, "mm_grad_ffn_in")
    du2 = mm_nn(dgu, w_gu_t, F32, "mm_du2")
    dx1, acc3 = ln_modulate_bwd(x1, du2, mod6, 4, dx1_part, "ln_modulate_2_bwd")
    dh1, dx_part, acc2 = resid_ln_bwd(xt, h1, mod6, 2, ln1_g, ln1_b, dx1, False, "resid_ln_1_bwd")
    g_o = mm_tn(merged, dh1, "mm_grad_out_proj")
    dmerged = mm_nt(dh1, w_oo, BF16, "mm_dmerged")
    dya, dyb, dproj = merge_gates_bwd(dmerged, ya, yb, proj)
    g_ba_full = mm_tn(o_a, dya, "mm_grad_branch_a")
    g_bb_full = mm_tn(o_b, dyb, "mm_grad_branch_b")
    doa = mm_nt(dya, w_ba, F32, "mm_doa")
    dob = mm_nt(dyb, w_bb, F32, "mm_dob")
    my_core = lax.axis_index("c")

    def by_core(blocks, rows, slots):
        contrib = jnp.concatenate([jnp.pad(b.reshape(N_DEV, -1, D), ((0, 0), (0, p - r), (0, 0)))
                                   for b, r, p in zip(blocks, rows, slots)], axis=1)
        split = contrib.reshape(N_CHIP, 2, contrib.shape[1], D).transpose(1, 0, 2, 3)
        return (lax.dynamic_index_in_dim(split, my_core, 0, keepdims=False),
                lax.dynamic_index_in_dim(split, 1 - my_core, 0, keepdims=False))

    keep_e, give_e = by_core([g_ba_full, g_bb_full, g_o, g_gu_t[:D_FF], g_gu_t[D_FF_PAD:D_FF_PAD + D_FF], g_dn[:D_FF]],
                             shard_rows[1:], slot_rows[1:])
    dproj, dlb, dgn, got_e = hgrn_bwd(proj, lb, hgrn_gnorm, o_raw, doa, st_a, give_e, dproj)
    chip_e = sum_pair(keep_e.reshape(-1, D), got_e.reshape(-1, D), "sum_grads_rest_chip").reshape(keep_e.shape)
    dxs, dbm, dcm, dproj, ddt, dwn, dalog, dbias, ddsk, parts_e = ssd_bwd(proj, xc, alog4, bias4, dskip4, ssm_norm,
                                                                          expand, dob, st_b, chip_e, dproj)
    dxc = jnp.concatenate([dxs, dbm, dcm], axis=1)
    dproj, dcw, dcb = conv_bwd(proj, dxc, conv_w, ssm_conv_b, dproj)
    dproj = dt_fill(ddt, dproj)
    g_in_t = mm_tn(dproj, u1, "mm_grad_in_proj")
    keep_l = _grad_in_blocks(g_in_t, my_core, slot_rows[0])
    give_l = _grad_in_blocks(g_in_t, 1 - my_core, slot_rows[0])
    got_l = exchange_sibling(give_l, "exchange_grad_in_sibling")
    chip_l = sum_pair(keep_l.reshape(-1, D), got_l.reshape(-1, D), "sum_grad_in_chip").reshape(keep_l.shape)
    du1, parts_l = mm_nn_exchange(dproj, w_in_t, F32, chip_l, "mm_du1")
    dx, acc1 = ln_modulate_bwd(xt, du1, mod6, 1, dx_part, "ln_modulate_1_bwd")
    gw_in = sum_parts(parts_l, "sum_grad_in")[:shard_rows[0]].T
    g_rows = sum_parts(parts_e, "sum_grads_rest")
    gw_ba, gw_bb, gw_o, gw_fg, gw_fu, gw_fd = (g_rows[o - slot_rows[0]:o - slot_rows[0] + r]
                                               for o, r in zip(row_offs[1:], shard_rows[1:]))
    gw_fg, gw_fu = gw_fg.T, gw_fu.T

    dmod = jnp.concatenate([acc1[1:2], acc1[0:1], acc2[0:1], acc3[1:2], acc3[0:1], acc4[0:1]], axis=1)
    small_fields = [dmod, acc4[3:4, :128], dlb, dgn, dcw[:CONV_TAPS], dcb, dbias, dalog, ddsk, dwn,
                    acc2[1:2], acc2[2:3], acc4[1:2], acc4[2:3]]
    small_out, offs = _pack(small_fields)
    small_sum_in = allgather_vmem(small_out, "allgather_small_grads")
    gsum, g_lb = reduce_small(small_sum_in, hgrn_lb, offs[2])
    (g_bada, loss_row, _, g_gn, g_cw_full, g_cb, g_bias4, g_alog4, g_dsk4, g_wn, g_l1g, g_l1b, g_l2g, g_l2b) = _unpack(
        gsum, offs, [(1, 6 * D), (1, 128), (1, D), (1, HK), (CONV_TAPS, CONV_DIM), (1, CONV_DIM),
                     (N_GROUPS, N_STATE), (N_GROUPS, N_STATE), (N_GROUPS, N_STATE), (1, B_INNER),
                     (1, D), (1, D), (1, D), (1, D)])
    loss = loss_row[0, 0]
    g_cw = lax.dynamic_slice(g_cw_full, (0, me * conv_cols), (CONV_TAPS, conv_cols))[None]
    g_dtb = g_bias4[:, :HEADS_PER_GROUP].reshape(1, 32)
    g_alog = g_alog4[:, :HEADS_PER_GROUP].reshape(1, 32)
    g_dsk = g_dsk4[:, :HEADS_PER_GROUP].reshape(1, 32)

    dmod_all = small_sum_in[:, 0, offs[0]:offs[0] + 6 * D]
    dmod_cols = lax.dynamic_slice(dmod_all, (0, me * ada_cols), (N_DEV, ada_cols))
    gw_ada = ada_weight_grad(c_all, dmod_cols)

    big = [("ada", w_ada[0], gw_ada, m_w_ada[0], v_w_ada[0]), ("in", w_in[0], gw_in, m_w_in[0], v_w_in[0]),
           ("branch_a", w_branch_a[0], gw_ba, m_w_branch_a[0], v_w_branch_a[0]),
           ("branch_b", w_branch_b[0], gw_bb, m_w_branch_b[0], v_w_branch_b[0]),
           ("o", w_o[0], gw_o, m_w_o[0], v_w_o[0]),
           ("ffn_gate", w_ffn_gate[0], gw_fg, m_w_ffn_gate[0], v_w_ffn_gate[0]),
           ("ffn_up", w_ffn_up[0], gw_fu, m_w_ffn_up[0], v_w_ffn_up[0]),
           ("ffn_down", w_ffn_down[0], gw_fd, m_w_ffn_down[0], v_w_ffn_down[0])]
    big_out = {}
    for nm, w_, g_, m_, v_ in big:
        d_, m2_, v2_ = adamw(w_, g_, m_, v_, "adamw_" + nm)
        big_out[nm] = (g_[None], d_[None], m2_[None], v2_[None])

    small_w = [b_ada, hgrn_lb, hgrn_gnorm, ssm_conv_w, ssm_conv_b, ssm_dt_bias, ssm_a_log, ssm_d, ssm_norm,
               ln1_g, ln1_b, ln2_g, ln2_b]
    small_g = [g_bada, g_lb, g_gn, g_cw, g_cb, g_dtb, g_alog, g_dsk, g_wn, g_l1g, g_l1b, g_l2g, g_l2b]
    small_m = [m_b_ada, m_hgrn_lb, m_hgrn_gnorm, m_ssm_conv_w, m_ssm_conv_b, m_ssm_dt_bias, m_ssm_a_log, m_ssm_d,
               m_ssm_norm, m_ln1_g, m_ln1_b, m_ln2_g, m_ln2_b]
    small_v = [v_b_ada, v_hgrn_lb, v_hgrn_gnorm, v_ssm_conv_w, v_ssm_conv_b, v_ssm_dt_bias, v_ssm_a_log, v_ssm_d,
               v_ssm_norm, v_ln1_g, v_ln1_b, v_ln2_g, v_ln2_b]
    shapes = [a.shape for a in small_w]
    small_g = [g_.reshape(s) for g_, s in zip(small_g, shapes)]
    pw, poffs = _pack(small_w)
    pg, _ = _pack(small_g)
    pm, _ = _pack(small_m)
    pv, _ = _pack(small_v)
    pd, pm2, pv2 = adamw(pw, pg, pm, pv, "adamw_small")
    s_d, s_m, s_v = (_unpack(p, poffs, shapes) for p in (pd, pm2, pv2))
    (sn_bada, sn_lb, sn_gn, sn_cw, sn_cb, sn_dtb, sn_alog, sn_dsk, sn_wn, sn_l1g, sn_l1b, sn_l2g, sn_l2b) = range(13)

    def order(kind):
        sm = [small_g, s_d, s_m, s_v][kind]
        bg = lambda nm: big_out[nm][kind]
        return [bg("ada"), sm[sn_bada], bg("in"), sm[sn_lb], sm[sn_gn], sm[sn_cw], sm[sn_cb], sm[sn_dtb], sm[sn_alog],
                sm[sn_dsk], sm[sn_wn], bg("branch_a"), bg("branch_b"), bg("o"), sm[sn_l1g], sm[sn_l1b],
                bg("ffn_gate"), bg("ffn_up"), bg("ffn_down"), sm[sn_l2g], sm[sn_l2b]]

    return (loss, dx[None], *order(0), *order(1), *order(2), *order(3))
```

```python
import numpy as np
import jax
import jax.numpy as jnp
from jax import lax
from jax.experimental import pallas as pl
from jax.experimental.pallas import tpu as pltpu

F32 = jnp.float32
BF16 = jnp.bfloat16
HI = lax.Precision.HIGHEST

N_DEV = 8
D = 1024
N_HEADS_A = 8
HK = 128
CHUNK = 64
SSD_CHUNK = 128
N_GROUPS = 4
HEADS_PER_GROUP = 8
HEAD_P = 64
N_STATE = 128
GROUP_W = HEADS_PER_GROUP * HEAD_P
B_INNER = 2048
CONV_DIM = 3072
D_FF = 2816
IN_DIM = 11296
N_PROJ = 12288
ALPHA = 2.0 ** 0.25
LN_EPS = 1e-5
RMS_EPS = 1e-6
Q_SCALE = 128 ** -0.5
EXP_CLIP = 80.0
ADAM_LR, ADAM_B1, ADAM_B2, ADAM_EPS, ADAM_WD, ADAM_STEP = 0.001, 0.9, 0.999, 1e-8, 0.01, 10
VMEM_LIMIT = 48 * 1024 * 1024
TOKEN_BLOCK = 512
ROW_TILE = 256
FFN_ROW_TILE = 128
MM_ROW_TILE = 1024
MM_TOKEN_TILE = 4096
MM_K_TILE = 3072
MM_COL_TILE = 1408
HGRN_HEADS_PER_STEP = 4
CHUNK_UNROLL = 8
MESH_ID = pl.DeviceIdType.MESH

NT_DIMS = (((1,), (1,)), ((), ()))
TN_DIMS = (((0,), (0,)), ((), ()))


def _cparams(sem=None):
    return pltpu.CompilerParams(dimension_semantics=sem, vmem_limit_bytes=VMEM_LIMIT)


def _sigmoid(x):
    return 1.0 / (1.0 + jnp.exp(-x))


def _dsilu(x, s):
    return s * (1.0 + x * (1.0 - s))


def _nt(a, b, precision=None):
    return lax.dot_general(a, b, NT_DIMS, precision=precision, preferred_element_type=F32)


def _tn(a, b, precision=None):
    return lax.dot_general(a, b, TN_DIMS, precision=precision, preferred_element_type=F32)


def _nn(a, b, precision=None):
    return jnp.dot(a, b, precision=precision, preferred_element_type=F32)


def _split(x, pieces):
    out = []
    for i in range(pieces):
        p = x.astype(BF16)
        out.append(p)
        if i + 1 < pieces:
            x = x - p.astype(F32)
    return out


def _sel(dot, x, sel01, pieces, x_first=True):
    acc = None
    for p in _split(x, pieces):
        term = dot(p, sel01) if x_first else dot(sel01, p)
        acc = term if acc is None else acc + term
    return acc


def _ln(x):
    mu = jnp.mean(x, axis=-1, keepdims=True)
    xc = x - mu
    rstd = lax.rsqrt(jnp.mean(xc * xc, axis=-1, keepdims=True) + LN_EPS)
    return xc * rstd, rstd


def _ln_bwd(dxh, xh, rstd):
    return rstd * (dxh - jnp.mean(dxh, axis=-1, keepdims=True) - xh * jnp.mean(dxh * xh, axis=-1, keepdims=True))


def _colsum(x):
    return jnp.sum(x, axis=0, keepdims=True)


def _tri(n, upper=False):
    r = lax.broadcasted_iota(jnp.int32, (n, n), 0)
    c = lax.broadcasted_iota(jnp.int32, (n, n), 1)
    return (c >= r) if upper else (r >= c)


def _my_pos():
    return lax.axis_index("x"), lax.axis_index("y"), lax.axis_index("c")


def _peer(pos, k):
    x, y, c = pos
    return (x ^ ((k >> 2) & 1), y ^ ((k >> 1) & 1), c ^ (k & 1))


def _flat(pos):
    return 4 * pos[0] + 2 * pos[1] + pos[2]


def allgather_vmem(v, name):
    n = v.shape[1]

    def body(v_ref, o_ref, send_sems, recv_sems, local_sem):
        me = _my_pos()
        mine = pltpu.make_async_copy(v_ref, o_ref.at[_flat(me)], local_sem)
        mine.start()
        sends = []
        for k in range(1, N_DEV):
            peer = _peer(me, k)
            cp = pltpu.make_async_remote_copy(v_ref, o_ref.at[_flat(me)], send_sems.at[k - 1], recv_sems.at[k - 1],
                                              device_id=peer, device_id_type=MESH_ID)
            cp.start()
            sends.append(cp)
        for k in range(1, N_DEV):
            peer = _peer(me, k)
            pltpu.make_async_remote_copy(v_ref, o_ref.at[_flat(peer)], send_sems.at[k - 1], recv_sems.at[k - 1],
                                         device_id=peer, device_id_type=MESH_ID).wait_recv()
        for cp in sends:
            cp.wait_send()
        mine.wait()

    return pl.pallas_call(
        body, name=name,
        out_shape=jax.ShapeDtypeStruct((N_DEV, 1, n), F32),
        in_specs=[pl.BlockSpec(memory_space=pltpu.VMEM)],
        out_specs=pl.BlockSpec(memory_space=pltpu.VMEM),
        scratch_shapes=[pltpu.SemaphoreType.DMA((N_DEV - 1,)), pltpu.SemaphoreType.DMA((N_DEV - 1,)),
                        pltpu.SemaphoreType.DMA],
        compiler_params=_cparams(),
    )(v)


def ada_modulation(c_all, w_ada_s, b_ada_r):
    ncol = w_ada_s.shape[1]

    def body(c_ref, w_ref, b_ref, o_ref, part_ref, send_sems, recv_sems):
        me = _my_pos()
        cval = c_ref[...]
        cond = cval * _sigmoid(cval)
        part = _nn(cond, w_ref[...], HI)
        for r in range(N_DEV):
            part_ref[r] = part[r:r + 1, :]
        sends = []
        for k in range(1, N_DEV):
            peer = _peer(me, k)
            cp = pltpu.make_async_remote_copy(part_ref.at[_flat(peer)], o_ref.at[_flat(me)], send_sems.at[k - 1],
                                              recv_sems.at[k - 1], device_id=peer, device_id_type=MESH_ID)
            cp.start()
            sends.append(cp)
        o_ref[_flat(me)] = part_ref[_flat(me)]
        for k in range(1, N_DEV):
            peer = _peer(me, k)
            pltpu.make_async_remote_copy(part_ref.at[_flat(peer)], o_ref.at[_flat(peer)], send_sems.at[k - 1],
                                         recv_sems.at[k - 1], device_id=peer, device_id_type=MESH_ID).wait_recv()
        for cp in sends:
            cp.wait_send()
        o_ref[...] = o_ref[...] + b_ref[...]

    return pl.pallas_call(
        body, name="ada_modulation",
        out_shape=jax.ShapeDtypeStruct((N_DEV, 1, ncol), F32),
        in_specs=[pl.BlockSpec(memory_space=pltpu.VMEM)] * 3,
        out_specs=pl.BlockSpec(memory_space=pltpu.VMEM),
        scratch_shapes=[pltpu.VMEM((N_DEV, 1, ncol), F32), pltpu.SemaphoreType.DMA((N_DEV - 1,)),
                        pltpu.SemaphoreType.DMA((N_DEV - 1,))],
        compiler_params=_cparams(),
    )(c_all, w_ada_s, b_ada_r)


def allgather_hbm(shard, name):
    def body(x_ref, out_ref, send_sems, recv_sems, local_sem):
        x, y, c = _my_pos()
        me, sibling = (x, y, c), (x, y, 1 - c)
        chips = [(1 - x, y), (x, 1 - y), (1 - x, 1 - y)]

        def slot(pos):
            return out_ref.at[_flat(pos)]

        def copy(k, block, to, src=None):
            return pltpu.make_async_remote_copy(slot(block) if src is None else src, slot(block), send_sems.at[k],
                                                recv_sems.at[k], device_id=to, device_id_type=MESH_ID)

        mine = pltpu.make_async_copy(x_ref, slot(me), local_sem)
        mine.start()
        first = [copy(0, me, sibling, src=x_ref)]
        first += [copy(1 + j, me, (*chip, c), src=x_ref) for j, chip in enumerate(chips)]
        for cp in first:
            cp.start()
        passed = [copy(4 + j, (*chip, c), sibling) for j, chip in enumerate(chips)]
        for j, chip in enumerate(chips):
            copy(1 + j, (*chip, c), me).wait_recv()
            passed[j].start()
        copy(0, sibling, me).wait_recv()
        for j, chip in enumerate(chips):
            copy(4 + j, (*chip, 1 - c), me).wait_recv()
        for cp in first + passed:
            cp.wait_send()
        mine.wait()

    return pl.pallas_call(
        body, name=name,
        out_shape=jax.ShapeDtypeStruct((N_DEV,) + shard.shape, shard.dtype),
        in_specs=[pl.BlockSpec(memory_space=pl.ANY)],
        out_specs=pl.BlockSpec(memory_space=pl.ANY),
        scratch_shapes=[pltpu.SemaphoreType.DMA((N_DEV - 1,)), pltpu.SemaphoreType.DMA((N_DEV - 1,)),
                        pltpu.SemaphoreType.DMA],
        compiler_params=_cparams(),
    )(shard)


N_CHIP = N_DEV // 2
SIBLING_SEMS = [pltpu.SemaphoreType.DMA, pltpu.SemaphoreType.DMA]
CHIP_SEMS = [pltpu.SemaphoreType.DMA((N_CHIP - 1,)), pltpu.SemaphoreType.DMA((N_CHIP - 1,)), pltpu.SemaphoreType.DMA]


def _sibling_exchange(s_ref, o_ref, send_sem, recv_sem):
    x, y, c = _my_pos()
    cp = pltpu.make_async_remote_copy(s_ref, o_ref, send_sem, recv_sem, device_id=(x, y, 1 - c), device_id_type=MESH_ID)
    return cp.start, cp.wait


def _chip_exchange(p_ref, o_ref, send_sems, recv_sems, local_sem):
    x, y, c = _my_pos()
    my_chip = 2 * x + y
    mine = pltpu.make_async_copy(p_ref.at[my_chip], o_ref.at[my_chip], local_sem)
    peers = [(x ^ (k >> 1), y ^ (k & 1)) for k in range(1, N_CHIP)]
    sends = [pltpu.make_async_remote_copy(p_ref.at[2 * px + py], o_ref.at[my_chip], send_sems.at[k], recv_sems.at[k],
                                          device_id=(px, py, c), device_id_type=MESH_ID)
             for k, (px, py) in enumerate(peers)]
    recvs = [pltpu.make_async_remote_copy(p_ref.at[2 * px + py], o_ref.at[2 * px + py], send_sems.at[k], recv_sems.at[k],
                                          device_id=(px, py, c), device_id_type=MESH_ID)
             for k, (px, py) in enumerate(peers)]

    def start():
        mine.start()
        for cp in sends:
            cp.start()

    def wait():
        for cp in recvs:
            cp.wait_recv()
        for cp in sends:
            cp.wait_send()
        mine.wait()

    return start, wait


def exchange_sibling(send, name):
    def body(s_ref, o_ref, send_sem, recv_sem):
        start, wait = _sibling_exchange(s_ref, o_ref, send_sem, recv_sem)
        start()
        wait()

    return pl.pallas_call(
        body, name=name,
        out_shape=jax.ShapeDtypeStruct(send.shape, send.dtype),
        in_specs=[pl.BlockSpec(memory_space=pl.ANY)],
        out_specs=pl.BlockSpec(memory_space=pl.ANY),
        scratch_shapes=SIBLING_SEMS,
        compiler_params=_cparams(),
    )(send)


LANES = 128


def _k_tile(kdim, unit=LANES):
    for cand in range(MM_K_TILE - MM_K_TILE % unit, 0, -unit):
        if kdim % cand == 0:
            return cand
    return kdim


def _lane_tile(n, cap):
    for cand in range(cap - cap % LANES, 0, -LANES):
        if n % cand == 0:
            return cand
    return n


def mm_nn(a, b, out_dtype, name):
    m, kdim = a.shape
    n = b.shape[1]
    tm, tn, tk = min(MM_ROW_TILE, m), _lane_tile(n, MM_COL_TILE), _k_tile(kdim)
    nk = kdim // tk

    def body(a_ref, b_ref, o_ref, acc_ref):
        p = _nn(a_ref[...], b_ref[...])
        if nk == 1:
            o_ref[...] = p.astype(o_ref.dtype)
        else:
            k = pl.program_id(2)

            @pl.when(k == 0)
            def _():
                acc_ref[...] = p

            @pl.when(k > 0)
            def _():
                acc_ref[...] += p

            @pl.when(k == nk - 1)
            def _():
                o_ref[...] = acc_ref[...].astype(o_ref.dtype)

    return pl.pallas_call(
        body, name=name, grid=(n // tn, m // tm, nk),
        out_shape=jax.ShapeDtypeStruct((m, n), out_dtype),
        in_specs=[pl.BlockSpec((tm, tk), lambda j, i, k: (i, k)), pl.BlockSpec((tk, tn), lambda j, i, k: (k, j))],
        out_specs=pl.BlockSpec((tm, tn), lambda j, i, k: (i, j)),
        scratch_shapes=[pltpu.VMEM((tm, tn), F32)],
        compiler_params=_cparams(("parallel", "parallel", "arbitrary")),
    )(a, b)


def mm_nt(a, b, out_dtype, name):
    m, kdim = a.shape
    n = b.shape[0]
    tm, tn, tk = min(MM_ROW_TILE, m), _lane_tile(n, MM_COL_TILE), _k_tile(kdim)
    nk = kdim // tk

    def body(a_ref, b_ref, o_ref, acc_ref):
        p = _nt(a_ref[...], b_ref[...])
        if nk == 1:
            o_ref[...] = p.astype(o_ref.dtype)
        else:
            k = pl.program_id(2)

            @pl.when(k == 0)
            def _():
                acc_ref[...] = p

            @pl.when(k > 0)
            def _():
                acc_ref[...] += p

            @pl.when(k == nk - 1)
            def _():
                o_ref[...] = acc_ref[...].astype(o_ref.dtype)

    return pl.pallas_call(
        body, name=name, grid=(n // tn, m // tm, nk),
        out_shape=jax.ShapeDtypeStruct((m, n), out_dtype),
        in_specs=[pl.BlockSpec((tm, tk), lambda j, i, k: (i, k)), pl.BlockSpec((tn, tk), lambda j, i, k: (j, k))],
        out_specs=pl.BlockSpec((tm, tn), lambda j, i, k: (i, j)),
        scratch_shapes=[pltpu.VMEM((tm, tn), F32)],
        compiler_params=_cparams(("parallel", "parallel", "arbitrary")),
    )(a, b)


def mm_nn_exchange(a, b, out_dtype, part, name):
    kblocks, m, kb = a.shape
    kdim = kblocks * kb
    n = b.shape[1]
    tm, tn, tk = min(MM_ROW_TILE, m), _lane_tile(n, MM_COL_TILE), _k_tile(kdim)
    gn, gm, nk = n // tn, m // tm, kdim // tk
    per_step = tk // kb

    def body(a_ref, b_ref, part_ref, o_ref, parts_ref, acc_ref, send_sems, recv_sems, local_sem):
        j, i, k = pl.program_id(0), pl.program_id(1), pl.program_id(2)
        xchg_start, xchg_wait = _chip_exchange(part_ref, parts_ref, send_sems, recv_sems, local_sem)

        @pl.when((j == 0) & (i == 0) & (k == 0))
        def _():
            xchg_start()

        p = _nn(a_ref[0], b_ref[0:kb, :])
        for c in range(1, per_step):
            p = p + _nn(a_ref[c], b_ref[c * kb:(c + 1) * kb, :])

        @pl.when(k == 0)
        def _():
            acc_ref[...] = p

        @pl.when(k > 0)
        def _():
            acc_ref[...] += p

        @pl.when(k == nk - 1)
        def _():
            o_ref[...] = acc_ref[...].astype(o_ref.dtype)

        @pl.when((j == gn - 1) & (i == gm - 1) & (k == nk - 1))
        def _():
            xchg_wait()

    hbm = pl.BlockSpec(memory_space=pl.ANY)
    return pl.pallas_call(
        body, name=name, grid=(gn, gm, nk),
        out_shape=[jax.ShapeDtypeStruct((m, n), out_dtype), jax.ShapeDtypeStruct(part.shape, part.dtype)],
        in_specs=[pl.BlockSpec((per_step, tm, kb), lambda j, i, k: (k, i, 0)),
                  pl.BlockSpec((tk, tn), lambda j, i, k: (k, j)), hbm],
        out_specs=[pl.BlockSpec((tm, tn), lambda j, i, k: (i, j)), hbm],
        scratch_shapes=[pltpu.VMEM((tm, tn), F32)] + CHIP_SEMS,
        compiler_params=_cparams(("arbitrary", "arbitrary", "arbitrary")),
    )(a, b, part)


def mm_nt_gather(a, b, out_dtype, shard, name):
    m, kdim = a.shape
    n = b.shape[0]
    tm, tn = min(MM_ROW_TILE, m), 1024
    assert kdim == 1024
    gj = m // tm
    nsteps = (n // tn) * gj
    forward_step = max(nsteps - 3, 0)

    def body(a_ref, b_ref, x_ref, o_ref, g_ref, send_sems, recv_sems, local_sem):
        step = pl.program_id(0) * gj + pl.program_id(1)
        x, y, c = _my_pos()
        me, sibling = (x, y, c), (x, y, 1 - c)
        chips = [(1 - x, y), (x, 1 - y), (1 - x, 1 - y)]

        def slot(pos):
            return g_ref.at[_flat(pos)]

        def copy(k, block, to, src=None):
            return pltpu.make_async_remote_copy(slot(block) if src is None else src, slot(block), send_sems.at[k],
                                                recv_sems.at[k], device_id=to, device_id_type=MESH_ID)

        mine = pltpu.make_async_copy(x_ref, slot(me), local_sem)
        first = [copy(0, me, sibling, src=x_ref)]
        first += [copy(1 + j, me, (*chip, c), src=x_ref) for j, chip in enumerate(chips)]
        passed = [copy(4 + j, (*chip, c), sibling) for j, chip in enumerate(chips)]

        @pl.when(step == 0)
        def _():
            mine.start()
            for cp in first:
                cp.start()

        o_ref[...] = _nt(a_ref[...], b_ref[...]).astype(o_ref.dtype)

        @pl.when(step == forward_step)
        def _():
            for j, chip in enumerate(chips):
                copy(1 + j, (*chip, c), me).wait_recv()
                passed[j].start()

        @pl.when(step == nsteps - 1)
        def _():
            copy(0, sibling, me).wait_recv()
            for j, chip in enumerate(chips):
                copy(4 + j, (*chip, 1 - c), me).wait_recv()
            for cp in first + passed:
                cp.wait_send()
            mine.wait()

    return pl.pallas_call(
        body, name=name, grid=(n // tn, gj),
        out_shape=[jax.ShapeDtypeStruct((m, n), out_dtype), jax.ShapeDtypeStruct((N_DEV,) + shard.shape, shard.dtype)],
        in_specs=[pl.BlockSpec((tm, kdim), lambda j, i: (i, 0)), pl.BlockSpec((tn, kdim), lambda j, i: (j, 0)),
                  pl.BlockSpec(memory_space=pl.ANY)],
        out_specs=[pl.BlockSpec((tm, tn), lambda j, i: (i, j)), pl.BlockSpec(memory_space=pl.ANY)],
        scratch_shapes=[pltpu.SemaphoreType.DMA((N_DEV - 1,)), pltpu.SemaphoreType.DMA((N_DEV - 1,)),
                        pltpu.SemaphoreType.DMA],
        compiler_params=_cparams(("arbitrary", "arbitrary")),
    )(a, b, shard)


def mm_tn(a, b, name):
    tt, tn = min(MM_TOKEN_TILE, b.shape[0]), _lane_tile(b.shape[1], MM_COL_TILE)
    tka = _lane_tile(a.shape[0] * a.shape[2] if a.ndim == 3 else a.shape[1], 1024)
    if a.ndim == 3:
        t, ka = a.shape[1], a.shape[0] * a.shape[2]
        a_spec = pl.BlockSpec((None, tt, tka), lambda i, j, s: (i, s, 0))
    else:
        t, ka = a.shape
        a_spec = pl.BlockSpec((tt, tka), lambda i, j, s: (s, i))
    n = b.shape[1]
    nt = t // tt

    def body(a_ref, b_ref, o_ref, *acc):
        p = _tn(a_ref[...], b_ref[...])
        if nt == 1:
            o_ref[...] = p.astype(o_ref.dtype)
        else:
            acc_ref, s = acc[0], pl.program_id(2)

            @pl.when(s == 0)
            def _():
                acc_ref[...] = p

            @pl.when(s > 0)
            def _():
                acc_ref[...] += p

            @pl.when(s == nt - 1)
            def _():
                o_ref[...] = acc_ref[...].astype(o_ref.dtype)

    return pl.pallas_call(
        body, name=name, grid=(ka // tka, n // tn, nt),
        out_shape=jax.ShapeDtypeStruct((ka, n), BF16),
        in_specs=[a_spec, pl.BlockSpec((tt, tn), lambda i, j, s: (s, j))],
        out_specs=pl.BlockSpec((tka, tn), lambda i, j, s: (i, j)),
        scratch_shapes=[] if nt == 1 else [pltpu.VMEM((tka, tn), F32)],
        compiler_params=_cparams(("parallel", "parallel", "arbitrary")),
    )(a, b)


def _tile(t, cap):
    return min(cap, t)


def ln_modulate(x, mod6, shift_row, scale_row, name):
    t = x.shape[0]
    tm = _tile(t, ROW_TILE)

    def body(x_ref, mod_ref, o_ref):
        xh, _ = _ln(x_ref[...])
        sc = mod_ref[scale_row:scale_row + 1, :]
        sh = mod_ref[shift_row:shift_row + 1, :]
        o_ref[...] = (xh * (1.0 + sc) + sh).astype(BF16)

    return pl.pallas_call(
        body, name=name, grid=(t // tm,),
        out_shape=jax.ShapeDtypeStruct((t, D), BF16),
        in_specs=[pl.BlockSpec((tm, D), lambda i: (i, 0)), pl.BlockSpec((6, D), lambda i: (0, 0))],
        out_specs=pl.BlockSpec((tm, D), lambda i: (i, 0)),
        compiler_params=_cparams(("parallel",)),
    )(x, mod6)


def resid_ln(x, h, mod6, gate_row, ln_g, ln_b, name):
    t = x.shape[0]
    tm = _tile(t, ROW_TILE)

    def body(x_ref, h_ref, mod_ref, g_ref, b_ref, o_ref):
        r = ALPHA * x_ref[...] + mod_ref[gate_row:gate_row + 1, :] * h_ref[...]
        rh, _ = _ln(r)
        o_ref[...] = rh * g_ref[...] + b_ref[...]

    row = pl.BlockSpec((tm, D), lambda i: (i, 0))
    vec = pl.BlockSpec((1, D), lambda i: (0, 0))
    return pl.pallas_call(
        body, name=name, grid=(t // tm,),
        out_shape=jax.ShapeDtypeStruct((t, D), F32),
        in_specs=[row, row, pl.BlockSpec((6, D), lambda i: (0, 0)), vec, vec],
        out_specs=row,
        compiler_params=_cparams(("parallel",)),
    )(x, h, mod6, ln_g, ln_b)


def resid_ln_bwd(x, h, mod6, gate_row, ln_g, ln_b, cot, with_loss, name):
    t = x.shape[0]
    tm = _tile(t, ROW_TILE)

    def body(x_ref, h_ref, mod_ref, g_ref, b_ref, c_ref, dh_ref, dx_ref, acc_ref):
        @pl.when(pl.program_id(0) == 0)
        def _():
            acc_ref[...] = jnp.zeros_like(acc_ref)

        gate = mod_ref[gate_row:gate_row + 1, :]
        hv = h_ref[...]
        r = ALPHA * x_ref[...] + gate * hv
        rh, rstd = _ln(r)
        lng = g_ref[...]
        if with_loss:
            diff = rh * lng + b_ref[...] - c_ref[...]
            dxo = diff * (1.0 / D)
            lsum = jnp.sum(_colsum(diff * diff), axis=-1, keepdims=True) * (0.5 / D)
            acc_ref[3:4, :] += jnp.broadcast_to(lsum, (1, D))
        else:
            dxo = c_ref[...]
        acc_ref[1:2, :] += _colsum(dxo * rh)
        acc_ref[2:3, :] += _colsum(dxo)
        dr = _ln_bwd(dxo * lng, rh, rstd)
        acc_ref[0:1, :] += _colsum(dr * hv)
        dh_ref[...] = (gate * dr).astype(BF16)
        dx_ref[...] = ALPHA * dr

    row = pl.BlockSpec((tm, D), lambda i: (i, 0))
    vec = pl.BlockSpec((1, D), lambda i: (0, 0))
    return pl.pallas_call(
        body, name=name, grid=(t // tm,),
        out_shape=[jax.ShapeDtypeStruct((t, D), BF16), jax.ShapeDtypeStruct((t, D), F32),
                   jax.ShapeDtypeStruct((8, D), F32)],
        in_specs=[row, row, pl.BlockSpec((6, D), lambda i: (0, 0)), vec, vec, row],
        out_specs=[row, row, pl.BlockSpec((8, D), lambda i: (0, 0))],
        compiler_params=_cparams(("arbitrary",)),
    )(x, h, mod6, ln_g, ln_b, cot)


def ln_modulate_bwd(x, du, mod6, scale_row, dx_part, name):
    t = x.shape[0]
    tm = _tile(t, ROW_TILE)

    def body(x_ref, du_ref, mod_ref, dp_ref, dx_ref, acc_ref):
        @pl.when(pl.program_id(0) == 0)
        def _():
            acc_ref[...] = jnp.zeros_like(acc_ref)

        xh, rstd = _ln(x_ref[...])
        du_v = du_ref[...]
        sc = mod_ref[scale_row:scale_row + 1, :]
        acc_ref[0:1, :] += _colsum(du_v * xh)
        acc_ref[1:2, :] += _colsum(du_v)
        dx_ref[...] = dp_ref[...] + _ln_bwd(du_v * (1.0 + sc), xh, rstd)

    row = pl.BlockSpec((tm, D), lambda i: (i, 0))
    return pl.pallas_call(
        body, name=name, grid=(t // tm,),
        out_shape=[jax.ShapeDtypeStruct((t, D), F32), jax.ShapeDtypeStruct((8, D), F32)],
        in_specs=[row, row, pl.BlockSpec((6, D), lambda i: (0, 0)), row],
        out_specs=[row, pl.BlockSpec((8, D), lambda i: (0, 0))],
        compiler_params=_cparams(("arbitrary",)),
    )(x, du, mod6, dx_part)


def merge_gates(ya, yb, proj):
    t = ya.shape[0]
    tm = _tile(t, ROW_TILE)

    def body(ya_ref, yb_ref, ga_ref, gb_ref, o_ref):
        o_ref[...] = (_sigmoid(ga_ref[...]) * ya_ref[...].astype(F32) +
                      _sigmoid(gb_ref[...]) * yb_ref[...].astype(F32)).astype(BF16)

    row = pl.BlockSpec((tm, D), lambda i: (i, 0))
    return pl.pallas_call(
        body, name="merge_gates", grid=(t // tm,),
        out_shape=jax.ShapeDtypeStruct((t, D), BF16),
        in_specs=[row, row, pl.BlockSpec((tm, D), lambda i: (i, GATE_BLOCK0)),
                  pl.BlockSpec((tm, D), lambda i: (i, GATE_BLOCK0 + 1))],
        out_specs=row,
        compiler_params=_cparams(("parallel",)),
    )(ya, yb, proj, proj)


def merge_gates_bwd(dm, ya, yb, proj):
    t = ya.shape[0]
    tm = _tile(t, ROW_TILE)

    def body(dm_ref, ya_ref, yb_ref, ga_ref, gb_ref, dya_ref, dyb_ref, dp_ref):
        dmv = dm_ref[...].astype(F32)
        sa = _sigmoid(ga_ref[...])
        sb = _sigmoid(gb_ref[...])
        dya_ref[...] = (dmv * sa).astype(BF16)
        dyb_ref[...] = (dmv * sb).astype(BF16)
        dp_ref[0] = (dmv * ya_ref[...].astype(F32) * sa * (1.0 - sa)).astype(BF16)
        dp_ref[1] = (dmv * yb_ref[...].astype(F32) * sb * (1.0 - sb)).astype(BF16)

    row = pl.BlockSpec((tm, D), lambda i: (i, 0))
    return pl.pallas_call(
        body, name="merge_gates_bwd", grid=(t // tm,),
        out_shape=[jax.ShapeDtypeStruct((t, D), BF16)] * 2 + [jax.ShapeDtypeStruct((N_PROJ // D, t, D), BF16)],
        in_specs=[row, row, row, pl.BlockSpec((tm, D), lambda i: (i, GATE_BLOCK0)),
                  pl.BlockSpec((tm, D), lambda i: (i, GATE_BLOCK0 + 1))],
        out_specs=[row, row, pl.BlockSpec((2, tm, D), lambda i: (GATE_BLOCK0 // 2, i, 0))],
        compiler_params=_cparams(("parallel",)),
    )(dm, ya, yb, proj, proj)


FF_CHUNK = 1408


def swiglu_act(gu):
    t = gu.shape[0]
    tm = _tile(t, FFN_ROW_TILE)

    def body(gu_ref, o_ref):
        for j in range(D_FF // FF_CHUNK):
            cs = slice(j * FF_CHUNK, (j + 1) * FF_CHUNK)
            g = gu_ref[:, cs].astype(F32)
            u = gu_ref[:, D_FF + j * FF_CHUNK:D_FF + (j + 1) * FF_CHUNK].astype(F32)
            o_ref[:, cs] = (g * _sigmoid(g) * u).astype(BF16)

    return pl.pallas_call(
        body, name="swiglu_act", grid=(t // tm,),
        out_shape=jax.ShapeDtypeStruct((t, D_FF), BF16),
        in_specs=[pl.BlockSpec((tm, 2 * D_FF), lambda i: (i, 0))],
        out_specs=pl.BlockSpec((tm, D_FF), lambda i: (i, 0)),
        compiler_params=_cparams(("parallel",)),
    )(gu)


def swiglu_act_bwd(gu, dact):
    t = gu.shape[0]
    tm = _tile(t, FFN_ROW_TILE)

    def body(gu_ref, da_ref, o_ref):
        for j in range(D_FF // FF_CHUNK):
            cs = slice(j * FF_CHUNK, (j + 1) * FF_CHUNK)
            us = slice(D_FF + j * FF_CHUNK, D_FF + (j + 1) * FF_CHUNK)
            g = gu_ref[:, cs].astype(F32)
            u = gu_ref[:, us].astype(F32)
            da = da_ref[:, cs].astype(F32)
            s = _sigmoid(g)
            o_ref[:, cs] = (da * u * _dsilu(g, s)).astype(BF16)
            o_ref[:, us] = (da * g * s).astype(BF16)

    return pl.pallas_call(
        body, name="swiglu_act_bwd", grid=(t // tm,),
        out_shape=jax.ShapeDtypeStruct((t, 2 * D_FF), BF16),
        in_specs=[pl.BlockSpec((tm, 2 * D_FF), lambda i: (i, 0)), pl.BlockSpec((tm, D_FF), lambda i: (i, 0))],
        out_specs=pl.BlockSpec((tm, 2 * D_FF), lambda i: (i, 0)),
        compiler_params=_cparams(("parallel",)),
    )(gu, dact)


def _hgrn_chunk_terms(q, fl, lbv, tril_f):
    sig = _sigmoid(fl)
    f = lbv + (1.0 - lbv) * sig
    lam = jnp.log(f)
    k = 1.0 - f
    sq = _sigmoid(q)
    qt = q * sq * Q_SCALE
    bc = _sel(_nn, lam, tril_f, 3, x_first=False)
    bmid = bc[CHUNK // 2 - 1:CHUNK // 2, :]
    bl = bc[CHUNK - 1:CHUNK, :]
    eq = jnp.exp(jnp.minimum(bc - bmid, EXP_CLIP))
    ek = jnp.exp(jnp.minimum(bmid - bc, EXP_CLIP))
    eb = jnp.exp(bc)
    ekl = jnp.exp(bl - bc)
    ebl = jnp.exp(bl)
    return sig, f, k, sq, qt, eq, ek, eb, ekl, ebl


def hgrn_fwd(proj, lb, gnorm):
    t = proj.shape[0]
    tb = _tile(t, TOKEN_BLOCK)
    ncb = tb // CHUNK

    hps = HGRN_HEADS_PER_STEP
    wide = hps * HK

    def body(q_ref, f_ref, i_ref, g_ref, lb_ref, gn_ref, oa_ref, oraw_ref, st_ref, state):
        @pl.when(pl.program_id(1) == 0)
        def _():
            state[...] = jnp.zeros_like(state)

        gn = gn_ref[...]
        mask = _tri(CHUNK)
        tril_f = mask.astype(BF16)

        def chunk(c, carry):
            sl = pl.ds(pl.multiple_of(c * CHUNK, CHUNK), CHUNK)
            for hh in range(hps):
                ln = slice(hh * HK, (hh + 1) * HK)
                q, fl, v, g = q_ref[sl, ln], f_ref[sl, ln], i_ref[sl, ln], g_ref[sl, ln]
                sig, f, k, sq, qt, eq, ek, eb, ekl, ebl = _hgrn_chunk_terms(q, fl, lb_ref[:, ln], tril_f)
                a = jnp.where(mask, _nt((qt * eq).astype(BF16), (k * ek).astype(BF16)), 0.0)
                st = state[hh]
                st_ref[hh, c] = st
                vb = v.astype(BF16)
                o = _nn(a.astype(BF16), vb) + _nt((qt * eb).astype(BF16), st.astype(BF16))
                state[hh] = st * ebl + _tn(vb, (k * ekl).astype(BF16))
                oraw_ref[sl, ln] = o
                rn = o * lax.rsqrt(jnp.mean(o * o, axis=-1, keepdims=True) + RMS_EPS)
                oa_ref[sl, ln] = (rn * gn * g * _sigmoid(g)).astype(BF16)
            return carry

        lax.fori_loop(0, ncb, chunk, 0, unroll=min(CHUNK_UNROLL, ncb))

    def col(block):
        return pl.BlockSpec((tb, wide), lambda h, j: (j, block * (N_HEADS_A // hps) + h))

    return pl.pallas_call(
        body, name="hgrn_fwd", grid=(N_HEADS_A // hps, t // tb),
        out_shape=[jax.ShapeDtypeStruct((t, D), BF16), jax.ShapeDtypeStruct((t, D), F32),
                   jax.ShapeDtypeStruct((N_HEADS_A, t // CHUNK, HK, HK), F32)],
        in_specs=[col(0), col(1), col(2), col(3), pl.BlockSpec((1, wide), lambda h, j: (0, h)),
                  pl.BlockSpec((1, HK), lambda h, j: (0, 0))],
        out_specs=[pl.BlockSpec((tb, wide), lambda h, j: (j, h)), pl.BlockSpec((tb, wide), lambda h, j: (j, h)),
                   pl.BlockSpec((hps, ncb, HK, HK), lambda h, j: (h, j, 0, 0))],
        scratch_shapes=[pltpu.VMEM((hps, HK, HK), F32)],
        compiler_params=_cparams(("parallel", "arbitrary")),
    )(proj, proj, proj, proj, lb, gnorm)


def hgrn_bwd(proj, lb, gnorm, o_raw, doa, states, give, dproj):
    t = proj.shape[0]
    tb = _tile(t, TOKEN_BLOCK)
    ncb = tb // CHUNK
    nb = t // tb
    hps = HGRN_HEADS_PER_STEP
    wide = hps * HK

    def body(q_ref, f_ref, i_ref, g_ref, lb_ref, gn_ref, oraw_ref, doa_ref, st_ref, give_ref, dp_in_ref,
             dp_ref, dlb_ref, dgn_ref, got_ref, dstate, send_sem, recv_sem):
        h, j = pl.program_id(0), pl.program_id(1)
        swap_start, swap_wait = _sibling_exchange(give_ref, got_ref, send_sem, recv_sem)

        @pl.when((h == 0) & (j == 0))
        def _():
            swap_start()

        @pl.when(j == 0)
        def _():
            dstate[...] = jnp.zeros_like(dstate)
            dlb_ref[...] = jnp.zeros_like(dlb_ref)

        @pl.when((j == 0) & (h == 0))
        def _():
            dgn_ref[...] = jnp.zeros_like(dgn_ref)

        gn = gn_ref[...]
        mask = _tri(CHUNK)
        mask_t = _tri(CHUNK, upper=True)
        tril_f = mask.astype(BF16)
        triu_f = mask_t.astype(BF16)

        def chunk(i, c0):
            c = ncb - 1 - i
            sl = pl.ds(pl.multiple_of(c * CHUNK, CHUNK), CHUNK)
            for hh in range(hps):
                ln = slice(hh * HK, (hh + 1) * HK)
                q, fl, v, g = q_ref[sl, ln], f_ref[sl, ln], i_ref[sl, ln], g_ref[sl, ln]
                lbv = lb_ref[:, ln]
                sig, f, k, sq, qt, eq, ek, eb, ekl, ebl = _hgrn_chunk_terms(q, fl, lbv, tril_f)
                qe = (qt * eq).astype(BF16)
                ke = (k * ek).astype(BF16)
                st32 = st_ref[hh, c]
                st = st32.astype(BF16)
                dst = dstate[hh]
                dstb = dst.astype(BF16)
                o = oraw_ref[sl, ln]
                rstd = lax.rsqrt(jnp.mean(o * o, axis=-1, keepdims=True) + RMS_EPS)
                rn = o * rstd
                sgm = _sigmoid(g)
                sg = g * sgm
                doa_v = doa_ref[sl, ln]
                drn = doa_v * gn * sg
                dgn_ref[...] += _colsum(doa_v * rn * sg)
                dp_ref[3, sl, ln] = (doa_v * rn * gn * _dsilu(g, sgm)).astype(BF16)
                do = rstd * (drn - rn * jnp.mean(drn * rn, axis=-1, keepdims=True))
                dob = do.astype(BF16)
                vb = v.astype(BF16)
                da = jnp.where(mask, _nt(dob, vb), 0.0).astype(BF16)
                da_t = jnp.where(mask_t, _nt(vb, dob), 0.0).astype(BF16)
                a_t = jnp.where(mask_t, _nt(ke, qe), 0.0).astype(BF16)
                kl = (k * ekl).astype(BF16)
                qb = (qt * eb).astype(BF16)
                dq_in = _nn(da, ke)
                dk_in = _nn(da_t, qe)
                dq_out = eb * _nn(dob, st)
                dk_out = ekl * _nn(vb, dstb)
                dqt = eq * dq_in + dq_out
                dk = ek * dk_in + dk_out
                dv = _nn(a_t, dob) + _nt(kl, dstb)
                dstate[hh] = dst * ebl + _tn(dob, qb)
                dbig = qe.astype(F32) * dq_in - ke.astype(F32) * dk_in + qt * dq_out - k * dk_out
                beyond = _colsum(k * dk_out) + ebl * _colsum(dst * st32)
                dlam = _sel(_nn, dbig, triu_f, 3, x_first=False) + beyond
                df = dlam / f - dk
                dp_ref[1, sl, ln] = (df * (1.0 - lbv) * sig * (1.0 - sig)).astype(BF16)
                dlb_ref[:, ln] += _colsum(df * (1.0 - sig))
                dp_ref[0, sl, ln] = (dqt * Q_SCALE * _dsilu(q, sq)).astype(BF16)
                dp_ref[2, sl, ln] = dv.astype(BF16)
            return c0

        lax.fori_loop(0, ncb, chunk, 0, unroll=min(CHUNK_UNROLL, ncb))

        @pl.when((h == N_HEADS_A // hps - 1) & (j == nb - 1))
        def _():
            swap_wait()

    def col(block):
        return pl.BlockSpec((tb, wide), lambda h, j: (nb - 1 - j, block * (N_HEADS_A // hps) + h))

    hcol = pl.BlockSpec((tb, wide), lambda h, j: (nb - 1 - j, h))
    hbm = pl.BlockSpec(memory_space=pl.ANY)
    return pl.pallas_call(
        body, name="hgrn_bwd", grid=(N_HEADS_A // hps, nb),
        out_shape=[jax.ShapeDtypeStruct(dproj.shape, dproj.dtype), jax.ShapeDtypeStruct((1, D), F32),
                   jax.ShapeDtypeStruct((1, HK), F32), jax.ShapeDtypeStruct(give.shape, give.dtype)],
        in_specs=[col(0), col(1), col(2), col(3), pl.BlockSpec((1, wide), lambda h, j: (0, h)),
                  pl.BlockSpec((1, HK), lambda h, j: (0, 0)), hcol, hcol,
                  pl.BlockSpec((hps, ncb, HK, HK), lambda h, j: (h, nb - 1 - j, 0, 0)), hbm, hbm],
        out_specs=[pl.BlockSpec((4, tb, wide), lambda h, j: (0, nb - 1 - j, h)),
                   pl.BlockSpec((1, wide), lambda h, j: (0, h)), pl.BlockSpec((1, HK), lambda h, j: (0, 0)), hbm],
        input_output_aliases={10: 0},
        scratch_shapes=[pltpu.VMEM((hps, HK, HK), F32)] + SIBLING_SEMS,
        compiler_params=_cparams(("arbitrary", "arbitrary")),
    )(proj, proj, proj, proj, lb, gnorm, o_raw, doa, states, give, dproj)


CONV_BLOCK0 = 6
CONV_TAPS = 4
HALO = 8


def conv_fwd(proj, conv_w, conv_b):
    t = proj.shape[0]
    tm = _tile(t, ROW_TILE)
    r = tm // HALO

    def body(x_ref, halo_ref, w_ref, b_ref, o_ref):
        i = pl.program_id(1)
        halo = jnp.where(i > 0, halo_ref[...], 0.0)
        ext = jnp.concatenate([halo, x_ref[...]], axis=0)
        pre = b_ref[...] + w_ref[CONV_TAPS - 1:CONV_TAPS, :] * ext[HALO:, :]
        for tap in range(CONV_TAPS - 1):
            pre = pre + w_ref[tap:tap + 1, :] * pltpu.roll(ext, CONV_TAPS - 1 - tap, axis=0)[HALO:, :]
        o_ref[...] = pre * _sigmoid(pre)

    return pl.pallas_call(
        body, name="conv_fwd", grid=(CONV_DIM // D, t // tm),
        out_shape=jax.ShapeDtypeStruct((t, CONV_DIM), F32),
        in_specs=[pl.BlockSpec((tm, D), lambda cb, i: (i, CONV_BLOCK0 + cb)),
                  pl.BlockSpec((HALO, D), lambda cb, i: (jnp.maximum(i * r - 1, 0), CONV_BLOCK0 + cb)),
                  pl.BlockSpec((CONV_TAPS, D), lambda cb, i: (0, cb)), pl.BlockSpec((1, D), lambda cb, i: (0, cb))],
        out_specs=pl.BlockSpec((tm, D), lambda cb, i: (i, cb)),
        compiler_params=_cparams(("parallel", "parallel")),
    )(proj, proj, conv_w, conv_b)


def conv_bwd(proj, dxc, conv_w, conv_b, dproj):
    t = proj.shape[0]
    tm = _tile(t, ROW_TILE)
    r = tm // HALO
    n = t // tm
    last_halo = t // HALO - 1

    def body(x_ref, prev_ref, next_ref, d_ref, dnext_ref, w_ref, b_ref, dp_in_ref, dx_ref, dw_ref, db_ref):
        i = pl.program_id(1)

        @pl.when(i == 0)
        def _():
            dw_ref[...] = jnp.zeros_like(dw_ref)
            db_ref[...] = jnp.zeros_like(db_ref)

        prev = jnp.where(i > 0, prev_ref[...], 0.0)
        ext = jnp.concatenate([prev, x_ref[...], next_ref[...]], axis=0)
        shifted = [pltpu.roll(ext, CONV_TAPS - 1 - tap, axis=0)[HALO:, :] for tap in range(CONV_TAPS - 1)]
        shifted.append(ext[HALO:, :])
        pre = b_ref[...]
        for tap in range(CONV_TAPS):
            pre = pre + w_ref[tap:tap + 1, :] * shifted[tap]
        s = _sigmoid(pre)
        d_ext = jnp.concatenate([d_ref[...].astype(F32),
                                 jnp.where(i < n - 1, dnext_ref[0:HALO, :].astype(F32), 0.0)], axis=0)
        dpre = d_ext * _dsilu(pre, s)
        dx = w_ref[CONV_TAPS - 1:CONV_TAPS, :] * dpre[:tm, :]
        for tap in range(CONV_TAPS - 1):
            back = CONV_TAPS - 1 - tap
            dx = dx + w_ref[tap:tap + 1, :] * pltpu.roll(dpre, tm + HALO - back, axis=0)[:tm, :]
        dx_ref[...] = dx.astype(BF16)
        dp = dpre[:tm, :]
        db_ref[...] += _colsum(dp)
        for tap in range(CONV_TAPS):
            dw_ref[tap:tap + 1, :] += _colsum(dp * shifted[tap][:tm, :])

    return pl.pallas_call(
        body, name="conv_bwd", grid=(CONV_DIM // D, n),
        out_shape=[jax.ShapeDtypeStruct(dproj.shape, dproj.dtype), jax.ShapeDtypeStruct((8, CONV_DIM), F32),
                   jax.ShapeDtypeStruct((1, CONV_DIM), F32)],
        in_specs=[pl.BlockSpec((tm, D), lambda cb, i: (i, CONV_BLOCK0 + cb)),
                  pl.BlockSpec((HALO, D), lambda cb, i: (jnp.maximum(i * r - 1, 0), CONV_BLOCK0 + cb)),
                  pl.BlockSpec((HALO, D), lambda cb, i: (jnp.minimum((i + 1) * r, last_halo), CONV_BLOCK0 + cb)),
                  pl.BlockSpec((tm, D), lambda cb, i: (i, cb)),
                  pl.BlockSpec((2 * HALO, D), lambda cb, i: (jnp.minimum((i + 1) * (r // 2), last_halo // 2), cb)),
                  pl.BlockSpec((CONV_TAPS, D), lambda cb, i: (0, cb)), pl.BlockSpec((1, D), lambda cb, i: (0, cb)),
                  pl.BlockSpec(memory_space=pl.ANY)],
        out_specs=[pl.BlockSpec((None, tm, D), lambda cb, i: (CONV_BLOCK0 + cb, i, 0)),
                   pl.BlockSpec((8, D), lambda cb, i: (0, cb)), pl.BlockSpec((1, D), lambda cb, i: (0, cb))],
        input_output_aliases={7: 0},
        compiler_params=_cparams(("parallel", "arbitrary")),
    )(proj, proj, proj, dxc, dxc, conv_w, conv_b, dproj)


def dt_fill(ddt, dproj):
    t = ddt.shape[0]
    tm = _tile(t, ROW_TILE)
    w = ddt.shape[1]

    def body(d_ref, dp_in_ref, o_ref):
        o_ref[:, :w] = d_ref[...]
        o_ref[:, w:] = jnp.zeros((tm, D - w), o_ref.dtype)

    return pl.pallas_call(
        body, name="dt_fill", grid=(t // tm,),
        out_shape=jax.ShapeDtypeStruct(dproj.shape, dproj.dtype),
        in_specs=[pl.BlockSpec((tm, w), lambda i: (i, 0)), pl.BlockSpec(memory_space=pl.ANY)],
        out_specs=pl.BlockSpec((None, tm, D), lambda i: (DT_COL_BLOCK, i, 0)),
        input_output_aliases={1: 0},
        compiler_params=_cparams(("parallel",)),
    )(ddt, dproj)


Z_BLOCK0 = 8
DT_COL_BLOCK = 9
DT_BLOCK0 = 8 * DT_COL_BLOCK
GATE_BLOCK0 = 10
B_BLOCK0 = 16
C_BLOCK0 = 20


def _head_expand():
    e = np.zeros((N_STATE, GROUP_W), np.float32)
    for hh in range(HEADS_PER_GROUP):
        e[hh, hh * HEAD_P:(hh + 1) * HEAD_P] = 1.0
    return jnp.asarray(e, BF16)


def _ssd_chunk_terms(dt, bias, alog, expand, tril_f, eye):
    dtb = dt + bias
    delta = jnp.maximum(dtb, 0.0) + jnp.log(1.0 + jnp.exp(-jnp.abs(dtb)))
    ea = jnp.exp(alog)
    a = -ea * delta
    acum = _sel(_nn, a, tril_f, 3, x_first=False)
    delta_e = _sel(_nn, delta, expand, 2)
    acum_e = _sel(_nn, acum, expand, 3)
    acum_t = _sel(_nt, acum, eye, 3, x_first=False)
    return dtb, delta, ea, a, acum, delta_e, acum_e, acum_t


def ssd_fwd(proj, xc, alog4, bias4, dskip4, wnorm, expand):
    t = proj.shape[0]
    tb = _tile(t, TOKEN_BLOCK)
    ncb = tb // SSD_CHUNK

    def body(xs_ref, b_ref, c_ref, dt_ref, z_ref, alog_ref, bias_ref, dsk_ref, wn_ref, e_ref, ob_ref, st_ref, state):
        @pl.when(pl.program_id(1) == 0)
        def _():
            state[...] = jnp.zeros_like(state)

        expand = e_ref[...]
        mask = _tri(SSD_CHUNK)
        tril_f = mask.astype(BF16)
        eye = (lax.broadcasted_iota(jnp.int32, (N_STATE, N_STATE), 0) ==
               lax.broadcasted_iota(jnp.int32, (N_STATE, N_STATE), 1)).astype(BF16)
        alog, bias = alog_ref[0], bias_ref[0]
        d_e = _sel(_nn, jnp.broadcast_to(dsk_ref[0], (8, N_STATE)), expand, 3)[0:1, :]
        wn = wn_ref[...]

        def chunk(c, carry):
            sl = pl.ds(pl.multiple_of(c * SSD_CHUNK, SSD_CHUNK), SSD_CHUNK)
            xs, bm, cm, dt, z = xs_ref[sl, :], b_ref[sl, :], c_ref[sl, :], dt_ref[sl, :], z_ref[sl, :]
            dtb, delta, ea, a, acum, delta_e, acum_e, acum_t = _ssd_chunk_terms(dt, bias, alog, expand, tril_f, eye)
            alast_e = acum_e[SSD_CHUNK - 1:SSD_CHUNK, :]
            xd = xs * delta_e
            xdb = xd.astype(BF16)
            cb_, bb_ = cm.astype(BF16), bm.astype(BF16)
            cbm = _nt(cb_, bb_)
            ys = []
            for hh in range(HEADS_PER_GROUP):
                lh = jnp.where(mask, jnp.exp(jnp.minimum(acum[:, hh:hh + 1] - acum_t[hh:hh + 1, :], 0.0)), 0.0)
                ys.append(_nn((cbm * lh).astype(BF16), xdb[:, hh * HEAD_P:(hh + 1) * HEAD_P]))
            st = state[...]
            st_ref[0, c] = st
            y = jnp.concatenate(ys, axis=1) + _nn(cb_, st.astype(BF16)) * jnp.exp(acum_e) + xs * d_e
            state[...] = st * jnp.exp(alast_e) + _tn(bb_, (xd * jnp.exp(alast_e - acum_e)).astype(BF16))
            yg = y * z * _sigmoid(z)
            ob_ref[sl, :] = (yg * lax.rsqrt(jnp.mean(yg * yg, axis=-1, keepdims=True) + RMS_EPS) * wn).astype(BF16)
            return carry

        lax.fori_loop(0, ncb, chunk, 0, unroll=min(CHUNK_UNROLL, ncb))

    small = pl.BlockSpec((1, 1, N_STATE), lambda g, j: (g, 0, 0))
    return pl.pallas_call(
        body, name="ssd_fwd", grid=(N_GROUPS, t // tb),
        out_shape=[jax.ShapeDtypeStruct((t, B_INNER), BF16),
                   jax.ShapeDtypeStruct((N_GROUPS, t // SSD_CHUNK, N_STATE, GROUP_W), F32)],
        in_specs=[pl.BlockSpec((tb, GROUP_W), lambda g, j: (j, g)),
                  pl.BlockSpec((tb, N_STATE), lambda g, j: (j, B_BLOCK0 + g)),
                  pl.BlockSpec((tb, N_STATE), lambda g, j: (j, C_BLOCK0 + g)),
                  pl.BlockSpec((tb, N_STATE), lambda g, j: (j, DT_BLOCK0 + g)),
                  pl.BlockSpec((tb, GROUP_W), lambda g, j: (j, Z_BLOCK0 + g)),
                  small, small, small, pl.BlockSpec((1, GROUP_W), lambda g, j: (0, g)),
                  pl.BlockSpec((N_STATE, GROUP_W), lambda g, j: (0, 0))],
        out_specs=[pl.BlockSpec((tb, GROUP_W), lambda g, j: (j, g)),
                   pl.BlockSpec((1, ncb, N_STATE, GROUP_W), lambda g, j: (g, j, 0, 0))],
        scratch_shapes=[pltpu.VMEM((N_STATE, GROUP_W), F32)],
        compiler_params=_cparams(("parallel", "arbitrary")),
    )(xc, xc, xc, proj, proj, alog4, bias4, dskip4, wnorm, expand)


def ssd_bwd(proj, xc, alog4, bias4, dskip4, wnorm, expand, dob, states, part, dproj):
    t = proj.shape[0]
    tb = _tile(t, TOKEN_BLOCK)
    ncb = tb // SSD_CHUNK
    nb = t // tb

    def body(xs_ref, b_ref, c_ref, dt_ref, z_ref, alog_ref, bias_ref, dsk_ref, wn_ref, e_ref, dob_ref, st_ref, part_ref,
             dp_in_ref, dxs_ref, db_ref, dc_ref, dz_ref, ddt_ref, dwn_ref, dalog_ref, dbias_ref, ddsk_ref, parts_ref, dstate,
             send_sems, recv_sems, local_sem):
        xchg_start, xchg_wait = _chip_exchange(part_ref, parts_ref, send_sems, recv_sems, local_sem)

        @pl.when((pl.program_id(0) == 0) & (pl.program_id(1) == 0))
        def _():
            xchg_start()

        @pl.when(pl.program_id(1) == 0)
        def _():
            dstate[...] = jnp.zeros_like(dstate)
            dwn_ref[...] = jnp.zeros_like(dwn_ref)
            dalog_ref[...] = jnp.zeros_like(dalog_ref)
            dbias_ref[...] = jnp.zeros_like(dbias_ref)
            ddsk_ref[...] = jnp.zeros_like(ddsk_ref)

        expand = e_ref[...]
        mask = _tri(SSD_CHUNK)
        mask_t = _tri(SSD_CHUNK, upper=True)
        tril_f = mask.astype(BF16)
        triu_f = mask_t.astype(BF16)
        eye = (lax.broadcasted_iota(jnp.int32, (N_STATE, N_STATE), 0) ==
               lax.broadcasted_iota(jnp.int32, (N_STATE, N_STATE), 1)).astype(BF16)
        alog, bias = alog_ref[0], bias_ref[0]
        d_e = _sel(_nn, jnp.broadcast_to(dsk_ref[0], (8, N_STATE)), expand, 3)[0:1, :]
        wn = wn_ref[...]

        def chunk(i, c0):
            c = ncb - 1 - i
            sl = pl.ds(pl.multiple_of(c * SSD_CHUNK, SSD_CHUNK), SSD_CHUNK)
            xs, bm, cm, dt, z = xs_ref[sl, :], b_ref[sl, :], c_ref[sl, :], dt_ref[sl, :], z_ref[sl, :]
            dtb, delta, ea, a, acum, delta_e, acum_e, acum_t = _ssd_chunk_terms(dt, bias, alog, expand, tril_f, eye)
            alast_e = acum_e[SSD_CHUNK - 1:SSD_CHUNK, :]
            eacum = jnp.exp(acum_e)
            wl = jnp.exp(alast_e - acum_e)
            xd = xs * delta_e
            xdb = xd.astype(BF16)
            cb_, bb_ = cm.astype(BF16), bm.astype(BF16)
            cbm = _nt(cb_, bb_)
            st32 = st_ref[0, c]
            stb = st32.astype(BF16)
            dst = dstate[...]
            dstb = dst.astype(BF16)
            lhs, mixes, ys = [], [], []
            for hh in range(HEADS_PER_GROUP):
                col, row = acum[:, hh:hh + 1], acum_t[hh:hh + 1, :]
                lh = jnp.where(mask, jnp.exp(jnp.minimum(col - row, 0.0)), 0.0)
                mix = (cbm * lh).astype(BF16)
                lhs.append(lh)
                mixes.append(mix)
                ys.append(_nn(mix, xdb[:, hh * HEAD_P:(hh + 1) * HEAD_P]))
            y_in = jnp.concatenate(ys, axis=1)
            y_out = _nn(cb_, stb) * eacum
            y = y_in + y_out + xs * d_e
            sgz = _sigmoid(z)
            sz = z * sgz
            yg = y * sz
            rstd = lax.rsqrt(jnp.mean(yg * yg, axis=-1, keepdims=True) + RMS_EPS)
            nrm = yg * rstd
            dob_v = dob_ref[sl, :]
            dn = dob_v * wn
            dwn_ref[...] += _colsum(dob_v * nrm)
            dyg = rstd * (dn - nrm * jnp.mean(dn * nrm, axis=-1, keepdims=True))
            dy = dyg * sz
            dz_ref[sl, :] = (dyg * y * _dsilu(z, sgz)).astype(BF16)
            dyb = dy.astype(BF16)
            dxds = []
            dcb = jnp.zeros((SSD_CHUNK, SSD_CHUNK), F32)
            for hh in range(HEADS_PER_GROUP):
                hs = slice(hh * HEAD_P, (hh + 1) * HEAD_P)
                dy_h, x_h = dyb[:, hs], xdb[:, hs]
                dxds.append(_tn(mixes[hh], dy_h))
                dcb = dcb + _nt(dy_h, x_h) * lhs[hh]
            dcbb = dcb.astype(BF16)
            dye = (dy * eacum).astype(BF16)
            xw = (xd * wl).astype(BF16)
            dxd_in = jnp.concatenate(dxds, axis=1)
            dxd_out = wl * _nn(bb_, dstb)
            dxd = dxd_in + dxd_out
            dc_ref[sl, :] = (_nn(dcbb, bb_) + _nt(dye, stb)).astype(dc_ref.dtype)
            db_ref[sl, :] = (_tn(dcbb, cb_) + _nt(xw, dstb)).astype(db_ref.dtype)
            dstate[...] = dst * jnp.exp(alast_e) + _tn(cb_, dye)
            col_out = xd * dxd_out
            dac = _sel(_nt, dyb.astype(F32) * y_in - xdb.astype(F32) * dxd_in + dy * y_out - col_out, expand, 3)
            beyond = _colsum(col_out) + jnp.exp(alast_e) * _colsum(dst * st32)
            da = (_sel(_nn, dac, triu_f, 3, x_first=False) +
                  _sel(_nt, jnp.broadcast_to(beyond, (8, GROUP_W)), expand, 3)[0:1, :])
            ddelta = _sel(_nt, dxd * xs, expand, 2) - da * ea
            dalog_ref[0] += _colsum(da * a)
            ddtb = ddelta * _sigmoid(dtb)
            dbias_ref[0] += _colsum(ddtb)
            ddt_ref[sl, :] = ddtb.astype(BF16)
            ddsk_ref[0] += _sel(_nt, jnp.broadcast_to(_colsum(dy * xs), (8, GROUP_W)), expand, 3)[0:1, :]
            dxs_ref[sl, :] = (dxd * delta_e + dy * d_e).astype(dxs_ref.dtype)
            return c0

        lax.fori_loop(0, ncb, chunk, 0, unroll=min(CHUNK_UNROLL, ncb))

        @pl.when((pl.program_id(0) == N_GROUPS - 1) & (pl.program_id(1) == nb - 1))
        def _():
            xchg_wait()

    small = pl.BlockSpec((1, 1, N_STATE), lambda g, j: (g, 0, 0))
    wide = pl.BlockSpec((tb, GROUP_W), lambda g, j: (nb - 1 - j, g))
    narrow = pl.BlockSpec((tb, N_STATE), lambda g, j: (nb - 1 - j, g))
    hbm = pl.BlockSpec(memory_space=pl.ANY)
    return pl.pallas_call(
        body, name="ssd_bwd", grid=(N_GROUPS, nb),
        out_shape=[jax.ShapeDtypeStruct((t, B_INNER), BF16), jax.ShapeDtypeStruct((t, GROUP_W), BF16),
                   jax.ShapeDtypeStruct((t, GROUP_W), BF16), jax.ShapeDtypeStruct(dproj.shape, dproj.dtype),
                   jax.ShapeDtypeStruct((t, GROUP_W), BF16), jax.ShapeDtypeStruct((1, B_INNER), F32),
                   jax.ShapeDtypeStruct((N_GROUPS, 1, N_STATE), F32), jax.ShapeDtypeStruct((N_GROUPS, 1, N_STATE), F32),
                   jax.ShapeDtypeStruct((N_GROUPS, 1, N_STATE), F32), jax.ShapeDtypeStruct(part.shape, part.dtype)],
        in_specs=[wide,
                  pl.BlockSpec((tb, N_STATE), lambda g, j: (nb - 1 - j, B_BLOCK0 + g)),
                  pl.BlockSpec((tb, N_STATE), lambda g, j: (nb - 1 - j, C_BLOCK0 + g)),
                  pl.BlockSpec((tb, N_STATE), lambda g, j: (nb - 1 - j, DT_BLOCK0 + g)),
                  pl.BlockSpec((tb, GROUP_W), lambda g, j: (nb - 1 - j, Z_BLOCK0 + g)),
                  small, small, small, pl.BlockSpec((1, GROUP_W), lambda g, j: (0, g)),
                  pl.BlockSpec((N_STATE, GROUP_W), lambda g, j: (0, 0)), wide,
                  pl.BlockSpec((1, ncb, N_STATE, GROUP_W), lambda g, j: (g, nb - 1 - j, 0, 0)), hbm, hbm],
        out_specs=[wide, narrow, narrow,
                   pl.BlockSpec((None, tb, GROUP_W), lambda g, j: (Z_BLOCK0 // 2 + g // 2, nb - 1 - j, g % 2)),
                   narrow, pl.BlockSpec((1, GROUP_W), lambda g, j: (0, g)), small, small, small, hbm],
        input_output_aliases={13: 3},
        scratch_shapes=[pltpu.VMEM((N_STATE, GROUP_W), F32)] + CHIP_SEMS,
        compiler_params=_cparams(("arbitrary", "arbitrary")),
    )(xc, xc, xc, proj, proj, alog4, bias4, dskip4, wnorm, expand, dob, states, part, dproj)


def lower_bound_fwd(hgrn_lb):
    def body(a_ref, o_ref):
        a0, a1 = a_ref[0:1, :], a_ref[1:2, :]
        m = jnp.maximum(a0, a1)
        e0, e1 = jnp.exp(a0 - m), jnp.exp(a1 - m)
        o_ref[...] = e0 / (e0 + e1)

    return pl.pallas_call(body, name="lower_bound_fwd", out_shape=jax.ShapeDtypeStruct((1, D), F32))(hgrn_lb)


def ada_weight_grad(c_all, dmod_cols):
    def body(c_ref, d_ref, o_ref):
        cval = c_ref[...]
        o_ref[...] = _tn(cval * _sigmoid(cval), d_ref[...], HI)

    return pl.pallas_call(body, name="ada_weight_grad",
                          out_shape=jax.ShapeDtypeStruct((D, dmod_cols.shape[1]), F32))(c_all, dmod_cols)


def reduce_small(gathered, hgrn_lb, dlb_off):
    n = gathered.shape[2]

    def body(g_ref, a_ref, o_ref, glb_ref):
        s = g_ref[0]
        for d in range(1, N_DEV):
            s = s + g_ref[d]
        o_ref[...] = s
        a0, a1 = a_ref[0:1, :], a_ref[1:2, :]
        m = jnp.maximum(a0, a1)
        e0, e1 = jnp.exp(a0 - m), jnp.exp(a1 - m)
        p0 = e0 / (e0 + e1)
        tq = s[:, dlb_off:dlb_off + D] * p0 * (1.0 - p0)
        glb_ref[0:1, :] = tq
        glb_ref[1:2, :] = -tq

    return pl.pallas_call(body, name="reduce_small",
                          out_shape=[jax.ShapeDtypeStruct((1, n), F32), jax.ShapeDtypeStruct((2, D), F32)])(gathered, hgrn_lb)


def _adam_math(w, g, m, v):
    m2 = ADAM_B1 * m + (1.0 - ADAM_B1) * g
    v2 = ADAM_B2 * v + (1.0 - ADAM_B2) * (g * g)
    m_hat = m2 / (1.0 - ADAM_B1 ** ADAM_STEP)
    v_hat = v2 / (1.0 - ADAM_B2 ** ADAM_STEP)
    delta = -ADAM_LR * (m_hat / (jnp.sqrt(v_hat) + ADAM_EPS) + ADAM_WD * w)
    return delta, m2, v2


def _row_tile(rows, mult=8, cap=128):
    for cand in range(cap - cap % mult, 0, -mult):
        if rows % cand == 0:
            return cand
    return rows


def sum_parts(parts, name):
    n, rows, cols = parts.shape
    tr = _row_tile(rows, 16, 256)

    def body(p_ref, o_ref):
        s = p_ref[0].astype(F32)
        for d in range(1, n):
            s = s + p_ref[d].astype(F32)
        o_ref[...] = s

    return pl.pallas_call(
        body, name=name, grid=(rows // tr,),
        out_shape=jax.ShapeDtypeStruct((rows, cols), F32),
        in_specs=[pl.BlockSpec((n, tr, cols), lambda i: (0, i, 0))],
        out_specs=pl.BlockSpec((tr, cols), lambda i: (i, 0)),
        compiler_params=_cparams(("parallel",)),
    )(parts)


def sum_pair(a, b, name):
    rows, cols = a.shape
    tr = _row_tile(rows, 16, 256)

    def body(a_ref, b_ref, o_ref):
        o_ref[...] = (a_ref[...].astype(F32) + b_ref[...].astype(F32)).astype(o_ref.dtype)

    blk = pl.BlockSpec((tr, cols), lambda i: (i, 0))
    return pl.pallas_call(
        body, name=name, grid=(rows // tr,),
        out_shape=jax.ShapeDtypeStruct((rows, cols), a.dtype),
        in_specs=[blk, blk], out_specs=blk,
        compiler_params=_cparams(("parallel",)),
    )(a, b)


def adamw(w, g, m, v, name):
    rows, cols = w.shape
    tr = _row_tile(rows)

    def body(w_ref, g_ref, m_ref, v_ref, d_ref, m2_ref, v2_ref):
        delta, m2, v2 = _adam_math(w_ref[...], g_ref[...], m_ref[...], v_ref[...])
        d_ref[...] = delta
        m2_ref[...] = m2
        v2_ref[...] = v2

    blk = pl.BlockSpec((tr, cols), lambda i: (i, 0))
    return pl.pallas_call(
        body, name=name, grid=(rows // tr,),
        out_shape=[jax.ShapeDtypeStruct((rows, cols), F32)] * 3,
        in_specs=[blk] * 4, out_specs=[blk] * 3,
        compiler_params=_cparams(("parallel",)),
    )(w, g, m, v)


def _pad128(n):
    return -(-n // 128) * 128


def _pack(arrays):
    offs, parts, off = [], [], 0
    for a in arrays:
        flat = a.reshape(1, -1)
        n = flat.shape[1]
        offs.append(off)
        parts.append(jnp.pad(flat, ((0, 0), (0, _pad128(n) - n))))
        off += _pad128(n)
    return jnp.concatenate(parts, axis=1), offs


def _unpack(vec, offs, shapes):
    out = []
    for off, shp in zip(offs, shapes):
        n = int(np.prod(shp))
        out.append(vec[0, off:off + n].reshape(shp))
    return out


IN_ROWS = IN_DIM // N_DEV
DT_ROW0 = 9216
DT_DEV, DT_LO = divmod(DT_ROW0, IN_ROWS)


GATE_SHIFT = D - 32


def _in_row_pieces(tile):
    pieces = []
    if tile == DT_COL_BLOCK:
        for g in range(N_GROUPS):
            o = DT_ROW0 + HEADS_PER_GROUP * g
            pieces.append((N_STATE * g, o // IN_ROWS, o % IN_ROWS, HEADS_PER_GROUP))
        return pieces
    r, end = tile * D, (tile + 1) * D
    while r < end:
        o = r if r < DT_ROW0 else r - GATE_SHIFT
        dev, loc = divmod(o, IN_ROWS)
        n = min(end - r, IN_ROWS - loc)
        pieces.append((r - tile * D, dev, loc, n))
        r += n
    return pieces


def assemble_w_in(g_all):
    ntile = N_PROJ // D

    def body(g_ref, o_ref):
        j = pl.program_id(0)
        for tile in range(ntile):
            @pl.when(j == tile)
            def _(tile=tile):
                if tile == DT_COL_BLOCK:
                    o_ref[...] = jnp.zeros_like(o_ref)
                for dst, dev, loc, n in _in_row_pieces(tile):
                    o_ref[pl.ds(dst, n), :] = g_ref[dev, pl.ds(loc, n), :]

    return pl.pallas_call(
        body, name="assemble_w_in", grid=(ntile,),
        out_shape=jax.ShapeDtypeStruct((N_PROJ, D), g_all.dtype),
        in_specs=[pl.BlockSpec(memory_space=pltpu.VMEM)],
        out_specs=pl.BlockSpec((D, D), lambda j: (j, 0)),
        compiler_params=_cparams(("arbitrary",)),
    )(g_all)


def _grad_in_blocks(g_t, core, slot):
    dt0 = DT_COL_BLOCK * D
    dt = g_t[dt0:dt0 + N_GROUPS * N_STATE].reshape(N_GROUPS, N_STATE, D)[:, :HEADS_PER_GROUP].reshape(32, D)
    with_dt = jnp.concatenate([g_t[DT_DEV * IN_ROWS:DT_ROW0], dt,
                               g_t[DT_ROW0 + 32 + GATE_SHIFT:(DT_DEV + 1) * IN_ROWS + GATE_SHIFT]], axis=0)
    blocks = []
    for q in range(N_CHIP):
        if 2 * q + 1 < DT_DEV:
            blk = lax.dynamic_slice_in_dim(g_t, IN_ROWS * (2 * q + core), IN_ROWS, axis=0)
        else:
            assert 2 * q == DT_DEV
            after = g_t[(DT_DEV + 1) * IN_ROWS + GATE_SHIFT:(DT_DEV + 2) * IN_ROWS + GATE_SHIFT]
            blk = jnp.where(core == 0, with_dt, after)
        blocks.append(jnp.pad(blk, ((0, slot - IN_ROWS), (0, 0))))
    return jnp.stack(blocks)


def kernel(x, c, w_ada, b_ada, w_in, hgrn_lb, hgrn_gnorm, ssm_conv_w, ssm_conv_b, ssm_dt_bias, ssm_a_log, ssm_d, ssm_norm, w_branch_a, w_branch_b, w_o, ln1_g, ln1_b, w_ffn_gate, w_ffn_up, w_ffn_down, ln2_g, ln2_b, loss_target, m_w_ada, m_b_ada, m_w_in, m_hgrn_lb, m_hgrn_gnorm, m_ssm_conv_w, m_ssm_conv_b, m_ssm_dt_bias, m_ssm_a_log, m_ssm_d, m_ssm_norm, m_w_branch_a, m_w_branch_b, m_w_o, m_ln1_g, m_ln1_b, m_w_ffn_gate, m_w_ffn_up, m_w_ffn_down, m_ln2_g, m_ln2_b, v_w_ada, v_b_ada, v_w_in, v_hgrn_lb, v_hgrn_gnorm, v_ssm_conv_w, v_ssm_conv_b, v_ssm_dt_bias, v_ssm_a_log, v_ssm_d, v_ssm_norm, v_w_branch_a, v_w_branch_b, v_w_o, v_ln1_g, v_ln1_b, v_w_ffn_gate, v_w_ffn_up, v_w_ffn_down, v_ln2_g, v_ln2_b):
    me = 4 * lax.axis_index("x") + 2 * lax.axis_index("y") + lax.axis_index("c")
    xt = x[0]
    tgt = loss_target[0]
    t = xt.shape[0]
    ada_cols = w_ada.shape[2]
    conv_cols = ssm_conv_w.shape[2]

    small_in, _ = _pack([c, ssm_conv_w[0]])
    small_all = allgather_vmem(small_in, "allgather_small_inputs")
    c_all = small_all[:, 0, :D]
    conv_w = small_all[:, 0, D:D + CONV_TAPS * conv_cols].reshape(N_DEV, CONV_TAPS, conv_cols)
    conv_w = conv_w.transpose(1, 0, 2).reshape(CONV_TAPS, CONV_DIM)
    mod = ada_modulation(c_all, w_ada[0], b_ada.reshape(N_DEV, 1, ada_cols))
    mod6 = mod.reshape(6, D)

    shards = [w_in[0].T, w_branch_a[0], w_branch_b[0], w_o[0], w_ffn_gate[0].T, w_ffn_up[0].T, w_ffn_down[0]]
    shard_rows = [s.shape[0] for s in shards]
    slot_rows = [-(-r // 32) * 32 for r in shard_rows]
    row_offs = [sum(slot_rows[:i]) for i in range(len(shards))]
    padded = [jnp.pad(s.astype(BF16), ((0, p - r), (0, 0))) for s, r, p in zip(shards, shard_rows, slot_rows)]
    w_in_t = assemble_w_in(allgather_hbm(padded[0], "allgather_w_in"))

    lb = lower_bound_fwd(hgrn_lb)
    u1 = ln_modulate(xt, mod6, 0, 1, "ln_modulate_1")
    proj, g_rest = mm_nt_gather(u1, w_in_t, F32, jnp.concatenate(padded[1:], axis=0), "mm_in_proj")
    g_ba, g_bb, g_o, g_fg, g_fu, g_fd = (g_rest[:, o - slot_rows[0]:o - slot_rows[0] + r]
                                         for o, r in zip(row_offs[1:], shard_rows[1:]))
    w_ba = g_ba.reshape(D, D)
    w_bb = g_bb.reshape(B_INNER, D)
    w_oo = g_o.reshape(D, D)
    w_gu_t = jnp.concatenate([g_fg.reshape(D_FF, D), g_fu.reshape(D_FF, D)], axis=0)
    w_dn = g_fd.reshape(D_FF, D)
    o_a, o_raw, st_a = hgrn_fwd(proj, lb, hgrn_gnorm)
    xc = conv_fwd(proj, conv_w, ssm_conv_b)
    pad3 = ((0, 0), (0, 0), (0, N_STATE - HEADS_PER_GROUP))
    alog4 = jnp.pad(ssm_a_log.reshape(N_GROUPS, 1, HEADS_PER_GROUP), pad3)
    bias4 = jnp.pad(ssm_dt_bias.reshape(N_GROUPS, 1, HEADS_PER_GROUP), pad3)
    dskip4 = jnp.pad(ssm_d.reshape(N_GROUPS, 1, HEADS_PER_GROUP), pad3)
    expand = _head_expand()
    o_b, st_b = ssd_fwd(proj, xc, alog4, bias4, dskip4, ssm_norm, expand)
    ya = mm_nn(o_a, w_ba, BF16, "mm_branch_a")
    yb = mm_nn(o_b, w_bb, BF16, "mm_branch_b")
    merged = merge_gates(ya, yb, proj)
    h1 = mm_nn(merged, w_oo, F32, "mm_out_proj")
    x1 = resid_ln(xt, h1, mod6, 2, ln1_g, ln1_b, "resid_ln_1")
    u2 = ln_modulate(x1, mod6, 3, 4, "ln_modulate_2")
    gu = mm_nt(u2, w_gu_t, BF16, "mm_ffn_in")
    act = swiglu_act(gu)
    h2 = mm_nn(act, w_dn, F32, "mm_ffn_out")

    dh2, dx1_part, acc4 = resid_ln_bwd(x1, h2, mod6, 5, ln2_g, ln2_b, tgt, True, "resid_ln_2_bwd")
    g_dn = mm_tn(act, dh2, "mm_grad_ffn_down")
    dact = mm_nt(dh2, w_dn, BF16, "mm_dact")
    dgu = swiglu_act_bwd(gu, dact)
    g_gu_t = mm_tn(dgu, u2, "mm_grad_ffn_in")
    du2 = mm_nn(dgu, w_gu_t, F32, "mm_du2")
    dx1, acc3 = ln_modulate_bwd(x1, du2, mod6, 4, dx1_part, "ln_modulate_2_bwd")
    dh1, dx_part, acc2 = resid_ln_bwd(xt, h1, mod6, 2, ln1_g, ln1_b, dx1, False, "resid_ln_1_bwd")
    g_o = mm_tn(merged, dh1, "mm_grad_out_proj")
    dmerged = mm_nt(dh1, w_oo, BF16, "mm_dmerged")
    dya, dyb, dproj = merge_gates_bwd(dmerged, ya, yb, proj)
    g_ba_full = mm_tn(o_a, dya, "mm_grad_branch_a")
    g_bb_full = mm_tn(o_b, dyb, "mm_grad_branch_b")
    doa = mm_nt(dya, w_ba, F32, "mm_doa")
    dob = mm_nt(dyb, w_bb, F32, "mm_dob")
    my_core = lax.axis_index("c")

    def by_core(blocks, rows, slots):
        contrib = jnp.concatenate([jnp.pad(b.reshape(N_DEV, -1, D), ((0, 0), (0, p - r), (0, 0)))
                                   for b, r, p in zip(blocks, rows, slots)], axis=1)
        split = contrib.reshape(N_CHIP, 2, contrib.shape[1], D).transpose(1, 0, 2, 3)
        return (lax.dynamic_index_in_dim(split, my_core, 0, keepdims=False),
                lax.dynamic_index_in_dim(split, 1 - my_core, 0, keepdims=False))

    keep_e, give_e = by_core([g_ba_full, g_bb_full, g_o, g_gu_t[:D_FF], g_gu_t[D_FF:], g_dn],
                             shard_rows[1:], slot_rows[1:])
    dproj, dlb, dgn, got_e = hgrn_bwd(proj, lb, hgrn_gnorm, o_raw, doa, st_a, give_e, dproj)
    chip_e = sum_pair(keep_e.reshape(-1, D), got_e.reshape(-1, D), "sum_grads_rest_chip").reshape(keep_e.shape)
    dxs, dbm, dcm, dproj, ddt, dwn, dalog, dbias, ddsk, parts_e = ssd_bwd(proj, xc, alog4, bias4, dskip4, ssm_norm,
                                                                          expand, dob, st_b, chip_e, dproj)
    dxc = jnp.concatenate([dxs, dbm, dcm], axis=1)
    dproj, dcw, dcb = conv_bwd(proj, dxc, conv_w, ssm_conv_b, dproj)
    dproj = dt_fill(ddt, dproj)
    g_in_t = mm_tn(dproj, u1, "mm_grad_in_proj")
    keep_l = _grad_in_blocks(g_in_t, my_core, slot_rows[0])
    give_l = _grad_in_blocks(g_in_t, 1 - my_core, slot_rows[0])
    got_l = exchange_sibling(give_l, "exchange_grad_in_sibling")
    chip_l = sum_pair(keep_l.reshape(-1, D), got_l.reshape(-1, D), "sum_grad_in_chip").reshape(keep_l.shape)
    du1, parts_l = mm_nn_exchange(dproj, w_in_t, F32, chip_l, "mm_du1")
    dx, acc1 = ln_modulate_bwd(xt, du1, mod6, 1, dx_part, "ln_modulate_1_bwd")
    gw_in = sum_parts(parts_l, "sum_grad_in")[:shard_rows[0]].T
    g_rows = sum_parts(parts_e, "sum_grads_rest")
    gw_ba, gw_bb, gw_o, gw_fg, gw_fu, gw_fd = (g_rows[o - slot_rows[0]:o - slot_rows[0] + r]
                                               for o, r in zip(row_offs[1:], shard_rows[1:]))
    gw_fg, gw_fu = gw_fg.T, gw_fu.T

    dmod = jnp.concatenate([acc1[1:2], acc1[0:1], acc2[0:1], acc3[1:2], acc3[0:1], acc4[0:1]], axis=1)
    small_fields = [dmod, acc4[3:4, :128], dlb, dgn, dcw[:CONV_TAPS], dcb, dbias, dalog, ddsk, dwn,
                    acc2[1:2], acc2[2:3], acc4[1:2], acc4[2:3]]
    small_out, offs = _pack(small_fields)
    small_sum_in = allgather_vmem(small_out, "allgather_small_grads")
    gsum, g_lb = reduce_small(small_sum_in, hgrn_lb, offs[2])
    (g_bada, loss_row, _, g_gn, g_cw_full, g_cb, g_bias4, g_alog4, g_dsk4, g_wn, g_l1g, g_l1b, g_l2g, g_l2b) = _unpack(
        gsum, offs, [(1, 6 * D), (1, 128), (1, D), (1, HK), (CONV_TAPS, CONV_DIM), (1, CONV_DIM),
                     (N_GROUPS, N_STATE), (N_GROUPS, N_STATE), (N_GROUPS, N_STATE), (1, B_INNER),
                     (1, D), (1, D), (1, D), (1, D)])
    loss = loss_row[0, 0]
    g_cw = lax.dynamic_slice(g_cw_full, (0, me * conv_cols), (CONV_TAPS, conv_cols))[None]
    g_dtb = g_bias4[:, :HEADS_PER_GROUP].reshape(1, 32)
    g_alog = g_alog4[:, :HEADS_PER_GROUP].reshape(1, 32)
    g_dsk = g_dsk4[:, :HEADS_PER_GROUP].reshape(1, 32)

    dmod_all = small_sum_in[:, 0, offs[0]:offs[0] + 6 * D]
    dmod_cols = lax.dynamic_slice(dmod_all, (0, me * ada_cols), (N_DEV, ada_cols))
    gw_ada = ada_weight_grad(c_all, dmod_cols)

    big = [("ada", w_ada[0], gw_ada, m_w_ada[0], v_w_ada[0]), ("in", w_in[0], gw_in, m_w_in[0], v_w_in[0]),
           ("branch_a", w_branch_a[0], gw_ba, m_w_branch_a[0], v_w_branch_a[0]),
           ("branch_b", w_branch_b[0], gw_bb, m_w_branch_b[0], v_w_branch_b[0]),
           ("o", w_o[0], gw_o, m_w_o[0], v_w_o[0]),
           ("ffn_gate", w_ffn_gate[0], gw_fg, m_w_ffn_gate[0], v_w_ffn_gate[0]),
           ("ffn_up", w_ffn_up[0], gw_fu, m_w_ffn_up[0], v_w_ffn_up[0]),
           ("ffn_down", w_ffn_down[0], gw_fd, m_w_ffn_down[0], v_w_ffn_down[0])]
    big_out = {}
    for nm, w_, g_, m_, v_ in big:
        d_, m2_, v2_ = adamw(w_, g_, m_, v_, "adamw_" + nm)
        big_out[nm] = (g_[None], d_[None], m2_[None], v2_[None])

    small_w = [b_ada, hgrn_lb, hgrn_gnorm, ssm_conv_w, ssm_conv_b, ssm_dt_bias, ssm_a_log, ssm_d, ssm_norm,
               ln1_g, ln1_b, ln2_g, ln2_b]
    small_g = [g_bada, g_lb, g_gn, g_cw, g_cb, g_dtb, g_alog, g_dsk, g_wn, g_l1g, g_l1b, g_l2g, g_l2b]
    small_m = [m_b_ada, m_hgrn_lb, m_hgrn_gnorm, m_ssm_conv_w, m_ssm_conv_b, m_ssm_dt_bias, m_ssm_a_log, m_ssm_d,
               m_ssm_norm, m_ln1_g, m_ln1_b, m_ln2_g, m_ln2_b]
    small_v = [v_b_ada, v_hgrn_lb, v_hgrn_gnorm, v_ssm_conv_w, v_ssm_conv_b, v_ssm_dt_bias, v_ssm_a_log, v_ssm_d,
               v_ssm_norm, v_ln1_g, v_ln1_b, v_ln2_g, v_ln2_b]
    shapes = [a.shape for a in small_w]
    small_g = [g_.reshape(s) for g_, s in zip(small_g, shapes)]
    pw, poffs = _pack(small_w)
    pg, _ = _pack(small_g)
    pm, _ = _pack(small_m)
    pv, _ = _pack(small_v)
    pd, pm2, pv2 = adamw(pw, pg, pm, pv, "adamw_small")
    s_d, s_m, s_v = (_unpack(p, poffs, shapes) for p in (pd, pm2, pv2))
    (sn_bada, sn_lb, sn_gn, sn_cw, sn_cb, sn_dtb, sn_alog, sn_dsk, sn_wn, sn_l1g, sn_l1b, sn_l2g, sn_l2b) = range(13)

    def order(kind):
        sm = [small_g, s_d, s_m, s_v][kind]
        bg = lambda nm: big_out[nm][kind]
        return [bg("ada"), sm[sn_bada], bg("in"), sm[sn_lb], sm[sn_gn], sm[sn_cw], sm[sn_cb], sm[sn_dtb], sm[sn_alog],
                sm[sn_dsk], sm[sn_wn], bg("branch_a"), bg("branch_b"), bg("o"), sm[sn_l1g], sm[sn_l1b],
                bg("ffn_gate"), bg("ffn_up"), bg("ffn_down"), sm[sn_l2g], sm[sn_l2b]]

    return (loss, dx[None], *order(0), *order(1), *order(2), *order(3))
```

```python
import numpy as np
import jax
import jax.numpy as jnp
from jax import lax
from jax.experimental import pallas as pl
from jax.experimental.pallas import tpu as pltpu

F32 = jnp.float32
BF16 = jnp.bfloat16
HI = lax.Precision.HIGHEST

N_DEV = 8
D = 1024
N_HEADS_A = 8
HK = 128
CHUNK = 64
SSD_CHUNK = 128
SSD_CHUNK_BWD = 256
N_GROUPS = 4
HEADS_PER_GROUP = 8
HEAD_P = 64
N_STATE = 128
GROUP_W = HEADS_PER_GROUP * HEAD_P
B_INNER = 2048
CONV_DIM = 3072
D_FF = 2816
IN_DIM = 11296
N_PROJ = 12288
ALPHA = 2.0 ** 0.25
LN_EPS = 1e-5
RMS_EPS = 1e-6
Q_SCALE = 128 ** -0.5
EXP_CLIP = 80.0
ADAM_LR, ADAM_B1, ADAM_B2, ADAM_EPS, ADAM_WD, ADAM_STEP = 0.001, 0.9, 0.999, 1e-8, 0.01, 10
VMEM_LIMIT = 48 * 1024 * 1024
TOKEN_BLOCK = 512
ROW_TILE = 256
FFN_ROW_TILE = 128
MM_ROW_TILE = 1024
MM_TOKEN_TILE = 4096
MM_K_TILE = 3072
MM_COL_TILE = 1408
HGRN_HEADS_PER_STEP = 4
CHUNK_UNROLL = 8
MESH_ID = pl.DeviceIdType.MESH

NT_DIMS = (((1,), (1,)), ((), ()))
TN_DIMS = (((0,), (0,)), ((), ()))


def _cparams(sem=None):
    return pltpu.CompilerParams(dimension_semantics=sem, vmem_limit_bytes=VMEM_LIMIT)


def _sigmoid(x):
    return 1.0 / (1.0 + jnp.exp(-x))


def _dsilu(x, s):
    return s * (1.0 + x * (1.0 - s))


def _nt(a, b, precision=None):
    return lax.dot_general(a, b, NT_DIMS, precision=precision, preferred_element_type=F32)


def _tn(a, b, precision=None):
    return lax.dot_general(a, b, TN_DIMS, precision=precision, preferred_element_type=F32)


def _nn(a, b, precision=None):
    return jnp.dot(a, b, precision=precision, preferred_element_type=F32)


def _split(x, pieces):
    out = []
    for i in range(pieces):
        p = x.astype(BF16)
        out.append(p)
        if i + 1 < pieces:
            x = x - p.astype(F32)
    return out


def _sel(dot, x, sel01, pieces, x_first=True):
    acc = None
    for p in _split(x, pieces):
        term = dot(p, sel01) if x_first else dot(sel01, p)
        acc = term if acc is None else acc + term
    return acc


def _ln(x):
    mu = jnp.mean(x, axis=-1, keepdims=True)
    xc = x - mu
    rstd = lax.rsqrt(jnp.mean(xc * xc, axis=-1, keepdims=True) + LN_EPS)
    return xc * rstd, rstd


def _ln_bwd(dxh, xh, rstd):
    return rstd * (dxh - jnp.mean(dxh, axis=-1, keepdims=True) - xh * jnp.mean(dxh * xh, axis=-1, keepdims=True))


def _colsum(x):
    return jnp.sum(x, axis=0, keepdims=True)


def _tri(n, upper=False):
    r = lax.broadcasted_iota(jnp.int32, (n, n), 0)
    c = lax.broadcasted_iota(jnp.int32, (n, n), 1)
    return (c >= r) if upper else (r >= c)


def _my_pos():
    return lax.axis_index("x"), lax.axis_index("y"), lax.axis_index("c")


def _peer(pos, k):
    x, y, c = pos
    return (x ^ ((k >> 2) & 1), y ^ ((k >> 1) & 1), c ^ (k & 1))


def _flat(pos):
    return 4 * pos[0] + 2 * pos[1] + pos[2]


def allgather_vmem(v, name):
    n = v.shape[1]

    def body(v_ref, o_ref, send_sems, recv_sems, local_sem):
        me = _my_pos()
        mine = pltpu.make_async_copy(v_ref, o_ref.at[_flat(me)], local_sem)
        mine.start()
        sends = []
        for k in range(1, N_DEV):
            peer = _peer(me, k)
            cp = pltpu.make_async_remote_copy(v_ref, o_ref.at[_flat(me)], send_sems.at[k - 1], recv_sems.at[k - 1],
                                              device_id=peer, device_id_type=MESH_ID)
            cp.start()
            sends.append(cp)
        for k in range(1, N_DEV):
            peer = _peer(me, k)
            pltpu.make_async_remote_copy(v_ref, o_ref.at[_flat(peer)], send_sems.at[k - 1], recv_sems.at[k - 1],
                                         device_id=peer, device_id_type=MESH_ID).wait_recv()
        for cp in sends:
            cp.wait_send()
        mine.wait()

    return pl.pallas_call(
        body, name=name,
        out_shape=jax.ShapeDtypeStruct((N_DEV, 1, n), F32),
        in_specs=[pl.BlockSpec(memory_space=pltpu.VMEM)],
        out_specs=pl.BlockSpec(memory_space=pltpu.VMEM),
        scratch_shapes=[pltpu.SemaphoreType.DMA((N_DEV - 1,)), pltpu.SemaphoreType.DMA((N_DEV - 1,)),
                        pltpu.SemaphoreType.DMA],
        compiler_params=_cparams(),
    )(v)


def ada_modulation(c_all, w_ada_s, b_ada_r):
    ncol = w_ada_s.shape[1]

    def body(c_ref, w_ref, b_ref, o_ref, part_ref, send_sems, recv_sems):
        me = _my_pos()
        cval = c_ref[...]
        cond = cval * _sigmoid(cval)
        part = _nn(cond, w_ref[...], HI)
        for r in range(N_DEV):
            part_ref[r] = part[r:r + 1, :]
        sends = []
        for k in range(1, N_DEV):
            peer = _peer(me, k)
            cp = pltpu.make_async_remote_copy(part_ref.at[_flat(peer)], o_ref.at[_flat(me)], send_sems.at[k - 1],
                                              recv_sems.at[k - 1], device_id=peer, device_id_type=MESH_ID)
            cp.start()
            sends.append(cp)
        o_ref[_flat(me)] = part_ref[_flat(me)]
        for k in range(1, N_DEV):
            peer = _peer(me, k)
            pltpu.make_async_remote_copy(part_ref.at[_flat(peer)], o_ref.at[_flat(peer)], send_sems.at[k - 1],
                                         recv_sems.at[k - 1], device_id=peer, device_id_type=MESH_ID).wait_recv()
        for cp in sends:
            cp.wait_send()
        o_ref[...] = o_ref[...] + b_ref[...]

    return pl.pallas_call(
        body, name="ada_modulation",
        out_shape=jax.ShapeDtypeStruct((N_DEV, 1, ncol), F32),
        in_specs=[pl.BlockSpec(memory_space=pltpu.VMEM)] * 3,
        out_specs=pl.BlockSpec(memory_space=pltpu.VMEM),
        scratch_shapes=[pltpu.VMEM((N_DEV, 1, ncol), F32), pltpu.SemaphoreType.DMA((N_DEV - 1,)),
                        pltpu.SemaphoreType.DMA((N_DEV - 1,))],
        compiler_params=_cparams(),
    )(c_all, w_ada_s, b_ada_r)


def allgather_hbm(shard, name):
    def body(x_ref, out_ref, send_sems, recv_sems, local_sem):
        x, y, c = _my_pos()
        me, sibling = (x, y, c), (x, y, 1 - c)
        chips = [(1 - x, y), (x, 1 - y), (1 - x, 1 - y)]

        def slot(pos):
            return out_ref.at[_flat(pos)]

        def copy(k, block, to, src=None):
            return pltpu.make_async_remote_copy(slot(block) if src is None else src, slot(block), send_sems.at[k],
                                                recv_sems.at[k], device_id=to, device_id_type=MESH_ID)

        mine = pltpu.make_async_copy(x_ref, slot(me), local_sem)
        mine.start()
        first = [copy(0, me, sibling, src=x_ref)]
        first += [copy(1 + j, me, (*chip, c), src=x_ref) for j, chip in enumerate(chips)]
        for cp in first:
            cp.start()
        passed = [copy(4 + j, (*chip, c), sibling) for j, chip in enumerate(chips)]
        for j, chip in enumerate(chips):
            copy(1 + j, (*chip, c), me).wait_recv()
            passed[j].start()
        copy(0, sibling, me).wait_recv()
        for j, chip in enumerate(chips):
            copy(4 + j, (*chip, 1 - c), me).wait_recv()
        for cp in first + passed:
            cp.wait_send()
        mine.wait()

    return pl.pallas_call(
        body, name=name,
        out_shape=jax.ShapeDtypeStruct((N_DEV,) + shard.shape, shard.dtype),
        in_specs=[pl.BlockSpec(memory_space=pl.ANY)],
        out_specs=pl.BlockSpec(memory_space=pl.ANY),
        scratch_shapes=[pltpu.SemaphoreType.DMA((N_DEV - 1,)), pltpu.SemaphoreType.DMA((N_DEV - 1,)),
                        pltpu.SemaphoreType.DMA],
        compiler_params=_cparams(),
    )(shard)


N_CHIP = N_DEV // 2
SIBLING_SEMS = [pltpu.SemaphoreType.DMA, pltpu.SemaphoreType.DMA]
CHIP_SEMS = [pltpu.SemaphoreType.DMA((N_CHIP - 1,)), pltpu.SemaphoreType.DMA((N_CHIP - 1,)), pltpu.SemaphoreType.DMA]


def _sibling_exchange(s_ref, o_ref, send_sem, recv_sem):
    x, y, c = _my_pos()
    cp = pltpu.make_async_remote_copy(s_ref, o_ref, send_sem, recv_sem, device_id=(x, y, 1 - c), device_id_type=MESH_ID)
    return cp.start, cp.wait


def _chip_exchange(p_ref, o_ref, send_sems, recv_sems, local_sem):
    x, y, c = _my_pos()
    my_chip = 2 * x + y
    mine = pltpu.make_async_copy(p_ref.at[my_chip], o_ref.at[my_chip], local_sem)
    peers = [(x ^ (k >> 1), y ^ (k & 1)) for k in range(1, N_CHIP)]
    sends = [pltpu.make_async_remote_copy(p_ref.at[2 * px + py], o_ref.at[my_chip], send_sems.at[k], recv_sems.at[k],
                                          device_id=(px, py, c), device_id_type=MESH_ID)
             for k, (px, py) in enumerate(peers)]
    recvs = [pltpu.make_async_remote_copy(p_ref.at[2 * px + py], o_ref.at[2 * px + py], send_sems.at[k], recv_sems.at[k],
                                          device_id=(px, py, c), device_id_type=MESH_ID)
             for k, (px, py) in enumerate(peers)]

    def start():
        mine.start()
        for cp in sends:
            cp.start()

    def wait():
        for cp in recvs:
            cp.wait_recv()
        for cp in sends:
            cp.wait_send()
        mine.wait()

    return start, wait


def exchange_sibling(send, name):
    def body(s_ref, o_ref, send_sem, recv_sem):
        start, wait = _sibling_exchange(s_ref, o_ref, send_sem, recv_sem)
        start()
        wait()

    return pl.pallas_call(
        body, name=name,
        out_shape=jax.ShapeDtypeStruct(send.shape, send.dtype),
        in_specs=[pl.BlockSpec(memory_space=pl.ANY)],
        out_specs=pl.BlockSpec(memory_space=pl.ANY),
        scratch_shapes=SIBLING_SEMS,
        compiler_params=_cparams(),
    )(send)


LANES = 128


def _k_tile(kdim, unit=LANES):
    for cand in range(MM_K_TILE - MM_K_TILE % unit, 0, -unit):
        if kdim % cand == 0:
            return cand
    return kdim


def _lane_tile(n, cap):
    for cand in range(cap - cap % LANES, 0, -LANES):
        if n % cand == 0:
            return cand
    return n


def mm_nn(a, b, out_dtype, name):
    m, kdim = a.shape
    n = b.shape[1]
    tm, tn, tk = min(MM_ROW_TILE, m), _lane_tile(n, MM_COL_TILE), _k_tile(kdim)
    nk = kdim // tk

    def body(a_ref, b_ref, o_ref, acc_ref):
        p = _nn(a_ref[...], b_ref[...])
        if nk == 1:
            o_ref[...] = p.astype(o_ref.dtype)
        else:
            k = pl.program_id(2)

            @pl.when(k == 0)
            def _():
                acc_ref[...] = p

            @pl.when(k > 0)
            def _():
                acc_ref[...] += p

            @pl.when(k == nk - 1)
            def _():
                o_ref[...] = acc_ref[...].astype(o_ref.dtype)

    return pl.pallas_call(
        body, name=name, grid=(n // tn, m // tm, nk),
        out_shape=jax.ShapeDtypeStruct((m, n), out_dtype),
        in_specs=[pl.BlockSpec((tm, tk), lambda j, i, k: (i, k)), pl.BlockSpec((tk, tn), lambda j, i, k: (k, j))],
        out_specs=pl.BlockSpec((tm, tn), lambda j, i, k: (i, j)),
        scratch_shapes=[pltpu.VMEM((tm, tn), F32)],
        compiler_params=_cparams(("parallel", "parallel", "arbitrary")),
    )(a, b)


def mm_nt(a, b, out_dtype, name):
    m, kdim = a.shape
    n = b.shape[0]
    tm, tn, tk = min(MM_ROW_TILE, m), _lane_tile(n, MM_COL_TILE), _k_tile(kdim)
    nk = kdim // tk

    def body(a_ref, b_ref, o_ref, acc_ref):
        p = _nt(a_ref[...], b_ref[...])
        if nk == 1:
            o_ref[...] = p.astype(o_ref.dtype)
        else:
            k = pl.program_id(2)

            @pl.when(k == 0)
            def _():
                acc_ref[...] = p

            @pl.when(k > 0)
            def _():
                acc_ref[...] += p

            @pl.when(k == nk - 1)
            def _():
                o_ref[...] = acc_ref[...].astype(o_ref.dtype)

    return pl.pallas_call(
        body, name=name, grid=(n // tn, m // tm, nk),
        out_shape=jax.ShapeDtypeStruct((m, n), out_dtype),
        in_specs=[pl.BlockSpec((tm, tk), lambda j, i, k: (i, k)), pl.BlockSpec((tn, tk), lambda j, i, k: (j, k))],
        out_specs=pl.BlockSpec((tm, tn), lambda j, i, k: (i, j)),
        scratch_shapes=[pltpu.VMEM((tm, tn), F32)],
        compiler_params=_cparams(("parallel", "parallel", "arbitrary")),
    )(a, b)


def mm_nn_exchange(a, b, out_dtype, part, name):
    kblocks, m, kb = a.shape
    kdim = kblocks * kb
    n = b.shape[1]
    tm, tn, tk = min(MM_ROW_TILE, m), _lane_tile(n, MM_COL_TILE), _k_tile(kdim)
    gn, gm, nk = n // tn, m // tm, kdim // tk
    per_step = tk // kb

    def body(a_ref, b_ref, part_ref, o_ref, parts_ref, acc_ref, send_sems, recv_sems, local_sem):
        j, i, k = pl.program_id(0), pl.program_id(1), pl.program_id(2)
        xchg_start, xchg_wait = _chip_exchange(part_ref, parts_ref, send_sems, recv_sems, local_sem)

        @pl.when((j == 0) & (i == 0) & (k == 0))
        def _():
            xchg_start()

        p = _nn(a_ref[0], b_ref[0:kb, :])
        for c in range(1, per_step):
            p = p + _nn(a_ref[c], b_ref[c * kb:(c + 1) * kb, :])

        @pl.when(k == 0)
        def _():
            acc_ref[...] = p

        @pl.when(k > 0)
        def _():
            acc_ref[...] += p

        @pl.when(k == nk - 1)
        def _():
            o_ref[...] = acc_ref[...].astype(o_ref.dtype)

        @pl.when((j == gn - 1) & (i == gm - 1) & (k == nk - 1))
        def _():
            xchg_wait()

    hbm = pl.BlockSpec(memory_space=pl.ANY)
    return pl.pallas_call(
        body, name=name, grid=(gn, gm, nk),
        out_shape=[jax.ShapeDtypeStruct((m, n), out_dtype), jax.ShapeDtypeStruct(part.shape, part.dtype)],
        in_specs=[pl.BlockSpec((per_step, tm, kb), lambda j, i, k: (k, i, 0)),
                  pl.BlockSpec((tk, tn), lambda j, i, k: (k, j)), hbm],
        out_specs=[pl.BlockSpec((tm, tn), lambda j, i, k: (i, j)), hbm],
        scratch_shapes=[pltpu.VMEM((tm, tn), F32)] + CHIP_SEMS,
        compiler_params=_cparams(("arbitrary", "arbitrary", "arbitrary")),
    )(a, b, part)


def mm_nt_gather(a, b, out_dtype, shard, name):
    m, kdim = a.shape
    n = b.shape[0]
    tm, tn = min(MM_ROW_TILE, m), 1024
    assert kdim == 1024
    gj = m // tm
    nsteps = (n // tn) * gj
    forward_step = max(nsteps - 8, 0)

    def body(a_ref, b_ref, x_ref, o_ref, g_ref, send_sems, recv_sems, local_sem):
        step = pl.program_id(0) * gj + pl.program_id(1)
        x, y, c = _my_pos()
        me, sibling = (x, y, c), (x, y, 1 - c)
        chips = [(1 - x, y), (x, 1 - y), (1 - x, 1 - y)]

        def slot(pos):
            return g_ref.at[_flat(pos)]

        def copy(k, block, to, src=None):
            return pltpu.make_async_remote_copy(slot(block) if src is None else src, slot(block), send_sems.at[k],
                                                recv_sems.at[k], device_id=to, device_id_type=MESH_ID)

        mine = pltpu.make_async_copy(x_ref, slot(me), local_sem)
        first = [copy(0, me, sibling, src=x_ref)]
        first += [copy(1 + j, me, (*chip, c), src=x_ref) for j, chip in enumerate(chips)]
        passed = [copy(4 + j, (*chip, c), sibling) for j, chip in enumerate(chips)]

        @pl.when(step == 0)
        def _():
            mine.start()
            for cp in first:
                cp.start()

        o_ref[...] = _nt(a_ref[...], b_ref[...]).astype(o_ref.dtype)

        @pl.when(step == forward_step)
        def _():
            for j, chip in enumerate(chips):
                copy(1 + j, (*chip, c), me).wait_recv()
                passed[j].start()

        @pl.when(step == nsteps - 1)
        def _():
            copy(0, sibling, me).wait_recv()
            for j, chip in enumerate(chips):
                copy(4 + j, (*chip, 1 - c), me).wait_recv()
            for cp in first + passed:
                cp.wait_send()
            mine.wait()

    return pl.pallas_call(
        body, name=name, grid=(n // tn, gj),
        out_shape=[jax.ShapeDtypeStruct((m, n), out_dtype), jax.ShapeDtypeStruct((N_DEV,) + shard.shape, shard.dtype)],
        in_specs=[pl.BlockSpec((tm, kdim), lambda j, i: (i, 0)), pl.BlockSpec((tn, kdim), lambda j, i: (j, 0)),
                  pl.BlockSpec(memory_space=pl.ANY)],
        out_specs=[pl.BlockSpec((tm, tn), lambda j, i: (i, j)), pl.BlockSpec(memory_space=pl.ANY)],
        scratch_shapes=[pltpu.SemaphoreType.DMA((N_DEV - 1,)), pltpu.SemaphoreType.DMA((N_DEV - 1,)),
                        pltpu.SemaphoreType.DMA],
        compiler_params=_cparams(("arbitrary", "arbitrary")),
    )(a, b, shard)


def mm_tn(a, b, name):
    tt, tn = min(MM_TOKEN_TILE, b.shape[0]), _lane_tile(b.shape[1], MM_COL_TILE)
    tka = _lane_tile(a.shape[0] * a.shape[2] if a.ndim == 3 else a.shape[1], 1024)
    if a.ndim == 3:
        t, ka = a.shape[1], a.shape[0] * a.shape[2]
        a_spec = pl.BlockSpec((None, tt, tka), lambda i, j, s: (i, s, 0))
    else:
        t, ka = a.shape
        a_spec = pl.BlockSpec((tt, tka), lambda i, j, s: (s, i))
    n = b.shape[1]
    nt = t // tt

    def body(a_ref, b_ref, o_ref, *acc):
        p = _tn(a_ref[...], b_ref[...])
        if nt == 1:
            o_ref[...] = p.astype(o_ref.dtype)
        else:
            acc_ref, s = acc[0], pl.program_id(2)

            @pl.when(s == 0)
            def _():
                acc_ref[...] = p

            @pl.when(s > 0)
            def _():
                acc_ref[...] += p

            @pl.when(s == nt - 1)
            def _():
                o_ref[...] = acc_ref[...].astype(o_ref.dtype)

    return pl.pallas_call(
        body, name=name, grid=(ka // tka, n // tn, nt),
        out_shape=jax.ShapeDtypeStruct((ka, n), BF16),
        in_specs=[a_spec, pl.BlockSpec((tt, tn), lambda i, j, s: (s, j))],
        out_specs=pl.BlockSpec((tka, tn), lambda i, j, s: (i, j)),
        scratch_shapes=[] if nt == 1 else [pltpu.VMEM((tka, tn), F32)],
        compiler_params=_cparams(("parallel", "parallel", "arbitrary")),
    )(a, b)


def _tile(t, cap):
    return min(cap, t)


def ln_modulate(x, mod6, shift_row, scale_row, name):
    t = x.shape[0]
    tm = _tile(t, ROW_TILE)

    def body(x_ref, mod_ref, o_ref):
        xh, _ = _ln(x_ref[...])
        sc = mod_ref[scale_row:scale_row + 1, :]
        sh = mod_ref[shift_row:shift_row + 1, :]
        o_ref[...] = (xh * (1.0 + sc) + sh).astype(BF16)

    return pl.pallas_call(
        body, name=name, grid=(t // tm,),
        out_shape=jax.ShapeDtypeStruct((t, D), BF16),
        in_specs=[pl.BlockSpec((tm, D), lambda i: (i, 0)), pl.BlockSpec((6, D), lambda i: (0, 0))],
        out_specs=pl.BlockSpec((tm, D), lambda i: (i, 0)),
        compiler_params=_cparams(("parallel",)),
    )(x, mod6)


def resid_ln(x, h, mod6, gate_row, ln_g, ln_b, name):
    t = x.shape[0]
    tm = _tile(t, ROW_TILE)

    def body(x_ref, h_ref, mod_ref, g_ref, b_ref, o_ref):
        r = ALPHA * x_ref[...] + mod_ref[gate_row:gate_row + 1, :] * h_ref[...]
        rh, _ = _ln(r)
        o_ref[...] = rh * g_ref[...] + b_ref[...]

    row = pl.BlockSpec((tm, D), lambda i: (i, 0))
    vec = pl.BlockSpec((1, D), lambda i: (0, 0))
    return pl.pallas_call(
        body, name=name, grid=(t // tm,),
        out_shape=jax.ShapeDtypeStruct((t, D), F32),
        in_specs=[row, row, pl.BlockSpec((6, D), lambda i: (0, 0)), vec, vec],
        out_specs=row,
        compiler_params=_cparams(("parallel",)),
    )(x, h, mod6, ln_g, ln_b)


def resid_ln_bwd(x, h, mod6, gate_row, ln_g, ln_b, cot, with_loss, name):
    t = x.shape[0]
    tm = _tile(t, ROW_TILE)

    def body(x_ref, h_ref, mod_ref, g_ref, b_ref, c_ref, dh_ref, dx_ref, acc_ref):
        @pl.when(pl.program_id(0) == 0)
        def _():
            acc_ref[...] = jnp.zeros_like(acc_ref)

        gate = mod_ref[gate_row:gate_row + 1, :]
        hv = h_ref[...]
        r = ALPHA * x_ref[...] + gate * hv
        rh, rstd = _ln(r)
        lng = g_ref[...]
        if with_loss:
            diff = rh * lng + b_ref[...] - c_ref[...]
            dxo = diff * (1.0 / D)
            lsum = jnp.sum(_colsum(diff * diff), axis=-1, keepdims=True) * (0.5 / D)
            acc_ref[3:4, :] += jnp.broadcast_to(lsum, (1, D))
        else:
            dxo = c_ref[...]
        acc_ref[1:2, :] += _colsum(dxo * rh)
        acc_ref[2:3, :] += _colsum(dxo)
        dr = _ln_bwd(dxo * lng, rh, rstd)
        acc_ref[0:1, :] += _colsum(dr * hv)
        dh_ref[...] = (gate * dr).astype(BF16)
        dx_ref[...] = ALPHA * dr

    row = pl.BlockSpec((tm, D), lambda i: (i, 0))
    vec = pl.BlockSpec((1, D), lambda i: (0, 0))
    return pl.pallas_call(
        body, name=name, grid=(t // tm,),
        out_shape=[jax.ShapeDtypeStruct((t, D), BF16), jax.ShapeDtypeStruct((t, D), F32),
                   jax.ShapeDtypeStruct((8, D), F32)],
        in_specs=[row, row, pl.BlockSpec((6, D), lambda i: (0, 0)), vec, vec, row],
        out_specs=[row, row, pl.BlockSpec((8, D), lambda i: (0, 0))],
        compiler_params=_cparams(("arbitrary",)),
    )(x, h, mod6, ln_g, ln_b, cot)


def ln_modulate_bwd(x, du, mod6, scale_row, dx_part, name):
    t = x.shape[0]
    tm = _tile(t, ROW_TILE)

    def body(x_ref, du_ref, mod_ref, dp_ref, dx_ref, acc_ref):
        @pl.when(pl.program_id(0) == 0)
        def _():
            acc_ref[...] = jnp.zeros_like(acc_ref)

        xh, rstd = _ln(x_ref[...])
        du_v = du_ref[...]
        sc = mod_ref[scale_row:scale_row + 1, :]
        acc_ref[0:1, :] += _colsum(du_v * xh)
        acc_ref[1:2, :] += _colsum(du_v)
        dx_ref[...] = dp_ref[...] + _ln_bwd(du_v * (1.0 + sc), xh, rstd)

    row = pl.BlockSpec((tm, D), lambda i: (i, 0))
    return pl.pallas_call(
        body, name=name, grid=(t // tm,),
        out_shape=[jax.ShapeDtypeStruct((t, D), F32), jax.ShapeDtypeStruct((8, D), F32)],
        in_specs=[row, row, pl.BlockSpec((6, D), lambda i: (0, 0)), row],
        out_specs=[row, pl.BlockSpec((8, D), lambda i: (0, 0))],
        compiler_params=_cparams(("arbitrary",)),
    )(x, du, mod6, dx_part)


def merge_gates(ya, yb, proj):
    t = ya.shape[0]
    tm = _tile(t, ROW_TILE)

    def body(ya_ref, yb_ref, ga_ref, gb_ref, o_ref):
        o_ref[...] = (_sigmoid(ga_ref[...]) * ya_ref[...].astype(F32) +
                      _sigmoid(gb_ref[...]) * yb_ref[...].astype(F32)).astype(BF16)

    row = pl.BlockSpec((tm, D), lambda i: (i, 0))
    return pl.pallas_call(
        body, name="merge_gates", grid=(t // tm,),
        out_shape=jax.ShapeDtypeStruct((t, D), BF16),
        in_specs=[row, row, pl.BlockSpec((tm, D), lambda i: (i, GATE_BLOCK0)),
                  pl.BlockSpec((tm, D), lambda i: (i, GATE_BLOCK0 + 1))],
        out_specs=row,
        compiler_params=_cparams(("parallel",)),
    )(ya, yb, proj, proj)


def merge_gates_bwd(dm, ya, yb, proj):
    t = ya.shape[0]
    tm = _tile(t, ROW_TILE)

    def body(dm_ref, ya_ref, yb_ref, ga_ref, gb_ref, dya_ref, dyb_ref, dp_ref):
        dmv = dm_ref[...].astype(F32)
        sa = _sigmoid(ga_ref[...])
        sb = _sigmoid(gb_ref[...])
        dya_ref[...] = (dmv * sa).astype(BF16)
        dyb_ref[...] = (dmv * sb).astype(BF16)
        dp_ref[0] = (dmv * ya_ref[...].astype(F32) * sa * (1.0 - sa)).astype(BF16)
        dp_ref[1] = (dmv * yb_ref[...].astype(F32) * sb * (1.0 - sb)).astype(BF16)

    row = pl.BlockSpec((tm, D), lambda i: (i, 0))
    return pl.pallas_call(
        body, name="merge_gates_bwd", grid=(t // tm,),
        out_shape=[jax.ShapeDtypeStruct((t, D), BF16)] * 2 + [jax.ShapeDtypeStruct((N_PROJ // D, t, D), BF16)],
        in_specs=[row, row, row, pl.BlockSpec((tm, D), lambda i: (i, GATE_BLOCK0)),
                  pl.BlockSpec((tm, D), lambda i: (i, GATE_BLOCK0 + 1))],
        out_specs=[row, row, pl.BlockSpec((2, tm, D), lambda i: (GATE_BLOCK0 // 2, i, 0))],
        compiler_params=_cparams(("parallel",)),
    )(dm, ya, yb, proj, proj)


FF_CHUNK = 1408


def swiglu_act(gu):
    t = gu.shape[0]
    tm = _tile(t, FFN_ROW_TILE)

    def body(gu_ref, o_ref):
        for j in range(D_FF // FF_CHUNK):
            cs = slice(j * FF_CHUNK, (j + 1) * FF_CHUNK)
            g = gu_ref[:, cs].astype(F32)
            u = gu_ref[:, D_FF + j * FF_CHUNK:D_FF + (j + 1) * FF_CHUNK].astype(F32)
            o_ref[:, cs] = (g * _sigmoid(g) * u).astype(BF16)

    return pl.pallas_call(
        body, name="swiglu_act", grid=(t // tm,),
        out_shape=jax.ShapeDtypeStruct((t, D_FF), BF16),
        in_specs=[pl.BlockSpec((tm, 2 * D_FF), lambda i: (i, 0))],
        out_specs=pl.BlockSpec((tm, D_FF), lambda i: (i, 0)),
        compiler_params=_cparams(("parallel",)),
    )(gu)


def swiglu_act_bwd(gu, dact):
    t = gu.shape[0]
    tm = _tile(t, FFN_ROW_TILE)

    def body(gu_ref, da_ref, o_ref):
        for j in range(D_FF // FF_CHUNK):
            cs = slice(j * FF_CHUNK, (j + 1) * FF_CHUNK)
            us = slice(D_FF + j * FF_CHUNK, D_FF + (j + 1) * FF_CHUNK)
            g = gu_ref[:, cs].astype(F32)
            u = gu_ref[:, us].astype(F32)
            da = da_ref[:, cs].astype(F32)
            s = _sigmoid(g)
            o_ref[:, cs] = (da * u * _dsilu(g, s)).astype(BF16)
            o_ref[:, us] = (da * g * s).astype(BF16)

    return pl.pallas_call(
        body, name="swiglu_act_bwd", grid=(t // tm,),
        out_shape=jax.ShapeDtypeStruct((t, 2 * D_FF), BF16),
        in_specs=[pl.BlockSpec((tm, 2 * D_FF), lambda i: (i, 0)), pl.BlockSpec((tm, D_FF), lambda i: (i, 0))],
        out_specs=pl.BlockSpec((tm, 2 * D_FF), lambda i: (i, 0)),
        compiler_params=_cparams(("parallel",)),
    )(gu, dact)


def _hgrn_chunk_terms(q, fl, lbv, tril_f):
    sig = _sigmoid(fl)
    f = lbv + (1.0 - lbv) * sig
    lam = jnp.log(f)
    k = 1.0 - f
    sq = _sigmoid(q)
    qt = q * sq * Q_SCALE
    bc = _sel(_nn, lam, tril_f, 3, x_first=False)
    bmid = bc[CHUNK // 2 - 1:CHUNK // 2, :]
    bl = bc[CHUNK - 1:CHUNK, :]
    eq = jnp.exp(jnp.minimum(bc - bmid, EXP_CLIP))
    ek = jnp.exp(jnp.minimum(bmid - bc, EXP_CLIP))
    eb = jnp.exp(bc)
    ekl = jnp.exp(bl - bc)
    ebl = jnp.exp(bl)
    return sig, f, k, sq, qt, eq, ek, eb, ekl, ebl


def hgrn_fwd(proj, lb, gnorm):
    t = proj.shape[0]
    tb = _tile(t, TOKEN_BLOCK)
    ncb = tb // CHUNK

    hps = HGRN_HEADS_PER_STEP
    wide = hps * HK

    def body(q_ref, f_ref, i_ref, g_ref, lb_ref, gn_ref, oa_ref, oraw_ref, st_ref, state):
        @pl.when(pl.program_id(1) == 0)
        def _():
            state[...] = jnp.zeros_like(state)

        gn = gn_ref[...]
        mask = _tri(CHUNK)
        tril_f = mask.astype(BF16)

        def chunk(c, carry):
            sl = pl.ds(pl.multiple_of(c * CHUNK, CHUNK), CHUNK)
            for hh in range(hps):
                ln = slice(hh * HK, (hh + 1) * HK)
                q, fl, v, g = q_ref[sl, ln], f_ref[sl, ln], i_ref[sl, ln], g_ref[sl, ln]
                sig, f, k, sq, qt, eq, ek, eb, ekl, ebl = _hgrn_chunk_terms(q, fl, lb_ref[:, ln], tril_f)
                a = jnp.where(mask, _nt((qt * eq).astype(BF16), (k * ek).astype(BF16)), 0.0)
                st = state[hh]
                st_ref[hh, c] = st
                vb = v.astype(BF16)
                o = _nn(a.astype(BF16), vb) + _nt((qt * eb).astype(BF16), st.astype(BF16))
                state[hh] = st * ebl + _tn(vb, (k * ekl).astype(BF16))
                oraw_ref[sl, ln] = o
                rn = o * lax.rsqrt(jnp.mean(o * o, axis=-1, keepdims=True) + RMS_EPS)
                oa_ref[sl, ln] = (rn * gn * g * _sigmoid(g)).astype(BF16)
            return carry

        lax.fori_loop(0, ncb, chunk, 0, unroll=min(CHUNK_UNROLL, ncb))

    def col(block):
        return pl.BlockSpec((tb, wide), lambda h, j: (j, block * (N_HEADS_A // hps) + h))

    return pl.pallas_call(
        body, name="hgrn_fwd", grid=(N_HEADS_A // hps, t // tb),
        out_shape=[jax.ShapeDtypeStruct((t, D), BF16), jax.ShapeDtypeStruct((t, D), F32),
                   jax.ShapeDtypeStruct((N_HEADS_A, t // CHUNK, HK, HK), F32)],
        in_specs=[col(0), col(1), col(2), col(3), pl.BlockSpec((1, wide), lambda h, j: (0, h)),
                  pl.BlockSpec((1, HK), lambda h, j: (0, 0))],
        out_specs=[pl.BlockSpec((tb, wide), lambda h, j: (j, h)), pl.BlockSpec((tb, wide), lambda h, j: (j, h)),
                   pl.BlockSpec((hps, ncb, HK, HK), lambda h, j: (h, j, 0, 0))],
        scratch_shapes=[pltpu.VMEM((hps, HK, HK), F32)],
        compiler_params=_cparams(("parallel", "arbitrary")),
    )(proj, proj, proj, proj, lb, gnorm)


def hgrn_bwd(proj, lb, gnorm, o_raw, doa, states, give, dproj):
    t = proj.shape[0]
    tb = _tile(t, TOKEN_BLOCK)
    ncb = tb // CHUNK
    nb = t // tb
    hps = HGRN_HEADS_PER_STEP
    wide = hps * HK

    def body(q_ref, f_ref, i_ref, g_ref, lb_ref, gn_ref, oraw_ref, doa_ref, st_ref, give_ref, dp_in_ref,
             dp_ref, dlb_ref, dgn_ref, got_ref, dstate, send_sem, recv_sem):
        h, j = pl.program_id(0), pl.program_id(1)
        swap_start, swap_wait = _sibling_exchange(give_ref, got_ref, send_sem, recv_sem)

        @pl.when((h == 0) & (j == 0))
        def _():
            swap_start()

        @pl.when(j == 0)
        def _():
            dstate[...] = jnp.zeros_like(dstate)
            dlb_ref[...] = jnp.zeros_like(dlb_ref)

        @pl.when((j == 0) & (h == 0))
        def _():
            dgn_ref[...] = jnp.zeros_like(dgn_ref)

        gn = gn_ref[...]
        mask = _tri(CHUNK)
        mask_t = _tri(CHUNK, upper=True)
        tril_f = mask.astype(BF16)
        triu_f = mask_t.astype(BF16)

        def chunk(i, c0):
            c = ncb - 1 - i
            sl = pl.ds(pl.multiple_of(c * CHUNK, CHUNK), CHUNK)
            for hh in range(hps):
                ln = slice(hh * HK, (hh + 1) * HK)
                q, fl, v, g = q_ref[sl, ln], f_ref[sl, ln], i_ref[sl, ln], g_ref[sl, ln]
                lbv = lb_ref[:, ln]
                sig, f, k, sq, qt, eq, ek, eb, ekl, ebl = _hgrn_chunk_terms(q, fl, lbv, tril_f)
                qe = (qt * eq).astype(BF16)
                ke = (k * ek).astype(BF16)
                st32 = st_ref[hh, c]
                st = st32.astype(BF16)
                dst = dstate[hh]
                dstb = dst.astype(BF16)
                o = oraw_ref[sl, ln]
                rstd = lax.rsqrt(jnp.mean(o * o, axis=-1, keepdims=True) + RMS_EPS)
                rn = o * rstd
                sgm = _sigmoid(g)
                sg = g * sgm
                doa_v = doa_ref[sl, ln]
                drn = doa_v * gn * sg
                dgn_ref[...] += _colsum(doa_v * rn * sg)
                dp_ref[3, sl, ln] = (doa_v * rn * gn * _dsilu(g, sgm)).astype(BF16)
                do = rstd * (drn - rn * jnp.mean(drn * rn, axis=-1, keepdims=True))
                dob = do.astype(BF16)
                vb = v.astype(BF16)
                da = jnp.where(mask, _nt(dob, vb), 0.0).astype(BF16)
                da_t = jnp.where(mask_t, _nt(vb, dob), 0.0).astype(BF16)
                a_t = jnp.where(mask_t, _nt(ke, qe), 0.0).astype(BF16)
                kl = (k * ekl).astype(BF16)
                qb = (qt * eb).astype(BF16)
                dq_in = _nn(da, ke)
                dk_in = _nn(da_t, qe)
                dq_out = eb * _nn(dob, st)
                dk_out = ekl * _nn(vb, dstb)
                dqt = eq * dq_in + dq_out
                dk = ek * dk_in + dk_out
                dv = _nn(a_t, dob) + _nt(kl, dstb)
                dstate[hh] = dst * ebl + _tn(dob, qb)
                dbig = qe.astype(F32) * dq_in - ke.astype(F32) * dk_in + qt * dq_out - k * dk_out
                beyond = _colsum(k * dk_out) + ebl * _colsum(dst * st32)
                dlam = _sel(_nn, dbig, triu_f, 3, x_first=False) + beyond
                df = dlam / f - dk
                dp_ref[1, sl, ln] = (df * (1.0 - lbv) * sig * (1.0 - sig)).astype(BF16)
                dlb_ref[:, ln] += _colsum(df * (1.0 - sig))
                dp_ref[0, sl, ln] = (dqt * Q_SCALE * _dsilu(q, sq)).astype(BF16)
                dp_ref[2, sl, ln] = dv.astype(BF16)
            return c0

        lax.fori_loop(0, ncb, chunk, 0, unroll=min(CHUNK_UNROLL, ncb))

        @pl.when((h == N_HEADS_A // hps - 1) & (j == nb - 1))
        def _():
            swap_wait()

    def col(block):
        return pl.BlockSpec((tb, wide), lambda h, j: (nb - 1 - j, block * (N_HEADS_A // hps) + h))

    hcol = pl.BlockSpec((tb, wide), lambda h, j: (nb - 1 - j, h))
    hbm = pl.BlockSpec(memory_space=pl.ANY)
    return pl.pallas_call(
        body, name="hgrn_bwd", grid=(N_HEADS_A // hps, nb),
        out_shape=[jax.ShapeDtypeStruct(dproj.shape, dproj.dtype), jax.ShapeDtypeStruct((1, D), F32),
                   jax.ShapeDtypeStruct((1, HK), F32), jax.ShapeDtypeStruct(give.shape, give.dtype)],
        in_specs=[col(0), col(1), col(2), col(3), pl.BlockSpec((1, wide), lambda h, j: (0, h)),
                  pl.BlockSpec((1, HK), lambda h, j: (0, 0)), hcol, hcol,
                  pl.BlockSpec((hps, ncb, HK, HK), lambda h, j: (h, nb - 1 - j, 0, 0)), hbm, hbm],
        out_specs=[pl.BlockSpec((4, tb, wide), lambda h, j: (0, nb - 1 - j, h)),
                   pl.BlockSpec((1, wide), lambda h, j: (0, h)), pl.BlockSpec((1, HK), lambda h, j: (0, 0)), hbm],
        input_output_aliases={10: 0},
        scratch_shapes=[pltpu.VMEM((hps, HK, HK), F32)] + SIBLING_SEMS,
        compiler_params=_cparams(("arbitrary", "arbitrary")),
    )(proj, proj, proj, proj, lb, gnorm, o_raw, doa, states, give, dproj)


CONV_BLOCK0 = 6
CONV_TAPS = 4
HALO = 8


def conv_fwd(proj, conv_w, conv_b):
    t = proj.shape[0]
    tm = _tile(t, ROW_TILE)
    r = tm // HALO

    def body(x_ref, halo_ref, w_ref, b_ref, o_ref):
        i = pl.program_id(1)
        halo = jnp.where(i > 0, halo_ref[...], 0.0)
        ext = jnp.concatenate([halo, x_ref[...]], axis=0)
        pre = b_ref[...] + w_ref[CONV_TAPS - 1:CONV_TAPS, :] * ext[HALO:, :]
        for tap in range(CONV_TAPS - 1):
            pre = pre + w_ref[tap:tap + 1, :] * pltpu.roll(ext, CONV_TAPS - 1 - tap, axis=0)[HALO:, :]
        o_ref[...] = pre * _sigmoid(pre)

    return pl.pallas_call(
        body, name="conv_fwd", grid=(CONV_DIM // D, t // tm),
        out_shape=jax.ShapeDtypeStruct((t, CONV_DIM), F32),
        in_specs=[pl.BlockSpec((tm, D), lambda cb, i: (i, CONV_BLOCK0 + cb)),
                  pl.BlockSpec((HALO, D), lambda cb, i: (jnp.maximum(i * r - 1, 0), CONV_BLOCK0 + cb)),
                  pl.BlockSpec((CONV_TAPS, D), lambda cb, i: (0, cb)), pl.BlockSpec((1, D), lambda cb, i: (0, cb))],
        out_specs=pl.BlockSpec((tm, D), lambda cb, i: (i, cb)),
        compiler_params=_cparams(("parallel", "parallel")),
    )(proj, proj, conv_w, conv_b)


def conv_bwd(proj, dxc, conv_w, conv_b, dproj):
    t = proj.shape[0]
    tm = _tile(t, ROW_TILE)
    r = tm // HALO
    n = t // tm
    last_halo = t // HALO - 1

    def body(x_ref, prev_ref, next_ref, d_ref, dnext_ref, w_ref, b_ref, dp_in_ref, dx_ref, dw_ref, db_ref):
        i = pl.program_id(1)

        @pl.when(i == 0)
        def _():
            dw_ref[...] = jnp.zeros_like(dw_ref)
            db_ref[...] = jnp.zeros_like(db_ref)

        prev = jnp.where(i > 0, prev_ref[...], 0.0)
        ext = jnp.concatenate([prev, x_ref[...], next_ref[...]], axis=0)
        shifted = [pltpu.roll(ext, CONV_TAPS - 1 - tap, axis=0)[HALO:, :] for tap in range(CONV_TAPS - 1)]
        shifted.append(ext[HALO:, :])
        pre = b_ref[...]
        for tap in range(CONV_TAPS):
            pre = pre + w_ref[tap:tap + 1, :] * shifted[tap]
        s = _sigmoid(pre)
        d_ext = jnp.concatenate([d_ref[...].astype(F32),
                                 jnp.where(i < n - 1, dnext_ref[0:HALO, :].astype(F32), 0.0)], axis=0)
        dpre = d_ext * _dsilu(pre, s)
        dx = w_ref[CONV_TAPS - 1:CONV_TAPS, :] * dpre[:tm, :]
        for tap in range(CONV_TAPS - 1):
            back = CONV_TAPS - 1 - tap
            dx = dx + w_ref[tap:tap + 1, :] * pltpu.roll(dpre, tm + HALO - back, axis=0)[:tm, :]
        dx_ref[...] = dx.astype(BF16)
        dp = dpre[:tm, :]
        db_ref[...] += _colsum(dp)
        for tap in range(CONV_TAPS):
            dw_ref[tap:tap + 1, :] += _colsum(dp * shifted[tap][:tm, :])

    return pl.pallas_call(
        body, name="conv_bwd", grid=(CONV_DIM // D, n),
        out_shape=[jax.ShapeDtypeStruct(dproj.shape, dproj.dtype), jax.ShapeDtypeStruct((8, CONV_DIM), F32),
                   jax.ShapeDtypeStruct((1, CONV_DIM), F32)],
        in_specs=[pl.BlockSpec((tm, D), lambda cb, i: (i, CONV_BLOCK0 + cb)),
                  pl.BlockSpec((HALO, D), lambda cb, i: (jnp.maximum(i * r - 1, 0), CONV_BLOCK0 + cb)),
                  pl.BlockSpec((HALO, D), lambda cb, i: (jnp.minimum((i + 1) * r, last_halo), CONV_BLOCK0 + cb)),
                  pl.BlockSpec((tm, D), lambda cb, i: (i, cb)),
                  pl.BlockSpec((2 * HALO, D), lambda cb, i: (jnp.minimum((i + 1) * (r // 2), last_halo // 2), cb)),
                  pl.BlockSpec((CONV_TAPS, D), lambda cb, i: (0, cb)), pl.BlockSpec((1, D), lambda cb, i: (0, cb)),
                  pl.BlockSpec(memory_space=pl.ANY)],
        out_specs=[pl.BlockSpec((None, tm, D), lambda cb, i: (CONV_BLOCK0 + cb, i, 0)),
                   pl.BlockSpec((8, D), lambda cb, i: (0, cb)), pl.BlockSpec((1, D), lambda cb, i: (0, cb))],
        input_output_aliases={7: 0},
        compiler_params=_cparams(("parallel", "arbitrary")),
    )(proj, proj, proj, dxc, dxc, conv_w, conv_b, dproj)


def dt_fill(ddt, dproj):
    t = ddt.shape[0]
    tm = _tile(t, ROW_TILE)
    w = ddt.shape[1]

    def body(d_ref, dp_in_ref, o_ref):
        o_ref[:, :w] = d_ref[...]
        o_ref[:, w:] = jnp.zeros((tm, D - w), o_ref.dtype)

    return pl.pallas_call(
        body, name="dt_fill", grid=(t // tm,),
        out_shape=jax.ShapeDtypeStruct(dproj.shape, dproj.dtype),
        in_specs=[pl.BlockSpec((tm, w), lambda i: (i, 0)), pl.BlockSpec(memory_space=pl.ANY)],
        out_specs=pl.BlockSpec((None, tm, D), lambda i: (DT_COL_BLOCK, i, 0)),
        input_output_aliases={1: 0},
        compiler_params=_cparams(("parallel",)),
    )(ddt, dproj)


Z_BLOCK0 = 8
DT_COL_BLOCK = 9
DT_BLOCK0 = 8 * DT_COL_BLOCK
GATE_BLOCK0 = 10
B_BLOCK0 = 16
C_BLOCK0 = 20


def _head_expand():
    e = np.zeros((N_STATE, GROUP_W), np.float32)
    for hh in range(HEADS_PER_GROUP):
        e[hh, hh * HEAD_P:(hh + 1) * HEAD_P] = 1.0
    return jnp.asarray(e, BF16)


def _ssd_chunk_terms(dt, bias, alog, expand, tril_f, eye):
    dtb = dt + bias
    delta = jnp.maximum(dtb, 0.0) + jnp.log(1.0 + jnp.exp(-jnp.abs(dtb)))
    ea = jnp.exp(alog)
    a = -ea * delta
    acum = _sel(_nn, a, tril_f, 3, x_first=False)
    delta_e = _sel(_nn, delta, expand, 2)
    acum_e = _sel(_nn, acum, expand, 3)
    acum_t = _sel(_nt, acum, eye, 3, x_first=False)
    return dtb, delta, ea, a, acum, delta_e, acum_e, acum_t


def ssd_fwd(proj, xc, alog4, bias4, dskip4, wnorm, expand):
    t = proj.shape[0]
    tb = _tile(t, TOKEN_BLOCK)
    ncb = tb // SSD_CHUNK

    def body(xs_ref, b_ref, c_ref, dt_ref, z_ref, alog_ref, bias_ref, dsk_ref, wn_ref, e_ref, ob_ref, st_ref, state):
        @pl.when(pl.program_id(1) == 0)
        def _():
            state[...] = jnp.zeros_like(state)

        expand = e_ref[...]
        mask = _tri(SSD_CHUNK)
        tril_f = mask.astype(BF16)
        eye = (lax.broadcasted_iota(jnp.int32, (N_STATE, N_STATE), 0) ==
               lax.broadcasted_iota(jnp.int32, (N_STATE, N_STATE), 1)).astype(BF16)
        alog, bias = alog_ref[0], bias_ref[0]
        d_e = _sel(_nn, jnp.broadcast_to(dsk_ref[0], (8, N_STATE)), expand, 3)[0:1, :]
        wn = wn_ref[...]

        def chunk(c, carry):
            sl = pl.ds(pl.multiple_of(c * SSD_CHUNK, SSD_CHUNK), SSD_CHUNK)
            xs, bm, cm, dt, z = xs_ref[sl, :], b_ref[sl, :], c_ref[sl, :], dt_ref[sl, :], z_ref[sl, :]
            dtb, delta, ea, a, acum, delta_e, acum_e, acum_t = _ssd_chunk_terms(dt, bias, alog, expand, tril_f, eye)
            alast_e = acum_e[SSD_CHUNK - 1:SSD_CHUNK, :]
            xd = xs * delta_e
            xdb = xd.astype(BF16)
            cb_, bb_ = cm.astype(BF16), bm.astype(BF16)
            cbm = _nt(cb_, bb_)
            ys = []
            for hh in range(HEADS_PER_GROUP):
                lh = jnp.where(mask, jnp.exp(jnp.minimum(acum[:, hh:hh + 1] - acum_t[hh:hh + 1, :], 0.0)), 0.0)
                ys.append(_nn((cbm * lh).astype(BF16), xdb[:, hh * HEAD_P:(hh + 1) * HEAD_P]))
            st = state[...]
            st_ref[0, c] = st
            y = jnp.concatenate(ys, axis=1) + _nn(cb_, st.astype(BF16)) * jnp.exp(acum_e) + xs * d_e
            state[...] = st * jnp.exp(alast_e) + _tn(bb_, (xd * jnp.exp(alast_e - acum_e)).astype(BF16))
            yg = y * z * _sigmoid(z)
            ob_ref[sl, :] = (yg * lax.rsqrt(jnp.mean(yg * yg, axis=-1, keepdims=True) + RMS_EPS) * wn).astype(BF16)
            return carry

        lax.fori_loop(0, ncb, chunk, 0, unroll=min(CHUNK_UNROLL, ncb))

    small = pl.BlockSpec((1, 1, N_STATE), lambda g, j: (g, 0, 0))
    return pl.pallas_call(
        body, name="ssd_fwd", grid=(N_GROUPS, t // tb),
        out_shape=[jax.ShapeDtypeStruct((t, B_INNER), BF16),
                   jax.ShapeDtypeStruct((N_GROUPS, t // SSD_CHUNK, N_STATE, GROUP_W), F32)],
        in_specs=[pl.BlockSpec((tb, GROUP_W), lambda g, j: (j, g)),
                  pl.BlockSpec((tb, N_STATE), lambda g, j: (j, B_BLOCK0 + g)),
                  pl.BlockSpec((tb, N_STATE), lambda g, j: (j, C_BLOCK0 + g)),
                  pl.BlockSpec((tb, N_STATE), lambda g, j: (j, DT_BLOCK0 + g)),
                  pl.BlockSpec((tb, GROUP_W), lambda g, j: (j, Z_BLOCK0 + g)),
                  small, small, small, pl.BlockSpec((1, GROUP_W), lambda g, j: (0, g)),
                  pl.BlockSpec((N_STATE, GROUP_W), lambda g, j: (0, 0))],
        out_specs=[pl.BlockSpec((tb, GROUP_W), lambda g, j: (j, g)),
                   pl.BlockSpec((1, ncb, N_STATE, GROUP_W), lambda g, j: (g, j, 0, 0))],
        scratch_shapes=[pltpu.VMEM((N_STATE, GROUP_W), F32)],
        compiler_params=_cparams(("parallel", "arbitrary")),
    )(xc, xc, xc, proj, proj, alog4, bias4, dskip4, wnorm, expand)


def ssd_bwd(proj, xc, alog4, bias4, dskip4, wnorm, expand, dob, states, part, dproj):
    t = proj.shape[0]
    tb = _tile(t, TOKEN_BLOCK)
    lc = min(SSD_CHUNK_BWD, tb)
    ncb = tb // lc
    nsaved = tb // SSD_CHUNK
    nb = t // tb

    def body(xs_ref, b_ref, c_ref, dt_ref, z_ref, alog_ref, bias_ref, dsk_ref, wn_ref, e_ref, dob_ref, st_ref, part_ref,
             dp_in_ref, dxs_ref, db_ref, dc_ref, dz_ref, ddt_ref, dwn_ref, dalog_ref, dbias_ref, ddsk_ref, parts_ref, dstate,
             send_sems, recv_sems, local_sem):
        xchg_start, xchg_wait = _chip_exchange(part_ref, parts_ref, send_sems, recv_sems, local_sem)

        @pl.when((pl.program_id(0) == 0) & (pl.program_id(1) == 0))
        def _():
            xchg_start()

        @pl.when(pl.program_id(1) == 0)
        def _():
            dstate[...] = jnp.zeros_like(dstate)
            dwn_ref[...] = jnp.zeros_like(dwn_ref)
            dalog_ref[...] = jnp.zeros_like(dalog_ref)
            dbias_ref[...] = jnp.zeros_like(dbias_ref)
            ddsk_ref[...] = jnp.zeros_like(ddsk_ref)

        expand = e_ref[...]
        mask = _tri(lc)
        mask_t = _tri(lc, upper=True)
        tril_f = mask.astype(BF16)
        triu_f = mask_t.astype(BF16)
        eye = (lax.broadcasted_iota(jnp.int32, (N_STATE, N_STATE), 0) ==
               lax.broadcasted_iota(jnp.int32, (N_STATE, N_STATE), 1)).astype(BF16)
        alog, bias = alog_ref[0], bias_ref[0]
        d_e = _sel(_nn, jnp.broadcast_to(dsk_ref[0], (8, N_STATE)), expand, 3)[0:1, :]
        wn = wn_ref[...]

        def chunk(i, c0):
            c = ncb - 1 - i
            sl = pl.ds(pl.multiple_of(c * lc, lc), lc)
            xs, bm, cm, dt, z = xs_ref[sl, :], b_ref[sl, :], c_ref[sl, :], dt_ref[sl, :], z_ref[sl, :]
            dtb, delta, ea, a, acum, delta_e, acum_e, acum_t = _ssd_chunk_terms(dt, bias, alog, expand, tril_f, eye)
            alast_e = acum_e[lc - 1:lc, :]
            eacum = jnp.exp(acum_e)
            wl = jnp.exp(alast_e - acum_e)
            xd = xs * delta_e
            xdb = xd.astype(BF16)
            cb_, bb_ = cm.astype(BF16), bm.astype(BF16)
            cbm = _nt(cb_, bb_)
            st32 = st_ref[0, c * (lc // SSD_CHUNK)]
            stb = st32.astype(BF16)
            dst = dstate[...]
            dstb = dst.astype(BF16)
            lhs, mixes, ys = [], [], []
            for hh in range(HEADS_PER_GROUP):
                col, row = acum[:, hh:hh + 1], acum_t[hh:hh + 1, :]
                lh = jnp.where(mask, jnp.exp(jnp.minimum(col - row, 0.0)), 0.0)
                mix = (cbm * lh).astype(BF16)
                lhs.append(lh)
                mixes.append(mix)
                ys.append(_nn(mix, xdb[:, hh * HEAD_P:(hh + 1) * HEAD_P]))
            y_in = jnp.concatenate(ys, axis=1)
            y_out = _nn(cb_, stb) * eacum
            y = y_in + y_out + xs * d_e
            sgz = _sigmoid(z)
            sz = z * sgz
            yg = y * sz
            rstd = lax.rsqrt(jnp.mean(yg * yg, axis=-1, keepdims=True) + RMS_EPS)
            nrm = yg * rstd
            dob_v = dob_ref[sl, :]
            dn = dob_v * wn
            dwn_ref[...] += _colsum(dob_v * nrm)
            dyg = rstd * (dn - nrm * jnp.mean(dn * nrm, axis=-1, keepdims=True))
            dy = dyg * sz
            dz_ref[sl, :] = (dyg * y * _dsilu(z, sgz)).astype(BF16)
            dyb = dy.astype(BF16)
            dxds = []
            dcb = jnp.zeros((lc, lc), F32)
            for hh in range(HEADS_PER_GROUP):
                hs = slice(hh * HEAD_P, (hh + 1) * HEAD_P)
                dy_h, x_h = dyb[:, hs], xdb[:, hs]
                dxds.append(_tn(mixes[hh], dy_h))
                dcb = dcb + _nt(dy_h, x_h) * lhs[hh]
            dcbb = dcb.astype(BF16)
            dye = (dy * eacum).astype(BF16)
            xw = (xd * wl).astype(BF16)
            dxd_in = jnp.concatenate(dxds, axis=1)
            dxd_out = wl * _nn(bb_, dstb)
            dxd = dxd_in + dxd_out
            dc_ref[sl, :] = (_nn(dcbb, bb_) + _nt(dye, stb)).astype(dc_ref.dtype)
            db_ref[sl, :] = (_tn(dcbb, cb_) + _nt(xw, dstb)).astype(db_ref.dtype)
            dstate[...] = dst * jnp.exp(alast_e) + _tn(cb_, dye)
            col_out = xd * dxd_out
            dac = _sel(_nt, dyb.astype(F32) * y_in - xdb.astype(F32) * dxd_in + dy * y_out - col_out, expand, 3)
            beyond = _colsum(col_out) + jnp.exp(alast_e) * _colsum(dst * st32)
            da = (_sel(_nn, dac, triu_f, 3, x_first=False) +
                  _sel(_nt, jnp.broadcast_to(beyond, (8, GROUP_W)), expand, 3)[0:1, :])
            ddelta = _sel(_nt, dxd * xs, expand, 2) - da * ea
            dalog_ref[0] += _colsum(da * a)
            ddtb = ddelta * _sigmoid(dtb)
            dbias_ref[0] += _colsum(ddtb)
            ddt_ref[sl, :] = ddtb.astype(BF16)
            ddsk_ref[0] += _sel(_nt, jnp.broadcast_to(_colsum(dy * xs), (8, GROUP_W)), expand, 3)[0:1, :]
            dxs_ref[sl, :] = (dxd * delta_e + dy * d_e).astype(dxs_ref.dtype)
            return c0

        lax.fori_loop(0, ncb, chunk, 0, unroll=min(CHUNK_UNROLL, ncb))

        @pl.when((pl.program_id(0) == N_GROUPS - 1) & (pl.program_id(1) == nb - 1))
        def _():
            xchg_wait()

    small = pl.BlockSpec((1, 1, N_STATE), lambda g, j: (g, 0, 0))
    wide = pl.BlockSpec((tb, GROUP_W), lambda g, j: (nb - 1 - j, g))
    narrow = pl.BlockSpec((tb, N_STATE), lambda g, j: (nb - 1 - j, g))
    hbm = pl.BlockSpec(memory_space=pl.ANY)
    return pl.pallas_call(
        body, name="ssd_bwd", grid=(N_GROUPS, nb),
        out_shape=[jax.ShapeDtypeStruct((t, B_INNER), BF16), jax.ShapeDtypeStruct((t, GROUP_W), BF16),
                   jax.ShapeDtypeStruct((t, GROUP_W), BF16), jax.ShapeDtypeStruct(dproj.shape, dproj.dtype),
                   jax.ShapeDtypeStruct((t, GROUP_W), BF16), jax.ShapeDtypeStruct((1, B_INNER), F32),
                   jax.ShapeDtypeStruct((N_GROUPS, 1, N_STATE), F32), jax.ShapeDtypeStruct((N_GROUPS, 1, N_STATE), F32),
                   jax.ShapeDtypeStruct((N_GROUPS, 1, N_STATE), F32), jax.ShapeDtypeStruct(part.shape, part.dtype)],
        in_specs=[wide,
                  pl.BlockSpec((tb, N_STATE), lambda g, j: (nb - 1 - j, B_BLOCK0 + g)),
                  pl.BlockSpec((tb, N_STATE), lambda g, j: (nb - 1 - j, C_BLOCK0 + g)),
                  pl.BlockSpec((tb, N_STATE), lambda g, j: (nb - 1 - j, DT_BLOCK0 + g)),
                  pl.BlockSpec((tb, GROUP_W), lambda g, j: (nb - 1 - j, Z_BLOCK0 + g)),
                  small, small, small, pl.BlockSpec((1, GROUP_W), lambda g, j: (0, g)),
                  pl.BlockSpec((N_STATE, GROUP_W), lambda g, j: (0, 0)), wide,
                  pl.BlockSpec((1, nsaved, N_STATE, GROUP_W), lambda g, j: (g, nb - 1 - j, 0, 0)), hbm, hbm],
        out_specs=[wide, narrow, narrow,
                   pl.BlockSpec((None, tb, GROUP_W), lambda g, j: (Z_BLOCK0 // 2 + g // 2, nb - 1 - j, g % 2)),
                   narrow, pl.BlockSpec((1, GROUP_W), lambda g, j: (0, g)), small, small, small, hbm],
        input_output_aliases={13: 3},
        scratch_shapes=[pltpu.VMEM((N_STATE, GROUP_W), F32)] + CHIP_SEMS,
        compiler_params=_cparams(("arbitrary", "arbitrary")),
    )(xc, xc, xc, proj, proj, alog4, bias4, dskip4, wnorm, expand, dob, states, part, dproj)


def lower_bound_fwd(hgrn_lb):
    def body(a_ref, o_ref):
        a0, a1 = a_ref[0:1, :], a_ref[1:2, :]
        m = jnp.maximum(a0, a1)
        e0, e1 = jnp.exp(a0 - m), jnp.exp(a1 - m)
        o_ref[...] = e0 / (e0 + e1)

    return pl.pallas_call(body, name="lower_bound_fwd", out_shape=jax.ShapeDtypeStruct((1, D), F32))(hgrn_lb)


def ada_weight_grad(c_all, dmod_cols):
    def body(c_ref, d_ref, o_ref):
        cval = c_ref[...]
        o_ref[...] = _tn(cval * _sigmoid(cval), d_ref[...], HI)

    return pl.pallas_call(body, name="ada_weight_grad",
                          out_shape=jax.ShapeDtypeStruct((D, dmod_cols.shape[1]), F32))(c_all, dmod_cols)


def reduce_small(gathered, hgrn_lb, dlb_off):
    n = gathered.shape[2]

    def body(g_ref, a_ref, o_ref, glb_ref):
        s = g_ref[0]
        for d in range(1, N_DEV):
            s = s + g_ref[d]
        o_ref[...] = s
        a0, a1 = a_ref[0:1, :], a_ref[1:2, :]
        m = jnp.maximum(a0, a1)
        e0, e1 = jnp.exp(a0 - m), jnp.exp(a1 - m)
        p0 = e0 / (e0 + e1)
        tq = s[:, dlb_off:dlb_off + D] * p0 * (1.0 - p0)
        glb_ref[0:1, :] = tq
        glb_ref[1:2, :] = -tq

    return pl.pallas_call(body, name="reduce_small",
                          out_shape=[jax.ShapeDtypeStruct((1, n), F32), jax.ShapeDtypeStruct((2, D), F32)])(gathered, hgrn_lb)


def _adam_math(w, g, m, v):
    m2 = ADAM_B1 * m + (1.0 - ADAM_B1) * g
    v2 = ADAM_B2 * v + (1.0 - ADAM_B2) * (g * g)
    m_hat = m2 / (1.0 - ADAM_B1 ** ADAM_STEP)
    v_hat = v2 / (1.0 - ADAM_B2 ** ADAM_STEP)
    delta = -ADAM_LR * (m_hat / (jnp.sqrt(v_hat) + ADAM_EPS) + ADAM_WD * w)
    return delta, m2, v2


def _row_tile(rows, mult=8, cap=128):
    for cand in range(cap - cap % mult, 0, -mult):
        if rows % cand == 0:
            return cand
    return rows


def sum_parts(parts, name):
    n, rows, cols = parts.shape
    tr = _row_tile(rows, 16, 256)

    def body(p_ref, o_ref):
        s = p_ref[0].astype(F32)
        for d in range(1, n):
            s = s + p_ref[d].astype(F32)
        o_ref[...] = s

    return pl.pallas_call(
        body, name=name, grid=(rows // tr,),
        out_shape=jax.ShapeDtypeStruct((rows, cols), F32),
        in_specs=[pl.BlockSpec((n, tr, cols), lambda i: (0, i, 0))],
        out_specs=pl.BlockSpec((tr, cols), lambda i: (i, 0)),
        compiler_params=_cparams(("parallel",)),
    )(parts)


def sum_pair(a, b, name):
    rows, cols = a.shape
    tr = _row_tile(rows, 16, 256)

    def body(a_ref, b_ref, o_ref):
        o_ref[...] = (a_ref[...].astype(F32) + b_ref[...].astype(F32)).astype(o_ref.dtype)

    blk = pl.BlockSpec((tr, cols), lambda i: (i, 0))
    return pl.pallas_call(
        body, name=name, grid=(rows // tr,),
        out_shape=jax.ShapeDtypeStruct((rows, cols), a.dtype),
        in_specs=[blk, blk], out_specs=blk,
        compiler_params=_cparams(("parallel",)),
    )(a, b)


def adamw(w, g, m, v, name):
    rows, cols = w.shape
    tr = _row_tile(rows)

    def body(w_ref, g_ref, m_ref, v_ref, d_ref, m2_ref, v2_ref):
        delta, m2, v2 = _adam_math(w_ref[...], g_ref[...], m_ref[...], v_ref[...])
        d_ref[...] = delta
        m2_ref[...] = m2
        v2_ref[...] = v2

    blk = pl.BlockSpec((tr, cols), lambda i: (i, 0))
    return pl.pallas_call(
        body, name=name, grid=(rows // tr,),
        out_shape=[jax.ShapeDtypeStruct((rows, cols), F32)] * 3,
        in_specs=[blk] * 4, out_specs=[blk] * 3,
        compiler_params=_cparams(("parallel",)),
    )(w, g, m, v)


def _pad128(n):
    return -(-n // 128) * 128


def _pack(arrays):
    offs, parts, off = [], [], 0
    for a in arrays:
        flat = a.reshape(1, -1)
        n = flat.shape[1]
        offs.append(off)
        parts.append(jnp.pad(flat, ((0, 0), (0, _pad128(n) - n))))
        off += _pad128(n)
    return jnp.concatenate(parts, axis=1), offs


def _unpack(vec, offs, shapes):
    out = []
    for off, shp in zip(offs, shapes):
        n = int(np.prod(shp))
        out.append(vec[0, off:off + n].reshape(shp))
    return out


IN_ROWS = IN_DIM // N_DEV
DT_ROW0 = 9216
DT_DEV, DT_LO = divmod(DT_ROW0, IN_ROWS)


GATE_SHIFT = D - 32


def _in_row_pieces(tile):
    pieces = []
    if tile == DT_COL_BLOCK:
        for g in range(N_GROUPS):
            o = DT_ROW0 + HEADS_PER_GROUP * g
            pieces.append((N_STATE * g, o // IN_ROWS, o % IN_ROWS, HEADS_PER_GROUP))
        return pieces
    r, end = tile * D, (tile + 1) * D
    while r < end:
        o = r if r < DT_ROW0 else r - GATE_SHIFT
        dev, loc = divmod(o, IN_ROWS)
        n = min(end - r, IN_ROWS - loc)
        pieces.append((r - tile * D, dev, loc, n))
        r += n
    return pieces


def assemble_w_in(g_all):
    ntile = N_PROJ // D

    def body(g_ref, o_ref):
        j = pl.program_id(0)
        for tile in range(ntile):
            @pl.when(j == tile)
            def _(tile=tile):
                if tile == DT_COL_BLOCK:
                    o_ref[...] = jnp.zeros_like(o_ref)
                for dst, dev, loc, n in _in_row_pieces(tile):
                    o_ref[pl.ds(dst, n), :] = g_ref[dev, pl.ds(loc, n), :]

    return pl.pallas_call(
        body, name="assemble_w_in", grid=(ntile,),
        out_shape=jax.ShapeDtypeStruct((N_PROJ, D), g_all.dtype),
        in_specs=[pl.BlockSpec(memory_space=pltpu.VMEM)],
        out_specs=pl.BlockSpec((D, D), lambda j: (j, 0)),
        compiler_params=_cparams(("arbitrary",)),
    )(g_all)


def _grad_in_blocks(g_t, core, slot):
    dt0 = DT_COL_BLOCK * D
    dt = g_t[dt0:dt0 + N_GROUPS * N_STATE].reshape(N_GROUPS, N_STATE, D)[:, :HEADS_PER_GROUP].reshape(32, D)
    with_dt = jnp.concatenate([g_t[DT_DEV * IN_ROWS:DT_ROW0], dt,
                               g_t[DT_ROW0 + 32 + GATE_SHIFT:(DT_DEV + 1) * IN_ROWS + GATE_SHIFT]], axis=0)
    blocks = []
    for q in range(N_CHIP):
        if 2 * q + 1 < DT_DEV:
            blk = lax.dynamic_slice_in_dim(g_t, IN_ROWS * (2 * q + core), IN_ROWS, axis=0)
        else:
            assert 2 * q == DT_DEV
            after = g_t[(DT_DEV + 1) * IN_ROWS + GATE_SHIFT:(DT_DEV + 2) * IN_ROWS + GATE_SHIFT]
            blk = jnp.where(core == 0, with_dt, after)
        blocks.append(jnp.pad(blk, ((0, slot - IN_ROWS), (0, 0))))
    return jnp.stack(blocks)


def kernel(x, c, w_ada, b_ada, w_in, hgrn_lb, hgrn_gnorm, ssm_conv_w, ssm_conv_b, ssm_dt_bias, ssm_a_log, ssm_d, ssm_norm, w_branch_a, w_branch_b, w_o, ln1_g, ln1_b, w_ffn_gate, w_ffn_up, w_ffn_down, ln2_g, ln2_b, loss_target, m_w_ada, m_b_ada, m_w_in, m_hgrn_lb, m_hgrn_gnorm, m_ssm_conv_w, m_ssm_conv_b, m_ssm_dt_bias, m_ssm_a_log, m_ssm_d, m_ssm_norm, m_w_branch_a, m_w_branch_b, m_w_o, m_ln1_g, m_ln1_b, m_w_ffn_gate, m_w_ffn_up, m_w_ffn_down, m_ln2_g, m_ln2_b, v_w_ada, v_b_ada, v_w_in, v_hgrn_lb, v_hgrn_gnorm, v_ssm_conv_w, v_ssm_conv_b, v_ssm_dt_bias, v_ssm_a_log, v_ssm_d, v_ssm_norm, v_w_branch_a, v_w_branch_b, v_w_o, v_ln1_g, v_ln1_b, v_w_ffn_gate, v_w_ffn_up, v_w_ffn_down, v_ln2_g, v_ln2_b):
    me = 4 * lax.axis_index("x") + 2 * lax.axis_index("y") + lax.axis_index("c")
    xt = x[0]
    tgt = loss_target[0]
    t = xt.shape[0]
    ada_cols = w_ada.shape[2]
    conv_cols = ssm_conv_w.shape[2]

    small_in, _ = _pack([c, ssm_conv_w[0]])
    small_all = allgather_vmem(small_in, "allgather_small_inputs")
    c_all = small_all[:, 0, :D]
    conv_w = small_all[:, 0, D:D + CONV_TAPS * conv_cols].reshape(N_DEV, CONV_TAPS, conv_cols)
    conv_w = conv_w.transpose(1, 0, 2).reshape(CONV_TAPS, CONV_DIM)
    mod = ada_modulation(c_all, w_ada[0], b_ada.reshape(N_DEV, 1, ada_cols))
    mod6 = mod.reshape(6, D)

    shards = [w_in[0].T, w_branch_a[0], w_branch_b[0], w_o[0], w_ffn_gate[0].T, w_ffn_up[0].T, w_ffn_down[0]]
    shard_rows = [s.shape[0] for s in shards]
    slot_rows = [-(-r // 32) * 32 for r in shard_rows]
    row_offs = [sum(slot_rows[:i]) for i in range(len(shards))]
    padded = [jnp.pad(s.astype(BF16), ((0, p - r), (0, 0))) for s, r, p in zip(shards, shard_rows, slot_rows)]
    w_in_t = assemble_w_in(allgather_hbm(padded[0], "allgather_w_in"))

    lb = lower_bound_fwd(hgrn_lb)
    u1 = ln_modulate(xt, mod6, 0, 1, "ln_modulate_1")
    proj, g_rest = mm_nt_gather(u1, w_in_t, F32, jnp.concatenate(padded[1:], axis=0), "mm_in_proj")
    g_ba, g_bb, g_o, g_fg, g_fu, g_fd = (g_rest[:, o - slot_rows[0]:o - slot_rows[0] + r]
                                         for o, r in zip(row_offs[1:], shard_rows[1:]))
    w_ba = g_ba.reshape(D, D)
    w_bb = g_bb.reshape(B_INNER, D)
    w_oo = g_o.reshape(D, D)
    w_gu_t = jnp.concatenate([g_fg.reshape(D_FF, D), g_fu.reshape(D_FF, D)], axis=0)
    w_dn = g_fd.reshape(D_FF, D)
    o_a, o_raw, st_a = hgrn_fwd(proj, lb, hgrn_gnorm)
    xc = conv_fwd(proj, conv_w, ssm_conv_b)
    pad3 = ((0, 0), (0, 0), (0, N_STATE - HEADS_PER_GROUP))
    alog4 = jnp.pad(ssm_a_log.reshape(N_GROUPS, 1, HEADS_PER_GROUP), pad3)
    bias4 = jnp.pad(ssm_dt_bias.reshape(N_GROUPS, 1, HEADS_PER_GROUP), pad3)
    dskip4 = jnp.pad(ssm_d.reshape(N_GROUPS, 1, HEADS_PER_GROUP), pad3)
    expand = _head_expand()
    o_b, st_b = ssd_fwd(proj, xc, alog4, bias4, dskip4, ssm_norm, expand)
    ya = mm_nn(o_a, w_ba, BF16, "mm_branch_a")
    yb = mm_nn(o_b, w_bb, BF16, "mm_branch_b")
    merged = merge_gates(ya, yb, proj)
    h1 = mm_nn(merged, w_oo, F32, "mm_out_proj")
    x1 = resid_ln(xt, h1, mod6, 2, ln1_g, ln1_b, "resid_ln_1")
    u2 = ln_modulate(x1, mod6, 3, 4, "ln_modulate_2")
    gu = mm_nt(u2, w_gu_t, BF16, "mm_ffn_in")
    act = swiglu_act(gu)
    h2 = mm_nn(act, w_dn, F32, "mm_ffn_out")

    dh2, dx1_part, acc4 = resid_ln_bwd(x1, h2, mod6, 5, ln2_g, ln2_b, tgt, True, "resid_ln_2_bwd")
    g_dn = mm_tn(act, dh2, "mm_grad_ffn_down")
    dact = mm_nt(dh2, w_dn, BF16, "mm_dact")
    dgu = swiglu_act_bwd(gu, dact)
    g_gu_t = mm_tn(dgu, u2, "mm_grad_ffn_in")
    du2 = mm_nn(dgu, w_gu_t, F32, "mm_du2")
    dx1, acc3 = ln_modulate_bwd(x1, du2, mod6, 4, dx1_part, "ln_modulate_2_bwd")
    dh1, dx_part, acc2 = resid_ln_bwd(xt, h1, mod6, 2, ln1_g, ln1_b, dx1, False, "resid_ln_1_bwd")
    g_o = mm_tn(merged, dh1, "mm_grad_out_proj")
    dmerged = mm_nt(dh1, w_oo, BF16, "mm_dmerged")
    dya, dyb, dproj = merge_gates_bwd(dmerged, ya, yb, proj)
    g_ba_full = mm_tn(o_a, dya, "mm_grad_branch_a")
    g_bb_full = mm_tn(o_b, dyb, "mm_grad_branch_b")
    doa = mm_nt(dya, w_ba, F32, "mm_doa")
    dob = mm_nt(dyb, w_bb, F32, "mm_dob")
    my_core = lax.axis_index("c")

    def by_core(blocks, rows, slots):
        contrib = jnp.concatenate([jnp.pad(b.reshape(N_DEV, -1, D), ((0, 0), (0, p - r), (0, 0)))
                                   for b, r, p in zip(blocks, rows, slots)], axis=1)
        split = contrib.reshape(N_CHIP, 2, contrib.shape[1], D).transpose(1, 0, 2, 3)
        return (lax.dynamic_index_in_dim(split, my_core, 0, keepdims=False),
                lax.dynamic_index_in_dim(split, 1 - my_core, 0, keepdims=False))

    keep_e, give_e = by_core([g_ba_full, g_bb_full, g_o, g_gu_t[:D_FF], g_gu_t[D_FF:], g_dn],
                             shard_rows[1:], slot_rows[1:])
    dproj, dlb, dgn, got_e = hgrn_bwd(proj, lb, hgrn_gnorm, o_raw, doa, st_a, give_e, dproj)
    chip_e = sum_pair(keep_e.reshape(-1, D), got_e.reshape(-1, D), "sum_grads_rest_chip").reshape(keep_e.shape)
    dxs, dbm, dcm, dproj, ddt, dwn, dalog, dbias, ddsk, parts_e = ssd_bwd(proj, xc, alog4, bias4, dskip4, ssm_norm,
                                                                          expand, dob, st_b, chip_e, dproj)
    dxc = jnp.concatenate([dxs, dbm, dcm], axis=1)
    dproj, dcw, dcb = conv_bwd(proj, dxc, conv_w, ssm_conv_b, dproj)
    dproj = dt_fill(ddt, dproj)
    g_in_t = mm_tn(dproj, u1, "mm_grad_in_proj")
    keep_l = _grad_in_blocks(g_in_t, my_core, slot_rows[0])
    give_l = _grad_in_blocks(g_in_t, 1 - my_core, slot_rows[0])
    got_l = exchange_sibling(give_l, "exchange_grad_in_sibling")
    chip_l = sum_pair(keep_l.reshape(-1, D), got_l.reshape(-1, D), "sum_grad_in_chip").reshape(keep_l.shape)
    du1, parts_l = mm_nn_exchange(dproj, w_in_t, F32, chip_l, "mm_du1")
    dx, acc1 = ln_modulate_bwd(xt, du1, mod6, 1, dx_part, "ln_modulate_1_bwd")
    gw_in = sum_parts(parts_l, "sum_grad_in")[:shard_rows[0]].T
    g_rows = sum_parts(parts_e, "sum_grads_rest")
    gw_ba, gw_bb, gw_o, gw_fg, gw_fu, gw_fd = (g_rows[o - slot_rows[0]:o - slot_rows[0] + r]
                                               for o, r in zip(row_offs[1:], shard_rows[1:]))
    gw_fg, gw_fu = gw_fg.T, gw_fu.T

    dmod = jnp.concatenate([acc1[1:2], acc1[0:1], acc2[0:1], acc3[1:2], acc3[0:1], acc4[0:1]], axis=1)
    small_fields = [dmod, acc4[3:4, :128], dlb, dgn, dcw[:CONV_TAPS], dcb, dbias, dalog, ddsk, dwn,
                    acc2[1:2], acc2[2:3], acc4[1:2], acc4[2:3]]
    small_out, offs = _pack(small_fields)
    small_sum_in = allgather_vmem(small_out, "allgather_small_grads")
    gsum, g_lb = reduce_small(small_sum_in, hgrn_lb, offs[2])
    (g_bada, loss_row, _, g_gn, g_cw_full, g_cb, g_bias4, g_alog4, g_dsk4, g_wn, g_l1g, g_l1b, g_l2g, g_l2b) = _unpack(
        gsum, offs, [(1, 6 * D), (1, 128), (1, D), (1, HK), (CONV_TAPS, CONV_DIM), (1, CONV_DIM),
                     (N_GROUPS, N_STATE), (N_GROUPS, N_STATE), (N_GROUPS, N_STATE), (1, B_INNER),
                     (1, D), (1, D), (1, D), (1, D)])
    loss = loss_row[0, 0]
    g_cw = lax.dynamic_slice(g_cw_full, (0, me * conv_cols), (CONV_TAPS, conv_cols))[None]
    g_dtb = g_bias4[:, :HEADS_PER_GROUP].reshape(1, 32)
    g_alog = g_alog4[:, :HEADS_PER_GROUP].reshape(1, 32)
    g_dsk = g_dsk4[:, :HEADS_PER_GROUP].reshape(1, 32)

    dmod_all = small_sum_in[:, 0, offs[0]:offs[0] + 6 * D]
    dmod_cols = lax.dynamic_slice(dmod_all, (0, me * ada_cols), (N_DEV, ada_cols))
    gw_ada = ada_weight_grad(c_all, dmod_cols)

    big = [("ada", w_ada[0], gw_ada, m_w_ada[0], v_w_ada[0]), ("in", w_in[0], gw_in, m_w_in[0], v_w_in[0]),
           ("branch_a", w_branch_a[0], gw_ba, m_w_branch_a[0], v_w_branch_a[0]),
           ("branch_b", w_branch_b[0], gw_bb, m_w_branch_b[0], v_w_branch_b[0]),
           ("o", w_o[0], gw_o, m_w_o[0], v_w_o[0]),
           ("ffn_gate", w_ffn_gate[0], gw_fg, m_w_ffn_gate[0], v_w_ffn_gate[0]),
           ("ffn_up", w_ffn_up[0], gw_fu, m_w_ffn_up[0], v_w_ffn_up[0]),
           ("ffn_down", w_ffn_down[0], gw_fd, m_w_ffn_down[0], v_w_ffn_down[0])]
    big_out = {}
    for nm, w_, g_, m_, v_ in big:
        d_, m2_, v2_ = adamw(w_, g_, m_, v_, "adamw_" + nm)
        big_out[nm] = (g_[None], d_[None], m2_[None], v2_[None])

    small_w = [b_ada, hgrn_lb, hgrn_gnorm, ssm_conv_w, ssm_conv_b, ssm_dt_bias, ssm_a_log, ssm_d, ssm_norm,
               ln1_g, ln1_b, ln2_g, ln2_b]
    small_g = [g_bada, g_lb, g_gn, g_cw, g_cb, g_dtb, g_alog, g_dsk, g_wn, g_l1g, g_l1b, g_l2g, g_l2b]
    small_m = [m_b_ada, m_hgrn_lb, m_hgrn_gnorm, m_ssm_conv_w, m_ssm_conv_b, m_ssm_dt_bias, m_ssm_a_log, m_ssm_d,
               m_ssm_norm, m_ln1_g, m_ln1_b, m_ln2_g, m_ln2_b]
    small_v = [v_b_ada, v_hgrn_lb, v_hgrn_gnorm, v_ssm_conv_w, v_ssm_conv_b, v_ssm_dt_bias, v_ssm_a_log, v_ssm_d,
               v_ssm_norm, v_ln1_g, v_ln1_b, v_ln2_g, v_ln2_b]
    shapes = [a.shape for a in small_w]
    small_g = [g_.reshape(s) for g_, s in zip(small_g, shapes)]
    pw, poffs = _pack(small_w)
    pg, _ = _pack(small_g)
    pm, _ = _pack(small_m)
    pv, _ = _pack(small_v)
    pd, pm2, pv2 = adamw(pw, pg, pm, pv, "adamw_small")
    s_d, s_m, s_v = (_unpack(p, poffs, shapes) for p in (pd, pm2, pv2))
    (sn_bada, sn_lb, sn_gn, sn_cw, sn_cb, sn_dtb, sn_alog, sn_dsk, sn_wn, sn_l1g, sn_l1b, sn_l2g, sn_l2b) = range(13)

    def order(kind):
        sm = [small_g, s_d, s_m, s_v][kind]
        bg = lambda nm: big_out[nm][kind]
        return [bg("ada"), sm[sn_bada], bg("in"), sm[sn_lb], sm[sn_gn], sm[sn_cw], sm[sn_cb], sm[sn_dtb], sm[sn_alog],
                sm[sn_dsk], sm[sn_wn], bg("branch_a"), bg("branch_b"), bg("o"), sm[sn_l1g], sm[sn_l1b],
                bg("ffn_gate"), bg("ffn_up"), bg("ffn_down"), sm[sn_l2g], sm[sn_l2b]]

    return (loss, dx[None], *order(0), *order(1), *order(2), *order(3))
```

```python
import numpy as np
import jax
import jax.numpy as jnp
from jax import lax
from jax.experimental import pallas as pl
from jax.experimental.pallas import tpu as pltpu

F32 = jnp.float32
BF16 = jnp.bfloat16
HI = lax.Precision.HIGHEST

N_DEV = 8
D = 1024
N_HEADS_A = 8
HK = 128
CHUNK = 64
SSD_CHUNK = 128
SSD_CHUNK_BWD = 256
N_GROUPS = 4
HEADS_PER_GROUP = 8
HEAD_P = 64
N_STATE = 128
GROUP_W = HEADS_PER_GROUP * HEAD_P
B_INNER = 2048
CONV_DIM = 3072
D_FF = 2816
IN_DIM = 11296
N_PROJ = 12288
ALPHA = 2.0 ** 0.25
LN_EPS = 1e-5
RMS_EPS = 1e-6
Q_SCALE = 128 ** -0.5
EXP_CLIP = 80.0
ADAM_LR, ADAM_B1, ADAM_B2, ADAM_EPS, ADAM_WD, ADAM_STEP = 0.001, 0.9, 0.999, 1e-8, 0.01, 10
VMEM_LIMIT = 48 * 1024 * 1024
TOKEN_BLOCK = 512
ROW_TILE = 256
FFN_ROW_TILE = 128
MM_ROW_TILE = 1024
MM_TOKEN_TILE = 4096
MM_K_TILE = 3072
MM_COL_TILE = 1408
HGRN_HEADS_PER_STEP = 4
CHUNK_UNROLL = 8
MESH_ID = pl.DeviceIdType.MESH

NT_DIMS = (((1,), (1,)), ((), ()))
TN_DIMS = (((0,), (0,)), ((), ()))


def _cparams(sem=None):
    return pltpu.CompilerParams(dimension_semantics=sem, vmem_limit_bytes=VMEM_LIMIT)


def _sigmoid(x):
    return 1.0 / (1.0 + jnp.exp(-x))


def _dsilu(x, s):
    return s * (1.0 + x * (1.0 - s))


def _nt(a, b, precision=None):
    return lax.dot_general(a, b, NT_DIMS, precision=precision, preferred_element_type=F32)


def _tn(a, b, precision=None):
    return lax.dot_general(a, b, TN_DIMS, precision=precision, preferred_element_type=F32)


def _nn(a, b, precision=None):
    return jnp.dot(a, b, precision=precision, preferred_element_type=F32)


def _split(x, pieces):
    out = []
    for i in range(pieces):
        p = x.astype(BF16)
        out.append(p)
        if i + 1 < pieces:
            x = x - p.astype(F32)
    return out


def _sel(dot, x, sel01, pieces, x_first=True):
    acc = None
    for p in _split(x, pieces):
        term = dot(p, sel01) if x_first else dot(sel01, p)
        acc = term if acc is None else acc + term
    return acc


def _ln(x):
    mu = jnp.mean(x, axis=-1, keepdims=True)
    xc = x - mu
    rstd = lax.rsqrt(jnp.mean(xc * xc, axis=-1, keepdims=True) + LN_EPS)
    return xc * rstd, rstd


def _ln_bwd(dxh, xh, rstd):
    return rstd * (dxh - jnp.mean(dxh, axis=-1, keepdims=True) - xh * jnp.mean(dxh * xh, axis=-1, keepdims=True))


def _colsum(x):
    return jnp.sum(x, axis=0, keepdims=True)


def _tri(n, upper=False):
    r = lax.broadcasted_iota(jnp.int32, (n, n), 0)
    c = lax.broadcasted_iota(jnp.int32, (n, n), 1)
    return (c >= r) if upper else (r >= c)


def _my_pos():
    return lax.axis_index("x"), lax.axis_index("y"), lax.axis_index("c")


def _peer(pos, k):
    x, y, c = pos
    return (x ^ ((k >> 2) & 1), y ^ ((k >> 1) & 1), c ^ (k & 1))


def _flat(pos):
    return 4 * pos[0] + 2 * pos[1] + pos[2]


def allgather_vmem(v, name):
    n = v.shape[1]

    def body(v_ref, o_ref, send_sems, recv_sems, local_sem):
        me = _my_pos()
        mine = pltpu.make_async_copy(v_ref, o_ref.at[_flat(me)], local_sem)
        mine.start()
        sends = []
        for k in range(1, N_DEV):
            peer = _peer(me, k)
            cp = pltpu.make_async_remote_copy(v_ref, o_ref.at[_flat(me)], send_sems.at[k - 1], recv_sems.at[k - 1],
                                              device_id=peer, device_id_type=MESH_ID)
            cp.start()
            sends.append(cp)
        for k in range(1, N_DEV):
            peer = _peer(me, k)
            pltpu.make_async_remote_copy(v_ref, o_ref.at[_flat(peer)], send_sems.at[k - 1], recv_sems.at[k - 1],
                                         device_id=peer, device_id_type=MESH_ID).wait_recv()
        for cp in sends:
            cp.wait_send()
        mine.wait()

    return pl.pallas_call(
        body, name=name,
        out_shape=jax.ShapeDtypeStruct((N_DEV, 1, n), F32),
        in_specs=[pl.BlockSpec(memory_space=pltpu.VMEM)],
        out_specs=pl.BlockSpec(memory_space=pltpu.VMEM),
        scratch_shapes=[pltpu.SemaphoreType.DMA((N_DEV - 1,)), pltpu.SemaphoreType.DMA((N_DEV - 1,)),
                        pltpu.SemaphoreType.DMA],
        compiler_params=_cparams(),
    )(v)


def ada_modulation(c_all, w_ada_s, b_ada_r):
    ncol = w_ada_s.shape[1]

    def body(c_ref, w_ref, b_ref, o_ref, part_ref, send_sems, recv_sems):
        me = _my_pos()
        cval = c_ref[...]
        cond = cval * _sigmoid(cval)
        part = _nn(cond, w_ref[...], HI)
        for r in range(N_DEV):
            part_ref[r] = part[r:r + 1, :]
        sends = []
        for k in range(1, N_DEV):
            peer = _peer(me, k)
            cp = pltpu.make_async_remote_copy(part_ref.at[_flat(peer)], o_ref.at[_flat(me)], send_sems.at[k - 1],
                                              recv_sems.at[k - 1], device_id=peer, device_id_type=MESH_ID)
            cp.start()
            sends.append(cp)
        o_ref[_flat(me)] = part_ref[_flat(me)]
        for k in range(1, N_DEV):
            peer = _peer(me, k)
            pltpu.make_async_remote_copy(part_ref.at[_flat(peer)], o_ref.at[_flat(peer)], send_sems.at[k - 1],
                                         recv_sems.at[k - 1], device_id=peer, device_id_type=MESH_ID).wait_recv()
        for cp in sends:
            cp.wait_send()
        o_ref[...] = o_ref[...] + b_ref[...]

    return pl.pallas_call(
        body, name="ada_modulation",
        out_shape=jax.ShapeDtypeStruct((N_DEV, 1, ncol), F32),
        in_specs=[pl.BlockSpec(memory_space=pltpu.VMEM)] * 3,
        out_specs=pl.BlockSpec(memory_space=pltpu.VMEM),
        scratch_shapes=[pltpu.VMEM((N_DEV, 1, ncol), F32), pltpu.SemaphoreType.DMA((N_DEV - 1,)),
                        pltpu.SemaphoreType.DMA((N_DEV - 1,))],
        compiler_params=_cparams(),
    )(c_all, w_ada_s, b_ada_r)


def allgather_hbm(shard, name):
    def body(x_ref, out_ref, send_sems, recv_sems, local_sem):
        x, y, c = _my_pos()
        me, sibling = (x, y, c), (x, y, 1 - c)
        chips = [(1 - x, y), (x, 1 - y), (1 - x, 1 - y)]

        def slot(pos):
            return out_ref.at[_flat(pos)]

        def copy(k, block, to, src=None):
            return pltpu.make_async_remote_copy(slot(block) if src is None else src, slot(block), send_sems.at[k],
                                                recv_sems.at[k], device_id=to, device_id_type=MESH_ID)

        mine = pltpu.make_async_copy(x_ref, slot(me), local_sem)
        mine.start()
        first = [copy(0, me, sibling, src=x_ref)]
        first += [copy(1 + j, me, (*chip, c), src=x_ref) for j, chip in enumerate(chips)]
        for cp in first:
            cp.start()
        passed = [copy(4 + j, (*chip, c), sibling) for j, chip in enumerate(chips)]
        for j, chip in enumerate(chips):
            copy(1 + j, (*chip, c), me).wait_recv()
            passed[j].start()
        copy(0, sibling, me).wait_recv()
        for j, chip in enumerate(chips):
            copy(4 + j, (*chip, 1 - c), me).wait_recv()
        for cp in first + passed:
            cp.wait_send()
        mine.wait()

    return pl.pallas_call(
        body, name=name,
        out_shape=jax.ShapeDtypeStruct((N_DEV,) + shard.shape, shard.dtype),
        in_specs=[pl.BlockSpec(memory_space=pl.ANY)],
        out_specs=pl.BlockSpec(memory_space=pl.ANY),
        scratch_shapes=[pltpu.SemaphoreType.DMA((N_DEV - 1,)), pltpu.SemaphoreType.DMA((N_DEV - 1,)),
                        pltpu.SemaphoreType.DMA],
        compiler_params=_cparams(),
    )(shard)


N_CHIP = N_DEV // 2
SIBLING_SEMS = [pltpu.SemaphoreType.DMA, pltpu.SemaphoreType.DMA]
CHIP_SEMS = [pltpu.SemaphoreType.DMA((N_CHIP - 1,)), pltpu.SemaphoreType.DMA((N_CHIP - 1,)), pltpu.SemaphoreType.DMA]


def _sibling_exchange(s_ref, o_ref, send_sem, recv_sem):
    x, y, c = _my_pos()
    cp = pltpu.make_async_remote_copy(s_ref, o_ref, send_sem, recv_sem, device_id=(x, y, 1 - c), device_id_type=MESH_ID)
    return cp.start, cp.wait


def _chip_exchange(p_ref, o_ref, send_sems, recv_sems, local_sem):
    x, y, c = _my_pos()
    my_chip = 2 * x + y
    mine = pltpu.make_async_copy(p_ref.at[my_chip], o_ref.at[my_chip], local_sem)
    peers = [(x ^ (k >> 1), y ^ (k & 1)) for k in range(1, N_CHIP)]
    sends = [pltpu.make_async_remote_copy(p_ref.at[2 * px + py], o_ref.at[my_chip], send_sems.at[k], recv_sems.at[k],
                                          device_id=(px, py, c), device_id_type=MESH_ID)
             for k, (px, py) in enumerate(peers)]
    recvs = [pltpu.make_async_remote_copy(p_ref.at[2 * px + py], o_ref.at[2 * px + py], send_sems.at[k], recv_sems.at[k],
                                          device_id=(px, py, c), device_id_type=MESH_ID)
             for k, (px, py) in enumerate(peers)]

    def start():
        mine.start()
        for cp in sends:
            cp.start()

    def wait():
        for cp in recvs:
            cp.wait_recv()
        for cp in sends:
            cp.wait_send()
        mine.wait()

    return start, wait


def exchange_sibling(send, name):
    def body(s_ref, o_ref, send_sem, recv_sem):
        start, wait = _sibling_exchange(s_ref, o_ref, send_sem, recv_sem)
        start()
        wait()

    return pl.pallas_call(
        body, name=name,
        out_shape=jax.ShapeDtypeStruct(send.shape, send.dtype),
        in_specs=[pl.BlockSpec(memory_space=pl.ANY)],
        out_specs=pl.BlockSpec(memory_space=pl.ANY),
        scratch_shapes=SIBLING_SEMS,
        compiler_params=_cparams(),
    )(send)


LANES = 128


def _k_tile(kdim, unit=LANES):
    for cand in range(MM_K_TILE - MM_K_TILE % unit, 0, -unit):
        if kdim % cand == 0:
            return cand
    return kdim


def _lane_tile(n, cap):
    for cand in range(cap - cap % LANES, 0, -LANES):
        if n % cand == 0:
            return cand
    return n


def mm_nn(a, b, out_dtype, name):
    m, kdim = a.shape
    n = b.shape[1]
    tm, tn, tk = min(MM_ROW_TILE, m), _lane_tile(n, MM_COL_TILE), _k_tile(kdim)
    nk = kdim // tk

    def body(a_ref, b_ref, o_ref, acc_ref):
        p = _nn(a_ref[...], b_ref[...])
        if nk == 1:
            o_ref[...] = p.astype(o_ref.dtype)
        else:
            k = pl.program_id(2)

            @pl.when(k == 0)
            def _():
                acc_ref[...] = p

            @pl.when(k > 0)
            def _():
                acc_ref[...] += p

            @pl.when(k == nk - 1)
            def _():
                o_ref[...] = acc_ref[...].astype(o_ref.dtype)

    return pl.pallas_call(
        body, name=name, grid=(n // tn, m // tm, nk),
        out_shape=jax.ShapeDtypeStruct((m, n), out_dtype),
        in_specs=[pl.BlockSpec((tm, tk), lambda j, i, k: (i, k)), pl.BlockSpec((tk, tn), lambda j, i, k: (k, j))],
        out_specs=pl.BlockSpec((tm, tn), lambda j, i, k: (i, j)),
        scratch_shapes=[pltpu.VMEM((tm, tn), F32)],
        compiler_params=_cparams(("parallel", "parallel", "arbitrary")),
    )(a, b)


def mm_nt(a, b, out_dtype, name):
    m, kdim = a.shape
    n = b.shape[0]
    tm, tn, tk = min(MM_ROW_TILE, m), _lane_tile(n, MM_COL_TILE), _k_tile(kdim)
    nk = kdim // tk

    def body(a_ref, b_ref, o_ref, acc_ref):
        p = _nt(a_ref[...], b_ref[...])
        if nk == 1:
            o_ref[...] = p.astype(o_ref.dtype)
        else:
            k = pl.program_id(2)

            @pl.when(k == 0)
            def _():
                acc_ref[...] = p

            @pl.when(k > 0)
            def _():
                acc_ref[...] += p

            @pl.when(k == nk - 1)
            def _():
                o_ref[...] = acc_ref[...].astype(o_ref.dtype)

    return pl.pallas_call(
        body, name=name, grid=(n // tn, m // tm, nk),
        out_shape=jax.ShapeDtypeStruct((m, n), out_dtype),
        in_specs=[pl.BlockSpec((tm, tk), lambda j, i, k: (i, k)), pl.BlockSpec((tn, tk), lambda j, i, k: (j, k))],
        out_specs=pl.BlockSpec((tm, tn), lambda j, i, k: (i, j)),
        scratch_shapes=[pltpu.VMEM((tm, tn), F32)],
        compiler_params=_cparams(("parallel", "parallel", "arbitrary")),
    )(a, b)


def mm_nn_exchange(a, b, out_dtype, part, name):
    kblocks, m, kb = a.shape
    kdim = kblocks * kb
    n = b.shape[1]
    tm, tn, tk = min(MM_ROW_TILE, m), _lane_tile(n, MM_COL_TILE), _k_tile(kdim)
    gn, gm, nk = n // tn, m // tm, kdim // tk
    per_step = tk // kb

    def body(a_ref, b_ref, part_ref, o_ref, parts_ref, acc_ref, send_sems, recv_sems, local_sem):
        j, i, k = pl.program_id(0), pl.program_id(1), pl.program_id(2)
        xchg_start, xchg_wait = _chip_exchange(part_ref, parts_ref, send_sems, recv_sems, local_sem)

        @pl.when((j == 0) & (i == 0) & (k == 0))
        def _():
            xchg_start()

        p = _nn(a_ref[0], b_ref[0:kb, :])
        for c in range(1, per_step):
            p = p + _nn(a_ref[c], b_ref[c * kb:(c + 1) * kb, :])

        @pl.when(k == 0)
        def _():
            acc_ref[...] = p

        @pl.when(k > 0)
        def _():
            acc_ref[...] += p

        @pl.when(k == nk - 1)
        def _():
            o_ref[...] = acc_ref[...].astype(o_ref.dtype)

        @pl.when((j == gn - 1) & (i == gm - 1) & (k == nk - 1))
        def _():
            xchg_wait()

    hbm = pl.BlockSpec(memory_space=pl.ANY)
    return pl.pallas_call(
        body, name=name, grid=(gn, gm, nk),
        out_shape=[jax.ShapeDtypeStruct((m, n), out_dtype), jax.ShapeDtypeStruct(part.shape, part.dtype)],
        in_specs=[pl.BlockSpec((per_step, tm, kb), lambda j, i, k: (k, i, 0)),
                  pl.BlockSpec((tk, tn), lambda j, i, k: (k, j)), hbm],
        out_specs=[pl.BlockSpec((tm, tn), lambda j, i, k: (i, j)), hbm],
        scratch_shapes=[pltpu.VMEM((tm, tn), F32)] + CHIP_SEMS,
        compiler_params=_cparams(("arbitrary", "arbitrary", "arbitrary")),
    )(a, b, part)


def mm_nt_gather(a, b, out_dtype, shard, name):
    m, kdim = a.shape
    n = b.shape[0]
    tm, tn = min(MM_ROW_TILE, m), 1024
    assert kdim == 1024
    gj = m // tm
    nsteps = (n // tn) * gj
    forward_step = max(nsteps - 2, 0)

    def body(a_ref, b_ref, x_ref, o_ref, g_ref, send_sems, recv_sems, local_sem):
        step = pl.program_id(0) * gj + pl.program_id(1)
        x, y, c = _my_pos()
        me, sibling = (x, y, c), (x, y, 1 - c)
        chips = [(1 - x, y), (x, 1 - y), (1 - x, 1 - y)]

        def slot(pos):
            return g_ref.at[_flat(pos)]

        def copy(k, block, to, src=None):
            return pltpu.make_async_remote_copy(slot(block) if src is None else src, slot(block), send_sems.at[k],
                                                recv_sems.at[k], device_id=to, device_id_type=MESH_ID)

        mine = pltpu.make_async_copy(x_ref, slot(me), local_sem)
        first = [copy(0, me, sibling, src=x_ref)]
        first += [copy(1 + j, me, (*chip, c), src=x_ref) for j, chip in enumerate(chips)]
        passed = [copy(4 + j, (*chip, c), sibling) for j, chip in enumerate(chips)]

        @pl.when(step == 0)
        def _():
            mine.start()
            for cp in first:
                cp.start()

        rows = pl.ds(pl.multiple_of(pl.program_id(1) * tm, tm), tm)
        o_ref[...] = _nt(a_ref[rows, :], b_ref[...]).astype(o_ref.dtype)

        @pl.when(step == forward_step)
        def _():
            for j, chip in enumerate(chips):
                copy(1 + j, (*chip, c), me).wait_recv()
                passed[j].start()

        @pl.when(step == nsteps - 1)
        def _():
            copy(0, sibling, me).wait_recv()
            for j, chip in enumerate(chips):
                copy(4 + j, (*chip, 1 - c), me).wait_recv()
            for cp in first + passed:
                cp.wait_send()
            mine.wait()

    return pl.pallas_call(
        body, name=name, grid=(n // tn, gj),
        out_shape=[jax.ShapeDtypeStruct((m, n), out_dtype), jax.ShapeDtypeStruct((N_DEV,) + shard.shape, shard.dtype)],
        in_specs=[pl.BlockSpec(memory_space=pltpu.VMEM), pl.BlockSpec((tn, kdim), lambda j, i: (j, 0)),
                  pl.BlockSpec(memory_space=pl.ANY)],
        out_specs=[pl.BlockSpec((tm, tn), lambda j, i: (i, j)), pl.BlockSpec(memory_space=pl.ANY)],
        scratch_shapes=[pltpu.SemaphoreType.DMA((N_DEV - 1,)), pltpu.SemaphoreType.DMA((N_DEV - 1,)),
                        pltpu.SemaphoreType.DMA],
        compiler_params=_cparams(("arbitrary", "arbitrary")),
    )(a, b, shard)


def mm_tn(a, b, name):
    tt, tn = min(MM_TOKEN_TILE, b.shape[0]), _lane_tile(b.shape[1], MM_COL_TILE)
    tka = _lane_tile(a.shape[0] * a.shape[2] if a.ndim == 3 else a.shape[1], 1024)
    if a.ndim == 3:
        t, ka = a.shape[1], a.shape[0] * a.shape[2]
        a_spec = pl.BlockSpec((None, tt, tka), lambda i, j, s: (i, s, 0))
    else:
        t, ka = a.shape
        a_spec = pl.BlockSpec((tt, tka), lambda i, j, s: (s, i))
    n = b.shape[1]
    nt = t // tt

    def body(a_ref, b_ref, o_ref, *acc):
        p = _tn(a_ref[...], b_ref[...])
        if nt == 1:
            o_ref[...] = p.astype(o_ref.dtype)
        else:
            acc_ref, s = acc[0], pl.program_id(2)

            @pl.when(s == 0)
            def _():
                acc_ref[...] = p

            @pl.when(s > 0)
            def _():
                acc_ref[...] += p

            @pl.when(s == nt - 1)
            def _():
                o_ref[...] = acc_ref[...].astype(o_ref.dtype)

    return pl.pallas_call(
        body, name=name, grid=(ka // tka, n // tn, nt),
        out_shape=jax.ShapeDtypeStruct((ka, n), BF16),
        in_specs=[a_spec, pl.BlockSpec((tt, tn), lambda i, j, s: (s, j))],
        out_specs=pl.BlockSpec((tka, tn), lambda i, j, s: (i, j)),
        scratch_shapes=[] if nt == 1 else [pltpu.VMEM((tka, tn), F32)],
        compiler_params=_cparams(("parallel", "parallel", "arbitrary")),
    )(a, b)


def _tile(t, cap):
    return min(cap, t)


def ln_modulate(x, mod6, shift_row, scale_row, name):
    t = x.shape[0]
    tm = _tile(t, ROW_TILE)

    def body(x_ref, mod_ref, o_ref):
        xh, _ = _ln(x_ref[...])
        sc = mod_ref[scale_row:scale_row + 1, :]
        sh = mod_ref[shift_row:shift_row + 1, :]
        o_ref[...] = (xh * (1.0 + sc) + sh).astype(BF16)

    return pl.pallas_call(
        body, name=name, grid=(t // tm,),
        out_shape=jax.ShapeDtypeStruct((t, D), BF16),
        in_specs=[pl.BlockSpec((tm, D), lambda i: (i, 0)), pl.BlockSpec((6, D), lambda i: (0, 0))],
        out_specs=pl.BlockSpec((tm, D), lambda i: (i, 0)),
        compiler_params=_cparams(("parallel",)),
    )(x, mod6)


def resid_ln(x, h, mod6, gate_row, ln_g, ln_b, name):
    t = x.shape[0]
    tm = _tile(t, ROW_TILE)

    def body(x_ref, h_ref, mod_ref, g_ref, b_ref, o_ref):
        r = ALPHA * x_ref[...] + mod_ref[gate_row:gate_row + 1, :] * h_ref[...]
        rh, _ = _ln(r)
        o_ref[...] = rh * g_ref[...] + b_ref[...]

    row = pl.BlockSpec((tm, D), lambda i: (i, 0))
    vec = pl.BlockSpec((1, D), lambda i: (0, 0))
    return pl.pallas_call(
        body, name=name, grid=(t // tm,),
        out_shape=jax.ShapeDtypeStruct((t, D), F32),
        in_specs=[row, row, pl.BlockSpec((6, D), lambda i: (0, 0)), vec, vec],
        out_specs=row,
        compiler_params=_cparams(("parallel",)),
    )(x, h, mod6, ln_g, ln_b)


def resid_ln_bwd(x, h, mod6, gate_row, ln_g, ln_b, cot, with_loss, name):
    t = x.shape[0]
    tm = _tile(t, ROW_TILE)

    def body(x_ref, h_ref, mod_ref, g_ref, b_ref, c_ref, dh_ref, dx_ref, acc_ref):
        @pl.when(pl.program_id(0) == 0)
        def _():
            acc_ref[...] = jnp.zeros_like(acc_ref)

        gate = mod_ref[gate_row:gate_row + 1, :]
        hv = h_ref[...]
        r = ALPHA * x_ref[...] + gate * hv
        rh, rstd = _ln(r)
        lng = g_ref[...]
        if with_loss:
            diff = rh * lng + b_ref[...] - c_ref[...]
            dxo = diff * (1.0 / D)
            lsum = jnp.sum(_colsum(diff * diff), axis=-1, keepdims=True) * (0.5 / D)
            acc_ref[3:4, :] += jnp.broadcast_to(lsum, (1, D))
        else:
            dxo = c_ref[...]
        acc_ref[1:2, :] += _colsum(dxo * rh)
        acc_ref[2:3, :] += _colsum(dxo)
        dr = _ln_bwd(dxo * lng, rh, rstd)
        acc_ref[0:1, :] += _colsum(dr * hv)
        dh_ref[...] = (gate * dr).astype(BF16)
        dx_ref[...] = ALPHA * dr

    row = pl.BlockSpec((tm, D), lambda i: (i, 0))
    vec = pl.BlockSpec((1, D), lambda i: (0, 0))
    return pl.pallas_call(
        body, name=name, grid=(t // tm,),
        out_shape=[jax.ShapeDtypeStruct((t, D), BF16), jax.ShapeDtypeStruct((t, D), F32),
                   jax.ShapeDtypeStruct((8, D), F32)],
        in_specs=[row, row, pl.BlockSpec((6, D), lambda i: (0, 0)), vec, vec, row],
        out_specs=[row, row, pl.BlockSpec((8, D), lambda i: (0, 0))],
        compiler_params=_cparams(("arbitrary",)),
    )(x, h, mod6, ln_g, ln_b, cot)


def ln_modulate_bwd(x, du, mod6, scale_row, dx_part, name):
    t = x.shape[0]
    tm = _tile(t, ROW_TILE)

    def body(x_ref, du_ref, mod_ref, dp_ref, dx_ref, acc_ref):
        @pl.when(pl.program_id(0) == 0)
        def _():
            acc_ref[...] = jnp.zeros_like(acc_ref)

        xh, rstd = _ln(x_ref[...])
        du_v = du_ref[...]
        sc = mod_ref[scale_row:scale_row + 1, :]
        acc_ref[0:1, :] += _colsum(du_v * xh)
        acc_ref[1:2, :] += _colsum(du_v)
        dx_ref[...] = dp_ref[...] + _ln_bwd(du_v * (1.0 + sc), xh, rstd)

    row = pl.BlockSpec((tm, D), lambda i: (i, 0))
    return pl.pallas_call(
        body, name=name, grid=(t // tm,),
        out_shape=[jax.ShapeDtypeStruct((t, D), F32), jax.ShapeDtypeStruct((8, D), F32)],
        in_specs=[row, row, pl.BlockSpec((6, D), lambda i: (0, 0)), row],
        out_specs=[row, pl.BlockSpec((8, D), lambda i: (0, 0))],
        compiler_params=_cparams(("arbitrary",)),
    )(x, du, mod6, dx_part)


def merge_gates(ya, yb, proj):
    t = ya.shape[0]
    tm = _tile(t, ROW_TILE)

    def body(ya_ref, yb_ref, ga_ref, gb_ref, o_ref):
        o_ref[...] = (_sigmoid(ga_ref[...]) * ya_ref[...].astype(F32) +
                      _sigmoid(gb_ref[...]) * yb_ref[...].astype(F32)).astype(BF16)

    row = pl.BlockSpec((tm, D), lambda i: (i, 0))
    return pl.pallas_call(
        body, name="merge_gates", grid=(t // tm,),
        out_shape=jax.ShapeDtypeStruct((t, D), BF16),
        in_specs=[row, row, pl.BlockSpec((tm, D), lambda i: (i, GATE_BLOCK0)),
                  pl.BlockSpec((tm, D), lambda i: (i, GATE_BLOCK0 + 1))],
        out_specs=row,
        compiler_params=_cparams(("parallel",)),
    )(ya, yb, proj, proj)


def merge_gates_bwd(dm, ya, yb, proj):
    t = ya.shape[0]
    tm = _tile(t, ROW_TILE)

    def body(dm_ref, ya_ref, yb_ref, ga_ref, gb_ref, dya_ref, dyb_ref, dp_ref):
        dmv = dm_ref[...].astype(F32)
        sa = _sigmoid(ga_ref[...])
        sb = _sigmoid(gb_ref[...])
        dya_ref[...] = (dmv * sa).astype(BF16)
        dyb_ref[...] = (dmv * sb).astype(BF16)
        dp_ref[0] = (dmv * ya_ref[...].astype(F32) * sa * (1.0 - sa)).astype(BF16)
        dp_ref[1] = (dmv * yb_ref[...].astype(F32) * sb * (1.0 - sb)).astype(BF16)

    row = pl.BlockSpec((tm, D), lambda i: (i, 0))
    return pl.pallas_call(
        body, name="merge_gates_bwd", grid=(t // tm,),
        out_shape=[jax.ShapeDtypeStruct((t, D), BF16)] * 2 + [jax.ShapeDtypeStruct((N_PROJ // D, t, D), BF16)],
        in_specs=[row, row, row, pl.BlockSpec((tm, D), lambda i: (i, GATE_BLOCK0)),
                  pl.BlockSpec((tm, D), lambda i: (i, GATE_BLOCK0 + 1))],
        out_specs=[row, row, pl.BlockSpec((2, tm, D), lambda i: (GATE_BLOCK0 // 2, i, 0))],
        compiler_params=_cparams(("parallel",)),
    )(dm, ya, yb, proj, proj)


FF_CHUNK = 1408


def swiglu_act(gu):
    t = gu.shape[0]
    tm = _tile(t, FFN_ROW_TILE)

    def body(gu_ref, o_ref):
        for j in range(D_FF // FF_CHUNK):
            cs = slice(j * FF_CHUNK, (j + 1) * FF_CHUNK)
            g = gu_ref[:, cs].astype(F32)
            u = gu_ref[:, D_FF + j * FF_CHUNK:D_FF + (j + 1) * FF_CHUNK].astype(F32)
            o_ref[:, cs] = (g * _sigmoid(g) * u).astype(BF16)

    return pl.pallas_call(
        body, name="swiglu_act", grid=(t // tm,),
        out_shape=jax.ShapeDtypeStruct((t, D_FF), BF16),
        in_specs=[pl.BlockSpec((tm, 2 * D_FF), lambda i: (i, 0))],
        out_specs=pl.BlockSpec((tm, D_FF), lambda i: (i, 0)),
        compiler_params=_cparams(("parallel",)),
    )(gu)


def swiglu_act_bwd(gu, dact):
    t = gu.shape[0]
    tm = _tile(t, FFN_ROW_TILE)

    def body(gu_ref, da_ref, o_ref):
        for j in range(D_FF // FF_CHUNK):
            cs = slice(j * FF_CHUNK, (j + 1) * FF_CHUNK)
            us = slice(D_FF + j * FF_CHUNK, D_FF + (j + 1) * FF_CHUNK)
            g = gu_ref[:, cs].astype(F32)
            u = gu_ref[:, us].astype(F32)
            da = da_ref[:, cs].astype(F32)
            s = _sigmoid(g)
            o_ref[:, cs] = (da * u * _dsilu(g, s)).astype(BF16)
            o_ref[:, us] = (da * g * s).astype(BF16)

    return pl.pallas_call(
        body, name="swiglu_act_bwd", grid=(t // tm,),
        out_shape=jax.ShapeDtypeStruct((t, 2 * D_FF), BF16),
        in_specs=[pl.BlockSpec((tm, 2 * D_FF), lambda i: (i, 0)), pl.BlockSpec((tm, D_FF), lambda i: (i, 0))],
        out_specs=pl.BlockSpec((tm, 2 * D_FF), lambda i: (i, 0)),
        compiler_params=_cparams(("parallel",)),
    )(gu, dact)


def _hgrn_chunk_terms(q, fl, lbv, tril_f):
    sig = _sigmoid(fl)
    f = lbv + (1.0 - lbv) * sig
    lam = jnp.log(f)
    k = 1.0 - f
    sq = _sigmoid(q)
    qt = q * sq * Q_SCALE
    bc = _sel(_nn, lam, tril_f, 3, x_first=False)
    bmid = bc[CHUNK // 2 - 1:CHUNK // 2, :]
    bl = bc[CHUNK - 1:CHUNK, :]
    eq = jnp.exp(jnp.minimum(bc - bmid, EXP_CLIP))
    ek = jnp.exp(jnp.minimum(bmid - bc, EXP_CLIP))
    eb = jnp.exp(bc)
    ekl = jnp.exp(bl - bc)
    ebl = jnp.exp(bl)
    return sig, f, k, sq, qt, eq, ek, eb, ekl, ebl


def hgrn_fwd(proj, lb, gnorm):
    t = proj.shape[0]
    tb = _tile(t, TOKEN_BLOCK)
    ncb = tb // CHUNK

    hps = HGRN_HEADS_PER_STEP
    wide = hps * HK

    def body(q_ref, f_ref, i_ref, g_ref, lb_ref, gn_ref, oa_ref, oraw_ref, st_ref, state):
        @pl.when(pl.program_id(1) == 0)
        def _():
            state[...] = jnp.zeros_like(state)

        gn = gn_ref[...]
        mask = _tri(CHUNK)
        tril_f = mask.astype(BF16)

        def chunk(c, carry):
            sl = pl.ds(pl.multiple_of(c * CHUNK, CHUNK), CHUNK)
            for hh in range(hps):
                ln = slice(hh * HK, (hh + 1) * HK)
                q, fl, v, g = q_ref[sl, ln], f_ref[sl, ln], i_ref[sl, ln], g_ref[sl, ln]
                sig, f, k, sq, qt, eq, ek, eb, ekl, ebl = _hgrn_chunk_terms(q, fl, lb_ref[:, ln], tril_f)
                a = jnp.where(mask, _nt((qt * eq).astype(BF16), (k * ek).astype(BF16)), 0.0)
                st = state[hh]
                st_ref[hh, c] = st
                vb = v.astype(BF16)
                o = _nn(a.astype(BF16), vb) + _nt((qt * eb).astype(BF16), st.astype(BF16))
                state[hh] = st * ebl + _tn(vb, (k * ekl).astype(BF16))
                oraw_ref[sl, ln] = o
                rn = o * lax.rsqrt(jnp.mean(o * o, axis=-1, keepdims=True) + RMS_EPS)
                oa_ref[sl, ln] = (rn * gn * g * _sigmoid(g)).astype(BF16)
            return carry

        lax.fori_loop(0, ncb, chunk, 0, unroll=min(CHUNK_UNROLL, ncb))

    def col(block):
        return pl.BlockSpec((tb, wide), lambda h, j: (j, block * (N_HEADS_A // hps) + h))

    return pl.pallas_call(
        body, name="hgrn_fwd", grid=(N_HEADS_A // hps, t // tb),
        out_shape=[jax.ShapeDtypeStruct((t, D), BF16), jax.ShapeDtypeStruct((t, D), F32),
                   jax.ShapeDtypeStruct((N_HEADS_A, t // CHUNK, HK, HK), F32)],
        in_specs=[col(0), col(1), col(2), col(3), pl.BlockSpec((1, wide), lambda h, j: (0, h)),
                  pl.BlockSpec((1, HK), lambda h, j: (0, 0))],
        out_specs=[pl.BlockSpec((tb, wide), lambda h, j: (j, h)), pl.BlockSpec((tb, wide), lambda h, j: (j, h)),
                   pl.BlockSpec((hps, ncb, HK, HK), lambda h, j: (h, j, 0, 0))],
        scratch_shapes=[pltpu.VMEM((hps, HK, HK), F32)],
        compiler_params=_cparams(("parallel", "arbitrary")),
    )(proj, proj, proj, proj, lb, gnorm)


def hgrn_bwd(proj, lb, gnorm, o_raw, doa, states, give, dproj):
    t = proj.shape[0]
    tb = _tile(t, TOKEN_BLOCK)
    ncb = tb // CHUNK
    nb = t // tb
    hps = HGRN_HEADS_PER_STEP
    wide = hps * HK

    def body(q_ref, f_ref, i_ref, g_ref, lb_ref, gn_ref, oraw_ref, doa_ref, st_ref, give_ref, dp_in_ref,
             dp_ref, dlb_ref, dgn_ref, got_ref, dstate, send_sem, recv_sem):
        h, j = pl.program_id(0), pl.program_id(1)
        swap_start, swap_wait = _sibling_exchange(give_ref, got_ref, send_sem, recv_sem)

        @pl.when((h == 0) & (j == 0))
        def _():
            swap_start()

        @pl.when(j == 0)
        def _():
            dstate[...] = jnp.zeros_like(dstate)
            dlb_ref[...] = jnp.zeros_like(dlb_ref)

        @pl.when((j == 0) & (h == 0))
        def _():
            dgn_ref[...] = jnp.zeros_like(dgn_ref)

        gn = gn_ref[...]
        mask = _tri(CHUNK)
        mask_t = _tri(CHUNK, upper=True)
        tril_f = mask.astype(BF16)
        triu_f = mask_t.astype(BF16)

        def chunk(i, c0):
            c = ncb - 1 - i
            sl = pl.ds(pl.multiple_of(c * CHUNK, CHUNK), CHUNK)
            for hh in range(hps):
                ln = slice(hh * HK, (hh + 1) * HK)
                q, fl, v, g = q_ref[sl, ln], f_ref[sl, ln], i_ref[sl, ln], g_ref[sl, ln]
                lbv = lb_ref[:, ln]
                sig, f, k, sq, qt, eq, ek, eb, ekl, ebl = _hgrn_chunk_terms(q, fl, lbv, tril_f)
                qe = (qt * eq).astype(BF16)
                ke = (k * ek).astype(BF16)
                st32 = st_ref[hh, c]
                st = st32.astype(BF16)
                dst = dstate[hh]
                dstb = dst.astype(BF16)
                o = oraw_ref[sl, ln]
                rstd = lax.rsqrt(jnp.mean(o * o, axis=-1, keepdims=True) + RMS_EPS)
                rn = o * rstd
                sgm = _sigmoid(g)
                sg = g * sgm
                doa_v = doa_ref[sl, ln]
                drn = doa_v * gn * sg
                dgn_ref[...] += _colsum(doa_v * rn * sg)
                dp_ref[3, sl, ln] = (doa_v * rn * gn * _dsilu(g, sgm)).astype(BF16)
                do = rstd * (drn - rn * jnp.mean(drn * rn, axis=-1, keepdims=True))
                dob = do.astype(BF16)
                vb = v.astype(BF16)
                da = jnp.where(mask, _nt(dob, vb), 0.0).astype(BF16)
                da_t = jnp.where(mask_t, _nt(vb, dob), 0.0).astype(BF16)
                a_t = jnp.where(mask_t, _nt(ke, qe), 0.0).astype(BF16)
                kl = (k * ekl).astype(BF16)
                qb = (qt * eb).astype(BF16)
                dq_in = _nn(da, ke)
                dk_in = _nn(da_t, qe)
                dq_out = eb * _nn(dob, st)
                dk_out = ekl * _nn(vb, dstb)
                dqt = eq * dq_in + dq_out
                dk = ek * dk_in + dk_out
                dv = _nn(a_t, dob) + _nt(kl, dstb)
                dstate[hh] = dst * ebl + _tn(dob, qb)
                dbig = qe.astype(F32) * dq_in - ke.astype(F32) * dk_in + qt * dq_out - k * dk_out
                beyond = _colsum(k * dk_out) + ebl * _colsum(dst * st32)
                dlam = _sel(_nn, dbig, triu_f, 3, x_first=False) + beyond
                df = dlam / f - dk
                dp_ref[1, sl, ln] = (df * (1.0 - lbv) * sig * (1.0 - sig)).astype(BF16)
                dlb_ref[:, ln] += _colsum(df * (1.0 - sig))
                dp_ref[0, sl, ln] = (dqt * Q_SCALE * _dsilu(q, sq)).astype(BF16)
                dp_ref[2, sl, ln] = dv.astype(BF16)
            return c0

        lax.fori_loop(0, ncb, chunk, 0, unroll=min(CHUNK_UNROLL, ncb))

        @pl.when((h == N_HEADS_A // hps - 1) & (j == nb - 1))
        def _():
            swap_wait()

    def col(block):
        return pl.BlockSpec((tb, wide), lambda h, j: (nb - 1 - j, block * (N_HEADS_A // hps) + h))

    hcol = pl.BlockSpec((tb, wide), lambda h, j: (nb - 1 - j, h))
    hbm = pl.BlockSpec(memory_space=pl.ANY)
    return pl.pallas_call(
        body, name="hgrn_bwd", grid=(N_HEADS_A // hps, nb),
        out_shape=[jax.ShapeDtypeStruct(dproj.shape, dproj.dtype), jax.ShapeDtypeStruct((1, D), F32),
                   jax.ShapeDtypeStruct((1, HK), F32), jax.ShapeDtypeStruct(give.shape, give.dtype)],
        in_specs=[col(0), col(1), col(2), col(3), pl.BlockSpec((1, wide), lambda h, j: (0, h)),
                  pl.BlockSpec((1, HK), lambda h, j: (0, 0)), hcol, hcol,
                  pl.BlockSpec((hps, ncb, HK, HK), lambda h, j: (h, nb - 1 - j, 0, 0)), hbm, hbm],
        out_specs=[pl.BlockSpec((4, tb, wide), lambda h, j: (0, nb - 1 - j, h)),
                   pl.BlockSpec((1, wide), lambda h, j: (0, h)), pl.BlockSpec((1, HK), lambda h, j: (0, 0)), hbm],
        input_output_aliases={10: 0},
        scratch_shapes=[pltpu.VMEM((hps, HK, HK), F32)] + SIBLING_SEMS,
        compiler_params=_cparams(("arbitrary", "arbitrary")),
    )(proj, proj, proj, proj, lb, gnorm, o_raw, doa, states, give, dproj)


CONV_BLOCK0 = 6
CONV_TAPS = 4
HALO = 8


def conv_fwd(proj, conv_w, conv_b):
    t = proj.shape[0]
    tm = _tile(t, ROW_TILE)
    r = tm // HALO

    def body(x_ref, halo_ref, w_ref, b_ref, o_ref, ds_ref):
        i = pl.program_id(1)
        halo = jnp.where(i > 0, halo_ref[...], 0.0)
        ext = jnp.concatenate([halo, x_ref[...]], axis=0)
        pre = b_ref[...] + w_ref[CONV_TAPS - 1:CONV_TAPS, :] * ext[HALO:, :]
        for tap in range(CONV_TAPS - 1):
            pre = pre + w_ref[tap:tap + 1, :] * pltpu.roll(ext, CONV_TAPS - 1 - tap, axis=0)[HALO:, :]
        s = _sigmoid(pre)
        o_ref[...] = pre * s
        ds_ref[...] = _dsilu(pre, s).astype(BF16)

    blk = pl.BlockSpec((tm, D), lambda cb, i: (i, cb))
    return pl.pallas_call(
        body, name="conv_fwd", grid=(CONV_DIM // D, t // tm),
        out_shape=[jax.ShapeDtypeStruct((t, CONV_DIM), F32), jax.ShapeDtypeStruct((t, CONV_DIM), BF16)],
        in_specs=[pl.BlockSpec((tm, D), lambda cb, i: (i, CONV_BLOCK0 + cb)),
                  pl.BlockSpec((HALO, D), lambda cb, i: (jnp.maximum(i * r - 1, 0), CONV_BLOCK0 + cb)),
                  pl.BlockSpec((CONV_TAPS, D), lambda cb, i: (0, cb)), pl.BlockSpec((1, D), lambda cb, i: (0, cb))],
        out_specs=[blk, blk],
        compiler_params=_cparams(("parallel", "parallel")),
    )(proj, proj, conv_w, conv_b)


def conv_bwd(proj, dxc, dsilu, conv_w, dproj):
    t = proj.shape[0]
    tm = _tile(t, ROW_TILE)
    r = tm // HALO
    n = t // tm
    last_halo = t // HALO - 1

    def body(x_ref, prev_ref, d_ref, dnext_ref, s_ref, snext_ref, w_ref, dp_in_ref, dx_ref, dw_ref, db_ref):
        i = pl.program_id(1)

        @pl.when(i == 0)
        def _():
            dw_ref[...] = jnp.zeros_like(dw_ref)
            db_ref[...] = jnp.zeros_like(db_ref)

        dpre = jnp.concatenate([d_ref[...].astype(F32) * s_ref[...].astype(F32),
                                jnp.where(i < n - 1, dnext_ref[0:HALO, :].astype(F32) * snext_ref[0:HALO, :].astype(F32),
                                          0.0)], axis=0)
        dx = w_ref[CONV_TAPS - 1:CONV_TAPS, :] * dpre[:tm, :]
        for tap in range(CONV_TAPS - 1):
            back = CONV_TAPS - 1 - tap
            dx = dx + w_ref[tap:tap + 1, :] * pltpu.roll(dpre, tm + HALO - back, axis=0)[:tm, :]
        dx_ref[...] = dx.astype(BF16)
        dp = dpre[:tm, :]
        db_ref[...] += _colsum(dp)
        prev = jnp.where(i > 0, prev_ref[...], 0.0)
        ext = jnp.concatenate([prev, x_ref[...]], axis=0)
        dw_ref[CONV_TAPS - 1:CONV_TAPS, :] += _colsum(dp * ext[HALO:, :])
        for tap in range(CONV_TAPS - 1):
            dw_ref[tap:tap + 1, :] += _colsum(dp * pltpu.roll(ext, CONV_TAPS - 1 - tap, axis=0)[HALO:, :])

    blk = pl.BlockSpec((tm, D), lambda cb, i: (i, cb))
    nxt = pl.BlockSpec((2 * HALO, D), lambda cb, i: (jnp.minimum((i + 1) * (r // 2), last_halo // 2), cb))
    return pl.pallas_call(
        body, name="conv_bwd", grid=(CONV_DIM // D, n),
        out_shape=[jax.ShapeDtypeStruct(dproj.shape, dproj.dtype), jax.ShapeDtypeStruct((8, CONV_DIM), F32),
                   jax.ShapeDtypeStruct((1, CONV_DIM), F32)],
        in_specs=[pl.BlockSpec((tm, D), lambda cb, i: (i, CONV_BLOCK0 + cb)),
                  pl.BlockSpec((HALO, D), lambda cb, i: (jnp.maximum(i * r - 1, 0), CONV_BLOCK0 + cb)),
                  blk, nxt, blk, nxt,
                  pl.BlockSpec((CONV_TAPS, D), lambda cb, i: (0, cb)), pl.BlockSpec(memory_space=pl.ANY)],
        out_specs=[pl.BlockSpec((None, tm, D), lambda cb, i: (CONV_BLOCK0 + cb, i, 0)),
                   pl.BlockSpec((8, D), lambda cb, i: (0, cb)), pl.BlockSpec((1, D), lambda cb, i: (0, cb))],
        input_output_aliases={7: 0},
        compiler_params=_cparams(("parallel", "arbitrary")),
    )(proj, proj, dxc, dxc, dsilu, dsilu, conv_w, dproj)


def dt_fill(ddt, dproj):
    t = ddt.shape[0]
    tm = _tile(t, ROW_TILE)
    w = ddt.shape[1]

    def body(d_ref, dp_in_ref, o_ref):
        o_ref[:, :w] = d_ref[...]
        o_ref[:, w:] = jnp.zeros((tm, D - w), o_ref.dtype)

    return pl.pallas_call(
        body, name="dt_fill", grid=(t // tm,),
        out_shape=jax.ShapeDtypeStruct(dproj.shape, dproj.dtype),
        in_specs=[pl.BlockSpec((tm, w), lambda i: (i, 0)), pl.BlockSpec(memory_space=pl.ANY)],
        out_specs=pl.BlockSpec((None, tm, D), lambda i: (DT_COL_BLOCK, i, 0)),
        input_output_aliases={1: 0},
        compiler_params=_cparams(("parallel",)),
    )(ddt, dproj)


Z_BLOCK0 = 8
DT_COL_BLOCK = 9
DT_BLOCK0 = 8 * DT_COL_BLOCK
GATE_BLOCK0 = 10
B_BLOCK0 = 16
C_BLOCK0 = 20


def _head_expand():
    e = np.zeros((N_STATE, GROUP_W), np.float32)
    for hh in range(HEADS_PER_GROUP):
        e[hh, hh * HEAD_P:(hh + 1) * HEAD_P] = 1.0
    return jnp.asarray(e, BF16)


def _ssd_chunk_terms(dt, bias, alog, expand, tril_f, eye):
    dtb = dt + bias
    delta = jnp.maximum(dtb, 0.0) + jnp.log(1.0 + jnp.exp(-jnp.abs(dtb)))
    ea = jnp.exp(alog)
    a = -ea * delta
    acum = _sel(_nn, a, tril_f, 3, x_first=False)
    delta_e = _sel(_nn, delta, expand, 2)
    acum_e = _sel(_nn, acum, expand, 3)
    acum_t = _sel(_nt, acum, eye, 3, x_first=False)
    return dtb, delta, ea, a, acum, delta_e, acum_e, acum_t


def ssd_fwd(proj, xc, alog4, bias4, dskip4, wnorm, expand):
    t = proj.shape[0]
    tb = _tile(t, TOKEN_BLOCK)
    ncb = tb // SSD_CHUNK

    def body(xs_ref, b_ref, c_ref, dt_ref, z_ref, alog_ref, bias_ref, dsk_ref, wn_ref, e_ref, ob_ref, st_ref, state):
        @pl.when(pl.program_id(1) == 0)
        def _():
            state[...] = jnp.zeros_like(state)

        expand = e_ref[...]
        mask = _tri(SSD_CHUNK)
        tril_f = mask.astype(BF16)
        eye = (lax.broadcasted_iota(jnp.int32, (N_STATE, N_STATE), 0) ==
               lax.broadcasted_iota(jnp.int32, (N_STATE, N_STATE), 1)).astype(BF16)
        alog, bias = alog_ref[0], bias_ref[0]
        d_e = _sel(_nn, jnp.broadcast_to(dsk_ref[0], (8, N_STATE)), expand, 3)[0:1, :]
        wn = wn_ref[...]

        def chunk(c, carry):
            sl = pl.ds(pl.multiple_of(c * SSD_CHUNK, SSD_CHUNK), SSD_CHUNK)
            xs, bm, cm, dt, z = xs_ref[sl, :], b_ref[sl, :], c_ref[sl, :], dt_ref[sl, :], z_ref[sl, :]
            dtb, delta, ea, a, acum, delta_e, acum_e, acum_t = _ssd_chunk_terms(dt, bias, alog, expand, tril_f, eye)
            alast_e = acum_e[SSD_CHUNK - 1:SSD_CHUNK, :]
            xd = xs * delta_e
            xdb = xd.astype(BF16)
            cb_, bb_ = cm.astype(BF16), bm.astype(BF16)
            cbm = _nt(cb_, bb_)
            ys = []
            for hh in range(HEADS_PER_GROUP):
                lh = jnp.where(mask, jnp.exp(jnp.minimum(acum[:, hh:hh + 1] - acum_t[hh:hh + 1, :], 0.0)), 0.0)
                ys.append(_nn((cbm * lh).astype(BF16), xdb[:, hh * HEAD_P:(hh + 1) * HEAD_P]))
            st = state[...]
            st_ref[0, c] = st
            y = jnp.concatenate(ys, axis=1) + _nn(cb_, st.astype(BF16)) * jnp.exp(acum_e) + xs * d_e
            state[...] = st * jnp.exp(alast_e) + _tn(bb_, (xd * jnp.exp(alast_e - acum_e)).astype(BF16))
            yg = y * z * _sigmoid(z)
            ob_ref[sl, :] = (yg * lax.rsqrt(jnp.mean(yg * yg, axis=-1, keepdims=True) + RMS_EPS) * wn).astype(BF16)
            return carry

        lax.fori_loop(0, ncb, chunk, 0, unroll=min(CHUNK_UNROLL, ncb))

    small = pl.BlockSpec((1, 1, N_STATE), lambda g, j: (g, 0, 0))
    return pl.pallas_call(
        body, name="ssd_fwd", grid=(N_GROUPS, t // tb),
        out_shape=[jax.ShapeDtypeStruct((t, B_INNER), BF16),
                   jax.ShapeDtypeStruct((N_GROUPS, t // SSD_CHUNK, N_STATE, GROUP_W), F32)],
        in_specs=[pl.BlockSpec((tb, GROUP_W), lambda g, j: (j, g)),
                  pl.BlockSpec((tb, N_STATE), lambda g, j: (j, B_BLOCK0 + g)),
                  pl.BlockSpec((tb, N_STATE), lambda g, j: (j, C_BLOCK0 + g)),
                  pl.BlockSpec((tb, N_STATE), lambda g, j: (j, DT_BLOCK0 + g)),
                  pl.BlockSpec((tb, GROUP_W), lambda g, j: (j, Z_BLOCK0 + g)),
                  small, small, small, pl.BlockSpec((1, GROUP_W), lambda g, j: (0, g)),
                  pl.BlockSpec((N_STATE, GROUP_W), lambda g, j: (0, 0))],
        out_specs=[pl.BlockSpec((tb, GROUP_W), lambda g, j: (j, g)),
                   pl.BlockSpec((1, ncb, N_STATE, GROUP_W), lambda g, j: (g, j, 0, 0))],
        scratch_shapes=[pltpu.VMEM((N_STATE, GROUP_W), F32)],
        compiler_params=_cparams(("parallel", "arbitrary")),
    )(xc, xc, xc, proj, proj, alog4, bias4, dskip4, wnorm, expand)


def ssd_bwd(proj, xc, alog4, bias4, dskip4, wnorm, expand, dob, states, part, dproj):
    t = proj.shape[0]
    tb = _tile(t, TOKEN_BLOCK)
    lc = min(SSD_CHUNK_BWD, tb)
    ncb = tb // lc
    nsaved = tb // SSD_CHUNK
    nb = t // tb

    def body(xs_ref, b_ref, c_ref, dt_ref, z_ref, alog_ref, bias_ref, dsk_ref, wn_ref, e_ref, dob_ref, st_ref, part_ref,
             dp_in_ref, dxs_ref, db_ref, dc_ref, dz_ref, ddt_ref, dwn_ref, dalog_ref, dbias_ref, ddsk_ref, parts_ref, dstate,
             send_sems, recv_sems, local_sem):
        xchg_start, xchg_wait = _chip_exchange(part_ref, parts_ref, send_sems, recv_sems, local_sem)

        @pl.when((pl.program_id(0) == 0) & (pl.program_id(1) == 0))
        def _():
            xchg_start()

        @pl.when(pl.program_id(1) == 0)
        def _():
            dstate[...] = jnp.zeros_like(dstate)
            dwn_ref[...] = jnp.zeros_like(dwn_ref)
            dalog_ref[...] = jnp.zeros_like(dalog_ref)
            dbias_ref[...] = jnp.zeros_like(dbias_ref)
            ddsk_ref[...] = jnp.zeros_like(ddsk_ref)

        expand = e_ref[...]
        mask = _tri(lc)
        mask_t = _tri(lc, upper=True)
        tril_f = mask.astype(BF16)
        triu_f = mask_t.astype(BF16)
        eye = (lax.broadcasted_iota(jnp.int32, (N_STATE, N_STATE), 0) ==
               lax.broadcasted_iota(jnp.int32, (N_STATE, N_STATE), 1)).astype(BF16)
        alog, bias = alog_ref[0], bias_ref[0]
        d_e = _sel(_nn, jnp.broadcast_to(dsk_ref[0], (8, N_STATE)), expand, 3)[0:1, :]
        wn = wn_ref[...]

        def chunk(i, c0):
            c = ncb - 1 - i
            sl = pl.ds(pl.multiple_of(c * lc, lc), lc)
            xs, bm, cm, dt, z = xs_ref[sl, :], b_ref[sl, :], c_ref[sl, :], dt_ref[sl, :], z_ref[sl, :]
            dtb, delta, ea, a, acum, delta_e, acum_e, acum_t = _ssd_chunk_terms(dt, bias, alog, expand, tril_f, eye)
            alast_e = acum_e[lc - 1:lc, :]
            eacum = jnp.exp(acum_e)
            wl = jnp.exp(alast_e - acum_e)
            xd = xs * delta_e
            xdb = xd.astype(BF16)
            cb_, bb_ = cm.astype(BF16), bm.astype(BF16)
            cbm = _nt(cb_, bb_)
            st32 = st_ref[0, c * (lc // SSD_CHUNK)]
            stb = st32.astype(BF16)
            dst = dstate[...]
            dstb = dst.astype(BF16)
            lhs, mixes, ys = [], [], []
            for hh in range(HEADS_PER_GROUP):
                col, row = acum[:, hh:hh + 1], acum_t[hh:hh + 1, :]
                lh = jnp.where(mask, jnp.exp(jnp.minimum(col - row, 0.0)), 0.0)
                mix = (cbm * lh).astype(BF16)
                lhs.append(lh)
                mixes.append(mix)
                ys.append(_nn(mix, xdb[:, hh * HEAD_P:(hh + 1) * HEAD_P]))
            y_in = jnp.concatenate(ys, axis=1)
            y_out = _nn(cb_, stb) * eacum
            y = y_in + y_out + xs * d_e
            sgz = _sigmoid(z)
            sz = z * sgz
            yg = y * sz
            rstd = lax.rsqrt(jnp.mean(yg * yg, axis=-1, keepdims=True) + RMS_EPS)
            nrm = yg * rstd
            dob_v = dob_ref[sl, :]
            dn = dob_v * wn
            dwn_ref[...] += _colsum(dob_v * nrm)
            dyg = rstd * (dn - nrm * jnp.mean(dn * nrm, axis=-1, keepdims=True))
            dy = dyg * sz
            dz_ref[sl, :] = (dyg * y * _dsilu(z, sgz)).astype(BF16)
            dyb = dy.astype(BF16)
            dxds = []
            dcb = jnp.zeros((lc, lc), F32)
            for hh in range(HEADS_PER_GROUP):
                hs = slice(hh * HEAD_P, (hh + 1) * HEAD_P)
                dy_h, x_h = dyb[:, hs], xdb[:, hs]
                dxds.append(_tn(mixes[hh], dy_h))
                dcb = dcb + _nt(dy_h, x_h) * lhs[hh]
            dcbb = dcb.astype(BF16)
            dye = (dy * eacum).astype(BF16)
            xw = (xd * wl).astype(BF16)
            dxd_in = jnp.concatenate(dxds, axis=1)
            dxd_out = wl * _nn(bb_, dstb)
            dxd = dxd_in + dxd_out
            dc_ref[sl, :] = (_nn(dcbb, bb_) + _nt(dye, stb)).astype(dc_ref.dtype)
            db_ref[sl, :] = (_tn(dcbb, cb_) + _nt(xw, dstb)).astype(db_ref.dtype)
            dstate[...] = dst * jnp.exp(alast_e) + _tn(cb_, dye)
            col_out = xd * dxd_out
            dac = _sel(_nt, dyb.astype(F32) * y_in - xdb.astype(F32) * dxd_in + dy * y_out - col_out, expand, 3)
            beyond = _colsum(col_out) + jnp.exp(alast_e) * _colsum(dst * st32)
            da = (_sel(_nn, dac, triu_f, 3, x_first=False) +
                  _sel(_nt, jnp.broadcast_to(beyond, (8, GROUP_W)), expand, 3)[0:1, :])
            ddelta = _sel(_nt, dxd * xs, expand, 2) - da * ea
            dalog_ref[0] += _colsum(da * a)
            ddtb = ddelta * _sigmoid(dtb)
            dbias_ref[0] += _colsum(ddtb)
            ddt_ref[sl, :] = ddtb.astype(BF16)
            ddsk_ref[0] += _sel(_nt, jnp.broadcast_to(_colsum(dy * xs), (8, GROUP_W)), expand, 3)[0:1, :]
            dxs_ref[sl, :] = (dxd * delta_e + dy * d_e).astype(dxs_ref.dtype)
            return c0

        lax.fori_loop(0, ncb, chunk, 0, unroll=min(CHUNK_UNROLL, ncb))

        @pl.when((pl.program_id(0) == N_GROUPS - 1) & (pl.program_id(1) == nb - 1))
        def _():
            xchg_wait()

    small = pl.BlockSpec((1, 1, N_STATE), lambda g, j: (g, 0, 0))
    wide = pl.BlockSpec((tb, GROUP_W), lambda g, j: (nb - 1 - j, g))
    narrow = pl.BlockSpec((tb, N_STATE), lambda g, j: (nb - 1 - j, g))
    hbm = pl.BlockSpec(memory_space=pl.ANY)
    return pl.pallas_call(
        body, name="ssd_bwd", grid=(N_GROUPS, nb),
        out_shape=[jax.ShapeDtypeStruct((t, B_INNER), BF16), jax.ShapeDtypeStruct((t, GROUP_W), BF16),
                   jax.ShapeDtypeStruct((t, GROUP_W), BF16), jax.ShapeDtypeStruct(dproj.shape, dproj.dtype),
                   jax.ShapeDtypeStruct((t, GROUP_W), BF16), jax.ShapeDtypeStruct((1, B_INNER), F32),
                   jax.ShapeDtypeStruct((N_GROUPS, 1, N_STATE), F32), jax.ShapeDtypeStruct((N_GROUPS, 1, N_STATE), F32),
                   jax.ShapeDtypeStruct((N_GROUPS, 1, N_STATE), F32), jax.ShapeDtypeStruct(part.shape, part.dtype)],
        in_specs=[wide,
                  pl.BlockSpec((tb, N_STATE), lambda g, j: (nb - 1 - j, B_BLOCK0 + g)),
                  pl.BlockSpec((tb, N_STATE), lambda g, j: (nb - 1 - j, C_BLOCK0 + g)),
                  pl.BlockSpec((tb, N_STATE), lambda g, j: (nb - 1 - j, DT_BLOCK0 + g)),
                  pl.BlockSpec((tb, GROUP_W), lambda g, j: (nb - 1 - j, Z_BLOCK0 + g)),
                  small, small, small, pl.BlockSpec((1, GROUP_W), lambda g, j: (0, g)),
                  pl.BlockSpec((N_STATE, GROUP_W), lambda g, j: (0, 0)), wide,
                  pl.BlockSpec((1, nsaved, N_STATE, GROUP_W), lambda g, j: (g, nb - 1 - j, 0, 0)), hbm, hbm],
        out_specs=[wide, narrow, narrow,
                   pl.BlockSpec((None, tb, GROUP_W), lambda g, j: (Z_BLOCK0 // 2 + g // 2, nb - 1 - j, g % 2)),
                   narrow, pl.BlockSpec((1, GROUP_W), lambda g, j: (0, g)), small, small, small, hbm],
        input_output_aliases={13: 3},
        scratch_shapes=[pltpu.VMEM((N_STATE, GROUP_W), F32)] + CHIP_SEMS,
        compiler_params=_cparams(("arbitrary", "arbitrary")),
    )(xc, xc, xc, proj, proj, alog4, bias4, dskip4, wnorm, expand, dob, states, part, dproj)


def lower_bound_fwd(hgrn_lb):
    def body(a_ref, o_ref):
        a0, a1 = a_ref[0:1, :], a_ref[1:2, :]
        m = jnp.maximum(a0, a1)
        e0, e1 = jnp.exp(a0 - m), jnp.exp(a1 - m)
        o_ref[...] = e0 / (e0 + e1)

    return pl.pallas_call(body, name="lower_bound_fwd", out_shape=jax.ShapeDtypeStruct((1, D), F32))(hgrn_lb)


def ada_weight_grad(c_all, dmod_cols):
    def body(c_ref, d_ref, o_ref):
        cval = c_ref[...]
        o_ref[...] = _tn(cval * _sigmoid(cval), d_ref[...], HI)

    return pl.pallas_call(body, name="ada_weight_grad",
                          out_shape=jax.ShapeDtypeStruct((D, dmod_cols.shape[1]), F32))(c_all, dmod_cols)


def reduce_small(gathered, hgrn_lb, dlb_off):
    n = gathered.shape[2]

    def body(g_ref, a_ref, o_ref, glb_ref):
        s = g_ref[0]
        for d in range(1, N_DEV):
            s = s + g_ref[d]
        o_ref[...] = s
        a0, a1 = a_ref[0:1, :], a_ref[1:2, :]
        m = jnp.maximum(a0, a1)
        e0, e1 = jnp.exp(a0 - m), jnp.exp(a1 - m)
        p0 = e0 / (e0 + e1)
        tq = s[:, dlb_off:dlb_off + D] * p0 * (1.0 - p0)
        glb_ref[0:1, :] = tq
        glb_ref[1:2, :] = -tq

    return pl.pallas_call(body, name="reduce_small",
                          out_shape=[jax.ShapeDtypeStruct((1, n), F32), jax.ShapeDtypeStruct((2, D), F32)])(gathered, hgrn_lb)


def _adam_math(w, g, m, v):
    m2 = ADAM_B1 * m + (1.0 - ADAM_B1) * g
    v2 = ADAM_B2 * v + (1.0 - ADAM_B2) * (g * g)
    m_hat = m2 / (1.0 - ADAM_B1 ** ADAM_STEP)
    v_hat = v2 / (1.0 - ADAM_B2 ** ADAM_STEP)
    delta = -ADAM_LR * (m_hat / (jnp.sqrt(v_hat) + ADAM_EPS) + ADAM_WD * w)
    return delta, m2, v2


def _row_tile(rows, mult=8, cap=128):
    for cand in range(cap - cap % mult, 0, -mult):
        if rows % cand == 0:
            return cand
    return rows


def sum_parts(parts, name):
    n, rows, cols = parts.shape
    tr = _row_tile(rows, 16, 256)

    def body(p_ref, o_ref):
        s = p_ref[0].astype(F32)
        for d in range(1, n):
            s = s + p_ref[d].astype(F32)
        o_ref[...] = s

    return pl.pallas_call(
        body, name=name, grid=(rows // tr,),
        out_shape=jax.ShapeDtypeStruct((rows, cols), F32),
        in_specs=[pl.BlockSpec((n, tr, cols), lambda i: (0, i, 0))],
        out_specs=pl.BlockSpec((tr, cols), lambda i: (i, 0)),
        compiler_params=_cparams(("parallel",)),
    )(parts)


def sum_pair(a, b, name):
    rows, cols = a.shape
    tr = _row_tile(rows, 16, 256)

    def body(a_ref, b_ref, o_ref):
        o_ref[...] = (a_ref[...].astype(F32) + b_ref[...].astype(F32)).astype(o_ref.dtype)

    blk = pl.BlockSpec((tr, cols), lambda i: (i, 0))
    return pl.pallas_call(
        body, name=name, grid=(rows // tr,),
        out_shape=jax.ShapeDtypeStruct((rows, cols), a.dtype),
        in_specs=[blk, blk], out_specs=blk,
        compiler_params=_cparams(("parallel",)),
    )(a, b)


def adamw(w, g, m, v, name):
    rows, cols = w.shape
    tr = _row_tile(rows)

    def body(w_ref, g_ref, m_ref, v_ref, d_ref, m2_ref, v2_ref):
        delta, m2, v2 = _adam_math(w_ref[...], g_ref[...], m_ref[...], v_ref[...])
        d_ref[...] = delta
        m2_ref[...] = m2
        v2_ref[...] = v2

    blk = pl.BlockSpec((tr, cols), lambda i: (i, 0))
    return pl.pallas_call(
        body, name=name, grid=(rows // tr,),
        out_shape=[jax.ShapeDtypeStruct((rows, cols), F32)] * 3,
        in_specs=[blk] * 4, out_specs=[blk] * 3,
        compiler_params=_cparams(("parallel",)),
    )(w, g, m, v)


def _pad128(n):
    return -(-n // 128) * 128


def _pack(arrays):
    offs, parts, off = [], [], 0
    for a in arrays:
        flat = a.reshape(1, -1)
        n = flat.shape[1]
        offs.append(off)
        parts.append(jnp.pad(flat, ((0, 0), (0, _pad128(n) - n))))
        off += _pad128(n)
    return jnp.concatenate(parts, axis=1), offs


def _unpack(vec, offs, shapes):
    out = []
    for off, shp in zip(offs, shapes):
        n = int(np.prod(shp))
        out.append(vec[0, off:off + n].reshape(shp))
    return out


IN_ROWS = IN_DIM // N_DEV
DT_ROW0 = 9216
DT_DEV, DT_LO = divmod(DT_ROW0, IN_ROWS)


GATE_SHIFT = D - 32


def _in_row_pieces(tile):
    pieces = []
    if tile == DT_COL_BLOCK:
        for g in range(N_GROUPS):
            o = DT_ROW0 + HEADS_PER_GROUP * g
            pieces.append((N_STATE * g, o // IN_ROWS, o % IN_ROWS, HEADS_PER_GROUP))
        return pieces
    r, end = tile * D, (tile + 1) * D
    while r < end:
        o = r if r < DT_ROW0 else r - GATE_SHIFT
        dev, loc = divmod(o, IN_ROWS)
        n = min(end - r, IN_ROWS - loc)
        pieces.append((r - tile * D, dev, loc, n))
        r += n
    return pieces


def assemble_w_in(g_all):
    ntile = N_PROJ // D

    def body(g_ref, o_ref):
        j = pl.program_id(0)
        for tile in range(ntile):
            @pl.when(j == tile)
            def _(tile=tile):
                if tile == DT_COL_BLOCK:
                    o_ref[...] = jnp.zeros_like(o_ref)
                for dst, dev, loc, n in _in_row_pieces(tile):
                    o_ref[pl.ds(dst, n), :] = g_ref[dev, pl.ds(loc, n), :]

    return pl.pallas_call(
        body, name="assemble_w_in", grid=(ntile,),
        out_shape=jax.ShapeDtypeStruct((N_PROJ, D), g_all.dtype),
        in_specs=[pl.BlockSpec(memory_space=pltpu.VMEM)],
        out_specs=pl.BlockSpec((D, D), lambda j: (j, 0)),
        compiler_params=_cparams(("arbitrary",)),
    )(g_all)


def _grad_in_blocks(g_t, core, slot):
    dt0 = DT_COL_BLOCK * D
    dt = g_t[dt0:dt0 + N_GROUPS * N_STATE].reshape(N_GROUPS, N_STATE, D)[:, :HEADS_PER_GROUP].reshape(32, D)
    with_dt = jnp.concatenate([g_t[DT_DEV * IN_ROWS:DT_ROW0], dt,
                               g_t[DT_ROW0 + 32 + GATE_SHIFT:(DT_DEV + 1) * IN_ROWS + GATE_SHIFT]], axis=0)
    blocks = []
    for q in range(N_CHIP):
        if 2 * q + 1 < DT_DEV:
            blk = lax.dynamic_slice_in_dim(g_t, IN_ROWS * (2 * q + core), IN_ROWS, axis=0)
        else:
            assert 2 * q == DT_DEV
            after = g_t[(DT_DEV + 1) * IN_ROWS + GATE_SHIFT:(DT_DEV + 2) * IN_ROWS + GATE_SHIFT]
            blk = jnp.where(core == 0, with_dt, after)
        blocks.append(jnp.pad(blk, ((0, slot - IN_ROWS), (0, 0))))
    return jnp.stack(blocks)


def kernel(x, c, w_ada, b_ada, w_in, hgrn_lb, hgrn_gnorm, ssm_conv_w, ssm_conv_b, ssm_dt_bias, ssm_a_log, ssm_d, ssm_norm, w_branch_a, w_branch_b, w_o, ln1_g, ln1_b, w_ffn_gate, w_ffn_up, w_ffn_down, ln2_g, ln2_b, loss_target, m_w_ada, m_b_ada, m_w_in, m_hgrn_lb, m_hgrn_gnorm, m_ssm_conv_w, m_ssm_conv_b, m_ssm_dt_bias, m_ssm_a_log, m_ssm_d, m_ssm_norm, m_w_branch_a, m_w_branch_b, m_w_o, m_ln1_g, m_ln1_b, m_w_ffn_gate, m_w_ffn_up, m_w_ffn_down, m_ln2_g, m_ln2_b, v_w_ada, v_b_ada, v_w_in, v_hgrn_lb, v_hgrn_gnorm, v_ssm_conv_w, v_ssm_conv_b, v_ssm_dt_bias, v_ssm_a_log, v_ssm_d, v_ssm_norm, v_w_branch_a, v_w_branch_b, v_w_o, v_ln1_g, v_ln1_b, v_w_ffn_gate, v_w_ffn_up, v_w_ffn_down, v_ln2_g, v_ln2_b):
    me = 4 * lax.axis_index("x") + 2 * lax.axis_index("y") + lax.axis_index("c")
    xt = x[0]
    tgt = loss_target[0]
    t = xt.shape[0]
    ada_cols = w_ada.shape[2]
    conv_cols = ssm_conv_w.shape[2]

    small_in, _ = _pack([c, ssm_conv_w[0]])
    small_all = allgather_vmem(small_in, "allgather_small_inputs")
    c_all = small_all[:, 0, :D]
    conv_w = small_all[:, 0, D:D + CONV_TAPS * conv_cols].reshape(N_DEV, CONV_TAPS, conv_cols)
    conv_w = conv_w.transpose(1, 0, 2).reshape(CONV_TAPS, CONV_DIM)
    mod = ada_modulation(c_all, w_ada[0], b_ada.reshape(N_DEV, 1, ada_cols))
    mod6 = mod.reshape(6, D)

    shards = [w_in[0].T, w_branch_a[0], w_branch_b[0], w_o[0], w_ffn_gate[0].T, w_ffn_up[0].T, w_ffn_down[0]]
    shard_rows = [s.shape[0] for s in shards]
    slot_rows = [-(-r // 32) * 32 for r in shard_rows]
    row_offs = [sum(slot_rows[:i]) for i in range(len(shards))]
    padded = [jnp.pad(s.astype(BF16), ((0, p - r), (0, 0))) for s, r, p in zip(shards, shard_rows, slot_rows)]
    w_in_t = assemble_w_in(allgather_hbm(padded[0], "allgather_w_in"))

    lb = lower_bound_fwd(hgrn_lb)
    u1 = ln_modulate(xt, mod6, 0, 1, "ln_modulate_1")
    proj, g_rest = mm_nt_gather(u1, w_in_t, F32, jnp.concatenate(padded[1:], axis=0), "mm_in_proj")
    g_ba, g_bb, g_o, g_fg, g_fu, g_fd = (g_rest[:, o - slot_rows[0]:o - slot_rows[0] + r]
                                         for o, r in zip(row_offs[1:], shard_rows[1:]))
    w_ba = g_ba.reshape(D, D)
    w_bb = g_bb.reshape(B_INNER, D)
    w_oo = g_o.reshape(D, D)
    w_gu_t = jnp.concatenate([g_fg.reshape(D_FF, D), g_fu.reshape(D_FF, D)], axis=0)
    w_dn = g_fd.reshape(D_FF, D)
    o_a, o_raw, st_a = hgrn_fwd(proj, lb, hgrn_gnorm)
    xc, conv_slope = conv_fwd(proj, conv_w, ssm_conv_b)
    pad3 = ((0, 0), (0, 0), (0, N_STATE - HEADS_PER_GROUP))
    alog4 = jnp.pad(ssm_a_log.reshape(N_GROUPS, 1, HEADS_PER_GROUP), pad3)
    bias4 = jnp.pad(ssm_dt_bias.reshape(N_GROUPS, 1, HEADS_PER_GROUP), pad3)
    dskip4 = jnp.pad(ssm_d.reshape(N_GROUPS, 1, HEADS_PER_GROUP), pad3)
    expand = _head_expand()
    o_b, st_b = ssd_fwd(proj, xc, alog4, bias4, dskip4, ssm_norm, expand)
    ya = mm_nn(o_a, w_ba, BF16, "mm_branch_a")
    yb = mm_nn(o_b, w_bb, BF16, "mm_branch_b")
    merged = merge_gates(ya, yb, proj)
    h1 = mm_nn(merged, w_oo, F32, "mm_out_proj")
    x1 = resid_ln(xt, h1, mod6, 2, ln1_g, ln1_b, "resid_ln_1")
    u2 = ln_modulate(x1, mod6, 3, 4, "ln_modulate_2")
    gu = mm_nt(u2, w_gu_t, BF16, "mm_ffn_in")
    act = swiglu_act(gu)
    h2 = mm_nn(act, w_dn, F32, "mm_ffn_out")

    dh2, dx1_part, acc4 = resid_ln_bwd(x1, h2, mod6, 5, ln2_g, ln2_b, tgt, True, "resid_ln_2_bwd")
    g_dn = mm_tn(act, dh2, "mm_grad_ffn_down")
    dact = mm_nt(dh2, w_dn, BF16, "mm_dact")
    dgu = swiglu_act_bwd(gu, dact)
    g_gu_t = mm_tn(dgu, u2, "mm_grad_ffn_in")
    du2 = mm_nn(dgu, w_gu_t, F32, "mm_du2")
    dx1, acc3 = ln_modulate_bwd(x1, du2, mod6, 4, dx1_part, "ln_modulate_2_bwd")
    dh1, dx_part, acc2 = resid_ln_bwd(xt, h1, mod6, 2, ln1_g, ln1_b, dx1, False, "resid_ln_1_bwd")
    g_o = mm_tn(merged, dh1, "mm_grad_out_proj")
    dmerged = mm_nt(dh1, w_oo, BF16, "mm_dmerged")
    dya, dyb, dproj = merge_gates_bwd(dmerged, ya, yb, proj)
    g_ba_full = mm_tn(o_a, dya, "mm_grad_branch_a")
    g_bb_full = mm_tn(o_b, dyb, "mm_grad_branch_b")
    doa = mm_nt(dya, w_ba, F32, "mm_doa")
    dob = mm_nt(dyb, w_bb, F32, "mm_dob")
    my_core = lax.axis_index("c")

    def by_core(blocks, rows, slots):
        contrib = jnp.concatenate([jnp.pad(b.reshape(N_DEV, -1, D), ((0, 0), (0, p - r), (0, 0)))
                                   for b, r, p in zip(blocks, rows, slots)], axis=1)
        split = contrib.reshape(N_CHIP, 2, contrib.shape[1], D).transpose(1, 0, 2, 3)
        return (lax.dynamic_index_in_dim(split, my_core, 0, keepdims=False),
                lax.dynamic_index_in_dim(split, 1 - my_core, 0, keepdims=False))

    keep_e, give_e = by_core([g_ba_full, g_bb_full, g_o, g_gu_t[:D_FF], g_gu_t[D_FF:], g_dn],
                             shard_rows[1:], slot_rows[1:])
    dproj, dlb, dgn, got_e = hgrn_bwd(proj, lb, hgrn_gnorm, o_raw, doa, st_a, give_e, dproj)
    chip_e = sum_pair(keep_e.reshape(-1, D), got_e.reshape(-1, D), "sum_grads_rest_chip").reshape(keep_e.shape)
    dxs, dbm, dcm, dproj, ddt, dwn, dalog, dbias, ddsk, parts_e = ssd_bwd(proj, xc, alog4, bias4, dskip4, ssm_norm,
                                                                          expand, dob, st_b, chip_e, dproj)
    dxc = jnp.concatenate([dxs, dbm, dcm], axis=1)
    dproj, dcw, dcb = conv_bwd(proj, dxc, conv_slope, conv_w, dproj)
    dproj = dt_fill(ddt, dproj)
    g_in_t = mm_tn(dproj, u1, "mm_grad_in_proj")
    keep_l = _grad_in_blocks(g_in_t, my_core, slot_rows[0])
    give_l = _grad_in_blocks(g_in_t, 1 - my_core, slot_rows[0])
    got_l = exchange_sibling(give_l, "exchange_grad_in_sibling")
    chip_l = sum_pair(keep_l.reshape(-1, D), got_l.reshape(-1, D), "sum_grad_in_chip").reshape(keep_l.shape)
    du1, parts_l = mm_nn_exchange(dproj, w_in_t, F32, chip_l, "mm_du1")
    dx, acc1 = ln_modulate_bwd(xt, du1, mod6, 1, dx_part, "ln_modulate_1_bwd")
    gw_in = sum_parts(parts_l, "sum_grad_in")[:shard_rows[0]].T
    g_rows = sum_parts(parts_e, "sum_grads_rest")
    gw_ba, gw_bb, gw_o, gw_fg, gw_fu, gw_fd = (g_rows[o - slot_rows[0]:o - slot_rows[0] + r]
                                               for o, r in zip(row_offs[1:], shard_rows[1:]))
    gw_fg, gw_fu = gw_fg.T, gw_fu.T

    dmod = jnp.concatenate([acc1[1:2], acc1[0:1], acc2[0:1], acc3[1:2], acc3[0:1], acc4[0:1]], axis=1)
    small_fields = [dmod, acc4[3:4, :128], dlb, dgn, dcw[:CONV_TAPS], dcb, dbias, dalog, ddsk, dwn,
                    acc2[1:2], acc2[2:3], acc4[1:2], acc4[2:3]]
    small_out, offs = _pack(small_fields)
    small_sum_in = allgather_vmem(small_out, "allgather_small_grads")
    gsum, g_lb = reduce_small(small_sum_in, hgrn_lb, offs[2])
    (g_bada, loss_row, _, g_gn, g_cw_full, g_cb, g_bias4, g_alog4, g_dsk4, g_wn, g_l1g, g_l1b, g_l2g, g_l2b) = _unpack(
        gsum, offs, [(1, 6 * D), (1, 128), (1, D), (1, HK), (CONV_TAPS, CONV_DIM), (1, CONV_DIM),
                     (N_GROUPS, N_STATE), (N_GROUPS, N_STATE), (N_GROUPS, N_STATE), (1, B_INNER),
                     (1, D), (1, D), (1, D), (1, D)])
    loss = loss_row[0, 0]
    g_cw = lax.dynamic_slice(g_cw_full, (0, me * conv_cols), (CONV_TAPS, conv_cols))[None]
    g_dtb = g_bias4[:, :HEADS_PER_GROUP].reshape(1, 32)
    g_alog = g_alog4[:, :HEADS_PER_GROUP].reshape(1, 32)
    g_dsk = g_dsk4[:, :HEADS_PER_GROUP].reshape(1, 32)

    dmod_all = small_sum_in[:, 0, offs[0]:offs[0] + 6 * D]
    dmod_cols = lax.dynamic_slice(dmod_all, (0, me * ada_cols), (N_DEV, ada_cols))
    gw_ada = ada_weight_grad(c_all, dmod_cols)

    big = [("ada", w_ada[0], gw_ada, m_w_ada[0], v_w_ada[0]), ("in", w_in[0], gw_in, m_w_in[0], v_w_in[0]),
           ("branch_a", w_branch_a[0], gw_ba, m_w_branch_a[0], v_w_branch_a[0]),
           ("branch_b", w_branch_b[0], gw_bb, m_w_branch_b[0], v_w_branch_b[0]),
           ("o", w_o[0], gw_o, m_w_o[0], v_w_o[0]),
           ("ffn_gate", w_ffn_gate[0], gw_fg, m_w_ffn_gate[0], v_w_ffn_gate[0]),
           ("ffn_up", w_ffn_up[0], gw_fu, m_w_ffn_up[0], v_w_ffn_up[0]),
           ("ffn_down", w_ffn_down[0], gw_fd, m_w_ffn_down[0], v_w_ffn_down[0])]
    big_out = {}
    for nm, w_, g_, m_, v_ in big:
        d_, m2_, v2_ = adamw(w_, g_, m_, v_, "adamw_" + nm)
        big_out[nm] = (g_[None], d_[None], m2_[None], v2_[None])

    small_w = [b_ada, hgrn_lb, hgrn_gnorm, ssm_conv_w, ssm_conv_b, ssm_dt_bias, ssm_a_log, ssm_d, ssm_norm,
               ln1_g, ln1_b, ln2_g, ln2_b]
    small_g = [g_bada, g_lb, g_gn, g_cw, g_cb, g_dtb, g_alog, g_dsk, g_wn, g_l1g, g_l1b, g_l2g, g_l2b]
    small_m = [m_b_ada, m_hgrn_lb, m_hgrn_gnorm, m_ssm_conv_w, m_ssm_conv_b, m_ssm_dt_bias, m_ssm_a_log, m_ssm_d,
               m_ssm_norm, m_ln1_g, m_ln1_b, m_ln2_g, m_ln2_b]
    small_v = [v_b_ada, v_hgrn_lb, v_hgrn_gnorm, v_ssm_conv_w, v_ssm_conv_b, v_ssm_dt_bias, v_ssm_a_log, v_ssm_d,
               v_ssm_norm, v_ln1_g, v_ln1_b, v_ln2_g, v_ln2_b]
    shapes = [a.shape for a in small_w]
    small_g = [g_.reshape(s) for g_, s in zip(small_g, shapes)]
    pw, poffs = _pack(small_w)
    pg, _ = _pack(small_g)
    pm, _ = _pack(small_m)
    pv, _ = _pack(small_v)
    pd, pm2, pv2 = adamw(pw, pg, pm, pv, "adamw_small")
    s_d, s_m, s_v = (_unpack(p, poffs, shapes) for p in (pd, pm2, pv2))
    (sn_bada, sn_lb, sn_gn, sn_cw, sn_cb, sn_dtb, sn_alog, sn_dsk, sn_wn, sn_l1g, sn_l1b, sn_l2g, sn_l2b) = range(13)

    def order(kind):
        sm = [small_g, s_d, s_m, s_v][kind]
        bg = lambda nm: big_out[nm][kind]
        return [bg("ada"), sm[sn_bada], bg("in"), sm[sn_lb], sm[sn_gn], sm[sn_cw], sm[sn_cb], sm[sn_dtb], sm[sn_alog],
                sm[sn_dsk], sm[sn_wn], bg("branch_a"), bg("branch_b"), bg("o"), sm[sn_l1g], sm[sn_l1b],
                bg("ffn_gate"), bg("ffn_up"), bg("ffn_down"), sm[sn_l2g], sm[sn_l2b]]

    return (loss, dx[None], *order(0), *order(1), *order(2), *order(3))
```

```python
import numpy as np
import jax
import jax.numpy as jnp
from jax import lax
from jax.experimental import pallas as pl
from jax.experimental.pallas import tpu as pltpu

F32 = jnp.float32
BF16 = jnp.bfloat16
HI = lax.Precision.HIGHEST

N_DEV = 8
D = 1024
N_HEADS_A = 8
HK = 128
CHUNK = 64
SSD_CHUNK = 128
SSD_CHUNK_BWD = 256
N_GROUPS = 4
HEADS_PER_GROUP = 8
HEAD_P = 64
N_STATE = 128
GROUP_W = HEADS_PER_GROUP * HEAD_P
B_INNER = 2048
CONV_DIM = 3072
D_FF = 2816
IN_DIM = 11296
N_PROJ = 12288
ALPHA = 2.0 ** 0.25
LN_EPS = 1e-5
RMS_EPS = 1e-6
Q_SCALE = 128 ** -0.5
EXP_CLIP = 80.0
ADAM_LR, ADAM_B1, ADAM_B2, ADAM_EPS, ADAM_WD, ADAM_STEP = 0.001, 0.9, 0.999, 1e-8, 0.01, 10
VMEM_LIMIT = 48 * 1024 * 1024
TOKEN_BLOCK = 512
ROW_TILE = 512
FFN_ROW_TILE = 256
MM_ROW_TILE = 1024
MM_TOKEN_TILE = 4096
MM_K_TILE = 3072
MM_COL_TILE = 1408
HGRN_HEADS_PER_STEP = 4
CHUNK_UNROLL = 8
MESH_ID = pl.DeviceIdType.MESH

NT_DIMS = (((1,), (1,)), ((), ()))
TN_DIMS = (((0,), (0,)), ((), ()))


def _cparams(sem=None):
    return pltpu.CompilerParams(dimension_semantics=sem, vmem_limit_bytes=VMEM_LIMIT)


def _sigmoid(x):
    return 1.0 / (1.0 + jnp.exp(-x))


def _dsilu(x, s):
    return s * (1.0 + x * (1.0 - s))


def _nt(a, b, precision=None):
    return lax.dot_general(a, b, NT_DIMS, precision=precision, preferred_element_type=F32)


def _tn(a, b, precision=None):
    return lax.dot_general(a, b, TN_DIMS, precision=precision, preferred_element_type=F32)


def _nn(a, b, precision=None):
    return jnp.dot(a, b, precision=precision, preferred_element_type=F32)


def _split(x, pieces):
    out = []
    for i in range(pieces):
        p = x.astype(BF16)
        out.append(p)
        if i + 1 < pieces:
            x = x - p.astype(F32)
    return out


def _sel(dot, x, sel01, pieces, x_first=True):
    acc = None
    for p in _split(x, pieces):
        term = dot(p, sel01) if x_first else dot(sel01, p)
        acc = term if acc is None else acc + term
    return acc


def _ln(x):
    mu = jnp.mean(x, axis=-1, keepdims=True)
    xc = x - mu
    rstd = lax.rsqrt(jnp.mean(xc * xc, axis=-1, keepdims=True) + LN_EPS)
    return xc * rstd, rstd


def _ln_bwd(dxh, xh, rstd):
    return rstd * (dxh - jnp.mean(dxh, axis=-1, keepdims=True) - xh * jnp.mean(dxh * xh, axis=-1, keepdims=True))


def _colsum(x):
    return jnp.sum(x, axis=0, keepdims=True)


def _tri(n, upper=False):
    r = lax.broadcasted_iota(jnp.int32, (n, n), 0)
    c = lax.broadcasted_iota(jnp.int32, (n, n), 1)
    return (c >= r) if upper else (r >= c)


def _my_pos():
    return lax.axis_index("x"), lax.axis_index("y"), lax.axis_index("c")


def _peer(pos, k):
    x, y, c = pos
    return (x ^ ((k >> 2) & 1), y ^ ((k >> 1) & 1), c ^ (k & 1))


def _flat(pos):
    return 4 * pos[0] + 2 * pos[1] + pos[2]


def allgather_vmem(v, name):
    n = v.shape[1]

    def body(v_ref, o_ref, send_sems, recv_sems, local_sem):
        me = _my_pos()
        mine = pltpu.make_async_copy(v_ref, o_ref.at[_flat(me)], local_sem)
        mine.start()
        sends = []
        for k in range(1, N_DEV):
            peer = _peer(me, k)
            cp = pltpu.make_async_remote_copy(v_ref, o_ref.at[_flat(me)], send_sems.at[k - 1], recv_sems.at[k - 1],
                                              device_id=peer, device_id_type=MESH_ID)
            cp.start()
            sends.append(cp)
        for k in range(1, N_DEV):
            peer = _peer(me, k)
            pltpu.make_async_remote_copy(v_ref, o_ref.at[_flat(peer)], send_sems.at[k - 1], recv_sems.at[k - 1],
                                         device_id=peer, device_id_type=MESH_ID).wait_recv()
        for cp in sends:
            cp.wait_send()
        mine.wait()

    return pl.pallas_call(
        body, name=name,
        out_shape=jax.ShapeDtypeStruct((N_DEV, 1, n), F32),
        in_specs=[pl.BlockSpec(memory_space=pltpu.VMEM)],
        out_specs=pl.BlockSpec(memory_space=pltpu.VMEM),
        scratch_shapes=[pltpu.SemaphoreType.DMA((N_DEV - 1,)), pltpu.SemaphoreType.DMA((N_DEV - 1,)),
                        pltpu.SemaphoreType.DMA],
        compiler_params=_cparams(),
    )(v)


def ada_modulation(c_all, w_ada_s, b_ada_r):
    ncol = w_ada_s.shape[1]

    def body(c_ref, w_ref, b_ref, o_ref, part_ref, send_sems, recv_sems):
        me = _my_pos()
        cval = c_ref[...]
        cond = cval * _sigmoid(cval)
        part = _nn(cond, w_ref[...], HI)
        for r in range(N_DEV):
            part_ref[r] = part[r:r + 1, :]
        sends = []
        for k in range(1, N_DEV):
            peer = _peer(me, k)
            cp = pltpu.make_async_remote_copy(part_ref.at[_flat(peer)], o_ref.at[_flat(me)], send_sems.at[k - 1],
                                              recv_sems.at[k - 1], device_id=peer, device_id_type=MESH_ID)
            cp.start()
            sends.append(cp)
        o_ref[_flat(me)] = part_ref[_flat(me)]
        for k in range(1, N_DEV):
            peer = _peer(me, k)
            pltpu.make_async_remote_copy(part_ref.at[_flat(peer)], o_ref.at[_flat(peer)], send_sems.at[k - 1],
                                         recv_sems.at[k - 1], device_id=peer, device_id_type=MESH_ID).wait_recv()
        for cp in sends:
            cp.wait_send()
        o_ref[...] = o_ref[...] + b_ref[...]

    return pl.pallas_call(
        body, name="ada_modulation",
        out_shape=jax.ShapeDtypeStruct((N_DEV, 1, ncol), F32),
        in_specs=[pl.BlockSpec(memory_space=pltpu.VMEM)] * 3,
        out_specs=pl.BlockSpec(memory_space=pltpu.VMEM),
        scratch_shapes=[pltpu.VMEM((N_DEV, 1, ncol), F32), pltpu.SemaphoreType.DMA((N_DEV - 1,)),
                        pltpu.SemaphoreType.DMA((N_DEV - 1,))],
        compiler_params=_cparams(),
    )(c_all, w_ada_s, b_ada_r)


def allgather_hbm(shard, name):
    def body(x_ref, out_ref, send_sems, recv_sems, local_sem):
        x, y, c = _my_pos()
        me, sibling = (x, y, c), (x, y, 1 - c)
        chips = [(1 - x, y), (x, 1 - y), (1 - x, 1 - y)]

        def slot(pos):
            return out_ref.at[_flat(pos)]

        def copy(k, block, to, src=None):
            return pltpu.make_async_remote_copy(slot(block) if src is None else src, slot(block), send_sems.at[k],
                                                recv_sems.at[k], device_id=to, device_id_type=MESH_ID)

        mine = pltpu.make_async_copy(x_ref, slot(me), local_sem)
        mine.start()
        first = [copy(0, me, sibling, src=x_ref)]
        first += [copy(1 + j, me, (*chip, c), src=x_ref) for j, chip in enumerate(chips)]
        for cp in first:
            cp.start()
        passed = [copy(4 + j, (*chip, c), sibling) for j, chip in enumerate(chips)]
        for j, chip in enumerate(chips):
            copy(1 + j, (*chip, c), me).wait_recv()
            passed[j].start()
        copy(0, sibling, me).wait_recv()
        for j, chip in enumerate(chips):
            copy(4 + j, (*chip, 1 - c), me).wait_recv()
        for cp in first + passed:
            cp.wait_send()
        mine.wait()

    return pl.pallas_call(
        body, name=name,
        out_shape=jax.ShapeDtypeStruct((N_DEV,) + shard.shape, shard.dtype),
        in_specs=[pl.BlockSpec(memory_space=pl.ANY)],
        out_specs=pl.BlockSpec(memory_space=pl.ANY),
        scratch_shapes=[pltpu.SemaphoreType.DMA((N_DEV - 1,)), pltpu.SemaphoreType.DMA((N_DEV - 1,)),
                        pltpu.SemaphoreType.DMA],
        compiler_params=_cparams(),
    )(shard)


N_CHIP = N_DEV // 2
SIBLING_SEMS = [pltpu.SemaphoreType.DMA, pltpu.SemaphoreType.DMA]
CHIP_SEMS = [pltpu.SemaphoreType.DMA((N_CHIP - 1,)), pltpu.SemaphoreType.DMA((N_CHIP - 1,)), pltpu.SemaphoreType.DMA]


def _sibling_exchange(s_ref, o_ref, send_sem, recv_sem):
    x, y, c = _my_pos()
    cp = pltpu.make_async_remote_copy(s_ref, o_ref, send_sem, recv_sem, device_id=(x, y, 1 - c), device_id_type=MESH_ID)
    return cp.start, cp.wait


def _chip_exchange(p_ref, o_ref, send_sems, recv_sems, local_sem):
    x, y, c = _my_pos()
    my_chip = 2 * x + y
    mine = pltpu.make_async_copy(p_ref.at[my_chip], o_ref.at[my_chip], local_sem)
    peers = [(x ^ (k >> 1), y ^ (k & 1)) for k in range(1, N_CHIP)]
    sends = [pltpu.make_async_remote_copy(p_ref.at[2 * px + py], o_ref.at[my_chip], send_sems.at[k], recv_sems.at[k],
                                          device_id=(px, py, c), device_id_type=MESH_ID)
             for k, (px, py) in enumerate(peers)]
    recvs = [pltpu.make_async_remote_copy(p_ref.at[2 * px + py], o_ref.at[2 * px + py], send_sems.at[k], recv_sems.at[k],
                                          device_id=(px, py, c), device_id_type=MESH_ID)
             for k, (px, py) in enumerate(peers)]

    def start():
        mine.start()
        for cp in sends:
            cp.start()

    def wait():
        for cp in recvs:
            cp.wait_recv()
        for cp in sends:
            cp.wait_send()
        mine.wait()

    return start, wait


def exchange_sibling(send, name):
    def body(s_ref, o_ref, send_sem, recv_sem):
        start, wait = _sibling_exchange(s_ref, o_ref, send_sem, recv_sem)
        start()
        wait()

    return pl.pallas_call(
        body, name=name,
        out_shape=jax.ShapeDtypeStruct(send.shape, send.dtype),
        in_specs=[pl.BlockSpec(memory_space=pl.ANY)],
        out_specs=pl.BlockSpec(memory_space=pl.ANY),
        scratch_shapes=SIBLING_SEMS,
        compiler_params=_cparams(),
    )(send)


LANES = 128


def _k_tile(kdim, unit=LANES):
    for cand in range(MM_K_TILE - MM_K_TILE % unit, 0, -unit):
        if kdim % cand == 0:
            return cand
    return kdim


def _lane_tile(n, cap):
    for cand in range(cap - cap % LANES, 0, -LANES):
        if n % cand == 0:
            return cand
    return n


def mm_nn(a, b, out_dtype, name):
    m, kdim = a.shape
    n = b.shape[1]
    tm, tn, tk = min(MM_ROW_TILE, m), _lane_tile(n, MM_COL_TILE), _k_tile(kdim)
    nk = kdim // tk

    def body(a_ref, b_ref, o_ref, acc_ref):
        p = _nn(a_ref[...], b_ref[...])
        if nk == 1:
            o_ref[...] = p.astype(o_ref.dtype)
        else:
            k = pl.program_id(2)

            @pl.when(k == 0)
            def _():
                acc_ref[...] = p

            @pl.when(k > 0)
            def _():
                acc_ref[...] += p

            @pl.when(k == nk - 1)
            def _():
                o_ref[...] = acc_ref[...].astype(o_ref.dtype)

    return pl.pallas_call(
        body, name=name, grid=(n // tn, m // tm, nk),
        out_shape=jax.ShapeDtypeStruct((m, n), out_dtype),
        in_specs=[pl.BlockSpec((tm, tk), lambda j, i, k: (i, k)), pl.BlockSpec((tk, tn), lambda j, i, k: (k, j))],
        out_specs=pl.BlockSpec((tm, tn), lambda j, i, k: (i, j)),
        scratch_shapes=[pltpu.VMEM((tm, tn), F32)],
        compiler_params=_cparams(("parallel", "parallel", "arbitrary")),
    )(a, b)


def mm_nt(a, b, out_dtype, name):
    m, kdim = a.shape
    n = b.shape[0]
    tm, tn, tk = min(MM_ROW_TILE, m), _lane_tile(n, MM_COL_TILE), _k_tile(kdim)
    nk = kdim // tk

    def body(a_ref, b_ref, o_ref, acc_ref):
        p = _nt(a_ref[...], b_ref[...])
        if nk == 1:
            o_ref[...] = p.astype(o_ref.dtype)
        else:
            k = pl.program_id(2)

            @pl.when(k == 0)
            def _():
                acc_ref[...] = p

            @pl.when(k > 0)
            def _():
                acc_ref[...] += p

            @pl.when(k == nk - 1)
            def _():
                o_ref[...] = acc_ref[...].astype(o_ref.dtype)

    return pl.pallas_call(
        body, name=name, grid=(n // tn, m // tm, nk),
        out_shape=jax.ShapeDtypeStruct((m, n), out_dtype),
        in_specs=[pl.BlockSpec((tm, tk), lambda j, i, k: (i, k)), pl.BlockSpec((tn, tk), lambda j, i, k: (j, k))],
        out_specs=pl.BlockSpec((tm, tn), lambda j, i, k: (i, j)),
        scratch_shapes=[pltpu.VMEM((tm, tn), F32)],
        compiler_params=_cparams(("parallel", "parallel", "arbitrary")),
    )(a, b)


def mm_nn_exchange(a, b, out_dtype, part, name):
    kblocks, m, kb = a.shape
    kdim = kblocks * kb
    n = b.shape[1]
    tm, tn, tk = min(MM_ROW_TILE, m), _lane_tile(n, MM_COL_TILE), _k_tile(kdim)
    gn, gm, nk = n // tn, m // tm, kdim // tk
    per_step = tk // kb

    def body(a_ref, b_ref, part_ref, o_ref, parts_ref, acc_ref, send_sems, recv_sems, local_sem):
        j, i, k = pl.program_id(0), pl.program_id(1), pl.program_id(2)
        xchg_start, xchg_wait = _chip_exchange(part_ref, parts_ref, send_sems, recv_sems, local_sem)

        @pl.when((j == 0) & (i == 0) & (k == 0))
        def _():
            xchg_start()

        p = _nn(a_ref[0], b_ref[0:kb, :])
        for c in range(1, per_step):
            p = p + _nn(a_ref[c], b_ref[c * kb:(c + 1) * kb, :])

        @pl.when(k == 0)
        def _():
            acc_ref[...] = p

        @pl.when(k > 0)
        def _():
            acc_ref[...] += p

        @pl.when(k == nk - 1)
        def _():
            o_ref[...] = acc_ref[...].astype(o_ref.dtype)

        @pl.when((j == gn - 1) & (i == gm - 1) & (k == nk - 1))
        def _():
            xchg_wait()

    hbm = pl.BlockSpec(memory_space=pl.ANY)
    return pl.pallas_call(
        body, name=name, grid=(gn, gm, nk),
        out_shape=[jax.ShapeDtypeStruct((m, n), out_dtype), jax.ShapeDtypeStruct(part.shape, part.dtype)],
        in_specs=[pl.BlockSpec((per_step, tm, kb), lambda j, i, k: (k, i, 0)),
                  pl.BlockSpec((tk, tn), lambda j, i, k: (k, j)), hbm],
        out_specs=[pl.BlockSpec((tm, tn), lambda j, i, k: (i, j)), hbm],
        scratch_shapes=[pltpu.VMEM((tm, tn), F32)] + CHIP_SEMS,
        compiler_params=_cparams(("arbitrary", "arbitrary", "arbitrary")),
    )(a, b, part)


def mm_nt_gather(a, b, out_dtype, shard, name):
    m, kdim = a.shape
    n = b.shape[0]
    tm, tn = min(MM_ROW_TILE, m), 1024
    assert kdim == 1024
    gj = m // tm
    nsteps = (n // tn) * gj
    forward_step = max(nsteps - 2, 0)

    def body(a_ref, b_ref, x_ref, o_ref, g_ref, send_sems, recv_sems, local_sem):
        step = pl.program_id(0) * gj + pl.program_id(1)
        x, y, c = _my_pos()
        me, sibling = (x, y, c), (x, y, 1 - c)
        chips = [(1 - x, y), (x, 1 - y), (1 - x, 1 - y)]

        def slot(pos):
            return g_ref.at[_flat(pos)]

        def copy(k, block, to, src=None):
            return pltpu.make_async_remote_copy(slot(block) if src is None else src, slot(block), send_sems.at[k],
                                                recv_sems.at[k], device_id=to, device_id_type=MESH_ID)

        mine = pltpu.make_async_copy(x_ref, slot(me), local_sem)
        first = [copy(0, me, sibling, src=x_ref)]
        first += [copy(1 + j, me, (*chip, c), src=x_ref) for j, chip in enumerate(chips)]
        passed = [copy(4 + j, (*chip, c), sibling) for j, chip in enumerate(chips)]

        @pl.when(step == 0)
        def _():
            mine.start()
            for cp in first:
                cp.start()

        rows = pl.ds(pl.multiple_of(pl.program_id(1) * tm, tm), tm)
        o_ref[...] = _nt(a_ref[rows, :], b_ref[...]).astype(o_ref.dtype)

        @pl.when(step == forward_step)
        def _():
            for j, chip in enumerate(chips):
                copy(1 + j, (*chip, c), me).wait_recv()
                passed[j].start()

        @pl.when(step == nsteps - 1)
        def _():
            copy(0, sibling, me).wait_recv()
            for j, chip in enumerate(chips):
                copy(4 + j, (*chip, 1 - c), me).wait_recv()
            for cp in first + passed:
                cp.wait_send()
            mine.wait()

    return pl.pallas_call(
        body, name=name, grid=(n // tn, gj),
        out_shape=[jax.ShapeDtypeStruct((m, n), out_dtype), jax.ShapeDtypeStruct((N_DEV,) + shard.shape, shard.dtype)],
        in_specs=[pl.BlockSpec(memory_space=pltpu.VMEM), pl.BlockSpec((tn, kdim), lambda j, i: (j, 0)),
                  pl.BlockSpec(memory_space=pl.ANY)],
        out_specs=[pl.BlockSpec((tm, tn), lambda j, i: (i, j)), pl.BlockSpec(memory_space=pl.ANY)],
        scratch_shapes=[pltpu.SemaphoreType.DMA((N_DEV - 1,)), pltpu.SemaphoreType.DMA((N_DEV - 1,)),
                        pltpu.SemaphoreType.DMA],
        compiler_params=_cparams(("arbitrary", "arbitrary")),
    )(a, b, shard)


def mm_tn(a, b, name):
    tt, tn = min(MM_TOKEN_TILE, b.shape[0]), _lane_tile(b.shape[1], MM_COL_TILE)
    tka = _lane_tile(a.shape[0] * a.shape[2] if a.ndim == 3 else a.shape[1], 1024)
    if a.ndim == 3:
        t, ka = a.shape[1], a.shape[0] * a.shape[2]
        a_spec = pl.BlockSpec((None, tt, tka), lambda i, j, s: (i, s, 0))
    else:
        t, ka = a.shape
        a_spec = pl.BlockSpec((tt, tka), lambda i, j, s: (s, i))
    n = b.shape[1]
    nt = t // tt

    def body(a_ref, b_ref, o_ref, *acc):
        p = _tn(a_ref[...], b_ref[...])
        if nt == 1:
            o_ref[...] = p.astype(o_ref.dtype)
        else:
            acc_ref, s = acc[0], pl.program_id(2)

            @pl.when(s == 0)
            def _():
                acc_ref[...] = p

            @pl.when(s > 0)
            def _():
                acc_ref[...] += p

            @pl.when(s == nt - 1)
            def _():
                o_ref[...] = acc_ref[...].astype(o_ref.dtype)

    return pl.pallas_call(
        body, name=name, grid=(ka // tka, n // tn, nt),
        out_shape=jax.ShapeDtypeStruct((ka, n), BF16),
        in_specs=[a_spec, pl.BlockSpec((tt, tn), lambda i, j, s: (s, j))],
        out_specs=pl.BlockSpec((tka, tn), lambda i, j, s: (i, j)),
        scratch_shapes=[] if nt == 1 else [pltpu.VMEM((tka, tn), F32)],
        compiler_params=_cparams(("parallel", "parallel", "arbitrary")),
    )(a, b)


def _tile(t, cap):
    return min(cap, t)


def ln_modulate(x, mod6, shift_row, scale_row, name):
    t = x.shape[0]
    tm = _tile(t, ROW_TILE)

    def body(x_ref, mod_ref, o_ref):
        xh, _ = _ln(x_ref[...])
        sc = mod_ref[scale_row:scale_row + 1, :]
        sh = mod_ref[shift_row:shift_row + 1, :]
        o_ref[...] = (xh * (1.0 + sc) + sh).astype(BF16)

    return pl.pallas_call(
        body, name=name, grid=(t // tm,),
        out_shape=jax.ShapeDtypeStruct((t, D), BF16),
        in_specs=[pl.BlockSpec((tm, D), lambda i: (i, 0)), pl.BlockSpec((6, D), lambda i: (0, 0))],
        out_specs=pl.BlockSpec((tm, D), lambda i: (i, 0)),
        compiler_params=_cparams(("parallel",)),
    )(x, mod6)


def resid_ln(x, h, mod6, gate_row, ln_g, ln_b, name):
    t = x.shape[0]
    tm = _tile(t, ROW_TILE)

    def body(x_ref, h_ref, mod_ref, g_ref, b_ref, o_ref):
        r = ALPHA * x_ref[...] + mod_ref[gate_row:gate_row + 1, :] * h_ref[...]
        rh, _ = _ln(r)
        o_ref[...] = rh * g_ref[...] + b_ref[...]

    row = pl.BlockSpec((tm, D), lambda i: (i, 0))
    vec = pl.BlockSpec((1, D), lambda i: (0, 0))
    return pl.pallas_call(
        body, name=name, grid=(t // tm,),
        out_shape=jax.ShapeDtypeStruct((t, D), F32),
        in_specs=[row, row, pl.BlockSpec((6, D), lambda i: (0, 0)), vec, vec],
        out_specs=row,
        compiler_params=_cparams(("parallel",)),
    )(x, h, mod6, ln_g, ln_b)


def resid_ln_bwd(x, h, mod6, gate_row, ln_g, ln_b, cot, with_loss, name):
    t = x.shape[0]
    tm = _tile(t, ROW_TILE)

    def body(x_ref, h_ref, mod_ref, g_ref, b_ref, c_ref, dh_ref, dx_ref, acc_ref):
        @pl.when(pl.program_id(0) == 0)
        def _():
            acc_ref[...] = jnp.zeros_like(acc_ref)

        gate = mod_ref[gate_row:gate_row + 1, :]
        hv = h_ref[...]
        r = ALPHA * x_ref[...] + gate * hv
        rh, rstd = _ln(r)
        lng = g_ref[...]
        if with_loss:
            diff = rh * lng + b_ref[...] - c_ref[...]
            dxo = diff * (1.0 / D)
            lsum = jnp.sum(_colsum(diff * diff), axis=-1, keepdims=True) * (0.5 / D)
            acc_ref[3:4, :] += jnp.broadcast_to(lsum, (1, D))
        else:
            dxo = c_ref[...]
        acc_ref[1:2, :] += _colsum(dxo * rh)
        acc_ref[2:3, :] += _colsum(dxo)
        dr = _ln_bwd(dxo * lng, rh, rstd)
        acc_ref[0:1, :] += _colsum(dr * hv)
        dh_ref[...] = (gate * dr).astype(BF16)
        dx_ref[...] = ALPHA * dr

    row = pl.BlockSpec((tm, D), lambda i: (i, 0))
    vec = pl.BlockSpec((1, D), lambda i: (0, 0))
    return pl.pallas_call(
        body, name=name, grid=(t // tm,),
        out_shape=[jax.ShapeDtypeStruct((t, D), BF16), jax.ShapeDtypeStruct((t, D), F32),
                   jax.ShapeDtypeStruct((8, D), F32)],
        in_specs=[row, row, pl.BlockSpec((6, D), lambda i: (0, 0)), vec, vec, row],
        out_specs=[row, row, pl.BlockSpec((8, D), lambda i: (0, 0))],
        compiler_params=_cparams(("arbitrary",)),
    )(x, h, mod6, ln_g, ln_b, cot)


def ln_modulate_bwd(x, du, mod6, scale_row, dx_part, name):
    t = x.shape[0]
    tm = _tile(t, ROW_TILE)

    def body(x_ref, du_ref, mod_ref, dp_ref, dx_ref, acc_ref):
        @pl.when(pl.program_id(0) == 0)
        def _():
            acc_ref[...] = jnp.zeros_like(acc_ref)

        xh, rstd = _ln(x_ref[...])
        du_v = du_ref[...]
        sc = mod_ref[scale_row:scale_row + 1, :]
        acc_ref[0:1, :] += _colsum(du_v * xh)
        acc_ref[1:2, :] += _colsum(du_v)
        dx_ref[...] = dp_ref[...] + _ln_bwd(du_v * (1.0 + sc), xh, rstd)

    row = pl.BlockSpec((tm, D), lambda i: (i, 0))
    return pl.pallas_call(
        body, name=name, grid=(t // tm,),
        out_shape=[jax.ShapeDtypeStruct((t, D), F32), jax.ShapeDtypeStruct((8, D), F32)],
        in_specs=[row, row, pl.BlockSpec((6, D), lambda i: (0, 0)), row],
        out_specs=[row, pl.BlockSpec((8, D), lambda i: (0, 0))],
        compiler_params=_cparams(("arbitrary",)),
    )(x, du, mod6, dx_part)


def merge_gates(ya, yb, proj):
    t = ya.shape[0]
    tm = _tile(t, ROW_TILE)

    def body(ya_ref, yb_ref, ga_ref, gb_ref, o_ref):
        o_ref[...] = (_sigmoid(ga_ref[...]) * ya_ref[...].astype(F32) +
                      _sigmoid(gb_ref[...]) * yb_ref[...].astype(F32)).astype(BF16)

    row = pl.BlockSpec((tm, D), lambda i: (i, 0))
    return pl.pallas_call(
        body, name="merge_gates", grid=(t // tm,),
        out_shape=jax.ShapeDtypeStruct((t, D), BF16),
        in_specs=[row, row, pl.BlockSpec((tm, D), lambda i: (i, GATE_BLOCK0)),
                  pl.BlockSpec((tm, D), lambda i: (i, GATE_BLOCK0 + 1))],
        out_specs=row,
        compiler_params=_cparams(("parallel",)),
    )(ya, yb, proj, proj)


def merge_gates_bwd(dm, ya, yb, proj):
    t = ya.shape[0]
    tm = _tile(t, ROW_TILE)

    def body(dm_ref, ya_ref, yb_ref, ga_ref, gb_ref, dya_ref, dyb_ref, dp_ref):
        dmv = dm_ref[...].astype(F32)
        sa = _sigmoid(ga_ref[...])
        sb = _sigmoid(gb_ref[...])
        dya_ref[...] = (dmv * sa).astype(BF16)
        dyb_ref[...] = (dmv * sb).astype(BF16)
        dp_ref[0] = (dmv * ya_ref[...].astype(F32) * sa * (1.0 - sa)).astype(BF16)
        dp_ref[1] = (dmv * yb_ref[...].astype(F32) * sb * (1.0 - sb)).astype(BF16)

    row = pl.BlockSpec((tm, D), lambda i: (i, 0))
    return pl.pallas_call(
        body, name="merge_gates_bwd", grid=(t // tm,),
        out_shape=[jax.ShapeDtypeStruct((t, D), BF16)] * 2 + [jax.ShapeDtypeStruct((N_PROJ // D, t, D), BF16)],
        in_specs=[row, row, row, pl.BlockSpec((tm, D), lambda i: (i, GATE_BLOCK0)),
                  pl.BlockSpec((tm, D), lambda i: (i, GATE_BLOCK0 + 1))],
        out_specs=[row, row, pl.BlockSpec((2, tm, D), lambda i: (GATE_BLOCK0 // 2, i, 0))],
        compiler_params=_cparams(("parallel",)),
    )(dm, ya, yb, proj, proj)


FF_CHUNK = 1408


def swiglu_act(gu):
    t = gu.shape[0]
    tm = _tile(t, FFN_ROW_TILE)

    def body(gu_ref, o_ref):
        for j in range(D_FF // FF_CHUNK):
            cs = slice(j * FF_CHUNK, (j + 1) * FF_CHUNK)
            g = gu_ref[:, cs].astype(F32)
            u = gu_ref[:, D_FF + j * FF_CHUNK:D_FF + (j + 1) * FF_CHUNK].astype(F32)
            o_ref[:, cs] = (g * _sigmoid(g) * u).astype(BF16)

    return pl.pallas_call(
        body, name="swiglu_act", grid=(t // tm,),
        out_shape=jax.ShapeDtypeStruct((t, D_FF), BF16),
        in_specs=[pl.BlockSpec((tm, 2 * D_FF), lambda i: (i, 0))],
        out_specs=pl.BlockSpec((tm, D_FF), lambda i: (i, 0)),
        compiler_params=_cparams(("parallel",)),
    )(gu)


def swiglu_act_bwd(gu, dact):
    t = gu.shape[0]
    tm = _tile(t, FFN_ROW_TILE)

    def body(gu_ref, da_ref, o_ref):
        for j in range(D_FF // FF_CHUNK):
            cs = slice(j * FF_CHUNK, (j + 1) * FF_CHUNK)
            us = slice(D_FF + j * FF_CHUNK, D_FF + (j + 1) * FF_CHUNK)
            g = gu_ref[:, cs].astype(F32)
            u = gu_ref[:, us].astype(F32)
            da = da_ref[:, cs].astype(F32)
            s = _sigmoid(g)
            o_ref[:, cs] = (da * u * _dsilu(g, s)).astype(BF16)
            o_ref[:, us] = (da * g * s).astype(BF16)

    return pl.pallas_call(
        body, name="swiglu_act_bwd", grid=(t // tm,),
        out_shape=jax.ShapeDtypeStruct((t, 2 * D_FF), BF16),
        in_specs=[pl.BlockSpec((tm, 2 * D_FF), lambda i: (i, 0)), pl.BlockSpec((tm, D_FF), lambda i: (i, 0))],
        out_specs=pl.BlockSpec((tm, 2 * D_FF), lambda i: (i, 0)),
        compiler_params=_cparams(("parallel",)),
    )(gu, dact)


def _hgrn_chunk_terms(q, fl, lbv, tril_f):
    sig = _sigmoid(fl)
    f = lbv + (1.0 - lbv) * sig
    lam = jnp.log(f)
    k = 1.0 - f
    sq = _sigmoid(q)
    qt = q * sq * Q_SCALE
    bc = _sel(_nn, lam, tril_f, 3, x_first=False)
    bmid = bc[CHUNK // 2 - 1:CHUNK // 2, :]
    bl = bc[CHUNK - 1:CHUNK, :]
    eq = jnp.exp(jnp.minimum(bc - bmid, EXP_CLIP))
    ek = jnp.exp(jnp.minimum(bmid - bc, EXP_CLIP))
    eb = jnp.exp(bc)
    ekl = jnp.exp(bl - bc)
    ebl = jnp.exp(bl)
    return sig, f, k, sq, qt, eq, ek, eb, ekl, ebl


def hgrn_fwd(proj, lb, gnorm):
    t = proj.shape[0]
    tb = _tile(t, TOKEN_BLOCK)
    ncb = tb // CHUNK

    hps = HGRN_HEADS_PER_STEP
    wide = hps * HK

    def body(q_ref, f_ref, i_ref, g_ref, lb_ref, gn_ref, oa_ref, oraw_ref, st_ref, state):
        @pl.when(pl.program_id(1) == 0)
        def _():
            state[...] = jnp.zeros_like(state)

        gn = gn_ref[...]
        mask = _tri(CHUNK)
        tril_f = mask.astype(BF16)

        def chunk(c, carry):
            sl = pl.ds(pl.multiple_of(c * CHUNK, CHUNK), CHUNK)
            for hh in range(hps):
                ln = slice(hh * HK, (hh + 1) * HK)
                q, fl, v, g = q_ref[sl, ln], f_ref[sl, ln], i_ref[sl, ln], g_ref[sl, ln]
                sig, f, k, sq, qt, eq, ek, eb, ekl, ebl = _hgrn_chunk_terms(q, fl, lb_ref[:, ln], tril_f)
                a = jnp.where(mask, _nt((qt * eq).astype(BF16), (k * ek).astype(BF16)), 0.0)
                st = state[hh]
                st_ref[hh, c] = st
                vb = v.astype(BF16)
                o = _nn(a.astype(BF16), vb) + _nt((qt * eb).astype(BF16), st.astype(BF16))
                state[hh] = st * ebl + _tn(vb, (k * ekl).astype(BF16))
                oraw_ref[sl, ln] = o
                rn = o * lax.rsqrt(jnp.mean(o * o, axis=-1, keepdims=True) + RMS_EPS)
                oa_ref[sl, ln] = (rn * gn * g * _sigmoid(g)).astype(BF16)
            return carry

        lax.fori_loop(0, ncb, chunk, 0, unroll=min(CHUNK_UNROLL, ncb))

    def col(block):
        return pl.BlockSpec((tb, wide), lambda h, j: (j, block * (N_HEADS_A // hps) + h))

    return pl.pallas_call(
        body, name="hgrn_fwd", grid=(N_HEADS_A // hps, t // tb),
        out_shape=[jax.ShapeDtypeStruct((t, D), BF16), jax.ShapeDtypeStruct((t, D), F32),
                   jax.ShapeDtypeStruct((N_HEADS_A, t // CHUNK, HK, HK), F32)],
        in_specs=[col(0), col(1), col(2), col(3), pl.BlockSpec((1, wide), lambda h, j: (0, h)),
                  pl.BlockSpec((1, HK), lambda h, j: (0, 0))],
        out_specs=[pl.BlockSpec((tb, wide), lambda h, j: (j, h)), pl.BlockSpec((tb, wide), lambda h, j: (j, h)),
                   pl.BlockSpec((hps, ncb, HK, HK), lambda h, j: (h, j, 0, 0))],
        scratch_shapes=[pltpu.VMEM((hps, HK, HK), F32)],
        compiler_params=_cparams(("parallel", "arbitrary")),
    )(proj, proj, proj, proj, lb, gnorm)


def hgrn_bwd(proj, lb, gnorm, o_raw, doa, states, give, dproj):
    t = proj.shape[0]
    tb = _tile(t, TOKEN_BLOCK)
    ncb = tb // CHUNK
    nb = t // tb
    hps = HGRN_HEADS_PER_STEP
    wide = hps * HK

    def body(q_ref, f_ref, i_ref, g_ref, lb_ref, gn_ref, oraw_ref, doa_ref, st_ref, give_ref, dp_in_ref,
             dp_ref, dlb_ref, dgn_ref, got_ref, dstate, send_sem, recv_sem):
        h, j = pl.program_id(0), pl.program_id(1)
        swap_start, swap_wait = _sibling_exchange(give_ref, got_ref, send_sem, recv_sem)

        @pl.when((h == 0) & (j == 0))
        def _():
            swap_start()

        @pl.when(j == 0)
        def _():
            dstate[...] = jnp.zeros_like(dstate)
            dlb_ref[...] = jnp.zeros_like(dlb_ref)

        @pl.when((j == 0) & (h == 0))
        def _():
            dgn_ref[...] = jnp.zeros_like(dgn_ref)

        gn = gn_ref[...]
        mask = _tri(CHUNK)
        mask_t = _tri(CHUNK, upper=True)
        tril_f = mask.astype(BF16)
        triu_f = mask_t.astype(BF16)

        def chunk(i, c0):
            c = ncb - 1 - i
            sl = pl.ds(pl.multiple_of(c * CHUNK, CHUNK), CHUNK)
            for hh in range(hps):
                ln = slice(hh * HK, (hh + 1) * HK)
                q, fl, v, g = q_ref[sl, ln], f_ref[sl, ln], i_ref[sl, ln], g_ref[sl, ln]
                lbv = lb_ref[:, ln]
                sig, f, k, sq, qt, eq, ek, eb, ekl, ebl = _hgrn_chunk_terms(q, fl, lbv, tril_f)
                qe = (qt * eq).astype(BF16)
                ke = (k * ek).astype(BF16)
                st32 = st_ref[hh, c]
                st = st32.astype(BF16)
                dst = dstate[hh]
                dstb = dst.astype(BF16)
                o = oraw_ref[sl, ln]
                rstd = lax.rsqrt(jnp.mean(o * o, axis=-1, keepdims=True) + RMS_EPS)
                rn = o * rstd
                sgm = _sigmoid(g)
                sg = g * sgm
                doa_v = doa_ref[sl, ln]
                drn = doa_v * gn * sg
                dgn_ref[...] += _colsum(doa_v * rn * sg)
                dp_ref[3, sl, ln] = (doa_v * rn * gn * _dsilu(g, sgm)).astype(BF16)
                do = rstd * (drn - rn * jnp.mean(drn * rn, axis=-1, keepdims=True))
                dob = do.astype(BF16)
                vb = v.astype(BF16)
                da = jnp.where(mask, _nt(dob, vb), 0.0).astype(BF16)
                da_t = jnp.where(mask_t, _nt(vb, dob), 0.0).astype(BF16)
                a_t = jnp.where(mask_t, _nt(ke, qe), 0.0).astype(BF16)
                kl = (k * ekl).astype(BF16)
                qb = (qt * eb).astype(BF16)
                dq_in = _nn(da, ke)
                dk_in = _nn(da_t, qe)
                dq_out = eb * _nn(dob, st)
                dk_out = ekl * _nn(vb, dstb)
                dqt = eq * dq_in + dq_out
                dk = ek * dk_in + dk_out
                dv = _nn(a_t, dob) + _nt(kl, dstb)
                dstate[hh] = dst * ebl + _tn(dob, qb)
                dbig = qe.astype(F32) * dq_in - ke.astype(F32) * dk_in + qt * dq_out - k * dk_out
                beyond = _colsum(k * dk_out) + ebl * _colsum(dst * st32)
                dlam = _sel(_nn, dbig, triu_f, 3, x_first=False) + beyond
                df = dlam / f - dk
                dp_ref[1, sl, ln] = (df * (1.0 - lbv) * sig * (1.0 - sig)).astype(BF16)
                dlb_ref[:, ln] += _colsum(df * (1.0 - sig))
                dp_ref[0, sl, ln] = (dqt * Q_SCALE * _dsilu(q, sq)).astype(BF16)
                dp_ref[2, sl, ln] = dv.astype(BF16)
            return c0

        lax.fori_loop(0, ncb, chunk, 0, unroll=min(CHUNK_UNROLL, ncb))

        @pl.when((h == N_HEADS_A // hps - 1) & (j == nb - 1))
        def _():
            swap_wait()

    def col(block):
        return pl.BlockSpec((tb, wide), lambda h, j: (nb - 1 - j, block * (N_HEADS_A // hps) + h))

    hcol = pl.BlockSpec((tb, wide), lambda h, j: (nb - 1 - j, h))
    hbm = pl.BlockSpec(memory_space=pl.ANY)
    return pl.pallas_call(
        body, name="hgrn_bwd", grid=(N_HEADS_A // hps, nb),
        out_shape=[jax.ShapeDtypeStruct(dproj.shape, dproj.dtype), jax.ShapeDtypeStruct((1, D), F32),
                   jax.ShapeDtypeStruct((1, HK), F32), jax.ShapeDtypeStruct(give.shape, give.dtype)],
        in_specs=[col(0), col(1), col(2), col(3), pl.BlockSpec((1, wide), lambda h, j: (0, h)),
                  pl.BlockSpec((1, HK), lambda h, j: (0, 0)), hcol, hcol,
                  pl.BlockSpec((hps, ncb, HK, HK), lambda h, j: (h, nb - 1 - j, 0, 0)), hbm, hbm],
        out_specs=[pl.BlockSpec((4, tb, wide), lambda h, j: (0, nb - 1 - j, h)),
                   pl.BlockSpec((1, wide), lambda h, j: (0, h)), pl.BlockSpec((1, HK), lambda h, j: (0, 0)), hbm],
        input_output_aliases={10: 0},
        scratch_shapes=[pltpu.VMEM((hps, HK, HK), F32)] + SIBLING_SEMS,
        compiler_params=_cparams(("arbitrary", "arbitrary")),
    )(proj, proj, proj, proj, lb, gnorm, o_raw, doa, states, give, dproj)


CONV_BLOCK0 = 6
CONV_TAPS = 4
HALO = 8


def conv_fwd(proj, conv_w, conv_b):
    t = proj.shape[0]
    tm = _tile(t, ROW_TILE)
    r = tm // HALO

    def body(x_ref, halo_ref, w_ref, b_ref, o_ref, ds_ref):
        i = pl.program_id(1)
        halo = jnp.where(i > 0, halo_ref[...], 0.0)
        ext = jnp.concatenate([halo, x_ref[...]], axis=0)
        pre = b_ref[...] + w_ref[CONV_TAPS - 1:CONV_TAPS, :] * ext[HALO:, :]
        for tap in range(CONV_TAPS - 1):
            pre = pre + w_ref[tap:tap + 1, :] * pltpu.roll(ext, CONV_TAPS - 1 - tap, axis=0)[HALO:, :]
        s = _sigmoid(pre)
        o_ref[...] = pre * s
        ds_ref[...] = _dsilu(pre, s).astype(BF16)

    blk = pl.BlockSpec((tm, D), lambda cb, i: (i, cb))
    return pl.pallas_call(
        body, name="conv_fwd", grid=(CONV_DIM // D, t // tm),
        out_shape=[jax.ShapeDtypeStruct((t, CONV_DIM), F32), jax.ShapeDtypeStruct((t, CONV_DIM), BF16)],
        in_specs=[pl.BlockSpec((tm, D), lambda cb, i: (i, CONV_BLOCK0 + cb)),
                  pl.BlockSpec((HALO, D), lambda cb, i: (jnp.maximum(i * r - 1, 0), CONV_BLOCK0 + cb)),
                  pl.BlockSpec((CONV_TAPS, D), lambda cb, i: (0, cb)), pl.BlockSpec((1, D), lambda cb, i: (0, cb))],
        out_specs=[blk, blk],
        compiler_params=_cparams(("parallel", "parallel")),
    )(proj, proj, conv_w, conv_b)


def conv_bwd(proj, dxc, dsilu, conv_w, dproj):
    t = proj.shape[0]
    tm = _tile(t, ROW_TILE)
    r = tm // HALO
    n = t // tm
    last_halo = t // HALO - 1

    def body(x_ref, prev_ref, d_ref, dnext_ref, s_ref, snext_ref, w_ref, dp_in_ref, dx_ref, dw_ref, db_ref):
        i = pl.program_id(1)

        @pl.when(i == 0)
        def _():
            dw_ref[...] = jnp.zeros_like(dw_ref)
            db_ref[...] = jnp.zeros_like(db_ref)

        dpre = jnp.concatenate([d_ref[...].astype(F32) * s_ref[...].astype(F32),
                                jnp.where(i < n - 1, dnext_ref[0:HALO, :].astype(F32) * snext_ref[0:HALO, :].astype(F32),
                                          0.0)], axis=0)
        dx = w_ref[CONV_TAPS - 1:CONV_TAPS, :] * dpre[:tm, :]
        for tap in range(CONV_TAPS - 1):
            back = CONV_TAPS - 1 - tap
            dx = dx + w_ref[tap:tap + 1, :] * pltpu.roll(dpre, tm + HALO - back, axis=0)[:tm, :]
        dx_ref[...] = dx.astype(BF16)
        dp = dpre[:tm, :]
        db_ref[...] += _colsum(dp)
        prev = jnp.where(i > 0, prev_ref[...], 0.0)
        ext = jnp.concatenate([prev, x_ref[...]], axis=0)
        dw_ref[CONV_TAPS - 1:CONV_TAPS, :] += _colsum(dp * ext[HALO:, :])
        for tap in range(CONV_TAPS - 1):
            dw_ref[tap:tap + 1, :] += _colsum(dp * pltpu.roll(ext, CONV_TAPS - 1 - tap, axis=0)[HALO:, :])

    blk = pl.BlockSpec((tm, D), lambda cb, i: (i, cb))
    nxt = pl.BlockSpec((2 * HALO, D), lambda cb, i: (jnp.minimum((i + 1) * (r // 2), last_halo // 2), cb))
    return pl.pallas_call(
        body, name="conv_bwd", grid=(CONV_DIM // D, n),
        out_shape=[jax.ShapeDtypeStruct(dproj.shape, dproj.dtype), jax.ShapeDtypeStruct((8, CONV_DIM), F32),
                   jax.ShapeDtypeStruct((1, CONV_DIM), F32)],
        in_specs=[pl.BlockSpec((tm, D), lambda cb, i: (i, CONV_BLOCK0 + cb)),
                  pl.BlockSpec((HALO, D), lambda cb, i: (jnp.maximum(i * r - 1, 0), CONV_BLOCK0 + cb)),
                  blk, nxt, blk, nxt,
                  pl.BlockSpec((CONV_TAPS, D), lambda cb, i: (0, cb)), pl.BlockSpec(memory_space=pl.ANY)],
        out_specs=[pl.BlockSpec((None, tm, D), lambda cb, i: (CONV_BLOCK0 + cb, i, 0)),
                   pl.BlockSpec((8, D), lambda cb, i: (0, cb)), pl.BlockSpec((1, D), lambda cb, i: (0, cb))],
        input_output_aliases={7: 0},
        compiler_params=_cparams(("parallel", "arbitrary")),
    )(proj, proj, dxc, dxc, dsilu, dsilu, conv_w, dproj)


def dt_fill(ddt, dproj):
    t = ddt.shape[0]
    tm = _tile(t, ROW_TILE)
    w = ddt.shape[1]

    def body(d_ref, dp_in_ref, o_ref):
        o_ref[:, :w] = d_ref[...]
        o_ref[:, w:] = jnp.zeros((tm, D - w), o_ref.dtype)

    return pl.pallas_call(
        body, name="dt_fill", grid=(t // tm,),
        out_shape=jax.ShapeDtypeStruct(dproj.shape, dproj.dtype),
        in_specs=[pl.BlockSpec((tm, w), lambda i: (i, 0)), pl.BlockSpec(memory_space=pl.ANY)],
        out_specs=pl.BlockSpec((None, tm, D), lambda i: (DT_COL_BLOCK, i, 0)),
        input_output_aliases={1: 0},
        compiler_params=_cparams(("parallel",)),
    )(ddt, dproj)


Z_BLOCK0 = 8
DT_COL_BLOCK = 9
DT_BLOCK0 = 8 * DT_COL_BLOCK
GATE_BLOCK0 = 10
B_BLOCK0 = 16
C_BLOCK0 = 20


def _head_expand():
    e = np.zeros((N_STATE, GROUP_W), np.float32)
    for hh in range(HEADS_PER_GROUP):
        e[hh, hh * HEAD_P:(hh + 1) * HEAD_P] = 1.0
    return jnp.asarray(e, BF16)


def _ssd_chunk_terms(dt, bias, alog, expand, tril_f, eye):
    dtb = dt + bias
    delta = jnp.maximum(dtb, 0.0) + jnp.log(1.0 + jnp.exp(-jnp.abs(dtb)))
    ea = jnp.exp(alog)
    a = -ea * delta
    acum = _sel(_nn, a, tril_f, 3, x_first=False)
    delta_e = _sel(_nn, delta, expand, 2)
    acum_e = _sel(_nn, acum, expand, 2)
    acum_t = _sel(_nt, acum, eye, 3, x_first=False)
    return dtb, delta, ea, a, acum, delta_e, acum_e, acum_t


def ssd_fwd(proj, xc, alog4, bias4, dskip4, wnorm, expand):
    t = proj.shape[0]
    tb = _tile(t, TOKEN_BLOCK)
    ncb = tb // SSD_CHUNK

    def body(xs_ref, b_ref, c_ref, dt_ref, z_ref, alog_ref, bias_ref, dsk_ref, wn_ref, e_ref, ob_ref, st_ref, state):
        @pl.when(pl.program_id(1) == 0)
        def _():
            state[...] = jnp.zeros_like(state)

        expand = e_ref[...]
        mask = _tri(SSD_CHUNK)
        tril_f = mask.astype(BF16)
        eye = (lax.broadcasted_iota(jnp.int32, (N_STATE, N_STATE), 0) ==
               lax.broadcasted_iota(jnp.int32, (N_STATE, N_STATE), 1)).astype(BF16)
        alog, bias = alog_ref[0], bias_ref[0]
        d_e = _sel(_nn, jnp.broadcast_to(dsk_ref[0], (8, N_STATE)), expand, 3)[0:1, :]
        wn = wn_ref[...]

        def chunk(c, carry):
            sl = pl.ds(pl.multiple_of(c * SSD_CHUNK, SSD_CHUNK), SSD_CHUNK)
            xs, bm, cm, dt, z = xs_ref[sl, :], b_ref[sl, :], c_ref[sl, :], dt_ref[sl, :], z_ref[sl, :]
            dtb, delta, ea, a, acum, delta_e, acum_e, acum_t = _ssd_chunk_terms(dt, bias, alog, expand, tril_f, eye)
            alast_e = acum_e[SSD_CHUNK - 1:SSD_CHUNK, :]
            xd = xs * delta_e
            xdb = xd.astype(BF16)
            cb_, bb_ = cm.astype(BF16), bm.astype(BF16)
            cbm = _nt(cb_, bb_)
            ys = []
            for hh in range(HEADS_PER_GROUP):
                lh = jnp.where(mask, jnp.exp(jnp.minimum(acum[:, hh:hh + 1] - acum_t[hh:hh + 1, :], 0.0)), 0.0)
                ys.append(_nn((cbm * lh).astype(BF16), xdb[:, hh * HEAD_P:(hh + 1) * HEAD_P]))
            st = state[...]
            st_ref[0, c] = st
            y = jnp.concatenate(ys, axis=1) + _nn(cb_, st.astype(BF16)) * jnp.exp(acum_e) + xs * d_e
            state[...] = st * jnp.exp(alast_e) + _tn(bb_, (xd * jnp.exp(alast_e - acum_e)).astype(BF16))
            yg = y * z * _sigmoid(z)
            ob_ref[sl, :] = (yg * lax.rsqrt(jnp.mean(yg * yg, axis=-1, keepdims=True) + RMS_EPS) * wn).astype(BF16)
            return carry

        lax.fori_loop(0, ncb, chunk, 0, unroll=min(CHUNK_UNROLL, ncb))

    small = pl.BlockSpec((1, 1, N_STATE), lambda g, j: (g, 0, 0))
    return pl.pallas_call(
        body, name="ssd_fwd", grid=(N_GROUPS, t // tb),
        out_shape=[jax.ShapeDtypeStruct((t, B_INNER), BF16),
                   jax.ShapeDtypeStruct((N_GROUPS, t // SSD_CHUNK, N_STATE, GROUP_W), F32)],
        in_specs=[pl.BlockSpec((tb, GROUP_W), lambda g, j: (j, g)),
                  pl.BlockSpec((tb, N_STATE), lambda g, j: (j, B_BLOCK0 + g)),
                  pl.BlockSpec((tb, N_STATE), lambda g, j: (j, C_BLOCK0 + g)),
                  pl.BlockSpec((tb, N_STATE), lambda g, j: (j, DT_BLOCK0 + g)),
                  pl.BlockSpec((tb, GROUP_W), lambda g, j: (j, Z_BLOCK0 + g)),
                  small, small, small, pl.BlockSpec((1, GROUP_W), lambda g, j: (0, g)),
                  pl.BlockSpec((N_STATE, GROUP_W), lambda g, j: (0, 0))],
        out_specs=[pl.BlockSpec((tb, GROUP_W), lambda g, j: (j, g)),
                   pl.BlockSpec((1, ncb, N_STATE, GROUP_W), lambda g, j: (g, j, 0, 0))],
        scratch_shapes=[pltpu.VMEM((N_STATE, GROUP_W), F32)],
        compiler_params=_cparams(("parallel", "arbitrary")),
    )(xc, xc, xc, proj, proj, alog4, bias4, dskip4, wnorm, expand)


def ssd_bwd(proj, xc, alog4, bias4, dskip4, wnorm, expand, dob, states, part, dproj):
    t = proj.shape[0]
    tb = _tile(t, TOKEN_BLOCK)
    lc = min(SSD_CHUNK_BWD, tb)
    ncb = tb // lc
    nsaved = tb // SSD_CHUNK
    nb = t // tb

    def body(xs_ref, b_ref, c_ref, dt_ref, z_ref, alog_ref, bias_ref, dsk_ref, wn_ref, e_ref, dob_ref, st_ref, part_ref,
             dp_in_ref, dxs_ref, db_ref, dc_ref, dz_ref, ddt_ref, dwn_ref, dalog_ref, dbias_ref, ddsk_ref, parts_ref, dstate,
             send_sems, recv_sems, local_sem):
        xchg_start, xchg_wait = _chip_exchange(part_ref, parts_ref, send_sems, recv_sems, local_sem)

        @pl.when((pl.program_id(0) == 0) & (pl.program_id(1) == 0))
        def _():
            xchg_start()

        @pl.when(pl.program_id(1) == 0)
        def _():
            dstate[...] = jnp.zeros_like(dstate)
            dwn_ref[...] = jnp.zeros_like(dwn_ref)
            dalog_ref[...] = jnp.zeros_like(dalog_ref)
            dbias_ref[...] = jnp.zeros_like(dbias_ref)
            ddsk_ref[...] = jnp.zeros_like(ddsk_ref)

        expand = e_ref[...]
        mask = _tri(lc)
        mask_t = _tri(lc, upper=True)
        tril_f = mask.astype(BF16)
        triu_f = mask_t.astype(BF16)
        eye = (lax.broadcasted_iota(jnp.int32, (N_STATE, N_STATE), 0) ==
               lax.broadcasted_iota(jnp.int32, (N_STATE, N_STATE), 1)).astype(BF16)
        alog, bias = alog_ref[0], bias_ref[0]
        d_e = _sel(_nn, jnp.broadcast_to(dsk_ref[0], (8, N_STATE)), expand, 3)[0:1, :]
        wn = wn_ref[...]

        def chunk(i, c0):
            c = ncb - 1 - i
            sl = pl.ds(pl.multiple_of(c * lc, lc), lc)
            xs, bm, cm, dt, z = xs_ref[sl, :], b_ref[sl, :], c_ref[sl, :], dt_ref[sl, :], z_ref[sl, :]
            dtb, delta, ea, a, acum, delta_e, acum_e, acum_t = _ssd_chunk_terms(dt, bias, alog, expand, tril_f, eye)
            alast_e = acum_e[lc - 1:lc, :]
            eacum = jnp.exp(acum_e)
            wl = jnp.exp(alast_e - acum_e)
            xd = xs * delta_e
            xdb = xd.astype(BF16)
            cb_, bb_ = cm.astype(BF16), bm.astype(BF16)
            cbm = _nt(cb_, bb_)
            st32 = st_ref[0, c * (lc // SSD_CHUNK)]
            stb = st32.astype(BF16)
            dst = dstate[...]
            dstb = dst.astype(BF16)
            lhs, mixes, ys = [], [], []
            for hh in range(HEADS_PER_GROUP):
                col, row = acum[:, hh:hh + 1], acum_t[hh:hh + 1, :]
                lh = jnp.where(mask, jnp.exp(jnp.minimum(col - row, 0.0)), 0.0)
                mix = (cbm * lh).astype(BF16)
                lhs.append(lh)
                mixes.append(mix)
                ys.append(_nn(mix, xdb[:, hh * HEAD_P:(hh + 1) * HEAD_P]))
            y_in = jnp.concatenate(ys, axis=1)
            y_out = _nn(cb_, stb) * eacum
            y = y_in + y_out + xs * d_e
            sgz = _sigmoid(z)
            sz = z * sgz
            yg = y * sz
            rstd = lax.rsqrt(jnp.mean(yg * yg, axis=-1, keepdims=True) + RMS_EPS)
            nrm = yg * rstd
            dob_v = dob_ref[sl, :]
            dn = dob_v * wn
            dwn_ref[...] += _colsum(dob_v * nrm)
            dyg = rstd * (dn - nrm * jnp.mean(dn * nrm, axis=-1, keepdims=True))
            dy = dyg * sz
            dz_ref[sl, :] = (dyg * y * _dsilu(z, sgz)).astype(BF16)
            dyb = dy.astype(BF16)
            dxds = []
            dcb = jnp.zeros((lc, lc), F32)
            for hh in range(HEADS_PER_GROUP):
                hs = slice(hh * HEAD_P, (hh + 1) * HEAD_P)
                dy_h, x_h = dyb[:, hs], xdb[:, hs]
                dxds.append(_tn(mixes[hh], dy_h))
                dcb = dcb + _nt(dy_h, x_h) * lhs[hh]
            dcbb = dcb.astype(BF16)
            dye = (dy * eacum).astype(BF16)
            xw = (xd * wl).astype(BF16)
            dxd_in = jnp.concatenate(dxds, axis=1)
            dxd_out = wl * _nn(bb_, dstb)
            dxd = dxd_in + dxd_out
            dc_ref[sl, :] = (_nn(dcbb, bb_) + _nt(dye, stb)).astype(dc_ref.dtype)
            db_ref[sl, :] = (_tn(dcbb, cb_) + _nt(xw, dstb)).astype(db_ref.dtype)
            dstate[...] = dst * jnp.exp(alast_e) + _tn(cb_, dye)
            col_out = xd * dxd_out
            dac = _sel(_nt, dyb.astype(F32) * y_in - xdb.astype(F32) * dxd_in + dy * y_out - col_out, expand, 2)
            beyond = _colsum(col_out) + jnp.exp(alast_e) * _colsum(dst * st32)
            da = (_sel(_nn, dac, triu_f, 3, x_first=False) +
                  _sel(_nt, jnp.broadcast_to(beyond, (8, GROUP_W)), expand, 3)[0:1, :])
            ddelta = _sel(_nt, dxd * xs, expand, 2) - da * ea
            dalog_ref[0] += _colsum(da * a)
            ddtb = ddelta * _sigmoid(dtb)
            dbias_ref[0] += _colsum(ddtb)
            ddt_ref[sl, :] = ddtb.astype(BF16)
            ddsk_ref[0] += _sel(_nt, jnp.broadcast_to(_colsum(dy * xs), (8, GROUP_W)), expand, 3)[0:1, :]
            dxs_ref[sl, :] = (dxd * delta_e + dy * d_e).astype(dxs_ref.dtype)
            return c0

        lax.fori_loop(0, ncb, chunk, 0, unroll=min(CHUNK_UNROLL, ncb))

        @pl.when((pl.program_id(0) == N_GROUPS - 1) & (pl.program_id(1) == nb - 1))
        def _():
            xchg_wait()

    small = pl.BlockSpec((1, 1, N_STATE), lambda g, j: (g, 0, 0))
    wide = pl.BlockSpec((tb, GROUP_W), lambda g, j: (nb - 1 - j, g))
    narrow = pl.BlockSpec((tb, N_STATE), lambda g, j: (nb - 1 - j, g))
    hbm = pl.BlockSpec(memory_space=pl.ANY)
    return pl.pallas_call(
        body, name="ssd_bwd", grid=(N_GROUPS, nb),
        out_shape=[jax.ShapeDtypeStruct((t, B_INNER), BF16), jax.ShapeDtypeStruct((t, GROUP_W), BF16),
                   jax.ShapeDtypeStruct((t, GROUP_W), BF16), jax.ShapeDtypeStruct(dproj.shape, dproj.dtype),
                   jax.ShapeDtypeStruct((t, GROUP_W), BF16), jax.ShapeDtypeStruct((1, B_INNER), F32),
                   jax.ShapeDtypeStruct((N_GROUPS, 1, N_STATE), F32), jax.ShapeDtypeStruct((N_GROUPS, 1, N_STATE), F32),
                   jax.ShapeDtypeStruct((N_GROUPS, 1, N_STATE), F32), jax.ShapeDtypeStruct(part.shape, part.dtype)],
        in_specs=[wide,
                  pl.BlockSpec((tb, N_STATE), lambda g, j: (nb - 1 - j, B_BLOCK0 + g)),
                  pl.BlockSpec((tb, N_STATE), lambda g, j: (nb - 1 - j, C_BLOCK0 + g)),
                  pl.BlockSpec((tb, N_STATE), lambda g, j: (nb - 1 - j, DT_BLOCK0 + g)),
                  pl.BlockSpec((tb, GROUP_W), lambda g, j: (nb - 1 - j, Z_BLOCK0 + g)),
                  small, small, small, pl.BlockSpec((1, GROUP_W), lambda g, j: (0, g)),
                  pl.BlockSpec((N_STATE, GROUP_W), lambda g, j: (0, 0)), wide,
                  pl.BlockSpec((1, nsaved, N_STATE, GROUP_W), lambda g, j: (g, nb - 1 - j, 0, 0)), hbm, hbm],
        out_specs=[wide, narrow, narrow,
                   pl.BlockSpec((None, tb, GROUP_W), lambda g, j: (Z_BLOCK0 // 2 + g // 2, nb - 1 - j, g % 2)),
                   narrow, pl.BlockSpec((1, GROUP_W), lambda g, j: (0, g)), small, small, small, hbm],
        input_output_aliases={13: 3},
        scratch_shapes=[pltpu.VMEM((N_STATE, GROUP_W), F32)] + CHIP_SEMS,
        compiler_params=_cparams(("arbitrary", "arbitrary")),
    )(xc, xc, xc, proj, proj, alog4, bias4, dskip4, wnorm, expand, dob, states, part, dproj)


def lower_bound_fwd(hgrn_lb):
    def body(a_ref, o_ref):
        a0, a1 = a_ref[0:1, :], a_ref[1:2, :]
        m = jnp.maximum(a0, a1)
        e0, e1 = jnp.exp(a0 - m), jnp.exp(a1 - m)
        o_ref[...] = e0 / (e0 + e1)

    return pl.pallas_call(body, name="lower_bound_fwd", out_shape=jax.ShapeDtypeStruct((1, D), F32))(hgrn_lb)


def ada_weight_grad(c_all, dmod_cols):
    def body(c_ref, d_ref, o_ref):
        cval = c_ref[...]
        o_ref[...] = _tn(cval * _sigmoid(cval), d_ref[...], HI)

    return pl.pallas_call(body, name="ada_weight_grad",
                          out_shape=jax.ShapeDtypeStruct((D, dmod_cols.shape[1]), F32))(c_all, dmod_cols)


def reduce_small(gathered, hgrn_lb, dlb_off):
    n = gathered.shape[2]

    def body(g_ref, a_ref, o_ref, glb_ref):
        s = g_ref[0]
        for d in range(1, N_DEV):
            s = s + g_ref[d]
        o_ref[...] = s
        a0, a1 = a_ref[0:1, :], a_ref[1:2, :]
        m = jnp.maximum(a0, a1)
        e0, e1 = jnp.exp(a0 - m), jnp.exp(a1 - m)
        p0 = e0 / (e0 + e1)
        tq = s[:, dlb_off:dlb_off + D] * p0 * (1.0 - p0)
        glb_ref[0:1, :] = tq
        glb_ref[1:2, :] = -tq

    return pl.pallas_call(body, name="reduce_small",
                          out_shape=[jax.ShapeDtypeStruct((1, n), F32), jax.ShapeDtypeStruct((2, D), F32)])(gathered, hgrn_lb)


def _adam_math(w, g, m, v):
    m2 = ADAM_B1 * m + (1.0 - ADAM_B1) * g
    v2 = ADAM_B2 * v + (1.0 - ADAM_B2) * (g * g)
    m_hat = m2 / (1.0 - ADAM_B1 ** ADAM_STEP)
    v_hat = v2 / (1.0 - ADAM_B2 ** ADAM_STEP)
    delta = -ADAM_LR * (m_hat / (jnp.sqrt(v_hat) + ADAM_EPS) + ADAM_WD * w)
    return delta, m2, v2


def _row_tile(rows, mult=8, cap=128):
    for cand in range(cap - cap % mult, 0, -mult):
        if rows % cand == 0:
            return cand
    return rows


def sum_parts(parts, name):
    n, rows, cols = parts.shape
    tr = _row_tile(rows, 16, 256)

    def body(p_ref, o_ref):
        s = p_ref[0].astype(F32)
        for d in range(1, n):
            s = s + p_ref[d].astype(F32)
        o_ref[...] = s

    return pl.pallas_call(
        body, name=name, grid=(rows // tr,),
        out_shape=jax.ShapeDtypeStruct((rows, cols), F32),
        in_specs=[pl.BlockSpec((n, tr, cols), lambda i: (0, i, 0))],
        out_specs=pl.BlockSpec((tr, cols), lambda i: (i, 0)),
        compiler_params=_cparams(("parallel",)),
    )(parts)


def sum_pair(a, b, name):
    rows, cols = a.shape
    tr = _row_tile(rows, 16, 256)

    def body(a_ref, b_ref, o_ref):
        o_ref[...] = (a_ref[...].astype(F32) + b_ref[...].astype(F32)).astype(o_ref.dtype)

    blk = pl.BlockSpec((tr, cols), lambda i: (i, 0))
    return pl.pallas_call(
        body, name=name, grid=(rows // tr,),
        out_shape=jax.ShapeDtypeStruct((rows, cols), a.dtype),
        in_specs=[blk, blk], out_specs=blk,
        compiler_params=_cparams(("parallel",)),
    )(a, b)


def adamw(w, g, m, v, name):
    rows, cols = w.shape
    tr = _row_tile(rows)

    def body(w_ref, g_ref, m_ref, v_ref, d_ref, m2_ref, v2_ref):
        delta, m2, v2 = _adam_math(w_ref[...], g_ref[...], m_ref[...], v_ref[...])
        d_ref[...] = delta
        m2_ref[...] = m2
        v2_ref[...] = v2

    blk = pl.BlockSpec((tr, cols), lambda i: (i, 0))
    return pl.pallas_call(
        body, name=name, grid=(rows // tr,),
        out_shape=[jax.ShapeDtypeStruct((rows, cols), F32)] * 3,
        in_specs=[blk] * 4, out_specs=[blk] * 3,
        compiler_params=_cparams(("parallel",)),
    )(w, g, m, v)


def _pad128(n):
    return -(-n // 128) * 128


def _pack(arrays):
    offs, parts, off = [], [], 0
    for a in arrays:
        flat = a.reshape(1, -1)
        n = flat.shape[1]
        offs.append(off)
        parts.append(jnp.pad(flat, ((0, 0), (0, _pad128(n) - n))))
        off += _pad128(n)
    return jnp.concatenate(parts, axis=1), offs


def _unpack(vec, offs, shapes):
    out = []
    for off, shp in zip(offs, shapes):
        n = int(np.prod(shp))
        out.append(vec[0, off:off + n].reshape(shp))
    return out


IN_ROWS = IN_DIM // N_DEV
DT_ROW0 = 9216
DT_DEV, DT_LO = divmod(DT_ROW0, IN_ROWS)


GATE_SHIFT = D - 32


def _in_row_pieces(tile):
    pieces = []
    if tile == DT_COL_BLOCK:
        for g in range(N_GROUPS):
            o = DT_ROW0 + HEADS_PER_GROUP * g
            pieces.append((N_STATE * g, o // IN_ROWS, o % IN_ROWS, HEADS_PER_GROUP))
        return pieces
    r, end = tile * D, (tile + 1) * D
    while r < end:
        o = r if r < DT_ROW0 else r - GATE_SHIFT
        dev, loc = divmod(o, IN_ROWS)
        n = min(end - r, IN_ROWS - loc)
        pieces.append((r - tile * D, dev, loc, n))
        r += n
    return pieces


def assemble_w_in(g_all):
    ntile = N_PROJ // D

    def body(g_ref, o_ref):
        j = pl.program_id(0)
        for tile in range(ntile):
            @pl.when(j == tile)
            def _(tile=tile):
                if tile == DT_COL_BLOCK:
                    o_ref[...] = jnp.zeros_like(o_ref)
                for dst, dev, loc, n in _in_row_pieces(tile):
                    o_ref[pl.ds(dst, n), :] = g_ref[dev, pl.ds(loc, n), :]

    return pl.pallas_call(
        body, name="assemble_w_in", grid=(ntile,),
        out_shape=jax.ShapeDtypeStruct((N_PROJ, D), g_all.dtype),
        in_specs=[pl.BlockSpec(memory_space=pltpu.VMEM)],
        out_specs=pl.BlockSpec((D, D), lambda j: (j, 0)),
        compiler_params=_cparams(("arbitrary",)),
    )(g_all)


def _grad_in_blocks(g_t, core, slot):
    dt0 = DT_COL_BLOCK * D
    dt = g_t[dt0:dt0 + N_GROUPS * N_STATE].reshape(N_GROUPS, N_STATE, D)[:, :HEADS_PER_GROUP].reshape(32, D)
    with_dt = jnp.concatenate([g_t[DT_DEV * IN_ROWS:DT_ROW0], dt,
                               g_t[DT_ROW0 + 32 + GATE_SHIFT:(DT_DEV + 1) * IN_ROWS + GATE_SHIFT]], axis=0)
    blocks = []
    for q in range(N_CHIP):
        if 2 * q + 1 < DT_DEV:
            blk = lax.dynamic_slice_in_dim(g_t, IN_ROWS * (2 * q + core), IN_ROWS, axis=0)
        else:
            assert 2 * q == DT_DEV
            after = g_t[(DT_DEV + 1) * IN_ROWS + GATE_SHIFT:(DT_DEV + 2) * IN_ROWS + GATE_SHIFT]
            blk = jnp.where(core == 0, with_dt, after)
        blocks.append(jnp.pad(blk, ((0, slot - IN_ROWS), (0, 0))))
    return jnp.stack(blocks)


def kernel(x, c, w_ada, b_ada, w_in, hgrn_lb, hgrn_gnorm, ssm_conv_w, ssm_conv_b, ssm_dt_bias, ssm_a_log, ssm_d, ssm_norm, w_branch_a, w_branch_b, w_o, ln1_g, ln1_b, w_ffn_gate, w_ffn_up, w_ffn_down, ln2_g, ln2_b, loss_target, m_w_ada, m_b_ada, m_w_in, m_hgrn_lb, m_hgrn_gnorm, m_ssm_conv_w, m_ssm_conv_b, m_ssm_dt_bias, m_ssm_a_log, m_ssm_d, m_ssm_norm, m_w_branch_a, m_w_branch_b, m_w_o, m_ln1_g, m_ln1_b, m_w_ffn_gate, m_w_ffn_up, m_w_ffn_down, m_ln2_g, m_ln2_b, v_w_ada, v_b_ada, v_w_in, v_hgrn_lb, v_hgrn_gnorm, v_ssm_conv_w, v_ssm_conv_b, v_ssm_dt_bias, v_ssm_a_log, v_ssm_d, v_ssm_norm, v_w_branch_a, v_w_branch_b, v_w_o, v_ln1_g, v_ln1_b, v_w_ffn_gate, v_w_ffn_up, v_w_ffn_down, v_ln2_g, v_ln2_b):
    me = 4 * lax.axis_index("x") + 2 * lax.axis_index("y") + lax.axis_index("c")
    xt = x[0]
    tgt = loss_target[0]
    t = xt.shape[0]
    ada_cols = w_ada.shape[2]
    conv_cols = ssm_conv_w.shape[2]

    small_in, _ = _pack([c, ssm_conv_w[0]])
    small_all = allgather_vmem(small_in, "allgather_small_inputs")
    c_all = small_all[:, 0, :D]
    conv_w = small_all[:, 0, D:D + CONV_TAPS * conv_cols].reshape(N_DEV, CONV_TAPS, conv_cols)
    conv_w = conv_w.transpose(1, 0, 2).reshape(CONV_TAPS, CONV_DIM)
    mod = ada_modulation(c_all, w_ada[0], b_ada.reshape(N_DEV, 1, ada_cols))
    mod6 = mod.reshape(6, D)

    shards = [w_in[0].T, w_branch_a[0], w_branch_b[0], w_o[0], w_ffn_gate[0].T, w_ffn_up[0].T, w_ffn_down[0]]
    shard_rows = [s.shape[0] for s in shards]
    slot_rows = [-(-r // 32) * 32 for r in shard_rows]
    row_offs = [sum(slot_rows[:i]) for i in range(len(shards))]
    padded = [jnp.pad(s.astype(BF16), ((0, p - r), (0, 0))) for s, r, p in zip(shards, shard_rows, slot_rows)]
    w_in_t = assemble_w_in(allgather_hbm(padded[0], "allgather_w_in"))

    lb = lower_bound_fwd(hgrn_lb)
    u1 = ln_modulate(xt, mod6, 0, 1, "ln_modulate_1")
    proj, g_rest = mm_nt_gather(u1, w_in_t, F32, jnp.concatenate(padded[1:], axis=0), "mm_in_proj")
    g_ba, g_bb, g_o, g_fg, g_fu, g_fd = (g_rest[:, o - slot_rows[0]:o - slot_rows[0] + r]
                                         for o, r in zip(row_offs[1:], shard_rows[1:]))
    w_ba = g_ba.reshape(D, D)
    w_bb = g_bb.reshape(B_INNER, D)
    w_oo = g_o.reshape(D, D)
    w_gu_t = jnp.concatenate([g_fg.reshape(D_FF, D), g_fu.reshape(D_FF, D)], axis=0)
    w_dn = g_fd.reshape(D_FF, D)
    o_a, o_raw, st_a = hgrn_fwd(proj, lb, hgrn_gnorm)
    xc, conv_slope = conv_fwd(proj, conv_w, ssm_conv_b)
    pad3 = ((0, 0), (0, 0), (0, N_STATE - HEADS_PER_GROUP))
    alog4 = jnp.pad(ssm_a_log.reshape(N_GROUPS, 1, HEADS_PER_GROUP), pad3)
    bias4 = jnp.pad(ssm_dt_bias.reshape(N_GROUPS, 1, HEADS_PER_GROUP), pad3)
    dskip4 = jnp.pad(ssm_d.reshape(N_GROUPS, 1, HEADS_PER_GROUP), pad3)
    expand = _head_expand()
    o_b, st_b = ssd_fwd(proj, xc, alog4, bias4, dskip4, ssm_norm, expand)
    ya = mm_nn(o_a, w_ba, BF16, "mm_branch_a")
    yb = mm_nn(o_b, w_bb, BF16, "mm_branch_b")
    merged = merge_gates(ya, yb, proj)
    h1 = mm_nn(merged, w_oo, F32, "mm_out_proj")
    x1 = resid_ln(xt, h1, mod6, 2, ln1_g, ln1_b, "resid_ln_1")
    u2 = ln_modulate(x1, mod6, 3, 4, "ln_modulate_2")
    gu = mm_nt(u2, w_gu_t, BF16, "mm_ffn_in")
    act = swiglu_act(gu)
    h2 = mm_nn(act, w_dn, F32, "mm_ffn_out")

    dh2, dx1_part, acc4 = resid_ln_bwd(x1, h2, mod6, 5, ln2_g, ln2_b, tgt, True, "resid_ln_2_bwd")
    g_dn = mm_tn(act, dh2, "mm_grad_ffn_down")
    dact = mm_nt(dh2, w_dn, BF16, "mm_dact")
    dgu = swiglu_act_bwd(gu, dact)
    g_gu_t = mm_tn(dgu, u2, "mm_grad_ffn_in")
    du2 = mm_nn(dgu, w_gu_t, F32, "mm_du2")
    dx1, acc3 = ln_modulate_bwd(x1, du2, mod6, 4, dx1_part, "ln_modulate_2_bwd")
    dh1, dx_part, acc2 = resid_ln_bwd(xt, h1, mod6, 2, ln1_g, ln1_b, dx1, False, "resid_ln_1_bwd")
    g_o = mm_tn(merged, dh1, "mm_grad_out_proj")
    dmerged = mm_nt(dh1, w_oo, BF16, "mm_dmerged")
    dya, dyb, dproj = merge_gates_bwd(dmerged, ya, yb, proj)
    g_ba_full = mm_tn(o_a, dya, "mm_grad_branch_a")
    g_bb_full = mm_tn(o_b, dyb, "mm_grad_branch_b")
    doa = mm_nt(dya, w_ba, F32, "mm_doa")
    dob = mm_nt(dyb, w_bb, F32, "mm_dob")
    my_core = lax.axis_index("c")

    def by_core(blocks, rows, slots):
        contrib = jnp.concatenate([jnp.pad(b.reshape(N_DEV, -1, D), ((0, 0), (0, p - r), (0, 0)))
                                   for b, r, p in zip(blocks, rows, slots)], axis=1)
        split = contrib.reshape(N_CHIP, 2, contrib.shape[1], D).transpose(1, 0, 2, 3)
        return (lax.dynamic_index_in_dim(split, my_core, 0, keepdims=False),
                lax.dynamic_index_in_dim(split, 1 - my_core, 0, keepdims=False))

    keep_e, give_e = by_core([g_ba_full, g_bb_full, g_o, g_gu_t[:D_FF], g_gu_t[D_FF:], g_dn],
                             shard_rows[1:], slot_rows[1:])
    dproj, dlb, dgn, got_e = hgrn_bwd(proj, lb, hgrn_gnorm, o_raw, doa, st_a, give_e, dproj)
    chip_e = sum_pair(keep_e.reshape(-1, D), got_e.reshape(-1, D), "sum_grads_rest_chip").reshape(keep_e.shape)
    dxs, dbm, dcm, dproj, ddt, dwn, dalog, dbias, ddsk, parts_e = ssd_bwd(proj, xc, alog4, bias4, dskip4, ssm_norm,
                                                                          expand, dob, st_b, chip_e, dproj)
    dxc = jnp.concatenate([dxs, dbm, dcm], axis=1)
    dproj, dcw, dcb = conv_bwd(proj, dxc, conv_slope, conv_w, dproj)
    dproj = dt_fill(ddt, dproj)
    g_in_t = mm_tn(dproj, u1, "mm_grad_in_proj")
    keep_l = _grad_in_blocks(g_in_t, my_core, slot_rows[0])
    give_l = _grad_in_blocks(g_in_t, 1 - my_core, slot_rows[0])
    got_l = exchange_sibling(give_l, "exchange_grad_in_sibling")
    chip_l = sum_pair(keep_l.reshape(-1, D), got_l.reshape(-1, D), "sum_grad_in_chip").reshape(keep_l.shape)
    du1, parts_l = mm_nn_exchange(dproj, w_in_t, F32, chip_l, "mm_du1")
    dx, acc1 = ln_modulate_bwd(xt, du1, mod6, 1, dx_part, "ln_modulate_1_bwd")
    gw_in = sum_parts(parts_l, "sum_grad_in")[:shard_rows[0]].T
    g_rows = sum_parts(parts_e, "sum_grads_rest")
    gw_ba, gw_bb, gw_o, gw_fg, gw_fu, gw_fd = (g_rows[o - slot_rows[0]:o - slot_rows[0] + r]
                                               for o, r in zip(row_offs[1:], shard_rows[1:]))
    gw_fg, gw_fu = gw_fg.T, gw_fu.T

    dmod = jnp.concatenate([acc1[1:2], acc1[0:1], acc2[0:1], acc3[1:2], acc3[0:1], acc4[0:1]], axis=1)
    small_fields = [dmod, acc4[3:4, :128], dlb, dgn, dcw[:CONV_TAPS], dcb, dbias, dalog, ddsk, dwn,
                    acc2[1:2], acc2[2:3], acc4[1:2], acc4[2:3]]
    small_out, offs = _pack(small_fields)
    small_sum_in = allgather_vmem(small_out, "allgather_small_grads")
    gsum, g_lb = reduce_small(small_sum_in, hgrn_lb, offs[2])
    (g_bada, loss_row, _, g_gn, g_cw_full, g_cb, g_bias4, g_alog4, g_dsk4, g_wn, g_l1g, g_l1b, g_l2g, g_l2b) = _unpack(
        gsum, offs, [(1, 6 * D), (1, 128), (1, D), (1, HK), (CONV_TAPS, CONV_DIM), (1, CONV_DIM),
                     (N_GROUPS, N_STATE), (N_GROUPS, N_STATE), (N_GROUPS, N_STATE), (1, B_INNER),
                     (1, D), (1, D), (1, D), (1, D)])
    loss = loss_row[0, 0]
    g_cw = lax.dynamic_slice(g_cw_full, (0, me * conv_cols), (CONV_TAPS, conv_cols))[None]
    g_dtb = g_bias4[:, :HEADS_PER_GROUP].reshape(1, 32)
    g_alog = g_alog4[:, :HEADS_PER_GROUP].reshape(1, 32)
    g_dsk = g_dsk4[:, :HEADS_PER_GROUP].reshape(1, 32)

    dmod_all = small_sum_in[:, 0, offs[0]:offs[0] + 6 * D]
    dmod_cols = lax.dynamic_slice(dmod_all, (0, me * ada_cols), (N_DEV, ada_cols))
    gw_ada = ada_weight_grad(c_all, dmod_cols)

    big = [("ada", w_ada[0], gw_ada, m_w_ada[0], v_w_ada[0]), ("in", w_in[0], gw_in, m_w_in[0], v_w_in[0]),
           ("branch_a", w_branch_a[0], gw_ba, m_w_branch_a[0], v_w_branch_a[0]),
           ("branch_b", w_branch_b[0], gw_bb, m_w_branch_b[0], v_w_branch_b[0]),
           ("o", w_o[0], gw_o, m_w_o[0], v_w_o[0]),
           ("ffn_gate", w_ffn_gate[0], gw_fg, m_w_ffn_gate[0], v_w_ffn_gate[0]),
           ("ffn_up", w_ffn_up[0], gw_fu, m_w_ffn_up[0], v_w_ffn_up[0]),
           ("ffn_down", w_ffn_down[0], gw_fd, m_w_ffn_down[0], v_w_ffn_down[0])]
    big_out = {}
    for nm, w_, g_, m_, v_ in big:
        d_, m2_, v2_ = adamw(w_, g_, m_, v_, "adamw_" + nm)
        big_out[nm] = (g_[None], d_[None], m2_[None], v2_[None])

    small_w = [b_ada, hgrn_lb, hgrn_gnorm, ssm_conv_w, ssm_conv_b, ssm_dt_bias, ssm_a_log, ssm_d, ssm_norm,
               ln1_g, ln1_b, ln2_g, ln2_b]
    small_g = [g_bada, g_lb, g_gn, g_cw, g_cb, g_dtb, g_alog, g_dsk, g_wn, g_l1g, g_l1b, g_l2g, g_l2b]
    small_m = [m_b_ada, m_hgrn_lb, m_hgrn_gnorm, m_ssm_conv_w, m_ssm_conv_b, m_ssm_dt_bias, m_ssm_a_log, m_ssm_d,
               m_ssm_norm, m_ln1_g, m_ln1_b, m_ln2_g, m_ln2_b]
    small_v = [v_b_ada, v_hgrn_lb, v_hgrn_gnorm, v_ssm_conv_w, v_ssm_conv_b, v_ssm_dt_bias, v_ssm_a_log, v_ssm_d,
               v_ssm_norm, v_ln1_g, v_ln1_b, v_ln2_g, v_ln2_b]
    shapes = [a.shape for a in small_w]
    small_g = [g_.reshape(s) for g_, s in zip(small_g, shapes)]
    pw, poffs = _pack(small_w)
    pg, _ = _pack(small_g)
    pm, _ = _pack(small_m)
    pv, _ = _pack(small_v)
    pd, pm2, pv2 = adamw(pw, pg, pm, pv, "adamw_small")
    s_d, s_m, s_v = (_unpack(p, poffs, shapes) for p in (pd, pm2, pv2))
    (sn_bada, sn_lb, sn_gn, sn_cw, sn_cb, sn_dtb, sn_alog, sn_dsk, sn_wn, sn_l1g, sn_l1b, sn_l2g, sn_l2b) = range(13)

    def order(kind):
        sm = [small_g, s_d, s_m, s_v][kind]
        bg = lambda nm: big_out[nm][kind]
        return [bg("ada"), sm[sn_bada], bg("in"), sm[sn_lb], sm[sn_gn], sm[sn_cw], sm[sn_cb], sm[sn_dtb], sm[sn_alog],
                sm[sn_dsk], sm[sn_wn], bg("branch_a"), bg("branch_b"), bg("o"), sm[sn_l1g], sm[sn_l1b],
                bg("ffn_gate"), bg("ffn_up"), bg("ffn_down"), sm[sn_l2g], sm[sn_l2b]]

    return (loss, dx[None], *order(0), *order(1), *order(2), *order(3))
```

```python
import numpy as np
import jax
import jax.numpy as jnp
from jax import lax
from jax.experimental import pallas as pl
from jax.experimental.pallas import tpu as pltpu

F32 = jnp.float32
BF16 = jnp.bfloat16
HI = lax.Precision.HIGHEST

N_DEV = 8
D = 1024
N_HEADS_A = 8
HK = 128
CHUNK = 64
SSD_CHUNK = 128
SSD_CHUNK_BWD = 256
N_GROUPS = 4
HEADS_PER_GROUP = 8
HEAD_P = 64
N_STATE = 128
GROUP_W = HEADS_PER_GROUP * HEAD_P
B_INNER = 2048
CONV_DIM = 3072
D_FF = 2816
IN_DIM = 11296
N_PROJ = 12288
ALPHA = 2.0 ** 0.25
LN_EPS = 1e-5
RMS_EPS = 1e-6
Q_SCALE = 128 ** -0.5
EXP_CLIP = 80.0
ADAM_LR, ADAM_B1, ADAM_B2, ADAM_EPS, ADAM_WD, ADAM_STEP = 0.001, 0.9, 0.999, 1e-8, 0.01, 10
VMEM_LIMIT = 48 * 1024 * 1024
TOKEN_BLOCK = 512
ROW_TILE = 512
WIDE_ROW_TILE = 1024
FFN_ROW_TILE = 512
MM_ROW_TILE = 1024
MM_TOKEN_TILE = 4096
MM_K_TILE = 3072
MM_COL_TILE = 1408
HGRN_HEADS_PER_STEP = 4
CHUNK_UNROLL = 8
MESH_ID = pl.DeviceIdType.MESH

NT_DIMS = (((1,), (1,)), ((), ()))
TN_DIMS = (((0,), (0,)), ((), ()))


def _cparams(sem=None):
    return pltpu.CompilerParams(dimension_semantics=sem, vmem_limit_bytes=VMEM_LIMIT)


def _sigmoid(x):
    return 1.0 / (1.0 + jnp.exp(-x))


def _dsilu(x, s):
    return s * (1.0 + x * (1.0 - s))


def _nt(a, b, precision=None):
    return lax.dot_general(a, b, NT_DIMS, precision=precision, preferred_element_type=F32)


def _tn(a, b, precision=None):
    return lax.dot_general(a, b, TN_DIMS, precision=precision, preferred_element_type=F32)


def _nn(a, b, precision=None):
    return jnp.dot(a, b, precision=precision, preferred_element_type=F32)


def _split(x, pieces):
    out = []
    for i in range(pieces):
        p = x.astype(BF16)
        out.append(p)
        if i + 1 < pieces:
            x = x - p.astype(F32)
    return out


def _sel(dot, x, sel01, pieces, x_first=True):
    acc = None
    for p in _split(x, pieces):
        term = dot(p, sel01) if x_first else dot(sel01, p)
        acc = term if acc is None else acc + term
    return acc


def _ln(x):
    mu = jnp.mean(x, axis=-1, keepdims=True)
    xc = x - mu
    rstd = lax.rsqrt(jnp.mean(xc * xc, axis=-1, keepdims=True) + LN_EPS)
    return xc * rstd, rstd


def _ln_bwd(dxh, xh, rstd):
    return rstd * (dxh - jnp.mean(dxh, axis=-1, keepdims=True) - xh * jnp.mean(dxh * xh, axis=-1, keepdims=True))


def _colsum(x):
    return jnp.sum(x, axis=0, keepdims=True)


def _tri(n, upper=False):
    r = lax.broadcasted_iota(jnp.int32, (n, n), 0)
    c = lax.broadcasted_iota(jnp.int32, (n, n), 1)
    return (c >= r) if upper else (r >= c)


def _my_pos():
    return lax.axis_index("x"), lax.axis_index("y"), lax.axis_index("c")


def _peer(pos, k):
    x, y, c = pos
    return (x ^ ((k >> 2) & 1), y ^ ((k >> 1) & 1), c ^ (k & 1))


def _flat(pos):
    return 4 * pos[0] + 2 * pos[1] + pos[2]


def allgather_vmem(v, name):
    n = v.shape[1]

    def body(v_ref, o_ref, send_sems, recv_sems, local_sem):
        me = _my_pos()
        mine = pltpu.make_async_copy(v_ref, o_ref.at[_flat(me)], local_sem)
        mine.start()
        sends = []
        for k in range(1, N_DEV):
            peer = _peer(me, k)
            cp = pltpu.make_async_remote_copy(v_ref, o_ref.at[_flat(me)], send_sems.at[k - 1], recv_sems.at[k - 1],
                                              device_id=peer, device_id_type=MESH_ID)
            cp.start()
            sends.append(cp)
        for k in range(1, N_DEV):
            peer = _peer(me, k)
            pltpu.make_async_remote_copy(v_ref, o_ref.at[_flat(peer)], send_sems.at[k - 1], recv_sems.at[k - 1],
                                         device_id=peer, device_id_type=MESH_ID).wait_recv()
        for cp in sends:
            cp.wait_send()
        mine.wait()

    return pl.pallas_call(
        body, name=name,
        out_shape=jax.ShapeDtypeStruct((N_DEV, 1, n), F32),
        in_specs=[pl.BlockSpec(memory_space=pltpu.VMEM)],
        out_specs=pl.BlockSpec(memory_space=pltpu.VMEM),
        scratch_shapes=[pltpu.SemaphoreType.DMA((N_DEV - 1,)), pltpu.SemaphoreType.DMA((N_DEV - 1,)),
                        pltpu.SemaphoreType.DMA],
        compiler_params=_cparams(),
    )(v)


def ada_modulation(c_all, w_ada_s, b_ada_r):
    ncol = w_ada_s.shape[1]

    def body(c_ref, w_ref, b_ref, o_ref, part_ref, send_sems, recv_sems):
        me = _my_pos()
        cval = c_ref[...]
        cond = cval * _sigmoid(cval)
        part = _nn(cond, w_ref[...], HI)
        for r in range(N_DEV):
            part_ref[r] = part[r:r + 1, :]
        sends = []
        for k in range(1, N_DEV):
            peer = _peer(me, k)
            cp = pltpu.make_async_remote_copy(part_ref.at[_flat(peer)], o_ref.at[_flat(me)], send_sems.at[k - 1],
                                              recv_sems.at[k - 1], device_id=peer, device_id_type=MESH_ID)
            cp.start()
            sends.append(cp)
        o_ref[_flat(me)] = part_ref[_flat(me)]
        for k in range(1, N_DEV):
            peer = _peer(me, k)
            pltpu.make_async_remote_copy(part_ref.at[_flat(peer)], o_ref.at[_flat(peer)], send_sems.at[k - 1],
                                         recv_sems.at[k - 1], device_id=peer, device_id_type=MESH_ID).wait_recv()
        for cp in sends:
            cp.wait_send()
        o_ref[...] = o_ref[...] + b_ref[...]

    return pl.pallas_call(
        body, name="ada_modulation",
        out_shape=jax.ShapeDtypeStruct((N_DEV, 1, ncol), F32),
        in_specs=[pl.BlockSpec(memory_space=pltpu.VMEM)] * 3,
        out_specs=pl.BlockSpec(memory_space=pltpu.VMEM),
        scratch_shapes=[pltpu.VMEM((N_DEV, 1, ncol), F32), pltpu.SemaphoreType.DMA((N_DEV - 1,)),
                        pltpu.SemaphoreType.DMA((N_DEV - 1,))],
        compiler_params=_cparams(),
    )(c_all, w_ada_s, b_ada_r)


def allgather_hbm(shard, name):
    def body(x_ref, out_ref, send_sems, recv_sems, local_sem):
        x, y, c = _my_pos()
        me, sibling = (x, y, c), (x, y, 1 - c)
        chips = [(1 - x, y), (x, 1 - y), (1 - x, 1 - y)]

        def slot(pos):
            return out_ref.at[_flat(pos)]

        def copy(k, block, to, src=None):
            return pltpu.make_async_remote_copy(slot(block) if src is None else src, slot(block), send_sems.at[k],
                                                recv_sems.at[k], device_id=to, device_id_type=MESH_ID)

        mine = pltpu.make_async_copy(x_ref, slot(me), local_sem)
        mine.start()
        first = [copy(0, me, sibling, src=x_ref)]
        first += [copy(1 + j, me, (*chip, c), src=x_ref) for j, chip in enumerate(chips)]
        for cp in first:
            cp.start()
        passed = [copy(4 + j, (*chip, c), sibling) for j, chip in enumerate(chips)]
        for j, chip in enumerate(chips):
            copy(1 + j, (*chip, c), me).wait_recv()
            passed[j].start()
        copy(0, sibling, me).wait_recv()
        for j, chip in enumerate(chips):
            copy(4 + j, (*chip, 1 - c), me).wait_recv()
        for cp in first + passed:
            cp.wait_send()
        mine.wait()

    return pl.pallas_call(
        body, name=name,
        out_shape=jax.ShapeDtypeStruct((N_DEV,) + shard.shape, shard.dtype),
        in_specs=[pl.BlockSpec(memory_space=pl.ANY)],
        out_specs=pl.BlockSpec(memory_space=pl.ANY),
        scratch_shapes=[pltpu.SemaphoreType.DMA((N_DEV - 1,)), pltpu.SemaphoreType.DMA((N_DEV - 1,)),
                        pltpu.SemaphoreType.DMA],
        compiler_params=_cparams(),
    )(shard)


N_CHIP = N_DEV // 2
SIBLING_SEMS = [pltpu.SemaphoreType.DMA, pltpu.SemaphoreType.DMA]
CHIP_SEMS = [pltpu.SemaphoreType.DMA((N_CHIP - 1,)), pltpu.SemaphoreType.DMA((N_CHIP - 1,)), pltpu.SemaphoreType.DMA]


def _sibling_exchange(s_ref, o_ref, send_sem, recv_sem):
    x, y, c = _my_pos()
    cp = pltpu.make_async_remote_copy(s_ref, o_ref, send_sem, recv_sem, device_id=(x, y, 1 - c), device_id_type=MESH_ID)
    return cp.start, cp.wait


def _chip_exchange(p_ref, o_ref, send_sems, recv_sems, local_sem):
    x, y, c = _my_pos()
    my_chip = 2 * x + y
    mine = pltpu.make_async_copy(p_ref.at[my_chip], o_ref.at[my_chip], local_sem)
    peers = [(x ^ (k >> 1), y ^ (k & 1)) for k in range(1, N_CHIP)]
    sends = [pltpu.make_async_remote_copy(p_ref.at[2 * px + py], o_ref.at[my_chip], send_sems.at[k], recv_sems.at[k],
                                          device_id=(px, py, c), device_id_type=MESH_ID)
             for k, (px, py) in enumerate(peers)]
    recvs = [pltpu.make_async_remote_copy(p_ref.at[2 * px + py], o_ref.at[2 * px + py], send_sems.at[k], recv_sems.at[k],
                                          device_id=(px, py, c), device_id_type=MESH_ID)
             for k, (px, py) in enumerate(peers)]

    def start():
        mine.start()
        for cp in sends:
            cp.start()

    def wait():
        for cp in recvs:
            cp.wait_recv()
        for cp in sends:
            cp.wait_send()
        mine.wait()

    return start, wait


def exchange_sibling(send, name):
    def body(s_ref, o_ref, send_sem, recv_sem):
        start, wait = _sibling_exchange(s_ref, o_ref, send_sem, recv_sem)
        start()
        wait()

    return pl.pallas_call(
        body, name=name,
        out_shape=jax.ShapeDtypeStruct(send.shape, send.dtype),
        in_specs=[pl.BlockSpec(memory_space=pl.ANY)],
        out_specs=pl.BlockSpec(memory_space=pl.ANY),
        scratch_shapes=SIBLING_SEMS,
        compiler_params=_cparams(),
    )(send)


LANES = 128


def _k_tile(kdim, unit=LANES):
    for cand in range(MM_K_TILE - MM_K_TILE % unit, 0, -unit):
        if kdim % cand == 0:
            return cand
    return kdim


def _lane_tile(n, cap):
    for cand in range(cap - cap % LANES, 0, -LANES):
        if n % cand == 0:
            return cand
    return n


def mm_nn(a, b, out_dtype, name):
    m, kdim = a.shape
    n = b.shape[1]
    tm, tn, tk = min(MM_ROW_TILE, m), _lane_tile(n, MM_COL_TILE), _k_tile(kdim)
    nk = kdim // tk

    def body(a_ref, b_ref, o_ref, acc_ref):
        p = _nn(a_ref[...], b_ref[...])
        if nk == 1:
            o_ref[...] = p.astype(o_ref.dtype)
        else:
            k = pl.program_id(2)

            @pl.when(k == 0)
            def _():
                acc_ref[...] = p

            @pl.when(k > 0)
            def _():
                acc_ref[...] += p

            @pl.when(k == nk - 1)
            def _():
                o_ref[...] = acc_ref[...].astype(o_ref.dtype)

    return pl.pallas_call(
        body, name=name, grid=(n // tn, m // tm, nk),
        out_shape=jax.ShapeDtypeStruct((m, n), out_dtype),
        in_specs=[pl.BlockSpec((tm, tk), lambda j, i, k: (i, k)), pl.BlockSpec((tk, tn), lambda j, i, k: (k, j))],
        out_specs=pl.BlockSpec((tm, tn), lambda j, i, k: (i, j)),
        scratch_shapes=[pltpu.VMEM((tm, tn), F32)],
        compiler_params=_cparams(("parallel", "parallel", "arbitrary")),
    )(a, b)


def mm_nt(a, b, out_dtype, name):
    m, kdim = a.shape
    n = b.shape[0]
    tm, tn, tk = min(MM_ROW_TILE, m), _lane_tile(n, MM_COL_TILE), _k_tile(kdim)
    nk = kdim // tk

    def body(a_ref, b_ref, o_ref, acc_ref):
        p = _nt(a_ref[...], b_ref[...])
        if nk == 1:
            o_ref[...] = p.astype(o_ref.dtype)
        else:
            k = pl.program_id(2)

            @pl.when(k == 0)
            def _():
                acc_ref[...] = p

            @pl.when(k > 0)
            def _():
                acc_ref[...] += p

            @pl.when(k == nk - 1)
            def _():
                o_ref[...] = acc_ref[...].astype(o_ref.dtype)

    return pl.pallas_call(
        body, name=name, grid=(n // tn, m // tm, nk),
        out_shape=jax.ShapeDtypeStruct((m, n), out_dtype),
        in_specs=[pl.BlockSpec((tm, tk), lambda j, i, k: (i, k)), pl.BlockSpec((tn, tk), lambda j, i, k: (j, k))],
        out_specs=pl.BlockSpec((tm, tn), lambda j, i, k: (i, j)),
        scratch_shapes=[pltpu.VMEM((tm, tn), F32)],
        compiler_params=_cparams(("parallel", "parallel", "arbitrary")),
    )(a, b)


def mm_nn_exchange(a, b, out_dtype, part, name):
    kblocks, m, kb = a.shape
    kdim = kblocks * kb
    n = b.shape[1]
    tm, tn, tk = min(MM_ROW_TILE, m), _lane_tile(n, MM_COL_TILE), _k_tile(kdim)
    gn, gm, nk = n // tn, m // tm, kdim // tk
    per_step = tk // kb

    def body(a_ref, b_ref, part_ref, o_ref, parts_ref, acc_ref, send_sems, recv_sems, local_sem):
        j, i, k = pl.program_id(0), pl.program_id(1), pl.program_id(2)
        xchg_start, xchg_wait = _chip_exchange(part_ref, parts_ref, send_sems, recv_sems, local_sem)

        @pl.when((j == 0) & (i == 0) & (k == 0))
        def _():
            xchg_start()

        p = _nn(a_ref[0], b_ref[0:kb, :])
        for c in range(1, per_step):
            p = p + _nn(a_ref[c], b_ref[c * kb:(c + 1) * kb, :])

        @pl.when(k == 0)
        def _():
            acc_ref[...] = p

        @pl.when(k > 0)
        def _():
            acc_ref[...] += p

        @pl.when(k == nk - 1)
        def _():
            o_ref[...] = acc_ref[...].astype(o_ref.dtype)

        @pl.when((j == gn - 1) & (i == gm - 1) & (k == nk - 1))
        def _():
            xchg_wait()

    hbm = pl.BlockSpec(memory_space=pl.ANY)
    return pl.pallas_call(
        body, name=name, grid=(gn, gm, nk),
        out_shape=[jax.ShapeDtypeStruct((m, n), out_dtype), jax.ShapeDtypeStruct(part.shape, part.dtype)],
        in_specs=[pl.BlockSpec((per_step, tm, kb), lambda j, i, k: (k, i, 0)),
                  pl.BlockSpec((tk, tn), lambda j, i, k: (k, j)), hbm],
        out_specs=[pl.BlockSpec((tm, tn), lambda j, i, k: (i, j)), hbm],
        scratch_shapes=[pltpu.VMEM((tm, tn), F32)] + CHIP_SEMS,
        compiler_params=_cparams(("arbitrary", "arbitrary", "arbitrary")),
    )(a, b, part)


def mm_nt_gather(a, b, out_dtype, shard, name):
    m, kdim = a.shape
    n = b.shape[0]
    tm, tn = min(MM_ROW_TILE, m), 1024
    assert kdim == 1024
    gj = m // tm
    nsteps = (n // tn) * gj
    forward_step = max(nsteps - 2, 0)

    def body(a_ref, b_ref, x_ref, o_ref, g_ref, send_sems, recv_sems, local_sem):
        step = pl.program_id(0) * gj + pl.program_id(1)
        x, y, c = _my_pos()
        me, sibling = (x, y, c), (x, y, 1 - c)
        chips = [(1 - x, y), (x, 1 - y), (1 - x, 1 - y)]

        def slot(pos):
            return g_ref.at[_flat(pos)]

        def copy(k, block, to, src=None):
            return pltpu.make_async_remote_copy(slot(block) if src is None else src, slot(block), send_sems.at[k],
                                                recv_sems.at[k], device_id=to, device_id_type=MESH_ID)

        mine = pltpu.make_async_copy(x_ref, slot(me), local_sem)
        first = [copy(0, me, sibling, src=x_ref)]
        first += [copy(1 + j, me, (*chip, c), src=x_ref) for j, chip in enumerate(chips)]
        passed = [copy(4 + j, (*chip, c), sibling) for j, chip in enumerate(chips)]

        @pl.when(step == 0)
        def _():
            mine.start()
            for cp in first:
                cp.start()

        rows = pl.ds(pl.multiple_of(pl.program_id(1) * tm, tm), tm)
        o_ref[...] = _nt(a_ref[rows, :], b_ref[...]).astype(o_ref.dtype)

        @pl.when(step == forward_step)
        def _():
            for j, chip in enumerate(chips):
                copy(1 + j, (*chip, c), me).wait_recv()
                passed[j].start()

        @pl.when(step == nsteps - 1)
        def _():
            copy(0, sibling, me).wait_recv()
            for j, chip in enumerate(chips):
                copy(4 + j, (*chip, 1 - c), me).wait_recv()
            for cp in first + passed:
                cp.wait_send()
            mine.wait()

    return pl.pallas_call(
        body, name=name, grid=(n // tn, gj),
        out_shape=[jax.ShapeDtypeStruct((m, n), out_dtype), jax.ShapeDtypeStruct((N_DEV,) + shard.shape, shard.dtype)],
        in_specs=[pl.BlockSpec(memory_space=pltpu.VMEM), pl.BlockSpec((tn, kdim), lambda j, i: (j, 0)),
                  pl.BlockSpec(memory_space=pl.ANY)],
        out_specs=[pl.BlockSpec((tm, tn), lambda j, i: (i, j)), pl.BlockSpec(memory_space=pl.ANY)],
        scratch_shapes=[pltpu.SemaphoreType.DMA((N_DEV - 1,)), pltpu.SemaphoreType.DMA((N_DEV - 1,)),
                        pltpu.SemaphoreType.DMA],
        compiler_params=_cparams(("arbitrary", "arbitrary")),
    )(a, b, shard)


def mm_tn(a, b, name):
    tt, tn = min(MM_TOKEN_TILE, b.shape[0]), _lane_tile(b.shape[1], MM_COL_TILE)
    tka = _lane_tile(a.shape[0] * a.shape[2] if a.ndim == 3 else a.shape[1], 1024)
    if a.ndim == 3:
        t, ka = a.shape[1], a.shape[0] * a.shape[2]
        a_spec = pl.BlockSpec((None, tt, tka), lambda i, j, s: (i, s, 0))
    else:
        t, ka = a.shape
        a_spec = pl.BlockSpec((tt, tka), lambda i, j, s: (s, i))
    n = b.shape[1]
    nt = t // tt

    def body(a_ref, b_ref, o_ref, *acc):
        p = _tn(a_ref[...], b_ref[...])
        if nt == 1:
            o_ref[...] = p.astype(o_ref.dtype)
        else:
            acc_ref, s = acc[0], pl.program_id(2)

            @pl.when(s == 0)
            def _():
                acc_ref[...] = p

            @pl.when(s > 0)
            def _():
                acc_ref[...] += p

            @pl.when(s == nt - 1)
            def _():
                o_ref[...] = acc_ref[...].astype(o_ref.dtype)

    return pl.pallas_call(
        body, name=name, grid=(ka // tka, n // tn, nt),
        out_shape=jax.ShapeDtypeStruct((ka, n), BF16),
        in_specs=[a_spec, pl.BlockSpec((tt, tn), lambda i, j, s: (s, j))],
        out_specs=pl.BlockSpec((tka, tn), lambda i, j, s: (i, j)),
        scratch_shapes=[] if nt == 1 else [pltpu.VMEM((tka, tn), F32)],
        compiler_params=_cparams(("parallel", "parallel", "arbitrary")),
    )(a, b)


def _tile(t, cap):
    return min(cap, t)


def ln_modulate(x, mod6, shift_row, scale_row, name):
    t = x.shape[0]
    tm = _tile(t, WIDE_ROW_TILE)

    def body(x_ref, mod_ref, o_ref):
        xh, _ = _ln(x_ref[...])
        sc = mod_ref[scale_row:scale_row + 1, :]
        sh = mod_ref[shift_row:shift_row + 1, :]
        o_ref[...] = (xh * (1.0 + sc) + sh).astype(BF16)

    return pl.pallas_call(
        body, name=name, grid=(t // tm,),
        out_shape=jax.ShapeDtypeStruct((t, D), BF16),
        in_specs=[pl.BlockSpec((tm, D), lambda i: (i, 0)), pl.BlockSpec((6, D), lambda i: (0, 0))],
        out_specs=pl.BlockSpec((tm, D), lambda i: (i, 0)),
        compiler_params=_cparams(("parallel",)),
    )(x, mod6)


def resid_ln(x, h, mod6, gate_row, ln_g, ln_b, name):
    t = x.shape[0]
    tm = _tile(t, WIDE_ROW_TILE)

    def body(x_ref, h_ref, mod_ref, g_ref, b_ref, o_ref):
        r = ALPHA * x_ref[...] + mod_ref[gate_row:gate_row + 1, :] * h_ref[...]
        rh, _ = _ln(r)
        o_ref[...] = rh * g_ref[...] + b_ref[...]

    row = pl.BlockSpec((tm, D), lambda i: (i, 0))
    vec = pl.BlockSpec((1, D), lambda i: (0, 0))
    return pl.pallas_call(
        body, name=name, grid=(t // tm,),
        out_shape=jax.ShapeDtypeStruct((t, D), F32),
        in_specs=[row, row, pl.BlockSpec((6, D), lambda i: (0, 0)), vec, vec],
        out_specs=row,
        compiler_params=_cparams(("parallel",)),
    )(x, h, mod6, ln_g, ln_b)


def resid_ln_bwd(x, h, mod6, gate_row, ln_g, ln_b, cot, with_loss, name):
    t = x.shape[0]
    tm = _tile(t, ROW_TILE)

    def body(x_ref, h_ref, mod_ref, g_ref, b_ref, c_ref, dh_ref, dx_ref, acc_ref):
        @pl.when(pl.program_id(0) == 0)
        def _():
            acc_ref[...] = jnp.zeros_like(acc_ref)

        gate = mod_ref[gate_row:gate_row + 1, :]
        hv = h_ref[...]
        r = ALPHA * x_ref[...] + gate * hv
        rh, rstd = _ln(r)
        lng = g_ref[...]
        if with_loss:
            diff = rh * lng + b_ref[...] - c_ref[...]
            dxo = diff * (1.0 / D)
            lsum = jnp.sum(_colsum(diff * diff), axis=-1, keepdims=True) * (0.5 / D)
            acc_ref[3:4, :] += jnp.broadcast_to(lsum, (1, D))
        else:
            dxo = c_ref[...]
        acc_ref[1:2, :] += _colsum(dxo * rh)
        acc_ref[2:3, :] += _colsum(dxo)
        dr = _ln_bwd(dxo * lng, rh, rstd)
        acc_ref[0:1, :] += _colsum(dr * hv)
        dh_ref[...] = (gate * dr).astype(BF16)
        dx_ref[...] = ALPHA * dr

    row = pl.BlockSpec((tm, D), lambda i: (i, 0))
    vec = pl.BlockSpec((1, D), lambda i: (0, 0))
    return pl.pallas_call(
        body, name=name, grid=(t // tm,),
        out_shape=[jax.ShapeDtypeStruct((t, D), BF16), jax.ShapeDtypeStruct((t, D), F32),
                   jax.ShapeDtypeStruct((8, D), F32)],
        in_specs=[row, row, pl.BlockSpec((6, D), lambda i: (0, 0)), vec, vec, row],
        out_specs=[row, row, pl.BlockSpec((8, D), lambda i: (0, 0))],
        compiler_params=_cparams(("arbitrary",)),
    )(x, h, mod6, ln_g, ln_b, cot)


def ln_modulate_bwd(x, du, mod6, scale_row, dx_part, name):
    t = x.shape[0]
    tm = _tile(t, ROW_TILE)

    def body(x_ref, du_ref, mod_ref, dp_ref, dx_ref, acc_ref):
        @pl.when(pl.program_id(0) == 0)
        def _():
            acc_ref[...] = jnp.zeros_like(acc_ref)

        xh, rstd = _ln(x_ref[...])
        du_v = du_ref[...]
        sc = mod_ref[scale_row:scale_row + 1, :]
        acc_ref[0:1, :] += _colsum(du_v * xh)
        acc_ref[1:2, :] += _colsum(du_v)
        dx_ref[...] = dp_ref[...] + _ln_bwd(du_v * (1.0 + sc), xh, rstd)

    row = pl.BlockSpec((tm, D), lambda i: (i, 0))
    return pl.pallas_call(
        body, name=name, grid=(t // tm,),
        out_shape=[jax.ShapeDtypeStruct((t, D), F32), jax.ShapeDtypeStruct((8, D), F32)],
        in_specs=[row, row, pl.BlockSpec((6, D), lambda i: (0, 0)), row],
        out_specs=[row, pl.BlockSpec((8, D), lambda i: (0, 0))],
        compiler_params=_cparams(("arbitrary",)),
    )(x, du, mod6, dx_part)


def merge_gates(ya, yb, proj):
    t = ya.shape[0]
    tm = _tile(t, WIDE_ROW_TILE)

    def body(ya_ref, yb_ref, ga_ref, gb_ref, o_ref):
        o_ref[...] = (_sigmoid(ga_ref[...]) * ya_ref[...].astype(F32) +
                      _sigmoid(gb_ref[...]) * yb_ref[...].astype(F32)).astype(BF16)

    row = pl.BlockSpec((tm, D), lambda i: (i, 0))
    return pl.pallas_call(
        body, name="merge_gates", grid=(t // tm,),
        out_shape=jax.ShapeDtypeStruct((t, D), BF16),
        in_specs=[row, row, pl.BlockSpec((tm, D), lambda i: (i, GATE_BLOCK0)),
                  pl.BlockSpec((tm, D), lambda i: (i, GATE_BLOCK0 + 1))],
        out_specs=row,
        compiler_params=_cparams(("parallel",)),
    )(ya, yb, proj, proj)


def merge_gates_bwd(dm, ya, yb, proj):
    t = ya.shape[0]
    tm = _tile(t, ROW_TILE)

    def body(dm_ref, ya_ref, yb_ref, ga_ref, gb_ref, dya_ref, dyb_ref, dp_ref):
        dmv = dm_ref[...].astype(F32)
        sa = _sigmoid(ga_ref[...])
        sb = _sigmoid(gb_ref[...])
        dya_ref[...] = (dmv * sa).astype(BF16)
        dyb_ref[...] = (dmv * sb).astype(BF16)
        dp_ref[0] = (dmv * ya_ref[...].astype(F32) * sa * (1.0 - sa)).astype(BF16)
        dp_ref[1] = (dmv * yb_ref[...].astype(F32) * sb * (1.0 - sb)).astype(BF16)

    row = pl.BlockSpec((tm, D), lambda i: (i, 0))
    return pl.pallas_call(
        body, name="merge_gates_bwd", grid=(t // tm,),
        out_shape=[jax.ShapeDtypeStruct((t, D), BF16)] * 2 + [jax.ShapeDtypeStruct((N_PROJ // D, t, D), BF16)],
        in_specs=[row, row, row, pl.BlockSpec((tm, D), lambda i: (i, GATE_BLOCK0)),
                  pl.BlockSpec((tm, D), lambda i: (i, GATE_BLOCK0 + 1))],
        out_specs=[row, row, pl.BlockSpec((2, tm, D), lambda i: (GATE_BLOCK0 // 2, i, 0))],
        compiler_params=_cparams(("parallel",)),
    )(dm, ya, yb, proj, proj)


FF_CHUNK = 1408


def swiglu_act(gu):
    t = gu.shape[0]
    tm = _tile(t, FFN_ROW_TILE)

    def body(gu_ref, o_ref):
        for j in range(D_FF // FF_CHUNK):
            cs = slice(j * FF_CHUNK, (j + 1) * FF_CHUNK)
            g = gu_ref[:, cs].astype(F32)
            u = gu_ref[:, D_FF + j * FF_CHUNK:D_FF + (j + 1) * FF_CHUNK].astype(F32)
            o_ref[:, cs] = (g * _sigmoid(g) * u).astype(BF16)

    return pl.pallas_call(
        body, name="swiglu_act", grid=(t // tm,),
        out_shape=jax.ShapeDtypeStruct((t, D_FF), BF16),
        in_specs=[pl.BlockSpec((tm, 2 * D_FF), lambda i: (i, 0))],
        out_specs=pl.BlockSpec((tm, D_FF), lambda i: (i, 0)),
        compiler_params=_cparams(("parallel",)),
    )(gu)


def swiglu_act_bwd(gu, dact):
    t = gu.shape[0]
    tm = _tile(t, FFN_ROW_TILE)

    def body(gu_ref, da_ref, o_ref):
        for j in range(D_FF // FF_CHUNK):
            cs = slice(j * FF_CHUNK, (j + 1) * FF_CHUNK)
            us = slice(D_FF + j * FF_CHUNK, D_FF + (j + 1) * FF_CHUNK)
            g = gu_ref[:, cs].astype(F32)
            u = gu_ref[:, us].astype(F32)
            da = da_ref[:, cs].astype(F32)
            s = _sigmoid(g)
            o_ref[:, cs] = (da * u * _dsilu(g, s)).astype(BF16)
            o_ref[:, us] = (da * g * s).astype(BF16)

    return pl.pallas_call(
        body, name="swiglu_act_bwd", grid=(t // tm,),
        out_shape=jax.ShapeDtypeStruct((t, 2 * D_FF), BF16),
        in_specs=[pl.BlockSpec((tm, 2 * D_FF), lambda i: (i, 0)), pl.BlockSpec((tm, D_FF), lambda i: (i, 0))],
        out_specs=pl.BlockSpec((tm, 2 * D_FF), lambda i: (i, 0)),
        compiler_params=_cparams(("parallel",)),
    )(gu, dact)


def _hgrn_chunk_terms(q, fl, lbv, tril_f):
    sig = _sigmoid(fl)
    f = lbv + (1.0 - lbv) * sig
    lam = jnp.log(f)
    k = 1.0 - f
    sq = _sigmoid(q)
    qt = q * sq * Q_SCALE
    bc = _sel(_nn, lam, tril_f, 3, x_first=False)
    bmid = bc[CHUNK // 2 - 1:CHUNK // 2, :]
    bl = bc[CHUNK - 1:CHUNK, :]
    eq = jnp.exp(jnp.minimum(bc - bmid, EXP_CLIP))
    ek = jnp.exp(jnp.minimum(bmid - bc, EXP_CLIP))
    eb = jnp.exp(bc)
    ekl = jnp.exp(bl - bc)
    ebl = jnp.exp(bl)
    return sig, f, k, sq, qt, eq, ek, eb, ekl, ebl


def hgrn_fwd(proj, lb, gnorm):
    t = proj.shape[0]
    tb = _tile(t, TOKEN_BLOCK)
    ncb = tb // CHUNK

    hps = HGRN_HEADS_PER_STEP
    wide = hps * HK

    def body(q_ref, f_ref, i_ref, g_ref, lb_ref, gn_ref, oa_ref, oraw_ref, st_ref, state):
        @pl.when(pl.program_id(1) == 0)
        def _():
            state[...] = jnp.zeros_like(state)

        gn = gn_ref[...]
        mask = _tri(CHUNK)
        tril_f = mask.astype(BF16)

        def chunk(c, carry):
            sl = pl.ds(pl.multiple_of(c * CHUNK, CHUNK), CHUNK)
            for hh in range(hps):
                ln = slice(hh * HK, (hh + 1) * HK)
                q, fl, v, g = q_ref[sl, ln], f_ref[sl, ln], i_ref[sl, ln], g_ref[sl, ln]
                sig, f, k, sq, qt, eq, ek, eb, ekl, ebl = _hgrn_chunk_terms(q, fl, lb_ref[:, ln], tril_f)
                a = jnp.where(mask, _nt((qt * eq).astype(BF16), (k * ek).astype(BF16)), 0.0)
                st = state[hh]
                st_ref[hh, c] = st
                vb = v.astype(BF16)
                o = _nn(a.astype(BF16), vb) + _nt((qt * eb).astype(BF16), st.astype(BF16))
                state[hh] = st * ebl + _tn(vb, (k * ekl).astype(BF16))
                oraw_ref[sl, ln] = o
                rn = o * lax.rsqrt(jnp.mean(o * o, axis=-1, keepdims=True) + RMS_EPS)
                oa_ref[sl, ln] = (rn * gn * g * _sigmoid(g)).astype(BF16)
            return carry

        lax.fori_loop(0, ncb, chunk, 0, unroll=min(CHUNK_UNROLL, ncb))

    def col(block):
        return pl.BlockSpec((tb, wide), lambda h, j: (j, block * (N_HEADS_A // hps) + h))

    return pl.pallas_call(
        body, name="hgrn_fwd", grid=(N_HEADS_A // hps, t // tb),
        out_shape=[jax.ShapeDtypeStruct((t, D), BF16), jax.ShapeDtypeStruct((t, D), F32),
                   jax.ShapeDtypeStruct((N_HEADS_A, t // CHUNK, HK, HK), F32)],
        in_specs=[col(0), col(1), col(2), col(3), pl.BlockSpec((1, wide), lambda h, j: (0, h)),
                  pl.BlockSpec((1, HK), lambda h, j: (0, 0))],
        out_specs=[pl.BlockSpec((tb, wide), lambda h, j: (j, h)), pl.BlockSpec((tb, wide), lambda h, j: (j, h)),
                   pl.BlockSpec((hps, ncb, HK, HK), lambda h, j: (h, j, 0, 0))],
        scratch_shapes=[pltpu.VMEM((hps, HK, HK), F32)],
        compiler_params=_cparams(("parallel", "arbitrary")),
    )(proj, proj, proj, proj, lb, gnorm)


def hgrn_bwd(proj, lb, gnorm, o_raw, doa, states, give, dproj):
    t = proj.shape[0]
    tb = _tile(t, TOKEN_BLOCK)
    ncb = tb // CHUNK
    nb = t // tb
    hps = HGRN_HEADS_PER_STEP
    wide = hps * HK

    def body(q_ref, f_ref, i_ref, g_ref, lb_ref, gn_ref, oraw_ref, doa_ref, st_ref, give_ref, dp_in_ref,
             dp_ref, dlb_ref, dgn_ref, got_ref, dstate, send_sem, recv_sem):
        h, j = pl.program_id(0), pl.program_id(1)
        swap_start, swap_wait = _sibling_exchange(give_ref, got_ref, send_sem, recv_sem)

        @pl.when((h == 0) & (j == 0))
        def _():
            swap_start()

        @pl.when(j == 0)
        def _():
            dstate[...] = jnp.zeros_like(dstate)
            dlb_ref[...] = jnp.zeros_like(dlb_ref)

        @pl.when((j == 0) & (h == 0))
        def _():
            dgn_ref[...] = jnp.zeros_like(dgn_ref)

        gn = gn_ref[...]
        mask = _tri(CHUNK)
        mask_t = _tri(CHUNK, upper=True)
        tril_f = mask.astype(BF16)
        triu_f = mask_t.astype(BF16)

        def chunk(i, c0):
            c = ncb - 1 - i
            sl = pl.ds(pl.multiple_of(c * CHUNK, CHUNK), CHUNK)
            for hh in range(hps):
                ln = slice(hh * HK, (hh + 1) * HK)
                q, fl, v, g = q_ref[sl, ln], f_ref[sl, ln], i_ref[sl, ln], g_ref[sl, ln]
                lbv = lb_ref[:, ln]
                sig, f, k, sq, qt, eq, ek, eb, ekl, ebl = _hgrn_chunk_terms(q, fl, lbv, tril_f)
                qe = (qt * eq).astype(BF16)
                ke = (k * ek).astype(BF16)
                st32 = st_ref[hh, c]
                st = st32.astype(BF16)
                dst = dstate[hh]
                dstb = dst.astype(BF16)
                o = oraw_ref[sl, ln]
                rstd = lax.rsqrt(jnp.mean(o * o, axis=-1, keepdims=True) + RMS_EPS)
                rn = o * rstd
                sgm = _sigmoid(g)
                sg = g * sgm
                doa_v = doa_ref[sl, ln]
                drn = doa_v * gn * sg
                dgn_ref[...] += _colsum(doa_v * rn * sg)
                dp_ref[3, sl, ln] = (doa_v * rn * gn * _dsilu(g, sgm)).astype(BF16)
                do = rstd * (drn - rn * jnp.mean(drn * rn, axis=-1, keepdims=True))
                dob = do.astype(BF16)
                vb = v.astype(BF16)
                da = jnp.where(mask, _nt(dob, vb), 0.0).astype(BF16)
                da_t = jnp.where(mask_t, _nt(vb, dob), 0.0).astype(BF16)
                a_t = jnp.where(mask_t, _nt(ke, qe), 0.0).astype(BF16)
                kl = (k * ekl).astype(BF16)
                qb = (qt * eb).astype(BF16)
                dq_in = _nn(da, ke)
                dk_in = _nn(da_t, qe)
                dq_out = eb * _nn(dob, st)
                dk_out = ekl * _nn(vb, dstb)
                dqt = eq * dq_in + dq_out
                dk = ek * dk_in + dk_out
                dv = _nn(a_t, dob) + _nt(kl, dstb)
                dstate[hh] = dst * ebl + _tn(dob, qb)
                dbig = qe.astype(F32) * dq_in - ke.astype(F32) * dk_in + qt * dq_out - k * dk_out
                beyond = _colsum(k * dk_out) + ebl * _colsum(dst * st32)
                dlam = _sel(_nn, dbig, triu_f, 3, x_first=False) + beyond
                df = dlam / f - dk
                dp_ref[1, sl, ln] = (df * (1.0 - lbv) * sig * (1.0 - sig)).astype(BF16)
                dlb_ref[:, ln] += _colsum(df * (1.0 - sig))
                dp_ref[0, sl, ln] = (dqt * Q_SCALE * _dsilu(q, sq)).astype(BF16)
                dp_ref[2, sl, ln] = dv.astype(BF16)
            return c0

        lax.fori_loop(0, ncb, chunk, 0, unroll=min(CHUNK_UNROLL, ncb))

        @pl.when((h == N_HEADS_A // hps - 1) & (j == nb - 1))
        def _():
            swap_wait()

    def col(block):
        return pl.BlockSpec((tb, wide), lambda h, j: (nb - 1 - j, block * (N_HEADS_A // hps) + h))

    hcol = pl.BlockSpec((tb, wide), lambda h, j: (nb - 1 - j, h))
    hbm = pl.BlockSpec(memory_space=pl.ANY)
    return pl.pallas_call(
        body, name="hgrn_bwd", grid=(N_HEADS_A // hps, nb),
        out_shape=[jax.ShapeDtypeStruct(dproj.shape, dproj.dtype), jax.ShapeDtypeStruct((1, D), F32),
                   jax.ShapeDtypeStruct((1, HK), F32), jax.ShapeDtypeStruct(give.shape, give.dtype)],
        in_specs=[col(0), col(1), col(2), col(3), pl.BlockSpec((1, wide), lambda h, j: (0, h)),
                  pl.BlockSpec((1, HK), lambda h, j: (0, 0)), hcol, hcol,
                  pl.BlockSpec((hps, ncb, HK, HK), lambda h, j: (h, nb - 1 - j, 0, 0)), hbm, hbm],
        out_specs=[pl.BlockSpec((4, tb, wide), lambda h, j: (0, nb - 1 - j, h)),
                   pl.BlockSpec((1, wide), lambda h, j: (0, h)), pl.BlockSpec((1, HK), lambda h, j: (0, 0)), hbm],
        input_output_aliases={10: 0},
        scratch_shapes=[pltpu.VMEM((hps, HK, HK), F32)] + SIBLING_SEMS,
        compiler_params=_cparams(("arbitrary", "arbitrary")),
    )(proj, proj, proj, proj, lb, gnorm, o_raw, doa, states, give, dproj)


CONV_BLOCK0 = 6
CONV_TAPS = 4
HALO = 8


def conv_fwd(proj, conv_w, conv_b):
    t = proj.shape[0]
    tm = _tile(t, ROW_TILE)
    r = tm // HALO

    def body(x_ref, halo_ref, w_ref, b_ref, o_ref, ds_ref):
        i = pl.program_id(1)
        halo = jnp.where(i > 0, halo_ref[...], 0.0)
        ext = jnp.concatenate([halo, x_ref[...]], axis=0)
        pre = b_ref[...] + w_ref[CONV_TAPS - 1:CONV_TAPS, :] * ext[HALO:, :]
        for tap in range(CONV_TAPS - 1):
            pre = pre + w_ref[tap:tap + 1, :] * pltpu.roll(ext, CONV_TAPS - 1 - tap, axis=0)[HALO:, :]
        s = _sigmoid(pre)
        o_ref[...] = pre * s
        ds_ref[...] = _dsilu(pre, s).astype(BF16)

    blk = pl.BlockSpec((tm, D), lambda cb, i: (i, cb))
    return pl.pallas_call(
        body, name="conv_fwd", grid=(CONV_DIM // D, t // tm),
        out_shape=[jax.ShapeDtypeStruct((t, CONV_DIM), F32), jax.ShapeDtypeStruct((t, CONV_DIM), BF16)],
        in_specs=[pl.BlockSpec((tm, D), lambda cb, i: (i, CONV_BLOCK0 + cb)),
                  pl.BlockSpec((HALO, D), lambda cb, i: (jnp.maximum(i * r - 1, 0), CONV_BLOCK0 + cb)),
                  pl.BlockSpec((CONV_TAPS, D), lambda cb, i: (0, cb)), pl.BlockSpec((1, D), lambda cb, i: (0, cb))],
        out_specs=[blk, blk],
        compiler_params=_cparams(("parallel", "parallel")),
    )(proj, proj, conv_w, conv_b)


def conv_bwd(proj, dxc, dsilu, conv_w, dproj):
    t = proj.shape[0]
    tm = _tile(t, ROW_TILE)
    r = tm // HALO
    n = t // tm
    last_halo = t // HALO - 1

    def body(x_ref, prev_ref, d_ref, dnext_ref, s_ref, snext_ref, w_ref, dp_in_ref, dx_ref, dw_ref, db_ref):
        i = pl.program_id(1)

        @pl.when(i == 0)
        def _():
            dw_ref[...] = jnp.zeros_like(dw_ref)
            db_ref[...] = jnp.zeros_like(db_ref)

        dpre = jnp.concatenate([d_ref[...].astype(F32) * s_ref[...].astype(F32),
                                jnp.where(i < n - 1, dnext_ref[0:HALO, :].astype(F32) * snext_ref[0:HALO, :].astype(F32),
                                          0.0)], axis=0)
        dx = w_ref[CONV_TAPS - 1:CONV_TAPS, :] * dpre[:tm, :]
        for tap in range(CONV_TAPS - 1):
            back = CONV_TAPS - 1 - tap
            dx = dx + w_ref[tap:tap + 1, :] * pltpu.roll(dpre, tm + HALO - back, axis=0)[:tm, :]
        dx_ref[...] = dx.astype(BF16)
        dp = dpre[:tm, :]
        db_ref[...] += _colsum(dp)
        prev = jnp.where(i > 0, prev_ref[...], 0.0)
        ext = jnp.concatenate([prev, x_ref[...]], axis=0)
        dw_ref[CONV_TAPS - 1:CONV_TAPS, :] += _colsum(dp * ext[HALO:, :])
        for tap in range(CONV_TAPS - 1):
            dw_ref[tap:tap + 1, :] += _colsum(dp * pltpu.roll(ext, CONV_TAPS - 1 - tap, axis=0)[HALO:, :])

    blk = pl.BlockSpec((tm, D), lambda cb, i: (i, cb))
    nxt = pl.BlockSpec((2 * HALO, D), lambda cb, i: (jnp.minimum((i + 1) * (r // 2), last_halo // 2), cb))
    return pl.pallas_call(
        body, name="conv_bwd", grid=(CONV_DIM // D, n),
        out_shape=[jax.ShapeDtypeStruct(dproj.shape, dproj.dtype), jax.ShapeDtypeStruct((8, CONV_DIM), F32),
                   jax.ShapeDtypeStruct((1, CONV_DIM), F32)],
        in_specs=[pl.BlockSpec((tm, D), lambda cb, i: (i, CONV_BLOCK0 + cb)),
                  pl.BlockSpec((HALO, D), lambda cb, i: (jnp.maximum(i * r - 1, 0), CONV_BLOCK0 + cb)),
                  blk, nxt, blk, nxt,
                  pl.BlockSpec((CONV_TAPS, D), lambda cb, i: (0, cb)), pl.BlockSpec(memory_space=pl.ANY)],
        out_specs=[pl.BlockSpec((None, tm, D), lambda cb, i: (CONV_BLOCK0 + cb, i, 0)),
                   pl.BlockSpec((8, D), lambda cb, i: (0, cb)), pl.BlockSpec((1, D), lambda cb, i: (0, cb))],
        input_output_aliases={7: 0},
        compiler_params=_cparams(("parallel", "arbitrary")),
    )(proj, proj, dxc, dxc, dsilu, dsilu, conv_w, dproj)


def dt_fill(ddt, dproj):
    t = ddt.shape[0]
    tm = _tile(t, WIDE_ROW_TILE)
    w = ddt.shape[1]

    def body(d_ref, dp_in_ref, o_ref):
        o_ref[:, :w] = d_ref[...]
        o_ref[:, w:] = jnp.zeros((tm, D - w), o_ref.dtype)

    return pl.pallas_call(
        body, name="dt_fill", grid=(t // tm,),
        out_shape=jax.ShapeDtypeStruct(dproj.shape, dproj.dtype),
        in_specs=[pl.BlockSpec((tm, w), lambda i: (i, 0)), pl.BlockSpec(memory_space=pl.ANY)],
        out_specs=pl.BlockSpec((None, tm, D), lambda i: (DT_COL_BLOCK, i, 0)),
        input_output_aliases={1: 0},
        compiler_params=_cparams(("parallel",)),
    )(ddt, dproj)


Z_BLOCK0 = 8
DT_COL_BLOCK = 9
DT_BLOCK0 = 8 * DT_COL_BLOCK
GATE_BLOCK0 = 10
B_BLOCK0 = 16
C_BLOCK0 = 20


def _head_expand():
    e = np.zeros((N_STATE, GROUP_W), np.float32)
    for hh in range(HEADS_PER_GROUP):
        e[hh, hh * HEAD_P:(hh + 1) * HEAD_P] = 1.0
    return jnp.asarray(e, BF16)


def _ssd_chunk_terms(dt, bias, alog, expand, tril_f, eye):
    dtb = dt + bias
    delta = jnp.maximum(dtb, 0.0) + jnp.log(1.0 + jnp.exp(-jnp.abs(dtb)))
    ea = jnp.exp(alog)
    a = -ea * delta
    acum = _sel(_nn, a, tril_f, 3, x_first=False)
    delta_e = _sel(_nn, delta, expand, 2)
    acum_e = _sel(_nn, acum, expand, 2)
    acum_t = _sel(_nt, acum, eye, 3, x_first=False)
    return dtb, delta, ea, a, acum, delta_e, acum_e, acum_t


def ssd_fwd(proj, xc, alog4, bias4, dskip4, wnorm, expand):
    t = proj.shape[0]
    tb = _tile(t, TOKEN_BLOCK)
    ncb = tb // SSD_CHUNK

    def body(xs_ref, b_ref, c_ref, dt_ref, z_ref, alog_ref, bias_ref, dsk_ref, wn_ref, e_ref, ob_ref, st_ref, state):
        @pl.when(pl.program_id(1) == 0)
        def _():
            state[...] = jnp.zeros_like(state)

        expand = e_ref[...]
        mask = _tri(SSD_CHUNK)
        tril_f = mask.astype(BF16)
        eye = (lax.broadcasted_iota(jnp.int32, (N_STATE, N_STATE), 0) ==
               lax.broadcasted_iota(jnp.int32, (N_STATE, N_STATE), 1)).astype(BF16)
        alog, bias = alog_ref[0], bias_ref[0]
        d_e = _sel(_nn, jnp.broadcast_to(dsk_ref[0], (8, N_STATE)), expand, 3)[0:1, :]
        wn = wn_ref[...]

        def chunk(c, carry):
            sl = pl.ds(pl.multiple_of(c * SSD_CHUNK, SSD_CHUNK), SSD_CHUNK)
            xs, bm, cm, dt, z = xs_ref[sl, :], b_ref[sl, :], c_ref[sl, :], dt_ref[sl, :], z_ref[sl, :]
            dtb, delta, ea, a, acum, delta_e, acum_e, acum_t = _ssd_chunk_terms(dt, bias, alog, expand, tril_f, eye)
            alast_e = acum_e[SSD_CHUNK - 1:SSD_CHUNK, :]
            xd = xs * delta_e
            xdb = xd.astype(BF16)
            cb_, bb_ = cm.astype(BF16), bm.astype(BF16)
            cbm = _nt(cb_, bb_)
            ys = []
            for hh in range(HEADS_PER_GROUP):
                lh = jnp.where(mask, jnp.exp(jnp.minimum(acum[:, hh:hh + 1] - acum_t[hh:hh + 1, :], 0.0)), 0.0)
                ys.append(_nn((cbm * lh).astype(BF16), xdb[:, hh * HEAD_P:(hh + 1) * HEAD_P]))
            st = state[...]
            st_ref[0, c] = st
            y = jnp.concatenate(ys, axis=1) + _nn(cb_, st.astype(BF16)) * jnp.exp(acum_e) + xs * d_e
            state[...] = st * jnp.exp(alast_e) + _tn(bb_, (xd * jnp.exp(alast_e - acum_e)).astype(BF16))
            yg = y * z * _sigmoid(z)
            ob_ref[sl, :] = (yg * lax.rsqrt(jnp.mean(yg * yg, axis=-1, keepdims=True) + RMS_EPS) * wn).astype(BF16)
            return carry

        lax.fori_loop(0, ncb, chunk, 0, unroll=min(CHUNK_UNROLL, ncb))

    small = pl.BlockSpec((1, 1, N_STATE), lambda g, j: (g, 0, 0))
    return pl.pallas_call(
        body, name="ssd_fwd", grid=(N_GROUPS, t // tb),
        out_shape=[jax.ShapeDtypeStruct((t, B_INNER), BF16),
                   jax.ShapeDtypeStruct((N_GROUPS, t // SSD_CHUNK, N_STATE, GROUP_W), F32)],
        in_specs=[pl.BlockSpec((tb, GROUP_W), lambda g, j: (j, g)),
                  pl.BlockSpec((tb, N_STATE), lambda g, j: (j, B_BLOCK0 + g)),
                  pl.BlockSpec((tb, N_STATE), lambda g, j: (j, C_BLOCK0 + g)),
                  pl.BlockSpec((tb, N_STATE), lambda g, j: (j, DT_BLOCK0 + g)),
                  pl.BlockSpec((tb, GROUP_W), lambda g, j: (j, Z_BLOCK0 + g)),
                  small, small, small, pl.BlockSpec((1, GROUP_W), lambda g, j: (0, g)),
                  pl.BlockSpec((N_STATE, GROUP_W), lambda g, j: (0, 0))],
        out_specs=[pl.BlockSpec((tb, GROUP_W), lambda g, j: (j, g)),
                   pl.BlockSpec((1, ncb, N_STATE, GROUP_W), lambda g, j: (g, j, 0, 0))],
        scratch_shapes=[pltpu.VMEM((N_STATE, GROUP_W), F32)],
        compiler_params=_cparams(("parallel", "arbitrary")),
    )(xc, xc, xc, proj, proj, alog4, bias4, dskip4, wnorm, expand)


def ssd_bwd(proj, xc, alog4, bias4, dskip4, wnorm, expand, dob, states, part, dproj):
    t = proj.shape[0]
    tb = _tile(t, TOKEN_BLOCK)
    lc = min(SSD_CHUNK_BWD, tb)
    ncb = tb // lc
    nsaved = tb // SSD_CHUNK
    nb = t // tb

    def body(xs_ref, b_ref, c_ref, dt_ref, z_ref, alog_ref, bias_ref, dsk_ref, wn_ref, e_ref, dob_ref, st_ref, part_ref,
             dp_in_ref, dxs_ref, db_ref, dc_ref, dz_ref, ddt_ref, dwn_ref, dalog_ref, dbias_ref, ddsk_ref, parts_ref, dstate,
             send_sems, recv_sems, local_sem):
        xchg_start, xchg_wait = _chip_exchange(part_ref, parts_ref, send_sems, recv_sems, local_sem)

        @pl.when((pl.program_id(0) == 0) & (pl.program_id(1) == 0))
        def _():
            xchg_start()

        @pl.when(pl.program_id(1) == 0)
        def _():
            dstate[...] = jnp.zeros_like(dstate)
            dwn_ref[...] = jnp.zeros_like(dwn_ref)
            dalog_ref[...] = jnp.zeros_like(dalog_ref)
            dbias_ref[...] = jnp.zeros_like(dbias_ref)
            ddsk_ref[...] = jnp.zeros_like(ddsk_ref)

        expand = e_ref[...]
        mask = _tri(lc)
        mask_t = _tri(lc, upper=True)
        tril_f = mask.astype(BF16)
        triu_f = mask_t.astype(BF16)
        eye = (lax.broadcasted_iota(jnp.int32, (N_STATE, N_STATE), 0) ==
               lax.broadcasted_iota(jnp.int32, (N_STATE, N_STATE), 1)).astype(BF16)
        alog, bias = alog_ref[0], bias_ref[0]
        d_e = _sel(_nn, jnp.broadcast_to(dsk_ref[0], (8, N_STATE)), expand, 3)[0:1, :]
        wn = wn_ref[...]

        def chunk(i, c0):
            c = ncb - 1 - i
            sl = pl.ds(pl.multiple_of(c * lc, lc), lc)
            xs, bm, cm, dt, z = xs_ref[sl, :], b_ref[sl, :], c_ref[sl, :], dt_ref[sl, :], z_ref[sl, :]
            dtb, delta, ea, a, acum, delta_e, acum_e, acum_t = _ssd_chunk_terms(dt, bias, alog, expand, tril_f, eye)
            alast_e = acum_e[lc - 1:lc, :]
            eacum = jnp.exp(acum_e)
            wl = jnp.exp(alast_e - acum_e)
            xd = xs * delta_e
            xdb = xd.astype(BF16)
            cb_, bb_ = cm.astype(BF16), bm.astype(BF16)
            cbm = _nt(cb_, bb_)
            st32 = st_ref[0, c * (lc // SSD_CHUNK)]
            stb = st32.astype(BF16)
            dst = dstate[...]
            dstb = dst.astype(BF16)
            lhs, mixes, ys = [], [], []
            for hh in range(HEADS_PER_GROUP):
                col, row = acum[:, hh:hh + 1], acum_t[hh:hh + 1, :]
                lh = jnp.where(mask, jnp.exp(jnp.minimum(col - row, 0.0)), 0.0)
                mix = (cbm * lh).astype(BF16)
                lhs.append(lh)
                mixes.append(mix)
                ys.append(_nn(mix, xdb[:, hh * HEAD_P:(hh + 1) * HEAD_P]))
            y_in = jnp.concatenate(ys, axis=1)
            y_out = _nn(cb_, stb) * eacum
            y = y_in + y_out + xs * d_e
            sgz = _sigmoid(z)
            sz = z * sgz
            yg = y * sz
            rstd = lax.rsqrt(jnp.mean(yg * yg, axis=-1, keepdims=True) + RMS_EPS)
            nrm = yg * rstd
            dob_v = dob_ref[sl, :]
            dn = dob_v * wn
            dwn_ref[...] += _colsum(dob_v * nrm)
            dyg = rstd * (dn - nrm * jnp.mean(dn * nrm, axis=-1, keepdims=True))
            dy = dyg * sz
            dz_ref[sl, :] = (dyg * y * _dsilu(z, sgz)).astype(BF16)
            dyb = dy.astype(BF16)
            dxds = []
            dcb = jnp.zeros((lc, lc), F32)
            for hh in range(HEADS_PER_GROUP):
                hs = slice(hh * HEAD_P, (hh + 1) * HEAD_P)
                dy_h, x_h = dyb[:, hs], xdb[:, hs]
                dxds.append(_tn(mixes[hh], dy_h))
                dcb = dcb + _nt(dy_h, x_h) * lhs[hh]
            dcbb = dcb.astype(BF16)
            dye = (dy * eacum).astype(BF16)
            xw = (xd * wl).astype(BF16)
            dxd_in = jnp.concatenate(dxds, axis=1)
            dxd_out = wl * _nn(bb_, dstb)
            dxd = dxd_in + dxd_out
            dc_ref[sl, :] = (_nn(dcbb, bb_) + _nt(dye, stb)).astype(dc_ref.dtype)
            db_ref[sl, :] = (_tn(dcbb, cb_) + _nt(xw, dstb)).astype(db_ref.dtype)
            dstate[...] = dst * jnp.exp(alast_e) + _tn(cb_, dye)
            col_out = xd * dxd_out
            dac = _sel(_nt, dyb.astype(F32) * y_in - xdb.astype(F32) * dxd_in + dy * y_out - col_out, expand, 2)
            beyond = _colsum(col_out) + jnp.exp(alast_e) * _colsum(dst * st32)
            da = (_sel(_nn, dac, triu_f, 3, x_first=False) +
                  _sel(_nt, jnp.broadcast_to(beyond, (8, GROUP_W)), expand, 3)[0:1, :])
            ddelta = _sel(_nt, dxd * xs, expand, 2) - da * ea
            dalog_ref[0] += _colsum(da * a)
            ddtb = ddelta * _sigmoid(dtb)
            dbias_ref[0] += _colsum(ddtb)
            ddt_ref[sl, :] = ddtb.astype(BF16)
            ddsk_ref[0] += _sel(_nt, jnp.broadcast_to(_colsum(dy * xs), (8, GROUP_W)), expand, 3)[0:1, :]
            dxs_ref[sl, :] = (dxd * delta_e + dy * d_e).astype(dxs_ref.dtype)
            return c0

        lax.fori_loop(0, ncb, chunk, 0, unroll=min(CHUNK_UNROLL, ncb))

        @pl.when((pl.program_id(0) == N_GROUPS - 1) & (pl.program_id(1) == nb - 1))
        def _():
            xchg_wait()

    small = pl.BlockSpec((1, 1, N_STATE), lambda g, j: (g, 0, 0))
    wide = pl.BlockSpec((tb, GROUP_W), lambda g, j: (nb - 1 - j, g))
    narrow = pl.BlockSpec((tb, N_STATE), lambda g, j: (nb - 1 - j, g))
    hbm = pl.BlockSpec(memory_space=pl.ANY)
    return pl.pallas_call(
        body, name="ssd_bwd", grid=(N_GROUPS, nb),
        out_shape=[jax.ShapeDtypeStruct((t, B_INNER), BF16), jax.ShapeDtypeStruct((t, GROUP_W), BF16),
                   jax.ShapeDtypeStruct((t, GROUP_W), BF16), jax.ShapeDtypeStruct(dproj.shape, dproj.dtype),
                   jax.ShapeDtypeStruct((t, GROUP_W), BF16), jax.ShapeDtypeStruct((1, B_INNER), F32),
                   jax.ShapeDtypeStruct((N_GROUPS, 1, N_STATE), F32), jax.ShapeDtypeStruct((N_GROUPS, 1, N_STATE), F32),
                   jax.ShapeDtypeStruct((N_GROUPS, 1, N_STATE), F32), jax.ShapeDtypeStruct(part.shape, part.dtype)],
        in_specs=[wide,
                  pl.BlockSpec((tb, N_STATE), lambda g, j: (nb - 1 - j, B_BLOCK0 + g)),
                  pl.BlockSpec((tb, N_STATE), lambda g, j: (nb - 1 - j, C_BLOCK0 + g)),
                  pl.BlockSpec((tb, N_STATE), lambda g, j: (nb - 1 - j, DT_BLOCK0 + g)),
                  pl.BlockSpec((tb, GROUP_W), lambda g, j: (nb - 1 - j, Z_BLOCK0 + g)),
                  small, small, small, pl.BlockSpec((1, GROUP_W), lambda g, j: (0, g)),
                  pl.BlockSpec((N_STATE, GROUP_W), lambda g, j: (0, 0)), wide,
                  pl.BlockSpec((1, nsaved, N_STATE, GROUP_W), lambda g, j: (g, nb - 1 - j, 0, 0)), hbm, hbm],
        out_specs=[wide, narrow, narrow,
                   pl.BlockSpec((None, tb, GROUP_W), lambda g, j: (Z_BLOCK0 // 2 + g // 2, nb - 1 - j, g % 2)),
                   narrow, pl.BlockSpec((1, GROUP_W), lambda g, j: (0, g)), small, small, small, hbm],
        input_output_aliases={13: 3},
        scratch_shapes=[pltpu.VMEM((N_STATE, GROUP_W), F32)] + CHIP_SEMS,
        compiler_params=_cparams(("arbitrary", "arbitrary")),
    )(xc, xc, xc, proj, proj, alog4, bias4, dskip4, wnorm, expand, dob, states, part, dproj)


def lower_bound_fwd(hgrn_lb):
    def body(a_ref, o_ref):
        a0, a1 = a_ref[0:1, :], a_ref[1:2, :]
        m = jnp.maximum(a0, a1)
        e0, e1 = jnp.exp(a0 - m), jnp.exp(a1 - m)
        o_ref[...] = e0 / (e0 + e1)

    return pl.pallas_call(body, name="lower_bound_fwd", out_shape=jax.ShapeDtypeStruct((1, D), F32))(hgrn_lb)


def ada_weight_grad(c_all, dmod_cols):
    def body(c_ref, d_ref, o_ref):
        cval = c_ref[...]
        o_ref[...] = _tn(cval * _sigmoid(cval), d_ref[...], HI)

    return pl.pallas_call(body, name="ada_weight_grad",
                          out_shape=jax.ShapeDtypeStruct((D, dmod_cols.shape[1]), F32))(c_all, dmod_cols)


def reduce_small(gathered, hgrn_lb, dlb_off):
    n = gathered.shape[2]

    def body(g_ref, a_ref, o_ref, glb_ref):
        s = g_ref[0]
        for d in range(1, N_DEV):
            s = s + g_ref[d]
        o_ref[...] = s
        a0, a1 = a_ref[0:1, :], a_ref[1:2, :]
        m = jnp.maximum(a0, a1)
        e0, e1 = jnp.exp(a0 - m), jnp.exp(a1 - m)
        p0 = e0 / (e0 + e1)
        tq = s[:, dlb_off:dlb_off + D] * p0 * (1.0 - p0)
        glb_ref[0:1, :] = tq
        glb_ref[1:2, :] = -tq

    return pl.pallas_call(body, name="reduce_small",
                          out_shape=[jax.ShapeDtypeStruct((1, n), F32), jax.ShapeDtypeStruct((2, D), F32)])(gathered, hgrn_lb)


def _adam_math(w, g, m, v):
    m2 = ADAM_B1 * m + (1.0 - ADAM_B1) * g
    v2 = ADAM_B2 * v + (1.0 - ADAM_B2) * (g * g)
    m_hat = m2 / (1.0 - ADAM_B1 ** ADAM_STEP)
    v_hat = v2 / (1.0 - ADAM_B2 ** ADAM_STEP)
    delta = -ADAM_LR * (m_hat / (jnp.sqrt(v_hat) + ADAM_EPS) + ADAM_WD * w)
    return delta, m2, v2


def _row_tile(rows, mult=8, cap=128):
    for cand in range(cap - cap % mult, 0, -mult):
        if rows % cand == 0:
            return cand
    return rows


def sum_parts(parts, name):
    n, rows, cols = parts.shape
    tr = _row_tile(rows, 16, 1024)

    def body(p_ref, o_ref):
        s = p_ref[0].astype(F32)
        for d in range(1, n):
            s = s + p_ref[d].astype(F32)
        o_ref[...] = s

    return pl.pallas_call(
        body, name=name, grid=(rows // tr,),
        out_shape=jax.ShapeDtypeStruct((rows, cols), F32),
        in_specs=[pl.BlockSpec((n, tr, cols), lambda i: (0, i, 0))],
        out_specs=pl.BlockSpec((tr, cols), lambda i: (i, 0)),
        compiler_params=_cparams(("parallel",)),
    )(parts)


def sum_pair(a, b, name):
    rows, cols = a.shape
    tr = _row_tile(rows, 16, 1024)

    def body(a_ref, b_ref, o_ref):
        o_ref[...] = (a_ref[...].astype(F32) + b_ref[...].astype(F32)).astype(o_ref.dtype)

    blk = pl.BlockSpec((tr, cols), lambda i: (i, 0))
    return pl.pallas_call(
        body, name=name, grid=(rows // tr,),
        out_shape=jax.ShapeDtypeStruct((rows, cols), a.dtype),
        in_specs=[blk, blk], out_specs=blk,
        compiler_params=_cparams(("parallel",)),
    )(a, b)


def adamw(w, g, m, v, name):
    rows, cols = w.shape
    tr = _row_tile(rows, 8, 256)

    def body(w_ref, g_ref, m_ref, v_ref, d_ref, m2_ref, v2_ref):
        delta, m2, v2 = _adam_math(w_ref[...], g_ref[...], m_ref[...], v_ref[...])
        d_ref[...] = delta
        m2_ref[...] = m2
        v2_ref[...] = v2

    blk = pl.BlockSpec((tr, cols), lambda i: (i, 0))
    return pl.pallas_call(
        body, name=name, grid=(rows // tr,),
        out_shape=[jax.ShapeDtypeStruct((rows, cols), F32)] * 3,
        in_specs=[blk] * 4, out_specs=[blk] * 3,
        compiler_params=_cparams(("parallel",)),
    )(w, g, m, v)


def _pad128(n):
    return -(-n // 128) * 128


def _pack(arrays):
    offs, parts, off = [], [], 0
    for a in arrays:
        flat = a.reshape(1, -1)
        n = flat.shape[1]
        offs.append(off)
        parts.append(jnp.pad(flat, ((0, 0), (0, _pad128(n) - n))))
        off += _pad128(n)
    return jnp.concatenate(parts, axis=1), offs


def _unpack(vec, offs, shapes):
    out = []
    for off, shp in zip(offs, shapes):
        n = int(np.prod(shp))
        out.append(vec[0, off:off + n].reshape(shp))
    return out


IN_ROWS = IN_DIM // N_DEV
DT_ROW0 = 9216
DT_DEV, DT_LO = divmod(DT_ROW0, IN_ROWS)


GATE_SHIFT = D - 32


def _in_row_pieces(tile):
    pieces = []
    if tile == DT_COL_BLOCK:
        for g in range(N_GROUPS):
            o = DT_ROW0 + HEADS_PER_GROUP * g
            pieces.append((N_STATE * g, o // IN_ROWS, o % IN_ROWS, HEADS_PER_GROUP))
        return pieces
    r, end = tile * D, (tile + 1) * D
    while r < end:
        o = r if r < DT_ROW0 else r - GATE_SHIFT
        dev, loc = divmod(o, IN_ROWS)
        n = min(end - r, IN_ROWS - loc)
        pieces.append((r - tile * D, dev, loc, n))
        r += n
    return pieces


def assemble_w_in(g_all):
    ntile = N_PROJ // D

    def body(g_ref, o_ref):
        j = pl.program_id(0)
        for tile in range(ntile):
            @pl.when(j == tile)
            def _(tile=tile):
                if tile == DT_COL_BLOCK:
                    o_ref[...] = jnp.zeros_like(o_ref)
                for dst, dev, loc, n in _in_row_pieces(tile):
                    o_ref[pl.ds(dst, n), :] = g_ref[dev, pl.ds(loc, n), :]

    return pl.pallas_call(
        body, name="assemble_w_in", grid=(ntile,),
        out_shape=jax.ShapeDtypeStruct((N_PROJ, D), g_all.dtype),
        in_specs=[pl.BlockSpec(memory_space=pltpu.VMEM)],
        out_specs=pl.BlockSpec((D, D), lambda j: (j, 0)),
        compiler_params=_cparams(("arbitrary",)),
    )(g_all)


def _grad_in_blocks(g_t, core, slot):
    dt0 = DT_COL_BLOCK * D
    dt = g_t[dt0:dt0 + N_GROUPS * N_STATE].reshape(N_GROUPS, N_STATE, D)[:, :HEADS_PER_GROUP].reshape(32, D)
    with_dt = jnp.concatenate([g_t[DT_DEV * IN_ROWS:DT_ROW0], dt,
                               g_t[DT_ROW0 + 32 + GATE_SHIFT:(DT_DEV + 1) * IN_ROWS + GATE_SHIFT]], axis=0)
    blocks = []
    for q in range(N_CHIP):
        if 2 * q + 1 < DT_DEV:
            blk = lax.dynamic_slice_in_dim(g_t, IN_ROWS * (2 * q + core), IN_ROWS, axis=0)
        else:
            assert 2 * q == DT_DEV
            after = g_t[(DT_DEV + 1) * IN_ROWS + GATE_SHIFT:(DT_DEV + 2) * IN_ROWS + GATE_SHIFT]
            blk = jnp.where(core == 0, with_dt, after)
        blocks.append(jnp.pad(blk, ((0, slot - IN_ROWS), (0, 0))))
    return jnp.stack(blocks)


def kernel(x, c, w_ada, b_ada, w_in, hgrn_lb, hgrn_gnorm, ssm_conv_w, ssm_conv_b, ssm_dt_bias, ssm_a_log, ssm_d, ssm_norm, w_branch_a, w_branch_b, w_o, ln1_g, ln1_b, w_ffn_gate, w_ffn_up, w_ffn_down, ln2_g, ln2_b, loss_target, m_w_ada, m_b_ada, m_w_in, m_hgrn_lb, m_hgrn_gnorm, m_ssm_conv_w, m_ssm_conv_b, m_ssm_dt_bias, m_ssm_a_log, m_ssm_d, m_ssm_norm, m_w_branch_a, m_w_branch_b, m_w_o, m_ln1_g, m_ln1_b, m_w_ffn_gate, m_w_ffn_up, m_w_ffn_down, m_ln2_g, m_ln2_b, v_w_ada, v_b_ada, v_w_in, v_hgrn_lb, v_hgrn_gnorm, v_ssm_conv_w, v_ssm_conv_b, v_ssm_dt_bias, v_ssm_a_log, v_ssm_d, v_ssm_norm, v_w_branch_a, v_w_branch_b, v_w_o, v_ln1_g, v_ln1_b, v_w_ffn_gate, v_w_ffn_up, v_w_ffn_down, v_ln2_g, v_ln2_b):
    me = 4 * lax.axis_index("x") + 2 * lax.axis_index("y") + lax.axis_index("c")
    xt = x[0]
    tgt = loss_target[0]
    t = xt.shape[0]
    ada_cols = w_ada.shape[2]
    conv_cols = ssm_conv_w.shape[2]

    small_in, _ = _pack([c, ssm_conv_w[0]])
    small_all = allgather_vmem(small_in, "allgather_small_inputs")
    c_all = small_all[:, 0, :D]
    conv_w = small_all[:, 0, D:D + CONV_TAPS * conv_cols].reshape(N_DEV, CONV_TAPS, conv_cols)
    conv_w = conv_w.transpose(1, 0, 2).reshape(CONV_TAPS, CONV_DIM)
    mod = ada_modulation(c_all, w_ada[0], b_ada.reshape(N_DEV, 1, ada_cols))
    mod6 = mod.reshape(6, D)

    shards = [w_in[0].T, w_branch_a[0], w_branch_b[0], w_o[0], w_ffn_gate[0].T, w_ffn_up[0].T, w_ffn_down[0]]
    shard_rows = [s.shape[0] for s in shards]
    slot_rows = [-(-r // 32) * 32 for r in shard_rows]
    row_offs = [sum(slot_rows[:i]) for i in range(len(shards))]
    padded = [jnp.pad(s.astype(BF16), ((0, p - r), (0, 0))) for s, r, p in zip(shards, shard_rows, slot_rows)]
    w_in_t = assemble_w_in(allgather_hbm(padded[0], "allgather_w_in"))

    lb = lower_bound_fwd(hgrn_lb)
    u1 = ln_modulate(xt, mod6, 0, 1, "ln_modulate_1")
    proj, g_rest = mm_nt_gather(u1, w_in_t, F32, jnp.concatenate(padded[1:], axis=0), "mm_in_proj")
    g_ba, g_bb, g_o, g_fg, g_fu, g_fd = (g_rest[:, o - slot_rows[0]:o - slot_rows[0] + r]
                                         for o, r in zip(row_offs[1:], shard_rows[1:]))
    w_ba = g_ba.reshape(D, D)
    w_bb = g_bb.reshape(B_INNER, D)
    w_oo = g_o.reshape(D, D)
    w_gu_t = jnp.concatenate([g_fg.reshape(D_FF, D), g_fu.reshape(D_FF, D)], axis=0)
    w_dn = g_fd.reshape(D_FF, D)
    o_a, o_raw, st_a = hgrn_fwd(proj, lb, hgrn_gnorm)
    xc, conv_slope = conv_fwd(proj, conv_w, ssm_conv_b)
    pad3 = ((0, 0), (0, 0), (0, N_STATE - HEADS_PER_GROUP))
    alog4 = jnp.pad(ssm_a_log.reshape(N_GROUPS, 1, HEADS_PER_GROUP), pad3)
    bias4 = jnp.pad(ssm_dt_bias.reshape(N_GROUPS, 1, HEADS_PER_GROUP), pad3)
    dskip4 = jnp.pad(ssm_d.reshape(N_GROUPS, 1, HEADS_PER_GROUP), pad3)
    expand = _head_expand()
    o_b, st_b = ssd_fwd(proj, xc, alog4, bias4, dskip4, ssm_norm, expand)
    ya = mm_nn(o_a, w_ba, BF16, "mm_branch_a")
    yb = mm_nn(o_b, w_bb, BF16, "mm_branch_b")
    merged = merge_gates(ya, yb, proj)
    h1 = mm_nn(merged, w_oo, F32, "mm_out_proj")
    x1 = resid_ln(xt, h1, mod6, 2, ln1_g, ln1_b, "resid_ln_1")
    u2 = ln_modulate(x1, mod6, 3, 4, "ln_modulate_2")
    gu = mm_nt(u2, w_gu_t, BF16, "mm_ffn_in")
    act = swiglu_act(gu)
    h2 = mm_nn(act, w_dn, F32, "mm_ffn_out")

    dh2, dx1_part, acc4 = resid_ln_bwd(x1, h2, mod6, 5, ln2_g, ln2_b, tgt, True, "resid_ln_2_bwd")
    g_dn = mm_tn(act, dh2, "mm_grad_ffn_down")
    dact = mm_nt(dh2, w_dn, BF16, "mm_dact")
    dgu = swiglu_act_bwd(gu, dact)
    g_gu_t = mm_tn(dgu, u2, "mm_grad_ffn_in")
    du2 = mm_nn(dgu, w_gu_t, F32, "mm_du2")
    dx1, acc3 = ln_modulate_bwd(x1, du2, mod6, 4, dx1_part, "ln_modulate_2_bwd")
    dh1, dx_part, acc2 = resid_ln_bwd(xt, h1, mod6, 2, ln1_g, ln1_b, dx1, False, "resid_ln_1_bwd")
    g_o = mm_tn(merged, dh1, "mm_grad_out_proj")
    dmerged = mm_nt(dh1, w_oo, BF16, "mm_dmerged")
    dya, dyb, dproj = merge_gates_bwd(dmerged, ya, yb, proj)
    g_ba_full = mm_tn(o_a, dya, "mm_grad_branch_a")
    g_bb_full = mm_tn(o_b, dyb, "mm_grad_branch_b")
    doa = mm_nt(dya, w_ba, F32, "mm_doa")
    dob = mm_nt(dyb, w_bb, F32, "mm_dob")
    my_core = lax.axis_index("c")

    def by_core(blocks, rows, slots):
        contrib = jnp.concatenate([jnp.pad(b.reshape(N_DEV, -1, D), ((0, 0), (0, p - r), (0, 0)))
                                   for b, r, p in zip(blocks, rows, slots)], axis=1)
        split = contrib.reshape(N_CHIP, 2, contrib.shape[1], D).transpose(1, 0, 2, 3)
        return (lax.dynamic_index_in_dim(split, my_core, 0, keepdims=False),
                lax.dynamic_index_in_dim(split, 1 - my_core, 0, keepdims=False))

    keep_e, give_e = by_core([g_ba_full, g_bb_full, g_o, g_gu_t[:D_FF], g_gu_t[D_FF:], g_dn],
                             shard_rows[1:], slot_rows[1:])
    dproj, dlb, dgn, got_e = hgrn_bwd(proj, lb, hgrn_gnorm, o_raw, doa, st_a, give_e, dproj)
    chip_e = sum_pair(keep_e.reshape(-1, D), got_e.reshape(-1, D), "sum_grads_rest_chip").reshape(keep_e.shape)
    dxs, dbm, dcm, dproj, ddt, dwn, dalog, dbias, ddsk, parts_e = ssd_bwd(proj, xc, alog4, bias4, dskip4, ssm_norm,
                                                                          expand, dob, st_b, chip_e, dproj)
    dxc = jnp.concatenate([dxs, dbm, dcm], axis=1)
    dproj, dcw, dcb = conv_bwd(proj, dxc, conv_slope, conv_w, dproj)
    dproj = dt_fill(ddt, dproj)
    g_in_t = mm_tn(dproj, u1, "mm_grad_in_proj")
    keep_l = _grad_in_blocks(g_in_t, my_core, slot_rows[0])
    give_l = _grad_in_blocks(g_in_t, 1 - my_core, slot_rows[0])
    got_l = exchange_sibling(give_l, "exchange_grad_in_sibling")
    chip_l = sum_pair(keep_l.reshape(-1, D), got_l.reshape(-1, D), "sum_grad_in_chip").reshape(keep_l.shape)
    du1, parts_l = mm_nn_exchange(dproj, w_in_t, F32, chip_l, "mm_du1")
    dx, acc1 = ln_modulate_bwd(xt, du1, mod6, 1, dx_part, "ln_modulate_1_bwd")
    gw_in = sum_parts(parts_l, "sum_grad_in")[:shard_rows[0]].T
    g_rows = sum_parts(parts_e, "sum_grads_rest")
    gw_ba, gw_bb, gw_o, gw_fg, gw_fu, gw_fd = (g_rows[o - slot_rows[0]:o - slot_rows[0] + r]
                                               for o, r in zip(row_offs[1:], shard_rows[1:]))
    gw_fg, gw_fu = gw_fg.T, gw_fu.T

    dmod = jnp.concatenate([acc1[1:2], acc1[0:1], acc2[0:1], acc3[1:2], acc3[0:1], acc4[0:1]], axis=1)
    small_fields = [dmod, acc4[3:4, :128], dlb, dgn, dcw[:CONV_TAPS], dcb, dbias, dalog, ddsk, dwn,
                    acc2[1:2], acc2[2:3], acc4[1:2], acc4[2:3]]
    small_out, offs = _pack(small_fields)
    small_sum_in = allgather_vmem(small_out, "allgather_small_grads")
    gsum, g_lb = reduce_small(small_sum_in, hgrn_lb, offs[2])
    (g_bada, loss_row, _, g_gn, g_cw_full, g_cb, g_bias4, g_alog4, g_dsk4, g_wn, g_l1g, g_l1b, g_l2g, g_l2b) = _unpack(
        gsum, offs, [(1, 6 * D), (1, 128), (1, D), (1, HK), (CONV_TAPS, CONV_DIM), (1, CONV_DIM),
                     (N_GROUPS, N_STATE), (N_GROUPS, N_STATE), (N_GROUPS, N_STATE), (1, B_INNER),
                     (1, D), (1, D), (1, D), (1, D)])
    loss = loss_row[0, 0]
    g_cw = lax.dynamic_slice(g_cw_full, (0, me * conv_cols), (CONV_TAPS, conv_cols))[None]
    g_dtb = g_bias4[:, :HEADS_PER_GROUP].reshape(1, 32)
    g_alog = g_alog4[:, :HEADS_PER_GROUP].reshape(1, 32)
    g_dsk = g_dsk4[:, :HEADS_PER_GROUP].reshape(1, 32)

    dmod_all = small_sum_in[:, 0, offs[0]:offs[0] + 6 * D]
    dmod_cols = lax.dynamic_slice(dmod_all, (0, me * ada_cols), (N_DEV, ada_cols))
    gw_ada = ada_weight_grad(c_all, dmod_cols)

    big = [("ada", w_ada[0], gw_ada, m_w_ada[0], v_w_ada[0]), ("in", w_in[0], gw_in, m_w_in[0], v_w_in[0]),
           ("branch_a", w_branch_a[0], gw_ba, m_w_branch_a[0], v_w_branch_a[0]),
           ("branch_b", w_branch_b[0], gw_bb, m_w_branch_b[0], v_w_branch_b[0]),
           ("o", w_o[0], gw_o, m_w_o[0], v_w_o[0]),
           ("ffn_gate", w_ffn_gate[0], gw_fg, m_w_ffn_gate[0], v_w_ffn_gate[0]),
           ("ffn_up", w_ffn_up[0], gw_fu, m_w_ffn_up[0], v_w_ffn_up[0]),
           ("ffn_down", w_ffn_down[0], gw_fd, m_w_ffn_down[0], v_w_ffn_down[0])]
    big_out = {}
    for nm, w_, g_, m_, v_ in big:
        d_, m2_, v2_ = adamw(w_, g_, m_, v_, "adamw_" + nm)
        big_out[nm] = (g_[None], d_[None], m2_[None], v2_[None])

    small_w = [b_ada, hgrn_lb, hgrn_gnorm, ssm_conv_w, ssm_conv_b, ssm_dt_bias, ssm_a_log, ssm_d, ssm_norm,
               ln1_g, ln1_b, ln2_g, ln2_b]
    small_g = [g_bada, g_lb, g_gn, g_cw, g_cb, g_dtb, g_alog, g_dsk, g_wn, g_l1g, g_l1b, g_l2g, g_l2b]
    small_m = [m_b_ada, m_hgrn_lb, m_hgrn_gnorm, m_ssm_conv_w, m_ssm_conv_b, m_ssm_dt_bias, m_ssm_a_log, m_ssm_d,
               m_ssm_norm, m_ln1_g, m_ln1_b, m_ln2_g, m_ln2_b]
    small_v = [v_b_ada, v_hgrn_lb, v_hgrn_gnorm, v_ssm_conv_w, v_ssm_conv_b, v_ssm_dt_bias, v_ssm_a_log, v_ssm_d,
               v_ssm_norm, v_ln1_g, v_ln1_b, v_ln2_g, v_ln2_b]
    shapes = [a.shape for a in small_w]
    small_g = [g_.reshape(s) for g_, s in zip(small_g, shapes)]
    pw, poffs = _pack(small_w)
    pg, _ = _pack(small_g)
    pm, _ = _pack(small_m)
    pv, _ = _pack(small_v)
    pd, pm2, pv2 = adamw(pw, pg, pm, pv, "adamw_small")
    s_d, s_m, s_v = (_unpack(p, poffs, shapes) for p in (pd, pm2, pv2))
    (sn_bada, sn_lb, sn_gn, sn_cw, sn_cb, sn_dtb, sn_alog, sn_dsk, sn_wn, sn_l1g, sn_l1b, sn_l2g, sn_l2b) = range(13)

    def order(kind):
        sm = [small_g, s_d, s_m, s_v][kind]
        bg = lambda nm: big_out[nm][kind]
        return [bg("ada"), sm[sn_bada], bg("in"), sm[sn_lb], sm[sn_gn], sm[sn_cw], sm[sn_cb], sm[sn_dtb], sm[sn_alog],
                sm[sn_dsk], sm[sn_wn], bg("branch_a"), bg("branch_b"), bg("o"), sm[sn_l1g], sm[sn_l1b],
                bg("ffn_gate"), bg("ffn_up"), bg("ffn_down"), sm[sn_l2g], sm[sn_l2b]]

    return (loss, dx[None], *order(0), *order(1), *order(2), *order(3))
```

```python
import numpy as np
import jax
import jax.numpy as jnp
from jax import lax
from jax.experimental import pallas as pl
from jax.experimental.pallas import tpu as pltpu

F32 = jnp.float32
BF16 = jnp.bfloat16
HI = lax.Precision.HIGHEST

N_DEV = 8
D = 1024
N_HEADS_A = 8
HK = 128
CHUNK = 64
SSD_CHUNK = 128
SSD_CHUNK_BWD = 256
N_GROUPS = 4
HEADS_PER_GROUP = 8
HEAD_P = 64
N_STATE = 128
GROUP_W = HEADS_PER_GROUP * HEAD_P
B_INNER = 2048
CONV_DIM = 3072
D_FF = 2816
IN_DIM = 11296
N_PROJ = 12288
ALPHA = 2.0 ** 0.25
LN_EPS = 1e-5
RMS_EPS = 1e-6
Q_SCALE = 128 ** -0.5
EXP_CLIP = 80.0
ADAM_LR, ADAM_B1, ADAM_B2, ADAM_EPS, ADAM_WD, ADAM_STEP = 0.001, 0.9, 0.999, 1e-8, 0.01, 10
VMEM_LIMIT = 48 * 1024 * 1024
TOKEN_BLOCK = 1024
ROW_TILE = 512
WIDE_ROW_TILE = 1024
FFN_ROW_TILE = 512
MM_ROW_TILE = 1024
MM_TOKEN_TILE = 4096
MM_K_TILE = 3072
MM_COL_TILE = 1408
HGRN_HEADS_PER_STEP = 4
CHUNK_UNROLL = 8
MESH_ID = pl.DeviceIdType.MESH

NT_DIMS = (((1,), (1,)), ((), ()))
TN_DIMS = (((0,), (0,)), ((), ()))


def _cparams(sem=None):
    return pltpu.CompilerParams(dimension_semantics=sem, vmem_limit_bytes=VMEM_LIMIT)


def _sigmoid(x):
    return 1.0 / (1.0 + jnp.exp(-x))


def _dsilu(x, s):
    return s * (1.0 + x * (1.0 - s))


def _nt(a, b, precision=None):
    return lax.dot_general(a, b, NT_DIMS, precision=precision, preferred_element_type=F32)


def _tn(a, b, precision=None):
    return lax.dot_general(a, b, TN_DIMS, precision=precision, preferred_element_type=F32)


def _nn(a, b, precision=None):
    return jnp.dot(a, b, precision=precision, preferred_element_type=F32)


def _split(x, pieces):
    out = []
    for i in range(pieces):
        p = x.astype(BF16)
        out.append(p)
        if i + 1 < pieces:
            x = x - p.astype(F32)
    return out


def _sel(dot, x, sel01, pieces, x_first=True):
    acc = None
    for p in _split(x, pieces):
        term = dot(p, sel01) if x_first else dot(sel01, p)
        acc = term if acc is None else acc + term
    return acc


def _ln(x):
    mu = jnp.mean(x, axis=-1, keepdims=True)
    xc = x - mu
    rstd = lax.rsqrt(jnp.mean(xc * xc, axis=-1, keepdims=True) + LN_EPS)
    return xc * rstd, rstd


def _ln_bwd(dxh, xh, rstd):
    return rstd * (dxh - jnp.mean(dxh, axis=-1, keepdims=True) - xh * jnp.mean(dxh * xh, axis=-1, keepdims=True))


def _colsum(x):
    return jnp.sum(x, axis=0, keepdims=True)


def _tri(n, upper=False):
    r = lax.broadcasted_iota(jnp.int32, (n, n), 0)
    c = lax.broadcasted_iota(jnp.int32, (n, n), 1)
    return (c >= r) if upper else (r >= c)


def _my_pos():
    return lax.axis_index("x"), lax.axis_index("y"), lax.axis_index("c")


def _peer(pos, k):
    x, y, c = pos
    return (x ^ ((k >> 2) & 1), y ^ ((k >> 1) & 1), c ^ (k & 1))


def _flat(pos):
    return 4 * pos[0] + 2 * pos[1] + pos[2]


def allgather_vmem(v, name):
    n = v.shape[1]

    def body(v_ref, o_ref, send_sems, recv_sems, local_sem):
        me = _my_pos()
        mine = pltpu.make_async_copy(v_ref, o_ref.at[_flat(me)], local_sem)
        mine.start()
        sends = []
        for k in range(1, N_DEV):
            peer = _peer(me, k)
            cp = pltpu.make_async_remote_copy(v_ref, o_ref.at[_flat(me)], send_sems.at[k - 1], recv_sems.at[k - 1],
                                              device_id=peer, device_id_type=MESH_ID)
            cp.start()
            sends.append(cp)
        for k in range(1, N_DEV):
            peer = _peer(me, k)
            pltpu.make_async_remote_copy(v_ref, o_ref.at[_flat(peer)], send_sems.at[k - 1], recv_sems.at[k - 1],
                                         device_id=peer, device_id_type=MESH_ID).wait_recv()
        for cp in sends:
            cp.wait_send()
        mine.wait()

    return pl.pallas_call(
        body, name=name,
        out_shape=jax.ShapeDtypeStruct((N_DEV, 1, n), F32),
        in_specs=[pl.BlockSpec(memory_space=pltpu.VMEM)],
        out_specs=pl.BlockSpec(memory_space=pltpu.VMEM),
        scratch_shapes=[pltpu.SemaphoreType.DMA((N_DEV - 1,)), pltpu.SemaphoreType.DMA((N_DEV - 1,)),
                        pltpu.SemaphoreType.DMA],
        compiler_params=_cparams(),
    )(v)


def ada_modulation(c_all, w_ada_s, b_ada_r):
    ncol = w_ada_s.shape[1]

    def body(c_ref, w_ref, b_ref, o_ref, part_ref, send_sems, recv_sems):
        me = _my_pos()
        cval = c_ref[...]
        cond = cval * _sigmoid(cval)
        part = _nn(cond, w_ref[...], HI)
        for r in range(N_DEV):
            part_ref[r] = part[r:r + 1, :]
        sends = []
        for k in range(1, N_DEV):
            peer = _peer(me, k)
            cp = pltpu.make_async_remote_copy(part_ref.at[_flat(peer)], o_ref.at[_flat(me)], send_sems.at[k - 1],
                                              recv_sems.at[k - 1], device_id=peer, device_id_type=MESH_ID)
            cp.start()
            sends.append(cp)
        o_ref[_flat(me)] = part_ref[_flat(me)]
        for k in range(1, N_DEV):
            peer = _peer(me, k)
            pltpu.make_async_remote_copy(part_ref.at[_flat(peer)], o_ref.at[_flat(peer)], send_sems.at[k - 1],
                                         recv_sems.at[k - 1], device_id=peer, device_id_type=MESH_ID).wait_recv()
        for cp in sends:
            cp.wait_send()
        o_ref[...] = o_ref[...] + b_ref[...]

    return pl.pallas_call(
        body, name="ada_modulation",
        out_shape=jax.ShapeDtypeStruct((N_DEV, 1, ncol), F32),
        in_specs=[pl.BlockSpec(memory_space=pltpu.VMEM)] * 3,
        out_specs=pl.BlockSpec(memory_space=pltpu.VMEM),
        scratch_shapes=[pltpu.VMEM((N_DEV, 1, ncol), F32), pltpu.SemaphoreType.DMA((N_DEV - 1,)),
                        pltpu.SemaphoreType.DMA((N_DEV - 1,))],
        compiler_params=_cparams(),
    )(c_all, w_ada_s, b_ada_r)


def allgather_hbm(shard, name):
    def body(x_ref, out_ref, send_sems, recv_sems, local_sem):
        x, y, c = _my_pos()
        me, sibling = (x, y, c), (x, y, 1 - c)
        chips = [(1 - x, y), (x, 1 - y), (1 - x, 1 - y)]

        def slot(pos):
            return out_ref.at[_flat(pos)]

        def copy(k, block, to, src=None):
            return pltpu.make_async_remote_copy(slot(block) if src is None else src, slot(block), send_sems.at[k],
                                                recv_sems.at[k], device_id=to, device_id_type=MESH_ID)

        mine = pltpu.make_async_copy(x_ref, slot(me), local_sem)
        mine.start()
        first = [copy(0, me, sibling, src=x_ref)]
        first += [copy(1 + j, me, (*chip, c), src=x_ref) for j, chip in enumerate(chips)]
        for cp in first:
            cp.start()
        passed = [copy(4 + j, (*chip, c), sibling) for j, chip in enumerate(chips)]
        for j, chip in enumerate(chips):
            copy(1 + j, (*chip, c), me).wait_recv()
            passed[j].start()
        copy(0, sibling, me).wait_recv()
        for j, chip in enumerate(chips):
            copy(4 + j, (*chip, 1 - c), me).wait_recv()
        for cp in first + passed:
            cp.wait_send()
        mine.wait()

    return pl.pallas_call(
        body, name=name,
        out_shape=jax.ShapeDtypeStruct((N_DEV,) + shard.shape, shard.dtype),
        in_specs=[pl.BlockSpec(memory_space=pl.ANY)],
        out_specs=pl.BlockSpec(memory_space=pl.ANY),
        scratch_shapes=[pltpu.SemaphoreType.DMA((N_DEV - 1,)), pltpu.SemaphoreType.DMA((N_DEV - 1,)),
                        pltpu.SemaphoreType.DMA],
        compiler_params=_cparams(),
    )(shard)


N_CHIP = N_DEV // 2
SIBLING_SEMS = [pltpu.SemaphoreType.DMA, pltpu.SemaphoreType.DMA]
CHIP_SEMS = [pltpu.SemaphoreType.DMA((N_CHIP - 1,)), pltpu.SemaphoreType.DMA((N_CHIP - 1,)), pltpu.SemaphoreType.DMA]


def _sibling_exchange(s_ref, o_ref, send_sem, recv_sem):
    x, y, c = _my_pos()
    cp = pltpu.make_async_remote_copy(s_ref, o_ref, send_sem, recv_sem, device_id=(x, y, 1 - c), device_id_type=MESH_ID)
    return cp.start, cp.wait


def _chip_exchange(p_ref, o_ref, send_sems, recv_sems, local_sem):
    x, y, c = _my_pos()
    my_chip = 2 * x + y
    mine = pltpu.make_async_copy(p_ref.at[my_chip], o_ref.at[my_chip], local_sem)
    peers = [(x ^ (k >> 1), y ^ (k & 1)) for k in range(1, N_CHIP)]
    sends = [pltpu.make_async_remote_copy(p_ref.at[2 * px + py], o_ref.at[my_chip], send_sems.at[k], recv_sems.at[k],
                                          device_id=(px, py, c), device_id_type=MESH_ID)
             for k, (px, py) in enumerate(peers)]
    recvs = [pltpu.make_async_remote_copy(p_ref.at[2 * px + py], o_ref.at[2 * px + py], send_sems.at[k], recv_sems.at[k],
                                          device_id=(px, py, c), device_id_type=MESH_ID)
             for k, (px, py) in enumerate(peers)]

    def start():
        mine.start()
        for cp in sends:
            cp.start()

    def wait():
        for cp in recvs:
            cp.wait_recv()
        for cp in sends:
            cp.wait_send()
        mine.wait()

    return start, wait


def exchange_sibling(send, name):
    def body(s_ref, o_ref, send_sem, recv_sem):
        start, wait = _sibling_exchange(s_ref, o_ref, send_sem, recv_sem)
        start()
        wait()

    return pl.pallas_call(
        body, name=name,
        out_shape=jax.ShapeDtypeStruct(send.shape, send.dtype),
        in_specs=[pl.BlockSpec(memory_space=pl.ANY)],
        out_specs=pl.BlockSpec(memory_space=pl.ANY),
        scratch_shapes=SIBLING_SEMS,
        compiler_params=_cparams(),
    )(send)


LANES = 128


def _k_tile(kdim, unit=LANES):
    for cand in range(MM_K_TILE - MM_K_TILE % unit, 0, -unit):
        if kdim % cand == 0:
            return cand
    return kdim


def _lane_tile(n, cap):
    for cand in range(cap - cap % LANES, 0, -LANES):
        if n % cand == 0:
            return cand
    return n


def _m_tile(m, kdim):
    return min(MM_ROW_TILE if kdim > D else 2 * MM_ROW_TILE, m)


def _mm(a, b, out_dtype, name, b_is_nk):
    m, kdim = a.shape
    n = b.shape[0] if b_is_nk else b.shape[1]
    tm, tn, tk = _m_tile(m, kdim), _lane_tile(n, MM_COL_TILE), _k_tile(kdim)
    nk = kdim // tk
    dot = _nt if b_is_nk else _nn

    def body(a_ref, b_ref, o_ref, *acc):
        p = dot(a_ref[...], b_ref[...])
        if nk == 1:
            o_ref[...] = p.astype(o_ref.dtype)
        else:
            acc_ref, k = acc[0], pl.program_id(2)

            @pl.when(k == 0)
            def _():
                acc_ref[...] = p

            @pl.when(k > 0)
            def _():
                acc_ref[...] += p

            @pl.when(k == nk - 1)
            def _():
                o_ref[...] = acc_ref[...].astype(o_ref.dtype)

    b_spec = (pl.BlockSpec((tn, tk), lambda j, i, k: (j, k)) if b_is_nk else
              pl.BlockSpec((tk, tn), lambda j, i, k: (k, j)))
    return pl.pallas_call(
        body, name=name, grid=(n // tn, m // tm, nk),
        out_shape=jax.ShapeDtypeStruct((m, n), out_dtype),
        in_specs=[pl.BlockSpec((tm, tk), lambda j, i, k: (i, k)), b_spec],
        out_specs=pl.BlockSpec((tm, tn), lambda j, i, k: (i, j)),
        scratch_shapes=[] if nk == 1 else [pltpu.VMEM((tm, tn), F32)],
        compiler_params=_cparams(("parallel", "parallel", "arbitrary")),
    )(a, b)


def mm_nn(a, b, out_dtype, name):
    return _mm(a, b, out_dtype, name, False)


def mm_nt(a, b, out_dtype, name):
    return _mm(a, b, out_dtype, name, True)


def mm_nn_exchange(a, b, out_dtype, part, name):
    kblocks, m, kb = a.shape
    kdim = kblocks * kb
    n = b.shape[1]
    tm, tn, tk = min(MM_ROW_TILE, m), _lane_tile(n, MM_COL_TILE), _k_tile(kdim)
    gn, gm, nk = n // tn, m // tm, kdim // tk
    per_step = tk // kb

    def body(a_ref, b_ref, part_ref, o_ref, parts_ref, acc_ref, send_sems, recv_sems, local_sem):
        j, i, k = pl.program_id(0), pl.program_id(1), pl.program_id(2)
        xchg_start, xchg_wait = _chip_exchange(part_ref, parts_ref, send_sems, recv_sems, local_sem)

        @pl.when((j == 0) & (i == 0) & (k == 0))
        def _():
            xchg_start()

        p = _nn(a_ref[0], b_ref[0:kb, :])
        for c in range(1, per_step):
            p = p + _nn(a_ref[c], b_ref[c * kb:(c + 1) * kb, :])

        @pl.when(k == 0)
        def _():
            acc_ref[...] = p

        @pl.when(k > 0)
        def _():
            acc_ref[...] += p

        @pl.when(k == nk - 1)
        def _():
            o_ref[...] = acc_ref[...].astype(o_ref.dtype)

        @pl.when((j == gn - 1) & (i == gm - 1) & (k == nk - 1))
        def _():
            xchg_wait()

    hbm = pl.BlockSpec(memory_space=pl.ANY)
    return pl.pallas_call(
        body, name=name, grid=(gn, gm, nk),
        out_shape=[jax.ShapeDtypeStruct((m, n), out_dtype), jax.ShapeDtypeStruct(part.shape, part.dtype)],
        in_specs=[pl.BlockSpec((per_step, tm, kb), lambda j, i, k: (k, i, 0)),
                  pl.BlockSpec((tk, tn), lambda j, i, k: (k, j)), hbm],
        out_specs=[pl.BlockSpec((tm, tn), lambda j, i, k: (i, j)), hbm],
        scratch_shapes=[pltpu.VMEM((tm, tn), F32)] + CHIP_SEMS,
        compiler_params=_cparams(("arbitrary", "arbitrary", "arbitrary")),
    )(a, b, part)


def mm_nt_gather(a, b, out_dtype, shard, name):
    m, kdim = a.shape
    n = b.shape[0]
    tm, tn = _m_tile(m, kdim), 1024
    assert kdim == 1024
    gj = m // tm
    nsteps = (n // tn) * gj
    forward_step = max(nsteps - 2, 0)

    def body(a_ref, b_ref, x_ref, o_ref, g_ref, send_sems, recv_sems, local_sem):
        step = pl.program_id(0) * gj + pl.program_id(1)
        x, y, c = _my_pos()
        me, sibling = (x, y, c), (x, y, 1 - c)
        chips = [(1 - x, y), (x, 1 - y), (1 - x, 1 - y)]

        def slot(pos):
            return g_ref.at[_flat(pos)]

        def copy(k, block, to, src=None):
            return pltpu.make_async_remote_copy(slot(block) if src is None else src, slot(block), send_sems.at[k],
                                                recv_sems.at[k], device_id=to, device_id_type=MESH_ID)

        mine = pltpu.make_async_copy(x_ref, slot(me), local_sem)
        first = [copy(0, me, sibling, src=x_ref)]
        first += [copy(1 + j, me, (*chip, c), src=x_ref) for j, chip in enumerate(chips)]
        passed = [copy(4 + j, (*chip, c), sibling) for j, chip in enumerate(chips)]

        @pl.when(step == 0)
        def _():
            mine.start()
            for cp in first:
                cp.start()

        rows = pl.ds(pl.multiple_of(pl.program_id(1) * tm, tm), tm)
        o_ref[...] = _nt(a_ref[rows, :], b_ref[...]).astype(o_ref.dtype)

        @pl.when(step == forward_step)
        def _():
            for j, chip in enumerate(chips):
                copy(1 + j, (*chip, c), me).wait_recv()
                passed[j].start()

        @pl.when(step == nsteps - 1)
        def _():
            copy(0, sibling, me).wait_recv()
            for j, chip in enumerate(chips):
                copy(4 + j, (*chip, 1 - c), me).wait_recv()
            for cp in first + passed:
                cp.wait_send()
            mine.wait()

    return pl.pallas_call(
        body, name=name, grid=(n // tn, gj),
        out_shape=[jax.ShapeDtypeStruct((m, n), out_dtype), jax.ShapeDtypeStruct((N_DEV,) + shard.shape, shard.dtype)],
        in_specs=[pl.BlockSpec(memory_space=pltpu.VMEM), pl.BlockSpec((tn, kdim), lambda j, i: (j, 0)),
                  pl.BlockSpec(memory_space=pl.ANY)],
        out_specs=[pl.BlockSpec((tm, tn), lambda j, i: (i, j)), pl.BlockSpec(memory_space=pl.ANY)],
        scratch_shapes=[pltpu.SemaphoreType.DMA((N_DEV - 1,)), pltpu.SemaphoreType.DMA((N_DEV - 1,)),
                        pltpu.SemaphoreType.DMA],
        compiler_params=_cparams(("arbitrary", "arbitrary")),
    )(a, b, shard)


def mm_tn(a, b, name):
    tt, tn = min(MM_TOKEN_TILE, b.shape[0]), _lane_tile(b.shape[1], MM_COL_TILE)
    tka = _lane_tile(a.shape[0] * a.shape[2] if a.ndim == 3 else a.shape[1], 1024)
    if a.ndim == 3:
        t, ka = a.shape[1], a.shape[0] * a.shape[2]
        a_spec = pl.BlockSpec((None, tt, tka), lambda i, j, s: (i, s, 0))
    else:
        t, ka = a.shape
        a_spec = pl.BlockSpec((tt, tka), lambda i, j, s: (s, i))
    n = b.shape[1]
    nt = t // tt

    def body(a_ref, b_ref, o_ref, *acc):
        p = _tn(a_ref[...], b_ref[...])
        if nt == 1:
            o_ref[...] = p.astype(o_ref.dtype)
        else:
            acc_ref, s = acc[0], pl.program_id(2)

            @pl.when(s == 0)
            def _():
                acc_ref[...] = p

            @pl.when(s > 0)
            def _():
                acc_ref[...] += p

            @pl.when(s == nt - 1)
            def _():
                o_ref[...] = acc_ref[...].astype(o_ref.dtype)

    return pl.pallas_call(
        body, name=name, grid=(ka // tka, n // tn, nt),
        out_shape=jax.ShapeDtypeStruct((ka, n), BF16),
        in_specs=[a_spec, pl.BlockSpec((tt, tn), lambda i, j, s: (s, j))],
        out_specs=pl.BlockSpec((tka, tn), lambda i, j, s: (i, j)),
        scratch_shapes=[] if nt == 1 else [pltpu.VMEM((tka, tn), F32)],
        compiler_params=_cparams(("parallel", "parallel", "arbitrary")),
    )(a, b)


def _tile(t, cap):
    return min(cap, t)


def ln_modulate(x, mod6, shift_row, scale_row, name):
    t = x.shape[0]
    tm = _tile(t, WIDE_ROW_TILE)

    def body(x_ref, mod_ref, o_ref):
        xh, _ = _ln(x_ref[...])
        sc = mod_ref[scale_row:scale_row + 1, :]
        sh = mod_ref[shift_row:shift_row + 1, :]
        o_ref[...] = (xh * (1.0 + sc) + sh).astype(BF16)

    return pl.pallas_call(
        body, name=name, grid=(t // tm,),
        out_shape=jax.ShapeDtypeStruct((t, D), BF16),
        in_specs=[pl.BlockSpec((tm, D), lambda i: (i, 0)), pl.BlockSpec((6, D), lambda i: (0, 0))],
        out_specs=pl.BlockSpec((tm, D), lambda i: (i, 0)),
        compiler_params=_cparams(("parallel",)),
    )(x, mod6)


def resid_ln(x, h, mod6, gate_row, ln_g, ln_b, name):
    t = x.shape[0]
    tm = _tile(t, WIDE_ROW_TILE)

    def body(x_ref, h_ref, mod_ref, g_ref, b_ref, o_ref):
        r = ALPHA * x_ref[...] + mod_ref[gate_row:gate_row + 1, :] * h_ref[...]
        rh, _ = _ln(r)
        o_ref[...] = rh * g_ref[...] + b_ref[...]

    row = pl.BlockSpec((tm, D), lambda i: (i, 0))
    vec = pl.BlockSpec((1, D), lambda i: (0, 0))
    return pl.pallas_call(
        body, name=name, grid=(t // tm,),
        out_shape=jax.ShapeDtypeStruct((t, D), F32),
        in_specs=[row, row, pl.BlockSpec((6, D), lambda i: (0, 0)), vec, vec],
        out_specs=row,
        compiler_params=_cparams(("parallel",)),
    )(x, h, mod6, ln_g, ln_b)


def resid_ln_bwd(x, h, mod6, gate_row, ln_g, ln_b, cot, with_loss, name):
    t = x.shape[0]
    tm = _tile(t, ROW_TILE)

    def body(x_ref, h_ref, mod_ref, g_ref, b_ref, c_ref, dh_ref, dx_ref, acc_ref):
        @pl.when(pl.program_id(0) == 0)
        def _():
            acc_ref[...] = jnp.zeros_like(acc_ref)

        gate = mod_ref[gate_row:gate_row + 1, :]
        hv = h_ref[...]
        r = ALPHA * x_ref[...] + gate * hv
        rh, rstd = _ln(r)
        lng = g_ref[...]
        if with_loss:
            diff = rh * lng + b_ref[...] - c_ref[...]
            dxo = diff * (1.0 / D)
            lsum = jnp.sum(_colsum(diff * diff), axis=-1, keepdims=True) * (0.5 / D)
            acc_ref[3:4, :] += jnp.broadcast_to(lsum, (1, D))
        else:
            dxo = c_ref[...]
        acc_ref[1:2, :] += _colsum(dxo * rh)
        acc_ref[2:3, :] += _colsum(dxo)
        dr = _ln_bwd(dxo * lng, rh, rstd)
        acc_ref[0:1, :] += _colsum(dr * hv)
        dh_ref[...] = (gate * dr).astype(BF16)
        dx_ref[...] = ALPHA * dr

    row = pl.BlockSpec((tm, D), lambda i: (i, 0))
    vec = pl.BlockSpec((1, D), lambda i: (0, 0))
    return pl.pallas_call(
        body, name=name, grid=(t // tm,),
        out_shape=[jax.ShapeDtypeStruct((t, D), BF16), jax.ShapeDtypeStruct((t, D), F32),
                   jax.ShapeDtypeStruct((8, D), F32)],
        in_specs=[row, row, pl.BlockSpec((6, D), lambda i: (0, 0)), vec, vec, row],
        out_specs=[row, row, pl.BlockSpec((8, D), lambda i: (0, 0))],
        compiler_params=_cparams(("arbitrary",)),
    )(x, h, mod6, ln_g, ln_b, cot)


def ln_modulate_bwd(x, du, mod6, scale_row, dx_part, name):
    t = x.shape[0]
    tm = _tile(t, ROW_TILE)

    def body(x_ref, du_ref, mod_ref, dp_ref, dx_ref, acc_ref):
        @pl.when(pl.program_id(0) == 0)
        def _():
            acc_ref[...] = jnp.zeros_like(acc_ref)

        xh, rstd = _ln(x_ref[...])
        du_v = du_ref[...]
        sc = mod_ref[scale_row:scale_row + 1, :]
        acc_ref[0:1, :] += _colsum(du_v * xh)
        acc_ref[1:2, :] += _colsum(du_v)
        dx_ref[...] = dp_ref[...] + _ln_bwd(du_v * (1.0 + sc), xh, rstd)

    row = pl.BlockSpec((tm, D), lambda i: (i, 0))
    return pl.pallas_call(
        body, name=name, grid=(t // tm,),
        out_shape=[jax.ShapeDtypeStruct((t, D), F32), jax.ShapeDtypeStruct((8, D), F32)],
        in_specs=[row, row, pl.BlockSpec((6, D), lambda i: (0, 0)), row],
        out_specs=[row, pl.BlockSpec((8, D), lambda i: (0, 0))],
        compiler_params=_cparams(("arbitrary",)),
    )(x, du, mod6, dx_part)


def merge_gates(ya, yb, proj):
    t = ya.shape[0]
    tm = _tile(t, WIDE_ROW_TILE)

    def body(ya_ref, yb_ref, ga_ref, gb_ref, o_ref):
        o_ref[...] = (_sigmoid(ga_ref[...]) * ya_ref[...].astype(F32) +
                      _sigmoid(gb_ref[...]) * yb_ref[...].astype(F32)).astype(BF16)

    row = pl.BlockSpec((tm, D), lambda i: (i, 0))
    return pl.pallas_call(
        body, name="merge_gates", grid=(t // tm,),
        out_shape=jax.ShapeDtypeStruct((t, D), BF16),
        in_specs=[row, row, pl.BlockSpec((tm, D), lambda i: (i, GATE_BLOCK0)),
                  pl.BlockSpec((tm, D), lambda i: (i, GATE_BLOCK0 + 1))],
        out_specs=row,
        compiler_params=_cparams(("parallel",)),
    )(ya, yb, proj, proj)


def merge_gates_bwd(dm, ya, yb, proj):
    t = ya.shape[0]
    tm = _tile(t, ROW_TILE)

    def body(dm_ref, ya_ref, yb_ref, ga_ref, gb_ref, dya_ref, dyb_ref, dp_ref):
        dmv = dm_ref[...].astype(F32)
        sa = _sigmoid(ga_ref[...])
        sb = _sigmoid(gb_ref[...])
        dya_ref[...] = (dmv * sa).astype(BF16)
        dyb_ref[...] = (dmv * sb).astype(BF16)
        dp_ref[0] = (dmv * ya_ref[...].astype(F32) * sa * (1.0 - sa)).astype(BF16)
        dp_ref[1] = (dmv * yb_ref[...].astype(F32) * sb * (1.0 - sb)).astype(BF16)

    row = pl.BlockSpec((tm, D), lambda i: (i, 0))
    return pl.pallas_call(
        body, name="merge_gates_bwd", grid=(t // tm,),
        out_shape=[jax.ShapeDtypeStruct((t, D), BF16)] * 2 + [jax.ShapeDtypeStruct((N_PROJ // D, t, D), BF16)],
        in_specs=[row, row, row, pl.BlockSpec((tm, D), lambda i: (i, GATE_BLOCK0)),
                  pl.BlockSpec((tm, D), lambda i: (i, GATE_BLOCK0 + 1))],
        out_specs=[row, row, pl.BlockSpec((2, tm, D), lambda i: (GATE_BLOCK0 // 2, i, 0))],
        compiler_params=_cparams(("parallel",)),
    )(dm, ya, yb, proj, proj)


FF_CHUNK = 1408


def swiglu_act(gu):
    t = gu.shape[0]
    tm = _tile(t, FFN_ROW_TILE)

    def body(gu_ref, o_ref):
        for j in range(D_FF // FF_CHUNK):
            cs = slice(j * FF_CHUNK, (j + 1) * FF_CHUNK)
            g = gu_ref[:, cs].astype(F32)
            u = gu_ref[:, D_FF + j * FF_CHUNK:D_FF + (j + 1) * FF_CHUNK].astype(F32)
            o_ref[:, cs] = (g * _sigmoid(g) * u).astype(BF16)

    return pl.pallas_call(
        body, name="swiglu_act", grid=(t // tm,),
        out_shape=jax.ShapeDtypeStruct((t, D_FF), BF16),
        in_specs=[pl.BlockSpec((tm, 2 * D_FF), lambda i: (i, 0))],
        out_specs=pl.BlockSpec((tm, D_FF), lambda i: (i, 0)),
        compiler_params=_cparams(("parallel",)),
    )(gu)


def swiglu_act_bwd(gu, dact):
    t = gu.shape[0]
    tm = _tile(t, FFN_ROW_TILE)

    def body(gu_ref, da_ref, o_ref):
        for j in range(D_FF // FF_CHUNK):
            cs = slice(j * FF_CHUNK, (j + 1) * FF_CHUNK)
            us = slice(D_FF + j * FF_CHUNK, D_FF + (j + 1) * FF_CHUNK)
            g = gu_ref[:, cs].astype(F32)
            u = gu_ref[:, us].astype(F32)
            da = da_ref[:, cs].astype(F32)
            s = _sigmoid(g)
            o_ref[:, cs] = (da * u * _dsilu(g, s)).astype(BF16)
            o_ref[:, us] = (da * g * s).astype(BF16)

    return pl.pallas_call(
        body, name="swiglu_act_bwd", grid=(t // tm,),
        out_shape=jax.ShapeDtypeStruct((t, 2 * D_FF), BF16),
        in_specs=[pl.BlockSpec((tm, 2 * D_FF), lambda i: (i, 0)), pl.BlockSpec((tm, D_FF), lambda i: (i, 0))],
        out_specs=pl.BlockSpec((tm, 2 * D_FF), lambda i: (i, 0)),
        compiler_params=_cparams(("parallel",)),
    )(gu, dact)


def _hgrn_chunk_terms(q, fl, lbv, tril_f):
    sig = _sigmoid(fl)
    f = lbv + (1.0 - lbv) * sig
    lam = jnp.log(f)
    k = 1.0 - f
    sq = _sigmoid(q)
    qt = q * sq * Q_SCALE
    bc = _sel(_nn, lam, tril_f, 3, x_first=False)
    bmid = bc[CHUNK // 2 - 1:CHUNK // 2, :]
    bl = bc[CHUNK - 1:CHUNK, :]
    eq = jnp.exp(jnp.minimum(bc - bmid, EXP_CLIP))
    ek = jnp.exp(jnp.minimum(bmid - bc, EXP_CLIP))
    eb = jnp.exp(bc)
    ekl = jnp.exp(bl - bc)
    ebl = jnp.exp(bl)
    return sig, f, k, sq, qt, eq, ek, eb, ekl, ebl


def hgrn_fwd(proj, lb, gnorm):
    t = proj.shape[0]
    tb = _tile(t, TOKEN_BLOCK)
    ncb = tb // CHUNK

    hps = HGRN_HEADS_PER_STEP
    wide = hps * HK

    def body(q_ref, f_ref, i_ref, g_ref, lb_ref, gn_ref, oa_ref, oraw_ref, st_ref, state):
        @pl.when(pl.program_id(1) == 0)
        def _():
            state[...] = jnp.zeros_like(state)

        gn = gn_ref[...]
        mask = _tri(CHUNK)
        tril_f = mask.astype(BF16)

        def chunk(c, carry):
            sl = pl.ds(pl.multiple_of(c * CHUNK, CHUNK), CHUNK)
            for hh in range(hps):
                ln = slice(hh * HK, (hh + 1) * HK)
                q, fl, v, g = q_ref[sl, ln], f_ref[sl, ln], i_ref[sl, ln], g_ref[sl, ln]
                sig, f, k, sq, qt, eq, ek, eb, ekl, ebl = _hgrn_chunk_terms(q, fl, lb_ref[:, ln], tril_f)
                a = jnp.where(mask, _nt((qt * eq).astype(BF16), (k * ek).astype(BF16)), 0.0)
                st = state[hh]
                st_ref[hh, c] = st
                vb = v.astype(BF16)
                o = _nn(a.astype(BF16), vb) + _nt((qt * eb).astype(BF16), st.astype(BF16))
                state[hh] = st * ebl + _tn(vb, (k * ekl).astype(BF16))
                oraw_ref[sl, ln] = o
                rn = o * lax.rsqrt(jnp.mean(o * o, axis=-1, keepdims=True) + RMS_EPS)
                oa_ref[sl, ln] = (rn * gn * g * _sigmoid(g)).astype(BF16)
            return carry

        lax.fori_loop(0, ncb, chunk, 0, unroll=min(CHUNK_UNROLL, ncb))

    def col(block):
        return pl.BlockSpec((tb, wide), lambda h, j: (j, block * (N_HEADS_A // hps) + h))

    return pl.pallas_call(
        body, name="hgrn_fwd", grid=(N_HEADS_A // hps, t // tb),
        out_shape=[jax.ShapeDtypeStruct((t, D), BF16), jax.ShapeDtypeStruct((t, D), F32),
                   jax.ShapeDtypeStruct((N_HEADS_A, t // CHUNK, HK, HK), F32)],
        in_specs=[col(0), col(1), col(2), col(3), pl.BlockSpec((1, wide), lambda h, j: (0, h)),
                  pl.BlockSpec((1, HK), lambda h, j: (0, 0))],
        out_specs=[pl.BlockSpec((tb, wide), lambda h, j: (j, h)), pl.BlockSpec((tb, wide), lambda h, j: (j, h)),
                   pl.BlockSpec((hps, ncb, HK, HK), lambda h, j: (h, j, 0, 0))],
        scratch_shapes=[pltpu.VMEM((hps, HK, HK), F32)],
        compiler_params=_cparams(("parallel", "arbitrary")),
    )(proj, proj, proj, proj, lb, gnorm)


def hgrn_bwd(proj, lb, gnorm, o_raw, doa, states, give, dproj):
    t = proj.shape[0]
    tb = _tile(t, TOKEN_BLOCK)
    ncb = tb // CHUNK
    nb = t // tb
    hps = HGRN_HEADS_PER_STEP
    wide = hps * HK

    def body(q_ref, f_ref, i_ref, g_ref, lb_ref, gn_ref, oraw_ref, doa_ref, st_ref, give_ref, dp_in_ref,
             dp_ref, dlb_ref, dgn_ref, got_ref, dstate, send_sem, recv_sem):
        h, j = pl.program_id(0), pl.program_id(1)
        swap_start, swap_wait = _sibling_exchange(give_ref, got_ref, send_sem, recv_sem)

        @pl.when((h == 0) & (j == 0))
        def _():
            swap_start()

        @pl.when(j == 0)
        def _():
            dstate[...] = jnp.zeros_like(dstate)
            dlb_ref[...] = jnp.zeros_like(dlb_ref)

        @pl.when((j == 0) & (h == 0))
        def _():
            dgn_ref[...] = jnp.zeros_like(dgn_ref)

        gn = gn_ref[...]
        mask = _tri(CHUNK)
        mask_t = _tri(CHUNK, upper=True)
        tril_f = mask.astype(BF16)
        triu_f = mask_t.astype(BF16)

        def chunk(i, c0):
            c = ncb - 1 - i
            sl = pl.ds(pl.multiple_of(c * CHUNK, CHUNK), CHUNK)
            for hh in range(hps):
                ln = slice(hh * HK, (hh + 1) * HK)
                q, fl, v, g = q_ref[sl, ln], f_ref[sl, ln], i_ref[sl, ln], g_ref[sl, ln]
                lbv = lb_ref[:, ln]
                sig, f, k, sq, qt, eq, ek, eb, ekl, ebl = _hgrn_chunk_terms(q, fl, lbv, tril_f)
                qe = (qt * eq).astype(BF16)
                ke = (k * ek).astype(BF16)
                st32 = st_ref[hh, c]
                st = st32.astype(BF16)
                dst = dstate[hh]
                dstb = dst.astype(BF16)
                o = oraw_ref[sl, ln]
                rstd = lax.rsqrt(jnp.mean(o * o, axis=-1, keepdims=True) + RMS_EPS)
                rn = o * rstd
                sgm = _sigmoid(g)
                sg = g * sgm
                doa_v = doa_ref[sl, ln]
                drn = doa_v * gn * sg
                dgn_ref[...] += _colsum(doa_v * rn * sg)
                dp_ref[3, sl, ln] = (doa_v * rn * gn * _dsilu(g, sgm)).astype(BF16)
                do = rstd * (drn - rn * jnp.mean(drn * rn, axis=-1, keepdims=True))
                dob = do.astype(BF16)
                vb = v.astype(BF16)
                da = jnp.where(mask, _nt(dob, vb), 0.0).astype(BF16)
                da_t = jnp.where(mask_t, _nt(vb, dob), 0.0).astype(BF16)
                a_t = jnp.where(mask_t, _nt(ke, qe), 0.0).astype(BF16)
                kl = (k * ekl).astype(BF16)
                qb = (qt * eb).astype(BF16)
                dq_in = _nn(da, ke)
                dk_in = _nn(da_t, qe)
                dq_out = eb * _nn(dob, st)
                dk_out = ekl * _nn(vb, dstb)
                dqt = eq * dq_in + dq_out
                dk = ek * dk_in + dk_out
                dv = _nn(a_t, dob) + _nt(kl, dstb)
                dstate[hh] = dst * ebl + _tn(dob, qb)
                dbig = qe.astype(F32) * dq_in - ke.astype(F32) * dk_in + qt * dq_out - k * dk_out
                beyond = _colsum(k * dk_out) + ebl * _colsum(dst * st32)
                dlam = _sel(_nn, dbig, triu_f, 3, x_first=False) + beyond
                df = dlam / f - dk
                dp_ref[1, sl, ln] = (df * (1.0 - lbv) * sig * (1.0 - sig)).astype(BF16)
                dlb_ref[:, ln] += _colsum(df * (1.0 - sig))
                dp_ref[0, sl, ln] = (dqt * Q_SCALE * _dsilu(q, sq)).astype(BF16)
                dp_ref[2, sl, ln] = dv.astype(BF16)
            return c0

        lax.fori_loop(0, ncb, chunk, 0, unroll=min(CHUNK_UNROLL, ncb))

        @pl.when((h == N_HEADS_A // hps - 1) & (j == nb - 1))
        def _():
            swap_wait()

    def col(block):
        return pl.BlockSpec((tb, wide), lambda h, j: (nb - 1 - j, block * (N_HEADS_A // hps) + h))

    hcol = pl.BlockSpec((tb, wide), lambda h, j: (nb - 1 - j, h))
    hbm = pl.BlockSpec(memory_space=pl.ANY)
    return pl.pallas_call(
        body, name="hgrn_bwd", grid=(N_HEADS_A // hps, nb),
        out_shape=[jax.ShapeDtypeStruct(dproj.shape, dproj.dtype), jax.ShapeDtypeStruct((1, D), F32),
                   jax.ShapeDtypeStruct((1, HK), F32), jax.ShapeDtypeStruct(give.shape, give.dtype)],
        in_specs=[col(0), col(1), col(2), col(3), pl.BlockSpec((1, wide), lambda h, j: (0, h)),
                  pl.BlockSpec((1, HK), lambda h, j: (0, 0)), hcol, hcol,
                  pl.BlockSpec((hps, ncb, HK, HK), lambda h, j: (h, nb - 1 - j, 0, 0)), hbm, hbm],
        out_specs=[pl.BlockSpec((4, tb, wide), lambda h, j: (0, nb - 1 - j, h)),
                   pl.BlockSpec((1, wide), lambda h, j: (0, h)), pl.BlockSpec((1, HK), lambda h, j: (0, 0)), hbm],
        input_output_aliases={10: 0},
        scratch_shapes=[pltpu.VMEM((hps, HK, HK), F32)] + SIBLING_SEMS,
        compiler_params=_cparams(("arbitrary", "arbitrary")),
    )(proj, proj, proj, proj, lb, gnorm, o_raw, doa, states, give, dproj)


CONV_BLOCK0 = 6
CONV_TAPS = 4
HALO = 8


def conv_fwd(proj, conv_w, conv_b):
    t = proj.shape[0]
    tm = _tile(t, ROW_TILE)
    r = tm // HALO

    def body(x_ref, halo_ref, w_ref, b_ref, o_ref, ds_ref):
        i = pl.program_id(1)
        halo = jnp.where(i > 0, halo_ref[...], 0.0)
        ext = jnp.concatenate([halo, x_ref[...]], axis=0)
        pre = b_ref[...] + w_ref[CONV_TAPS - 1:CONV_TAPS, :] * ext[HALO:, :]
        for tap in range(CONV_TAPS - 1):
            pre = pre + w_ref[tap:tap + 1, :] * pltpu.roll(ext, CONV_TAPS - 1 - tap, axis=0)[HALO:, :]
        s = _sigmoid(pre)
        o_ref[...] = pre * s
        ds_ref[...] = _dsilu(pre, s).astype(BF16)

    blk = pl.BlockSpec((tm, D), lambda cb, i: (i, cb))
    return pl.pallas_call(
        body, name="conv_fwd", grid=(CONV_DIM // D, t // tm),
        out_shape=[jax.ShapeDtypeStruct((t, CONV_DIM), F32), jax.ShapeDtypeStruct((t, CONV_DIM), BF16)],
        in_specs=[pl.BlockSpec((tm, D), lambda cb, i: (i, CONV_BLOCK0 + cb)),
                  pl.BlockSpec((HALO, D), lambda cb, i: (jnp.maximum(i * r - 1, 0), CONV_BLOCK0 + cb)),
                  pl.BlockSpec((CONV_TAPS, D), lambda cb, i: (0, cb)), pl.BlockSpec((1, D), lambda cb, i: (0, cb))],
        out_specs=[blk, blk],
        compiler_params=_cparams(("parallel", "parallel")),
    )(proj, proj, conv_w, conv_b)


def conv_bwd(proj, dxc, dsilu, conv_w, dproj):
    t = proj.shape[0]
    tm = _tile(t, ROW_TILE)
    r = tm // HALO
    n = t // tm
    last_halo = t // HALO - 1

    def body(x_ref, prev_ref, d_ref, dnext_ref, s_ref, snext_ref, w_ref, dp_in_ref, dx_ref, dw_ref, db_ref):
        i = pl.program_id(1)

        @pl.when(i == 0)
        def _():
            dw_ref[...] = jnp.zeros_like(dw_ref)
            db_ref[...] = jnp.zeros_like(db_ref)

        dpre = jnp.concatenate([d_ref[...].astype(F32) * s_ref[...].astype(F32),
                                jnp.where(i < n - 1, dnext_ref[0:HALO, :].astype(F32) * snext_ref[0:HALO, :].astype(F32),
                                          0.0)], axis=0)
        dx = w_ref[CONV_TAPS - 1:CONV_TAPS, :] * dpre[:tm, :]
        for tap in range(CONV_TAPS - 1):
            back = CONV_TAPS - 1 - tap
            dx = dx + w_ref[tap:tap + 1, :] * pltpu.roll(dpre, tm + HALO - back, axis=0)[:tm, :]
        dx_ref[...] = dx.astype(BF16)
        dp = dpre[:tm, :]
        db_ref[...] += _colsum(dp)
        prev = jnp.where(i > 0, prev_ref[...], 0.0)
        ext = jnp.concatenate([prev, x_ref[...]], axis=0)
        dw_ref[CONV_TAPS - 1:CONV_TAPS, :] += _colsum(dp * ext[HALO:, :])
        for tap in range(CONV_TAPS - 1):
            dw_ref[tap:tap + 1, :] += _colsum(dp * pltpu.roll(ext, CONV_TAPS - 1 - tap, axis=0)[HALO:, :])

    blk = pl.BlockSpec((tm, D), lambda cb, i: (i, cb))
    nxt = pl.BlockSpec((2 * HALO, D), lambda cb, i: (jnp.minimum((i + 1) * (r // 2), last_halo // 2), cb))
    return pl.pallas_call(
        body, name="conv_bwd", grid=(CONV_DIM // D, n),
        out_shape=[jax.ShapeDtypeStruct(dproj.shape, dproj.dtype), jax.ShapeDtypeStruct((8, CONV_DIM), F32),
                   jax.ShapeDtypeStruct((1, CONV_DIM), F32)],
        in_specs=[pl.BlockSpec((tm, D), lambda cb, i: (i, CONV_BLOCK0 + cb)),
                  pl.BlockSpec((HALO, D), lambda cb, i: (jnp.maximum(i * r - 1, 0), CONV_BLOCK0 + cb)),
                  blk, nxt, blk, nxt,
                  pl.BlockSpec((CONV_TAPS, D), lambda cb, i: (0, cb)), pl.BlockSpec(memory_space=pl.ANY)],
        out_specs=[pl.BlockSpec((None, tm, D), lambda cb, i: (CONV_BLOCK0 + cb, i, 0)),
                   pl.BlockSpec((8, D), lambda cb, i: (0, cb)), pl.BlockSpec((1, D), lambda cb, i: (0, cb))],
        input_output_aliases={7: 0},
        compiler_params=_cparams(("parallel", "arbitrary")),
    )(proj, proj, dxc, dxc, dsilu, dsilu, conv_w, dproj)


def dt_fill(ddt, dproj):
    t = ddt.shape[0]
    tm = _tile(t, WIDE_ROW_TILE)
    w = ddt.shape[1]

    def body(d_ref, dp_in_ref, o_ref):
        o_ref[:, :w] = d_ref[...]
        o_ref[:, w:] = jnp.zeros((tm, D - w), o_ref.dtype)

    return pl.pallas_call(
        body, name="dt_fill", grid=(t // tm,),
        out_shape=jax.ShapeDtypeStruct(dproj.shape, dproj.dtype),
        in_specs=[pl.BlockSpec((tm, w), lambda i: (i, 0)), pl.BlockSpec(memory_space=pl.ANY)],
        out_specs=pl.BlockSpec((None, tm, D), lambda i: (DT_COL_BLOCK, i, 0)),
        input_output_aliases={1: 0},
        compiler_params=_cparams(("parallel",)),
    )(ddt, dproj)


Z_BLOCK0 = 8
DT_COL_BLOCK = 9
DT_BLOCK0 = 8 * DT_COL_BLOCK
GATE_BLOCK0 = 10
B_BLOCK0 = 16
C_BLOCK0 = 20


def _head_expand():
    e = np.zeros((N_STATE, GROUP_W), np.float32)
    for hh in range(HEADS_PER_GROUP):
        e[hh, hh * HEAD_P:(hh + 1) * HEAD_P] = 1.0
    return jnp.asarray(e, BF16)


def _ssd_chunk_terms(dt, bias, alog, expand, tril_f, eye):
    dtb = dt + bias
    delta = jnp.maximum(dtb, 0.0) + jnp.log(1.0 + jnp.exp(-jnp.abs(dtb)))
    ea = jnp.exp(alog)
    a = -ea * delta
    acum = _sel(_nn, a, tril_f, 3, x_first=False)
    delta_e = _sel(_nn, delta, expand, 2)
    acum_e = _sel(_nn, acum, expand, 2)
    acum_t = _sel(_nt, acum, eye, 3, x_first=False)
    return dtb, delta, ea, a, acum, delta_e, acum_e, acum_t


def ssd_fwd(proj, xc, alog4, bias4, dskip4, wnorm, expand):
    t = proj.shape[0]
    tb = _tile(t, TOKEN_BLOCK)
    ncb = tb // SSD_CHUNK

    def body(xs_ref, b_ref, c_ref, dt_ref, z_ref, alog_ref, bias_ref, dsk_ref, wn_ref, e_ref, ob_ref, st_ref, state):
        @pl.when(pl.program_id(1) == 0)
        def _():
            state[...] = jnp.zeros_like(state)

        expand = e_ref[...]
        mask = _tri(SSD_CHUNK)
        tril_f = mask.astype(BF16)
        eye = (lax.broadcasted_iota(jnp.int32, (N_STATE, N_STATE), 0) ==
               lax.broadcasted_iota(jnp.int32, (N_STATE, N_STATE), 1)).astype(BF16)
        alog, bias = alog_ref[0], bias_ref[0]
        d_e = _sel(_nn, jnp.broadcast_to(dsk_ref[0], (8, N_STATE)), expand, 3)[0:1, :]
        wn = wn_ref[...]

        def chunk(c, carry):
            sl = pl.ds(pl.multiple_of(c * SSD_CHUNK, SSD_CHUNK), SSD_CHUNK)
            xs, bm, cm, dt, z = xs_ref[sl, :], b_ref[sl, :], c_ref[sl, :], dt_ref[sl, :], z_ref[sl, :]
            dtb, delta, ea, a, acum, delta_e, acum_e, acum_t = _ssd_chunk_terms(dt, bias, alog, expand, tril_f, eye)
            alast_e = acum_e[SSD_CHUNK - 1:SSD_CHUNK, :]
            xd = xs * delta_e
            xdb = xd.astype(BF16)
            cb_, bb_ = cm.astype(BF16), bm.astype(BF16)
            cbm = _nt(cb_, bb_)
            ys = []
            for hh in range(HEADS_PER_GROUP):
                lh = jnp.where(mask, jnp.exp(jnp.minimum(acum[:, hh:hh + 1] - acum_t[hh:hh + 1, :], 0.0)), 0.0)
                ys.append(_nn((cbm * lh).astype(BF16), xdb[:, hh * HEAD_P:(hh + 1) * HEAD_P]))
            st = state[...]
            st_ref[0, c] = st
            y = jnp.concatenate(ys, axis=1) + _nn(cb_, st.astype(BF16)) * jnp.exp(acum_e) + xs * d_e
            state[...] = st * jnp.exp(alast_e) + _tn(bb_, (xd * jnp.exp(alast_e - acum_e)).astype(BF16))
            yg = y * z * _sigmoid(z)
            ob_ref[sl, :] = (yg * lax.rsqrt(jnp.mean(yg * yg, axis=-1, keepdims=True) + RMS_EPS) * wn).astype(BF16)
            return carry

        lax.fori_loop(0, ncb, chunk, 0, unroll=min(CHUNK_UNROLL, ncb))

    small = pl.BlockSpec((1, 1, N_STATE), lambda g, j: (g, 0, 0))
    return pl.pallas_call(
        body, name="ssd_fwd", grid=(N_GROUPS, t // tb),
        out_shape=[jax.ShapeDtypeStruct((t, B_INNER), BF16),
                   jax.ShapeDtypeStruct((N_GROUPS, t // SSD_CHUNK, N_STATE, GROUP_W), F32)],
        in_specs=[pl.BlockSpec((tb, GROUP_W), lambda g, j: (j, g)),
                  pl.BlockSpec((tb, N_STATE), lambda g, j: (j, B_BLOCK0 + g)),
                  pl.BlockSpec((tb, N_STATE), lambda g, j: (j, C_BLOCK0 + g)),
                  pl.BlockSpec((tb, N_STATE), lambda g, j: (j, DT_BLOCK0 + g)),
                  pl.BlockSpec((tb, GROUP_W), lambda g, j: (j, Z_BLOCK0 + g)),
                  small, small, small, pl.BlockSpec((1, GROUP_W), lambda g, j: (0, g)),
                  pl.BlockSpec((N_STATE, GROUP_W), lambda g, j: (0, 0))],
        out_specs=[pl.BlockSpec((tb, GROUP_W), lambda g, j: (j, g)),
                   pl.BlockSpec((1, ncb, N_STATE, GROUP_W), lambda g, j: (g, j, 0, 0))],
        scratch_shapes=[pltpu.VMEM((N_STATE, GROUP_W), F32)],
        compiler_params=_cparams(("parallel", "arbitrary")),
    )(xc, xc, xc, proj, proj, alog4, bias4, dskip4, wnorm, expand)


def ssd_bwd(proj, xc, alog4, bias4, dskip4, wnorm, expand, dob, states, part, dproj):
    t = proj.shape[0]
    tb = _tile(t, TOKEN_BLOCK)
    lc = min(SSD_CHUNK_BWD, tb)
    ncb = tb // lc
    nsaved = tb // SSD_CHUNK
    nb = t // tb

    def body(xs_ref, b_ref, c_ref, dt_ref, z_ref, alog_ref, bias_ref, dsk_ref, wn_ref, e_ref, dob_ref, st_ref, part_ref,
             dp_in_ref, dxs_ref, db_ref, dc_ref, dz_ref, ddt_ref, dwn_ref, dalog_ref, dbias_ref, ddsk_ref, parts_ref, dstate,
             send_sems, recv_sems, local_sem):
        xchg_start, xchg_wait = _chip_exchange(part_ref, parts_ref, send_sems, recv_sems, local_sem)

        @pl.when((pl.program_id(0) == 0) & (pl.program_id(1) == 0))
        def _():
            xchg_start()

        @pl.when(pl.program_id(1) == 0)
        def _():
            dstate[...] = jnp.zeros_like(dstate)
            dwn_ref[...] = jnp.zeros_like(dwn_ref)
            dalog_ref[...] = jnp.zeros_like(dalog_ref)
            dbias_ref[...] = jnp.zeros_like(dbias_ref)
            ddsk_ref[...] = jnp.zeros_like(ddsk_ref)

        expand = e_ref[...]
        mask = _tri(lc)
        mask_t = _tri(lc, upper=True)
        tril_f = mask.astype(BF16)
        triu_f = mask_t.astype(BF16)
        eye = (lax.broadcasted_iota(jnp.int32, (N_STATE, N_STATE), 0) ==
               lax.broadcasted_iota(jnp.int32, (N_STATE, N_STATE), 1)).astype(BF16)
        alog, bias = alog_ref[0], bias_ref[0]
        d_e = _sel(_nn, jnp.broadcast_to(dsk_ref[0], (8, N_STATE)), expand, 3)[0:1, :]
        wn = wn_ref[...]

        def chunk(i, c0):
            c = ncb - 1 - i
            sl = pl.ds(pl.multiple_of(c * lc, lc), lc)
            xs, bm, cm, dt, z = xs_ref[sl, :], b_ref[sl, :], c_ref[sl, :], dt_ref[sl, :], z_ref[sl, :]
            dtb, delta, ea, a, acum, delta_e, acum_e, acum_t = _ssd_chunk_terms(dt, bias, alog, expand, tril_f, eye)
            alast_e = acum_e[lc - 1:lc, :]
            eacum = jnp.exp(acum_e)
            wl = jnp.exp(alast_e - acum_e)
            xd = xs * delta_e
            xdb = xd.astype(BF16)
            cb_, bb_ = cm.astype(BF16), bm.astype(BF16)
            cbm = _nt(cb_, bb_)
            st32 = st_ref[0, c * (lc // SSD_CHUNK)]
            stb = st32.astype(BF16)
            dst = dstate[...]
            dstb = dst.astype(BF16)
            lhs, mixes, ys = [], [], []
            for hh in range(HEADS_PER_GROUP):
                col, row = acum[:, hh:hh + 1], acum_t[hh:hh + 1, :]
                lh = jnp.where(mask, jnp.exp(jnp.minimum(col - row, 0.0)), 0.0)
                mix = (cbm * lh).astype(BF16)
                lhs.append(lh)
                mixes.append(mix)
                ys.append(_nn(mix, xdb[:, hh * HEAD_P:(hh + 1) * HEAD_P]))
            y_in = jnp.concatenate(ys, axis=1)
            y_out = _nn(cb_, stb) * eacum
            y = y_in + y_out + xs * d_e
            sgz = _sigmoid(z)
            sz = z * sgz
            yg = y * sz
            rstd = lax.rsqrt(jnp.mean(yg * yg, axis=-1, keepdims=True) + RMS_EPS)
            nrm = yg * rstd
            dob_v = dob_ref[sl, :]
            dn = dob_v * wn
            dwn_ref[...] += _colsum(dob_v * nrm)
            dyg = rstd * (dn - nrm * jnp.mean(dn * nrm, axis=-1, keepdims=True))
            dy = dyg * sz
            dz_ref[sl, :] = (dyg * y * _dsilu(z, sgz)).astype(BF16)
            dyb = dy.astype(BF16)
            dxds = []
            dcb = jnp.zeros((lc, lc), F32)
            for hh in range(HEADS_PER_GROUP):
                hs = slice(hh * HEAD_P, (hh + 1) * HEAD_P)
                dy_h, x_h = dyb[:, hs], xdb[:, hs]
                dxds.append(_tn(mixes[hh], dy_h))
                dcb = dcb + _nt(dy_h, x_h) * lhs[hh]
            dcbb = dcb.astype(BF16)
            dye = (dy * eacum).astype(BF16)
            xw = (xd * wl).astype(BF16)
            dxd_in = jnp.concatenate(dxds, axis=1)
            dxd_out = wl * _nn(bb_, dstb)
            dxd = dxd_in + dxd_out
            dc_ref[sl, :] = (_nn(dcbb, bb_) + _nt(dye, stb)).astype(dc_ref.dtype)
            db_ref[sl, :] = (_tn(dcbb, cb_) + _nt(xw, dstb)).astype(db_ref.dtype)
            dstate[...] = dst * jnp.exp(alast_e) + _tn(cb_, dye)
            col_out = xd * dxd_out
            dac = _sel(_nt, dyb.astype(F32) * y_in - xdb.astype(F32) * dxd_in + dy * y_out - col_out, expand, 2)
            beyond = _colsum(col_out) + jnp.exp(alast_e) * _colsum(dst * st32)
            da = (_sel(_nn, dac, triu_f, 3, x_first=False) +
                  _sel(_nt, jnp.broadcast_to(beyond, (8, GROUP_W)), expand, 3)[0:1, :])
            ddelta = _sel(_nt, dxd * xs, expand, 2) - da * ea
            dalog_ref[0] += _colsum(da * a)
            ddtb = ddelta * _sigmoid(dtb)
            dbias_ref[0] += _colsum(ddtb)
            ddt_ref[sl, :] = ddtb.astype(BF16)
            ddsk_ref[0] += _sel(_nt, jnp.broadcast_to(_colsum(dy * xs), (8, GROUP_W)), expand, 3)[0:1, :]
            dxs_ref[sl, :] = (dxd * delta_e + dy * d_e).astype(dxs_ref.dtype)
            return c0

        lax.fori_loop(0, ncb, chunk, 0, unroll=min(CHUNK_UNROLL, ncb))

        @pl.when((pl.program_id(0) == N_GROUPS - 1) & (pl.program_id(1) == nb - 1))
        def _():
            xchg_wait()

    small = pl.BlockSpec((1, 1, N_STATE), lambda g, j: (g, 0, 0))
    wide = pl.BlockSpec((tb, GROUP_W), lambda g, j: (nb - 1 - j, g))
    narrow = pl.BlockSpec((tb, N_STATE), lambda g, j: (nb - 1 - j, g))
    hbm = pl.BlockSpec(memory_space=pl.ANY)
    return pl.pallas_call(
        body, name="ssd_bwd", grid=(N_GROUPS, nb),
        out_shape=[jax.ShapeDtypeStruct((t, B_INNER), BF16), jax.ShapeDtypeStruct((t, GROUP_W), BF16),
                   jax.ShapeDtypeStruct((t, GROUP_W), BF16), jax.ShapeDtypeStruct(dproj.shape, dproj.dtype),
                   jax.ShapeDtypeStruct((t, GROUP_W), BF16), jax.ShapeDtypeStruct((1, B_INNER), F32),
                   jax.ShapeDtypeStruct((N_GROUPS, 1, N_STATE), F32), jax.ShapeDtypeStruct((N_GROUPS, 1, N_STATE), F32),
                   jax.ShapeDtypeStruct((N_GROUPS, 1, N_STATE), F32), jax.ShapeDtypeStruct(part.shape, part.dtype)],
        in_specs=[wide,
                  pl.BlockSpec((tb, N_STATE), lambda g, j: (nb - 1 - j, B_BLOCK0 + g)),
                  pl.BlockSpec((tb, N_STATE), lambda g, j: (nb - 1 - j, C_BLOCK0 + g)),
                  pl.BlockSpec((tb, N_STATE), lambda g, j: (nb - 1 - j, DT_BLOCK0 + g)),
                  pl.BlockSpec((tb, GROUP_W), lambda g, j: (nb - 1 - j, Z_BLOCK0 + g)),
                  small, small, small, pl.BlockSpec((1, GROUP_W), lambda g, j: (0, g)),
                  pl.BlockSpec((N_STATE, GROUP_W), lambda g, j: (0, 0)), wide,
                  pl.BlockSpec((1, nsaved, N_STATE, GROUP_W), lambda g, j: (g, nb - 1 - j, 0, 0)), hbm, hbm],
        out_specs=[wide, narrow, narrow,
                   pl.BlockSpec((None, tb, GROUP_W), lambda g, j: (Z_BLOCK0 // 2 + g // 2, nb - 1 - j, g % 2)),
                   narrow, pl.BlockSpec((1, GROUP_W), lambda g, j: (0, g)), small, small, small, hbm],
        input_output_aliases={13: 3},
        scratch_shapes=[pltpu.VMEM((N_STATE, GROUP_W), F32)] + CHIP_SEMS,
        compiler_params=_cparams(("arbitrary", "arbitrary")),
    )(xc, xc, xc, proj, proj, alog4, bias4, dskip4, wnorm, expand, dob, states, part, dproj)


def lower_bound_fwd(hgrn_lb):
    def body(a_ref, o_ref):
        a0, a1 = a_ref[0:1, :], a_ref[1:2, :]
        m = jnp.maximum(a0, a1)
        e0, e1 = jnp.exp(a0 - m), jnp.exp(a1 - m)
        o_ref[...] = e0 / (e0 + e1)

    return pl.pallas_call(body, name="lower_bound_fwd", out_shape=jax.ShapeDtypeStruct((1, D), F32))(hgrn_lb)


def ada_weight_grad(c_all, dmod_cols):
    def body(c_ref, d_ref, o_ref):
        cval = c_ref[...]
        o_ref[...] = _tn(cval * _sigmoid(cval), d_ref[...], HI)

    return pl.pallas_call(body, name="ada_weight_grad",
                          out_shape=jax.ShapeDtypeStruct((D, dmod_cols.shape[1]), F32))(c_all, dmod_cols)


def reduce_small(gathered, hgrn_lb, dlb_off):
    n = gathered.shape[2]

    def body(g_ref, a_ref, o_ref, glb_ref):
        s = g_ref[0]
        for d in range(1, N_DEV):
            s = s + g_ref[d]
        o_ref[...] = s
        a0, a1 = a_ref[0:1, :], a_ref[1:2, :]
        m = jnp.maximum(a0, a1)
        e0, e1 = jnp.exp(a0 - m), jnp.exp(a1 - m)
        p0 = e0 / (e0 + e1)
        tq = s[:, dlb_off:dlb_off + D] * p0 * (1.0 - p0)
        glb_ref[0:1, :] = tq
        glb_ref[1:2, :] = -tq

    return pl.pallas_call(body, name="reduce_small",
                          out_shape=[jax.ShapeDtypeStruct((1, n), F32), jax.ShapeDtypeStruct((2, D), F32)])(gathered, hgrn_lb)


def _adam_math(w, g, m, v):
    m2 = ADAM_B1 * m + (1.0 - ADAM_B1) * g
    v2 = ADAM_B2 * v + (1.0 - ADAM_B2) * (g * g)
    m_hat = m2 / (1.0 - ADAM_B1 ** ADAM_STEP)
    v_hat = v2 / (1.0 - ADAM_B2 ** ADAM_STEP)
    delta = -ADAM_LR * (m_hat / (jnp.sqrt(v_hat) + ADAM_EPS) + ADAM_WD * w)
    return delta, m2, v2


def _row_tile(rows, mult=8, cap=128):
    for cand in range(cap - cap % mult, 0, -mult):
        if rows % cand == 0:
            return cand
    return rows


def sum_parts(parts, name):
    n, rows, cols = parts.shape
    tr = _row_tile(rows, 16, 1024)

    def body(p_ref, o_ref):
        s = p_ref[0].astype(F32)
        for d in range(1, n):
            s = s + p_ref[d].astype(F32)
        o_ref[...] = s

    return pl.pallas_call(
        body, name=name, grid=(rows // tr,),
        out_shape=jax.ShapeDtypeStruct((rows, cols), F32),
        in_specs=[pl.BlockSpec((n, tr, cols), lambda i: (0, i, 0))],
        out_specs=pl.BlockSpec((tr, cols), lambda i: (i, 0)),
        compiler_params=_cparams(("parallel",)),
    )(parts)


def sum_pair(a, b, name):
    rows, cols = a.shape
    tr = _row_tile(rows, 16, 1024)

    def body(a_ref, b_ref, o_ref):
        o_ref[...] = (a_ref[...].astype(F32) + b_ref[...].astype(F32)).astype(o_ref.dtype)

    blk = pl.BlockSpec((tr, cols), lambda i: (i, 0))
    return pl.pallas_call(
        body, name=name, grid=(rows // tr,),
        out_shape=jax.ShapeDtypeStruct((rows, cols), a.dtype),
        in_specs=[blk, blk], out_specs=blk,
        compiler_params=_cparams(("parallel",)),
    )(a, b)


def adamw(w, g, m, v, name):
    rows, cols = w.shape
    tr = _row_tile(rows, 8, 256)

    def body(w_ref, g_ref, m_ref, v_ref, d_ref, m2_ref, v2_ref):
        delta, m2, v2 = _adam_math(w_ref[...], g_ref[...], m_ref[...], v_ref[...])
        d_ref[...] = delta
        m2_ref[...] = m2
        v2_ref[...] = v2

    blk = pl.BlockSpec((tr, cols), lambda i: (i, 0))
    return pl.pallas_call(
        body, name=name, grid=(rows // tr,),
        out_shape=[jax.ShapeDtypeStruct((rows, cols), F32)] * 3,
        in_specs=[blk] * 4, out_specs=[blk] * 3,
        compiler_params=_cparams(("parallel",)),
    )(w, g, m, v)


def _pad128(n):
    return -(-n // 128) * 128


def _pack(arrays):
    offs, parts, off = [], [], 0
    for a in arrays:
        flat = a.reshape(1, -1)
        n = flat.shape[1]
        offs.append(off)
        parts.append(jnp.pad(flat, ((0, 0), (0, _pad128(n) - n))))
        off += _pad128(n)
    return jnp.concatenate(parts, axis=1), offs


def _unpack(vec, offs, shapes):
    out = []
    for off, shp in zip(offs, shapes):
        n = int(np.prod(shp))
        out.append(vec[0, off:off + n].reshape(shp))
    return out


IN_ROWS = IN_DIM // N_DEV
DT_ROW0 = 9216
DT_DEV, DT_LO = divmod(DT_ROW0, IN_ROWS)


GATE_SHIFT = D - 32


def _in_row_pieces(tile):
    pieces = []
    if tile == DT_COL_BLOCK:
        for g in range(N_GROUPS):
            o = DT_ROW0 + HEADS_PER_GROUP * g
            pieces.append((N_STATE * g, o // IN_ROWS, o % IN_ROWS, HEADS_PER_GROUP))
        return pieces
    r, end = tile * D, (tile + 1) * D
    while r < end:
        o = r if r < DT_ROW0 else r - GATE_SHIFT
        dev, loc = divmod(o, IN_ROWS)
        n = min(end - r, IN_ROWS - loc)
        pieces.append((r - tile * D, dev, loc, n))
        r += n
    return pieces


def assemble_w_in(g_all):
    ntile = N_PROJ // D

    def body(g_ref, o_ref):
        j = pl.program_id(0)
        for tile in range(ntile):
            @pl.when(j == tile)
            def _(tile=tile):
                if tile == DT_COL_BLOCK:
                    o_ref[...] = jnp.zeros_like(o_ref)
                for dst, dev, loc, n in _in_row_pieces(tile):
                    o_ref[pl.ds(dst, n), :] = g_ref[dev, pl.ds(loc, n), :]

    return pl.pallas_call(
        body, name="assemble_w_in", grid=(ntile,),
        out_shape=jax.ShapeDtypeStruct((N_PROJ, D), g_all.dtype),
        in_specs=[pl.BlockSpec(memory_space=pltpu.VMEM)],
        out_specs=pl.BlockSpec((D, D), lambda j: (j, 0)),
        compiler_params=_cparams(("arbitrary",)),
    )(g_all)


def _grad_in_blocks(g_t, core, slot):
    dt0 = DT_COL_BLOCK * D
    dt = g_t[dt0:dt0 + N_GROUPS * N_STATE].reshape(N_GROUPS, N_STATE, D)[:, :HEADS_PER_GROUP].reshape(32, D)
    with_dt = jnp.concatenate([g_t[DT_DEV * IN_ROWS:DT_ROW0], dt,
                               g_t[DT_ROW0 + 32 + GATE_SHIFT:(DT_DEV + 1) * IN_ROWS + GATE_SHIFT]], axis=0)
    blocks = []
    for q in range(N_CHIP):
        if 2 * q + 1 < DT_DEV:
            blk = lax.dynamic_slice_in_dim(g_t, IN_ROWS * (2 * q + core), IN_ROWS, axis=0)
        else:
            assert 2 * q == DT_DEV
            after = g_t[(DT_DEV + 1) * IN_ROWS + GATE_SHIFT:(DT_DEV + 2) * IN_ROWS + GATE_SHIFT]
            blk = jnp.where(core == 0, with_dt, after)
        blocks.append(jnp.pad(blk, ((0, slot - IN_ROWS), (0, 0))))
    return jnp.stack(blocks)


def kernel(x, c, w_ada, b_ada, w_in, hgrn_lb, hgrn_gnorm, ssm_conv_w, ssm_conv_b, ssm_dt_bias, ssm_a_log, ssm_d, ssm_norm, w_branch_a, w_branch_b, w_o, ln1_g, ln1_b, w_ffn_gate, w_ffn_up, w_ffn_down, ln2_g, ln2_b, loss_target, m_w_ada, m_b_ada, m_w_in, m_hgrn_lb, m_hgrn_gnorm, m_ssm_conv_w, m_ssm_conv_b, m_ssm_dt_bias, m_ssm_a_log, m_ssm_d, m_ssm_norm, m_w_branch_a, m_w_branch_b, m_w_o, m_ln1_g, m_ln1_b, m_w_ffn_gate, m_w_ffn_up, m_w_ffn_down, m_ln2_g, m_ln2_b, v_w_ada, v_b_ada, v_w_in, v_hgrn_lb, v_hgrn_gnorm, v_ssm_conv_w, v_ssm_conv_b, v_ssm_dt_bias, v_ssm_a_log, v_ssm_d, v_ssm_norm, v_w_branch_a, v_w_branch_b, v_w_o, v_ln1_g, v_ln1_b, v_w_ffn_gate, v_w_ffn_up, v_w_ffn_down, v_ln2_g, v_ln2_b):
    me = 4 * lax.axis_index("x") + 2 * lax.axis_index("y") + lax.axis_index("c")
    xt = x[0]
    tgt = loss_target[0]
    t = xt.shape[0]
    ada_cols = w_ada.shape[2]
    conv_cols = ssm_conv_w.shape[2]

    small_in, _ = _pack([c, ssm_conv_w[0]])
    small_all = allgather_vmem(small_in, "allgather_small_inputs")
    c_all = small_all[:, 0, :D]
    conv_w = small_all[:, 0, D:D + CONV_TAPS * conv_cols].reshape(N_DEV, CONV_TAPS, conv_cols)
    conv_w = conv_w.transpose(1, 0, 2).reshape(CONV_TAPS, CONV_DIM)
    mod = ada_modulation(c_all, w_ada[0], b_ada.reshape(N_DEV, 1, ada_cols))
    mod6 = mod.reshape(6, D)

    shards = [w_in[0].T, w_branch_a[0], w_branch_b[0], w_o[0], w_ffn_gate[0].T, w_ffn_up[0].T, w_ffn_down[0]]
    shard_rows = [s.shape[0] for s in shards]
    slot_rows = [-(-r // 32) * 32 for r in shard_rows]
    row_offs = [sum(slot_rows[:i]) for i in range(len(shards))]
    padded = [jnp.pad(s.astype(BF16), ((0, p - r), (0, 0))) for s, r, p in zip(shards, shard_rows, slot_rows)]
    w_in_t = assemble_w_in(allgather_hbm(padded[0], "allgather_w_in"))

    lb = lower_bound_fwd(hgrn_lb)
    u1 = ln_modulate(xt, mod6, 0, 1, "ln_modulate_1")
    proj, g_rest = mm_nt_gather(u1, w_in_t, F32, jnp.concatenate(padded[1:], axis=0), "mm_in_proj")
    g_ba, g_bb, g_o, g_fg, g_fu, g_fd = (g_rest[:, o - slot_rows[0]:o - slot_rows[0] + r]
                                         for o, r in zip(row_offs[1:], shard_rows[1:]))
    w_ba = g_ba.reshape(D, D)
    w_bb = g_bb.reshape(B_INNER, D)
    w_oo = g_o.reshape(D, D)
    w_gu_t = jnp.concatenate([g_fg.reshape(D_FF, D), g_fu.reshape(D_FF, D)], axis=0)
    w_dn = g_fd.reshape(D_FF, D)
    o_a, o_raw, st_a = hgrn_fwd(proj, lb, hgrn_gnorm)
    xc, conv_slope = conv_fwd(proj, conv_w, ssm_conv_b)
    pad3 = ((0, 0), (0, 0), (0, N_STATE - HEADS_PER_GROUP))
    alog4 = jnp.pad(ssm_a_log.reshape(N_GROUPS, 1, HEADS_PER_GROUP), pad3)
    bias4 = jnp.pad(ssm_dt_bias.reshape(N_GROUPS, 1, HEADS_PER_GROUP), pad3)
    dskip4 = jnp.pad(ssm_d.reshape(N_GROUPS, 1, HEADS_PER_GROUP), pad3)
    expand = _head_expand()
    o_b, st_b = ssd_fwd(proj, xc, alog4, bias4, dskip4, ssm_norm, expand)
    ya = mm_nn(o_a, w_ba, BF16, "mm_branch_a")
    yb = mm_nn(o_b, w_bb, BF16, "mm_branch_b")
    merged = merge_gates(ya, yb, proj)
    h1 = mm_nn(merged, w_oo, F32, "mm_out_proj")
    x1 = resid_ln(xt, h1, mod6, 2, ln1_g, ln1_b, "resid_ln_1")
    u2 = ln_modulate(x1, mod6, 3, 4, "ln_modulate_2")
    gu = mm_nt(u2, w_gu_t, BF16, "mm_ffn_in")
    act = swiglu_act(gu)
    h2 = mm_nn(act, w_dn, F32, "mm_ffn_out")

    dh2, dx1_part, acc4 = resid_ln_bwd(x1, h2, mod6, 5, ln2_g, ln2_b, tgt, True, "resid_ln_2_bwd")
    g_dn = mm_tn(act, dh2, "mm_grad_ffn_down")
    dact = mm_nt(dh2, w_dn, BF16, "mm_dact")
    dgu = swiglu_act_bwd(gu, dact)
    g_gu_t = mm_tn(dgu, u2, "mm_grad_ffn_in")
    du2 = mm_nn(dgu, w_gu_t, F32, "mm_du2")
    dx1, acc3 = ln_modulate_bwd(x1, du2, mod6, 4, dx1_part, "ln_modulate_2_bwd")
    dh1, dx_part, acc2 = resid_ln_bwd(xt, h1, mod6, 2, ln1_g, ln1_b, dx1, False, "resid_ln_1_bwd")
    g_o = mm_tn(merged, dh1, "mm_grad_out_proj")
    dmerged = mm_nt(dh1, w_oo, BF16, "mm_dmerged")
    dya, dyb, dproj = merge_gates_bwd(dmerged, ya, yb, proj)
    g_ba_full = mm_tn(o_a, dya, "mm_grad_branch_a")
    g_bb_full = mm_tn(o_b, dyb, "mm_grad_branch_b")
    doa = mm_nt(dya, w_ba, F32, "mm_doa")
    dob = mm_nt(dyb, w_bb, F32, "mm_dob")
    my_core = lax.axis_index("c")

    def by_core(blocks, rows, slots):
        contrib = jnp.concatenate([jnp.pad(b.reshape(N_DEV, -1, D), ((0, 0), (0, p - r), (0, 0)))
                                   for b, r, p in zip(blocks, rows, slots)], axis=1)
        split = contrib.reshape(N_CHIP, 2, contrib.shape[1], D).transpose(1, 0, 2, 3)
        return (lax.dynamic_index_in_dim(split, my_core, 0, keepdims=False),
                lax.dynamic_index_in_dim(split, 1 - my_core, 0, keepdims=False))

    keep_e, give_e = by_core([g_ba_full, g_bb_full, g_o, g_gu_t[:D_FF], g_gu_t[D_FF:], g_dn],
                             shard_rows[1:], slot_rows[1:])
    dproj, dlb, dgn, got_e = hgrn_bwd(proj, lb, hgrn_gnorm, o_raw, doa, st_a, give_e, dproj)
    chip_e = sum_pair(keep_e.reshape(-1, D), got_e.reshape(-1, D), "sum_grads_rest_chip").reshape(keep_e.shape)
    dxs, dbm, dcm, dproj, ddt, dwn, dalog, dbias, ddsk, parts_e = ssd_bwd(proj, xc, alog4, bias4, dskip4, ssm_norm,
                                                                          expand, dob, st_b, chip_e, dproj)
    dxc = jnp.concatenate([dxs, dbm, dcm], axis=1)
    dproj, dcw, dcb = conv_bwd(proj, dxc, conv_slope, conv_w, dproj)
    dproj = dt_fill(ddt, dproj)
    g_in_t = mm_tn(dproj, u1, "mm_grad_in_proj")
    keep_l = _grad_in_blocks(g_in_t, my_core, slot_rows[0])
    give_l = _grad_in_blocks(g_in_t, 1 - my_core, slot_rows[0])
    got_l = exchange_sibling(give_l, "exchange_grad_in_sibling")
    chip_l = sum_pair(keep_l.reshape(-1, D), got_l.reshape(-1, D), "sum_grad_in_chip").reshape(keep_l.shape)
    du1, parts_l = mm_nn_exchange(dproj, w_in_t, F32, chip_l, "mm_du1")
    dx, acc1 = ln_modulate_bwd(xt, du1, mod6, 1, dx_part, "ln_modulate_1_bwd")
    gw_in = sum_parts(parts_l, "sum_grad_in")[:shard_rows[0]].T
    g_rows = sum_parts(parts_e, "sum_grads_rest")
    gw_ba, gw_bb, gw_o, gw_fg, gw_fu, gw_fd = (g_rows[o - slot_rows[0]:o - slot_rows[0] + r]
                                               for o, r in zip(row_offs[1:], shard_rows[1:]))
    gw_fg, gw_fu = gw_fg.T, gw_fu.T

    dmod = jnp.concatenate([acc1[1:2], acc1[0:1], acc2[0:1], acc3[1:2], acc3[0:1], acc4[0:1]], axis=1)
    small_fields = [dmod, acc4[3:4, :128], dlb, dgn, dcw[:CONV_TAPS], dcb, dbias, dalog, ddsk, dwn,
                    acc2[1:2], acc2[2:3], acc4[1:2], acc4[2:3]]
    small_out, offs = _pack(small_fields)
    small_sum_in = allgather_vmem(small_out, "allgather_small_grads")
    gsum, g_lb = reduce_small(small_sum_in, hgrn_lb, offs[2])
    (g_bada, loss_row, _, g_gn, g_cw_full, g_cb, g_bias4, g_alog4, g_dsk4, g_wn, g_l1g, g_l1b, g_l2g, g_l2b) = _unpack(
        gsum, offs, [(1, 6 * D), (1, 128), (1, D), (1, HK), (CONV_TAPS, CONV_DIM), (1, CONV_DIM),
                     (N_GROUPS, N_STATE), (N_GROUPS, N_STATE), (N_GROUPS, N_STATE), (1, B_INNER),
                     (1, D), (1, D), (1, D), (1, D)])
    loss = loss_row[0, 0]
    g_cw = lax.dynamic_slice(g_cw_full, (0, me * conv_cols), (CONV_TAPS, conv_cols))[None]
    g_dtb = g_bias4[:, :HEADS_PER_GROUP].reshape(1, 32)
    g_alog = g_alog4[:, :HEADS_PER_GROUP].reshape(1, 32)
    g_dsk = g_dsk4[:, :HEADS_PER_GROUP].reshape(1, 32)

    dmod_all = small_sum_in[:, 0, offs[0]:offs[0] + 6 * D]
    dmod_cols = lax.dynamic_slice(dmod_all, (0, me * ada_cols), (N_DEV, ada_cols))
    gw_ada = ada_weight_grad(c_all, dmod_cols)

    big = [("ada", w_ada[0], gw_ada, m_w_ada[0], v_w_ada[0]), ("in", w_in[0], gw_in, m_w_in[0], v_w_in[0]),
           ("branch_a", w_branch_a[0], gw_ba, m_w_branch_a[0], v_w_branch_a[0]),
           ("branch_b", w_branch_b[0], gw_bb, m_w_branch_b[0], v_w_branch_b[0]),
           ("o", w_o[0], gw_o, m_w_o[0], v_w_o[0]),
           ("ffn_gate", w_ffn_gate[0], gw_fg, m_w_ffn_gate[0], v_w_ffn_gate[0]),
           ("ffn_up", w_ffn_up[0], gw_fu, m_w_ffn_up[0], v_w_ffn_up[0]),
           ("ffn_down", w_ffn_down[0], gw_fd, m_w_ffn_down[0], v_w_ffn_down[0])]
    big_out = {}
    for nm, w_, g_, m_, v_ in big:
        d_, m2_, v2_ = adamw(w_, g_, m_, v_, "adamw_" + nm)
        big_out[nm] = (g_[None], d_[None], m2_[None], v2_[None])

    small_w = [b_ada, hgrn_lb, hgrn_gnorm, ssm_conv_w, ssm_conv_b, ssm_dt_bias, ssm_a_log, ssm_d, ssm_norm,
               ln1_g, ln1_b, ln2_g, ln2_b]
    small_g = [g_bada, g_lb, g_gn, g_cw, g_cb, g_dtb, g_alog, g_dsk, g_wn, g_l1g, g_l1b, g_l2g, g_l2b]
    small_m = [m_b_ada, m_hgrn_lb, m_hgrn_gnorm, m_ssm_conv_w, m_ssm_conv_b, m_ssm_dt_bias, m_ssm_a_log, m_ssm_d,
               m_ssm_norm, m_ln1_g, m_ln1_b, m_ln2_g, m_ln2_b]
    small_v = [v_b_ada, v_hgrn_lb, v_hgrn_gnorm, v_ssm_conv_w, v_ssm_conv_b, v_ssm_dt_bias, v_ssm_a_log, v_ssm_d,
               v_ssm_norm, v_ln1_g, v_ln1_b, v_ln2_g, v_ln2_b]
    shapes = [a.shape for a in small_w]
    small_g = [g_.reshape(s) for g_, s in zip(small_g, shapes)]
    pw, poffs = _pack(small_w)
    pg, _ = _pack(small_g)
    pm, _ = _pack(small_m)
    pv, _ = _pack(small_v)
    pd, pm2, pv2 = adamw(pw, pg, pm, pv, "adamw_small")
    s_d, s_m, s_v = (_unpack(p, poffs, shapes) for p in (pd, pm2, pv2))
    (sn_bada, sn_lb, sn_gn, sn_cw, sn_cb, sn_dtb, sn_alog, sn_dsk, sn_wn, sn_l1g, sn_l1b, sn_l2g, sn_l2b) = range(13)

    def order(kind):
        sm = [small_g, s_d, s_m, s_v][kind]
        bg = lambda nm: big_out[nm][kind]
        return [bg("ada"), sm[sn_bada], bg("in"), sm[sn_lb], sm[sn_gn], sm[sn_cw], sm[sn_cb], sm[sn_dtb], sm[sn_alog],
                sm[sn_dsk], sm[sn_wn], bg("branch_a"), bg("branch_b"), bg("o"), sm[sn_l1g], sm[sn_l1b],
                bg("ffn_gate"), bg("ffn_up"), bg("ffn_down"), sm[sn_l2g], sm[sn_l2b]]

    return (loss, dx[None], *order(0), *order(1), *order(2), *order(3))
```

```python
import numpy as np
import jax
import jax.numpy as jnp
from jax import lax
from jax.experimental import pallas as pl
from jax.experimental.pallas import tpu as pltpu

F32 = jnp.float32
BF16 = jnp.bfloat16
HI = lax.Precision.HIGHEST

N_DEV = 8
D = 1024
N_HEADS_A = 8
HK = 128
CHUNK = 64
SSD_CHUNK = 128
SSD_CHUNK_BWD = 256
N_GROUPS = 4
HEADS_PER_GROUP = 8
HEAD_P = 64
N_STATE = 128
GROUP_W = HEADS_PER_GROUP * HEAD_P
B_INNER = 2048
CONV_DIM = 3072
D_FF = 2816
IN_DIM = 11296
N_PROJ = 12288
ALPHA = 2.0 ** 0.25
LN_EPS = 1e-5
RMS_EPS = 1e-6
Q_SCALE = 128 ** -0.5
EXP_CLIP = 80.0
ADAM_LR, ADAM_B1, ADAM_B2, ADAM_EPS, ADAM_WD, ADAM_STEP = 0.001, 0.9, 0.999, 1e-8, 0.01, 10
VMEM_LIMIT = 48 * 1024 * 1024
TOKEN_BLOCK = 1024
ROW_TILE = 512
WIDE_ROW_TILE = 1024
FFN_ROW_TILE = 512
MM_ROW_TILE = 1024
MM_TOKEN_TILE = 4096
MM_K_TILE = 3072
MM_COL_TILE = 1408
HGRN_HEADS_PER_STEP = 4
CHUNK_UNROLL = 8
MESH_ID = pl.DeviceIdType.MESH

NT_DIMS = (((1,), (1,)), ((), ()))
TN_DIMS = (((0,), (0,)), ((), ()))


def _cparams(sem=None):
    return pltpu.CompilerParams(dimension_semantics=sem, vmem_limit_bytes=VMEM_LIMIT)


def _sigmoid(x):
    return 1.0 / (1.0 + jnp.exp(-x))


def _dsilu(x, s):
    return s * (1.0 + x * (1.0 - s))


def _nt(a, b, precision=None):
    return lax.dot_general(a, b, NT_DIMS, precision=precision, preferred_element_type=F32)


def _tn(a, b, precision=None):
    return lax.dot_general(a, b, TN_DIMS, precision=precision, preferred_element_type=F32)


def _nn(a, b, precision=None):
    return jnp.dot(a, b, precision=precision, preferred_element_type=F32)


def _split(x, pieces):
    out = []
    for i in range(pieces):
        p = x.astype(BF16)
        out.append(p)
        if i + 1 < pieces:
            x = x - p.astype(F32)
    return out


def _sel(dot, x, sel01, pieces, x_first=True):
    acc = None
    for p in _split(x, pieces):
        term = dot(p, sel01) if x_first else dot(sel01, p)
        acc = term if acc is None else acc + term
    return acc


def _ln(x):
    mu = jnp.mean(x, axis=-1, keepdims=True)
    xc = x - mu
    rstd = lax.rsqrt(jnp.mean(xc * xc, axis=-1, keepdims=True) + LN_EPS)
    return xc * rstd, rstd


def _ln_bwd(dxh, xh, rstd):
    return rstd * (dxh - jnp.mean(dxh, axis=-1, keepdims=True) - xh * jnp.mean(dxh * xh, axis=-1, keepdims=True))


def _colsum(x):
    return jnp.sum(x, axis=0, keepdims=True)


def _tri(n, upper=False):
    r = lax.broadcasted_iota(jnp.int32, (n, n), 0)
    c = lax.broadcasted_iota(jnp.int32, (n, n), 1)
    return (c >= r) if upper else (r >= c)


def _my_pos():
    return lax.axis_index("x"), lax.axis_index("y"), lax.axis_index("c")


def _peer(pos, k):
    x, y, c = pos
    return (x ^ ((k >> 2) & 1), y ^ ((k >> 1) & 1), c ^ (k & 1))


def _flat(pos):
    return 4 * pos[0] + 2 * pos[1] + pos[2]


def allgather_vmem(v, name):
    n = v.shape[1]

    def body(v_ref, o_ref, send_sems, recv_sems, local_sem):
        me = _my_pos()
        mine = pltpu.make_async_copy(v_ref, o_ref.at[_flat(me)], local_sem)
        mine.start()
        sends = []
        for k in range(1, N_DEV):
            peer = _peer(me, k)
            cp = pltpu.make_async_remote_copy(v_ref, o_ref.at[_flat(me)], send_sems.at[k - 1], recv_sems.at[k - 1],
                                              device_id=peer, device_id_type=MESH_ID)
            cp.start()
            sends.append(cp)
        for k in range(1, N_DEV):
            peer = _peer(me, k)
            pltpu.make_async_remote_copy(v_ref, o_ref.at[_flat(peer)], send_sems.at[k - 1], recv_sems.at[k - 1],
                                         device_id=peer, device_id_type=MESH_ID).wait_recv()
        for cp in sends:
            cp.wait_send()
        mine.wait()

    return pl.pallas_call(
        body, name=name,
        out_shape=jax.ShapeDtypeStruct((N_DEV, 1, n), F32),
        in_specs=[pl.BlockSpec(memory_space=pltpu.VMEM)],
        out_specs=pl.BlockSpec(memory_space=pltpu.VMEM),
        scratch_shapes=[pltpu.SemaphoreType.DMA((N_DEV - 1,)), pltpu.SemaphoreType.DMA((N_DEV - 1,)),
                        pltpu.SemaphoreType.DMA],
        compiler_params=_cparams(),
    )(v)


def ada_modulation(c_all, w_ada_s, b_ada_r):
    ncol = w_ada_s.shape[1]

    def body(c_ref, w_ref, b_ref, o_ref, part_ref, send_sems, recv_sems):
        me = _my_pos()
        cval = c_ref[...]
        cond = cval * _sigmoid(cval)
        part = _nn(cond, w_ref[...], HI)
        for r in range(N_DEV):
            part_ref[r] = part[r:r + 1, :]
        sends = []
        for k in range(1, N_DEV):
            peer = _peer(me, k)
            cp = pltpu.make_async_remote_copy(part_ref.at[_flat(peer)], o_ref.at[_flat(me)], send_sems.at[k - 1],
                                              recv_sems.at[k - 1], device_id=peer, device_id_type=MESH_ID)
            cp.start()
            sends.append(cp)
        o_ref[_flat(me)] = part_ref[_flat(me)]
        for k in range(1, N_DEV):
            peer = _peer(me, k)
            pltpu.make_async_remote_copy(part_ref.at[_flat(peer)], o_ref.at[_flat(peer)], send_sems.at[k - 1],
                                         recv_sems.at[k - 1], device_id=peer, device_id_type=MESH_ID).wait_recv()
        for cp in sends:
            cp.wait_send()
        o_ref[...] = o_ref[...] + b_ref[...]

    return pl.pallas_call(
        body, name="ada_modulation",
        out_shape=jax.ShapeDtypeStruct((N_DEV, 1, ncol), F32),
        in_specs=[pl.BlockSpec(memory_space=pltpu.VMEM)] * 3,
        out_specs=pl.BlockSpec(memory_space=pltpu.VMEM),
        scratch_shapes=[pltpu.VMEM((N_DEV, 1, ncol), F32), pltpu.SemaphoreType.DMA((N_DEV - 1,)),
                        pltpu.SemaphoreType.DMA((N_DEV - 1,))],
        compiler_params=_cparams(),
    )(c_all, w_ada_s, b_ada_r)


def allgather_hbm(shard, name):
    def body(x_ref, out_ref, send_sems, recv_sems, local_sem):
        x, y, c = _my_pos()
        me, sibling = (x, y, c), (x, y, 1 - c)
        chips = [(1 - x, y), (x, 1 - y), (1 - x, 1 - y)]

        def slot(pos):
            return out_ref.at[_flat(pos)]

        def copy(k, block, to, src=None):
            return pltpu.make_async_remote_copy(slot(block) if src is None else src, slot(block), send_sems.at[k],
                                                recv_sems.at[k], device_id=to, device_id_type=MESH_ID)

        mine = pltpu.make_async_copy(x_ref, slot(me), local_sem)
        mine.start()
        first = [copy(0, me, sibling, src=x_ref)]
        first += [copy(1 + j, me, (*chip, c), src=x_ref) for j, chip in enumerate(chips)]
        for cp in first:
            cp.start()
        passed = [copy(4 + j, (*chip, c), sibling) for j, chip in enumerate(chips)]
        for j, chip in enumerate(chips):
            copy(1 + j, (*chip, c), me).wait_recv()
            passed[j].start()
        copy(0, sibling, me).wait_recv()
        for j, chip in enumerate(chips):
            copy(4 + j, (*chip, 1 - c), me).wait_recv()
        for cp in first + passed:
            cp.wait_send()
        mine.wait()

    return pl.pallas_call(
        body, name=name,
        out_shape=jax.ShapeDtypeStruct((N_DEV,) + shard.shape, shard.dtype),
        in_specs=[pl.BlockSpec(memory_space=pl.ANY)],
        out_specs=pl.BlockSpec(memory_space=pl.ANY),
        scratch_shapes=[pltpu.SemaphoreType.DMA((N_DEV - 1,)), pltpu.SemaphoreType.DMA((N_DEV - 1,)),
                        pltpu.SemaphoreType.DMA],
        compiler_params=_cparams(),
    )(shard)


N_CHIP = N_DEV // 2
SIBLING_SEMS = [pltpu.SemaphoreType.DMA, pltpu.SemaphoreType.DMA]
CHIP_SEMS = [pltpu.SemaphoreType.DMA((N_CHIP - 1,)), pltpu.SemaphoreType.DMA((N_CHIP - 1,)), pltpu.SemaphoreType.DMA]


def _sibling_exchange(s_ref, o_ref, send_sem, recv_sem):
    x, y, c = _my_pos()
    cp = pltpu.make_async_remote_copy(s_ref, o_ref, send_sem, recv_sem, device_id=(x, y, 1 - c), device_id_type=MESH_ID)
    return cp.start, cp.wait


def _chip_exchange(p_ref, o_ref, send_sems, recv_sems, local_sem):
    x, y, c = _my_pos()
    my_chip = 2 * x + y
    mine = pltpu.make_async_copy(p_ref.at[my_chip], o_ref.at[my_chip], local_sem)
    peers = [(x ^ (k >> 1), y ^ (k & 1)) for k in range(1, N_CHIP)]
    sends = [pltpu.make_async_remote_copy(p_ref.at[2 * px + py], o_ref.at[my_chip], send_sems.at[k], recv_sems.at[k],
                                          device_id=(px, py, c), device_id_type=MESH_ID)
             for k, (px, py) in enumerate(peers)]
    recvs = [pltpu.make_async_remote_copy(p_ref.at[2 * px + py], o_ref.at[2 * px + py], send_sems.at[k], recv_sems.at[k],
                                          device_id=(px, py, c), device_id_type=MESH_ID)
             for k, (px, py) in enumerate(peers)]

    def start():
        mine.start()
        for cp in sends:
            cp.start()

    def wait():
        for cp in recvs:
            cp.wait_recv()
        for cp in sends:
            cp.wait_send()
        mine.wait()

    return start, wait


def exchange_sibling(send, name):
    def body(s_ref, o_ref, send_sem, recv_sem):
        start, wait = _sibling_exchange(s_ref, o_ref, send_sem, recv_sem)
        start()
        wait()

    return pl.pallas_call(
        body, name=name,
        out_shape=jax.ShapeDtypeStruct(send.shape, send.dtype),
        in_specs=[pl.BlockSpec(memory_space=pl.ANY)],
        out_specs=pl.BlockSpec(memory_space=pl.ANY),
        scratch_shapes=SIBLING_SEMS,
        compiler_params=_cparams(),
    )(send)


LANES = 128


def _k_tile(kdim, unit=LANES):
    for cand in range(MM_K_TILE - MM_K_TILE % unit, 0, -unit):
        if kdim % cand == 0:
            return cand
    return kdim


def _lane_tile(n, cap):
    for cand in range(cap - cap % LANES, 0, -LANES):
        if n % cand == 0:
            return cand
    return n


def _m_tile(m, kdim):
    return min(MM_ROW_TILE if kdim > D else 2 * MM_ROW_TILE, m)


def _mm(a, b, out_dtype, name, b_is_nk):
    m, kdim = a.shape
    n = b.shape[0] if b_is_nk else b.shape[1]
    tm, tn, tk = _m_tile(m, kdim), _lane_tile(n, MM_COL_TILE), _k_tile(kdim)
    nk = kdim // tk
    dot = _nt if b_is_nk else _nn

    def body(a_ref, b_ref, o_ref, *acc):
        p = dot(a_ref[...], b_ref[...])
        if nk == 1:
            o_ref[...] = p.astype(o_ref.dtype)
        else:
            acc_ref, k = acc[0], pl.program_id(2)

            @pl.when(k == 0)
            def _():
                acc_ref[...] = p

            @pl.when(k > 0)
            def _():
                acc_ref[...] += p

            @pl.when(k == nk - 1)
            def _():
                o_ref[...] = acc_ref[...].astype(o_ref.dtype)

    b_spec = (pl.BlockSpec((tn, tk), lambda j, i, k: (j, k)) if b_is_nk else
              pl.BlockSpec((tk, tn), lambda j, i, k: (k, j)))
    return pl.pallas_call(
        body, name=name, grid=(n // tn, m // tm, nk),
        out_shape=jax.ShapeDtypeStruct((m, n), out_dtype),
        in_specs=[pl.BlockSpec((tm, tk), lambda j, i, k: (i, k)), b_spec],
        out_specs=pl.BlockSpec((tm, tn), lambda j, i, k: (i, j)),
        scratch_shapes=[] if nk == 1 else [pltpu.VMEM((tm, tn), F32)],
        compiler_params=_cparams(("parallel", "parallel", "arbitrary")),
    )(a, b)


def mm_nn(a, b, out_dtype, name):
    return _mm(a, b, out_dtype, name, False)


def mm_nt(a, b, out_dtype, name):
    return _mm(a, b, out_dtype, name, True)


def mm_nn_exchange(a, b, out_dtype, part, name):
    kblocks, m, kb = a.shape
    kdim = kblocks * kb
    n = b.shape[1]
    tm, tn, tk = min(MM_ROW_TILE, m), _lane_tile(n, MM_COL_TILE), _k_tile(kdim)
    gn, gm, nk = n // tn, m // tm, kdim // tk
    per_step = tk // kb

    def body(a_ref, b_ref, part_ref, o_ref, parts_ref, acc_ref, send_sems, recv_sems, local_sem):
        j, i, k = pl.program_id(0), pl.program_id(1), pl.program_id(2)
        xchg_start, xchg_wait = _chip_exchange(part_ref, parts_ref, send_sems, recv_sems, local_sem)

        @pl.when((j == 0) & (i == 0) & (k == 0))
        def _():
            xchg_start()

        p = _nn(a_ref[0], b_ref[0:kb, :])
        for c in range(1, per_step):
            p = p + _nn(a_ref[c], b_ref[c * kb:(c + 1) * kb, :])

        @pl.when(k == 0)
        def _():
            acc_ref[...] = p

        @pl.when(k > 0)
        def _():
            acc_ref[...] += p

        @pl.when(k == nk - 1)
        def _():
            o_ref[...] = acc_ref[...].astype(o_ref.dtype)

        @pl.when((j == gn - 1) & (i == gm - 1) & (k == nk - 1))
        def _():
            xchg_wait()

    hbm = pl.BlockSpec(memory_space=pl.ANY)
    return pl.pallas_call(
        body, name=name, grid=(gn, gm, nk),
        out_shape=[jax.ShapeDtypeStruct((m, n), out_dtype), jax.ShapeDtypeStruct(part.shape, part.dtype)],
        in_specs=[pl.BlockSpec((per_step, tm, kb), lambda j, i, k: (k, i, 0)),
                  pl.BlockSpec((tk, tn), lambda j, i, k: (k, j)), hbm],
        out_specs=[pl.BlockSpec((tm, tn), lambda j, i, k: (i, j)), hbm],
        scratch_shapes=[pltpu.VMEM((tm, tn), F32)] + CHIP_SEMS,
        compiler_params=_cparams(("arbitrary", "arbitrary", "arbitrary")),
    )(a, b, part)


def mm_nt_gather(a, b, out_dtype, shard, name):
    m, kdim = a.shape
    n = b.shape[0]
    tm, tn = _m_tile(m, kdim), 1024
    assert kdim == 1024
    gj = m // tm
    nsteps = (n // tn) * gj
    forward_step = max(nsteps - 2, 0)

    def body(a_ref, b_ref, x_ref, o_ref, g_ref, send_sems, recv_sems, local_sem):
        step = pl.program_id(0) * gj + pl.program_id(1)
        x, y, c = _my_pos()
        me, sibling = (x, y, c), (x, y, 1 - c)
        chips = [(1 - x, y), (x, 1 - y), (1 - x, 1 - y)]

        def slot(pos):
            return g_ref.at[_flat(pos)]

        def copy(k, block, to, src=None):
            return pltpu.make_async_remote_copy(slot(block) if src is None else src, slot(block), send_sems.at[k],
                                                recv_sems.at[k], device_id=to, device_id_type=MESH_ID)

        mine = pltpu.make_async_copy(x_ref, slot(me), local_sem)
        first = [copy(0, me, sibling, src=x_ref)]
        first += [copy(1 + j, me, (*chip, c), src=x_ref) for j, chip in enumerate(chips)]
        passed = [copy(4 + j, (*chip, c), sibling) for j, chip in enumerate(chips)]

        @pl.when(step == 0)
        def _():
            mine.start()
            for cp in first:
                cp.start()

        rows = pl.ds(pl.multiple_of(pl.program_id(1) * tm, tm), tm)
        o_ref[...] = _nt(a_ref[rows, :], b_ref[...]).astype(o_ref.dtype)

        @pl.when(step == forward_step)
        def _():
            for j, chip in enumerate(chips):
                copy(1 + j, (*chip, c), me).wait_recv()
                passed[j].start()

        @pl.when(step == nsteps - 1)
        def _():
            copy(0, sibling, me).wait_recv()
            for j, chip in enumerate(chips):
                copy(4 + j, (*chip, 1 - c), me).wait_recv()
            for cp in first + passed:
                cp.wait_send()
            mine.wait()

    return pl.pallas_call(
        body, name=name, grid=(n // tn, gj),
        out_shape=[jax.ShapeDtypeStruct((m, n), out_dtype), jax.ShapeDtypeStruct((N_DEV,) + shard.shape, shard.dtype)],
        in_specs=[pl.BlockSpec(memory_space=pltpu.VMEM), pl.BlockSpec((tn, kdim), lambda j, i: (j, 0)),
                  pl.BlockSpec(memory_space=pl.ANY)],
        out_specs=[pl.BlockSpec((tm, tn), lambda j, i: (i, j)), pl.BlockSpec(memory_space=pl.ANY)],
        scratch_shapes=[pltpu.SemaphoreType.DMA((N_DEV - 1,)), pltpu.SemaphoreType.DMA((N_DEV - 1,)),
                        pltpu.SemaphoreType.DMA],
        compiler_params=_cparams(("arbitrary", "arbitrary")),
    )(a, b, shard)


def mm_tn(a, b, name):
    tt, tn = min(MM_TOKEN_TILE, b.shape[0]), _lane_tile(b.shape[1], MM_COL_TILE)
    tka = _lane_tile(a.shape[0] * a.shape[2] if a.ndim == 3 else a.shape[1], 1024)
    if a.ndim == 3:
        t, ka = a.shape[1], a.shape[0] * a.shape[2]
        a_spec = pl.BlockSpec((None, tt, tka), lambda i, j, s: (i, s, 0))
    else:
        t, ka = a.shape
        a_spec = pl.BlockSpec((tt, tka), lambda i, j, s: (s, i))
    n = b.shape[1]
    nt = t // tt

    def body(a_ref, b_ref, o_ref, *acc):
        p = _tn(a_ref[...], b_ref[...])
        if nt == 1:
            o_ref[...] = p.astype(o_ref.dtype)
        else:
            acc_ref, s = acc[0], pl.program_id(2)

            @pl.when(s == 0)
            def _():
                acc_ref[...] = p

            @pl.when(s > 0)
            def _():
                acc_ref[...] += p

            @pl.when(s == nt - 1)
            def _():
                o_ref[...] = acc_ref[...].astype(o_ref.dtype)

    return pl.pallas_call(
        body, name=name, grid=(ka // tka, n // tn, nt),
        out_shape=jax.ShapeDtypeStruct((ka, n), BF16),
        in_specs=[a_spec, pl.BlockSpec((tt, tn), lambda i, j, s: (s, j))],
        out_specs=pl.BlockSpec((tka, tn), lambda i, j, s: (i, j)),
        scratch_shapes=[] if nt == 1 else [pltpu.VMEM((tka, tn), F32)],
        compiler_params=_cparams(("parallel", "parallel", "arbitrary")),
    )(a, b)


def _tile(t, cap):
    return min(cap, t)


def ln_modulate(x, mod6, shift_row, scale_row, name):
    t = x.shape[0]
    tm = _tile(t, WIDE_ROW_TILE)

    def body(x_ref, mod_ref, o_ref):
        xh, _ = _ln(x_ref[...])
        sc = mod_ref[scale_row:scale_row + 1, :]
        sh = mod_ref[shift_row:shift_row + 1, :]
        o_ref[...] = (xh * (1.0 + sc) + sh).astype(BF16)

    return pl.pallas_call(
        body, name=name, grid=(t // tm,),
        out_shape=jax.ShapeDtypeStruct((t, D), BF16),
        in_specs=[pl.BlockSpec((tm, D), lambda i: (i, 0)), pl.BlockSpec((6, D), lambda i: (0, 0))],
        out_specs=pl.BlockSpec((tm, D), lambda i: (i, 0)),
        compiler_params=_cparams(("parallel",)),
    )(x, mod6)


def resid_ln(x, h, mod6, gate_row, ln_g, ln_b, name):
    t = x.shape[0]
    tm = _tile(t, WIDE_ROW_TILE)

    def body(x_ref, h_ref, mod_ref, g_ref, b_ref, o_ref):
        r = ALPHA * x_ref[...] + mod_ref[gate_row:gate_row + 1, :] * h_ref[...]
        rh, _ = _ln(r)
        o_ref[...] = rh * g_ref[...] + b_ref[...]

    row = pl.BlockSpec((tm, D), lambda i: (i, 0))
    vec = pl.BlockSpec((1, D), lambda i: (0, 0))
    return pl.pallas_call(
        body, name=name, grid=(t // tm,),
        out_shape=jax.ShapeDtypeStruct((t, D), F32),
        in_specs=[row, row, pl.BlockSpec((6, D), lambda i: (0, 0)), vec, vec],
        out_specs=row,
        compiler_params=_cparams(("parallel",)),
    )(x, h, mod6, ln_g, ln_b)


def resid_ln_bwd(x, h, mod6, gate_row, ln_g, ln_b, cot, with_loss, name):
    t = x.shape[0]
    tm = _tile(t, ROW_TILE)

    def body(x_ref, h_ref, mod_ref, g_ref, b_ref, c_ref, dh_ref, dx_ref, acc_ref):
        @pl.when(pl.program_id(0) == 0)
        def _():
            acc_ref[...] = jnp.zeros_like(acc_ref)

        gate = mod_ref[gate_row:gate_row + 1, :]
        hv = h_ref[...]
        r = ALPHA * x_ref[...] + gate * hv
        rh, rstd = _ln(r)
        lng = g_ref[...]
        if with_loss:
            diff = rh * lng + b_ref[...] - c_ref[...]
            dxo = diff * (1.0 / D)
            lsum = jnp.sum(_colsum(diff * diff), axis=-1, keepdims=True) * (0.5 / D)
            acc_ref[3:4, :] += jnp.broadcast_to(lsum, (1, D))
        else:
            dxo = c_ref[...]
        acc_ref[1:2, :] += _colsum(dxo * rh)
        acc_ref[2:3, :] += _colsum(dxo)
        dr = _ln_bwd(dxo * lng, rh, rstd)
        acc_ref[0:1, :] += _colsum(dr * hv)
        dh_ref[...] = (gate * dr).astype(BF16)
        dx_ref[...] = ALPHA * dr

    row = pl.BlockSpec((tm, D), lambda i: (i, 0))
    vec = pl.BlockSpec((1, D), lambda i: (0, 0))
    return pl.pallas_call(
        body, name=name, grid=(t // tm,),
        out_shape=[jax.ShapeDtypeStruct((t, D), BF16), jax.ShapeDtypeStruct((t, D), F32),
                   jax.ShapeDtypeStruct((8, D), F32)],
        in_specs=[row, row, pl.BlockSpec((6, D), lambda i: (0, 0)), vec, vec, row],
        out_specs=[row, row, pl.BlockSpec((8, D), lambda i: (0, 0))],
        compiler_params=_cparams(("arbitrary",)),
    )(x, h, mod6, ln_g, ln_b, cot)


def ln_modulate_bwd(x, du, mod6, scale_row, dx_part, name):
    t = x.shape[0]
    tm = _tile(t, ROW_TILE)

    def body(x_ref, du_ref, mod_ref, dp_ref, dx_ref, acc_ref):
        @pl.when(pl.program_id(0) == 0)
        def _():
            acc_ref[...] = jnp.zeros_like(acc_ref)

        xh, rstd = _ln(x_ref[...])
        du_v = du_ref[...]
        sc = mod_ref[scale_row:scale_row + 1, :]
        acc_ref[0:1, :] += _colsum(du_v * xh)
        acc_ref[1:2, :] += _colsum(du_v)
        dx_ref[...] = dp_ref[...] + _ln_bwd(du_v * (1.0 + sc), xh, rstd)

    row = pl.BlockSpec((tm, D), lambda i: (i, 0))
    return pl.pallas_call(
        body, name=name, grid=(t // tm,),
        out_shape=[jax.ShapeDtypeStruct((t, D), F32), jax.ShapeDtypeStruct((8, D), F32)],
        in_specs=[row, row, pl.BlockSpec((6, D), lambda i: (0, 0)), row],
        out_specs=[row, pl.BlockSpec((8, D), lambda i: (0, 0))],
        compiler_params=_cparams(("arbitrary",)),
    )(x, du, mod6, dx_part)


def merge_gates(ya, yb, proj):
    t = ya.shape[0]
    tm = _tile(t, WIDE_ROW_TILE)

    def body(ya_ref, yb_ref, ga_ref, gb_ref, o_ref):
        o_ref[...] = (_sigmoid(ga_ref[...]) * ya_ref[...].astype(F32) +
                      _sigmoid(gb_ref[...]) * yb_ref[...].astype(F32)).astype(BF16)

    row = pl.BlockSpec((tm, D), lambda i: (i, 0))
    return pl.pallas_call(
        body, name="merge_gates", grid=(t // tm,),
        out_shape=jax.ShapeDtypeStruct((t, D), BF16),
        in_specs=[row, row, pl.BlockSpec((tm, D), lambda i: (i, GATE_BLOCK0)),
                  pl.BlockSpec((tm, D), lambda i: (i, GATE_BLOCK0 + 1))],
        out_specs=row,
        compiler_params=_cparams(("parallel",)),
    )(ya, yb, proj, proj)


def merge_gates_bwd(dh, w_o, ya, yb, proj):
    t = ya.shape[0]
    tm = _tile(t, ROW_TILE)

    def body(dh_ref, w_ref, ya_ref, yb_ref, ga_ref, gb_ref, dya_ref, dyb_ref, dp_ref):
        dmv = _nt(dh_ref[...], w_ref[...])
        sa = _sigmoid(ga_ref[...])
        sb = _sigmoid(gb_ref[...])
        dya_ref[...] = (dmv * sa).astype(BF16)
        dyb_ref[...] = (dmv * sb).astype(BF16)
        dp_ref[0] = (dmv * ya_ref[...].astype(F32) * sa * (1.0 - sa)).astype(BF16)
        dp_ref[1] = (dmv * yb_ref[...].astype(F32) * sb * (1.0 - sb)).astype(BF16)

    row = pl.BlockSpec((tm, D), lambda i: (i, 0))
    return pl.pallas_call(
        body, name="merge_gates_bwd", grid=(t // tm,),
        out_shape=[jax.ShapeDtypeStruct((t, D), BF16)] * 2 + [jax.ShapeDtypeStruct((N_PROJ // D, t, D), BF16)],
        in_specs=[row, pl.BlockSpec((D, D), lambda i: (0, 0)), row, row,
                  pl.BlockSpec((tm, D), lambda i: (i, GATE_BLOCK0)),
                  pl.BlockSpec((tm, D), lambda i: (i, GATE_BLOCK0 + 1))],
        out_specs=[row, row, pl.BlockSpec((2, tm, D), lambda i: (GATE_BLOCK0 // 2, i, 0))],
        compiler_params=_cparams(("parallel",)),
    )(dh, w_o, ya, yb, proj, proj)


FF_CHUNK = 1408


def swiglu_act(gu):
    t = gu.shape[0]
    tm = _tile(t, FFN_ROW_TILE)

    def body(gu_ref, o_ref):
        for j in range(D_FF // FF_CHUNK):
            cs = slice(j * FF_CHUNK, (j + 1) * FF_CHUNK)
            g = gu_ref[:, cs].astype(F32)
            u = gu_ref[:, D_FF + j * FF_CHUNK:D_FF + (j + 1) * FF_CHUNK].astype(F32)
            o_ref[:, cs] = (g * _sigmoid(g) * u).astype(BF16)

    return pl.pallas_call(
        body, name="swiglu_act", grid=(t // tm,),
        out_shape=jax.ShapeDtypeStruct((t, D_FF), BF16),
        in_specs=[pl.BlockSpec((tm, 2 * D_FF), lambda i: (i, 0))],
        out_specs=pl.BlockSpec((tm, D_FF), lambda i: (i, 0)),
        compiler_params=_cparams(("parallel",)),
    )(gu)


def swiglu_act_bwd(gu, dact):
    t = gu.shape[0]
    tm = _tile(t, FFN_ROW_TILE)

    def body(gu_ref, da_ref, o_ref):
        for j in range(D_FF // FF_CHUNK):
            cs = slice(j * FF_CHUNK, (j + 1) * FF_CHUNK)
            us = slice(D_FF + j * FF_CHUNK, D_FF + (j + 1) * FF_CHUNK)
            g = gu_ref[:, cs].astype(F32)
            u = gu_ref[:, us].astype(F32)
            da = da_ref[:, cs].astype(F32)
            s = _sigmoid(g)
            o_ref[:, cs] = (da * u * _dsilu(g, s)).astype(BF16)
            o_ref[:, us] = (da * g * s).astype(BF16)

    return pl.pallas_call(
        body, name="swiglu_act_bwd", grid=(t // tm,),
        out_shape=jax.ShapeDtypeStruct((t, 2 * D_FF), BF16),
        in_specs=[pl.BlockSpec((tm, 2 * D_FF), lambda i: (i, 0)), pl.BlockSpec((tm, D_FF), lambda i: (i, 0))],
        out_specs=pl.BlockSpec((tm, 2 * D_FF), lambda i: (i, 0)),
        compiler_params=_cparams(("parallel",)),
    )(gu, dact)


def _hgrn_chunk_terms(q, fl, lbv, tril_f):
    sig = _sigmoid(fl)
    f = lbv + (1.0 - lbv) * sig
    lam = jnp.log(f)
    k = 1.0 - f
    sq = _sigmoid(q)
    qt = q * sq * Q_SCALE
    bc = _sel(_nn, lam, tril_f, 3, x_first=False)
    bmid = bc[CHUNK // 2 - 1:CHUNK // 2, :]
    bl = bc[CHUNK - 1:CHUNK, :]
    eq = jnp.exp(jnp.minimum(bc - bmid, EXP_CLIP))
    ek = jnp.exp(jnp.minimum(bmid - bc, EXP_CLIP))
    eb = jnp.exp(bc)
    ekl = jnp.exp(bl - bc)
    ebl = jnp.exp(bl)
    return sig, f, k, sq, qt, eq, ek, eb, ekl, ebl


def hgrn_fwd(proj, lb, gnorm):
    t = proj.shape[0]
    tb = _tile(t, TOKEN_BLOCK)
    ncb = tb // CHUNK

    hps = HGRN_HEADS_PER_STEP
    wide = hps * HK

    def body(q_ref, f_ref, i_ref, g_ref, lb_ref, gn_ref, oa_ref, oraw_ref, st_ref, state):
        @pl.when(pl.program_id(1) == 0)
        def _():
            state[...] = jnp.zeros_like(state)

        gn = gn_ref[...]
        mask = _tri(CHUNK)
        tril_f = mask.astype(BF16)

        def chunk(c, carry):
            sl = pl.ds(pl.multiple_of(c * CHUNK, CHUNK), CHUNK)
            for hh in range(hps):
                ln = slice(hh * HK, (hh + 1) * HK)
                q, fl, v, g = q_ref[sl, ln], f_ref[sl, ln], i_ref[sl, ln], g_ref[sl, ln]
                sig, f, k, sq, qt, eq, ek, eb, ekl, ebl = _hgrn_chunk_terms(q, fl, lb_ref[:, ln], tril_f)
                a = jnp.where(mask, _nt((qt * eq).astype(BF16), (k * ek).astype(BF16)), 0.0)
                st = state[hh]
                st_ref[hh, c] = st
                vb = v.astype(BF16)
                o = _nn(a.astype(BF16), vb) + _nt((qt * eb).astype(BF16), st.astype(BF16))
                state[hh] = st * ebl + _tn(vb, (k * ekl).astype(BF16))
                oraw_ref[sl, ln] = o
                rn = o * lax.rsqrt(jnp.mean(o * o, axis=-1, keepdims=True) + RMS_EPS)
                oa_ref[sl, ln] = (rn * gn * g * _sigmoid(g)).astype(BF16)
            return carry

        lax.fori_loop(0, ncb, chunk, 0, unroll=min(CHUNK_UNROLL, ncb))

    def col(block):
        return pl.BlockSpec((tb, wide), lambda h, j: (j, block * (N_HEADS_A // hps) + h))

    return pl.pallas_call(
        body, name="hgrn_fwd", grid=(N_HEADS_A // hps, t // tb),
        out_shape=[jax.ShapeDtypeStruct((t, D), BF16), jax.ShapeDtypeStruct((t, D), F32),
                   jax.ShapeDtypeStruct((N_HEADS_A, t // CHUNK, HK, HK), F32)],
        in_specs=[col(0), col(1), col(2), col(3), pl.BlockSpec((1, wide), lambda h, j: (0, h)),
                  pl.BlockSpec((1, HK), lambda h, j: (0, 0))],
        out_specs=[pl.BlockSpec((tb, wide), lambda h, j: (j, h)), pl.BlockSpec((tb, wide), lambda h, j: (j, h)),
                   pl.BlockSpec((hps, ncb, HK, HK), lambda h, j: (h, j, 0, 0))],
        scratch_shapes=[pltpu.VMEM((hps, HK, HK), F32)],
        compiler_params=_cparams(("parallel", "arbitrary")),
    )(proj, proj, proj, proj, lb, gnorm)


def hgrn_bwd(proj, lb, gnorm, o_raw, dya, w_ba, states, give, dproj):
    t = proj.shape[0]
    tb = _tile(t, TOKEN_BLOCK)
    ncb = tb // CHUNK
    nb = t // tb
    hps = HGRN_HEADS_PER_STEP
    wide = hps * HK

    def body(q_ref, f_ref, i_ref, g_ref, lb_ref, gn_ref, oraw_ref, dya_ref, wba_ref, st_ref, give_ref, dp_in_ref,
             dp_ref, dlb_ref, dgn_ref, got_ref, dstate, doa_ref, send_sem, recv_sem):
        h, j = pl.program_id(0), pl.program_id(1)
        swap_start, swap_wait = _sibling_exchange(give_ref, got_ref, send_sem, recv_sem)
        doa_ref[...] = _nt(dya_ref[...], wba_ref[...])

        @pl.when((h == 0) & (j == 0))
        def _():
            swap_start()

        @pl.when(j == 0)
        def _():
            dstate[...] = jnp.zeros_like(dstate)
            dlb_ref[...] = jnp.zeros_like(dlb_ref)

        @pl.when((j == 0) & (h == 0))
        def _():
            dgn_ref[...] = jnp.zeros_like(dgn_ref)

        gn = gn_ref[...]
        mask = _tri(CHUNK)
        mask_t = _tri(CHUNK, upper=True)
        tril_f = mask.astype(BF16)
        triu_f = mask_t.astype(BF16)

        def chunk(i, c0):
            c = ncb - 1 - i
            sl = pl.ds(pl.multiple_of(c * CHUNK, CHUNK), CHUNK)
            for hh in range(hps):
                ln = slice(hh * HK, (hh + 1) * HK)
                q, fl, v, g = q_ref[sl, ln], f_ref[sl, ln], i_ref[sl, ln], g_ref[sl, ln]
                lbv = lb_ref[:, ln]
                sig, f, k, sq, qt, eq, ek, eb, ekl, ebl = _hgrn_chunk_terms(q, fl, lbv, tril_f)
                qe = (qt * eq).astype(BF16)
                ke = (k * ek).astype(BF16)
                st32 = st_ref[hh, c]
                st = st32.astype(BF16)
                dst = dstate[hh]
                dstb = dst.astype(BF16)
                o = oraw_ref[sl, ln]
                rstd = lax.rsqrt(jnp.mean(o * o, axis=-1, keepdims=True) + RMS_EPS)
                rn = o * rstd
                sgm = _sigmoid(g)
                sg = g * sgm
                doa_v = doa_ref[sl, ln]
                drn = doa_v * gn * sg
                dgn_ref[...] += _colsum(doa_v * rn * sg)
                dp_ref[3, sl, ln] = (doa_v * rn * gn * _dsilu(g, sgm)).astype(BF16)
                do = rstd * (drn - rn * jnp.mean(drn * rn, axis=-1, keepdims=True))
                dob = do.astype(BF16)
                vb = v.astype(BF16)
                da = jnp.where(mask, _nt(dob, vb), 0.0).astype(BF16)
                da_t = jnp.where(mask_t, _nt(vb, dob), 0.0).astype(BF16)
                a_t = jnp.where(mask_t, _nt(ke, qe), 0.0).astype(BF16)
                kl = (k * ekl).astype(BF16)
                qb = (qt * eb).astype(BF16)
                dq_in = _nn(da, ke)
                dk_in = _nn(da_t, qe)
                dq_out = eb * _nn(dob, st)
                dk_out = ekl * _nn(vb, dstb)
                dqt = eq * dq_in + dq_out
                dk = ek * dk_in + dk_out
                dv = _nn(a_t, dob) + _nt(kl, dstb)
                dstate[hh] = dst * ebl + _tn(dob, qb)
                dbig = qe.astype(F32) * dq_in - ke.astype(F32) * dk_in + qt * dq_out - k * dk_out
                beyond = _colsum(k * dk_out) + ebl * _colsum(dst * st32)
                dlam = _sel(_nn, dbig, triu_f, 3, x_first=False) + beyond
                df = dlam / f - dk
                dp_ref[1, sl, ln] = (df * (1.0 - lbv) * sig * (1.0 - sig)).astype(BF16)
                dlb_ref[:, ln] += _colsum(df * (1.0 - sig))
                dp_ref[0, sl, ln] = (dqt * Q_SCALE * _dsilu(q, sq)).astype(BF16)
                dp_ref[2, sl, ln] = dv.astype(BF16)
            return c0

        lax.fori_loop(0, ncb, chunk, 0, unroll=min(CHUNK_UNROLL, ncb))

        @pl.when((h == N_HEADS_A // hps - 1) & (j == nb - 1))
        def _():
            swap_wait()

    def col(block):
        return pl.BlockSpec((tb, wide), lambda h, j: (nb - 1 - j, block * (N_HEADS_A // hps) + h))

    hcol = pl.BlockSpec((tb, wide), lambda h, j: (nb - 1 - j, h))
    hbm = pl.BlockSpec(memory_space=pl.ANY)
    return pl.pallas_call(
        body, name="hgrn_bwd", grid=(N_HEADS_A // hps, nb),
        out_shape=[jax.ShapeDtypeStruct(dproj.shape, dproj.dtype), jax.ShapeDtypeStruct((1, D), F32),
                   jax.ShapeDtypeStruct((1, HK), F32), jax.ShapeDtypeStruct(give.shape, give.dtype)],
        in_specs=[col(0), col(1), col(2), col(3), pl.BlockSpec((1, wide), lambda h, j: (0, h)),
                  pl.BlockSpec((1, HK), lambda h, j: (0, 0)), hcol,
                  pl.BlockSpec((tb, D), lambda h, j: (nb - 1 - j, 0)), pl.BlockSpec((wide, D), lambda h, j: (h, 0)),
                  pl.BlockSpec((hps, ncb, HK, HK), lambda h, j: (h, nb - 1 - j, 0, 0)), hbm, hbm],
        out_specs=[pl.BlockSpec((4, tb, wide), lambda h, j: (0, nb - 1 - j, h)),
                   pl.BlockSpec((1, wide), lambda h, j: (0, h)), pl.BlockSpec((1, HK), lambda h, j: (0, 0)), hbm],
        input_output_aliases={11: 0},
        scratch_shapes=[pltpu.VMEM((hps, HK, HK), F32), pltpu.VMEM((tb, wide), F32)] + SIBLING_SEMS,
        compiler_params=_cparams(("arbitrary", "arbitrary")),
    )(proj, proj, proj, proj, lb, gnorm, o_raw, dya, w_ba, states, give, dproj)


CONV_BLOCK0 = 6
CONV_TAPS = 4
HALO = 8


def conv_fwd(proj, conv_w, conv_b):
    t = proj.shape[0]
    tm = _tile(t, ROW_TILE)
    r = tm // HALO

    def body(x_ref, halo_ref, w_ref, b_ref, o_ref, ds_ref):
        i = pl.program_id(1)
        halo = jnp.where(i > 0, halo_ref[...], 0.0)
        ext = jnp.concatenate([halo, x_ref[...]], axis=0)
        pre = b_ref[...] + w_ref[CONV_TAPS - 1:CONV_TAPS, :] * ext[HALO:, :]
        for tap in range(CONV_TAPS - 1):
            pre = pre + w_ref[tap:tap + 1, :] * pltpu.roll(ext, CONV_TAPS - 1 - tap, axis=0)[HALO:, :]
        s = _sigmoid(pre)
        o_ref[...] = pre * s
        ds_ref[...] = _dsilu(pre, s).astype(BF16)

    blk = pl.BlockSpec((tm, D), lambda cb, i: (i, cb))
    return pl.pallas_call(
        body, name="conv_fwd", grid=(CONV_DIM // D, t // tm),
        out_shape=[jax.ShapeDtypeStruct((t, CONV_DIM), F32), jax.ShapeDtypeStruct((t, CONV_DIM), BF16)],
        in_specs=[pl.BlockSpec((tm, D), lambda cb, i: (i, CONV_BLOCK0 + cb)),
                  pl.BlockSpec((HALO, D), lambda cb, i: (jnp.maximum(i * r - 1, 0), CONV_BLOCK0 + cb)),
                  pl.BlockSpec((CONV_TAPS, D), lambda cb, i: (0, cb)), pl.BlockSpec((1, D), lambda cb, i: (0, cb))],
        out_specs=[blk, blk],
        compiler_params=_cparams(("parallel", "parallel")),
    )(proj, proj, conv_w, conv_b)


def conv_bwd(proj, dxc, dsilu, conv_w, dproj):
    t = proj.shape[0]
    tm = _tile(t, ROW_TILE)
    r = tm // HALO
    n = t // tm
    last_halo = t // HALO - 1

    def body(x_ref, prev_ref, d_ref, dnext_ref, s_ref, snext_ref, w_ref, dp_in_ref, dx_ref, dw_ref, db_ref):
        i = pl.program_id(1)

        @pl.when(i == 0)
        def _():
            dw_ref[...] = jnp.zeros_like(dw_ref)
            db_ref[...] = jnp.zeros_like(db_ref)

        dpre = jnp.concatenate([d_ref[...].astype(F32) * s_ref[...].astype(F32),
                                jnp.where(i < n - 1, dnext_ref[0:HALO, :].astype(F32) * snext_ref[0:HALO, :].astype(F32),
                                          0.0)], axis=0)
        dx = w_ref[CONV_TAPS - 1:CONV_TAPS, :] * dpre[:tm, :]
        for tap in range(CONV_TAPS - 1):
            back = CONV_TAPS - 1 - tap
            dx = dx + w_ref[tap:tap + 1, :] * pltpu.roll(dpre, tm + HALO - back, axis=0)[:tm, :]
        dx_ref[...] = dx.astype(BF16)
        dp = dpre[:tm, :]
        db_ref[...] += _colsum(dp)
        prev = jnp.where(i > 0, prev_ref[...], 0.0)
        ext = jnp.concatenate([prev, x_ref[...]], axis=0)
        dw_ref[CONV_TAPS - 1:CONV_TAPS, :] += _colsum(dp * ext[HALO:, :])
        for tap in range(CONV_TAPS - 1):
            dw_ref[tap:tap + 1, :] += _colsum(dp * pltpu.roll(ext, CONV_TAPS - 1 - tap, axis=0)[HALO:, :])

    blk = pl.BlockSpec((tm, D), lambda cb, i: (i, cb))
    nxt = pl.BlockSpec((2 * HALO, D), lambda cb, i: (jnp.minimum((i + 1) * (r // 2), last_halo // 2), cb))
    return pl.pallas_call(
        body, name="conv_bwd", grid=(CONV_DIM // D, n),
        out_shape=[jax.ShapeDtypeStruct(dproj.shape, dproj.dtype), jax.ShapeDtypeStruct((8, CONV_DIM), F32),
                   jax.ShapeDtypeStruct((1, CONV_DIM), F32)],
        in_specs=[pl.BlockSpec((tm, D), lambda cb, i: (i, CONV_BLOCK0 + cb)),
                  pl.BlockSpec((HALO, D), lambda cb, i: (jnp.maximum(i * r - 1, 0), CONV_BLOCK0 + cb)),
                  blk, nxt, blk, nxt,
                  pl.BlockSpec((CONV_TAPS, D), lambda cb, i: (0, cb)), pl.BlockSpec(memory_space=pl.ANY)],
        out_specs=[pl.BlockSpec((None, tm, D), lambda cb, i: (CONV_BLOCK0 + cb, i, 0)),
                   pl.BlockSpec((8, D), lambda cb, i: (0, cb)), pl.BlockSpec((1, D), lambda cb, i: (0, cb))],
        input_output_aliases={7: 0},
        compiler_params=_cparams(("parallel", "arbitrary")),
    )(proj, proj, dxc, dxc, dsilu, dsilu, conv_w, dproj)


def dt_fill(ddt, dproj):
    t = ddt.shape[0]
    tm = _tile(t, WIDE_ROW_TILE)
    w = ddt.shape[1]

    def body(d_ref, dp_in_ref, o_ref):
        o_ref[:, :w] = d_ref[...]
        o_ref[:, w:] = jnp.zeros((tm, D - w), o_ref.dtype)

    return pl.pallas_call(
        body, name="dt_fill", grid=(t // tm,),
        out_shape=jax.ShapeDtypeStruct(dproj.shape, dproj.dtype),
        in_specs=[pl.BlockSpec((tm, w), lambda i: (i, 0)), pl.BlockSpec(memory_space=pl.ANY)],
        out_specs=pl.BlockSpec((None, tm, D), lambda i: (DT_COL_BLOCK, i, 0)),
        input_output_aliases={1: 0},
        compiler_params=_cparams(("parallel",)),
    )(ddt, dproj)


Z_BLOCK0 = 8
DT_COL_BLOCK = 9
DT_BLOCK0 = 8 * DT_COL_BLOCK
GATE_BLOCK0 = 10
B_BLOCK0 = 16
C_BLOCK0 = 20


def _head_expand():
    e = np.zeros((N_STATE, GROUP_W), np.float32)
    for hh in range(HEADS_PER_GROUP):
        e[hh, hh * HEAD_P:(hh + 1) * HEAD_P] = 1.0
    return jnp.asarray(e, BF16)


def _ssd_chunk_terms(dt, bias, alog, expand, tril_f, eye):
    dtb = dt + bias
    delta = jnp.maximum(dtb, 0.0) + jnp.log(1.0 + jnp.exp(-jnp.abs(dtb)))
    ea = jnp.exp(alog)
    a = -ea * delta
    acum = _sel(_nn, a, tril_f, 3, x_first=False)
    delta_e = _sel(_nn, delta, expand, 2)
    acum_e = _sel(_nn, acum, expand, 2)
    acum_t = _sel(_nt, acum, eye, 3, x_first=False)
    return dtb, delta, ea, a, acum, delta_e, acum_e, acum_t


def ssd_fwd(proj, xc, alog4, bias4, dskip4, wnorm, expand):
    t = proj.shape[0]
    tb = _tile(t, TOKEN_BLOCK)
    ncb = tb // SSD_CHUNK

    def body(xs_ref, b_ref, c_ref, dt_ref, z_ref, alog_ref, bias_ref, dsk_ref, wn_ref, e_ref, ob_ref, st_ref, state):
        @pl.when(pl.program_id(1) == 0)
        def _():
            state[...] = jnp.zeros_like(state)

        expand = e_ref[...]
        mask = _tri(SSD_CHUNK)
        tril_f = mask.astype(BF16)
        eye = (lax.broadcasted_iota(jnp.int32, (N_STATE, N_STATE), 0) ==
               lax.broadcasted_iota(jnp.int32, (N_STATE, N_STATE), 1)).astype(BF16)
        alog, bias = alog_ref[0], bias_ref[0]
        d_e = _sel(_nn, jnp.broadcast_to(dsk_ref[0], (8, N_STATE)), expand, 3)[0:1, :]
        wn = wn_ref[...]

        def chunk(c, carry):
            sl = pl.ds(pl.multiple_of(c * SSD_CHUNK, SSD_CHUNK), SSD_CHUNK)
            xs, bm, cm, dt, z = xs_ref[sl, :], b_ref[sl, :], c_ref[sl, :], dt_ref[sl, :], z_ref[sl, :]
            dtb, delta, ea, a, acum, delta_e, acum_e, acum_t = _ssd_chunk_terms(dt, bias, alog, expand, tril_f, eye)
            alast_e = acum_e[SSD_CHUNK - 1:SSD_CHUNK, :]
            xd = xs * delta_e
            xdb = xd.astype(BF16)
            cb_, bb_ = cm.astype(BF16), bm.astype(BF16)
            cbm = _nt(cb_, bb_)
            ys = []
            for hh in range(HEADS_PER_GROUP):
                lh = jnp.where(mask, jnp.exp(jnp.minimum(acum[:, hh:hh + 1] - acum_t[hh:hh + 1, :], 0.0)), 0.0)
                ys.append(_nn((cbm * lh).astype(BF16), xdb[:, hh * HEAD_P:(hh + 1) * HEAD_P]))
            st = state[...]
            st_ref[0, c] = st
            y = jnp.concatenate(ys, axis=1) + _nn(cb_, st.astype(BF16)) * jnp.exp(acum_e) + xs * d_e
            state[...] = st * jnp.exp(alast_e) + _tn(bb_, (xd * jnp.exp(alast_e - acum_e)).astype(BF16))
            yg = y * z * _sigmoid(z)
            ob_ref[sl, :] = (yg * lax.rsqrt(jnp.mean(yg * yg, axis=-1, keepdims=True) + RMS_EPS) * wn).astype(BF16)
            return carry

        lax.fori_loop(0, ncb, chunk, 0, unroll=min(CHUNK_UNROLL, ncb))

    small = pl.BlockSpec((1, 1, N_STATE), lambda g, j: (g, 0, 0))
    return pl.pallas_call(
        body, name="ssd_fwd", grid=(N_GROUPS, t // tb),
        out_shape=[jax.ShapeDtypeStruct((t, B_INNER), BF16),
                   jax.ShapeDtypeStruct((N_GROUPS, t // SSD_CHUNK, N_STATE, GROUP_W), F32)],
        in_specs=[pl.BlockSpec((tb, GROUP_W), lambda g, j: (j, g)),
                  pl.BlockSpec((tb, N_STATE), lambda g, j: (j, B_BLOCK0 + g)),
                  pl.BlockSpec((tb, N_STATE), lambda g, j: (j, C_BLOCK0 + g)),
                  pl.BlockSpec((tb, N_STATE), lambda g, j: (j, DT_BLOCK0 + g)),
                  pl.BlockSpec((tb, GROUP_W), lambda g, j: (j, Z_BLOCK0 + g)),
                  small, small, small, pl.BlockSpec((1, GROUP_W), lambda g, j: (0, g)),
                  pl.BlockSpec((N_STATE, GROUP_W), lambda g, j: (0, 0))],
        out_specs=[pl.BlockSpec((tb, GROUP_W), lambda g, j: (j, g)),
                   pl.BlockSpec((1, ncb, N_STATE, GROUP_W), lambda g, j: (g, j, 0, 0))],
        scratch_shapes=[pltpu.VMEM((N_STATE, GROUP_W), F32)],
        compiler_params=_cparams(("parallel", "arbitrary")),
    )(xc, xc, xc, proj, proj, alog4, bias4, dskip4, wnorm, expand)


def ssd_bwd(proj, xc, alog4, bias4, dskip4, wnorm, expand, dyb, w_bb, states, part, dproj):
    t = proj.shape[0]
    tb = _tile(t, TOKEN_BLOCK)
    lc = min(SSD_CHUNK_BWD, tb)
    ncb = tb // lc
    nsaved = tb // SSD_CHUNK
    nb = t // tb

    def body(xs_ref, b_ref, c_ref, dt_ref, z_ref, alog_ref, bias_ref, dsk_ref, wn_ref, e_ref, dyb_ref, wbb_ref, st_ref,
             part_ref, dp_in_ref, dxs_ref, db_ref, dc_ref, dz_ref, ddt_ref, dwn_ref, dalog_ref, dbias_ref, ddsk_ref,
             parts_ref, dstate, dob_ref, send_sems, recv_sems, local_sem):
        xchg_start, xchg_wait = _chip_exchange(part_ref, parts_ref, send_sems, recv_sems, local_sem)
        dob_ref[...] = _nt(dyb_ref[...], wbb_ref[...])

        @pl.when((pl.program_id(0) == 0) & (pl.program_id(1) == 0))
        def _():
            xchg_start()

        @pl.when(pl.program_id(1) == 0)
        def _():
            dstate[...] = jnp.zeros_like(dstate)
            dwn_ref[...] = jnp.zeros_like(dwn_ref)
            dalog_ref[...] = jnp.zeros_like(dalog_ref)
            dbias_ref[...] = jnp.zeros_like(dbias_ref)
            ddsk_ref[...] = jnp.zeros_like(ddsk_ref)

        expand = e_ref[...]
        mask = _tri(lc)
        mask_t = _tri(lc, upper=True)
        tril_f = mask.astype(BF16)
        triu_f = mask_t.astype(BF16)
        eye = (lax.broadcasted_iota(jnp.int32, (N_STATE, N_STATE), 0) ==
               lax.broadcasted_iota(jnp.int32, (N_STATE, N_STATE), 1)).astype(BF16)
        alog, bias = alog_ref[0], bias_ref[0]
        d_e = _sel(_nn, jnp.broadcast_to(dsk_ref[0], (8, N_STATE)), expand, 3)[0:1, :]
        wn = wn_ref[...]

        def chunk(i, c0):
            c = ncb - 1 - i
            sl = pl.ds(pl.multiple_of(c * lc, lc), lc)
            xs, bm, cm, dt, z = xs_ref[sl, :], b_ref[sl, :], c_ref[sl, :], dt_ref[sl, :], z_ref[sl, :]
            dtb, delta, ea, a, acum, delta_e, acum_e, acum_t = _ssd_chunk_terms(dt, bias, alog, expand, tril_f, eye)
            alast_e = acum_e[lc - 1:lc, :]
            eacum = jnp.exp(acum_e)
            wl = jnp.exp(alast_e - acum_e)
            xd = xs * delta_e
            xdb = xd.astype(BF16)
            cb_, bb_ = cm.astype(BF16), bm.astype(BF16)
            cbm = _nt(cb_, bb_)
            st32 = st_ref[0, c * (lc // SSD_CHUNK)]
            stb = st32.astype(BF16)
            dst = dstate[...]
            dstb = dst.astype(BF16)
            lhs, mixes, ys = [], [], []
            for hh in range(HEADS_PER_GROUP):
                col, row = acum[:, hh:hh + 1], acum_t[hh:hh + 1, :]
                lh = jnp.where(mask, jnp.exp(jnp.minimum(col - row, 0.0)), 0.0)
                mix = (cbm * lh).astype(BF16)
                lhs.append(lh)
                mixes.append(mix)
                ys.append(_nn(mix, xdb[:, hh * HEAD_P:(hh + 1) * HEAD_P]))
            y_in = jnp.concatenate(ys, axis=1)
            y_out = _nn(cb_, stb) * eacum
            y = y_in + y_out + xs * d_e
            sgz = _sigmoid(z)
            sz = z * sgz
            yg = y * sz
            rstd = lax.rsqrt(jnp.mean(yg * yg, axis=-1, keepdims=True) + RMS_EPS)
            nrm = yg * rstd
            dob_v = dob_ref[sl, :]
            dn = dob_v * wn
            dwn_ref[...] += _colsum(dob_v * nrm)
            dyg = rstd * (dn - nrm * jnp.mean(dn * nrm, axis=-1, keepdims=True))
            dy = dyg * sz
            dz_ref[sl, :] = (dyg * y * _dsilu(z, sgz)).astype(BF16)
            dyb = dy.astype(BF16)
            dxds = []
            dcb = jnp.zeros((lc, lc), F32)
            for hh in range(HEADS_PER_GROUP):
                hs = slice(hh * HEAD_P, (hh + 1) * HEAD_P)
                dy_h, x_h = dyb[:, hs], xdb[:, hs]
                dxds.append(_tn(mixes[hh], dy_h))
                dcb = dcb + _nt(dy_h, x_h) * lhs[hh]
            dcbb = dcb.astype(BF16)
            dye = (dy * eacum).astype(BF16)
            xw = (xd * wl).astype(BF16)
            dxd_in = jnp.concatenate(dxds, axis=1)
            dxd_out = wl * _nn(bb_, dstb)
            dxd = dxd_in + dxd_out
            dc_ref[sl, :] = (_nn(dcbb, bb_) + _nt(dye, stb)).astype(dc_ref.dtype)
            db_ref[sl, :] = (_tn(dcbb, cb_) + _nt(xw, dstb)).astype(db_ref.dtype)
            dstate[...] = dst * jnp.exp(alast_e) + _tn(cb_, dye)
            col_out = xd * dxd_out
            dac = _sel(_nt, dyb.astype(F32) * y_in - xdb.astype(F32) * dxd_in + dy * y_out - col_out, expand, 2)
            beyond = _colsum(col_out) + jnp.exp(alast_e) * _colsum(dst * st32)
            da = (_sel(_nn, dac, triu_f, 3, x_first=False) +
                  _sel(_nt, jnp.broadcast_to(beyond, (8, GROUP_W)), expand, 3)[0:1, :])
            ddelta = _sel(_nt, dxd * xs, expand, 2) - da * ea
            dalog_ref[0] += _colsum(da * a)
            ddtb = ddelta * _sigmoid(dtb)
            dbias_ref[0] += _colsum(ddtb)
            ddt_ref[sl, :] = ddtb.astype(BF16)
            ddsk_ref[0] += _sel(_nt, jnp.broadcast_to(_colsum(dy * xs), (8, GROUP_W)), expand, 3)[0:1, :]
            dxs_ref[sl, :] = (dxd * delta_e + dy * d_e).astype(dxs_ref.dtype)
            return c0

        lax.fori_loop(0, ncb, chunk, 0, unroll=min(CHUNK_UNROLL, ncb))

        @pl.when((pl.program_id(0) == N_GROUPS - 1) & (pl.program_id(1) == nb - 1))
        def _():
            xchg_wait()

    small = pl.BlockSpec((1, 1, N_STATE), lambda g, j: (g, 0, 0))
    wide = pl.BlockSpec((tb, GROUP_W), lambda g, j: (nb - 1 - j, g))
    narrow = pl.BlockSpec((tb, N_STATE), lambda g, j: (nb - 1 - j, g))
    hbm = pl.BlockSpec(memory_space=pl.ANY)
    return pl.pallas_call(
        body, name="ssd_bwd", grid=(N_GROUPS, nb),
        out_shape=[jax.ShapeDtypeStruct((t, B_INNER), BF16), jax.ShapeDtypeStruct((t, GROUP_W), BF16),
                   jax.ShapeDtypeStruct((t, GROUP_W), BF16), jax.ShapeDtypeStruct(dproj.shape, dproj.dtype),
                   jax.ShapeDtypeStruct((t, GROUP_W), BF16), jax.ShapeDtypeStruct((1, B_INNER), F32),
                   jax.ShapeDtypeStruct((N_GROUPS, 1, N_STATE), F32), jax.ShapeDtypeStruct((N_GROUPS, 1, N_STATE), F32),
                   jax.ShapeDtypeStruct((N_GROUPS, 1, N_STATE), F32), jax.ShapeDtypeStruct(part.shape, part.dtype)],
        in_specs=[wide,
                  pl.BlockSpec((tb, N_STATE), lambda g, j: (nb - 1 - j, B_BLOCK0 + g)),
                  pl.BlockSpec((tb, N_STATE), lambda g, j: (nb - 1 - j, C_BLOCK0 + g)),
                  pl.BlockSpec((tb, N_STATE), lambda g, j: (nb - 1 - j, DT_BLOCK0 + g)),
                  pl.BlockSpec((tb, GROUP_W), lambda g, j: (nb - 1 - j, Z_BLOCK0 + g)),
                  small, small, small, pl.BlockSpec((1, GROUP_W), lambda g, j: (0, g)),
                  pl.BlockSpec((N_STATE, GROUP_W), lambda g, j: (0, 0)),
                  pl.BlockSpec((tb, D), lambda g, j: (nb - 1 - j, 0)), pl.BlockSpec((GROUP_W, D), lambda g, j: (g, 0)),
                  pl.BlockSpec((1, nsaved, N_STATE, GROUP_W), lambda g, j: (g, nb - 1 - j, 0, 0)), hbm, hbm],
        out_specs=[wide, narrow, narrow,
                   pl.BlockSpec((None, tb, GROUP_W), lambda g, j: (Z_BLOCK0 // 2 + g // 2, nb - 1 - j, g % 2)),
                   narrow, pl.BlockSpec((1, GROUP_W), lambda g, j: (0, g)), small, small, small, hbm],
        input_output_aliases={14: 3},
        scratch_shapes=[pltpu.VMEM((N_STATE, GROUP_W), F32), pltpu.VMEM((tb, GROUP_W), F32)] + CHIP_SEMS,
        compiler_params=_cparams(("arbitrary", "arbitrary")),
    )(xc, xc, xc, proj, proj, alog4, bias4, dskip4, wnorm, expand, dyb, w_bb, states, part, dproj)


def lower_bound_fwd(hgrn_lb):
    def body(a_ref, o_ref):
        a0, a1 = a_ref[0:1, :], a_ref[1:2, :]
        m = jnp.maximum(a0, a1)
        e0, e1 = jnp.exp(a0 - m), jnp.exp(a1 - m)
        o_ref[...] = e0 / (e0 + e1)

    return pl.pallas_call(body, name="lower_bound_fwd", out_shape=jax.ShapeDtypeStruct((1, D), F32))(hgrn_lb)


def ada_weight_grad(c_all, dmod_cols):
    def body(c_ref, d_ref, o_ref):
        cval = c_ref[...]
        o_ref[...] = _tn(cval * _sigmoid(cval), d_ref[...], HI)

    return pl.pallas_call(body, name="ada_weight_grad",
                          out_shape=jax.ShapeDtypeStruct((D, dmod_cols.shape[1]), F32))(c_all, dmod_cols)


def reduce_small(gathered, hgrn_lb, dlb_off):
    n = gathered.shape[2]

    def body(g_ref, a_ref, o_ref, glb_ref):
        s = g_ref[0]
        for d in range(1, N_DEV):
            s = s + g_ref[d]
        o_ref[...] = s
        a0, a1 = a_ref[0:1, :], a_ref[1:2, :]
        m = jnp.maximum(a0, a1)
        e0, e1 = jnp.exp(a0 - m), jnp.exp(a1 - m)
        p0 = e0 / (e0 + e1)
        tq = s[:, dlb_off:dlb_off + D] * p0 * (1.0 - p0)
        glb_ref[0:1, :] = tq
        glb_ref[1:2, :] = -tq

    return pl.pallas_call(body, name="reduce_small",
                          out_shape=[jax.ShapeDtypeStruct((1, n), F32), jax.ShapeDtypeStruct((2, D), F32)])(gathered, hgrn_lb)


def _adam_math(w, g, m, v):
    m2 = ADAM_B1 * m + (1.0 - ADAM_B1) * g
    v2 = ADAM_B2 * v + (1.0 - ADAM_B2) * (g * g)
    m_hat = m2 / (1.0 - ADAM_B1 ** ADAM_STEP)
    v_hat = v2 / (1.0 - ADAM_B2 ** ADAM_STEP)
    delta = -ADAM_LR * (m_hat / (jnp.sqrt(v_hat) + ADAM_EPS) + ADAM_WD * w)
    return delta, m2, v2


def _row_tile(rows, mult=8, cap=128):
    for cand in range(cap - cap % mult, 0, -mult):
        if rows % cand == 0:
            return cand
    return rows


def sum_parts(parts, name):
    n, rows, cols = parts.shape
    tr = _row_tile(rows, 16, 1024)

    def body(p_ref, o_ref):
        s = p_ref[0].astype(F32)
        for d in range(1, n):
            s = s + p_ref[d].astype(F32)
        o_ref[...] = s

    return pl.pallas_call(
        body, name=name, grid=(rows // tr,),
        out_shape=jax.ShapeDtypeStruct((rows, cols), F32),
        in_specs=[pl.BlockSpec((n, tr, cols), lambda i: (0, i, 0))],
        out_specs=pl.BlockSpec((tr, cols), lambda i: (i, 0)),
        compiler_params=_cparams(("parallel",)),
    )(parts)


def sum_pair(a, b, name):
    rows, cols = a.shape
    tr = _row_tile(rows, 16, 1024)

    def body(a_ref, b_ref, o_ref):
        o_ref[...] = (a_ref[...].astype(F32) + b_ref[...].astype(F32)).astype(o_ref.dtype)

    blk = pl.BlockSpec((tr, cols), lambda i: (i, 0))
    return pl.pallas_call(
        body, name=name, grid=(rows // tr,),
        out_shape=jax.ShapeDtypeStruct((rows, cols), a.dtype),
        in_specs=[blk, blk], out_specs=blk,
        compiler_params=_cparams(("parallel",)),
    )(a, b)


def adamw(w, g, m, v, name):
    rows, cols = w.shape
    tr = _row_tile(rows, 8, 256)

    def body(w_ref, g_ref, m_ref, v_ref, d_ref, m2_ref, v2_ref):
        delta, m2, v2 = _adam_math(w_ref[...], g_ref[...], m_ref[...], v_ref[...])
        d_ref[...] = delta
        m2_ref[...] = m2
        v2_ref[...] = v2

    blk = pl.BlockSpec((tr, cols), lambda i: (i, 0))
    return pl.pallas_call(
        body, name=name, grid=(rows // tr,),
        out_shape=[jax.ShapeDtypeStruct((rows, cols), F32)] * 3,
        in_specs=[blk] * 4, out_specs=[blk] * 3,
        compiler_params=_cparams(("parallel",)),
    )(w, g, m, v)


def _pad128(n):
    return -(-n // 128) * 128


def _pack(arrays):
    offs, parts, off = [], [], 0
    for a in arrays:
        flat = a.reshape(1, -1)
        n = flat.shape[1]
        offs.append(off)
        parts.append(jnp.pad(flat, ((0, 0), (0, _pad128(n) - n))))
        off += _pad128(n)
    return jnp.concatenate(parts, axis=1), offs


def _unpack(vec, offs, shapes):
    out = []
    for off, shp in zip(offs, shapes):
        n = int(np.prod(shp))
        out.append(vec[0, off:off + n].reshape(shp))
    return out


IN_ROWS = IN_DIM // N_DEV
DT_ROW0 = 9216
DT_DEV, DT_LO = divmod(DT_ROW0, IN_ROWS)


GATE_SHIFT = D - 32


def _in_row_pieces(tile):
    pieces = []
    if tile == DT_COL_BLOCK:
        for g in range(N_GROUPS):
            o = DT_ROW0 + HEADS_PER_GROUP * g
            pieces.append((N_STATE * g, o // IN_ROWS, o % IN_ROWS, HEADS_PER_GROUP))
        return pieces
    r, end = tile * D, (tile + 1) * D
    while r < end:
        o = r if r < DT_ROW0 else r - GATE_SHIFT
        dev, loc = divmod(o, IN_ROWS)
        n = min(end - r, IN_ROWS - loc)
        pieces.append((r - tile * D, dev, loc, n))
        r += n
    return pieces


def assemble_w_in(g_all):
    ntile = N_PROJ // D

    def body(g_ref, o_ref):
        j = pl.program_id(0)
        for tile in range(ntile):
            @pl.when(j == tile)
            def _(tile=tile):
                if tile == DT_COL_BLOCK:
                    o_ref[...] = jnp.zeros_like(o_ref)
                for dst, dev, loc, n in _in_row_pieces(tile):
                    o_ref[pl.ds(dst, n), :] = g_ref[dev, pl.ds(loc, n), :]

    return pl.pallas_call(
        body, name="assemble_w_in", grid=(ntile,),
        out_shape=jax.ShapeDtypeStruct((N_PROJ, D), g_all.dtype),
        in_specs=[pl.BlockSpec(memory_space=pltpu.VMEM)],
        out_specs=pl.BlockSpec((D, D), lambda j: (j, 0)),
        compiler_params=_cparams(("arbitrary",)),
    )(g_all)


def _grad_in_blocks(g_t, core, slot):
    dt0 = DT_COL_BLOCK * D
    dt = g_t[dt0:dt0 + N_GROUPS * N_STATE].reshape(N_GROUPS, N_STATE, D)[:, :HEADS_PER_GROUP].reshape(32, D)
    with_dt = jnp.concatenate([g_t[DT_DEV * IN_ROWS:DT_ROW0], dt,
                               g_t[DT_ROW0 + 32 + GATE_SHIFT:(DT_DEV + 1) * IN_ROWS + GATE_SHIFT]], axis=0)
    blocks = []
    for q in range(N_CHIP):
        if 2 * q + 1 < DT_DEV:
            blk = lax.dynamic_slice_in_dim(g_t, IN_ROWS * (2 * q + core), IN_ROWS, axis=0)
        else:
            assert 2 * q == DT_DEV
            after = g_t[(DT_DEV + 1) * IN_ROWS + GATE_SHIFT:(DT_DEV + 2) * IN_ROWS + GATE_SHIFT]
            blk = jnp.where(core == 0, with_dt, after)
        blocks.append(jnp.pad(blk, ((0, slot - IN_ROWS), (0, 0))))
    return jnp.stack(blocks)


def kernel(x, c, w_ada, b_ada, w_in, hgrn_lb, hgrn_gnorm, ssm_conv_w, ssm_conv_b, ssm_dt_bias, ssm_a_log, ssm_d, ssm_norm, w_branch_a, w_branch_b, w_o, ln1_g, ln1_b, w_ffn_gate, w_ffn_up, w_ffn_down, ln2_g, ln2_b, loss_target, m_w_ada, m_b_ada, m_w_in, m_hgrn_lb, m_hgrn_gnorm, m_ssm_conv_w, m_ssm_conv_b, m_ssm_dt_bias, m_ssm_a_log, m_ssm_d, m_ssm_norm, m_w_branch_a, m_w_branch_b, m_w_o, m_ln1_g, m_ln1_b, m_w_ffn_gate, m_w_ffn_up, m_w_ffn_down, m_ln2_g, m_ln2_b, v_w_ada, v_b_ada, v_w_in, v_hgrn_lb, v_hgrn_gnorm, v_ssm_conv_w, v_ssm_conv_b, v_ssm_dt_bias, v_ssm_a_log, v_ssm_d, v_ssm_norm, v_w_branch_a, v_w_branch_b, v_w_o, v_ln1_g, v_ln1_b, v_w_ffn_gate, v_w_ffn_up, v_w_ffn_down, v_ln2_g, v_ln2_b):
    me = 4 * lax.axis_index("x") + 2 * lax.axis_index("y") + lax.axis_index("c")
    xt = x[0]
    tgt = loss_target[0]
    t = xt.shape[0]
    ada_cols = w_ada.shape[2]
    conv_cols = ssm_conv_w.shape[2]

    small_in, _ = _pack([c, ssm_conv_w[0]])
    small_all = allgather_vmem(small_in, "allgather_small_inputs")
    c_all = small_all[:, 0, :D]
    conv_w = small_all[:, 0, D:D + CONV_TAPS * conv_cols].reshape(N_DEV, CONV_TAPS, conv_cols)
    conv_w = conv_w.transpose(1, 0, 2).reshape(CONV_TAPS, CONV_DIM)
    mod = ada_modulation(c_all, w_ada[0], b_ada.reshape(N_DEV, 1, ada_cols))
    mod6 = mod.reshape(6, D)

    shards = [w_in[0].T, w_branch_a[0], w_branch_b[0], w_o[0], w_ffn_gate[0].T, w_ffn_up[0].T, w_ffn_down[0]]
    shard_rows = [s.shape[0] for s in shards]
    slot_rows = [-(-r // 32) * 32 for r in shard_rows]
    row_offs = [sum(slot_rows[:i]) for i in range(len(shards))]
    padded = [jnp.pad(s.astype(BF16), ((0, p - r), (0, 0))) for s, r, p in zip(shards, shard_rows, slot_rows)]
    w_in_t = assemble_w_in(allgather_hbm(padded[0], "allgather_w_in"))

    lb = lower_bound_fwd(hgrn_lb)
    u1 = ln_modulate(xt, mod6, 0, 1, "ln_modulate_1")
    proj, g_rest = mm_nt_gather(u1, w_in_t, F32, jnp.concatenate(padded[1:], axis=0), "mm_in_proj")
    g_ba, g_bb, g_o, g_fg, g_fu, g_fd = (g_rest[:, o - slot_rows[0]:o - slot_rows[0] + r]
                                         for o, r in zip(row_offs[1:], shard_rows[1:]))
    w_ba = g_ba.reshape(D, D)
    w_bb = g_bb.reshape(B_INNER, D)
    w_oo = g_o.reshape(D, D)
    w_gu_t = jnp.concatenate([g_fg.reshape(D_FF, D), g_fu.reshape(D_FF, D)], axis=0)
    w_dn = g_fd.reshape(D_FF, D)
    o_a, o_raw, st_a = hgrn_fwd(proj, lb, hgrn_gnorm)
    xc, conv_slope = conv_fwd(proj, conv_w, ssm_conv_b)
    pad3 = ((0, 0), (0, 0), (0, N_STATE - HEADS_PER_GROUP))
    alog4 = jnp.pad(ssm_a_log.reshape(N_GROUPS, 1, HEADS_PER_GROUP), pad3)
    bias4 = jnp.pad(ssm_dt_bias.reshape(N_GROUPS, 1, HEADS_PER_GROUP), pad3)
    dskip4 = jnp.pad(ssm_d.reshape(N_GROUPS, 1, HEADS_PER_GROUP), pad3)
    expand = _head_expand()
    o_b, st_b = ssd_fwd(proj, xc, alog4, bias4, dskip4, ssm_norm, expand)
    ya = mm_nn(o_a, w_ba, BF16, "mm_branch_a")
    yb = mm_nn(o_b, w_bb, BF16, "mm_branch_b")
    merged = merge_gates(ya, yb, proj)
    h1 = mm_nn(merged, w_oo, F32, "mm_out_proj")
    x1 = resid_ln(xt, h1, mod6, 2, ln1_g, ln1_b, "resid_ln_1")
    u2 = ln_modulate(x1, mod6, 3, 4, "ln_modulate_2")
    gu = mm_nt(u2, w_gu_t, BF16, "mm_ffn_in")
    act = swiglu_act(gu)
    h2 = mm_nn(act, w_dn, F32, "mm_ffn_out")

    dh2, dx1_part, acc4 = resid_ln_bwd(x1, h2, mod6, 5, ln2_g, ln2_b, tgt, True, "resid_ln_2_bwd")
    g_dn = mm_tn(act, dh2, "mm_grad_ffn_down")
    dact = mm_nt(dh2, w_dn, BF16, "mm_dact")
    dgu = swiglu_act_bwd(gu, dact)
    g_gu_t = mm_tn(dgu, u2, "mm_grad_ffn_in")
    du2 = mm_nn(dgu, w_gu_t, F32, "mm_du2")
    dx1, acc3 = ln_modulate_bwd(x1, du2, mod6, 4, dx1_part, "ln_modulate_2_bwd")
    dh1, dx_part, acc2 = resid_ln_bwd(xt, h1, mod6, 2, ln1_g, ln1_b, dx1, False, "resid_ln_1_bwd")
    g_o = mm_tn(merged, dh1, "mm_grad_out_proj")
    dya, dyb, dproj = merge_gates_bwd(dh1, w_oo, ya, yb, proj)
    g_ba_full = mm_tn(o_a, dya, "mm_grad_branch_a")
    g_bb_full = mm_tn(o_b, dyb, "mm_grad_branch_b")
    my_core = lax.axis_index("c")

    def by_core(blocks, rows, slots):
        contrib = jnp.concatenate([jnp.pad(b.reshape(N_DEV, -1, D), ((0, 0), (0, p - r), (0, 0)))
                                   for b, r, p in zip(blocks, rows, slots)], axis=1)
        split = contrib.reshape(N_CHIP, 2, contrib.shape[1], D).transpose(1, 0, 2, 3)
        return (lax.dynamic_index_in_dim(split, my_core, 0, keepdims=False),
                lax.dynamic_index_in_dim(split, 1 - my_core, 0, keepdims=False))

    keep_e, give_e = by_core([g_ba_full, g_bb_full, g_o, g_gu_t[:D_FF], g_gu_t[D_FF:], g_dn],
                             shard_rows[1:], slot_rows[1:])
    dproj, dlb, dgn, got_e = hgrn_bwd(proj, lb, hgrn_gnorm, o_raw, dya, w_ba, st_a, give_e, dproj)
    chip_e = sum_pair(keep_e.reshape(-1, D), got_e.reshape(-1, D), "sum_grads_rest_chip").reshape(keep_e.shape)
    dxs, dbm, dcm, dproj, ddt, dwn, dalog, dbias, ddsk, parts_e = ssd_bwd(proj, xc, alog4, bias4, dskip4, ssm_norm,
                                                                          expand, dyb, w_bb, st_b, chip_e, dproj)
    dxc = jnp.concatenate([dxs, dbm, dcm], axis=1)
    dproj, dcw, dcb = conv_bwd(proj, dxc, conv_slope, conv_w, dproj)
    dproj = dt_fill(ddt, dproj)
    g_in_t = mm_tn(dproj, u1, "mm_grad_in_proj")
    keep_l = _grad_in_blocks(g_in_t, my_core, slot_rows[0])
    give_l = _grad_in_blocks(g_in_t, 1 - my_core, slot_rows[0])
    got_l = exchange_sibling(give_l, "exchange_grad_in_sibling")
    chip_l = sum_pair(keep_l.reshape(-1, D), got_l.reshape(-1, D), "sum_grad_in_chip").reshape(keep_l.shape)
    du1, parts_l = mm_nn_exchange(dproj, w_in_t, F32, chip_l, "mm_du1")
    dx, acc1 = ln_modulate_bwd(xt, du1, mod6, 1, dx_part, "ln_modulate_1_bwd")
    gw_in = sum_parts(parts_l, "sum_grad_in")[:shard_rows[0]].T
    g_rows = sum_parts(parts_e, "sum_grads_rest")
    gw_ba, gw_bb, gw_o, gw_fg, gw_fu, gw_fd = (g_rows[o - slot_rows[0]:o - slot_rows[0] + r]
                                               for o, r in zip(row_offs[1:], shard_rows[1:]))
    gw_fg, gw_fu = gw_fg.T, gw_fu.T

    dmod = jnp.concatenate([acc1[1:2], acc1[0:1], acc2[0:1], acc3[1:2], acc3[0:1], acc4[0:1]], axis=1)
    small_fields = [dmod, acc4[3:4, :128], dlb, dgn, dcw[:CONV_TAPS], dcb, dbias, dalog, ddsk, dwn,
                    acc2[1:2], acc2[2:3], acc4[1:2], acc4[2:3]]
    small_out, offs = _pack(small_fields)
    small_sum_in = allgather_vmem(small_out, "allgather_small_grads")
    gsum, g_lb = reduce_small(small_sum_in, hgrn_lb, offs[2])
    (g_bada, loss_row, _, g_gn, g_cw_full, g_cb, g_bias4, g_alog4, g_dsk4, g_wn, g_l1g, g_l1b, g_l2g, g_l2b) = _unpack(
        gsum, offs, [(1, 6 * D), (1, 128), (1, D), (1, HK), (CONV_TAPS, CONV_DIM), (1, CONV_DIM),
                     (N_GROUPS, N_STATE), (N_GROUPS, N_STATE), (N_GROUPS, N_STATE), (1, B_INNER),
                     (1, D), (1, D), (1, D), (1, D)])
    loss = loss_row[0, 0]
    g_cw = lax.dynamic_slice(g_cw_full, (0, me * conv_cols), (CONV_TAPS, conv_cols))[None]
    g_dtb = g_bias4[:, :HEADS_PER_GROUP].reshape(1, 32)
    g_alog = g_alog4[:, :HEADS_PER_GROUP].reshape(1, 32)
    g_dsk = g_dsk4[:, :HEADS_PER_GROUP].reshape(1, 32)

    dmod_all = small_sum_in[:, 0, offs[0]:offs[0] + 6 * D]
    dmod_cols = lax.dynamic_slice(dmod_all, (0, me * ada_cols), (N_DEV, ada_cols))
    gw_ada = ada_weight_grad(c_all, dmod_cols)

    big = [("ada", w_ada[0], gw_ada, m_w_ada[0], v_w_ada[0]), ("in", w_in[0], gw_in, m_w_in[0], v_w_in[0]),
           ("branch_a", w_branch_a[0], gw_ba, m_w_branch_a[0], v_w_branch_a[0]),
           ("branch_b", w_branch_b[0], gw_bb, m_w_branch_b[0], v_w_branch_b[0]),
           ("o", w_o[0], gw_o, m_w_o[0], v_w_o[0]),
           ("ffn_gate", w_ffn_gate[0], gw_fg, m_w_ffn_gate[0], v_w_ffn_gate[0]),
           ("ffn_up", w_ffn_up[0], gw_fu, m_w_ffn_up[0], v_w_ffn_up[0]),
           ("ffn_down", w_ffn_down[0], gw_fd, m_w_ffn_down[0], v_w_ffn_down[0])]
    big_out = {}
    for nm, w_, g_, m_, v_ in big:
        d_, m2_, v2_ = adamw(w_, g_, m_, v_, "adamw_" + nm)
        big_out[nm] = (g_[None], d_[None], m2_[None], v2_[None])

    small_w = [b_ada, hgrn_lb, hgrn_gnorm, ssm_conv_w, ssm_conv_b, ssm_dt_bias, ssm_a_log, ssm_d, ssm_norm,
               ln1_g, ln1_b, ln2_g, ln2_b]
    small_g = [g_bada, g_lb, g_gn, g_cw, g_cb, g_dtb, g_alog, g_dsk, g_wn, g_l1g, g_l1b, g_l2g, g_l2b]
    small_m = [m_b_ada, m_hgrn_lb, m_hgrn_gnorm, m_ssm_conv_w, m_ssm_conv_b, m_ssm_dt_bias, m_ssm_a_log, m_ssm_d,
               m_ssm_norm, m_ln1_g, m_ln1_b, m_ln2_g, m_ln2_b]
    small_v = [v_b_ada, v_hgrn_lb, v_hgrn_gnorm, v_ssm_conv_w, v_ssm_conv_b, v_ssm_dt_bias, v_ssm_a_log, v_ssm_d,
               v_ssm_norm, v_ln1_g, v_ln1_b, v_ln2_g, v_ln2_b]
    shapes = [a.shape for a in small_w]
    small_g = [g_.reshape(s) for g_, s in zip(small_g, shapes)]
    pw, poffs = _pack(small_w)
    pg, _ = _pack(small_g)
    pm, _ = _pack(small_m)
    pv, _ = _pack(small_v)
    pd, pm2, pv2 = adamw(pw, pg, pm, pv, "adamw_small")
    s_d, s_m, s_v = (_unpack(p, poffs, shapes) for p in (pd, pm2, pv2))
    (sn_bada, sn_lb, sn_gn, sn_cw, sn_cb, sn_dtb, sn_alog, sn_dsk, sn_wn, sn_l1g, sn_l1b, sn_l2g, sn_l2b) = range(13)

    def order(kind):
        sm = [small_g, s_d, s_m, s_v][kind]
        bg = lambda nm: big_out[nm][kind]
        return [bg("ada"), sm[sn_bada], bg("in"), sm[sn_lb], sm[sn_gn], sm[sn_cw], sm[sn_cb], sm[sn_dtb], sm[sn_alog],
                sm[sn_dsk], sm[sn_wn], bg("branch_a"), bg("branch_b"), bg("o"), sm[sn_l1g], sm[sn_l1b],
                bg("ffn_gate"), bg("ffn_up"), bg("ffn_down"), sm[sn_l2g], sm[sn_l2b]]

    return (loss, dx[None], *order(0), *order(1), *order(2), *order(3))
```

```python
import numpy as np
import jax
import jax.numpy as jnp
from jax import lax
from jax.experimental import pallas as pl
from jax.experimental.pallas import tpu as pltpu

F32 = jnp.float32
BF16 = jnp.bfloat16
HI = lax.Precision.HIGHEST

N_DEV = 8
D = 1024
N_HEADS_A = 8
HK = 128
CHUNK = 64
SSD_CHUNK = 128
SSD_CHUNK_BWD = 256
N_GROUPS = 4
HEADS_PER_GROUP = 8
HEAD_P = 64
N_STATE = 128
GROUP_W = HEADS_PER_GROUP * HEAD_P
B_INNER = 2048
CONV_DIM = 3072
D_FF = 2816
IN_DIM = 11296
N_PROJ = 12288
ALPHA = 2.0 ** 0.25
LN_EPS = 1e-5
RMS_EPS = 1e-6
Q_SCALE = 128 ** -0.5
EXP_CLIP = 80.0
ADAM_LR, ADAM_B1, ADAM_B2, ADAM_EPS, ADAM_WD, ADAM_STEP = 0.001, 0.9, 0.999, 1e-8, 0.01, 10
VMEM_LIMIT = 48 * 1024 * 1024
TOKEN_BLOCK = 1024
ROW_TILE = 512
WIDE_ROW_TILE = 1024
MM_ROW_TILE = 1024
MM_TOKEN_TILE = 4096
MM_K_TILE = 3072
MM_COL_TILE = 1408
HGRN_HEADS_PER_STEP = 4
CHUNK_UNROLL = 8
MESH_ID = pl.DeviceIdType.MESH

NT_DIMS = (((1,), (1,)), ((), ()))
TN_DIMS = (((0,), (0,)), ((), ()))


def _cparams(sem=None):
    return pltpu.CompilerParams(dimension_semantics=sem, vmem_limit_bytes=VMEM_LIMIT)


def _sigmoid(x):
    return 1.0 / (1.0 + jnp.exp(-x))


def _dsilu(x, s):
    return s * (1.0 + x * (1.0 - s))


def _nt(a, b, precision=None):
    return lax.dot_general(a, b, NT_DIMS, precision=precision, preferred_element_type=F32)


def _tn(a, b, precision=None):
    return lax.dot_general(a, b, TN_DIMS, precision=precision, preferred_element_type=F32)


def _nn(a, b, precision=None):
    return jnp.dot(a, b, precision=precision, preferred_element_type=F32)


def _split(x, pieces):
    out = []
    for i in range(pieces):
        p = x.astype(BF16)
        out.append(p)
        if i + 1 < pieces:
            x = x - p.astype(F32)
    return out


def _sel(dot, x, sel01, pieces, x_first=True):
    acc = None
    for p in _split(x, pieces):
        term = dot(p, sel01) if x_first else dot(sel01, p)
        acc = term if acc is None else acc + term
    return acc


def _ln(x):
    mu = jnp.mean(x, axis=-1, keepdims=True)
    xc = x - mu
    rstd = lax.rsqrt(jnp.mean(xc * xc, axis=-1, keepdims=True) + LN_EPS)
    return xc * rstd, rstd


def _ln_bwd(dxh, xh, rstd):
    return rstd * (dxh - jnp.mean(dxh, axis=-1, keepdims=True) - xh * jnp.mean(dxh * xh, axis=-1, keepdims=True))


def _colsum(x):
    return jnp.sum(x, axis=0, keepdims=True)


def _tri(n, upper=False):
    r = lax.broadcasted_iota(jnp.int32, (n, n), 0)
    c = lax.broadcasted_iota(jnp.int32, (n, n), 1)
    return (c >= r) if upper else (r >= c)


def _my_pos():
    return lax.axis_index("x"), lax.axis_index("y"), lax.axis_index("c")


def _peer(pos, k):
    x, y, c = pos
    return (x ^ ((k >> 2) & 1), y ^ ((k >> 1) & 1), c ^ (k & 1))


def _flat(pos):
    return 4 * pos[0] + 2 * pos[1] + pos[2]


def allgather_vmem(v, name):
    n = v.shape[1]

    def body(v_ref, o_ref, send_sems, recv_sems, local_sem):
        me = _my_pos()
        mine = pltpu.make_async_copy(v_ref, o_ref.at[_flat(me)], local_sem)
        mine.start()
        sends = []
        for k in range(1, N_DEV):
            peer = _peer(me, k)
            cp = pltpu.make_async_remote_copy(v_ref, o_ref.at[_flat(me)], send_sems.at[k - 1], recv_sems.at[k - 1],
                                              device_id=peer, device_id_type=MESH_ID)
            cp.start()
            sends.append(cp)
        for k in range(1, N_DEV):
            peer = _peer(me, k)
            pltpu.make_async_remote_copy(v_ref, o_ref.at[_flat(peer)], send_sems.at[k - 1], recv_sems.at[k - 1],
                                         device_id=peer, device_id_type=MESH_ID).wait_recv()
        for cp in sends:
            cp.wait_send()
        mine.wait()

    return pl.pallas_call(
        body, name=name,
        out_shape=jax.ShapeDtypeStruct((N_DEV, 1, n), F32),
        in_specs=[pl.BlockSpec(memory_space=pltpu.VMEM)],
        out_specs=pl.BlockSpec(memory_space=pltpu.VMEM),
        scratch_shapes=[pltpu.SemaphoreType.DMA((N_DEV - 1,)), pltpu.SemaphoreType.DMA((N_DEV - 1,)),
                        pltpu.SemaphoreType.DMA],
        compiler_params=_cparams(),
    )(v)


def ada_modulation(c_all, w_ada_s, b_ada_r):
    ncol = w_ada_s.shape[1]

    def body(c_ref, w_ref, b_ref, o_ref, part_ref, send_sems, recv_sems):
        me = _my_pos()
        cval = c_ref[...]
        cond = cval * _sigmoid(cval)
        part = _nn(cond, w_ref[...], HI)
        for r in range(N_DEV):
            part_ref[r] = part[r:r + 1, :]
        sends = []
        for k in range(1, N_DEV):
            peer = _peer(me, k)
            cp = pltpu.make_async_remote_copy(part_ref.at[_flat(peer)], o_ref.at[_flat(me)], send_sems.at[k - 1],
                                              recv_sems.at[k - 1], device_id=peer, device_id_type=MESH_ID)
            cp.start()
            sends.append(cp)
        o_ref[_flat(me)] = part_ref[_flat(me)]
        for k in range(1, N_DEV):
            peer = _peer(me, k)
            pltpu.make_async_remote_copy(part_ref.at[_flat(peer)], o_ref.at[_flat(peer)], send_sems.at[k - 1],
                                         recv_sems.at[k - 1], device_id=peer, device_id_type=MESH_ID).wait_recv()
        for cp in sends:
            cp.wait_send()
        o_ref[...] = o_ref[...] + b_ref[...]

    return pl.pallas_call(
        body, name="ada_modulation",
        out_shape=jax.ShapeDtypeStruct((N_DEV, 1, ncol), F32),
        in_specs=[pl.BlockSpec(memory_space=pltpu.VMEM)] * 3,
        out_specs=pl.BlockSpec(memory_space=pltpu.VMEM),
        scratch_shapes=[pltpu.VMEM((N_DEV, 1, ncol), F32), pltpu.SemaphoreType.DMA((N_DEV - 1,)),
                        pltpu.SemaphoreType.DMA((N_DEV - 1,))],
        compiler_params=_cparams(),
    )(c_all, w_ada_s, b_ada_r)


def allgather_hbm(shard, name):
    def body(x_ref, out_ref, send_sems, recv_sems, local_sem):
        x, y, c = _my_pos()
        me, sibling = (x, y, c), (x, y, 1 - c)
        chips = [(1 - x, y), (x, 1 - y), (1 - x, 1 - y)]

        def slot(pos):
            return out_ref.at[_flat(pos)]

        def copy(k, block, to, src=None):
            return pltpu.make_async_remote_copy(slot(block) if src is None else src, slot(block), send_sems.at[k],
                                                recv_sems.at[k], device_id=to, device_id_type=MESH_ID)

        mine = pltpu.make_async_copy(x_ref, slot(me), local_sem)
        mine.start()
        first = [copy(0, me, sibling, src=x_ref)]
        first += [copy(1 + j, me, (*chip, c), src=x_ref) for j, chip in enumerate(chips)]
        for cp in first:
            cp.start()
        passed = [copy(4 + j, (*chip, c), sibling) for j, chip in enumerate(chips)]
        for j, chip in enumerate(chips):
            copy(1 + j, (*chip, c), me).wait_recv()
            passed[j].start()
        copy(0, sibling, me).wait_recv()
        for j, chip in enumerate(chips):
            copy(4 + j, (*chip, 1 - c), me).wait_recv()
        for cp in first + passed:
            cp.wait_send()
        mine.wait()

    return pl.pallas_call(
        body, name=name,
        out_shape=jax.ShapeDtypeStruct((N_DEV,) + shard.shape, shard.dtype),
        in_specs=[pl.BlockSpec(memory_space=pl.ANY)],
        out_specs=pl.BlockSpec(memory_space=pl.ANY),
        scratch_shapes=[pltpu.SemaphoreType.DMA((N_DEV - 1,)), pltpu.SemaphoreType.DMA((N_DEV - 1,)),
                        pltpu.SemaphoreType.DMA],
        compiler_params=_cparams(),
    )(shard)


N_CHIP = N_DEV // 2
SIBLING_SEMS = [pltpu.SemaphoreType.DMA, pltpu.SemaphoreType.DMA]
CHIP_SEMS = [pltpu.SemaphoreType.DMA((N_CHIP - 1,)), pltpu.SemaphoreType.DMA((N_CHIP - 1,)), pltpu.SemaphoreType.DMA]


def _sibling_exchange(s_ref, o_ref, send_sem, recv_sem):
    x, y, c = _my_pos()
    cp = pltpu.make_async_remote_copy(s_ref, o_ref, send_sem, recv_sem, device_id=(x, y, 1 - c), device_id_type=MESH_ID)
    return cp.start, cp.wait


def _chip_exchange(p_ref, o_ref, send_sems, recv_sems, local_sem):
    x, y, c = _my_pos()
    my_chip = 2 * x + y
    mine = pltpu.make_async_copy(p_ref.at[my_chip], o_ref.at[my_chip], local_sem)
    peers = [(x ^ (k >> 1), y ^ (k & 1)) for k in range(1, N_CHIP)]
    sends = [pltpu.make_async_remote_copy(p_ref.at[2 * px + py], o_ref.at[my_chip], send_sems.at[k], recv_sems.at[k],
                                          device_id=(px, py, c), device_id_type=MESH_ID)
             for k, (px, py) in enumerate(peers)]
    recvs = [pltpu.make_async_remote_copy(p_ref.at[2 * px + py], o_ref.at[2 * px + py], send_sems.at[k], recv_sems.at[k],
                                          device_id=(px, py, c), device_id_type=MESH_ID)
             for k, (px, py) in enumerate(peers)]

    def start():
        mine.start()
        for cp in sends:
            cp.start()

    def wait():
        for cp in recvs:
            cp.wait_recv()
        for cp in sends:
            cp.wait_send()
        mine.wait()

    return start, wait


def exchange_sibling(send, name):
    def body(s_ref, o_ref, send_sem, recv_sem):
        start, wait = _sibling_exchange(s_ref, o_ref, send_sem, recv_sem)
        start()
        wait()

    return pl.pallas_call(
        body, name=name,
        out_shape=jax.ShapeDtypeStruct(send.shape, send.dtype),
        in_specs=[pl.BlockSpec(memory_space=pl.ANY)],
        out_specs=pl.BlockSpec(memory_space=pl.ANY),
        scratch_shapes=SIBLING_SEMS,
        compiler_params=_cparams(),
    )(send)


LANES = 128


def _k_tile(kdim, unit=LANES):
    for cand in range(MM_K_TILE - MM_K_TILE % unit, 0, -unit):
        if kdim % cand == 0:
            return cand
    return kdim


def _lane_tile(n, cap):
    for cand in range(cap - cap % LANES, 0, -LANES):
        if n % cand == 0:
            return cand
    return n


def _m_tile(m, kdim):
    return min(MM_ROW_TILE if kdim > D else 2 * MM_ROW_TILE, m)


def _mm(a, b, out_dtype, name, b_is_nk):
    if a.ndim == 3:
        m, tk = a.shape[1], a.shape[2]
        kdim = a.shape[0] * tk
        a_spec = pl.BlockSpec((None, _m_tile(m, kdim), tk), lambda j, i, k: (k, i, 0))
    else:
        m, kdim = a.shape
        tk = _k_tile(kdim)
        a_spec = pl.BlockSpec((_m_tile(m, kdim), tk), lambda j, i, k: (i, k))
    n = b.shape[0] if b_is_nk else b.shape[1]
    tm, tn = _m_tile(m, kdim), _lane_tile(n, MM_COL_TILE)
    nk = kdim // tk
    dot = _nt if b_is_nk else _nn

    def body(a_ref, b_ref, o_ref, *acc):
        p = dot(a_ref[...], b_ref[...])
        if nk == 1:
            o_ref[...] = p.astype(o_ref.dtype)
        else:
            acc_ref, k = acc[0], pl.program_id(2)

            @pl.when(k == 0)
            def _():
                acc_ref[...] = p

            @pl.when(k > 0)
            def _():
                acc_ref[...] += p

            @pl.when(k == nk - 1)
            def _():
                o_ref[...] = acc_ref[...].astype(o_ref.dtype)

    b_spec = (pl.BlockSpec((tn, tk), lambda j, i, k: (j, k)) if b_is_nk else
              pl.BlockSpec((tk, tn), lambda j, i, k: (k, j)))
    return pl.pallas_call(
        body, name=name, grid=(n // tn, m // tm, nk),
        out_shape=jax.ShapeDtypeStruct((m, n), out_dtype),
        in_specs=[a_spec, b_spec],
        out_specs=pl.BlockSpec((tm, tn), lambda j, i, k: (i, j)),
        scratch_shapes=[] if nk == 1 else [pltpu.VMEM((tm, tn), F32)],
        compiler_params=_cparams(("parallel", "parallel", "arbitrary")),
    )(a, b)


def mm_nn(a, b, out_dtype, name):
    return _mm(a, b, out_dtype, name, False)


def mm_nt(a, b, out_dtype, name):
    return _mm(a, b, out_dtype, name, True)


def mm_nn_exchange(a, b, out_dtype, part, name):
    kblocks, m, kb = a.shape
    kdim = kblocks * kb
    n = b.shape[1]
    tm, tn, tk = min(MM_ROW_TILE, m), _lane_tile(n, MM_COL_TILE), _k_tile(kdim)
    gn, gm, nk = n // tn, m // tm, kdim // tk
    per_step = tk // kb

    def body(a_ref, b_ref, part_ref, o_ref, parts_ref, acc_ref, send_sems, recv_sems, local_sem):
        j, i, k = pl.program_id(0), pl.program_id(1), pl.program_id(2)
        xchg_start, xchg_wait = _chip_exchange(part_ref, parts_ref, send_sems, recv_sems, local_sem)

        @pl.when((j == 0) & (i == 0) & (k == 0))
        def _():
            xchg_start()

        p = _nn(a_ref[0], b_ref[0:kb, :])
        for c in range(1, per_step):
            p = p + _nn(a_ref[c], b_ref[c * kb:(c + 1) * kb, :])

        @pl.when(k == 0)
        def _():
            acc_ref[...] = p

        @pl.when(k > 0)
        def _():
            acc_ref[...] += p

        @pl.when(k == nk - 1)
        def _():
            o_ref[...] = acc_ref[...].astype(o_ref.dtype)

        @pl.when((j == gn - 1) & (i == gm - 1) & (k == nk - 1))
        def _():
            xchg_wait()

    hbm = pl.BlockSpec(memory_space=pl.ANY)
    return pl.pallas_call(
        body, name=name, grid=(gn, gm, nk),
        out_shape=[jax.ShapeDtypeStruct((m, n), out_dtype), jax.ShapeDtypeStruct(part.shape, part.dtype)],
        in_specs=[pl.BlockSpec((per_step, tm, kb), lambda j, i, k: (k, i, 0)),
                  pl.BlockSpec((tk, tn), lambda j, i, k: (k, j)), hbm],
        out_specs=[pl.BlockSpec((tm, tn), lambda j, i, k: (i, j)), hbm],
        scratch_shapes=[pltpu.VMEM((tm, tn), F32)] + CHIP_SEMS,
        compiler_params=_cparams(("arbitrary", "arbitrary", "arbitrary")),
    )(a, b, part)


def mm_nt_gather(a, b, out_dtype, shard, name):
    m, kdim = a.shape
    n = b.shape[0]
    tm, tn = _m_tile(m, kdim), 1024
    assert kdim == 1024
    gj = m // tm
    nsteps = (n // tn) * gj
    forward_step = max(nsteps - 2, 0)

    def body(a_ref, b_ref, x_ref, o_ref, g_ref, send_sems, recv_sems, local_sem):
        step = pl.program_id(0) * gj + pl.program_id(1)
        x, y, c = _my_pos()
        me, sibling = (x, y, c), (x, y, 1 - c)
        chips = [(1 - x, y), (x, 1 - y), (1 - x, 1 - y)]

        def slot(pos):
            return g_ref.at[_flat(pos)]

        def copy(k, block, to, src=None):
            return pltpu.make_async_remote_copy(slot(block) if src is None else src, slot(block), send_sems.at[k],
                                                recv_sems.at[k], device_id=to, device_id_type=MESH_ID)

        mine = pltpu.make_async_copy(x_ref, slot(me), local_sem)
        first = [copy(0, me, sibling, src=x_ref)]
        first += [copy(1 + j, me, (*chip, c), src=x_ref) for j, chip in enumerate(chips)]
        passed = [copy(4 + j, (*chip, c), sibling) for j, chip in enumerate(chips)]

        @pl.when(step == 0)
        def _():
            mine.start()
            for cp in first:
                cp.start()

        rows = pl.ds(pl.multiple_of(pl.program_id(1) * tm, tm), tm)
        o_ref[...] = _nt(a_ref[rows, :], b_ref[...]).astype(o_ref.dtype)

        @pl.when(step == forward_step)
        def _():
            for j, chip in enumerate(chips):
                copy(1 + j, (*chip, c), me).wait_recv()
                passed[j].start()

        @pl.when(step == nsteps - 1)
        def _():
            copy(0, sibling, me).wait_recv()
            for j, chip in enumerate(chips):
                copy(4 + j, (*chip, 1 - c), me).wait_recv()
            for cp in first + passed:
                cp.wait_send()
            mine.wait()

    return pl.pallas_call(
        body, name=name, grid=(n // tn, gj),
        out_shape=[jax.ShapeDtypeStruct((m, n), out_dtype), jax.ShapeDtypeStruct((N_DEV,) + shard.shape, shard.dtype)],
        in_specs=[pl.BlockSpec(memory_space=pltpu.VMEM), pl.BlockSpec((tn, kdim), lambda j, i: (j, 0)),
                  pl.BlockSpec(memory_space=pl.ANY)],
        out_specs=[pl.BlockSpec((tm, tn), lambda j, i: (i, j)), pl.BlockSpec(memory_space=pl.ANY)],
        scratch_shapes=[pltpu.SemaphoreType.DMA((N_DEV - 1,)), pltpu.SemaphoreType.DMA((N_DEV - 1,)),
                        pltpu.SemaphoreType.DMA],
        compiler_params=_cparams(("arbitrary", "arbitrary")),
    )(a, b, shard)


def mm_tn(a, b, name):
    tt, tn = min(MM_TOKEN_TILE, b.shape[0]), _lane_tile(b.shape[1], MM_COL_TILE)
    tka = _lane_tile(a.shape[-1], 1024)
    if a.ndim == 3:
        t, ka = a.shape[1], a.shape[0] * a.shape[2]
        per = a.shape[2] // tka
        a_spec = pl.BlockSpec((None, tt, tka), lambda i, j, s: (i // per, s, i % per))
    else:
        t, ka = a.shape
        a_spec = pl.BlockSpec((tt, tka), lambda i, j, s: (s, i))
    n = b.shape[1]
    nt = t // tt

    def body(a_ref, b_ref, o_ref, *acc):
        p = _tn(a_ref[...], b_ref[...])
        if nt == 1:
            o_ref[...] = p.astype(o_ref.dtype)
        else:
            acc_ref, s = acc[0], pl.program_id(2)

            @pl.when(s == 0)
            def _():
                acc_ref[...] = p

            @pl.when(s > 0)
            def _():
                acc_ref[...] += p

            @pl.when(s == nt - 1)
            def _():
                o_ref[...] = acc_ref[...].astype(o_ref.dtype)

    return pl.pallas_call(
        body, name=name, grid=(ka // tka, n // tn, nt),
        out_shape=jax.ShapeDtypeStruct((ka, n), BF16),
        in_specs=[a_spec, pl.BlockSpec((tt, tn), lambda i, j, s: (s, j))],
        out_specs=pl.BlockSpec((tka, tn), lambda i, j, s: (i, j)),
        scratch_shapes=[] if nt == 1 else [pltpu.VMEM((tka, tn), F32)],
        compiler_params=_cparams(("parallel", "parallel", "arbitrary")),
    )(a, b)


def _tile(t, cap):
    return min(cap, t)


def ln_modulate(x, mod6, shift_row, scale_row, name):
    t = x.shape[0]
    tm = _tile(t, WIDE_ROW_TILE)

    def body(x_ref, mod_ref, o_ref):
        xh, _ = _ln(x_ref[...])
        sc = mod_ref[scale_row:scale_row + 1, :]
        sh = mod_ref[shift_row:shift_row + 1, :]
        o_ref[...] = (xh * (1.0 + sc) + sh).astype(BF16)

    return pl.pallas_call(
        body, name=name, grid=(t // tm,),
        out_shape=jax.ShapeDtypeStruct((t, D), BF16),
        in_specs=[pl.BlockSpec((tm, D), lambda i: (i, 0)), pl.BlockSpec((6, D), lambda i: (0, 0))],
        out_specs=pl.BlockSpec((tm, D), lambda i: (i, 0)),
        compiler_params=_cparams(("parallel",)),
    )(x, mod6)


def resid_ln(x, h, mod6, gate_row, ln_g, ln_b, name):
    t = x.shape[0]
    tm = _tile(t, WIDE_ROW_TILE)

    def body(x_ref, h_ref, mod_ref, g_ref, b_ref, o_ref):
        r = ALPHA * x_ref[...] + mod_ref[gate_row:gate_row + 1, :] * h_ref[...]
        rh, _ = _ln(r)
        o_ref[...] = rh * g_ref[...] + b_ref[...]

    row = pl.BlockSpec((tm, D), lambda i: (i, 0))
    vec = pl.BlockSpec((1, D), lambda i: (0, 0))
    return pl.pallas_call(
        body, name=name, grid=(t // tm,),
        out_shape=jax.ShapeDtypeStruct((t, D), F32),
        in_specs=[row, row, pl.BlockSpec((6, D), lambda i: (0, 0)), vec, vec],
        out_specs=row,
        compiler_params=_cparams(("parallel",)),
    )(x, h, mod6, ln_g, ln_b)


def resid_ln_bwd(x, h, mod6, gate_row, ln_g, ln_b, cot, with_loss, name):
    t = x.shape[0]
    tm = _tile(t, ROW_TILE)

    def body(x_ref, h_ref, mod_ref, g_ref, b_ref, c_ref, dh_ref, dx_ref, acc_ref):
        @pl.when(pl.program_id(0) == 0)
        def _():
            acc_ref[...] = jnp.zeros_like(acc_ref)

        gate = mod_ref[gate_row:gate_row + 1, :]
        hv = h_ref[...]
        r = ALPHA * x_ref[...] + gate * hv
        rh, rstd = _ln(r)
        lng = g_ref[...]
        if with_loss:
            diff = rh * lng + b_ref[...] - c_ref[...]
            dxo = diff * (1.0 / D)
            lsum = jnp.sum(_colsum(diff * diff), axis=-1, keepdims=True) * (0.5 / D)
            acc_ref[3:4, :] += jnp.broadcast_to(lsum, (1, D))
        else:
            dxo = c_ref[...]
        acc_ref[1:2, :] += _colsum(dxo * rh)
        acc_ref[2:3, :] += _colsum(dxo)
        dr = _ln_bwd(dxo * lng, rh, rstd)
        acc_ref[0:1, :] += _colsum(dr * hv)
        dh_ref[...] = (gate * dr).astype(BF16)
        dx_ref[...] = ALPHA * dr

    row = pl.BlockSpec((tm, D), lambda i: (i, 0))
    vec = pl.BlockSpec((1, D), lambda i: (0, 0))
    return pl.pallas_call(
        body, name=name, grid=(t // tm,),
        out_shape=[jax.ShapeDtypeStruct((t, D), BF16), jax.ShapeDtypeStruct((t, D), F32),
                   jax.ShapeDtypeStruct((8, D), F32)],
        in_specs=[row, row, pl.BlockSpec((6, D), lambda i: (0, 0)), vec, vec, row],
        out_specs=[row, row, pl.BlockSpec((8, D), lambda i: (0, 0))],
        compiler_params=_cparams(("arbitrary",)),
    )(x, h, mod6, ln_g, ln_b, cot)


def ln_modulate_bwd(x, du, mod6, scale_row, dx_part, name):
    t = x.shape[0]
    tm = _tile(t, ROW_TILE)

    def body(x_ref, du_ref, mod_ref, dp_ref, dx_ref, acc_ref):
        @pl.when(pl.program_id(0) == 0)
        def _():
            acc_ref[...] = jnp.zeros_like(acc_ref)

        xh, rstd = _ln(x_ref[...])
        du_v = du_ref[...]
        sc = mod_ref[scale_row:scale_row + 1, :]
        acc_ref[0:1, :] += _colsum(du_v * xh)
        acc_ref[1:2, :] += _colsum(du_v)
        dx_ref[...] = dp_ref[...] + _ln_bwd(du_v * (1.0 + sc), xh, rstd)

    row = pl.BlockSpec((tm, D), lambda i: (i, 0))
    return pl.pallas_call(
        body, name=name, grid=(t // tm,),
        out_shape=[jax.ShapeDtypeStruct((t, D), F32), jax.ShapeDtypeStruct((8, D), F32)],
        in_specs=[row, row, pl.BlockSpec((6, D), lambda i: (0, 0)), row],
        out_specs=[row, pl.BlockSpec((8, D), lambda i: (0, 0))],
        compiler_params=_cparams(("arbitrary",)),
    )(x, du, mod6, dx_part)


def merge_gates(ya, yb, proj):
    t = ya.shape[0]
    tm = _tile(t, WIDE_ROW_TILE)

    def body(ya_ref, yb_ref, ga_ref, gb_ref, o_ref):
        o_ref[...] = (_sigmoid(ga_ref[...]) * ya_ref[...].astype(F32) +
                      _sigmoid(gb_ref[...]) * yb_ref[...].astype(F32)).astype(BF16)

    row = pl.BlockSpec((tm, D), lambda i: (i, 0))
    return pl.pallas_call(
        body, name="merge_gates", grid=(t // tm,),
        out_shape=jax.ShapeDtypeStruct((t, D), BF16),
        in_specs=[row, row, pl.BlockSpec((tm, D), lambda i: (i, GATE_BLOCK0)),
                  pl.BlockSpec((tm, D), lambda i: (i, GATE_BLOCK0 + 1))],
        out_specs=row,
        compiler_params=_cparams(("parallel",)),
    )(ya, yb, proj, proj)


def merge_gates_bwd(dh, w_o, ya, yb, proj):
    t = ya.shape[0]
    tm = _tile(t, ROW_TILE)

    def body(dh_ref, w_ref, ya_ref, yb_ref, ga_ref, gb_ref, dya_ref, dyb_ref, dp_ref):
        dmv = _nt(dh_ref[...], w_ref[...])
        sa = _sigmoid(ga_ref[...])
        sb = _sigmoid(gb_ref[...])
        dya_ref[...] = (dmv * sa).astype(BF16)
        dyb_ref[...] = (dmv * sb).astype(BF16)
        dp_ref[0] = (dmv * ya_ref[...].astype(F32) * sa * (1.0 - sa)).astype(BF16)
        dp_ref[1] = (dmv * yb_ref[...].astype(F32) * sb * (1.0 - sb)).astype(BF16)

    row = pl.BlockSpec((tm, D), lambda i: (i, 0))
    return pl.pallas_call(
        body, name="merge_gates_bwd", grid=(t // tm,),
        out_shape=[jax.ShapeDtypeStruct((t, D), BF16)] * 2 + [jax.ShapeDtypeStruct((N_PROJ // D, t, D), BF16)],
        in_specs=[row, pl.BlockSpec((D, D), lambda i: (0, 0)), row, row,
                  pl.BlockSpec((tm, D), lambda i: (i, GATE_BLOCK0)),
                  pl.BlockSpec((tm, D), lambda i: (i, GATE_BLOCK0 + 1))],
        out_specs=[row, row, pl.BlockSpec((2, tm, D), lambda i: (GATE_BLOCK0 // 2, i, 0))],
        compiler_params=_cparams(("parallel",)),
    )(dh, w_o, ya, yb, proj, proj)


FF_CHUNK = 1408


def ffn_in_act(u, w_gu_t):
    t = u.shape[0]
    tm = _tile(t, ROW_TILE)
    nj = D_FF // FF_CHUNK

    def body(a_ref, bg_ref, bu_ref, gu_ref, act_ref):
        a = a_ref[...]
        g = _nt(a, bg_ref[...])
        up = _nt(a, bu_ref[...])
        gu_ref[0] = g.astype(BF16)
        gu_ref[1] = up.astype(BF16)
        act_ref[...] = (g * _sigmoid(g) * up).astype(BF16)

    return pl.pallas_call(
        body, name="ffn_in_act", grid=(nj, t // tm),
        out_shape=[jax.ShapeDtypeStruct((2, t, D_FF), BF16), jax.ShapeDtypeStruct((t, D_FF), BF16)],
        in_specs=[pl.BlockSpec((tm, D), lambda j, i: (i, 0)), pl.BlockSpec((FF_CHUNK, D), lambda j, i: (j, 0)),
                  pl.BlockSpec((FF_CHUNK, D), lambda j, i: (nj + j, 0))],
        out_specs=[pl.BlockSpec((2, tm, FF_CHUNK), lambda j, i: (0, i, j)),
                   pl.BlockSpec((tm, FF_CHUNK), lambda j, i: (i, j))],
        compiler_params=_cparams(("parallel", "parallel")),
    )(u, w_gu_t, w_gu_t)


def ffn_act_bwd(dh, w_dn, gu):
    t = dh.shape[0]
    tm = _tile(t, ROW_TILE)

    def body(a_ref, b_ref, gu_ref, o_ref):
        da = _nt(a_ref[...], b_ref[...])
        g = gu_ref[0].astype(F32)
        up = gu_ref[1].astype(F32)
        s = _sigmoid(g)
        o_ref[0] = (da * up * _dsilu(g, s)).astype(BF16)
        o_ref[1] = (da * g * s).astype(BF16)

    blk = pl.BlockSpec((2, tm, FF_CHUNK), lambda j, i: (0, i, j))
    return pl.pallas_call(
        body, name="ffn_act_bwd", grid=(D_FF // FF_CHUNK, t // tm),
        out_shape=jax.ShapeDtypeStruct((2, t, D_FF), BF16),
        in_specs=[pl.BlockSpec((tm, D), lambda j, i: (i, 0)), pl.BlockSpec((FF_CHUNK, D), lambda j, i: (j, 0)), blk],
        out_specs=blk,
        compiler_params=_cparams(("parallel", "parallel")),
    )(dh, w_dn, gu)


def _hgrn_chunk_terms(q, fl, lbv, tril_f):
    sig = _sigmoid(fl)
    f = lbv + (1.0 - lbv) * sig
    lam = jnp.log(f)
    k = 1.0 - f
    sq = _sigmoid(q)
    qt = q * sq * Q_SCALE
    bc = _sel(_nn, lam, tril_f, 3, x_first=False)
    bmid = bc[CHUNK // 2 - 1:CHUNK // 2, :]
    bl = bc[CHUNK - 1:CHUNK, :]
    eq = jnp.exp(jnp.minimum(bc - bmid, EXP_CLIP))
    ek = jnp.exp(jnp.minimum(bmid - bc, EXP_CLIP))
    eb = jnp.exp(bc)
    ekl = jnp.exp(bl - bc)
    ebl = jnp.exp(bl)
    return sig, f, k, sq, qt, eq, ek, eb, ekl, ebl


def hgrn_fwd(proj, lb, gnorm):
    t = proj.shape[0]
    tb = _tile(t, TOKEN_BLOCK)
    ncb = tb // CHUNK

    hps = HGRN_HEADS_PER_STEP
    wide = hps * HK

    def body(q_ref, f_ref, i_ref, g_ref, lb_ref, gn_ref, oa_ref, oraw_ref, st_ref, state):
        @pl.when(pl.program_id(1) == 0)
        def _():
            state[...] = jnp.zeros_like(state)

        gn = gn_ref[...]
        mask = _tri(CHUNK)
        tril_f = mask.astype(BF16)

        def chunk(c, carry):
            sl = pl.ds(pl.multiple_of(c * CHUNK, CHUNK), CHUNK)
            for hh in range(hps):
                ln = slice(hh * HK, (hh + 1) * HK)
                q, fl, v, g = q_ref[sl, ln], f_ref[sl, ln], i_ref[sl, ln], g_ref[sl, ln]
                sig, f, k, sq, qt, eq, ek, eb, ekl, ebl = _hgrn_chunk_terms(q, fl, lb_ref[:, ln], tril_f)
                a = jnp.where(mask, _nt((qt * eq).astype(BF16), (k * ek).astype(BF16)), 0.0)
                st = state[hh]
                st_ref[hh, c] = st
                vb = v.astype(BF16)
                o = _nn(a.astype(BF16), vb) + _nt((qt * eb).astype(BF16), st.astype(BF16))
                state[hh] = st * ebl + _tn(vb, (k * ekl).astype(BF16))
                oraw_ref[sl, ln] = o
                rn = o * lax.rsqrt(jnp.mean(o * o, axis=-1, keepdims=True) + RMS_EPS)
                oa_ref[sl, ln] = (rn * gn * g * _sigmoid(g)).astype(BF16)
            return carry

        lax.fori_loop(0, ncb, chunk, 0, unroll=min(CHUNK_UNROLL, ncb))

    def col(block):
        return pl.BlockSpec((tb, wide), lambda h, j: (j, block * (N_HEADS_A // hps) + h))

    return pl.pallas_call(
        body, name="hgrn_fwd", grid=(N_HEADS_A // hps, t // tb),
        out_shape=[jax.ShapeDtypeStruct((t, D), BF16), jax.ShapeDtypeStruct((t, D), F32),
                   jax.ShapeDtypeStruct((N_HEADS_A, t // CHUNK, HK, HK), F32)],
        in_specs=[col(0), col(1), col(2), col(3), pl.BlockSpec((1, wide), lambda h, j: (0, h)),
                  pl.BlockSpec((1, HK), lambda h, j: (0, 0))],
        out_specs=[pl.BlockSpec((tb, wide), lambda h, j: (j, h)), pl.BlockSpec((tb, wide), lambda h, j: (j, h)),
                   pl.BlockSpec((hps, ncb, HK, HK), lambda h, j: (h, j, 0, 0))],
        scratch_shapes=[pltpu.VMEM((hps, HK, HK), F32)],
        compiler_params=_cparams(("parallel", "arbitrary")),
    )(proj, proj, proj, proj, lb, gnorm)


def hgrn_bwd(proj, lb, gnorm, o_raw, dya, w_ba, states, give, dproj):
    t = proj.shape[0]
    tb = _tile(t, TOKEN_BLOCK)
    ncb = tb // CHUNK
    nb = t // tb
    hps = HGRN_HEADS_PER_STEP
    wide = hps * HK

    def body(q_ref, f_ref, i_ref, g_ref, lb_ref, gn_ref, oraw_ref, dya_ref, wba_ref, st_ref, give_ref, dp_in_ref,
             dp_ref, dlb_ref, dgn_ref, got_ref, dstate, doa_ref, send_sem, recv_sem):
        h, j = pl.program_id(0), pl.program_id(1)
        swap_start, swap_wait = _sibling_exchange(give_ref, got_ref, send_sem, recv_sem)
        doa_ref[...] = _nt(dya_ref[...], wba_ref[...])

        @pl.when((h == 0) & (j == 0))
        def _():
            swap_start()

        @pl.when(j == 0)
        def _():
            dstate[...] = jnp.zeros_like(dstate)
            dlb_ref[...] = jnp.zeros_like(dlb_ref)

        @pl.when((j == 0) & (h == 0))
        def _():
            dgn_ref[...] = jnp.zeros_like(dgn_ref)

        gn = gn_ref[...]
        mask = _tri(CHUNK)
        mask_t = _tri(CHUNK, upper=True)
        tril_f = mask.astype(BF16)
        triu_f = mask_t.astype(BF16)

        def chunk(i, c0):
            c = ncb - 1 - i
            sl = pl.ds(pl.multiple_of(c * CHUNK, CHUNK), CHUNK)
            for hh in range(hps):
                ln = slice(hh * HK, (hh + 1) * HK)
                q, fl, v, g = q_ref[sl, ln], f_ref[sl, ln], i_ref[sl, ln], g_ref[sl, ln]
                lbv = lb_ref[:, ln]
                sig, f, k, sq, qt, eq, ek, eb, ekl, ebl = _hgrn_chunk_terms(q, fl, lbv, tril_f)
                qe = (qt * eq).astype(BF16)
                ke = (k * ek).astype(BF16)
                st32 = st_ref[hh, c]
                st = st32.astype(BF16)
                dst = dstate[hh]
                dstb = dst.astype(BF16)
                o = oraw_ref[sl, ln]
                rstd = lax.rsqrt(jnp.mean(o * o, axis=-1, keepdims=True) + RMS_EPS)
                rn = o * rstd
                sgm = _sigmoid(g)
                sg = g * sgm
                doa_v = doa_ref[sl, ln]
                drn = doa_v * gn * sg
                dgn_ref[...] += _colsum(doa_v * rn * sg)
                dp_ref[3, sl, ln] = (doa_v * rn * gn * _dsilu(g, sgm)).astype(BF16)
                do = rstd * (drn - rn * jnp.mean(drn * rn, axis=-1, keepdims=True))
                dob = do.astype(BF16)
                vb = v.astype(BF16)
                da = jnp.where(mask, _nt(dob, vb), 0.0).astype(BF16)
                da_t = jnp.where(mask_t, _nt(vb, dob), 0.0).astype(BF16)
                a_t = jnp.where(mask_t, _nt(ke, qe), 0.0).astype(BF16)
                kl = (k * ekl).astype(BF16)
                qb = (qt * eb).astype(BF16)
                dq_in = _nn(da, ke)
                dk_in = _nn(da_t, qe)
                dq_out = eb * _nn(dob, st)
                dk_out = ekl * _nn(vb, dstb)
                dqt = eq * dq_in + dq_out
                dk = ek * dk_in + dk_out
                dv = _nn(a_t, dob) + _nt(kl, dstb)
                dstate[hh] = dst * ebl + _tn(dob, qb)
                dbig = qe.astype(F32) * dq_in - ke.astype(F32) * dk_in + qt * dq_out - k * dk_out
                beyond = _colsum(k * dk_out) + ebl * _colsum(dst * st32)
                dlam = _sel(_nn, dbig, triu_f, 3, x_first=False) + beyond
                df = dlam / f - dk
                dp_ref[1, sl, ln] = (df * (1.0 - lbv) * sig * (1.0 - sig)).astype(BF16)
                dlb_ref[:, ln] += _colsum(df * (1.0 - sig))
                dp_ref[0, sl, ln] = (dqt * Q_SCALE * _dsilu(q, sq)).astype(BF16)
                dp_ref[2, sl, ln] = dv.astype(BF16)
            return c0

        lax.fori_loop(0, ncb, chunk, 0, unroll=min(CHUNK_UNROLL, ncb))

        @pl.when((h == N_HEADS_A // hps - 1) & (j == nb - 1))
        def _():
            swap_wait()

    def col(block):
        return pl.BlockSpec((tb, wide), lambda h, j: (nb - 1 - j, block * (N_HEADS_A // hps) + h))

    hcol = pl.BlockSpec((tb, wide), lambda h, j: (nb - 1 - j, h))
    hbm = pl.BlockSpec(memory_space=pl.ANY)
    return pl.pallas_call(
        body, name="hgrn_bwd", grid=(N_HEADS_A // hps, nb),
        out_shape=[jax.ShapeDtypeStruct(dproj.shape, dproj.dtype), jax.ShapeDtypeStruct((1, D), F32),
                   jax.ShapeDtypeStruct((1, HK), F32), jax.ShapeDtypeStruct(give.shape, give.dtype)],
        in_specs=[col(0), col(1), col(2), col(3), pl.BlockSpec((1, wide), lambda h, j: (0, h)),
                  pl.BlockSpec((1, HK), lambda h, j: (0, 0)), hcol,
                  pl.BlockSpec((tb, D), lambda h, j: (nb - 1 - j, 0)), pl.BlockSpec((wide, D), lambda h, j: (h, 0)),
                  pl.BlockSpec((hps, ncb, HK, HK), lambda h, j: (h, nb - 1 - j, 0, 0)), hbm, hbm],
        out_specs=[pl.BlockSpec((4, tb, wide), lambda h, j: (0, nb - 1 - j, h)),
                   pl.BlockSpec((1, wide), lambda h, j: (0, h)), pl.BlockSpec((1, HK), lambda h, j: (0, 0)), hbm],
        input_output_aliases={11: 0},
        scratch_shapes=[pltpu.VMEM((hps, HK, HK), F32), pltpu.VMEM((tb, wide), F32)] + SIBLING_SEMS,
        compiler_params=_cparams(("arbitrary", "arbitrary")),
    )(proj, proj, proj, proj, lb, gnorm, o_raw, dya, w_ba, states, give, dproj)


CONV_BLOCK0 = 6
CONV_TAPS = 4
HALO = 8


def conv_fwd(proj, conv_w, conv_b):
    t = proj.shape[0]
    tm = _tile(t, ROW_TILE)
    r = tm // HALO

    def body(x_ref, halo_ref, w_ref, b_ref, o_ref, ds_ref):
        i = pl.program_id(1)
        halo = jnp.where(i > 0, halo_ref[...], 0.0)
        ext = jnp.concatenate([halo, x_ref[...]], axis=0)
        pre = b_ref[...] + w_ref[CONV_TAPS - 1:CONV_TAPS, :] * ext[HALO:, :]
        for tap in range(CONV_TAPS - 1):
            pre = pre + w_ref[tap:tap + 1, :] * pltpu.roll(ext, CONV_TAPS - 1 - tap, axis=0)[HALO:, :]
        s = _sigmoid(pre)
        o_ref[...] = pre * s
        ds_ref[...] = _dsilu(pre, s).astype(BF16)

    blk = pl.BlockSpec((tm, D), lambda cb, i: (i, cb))
    return pl.pallas_call(
        body, name="conv_fwd", grid=(CONV_DIM // D, t // tm),
        out_shape=[jax.ShapeDtypeStruct((t, CONV_DIM), F32), jax.ShapeDtypeStruct((t, CONV_DIM), BF16)],
        in_specs=[pl.BlockSpec((tm, D), lambda cb, i: (i, CONV_BLOCK0 + cb)),
                  pl.BlockSpec((HALO, D), lambda cb, i: (jnp.maximum(i * r - 1, 0), CONV_BLOCK0 + cb)),
                  pl.BlockSpec((CONV_TAPS, D), lambda cb, i: (0, cb)), pl.BlockSpec((1, D), lambda cb, i: (0, cb))],
        out_specs=[blk, blk],
        compiler_params=_cparams(("parallel", "parallel")),
    )(proj, proj, conv_w, conv_b)


def conv_bwd(proj, dxc, dsilu, conv_w, dproj):
    t = proj.shape[0]
    tm = _tile(t, ROW_TILE)
    r = tm // HALO
    n = t // tm
    last_halo = t // HALO - 1

    def body(x_ref, prev_ref, d_ref, dnext_ref, s_ref, snext_ref, w_ref, dp_in_ref, dx_ref, dw_ref, db_ref):
        i = pl.program_id(1)

        @pl.when(i == 0)
        def _():
            dw_ref[...] = jnp.zeros_like(dw_ref)
            db_ref[...] = jnp.zeros_like(db_ref)

        dpre = jnp.concatenate([d_ref[...].astype(F32) * s_ref[...].astype(F32),
                                jnp.where(i < n - 1, dnext_ref[0:HALO, :].astype(F32) * snext_ref[0:HALO, :].astype(F32),
                                          0.0)], axis=0)
        dx = w_ref[CONV_TAPS - 1:CONV_TAPS, :] * dpre[:tm, :]
        for tap in range(CONV_TAPS - 1):
            back = CONV_TAPS - 1 - tap
            dx = dx + w_ref[tap:tap + 1, :] * pltpu.roll(dpre, tm + HALO - back, axis=0)[:tm, :]
        dx_ref[...] = dx.astype(BF16)
        dp = dpre[:tm, :]
        db_ref[...] += _colsum(dp)
        prev = jnp.where(i > 0, prev_ref[...], 0.0)
        ext = jnp.concatenate([prev, x_ref[...]], axis=0)
        dw_ref[CONV_TAPS - 1:CONV_TAPS, :] += _colsum(dp * ext[HALO:, :])
        for tap in range(CONV_TAPS - 1):
            dw_ref[tap:tap + 1, :] += _colsum(dp * pltpu.roll(ext, CONV_TAPS - 1 - tap, axis=0)[HALO:, :])

    blk = pl.BlockSpec((tm, D), lambda cb, i: (i, cb))
    nxt = pl.BlockSpec((2 * HALO, D), lambda cb, i: (jnp.minimum((i + 1) * (r // 2), last_halo // 2), cb))
    return pl.pallas_call(
        body, name="conv_bwd", grid=(CONV_DIM // D, n),
        out_shape=[jax.ShapeDtypeStruct(dproj.shape, dproj.dtype), jax.ShapeDtypeStruct((8, CONV_DIM), F32),
                   jax.ShapeDtypeStruct((1, CONV_DIM), F32)],
        in_specs=[pl.BlockSpec((tm, D), lambda cb, i: (i, CONV_BLOCK0 + cb)),
                  pl.BlockSpec((HALO, D), lambda cb, i: (jnp.maximum(i * r - 1, 0), CONV_BLOCK0 + cb)),
                  blk, nxt, blk, nxt,
                  pl.BlockSpec((CONV_TAPS, D), lambda cb, i: (0, cb)), pl.BlockSpec(memory_space=pl.ANY)],
        out_specs=[pl.BlockSpec((None, tm, D), lambda cb, i: (CONV_BLOCK0 + cb, i, 0)),
                   pl.BlockSpec((8, D), lambda cb, i: (0, cb)), pl.BlockSpec((1, D), lambda cb, i: (0, cb))],
        input_output_aliases={7: 0},
        compiler_params=_cparams(("parallel", "arbitrary")),
    )(proj, proj, dxc, dxc, dsilu, dsilu, conv_w, dproj)


def dt_fill(ddt, dproj):
    t = ddt.shape[0]
    tm = _tile(t, WIDE_ROW_TILE)
    w = ddt.shape[1]

    def body(d_ref, dp_in_ref, o_ref):
        o_ref[:, :w] = d_ref[...]
        o_ref[:, w:] = jnp.zeros((tm, D - w), o_ref.dtype)

    return pl.pallas_call(
        body, name="dt_fill", grid=(t // tm,),
        out_shape=jax.ShapeDtypeStruct(dproj.shape, dproj.dtype),
        in_specs=[pl.BlockSpec((tm, w), lambda i: (i, 0)), pl.BlockSpec(memory_space=pl.ANY)],
        out_specs=pl.BlockSpec((None, tm, D), lambda i: (DT_COL_BLOCK, i, 0)),
        input_output_aliases={1: 0},
        compiler_params=_cparams(("parallel",)),
    )(ddt, dproj)


Z_BLOCK0 = 8
DT_COL_BLOCK = 9
DT_BLOCK0 = 8 * DT_COL_BLOCK
GATE_BLOCK0 = 10
B_BLOCK0 = 16
C_BLOCK0 = 20


def _head_expand():
    e = np.zeros((N_STATE, GROUP_W), np.float32)
    for hh in range(HEADS_PER_GROUP):
        e[hh, hh * HEAD_P:(hh + 1) * HEAD_P] = 1.0
    return jnp.asarray(e, BF16)


def _ssd_chunk_terms(dt, bias, alog, expand, tril_f, eye):
    dtb = dt + bias
    delta = jnp.maximum(dtb, 0.0) + jnp.log(1.0 + jnp.exp(-jnp.abs(dtb)))
    ea = jnp.exp(alog)
    a = -ea * delta
    acum = _sel(_nn, a, tril_f, 3, x_first=False)
    delta_e = _sel(_nn, delta, expand, 2)
    acum_e = _sel(_nn, acum, expand, 2)
    acum_t = _sel(_nt, acum, eye, 3, x_first=False)
    return dtb, delta, ea, a, acum, delta_e, acum_e, acum_t


def ssd_fwd(proj, xc, alog4, bias4, dskip4, wnorm, expand):
    t = proj.shape[0]
    tb = _tile(t, TOKEN_BLOCK)
    ncb = tb // SSD_CHUNK

    def body(xs_ref, b_ref, c_ref, dt_ref, z_ref, alog_ref, bias_ref, dsk_ref, wn_ref, e_ref, ob_ref, st_ref, state):
        @pl.when(pl.program_id(1) == 0)
        def _():
            state[...] = jnp.zeros_like(state)

        expand = e_ref[...]
        mask = _tri(SSD_CHUNK)
        tril_f = mask.astype(BF16)
        eye = (lax.broadcasted_iota(jnp.int32, (N_STATE, N_STATE), 0) ==
               lax.broadcasted_iota(jnp.int32, (N_STATE, N_STATE), 1)).astype(BF16)
        alog, bias = alog_ref[0], bias_ref[0]
        d_e = _sel(_nn, jnp.broadcast_to(dsk_ref[0], (8, N_STATE)), expand, 3)[0:1, :]
        wn = wn_ref[...]

        def chunk(c, carry):
            sl = pl.ds(pl.multiple_of(c * SSD_CHUNK, SSD_CHUNK), SSD_CHUNK)
            xs, bm, cm, dt, z = xs_ref[sl, :], b_ref[sl, :], c_ref[sl, :], dt_ref[sl, :], z_ref[sl, :]
            dtb, delta, ea, a, acum, delta_e, acum_e, acum_t = _ssd_chunk_terms(dt, bias, alog, expand, tril_f, eye)
            alast_e = acum_e[SSD_CHUNK - 1:SSD_CHUNK, :]
            xd = xs * delta_e
            xdb = xd.astype(BF16)
            cb_, bb_ = cm.astype(BF16), bm.astype(BF16)
            cbm = _nt(cb_, bb_)
            ys = []
            for hh in range(HEADS_PER_GROUP):
                lh = jnp.where(mask, jnp.exp(jnp.minimum(acum[:, hh:hh + 1] - acum_t[hh:hh + 1, :], 0.0)), 0.0)
                ys.append(_nn((cbm * lh).astype(BF16), xdb[:, hh * HEAD_P:(hh + 1) * HEAD_P]))
            st = state[...]
            st_ref[0, c] = st
            y = jnp.concatenate(ys, axis=1) + _nn(cb_, st.astype(BF16)) * jnp.exp(acum_e) + xs * d_e
            state[...] = st * jnp.exp(alast_e) + _tn(bb_, (xd * jnp.exp(alast_e - acum_e)).astype(BF16))
            yg = y * z * _sigmoid(z)
            ob_ref[sl, :] = (yg * lax.rsqrt(jnp.mean(yg * yg, axis=-1, keepdims=True) + RMS_EPS) * wn).astype(BF16)
            return carry

        lax.fori_loop(0, ncb, chunk, 0, unroll=min(CHUNK_UNROLL, ncb))

    small = pl.BlockSpec((1, 1, N_STATE), lambda g, j: (g, 0, 0))
    return pl.pallas_call(
        body, name="ssd_fwd", grid=(N_GROUPS, t // tb),
        out_shape=[jax.ShapeDtypeStruct((t, B_INNER), BF16),
                   jax.ShapeDtypeStruct((N_GROUPS, t // SSD_CHUNK, N_STATE, GROUP_W), F32)],
        in_specs=[pl.BlockSpec((tb, GROUP_W), lambda g, j: (j, g)),
                  pl.BlockSpec((tb, N_STATE), lambda g, j: (j, B_BLOCK0 + g)),
                  pl.BlockSpec((tb, N_STATE), lambda g, j: (j, C_BLOCK0 + g)),
                  pl.BlockSpec((tb, N_STATE), lambda g, j: (j, DT_BLOCK0 + g)),
                  pl.BlockSpec((tb, GROUP_W), lambda g, j: (j, Z_BLOCK0 + g)),
                  small, small, small, pl.BlockSpec((1, GROUP_W), lambda g, j: (0, g)),
                  pl.BlockSpec((N_STATE, GROUP_W), lambda g, j: (0, 0))],
        out_specs=[pl.BlockSpec((tb, GROUP_W), lambda g, j: (j, g)),
                   pl.BlockSpec((1, ncb, N_STATE, GROUP_W), lambda g, j: (g, j, 0, 0))],
        scratch_shapes=[pltpu.VMEM((N_STATE, GROUP_W), F32)],
        compiler_params=_cparams(("parallel", "arbitrary")),
    )(xc, xc, xc, proj, proj, alog4, bias4, dskip4, wnorm, expand)


def ssd_bwd(proj, xc, alog4, bias4, dskip4, wnorm, expand, dyb, w_bb, states, part, dproj):
    t = proj.shape[0]
    tb = _tile(t, TOKEN_BLOCK)
    lc = min(SSD_CHUNK_BWD, tb)
    ncb = tb // lc
    nsaved = tb // SSD_CHUNK
    nb = t // tb

    def body(xs_ref, b_ref, c_ref, dt_ref, z_ref, alog_ref, bias_ref, dsk_ref, wn_ref, e_ref, dyb_ref, wbb_ref, st_ref,
             part_ref, dp_in_ref, dxs_ref, db_ref, dc_ref, dz_ref, ddt_ref, dwn_ref, dalog_ref, dbias_ref, ddsk_ref,
             parts_ref, dstate, dob_ref, send_sems, recv_sems, local_sem):
        xchg_start, xchg_wait = _chip_exchange(part_ref, parts_ref, send_sems, recv_sems, local_sem)
        dob_ref[...] = _nt(dyb_ref[...], wbb_ref[...])

        @pl.when((pl.program_id(0) == 0) & (pl.program_id(1) == 0))
        def _():
            xchg_start()

        @pl.when(pl.program_id(1) == 0)
        def _():
            dstate[...] = jnp.zeros_like(dstate)
            dwn_ref[...] = jnp.zeros_like(dwn_ref)
            dalog_ref[...] = jnp.zeros_like(dalog_ref)
            dbias_ref[...] = jnp.zeros_like(dbias_ref)
            ddsk_ref[...] = jnp.zeros_like(ddsk_ref)

        expand = e_ref[...]
        mask = _tri(lc)
        mask_t = _tri(lc, upper=True)
        tril_f = mask.astype(BF16)
        triu_f = mask_t.astype(BF16)
        eye = (lax.broadcasted_iota(jnp.int32, (N_STATE, N_STATE), 0) ==
               lax.broadcasted_iota(jnp.int32, (N_STATE, N_STATE), 1)).astype(BF16)
        alog, bias = alog_ref[0], bias_ref[0]
        d_e = _sel(_nn, jnp.broadcast_to(dsk_ref[0], (8, N_STATE)), expand, 3)[0:1, :]
        wn = wn_ref[...]

        def chunk(i, c0):
            c = ncb - 1 - i
            sl = pl.ds(pl.multiple_of(c * lc, lc), lc)
            xs, bm, cm, dt, z = xs_ref[sl, :], b_ref[sl, :], c_ref[sl, :], dt_ref[sl, :], z_ref[sl, :]
            dtb, delta, ea, a, acum, delta_e, acum_e, acum_t = _ssd_chunk_terms(dt, bias, alog, expand, tril_f, eye)
            alast_e = acum_e[lc - 1:lc, :]
            eacum = jnp.exp(acum_e)
            wl = jnp.exp(alast_e - acum_e)
            xd = xs * delta_e
            xdb = xd.astype(BF16)
            cb_, bb_ = cm.astype(BF16), bm.astype(BF16)
            cbm = _nt(cb_, bb_)
            st32 = st_ref[0, c * (lc // SSD_CHUNK)]
            stb = st32.astype(BF16)
            dst = dstate[...]
            dstb = dst.astype(BF16)
            lhs, mixes, ys = [], [], []
            for hh in range(HEADS_PER_GROUP):
                col, row = acum[:, hh:hh + 1], acum_t[hh:hh + 1, :]
                lh = jnp.where(mask, jnp.exp(jnp.minimum(col - row, 0.0)), 0.0)
                mix = (cbm * lh).astype(BF16)
                lhs.append(lh)
                mixes.append(mix)
                ys.append(_nn(mix, xdb[:, hh * HEAD_P:(hh + 1) * HEAD_P]))
            y_in = jnp.concatenate(ys, axis=1)
            y_out = _nn(cb_, stb) * eacum
            y = y_in + y_out + xs * d_e
            sgz = _sigmoid(z)
            sz = z * sgz
            yg = y * sz
            rstd = lax.rsqrt(jnp.mean(yg * yg, axis=-1, keepdims=True) + RMS_EPS)
            nrm = yg * rstd
            dob_v = dob_ref[sl, :]
            dn = dob_v * wn
            dwn_ref[...] += _colsum(dob_v * nrm)
            dyg = rstd * (dn - nrm * jnp.mean(dn * nrm, axis=-1, keepdims=True))
            dy = dyg * sz
            dz_ref[sl, :] = (dyg * y * _dsilu(z, sgz)).astype(BF16)
            dyb = dy.astype(BF16)
            dxds = []
            dcb = jnp.zeros((lc, lc), F32)
            for hh in range(HEADS_PER_GROUP):
                hs = slice(hh * HEAD_P, (hh + 1) * HEAD_P)
                dy_h, x_h = dyb[:, hs], xdb[:, hs]
                dxds.append(_tn(mixes[hh], dy_h))
                dcb = dcb + _nt(dy_h, x_h) * lhs[hh]
            dcbb = dcb.astype(BF16)
            dye = (dy * eacum).astype(BF16)
            xw = (xd * wl).astype(BF16)
            dxd_in = jnp.concatenate(dxds, axis=1)
            dxd_out = wl * _nn(bb_, dstb)
            dxd = dxd_in + dxd_out
            dc_ref[sl, :] = (_nn(dcbb, bb_) + _nt(dye, stb)).astype(dc_ref.dtype)
            db_ref[sl, :] = (_tn(dcbb, cb_) + _nt(xw, dstb)).astype(db_ref.dtype)
            dstate[...] = dst * jnp.exp(alast_e) + _tn(cb_, dye)
            col_out = xd * dxd_out
            dac = _sel(_nt, dyb.astype(F32) * y_in - xdb.astype(F32) * dxd_in + dy * y_out - col_out, expand, 2)
            beyond = _colsum(col_out) + jnp.exp(alast_e) * _colsum(dst * st32)
            da = (_sel(_nn, dac, triu_f, 3, x_first=False) +
                  _sel(_nt, jnp.broadcast_to(beyond, (8, GROUP_W)), expand, 3)[0:1, :])
            ddelta = _sel(_nt, dxd * xs, expand, 2) - da * ea
            dalog_ref[0] += _colsum(da * a)
            ddtb = ddelta * _sigmoid(dtb)
            dbias_ref[0] += _colsum(ddtb)
            ddt_ref[sl, :] = ddtb.astype(BF16)
            ddsk_ref[0] += _sel(_nt, jnp.broadcast_to(_colsum(dy * xs), (8, GROUP_W)), expand, 3)[0:1, :]
            dxs_ref[sl, :] = (dxd * delta_e + dy * d_e).astype(dxs_ref.dtype)
            return c0

        lax.fori_loop(0, ncb, chunk, 0, unroll=min(CHUNK_UNROLL, ncb))

        @pl.when((pl.program_id(0) == N_GROUPS - 1) & (pl.program_id(1) == nb - 1))
        def _():
            xchg_wait()

    small = pl.BlockSpec((1, 1, N_STATE), lambda g, j: (g, 0, 0))
    wide = pl.BlockSpec((tb, GROUP_W), lambda g, j: (nb - 1 - j, g))
    narrow = pl.BlockSpec((tb, N_STATE), lambda g, j: (nb - 1 - j, g))
    hbm = pl.BlockSpec(memory_space=pl.ANY)
    return pl.pallas_call(
        body, name="ssd_bwd", grid=(N_GROUPS, nb),
        out_shape=[jax.ShapeDtypeStruct((t, B_INNER), BF16), jax.ShapeDtypeStruct((t, GROUP_W), BF16),
                   jax.ShapeDtypeStruct((t, GROUP_W), BF16), jax.ShapeDtypeStruct(dproj.shape, dproj.dtype),
                   jax.ShapeDtypeStruct((t, GROUP_W), BF16), jax.ShapeDtypeStruct((1, B_INNER), F32),
                   jax.ShapeDtypeStruct((N_GROUPS, 1, N_STATE), F32), jax.ShapeDtypeStruct((N_GROUPS, 1, N_STATE), F32),
                   jax.ShapeDtypeStruct((N_GROUPS, 1, N_STATE), F32), jax.ShapeDtypeStruct(part.shape, part.dtype)],
        in_specs=[wide,
                  pl.BlockSpec((tb, N_STATE), lambda g, j: (nb - 1 - j, B_BLOCK0 + g)),
                  pl.BlockSpec((tb, N_STATE), lambda g, j: (nb - 1 - j, C_BLOCK0 + g)),
                  pl.BlockSpec((tb, N_STATE), lambda g, j: (nb - 1 - j, DT_BLOCK0 + g)),
                  pl.BlockSpec((tb, GROUP_W), lambda g, j: (nb - 1 - j, Z_BLOCK0 + g)),
                  small, small, small, pl.BlockSpec((1, GROUP_W), lambda g, j: (0, g)),
                  pl.BlockSpec((N_STATE, GROUP_W), lambda g, j: (0, 0)),
                  pl.BlockSpec((tb, D), lambda g, j: (nb - 1 - j, 0)), pl.BlockSpec((GROUP_W, D), lambda g, j: (g, 0)),
                  pl.BlockSpec((1, nsaved, N_STATE, GROUP_W), lambda g, j: (g, nb - 1 - j, 0, 0)), hbm, hbm],
        out_specs=[wide, narrow, narrow,
                   pl.BlockSpec((None, tb, GROUP_W), lambda g, j: (Z_BLOCK0 // 2 + g // 2, nb - 1 - j, g % 2)),
                   narrow, pl.BlockSpec((1, GROUP_W), lambda g, j: (0, g)), small, small, small, hbm],
        input_output_aliases={14: 3},
        scratch_shapes=[pltpu.VMEM((N_STATE, GROUP_W), F32), pltpu.VMEM((tb, GROUP_W), F32)] + CHIP_SEMS,
        compiler_params=_cparams(("arbitrary", "arbitrary")),
    )(xc, xc, xc, proj, proj, alog4, bias4, dskip4, wnorm, expand, dyb, w_bb, states, part, dproj)


def lower_bound_fwd(hgrn_lb):
    def body(a_ref, o_ref):
        a0, a1 = a_ref[0:1, :], a_ref[1:2, :]
        m = jnp.maximum(a0, a1)
        e0, e1 = jnp.exp(a0 - m), jnp.exp(a1 - m)
        o_ref[...] = e0 / (e0 + e1)

    return pl.pallas_call(body, name="lower_bound_fwd", out_shape=jax.ShapeDtypeStruct((1, D), F32))(hgrn_lb)


def ada_weight_grad(c_all, dmod_cols):
    def body(c_ref, d_ref, o_ref):
        cval = c_ref[...]
        o_ref[...] = _tn(cval * _sigmoid(cval), d_ref[...], HI)

    return pl.pallas_call(body, name="ada_weight_grad",
                          out_shape=jax.ShapeDtypeStruct((D, dmod_cols.shape[1]), F32))(c_all, dmod_cols)


def reduce_small(gathered, hgrn_lb, dlb_off):
    n = gathered.shape[2]

    def body(g_ref, a_ref, o_ref, glb_ref):
        s = g_ref[0]
        for d in range(1, N_DEV):
            s = s + g_ref[d]
        o_ref[...] = s
        a0, a1 = a_ref[0:1, :], a_ref[1:2, :]
        m = jnp.maximum(a0, a1)
        e0, e1 = jnp.exp(a0 - m), jnp.exp(a1 - m)
        p0 = e0 / (e0 + e1)
        tq = s[:, dlb_off:dlb_off + D] * p0 * (1.0 - p0)
        glb_ref[0:1, :] = tq
        glb_ref[1:2, :] = -tq

    return pl.pallas_call(body, name="reduce_small",
                          out_shape=[jax.ShapeDtypeStruct((1, n), F32), jax.ShapeDtypeStruct((2, D), F32)])(gathered, hgrn_lb)


def _adam_math(w, g, m, v):
    m2 = ADAM_B1 * m + (1.0 - ADAM_B1) * g
    v2 = ADAM_B2 * v + (1.0 - ADAM_B2) * (g * g)
    m_hat = m2 / (1.0 - ADAM_B1 ** ADAM_STEP)
    v_hat = v2 / (1.0 - ADAM_B2 ** ADAM_STEP)
    delta = -ADAM_LR * (m_hat / (jnp.sqrt(v_hat) + ADAM_EPS) + ADAM_WD * w)
    return delta, m2, v2


def _row_tile(rows, mult=8, cap=128):
    for cand in range(cap - cap % mult, 0, -mult):
        if rows % cand == 0:
            return cand
    return rows


def sum_parts(parts, name):
    n, rows, cols = parts.shape
    tr = _row_tile(rows, 16, 1024)

    def body(p_ref, o_ref):
        s = p_ref[0].astype(F32)
        for d in range(1, n):
            s = s + p_ref[d].astype(F32)
        o_ref[...] = s

    return pl.pallas_call(
        body, name=name, grid=(rows // tr,),
        out_shape=jax.ShapeDtypeStruct((rows, cols), F32),
        in_specs=[pl.BlockSpec((n, tr, cols), lambda i: (0, i, 0))],
        out_specs=pl.BlockSpec((tr, cols), lambda i: (i, 0)),
        compiler_params=_cparams(("parallel",)),
    )(parts)


def sum_pair(a, b, name):
    rows, cols = a.shape
    tr = _row_tile(rows, 16, 1024)

    def body(a_ref, b_ref, o_ref):
        o_ref[...] = (a_ref[...].astype(F32) + b_ref[...].astype(F32)).astype(o_ref.dtype)

    blk = pl.BlockSpec((tr, cols), lambda i: (i, 0))
    return pl.pallas_call(
        body, name=name, grid=(rows // tr,),
        out_shape=jax.ShapeDtypeStruct((rows, cols), a.dtype),
        in_specs=[blk, blk], out_specs=blk,
        compiler_params=_cparams(("parallel",)),
    )(a, b)


def adamw(w, g, m, v, name):
    rows, cols = w.shape
    tr = _row_tile(rows, 8, 256)

    def body(w_ref, g_ref, m_ref, v_ref, d_ref, m2_ref, v2_ref):
        delta, m2, v2 = _adam_math(w_ref[...], g_ref[...], m_ref[...], v_ref[...])
        d_ref[...] = delta
        m2_ref[...] = m2
        v2_ref[...] = v2

    blk = pl.BlockSpec((tr, cols), lambda i: (i, 0))
    return pl.pallas_call(
        body, name=name, grid=(rows // tr,),
        out_shape=[jax.ShapeDtypeStruct((rows, cols), F32)] * 3,
        in_specs=[blk] * 4, out_specs=[blk] * 3,
        compiler_params=_cparams(("parallel",)),
    )(w, g, m, v)


def _pad128(n):
    return -(-n // 128) * 128


def _pack(arrays):
    offs, parts, off = [], [], 0
    for a in arrays:
        flat = a.reshape(1, -1)
        n = flat.shape[1]
        offs.append(off)
        parts.append(jnp.pad(flat, ((0, 0), (0, _pad128(n) - n))))
        off += _pad128(n)
    return jnp.concatenate(parts, axis=1), offs


def _unpack(vec, offs, shapes):
    out = []
    for off, shp in zip(offs, shapes):
        n = int(np.prod(shp))
        out.append(vec[0, off:off + n].reshape(shp))
    return out


IN_ROWS = IN_DIM // N_DEV
DT_ROW0 = 9216
DT_DEV, DT_LO = divmod(DT_ROW0, IN_ROWS)


GATE_SHIFT = D - 32


def _in_row_pieces(tile):
    pieces = []
    if tile == DT_COL_BLOCK:
        for g in range(N_GROUPS):
            o = DT_ROW0 + HEADS_PER_GROUP * g
            pieces.append((N_STATE * g, o // IN_ROWS, o % IN_ROWS, HEADS_PER_GROUP))
        return pieces
    r, end = tile * D, (tile + 1) * D
    while r < end:
        o = r if r < DT_ROW0 else r - GATE_SHIFT
        dev, loc = divmod(o, IN_ROWS)
        n = min(end - r, IN_ROWS - loc)
        pieces.append((r - tile * D, dev, loc, n))
        r += n
    return pieces


def assemble_w_in(g_all):
    ntile = N_PROJ // D

    def body(g_ref, o_ref):
        j = pl.program_id(0)
        for tile in range(ntile):
            @pl.when(j == tile)
            def _(tile=tile):
                if tile == DT_COL_BLOCK:
                    o_ref[...] = jnp.zeros_like(o_ref)
                for dst, dev, loc, n in _in_row_pieces(tile):
                    o_ref[pl.ds(dst, n), :] = g_ref[dev, pl.ds(loc, n), :]

    return pl.pallas_call(
        body, name="assemble_w_in", grid=(ntile,),
        out_shape=jax.ShapeDtypeStruct((N_PROJ, D), g_all.dtype),
        in_specs=[pl.BlockSpec(memory_space=pltpu.VMEM)],
        out_specs=pl.BlockSpec((D, D), lambda j: (j, 0)),
        compiler_params=_cparams(("arbitrary",)),
    )(g_all)


def _grad_in_blocks(g_t, core, slot):
    dt0 = DT_COL_BLOCK * D
    dt = g_t[dt0:dt0 + N_GROUPS * N_STATE].reshape(N_GROUPS, N_STATE, D)[:, :HEADS_PER_GROUP].reshape(32, D)
    with_dt = jnp.concatenate([g_t[DT_DEV * IN_ROWS:DT_ROW0], dt,
                               g_t[DT_ROW0 + 32 + GATE_SHIFT:(DT_DEV + 1) * IN_ROWS + GATE_SHIFT]], axis=0)
    blocks = []
    for q in range(N_CHIP):
        if 2 * q + 1 < DT_DEV:
            blk = lax.dynamic_slice_in_dim(g_t, IN_ROWS * (2 * q + core), IN_ROWS, axis=0)
        else:
            assert 2 * q == DT_DEV
            after = g_t[(DT_DEV + 1) * IN_ROWS + GATE_SHIFT:(DT_DEV + 2) * IN_ROWS + GATE_SHIFT]
            blk = jnp.where(core == 0, with_dt, after)
        blocks.append(jnp.pad(blk, ((0, slot - IN_ROWS), (0, 0))))
    return jnp.stack(blocks)


def kernel(x, c, w_ada, b_ada, w_in, hgrn_lb, hgrn_gnorm, ssm_conv_w, ssm_conv_b, ssm_dt_bias, ssm_a_log, ssm_d, ssm_norm, w_branch_a, w_branch_b, w_o, ln1_g, ln1_b, w_ffn_gate, w_ffn_up, w_ffn_down, ln2_g, ln2_b, loss_target, m_w_ada, m_b_ada, m_w_in, m_hgrn_lb, m_hgrn_gnorm, m_ssm_conv_w, m_ssm_conv_b, m_ssm_dt_bias, m_ssm_a_log, m_ssm_d, m_ssm_norm, m_w_branch_a, m_w_branch_b, m_w_o, m_ln1_g, m_ln1_b, m_w_ffn_gate, m_w_ffn_up, m_w_ffn_down, m_ln2_g, m_ln2_b, v_w_ada, v_b_ada, v_w_in, v_hgrn_lb, v_hgrn_gnorm, v_ssm_conv_w, v_ssm_conv_b, v_ssm_dt_bias, v_ssm_a_log, v_ssm_d, v_ssm_norm, v_w_branch_a, v_w_branch_b, v_w_o, v_ln1_g, v_ln1_b, v_w_ffn_gate, v_w_ffn_up, v_w_ffn_down, v_ln2_g, v_ln2_b):
    me = 4 * lax.axis_index("x") + 2 * lax.axis_index("y") + lax.axis_index("c")
    xt = x[0]
    tgt = loss_target[0]
    t = xt.shape[0]
    ada_cols = w_ada.shape[2]
    conv_cols = ssm_conv_w.shape[2]

    small_in, _ = _pack([c, ssm_conv_w[0]])
    small_all = allgather_vmem(small_in, "allgather_small_inputs")
    c_all = small_all[:, 0, :D]
    conv_w = small_all[:, 0, D:D + CONV_TAPS * conv_cols].reshape(N_DEV, CONV_TAPS, conv_cols)
    conv_w = conv_w.transpose(1, 0, 2).reshape(CONV_TAPS, CONV_DIM)
    mod = ada_modulation(c_all, w_ada[0], b_ada.reshape(N_DEV, 1, ada_cols))
    mod6 = mod.reshape(6, D)

    shards = [w_in[0].T, w_branch_a[0], w_branch_b[0], w_o[0], w_ffn_gate[0].T, w_ffn_up[0].T, w_ffn_down[0]]
    shard_rows = [s.shape[0] for s in shards]
    slot_rows = [-(-r // 32) * 32 for r in shard_rows]
    row_offs = [sum(slot_rows[:i]) for i in range(len(shards))]
    padded = [jnp.pad(s.astype(BF16), ((0, p - r), (0, 0))) for s, r, p in zip(shards, shard_rows, slot_rows)]
    w_in_t = assemble_w_in(allgather_hbm(padded[0], "allgather_w_in"))

    lb = lower_bound_fwd(hgrn_lb)
    u1 = ln_modulate(xt, mod6, 0, 1, "ln_modulate_1")
    proj, g_rest = mm_nt_gather(u1, w_in_t, F32, jnp.concatenate(padded[1:], axis=0), "mm_in_proj")
    g_ba, g_bb, g_o, g_fg, g_fu, g_fd = (g_rest[:, o - slot_rows[0]:o - slot_rows[0] + r]
                                         for o, r in zip(row_offs[1:], shard_rows[1:]))
    w_ba = g_ba.reshape(D, D)
    w_bb = g_bb.reshape(B_INNER, D)
    w_oo = g_o.reshape(D, D)
    w_gu_t = jnp.concatenate([g_fg.reshape(D_FF, D), g_fu.reshape(D_FF, D)], axis=0)
    w_dn = g_fd.reshape(D_FF, D)
    o_a, o_raw, st_a = hgrn_fwd(proj, lb, hgrn_gnorm)
    xc, conv_slope = conv_fwd(proj, conv_w, ssm_conv_b)
    pad3 = ((0, 0), (0, 0), (0, N_STATE - HEADS_PER_GROUP))
    alog4 = jnp.pad(ssm_a_log.reshape(N_GROUPS, 1, HEADS_PER_GROUP), pad3)
    bias4 = jnp.pad(ssm_dt_bias.reshape(N_GROUPS, 1, HEADS_PER_GROUP), pad3)
    dskip4 = jnp.pad(ssm_d.reshape(N_GROUPS, 1, HEADS_PER_GROUP), pad3)
    expand = _head_expand()
    o_b, st_b = ssd_fwd(proj, xc, alog4, bias4, dskip4, ssm_norm, expand)
    ya = mm_nn(o_a, w_ba, BF16, "mm_branch_a")
    yb = mm_nn(o_b, w_bb, BF16, "mm_branch_b")
    merged = merge_gates(ya, yb, proj)
    h1 = mm_nn(merged, w_oo, F32, "mm_out_proj")
    x1 = resid_ln(xt, h1, mod6, 2, ln1_g, ln1_b, "resid_ln_1")
    u2 = ln_modulate(x1, mod6, 3, 4, "ln_modulate_2")
    gu, act = ffn_in_act(u2, w_gu_t)
    h2 = mm_nn(act, w_dn, F32, "mm_ffn_out")

    dh2, dx1_part, acc4 = resid_ln_bwd(x1, h2, mod6, 5, ln2_g, ln2_b, tgt, True, "resid_ln_2_bwd")
    g_dn = mm_tn(act, dh2, "mm_grad_ffn_down")
    dgu = ffn_act_bwd(dh2, w_dn, gu)
    g_gu_t = mm_tn(dgu, u2, "mm_grad_ffn_in")
    du2 = mm_nn(dgu, w_gu_t, F32, "mm_du2")
    dx1, acc3 = ln_modulate_bwd(x1, du2, mod6, 4, dx1_part, "ln_modulate_2_bwd")
    dh1, dx_part, acc2 = resid_ln_bwd(xt, h1, mod6, 2, ln1_g, ln1_b, dx1, False, "resid_ln_1_bwd")
    g_o = mm_tn(merged, dh1, "mm_grad_out_proj")
    dya, dyb, dproj = merge_gates_bwd(dh1, w_oo, ya, yb, proj)
    g_ba_full = mm_tn(o_a, dya, "mm_grad_branch_a")
    g_bb_full = mm_tn(o_b, dyb, "mm_grad_branch_b")
    my_core = lax.axis_index("c")

    def by_core(blocks, rows, slots):
        contrib = jnp.concatenate([jnp.pad(b.reshape(N_DEV, -1, D), ((0, 0), (0, p - r), (0, 0)))
                                   for b, r, p in zip(blocks, rows, slots)], axis=1)
        split = contrib.reshape(N_CHIP, 2, contrib.shape[1], D).transpose(1, 0, 2, 3)
        return (lax.dynamic_index_in_dim(split, my_core, 0, keepdims=False),
                lax.dynamic_index_in_dim(split, 1 - my_core, 0, keepdims=False))

    keep_e, give_e = by_core([g_ba_full, g_bb_full, g_o, g_gu_t[:D_FF], g_gu_t[D_FF:], g_dn],
                             shard_rows[1:], slot_rows[1:])
    dproj, dlb, dgn, got_e = hgrn_bwd(proj, lb, hgrn_gnorm, o_raw, dya, w_ba, st_a, give_e, dproj)
    chip_e = sum_pair(keep_e.reshape(-1, D), got_e.reshape(-1, D), "sum_grads_rest_chip").reshape(keep_e.shape)
    dxs, dbm, dcm, dproj, ddt, dwn, dalog, dbias, ddsk, parts_e = ssd_bwd(proj, xc, alog4, bias4, dskip4, ssm_norm,
                                                                          expand, dyb, w_bb, st_b, chip_e, dproj)
    dxc = jnp.concatenate([dxs, dbm, dcm], axis=1)
    dproj, dcw, dcb = conv_bwd(proj, dxc, conv_slope, conv_w, dproj)
    dproj = dt_fill(ddt, dproj)
    g_in_t = mm_tn(dproj, u1, "mm_grad_in_proj")
    keep_l = _grad_in_blocks(g_in_t, my_core, slot_rows[0])
    give_l = _grad_in_blocks(g_in_t, 1 - my_core, slot_rows[0])
    got_l = exchange_sibling(give_l, "exchange_grad_in_sibling")
    chip_l = sum_pair(keep_l.reshape(-1, D), got_l.reshape(-1, D), "sum_grad_in_chip").reshape(keep_l.shape)
    du1, parts_l = mm_nn_exchange(dproj, w_in_t, F32, chip_l, "mm_du1")
    dx, acc1 = ln_modulate_bwd(xt, du1, mod6, 1, dx_part, "ln_modulate_1_bwd")
    gw_in = sum_parts(parts_l, "sum_grad_in")[:shard_rows[0]].T
    g_rows = sum_parts(parts_e, "sum_grads_rest")
    gw_ba, gw_bb, gw_o, gw_fg, gw_fu, gw_fd = (g_rows[o - slot_rows[0]:o - slot_rows[0] + r]
                                               for o, r in zip(row_offs[1:], shard_rows[1:]))
    gw_fg, gw_fu = gw_fg.T, gw_fu.T

    dmod = jnp.concatenate([acc1[1:2], acc1[0:1], acc2[0:1], acc3[1:2], acc3[0:1], acc4[0:1]], axis=1)
    small_fields = [dmod, acc4[3:4, :128], dlb, dgn, dcw[:CONV_TAPS], dcb, dbias, dalog, ddsk, dwn,
                    acc2[1:2], acc2[2:3], acc4[1:2], acc4[2:3]]
    small_out, offs = _pack(small_fields)
    small_sum_in = allgather_vmem(small_out, "allgather_small_grads")
    gsum, g_lb = reduce_small(small_sum_in, hgrn_lb, offs[2])
    (g_bada, loss_row, _, g_gn, g_cw_full, g_cb, g_bias4, g_alog4, g_dsk4, g_wn, g_l1g, g_l1b, g_l2g, g_l2b) = _unpack(
        gsum, offs, [(1, 6 * D), (1, 128), (1, D), (1, HK), (CONV_TAPS, CONV_DIM), (1, CONV_DIM),
                     (N_GROUPS, N_STATE), (N_GROUPS, N_STATE), (N_GROUPS, N_STATE), (1, B_INNER),
                     (1, D), (1, D), (1, D), (1, D)])
    loss = loss_row[0, 0]
    g_cw = lax.dynamic_slice(g_cw_full, (0, me * conv_cols), (CONV_TAPS, conv_cols))[None]
    g_dtb = g_bias4[:, :HEADS_PER_GROUP].reshape(1, 32)
    g_alog = g_alog4[:, :HEADS_PER_GROUP].reshape(1, 32)
    g_dsk = g_dsk4[:, :HEADS_PER_GROUP].reshape(1, 32)

    dmod_all = small_sum_in[:, 0, offs[0]:offs[0] + 6 * D]
    dmod_cols = lax.dynamic_slice(dmod_all, (0, me * ada_cols), (N_DEV, ada_cols))
    gw_ada = ada_weight_grad(c_all, dmod_cols)

    big = [("ada", w_ada[0], gw_ada, m_w_ada[0], v_w_ada[0]), ("in", w_in[0], gw_in, m_w_in[0], v_w_in[0]),
           ("branch_a", w_branch_a[0], gw_ba, m_w_branch_a[0], v_w_branch_a[0]),
           ("branch_b", w_branch_b[0], gw_bb, m_w_branch_b[0], v_w_branch_b[0]),
           ("o", w_o[0], gw_o, m_w_o[0], v_w_o[0]),
           ("ffn_gate", w_ffn_gate[0], gw_fg, m_w_ffn_gate[0], v_w_ffn_gate[0]),
           ("ffn_up", w_ffn_up[0], gw_fu, m_w_ffn_up[0], v_w_ffn_up[0]),
           ("ffn_down", w_ffn_down[0], gw_fd, m_w_ffn_down[0], v_w_ffn_down[0])]
    big_out = {}
    for nm, w_, g_, m_, v_ in big:
        d_, m2_, v2_ = adamw(w_, g_, m_, v_, "adamw_" + nm)
        big_out[nm] = (g_[None], d_[None], m2_[None], v2_[None])

    small_w = [b_ada, hgrn_lb, hgrn_gnorm, ssm_conv_w, ssm_conv_b, ssm_dt_bias, ssm_a_log, ssm_d, ssm_norm,
               ln1_g, ln1_b, ln2_g, ln2_b]
    small_g = [g_bada, g_lb, g_gn, g_cw, g_cb, g_dtb, g_alog, g_dsk, g_wn, g_l1g, g_l1b, g_l2g, g_l2b]
    small_m = [m_b_ada, m_hgrn_lb, m_hgrn_gnorm, m_ssm_conv_w, m_ssm_conv_b, m_ssm_dt_bias, m_ssm_a_log, m_ssm_d,
               m_ssm_norm, m_ln1_g, m_ln1_b, m_ln2_g, m_ln2_b]
    small_v = [v_b_ada, v_hgrn_lb, v_hgrn_gnorm, v_ssm_conv_w, v_ssm_conv_b, v_ssm_dt_bias, v_ssm_a_log, v_ssm_d,
               v_ssm_norm, v_ln1_g, v_ln1_b, v_ln2_g, v_ln2_b]
    shapes = [a.shape for a in small_w]
    small_g = [g_.reshape(s) for g_, s in zip(small_g, shapes)]
    pw, poffs = _pack(small_w)
    pg, _ = _pack(small_g)
    pm, _ = _pack(small_m)
    pv, _ = _pack(small_v)
    pd, pm2, pv2 = adamw(pw, pg, pm, pv, "adamw_small")
    s_d, s_m, s_v = (_unpack(p, poffs, shapes) for p in (pd, pm2, pv2))
    (sn_bada, sn_lb, sn_gn, sn_cw, sn_cb, sn_dtb, sn_alog, sn_dsk, sn_wn, sn_l1g, sn_l1b, sn_l2g, sn_l2b) = range(13)

    def order(kind):
        sm = [small_g, s_d, s_m, s_v][kind]
        bg = lambda nm: big_out[nm][kind]
        return [bg("ada"), sm[sn_bada], bg("in"), sm[sn_lb], sm[sn_gn], sm[sn_cw], sm[sn_cb], sm[sn_dtb], sm[sn_alog],
                sm[sn_dsk], sm[sn_wn], bg("branch_a"), bg("branch_b"), bg("o"), sm[sn_l1g], sm[sn_l1b],
                bg("ffn_gate"), bg("ffn_up"), bg("ffn_down"), sm[sn_l2g], sm[sn_l2b]]

    return (loss, dx[None], *order(0), *order(1), *order(2), *order(3))
```

```python
import numpy as np
import jax
import jax.numpy as jnp
from jax import lax
from jax.experimental import pallas as pl
from jax.experimental.pallas import tpu as pltpu

F32 = jnp.float32
BF16 = jnp.bfloat16
HI = lax.Precision.HIGHEST

N_DEV = 8
D = 1024
N_HEADS_A = 8
HK = 128
CHUNK = 64
SSD_CHUNK = 128
SSD_CHUNK_BWD = 256
N_GROUPS = 4
HEADS_PER_GROUP = 8
HEAD_P = 64
N_STATE = 128
GROUP_W = HEADS_PER_GROUP * HEAD_P
B_INNER = 2048
CONV_DIM = 3072
D_FF = 2816
IN_DIM = 11296
N_PROJ = 12288
ALPHA = 2.0 ** 0.25
LN_EPS = 1e-5
RMS_EPS = 1e-6
Q_SCALE = 128 ** -0.5
EXP_CLIP = 80.0
ADAM_LR, ADAM_B1, ADAM_B2, ADAM_EPS, ADAM_WD, ADAM_STEP = 0.001, 0.9, 0.999, 1e-8, 0.01, 10
VMEM_LIMIT = 48 * 1024 * 1024
TOKEN_BLOCK = 1024
ROW_TILE = 512
WIDE_ROW_TILE = 1024
MM_ROW_TILE = 1024
MM_TOKEN_TILE = 4096
MM_K_TILE = 3072
MM_COL_TILE = 1408
HGRN_HEADS_PER_STEP = 4
CHUNK_UNROLL = 8
MESH_ID = pl.DeviceIdType.MESH

NT_DIMS = (((1,), (1,)), ((), ()))
TN_DIMS = (((0,), (0,)), ((), ()))


def _cparams(sem=None):
    return pltpu.CompilerParams(dimension_semantics=sem, vmem_limit_bytes=VMEM_LIMIT)


def _sigmoid(x):
    return 1.0 / (1.0 + jnp.exp(-x))


def _dsilu(x, s):
    return s * (1.0 + x * (1.0 - s))


def _nt(a, b, precision=None):
    return lax.dot_general(a, b, NT_DIMS, precision=precision, preferred_element_type=F32)


def _tn(a, b, precision=None):
    return lax.dot_general(a, b, TN_DIMS, precision=precision, preferred_element_type=F32)


def _nn(a, b, precision=None):
    return jnp.dot(a, b, precision=precision, preferred_element_type=F32)


def _split(x, pieces):
    out = []
    for i in range(pieces):
        p = x.astype(BF16)
        out.append(p)
        if i + 1 < pieces:
            x = x - p.astype(F32)
    return out


def _sel(dot, x, sel01, pieces, x_first=True):
    acc = None
    for p in _split(x, pieces):
        term = dot(p, sel01) if x_first else dot(sel01, p)
        acc = term if acc is None else acc + term
    return acc


def _ln(x):
    mu = jnp.mean(x, axis=-1, keepdims=True)
    xc = x - mu
    rstd = lax.rsqrt(jnp.mean(xc * xc, axis=-1, keepdims=True) + LN_EPS)
    return xc * rstd, rstd


def _ln_bwd(dxh, xh, rstd):
    return rstd * (dxh - jnp.mean(dxh, axis=-1, keepdims=True) - xh * jnp.mean(dxh * xh, axis=-1, keepdims=True))


def _colsum(x):
    return jnp.sum(x, axis=0, keepdims=True)


def _tri(n, upper=False):
    r = lax.broadcasted_iota(jnp.int32, (n, n), 0)
    c = lax.broadcasted_iota(jnp.int32, (n, n), 1)
    return (c >= r) if upper else (r >= c)


def _my_pos():
    return lax.axis_index("x"), lax.axis_index("y"), lax.axis_index("c")


def _peer(pos, k):
    x, y, c = pos
    return (x ^ ((k >> 2) & 1), y ^ ((k >> 1) & 1), c ^ (k & 1))


def _flat(pos):
    return 4 * pos[0] + 2 * pos[1] + pos[2]


def allgather_vmem(v, name):
    n = v.shape[1]

    def body(v_ref, o_ref, send_sems, recv_sems, local_sem):
        me = _my_pos()
        mine = pltpu.make_async_copy(v_ref, o_ref.at[_flat(me)], local_sem)
        mine.start()
        sends = []
        for k in range(1, N_DEV):
            peer = _peer(me, k)
            cp = pltpu.make_async_remote_copy(v_ref, o_ref.at[_flat(me)], send_sems.at[k - 1], recv_sems.at[k - 1],
                                              device_id=peer, device_id_type=MESH_ID)
            cp.start()
            sends.append(cp)
        for k in range(1, N_DEV):
            peer = _peer(me, k)
            pltpu.make_async_remote_copy(v_ref, o_ref.at[_flat(peer)], send_sems.at[k - 1], recv_sems.at[k - 1],
                                         device_id=peer, device_id_type=MESH_ID).wait_recv()
        for cp in sends:
            cp.wait_send()
        mine.wait()

    return pl.pallas_call(
        body, name=name,
        out_shape=jax.ShapeDtypeStruct((N_DEV, 1, n), F32),
        in_specs=[pl.BlockSpec(memory_space=pltpu.VMEM)],
        out_specs=pl.BlockSpec(memory_space=pltpu.VMEM),
        scratch_shapes=[pltpu.SemaphoreType.DMA((N_DEV - 1,)), pltpu.SemaphoreType.DMA((N_DEV - 1,)),
                        pltpu.SemaphoreType.DMA],
        compiler_params=_cparams(),
    )(v)


def ada_modulation(c_all, w_ada_s, b_ada_r):
    ncol = w_ada_s.shape[1]

    def body(c_ref, w_ref, b_ref, o_ref, part_ref, send_sems, recv_sems):
        me = _my_pos()
        cval = c_ref[...]
        cond = cval * _sigmoid(cval)
        part = _nn(cond, w_ref[...], HI)
        for r in range(N_DEV):
            part_ref[r] = part[r:r + 1, :]
        sends = []
        for k in range(1, N_DEV):
            peer = _peer(me, k)
            cp = pltpu.make_async_remote_copy(part_ref.at[_flat(peer)], o_ref.at[_flat(me)], send_sems.at[k - 1],
                                              recv_sems.at[k - 1], device_id=peer, device_id_type=MESH_ID)
            cp.start()
            sends.append(cp)
        o_ref[_flat(me)] = part_ref[_flat(me)]
        for k in range(1, N_DEV):
            peer = _peer(me, k)
            pltpu.make_async_remote_copy(part_ref.at[_flat(peer)], o_ref.at[_flat(peer)], send_sems.at[k - 1],
                                         recv_sems.at[k - 1], device_id=peer, device_id_type=MESH_ID).wait_recv()
        for cp in sends:
            cp.wait_send()
        o_ref[...] = o_ref[...] + b_ref[...]

    return pl.pallas_call(
        body, name="ada_modulation",
        out_shape=jax.ShapeDtypeStruct((N_DEV, 1, ncol), F32),
        in_specs=[pl.BlockSpec(memory_space=pltpu.VMEM)] * 3,
        out_specs=pl.BlockSpec(memory_space=pltpu.VMEM),
        scratch_shapes=[pltpu.VMEM((N_DEV, 1, ncol), F32), pltpu.SemaphoreType.DMA((N_DEV - 1,)),
                        pltpu.SemaphoreType.DMA((N_DEV - 1,))],
        compiler_params=_cparams(),
    )(c_all, w_ada_s, b_ada_r)


def allgather_hbm(shard, name):
    def body(x_ref, out_ref, send_sems, recv_sems, local_sem):
        x, y, c = _my_pos()
        me, sibling = (x, y, c), (x, y, 1 - c)
        chips = [(1 - x, y), (x, 1 - y), (1 - x, 1 - y)]

        def slot(pos):
            return out_ref.at[_flat(pos)]

        def copy(k, block, to, src=None):
            return pltpu.make_async_remote_copy(slot(block) if src is None else src, slot(block), send_sems.at[k],
                                                recv_sems.at[k], device_id=to, device_id_type=MESH_ID)

        mine = pltpu.make_async_copy(x_ref, slot(me), local_sem)
        mine.start()
        first = [copy(0, me, sibling, src=x_ref)]
        first += [copy(1 + j, me, (*chip, c), src=x_ref) for j, chip in enumerate(chips)]
        for cp in first:
            cp.start()
        passed = [copy(4 + j, (*chip, c), sibling) for j, chip in enumerate(chips)]
        for j, chip in enumerate(chips):
            copy(1 + j, (*chip, c), me).wait_recv()
            passed[j].start()
        copy(0, sibling, me).wait_recv()
        for j, chip in enumerate(chips):
            copy(4 + j, (*chip, 1 - c), me).wait_recv()
        for cp in first + passed:
            cp.wait_send()
        mine.wait()

    return pl.pallas_call(
        body, name=name,
        out_shape=jax.ShapeDtypeStruct((N_DEV,) + shard.shape, shard.dtype),
        in_specs=[pl.BlockSpec(memory_space=pl.ANY)],
        out_specs=pl.BlockSpec(memory_space=pl.ANY),
        scratch_shapes=[pltpu.SemaphoreType.DMA((N_DEV - 1,)), pltpu.SemaphoreType.DMA((N_DEV - 1,)),
                        pltpu.SemaphoreType.DMA],
        compiler_params=_cparams(),
    )(shard)


N_CHIP = N_DEV // 2
SIBLING_SEMS = [pltpu.SemaphoreType.DMA, pltpu.SemaphoreType.DMA]
CHIP_SEMS = [pltpu.SemaphoreType.DMA((N_CHIP - 1,)), pltpu.SemaphoreType.DMA((N_CHIP - 1,)), pltpu.SemaphoreType.DMA]


def _sibling_exchange(s_ref, o_ref, send_sem, recv_sem):
    x, y, c = _my_pos()
    cp = pltpu.make_async_remote_copy(s_ref, o_ref, send_sem, recv_sem, device_id=(x, y, 1 - c), device_id_type=MESH_ID)
    return cp.start, cp.wait


def _chip_exchange(p_ref, o_ref, send_sems, recv_sems, local_sem):
    x, y, c = _my_pos()
    my_chip = 2 * x + y
    mine = pltpu.make_async_copy(p_ref.at[my_chip], o_ref.at[my_chip], local_sem)
    peers = [(x ^ (k >> 1), y ^ (k & 1)) for k in range(1, N_CHIP)]
    sends = [pltpu.make_async_remote_copy(p_ref.at[2 * px + py], o_ref.at[my_chip], send_sems.at[k], recv_sems.at[k],
                                          device_id=(px, py, c), device_id_type=MESH_ID)
             for k, (px, py) in enumerate(peers)]
    recvs = [pltpu.make_async_remote_copy(p_ref.at[2 * px + py], o_ref.at[2 * px + py], send_sems.at[k], recv_sems.at[k],
                                          device_id=(px, py, c), device_id_type=MESH_ID)
             for k, (px, py) in enumerate(peers)]

    def start():
        mine.start()
        for cp in sends:
            cp.start()

    def wait():
        for cp in recvs:
            cp.wait_recv()
        for cp in sends:
            cp.wait_send()
        mine.wait()

    return start, wait


def exchange_sibling(send, name):
    def body(s_ref, o_ref, send_sem, recv_sem):
        start, wait = _sibling_exchange(s_ref, o_ref, send_sem, recv_sem)
        start()
        wait()

    return pl.pallas_call(
        body, name=name,
        out_shape=jax.ShapeDtypeStruct(send.shape, send.dtype),
        in_specs=[pl.BlockSpec(memory_space=pl.ANY)],
        out_specs=pl.BlockSpec(memory_space=pl.ANY),
        scratch_shapes=SIBLING_SEMS,
        compiler_params=_cparams(),
    )(send)


LANES = 128


def _k_tile(kdim, unit=LANES):
    for cand in range(MM_K_TILE - MM_K_TILE % unit, 0, -unit):
        if kdim % cand == 0:
            return cand
    return kdim


def _lane_tile(n, cap):
    for cand in range(cap - cap % LANES, 0, -LANES):
        if n % cand == 0:
            return cand
    return n


def _m_tile(m, kdim):
    return min(MM_ROW_TILE if kdim > D else 2 * MM_ROW_TILE, m)


def _mm(a, b, out_dtype, name, b_is_nk):
    if a.ndim == 3:
        m, tk = a.shape[1], a.shape[2]
        kdim = a.shape[0] * tk
        a_spec = pl.BlockSpec((None, _m_tile(m, kdim), tk), lambda j, i, k: (k, i, 0))
    else:
        m, kdim = a.shape
        tk = _k_tile(kdim)
        a_spec = pl.BlockSpec((_m_tile(m, kdim), tk), lambda j, i, k: (i, k))
    n = b.shape[0] if b_is_nk else b.shape[1]
    tm, tn = _m_tile(m, kdim), _lane_tile(n, MM_COL_TILE)
    nk = kdim // tk
    dot = _nt if b_is_nk else _nn

    def body(a_ref, b_ref, o_ref, *acc):
        p = dot(a_ref[...], b_ref[...])
        if nk == 1:
            o_ref[...] = p.astype(o_ref.dtype)
        else:
            acc_ref, k = acc[0], pl.program_id(2)

            @pl.when(k == 0)
            def _():
                acc_ref[...] = p

            @pl.when(k > 0)
            def _():
                acc_ref[...] += p

            @pl.when(k == nk - 1)
            def _():
                o_ref[...] = acc_ref[...].astype(o_ref.dtype)

    b_spec = (pl.BlockSpec((tn, tk), lambda j, i, k: (j, k)) if b_is_nk else
              pl.BlockSpec((tk, tn), lambda j, i, k: (k, j)))
    return pl.pallas_call(
        body, name=name, grid=(n // tn, m // tm, nk),
        out_shape=jax.ShapeDtypeStruct((m, n), out_dtype),
        in_specs=[a_spec, b_spec],
        out_specs=pl.BlockSpec((tm, tn), lambda j, i, k: (i, j)),
        scratch_shapes=[] if nk == 1 else [pltpu.VMEM((tm, tn), F32)],
        compiler_params=_cparams(("parallel", "parallel", "arbitrary")),
    )(a, b)


def mm_nn(a, b, out_dtype, name):
    return _mm(a, b, out_dtype, name, False)


def mm_nt(a, b, out_dtype, name):
    return _mm(a, b, out_dtype, name, True)


def mm_nn_exchange(a, b, out_dtype, part, name):
    kblocks, m, kb = a.shape
    kdim = kblocks * kb
    n = b.shape[1]
    tm, tn, tk = min(MM_ROW_TILE, m), _lane_tile(n, MM_COL_TILE), _k_tile(kdim)
    gn, gm, nk = n // tn, m // tm, kdim // tk
    per_step = tk // kb

    def body(a_ref, b_ref, part_ref, o_ref, parts_ref, acc_ref, send_sems, recv_sems, local_sem):
        j, i, k = pl.program_id(0), pl.program_id(1), pl.program_id(2)
        xchg_start, xchg_wait = _chip_exchange(part_ref, parts_ref, send_sems, recv_sems, local_sem)

        @pl.when((j == 0) & (i == 0) & (k == 0))
        def _():
            xchg_start()

        p = _nn(a_ref[0], b_ref[0:kb, :])
        for c in range(1, per_step):
            p = p + _nn(a_ref[c], b_ref[c * kb:(c + 1) * kb, :])

        @pl.when(k == 0)
        def _():
            acc_ref[...] = p

        @pl.when(k > 0)
        def _():
            acc_ref[...] += p

        @pl.when(k == nk - 1)
        def _():
            o_ref[...] = acc_ref[...].astype(o_ref.dtype)

        @pl.when((j == gn - 1) & (i == gm - 1) & (k == nk - 1))
        def _():
            xchg_wait()

    hbm = pl.BlockSpec(memory_space=pl.ANY)
    return pl.pallas_call(
        body, name=name, grid=(gn, gm, nk),
        out_shape=[jax.ShapeDtypeStruct((m, n), out_dtype), jax.ShapeDtypeStruct(part.shape, part.dtype)],
        in_specs=[pl.BlockSpec((per_step, tm, kb), lambda j, i, k: (k, i, 0)),
                  pl.BlockSpec((tk, tn), lambda j, i, k: (k, j)), hbm],
        out_specs=[pl.BlockSpec((tm, tn), lambda j, i, k: (i, j)), hbm],
        scratch_shapes=[pltpu.VMEM((tm, tn), F32)] + CHIP_SEMS,
        compiler_params=_cparams(("arbitrary", "arbitrary", "arbitrary")),
    )(a, b, part)


def mm_nt_gather(a, b, out_dtype, shard, name):
    m, kdim = a.shape
    n = b.shape[0]
    tm, tn = _m_tile(m, kdim), 1024
    assert kdim == 1024
    gj = m // tm
    nsteps = (n // tn) * gj
    forward_step = max(nsteps - 2, 0)

    def body(a_ref, b_ref, x_ref, o_ref, g_ref, send_sems, recv_sems, local_sem):
        step = pl.program_id(0) * gj + pl.program_id(1)
        x, y, c = _my_pos()
        me, sibling = (x, y, c), (x, y, 1 - c)
        chips = [(1 - x, y), (x, 1 - y), (1 - x, 1 - y)]

        def slot(pos):
            return g_ref.at[_flat(pos)]

        def copy(k, block, to, src=None):
            return pltpu.make_async_remote_copy(slot(block) if src is None else src, slot(block), send_sems.at[k],
                                                recv_sems.at[k], device_id=to, device_id_type=MESH_ID)

        mine = pltpu.make_async_copy(x_ref, slot(me), local_sem)
        first = [copy(0, me, sibling, src=x_ref)]
        first += [copy(1 + j, me, (*chip, c), src=x_ref) for j, chip in enumerate(chips)]
        passed = [copy(4 + j, (*chip, c), sibling) for j, chip in enumerate(chips)]

        @pl.when(step == 0)
        def _():
            mine.start()
            for cp in first:
                cp.start()

        rows = pl.ds(pl.multiple_of(pl.program_id(1) * tm, tm), tm)
        o_ref[...] = _nt(a_ref[rows, :], b_ref[...]).astype(o_ref.dtype)

        @pl.when(step == forward_step)
        def _():
            for j, chip in enumerate(chips):
                copy(1 + j, (*chip, c), me).wait_recv()
                passed[j].start()

        @pl.when(step == nsteps - 1)
        def _():
            copy(0, sibling, me).wait_recv()
            for j, chip in enumerate(chips):
                copy(4 + j, (*chip, 1 - c), me).wait_recv()
            for cp in first + passed:
                cp.wait_send()
            mine.wait()

    return pl.pallas_call(
        body, name=name, grid=(n // tn, gj),
        out_shape=[jax.ShapeDtypeStruct((m, n), out_dtype), jax.ShapeDtypeStruct((N_DEV,) + shard.shape, shard.dtype)],
        in_specs=[pl.BlockSpec(memory_space=pltpu.VMEM), pl.BlockSpec((tn, kdim), lambda j, i: (j, 0)),
                  pl.BlockSpec(memory_space=pl.ANY)],
        out_specs=[pl.BlockSpec((tm, tn), lambda j, i: (i, j)), pl.BlockSpec(memory_space=pl.ANY)],
        scratch_shapes=[pltpu.SemaphoreType.DMA((N_DEV - 1,)), pltpu.SemaphoreType.DMA((N_DEV - 1,)),
                        pltpu.SemaphoreType.DMA],
        compiler_params=_cparams(("arbitrary", "arbitrary")),
    )(a, b, shard)


def mm_tn(a, b, name):
    tt, tn = min(MM_TOKEN_TILE, b.shape[0]), _lane_tile(b.shape[1], MM_COL_TILE)
    tka = _lane_tile(a.shape[-1], 1024)
    if a.ndim == 3:
        t, ka = a.shape[1], a.shape[0] * a.shape[2]
        per = a.shape[2] // tka
        a_spec = pl.BlockSpec((None, tt, tka), lambda i, j, s: (i // per, s, i % per))
    else:
        t, ka = a.shape
        a_spec = pl.BlockSpec((tt, tka), lambda i, j, s: (s, i))
    n = b.shape[1]
    nt = t // tt

    def body(a_ref, b_ref, o_ref, *acc):
        p = _tn(a_ref[...], b_ref[...])
        if nt == 1:
            o_ref[...] = p.astype(o_ref.dtype)
        else:
            acc_ref, s = acc[0], pl.program_id(2)

            @pl.when(s == 0)
            def _():
                acc_ref[...] = p

            @pl.when(s > 0)
            def _():
                acc_ref[...] += p

            @pl.when(s == nt - 1)
            def _():
                o_ref[...] = acc_ref[...].astype(o_ref.dtype)

    return pl.pallas_call(
        body, name=name, grid=(ka // tka, n // tn, nt),
        out_shape=jax.ShapeDtypeStruct((ka, n), BF16),
        in_specs=[a_spec, pl.BlockSpec((tt, tn), lambda i, j, s: (s, j))],
        out_specs=pl.BlockSpec((tka, tn), lambda i, j, s: (i, j)),
        scratch_shapes=[] if nt == 1 else [pltpu.VMEM((tka, tn), F32)],
        compiler_params=_cparams(("parallel", "parallel", "arbitrary")),
    )(a, b)


def _tile(t, cap):
    return min(cap, t)


def ln_modulate(x, mod6, shift_row, scale_row, name):
    t = x.shape[0]
    tm = _tile(t, WIDE_ROW_TILE)

    def body(x_ref, mod_ref, o_ref):
        xh, _ = _ln(x_ref[...])
        sc = mod_ref[scale_row:scale_row + 1, :]
        sh = mod_ref[shift_row:shift_row + 1, :]
        o_ref[...] = (xh * (1.0 + sc) + sh).astype(BF16)

    return pl.pallas_call(
        body, name=name, grid=(t // tm,),
        out_shape=jax.ShapeDtypeStruct((t, D), BF16),
        in_specs=[pl.BlockSpec((tm, D), lambda i: (i, 0)), pl.BlockSpec((6, D), lambda i: (0, 0))],
        out_specs=pl.BlockSpec((tm, D), lambda i: (i, 0)),
        compiler_params=_cparams(("parallel",)),
    )(x, mod6)


def resid_ln(x, h, mod6, gate_row, ln_g, ln_b, name):
    t = x.shape[0]
    tm = _tile(t, WIDE_ROW_TILE)

    def body(x_ref, h_ref, mod_ref, g_ref, b_ref, o_ref):
        r = ALPHA * x_ref[...] + mod_ref[gate_row:gate_row + 1, :] * h_ref[...]
        rh, _ = _ln(r)
        o_ref[...] = rh * g_ref[...] + b_ref[...]

    row = pl.BlockSpec((tm, D), lambda i: (i, 0))
    vec = pl.BlockSpec((1, D), lambda i: (0, 0))
    return pl.pallas_call(
        body, name=name, grid=(t // tm,),
        out_shape=jax.ShapeDtypeStruct((t, D), F32),
        in_specs=[row, row, pl.BlockSpec((6, D), lambda i: (0, 0)), vec, vec],
        out_specs=row,
        compiler_params=_cparams(("parallel",)),
    )(x, h, mod6, ln_g, ln_b)


def resid_ln_bwd(x, h, mod6, gate_row, ln_g, ln_b, cot, with_loss, name):
    t = x.shape[0]
    tm = _tile(t, ROW_TILE)
    h_is_product = isinstance(h, tuple)

    def body(x_ref, *refs):
        if h_is_product:
            a_ref, w_ref, mod_ref, g_ref, b_ref, c_ref, dh_ref, dx_ref, acc_ref = refs
            hv = _nn(a_ref[...], w_ref[...])
        else:
            h_ref, mod_ref, g_ref, b_ref, c_ref, dh_ref, dx_ref, acc_ref = refs
            hv = h_ref[...]

        @pl.when(pl.program_id(0) == 0)
        def _():
            acc_ref[...] = jnp.zeros_like(acc_ref)

        gate = mod_ref[gate_row:gate_row + 1, :]
        r = ALPHA * x_ref[...] + gate * hv
        rh, rstd = _ln(r)
        lng = g_ref[...]
        if with_loss:
            diff = rh * lng + b_ref[...] - c_ref[...]
            dxo = diff * (1.0 / D)
            lsum = jnp.sum(_colsum(diff * diff), axis=-1, keepdims=True) * (0.5 / D)
            acc_ref[3:4, :] += jnp.broadcast_to(lsum, (1, D))
        else:
            dxo = c_ref[...]
        acc_ref[1:2, :] += _colsum(dxo * rh)
        acc_ref[2:3, :] += _colsum(dxo)
        dr = _ln_bwd(dxo * lng, rh, rstd)
        acc_ref[0:1, :] += _colsum(dr * hv)
        dh_ref[...] = (gate * dr).astype(BF16)
        dx_ref[...] = ALPHA * dr

    row = pl.BlockSpec((tm, D), lambda i: (i, 0))
    vec = pl.BlockSpec((1, D), lambda i: (0, 0))
    if h_is_product:
        kdim = h[0].shape[1]
        h_args = list(h)
        h_specs = [pl.BlockSpec((tm, kdim), lambda i: (i, 0)), pl.BlockSpec((kdim, D), lambda i: (0, 0))]
    else:
        h_args, h_specs = [h], [row]
    return pl.pallas_call(
        body, name=name, grid=(t // tm,),
        out_shape=[jax.ShapeDtypeStruct((t, D), BF16), jax.ShapeDtypeStruct((t, D), F32),
                   jax.ShapeDtypeStruct((8, D), F32)],
        in_specs=[row] + h_specs + [pl.BlockSpec((6, D), lambda i: (0, 0)), vec, vec, row],
        out_specs=[row, row, pl.BlockSpec((8, D), lambda i: (0, 0))],
        compiler_params=_cparams(("arbitrary",)),
    )(x, *h_args, mod6, ln_g, ln_b, cot)


def ln_modulate_bwd(x, du, mod6, scale_row, dx_part, name):
    t = x.shape[0]
    tm = _tile(t, ROW_TILE)
    du_is_product = isinstance(du, tuple)

    def body(x_ref, *refs):
        if du_is_product:
            a_ref, w_ref, mod_ref, dp_ref, dx_ref, acc_ref = refs
            kb = a_ref.shape[2]
            du_v = _nn(a_ref[0], w_ref[0:kb, :])
            for blk in range(1, a_ref.shape[0]):
                du_v = du_v + _nn(a_ref[blk], w_ref[blk * kb:(blk + 1) * kb, :])
        else:
            du_ref, mod_ref, dp_ref, dx_ref, acc_ref = refs
            du_v = du_ref[...]

        @pl.when(pl.program_id(0) == 0)
        def _():
            acc_ref[...] = jnp.zeros_like(acc_ref)

        xh, rstd = _ln(x_ref[...])
        sc = mod_ref[scale_row:scale_row + 1, :]
        acc_ref[0:1, :] += _colsum(du_v * xh)
        acc_ref[1:2, :] += _colsum(du_v)
        dx_ref[...] = dp_ref[...] + _ln_bwd(du_v * (1.0 + sc), xh, rstd)

    row = pl.BlockSpec((tm, D), lambda i: (i, 0))
    if du_is_product:
        nblk, _, kb = du[0].shape
        du_args = list(du)
        du_specs = [pl.BlockSpec((nblk, tm, kb), lambda i: (0, i, 0)), pl.BlockSpec(memory_space=pltpu.VMEM)]
    else:
        du_args, du_specs = [du], [row]
    return pl.pallas_call(
        body, name=name, grid=(t // tm,),
        out_shape=[jax.ShapeDtypeStruct((t, D), F32), jax.ShapeDtypeStruct((8, D), F32)],
        in_specs=[row] + du_specs + [pl.BlockSpec((6, D), lambda i: (0, 0)), row],
        out_specs=[row, pl.BlockSpec((8, D), lambda i: (0, 0))],
        compiler_params=_cparams(("arbitrary",)),
    )(x, *du_args, mod6, dx_part)


def merge_gates(ya, yb, proj):
    t = ya.shape[0]
    tm = _tile(t, WIDE_ROW_TILE)

    def body(ya_ref, yb_ref, ga_ref, gb_ref, o_ref):
        o_ref[...] = (_sigmoid(ga_ref[...]) * ya_ref[...].astype(F32) +
                      _sigmoid(gb_ref[...]) * yb_ref[...].astype(F32)).astype(BF16)

    row = pl.BlockSpec((tm, D), lambda i: (i, 0))
    return pl.pallas_call(
        body, name="merge_gates", grid=(t // tm,),
        out_shape=jax.ShapeDtypeStruct((t, D), BF16),
        in_specs=[row, row, pl.BlockSpec((tm, D), lambda i: (i, GATE_BLOCK0)),
                  pl.BlockSpec((tm, D), lambda i: (i, GATE_BLOCK0 + 1))],
        out_specs=row,
        compiler_params=_cparams(("parallel",)),
    )(ya, yb, proj, proj)


def merge_gates_bwd(dh, w_o, ya, yb, proj):
    t = ya.shape[0]
    tm = _tile(t, ROW_TILE)

    def body(dh_ref, w_ref, ya_ref, yb_ref, ga_ref, gb_ref, dya_ref, dyb_ref, dp_ref):
        dmv = _nt(dh_ref[...], w_ref[...])
        sa = _sigmoid(ga_ref[...])
        sb = _sigmoid(gb_ref[...])
        dya_ref[...] = (dmv * sa).astype(BF16)
        dyb_ref[...] = (dmv * sb).astype(BF16)
        dp_ref[0] = (dmv * ya_ref[...].astype(F32) * sa * (1.0 - sa)).astype(BF16)
        dp_ref[1] = (dmv * yb_ref[...].astype(F32) * sb * (1.0 - sb)).astype(BF16)

    row = pl.BlockSpec((tm, D), lambda i: (i, 0))
    return pl.pallas_call(
        body, name="merge_gates_bwd", grid=(t // tm,),
        out_shape=[jax.ShapeDtypeStruct((t, D), BF16)] * 2 + [jax.ShapeDtypeStruct((N_PROJ // D, t, D), BF16)],
        in_specs=[row, pl.BlockSpec((D, D), lambda i: (0, 0)), row, row,
                  pl.BlockSpec((tm, D), lambda i: (i, GATE_BLOCK0)),
                  pl.BlockSpec((tm, D), lambda i: (i, GATE_BLOCK0 + 1))],
        out_specs=[row, row, pl.BlockSpec((2, tm, D), lambda i: (GATE_BLOCK0 // 2, i, 0))],
        compiler_params=_cparams(("parallel",)),
    )(dh, w_o, ya, yb, proj, proj)


FF_CHUNK = 1408


def ffn_in_act(u, w_gu_t):
    t = u.shape[0]
    tm = _tile(t, ROW_TILE)
    nj = D_FF // FF_CHUNK

    def body(a_ref, bg_ref, bu_ref, gu_ref, act_ref):
        a = a_ref[...]
        g = _nt(a, bg_ref[...])
        up = _nt(a, bu_ref[...])
        gu_ref[0] = g.astype(BF16)
        gu_ref[1] = up.astype(BF16)
        act_ref[...] = (g * _sigmoid(g) * up).astype(BF16)

    return pl.pallas_call(
        body, name="ffn_in_act", grid=(nj, t // tm),
        out_shape=[jax.ShapeDtypeStruct((2, t, D_FF), BF16), jax.ShapeDtypeStruct((t, D_FF), BF16)],
        in_specs=[pl.BlockSpec((tm, D), lambda j, i: (i, 0)), pl.BlockSpec((FF_CHUNK, D), lambda j, i: (j, 0)),
                  pl.BlockSpec((FF_CHUNK, D), lambda j, i: (nj + j, 0))],
        out_specs=[pl.BlockSpec((2, tm, FF_CHUNK), lambda j, i: (0, i, j)),
                   pl.BlockSpec((tm, FF_CHUNK), lambda j, i: (i, j))],
        compiler_params=_cparams(("parallel", "parallel")),
    )(u, w_gu_t, w_gu_t)


def ffn_act_bwd(dh, w_dn, gu):
    t = dh.shape[0]
    tm = _tile(t, ROW_TILE)

    def body(a_ref, b_ref, gu_ref, o_ref):
        da = _nt(a_ref[...], b_ref[...])
        g = gu_ref[0].astype(F32)
        up = gu_ref[1].astype(F32)
        s = _sigmoid(g)
        o_ref[0] = (da * up * _dsilu(g, s)).astype(BF16)
        o_ref[1] = (da * g * s).astype(BF16)

    blk = pl.BlockSpec((2, tm, FF_CHUNK), lambda j, i: (0, i, j))
    return pl.pallas_call(
        body, name="ffn_act_bwd", grid=(D_FF // FF_CHUNK, t // tm),
        out_shape=jax.ShapeDtypeStruct((2, t, D_FF), BF16),
        in_specs=[pl.BlockSpec((tm, D), lambda j, i: (i, 0)), pl.BlockSpec((FF_CHUNK, D), lambda j, i: (j, 0)), blk],
        out_specs=blk,
        compiler_params=_cparams(("parallel", "parallel")),
    )(dh, w_dn, gu)


def _hgrn_chunk_terms(q, fl, lbv, tril_f):
    sig = _sigmoid(fl)
    f = lbv + (1.0 - lbv) * sig
    lam = jnp.log(f)
    k = 1.0 - f
    sq = _sigmoid(q)
    qt = q * sq * Q_SCALE
    bc = _sel(_nn, lam, tril_f, 3, x_first=False)
    bmid = bc[CHUNK // 2 - 1:CHUNK // 2, :]
    bl = bc[CHUNK - 1:CHUNK, :]
    eq = jnp.exp(jnp.minimum(bc - bmid, EXP_CLIP))
    ek = jnp.exp(jnp.minimum(bmid - bc, EXP_CLIP))
    eb = jnp.exp(bc)
    ekl = jnp.exp(bl - bc)
    ebl = jnp.exp(bl)
    return sig, f, k, sq, qt, eq, ek, eb, ekl, ebl


def hgrn_fwd(proj, lb, gnorm):
    t = proj.shape[0]
    tb = _tile(t, TOKEN_BLOCK)
    ncb = tb // CHUNK

    hps = HGRN_HEADS_PER_STEP
    wide = hps * HK

    def body(q_ref, f_ref, i_ref, g_ref, lb_ref, gn_ref, oa_ref, oraw_ref, st_ref, state):
        @pl.when(pl.program_id(1) == 0)
        def _():
            state[...] = jnp.zeros_like(state)

        gn = gn_ref[...]
        mask = _tri(CHUNK)
        tril_f = mask.astype(BF16)

        def chunk(c, carry):
            sl = pl.ds(pl.multiple_of(c * CHUNK, CHUNK), CHUNK)
            for hh in range(hps):
                ln = slice(hh * HK, (hh + 1) * HK)
                q, fl, v, g = q_ref[sl, ln], f_ref[sl, ln], i_ref[sl, ln], g_ref[sl, ln]
                sig, f, k, sq, qt, eq, ek, eb, ekl, ebl = _hgrn_chunk_terms(q, fl, lb_ref[:, ln], tril_f)
                a = jnp.where(mask, _nt((qt * eq).astype(BF16), (k * ek).astype(BF16)), 0.0)
                st = state[hh]
                st_ref[hh, c] = st
                vb = v.astype(BF16)
                o = _nn(a.astype(BF16), vb) + _nt((qt * eb).astype(BF16), st.astype(BF16))
                state[hh] = st * ebl + _tn(vb, (k * ekl).astype(BF16))
                oraw_ref[sl, ln] = o
                rn = o * lax.rsqrt(jnp.mean(o * o, axis=-1, keepdims=True) + RMS_EPS)
                oa_ref[sl, ln] = (rn * gn * g * _sigmoid(g)).astype(BF16)
            return carry

        lax.fori_loop(0, ncb, chunk, 0, unroll=min(CHUNK_UNROLL, ncb))

    def col(block):
        return pl.BlockSpec((tb, wide), lambda h, j: (j, block * (N_HEADS_A // hps) + h))

    return pl.pallas_call(
        body, name="hgrn_fwd", grid=(N_HEADS_A // hps, t // tb),
        out_shape=[jax.ShapeDtypeStruct((t, D), BF16), jax.ShapeDtypeStruct((t, D), F32),
                   jax.ShapeDtypeStruct((N_HEADS_A, t // CHUNK, HK, HK), F32)],
        in_specs=[col(0), col(1), col(2), col(3), pl.BlockSpec((1, wide), lambda h, j: (0, h)),
                  pl.BlockSpec((1, HK), lambda h, j: (0, 0))],
        out_specs=[pl.BlockSpec((tb, wide), lambda h, j: (j, h)), pl.BlockSpec((tb, wide), lambda h, j: (j, h)),
                   pl.BlockSpec((hps, ncb, HK, HK), lambda h, j: (h, j, 0, 0))],
        scratch_shapes=[pltpu.VMEM((hps, HK, HK), F32)],
        compiler_params=_cparams(("parallel", "arbitrary")),
    )(proj, proj, proj, proj, lb, gnorm)


def hgrn_bwd(proj, lb, gnorm, o_raw, dya, w_ba, states, give, dproj):
    t = proj.shape[0]
    tb = _tile(t, TOKEN_BLOCK)
    ncb = tb // CHUNK
    nb = t // tb
    hps = HGRN_HEADS_PER_STEP
    wide = hps * HK

    def body(q_ref, f_ref, i_ref, g_ref, lb_ref, gn_ref, oraw_ref, dya_ref, wba_ref, st_ref, give_ref, dp_in_ref,
             dp_ref, dlb_ref, dgn_ref, got_ref, dstate, doa_ref, send_sem, recv_sem):
        h, j = pl.program_id(0), pl.program_id(1)
        swap_start, swap_wait = _sibling_exchange(give_ref, got_ref, send_sem, recv_sem)
        doa_ref[...] = _nt(dya_ref[...], wba_ref[...])

        @pl.when((h == 0) & (j == 0))
        def _():
            swap_start()

        @pl.when(j == 0)
        def _():
            dstate[...] = jnp.zeros_like(dstate)
            dlb_ref[...] = jnp.zeros_like(dlb_ref)

        @pl.when((j == 0) & (h == 0))
        def _():
            dgn_ref[...] = jnp.zeros_like(dgn_ref)

        gn = gn_ref[...]
        mask = _tri(CHUNK)
        mask_t = _tri(CHUNK, upper=True)
        tril_f = mask.astype(BF16)
        triu_f = mask_t.astype(BF16)

        def chunk(i, c0):
            c = ncb - 1 - i
            sl = pl.ds(pl.multiple_of(c * CHUNK, CHUNK), CHUNK)
            for hh in range(hps):
                ln = slice(hh * HK, (hh + 1) * HK)
                q, fl, v, g = q_ref[sl, ln], f_ref[sl, ln], i_ref[sl, ln], g_ref[sl, ln]
                lbv = lb_ref[:, ln]
                sig, f, k, sq, qt, eq, ek, eb, ekl, ebl = _hgrn_chunk_terms(q, fl, lbv, tril_f)
                qe = (qt * eq).astype(BF16)
                ke = (k * ek).astype(BF16)
                st32 = st_ref[hh, c]
                st = st32.astype(BF16)
                dst = dstate[hh]
                dstb = dst.astype(BF16)
                o = oraw_ref[sl, ln]
                rstd = lax.rsqrt(jnp.mean(o * o, axis=-1, keepdims=True) + RMS_EPS)
                rn = o * rstd
                sgm = _sigmoid(g)
                sg = g * sgm
                doa_v = doa_ref[sl, ln]
                drn = doa_v * gn * sg
                dgn_ref[...] += _colsum(doa_v * rn * sg)
                dp_ref[3, sl, ln] = (doa_v * rn * gn * _dsilu(g, sgm)).astype(BF16)
                do = rstd * (drn - rn * jnp.mean(drn * rn, axis=-1, keepdims=True))
                dob = do.astype(BF16)
                vb = v.astype(BF16)
                da = jnp.where(mask, _nt(dob, vb), 0.0).astype(BF16)
                da_t = jnp.where(mask_t, _nt(vb, dob), 0.0).astype(BF16)
                a_t = jnp.where(mask_t, _nt(ke, qe), 0.0).astype(BF16)
                kl = (k * ekl).astype(BF16)
                qb = (qt * eb).astype(BF16)
                dq_in = _nn(da, ke)
                dk_in = _nn(da_t, qe)
                dq_out = eb * _nn(dob, st)
                dk_out = ekl * _nn(vb, dstb)
                dqt = eq * dq_in + dq_out
                dk = ek * dk_in + dk_out
                dv = _nn(a_t, dob) + _nt(kl, dstb)
                dstate[hh] = dst * ebl + _tn(dob, qb)
                dbig = qe.astype(F32) * dq_in - ke.astype(F32) * dk_in + qt * dq_out - k * dk_out
                beyond = _colsum(k * dk_out) + ebl * _colsum(dst * st32)
                dlam = _sel(_nn, dbig, triu_f, 3, x_first=False) + beyond
                df = dlam / f - dk
                dp_ref[1, sl, ln] = (df * (1.0 - lbv) * sig * (1.0 - sig)).astype(BF16)
                dlb_ref[:, ln] += _colsum(df * (1.0 - sig))
                dp_ref[0, sl, ln] = (dqt * Q_SCALE * _dsilu(q, sq)).astype(BF16)
                dp_ref[2, sl, ln] = dv.astype(BF16)
            return c0

        lax.fori_loop(0, ncb, chunk, 0, unroll=min(CHUNK_UNROLL, ncb))

        @pl.when((h == N_HEADS_A // hps - 1) & (j == nb - 1))
        def _():
            swap_wait()

    def col(block):
        return pl.BlockSpec((tb, wide), lambda h, j: (nb - 1 - j, block * (N_HEADS_A // hps) + h))

    hcol = pl.BlockSpec((tb, wide), lambda h, j: (nb - 1 - j, h))
    hbm = pl.BlockSpec(memory_space=pl.ANY)
    return pl.pallas_call(
        body, name="hgrn_bwd", grid=(N_HEADS_A // hps, nb),
        out_shape=[jax.ShapeDtypeStruct(dproj.shape, dproj.dtype), jax.ShapeDtypeStruct((1, D), F32),
                   jax.ShapeDtypeStruct((1, HK), F32), jax.ShapeDtypeStruct(give.shape, give.dtype)],
        in_specs=[col(0), col(1), col(2), col(3), pl.BlockSpec((1, wide), lambda h, j: (0, h)),
                  pl.BlockSpec((1, HK), lambda h, j: (0, 0)), hcol,
                  pl.BlockSpec((tb, D), lambda h, j: (nb - 1 - j, 0)), pl.BlockSpec((wide, D), lambda h, j: (h, 0)),
                  pl.BlockSpec((hps, ncb, HK, HK), lambda h, j: (h, nb - 1 - j, 0, 0)), hbm, hbm],
        out_specs=[pl.BlockSpec((4, tb, wide), lambda h, j: (0, nb - 1 - j, h)),
                   pl.BlockSpec((1, wide), lambda h, j: (0, h)), pl.BlockSpec((1, HK), lambda h, j: (0, 0)), hbm],
        input_output_aliases={11: 0},
        scratch_shapes=[pltpu.VMEM((hps, HK, HK), F32), pltpu.VMEM((tb, wide), F32)] + SIBLING_SEMS,
        compiler_params=_cparams(("arbitrary", "arbitrary")),
    )(proj, proj, proj, proj, lb, gnorm, o_raw, dya, w_ba, states, give, dproj)


CONV_BLOCK0 = 6
CONV_TAPS = 4
HALO = 8


def conv_fwd(proj, conv_w, conv_b):
    t = proj.shape[0]
    tm = _tile(t, ROW_TILE)
    r = tm // HALO

    def body(x_ref, halo_ref, w_ref, b_ref, o_ref, ds_ref):
        i = pl.program_id(1)
        halo = jnp.where(i > 0, halo_ref[...], 0.0)
        ext = jnp.concatenate([halo, x_ref[...]], axis=0)
        pre = b_ref[...] + w_ref[CONV_TAPS - 1:CONV_TAPS, :] * ext[HALO:, :]
        for tap in range(CONV_TAPS - 1):
            pre = pre + w_ref[tap:tap + 1, :] * pltpu.roll(ext, CONV_TAPS - 1 - tap, axis=0)[HALO:, :]
        s = _sigmoid(pre)
        o_ref[...] = pre * s
        ds_ref[...] = _dsilu(pre, s).astype(BF16)

    blk = pl.BlockSpec((tm, D), lambda cb, i: (i, cb))
    return pl.pallas_call(
        body, name="conv_fwd", grid=(CONV_DIM // D, t // tm),
        out_shape=[jax.ShapeDtypeStruct((t, CONV_DIM), F32), jax.ShapeDtypeStruct((t, CONV_DIM), BF16)],
        in_specs=[pl.BlockSpec((tm, D), lambda cb, i: (i, CONV_BLOCK0 + cb)),
                  pl.BlockSpec((HALO, D), lambda cb, i: (jnp.maximum(i * r - 1, 0), CONV_BLOCK0 + cb)),
                  pl.BlockSpec((CONV_TAPS, D), lambda cb, i: (0, cb)), pl.BlockSpec((1, D), lambda cb, i: (0, cb))],
        out_specs=[blk, blk],
        compiler_params=_cparams(("parallel", "parallel")),
    )(proj, proj, conv_w, conv_b)


def conv_bwd(proj, dxc, dsilu, conv_w, dproj):
    t = proj.shape[0]
    tm = _tile(t, ROW_TILE)
    r = tm // HALO
    n = t // tm
    last_halo = t // HALO - 1

    def body(x_ref, prev_ref, d_ref, dnext_ref, s_ref, snext_ref, w_ref, dp_in_ref, dx_ref, dw_ref, db_ref):
        i = pl.program_id(1)

        @pl.when(i == 0)
        def _():
            dw_ref[...] = jnp.zeros_like(dw_ref)
            db_ref[...] = jnp.zeros_like(db_ref)

        dpre = jnp.concatenate([d_ref[...].astype(F32) * s_ref[...].astype(F32),
                                jnp.where(i < n - 1, dnext_ref[0:HALO, :].astype(F32) * snext_ref[0:HALO, :].astype(F32),
                                          0.0)], axis=0)
        dx = w_ref[CONV_TAPS - 1:CONV_TAPS, :] * dpre[:tm, :]
        for tap in range(CONV_TAPS - 1):
            back = CONV_TAPS - 1 - tap
            dx = dx + w_ref[tap:tap + 1, :] * pltpu.roll(dpre, tm + HALO - back, axis=0)[:tm, :]
        dx_ref[...] = dx.astype(BF16)
        dp = dpre[:tm, :]
        db_ref[...] += _colsum(dp)
        prev = jnp.where(i > 0, prev_ref[...], 0.0)
        ext = jnp.concatenate([prev, x_ref[...]], axis=0)
        dw_ref[CONV_TAPS - 1:CONV_TAPS, :] += _colsum(dp * ext[HALO:, :])
        for tap in range(CONV_TAPS - 1):
            dw_ref[tap:tap + 1, :] += _colsum(dp * pltpu.roll(ext, CONV_TAPS - 1 - tap, axis=0)[HALO:, :])

    blk = pl.BlockSpec((tm, D), lambda cb, i: (i, cb))
    nxt = pl.BlockSpec((2 * HALO, D), lambda cb, i: (jnp.minimum((i + 1) * (r // 2), last_halo // 2), cb))
    return pl.pallas_call(
        body, name="conv_bwd", grid=(CONV_DIM // D, n),
        out_shape=[jax.ShapeDtypeStruct(dproj.shape, dproj.dtype), jax.ShapeDtypeStruct((8, CONV_DIM), F32),
                   jax.ShapeDtypeStruct((1, CONV_DIM), F32)],
        in_specs=[pl.BlockSpec((tm, D), lambda cb, i: (i, CONV_BLOCK0 + cb)),
                  pl.BlockSpec((HALO, D), lambda cb, i: (jnp.maximum(i * r - 1, 0), CONV_BLOCK0 + cb)),
                  blk, nxt, blk, nxt,
                  pl.BlockSpec((CONV_TAPS, D), lambda cb, i: (0, cb)), pl.BlockSpec(memory_space=pl.ANY)],
        out_specs=[pl.BlockSpec((None, tm, D), lambda cb, i: (CONV_BLOCK0 + cb, i, 0)),
                   pl.BlockSpec((8, D), lambda cb, i: (0, cb)), pl.BlockSpec((1, D), lambda cb, i: (0, cb))],
        input_output_aliases={7: 0},
        compiler_params=_cparams(("parallel", "arbitrary")),
    )(proj, proj, dxc, dxc, dsilu, dsilu, conv_w, dproj)


def dt_fill(ddt, dproj):
    t = ddt.shape[0]
    tm = _tile(t, WIDE_ROW_TILE)
    w = ddt.shape[1]

    def body(d_ref, dp_in_ref, o_ref):
        o_ref[:, :w] = d_ref[...]
        o_ref[:, w:] = jnp.zeros((tm, D - w), o_ref.dtype)

    return pl.pallas_call(
        body, name="dt_fill", grid=(t // tm,),
        out_shape=jax.ShapeDtypeStruct(dproj.shape, dproj.dtype),
        in_specs=[pl.BlockSpec((tm, w), lambda i: (i, 0)), pl.BlockSpec(memory_space=pl.ANY)],
        out_specs=pl.BlockSpec((None, tm, D), lambda i: (DT_COL_BLOCK, i, 0)),
        input_output_aliases={1: 0},
        compiler_params=_cparams(("parallel",)),
    )(ddt, dproj)


Z_BLOCK0 = 8
DT_COL_BLOCK = 9
DT_BLOCK0 = 8 * DT_COL_BLOCK
GATE_BLOCK0 = 10
B_BLOCK0 = 16
C_BLOCK0 = 20


def _head_expand():
    e = np.zeros((N_STATE, GROUP_W), np.float32)
    for hh in range(HEADS_PER_GROUP):
        e[hh, hh * HEAD_P:(hh + 1) * HEAD_P] = 1.0
    return jnp.asarray(e, BF16)


def _ssd_chunk_terms(dt, bias, alog, expand, tril_f, eye):
    dtb = dt + bias
    delta = jnp.maximum(dtb, 0.0) + jnp.log(1.0 + jnp.exp(-jnp.abs(dtb)))
    ea = jnp.exp(alog)
    a = -ea * delta
    acum = _sel(_nn, a, tril_f, 3, x_first=False)
    delta_e = _sel(_nn, delta, expand, 2)
    acum_e = _sel(_nn, acum, expand, 2)
    acum_t = _sel(_nt, acum, eye, 3, x_first=False)
    return dtb, delta, ea, a, acum, delta_e, acum_e, acum_t


def ssd_fwd(proj, xc, alog4, bias4, dskip4, wnorm, expand):
    t = proj.shape[0]
    tb = _tile(t, TOKEN_BLOCK)
    ncb = tb // SSD_CHUNK

    def body(xs_ref, b_ref, c_ref, dt_ref, z_ref, alog_ref, bias_ref, dsk_ref, wn_ref, e_ref, ob_ref, st_ref, state):
        @pl.when(pl.program_id(1) == 0)
        def _():
            state[...] = jnp.zeros_like(state)

        expand = e_ref[...]
        mask = _tri(SSD_CHUNK)
        tril_f = mask.astype(BF16)
        eye = (lax.broadcasted_iota(jnp.int32, (N_STATE, N_STATE), 0) ==
               lax.broadcasted_iota(jnp.int32, (N_STATE, N_STATE), 1)).astype(BF16)
        alog, bias = alog_ref[0], bias_ref[0]
        d_e = _sel(_nn, jnp.broadcast_to(dsk_ref[0], (8, N_STATE)), expand, 3)[0:1, :]
        wn = wn_ref[...]

        def chunk(c, carry):
            sl = pl.ds(pl.multiple_of(c * SSD_CHUNK, SSD_CHUNK), SSD_CHUNK)
            xs, bm, cm, dt, z = xs_ref[sl, :], b_ref[sl, :], c_ref[sl, :], dt_ref[sl, :], z_ref[sl, :]
            dtb, delta, ea, a, acum, delta_e, acum_e, acum_t = _ssd_chunk_terms(dt, bias, alog, expand, tril_f, eye)
            alast_e = acum_e[SSD_CHUNK - 1:SSD_CHUNK, :]
            xd = xs * delta_e
            xdb = xd.astype(BF16)
            cb_, bb_ = cm.astype(BF16), bm.astype(BF16)
            cbm = _nt(cb_, bb_)
            ys = []
            for hh in range(HEADS_PER_GROUP):
                lh = jnp.where(mask, jnp.exp(jnp.minimum(acum[:, hh:hh + 1] - acum_t[hh:hh + 1, :], 0.0)), 0.0)
                ys.append(_nn((cbm * lh).astype(BF16), xdb[:, hh * HEAD_P:(hh + 1) * HEAD_P]))
            st = state[...]
            st_ref[0, c] = st
            y = jnp.concatenate(ys, axis=1) + _nn(cb_, st.astype(BF16)) * jnp.exp(acum_e) + xs * d_e
            state[...] = st * jnp.exp(alast_e) + _tn(bb_, (xd * jnp.exp(alast_e - acum_e)).astype(BF16))
            yg = y * z * _sigmoid(z)
            ob_ref[sl, :] = (yg * lax.rsqrt(jnp.mean(yg * yg, axis=-1, keepdims=True) + RMS_EPS) * wn).astype(BF16)
            return carry

        lax.fori_loop(0, ncb, chunk, 0, unroll=min(CHUNK_UNROLL, ncb))

    small = pl.BlockSpec((1, 1, N_STATE), lambda g, j: (g, 0, 0))
    return pl.pallas_call(
        body, name="ssd_fwd", grid=(N_GROUPS, t // tb),
        out_shape=[jax.ShapeDtypeStruct((t, B_INNER), BF16),
                   jax.ShapeDtypeStruct((N_GROUPS, t // SSD_CHUNK, N_STATE, GROUP_W), F32)],
        in_specs=[pl.BlockSpec((tb, GROUP_W), lambda g, j: (j, g)),
                  pl.BlockSpec((tb, N_STATE), lambda g, j: (j, B_BLOCK0 + g)),
                  pl.BlockSpec((tb, N_STATE), lambda g, j: (j, C_BLOCK0 + g)),
                  pl.BlockSpec((tb, N_STATE), lambda g, j: (j, DT_BLOCK0 + g)),
                  pl.BlockSpec((tb, GROUP_W), lambda g, j: (j, Z_BLOCK0 + g)),
                  small, small, small, pl.BlockSpec((1, GROUP_W), lambda g, j: (0, g)),
                  pl.BlockSpec((N_STATE, GROUP_W), lambda g, j: (0, 0))],
        out_specs=[pl.BlockSpec((tb, GROUP_W), lambda g, j: (j, g)),
                   pl.BlockSpec((1, ncb, N_STATE, GROUP_W), lambda g, j: (g, j, 0, 0))],
        scratch_shapes=[pltpu.VMEM((N_STATE, GROUP_W), F32)],
        compiler_params=_cparams(("parallel", "arbitrary")),
    )(xc, xc, xc, proj, proj, alog4, bias4, dskip4, wnorm, expand)


def ssd_bwd(proj, xc, alog4, bias4, dskip4, wnorm, expand, dyb, w_bb, states, part, dproj):
    t = proj.shape[0]
    tb = _tile(t, TOKEN_BLOCK)
    lc = min(SSD_CHUNK_BWD, tb)
    ncb = tb // lc
    nsaved = tb // SSD_CHUNK
    nb = t // tb

    def body(xs_ref, b_ref, c_ref, dt_ref, z_ref, alog_ref, bias_ref, dsk_ref, wn_ref, e_ref, dyb_ref, wbb_ref, st_ref,
             part_ref, dp_in_ref, dxs_ref, db_ref, dc_ref, dz_ref, ddt_ref, dwn_ref, dalog_ref, dbias_ref, ddsk_ref,
             parts_ref, dstate, dob_ref, send_sems, recv_sems, local_sem):
        xchg_start, xchg_wait = _chip_exchange(part_ref, parts_ref, send_sems, recv_sems, local_sem)
        dob_ref[...] = _nt(dyb_ref[...], wbb_ref[...])

        @pl.when((pl.program_id(0) == 0) & (pl.program_id(1) == 0))
        def _():
            xchg_start()

        @pl.when(pl.program_id(1) == 0)
        def _():
            dstate[...] = jnp.zeros_like(dstate)
            dwn_ref[...] = jnp.zeros_like(dwn_ref)
            dalog_ref[...] = jnp.zeros_like(dalog_ref)
            dbias_ref[...] = jnp.zeros_like(dbias_ref)
            ddsk_ref[...] = jnp.zeros_like(ddsk_ref)

        expand = e_ref[...]
        mask = _tri(lc)
        mask_t = _tri(lc, upper=True)
        tril_f = mask.astype(BF16)
        triu_f = mask_t.astype(BF16)
        eye = (lax.broadcasted_iota(jnp.int32, (N_STATE, N_STATE), 0) ==
               lax.broadcasted_iota(jnp.int32, (N_STATE, N_STATE), 1)).astype(BF16)
        alog, bias = alog_ref[0], bias_ref[0]
        d_e = _sel(_nn, jnp.broadcast_to(dsk_ref[0], (8, N_STATE)), expand, 3)[0:1, :]
        wn = wn_ref[...]

        def chunk(i, c0):
            c = ncb - 1 - i
            sl = pl.ds(pl.multiple_of(c * lc, lc), lc)
            xs, bm, cm, dt, z = xs_ref[sl, :], b_ref[sl, :], c_ref[sl, :], dt_ref[sl, :], z_ref[sl, :]
            dtb, delta, ea, a, acum, delta_e, acum_e, acum_t = _ssd_chunk_terms(dt, bias, alog, expand, tril_f, eye)
            alast_e = acum_e[lc - 1:lc, :]
            eacum = jnp.exp(acum_e)
            wl = jnp.exp(alast_e - acum_e)
            xd = xs * delta_e
            xdb = xd.astype(BF16)
            cb_, bb_ = cm.astype(BF16), bm.astype(BF16)
            cbm = _nt(cb_, bb_)
            st32 = st_ref[0, c * (lc // SSD_CHUNK)]
            stb = st32.astype(BF16)
            dst = dstate[...]
            dstb = dst.astype(BF16)
            lhs, mixes, ys = [], [], []
            for hh in range(HEADS_PER_GROUP):
                col, row = acum[:, hh:hh + 1], acum_t[hh:hh + 1, :]
                lh = jnp.where(mask, jnp.exp(jnp.minimum(col - row, 0.0)), 0.0)
                mix = (cbm * lh).astype(BF16)
                lhs.append(lh)
                mixes.append(mix)
                ys.append(_nn(mix, xdb[:, hh * HEAD_P:(hh + 1) * HEAD_P]))
            y_in = jnp.concatenate(ys, axis=1)
            y_out = _nn(cb_, stb) * eacum
            y = y_in + y_out + xs * d_e
            sgz = _sigmoid(z)
            sz = z * sgz
            yg = y * sz
            rstd = lax.rsqrt(jnp.mean(yg * yg, axis=-1, keepdims=True) + RMS_EPS)
            nrm = yg * rstd
            dob_v = dob_ref[sl, :]
            dn = dob_v * wn
            dwn_ref[...] += _colsum(dob_v * nrm)
            dyg = rstd * (dn - nrm * jnp.mean(dn * nrm, axis=-1, keepdims=True))
            dy = dyg * sz
            dz_ref[sl, :] = (dyg * y * _dsilu(z, sgz)).astype(BF16)
            dyb = dy.astype(BF16)
            dxds = []
            dcb = jnp.zeros((lc, lc), F32)
            for hh in range(HEADS_PER_GROUP):
                hs = slice(hh * HEAD_P, (hh + 1) * HEAD_P)
                dy_h, x_h = dyb[:, hs], xdb[:, hs]
                dxds.append(_tn(mixes[hh], dy_h))
                dcb = dcb + _nt(dy_h, x_h) * lhs[hh]
            dcbb = dcb.astype(BF16)
            dye = (dy * eacum).astype(BF16)
            xw = (xd * wl).astype(BF16)
            dxd_in = jnp.concatenate(dxds, axis=1)
            dxd_out = wl * _nn(bb_, dstb)
            dxd = dxd_in + dxd_out
            dc_ref[sl, :] = (_nn(dcbb, bb_) + _nt(dye, stb)).astype(dc_ref.dtype)
            db_ref[sl, :] = (_tn(dcbb, cb_) + _nt(xw, dstb)).astype(db_ref.dtype)
            dstate[...] = dst * jnp.exp(alast_e) + _tn(cb_, dye)
            col_out = xd * dxd_out
            dac = _sel(_nt, dyb.astype(F32) * y_in - xdb.astype(F32) * dxd_in + dy * y_out - col_out, expand, 2)
            beyond = _colsum(col_out) + jnp.exp(alast_e) * _colsum(dst * st32)
            da = (_sel(_nn, dac, triu_f, 3, x_first=False) +
                  _sel(_nt, jnp.broadcast_to(beyond, (8, GROUP_W)), expand, 3)[0:1, :])
            ddelta = _sel(_nt, dxd * xs, expand, 2) - da * ea
            dalog_ref[0] += _colsum(da * a)
            ddtb = ddelta * _sigmoid(dtb)
            dbias_ref[0] += _colsum(ddtb)
            ddt_ref[sl, :] = ddtb.astype(BF16)
            ddsk_ref[0] += _sel(_nt, jnp.broadcast_to(_colsum(dy * xs), (8, GROUP_W)), expand, 3)[0:1, :]
            dxs_ref[sl, :] = (dxd * delta_e + dy * d_e).astype(dxs_ref.dtype)
            return c0

        lax.fori_loop(0, ncb, chunk, 0, unroll=min(CHUNK_UNROLL, ncb))

        @pl.when((pl.program_id(0) == N_GROUPS - 1) & (pl.program_id(1) == nb - 1))
        def _():
            xchg_wait()

    small = pl.BlockSpec((1, 1, N_STATE), lambda g, j: (g, 0, 0))
    wide = pl.BlockSpec((tb, GROUP_W), lambda g, j: (nb - 1 - j, g))
    narrow = pl.BlockSpec((tb, N_STATE), lambda g, j: (nb - 1 - j, g))
    hbm = pl.BlockSpec(memory_space=pl.ANY)
    return pl.pallas_call(
        body, name="ssd_bwd", grid=(N_GROUPS, nb),
        out_shape=[jax.ShapeDtypeStruct((t, B_INNER), BF16), jax.ShapeDtypeStruct((t, GROUP_W), BF16),
                   jax.ShapeDtypeStruct((t, GROUP_W), BF16), jax.ShapeDtypeStruct(dproj.shape, dproj.dtype),
                   jax.ShapeDtypeStruct((t, GROUP_W), BF16), jax.ShapeDtypeStruct((1, B_INNER), F32),
                   jax.ShapeDtypeStruct((N_GROUPS, 1, N_STATE), F32), jax.ShapeDtypeStruct((N_GROUPS, 1, N_STATE), F32),
                   jax.ShapeDtypeStruct((N_GROUPS, 1, N_STATE), F32), jax.ShapeDtypeStruct(part.shape, part.dtype)],
        in_specs=[wide,
                  pl.BlockSpec((tb, N_STATE), lambda g, j: (nb - 1 - j, B_BLOCK0 + g)),
                  pl.BlockSpec((tb, N_STATE), lambda g, j: (nb - 1 - j, C_BLOCK0 + g)),
                  pl.BlockSpec((tb, N_STATE), lambda g, j: (nb - 1 - j, DT_BLOCK0 + g)),
                  pl.BlockSpec((tb, GROUP_W), lambda g, j: (nb - 1 - j, Z_BLOCK0 + g)),
                  small, small, small, pl.BlockSpec((1, GROUP_W), lambda g, j: (0, g)),
                  pl.BlockSpec((N_STATE, GROUP_W), lambda g, j: (0, 0)),
                  pl.BlockSpec((tb, D), lambda g, j: (nb - 1 - j, 0)), pl.BlockSpec((GROUP_W, D), lambda g, j: (g, 0)),
                  pl.BlockSpec((1, nsaved, N_STATE, GROUP_W), lambda g, j: (g, nb - 1 - j, 0, 0)), hbm, hbm],
        out_specs=[wide, narrow, narrow,
                   pl.BlockSpec((None, tb, GROUP_W), lambda g, j: (Z_BLOCK0 // 2 + g // 2, nb - 1 - j, g % 2)),
                   narrow, pl.BlockSpec((1, GROUP_W), lambda g, j: (0, g)), small, small, small, hbm],
        input_output_aliases={14: 3},
        scratch_shapes=[pltpu.VMEM((N_STATE, GROUP_W), F32), pltpu.VMEM((tb, GROUP_W), F32)] + CHIP_SEMS,
        compiler_params=_cparams(("arbitrary", "arbitrary")),
    )(xc, xc, xc, proj, proj, alog4, bias4, dskip4, wnorm, expand, dyb, w_bb, states, part, dproj)


def lower_bound_fwd(hgrn_lb):
    def body(a_ref, o_ref):
        a0, a1 = a_ref[0:1, :], a_ref[1:2, :]
        m = jnp.maximum(a0, a1)
        e0, e1 = jnp.exp(a0 - m), jnp.exp(a1 - m)
        o_ref[...] = e0 / (e0 + e1)

    return pl.pallas_call(body, name="lower_bound_fwd", out_shape=jax.ShapeDtypeStruct((1, D), F32))(hgrn_lb)


def ada_weight_grad(c_all, dmod_cols):
    def body(c_ref, d_ref, o_ref):
        cval = c_ref[...]
        o_ref[...] = _tn(cval * _sigmoid(cval), d_ref[...], HI)

    return pl.pallas_call(body, name="ada_weight_grad",
                          out_shape=jax.ShapeDtypeStruct((D, dmod_cols.shape[1]), F32))(c_all, dmod_cols)


def reduce_small(gathered, hgrn_lb, dlb_off):
    n = gathered.shape[2]

    def body(g_ref, a_ref, o_ref, glb_ref):
        s = g_ref[0]
        for d in range(1, N_DEV):
            s = s + g_ref[d]
        o_ref[...] = s
        a0, a1 = a_ref[0:1, :], a_ref[1:2, :]
        m = jnp.maximum(a0, a1)
        e0, e1 = jnp.exp(a0 - m), jnp.exp(a1 - m)
        p0 = e0 / (e0 + e1)
        tq = s[:, dlb_off:dlb_off + D] * p0 * (1.0 - p0)
        glb_ref[0:1, :] = tq
        glb_ref[1:2, :] = -tq

    return pl.pallas_call(body, name="reduce_small",
                          out_shape=[jax.ShapeDtypeStruct((1, n), F32), jax.ShapeDtypeStruct((2, D), F32)])(gathered, hgrn_lb)


def _adam_math(w, g, m, v):
    m2 = ADAM_B1 * m + (1.0 - ADAM_B1) * g
    v2 = ADAM_B2 * v + (1.0 - ADAM_B2) * (g * g)
    m_hat = m2 / (1.0 - ADAM_B1 ** ADAM_STEP)
    v_hat = v2 / (1.0 - ADAM_B2 ** ADAM_STEP)
    delta = -ADAM_LR * (m_hat / (jnp.sqrt(v_hat) + ADAM_EPS) + ADAM_WD * w)
    return delta, m2, v2


def _row_tile(rows, mult=8, cap=128):
    for cand in range(cap - cap % mult, 0, -mult):
        if rows % cand == 0:
            return cand
    return rows


def sum_parts(parts, name):
    n, rows, cols = parts.shape
    tr = _row_tile(rows, 16, 1024)

    def body(p_ref, o_ref):
        s = p_ref[0].astype(F32)
        for d in range(1, n):
            s = s + p_ref[d].astype(F32)
        o_ref[...] = s

    return pl.pallas_call(
        body, name=name, grid=(rows // tr,),
        out_shape=jax.ShapeDtypeStruct((rows, cols), F32),
        in_specs=[pl.BlockSpec((n, tr, cols), lambda i: (0, i, 0))],
        out_specs=pl.BlockSpec((tr, cols), lambda i: (i, 0)),
        compiler_params=_cparams(("parallel",)),
    )(parts)


def sum_pair(a, b, name):
    rows, cols = a.shape
    tr = _row_tile(rows, 16, 1024)

    def body(a_ref, b_ref, o_ref):
        o_ref[...] = (a_ref[...].astype(F32) + b_ref[...].astype(F32)).astype(o_ref.dtype)

    blk = pl.BlockSpec((tr, cols), lambda i: (i, 0))
    return pl.pallas_call(
        body, name=name, grid=(rows // tr,),
        out_shape=jax.ShapeDtypeStruct((rows, cols), a.dtype),
        in_specs=[blk, blk], out_specs=blk,
        compiler_params=_cparams(("parallel",)),
    )(a, b)


def adamw(w, g, m, v, name):
    rows, cols = w.shape
    tr = _row_tile(rows, 8, 256)

    def body(w_ref, g_ref, m_ref, v_ref, d_ref, m2_ref, v2_ref):
        delta, m2, v2 = _adam_math(w_ref[...], g_ref[...], m_ref[...], v_ref[...])
        d_ref[...] = delta
        m2_ref[...] = m2
        v2_ref[...] = v2

    blk = pl.BlockSpec((tr, cols), lambda i: (i, 0))
    return pl.pallas_call(
        body, name=name, grid=(rows // tr,),
        out_shape=[jax.ShapeDtypeStruct((rows, cols), F32)] * 3,
        in_specs=[blk] * 4, out_specs=[blk] * 3,
        compiler_params=_cparams(("parallel",)),
    )(w, g, m, v)


def _pad128(n):
    return -(-n // 128) * 128


def _pack(arrays):
    offs, parts, off = [], [], 0
    for a in arrays:
        flat = a.reshape(1, -1)
        n = flat.shape[1]
        offs.append(off)
        parts.append(jnp.pad(flat, ((0, 0), (0, _pad128(n) - n))))
        off += _pad128(n)
    return jnp.concatenate(parts, axis=1), offs


def _unpack(vec, offs, shapes):
    out = []
    for off, shp in zip(offs, shapes):
        n = int(np.prod(shp))
        out.append(vec[0, off:off + n].reshape(shp))
    return out


IN_ROWS = IN_DIM // N_DEV
DT_ROW0 = 9216
DT_DEV, DT_LO = divmod(DT_ROW0, IN_ROWS)


GATE_SHIFT = D - 32


def _in_row_pieces(tile):
    pieces = []
    if tile == DT_COL_BLOCK:
        for g in range(N_GROUPS):
            o = DT_ROW0 + HEADS_PER_GROUP * g
            pieces.append((N_STATE * g, o // IN_ROWS, o % IN_ROWS, HEADS_PER_GROUP))
        return pieces
    r, end = tile * D, (tile + 1) * D
    while r < end:
        o = r if r < DT_ROW0 else r - GATE_SHIFT
        dev, loc = divmod(o, IN_ROWS)
        n = min(end - r, IN_ROWS - loc)
        pieces.append((r - tile * D, dev, loc, n))
        r += n
    return pieces


def assemble_w_in(g_all):
    ntile = N_PROJ // D

    def body(g_ref, o_ref):
        j = pl.program_id(0)
        for tile in range(ntile):
            @pl.when(j == tile)
            def _(tile=tile):
                if tile == DT_COL_BLOCK:
                    o_ref[...] = jnp.zeros_like(o_ref)
                for dst, dev, loc, n in _in_row_pieces(tile):
                    o_ref[pl.ds(dst, n), :] = g_ref[dev, pl.ds(loc, n), :]

    return pl.pallas_call(
        body, name="assemble_w_in", grid=(ntile,),
        out_shape=jax.ShapeDtypeStruct((N_PROJ, D), g_all.dtype),
        in_specs=[pl.BlockSpec(memory_space=pltpu.VMEM)],
        out_specs=pl.BlockSpec((D, D), lambda j: (j, 0)),
        compiler_params=_cparams(("arbitrary",)),
    )(g_all)


def _grad_in_blocks(g_t, core, slot):
    dt0 = DT_COL_BLOCK * D
    dt = g_t[dt0:dt0 + N_GROUPS * N_STATE].reshape(N_GROUPS, N_STATE, D)[:, :HEADS_PER_GROUP].reshape(32, D)
    with_dt = jnp.concatenate([g_t[DT_DEV * IN_ROWS:DT_ROW0], dt,
                               g_t[DT_ROW0 + 32 + GATE_SHIFT:(DT_DEV + 1) * IN_ROWS + GATE_SHIFT]], axis=0)
    blocks = []
    for q in range(N_CHIP):
        if 2 * q + 1 < DT_DEV:
            blk = lax.dynamic_slice_in_dim(g_t, IN_ROWS * (2 * q + core), IN_ROWS, axis=0)
        else:
            assert 2 * q == DT_DEV
            after = g_t[(DT_DEV + 1) * IN_ROWS + GATE_SHIFT:(DT_DEV + 2) * IN_ROWS + GATE_SHIFT]
            blk = jnp.where(core == 0, with_dt, after)
        blocks.append(jnp.pad(blk, ((0, slot - IN_ROWS), (0, 0))))
    return jnp.stack(blocks)


def kernel(x, c, w_ada, b_ada, w_in, hgrn_lb, hgrn_gnorm, ssm_conv_w, ssm_conv_b, ssm_dt_bias, ssm_a_log, ssm_d, ssm_norm, w_branch_a, w_branch_b, w_o, ln1_g, ln1_b, w_ffn_gate, w_ffn_up, w_ffn_down, ln2_g, ln2_b, loss_target, m_w_ada, m_b_ada, m_w_in, m_hgrn_lb, m_hgrn_gnorm, m_ssm_conv_w, m_ssm_conv_b, m_ssm_dt_bias, m_ssm_a_log, m_ssm_d, m_ssm_norm, m_w_branch_a, m_w_branch_b, m_w_o, m_ln1_g, m_ln1_b, m_w_ffn_gate, m_w_ffn_up, m_w_ffn_down, m_ln2_g, m_ln2_b, v_w_ada, v_b_ada, v_w_in, v_hgrn_lb, v_hgrn_gnorm, v_ssm_conv_w, v_ssm_conv_b, v_ssm_dt_bias, v_ssm_a_log, v_ssm_d, v_ssm_norm, v_w_branch_a, v_w_branch_b, v_w_o, v_ln1_g, v_ln1_b, v_w_ffn_gate, v_w_ffn_up, v_w_ffn_down, v_ln2_g, v_ln2_b):
    me = 4 * lax.axis_index("x") + 2 * lax.axis_index("y") + lax.axis_index("c")
    xt = x[0]
    tgt = loss_target[0]
    t = xt.shape[0]
    ada_cols = w_ada.shape[2]
    conv_cols = ssm_conv_w.shape[2]

    small_in, _ = _pack([c, ssm_conv_w[0]])
    small_all = allgather_vmem(small_in, "allgather_small_inputs")
    c_all = small_all[:, 0, :D]
    conv_w = small_all[:, 0, D:D + CONV_TAPS * conv_cols].reshape(N_DEV, CONV_TAPS, conv_cols)
    conv_w = conv_w.transpose(1, 0, 2).reshape(CONV_TAPS, CONV_DIM)
    mod = ada_modulation(c_all, w_ada[0], b_ada.reshape(N_DEV, 1, ada_cols))
    mod6 = mod.reshape(6, D)

    shards = [w_in[0].T, w_branch_a[0], w_branch_b[0], w_o[0], w_ffn_gate[0].T, w_ffn_up[0].T, w_ffn_down[0]]
    shard_rows = [s.shape[0] for s in shards]
    slot_rows = [-(-r // 32) * 32 for r in shard_rows]
    row_offs = [sum(slot_rows[:i]) for i in range(len(shards))]
    padded = [jnp.pad(s.astype(BF16), ((0, p - r), (0, 0))) for s, r, p in zip(shards, shard_rows, slot_rows)]
    w_in_t = assemble_w_in(allgather_hbm(padded[0], "allgather_w_in"))

    lb = lower_bound_fwd(hgrn_lb)
    u1 = ln_modulate(xt, mod6, 0, 1, "ln_modulate_1")
    proj, g_rest = mm_nt_gather(u1, w_in_t, F32, jnp.concatenate(padded[1:], axis=0), "mm_in_proj")
    g_ba, g_bb, g_o, g_fg, g_fu, g_fd = (g_rest[:, o - slot_rows[0]:o - slot_rows[0] + r]
                                         for o, r in zip(row_offs[1:], shard_rows[1:]))
    w_ba = g_ba.reshape(D, D)
    w_bb = g_bb.reshape(B_INNER, D)
    w_oo = g_o.reshape(D, D)
    w_gu_t = jnp.concatenate([g_fg.reshape(D_FF, D), g_fu.reshape(D_FF, D)], axis=0)
    w_dn = g_fd.reshape(D_FF, D)
    o_a, o_raw, st_a = hgrn_fwd(proj, lb, hgrn_gnorm)
    xc, conv_slope = conv_fwd(proj, conv_w, ssm_conv_b)
    pad3 = ((0, 0), (0, 0), (0, N_STATE - HEADS_PER_GROUP))
    alog4 = jnp.pad(ssm_a_log.reshape(N_GROUPS, 1, HEADS_PER_GROUP), pad3)
    bias4 = jnp.pad(ssm_dt_bias.reshape(N_GROUPS, 1, HEADS_PER_GROUP), pad3)
    dskip4 = jnp.pad(ssm_d.reshape(N_GROUPS, 1, HEADS_PER_GROUP), pad3)
    expand = _head_expand()
    o_b, st_b = ssd_fwd(proj, xc, alog4, bias4, dskip4, ssm_norm, expand)
    ya = mm_nn(o_a, w_ba, BF16, "mm_branch_a")
    yb = mm_nn(o_b, w_bb, BF16, "mm_branch_b")
    merged = merge_gates(ya, yb, proj)
    h1 = mm_nn(merged, w_oo, F32, "mm_out_proj")
    x1 = resid_ln(xt, h1, mod6, 2, ln1_g, ln1_b, "resid_ln_1")
    u2 = ln_modulate(x1, mod6, 3, 4, "ln_modulate_2")
    gu, act = ffn_in_act(u2, w_gu_t)

    dh2, dx1_part, acc4 = resid_ln_bwd(x1, (act, w_dn), mod6, 5, ln2_g, ln2_b, tgt, True, "resid_ln_2_bwd")
    g_dn = mm_tn(act, dh2, "mm_grad_ffn_down")
    dgu = ffn_act_bwd(dh2, w_dn, gu)
    g_gu_t = mm_tn(dgu, u2, "mm_grad_ffn_in")
    dx1, acc3 = ln_modulate_bwd(x1, (dgu, w_gu_t), mod6, 4, dx1_part, "ln_modulate_2_bwd")
    dh1, dx_part, acc2 = resid_ln_bwd(xt, h1, mod6, 2, ln1_g, ln1_b, dx1, False, "resid_ln_1_bwd")
    g_o = mm_tn(merged, dh1, "mm_grad_out_proj")
    dya, dyb, dproj = merge_gates_bwd(dh1, w_oo, ya, yb, proj)
    g_ba_full = mm_tn(o_a, dya, "mm_grad_branch_a")
    g_bb_full = mm_tn(o_b, dyb, "mm_grad_branch_b")
    my_core = lax.axis_index("c")

    def by_core(blocks, rows, slots):
        contrib = jnp.concatenate([jnp.pad(b.reshape(N_DEV, -1, D), ((0, 0), (0, p - r), (0, 0)))
                                   for b, r, p in zip(blocks, rows, slots)], axis=1)
        split = contrib.reshape(N_CHIP, 2, contrib.shape[1], D).transpose(1, 0, 2, 3)
        return (lax.dynamic_index_in_dim(split, my_core, 0, keepdims=False),
                lax.dynamic_index_in_dim(split, 1 - my_core, 0, keepdims=False))

    keep_e, give_e = by_core([g_ba_full, g_bb_full, g_o, g_gu_t[:D_FF], g_gu_t[D_FF:], g_dn],
                             shard_rows[1:], slot_rows[1:])
    dproj, dlb, dgn, got_e = hgrn_bwd(proj, lb, hgrn_gnorm, o_raw, dya, w_ba, st_a, give_e, dproj)
    chip_e = sum_pair(keep_e.reshape(-1, D), got_e.reshape(-1, D), "sum_grads_rest_chip").reshape(keep_e.shape)
    dxs, dbm, dcm, dproj, ddt, dwn, dalog, dbias, ddsk, parts_e = ssd_bwd(proj, xc, alog4, bias4, dskip4, ssm_norm,
                                                                          expand, dyb, w_bb, st_b, chip_e, dproj)
    dxc = jnp.concatenate([dxs, dbm, dcm], axis=1)
    dproj, dcw, dcb = conv_bwd(proj, dxc, conv_slope, conv_w, dproj)
    dproj = dt_fill(ddt, dproj)
    g_in_t = mm_tn(dproj, u1, "mm_grad_in_proj")
    keep_l = _grad_in_blocks(g_in_t, my_core, slot_rows[0])
    give_l = _grad_in_blocks(g_in_t, 1 - my_core, slot_rows[0])
    got_l = exchange_sibling(give_l, "exchange_grad_in_sibling")
    chip_l = sum_pair(keep_l.reshape(-1, D), got_l.reshape(-1, D), "sum_grad_in_chip").reshape(keep_l.shape)
    du1, parts_l = mm_nn_exchange(dproj, w_in_t, F32, chip_l, "mm_du1")
    dx, acc1 = ln_modulate_bwd(xt, du1, mod6, 1, dx_part, "ln_modulate_1_bwd")
    gw_in = sum_parts(parts_l, "sum_grad_in")[:shard_rows[0]].T
    g_rows = sum_parts(parts_e, "sum_grads_rest")
    gw_ba, gw_bb, gw_o, gw_fg, gw_fu, gw_fd = (g_rows[o - slot_rows[0]:o - slot_rows[0] + r]
                                               for o, r in zip(row_offs[1:], shard_rows[1:]))
    gw_fg, gw_fu = gw_fg.T, gw_fu.T

    dmod = jnp.concatenate([acc1[1:2], acc1[0:1], acc2[0:1], acc3[1:2], acc3[0:1], acc4[0:1]], axis=1)
    small_fields = [dmod, acc4[3:4, :128], dlb, dgn, dcw[:CONV_TAPS], dcb, dbias, dalog, ddsk, dwn,
                    acc2[1:2], acc2[2:3], acc4[1:2], acc4[2:3]]
    small_out, offs = _pack(small_fields)
    small_sum_in = allgather_vmem(small_out, "allgather_small_grads")
    gsum, g_lb = reduce_small(small_sum_in, hgrn_lb, offs[2])
    (g_bada, loss_row, _, g_gn, g_cw_full, g_cb, g_bias4, g_alog4, g_dsk4, g_wn, g_l1g, g_l1b, g_l2g, g_l2b) = _unpack(
        gsum, offs, [(1, 6 * D), (1, 128), (1, D), (1, HK), (CONV_TAPS, CONV_DIM), (1, CONV_DIM),
                     (N_GROUPS, N_STATE), (N_GROUPS, N_STATE), (N_GROUPS, N_STATE), (1, B_INNER),
                     (1, D), (1, D), (1, D), (1, D)])
    loss = loss_row[0, 0]
    g_cw = lax.dynamic_slice(g_cw_full, (0, me * conv_cols), (CONV_TAPS, conv_cols))[None]
    g_dtb = g_bias4[:, :HEADS_PER_GROUP].reshape(1, 32)
    g_alog = g_alog4[:, :HEADS_PER_GROUP].reshape(1, 32)
    g_dsk = g_dsk4[:, :HEADS_PER_GROUP].reshape(1, 32)

    dmod_all = small_sum_in[:, 0, offs[0]:offs[0] + 6 * D]
    dmod_cols = lax.dynamic_slice(dmod_all, (0, me * ada_cols), (N_DEV, ada_cols))
    gw_ada = ada_weight_grad(c_all, dmod_cols)

    big = [("ada", w_ada[0], gw_ada, m_w_ada[0], v_w_ada[0]), ("in", w_in[0], gw_in, m_w_in[0], v_w_in[0]),
           ("branch_a", w_branch_a[0], gw_ba, m_w_branch_a[0], v_w_branch_a[0]),
           ("branch_b", w_branch_b[0], gw_bb, m_w_branch_b[0], v_w_branch_b[0]),
           ("o", w_o[0], gw_o, m_w_o[0], v_w_o[0]),
           ("ffn_gate", w_ffn_gate[0], gw_fg, m_w_ffn_gate[0], v_w_ffn_gate[0]),
           ("ffn_up", w_ffn_up[0], gw_fu, m_w_ffn_up[0], v_w_ffn_up[0]),
           ("ffn_down", w_ffn_down[0], gw_fd, m_w_ffn_down[0], v_w_ffn_down[0])]
    big_out = {}
    for nm, w_, g_, m_, v_ in big:
        d_, m2_, v2_ = adamw(w_, g_, m_, v_, "adamw_" + nm)
        big_out[nm] = (g_[None], d_[None], m2_[None], v2_[None])

    small_w = [b_ada, hgrn_lb, hgrn_gnorm, ssm_conv_w, ssm_conv_b, ssm_dt_bias, ssm_a_log, ssm_d, ssm_norm,
               ln1_g, ln1_b, ln2_g, ln2_b]
    small_g = [g_bada, g_lb, g_gn, g_cw, g_cb, g_dtb, g_alog, g_dsk, g_wn, g_l1g, g_l1b, g_l2g, g_l2b]
    small_m = [m_b_ada, m_hgrn_lb, m_hgrn_gnorm, m_ssm_conv_w, m_ssm_conv_b, m_ssm_dt_bias, m_ssm_a_log, m_ssm_d,
               m_ssm_norm, m_ln1_g, m_ln1_b, m_ln2_g, m_ln2_b]
    small_v = [v_b_ada, v_hgrn_lb, v_hgrn_gnorm, v_ssm_conv_w, v_ssm_conv_b, v_ssm_dt_bias, v_ssm_a_log, v_ssm_d,
               v_ssm_norm, v_ln1_g, v_ln1_b, v_ln2_g, v_ln2_b]
    shapes = [a.shape for a in small_w]
    small_g = [g_.reshape(s) for g_, s in zip(small_g, shapes)]
    pw, poffs = _pack(small_w)
    pg, _ = _pack(small_g)
    pm, _ = _pack(small_m)
    pv, _ = _pack(small_v)
    pd, pm2, pv2 = adamw(pw, pg, pm, pv, "adamw_small")
    s_d, s_m, s_v = (_unpack(p, poffs, shapes) for p in (pd, pm2, pv2))
    (sn_bada, sn_lb, sn_gn, sn_cw, sn_cb, sn_dtb, sn_alog, sn_dsk, sn_wn, sn_l1g, sn_l1b, sn_l2g, sn_l2b) = range(13)

    def order(kind):
        sm = [small_g, s_d, s_m, s_v][kind]
        bg = lambda nm: big_out[nm][kind]
        return [bg("ada"), sm[sn_bada], bg("in"), sm[sn_lb], sm[sn_gn], sm[sn_cw], sm[sn_cb], sm[sn_dtb], sm[sn_alog],
                sm[sn_dsk], sm[sn_wn], bg("branch_a"), bg("branch_b"), bg("o"), sm[sn_l1g], sm[sn_l1b],
                bg("ffn_gate"), bg("ffn_up"), bg("ffn_down"), sm[sn_l2g], sm[sn_l2b]]

    return (loss, dx[None], *order(0), *order(1), *order(2), *order(3))
```

```python
import numpy as np
import jax
import jax.numpy as jnp
from jax import lax
from jax.experimental import pallas as pl
from jax.experimental.pallas import tpu as pltpu

F32 = jnp.float32
BF16 = jnp.bfloat16
HI = lax.Precision.HIGHEST

N_DEV = 8
D = 1024
N_HEADS_A = 8
HK = 128
CHUNK = 64
SSD_CHUNK = 128
SSD_CHUNK_BWD = 256
N_GROUPS = 4
HEADS_PER_GROUP = 8
HEAD_P = 64
N_STATE = 128
GROUP_W = HEADS_PER_GROUP * HEAD_P
B_INNER = 2048
CONV_DIM = 3072
D_FF = 2816
IN_DIM = 11296
N_PROJ = 12288
ALPHA = 2.0 ** 0.25
LN_EPS = 1e-5
RMS_EPS = 1e-6
Q_SCALE = 128 ** -0.5
EXP_CLIP = 80.0
ADAM_LR, ADAM_B1, ADAM_B2, ADAM_EPS, ADAM_WD, ADAM_STEP = 0.001, 0.9, 0.999, 1e-8, 0.01, 10
VMEM_LIMIT = 48 * 1024 * 1024
TOKEN_BLOCK = 1024
ROW_TILE = 512
WIDE_ROW_TILE = 1024
MM_ROW_TILE = 1024
MM_TOKEN_TILE = 4096
MM_K_TILE = 3072
MM_COL_TILE = 1408
HGRN_HEADS_PER_STEP = 4
CHUNK_UNROLL = 8
MESH_ID = pl.DeviceIdType.MESH

NT_DIMS = (((1,), (1,)), ((), ()))
TN_DIMS = (((0,), (0,)), ((), ()))


def _cparams(sem=None):
    return pltpu.CompilerParams(dimension_semantics=sem, vmem_limit_bytes=VMEM_LIMIT)


def _sigmoid(x):
    return 1.0 / (1.0 + jnp.exp(-x))


def _dsilu(x, s):
    return s * (1.0 + x * (1.0 - s))


def _nt(a, b, precision=None):
    return lax.dot_general(a, b, NT_DIMS, precision=precision, preferred_element_type=F32)


def _tn(a, b, precision=None):
    return lax.dot_general(a, b, TN_DIMS, precision=precision, preferred_element_type=F32)


def _nn(a, b, precision=None):
    return jnp.dot(a, b, precision=precision, preferred_element_type=F32)


def _split(x, pieces):
    out = []
    for i in range(pieces):
        p = x.astype(BF16)
        out.append(p)
        if i + 1 < pieces:
            x = x - p.astype(F32)
    return out


def _sel(dot, x, sel01, pieces, x_first=True):
    acc = None
    for p in _split(x, pieces):
        term = dot(p, sel01) if x_first else dot(sel01, p)
        acc = term if acc is None else acc + term
    return acc


def _ln(x):
    mu = jnp.mean(x, axis=-1, keepdims=True)
    xc = x - mu
    rstd = lax.rsqrt(jnp.mean(xc * xc, axis=-1, keepdims=True) + LN_EPS)
    return xc * rstd, rstd


def _ln_bwd(dxh, xh, rstd):
    return rstd * (dxh - jnp.mean(dxh, axis=-1, keepdims=True) - xh * jnp.mean(dxh * xh, axis=-1, keepdims=True))


def _colsum(x):
    return jnp.sum(x, axis=0, keepdims=True)


def _tri(n, upper=False):
    r = lax.broadcasted_iota(jnp.int32, (n, n), 0)
    c = lax.broadcasted_iota(jnp.int32, (n, n), 1)
    return (c >= r) if upper else (r >= c)


def _my_pos():
    return lax.axis_index("x"), lax.axis_index("y"), lax.axis_index("c")


def _peer(pos, k):
    x, y, c = pos
    return (x ^ ((k >> 2) & 1), y ^ ((k >> 1) & 1), c ^ (k & 1))


def _flat(pos):
    return 4 * pos[0] + 2 * pos[1] + pos[2]


def allgather_vmem(v, name):
    n = v.shape[1]

    def body(v_ref, o_ref, send_sems, recv_sems, local_sem):
        me = _my_pos()
        mine = pltpu.make_async_copy(v_ref, o_ref.at[_flat(me)], local_sem)
        mine.start()
        sends = []
        for k in range(1, N_DEV):
            peer = _peer(me, k)
            cp = pltpu.make_async_remote_copy(v_ref, o_ref.at[_flat(me)], send_sems.at[k - 1], recv_sems.at[k - 1],
                                              device_id=peer, device_id_type=MESH_ID)
            cp.start()
            sends.append(cp)
        for k in range(1, N_DEV):
            peer = _peer(me, k)
            pltpu.make_async_remote_copy(v_ref, o_ref.at[_flat(peer)], send_sems.at[k - 1], recv_sems.at[k - 1],
                                         device_id=peer, device_id_type=MESH_ID).wait_recv()
        for cp in sends:
            cp.wait_send()
        mine.wait()

    return pl.pallas_call(
        body, name=name,
        out_shape=jax.ShapeDtypeStruct((N_DEV, 1, n), F32),
        in_specs=[pl.BlockSpec(memory_space=pltpu.VMEM)],
        out_specs=pl.BlockSpec(memory_space=pltpu.VMEM),
        scratch_shapes=[pltpu.SemaphoreType.DMA((N_DEV - 1,)), pltpu.SemaphoreType.DMA((N_DEV - 1,)),
                        pltpu.SemaphoreType.DMA],
        compiler_params=_cparams(),
    )(v)


def ada_modulation(c_all, w_ada_s, b_ada_r):
    ncol = w_ada_s.shape[1]

    def body(c_ref, w_ref, b_ref, o_ref, part_ref, send_sems, recv_sems):
        me = _my_pos()
        cval = c_ref[...]
        cond = cval * _sigmoid(cval)
        part = _nn(cond, w_ref[...], HI)
        for r in range(N_DEV):
            part_ref[r] = part[r:r + 1, :]
        sends = []
        for k in range(1, N_DEV):
            peer = _peer(me, k)
            cp = pltpu.make_async_remote_copy(part_ref.at[_flat(peer)], o_ref.at[_flat(me)], send_sems.at[k - 1],
                                              recv_sems.at[k - 1], device_id=peer, device_id_type=MESH_ID)
            cp.start()
            sends.append(cp)
        o_ref[_flat(me)] = part_ref[_flat(me)]
        for k in range(1, N_DEV):
            peer = _peer(me, k)
            pltpu.make_async_remote_copy(part_ref.at[_flat(peer)], o_ref.at[_flat(peer)], send_sems.at[k - 1],
                                         recv_sems.at[k - 1], device_id=peer, device_id_type=MESH_ID).wait_recv()
        for cp in sends:
            cp.wait_send()
        o_ref[...] = o_ref[...] + b_ref[...]

    return pl.pallas_call(
        body, name="ada_modulation",
        out_shape=jax.ShapeDtypeStruct((N_DEV, 1, ncol), F32),
        in_specs=[pl.BlockSpec(memory_space=pltpu.VMEM)] * 3,
        out_specs=pl.BlockSpec(memory_space=pltpu.VMEM),
        scratch_shapes=[pltpu.VMEM((N_DEV, 1, ncol), F32), pltpu.SemaphoreType.DMA((N_DEV - 1,)),
                        pltpu.SemaphoreType.DMA((N_DEV - 1,))],
        compiler_params=_cparams(),
    )(c_all, w_ada_s, b_ada_r)


def allgather_hbm(shard, name):
    def body(x_ref, out_ref, send_sems, recv_sems, local_sem):
        x, y, c = _my_pos()
        me, sibling = (x, y, c), (x, y, 1 - c)
        chips = [(1 - x, y), (x, 1 - y), (1 - x, 1 - y)]

        def slot(pos):
            return out_ref.at[_flat(pos)]

        def copy(k, block, to, src=None):
            return pltpu.make_async_remote_copy(slot(block) if src is None else src, slot(block), send_sems.at[k],
                                                recv_sems.at[k], device_id=to, device_id_type=MESH_ID)

        mine = pltpu.make_async_copy(x_ref, slot(me), local_sem)
        mine.start()
        first = [copy(0, me, sibling, src=x_ref)]
        first += [copy(1 + j, me, (*chip, c), src=x_ref) for j, chip in enumerate(chips)]
        for cp in first:
            cp.start()
        passed = [copy(4 + j, (*chip, c), sibling) for j, chip in enumerate(chips)]
        for j, chip in enumerate(chips):
            copy(1 + j, (*chip, c), me).wait_recv()
            passed[j].start()
        copy(0, sibling, me).wait_recv()
        for j, chip in enumerate(chips):
            copy(4 + j, (*chip, 1 - c), me).wait_recv()
        for cp in first + passed:
            cp.wait_send()
        mine.wait()

    return pl.pallas_call(
        body, name=name,
        out_shape=jax.ShapeDtypeStruct((N_DEV,) + shard.shape, shard.dtype),
        in_specs=[pl.BlockSpec(memory_space=pl.ANY)],
        out_specs=pl.BlockSpec(memory_space=pl.ANY),
        scratch_shapes=[pltpu.SemaphoreType.DMA((N_DEV - 1,)), pltpu.SemaphoreType.DMA((N_DEV - 1,)),
                        pltpu.SemaphoreType.DMA],
        compiler_params=_cparams(),
    )(shard)


N_CHIP = N_DEV // 2
SIBLING_SEMS = [pltpu.SemaphoreType.DMA, pltpu.SemaphoreType.DMA]
CHIP_SEMS = [pltpu.SemaphoreType.DMA((N_CHIP - 1,)), pltpu.SemaphoreType.DMA((N_CHIP - 1,)), pltpu.SemaphoreType.DMA]


def _sibling_exchange(s_ref, o_ref, send_sem, recv_sem):
    x, y, c = _my_pos()
    cp = pltpu.make_async_remote_copy(s_ref, o_ref, send_sem, recv_sem, device_id=(x, y, 1 - c), device_id_type=MESH_ID)
    return cp.start, cp.wait


def _chip_exchange(p_ref, o_ref, send_sems, recv_sems, local_sem):
    x, y, c = _my_pos()
    my_chip = 2 * x + y
    mine = pltpu.make_async_copy(p_ref.at[my_chip], o_ref.at[my_chip], local_sem)
    peers = [(x ^ (k >> 1), y ^ (k & 1)) for k in range(1, N_CHIP)]
    sends = [pltpu.make_async_remote_copy(p_ref.at[2 * px + py], o_ref.at[my_chip], send_sems.at[k], recv_sems.at[k],
                                          device_id=(px, py, c), device_id_type=MESH_ID)
             for k, (px, py) in enumerate(peers)]
    recvs = [pltpu.make_async_remote_copy(p_ref.at[2 * px + py], o_ref.at[2 * px + py], send_sems.at[k], recv_sems.at[k],
                                          device_id=(px, py, c), device_id_type=MESH_ID)
             for k, (px, py) in enumerate(peers)]

    def start():
        mine.start()
        for cp in sends:
            cp.start()

    def wait():
        for cp in recvs:
            cp.wait_recv()
        for cp in sends:
            cp.wait_send()
        mine.wait()

    return start, wait


def exchange_sibling(send, name):
    def body(s_ref, o_ref, send_sem, recv_sem):
        start, wait = _sibling_exchange(s_ref, o_ref, send_sem, recv_sem)
        start()
        wait()

    return pl.pallas_call(
        body, name=name,
        out_shape=jax.ShapeDtypeStruct(send.shape, send.dtype),
        in_specs=[pl.BlockSpec(memory_space=pl.ANY)],
        out_specs=pl.BlockSpec(memory_space=pl.ANY),
        scratch_shapes=SIBLING_SEMS,
        compiler_params=_cparams(),
    )(send)


LANES = 128


def _k_tile(kdim, unit=LANES):
    for cand in range(MM_K_TILE - MM_K_TILE % unit, 0, -unit):
        if kdim % cand == 0:
            return cand
    return kdim


def _lane_tile(n, cap):
    for cand in range(cap - cap % LANES, 0, -LANES):
        if n % cand == 0:
            return cand
    return n


def _m_tile(m, kdim):
    return min(MM_ROW_TILE if kdim > D else 2 * MM_ROW_TILE, m)


def _mm(a, b, out_dtype, name, b_is_nk):
    if a.ndim == 3:
        m, tk = a.shape[1], a.shape[2]
        kdim = a.shape[0] * tk
        a_spec = pl.BlockSpec((None, _m_tile(m, kdim), tk), lambda j, i, k: (k, i, 0))
    else:
        m, kdim = a.shape
        tk = _k_tile(kdim)
        a_spec = pl.BlockSpec((_m_tile(m, kdim), tk), lambda j, i, k: (i, k))
    n = b.shape[0] if b_is_nk else b.shape[1]
    tm, tn = _m_tile(m, kdim), _lane_tile(n, MM_COL_TILE)
    nk = kdim // tk
    dot = _nt if b_is_nk else _nn

    def body(a_ref, b_ref, o_ref, *acc):
        p = dot(a_ref[...], b_ref[...])
        if nk == 1:
            o_ref[...] = p.astype(o_ref.dtype)
        else:
            acc_ref, k = acc[0], pl.program_id(2)

            @pl.when(k == 0)
            def _():
                acc_ref[...] = p

            @pl.when(k > 0)
            def _():
                acc_ref[...] += p

            @pl.when(k == nk - 1)
            def _():
                o_ref[...] = acc_ref[...].astype(o_ref.dtype)

    b_spec = (pl.BlockSpec((tn, tk), lambda j, i, k: (j, k)) if b_is_nk else
              pl.BlockSpec((tk, tn), lambda j, i, k: (k, j)))
    return pl.pallas_call(
        body, name=name, grid=(n // tn, m // tm, nk),
        out_shape=jax.ShapeDtypeStruct((m, n), out_dtype),
        in_specs=[a_spec, b_spec],
        out_specs=pl.BlockSpec((tm, tn), lambda j, i, k: (i, j)),
        scratch_shapes=[] if nk == 1 else [pltpu.VMEM((tm, tn), F32)],
        compiler_params=_cparams(("parallel", "parallel", "arbitrary")),
    )(a, b)


def mm_nn(a, b, out_dtype, name):
    return _mm(a, b, out_dtype, name, False)


def mm_nt(a, b, out_dtype, name):
    return _mm(a, b, out_dtype, name, True)


def mm_nn_exchange(a, b, out_dtype, part, name):
    kblocks, m, kb = a.shape
    kdim = kblocks * kb
    n = b.shape[1]
    tm, tn, tk = min(MM_ROW_TILE, m), _lane_tile(n, MM_COL_TILE), _k_tile(kdim)
    gn, gm, nk = n // tn, m // tm, kdim // tk
    per_step = tk // kb

    def body(a_ref, b_ref, part_ref, o_ref, parts_ref, acc_ref, send_sems, recv_sems, local_sem):
        j, i, k = pl.program_id(0), pl.program_id(1), pl.program_id(2)
        xchg_start, xchg_wait = _chip_exchange(part_ref, parts_ref, send_sems, recv_sems, local_sem)

        @pl.when((j == 0) & (i == 0) & (k == 0))
        def _():
            xchg_start()

        p = _nn(a_ref[0], b_ref[0:kb, :])
        for c in range(1, per_step):
            p = p + _nn(a_ref[c], b_ref[c * kb:(c + 1) * kb, :])

        @pl.when(k == 0)
        def _():
            acc_ref[...] = p

        @pl.when(k > 0)
        def _():
            acc_ref[...] += p

        @pl.when(k == nk - 1)
        def _():
            o_ref[...] = acc_ref[...].astype(o_ref.dtype)

        @pl.when((j == gn - 1) & (i == gm - 1) & (k == nk - 1))
        def _():
            xchg_wait()

    hbm = pl.BlockSpec(memory_space=pl.ANY)
    return pl.pallas_call(
        body, name=name, grid=(gn, gm, nk),
        out_shape=[jax.ShapeDtypeStruct((m, n), out_dtype), jax.ShapeDtypeStruct(part.shape, part.dtype)],
        in_specs=[pl.BlockSpec((per_step, tm, kb), lambda j, i, k: (k, i, 0)),
                  pl.BlockSpec((tk, tn), lambda j, i, k: (k, j)), hbm],
        out_specs=[pl.BlockSpec((tm, tn), lambda j, i, k: (i, j)), hbm],
        scratch_shapes=[pltpu.VMEM((tm, tn), F32)] + CHIP_SEMS,
        compiler_params=_cparams(("arbitrary", "arbitrary", "arbitrary")),
    )(a, b, part)


def mm_nt_gather(a, b, out_dtype, shard, name):
    m, kdim = a.shape
    n = b.shape[0]
    tm, tn = _m_tile(m, kdim), 1024
    assert kdim == 1024
    gj = m // tm
    nsteps = (n // tn) * gj
    forward_step = max(nsteps - 2, 0)

    def body(a_ref, b_ref, x_ref, o_ref, g_ref, send_sems, recv_sems, local_sem):
        step = pl.program_id(0) * gj + pl.program_id(1)
        x, y, c = _my_pos()
        me, sibling = (x, y, c), (x, y, 1 - c)
        chips = [(1 - x, y), (x, 1 - y), (1 - x, 1 - y)]

        def slot(pos):
            return g_ref.at[_flat(pos)]

        def copy(k, block, to, src=None):
            return pltpu.make_async_remote_copy(slot(block) if src is None else src, slot(block), send_sems.at[k],
                                                recv_sems.at[k], device_id=to, device_id_type=MESH_ID)

        mine = pltpu.make_async_copy(x_ref, slot(me), local_sem)
        first = [copy(0, me, sibling, src=x_ref)]
        first += [copy(1 + j, me, (*chip, c), src=x_ref) for j, chip in enumerate(chips)]
        passed = [copy(4 + j, (*chip, c), sibling) for j, chip in enumerate(chips)]

        @pl.when(step == 0)
        def _():
            mine.start()
            for cp in first:
                cp.start()

        rows = pl.ds(pl.multiple_of(pl.program_id(1) * tm, tm), tm)
        o_ref[...] = _nt(a_ref[rows, :], b_ref[...]).astype(o_ref.dtype)

        @pl.when(step == forward_step)
        def _():
            for j, chip in enumerate(chips):
                copy(1 + j, (*chip, c), me).wait_recv()
                passed[j].start()

        @pl.when(step == nsteps - 1)
        def _():
            copy(0, sibling, me).wait_recv()
            for j, chip in enumerate(chips):
                copy(4 + j, (*chip, 1 - c), me).wait_recv()
            for cp in first + passed:
                cp.wait_send()
            mine.wait()

    return pl.pallas_call(
        body, name=name, grid=(n // tn, gj),
        out_shape=[jax.ShapeDtypeStruct((m, n), out_dtype), jax.ShapeDtypeStruct((N_DEV,) + shard.shape, shard.dtype)],
        in_specs=[pl.BlockSpec(memory_space=pltpu.VMEM), pl.BlockSpec((tn, kdim), lambda j, i: (j, 0)),
                  pl.BlockSpec(memory_space=pl.ANY)],
        out_specs=[pl.BlockSpec((tm, tn), lambda j, i: (i, j)), pl.BlockSpec(memory_space=pl.ANY)],
        scratch_shapes=[pltpu.SemaphoreType.DMA((N_DEV - 1,)), pltpu.SemaphoreType.DMA((N_DEV - 1,)),
                        pltpu.SemaphoreType.DMA],
        compiler_params=_cparams(("arbitrary", "arbitrary")),
    )(a, b, shard)


def mm_tn(a, b, name):
    tt, tn = min(MM_TOKEN_TILE, b.shape[0]), _lane_tile(b.shape[1], MM_COL_TILE)
    tka = _lane_tile(a.shape[-1], 1024)
    if a.ndim == 3:
        t, ka = a.shape[1], a.shape[0] * a.shape[2]
        per = a.shape[2] // tka
        a_spec = pl.BlockSpec((None, tt, tka), lambda i, j, s: (i // per, s, i % per))
    else:
        t, ka = a.shape
        a_spec = pl.BlockSpec((tt, tka), lambda i, j, s: (s, i))
    n = b.shape[1]
    nt = t // tt

    def body(a_ref, b_ref, o_ref, *acc):
        p = _tn(a_ref[...], b_ref[...])
        if nt == 1:
            o_ref[...] = p.astype(o_ref.dtype)
        else:
            acc_ref, s = acc[0], pl.program_id(2)

            @pl.when(s == 0)
            def _():
                acc_ref[...] = p

            @pl.when(s > 0)
            def _():
                acc_ref[...] += p

            @pl.when(s == nt - 1)
            def _():
                o_ref[...] = acc_ref[...].astype(o_ref.dtype)

    return pl.pallas_call(
        body, name=name, grid=(ka // tka, n // tn, nt),
        out_shape=jax.ShapeDtypeStruct((ka, n), BF16),
        in_specs=[a_spec, pl.BlockSpec((tt, tn), lambda i, j, s: (s, j))],
        out_specs=pl.BlockSpec((tka, tn), lambda i, j, s: (i, j)),
        scratch_shapes=[] if nt == 1 else [pltpu.VMEM((tka, tn), F32)],
        compiler_params=_cparams(("parallel", "parallel", "arbitrary")),
    )(a, b)


def _tile(t, cap):
    return min(cap, t)


def ln_modulate(x, mod6, shift_row, scale_row, name):
    t = x.shape[0]
    tm = _tile(t, WIDE_ROW_TILE)

    def body(x_ref, mod_ref, o_ref):
        xh, _ = _ln(x_ref[...])
        sc = mod_ref[scale_row:scale_row + 1, :]
        sh = mod_ref[shift_row:shift_row + 1, :]
        o_ref[...] = (xh * (1.0 + sc) + sh).astype(BF16)

    return pl.pallas_call(
        body, name=name, grid=(t // tm,),
        out_shape=jax.ShapeDtypeStruct((t, D), BF16),
        in_specs=[pl.BlockSpec((tm, D), lambda i: (i, 0)), pl.BlockSpec((6, D), lambda i: (0, 0))],
        out_specs=pl.BlockSpec((tm, D), lambda i: (i, 0)),
        compiler_params=_cparams(("parallel",)),
    )(x, mod6)


def resid_ln_bwd(x, h, mod6, gate_row, ln_g, ln_b, cot, with_loss, name):
    t = x.shape[0]
    tm = _tile(t, ROW_TILE)
    h_is_product = isinstance(h, tuple)

    def body(x_ref, *refs):
        if h_is_product:
            a_ref, w_ref, mod_ref, g_ref, b_ref, c_ref, dh_ref, dx_ref, acc_ref = refs
            hv = _nn(a_ref[...], w_ref[...])
        else:
            h_ref, mod_ref, g_ref, b_ref, c_ref, dh_ref, dx_ref, acc_ref = refs
            hv = h_ref[...]

        @pl.when(pl.program_id(0) == 0)
        def _():
            acc_ref[...] = jnp.zeros_like(acc_ref)

        gate = mod_ref[gate_row:gate_row + 1, :]
        r = ALPHA * x_ref[...] + gate * hv
        rh, rstd = _ln(r)
        lng = g_ref[...]
        if with_loss:
            diff = rh * lng + b_ref[...] - c_ref[...]
            dxo = diff * (1.0 / D)
            lsum = jnp.sum(_colsum(diff * diff), axis=-1, keepdims=True) * (0.5 / D)
            acc_ref[3:4, :] += jnp.broadcast_to(lsum, (1, D))
        else:
            dxo = c_ref[...]
        acc_ref[1:2, :] += _colsum(dxo * rh)
        acc_ref[2:3, :] += _colsum(dxo)
        dr = _ln_bwd(dxo * lng, rh, rstd)
        acc_ref[0:1, :] += _colsum(dr * hv)
        dh_ref[...] = (gate * dr).astype(BF16)
        dx_ref[...] = ALPHA * dr

    row = pl.BlockSpec((tm, D), lambda i: (i, 0))
    vec = pl.BlockSpec((1, D), lambda i: (0, 0))
    if h_is_product:
        kdim = h[0].shape[1]
        h_args = list(h)
        h_specs = [pl.BlockSpec((tm, kdim), lambda i: (i, 0)), pl.BlockSpec((kdim, D), lambda i: (0, 0))]
    else:
        h_args, h_specs = [h], [row]
    return pl.pallas_call(
        body, name=name, grid=(t // tm,),
        out_shape=[jax.ShapeDtypeStruct((t, D), BF16), jax.ShapeDtypeStruct((t, D), F32),
                   jax.ShapeDtypeStruct((8, D), F32)],
        in_specs=[row] + h_specs + [pl.BlockSpec((6, D), lambda i: (0, 0)), vec, vec, row],
        out_specs=[row, row, pl.BlockSpec((8, D), lambda i: (0, 0))],
        compiler_params=_cparams(("arbitrary",)),
    )(x, *h_args, mod6, ln_g, ln_b, cot)


def ln_modulate_bwd(x, du, mod6, scale_row, dx_part, name):
    t = x.shape[0]
    tm = _tile(t, ROW_TILE)
    du_is_product = isinstance(du, tuple)

    def body(x_ref, *refs):
        if du_is_product:
            a_ref, w_ref, mod_ref, dp_ref, dx_ref, acc_ref = refs
            kb = a_ref.shape[2]
            du_v = _nn(a_ref[0], w_ref[0:kb, :])
            for blk in range(1, a_ref.shape[0]):
                du_v = du_v + _nn(a_ref[blk], w_ref[blk * kb:(blk + 1) * kb, :])
        else:
            du_ref, mod_ref, dp_ref, dx_ref, acc_ref = refs
            du_v = du_ref[...]

        @pl.when(pl.program_id(0) == 0)
        def _():
            acc_ref[...] = jnp.zeros_like(acc_ref)

        xh, rstd = _ln(x_ref[...])
        sc = mod_ref[scale_row:scale_row + 1, :]
        acc_ref[0:1, :] += _colsum(du_v * xh)
        acc_ref[1:2, :] += _colsum(du_v)
        dx_ref[...] = dp_ref[...] + _ln_bwd(du_v * (1.0 + sc), xh, rstd)

    row = pl.BlockSpec((tm, D), lambda i: (i, 0))
    if du_is_product:
        nblk, _, kb = du[0].shape
        du_args = list(du)
        du_specs = [pl.BlockSpec((nblk, tm, kb), lambda i: (0, i, 0)), pl.BlockSpec(memory_space=pltpu.VMEM)]
    else:
        du_args, du_specs = [du], [row]
    return pl.pallas_call(
        body, name=name, grid=(t // tm,),
        out_shape=[jax.ShapeDtypeStruct((t, D), F32), jax.ShapeDtypeStruct((8, D), F32)],
        in_specs=[row] + du_specs + [pl.BlockSpec((6, D), lambda i: (0, 0)), row],
        out_specs=[row, pl.BlockSpec((8, D), lambda i: (0, 0))],
        compiler_params=_cparams(("arbitrary",)),
    )(x, *du_args, mod6, dx_part)


def mixer_tail(ya, yb, proj, w_o, x, mod6, ln_g, ln_b):
    t = ya.shape[0]
    tm = _tile(t, ROW_TILE)

    def body(ya_ref, yb_ref, ga_ref, gb_ref, w_ref, x_ref, mod_ref, g_ref, b_ref, m_ref, h_ref, x1_ref, u2_ref):
        merged = (_sigmoid(ga_ref[...]) * ya_ref[...].astype(F32) +
                  _sigmoid(gb_ref[...]) * yb_ref[...].astype(F32)).astype(BF16)
        m_ref[...] = merged
        hv = _nn(merged, w_ref[...])
        h_ref[...] = hv
        rh, _ = _ln(ALPHA * x_ref[...] + mod_ref[2:3, :] * hv)
        x1 = rh * g_ref[...] + b_ref[...]
        x1_ref[...] = x1
        xh, _ = _ln(x1)
        u2_ref[...] = (xh * (1.0 + mod_ref[4:5, :]) + mod_ref[3:4, :]).astype(BF16)

    row = pl.BlockSpec((tm, D), lambda i: (i, 0))
    vec = pl.BlockSpec((1, D), lambda i: (0, 0))
    return pl.pallas_call(
        body, name="mixer_tail", grid=(t // tm,),
        out_shape=[jax.ShapeDtypeStruct((t, D), BF16), jax.ShapeDtypeStruct((t, D), F32),
                   jax.ShapeDtypeStruct((t, D), F32), jax.ShapeDtypeStruct((t, D), BF16)],
        in_specs=[row, row, pl.BlockSpec((tm, D), lambda i: (i, GATE_BLOCK0)),
                  pl.BlockSpec((tm, D), lambda i: (i, GATE_BLOCK0 + 1)), pl.BlockSpec((D, D), lambda i: (0, 0)),
                  row, pl.BlockSpec((6, D), lambda i: (0, 0)), vec, vec],
        out_specs=[row, row, row, row],
        compiler_params=_cparams(("parallel",)),
    )(ya, yb, proj, proj, w_o, x, mod6, ln_g, ln_b)


def merge_gates_bwd(dh, w_o, ya, yb, proj):
    t = ya.shape[0]
    tm = _tile(t, ROW_TILE)

    def body(dh_ref, w_ref, ya_ref, yb_ref, ga_ref, gb_ref, dya_ref, dyb_ref, dp_ref):
        dmv = _nt(dh_ref[...], w_ref[...])
        sa = _sigmoid(ga_ref[...])
        sb = _sigmoid(gb_ref[...])
        dya_ref[...] = (dmv * sa).astype(BF16)
        dyb_ref[...] = (dmv * sb).astype(BF16)
        dp_ref[0] = (dmv * ya_ref[...].astype(F32) * sa * (1.0 - sa)).astype(BF16)
        dp_ref[1] = (dmv * yb_ref[...].astype(F32) * sb * (1.0 - sb)).astype(BF16)

    row = pl.BlockSpec((tm, D), lambda i: (i, 0))
    return pl.pallas_call(
        body, name="merge_gates_bwd", grid=(t // tm,),
        out_shape=[jax.ShapeDtypeStruct((t, D), BF16)] * 2 + [jax.ShapeDtypeStruct((N_PROJ // D, t, D), BF16)],
        in_specs=[row, pl.BlockSpec((D, D), lambda i: (0, 0)), row, row,
                  pl.BlockSpec((tm, D), lambda i: (i, GATE_BLOCK0)),
                  pl.BlockSpec((tm, D), lambda i: (i, GATE_BLOCK0 + 1))],
        out_specs=[row, row, pl.BlockSpec((2, tm, D), lambda i: (GATE_BLOCK0 // 2, i, 0))],
        compiler_params=_cparams(("parallel",)),
    )(dh, w_o, ya, yb, proj, proj)


FF_CHUNK = 1408


def ffn_in_act(u, w_gu_t):
    t = u.shape[0]
    tm = _tile(t, ROW_TILE)
    nj = D_FF // FF_CHUNK

    def body(a_ref, bg_ref, bu_ref, gu_ref, act_ref):
        a = a_ref[...]
        g = _nt(a, bg_ref[...])
        up = _nt(a, bu_ref[...])
        gu_ref[0] = g.astype(BF16)
        gu_ref[1] = up.astype(BF16)
        act_ref[...] = (g * _sigmoid(g) * up).astype(BF16)

    return pl.pallas_call(
        body, name="ffn_in_act", grid=(nj, t // tm),
        out_shape=[jax.ShapeDtypeStruct((2, t, D_FF), BF16), jax.ShapeDtypeStruct((t, D_FF), BF16)],
        in_specs=[pl.BlockSpec((tm, D), lambda j, i: (i, 0)), pl.BlockSpec((FF_CHUNK, D), lambda j, i: (j, 0)),
                  pl.BlockSpec((FF_CHUNK, D), lambda j, i: (nj + j, 0))],
        out_specs=[pl.BlockSpec((2, tm, FF_CHUNK), lambda j, i: (0, i, j)),
                   pl.BlockSpec((tm, FF_CHUNK), lambda j, i: (i, j))],
        compiler_params=_cparams(("parallel", "parallel")),
    )(u, w_gu_t, w_gu_t)


def ffn_act_bwd(dh, w_dn, gu):
    t = dh.shape[0]
    tm = _tile(t, ROW_TILE)

    def body(a_ref, b_ref, gu_ref, o_ref):
        da = _nt(a_ref[...], b_ref[...])
        g = gu_ref[0].astype(F32)
        up = gu_ref[1].astype(F32)
        s = _sigmoid(g)
        o_ref[0] = (da * up * _dsilu(g, s)).astype(BF16)
        o_ref[1] = (da * g * s).astype(BF16)

    blk = pl.BlockSpec((2, tm, FF_CHUNK), lambda j, i: (0, i, j))
    return pl.pallas_call(
        body, name="ffn_act_bwd", grid=(D_FF // FF_CHUNK, t // tm),
        out_shape=jax.ShapeDtypeStruct((2, t, D_FF), BF16),
        in_specs=[pl.BlockSpec((tm, D), lambda j, i: (i, 0)), pl.BlockSpec((FF_CHUNK, D), lambda j, i: (j, 0)), blk],
        out_specs=blk,
        compiler_params=_cparams(("parallel", "parallel")),
    )(dh, w_dn, gu)


def _hgrn_chunk_terms(q, fl, lbv, tril_f):
    sig = _sigmoid(fl)
    f = lbv + (1.0 - lbv) * sig
    lam = jnp.log(f)
    k = 1.0 - f
    sq = _sigmoid(q)
    qt = q * sq * Q_SCALE
    bc = _sel(_nn, lam, tril_f, 3, x_first=False)
    bmid = bc[CHUNK // 2 - 1:CHUNK // 2, :]
    bl = bc[CHUNK - 1:CHUNK, :]
    eq = jnp.exp(jnp.minimum(bc - bmid, EXP_CLIP))
    ek = jnp.exp(jnp.minimum(bmid - bc, EXP_CLIP))
    eb = jnp.exp(bc)
    ekl = jnp.exp(bl - bc)
    ebl = jnp.exp(bl)
    return sig, f, k, sq, qt, eq, ek, eb, ekl, ebl


def hgrn_fwd(proj, lb, gnorm):
    t = proj.shape[0]
    tb = _tile(t, TOKEN_BLOCK)
    ncb = tb // CHUNK

    hps = HGRN_HEADS_PER_STEP
    wide = hps * HK

    def body(q_ref, f_ref, i_ref, g_ref, lb_ref, gn_ref, oa_ref, oraw_ref, st_ref, state):
        @pl.when(pl.program_id(1) == 0)
        def _():
            state[...] = jnp.zeros_like(state)

        gn = gn_ref[...]
        mask = _tri(CHUNK)
        tril_f = mask.astype(BF16)

        def chunk(c, carry):
            sl = pl.ds(pl.multiple_of(c * CHUNK, CHUNK), CHUNK)
            for hh in range(hps):
                ln = slice(hh * HK, (hh + 1) * HK)
                q, fl, v, g = q_ref[sl, ln], f_ref[sl, ln], i_ref[sl, ln], g_ref[sl, ln]
                sig, f, k, sq, qt, eq, ek, eb, ekl, ebl = _hgrn_chunk_terms(q, fl, lb_ref[:, ln], tril_f)
                a = jnp.where(mask, _nt((qt * eq).astype(BF16), (k * ek).astype(BF16)), 0.0)
                st = state[hh]
                st_ref[hh, c] = st
                vb = v.astype(BF16)
                o = _nn(a.astype(BF16), vb) + _nt((qt * eb).astype(BF16), st.astype(BF16))
                state[hh] = st * ebl + _tn(vb, (k * ekl).astype(BF16))
                oraw_ref[sl, ln] = o
                rn = o * lax.rsqrt(jnp.mean(o * o, axis=-1, keepdims=True) + RMS_EPS)
                oa_ref[sl, ln] = (rn * gn * g * _sigmoid(g)).astype(BF16)
            return carry

        lax.fori_loop(0, ncb, chunk, 0, unroll=min(CHUNK_UNROLL, ncb))

    def col(block):
        return pl.BlockSpec((tb, wide), lambda h, j: (j, block * (N_HEADS_A // hps) + h))

    return pl.pallas_call(
        body, name="hgrn_fwd", grid=(N_HEADS_A // hps, t // tb),
        out_shape=[jax.ShapeDtypeStruct((t, D), BF16), jax.ShapeDtypeStruct((t, D), F32),
                   jax.ShapeDtypeStruct((N_HEADS_A, t // CHUNK, HK, HK), F32)],
        in_specs=[col(0), col(1), col(2), col(3), pl.BlockSpec((1, wide), lambda h, j: (0, h)),
                  pl.BlockSpec((1, HK), lambda h, j: (0, 0))],
        out_specs=[pl.BlockSpec((tb, wide), lambda h, j: (j, h)), pl.BlockSpec((tb, wide), lambda h, j: (j, h)),
                   pl.BlockSpec((hps, ncb, HK, HK), lambda h, j: (h, j, 0, 0))],
        scratch_shapes=[pltpu.VMEM((hps, HK, HK), F32)],
        compiler_params=_cparams(("parallel", "arbitrary")),
    )(proj, proj, proj, proj, lb, gnorm)


def hgrn_bwd(proj, lb, gnorm, o_raw, dya, w_ba, states, give, dproj):
    t = proj.shape[0]
    tb = _tile(t, TOKEN_BLOCK)
    ncb = tb // CHUNK
    nb = t // tb
    hps = HGRN_HEADS_PER_STEP
    wide = hps * HK

    def body(q_ref, f_ref, i_ref, g_ref, lb_ref, gn_ref, oraw_ref, dya_ref, wba_ref, st_ref, give_ref, dp_in_ref,
             dp_ref, dlb_ref, dgn_ref, got_ref, dstate, doa_ref, send_sem, recv_sem):
        h, j = pl.program_id(0), pl.program_id(1)
        swap_start, swap_wait = _sibling_exchange(give_ref, got_ref, send_sem, recv_sem)
        doa_ref[...] = _nt(dya_ref[...], wba_ref[...])

        @pl.when((h == 0) & (j == 0))
        def _():
            swap_start()

        @pl.when(j == 0)
        def _():
            dstate[...] = jnp.zeros_like(dstate)
            dlb_ref[...] = jnp.zeros_like(dlb_ref)

        @pl.when((j == 0) & (h == 0))
        def _():
            dgn_ref[...] = jnp.zeros_like(dgn_ref)

        gn = gn_ref[...]
        mask = _tri(CHUNK)
        mask_t = _tri(CHUNK, upper=True)
        tril_f = mask.astype(BF16)
        triu_f = mask_t.astype(BF16)

        def chunk(i, c0):
            c = ncb - 1 - i
            sl = pl.ds(pl.multiple_of(c * CHUNK, CHUNK), CHUNK)
            for hh in range(hps):
                ln = slice(hh * HK, (hh + 1) * HK)
                q, fl, v, g = q_ref[sl, ln], f_ref[sl, ln], i_ref[sl, ln], g_ref[sl, ln]
                lbv = lb_ref[:, ln]
                sig, f, k, sq, qt, eq, ek, eb, ekl, ebl = _hgrn_chunk_terms(q, fl, lbv, tril_f)
                qe = (qt * eq).astype(BF16)
                ke = (k * ek).astype(BF16)
                st32 = st_ref[hh, c]
                st = st32.astype(BF16)
                dst = dstate[hh]
                dstb = dst.astype(BF16)
                o = oraw_ref[sl, ln]
                rstd = lax.rsqrt(jnp.mean(o * o, axis=-1, keepdims=True) + RMS_EPS)
                rn = o * rstd
                sgm = _sigmoid(g)
                sg = g * sgm
                doa_v = doa_ref[sl, ln]
                drn = doa_v * gn * sg
                dgn_ref[...] += _colsum(doa_v * rn * sg)
                dp_ref[3, sl, ln] = (doa_v * rn * gn * _dsilu(g, sgm)).astype(BF16)
                do = rstd * (drn - rn * jnp.mean(drn * rn, axis=-1, keepdims=True))
                dob = do.astype(BF16)
                vb = v.astype(BF16)
                da = jnp.where(mask, _nt(dob, vb), 0.0).astype(BF16)
                da_t = jnp.where(mask_t, _nt(vb, dob), 0.0).astype(BF16)
                a_t = jnp.where(mask_t, _nt(ke, qe), 0.0).astype(BF16)
                kl = (k * ekl).astype(BF16)
                qb = (qt * eb).astype(BF16)
                dq_in = _nn(da, ke)
                dk_in = _nn(da_t, qe)
                dq_out = eb * _nn(dob, st)
                dk_out = ekl * _nn(vb, dstb)
                dqt = eq * dq_in + dq_out
                dk = ek * dk_in + dk_out
                dv = _nn(a_t, dob) + _nt(kl, dstb)
                dstate[hh] = dst * ebl + _tn(dob, qb)
                dbig = qe.astype(F32) * dq_in - ke.astype(F32) * dk_in + qt * dq_out - k * dk_out
                beyond = _colsum(k * dk_out) + ebl * _colsum(dst * st32)
                dlam = _sel(_nn, dbig, triu_f, 3, x_first=False) + beyond
                df = dlam / f - dk
                dp_ref[1, sl, ln] = (df * (1.0 - lbv) * sig * (1.0 - sig)).astype(BF16)
                dlb_ref[:, ln] += _colsum(df * (1.0 - sig))
                dp_ref[0, sl, ln] = (dqt * Q_SCALE * _dsilu(q, sq)).astype(BF16)
                dp_ref[2, sl, ln] = dv.astype(BF16)
            return c0

        lax.fori_loop(0, ncb, chunk, 0, unroll=min(CHUNK_UNROLL, ncb))

        @pl.when((h == N_HEADS_A // hps - 1) & (j == nb - 1))
        def _():
            swap_wait()

    def col(block):
        return pl.BlockSpec((tb, wide), lambda h, j: (nb - 1 - j, block * (N_HEADS_A // hps) + h))

    hcol = pl.BlockSpec((tb, wide), lambda h, j: (nb - 1 - j, h))
    hbm = pl.BlockSpec(memory_space=pl.ANY)
    return pl.pallas_call(
        body, name="hgrn_bwd", grid=(N_HEADS_A // hps, nb),
        out_shape=[jax.ShapeDtypeStruct(dproj.shape, dproj.dtype), jax.ShapeDtypeStruct((1, D), F32),
                   jax.ShapeDtypeStruct((1, HK), F32), jax.ShapeDtypeStruct(give.shape, give.dtype)],
        in_specs=[col(0), col(1), col(2), col(3), pl.BlockSpec((1, wide), lambda h, j: (0, h)),
                  pl.BlockSpec((1, HK), lambda h, j: (0, 0)), hcol,
                  pl.BlockSpec((tb, D), lambda h, j: (nb - 1 - j, 0)), pl.BlockSpec((wide, D), lambda h, j: (h, 0)),
                  pl.BlockSpec((hps, ncb, HK, HK), lambda h, j: (h, nb - 1 - j, 0, 0)), hbm, hbm],
        out_specs=[pl.BlockSpec((4, tb, wide), lambda h, j: (0, nb - 1 - j, h)),
                   pl.BlockSpec((1, wide), lambda h, j: (0, h)), pl.BlockSpec((1, HK), lambda h, j: (0, 0)), hbm],
        input_output_aliases={11: 0},
        scratch_shapes=[pltpu.VMEM((hps, HK, HK), F32), pltpu.VMEM((tb, wide), F32)] + SIBLING_SEMS,
        compiler_params=_cparams(("arbitrary", "arbitrary")),
    )(proj, proj, proj, proj, lb, gnorm, o_raw, dya, w_ba, states, give, dproj)


CONV_BLOCK0 = 6
CONV_TAPS = 4
HALO = 8


def conv_fwd(proj, conv_w, conv_b):
    t = proj.shape[0]
    tm = _tile(t, ROW_TILE)
    r = tm // HALO

    def body(x_ref, halo_ref, w_ref, b_ref, o_ref, ds_ref):
        i = pl.program_id(1)
        halo = jnp.where(i > 0, halo_ref[...], 0.0)
        ext = jnp.concatenate([halo, x_ref[...]], axis=0)
        pre = b_ref[...] + w_ref[CONV_TAPS - 1:CONV_TAPS, :] * ext[HALO:, :]
        for tap in range(CONV_TAPS - 1):
            pre = pre + w_ref[tap:tap + 1, :] * pltpu.roll(ext, CONV_TAPS - 1 - tap, axis=0)[HALO:, :]
        s = _sigmoid(pre)
        o_ref[...] = pre * s
        ds_ref[...] = _dsilu(pre, s).astype(BF16)

    blk = pl.BlockSpec((tm, D), lambda cb, i: (i, cb))
    return pl.pallas_call(
        body, name="conv_fwd", grid=(CONV_DIM // D, t // tm),
        out_shape=[jax.ShapeDtypeStruct((t, CONV_DIM), F32), jax.ShapeDtypeStruct((t, CONV_DIM), BF16)],
        in_specs=[pl.BlockSpec((tm, D), lambda cb, i: (i, CONV_BLOCK0 + cb)),
                  pl.BlockSpec((HALO, D), lambda cb, i: (jnp.maximum(i * r - 1, 0), CONV_BLOCK0 + cb)),
                  pl.BlockSpec((CONV_TAPS, D), lambda cb, i: (0, cb)), pl.BlockSpec((1, D), lambda cb, i: (0, cb))],
        out_specs=[blk, blk],
        compiler_params=_cparams(("parallel", "parallel")),
    )(proj, proj, conv_w, conv_b)


def conv_bwd(proj, dxc, dsilu, conv_w, dproj):
    t = proj.shape[0]
    tm = _tile(t, ROW_TILE)
    r = tm // HALO
    n = t // tm
    last_halo = t // HALO - 1

    def body(x_ref, prev_ref, d_ref, dnext_ref, s_ref, snext_ref, w_ref, dp_in_ref, dx_ref, dw_ref, db_ref):
        i = pl.program_id(1)

        @pl.when(i == 0)
        def _():
            dw_ref[...] = jnp.zeros_like(dw_ref)
            db_ref[...] = jnp.zeros_like(db_ref)

        dpre = jnp.concatenate([d_ref[...].astype(F32) * s_ref[...].astype(F32),
                                jnp.where(i < n - 1, dnext_ref[0:HALO, :].astype(F32) * snext_ref[0:HALO, :].astype(F32),
                                          0.0)], axis=0)
        dx = w_ref[CONV_TAPS - 1:CONV_TAPS, :] * dpre[:tm, :]
        for tap in range(CONV_TAPS - 1):
            back = CONV_TAPS - 1 - tap
            dx = dx + w_ref[tap:tap + 1, :] * pltpu.roll(dpre, tm + HALO - back, axis=0)[:tm, :]
        dx_ref[...] = dx.astype(BF16)
        dp = dpre[:tm, :]
        db_ref[...] += _colsum(dp)
        prev = jnp.where(i > 0, prev_ref[...], 0.0)
        ext = jnp.concatenate([prev, x_ref[...]], axis=0)
        dw_ref[CONV_TAPS - 1:CONV_TAPS, :] += _colsum(dp * ext[HALO:, :])
        for tap in range(CONV_TAPS - 1):
            dw_ref[tap:tap + 1, :] += _colsum(dp * pltpu.roll(ext, CONV_TAPS - 1 - tap, axis=0)[HALO:, :])

    blk = pl.BlockSpec((tm, D), lambda cb, i: (i, cb))
    nxt = pl.BlockSpec((2 * HALO, D), lambda cb, i: (jnp.minimum((i + 1) * (r // 2), last_halo // 2), cb))
    return pl.pallas_call(
        body, name="conv_bwd", grid=(CONV_DIM // D, n),
        out_shape=[jax.ShapeDtypeStruct(dproj.shape, dproj.dtype), jax.ShapeDtypeStruct((8, CONV_DIM), F32),
                   jax.ShapeDtypeStruct((1, CONV_DIM), F32)],
        in_specs=[pl.BlockSpec((tm, D), lambda cb, i: (i, CONV_BLOCK0 + cb)),
                  pl.BlockSpec((HALO, D), lambda cb, i: (jnp.maximum(i * r - 1, 0), CONV_BLOCK0 + cb)),
                  blk, nxt, blk, nxt,
                  pl.BlockSpec((CONV_TAPS, D), lambda cb, i: (0, cb)), pl.BlockSpec(memory_space=pl.ANY)],
        out_specs=[pl.BlockSpec((None, tm, D), lambda cb, i: (CONV_BLOCK0 + cb, i, 0)),
                   pl.BlockSpec((8, D), lambda cb, i: (0, cb)), pl.BlockSpec((1, D), lambda cb, i: (0, cb))],
        input_output_aliases={7: 0},
        compiler_params=_cparams(("parallel", "arbitrary")),
    )(proj, proj, dxc, dxc, dsilu, dsilu, conv_w, dproj)


def dt_fill(ddt, dproj):
    t = ddt.shape[0]
    tm = _tile(t, WIDE_ROW_TILE)
    w = ddt.shape[1]

    def body(d_ref, dp_in_ref, o_ref):
        o_ref[:, :w] = d_ref[...]
        o_ref[:, w:] = jnp.zeros((tm, D - w), o_ref.dtype)

    return pl.pallas_call(
        body, name="dt_fill", grid=(t // tm,),
        out_shape=jax.ShapeDtypeStruct(dproj.shape, dproj.dtype),
        in_specs=[pl.BlockSpec((tm, w), lambda i: (i, 0)), pl.BlockSpec(memory_space=pl.ANY)],
        out_specs=pl.BlockSpec((None, tm, D), lambda i: (DT_COL_BLOCK, i, 0)),
        input_output_aliases={1: 0},
        compiler_params=_cparams(("parallel",)),
    )(ddt, dproj)


Z_BLOCK0 = 8
DT_COL_BLOCK = 9
DT_BLOCK0 = 8 * DT_COL_BLOCK
GATE_BLOCK0 = 10
B_BLOCK0 = 16
C_BLOCK0 = 20


def _head_expand():
    e = np.zeros((N_STATE, GROUP_W), np.float32)
    for hh in range(HEADS_PER_GROUP):
        e[hh, hh * HEAD_P:(hh + 1) * HEAD_P] = 1.0
    return jnp.asarray(e, BF16)


def _ssd_chunk_terms(dt, bias, alog, expand, tril_f, eye):
    dtb = dt + bias
    delta = jnp.maximum(dtb, 0.0) + jnp.log(1.0 + jnp.exp(-jnp.abs(dtb)))
    ea = jnp.exp(alog)
    a = -ea * delta
    acum = _sel(_nn, a, tril_f, 3, x_first=False)
    delta_e = _sel(_nn, delta, expand, 2)
    acum_e = _sel(_nn, acum, expand, 2)
    acum_t = _sel(_nt, acum, eye, 3, x_first=False)
    return dtb, delta, ea, a, acum, delta_e, acum_e, acum_t


def ssd_fwd(proj, xc, alog4, bias4, dskip4, wnorm, expand):
    t = proj.shape[0]
    tb = _tile(t, TOKEN_BLOCK)
    ncb = tb // SSD_CHUNK

    def body(xs_ref, b_ref, c_ref, dt_ref, z_ref, alog_ref, bias_ref, dsk_ref, wn_ref, e_ref, ob_ref, st_ref, state):
        @pl.when(pl.program_id(1) == 0)
        def _():
            state[...] = jnp.zeros_like(state)

        expand = e_ref[...]
        mask = _tri(SSD_CHUNK)
        tril_f = mask.astype(BF16)
        eye = (lax.broadcasted_iota(jnp.int32, (N_STATE, N_STATE), 0) ==
               lax.broadcasted_iota(jnp.int32, (N_STATE, N_STATE), 1)).astype(BF16)
        alog, bias = alog_ref[0], bias_ref[0]
        d_e = _sel(_nn, jnp.broadcast_to(dsk_ref[0], (8, N_STATE)), expand, 3)[0:1, :]
        wn = wn_ref[...]

        def chunk(c, carry):
            sl = pl.ds(pl.multiple_of(c * SSD_CHUNK, SSD_CHUNK), SSD_CHUNK)
            xs, bm, cm, dt, z = xs_ref[sl, :], b_ref[sl, :], c_ref[sl, :], dt_ref[sl, :], z_ref[sl, :]
            dtb, delta, ea, a, acum, delta_e, acum_e, acum_t = _ssd_chunk_terms(dt, bias, alog, expand, tril_f, eye)
            alast_e = acum_e[SSD_CHUNK - 1:SSD_CHUNK, :]
            xd = xs * delta_e
            xdb = xd.astype(BF16)
            cb_, bb_ = cm.astype(BF16), bm.astype(BF16)
            cbm = _nt(cb_, bb_)
            ys = []
            for hh in range(HEADS_PER_GROUP):
                lh = jnp.where(mask, jnp.exp(jnp.minimum(acum[:, hh:hh + 1] - acum_t[hh:hh + 1, :], 0.0)), 0.0)
                ys.append(_nn((cbm * lh).astype(BF16), xdb[:, hh * HEAD_P:(hh + 1) * HEAD_P]))
            st = state[...]
            st_ref[0, c] = st
            y = jnp.concatenate(ys, axis=1) + _nn(cb_, st.astype(BF16)) * jnp.exp(acum_e) + xs * d_e
            state[...] = st * jnp.exp(alast_e) + _tn(bb_, (xd * jnp.exp(alast_e - acum_e)).astype(BF16))
            yg = y * z * _sigmoid(z)
            ob_ref[sl, :] = (yg * lax.rsqrt(jnp.mean(yg * yg, axis=-1, keepdims=True) + RMS_EPS) * wn).astype(BF16)
            return carry

        lax.fori_loop(0, ncb, chunk, 0, unroll=min(CHUNK_UNROLL, ncb))

    small = pl.BlockSpec((1, 1, N_STATE), lambda g, j: (g, 0, 0))
    return pl.pallas_call(
        body, name="ssd_fwd", grid=(N_GROUPS, t // tb),
        out_shape=[jax.ShapeDtypeStruct((t, B_INNER), BF16),
                   jax.ShapeDtypeStruct((N_GROUPS, t // SSD_CHUNK, N_STATE, GROUP_W), F32)],
        in_specs=[pl.BlockSpec((tb, GROUP_W), lambda g, j: (j, g)),
                  pl.BlockSpec((tb, N_STATE), lambda g, j: (j, B_BLOCK0 + g)),
                  pl.BlockSpec((tb, N_STATE), lambda g, j: (j, C_BLOCK0 + g)),
                  pl.BlockSpec((tb, N_STATE), lambda g, j: (j, DT_BLOCK0 + g)),
                  pl.BlockSpec((tb, GROUP_W), lambda g, j: (j, Z_BLOCK0 + g)),
                  small, small, small, pl.BlockSpec((1, GROUP_W), lambda g, j: (0, g)),
                  pl.BlockSpec((N_STATE, GROUP_W), lambda g, j: (0, 0))],
        out_specs=[pl.BlockSpec((tb, GROUP_W), lambda g, j: (j, g)),
                   pl.BlockSpec((1, ncb, N_STATE, GROUP_W), lambda g, j: (g, j, 0, 0))],
        scratch_shapes=[pltpu.VMEM((N_STATE, GROUP_W), F32)],
        compiler_params=_cparams(("parallel", "arbitrary")),
    )(xc, xc, xc, proj, proj, alog4, bias4, dskip4, wnorm, expand)


def ssd_bwd(proj, xc, alog4, bias4, dskip4, wnorm, expand, dyb, w_bb, states, part, dproj):
    t = proj.shape[0]
    tb = _tile(t, TOKEN_BLOCK)
    lc = min(SSD_CHUNK_BWD, tb)
    ncb = tb // lc
    nsaved = tb // SSD_CHUNK
    nb = t // tb

    def body(xs_ref, b_ref, c_ref, dt_ref, z_ref, alog_ref, bias_ref, dsk_ref, wn_ref, e_ref, dyb_ref, wbb_ref, st_ref,
             part_ref, dp_in_ref, dxs_ref, db_ref, dc_ref, dz_ref, ddt_ref, dwn_ref, dalog_ref, dbias_ref, ddsk_ref,
             parts_ref, dstate, dob_ref, send_sems, recv_sems, local_sem):
        xchg_start, xchg_wait = _chip_exchange(part_ref, parts_ref, send_sems, recv_sems, local_sem)
        dob_ref[...] = _nt(dyb_ref[...], wbb_ref[...])

        @pl.when((pl.program_id(0) == 0) & (pl.program_id(1) == 0))
        def _():
            xchg_start()

        @pl.when(pl.program_id(1) == 0)
        def _():
            dstate[...] = jnp.zeros_like(dstate)
            dwn_ref[...] = jnp.zeros_like(dwn_ref)
            dalog_ref[...] = jnp.zeros_like(dalog_ref)
            dbias_ref[...] = jnp.zeros_like(dbias_ref)
            ddsk_ref[...] = jnp.zeros_like(ddsk_ref)

        expand = e_ref[...]
        mask = _tri(lc)
        mask_t = _tri(lc, upper=True)
        tril_f = mask.astype(BF16)
        triu_f = mask_t.astype(BF16)
        eye = (lax.broadcasted_iota(jnp.int32, (N_STATE, N_STATE), 0) ==
               lax.broadcasted_iota(jnp.int32, (N_STATE, N_STATE), 1)).astype(BF16)
        alog, bias = alog_ref[0], bias_ref[0]
        d_e = _sel(_nn, jnp.broadcast_to(dsk_ref[0], (8, N_STATE)), expand, 3)[0:1, :]
        wn = wn_ref[...]

        def chunk(i, c0):
            c = ncb - 1 - i
            sl = pl.ds(pl.multiple_of(c * lc, lc), lc)
            xs, bm, cm, dt, z = xs_ref[sl, :], b_ref[sl, :], c_ref[sl, :], dt_ref[sl, :], z_ref[sl, :]
            dtb, delta, ea, a, acum, delta_e, acum_e, acum_t = _ssd_chunk_terms(dt, bias, alog, expand, tril_f, eye)
            alast_e = acum_e[lc - 1:lc, :]
            eacum = jnp.exp(acum_e)
            wl = jnp.exp(alast_e - acum_e)
            xd = xs * delta_e
            xdb = xd.astype(BF16)
            cb_, bb_ = cm.astype(BF16), bm.astype(BF16)
            cbm = _nt(cb_, bb_)
            st32 = st_ref[0, c * (lc // SSD_CHUNK)]
            stb = st32.astype(BF16)
            dst = dstate[...]
            dstb = dst.astype(BF16)
            lhs, mixes, ys = [], [], []
            for hh in range(HEADS_PER_GROUP):
                col, row = acum[:, hh:hh + 1], acum_t[hh:hh + 1, :]
                lh = jnp.where(mask, jnp.exp(jnp.minimum(col - row, 0.0)), 0.0)
                mix = (cbm * lh).astype(BF16)
                lhs.append(lh)
                mixes.append(mix)
                ys.append(_nn(mix, xdb[:, hh * HEAD_P:(hh + 1) * HEAD_P]))
            y_in = jnp.concatenate(ys, axis=1)
            y_out = _nn(cb_, stb) * eacum
            y = y_in + y_out + xs * d_e
            sgz = _sigmoid(z)
            sz = z * sgz
            yg = y * sz
            rstd = lax.rsqrt(jnp.mean(yg * yg, axis=-1, keepdims=True) + RMS_EPS)
            nrm = yg * rstd
            dob_v = dob_ref[sl, :]
            dn = dob_v * wn
            dwn_ref[...] += _colsum(dob_v * nrm)
            dyg = rstd * (dn - nrm * jnp.mean(dn * nrm, axis=-1, keepdims=True))
            dy = dyg * sz
            dz_ref[sl, :] = (dyg * y * _dsilu(z, sgz)).astype(BF16)
            dyb = dy.astype(BF16)
            dxds = []
            dcb = jnp.zeros((lc, lc), F32)
            for hh in range(HEADS_PER_GROUP):
                hs = slice(hh * HEAD_P, (hh + 1) * HEAD_P)
                dy_h, x_h = dyb[:, hs], xdb[:, hs]
                dxds.append(_tn(mixes[hh], dy_h))
                dcb = dcb + _nt(dy_h, x_h) * lhs[hh]
            dcbb = dcb.astype(BF16)
            dye = (dy * eacum).astype(BF16)
            xw = (xd * wl).astype(BF16)
            dxd_in = jnp.concatenate(dxds, axis=1)
            dxd_out = wl * _nn(bb_, dstb)
            dxd = dxd_in + dxd_out
            dc_ref[sl, :] = (_nn(dcbb, bb_) + _nt(dye, stb)).astype(dc_ref.dtype)
            db_ref[sl, :] = (_tn(dcbb, cb_) + _nt(xw, dstb)).astype(db_ref.dtype)
            dstate[...] = dst * jnp.exp(alast_e) + _tn(cb_, dye)
            col_out = xd * dxd_out
            dac = _sel(_nt, dyb.astype(F32) * y_in - xdb.astype(F32) * dxd_in + dy * y_out - col_out, expand, 2)
            beyond = _colsum(col_out) + jnp.exp(alast_e) * _colsum(dst * st32)
            da = (_sel(_nn, dac, triu_f, 3, x_first=False) +
                  _sel(_nt, jnp.broadcast_to(beyond, (8, GROUP_W)), expand, 3)[0:1, :])
            ddelta = _sel(_nt, dxd * xs, expand, 2) - da * ea
            dalog_ref[0] += _colsum(da * a)
            ddtb = ddelta * _sigmoid(dtb)
            dbias_ref[0] += _colsum(ddtb)
            ddt_ref[sl, :] = ddtb.astype(BF16)
            ddsk_ref[0] += _sel(_nt, jnp.broadcast_to(_colsum(dy * xs), (8, GROUP_W)), expand, 3)[0:1, :]
            dxs_ref[sl, :] = (dxd * delta_e + dy * d_e).astype(dxs_ref.dtype)
            return c0

        lax.fori_loop(0, ncb, chunk, 0, unroll=min(CHUNK_UNROLL, ncb))

        @pl.when((pl.program_id(0) == N_GROUPS - 1) & (pl.program_id(1) == nb - 1))
        def _():
            xchg_wait()

    small = pl.BlockSpec((1, 1, N_STATE), lambda g, j: (g, 0, 0))
    wide = pl.BlockSpec((tb, GROUP_W), lambda g, j: (nb - 1 - j, g))
    narrow = pl.BlockSpec((tb, N_STATE), lambda g, j: (nb - 1 - j, g))
    hbm = pl.BlockSpec(memory_space=pl.ANY)
    return pl.pallas_call(
        body, name="ssd_bwd", grid=(N_GROUPS, nb),
        out_shape=[jax.ShapeDtypeStruct((t, B_INNER), BF16), jax.ShapeDtypeStruct((t, GROUP_W), BF16),
                   jax.ShapeDtypeStruct((t, GROUP_W), BF16), jax.ShapeDtypeStruct(dproj.shape, dproj.dtype),
                   jax.ShapeDtypeStruct((t, GROUP_W), BF16), jax.ShapeDtypeStruct((1, B_INNER), F32),
                   jax.ShapeDtypeStruct((N_GROUPS, 1, N_STATE), F32), jax.ShapeDtypeStruct((N_GROUPS, 1, N_STATE), F32),
                   jax.ShapeDtypeStruct((N_GROUPS, 1, N_STATE), F32), jax.ShapeDtypeStruct(part.shape, part.dtype)],
        in_specs=[wide,
                  pl.BlockSpec((tb, N_STATE), lambda g, j: (nb - 1 - j, B_BLOCK0 + g)),
                  pl.BlockSpec((tb, N_STATE), lambda g, j: (nb - 1 - j, C_BLOCK0 + g)),
                  pl.BlockSpec((tb, N_STATE), lambda g, j: (nb - 1 - j, DT_BLOCK0 + g)),
                  pl.BlockSpec((tb, GROUP_W), lambda g, j: (nb - 1 - j, Z_BLOCK0 + g)),
                  small, small, small, pl.BlockSpec((1, GROUP_W), lambda g, j: (0, g)),
                  pl.BlockSpec((N_STATE, GROUP_W), lambda g, j: (0, 0)),
                  pl.BlockSpec((tb, D), lambda g, j: (nb - 1 - j, 0)), pl.BlockSpec((GROUP_W, D), lambda g, j: (g, 0)),
                  pl.BlockSpec((1, nsaved, N_STATE, GROUP_W), lambda g, j: (g, nb - 1 - j, 0, 0)), hbm, hbm],
        out_specs=[wide, narrow, narrow,
                   pl.BlockSpec((None, tb, GROUP_W), lambda g, j: (Z_BLOCK0 // 2 + g // 2, nb - 1 - j, g % 2)),
                   narrow, pl.BlockSpec((1, GROUP_W), lambda g, j: (0, g)), small, small, small, hbm],
        input_output_aliases={14: 3},
        scratch_shapes=[pltpu.VMEM((N_STATE, GROUP_W), F32), pltpu.VMEM((tb, GROUP_W), F32)] + CHIP_SEMS,
        compiler_params=_cparams(("arbitrary", "arbitrary")),
    )(xc, xc, xc, proj, proj, alog4, bias4, dskip4, wnorm, expand, dyb, w_bb, states, part, dproj)


def lower_bound_fwd(hgrn_lb):
    def body(a_ref, o_ref):
        a0, a1 = a_ref[0:1, :], a_ref[1:2, :]
        m = jnp.maximum(a0, a1)
        e0, e1 = jnp.exp(a0 - m), jnp.exp(a1 - m)
        o_ref[...] = e0 / (e0 + e1)

    return pl.pallas_call(body, name="lower_bound_fwd", out_shape=jax.ShapeDtypeStruct((1, D), F32))(hgrn_lb)


def ada_weight_grad(c_all, dmod_cols):
    def body(c_ref, d_ref, o_ref):
        cval = c_ref[...]
        o_ref[...] = _tn(cval * _sigmoid(cval), d_ref[...], HI)

    return pl.pallas_call(body, name="ada_weight_grad",
                          out_shape=jax.ShapeDtypeStruct((D, dmod_cols.shape[1]), F32))(c_all, dmod_cols)


def reduce_small(gathered, hgrn_lb, dlb_off):
    n = gathered.shape[2]

    def body(g_ref, a_ref, o_ref, glb_ref):
        s = g_ref[0]
        for d in range(1, N_DEV):
            s = s + g_ref[d]
        o_ref[...] = s
        a0, a1 = a_ref[0:1, :], a_ref[1:2, :]
        m = jnp.maximum(a0, a1)
        e0, e1 = jnp.exp(a0 - m), jnp.exp(a1 - m)
        p0 = e0 / (e0 + e1)
        tq = s[:, dlb_off:dlb_off + D] * p0 * (1.0 - p0)
        glb_ref[0:1, :] = tq
        glb_ref[1:2, :] = -tq

    return pl.pallas_call(body, name="reduce_small",
                          out_shape=[jax.ShapeDtypeStruct((1, n), F32), jax.ShapeDtypeStruct((2, D), F32)])(gathered, hgrn_lb)


def _adam_math(w, g, m, v):
    m2 = ADAM_B1 * m + (1.0 - ADAM_B1) * g
    v2 = ADAM_B2 * v + (1.0 - ADAM_B2) * (g * g)
    m_hat = m2 / (1.0 - ADAM_B1 ** ADAM_STEP)
    v_hat = v2 / (1.0 - ADAM_B2 ** ADAM_STEP)
    delta = -ADAM_LR * (m_hat / (jnp.sqrt(v_hat) + ADAM_EPS) + ADAM_WD * w)
    return delta, m2, v2


def _row_tile(rows, mult=8, cap=128):
    for cand in range(cap - cap % mult, 0, -mult):
        if rows % cand == 0:
            return cand
    return rows


def sum_parts(parts, name):
    n, rows, cols = parts.shape
    tr = _row_tile(rows, 16, 1024)

    def body(p_ref, o_ref):
        s = p_ref[0].astype(F32)
        for d in range(1, n):
            s = s + p_ref[d].astype(F32)
        o_ref[...] = s

    return pl.pallas_call(
        body, name=name, grid=(rows // tr,),
        out_shape=jax.ShapeDtypeStruct((rows, cols), F32),
        in_specs=[pl.BlockSpec((n, tr, cols), lambda i: (0, i, 0))],
        out_specs=pl.BlockSpec((tr, cols), lambda i: (i, 0)),
        compiler_params=_cparams(("parallel",)),
    )(parts)


def sum_pair(a, b, name):
    rows, cols = a.shape
    tr = _row_tile(rows, 16, 1024)

    def body(a_ref, b_ref, o_ref):
        o_ref[...] = (a_ref[...].astype(F32) + b_ref[...].astype(F32)).astype(o_ref.dtype)

    blk = pl.BlockSpec((tr, cols), lambda i: (i, 0))
    return pl.pallas_call(
        body, name=name, grid=(rows // tr,),
        out_shape=jax.ShapeDtypeStruct((rows, cols), a.dtype),
        in_specs=[blk, blk], out_specs=blk,
        compiler_params=_cparams(("parallel",)),
    )(a, b)


def adamw(w, g, m, v, name):
    rows, cols = w.shape
    tr = _row_tile(rows, 8, 256)

    def body(w_ref, g_ref, m_ref, v_ref, d_ref, m2_ref, v2_ref):
        delta, m2, v2 = _adam_math(w_ref[...], g_ref[...], m_ref[...], v_ref[...])
        d_ref[...] = delta
        m2_ref[...] = m2
        v2_ref[...] = v2

    blk = pl.BlockSpec((tr, cols), lambda i: (i, 0))
    return pl.pallas_call(
        body, name=name, grid=(rows // tr,),
        out_shape=[jax.ShapeDtypeStruct((rows, cols), F32)] * 3,
        in_specs=[blk] * 4, out_specs=[blk] * 3,
        compiler_params=_cparams(("parallel",)),
    )(w, g, m, v)


def _pad128(n):
    return -(-n // 128) * 128


def _pack(arrays):
    offs, parts, off = [], [], 0
    for a in arrays:
        flat = a.reshape(1, -1)
        n = flat.shape[1]
        offs.append(off)
        parts.append(jnp.pad(flat, ((0, 0), (0, _pad128(n) - n))))
        off += _pad128(n)
    return jnp.concatenate(parts, axis=1), offs


def _unpack(vec, offs, shapes):
    out = []
    for off, shp in zip(offs, shapes):
        n = int(np.prod(shp))
        out.append(vec[0, off:off + n].reshape(shp))
    return out


IN_ROWS = IN_DIM // N_DEV
DT_ROW0 = 9216
DT_DEV, DT_LO = divmod(DT_ROW0, IN_ROWS)


GATE_SHIFT = D - 32


def _in_row_pieces(tile):
    pieces = []
    if tile == DT_COL_BLOCK:
        for g in range(N_GROUPS):
            o = DT_ROW0 + HEADS_PER_GROUP * g
            pieces.append((N_STATE * g, o // IN_ROWS, o % IN_ROWS, HEADS_PER_GROUP))
        return pieces
    r, end = tile * D, (tile + 1) * D
    while r < end:
        o = r if r < DT_ROW0 else r - GATE_SHIFT
        dev, loc = divmod(o, IN_ROWS)
        n = min(end - r, IN_ROWS - loc)
        pieces.append((r - tile * D, dev, loc, n))
        r += n
    return pieces


def assemble_w_in(g_all):
    ntile = N_PROJ // D

    def body(g_ref, o_ref):
        j = pl.program_id(0)
        for tile in range(ntile):
            @pl.when(j == tile)
            def _(tile=tile):
                if tile == DT_COL_BLOCK:
                    o_ref[...] = jnp.zeros_like(o_ref)
                for dst, dev, loc, n in _in_row_pieces(tile):
                    o_ref[pl.ds(dst, n), :] = g_ref[dev, pl.ds(loc, n), :]

    return pl.pallas_call(
        body, name="assemble_w_in", grid=(ntile,),
        out_shape=jax.ShapeDtypeStruct((N_PROJ, D), g_all.dtype),
        in_specs=[pl.BlockSpec(memory_space=pltpu.VMEM)],
        out_specs=pl.BlockSpec((D, D), lambda j: (j, 0)),
        compiler_params=_cparams(("arbitrary",)),
    )(g_all)


def _grad_in_blocks(g_t, core, slot):
    dt0 = DT_COL_BLOCK * D
    dt = g_t[dt0:dt0 + N_GROUPS * N_STATE].reshape(N_GROUPS, N_STATE, D)[:, :HEADS_PER_GROUP].reshape(32, D)
    with_dt = jnp.concatenate([g_t[DT_DEV * IN_ROWS:DT_ROW0], dt,
                               g_t[DT_ROW0 + 32 + GATE_SHIFT:(DT_DEV + 1) * IN_ROWS + GATE_SHIFT]], axis=0)
    blocks = []
    for q in range(N_CHIP):
        if 2 * q + 1 < DT_DEV:
            blk = lax.dynamic_slice_in_dim(g_t, IN_ROWS * (2 * q + core), IN_ROWS, axis=0)
        else:
            assert 2 * q == DT_DEV
            after = g_t[(DT_DEV + 1) * IN_ROWS + GATE_SHIFT:(DT_DEV + 2) * IN_ROWS + GATE_SHIFT]
            blk = jnp.where(core == 0, with_dt, after)
        blocks.append(jnp.pad(blk, ((0, slot - IN_ROWS), (0, 0))))
    return jnp.stack(blocks)


def kernel(x, c, w_ada, b_ada, w_in, hgrn_lb, hgrn_gnorm, ssm_conv_w, ssm_conv_b, ssm_dt_bias, ssm_a_log, ssm_d, ssm_norm, w_branch_a, w_branch_b, w_o, ln1_g, ln1_b, w_ffn_gate, w_ffn_up, w_ffn_down, ln2_g, ln2_b, loss_target, m_w_ada, m_b_ada, m_w_in, m_hgrn_lb, m_hgrn_gnorm, m_ssm_conv_w, m_ssm_conv_b, m_ssm_dt_bias, m_ssm_a_log, m_ssm_d, m_ssm_norm, m_w_branch_a, m_w_branch_b, m_w_o, m_ln1_g, m_ln1_b, m_w_ffn_gate, m_w_ffn_up, m_w_ffn_down, m_ln2_g, m_ln2_b, v_w_ada, v_b_ada, v_w_in, v_hgrn_lb, v_hgrn_gnorm, v_ssm_conv_w, v_ssm_conv_b, v_ssm_dt_bias, v_ssm_a_log, v_ssm_d, v_ssm_norm, v_w_branch_a, v_w_branch_b, v_w_o, v_ln1_g, v_ln1_b, v_w_ffn_gate, v_w_ffn_up, v_w_ffn_down, v_ln2_g, v_ln2_b):
    me = 4 * lax.axis_index("x") + 2 * lax.axis_index("y") + lax.axis_index("c")
    xt = x[0]
    tgt = loss_target[0]
    t = xt.shape[0]
    ada_cols = w_ada.shape[2]
    conv_cols = ssm_conv_w.shape[2]

    small_in, _ = _pack([c, ssm_conv_w[0]])
    small_all = allgather_vmem(small_in, "allgather_small_inputs")
    c_all = small_all[:, 0, :D]
    conv_w = small_all[:, 0, D:D + CONV_TAPS * conv_cols].reshape(N_DEV, CONV_TAPS, conv_cols)
    conv_w = conv_w.transpose(1, 0, 2).reshape(CONV_TAPS, CONV_DIM)
    mod = ada_modulation(c_all, w_ada[0], b_ada.reshape(N_DEV, 1, ada_cols))
    mod6 = mod.reshape(6, D)

    shards = [w_in[0].T, w_branch_a[0], w_branch_b[0], w_o[0], w_ffn_gate[0].T, w_ffn_up[0].T, w_ffn_down[0]]
    shard_rows = [s.shape[0] for s in shards]
    slot_rows = [-(-r // 32) * 32 for r in shard_rows]
    row_offs = [sum(slot_rows[:i]) for i in range(len(shards))]
    padded = [jnp.pad(s.astype(BF16), ((0, p - r), (0, 0))) for s, r, p in zip(shards, shard_rows, slot_rows)]
    w_in_t = assemble_w_in(allgather_hbm(padded[0], "allgather_w_in"))

    lb = lower_bound_fwd(hgrn_lb)
    u1 = ln_modulate(xt, mod6, 0, 1, "ln_modulate_1")
    proj, g_rest = mm_nt_gather(u1, w_in_t, F32, jnp.concatenate(padded[1:], axis=0), "mm_in_proj")
    g_ba, g_bb, g_o, g_fg, g_fu, g_fd = (g_rest[:, o - slot_rows[0]:o - slot_rows[0] + r]
                                         for o, r in zip(row_offs[1:], shard_rows[1:]))
    w_ba = g_ba.reshape(D, D)
    w_bb = g_bb.reshape(B_INNER, D)
    w_oo = g_o.reshape(D, D)
    w_gu_t = jnp.concatenate([g_fg.reshape(D_FF, D), g_fu.reshape(D_FF, D)], axis=0)
    w_dn = g_fd.reshape(D_FF, D)
    o_a, o_raw, st_a = hgrn_fwd(proj, lb, hgrn_gnorm)
    xc, conv_slope = conv_fwd(proj, conv_w, ssm_conv_b)
    pad3 = ((0, 0), (0, 0), (0, N_STATE - HEADS_PER_GROUP))
    alog4 = jnp.pad(ssm_a_log.reshape(N_GROUPS, 1, HEADS_PER_GROUP), pad3)
    bias4 = jnp.pad(ssm_dt_bias.reshape(N_GROUPS, 1, HEADS_PER_GROUP), pad3)
    dskip4 = jnp.pad(ssm_d.reshape(N_GROUPS, 1, HEADS_PER_GROUP), pad3)
    expand = _head_expand()
    o_b, st_b = ssd_fwd(proj, xc, alog4, bias4, dskip4, ssm_norm, expand)
    ya = mm_nn(o_a, w_ba, BF16, "mm_branch_a")
    yb = mm_nn(o_b, w_bb, BF16, "mm_branch_b")
    merged, h1, x1, u2 = mixer_tail(ya, yb, proj, w_oo, xt, mod6, ln1_g, ln1_b)
    gu, act = ffn_in_act(u2, w_gu_t)

    dh2, dx1_part, acc4 = resid_ln_bwd(x1, (act, w_dn), mod6, 5, ln2_g, ln2_b, tgt, True, "resid_ln_2_bwd")
    g_dn = mm_tn(act, dh2, "mm_grad_ffn_down")
    dgu = ffn_act_bwd(dh2, w_dn, gu)
    g_gu_t = mm_tn(dgu, u2, "mm_grad_ffn_in")
    dx1, acc3 = ln_modulate_bwd(x1, (dgu, w_gu_t), mod6, 4, dx1_part, "ln_modulate_2_bwd")
    dh1, dx_part, acc2 = resid_ln_bwd(xt, h1, mod6, 2, ln1_g, ln1_b, dx1, False, "resid_ln_1_bwd")
    g_o = mm_tn(merged, dh1, "mm_grad_out_proj")
    dya, dyb, dproj = merge_gates_bwd(dh1, w_oo, ya, yb, proj)
    g_ba_full = mm_tn(o_a, dya, "mm_grad_branch_a")
    g_bb_full = mm_tn(o_b, dyb, "mm_grad_branch_b")
    my_core = lax.axis_index("c")

    def by_core(blocks, rows, slots):
        contrib = jnp.concatenate([jnp.pad(b.reshape(N_DEV, -1, D), ((0, 0), (0, p - r), (0, 0)))
                                   for b, r, p in zip(blocks, rows, slots)], axis=1)
        split = contrib.reshape(N_CHIP, 2, contrib.shape[1], D).transpose(1, 0, 2, 3)
        return (lax.dynamic_index_in_dim(split, my_core, 0, keepdims=False),
                lax.dynamic_index_in_dim(split, 1 - my_core, 0, keepdims=False))

    keep_e, give_e = by_core([g_ba_full, g_bb_full, g_o, g_gu_t[:D_FF], g_gu_t[D_FF:], g_dn],
                             shard_rows[1:], slot_rows[1:])
    dproj, dlb, dgn, got_e = hgrn_bwd(proj, lb, hgrn_gnorm, o_raw, dya, w_ba, st_a, give_e, dproj)
    chip_e = sum_pair(keep_e.reshape(-1, D), got_e.reshape(-1, D), "sum_grads_rest_chip").reshape(keep_e.shape)
    dxs, dbm, dcm, dproj, ddt, dwn, dalog, dbias, ddsk, parts_e = ssd_bwd(proj, xc, alog4, bias4, dskip4, ssm_norm,
                                                                          expand, dyb, w_bb, st_b, chip_e, dproj)
    dxc = jnp.concatenate([dxs, dbm, dcm], axis=1)
    dproj, dcw, dcb = conv_bwd(proj, dxc, conv_slope, conv_w, dproj)
    dproj = dt_fill(ddt, dproj)
    g_in_t = mm_tn(dproj, u1, "mm_grad_in_proj")
    keep_l = _grad_in_blocks(g_in_t, my_core, slot_rows[0])
    give_l = _grad_in_blocks(g_in_t, 1 - my_core, slot_rows[0])
    got_l = exchange_sibling(give_l, "exchange_grad_in_sibling")
    chip_l = sum_pair(keep_l.reshape(-1, D), got_l.reshape(-1, D), "sum_grad_in_chip").reshape(keep_l.shape)
    du1, parts_l = mm_nn_exchange(dproj, w_in_t, F32, chip_l, "mm_du1")
    dx, acc1 = ln_modulate_bwd(xt, du1, mod6, 1, dx_part, "ln_modulate_1_bwd")
    gw_in = sum_parts(parts_l, "sum_grad_in")[:shard_rows[0]].T
    g_rows = sum_parts(parts_e, "sum_grads_rest")
    gw_ba, gw_bb, gw_o, gw_fg, gw_fu, gw_fd = (g_rows[o - slot_rows[0]:o - slot_rows[0] + r]
                                               for o, r in zip(row_offs[1:], shard_rows[1:]))
    gw_fg, gw_fu = gw_fg.T, gw_fu.T

    dmod = jnp.concatenate([acc1[1:2], acc1[0:1], acc2[0:1], acc3[1:2], acc3[0:1], acc4[0:1]], axis=1)
    small_fields = [dmod, acc4[3:4, :128], dlb, dgn, dcw[:CONV_TAPS], dcb, dbias, dalog, ddsk, dwn,
                    acc2[1:2], acc2[2:3], acc4[1:2], acc4[2:3]]
    small_out, offs = _pack(small_fields)
    small_sum_in = allgather_vmem(small_out, "allgather_small_grads")
    gsum, g_lb = reduce_small(small_sum_in, hgrn_lb, offs[2])
    (g_bada, loss_row, _, g_gn, g_cw_full, g_cb, g_bias4, g_alog4, g_dsk4, g_wn, g_l1g, g_l1b, g_l2g, g_l2b) = _unpack(
        gsum, offs, [(1, 6 * D), (1, 128), (1, D), (1, HK), (CONV_TAPS, CONV_DIM), (1, CONV_DIM),
                     (N_GROUPS, N_STATE), (N_GROUPS, N_STATE), (N_GROUPS, N_STATE), (1, B_INNER),
                     (1, D), (1, D), (1, D), (1, D)])
    loss = loss_row[0, 0]
    g_cw = lax.dynamic_slice(g_cw_full, (0, me * conv_cols), (CONV_TAPS, conv_cols))[None]
    g_dtb = g_bias4[:, :HEADS_PER_GROUP].reshape(1, 32)
    g_alog = g_alog4[:, :HEADS_PER_GROUP].reshape(1, 32)
    g_dsk = g_dsk4[:, :HEADS_PER_GROUP].reshape(1, 32)

    dmod_all = small_sum_in[:, 0, offs[0]:offs[0] + 6 * D]
    dmod_cols = lax.dynamic_slice(dmod_all, (0, me * ada_cols), (N_DEV, ada_cols))
    gw_ada = ada_weight_grad(c_all, dmod_cols)

    big = [("ada", w_ada[0], gw_ada, m_w_ada[0], v_w_ada[0]), ("in", w_in[0], gw_in, m_w_in[0], v_w_in[0]),
           ("branch_a", w_branch_a[0], gw_ba, m_w_branch_a[0], v_w_branch_a[0]),
           ("branch_b", w_branch_b[0], gw_bb, m_w_branch_b[0], v_w_branch_b[0]),
           ("o", w_o[0], gw_o, m_w_o[0], v_w_o[0]),
           ("ffn_gate", w_ffn_gate[0], gw_fg, m_w_ffn_gate[0], v_w_ffn_gate[0]),
           ("ffn_up", w_ffn_up[0], gw_fu, m_w_ffn_up[0], v_w_ffn_up[0]),
           ("ffn_down", w_ffn_down[0], gw_fd, m_w_ffn_down[0], v_w_ffn_down[0])]
    big_out = {}
    for nm, w_, g_, m_, v_ in big:
        d_, m2_, v2_ = adamw(w_, g_, m_, v_, "adamw_" + nm)
        big_out[nm] = (g_[None], d_[None], m2_[None], v2_[None])

    small_w = [b_ada, hgrn_lb, hgrn_gnorm, ssm_conv_w, ssm_conv_b, ssm_dt_bias, ssm_a_log, ssm_d, ssm_norm,
               ln1_g, ln1_b, ln2_g, ln2_b]
    small_g = [g_bada, g_lb, g_gn, g_cw, g_cb, g_dtb, g_alog, g_dsk, g_wn, g_l1g, g_l1b, g_l2g, g_l2b]
    small_m = [m_b_ada, m_hgrn_lb, m_hgrn_gnorm, m_ssm_conv_w, m_ssm_conv_b, m_ssm_dt_bias, m_ssm_a_log, m_ssm_d,
               m_ssm_norm, m_ln1_g, m_ln1_b, m_ln2_g, m_ln2_b]
    small_v = [v_b_ada, v_hgrn_lb, v_hgrn_gnorm, v_ssm_conv_w, v_ssm_conv_b, v_ssm_dt_bias, v_ssm_a_log, v_ssm_d,
               v_ssm_norm, v_ln1_g, v_ln1_b, v_ln2_g, v_ln2_b]
    shapes = [a.shape for a in small_w]
    small_g = [g_.reshape(s) for g_, s in zip(small_g, shapes)]
    pw, poffs = _pack(small_w)
    pg, _ = _pack(small_g)
    pm, _ = _pack(small_m)
    pv, _ = _pack(small_v)
    pd, pm2, pv2 = adamw(pw, pg, pm, pv, "adamw_small")
    s_d, s_m, s_v = (_unpack(p, poffs, shapes) for p in (pd, pm2, pv2))
    (sn_bada, sn_lb, sn_gn, sn_cw, sn_cb, sn_dtb, sn_alog, sn_dsk, sn_wn, sn_l1g, sn_l1b, sn_l2g, sn_l2b) = range(13)

    def order(kind):
        sm = [small_g, s_d, s_m, s_v][kind]
        bg = lambda nm: big_out[nm][kind]
        return [bg("ada"), sm[sn_bada], bg("in"), sm[sn_lb], sm[sn_gn], sm[sn_cw], sm[sn_cb], sm[sn_dtb], sm[sn_alog],
                sm[sn_dsk], sm[sn_wn], bg("branch_a"), bg("branch_b"), bg("o"), sm[sn_l1g], sm[sn_l1b],
                bg("ffn_gate"), bg("ffn_up"), bg("ffn_down"), sm[sn_l2g], sm[sn_l2b]]

    return (loss, dx[None], *order(0), *order(1), *order(2), *order(3))
```

```python
import numpy as np
import jax
import jax.numpy as jnp
from jax import lax
from jax.experimental import pallas as pl
from jax.experimental.pallas import tpu as pltpu

F32 = jnp.float32
BF16 = jnp.bfloat16
HI = lax.Precision.HIGHEST

N_DEV = 8
D = 1024
N_HEADS_A = 8
HK = 128
CHUNK = 64
SSD_CHUNK = 128
SSD_CHUNK_BWD = 256
N_GROUPS = 4
HEADS_PER_GROUP = 8
HEAD_P = 64
N_STATE = 128
GROUP_W = HEADS_PER_GROUP * HEAD_P
B_INNER = 2048
CONV_DIM = 3072
D_FF = 2816
IN_DIM = 11296
N_PROJ = 12288
ALPHA = 2.0 ** 0.25
LN_EPS = 1e-5
RMS_EPS = 1e-6
Q_SCALE = 128 ** -0.5
EXP_CLIP = 80.0
ADAM_LR, ADAM_B1, ADAM_B2, ADAM_EPS, ADAM_WD, ADAM_STEP = 0.001, 0.9, 0.999, 1e-8, 0.01, 10
VMEM_LIMIT = 48 * 1024 * 1024
TOKEN_BLOCK = 1024
ROW_TILE = 512
WIDE_ROW_TILE = 1024
MM_ROW_TILE = 1024
MM_TOKEN_TILE = 4096
MM_K_TILE = 3072
MM_COL_TILE = 1408
HGRN_HEADS_PER_STEP = 4
CHUNK_UNROLL = 8
MESH_ID = pl.DeviceIdType.MESH

NT_DIMS = (((1,), (1,)), ((), ()))
TN_DIMS = (((0,), (0,)), ((), ()))


def _cparams(sem=None):
    return pltpu.CompilerParams(dimension_semantics=sem, vmem_limit_bytes=VMEM_LIMIT)


def _sigmoid(x):
    return 1.0 / (1.0 + jnp.exp(-x))


def _dsilu(x, s):
    return s * (1.0 + x * (1.0 - s))


def _nt(a, b, precision=None):
    return lax.dot_general(a, b, NT_DIMS, precision=precision, preferred_element_type=F32)


def _tn(a, b, precision=None):
    return lax.dot_general(a, b, TN_DIMS, precision=precision, preferred_element_type=F32)


def _nn(a, b, precision=None):
    return jnp.dot(a, b, precision=precision, preferred_element_type=F32)


def _split(x, pieces):
    out = []
    for i in range(pieces):
        p = x.astype(BF16)
        out.append(p)
        if i + 1 < pieces:
            x = x - p.astype(F32)
    return out


def _sel(dot, x, sel01, pieces, x_first=True):
    acc = None
    for p in _split(x, pieces):
        term = dot(p, sel01) if x_first else dot(sel01, p)
        acc = term if acc is None else acc + term
    return acc


def _ln(x):
    mu = jnp.mean(x, axis=-1, keepdims=True)
    xc = x - mu
    rstd = lax.rsqrt(jnp.mean(xc * xc, axis=-1, keepdims=True) + LN_EPS)
    return xc * rstd, rstd


def _ln_bwd(dxh, xh, rstd):
    return rstd * (dxh - jnp.mean(dxh, axis=-1, keepdims=True) - xh * jnp.mean(dxh * xh, axis=-1, keepdims=True))


def _colsum(x):
    return jnp.sum(x, axis=0, keepdims=True)


def _tri(n, upper=False):
    r = lax.broadcasted_iota(jnp.int32, (n, n), 0)
    c = lax.broadcasted_iota(jnp.int32, (n, n), 1)
    return (c >= r) if upper else (r >= c)


def _my_pos():
    return lax.axis_index("x"), lax.axis_index("y"), lax.axis_index("c")


def _peer(pos, k):
    x, y, c = pos
    return (x ^ ((k >> 2) & 1), y ^ ((k >> 1) & 1), c ^ (k & 1))


def _flat(pos):
    return 4 * pos[0] + 2 * pos[1] + pos[2]


def allgather_vmem(v, name):
    n = v.shape[1]

    def body(v_ref, o_ref, send_sems, recv_sems, local_sem):
        me = _my_pos()
        mine = pltpu.make_async_copy(v_ref, o_ref.at[_flat(me)], local_sem)
        mine.start()
        sends = []
        for k in range(1, N_DEV):
            peer = _peer(me, k)
            cp = pltpu.make_async_remote_copy(v_ref, o_ref.at[_flat(me)], send_sems.at[k - 1], recv_sems.at[k - 1],
                                              device_id=peer, device_id_type=MESH_ID)
            cp.start()
            sends.append(cp)
        for k in range(1, N_DEV):
            peer = _peer(me, k)
            pltpu.make_async_remote_copy(v_ref, o_ref.at[_flat(peer)], send_sems.at[k - 1], recv_sems.at[k - 1],
                                         device_id=peer, device_id_type=MESH_ID).wait_recv()
        for cp in sends:
            cp.wait_send()
        mine.wait()

    return pl.pallas_call(
        body, name=name,
        out_shape=jax.ShapeDtypeStruct((N_DEV, 1, n), F32),
        in_specs=[pl.BlockSpec(memory_space=pltpu.VMEM)],
        out_specs=pl.BlockSpec(memory_space=pltpu.VMEM),
        scratch_shapes=[pltpu.SemaphoreType.DMA((N_DEV - 1,)), pltpu.SemaphoreType.DMA((N_DEV - 1,)),
                        pltpu.SemaphoreType.DMA],
        compiler_params=_cparams(),
    )(v)


def ada_modulation(c_all, w_ada_s, b_ada_r):
    ncol = w_ada_s.shape[1]

    def body(c_ref, w_ref, b_ref, o_ref, part_ref, send_sems, recv_sems):
        me = _my_pos()
        cval = c_ref[...]
        cond = cval * _sigmoid(cval)
        part = _nn(cond, w_ref[...], HI)
        for r in range(N_DEV):
            part_ref[r] = part[r:r + 1, :]
        sends = []
        for k in range(1, N_DEV):
            peer = _peer(me, k)
            cp = pltpu.make_async_remote_copy(part_ref.at[_flat(peer)], o_ref.at[_flat(me)], send_sems.at[k - 1],
                                              recv_sems.at[k - 1], device_id=peer, device_id_type=MESH_ID)
            cp.start()
            sends.append(cp)
        o_ref[_flat(me)] = part_ref[_flat(me)]
        for k in range(1, N_DEV):
            peer = _peer(me, k)
            pltpu.make_async_remote_copy(part_ref.at[_flat(peer)], o_ref.at[_flat(peer)], send_sems.at[k - 1],
                                         recv_sems.at[k - 1], device_id=peer, device_id_type=MESH_ID).wait_recv()
        for cp in sends:
            cp.wait_send()
        o_ref[...] = o_ref[...] + b_ref[...]

    return pl.pallas_call(
        body, name="ada_modulation",
        out_shape=jax.ShapeDtypeStruct((N_DEV, 1, ncol), F32),
        in_specs=[pl.BlockSpec(memory_space=pltpu.VMEM)] * 3,
        out_specs=pl.BlockSpec(memory_space=pltpu.VMEM),
        scratch_shapes=[pltpu.VMEM((N_DEV, 1, ncol), F32), pltpu.SemaphoreType.DMA((N_DEV - 1,)),
                        pltpu.SemaphoreType.DMA((N_DEV - 1,))],
        compiler_params=_cparams(),
    )(c_all, w_ada_s, b_ada_r)


def allgather_hbm(shard, name):
    def body(x_ref, out_ref, send_sems, recv_sems, local_sem):
        x, y, c = _my_pos()
        me, sibling = (x, y, c), (x, y, 1 - c)
        chips = [(1 - x, y), (x, 1 - y), (1 - x, 1 - y)]

        def slot(pos):
            return out_ref.at[_flat(pos)]

        def copy(k, block, to, src=None):
            return pltpu.make_async_remote_copy(slot(block) if src is None else src, slot(block), send_sems.at[k],
                                                recv_sems.at[k], device_id=to, device_id_type=MESH_ID)

        mine = pltpu.make_async_copy(x_ref, slot(me), local_sem)
        mine.start()
        first = [copy(0, me, sibling, src=x_ref)]
        first += [copy(1 + j, me, (*chip, c), src=x_ref) for j, chip in enumerate(chips)]
        for cp in first:
            cp.start()
        passed = [copy(4 + j, (*chip, c), sibling) for j, chip in enumerate(chips)]
        for j, chip in enumerate(chips):
            copy(1 + j, (*chip, c), me).wait_recv()
            passed[j].start()
        copy(0, sibling, me).wait_recv()
        for j, chip in enumerate(chips):
            copy(4 + j, (*chip, 1 - c), me).wait_recv()
        for cp in first + passed:
            cp.wait_send()
        mine.wait()

    return pl.pallas_call(
        body, name=name,
        out_shape=jax.ShapeDtypeStruct((N_DEV,) + shard.shape, shard.dtype),
        in_specs=[pl.BlockSpec(memory_space=pl.ANY)],
        out_specs=pl.BlockSpec(memory_space=pl.ANY),
        scratch_shapes=[pltpu.SemaphoreType.DMA((N_DEV - 1,)), pltpu.SemaphoreType.DMA((N_DEV - 1,)),
                        pltpu.SemaphoreType.DMA],
        compiler_params=_cparams(),
    )(shard)


N_CHIP = N_DEV // 2
SIBLING_SEMS = [pltpu.SemaphoreType.DMA, pltpu.SemaphoreType.DMA]
CHIP_SEMS = [pltpu.SemaphoreType.DMA((N_CHIP - 1,)), pltpu.SemaphoreType.DMA((N_CHIP - 1,)), pltpu.SemaphoreType.DMA]


def _sibling_exchange(s_ref, o_ref, send_sem, recv_sem):
    x, y, c = _my_pos()
    cp = pltpu.make_async_remote_copy(s_ref, o_ref, send_sem, recv_sem, device_id=(x, y, 1 - c), device_id_type=MESH_ID)
    return cp.start, cp.wait


def _chip_exchange(p_ref, o_ref, send_sems, recv_sems, local_sem):
    x, y, c = _my_pos()
    my_chip = 2 * x + y
    mine = pltpu.make_async_copy(p_ref.at[my_chip], o_ref.at[my_chip], local_sem)
    peers = [(x ^ (k >> 1), y ^ (k & 1)) for k in range(1, N_CHIP)]
    sends = [pltpu.make_async_remote_copy(p_ref.at[2 * px + py], o_ref.at[my_chip], send_sems.at[k], recv_sems.at[k],
                                          device_id=(px, py, c), device_id_type=MESH_ID)
             for k, (px, py) in enumerate(peers)]
    recvs = [pltpu.make_async_remote_copy(p_ref.at[2 * px + py], o_ref.at[2 * px + py], send_sems.at[k], recv_sems.at[k],
                                          device_id=(px, py, c), device_id_type=MESH_ID)
             for k, (px, py) in enumerate(peers)]

    def start():
        mine.start()
        for cp in sends:
            cp.start()

    def wait():
        for cp in recvs:
            cp.wait_recv()
        for cp in sends:
            cp.wait_send()
        mine.wait()

    return start, wait


def exchange_sibling(send, name):
    def body(s_ref, o_ref, send_sem, recv_sem):
        start, wait = _sibling_exchange(s_ref, o_ref, send_sem, recv_sem)
        start()
        wait()

    return pl.pallas_call(
        body, name=name,
        out_shape=jax.ShapeDtypeStruct(send.shape, send.dtype),
        in_specs=[pl.BlockSpec(memory_space=pl.ANY)],
        out_specs=pl.BlockSpec(memory_space=pl.ANY),
        scratch_shapes=SIBLING_SEMS,
        compiler_params=_cparams(),
    )(send)


LANES = 128


def _k_tile(kdim, unit=LANES):
    for cand in range(MM_K_TILE - MM_K_TILE % unit, 0, -unit):
        if kdim % cand == 0:
            return cand
    return kdim


def _lane_tile(n, cap):
    for cand in range(cap - cap % LANES, 0, -LANES):
        if n % cand == 0:
            return cand
    return n


def _m_tile(m, kdim):
    return min(MM_ROW_TILE if kdim > D else 2 * MM_ROW_TILE, m)


def _mm(a, b, out_dtype, name, b_is_nk):
    if a.ndim == 3:
        m, tk = a.shape[1], a.shape[2]
        kdim = a.shape[0] * tk
        a_spec = pl.BlockSpec((None, _m_tile(m, kdim), tk), lambda j, i, k: (k, i, 0))
    else:
        m, kdim = a.shape
        tk = _k_tile(kdim)
        a_spec = pl.BlockSpec((_m_tile(m, kdim), tk), lambda j, i, k: (i, k))
    n = b.shape[0] if b_is_nk else b.shape[1]
    tm, tn = _m_tile(m, kdim), _lane_tile(n, MM_COL_TILE)
    nk = kdim // tk
    dot = _nt if b_is_nk else _nn

    def body(a_ref, b_ref, o_ref, *acc):
        p = dot(a_ref[...], b_ref[...])
        if nk == 1:
            o_ref[...] = p.astype(o_ref.dtype)
        else:
            acc_ref, k = acc[0], pl.program_id(2)

            @pl.when(k == 0)
            def _():
                acc_ref[...] = p

            @pl.when(k > 0)
            def _():
                acc_ref[...] += p

            @pl.when(k == nk - 1)
            def _():
                o_ref[...] = acc_ref[...].astype(o_ref.dtype)

    b_spec = (pl.BlockSpec((tn, tk), lambda j, i, k: (j, k)) if b_is_nk else
              pl.BlockSpec((tk, tn), lambda j, i, k: (k, j)))
    return pl.pallas_call(
        body, name=name, grid=(n // tn, m // tm, nk),
        out_shape=jax.ShapeDtypeStruct((m, n), out_dtype),
        in_specs=[a_spec, b_spec],
        out_specs=pl.BlockSpec((tm, tn), lambda j, i, k: (i, j)),
        scratch_shapes=[] if nk == 1 else [pltpu.VMEM((tm, tn), F32)],
        compiler_params=_cparams(("parallel", "parallel", "arbitrary")),
    )(a, b)


def mm_nn(a, b, out_dtype, name):
    return _mm(a, b, out_dtype, name, False)


def mm_nt(a, b, out_dtype, name):
    return _mm(a, b, out_dtype, name, True)


def mm_nn_exchange(a, b, out_dtype, part, name):
    kblocks, m, kb = a.shape
    kdim = kblocks * kb
    n = b.shape[1]
    tm, tn, tk = min(MM_ROW_TILE, m), _lane_tile(n, MM_COL_TILE), _k_tile(kdim)
    gn, gm, nk = n // tn, m // tm, kdim // tk
    per_step = tk // kb

    def body(a_ref, b_ref, part_ref, o_ref, parts_ref, acc_ref, send_sems, recv_sems, local_sem):
        j, i, k = pl.program_id(0), pl.program_id(1), pl.program_id(2)
        xchg_start, xchg_wait = _chip_exchange(part_ref, parts_ref, send_sems, recv_sems, local_sem)

        @pl.when((j == 0) & (i == 0) & (k == 0))
        def _():
            xchg_start()

        p = _nn(a_ref[0], b_ref[0:kb, :])
        for c in range(1, per_step):
            p = p + _nn(a_ref[c], b_ref[c * kb:(c + 1) * kb, :])

        @pl.when(k == 0)
        def _():
            acc_ref[...] = p

        @pl.when(k > 0)
        def _():
            acc_ref[...] += p

        @pl.when(k == nk - 1)
        def _():
            o_ref[...] = acc_ref[...].astype(o_ref.dtype)

        @pl.when((j == gn - 1) & (i == gm - 1) & (k == nk - 1))
        def _():
            xchg_wait()

    hbm = pl.BlockSpec(memory_space=pl.ANY)
    return pl.pallas_call(
        body, name=name, grid=(gn, gm, nk),
        out_shape=[jax.ShapeDtypeStruct((m, n), out_dtype), jax.ShapeDtypeStruct(part.shape, part.dtype)],
        in_specs=[pl.BlockSpec((per_step, tm, kb), lambda j, i, k: (k, i, 0)),
                  pl.BlockSpec((tk, tn), lambda j, i, k: (k, j)), hbm],
        out_specs=[pl.BlockSpec((tm, tn), lambda j, i, k: (i, j)), hbm],
        scratch_shapes=[pltpu.VMEM((tm, tn), F32)] + CHIP_SEMS,
        compiler_params=_cparams(("arbitrary", "arbitrary", "arbitrary")),
    )(a, b, part)


def mm_nt_gather(a, b, out_dtype, shard, name):
    m, kdim = a.shape
    n = b.shape[0]
    tm, tn = _m_tile(m, kdim), 1024
    assert kdim == 1024
    gj = m // tm
    nsteps = (n // tn) * gj
    forward_step = max(nsteps - 2, 0)

    def body(a_ref, b_ref, x_ref, o_ref, g_ref, send_sems, recv_sems, local_sem):
        step = pl.program_id(0) * gj + pl.program_id(1)
        x, y, c = _my_pos()
        me, sibling = (x, y, c), (x, y, 1 - c)
        chips = [(1 - x, y), (x, 1 - y), (1 - x, 1 - y)]

        def slot(pos):
            return g_ref.at[_flat(pos)]

        def copy(k, block, to, src=None):
            return pltpu.make_async_remote_copy(slot(block) if src is None else src, slot(block), send_sems.at[k],
                                                recv_sems.at[k], device_id=to, device_id_type=MESH_ID)

        mine = pltpu.make_async_copy(x_ref, slot(me), local_sem)
        first = [copy(0, me, sibling, src=x_ref)]
        first += [copy(1 + j, me, (*chip, c), src=x_ref) for j, chip in enumerate(chips)]
        passed = [copy(4 + j, (*chip, c), sibling) for j, chip in enumerate(chips)]

        @pl.when(step == 0)
        def _():
            mine.start()
            for cp in first:
                cp.start()

        rows = pl.ds(pl.multiple_of(pl.program_id(1) * tm, tm), tm)
        o_ref[...] = _nt(a_ref[rows, :], b_ref[...]).astype(o_ref.dtype)

        @pl.when(step == forward_step)
        def _():
            for j, chip in enumerate(chips):
                copy(1 + j, (*chip, c), me).wait_recv()
                passed[j].start()

        @pl.when(step == nsteps - 1)
        def _():
            copy(0, sibling, me).wait_recv()
            for j, chip in enumerate(chips):
                copy(4 + j, (*chip, 1 - c), me).wait_recv()
            for cp in first + passed:
                cp.wait_send()
            mine.wait()

    return pl.pallas_call(
        body, name=name, grid=(n // tn, gj),
        out_shape=[jax.ShapeDtypeStruct((m, n), out_dtype), jax.ShapeDtypeStruct((N_DEV,) + shard.shape, shard.dtype)],
        in_specs=[pl.BlockSpec(memory_space=pltpu.VMEM), pl.BlockSpec((tn, kdim), lambda j, i: (j, 0)),
                  pl.BlockSpec(memory_space=pl.ANY)],
        out_specs=[pl.BlockSpec((tm, tn), lambda j, i: (i, j)), pl.BlockSpec(memory_space=pl.ANY)],
        scratch_shapes=[pltpu.SemaphoreType.DMA((N_DEV - 1,)), pltpu.SemaphoreType.DMA((N_DEV - 1,)),
                        pltpu.SemaphoreType.DMA],
        compiler_params=_cparams(("arbitrary", "arbitrary")),
    )(a, b, shard)


def mm_tn(a, b, name):
    tt, tn = min(MM_TOKEN_TILE, b.shape[0]), _lane_tile(b.shape[1], MM_COL_TILE)
    tka = _lane_tile(a.shape[-1], 1024)
    if a.ndim == 3:
        t, ka = a.shape[1], a.shape[0] * a.shape[2]
        per = a.shape[2] // tka
        a_spec = pl.BlockSpec((None, tt, tka), lambda i, j, s: (i // per, s, i % per))
    else:
        t, ka = a.shape
        a_spec = pl.BlockSpec((tt, tka), lambda i, j, s: (s, i))
    n = b.shape[1]
    nt = t // tt

    def body(a_ref, b_ref, o_ref, *acc):
        p = _tn(a_ref[...], b_ref[...])
        if nt == 1:
            o_ref[...] = p.astype(o_ref.dtype)
        else:
            acc_ref, s = acc[0], pl.program_id(2)

            @pl.when(s == 0)
            def _():
                acc_ref[...] = p

            @pl.when(s > 0)
            def _():
                acc_ref[...] += p

            @pl.when(s == nt - 1)
            def _():
                o_ref[...] = acc_ref[...].astype(o_ref.dtype)

    return pl.pallas_call(
        body, name=name, grid=(ka // tka, n // tn, nt),
        out_shape=jax.ShapeDtypeStruct((ka, n), BF16),
        in_specs=[a_spec, pl.BlockSpec((tt, tn), lambda i, j, s: (s, j))],
        out_specs=pl.BlockSpec((tka, tn), lambda i, j, s: (i, j)),
        scratch_shapes=[] if nt == 1 else [pltpu.VMEM((tka, tn), F32)],
        compiler_params=_cparams(("parallel", "parallel", "arbitrary")),
    )(a, b)


def _tile(t, cap):
    return min(cap, t)


def ln_modulate(x, mod6, shift_row, scale_row, name):
    t = x.shape[0]
    tm = _tile(t, WIDE_ROW_TILE)

    def body(x_ref, mod_ref, o_ref):
        xh, _ = _ln(x_ref[...])
        sc = mod_ref[scale_row:scale_row + 1, :]
        sh = mod_ref[shift_row:shift_row + 1, :]
        o_ref[...] = (xh * (1.0 + sc) + sh).astype(BF16)

    return pl.pallas_call(
        body, name=name, grid=(t // tm,),
        out_shape=jax.ShapeDtypeStruct((t, D), BF16),
        in_specs=[pl.BlockSpec((tm, D), lambda i: (i, 0)), pl.BlockSpec((6, D), lambda i: (0, 0))],
        out_specs=pl.BlockSpec((tm, D), lambda i: (i, 0)),
        compiler_params=_cparams(("parallel",)),
    )(x, mod6)


def resid_ln_bwd(x, h, mod6, gate_row, ln_g, ln_b, cot, with_loss, name):
    t = x.shape[0]
    tm = _tile(t, ROW_TILE)
    h_is_product = isinstance(h, tuple)

    def body(x_ref, *refs):
        if h_is_product:
            a_ref, w_ref, mod_ref, g_ref, b_ref, c_ref, dh_ref, dx_ref, acc_ref = refs
            hv = _nn(a_ref[...], w_ref[...])
        else:
            h_ref, mod_ref, g_ref, b_ref, c_ref, dh_ref, dx_ref, acc_ref = refs
            hv = h_ref[...]

        @pl.when(pl.program_id(0) == 0)
        def _():
            acc_ref[...] = jnp.zeros_like(acc_ref)

        gate = mod_ref[gate_row:gate_row + 1, :]
        r = ALPHA * x_ref[...] + gate * hv
        rh, rstd = _ln(r)
        lng = g_ref[...]
        if with_loss:
            diff = rh * lng + b_ref[...] - c_ref[...]
            dxo = diff * (1.0 / D)
            lsum = jnp.sum(_colsum(diff * diff), axis=-1, keepdims=True) * (0.5 / D)
            acc_ref[3:4, :] += jnp.broadcast_to(lsum, (1, D))
        else:
            dxo = c_ref[...]
        acc_ref[1:2, :] += _colsum(dxo * rh)
        acc_ref[2:3, :] += _colsum(dxo)
        dr = _ln_bwd(dxo * lng, rh, rstd)
        acc_ref[0:1, :] += _colsum(dr * hv)
        dh_ref[...] = (gate * dr).astype(BF16)
        dx_ref[...] = ALPHA * dr

    row = pl.BlockSpec((tm, D), lambda i: (i, 0))
    vec = pl.BlockSpec((1, D), lambda i: (0, 0))
    if h_is_product:
        kdim = h[0].shape[1]
        h_args = list(h)
        h_specs = [pl.BlockSpec((tm, kdim), lambda i: (i, 0)), pl.BlockSpec((kdim, D), lambda i: (0, 0))]
    else:
        h_args, h_specs = [h], [row]
    return pl.pallas_call(
        body, name=name, grid=(t // tm,),
        out_shape=[jax.ShapeDtypeStruct((t, D), BF16), jax.ShapeDtypeStruct((t, D), F32),
                   jax.ShapeDtypeStruct((8, D), F32)],
        in_specs=[row] + h_specs + [pl.BlockSpec((6, D), lambda i: (0, 0)), vec, vec, row],
        out_specs=[row, row, pl.BlockSpec((8, D), lambda i: (0, 0))],
        compiler_params=_cparams(("arbitrary",)),
    )(x, *h_args, mod6, ln_g, ln_b, cot)


def ln_modulate_bwd(x, du, mod6, scale_row, dx_part, name):
    t = x.shape[0]
    tm = _tile(t, ROW_TILE)
    du_is_product = isinstance(du, tuple)

    def body(x_ref, *refs):
        if du_is_product:
            a_ref, w_ref, mod_ref, dp_ref, dx_ref, acc_ref = refs
            kb = a_ref.shape[2]
            du_v = _nn(a_ref[0], w_ref[0:kb, :])
            for blk in range(1, a_ref.shape[0]):
                du_v = du_v + _nn(a_ref[blk], w_ref[blk * kb:(blk + 1) * kb, :])
        else:
            du_ref, mod_ref, dp_ref, dx_ref, acc_ref = refs
            du_v = du_ref[...]

        @pl.when(pl.program_id(0) == 0)
        def _():
            acc_ref[...] = jnp.zeros_like(acc_ref)

        xh, rstd = _ln(x_ref[...])
        sc = mod_ref[scale_row:scale_row + 1, :]
        acc_ref[0:1, :] += _colsum(du_v * xh)
        acc_ref[1:2, :] += _colsum(du_v)
        dx_ref[...] = dp_ref[...] + _ln_bwd(du_v * (1.0 + sc), xh, rstd)

    row = pl.BlockSpec((tm, D), lambda i: (i, 0))
    if du_is_product:
        nblk, _, kb = du[0].shape
        du_args = list(du)
        du_specs = [pl.BlockSpec((nblk, tm, kb), lambda i: (0, i, 0)), pl.BlockSpec(memory_space=pltpu.VMEM)]
    else:
        du_args, du_specs = [du], [row]
    return pl.pallas_call(
        body, name=name, grid=(t // tm,),
        out_shape=[jax.ShapeDtypeStruct((t, D), F32), jax.ShapeDtypeStruct((8, D), F32)],
        in_specs=[row] + du_specs + [pl.BlockSpec((6, D), lambda i: (0, 0)), row],
        out_specs=[row, pl.BlockSpec((8, D), lambda i: (0, 0))],
        compiler_params=_cparams(("arbitrary",)),
    )(x, *du_args, mod6, dx_part)


def mixer_tail_bwd(x1, dgu, w_gu_t, mod6, dx1_part, x, h, ln_g, ln_b):
    t = x.shape[0]
    tm = _tile(t, ROW_TILE // 2)
    nblk, _, kb = dgu.shape

    def body(x1_ref, a_ref, w_ref, mod_ref, dp_ref, x_ref, h_ref, g_ref, b_ref, dh_ref, dx_ref, acc_ref):
        @pl.when(pl.program_id(0) == 0)
        def _():
            acc_ref[...] = jnp.zeros_like(acc_ref)

        du = _nn(a_ref[0], w_ref[0:kb, :])
        for blk in range(1, nblk):
            du = du + _nn(a_ref[blk], w_ref[blk * kb:(blk + 1) * kb, :])
        xh, rstd1 = _ln(x1_ref[...])
        acc_ref[0:1, :] += _colsum(du * xh)
        acc_ref[1:2, :] += _colsum(du)
        dx1 = dp_ref[...] + _ln_bwd(du * (1.0 + mod_ref[4:5, :]), xh, rstd1)
        gate = mod_ref[2:3, :]
        hv = h_ref[...]
        rh, rstd = _ln(ALPHA * x_ref[...] + gate * hv)
        acc_ref[3:4, :] += _colsum(dx1 * rh)
        acc_ref[4:5, :] += _colsum(dx1)
        dr = _ln_bwd(dx1 * g_ref[...], rh, rstd)
        acc_ref[2:3, :] += _colsum(dr * hv)
        dh_ref[...] = (gate * dr).astype(BF16)
        dx_ref[...] = ALPHA * dr

    row = pl.BlockSpec((tm, D), lambda i: (i, 0))
    vec = pl.BlockSpec((1, D), lambda i: (0, 0))
    return pl.pallas_call(
        body, name="mixer_tail_bwd", grid=(t // tm,),
        out_shape=[jax.ShapeDtypeStruct((t, D), BF16), jax.ShapeDtypeStruct((t, D), F32),
                   jax.ShapeDtypeStruct((8, D), F32)],
        in_specs=[row, pl.BlockSpec((nblk, tm, kb), lambda i: (0, i, 0)), pl.BlockSpec(memory_space=pltpu.VMEM),
                  pl.BlockSpec((6, D), lambda i: (0, 0)), row, row, row, vec, vec],
        out_specs=[row, row, pl.BlockSpec((8, D), lambda i: (0, 0))],
        compiler_params=_cparams(("arbitrary",)),
    )(x1, dgu, w_gu_t, mod6, dx1_part, x, h, ln_g, ln_b)


def mixer_tail(ya, yb, proj, w_o, x, mod6, ln_g, ln_b):
    t = ya.shape[0]
    tm = _tile(t, ROW_TILE)

    def body(ya_ref, yb_ref, ga_ref, gb_ref, w_ref, x_ref, mod_ref, g_ref, b_ref, m_ref, h_ref, x1_ref, u2_ref):
        merged = (_sigmoid(ga_ref[...]) * ya_ref[...].astype(F32) +
                  _sigmoid(gb_ref[...]) * yb_ref[...].astype(F32)).astype(BF16)
        m_ref[...] = merged
        hv = _nn(merged, w_ref[...])
        h_ref[...] = hv
        rh, _ = _ln(ALPHA * x_ref[...] + mod_ref[2:3, :] * hv)
        x1 = rh * g_ref[...] + b_ref[...]
        x1_ref[...] = x1
        xh, _ = _ln(x1)
        u2_ref[...] = (xh * (1.0 + mod_ref[4:5, :]) + mod_ref[3:4, :]).astype(BF16)

    row = pl.BlockSpec((tm, D), lambda i: (i, 0))
    vec = pl.BlockSpec((1, D), lambda i: (0, 0))
    return pl.pallas_call(
        body, name="mixer_tail", grid=(t // tm,),
        out_shape=[jax.ShapeDtypeStruct((t, D), BF16), jax.ShapeDtypeStruct((t, D), F32),
                   jax.ShapeDtypeStruct((t, D), F32), jax.ShapeDtypeStruct((t, D), BF16)],
        in_specs=[row, row, pl.BlockSpec((tm, D), lambda i: (i, GATE_BLOCK0)),
                  pl.BlockSpec((tm, D), lambda i: (i, GATE_BLOCK0 + 1)), pl.BlockSpec((D, D), lambda i: (0, 0)),
                  row, pl.BlockSpec((6, D), lambda i: (0, 0)), vec, vec],
        out_specs=[row, row, row, row],
        compiler_params=_cparams(("parallel",)),
    )(ya, yb, proj, proj, w_o, x, mod6, ln_g, ln_b)


def merge_gates_bwd(dh, w_o, ya, yb, proj):
    t = ya.shape[0]
    tm = _tile(t, ROW_TILE)

    def body(dh_ref, w_ref, ya_ref, yb_ref, ga_ref, gb_ref, dya_ref, dyb_ref, dp_ref):
        dmv = _nt(dh_ref[...], w_ref[...])
        sa = _sigmoid(ga_ref[...])
        sb = _sigmoid(gb_ref[...])
        dya_ref[...] = (dmv * sa).astype(BF16)
        dyb_ref[...] = (dmv * sb).astype(BF16)
        dp_ref[0] = (dmv * ya_ref[...].astype(F32) * sa * (1.0 - sa)).astype(BF16)
        dp_ref[1] = (dmv * yb_ref[...].astype(F32) * sb * (1.0 - sb)).astype(BF16)

    row = pl.BlockSpec((tm, D), lambda i: (i, 0))
    return pl.pallas_call(
        body, name="merge_gates_bwd", grid=(t // tm,),
        out_shape=[jax.ShapeDtypeStruct((t, D), BF16)] * 2 + [jax.ShapeDtypeStruct((N_PROJ // D, t, D), BF16)],
        in_specs=[row, pl.BlockSpec((D, D), lambda i: (0, 0)), row, row,
                  pl.BlockSpec((tm, D), lambda i: (i, GATE_BLOCK0)),
                  pl.BlockSpec((tm, D), lambda i: (i, GATE_BLOCK0 + 1))],
        out_specs=[row, row, pl.BlockSpec((2, tm, D), lambda i: (GATE_BLOCK0 // 2, i, 0))],
        compiler_params=_cparams(("parallel",)),
    )(dh, w_o, ya, yb, proj, proj)


FF_CHUNK = 1408


def ffn_in_act(u, w_gu_t):
    t = u.shape[0]
    tm = _tile(t, ROW_TILE)
    nj = D_FF // FF_CHUNK

    def body(a_ref, bg_ref, bu_ref, gu_ref, act_ref):
        a = a_ref[...]
        g = _nt(a, bg_ref[...])
        up = _nt(a, bu_ref[...])
        gu_ref[0] = g.astype(BF16)
        gu_ref[1] = up.astype(BF16)
        act_ref[...] = (g * _sigmoid(g) * up).astype(BF16)

    return pl.pallas_call(
        body, name="ffn_in_act", grid=(nj, t // tm),
        out_shape=[jax.ShapeDtypeStruct((2, t, D_FF), BF16), jax.ShapeDtypeStruct((t, D_FF), BF16)],
        in_specs=[pl.BlockSpec((tm, D), lambda j, i: (i, 0)), pl.BlockSpec((FF_CHUNK, D), lambda j, i: (j, 0)),
                  pl.BlockSpec((FF_CHUNK, D), lambda j, i: (nj + j, 0))],
        out_specs=[pl.BlockSpec((2, tm, FF_CHUNK), lambda j, i: (0, i, j)),
                   pl.BlockSpec((tm, FF_CHUNK), lambda j, i: (i, j))],
        compiler_params=_cparams(("parallel", "parallel")),
    )(u, w_gu_t, w_gu_t)


def ffn_act_bwd(dh, w_dn, gu):
    t = dh.shape[0]
    tm = _tile(t, ROW_TILE)

    def body(a_ref, b_ref, gu_ref, o_ref):
        da = _nt(a_ref[...], b_ref[...])
        g = gu_ref[0].astype(F32)
        up = gu_ref[1].astype(F32)
        s = _sigmoid(g)
        o_ref[0] = (da * up * _dsilu(g, s)).astype(BF16)
        o_ref[1] = (da * g * s).astype(BF16)

    blk = pl.BlockSpec((2, tm, FF_CHUNK), lambda j, i: (0, i, j))
    return pl.pallas_call(
        body, name="ffn_act_bwd", grid=(D_FF // FF_CHUNK, t // tm),
        out_shape=jax.ShapeDtypeStruct((2, t, D_FF), BF16),
        in_specs=[pl.BlockSpec((tm, D), lambda j, i: (i, 0)), pl.BlockSpec((FF_CHUNK, D), lambda j, i: (j, 0)), blk],
        out_specs=blk,
        compiler_params=_cparams(("parallel", "parallel")),
    )(dh, w_dn, gu)


def _hgrn_chunk_terms(q, fl, lbv, tril_f):
    sig = _sigmoid(fl)
    f = lbv + (1.0 - lbv) * sig
    lam = jnp.log(f)
    k = 1.0 - f
    sq = _sigmoid(q)
    qt = q * sq * Q_SCALE
    bc = _sel(_nn, lam, tril_f, 3, x_first=False)
    bmid = bc[CHUNK // 2 - 1:CHUNK // 2, :]
    bl = bc[CHUNK - 1:CHUNK, :]
    eq = jnp.exp(jnp.minimum(bc - bmid, EXP_CLIP))
    ek = jnp.exp(jnp.minimum(bmid - bc, EXP_CLIP))
    eb = jnp.exp(bc)
    ekl = jnp.exp(bl - bc)
    ebl = jnp.exp(bl)
    return sig, f, k, sq, qt, eq, ek, eb, ekl, ebl


def hgrn_fwd(proj, lb, gnorm):
    t = proj.shape[0]
    tb = _tile(t, TOKEN_BLOCK)
    ncb = tb // CHUNK

    hps = HGRN_HEADS_PER_STEP
    wide = hps * HK

    def body(q_ref, f_ref, i_ref, g_ref, lb_ref, gn_ref, oa_ref, oraw_ref, st_ref, state):
        @pl.when(pl.program_id(1) == 0)
        def _():
            state[...] = jnp.zeros_like(state)

        gn = gn_ref[...]
        mask = _tri(CHUNK)
        tril_f = mask.astype(BF16)

        def chunk(c, carry):
            sl = pl.ds(pl.multiple_of(c * CHUNK, CHUNK), CHUNK)
            for hh in range(hps):
                ln = slice(hh * HK, (hh + 1) * HK)
                q, fl, v, g = q_ref[sl, ln], f_ref[sl, ln], i_ref[sl, ln], g_ref[sl, ln]
                sig, f, k, sq, qt, eq, ek, eb, ekl, ebl = _hgrn_chunk_terms(q, fl, lb_ref[:, ln], tril_f)
                a = jnp.where(mask, _nt((qt * eq).astype(BF16), (k * ek).astype(BF16)), 0.0)
                st = state[hh]
                st_ref[hh, c] = st
                vb = v.astype(BF16)
                o = _nn(a.astype(BF16), vb) + _nt((qt * eb).astype(BF16), st.astype(BF16))
                state[hh] = st * ebl + _tn(vb, (k * ekl).astype(BF16))
                oraw_ref[sl, ln] = o
                rn = o * lax.rsqrt(jnp.mean(o * o, axis=-1, keepdims=True) + RMS_EPS)
                oa_ref[sl, ln] = (rn * gn * g * _sigmoid(g)).astype(BF16)
            return carry

        lax.fori_loop(0, ncb, chunk, 0, unroll=min(CHUNK_UNROLL, ncb))

    def col(block):
        return pl.BlockSpec((tb, wide), lambda h, j: (j, block * (N_HEADS_A // hps) + h))

    return pl.pallas_call(
        body, name="hgrn_fwd", grid=(N_HEADS_A // hps, t // tb),
        out_shape=[jax.ShapeDtypeStruct((t, D), BF16), jax.ShapeDtypeStruct((t, D), F32),
                   jax.ShapeDtypeStruct((N_HEADS_A, t // CHUNK, HK, HK), F32)],
        in_specs=[col(0), col(1), col(2), col(3), pl.BlockSpec((1, wide), lambda h, j: (0, h)),
                  pl.BlockSpec((1, HK), lambda h, j: (0, 0))],
        out_specs=[pl.BlockSpec((tb, wide), lambda h, j: (j, h)), pl.BlockSpec((tb, wide), lambda h, j: (j, h)),
                   pl.BlockSpec((hps, ncb, HK, HK), lambda h, j: (h, j, 0, 0))],
        scratch_shapes=[pltpu.VMEM((hps, HK, HK), F32)],
        compiler_params=_cparams(("parallel", "arbitrary")),
    )(proj, proj, proj, proj, lb, gnorm)


def hgrn_bwd(proj, lb, gnorm, o_raw, dya, w_ba, states, give, dproj):
    t = proj.shape[0]
    tb = _tile(t, TOKEN_BLOCK)
    ncb = tb // CHUNK
    nb = t // tb
    hps = HGRN_HEADS_PER_STEP
    wide = hps * HK

    def body(q_ref, f_ref, i_ref, g_ref, lb_ref, gn_ref, oraw_ref, dya_ref, wba_ref, st_ref, give_ref, dp_in_ref,
             dp_ref, dlb_ref, dgn_ref, got_ref, dstate, doa_ref, send_sem, recv_sem):
        h, j = pl.program_id(0), pl.program_id(1)
        swap_start, swap_wait = _sibling_exchange(give_ref, got_ref, send_sem, recv_sem)
        doa_ref[...] = _nt(dya_ref[...], wba_ref[...])

        @pl.when((h == 0) & (j == 0))
        def _():
            swap_start()

        @pl.when(j == 0)
        def _():
            dstate[...] = jnp.zeros_like(dstate)
            dlb_ref[...] = jnp.zeros_like(dlb_ref)

        @pl.when((j == 0) & (h == 0))
        def _():
            dgn_ref[...] = jnp.zeros_like(dgn_ref)

        gn = gn_ref[...]
        mask = _tri(CHUNK)
        mask_t = _tri(CHUNK, upper=True)
        tril_f = mask.astype(BF16)
        triu_f = mask_t.astype(BF16)

        def chunk(i, c0):
            c = ncb - 1 - i
            sl = pl.ds(pl.multiple_of(c * CHUNK, CHUNK), CHUNK)
            for hh in range(hps):
                ln = slice(hh * HK, (hh + 1) * HK)
                q, fl, v, g = q_ref[sl, ln], f_ref[sl, ln], i_ref[sl, ln], g_ref[sl, ln]
                lbv = lb_ref[:, ln]
                sig, f, k, sq, qt, eq, ek, eb, ekl, ebl = _hgrn_chunk_terms(q, fl, lbv, tril_f)
                qe = (qt * eq).astype(BF16)
                ke = (k * ek).astype(BF16)
                st32 = st_ref[hh, c]
                st = st32.astype(BF16)
                dst = dstate[hh]
                dstb = dst.astype(BF16)
                o = oraw_ref[sl, ln]
                rstd = lax.rsqrt(jnp.mean(o * o, axis=-1, keepdims=True) + RMS_EPS)
                rn = o * rstd
                sgm = _sigmoid(g)
                sg = g * sgm
                doa_v = doa_ref[sl, ln]
                drn = doa_v * gn * sg
                dgn_ref[...] += _colsum(doa_v * rn * sg)
                dp_ref[3, sl, ln] = (doa_v * rn * gn * _dsilu(g, sgm)).astype(BF16)
                do = rstd * (drn - rn * jnp.mean(drn * rn, axis=-1, keepdims=True))
                dob = do.astype(BF16)
                vb = v.astype(BF16)
                da = jnp.where(mask, _nt(dob, vb), 0.0).astype(BF16)
                da_t = jnp.where(mask_t, _nt(vb, dob), 0.0).astype(BF16)
                a_t = jnp.where(mask_t, _nt(ke, qe), 0.0).astype(BF16)
                kl = (k * ekl).astype(BF16)
                qb = (qt * eb).astype(BF16)
                dq_in = _nn(da, ke)
                dk_in = _nn(da_t, qe)
                dq_out = eb * _nn(dob, st)
                dk_out = ekl * _nn(vb, dstb)
                dqt = eq * dq_in + dq_out
                dk = ek * dk_in + dk_out
                dv = _nn(a_t, dob) + _nt(kl, dstb)
                dstate[hh] = dst * ebl + _tn(dob, qb)
                dbig = qe.astype(F32) * dq_in - ke.astype(F32) * dk_in + qt * dq_out - k * dk_out
                beyond = _colsum(k * dk_out) + ebl * _colsum(dst * st32)
                dlam = _sel(_nn, dbig, triu_f, 3, x_first=False) + beyond
                df = dlam / f - dk
                dp_ref[1, sl, ln] = (df * (1.0 - lbv) * sig * (1.0 - sig)).astype(BF16)
                dlb_ref[:, ln] += _colsum(df * (1.0 - sig))
                dp_ref[0, sl, ln] = (dqt * Q_SCALE * _dsilu(q, sq)).astype(BF16)
                dp_ref[2, sl, ln] = dv.astype(BF16)
            return c0

        lax.fori_loop(0, ncb, chunk, 0, unroll=min(CHUNK_UNROLL, ncb))

        @pl.when((h == N_HEADS_A // hps - 1) & (j == nb - 1))
        def _():
            swap_wait()

    def col(block):
        return pl.BlockSpec((tb, wide), lambda h, j: (nb - 1 - j, block * (N_HEADS_A // hps) + h))

    hcol = pl.BlockSpec((tb, wide), lambda h, j: (nb - 1 - j, h))
    hbm = pl.BlockSpec(memory_space=pl.ANY)
    return pl.pallas_call(
        body, name="hgrn_bwd", grid=(N_HEADS_A // hps, nb),
        out_shape=[jax.ShapeDtypeStruct(dproj.shape, dproj.dtype), jax.ShapeDtypeStruct((1, D), F32),
                   jax.ShapeDtypeStruct((1, HK), F32), jax.ShapeDtypeStruct(give.shape, give.dtype)],
        in_specs=[col(0), col(1), col(2), col(3), pl.BlockSpec((1, wide), lambda h, j: (0, h)),
                  pl.BlockSpec((1, HK), lambda h, j: (0, 0)), hcol,
                  pl.BlockSpec((tb, D), lambda h, j: (nb - 1 - j, 0)), pl.BlockSpec((wide, D), lambda h, j: (h, 0)),
                  pl.BlockSpec((hps, ncb, HK, HK), lambda h, j: (h, nb - 1 - j, 0, 0)), hbm, hbm],
        out_specs=[pl.BlockSpec((4, tb, wide), lambda h, j: (0, nb - 1 - j, h)),
                   pl.BlockSpec((1, wide), lambda h, j: (0, h)), pl.BlockSpec((1, HK), lambda h, j: (0, 0)), hbm],
        input_output_aliases={11: 0},
        scratch_shapes=[pltpu.VMEM((hps, HK, HK), F32), pltpu.VMEM((tb, wide), F32)] + SIBLING_SEMS,
        compiler_params=_cparams(("arbitrary", "arbitrary")),
    )(proj, proj, proj, proj, lb, gnorm, o_raw, dya, w_ba, states, give, dproj)


CONV_BLOCK0 = 6
CONV_TAPS = 4
HALO = 8


def conv_fwd(proj, conv_w, conv_b):
    t = proj.shape[0]
    tm = _tile(t, ROW_TILE)
    r = tm // HALO

    def body(x_ref, halo_ref, w_ref, b_ref, o_ref, ds_ref):
        i = pl.program_id(1)
        halo = jnp.where(i > 0, halo_ref[...], 0.0)
        ext = jnp.concatenate([halo, x_ref[...]], axis=0)
        pre = b_ref[...] + w_ref[CONV_TAPS - 1:CONV_TAPS, :] * ext[HALO:, :]
        for tap in range(CONV_TAPS - 1):
            pre = pre + w_ref[tap:tap + 1, :] * pltpu.roll(ext, CONV_TAPS - 1 - tap, axis=0)[HALO:, :]
        s = _sigmoid(pre)
        o_ref[...] = pre * s
        ds_ref[...] = _dsilu(pre, s).astype(BF16)

    blk = pl.BlockSpec((tm, D), lambda cb, i: (i, cb))
    return pl.pallas_call(
        body, name="conv_fwd", grid=(CONV_DIM // D, t // tm),
        out_shape=[jax.ShapeDtypeStruct((t, CONV_DIM), F32), jax.ShapeDtypeStruct((t, CONV_DIM), BF16)],
        in_specs=[pl.BlockSpec((tm, D), lambda cb, i: (i, CONV_BLOCK0 + cb)),
                  pl.BlockSpec((HALO, D), lambda cb, i: (jnp.maximum(i * r - 1, 0), CONV_BLOCK0 + cb)),
                  pl.BlockSpec((CONV_TAPS, D), lambda cb, i: (0, cb)), pl.BlockSpec((1, D), lambda cb, i: (0, cb))],
        out_specs=[blk, blk],
        compiler_params=_cparams(("parallel", "parallel")),
    )(proj, proj, conv_w, conv_b)


def conv_bwd(proj, dxc, dsilu, conv_w, dproj):
    t = proj.shape[0]
    tm = _tile(t, ROW_TILE)
    r = tm // HALO
    n = t // tm
    last_halo = t // HALO - 1

    def body(x_ref, prev_ref, d_ref, dnext_ref, s_ref, snext_ref, w_ref, dp_in_ref, dx_ref, dw_ref, db_ref):
        i = pl.program_id(1)

        @pl.when(i == 0)
        def _():
            dw_ref[...] = jnp.zeros_like(dw_ref)
            db_ref[...] = jnp.zeros_like(db_ref)

        dpre = jnp.concatenate([d_ref[...].astype(F32) * s_ref[...].astype(F32),
                                jnp.where(i < n - 1, dnext_ref[0:HALO, :].astype(F32) * snext_ref[0:HALO, :].astype(F32),
                                          0.0)], axis=0)
        dx = w_ref[CONV_TAPS - 1:CONV_TAPS, :] * dpre[:tm, :]
        for tap in range(CONV_TAPS - 1):
            back = CONV_TAPS - 1 - tap
            dx = dx + w_ref[tap:tap + 1, :] * pltpu.roll(dpre, tm + HALO - back, axis=0)[:tm, :]
        dx_ref[...] = dx.astype(BF16)
        dp = dpre[:tm, :]
        db_ref[...] += _colsum(dp)
        prev = jnp.where(i > 0, prev_ref[...], 0.0)
        ext = jnp.concatenate([prev, x_ref[...]], axis=0)
        dw_ref[CONV_TAPS - 1:CONV_TAPS, :] += _colsum(dp * ext[HALO:, :])
        for tap in range(CONV_TAPS - 1):
            dw_ref[tap:tap + 1, :] += _colsum(dp * pltpu.roll(ext, CONV_TAPS - 1 - tap, axis=0)[HALO:, :])

    blk = pl.BlockSpec((tm, D), lambda cb, i: (i, cb))
    nxt = pl.BlockSpec((2 * HALO, D), lambda cb, i: (jnp.minimum((i + 1) * (r // 2), last_halo // 2), cb))
    return pl.pallas_call(
        body, name="conv_bwd", grid=(CONV_DIM // D, n),
        out_shape=[jax.ShapeDtypeStruct(dproj.shape, dproj.dtype), jax.ShapeDtypeStruct((8, CONV_DIM), F32),
                   jax.ShapeDtypeStruct((1, CONV_DIM), F32)],
        in_specs=[pl.BlockSpec((tm, D), lambda cb, i: (i, CONV_BLOCK0 + cb)),
                  pl.BlockSpec((HALO, D), lambda cb, i: (jnp.maximum(i * r - 1, 0), CONV_BLOCK0 + cb)),
                  blk, nxt, blk, nxt,
                  pl.BlockSpec((CONV_TAPS, D), lambda cb, i: (0, cb)), pl.BlockSpec(memory_space=pl.ANY)],
        out_specs=[pl.BlockSpec((None, tm, D), lambda cb, i: (CONV_BLOCK0 + cb, i, 0)),
                   pl.BlockSpec((8, D), lambda cb, i: (0, cb)), pl.BlockSpec((1, D), lambda cb, i: (0, cb))],
        input_output_aliases={7: 0},
        compiler_params=_cparams(("parallel", "arbitrary")),
    )(proj, proj, dxc, dxc, dsilu, dsilu, conv_w, dproj)


def dt_fill(ddt, dproj):
    t = ddt.shape[0]
    tm = _tile(t, WIDE_ROW_TILE)
    w = ddt.shape[1]

    def body(d_ref, dp_in_ref, o_ref):
        o_ref[:, :w] = d_ref[...]
        o_ref[:, w:] = jnp.zeros((tm, D - w), o_ref.dtype)

    return pl.pallas_call(
        body, name="dt_fill", grid=(t // tm,),
        out_shape=jax.ShapeDtypeStruct(dproj.shape, dproj.dtype),
        in_specs=[pl.BlockSpec((tm, w), lambda i: (i, 0)), pl.BlockSpec(memory_space=pl.ANY)],
        out_specs=pl.BlockSpec((None, tm, D), lambda i: (DT_COL_BLOCK, i, 0)),
        input_output_aliases={1: 0},
        compiler_params=_cparams(("parallel",)),
    )(ddt, dproj)


Z_BLOCK0 = 8
DT_COL_BLOCK = 9
DT_BLOCK0 = 8 * DT_COL_BLOCK
GATE_BLOCK0 = 10
B_BLOCK0 = 16
C_BLOCK0 = 20


def _head_expand():
    e = np.zeros((N_STATE, GROUP_W), np.float32)
    for hh in range(HEADS_PER_GROUP):
        e[hh, hh * HEAD_P:(hh + 1) * HEAD_P] = 1.0
    return jnp.asarray(e, BF16)


def _ssd_chunk_terms(dt, bias, alog, expand, tril_f, eye):
    dtb = dt + bias
    delta = jnp.maximum(dtb, 0.0) + jnp.log(1.0 + jnp.exp(-jnp.abs(dtb)))
    ea = jnp.exp(alog)
    a = -ea * delta
    acum = _sel(_nn, a, tril_f, 3, x_first=False)
    delta_e = _sel(_nn, delta, expand, 2)
    acum_e = _sel(_nn, acum, expand, 2)
    acum_t = _sel(_nt, acum, eye, 3, x_first=False)
    return dtb, delta, ea, a, acum, delta_e, acum_e, acum_t


def ssd_fwd(proj, xc, alog4, bias4, dskip4, wnorm, expand):
    t = proj.shape[0]
    tb = _tile(t, TOKEN_BLOCK)
    ncb = tb // SSD_CHUNK

    def body(xs_ref, b_ref, c_ref, dt_ref, z_ref, alog_ref, bias_ref, dsk_ref, wn_ref, e_ref, ob_ref, st_ref, state):
        @pl.when(pl.program_id(1) == 0)
        def _():
            state[...] = jnp.zeros_like(state)

        expand = e_ref[...]
        mask = _tri(SSD_CHUNK)
        tril_f = mask.astype(BF16)
        eye = (lax.broadcasted_iota(jnp.int32, (N_STATE, N_STATE), 0) ==
               lax.broadcasted_iota(jnp.int32, (N_STATE, N_STATE), 1)).astype(BF16)
        alog, bias = alog_ref[0], bias_ref[0]
        d_e = _sel(_nn, jnp.broadcast_to(dsk_ref[0], (8, N_STATE)), expand, 3)[0:1, :]
        wn = wn_ref[...]

        def chunk(c, carry):
            sl = pl.ds(pl.multiple_of(c * SSD_CHUNK, SSD_CHUNK), SSD_CHUNK)
            xs, bm, cm, dt, z = xs_ref[sl, :], b_ref[sl, :], c_ref[sl, :], dt_ref[sl, :], z_ref[sl, :]
            dtb, delta, ea, a, acum, delta_e, acum_e, acum_t = _ssd_chunk_terms(dt, bias, alog, expand, tril_f, eye)
            alast_e = acum_e[SSD_CHUNK - 1:SSD_CHUNK, :]
            xd = xs * delta_e
            xdb = xd.astype(BF16)
            cb_, bb_ = cm.astype(BF16), bm.astype(BF16)
            cbm = _nt(cb_, bb_)
            ys = []
            for hh in range(HEADS_PER_GROUP):
                lh = jnp.where(mask, jnp.exp(jnp.minimum(acum[:, hh:hh + 1] - acum_t[hh:hh + 1, :], 0.0)), 0.0)
                ys.append(_nn((cbm * lh).astype(BF16), xdb[:, hh * HEAD_P:(hh + 1) * HEAD_P]))
            st = state[...]
            st_ref[0, c] = st
            y = jnp.concatenate(ys, axis=1) + _nn(cb_, st.astype(BF16)) * jnp.exp(acum_e) + xs * d_e
            state[...] = st * jnp.exp(alast_e) + _tn(bb_, (xd * jnp.exp(alast_e - acum_e)).astype(BF16))
            yg = y * z * _sigmoid(z)
            ob_ref[sl, :] = (yg * lax.rsqrt(jnp.mean(yg * yg, axis=-1, keepdims=True) + RMS_EPS) * wn).astype(BF16)
            return carry

        lax.fori_loop(0, ncb, chunk, 0, unroll=min(CHUNK_UNROLL, ncb))

    small = pl.BlockSpec((1, 1, N_STATE), lambda g, j: (g, 0, 0))
    return pl.pallas_call(
        body, name="ssd_fwd", grid=(N_GROUPS, t // tb),
        out_shape=[jax.ShapeDtypeStruct((t, B_INNER), BF16),
                   jax.ShapeDtypeStruct((N_GROUPS, t // SSD_CHUNK, N_STATE, GROUP_W), F32)],
        in_specs=[pl.BlockSpec((tb, GROUP_W), lambda g, j: (j, g)),
                  pl.BlockSpec((tb, N_STATE), lambda g, j: (j, B_BLOCK0 + g)),
                  pl.BlockSpec((tb, N_STATE), lambda g, j: (j, C_BLOCK0 + g)),
                  pl.BlockSpec((tb, N_STATE), lambda g, j: (j, DT_BLOCK0 + g)),
                  pl.BlockSpec((tb, GROUP_W), lambda g, j: (j, Z_BLOCK0 + g)),
                  small, small, small, pl.BlockSpec((1, GROUP_W), lambda g, j: (0, g)),
                  pl.BlockSpec((N_STATE, GROUP_W), lambda g, j: (0, 0))],
        out_specs=[pl.BlockSpec((tb, GROUP_W), lambda g, j: (j, g)),
                   pl.BlockSpec((1, ncb, N_STATE, GROUP_W), lambda g, j: (g, j, 0, 0))],
        scratch_shapes=[pltpu.VMEM((N_STATE, GROUP_W), F32)],
        compiler_params=_cparams(("parallel", "arbitrary")),
    )(xc, xc, xc, proj, proj, alog4, bias4, dskip4, wnorm, expand)


def ssd_bwd(proj, xc, alog4, bias4, dskip4, wnorm, expand, dyb, w_bb, states, part, dproj):
    t = proj.shape[0]
    tb = _tile(t, TOKEN_BLOCK)
    lc = min(SSD_CHUNK_BWD, tb)
    ncb = tb // lc
    nsaved = tb // SSD_CHUNK
    nb = t // tb

    def body(xs_ref, b_ref, c_ref, dt_ref, z_ref, alog_ref, bias_ref, dsk_ref, wn_ref, e_ref, dyb_ref, wbb_ref, st_ref,
             part_ref, dp_in_ref, dxs_ref, db_ref, dc_ref, dz_ref, ddt_ref, dwn_ref, dalog_ref, dbias_ref, ddsk_ref,
             parts_ref, dstate, dob_ref, send_sems, recv_sems, local_sem):
        xchg_start, xchg_wait = _chip_exchange(part_ref, parts_ref, send_sems, recv_sems, local_sem)
        dob_ref[...] = _nt(dyb_ref[...], wbb_ref[...])

        @pl.when((pl.program_id(0) == 0) & (pl.program_id(1) == 0))
        def _():
            xchg_start()

        @pl.when(pl.program_id(1) == 0)
        def _():
            dstate[...] = jnp.zeros_like(dstate)
            dwn_ref[...] = jnp.zeros_like(dwn_ref)
            dalog_ref[...] = jnp.zeros_like(dalog_ref)
            dbias_ref[...] = jnp.zeros_like(dbias_ref)
            ddsk_ref[...] = jnp.zeros_like(ddsk_ref)

        expand = e_ref[...]
        mask = _tri(lc)
        mask_t = _tri(lc, upper=True)
        tril_f = mask.astype(BF16)
        triu_f = mask_t.astype(BF16)
        eye = (lax.broadcasted_iota(jnp.int32, (N_STATE, N_STATE), 0) ==
               lax.broadcasted_iota(jnp.int32, (N_STATE, N_STATE), 1)).astype(BF16)
        alog, bias = alog_ref[0], bias_ref[0]
        d_e = _sel(_nn, jnp.broadcast_to(dsk_ref[0], (8, N_STATE)), expand, 3)[0:1, :]
        wn = wn_ref[...]

        def chunk(i, c0):
            c = ncb - 1 - i
            sl = pl.ds(pl.multiple_of(c * lc, lc), lc)
            xs, bm, cm, dt, z = xs_ref[sl, :], b_ref[sl, :], c_ref[sl, :], dt_ref[sl, :], z_ref[sl, :]
            dtb, delta, ea, a, acum, delta_e, acum_e, acum_t = _ssd_chunk_terms(dt, bias, alog, expand, tril_f, eye)
            alast_e = acum_e[lc - 1:lc, :]
            eacum = jnp.exp(acum_e)
            wl = jnp.exp(alast_e - acum_e)
            xd = xs * delta_e
            xdb = xd.astype(BF16)
            cb_, bb_ = cm.astype(BF16), bm.astype(BF16)
            cbm = _nt(cb_, bb_)
            st32 = st_ref[0, c * (lc // SSD_CHUNK)]
            stb = st32.astype(BF16)
            dst = dstate[...]
            dstb = dst.astype(BF16)
            lhs, mixes, ys = [], [], []
            for hh in range(HEADS_PER_GROUP):
                col, row = acum[:, hh:hh + 1], acum_t[hh:hh + 1, :]
                lh = jnp.where(mask, jnp.exp(jnp.minimum(col - row, 0.0)), 0.0)
                mix = (cbm * lh).astype(BF16)
                lhs.append(lh)
                mixes.append(mix)
                ys.append(_nn(mix, xdb[:, hh * HEAD_P:(hh + 1) * HEAD_P]))
            y_in = jnp.concatenate(ys, axis=1)
            y_out = _nn(cb_, stb) * eacum
            y = y_in + y_out + xs * d_e
            sgz = _sigmoid(z)
            sz = z * sgz
            yg = y * sz
            rstd = lax.rsqrt(jnp.mean(yg * yg, axis=-1, keepdims=True) + RMS_EPS)
            nrm = yg * rstd
            dob_v = dob_ref[sl, :]
            dn = dob_v * wn
            dwn_ref[...] += _colsum(dob_v * nrm)
            dyg = rstd * (dn - nrm * jnp.mean(dn * nrm, axis=-1, keepdims=True))
            dy = dyg * sz
            dz_ref[sl, :] = (dyg * y * _dsilu(z, sgz)).astype(BF16)
            dyb = dy.astype(BF16)
            dxds = []
            dcb = jnp.zeros((lc, lc), F32)
            for hh in range(HEADS_PER_GROUP):
                hs = slice(hh * HEAD_P, (hh + 1) * HEAD_P)
                dy_h, x_h = dyb[:, hs], xdb[:, hs]
                dxds.append(_tn(mixes[hh], dy_h))
                dcb = dcb + _nt(dy_h, x_h) * lhs[hh]
            dcbb = dcb.astype(BF16)
            dye = (dy * eacum).astype(BF16)
            xw = (xd * wl).astype(BF16)
            dxd_in = jnp.concatenate(dxds, axis=1)
            dxd_out = wl * _nn(bb_, dstb)
            dxd = dxd_in + dxd_out
            dc_ref[sl, :] = (_nn(dcbb, bb_) + _nt(dye, stb)).astype(dc_ref.dtype)
            db_ref[sl, :] = (_tn(dcbb, cb_) + _nt(xw, dstb)).astype(db_ref.dtype)
            dstate[...] = dst * jnp.exp(alast_e) + _tn(cb_, dye)
            col_out = xd * dxd_out
            dac = _sel(_nt, dyb.astype(F32) * y_in - xdb.astype(F32) * dxd_in + dy * y_out - col_out, expand, 2)
            beyond = _colsum(col_out) + jnp.exp(alast_e) * _colsum(dst * st32)
            da = (_sel(_nn, dac, triu_f, 3, x_first=False) +
                  _sel(_nt, jnp.broadcast_to(beyond, (8, GROUP_W)), expand, 3)[0:1, :])
            ddelta = _sel(_nt, dxd * xs, expand, 2) - da * ea
            dalog_ref[0] += _colsum(da * a)
            ddtb = ddelta * _sigmoid(dtb)
            dbias_ref[0] += _colsum(ddtb)
            ddt_ref[sl, :] = ddtb.astype(BF16)
            ddsk_ref[0] += _sel(_nt, jnp.broadcast_to(_colsum(dy * xs), (8, GROUP_W)), expand, 3)[0:1, :]
            dxs_ref[sl, :] = (dxd * delta_e + dy * d_e).astype(dxs_ref.dtype)
            return c0

        lax.fori_loop(0, ncb, chunk, 0, unroll=min(CHUNK_UNROLL, ncb))

        @pl.when((pl.program_id(0) == N_GROUPS - 1) & (pl.program_id(1) == nb - 1))
        def _():
            xchg_wait()

    small = pl.BlockSpec((1, 1, N_STATE), lambda g, j: (g, 0, 0))
    wide = pl.BlockSpec((tb, GROUP_W), lambda g, j: (nb - 1 - j, g))
    narrow = pl.BlockSpec((tb, N_STATE), lambda g, j: (nb - 1 - j, g))
    hbm = pl.BlockSpec(memory_space=pl.ANY)
    return pl.pallas_call(
        body, name="ssd_bwd", grid=(N_GROUPS, nb),
        out_shape=[jax.ShapeDtypeStruct((t, B_INNER), BF16), jax.ShapeDtypeStruct((t, GROUP_W), BF16),
                   jax.ShapeDtypeStruct((t, GROUP_W), BF16), jax.ShapeDtypeStruct(dproj.shape, dproj.dtype),
                   jax.ShapeDtypeStruct((t, GROUP_W), BF16), jax.ShapeDtypeStruct((1, B_INNER), F32),
                   jax.ShapeDtypeStruct((N_GROUPS, 1, N_STATE), F32), jax.ShapeDtypeStruct((N_GROUPS, 1, N_STATE), F32),
                   jax.ShapeDtypeStruct((N_GROUPS, 1, N_STATE), F32), jax.ShapeDtypeStruct(part.shape, part.dtype)],
        in_specs=[wide,
                  pl.BlockSpec((tb, N_STATE), lambda g, j: (nb - 1 - j, B_BLOCK0 + g)),
                  pl.BlockSpec((tb, N_STATE), lambda g, j: (nb - 1 - j, C_BLOCK0 + g)),
                  pl.BlockSpec((tb, N_STATE), lambda g, j: (nb - 1 - j, DT_BLOCK0 + g)),
                  pl.BlockSpec((tb, GROUP_W), lambda g, j: (nb - 1 - j, Z_BLOCK0 + g)),
                  small, small, small, pl.BlockSpec((1, GROUP_W), lambda g, j: (0, g)),
                  pl.BlockSpec((N_STATE, GROUP_W), lambda g, j: (0, 0)),
                  pl.BlockSpec((tb, D), lambda g, j: (nb - 1 - j, 0)), pl.BlockSpec((GROUP_W, D), lambda g, j: (g, 0)),
                  pl.BlockSpec((1, nsaved, N_STATE, GROUP_W), lambda g, j: (g, nb - 1 - j, 0, 0)), hbm, hbm],
        out_specs=[wide, narrow, narrow,
                   pl.BlockSpec((None, tb, GROUP_W), lambda g, j: (Z_BLOCK0 // 2 + g // 2, nb - 1 - j, g % 2)),
                   narrow, pl.BlockSpec((1, GROUP_W), lambda g, j: (0, g)), small, small, small, hbm],
        input_output_aliases={14: 3},
        scratch_shapes=[pltpu.VMEM((N_STATE, GROUP_W), F32), pltpu.VMEM((tb, GROUP_W), F32)] + CHIP_SEMS,
        compiler_params=_cparams(("arbitrary", "arbitrary")),
    )(xc, xc, xc, proj, proj, alog4, bias4, dskip4, wnorm, expand, dyb, w_bb, states, part, dproj)


def lower_bound_fwd(hgrn_lb):
    def body(a_ref, o_ref):
        a0, a1 = a_ref[0:1, :], a_ref[1:2, :]
        m = jnp.maximum(a0, a1)
        e0, e1 = jnp.exp(a0 - m), jnp.exp(a1 - m)
        o_ref[...] = e0 / (e0 + e1)

    return pl.pallas_call(body, name="lower_bound_fwd", out_shape=jax.ShapeDtypeStruct((1, D), F32))(hgrn_lb)


def ada_weight_grad(c_all, dmod_cols):
    def body(c_ref, d_ref, o_ref):
        cval = c_ref[...]
        o_ref[...] = _tn(cval * _sigmoid(cval), d_ref[...], HI)

    return pl.pallas_call(body, name="ada_weight_grad",
                          out_shape=jax.ShapeDtypeStruct((D, dmod_cols.shape[1]), F32))(c_all, dmod_cols)


def reduce_small(gathered, hgrn_lb, dlb_off):
    n = gathered.shape[2]

    def body(g_ref, a_ref, o_ref, glb_ref):
        s = g_ref[0]
        for d in range(1, N_DEV):
            s = s + g_ref[d]
        o_ref[...] = s
        a0, a1 = a_ref[0:1, :], a_ref[1:2, :]
        m = jnp.maximum(a0, a1)
        e0, e1 = jnp.exp(a0 - m), jnp.exp(a1 - m)
        p0 = e0 / (e0 + e1)
        tq = s[:, dlb_off:dlb_off + D] * p0 * (1.0 - p0)
        glb_ref[0:1, :] = tq
        glb_ref[1:2, :] = -tq

    return pl.pallas_call(body, name="reduce_small",
                          out_shape=[jax.ShapeDtypeStruct((1, n), F32), jax.ShapeDtypeStruct((2, D), F32)])(gathered, hgrn_lb)


def _adam_math(w, g, m, v):
    m2 = ADAM_B1 * m + (1.0 - ADAM_B1) * g
    v2 = ADAM_B2 * v + (1.0 - ADAM_B2) * (g * g)
    m_hat = m2 / (1.0 - ADAM_B1 ** ADAM_STEP)
    v_hat = v2 / (1.0 - ADAM_B2 ** ADAM_STEP)
    delta = -ADAM_LR * (m_hat / (jnp.sqrt(v_hat) + ADAM_EPS) + ADAM_WD * w)
    return delta, m2, v2


def _row_tile(rows, mult=8, cap=128):
    for cand in range(cap - cap % mult, 0, -mult):
        if rows % cand == 0:
            return cand
    return rows


def sum_parts(parts, name):
    n, rows, cols = parts.shape
    tr = _row_tile(rows, 16, 1024)

    def body(p_ref, o_ref):
        s = p_ref[0].astype(F32)
        for d in range(1, n):
            s = s + p_ref[d].astype(F32)
        o_ref[...] = s

    return pl.pallas_call(
        body, name=name, grid=(rows // tr,),
        out_shape=jax.ShapeDtypeStruct((rows, cols), F32),
        in_specs=[pl.BlockSpec((n, tr, cols), lambda i: (0, i, 0))],
        out_specs=pl.BlockSpec((tr, cols), lambda i: (i, 0)),
        compiler_params=_cparams(("parallel",)),
    )(parts)


def sum_pair(a, b, name):
    rows, cols = a.shape
    tr = _row_tile(rows, 16, 1024)

    def body(a_ref, b_ref, o_ref):
        o_ref[...] = (a_ref[...].astype(F32) + b_ref[...].astype(F32)).astype(o_ref.dtype)

    blk = pl.BlockSpec((tr, cols), lambda i: (i, 0))
    return pl.pallas_call(
        body, name=name, grid=(rows // tr,),
        out_shape=jax.ShapeDtypeStruct((rows, cols), a.dtype),
        in_specs=[blk, blk], out_specs=blk,
        compiler_params=_cparams(("parallel",)),
    )(a, b)


def adamw(w, g, m, v, name):
    rows, cols = w.shape
    tr = _row_tile(rows, 8, 256)

    def body(w_ref, g_ref, m_ref, v_ref, d_ref, m2_ref, v2_ref):
        delta, m2, v2 = _adam_math(w_ref[...], g_ref[...], m_ref[...], v_ref[...])
        d_ref[...] = delta
        m2_ref[...] = m2
        v2_ref[...] = v2

    blk = pl.BlockSpec((tr, cols), lambda i: (i, 0))
    return pl.pallas_call(
        body, name=name, grid=(rows // tr,),
        out_shape=[jax.ShapeDtypeStruct((rows, cols), F32)] * 3,
        in_specs=[blk] * 4, out_specs=[blk] * 3,
        compiler_params=_cparams(("parallel",)),
    )(w, g, m, v)


def _pad128(n):
    return -(-n // 128) * 128


def _pack(arrays):
    offs, parts, off = [], [], 0
    for a in arrays:
        flat = a.reshape(1, -1)
        n = flat.shape[1]
        offs.append(off)
        parts.append(jnp.pad(flat, ((0, 0), (0, _pad128(n) - n))))
        off += _pad128(n)
    return jnp.concatenate(parts, axis=1), offs


def _unpack(vec, offs, shapes):
    out = []
    for off, shp in zip(offs, shapes):
        n = int(np.prod(shp))
        out.append(vec[0, off:off + n].reshape(shp))
    return out


IN_ROWS = IN_DIM // N_DEV
DT_ROW0 = 9216
DT_DEV, DT_LO = divmod(DT_ROW0, IN_ROWS)


GATE_SHIFT = D - 32


def _in_row_pieces(tile):
    pieces = []
    if tile == DT_COL_BLOCK:
        for g in range(N_GROUPS):
            o = DT_ROW0 + HEADS_PER_GROUP * g
            pieces.append((N_STATE * g, o // IN_ROWS, o % IN_ROWS, HEADS_PER_GROUP))
        return pieces
    r, end = tile * D, (tile + 1) * D
    while r < end:
        o = r if r < DT_ROW0 else r - GATE_SHIFT
        dev, loc = divmod(o, IN_ROWS)
        n = min(end - r, IN_ROWS - loc)
        pieces.append((r - tile * D, dev, loc, n))
        r += n
    return pieces


def assemble_w_in(g_all):
    ntile = N_PROJ // D

    def body(g_ref, o_ref):
        j = pl.program_id(0)
        for tile in range(ntile):
            @pl.when(j == tile)
            def _(tile=tile):
                if tile == DT_COL_BLOCK:
                    o_ref[...] = jnp.zeros_like(o_ref)
                for dst, dev, loc, n in _in_row_pieces(tile):
                    o_ref[pl.ds(dst, n), :] = g_ref[dev, pl.ds(loc, n), :]

    return pl.pallas_call(
        body, name="assemble_w_in", grid=(ntile,),
        out_shape=jax.ShapeDtypeStruct((N_PROJ, D), g_all.dtype),
        in_specs=[pl.BlockSpec(memory_space=pltpu.VMEM)],
        out_specs=pl.BlockSpec((D, D), lambda j: (j, 0)),
        compiler_params=_cparams(("arbitrary",)),
    )(g_all)


def _grad_in_blocks(g_t, core, slot):
    dt0 = DT_COL_BLOCK * D
    dt = g_t[dt0:dt0 + N_GROUPS * N_STATE].reshape(N_GROUPS, N_STATE, D)[:, :HEADS_PER_GROUP].reshape(32, D)
    with_dt = jnp.concatenate([g_t[DT_DEV * IN_ROWS:DT_ROW0], dt,
                               g_t[DT_ROW0 + 32 + GATE_SHIFT:(DT_DEV + 1) * IN_ROWS + GATE_SHIFT]], axis=0)
    blocks = []
    for q in range(N_CHIP):
        if 2 * q + 1 < DT_DEV:
            blk = lax.dynamic_slice_in_dim(g_t, IN_ROWS * (2 * q + core), IN_ROWS, axis=0)
        else:
            assert 2 * q == DT_DEV
            after = g_t[(DT_DEV + 1) * IN_ROWS + GATE_SHIFT:(DT_DEV + 2) * IN_ROWS + GATE_SHIFT]
            blk = jnp.where(core == 0, with_dt, after)
        blocks.append(jnp.pad(blk, ((0, slot - IN_ROWS), (0, 0))))
    return jnp.stack(blocks)


def kernel(x, c, w_ada, b_ada, w_in, hgrn_lb, hgrn_gnorm, ssm_conv_w, ssm_conv_b, ssm_dt_bias, ssm_a_log, ssm_d, ssm_norm, w_branch_a, w_branch_b, w_o, ln1_g, ln1_b, w_ffn_gate, w_ffn_up, w_ffn_down, ln2_g, ln2_b, loss_target, m_w_ada, m_b_ada, m_w_in, m_hgrn_lb, m_hgrn_gnorm, m_ssm_conv_w, m_ssm_conv_b, m_ssm_dt_bias, m_ssm_a_log, m_ssm_d, m_ssm_norm, m_w_branch_a, m_w_branch_b, m_w_o, m_ln1_g, m_ln1_b, m_w_ffn_gate, m_w_ffn_up, m_w_ffn_down, m_ln2_g, m_ln2_b, v_w_ada, v_b_ada, v_w_in, v_hgrn_lb, v_hgrn_gnorm, v_ssm_conv_w, v_ssm_conv_b, v_ssm_dt_bias, v_ssm_a_log, v_ssm_d, v_ssm_norm, v_w_branch_a, v_w_branch_b, v_w_o, v_ln1_g, v_ln1_b, v_w_ffn_gate, v_w_ffn_up, v_w_ffn_down, v_ln2_g, v_ln2_b):
    me = 4 * lax.axis_index("x") + 2 * lax.axis_index("y") + lax.axis_index("c")
    xt = x[0]
    tgt = loss_target[0]
    t = xt.shape[0]
    ada_cols = w_ada.shape[2]
    conv_cols = ssm_conv_w.shape[2]

    small_in, _ = _pack([c, ssm_conv_w[0]])
    small_all = allgather_vmem(small_in, "allgather_small_inputs")
    c_all = small_all[:, 0, :D]
    conv_w = small_all[:, 0, D:D + CONV_TAPS * conv_cols].reshape(N_DEV, CONV_TAPS, conv_cols)
    conv_w = conv_w.transpose(1, 0, 2).reshape(CONV_TAPS, CONV_DIM)
    mod = ada_modulation(c_all, w_ada[0], b_ada.reshape(N_DEV, 1, ada_cols))
    mod6 = mod.reshape(6, D)

    shards = [w_in[0].T, w_branch_a[0], w_branch_b[0], w_o[0], w_ffn_gate[0].T, w_ffn_up[0].T, w_ffn_down[0]]
    shard_rows = [s.shape[0] for s in shards]
    slot_rows = [-(-r // 32) * 32 for r in shard_rows]
    row_offs = [sum(slot_rows[:i]) for i in range(len(shards))]
    padded = [jnp.pad(s.astype(BF16), ((0, p - r), (0, 0))) for s, r, p in zip(shards, shard_rows, slot_rows)]
    w_in_t = assemble_w_in(allgather_hbm(padded[0], "allgather_w_in"))

    lb = lower_bound_fwd(hgrn_lb)
    u1 = ln_modulate(xt, mod6, 0, 1, "ln_modulate_1")
    proj, g_rest = mm_nt_gather(u1, w_in_t, F32, jnp.concatenate(padded[1:], axis=0), "mm_in_proj")
    g_ba, g_bb, g_o, g_fg, g_fu, g_fd = (g_rest[:, o - slot_rows[0]:o - slot_rows[0] + r]
                                         for o, r in zip(row_offs[1:], shard_rows[1:]))
    w_ba = g_ba.reshape(D, D)
    w_bb = g_bb.reshape(B_INNER, D)
    w_oo = g_o.reshape(D, D)
    w_gu_t = jnp.concatenate([g_fg.reshape(D_FF, D), g_fu.reshape(D_FF, D)], axis=0)
    w_dn = g_fd.reshape(D_FF, D)
    o_a, o_raw, st_a = hgrn_fwd(proj, lb, hgrn_gnorm)
    xc, conv_slope = conv_fwd(proj, conv_w, ssm_conv_b)
    pad3 = ((0, 0), (0, 0), (0, N_STATE - HEADS_PER_GROUP))
    alog4 = jnp.pad(ssm_a_log.reshape(N_GROUPS, 1, HEADS_PER_GROUP), pad3)
    bias4 = jnp.pad(ssm_dt_bias.reshape(N_GROUPS, 1, HEADS_PER_GROUP), pad3)
    dskip4 = jnp.pad(ssm_d.reshape(N_GROUPS, 1, HEADS_PER_GROUP), pad3)
    expand = _head_expand()
    o_b, st_b = ssd_fwd(proj, xc, alog4, bias4, dskip4, ssm_norm, expand)
    ya = mm_nn(o_a, w_ba, BF16, "mm_branch_a")
    yb = mm_nn(o_b, w_bb, BF16, "mm_branch_b")
    merged, h1, x1, u2 = mixer_tail(ya, yb, proj, w_oo, xt, mod6, ln1_g, ln1_b)
    gu, act = ffn_in_act(u2, w_gu_t)

    dh2, dx1_part, acc4 = resid_ln_bwd(x1, (act, w_dn), mod6, 5, ln2_g, ln2_b, tgt, True, "resid_ln_2_bwd")
    g_dn = mm_tn(act, dh2, "mm_grad_ffn_down")
    dgu = ffn_act_bwd(dh2, w_dn, gu)
    g_gu_t = mm_tn(dgu, u2, "mm_grad_ffn_in")
    dh1, dx_part, acc2 = mixer_tail_bwd(x1, dgu, w_gu_t, mod6, dx1_part, xt, h1, ln1_g, ln1_b)
    g_o = mm_tn(merged, dh1, "mm_grad_out_proj")
    dya, dyb, dproj = merge_gates_bwd(dh1, w_oo, ya, yb, proj)
    g_ba_full = mm_tn(o_a, dya, "mm_grad_branch_a")
    g_bb_full = mm_tn(o_b, dyb, "mm_grad_branch_b")
    my_core = lax.axis_index("c")

    def by_core(blocks, rows, slots):
        contrib = jnp.concatenate([jnp.pad(b.reshape(N_DEV, -1, D), ((0, 0), (0, p - r), (0, 0)))
                                   for b, r, p in zip(blocks, rows, slots)], axis=1)
        split = contrib.reshape(N_CHIP, 2, contrib.shape[1], D).transpose(1, 0, 2, 3)
        return (lax.dynamic_index_in_dim(split, my_core, 0, keepdims=False),
                lax.dynamic_index_in_dim(split, 1 - my_core, 0, keepdims=False))

    keep_e, give_e = by_core([g_ba_full, g_bb_full, g_o, g_gu_t[:D_FF], g_gu_t[D_FF:], g_dn],
                             shard_rows[1:], slot_rows[1:])
    dproj, dlb, dgn, got_e = hgrn_bwd(proj, lb, hgrn_gnorm, o_raw, dya, w_ba, st_a, give_e, dproj)
    chip_e = sum_pair(keep_e.reshape(-1, D), got_e.reshape(-1, D), "sum_grads_rest_chip").reshape(keep_e.shape)
    dxs, dbm, dcm, dproj, ddt, dwn, dalog, dbias, ddsk, parts_e = ssd_bwd(proj, xc, alog4, bias4, dskip4, ssm_norm,
                                                                          expand, dyb, w_bb, st_b, chip_e, dproj)
    dxc = jnp.concatenate([dxs, dbm, dcm], axis=1)
    dproj, dcw, dcb = conv_bwd(proj, dxc, conv_slope, conv_w, dproj)
    dproj = dt_fill(ddt, dproj)
    g_in_t = mm_tn(dproj, u1, "mm_grad_in_proj")
    keep_l = _grad_in_blocks(g_in_t, my_core, slot_rows[0])
    give_l = _grad_in_blocks(g_in_t, 1 - my_core, slot_rows[0])
    got_l = exchange_sibling(give_l, "exchange_grad_in_sibling")
    chip_l = sum_pair(keep_l.reshape(-1, D), got_l.reshape(-1, D), "sum_grad_in_chip").reshape(keep_l.shape)
    du1, parts_l = mm_nn_exchange(dproj, w_in_t, F32, chip_l, "mm_du1")
    dx, acc1 = ln_modulate_bwd(xt, du1, mod6, 1, dx_part, "ln_modulate_1_bwd")
    gw_in = sum_parts(parts_l, "sum_grad_in")[:shard_rows[0]].T
    g_rows = sum_parts(parts_e, "sum_grads_rest")
    gw_ba, gw_bb, gw_o, gw_fg, gw_fu, gw_fd = (g_rows[o - slot_rows[0]:o - slot_rows[0] + r]
                                               for o, r in zip(row_offs[1:], shard_rows[1:]))
    gw_fg, gw_fu = gw_fg.T, gw_fu.T

    dmod = jnp.concatenate([acc1[1:2], acc1[0:1], acc2[2:3], acc2[1:2], acc2[0:1], acc4[0:1]], axis=1)
    small_fields = [dmod, acc4[3:4, :128], dlb, dgn, dcw[:CONV_TAPS], dcb, dbias, dalog, ddsk, dwn,
                    acc2[3:4], acc2[4:5], acc4[1:2], acc4[2:3]]
    small_out, offs = _pack(small_fields)
    small_sum_in = allgather_vmem(small_out, "allgather_small_grads")
    gsum, g_lb = reduce_small(small_sum_in, hgrn_lb, offs[2])
    (g_bada, loss_row, _, g_gn, g_cw_full, g_cb, g_bias4, g_alog4, g_dsk4, g_wn, g_l1g, g_l1b, g_l2g, g_l2b) = _unpack(
        gsum, offs, [(1, 6 * D), (1, 128), (1, D), (1, HK), (CONV_TAPS, CONV_DIM), (1, CONV_DIM),
                     (N_GROUPS, N_STATE), (N_GROUPS, N_STATE), (N_GROUPS, N_STATE), (1, B_INNER),
                     (1, D), (1, D), (1, D), (1, D)])
    loss = loss_row[0, 0]
    g_cw = lax.dynamic_slice(g_cw_full, (0, me * conv_cols), (CONV_TAPS, conv_cols))[None]
    g_dtb = g_bias4[:, :HEADS_PER_GROUP].reshape(1, 32)
    g_alog = g_alog4[:, :HEADS_PER_GROUP].reshape(1, 32)
    g_dsk = g_dsk4[:, :HEADS_PER_GROUP].reshape(1, 32)

    dmod_all = small_sum_in[:, 0, offs[0]:offs[0] + 6 * D]
    dmod_cols = lax.dynamic_slice(dmod_all, (0, me * ada_cols), (N_DEV, ada_cols))
    gw_ada = ada_weight_grad(c_all, dmod_cols)

    big = [("ada", w_ada[0], gw_ada, m_w_ada[0], v_w_ada[0]), ("in", w_in[0], gw_in, m_w_in[0], v_w_in[0]),
           ("branch_a", w_branch_a[0], gw_ba, m_w_branch_a[0], v_w_branch_a[0]),
           ("branch_b", w_branch_b[0], gw_bb, m_w_branch_b[0], v_w_branch_b[0]),
           ("o", w_o[0], gw_o, m_w_o[0], v_w_o[0]),
           ("ffn_gate", w_ffn_gate[0], gw_fg, m_w_ffn_gate[0], v_w_ffn_gate[0]),
           ("ffn_up", w_ffn_up[0], gw_fu, m_w_ffn_up[0], v_w_ffn_up[0]),
           ("ffn_down", w_ffn_down[0], gw_fd, m_w_ffn_down[0], v_w_ffn_down[0])]
    big_out = {}
    for nm, w_, g_, m_, v_ in big:
        d_, m2_, v2_ = adamw(w_, g_, m_, v_, "adamw_" + nm)
        big_out[nm] = (g_[None], d_[None], m2_[None], v2_[None])

    small_w = [b_ada, hgrn_lb, hgrn_gnorm, ssm_conv_w, ssm_conv_b, ssm_dt_bias, ssm_a_log, ssm_d, ssm_norm,
               ln1_g, ln1_b, ln2_g, ln2_b]
    small_g = [g_bada, g_lb, g_gn, g_cw, g_cb, g_dtb, g_alog, g_dsk, g_wn, g_l1g, g_l1b, g_l2g, g_l2b]
    small_m = [m_b_ada, m_hgrn_lb, m_hgrn_gnorm, m_ssm_conv_w, m_ssm_conv_b, m_ssm_dt_bias, m_ssm_a_log, m_ssm_d,
               m_ssm_norm, m_ln1_g, m_ln1_b, m_ln2_g, m_ln2_b]
    small_v = [v_b_ada, v_hgrn_lb, v_hgrn_gnorm, v_ssm_conv_w, v_ssm_conv_b, v_ssm_dt_bias, v_ssm_a_log, v_ssm_d,
               v_ssm_norm, v_ln1_g, v_ln1_b, v_ln2_g, v_ln2_b]
    shapes = [a.shape for a in small_w]
    small_g = [g_.reshape(s) for g_, s in zip(small_g, shapes)]
    pw, poffs = _pack(small_w)
    pg, _ = _pack(small_g)
    pm, _ = _pack(small_m)
    pv, _ = _pack(small_v)
    pd, pm2, pv2 = adamw(pw, pg, pm, pv, "adamw_small")
    s_d, s_m, s_v = (_unpack(p, poffs, shapes) for p in (pd, pm2, pv2))
    (sn_bada, sn_lb, sn_gn, sn_cw, sn_cb, sn_dtb, sn_alog, sn_dsk, sn_wn, sn_l1g, sn_l1b, sn_l2g, sn_l2b) = range(13)

    def order(kind):
        sm = [small_g, s_d, s_m, s_v][kind]
        bg = lambda nm: big_out[nm][kind]
        return [bg("ada"), sm[sn_bada], bg("in"), sm[sn_lb], sm[sn_gn], sm[sn_cw], sm[sn_cb], sm[sn_dtb], sm[sn_alog],
                sm[sn_dsk], sm[sn_wn], bg("branch_a"), bg("branch_b"), bg("o"), sm[sn_l1g], sm[sn_l1b],
                bg("ffn_gate"), bg("ffn_up"), bg("ffn_down"), sm[sn_l2g], sm[sn_l2b]]

    return (loss, dx[None], *order(0), *order(1), *order(2), *order(3))
```

```python
import numpy as np
import jax
import jax.numpy as jnp
from jax import lax
from jax.experimental import pallas as pl
from jax.experimental.pallas import tpu as pltpu

F32 = jnp.float32
BF16 = jnp.bfloat16
HI = lax.Precision.HIGHEST

N_DEV = 8
D = 1024
N_HEADS_A = 8
HK = 128
CHUNK = 64
SSD_CHUNK = 128
SSD_CHUNK_BWD = 256
N_GROUPS = 4
HEADS_PER_GROUP = 8
HEAD_P = 64
N_STATE = 128
GROUP_W = HEADS_PER_GROUP * HEAD_P
B_INNER = 2048
CONV_DIM = 3072
D_FF = 2816
IN_DIM = 11296
N_PROJ = 12288
ALPHA = 2.0 ** 0.25
LN_EPS = 1e-5
RMS_EPS = 1e-6
Q_SCALE = 128 ** -0.5
EXP_CLIP = 80.0
ADAM_LR, ADAM_B1, ADAM_B2, ADAM_EPS, ADAM_WD, ADAM_STEP = 0.001, 0.9, 0.999, 1e-8, 0.01, 10
VMEM_LIMIT = 48 * 1024 * 1024
TOKEN_BLOCK = 1024
ROW_TILE = 512
WIDE_ROW_TILE = 1024
MM_ROW_TILE = 1024
MM_TOKEN_TILE = 4096
MM_K_TILE = 3072
MM_COL_TILE = 1408
HGRN_HEADS_PER_STEP = 4
CHUNK_UNROLL = 8
MESH_ID = pl.DeviceIdType.MESH

NT_DIMS = (((1,), (1,)), ((), ()))
TN_DIMS = (((0,), (0,)), ((), ()))


def _cparams(sem=None):
    return pltpu.CompilerParams(dimension_semantics=sem, vmem_limit_bytes=VMEM_LIMIT)


def _sigmoid(x):
    return 1.0 / (1.0 + jnp.exp(-x))


def _dsilu(x, s):
    return s * (1.0 + x * (1.0 - s))


def _nt(a, b, precision=None):
    return lax.dot_general(a, b, NT_DIMS, precision=precision, preferred_element_type=F32)


def _tn(a, b, precision=None):
    return lax.dot_general(a, b, TN_DIMS, precision=precision, preferred_element_type=F32)


def _nn(a, b, precision=None):
    return jnp.dot(a, b, precision=precision, preferred_element_type=F32)


def _split(x, pieces):
    out = []
    for i in range(pieces):
        p = x.astype(BF16)
        out.append(p)
        if i + 1 < pieces:
            x = x - p.astype(F32)
    return out


def _sel(dot, x, sel01, pieces, x_first=True):
    acc = None
    for p in _split(x, pieces):
        term = dot(p, sel01) if x_first else dot(sel01, p)
        acc = term if acc is None else acc + term
    return acc


def _ln(x):
    mu = jnp.mean(x, axis=-1, keepdims=True)
    xc = x - mu
    rstd = lax.rsqrt(jnp.mean(xc * xc, axis=-1, keepdims=True) + LN_EPS)
    return xc * rstd, rstd


def _ln_bwd(dxh, xh, rstd):
    return rstd * (dxh - jnp.mean(dxh, axis=-1, keepdims=True) - xh * jnp.mean(dxh * xh, axis=-1, keepdims=True))


def _colsum(x):
    return jnp.sum(x, axis=0, keepdims=True)


def _tri(n, upper=False):
    r = lax.broadcasted_iota(jnp.int32, (n, n), 0)
    c = lax.broadcasted_iota(jnp.int32, (n, n), 1)
    return (c >= r) if upper else (r >= c)


def _my_pos():
    return lax.axis_index("x"), lax.axis_index("y"), lax.axis_index("c")


def _peer(pos, k):
    x, y, c = pos
    return (x ^ ((k >> 2) & 1), y ^ ((k >> 1) & 1), c ^ (k & 1))


def _flat(pos):
    return 4 * pos[0] + 2 * pos[1] + pos[2]


def allgather_vmem(v, name):
    n = v.shape[1]

    def body(v_ref, o_ref, send_sems, recv_sems, local_sem):
        me = _my_pos()
        mine = pltpu.make_async_copy(v_ref, o_ref.at[_flat(me)], local_sem)
        mine.start()
        sends = []
        for k in range(1, N_DEV):
            peer = _peer(me, k)
            cp = pltpu.make_async_remote_copy(v_ref, o_ref.at[_flat(me)], send_sems.at[k - 1], recv_sems.at[k - 1],
                                              device_id=peer, device_id_type=MESH_ID)
            cp.start()
            sends.append(cp)
        for k in range(1, N_DEV):
            peer = _peer(me, k)
            pltpu.make_async_remote_copy(v_ref, o_ref.at[_flat(peer)], send_sems.at[k - 1], recv_sems.at[k - 1],
                                         device_id=peer, device_id_type=MESH_ID).wait_recv()
        for cp in sends:
            cp.wait_send()
        mine.wait()

    return pl.pallas_call(
        body, name=name,
        out_shape=jax.ShapeDtypeStruct((N_DEV, 1, n), F32),
        in_specs=[pl.BlockSpec(memory_space=pltpu.VMEM)],
        out_specs=pl.BlockSpec(memory_space=pltpu.VMEM),
        scratch_shapes=[pltpu.SemaphoreType.DMA((N_DEV - 1,)), pltpu.SemaphoreType.DMA((N_DEV - 1,)),
                        pltpu.SemaphoreType.DMA],
        compiler_params=_cparams(),
    )(v)


def ada_modulation(c_all, w_ada_s, b_ada_r):
    ncol = w_ada_s.shape[1]

    def body(c_ref, w_ref, b_ref, o_ref, part_ref, send_sems, recv_sems):
        me = _my_pos()
        cval = c_ref[...]
        cond = cval * _sigmoid(cval)
        part = _nn(cond, w_ref[...], HI)
        for r in range(N_DEV):
            part_ref[r] = part[r:r + 1, :]
        sends = []
        for k in range(1, N_DEV):
            peer = _peer(me, k)
            cp = pltpu.make_async_remote_copy(part_ref.at[_flat(peer)], o_ref.at[_flat(me)], send_sems.at[k - 1],
                                              recv_sems.at[k - 1], device_id=peer, device_id_type=MESH_ID)
            cp.start()
            sends.append(cp)
        o_ref[_flat(me)] = part_ref[_flat(me)]
        for k in range(1, N_DEV):
            peer = _peer(me, k)
            pltpu.make_async_remote_copy(part_ref.at[_flat(peer)], o_ref.at[_flat(peer)], send_sems.at[k - 1],
                                         recv_sems.at[k - 1], device_id=peer, device_id_type=MESH_ID).wait_recv()
        for cp in sends:
            cp.wait_send()
        o_ref[...] = o_ref[...] + b_ref[...]

    return pl.pallas_call(
        body, name="ada_modulation",
        out_shape=jax.ShapeDtypeStruct((N_DEV, 1, ncol), F32),
        in_specs=[pl.BlockSpec(memory_space=pltpu.VMEM)] * 3,
        out_specs=pl.BlockSpec(memory_space=pltpu.VMEM),
        scratch_shapes=[pltpu.VMEM((N_DEV, 1, ncol), F32), pltpu.SemaphoreType.DMA((N_DEV - 1,)),
                        pltpu.SemaphoreType.DMA((N_DEV - 1,))],
        compiler_params=_cparams(),
    )(c_all, w_ada_s, b_ada_r)


def allgather_hbm(shard, name):
    def body(x_ref, out_ref, send_sems, recv_sems, local_sem):
        x, y, c = _my_pos()
        me, sibling = (x, y, c), (x, y, 1 - c)
        chips = [(1 - x, y), (x, 1 - y), (1 - x, 1 - y)]

        def slot(pos):
            return out_ref.at[_flat(pos)]

        def copy(k, block, to, src=None):
            return pltpu.make_async_remote_copy(slot(block) if src is None else src, slot(block), send_sems.at[k],
                                                recv_sems.at[k], device_id=to, device_id_type=MESH_ID)

        mine = pltpu.make_async_copy(x_ref, slot(me), local_sem)
        mine.start()
        first = [copy(0, me, sibling, src=x_ref)]
        first += [copy(1 + j, me, (*chip, c), src=x_ref) for j, chip in enumerate(chips)]
        for cp in first:
            cp.start()
        passed = [copy(4 + j, (*chip, c), sibling) for j, chip in enumerate(chips)]
        for j, chip in enumerate(chips):
            copy(1 + j, (*chip, c), me).wait_recv()
            passed[j].start()
        copy(0, sibling, me).wait_recv()
        for j, chip in enumerate(chips):
            copy(4 + j, (*chip, 1 - c), me).wait_recv()
        for cp in first + passed:
            cp.wait_send()
        mine.wait()

    return pl.pallas_call(
        body, name=name,
        out_shape=jax.ShapeDtypeStruct((N_DEV,) + shard.shape, shard.dtype),
        in_specs=[pl.BlockSpec(memory_space=pl.ANY)],
        out_specs=pl.BlockSpec(memory_space=pl.ANY),
        scratch_shapes=[pltpu.SemaphoreType.DMA((N_DEV - 1,)), pltpu.SemaphoreType.DMA((N_DEV - 1,)),
                        pltpu.SemaphoreType.DMA],
        compiler_params=_cparams(),
    )(shard)


N_CHIP = N_DEV // 2
SIBLING_SEMS = [pltpu.SemaphoreType.DMA, pltpu.SemaphoreType.DMA]
CHIP_SEMS = [pltpu.SemaphoreType.DMA((N_CHIP - 1,)), pltpu.SemaphoreType.DMA((N_CHIP - 1,)), pltpu.SemaphoreType.DMA]


def _sibling_exchange(s_ref, o_ref, send_sem, recv_sem):
    x, y, c = _my_pos()
    cp = pltpu.make_async_remote_copy(s_ref, o_ref, send_sem, recv_sem, device_id=(x, y, 1 - c), device_id_type=MESH_ID)
    return cp.start, cp.wait


def _chip_exchange(p_ref, o_ref, send_sems, recv_sems, local_sem):
    x, y, c = _my_pos()
    my_chip = 2 * x + y
    mine = pltpu.make_async_copy(p_ref.at[my_chip], o_ref.at[my_chip], local_sem)
    peers = [(x ^ (k >> 1), y ^ (k & 1)) for k in range(1, N_CHIP)]
    sends = [pltpu.make_async_remote_copy(p_ref.at[2 * px + py], o_ref.at[my_chip], send_sems.at[k], recv_sems.at[k],
                                          device_id=(px, py, c), device_id_type=MESH_ID)
             for k, (px, py) in enumerate(peers)]
    recvs = [pltpu.make_async_remote_copy(p_ref.at[2 * px + py], o_ref.at[2 * px + py], send_sems.at[k], recv_sems.at[k],
                                          device_id=(px, py, c), device_id_type=MESH_ID)
             for k, (px, py) in enumerate(peers)]

    def start():
        mine.start()
        for cp in sends:
            cp.start()

    def wait():
        for cp in recvs:
            cp.wait_recv()
        for cp in sends:
            cp.wait_send()
        mine.wait()

    return start, wait


def exchange_sibling(send, name):
    def body(s_ref, o_ref, send_sem, recv_sem):
        start, wait = _sibling_exchange(s_ref, o_ref, send_sem, recv_sem)
        start()
        wait()

    return pl.pallas_call(
        body, name=name,
        out_shape=jax.ShapeDtypeStruct(send.shape, send.dtype),
        in_specs=[pl.BlockSpec(memory_space=pl.ANY)],
        out_specs=pl.BlockSpec(memory_space=pl.ANY),
        scratch_shapes=SIBLING_SEMS,
        compiler_params=_cparams(),
    )(send)


LANES = 128


def _k_tile(kdim, unit=LANES):
    for cand in range(MM_K_TILE - MM_K_TILE % unit, 0, -unit):
        if kdim % cand == 0:
            return cand
    return kdim


def _lane_tile(n, cap):
    for cand in range(cap - cap % LANES, 0, -LANES):
        if n % cand == 0:
            return cand
    return n


def _m_tile(m, kdim):
    return min(MM_ROW_TILE if kdim > D else 2 * MM_ROW_TILE, m)


def mm_nn_exchange(a, b, out_dtype, part, name):
    kblocks, m, kb = a.shape
    kdim = kblocks * kb
    n = b.shape[1]
    tm, tn, tk = min(MM_ROW_TILE, m), _lane_tile(n, MM_COL_TILE), _k_tile(kdim)
    gn, gm, nk = n // tn, m // tm, kdim // tk
    per_step = tk // kb

    def body(a_ref, b_ref, part_ref, o_ref, parts_ref, acc_ref, send_sems, recv_sems, local_sem):
        j, i, k = pl.program_id(0), pl.program_id(1), pl.program_id(2)
        xchg_start, xchg_wait = _chip_exchange(part_ref, parts_ref, send_sems, recv_sems, local_sem)

        @pl.when((j == 0) & (i == 0) & (k == 0))
        def _():
            xchg_start()

        p = _nn(a_ref[0], b_ref[0:kb, :])
        for c in range(1, per_step):
            p = p + _nn(a_ref[c], b_ref[c * kb:(c + 1) * kb, :])

        @pl.when(k == 0)
        def _():
            acc_ref[...] = p

        @pl.when(k > 0)
        def _():
            acc_ref[...] += p

        @pl.when(k == nk - 1)
        def _():
            o_ref[...] = acc_ref[...].astype(o_ref.dtype)

        @pl.when((j == gn - 1) & (i == gm - 1) & (k == nk - 1))
        def _():
            xchg_wait()

    hbm = pl.BlockSpec(memory_space=pl.ANY)
    return pl.pallas_call(
        body, name=name, grid=(gn, gm, nk),
        out_shape=[jax.ShapeDtypeStruct((m, n), out_dtype), jax.ShapeDtypeStruct(part.shape, part.dtype)],
        in_specs=[pl.BlockSpec((per_step, tm, kb), lambda j, i, k: (k, i, 0)),
                  pl.BlockSpec((tk, tn), lambda j, i, k: (k, j)), hbm],
        out_specs=[pl.BlockSpec((tm, tn), lambda j, i, k: (i, j)), hbm],
        scratch_shapes=[pltpu.VMEM((tm, tn), F32)] + CHIP_SEMS,
        compiler_params=_cparams(("arbitrary", "arbitrary", "arbitrary")),
    )(a, b, part)


def mm_nt_gather(a, b, out_dtype, shard, name):
    m, kdim = a.shape
    n = b.shape[0]
    tm, tn = _m_tile(m, kdim), 1024
    assert kdim == 1024
    gj = m // tm
    nsteps = (n // tn) * gj
    forward_step = max(nsteps - 2, 0)

    def body(a_ref, b_ref, x_ref, o_ref, g_ref, send_sems, recv_sems, local_sem):
        step = pl.program_id(0) * gj + pl.program_id(1)
        x, y, c = _my_pos()
        me, sibling = (x, y, c), (x, y, 1 - c)
        chips = [(1 - x, y), (x, 1 - y), (1 - x, 1 - y)]

        def slot(pos):
            return g_ref.at[_flat(pos)]

        def copy(k, block, to, src=None):
            return pltpu.make_async_remote_copy(slot(block) if src is None else src, slot(block), send_sems.at[k],
                                                recv_sems.at[k], device_id=to, device_id_type=MESH_ID)

        mine = pltpu.make_async_copy(x_ref, slot(me), local_sem)
        first = [copy(0, me, sibling, src=x_ref)]
        first += [copy(1 + j, me, (*chip, c), src=x_ref) for j, chip in enumerate(chips)]
        passed = [copy(4 + j, (*chip, c), sibling) for j, chip in enumerate(chips)]

        @pl.when(step == 0)
        def _():
            mine.start()
            for cp in first:
                cp.start()

        rows = pl.ds(pl.multiple_of(pl.program_id(1) * tm, tm), tm)
        o_ref[...] = _nt(a_ref[rows, :], b_ref[...]).astype(o_ref.dtype)

        @pl.when(step == forward_step)
        def _():
            for j, chip in enumerate(chips):
                copy(1 + j, (*chip, c), me).wait_recv()
                passed[j].start()

        @pl.when(step == nsteps - 1)
        def _():
            copy(0, sibling, me).wait_recv()
            for j, chip in enumerate(chips):
                copy(4 + j, (*chip, 1 - c), me).wait_recv()
            for cp in first + passed:
                cp.wait_send()
            mine.wait()

    return pl.pallas_call(
        body, name=name, grid=(n // tn, gj),
        out_shape=[jax.ShapeDtypeStruct((m, n), out_dtype), jax.ShapeDtypeStruct((N_DEV,) + shard.shape, shard.dtype)],
        in_specs=[pl.BlockSpec(memory_space=pltpu.VMEM), pl.BlockSpec((tn, kdim), lambda j, i: (j, 0)),
                  pl.BlockSpec(memory_space=pl.ANY)],
        out_specs=[pl.BlockSpec((tm, tn), lambda j, i: (i, j)), pl.BlockSpec(memory_space=pl.ANY)],
        scratch_shapes=[pltpu.SemaphoreType.DMA((N_DEV - 1,)), pltpu.SemaphoreType.DMA((N_DEV - 1,)),
                        pltpu.SemaphoreType.DMA],
        compiler_params=_cparams(("arbitrary", "arbitrary")),
    )(a, b, shard)


def mm_tn(a, b, name):
    tt, tn = min(MM_TOKEN_TILE, b.shape[0]), _lane_tile(b.shape[1], MM_COL_TILE)
    tka = _lane_tile(a.shape[-1], 1024)
    if a.ndim == 3:
        t, ka = a.shape[1], a.shape[0] * a.shape[2]
        per = a.shape[2] // tka
        a_spec = pl.BlockSpec((None, tt, tka), lambda i, j, s: (i // per, s, i % per))
    else:
        t, ka = a.shape
        a_spec = pl.BlockSpec((tt, tka), lambda i, j, s: (s, i))
    n = b.shape[1]
    nt = t // tt

    def body(a_ref, b_ref, o_ref, *acc):
        p = _tn(a_ref[...], b_ref[...])
        if nt == 1:
            o_ref[...] = p.astype(o_ref.dtype)
        else:
            acc_ref, s = acc[0], pl.program_id(2)

            @pl.when(s == 0)
            def _():
                acc_ref[...] = p

            @pl.when(s > 0)
            def _():
                acc_ref[...] += p

            @pl.when(s == nt - 1)
            def _():
                o_ref[...] = acc_ref[...].astype(o_ref.dtype)

    return pl.pallas_call(
        body, name=name, grid=(ka // tka, n // tn, nt),
        out_shape=jax.ShapeDtypeStruct((ka, n), BF16),
        in_specs=[a_spec, pl.BlockSpec((tt, tn), lambda i, j, s: (s, j))],
        out_specs=pl.BlockSpec((tka, tn), lambda i, j, s: (i, j)),
        scratch_shapes=[] if nt == 1 else [pltpu.VMEM((tka, tn), F32)],
        compiler_params=_cparams(("parallel", "parallel", "arbitrary")),
    )(a, b)


def _tile(t, cap):
    return min(cap, t)


def ln_modulate(x, mod6, shift_row, scale_row, name):
    t = x.shape[0]
    tm = _tile(t, WIDE_ROW_TILE)

    def body(x_ref, mod_ref, o_ref):
        xh, _ = _ln(x_ref[...])
        sc = mod_ref[scale_row:scale_row + 1, :]
        sh = mod_ref[shift_row:shift_row + 1, :]
        o_ref[...] = (xh * (1.0 + sc) + sh).astype(BF16)

    return pl.pallas_call(
        body, name=name, grid=(t // tm,),
        out_shape=jax.ShapeDtypeStruct((t, D), BF16),
        in_specs=[pl.BlockSpec((tm, D), lambda i: (i, 0)), pl.BlockSpec((6, D), lambda i: (0, 0))],
        out_specs=pl.BlockSpec((tm, D), lambda i: (i, 0)),
        compiler_params=_cparams(("parallel",)),
    )(x, mod6)


def ffn_tail_loss_bwd(x1, act, w_dn, mod6, ln_g, ln_b, target):
    t = x1.shape[0]
    tm = _tile(t, ROW_TILE)
    kdim = act.shape[1]

    def body(x_ref, a_ref, w_ref, mod_ref, g_ref, b_ref, c_ref, dh_ref, dx_ref, acc_ref):
        @pl.when(pl.program_id(0) == 0)
        def _():
            acc_ref[...] = jnp.zeros_like(acc_ref)

        hv = _nn(a_ref[...], w_ref[...])
        gate = mod_ref[5:6, :]
        rh, rstd = _ln(ALPHA * x_ref[...] + gate * hv)
        lng = g_ref[...]
        diff = rh * lng + b_ref[...] - c_ref[...]
        dxo = diff * (1.0 / D)
        lsum = jnp.sum(_colsum(diff * diff), axis=-1, keepdims=True) * (0.5 / D)
        acc_ref[3:4, :] += jnp.broadcast_to(lsum, (1, D))
        acc_ref[1:2, :] += _colsum(dxo * rh)
        acc_ref[2:3, :] += _colsum(dxo)
        dr = _ln_bwd(dxo * lng, rh, rstd)
        acc_ref[0:1, :] += _colsum(dr * hv)
        dh_ref[...] = (gate * dr).astype(BF16)
        dx_ref[...] = ALPHA * dr

    row = pl.BlockSpec((tm, D), lambda i: (i, 0))
    vec = pl.BlockSpec((1, D), lambda i: (0, 0))
    return pl.pallas_call(
        body, name="ffn_tail_loss_bwd", grid=(t // tm,),
        out_shape=[jax.ShapeDtypeStruct((t, D), BF16), jax.ShapeDtypeStruct((t, D), F32),
                   jax.ShapeDtypeStruct((8, D), F32)],
        in_specs=[row, pl.BlockSpec((tm, kdim), lambda i: (i, 0)), pl.BlockSpec((kdim, D), lambda i: (0, 0)),
                  pl.BlockSpec((6, D), lambda i: (0, 0)), vec, vec, row],
        out_specs=[row, row, pl.BlockSpec((8, D), lambda i: (0, 0))],
        compiler_params=_cparams(("arbitrary",)),
    )(x1, act, w_dn, mod6, ln_g, ln_b, target)


def ln_modulate_bwd(x, du, mod6, scale_row, dx_part, name):
    t = x.shape[0]
    tm = _tile(t, ROW_TILE)

    def body(x_ref, du_ref, mod_ref, dp_ref, dx_ref, acc_ref):
        @pl.when(pl.program_id(0) == 0)
        def _():
            acc_ref[...] = jnp.zeros_like(acc_ref)

        xh, rstd = _ln(x_ref[...])
        du_v = du_ref[...]
        sc = mod_ref[scale_row:scale_row + 1, :]
        acc_ref[0:1, :] += _colsum(du_v * xh)
        acc_ref[1:2, :] += _colsum(du_v)
        dx_ref[...] = dp_ref[...] + _ln_bwd(du_v * (1.0 + sc), xh, rstd)

    row = pl.BlockSpec((tm, D), lambda i: (i, 0))
    return pl.pallas_call(
        body, name=name, grid=(t // tm,),
        out_shape=[jax.ShapeDtypeStruct((t, D), F32), jax.ShapeDtypeStruct((8, D), F32)],
        in_specs=[row, row, pl.BlockSpec((6, D), lambda i: (0, 0)), row],
        out_specs=[row, pl.BlockSpec((8, D), lambda i: (0, 0))],
        compiler_params=_cparams(("arbitrary",)),
    )(x, du, mod6, dx_part)


def mixer_tail_bwd(x1, dgu, w_gu_t, mod6, dx1_part, x, h, ln_g, ln_b):
    t = x.shape[0]
    tm = _tile(t, ROW_TILE // 2)
    nblk, _, kb = dgu.shape

    def body(x1_ref, a_ref, w_ref, mod_ref, dp_ref, x_ref, h_ref, g_ref, b_ref, dh_ref, dx_ref, acc_ref):
        @pl.when(pl.program_id(0) == 0)
        def _():
            acc_ref[...] = jnp.zeros_like(acc_ref)

        du = _nn(a_ref[0], w_ref[0:kb, :])
        for blk in range(1, nblk):
            du = du + _nn(a_ref[blk], w_ref[blk * kb:(blk + 1) * kb, :])
        xh, rstd1 = _ln(x1_ref[...])
        acc_ref[0:1, :] += _colsum(du * xh)
        acc_ref[1:2, :] += _colsum(du)
        dx1 = dp_ref[...] + _ln_bwd(du * (1.0 + mod_ref[4:5, :]), xh, rstd1)
        gate = mod_ref[2:3, :]
        hv = h_ref[...]
        rh, rstd = _ln(ALPHA * x_ref[...] + gate * hv)
        acc_ref[3:4, :] += _colsum(dx1 * rh)
        acc_ref[4:5, :] += _colsum(dx1)
        dr = _ln_bwd(dx1 * g_ref[...], rh, rstd)
        acc_ref[2:3, :] += _colsum(dr * hv)
        dh_ref[...] = (gate * dr).astype(BF16)
        dx_ref[...] = ALPHA * dr

    row = pl.BlockSpec((tm, D), lambda i: (i, 0))
    vec = pl.BlockSpec((1, D), lambda i: (0, 0))
    return pl.pallas_call(
        body, name="mixer_tail_bwd", grid=(t // tm,),
        out_shape=[jax.ShapeDtypeStruct((t, D), BF16), jax.ShapeDtypeStruct((t, D), F32),
                   jax.ShapeDtypeStruct((8, D), F32)],
        in_specs=[row, pl.BlockSpec((nblk, tm, kb), lambda i: (0, i, 0)), pl.BlockSpec(memory_space=pltpu.VMEM),
                  pl.BlockSpec((6, D), lambda i: (0, 0)), row, row, row, vec, vec],
        out_specs=[row, row, pl.BlockSpec((8, D), lambda i: (0, 0))],
        compiler_params=_cparams(("arbitrary",)),
    )(x1, dgu, w_gu_t, mod6, dx1_part, x, h, ln_g, ln_b)


def mixer_tail(o_a, o_b, w_ba, w_bb, proj, w_o, x, mod6, ln_g, ln_b):
    t = o_a.shape[0]
    tm = _tile(t, ROW_TILE // 2)

    def body(oa_ref, ob_ref, wa_ref, wb_ref, ga_ref, gb_ref, w_ref, x_ref, mod_ref, g_ref, b_ref,
             ya_ref, yb_ref, m_ref, h_ref, x1_ref, u2_ref):
        ya = _nn(oa_ref[...], wa_ref[...])
        yb = _nn(ob_ref[...], wb_ref[...])
        ya_ref[...] = ya.astype(BF16)
        yb_ref[...] = yb.astype(BF16)
        merged = (_sigmoid(ga_ref[...]) * ya + _sigmoid(gb_ref[...]) * yb).astype(BF16)
        m_ref[...] = merged
        hv = _nn(merged, w_ref[...])
        h_ref[...] = hv
        rh, _ = _ln(ALPHA * x_ref[...] + mod_ref[2:3, :] * hv)
        x1 = rh * g_ref[...] + b_ref[...]
        x1_ref[...] = x1
        xh, _ = _ln(x1)
        u2_ref[...] = (xh * (1.0 + mod_ref[4:5, :]) + mod_ref[3:4, :]).astype(BF16)

    row = pl.BlockSpec((tm, D), lambda i: (i, 0))
    vec = pl.BlockSpec((1, D), lambda i: (0, 0))
    whole = pl.BlockSpec(memory_space=pltpu.VMEM)
    return pl.pallas_call(
        body, name="mixer_tail", grid=(t // tm,),
        out_shape=[jax.ShapeDtypeStruct((t, D), BF16), jax.ShapeDtypeStruct((t, D), BF16),
                   jax.ShapeDtypeStruct((t, D), BF16), jax.ShapeDtypeStruct((t, D), F32),
                   jax.ShapeDtypeStruct((t, D), F32), jax.ShapeDtypeStruct((t, D), BF16)],
        in_specs=[row, pl.BlockSpec((tm, o_b.shape[1]), lambda i: (i, 0)), whole, whole,
                  pl.BlockSpec((tm, D), lambda i: (i, GATE_BLOCK0)),
                  pl.BlockSpec((tm, D), lambda i: (i, GATE_BLOCK0 + 1)), whole,
                  row, pl.BlockSpec((6, D), lambda i: (0, 0)), vec, vec],
        out_specs=[row] * 6,
        compiler_params=_cparams(("parallel",)),
    )(o_a, o_b, w_ba, w_bb, proj, proj, w_o, x, mod6, ln_g, ln_b)


def merge_gates_bwd(dh, w_o, ya, yb, proj):
    t = ya.shape[0]
    tm = _tile(t, ROW_TILE)

    def body(dh_ref, w_ref, ya_ref, yb_ref, ga_ref, gb_ref, dya_ref, dyb_ref, dp_ref):
        dmv = _nt(dh_ref[...], w_ref[...])
        sa = _sigmoid(ga_ref[...])
        sb = _sigmoid(gb_ref[...])
        dya_ref[...] = (dmv * sa).astype(BF16)
        dyb_ref[...] = (dmv * sb).astype(BF16)
        dp_ref[0] = (dmv * ya_ref[...].astype(F32) * sa * (1.0 - sa)).astype(BF16)
        dp_ref[1] = (dmv * yb_ref[...].astype(F32) * sb * (1.0 - sb)).astype(BF16)

    row = pl.BlockSpec((tm, D), lambda i: (i, 0))
    return pl.pallas_call(
        body, name="merge_gates_bwd", grid=(t // tm,),
        out_shape=[jax.ShapeDtypeStruct((t, D), BF16)] * 2 + [jax.ShapeDtypeStruct((N_PROJ // D, t, D), BF16)],
        in_specs=[row, pl.BlockSpec((D, D), lambda i: (0, 0)), row, row,
                  pl.BlockSpec((tm, D), lambda i: (i, GATE_BLOCK0)),
                  pl.BlockSpec((tm, D), lambda i: (i, GATE_BLOCK0 + 1))],
        out_specs=[row, row, pl.BlockSpec((2, tm, D), lambda i: (GATE_BLOCK0 // 2, i, 0))],
        compiler_params=_cparams(("parallel",)),
    )(dh, w_o, ya, yb, proj, proj)


FF_CHUNK = 1408


def ffn_in_act(u, w_gu_t):
    t = u.shape[0]
    tm = _tile(t, ROW_TILE)
    nj = D_FF // FF_CHUNK

    def body(a_ref, bg_ref, bu_ref, gu_ref, act_ref):
        a = a_ref[...]
        g = _nt(a, bg_ref[...])
        up = _nt(a, bu_ref[...])
        gu_ref[0] = g.astype(BF16)
        gu_ref[1] = up.astype(BF16)
        act_ref[...] = (g * _sigmoid(g) * up).astype(BF16)

    return pl.pallas_call(
        body, name="ffn_in_act", grid=(nj, t // tm),
        out_shape=[jax.ShapeDtypeStruct((2, t, D_FF), BF16), jax.ShapeDtypeStruct((t, D_FF), BF16)],
        in_specs=[pl.BlockSpec((tm, D), lambda j, i: (i, 0)), pl.BlockSpec((FF_CHUNK, D), lambda j, i: (j, 0)),
                  pl.BlockSpec((FF_CHUNK, D), lambda j, i: (nj + j, 0))],
        out_specs=[pl.BlockSpec((2, tm, FF_CHUNK), lambda j, i: (0, i, j)),
                   pl.BlockSpec((tm, FF_CHUNK), lambda j, i: (i, j))],
        compiler_params=_cparams(("parallel", "parallel")),
    )(u, w_gu_t, w_gu_t)


def ffn_act_bwd(dh, w_dn, gu):
    t = dh.shape[0]
    tm = _tile(t, ROW_TILE)

    def body(a_ref, b_ref, gu_ref, o_ref):
        da = _nt(a_ref[...], b_ref[...])
        g = gu_ref[0].astype(F32)
        up = gu_ref[1].astype(F32)
        s = _sigmoid(g)
        o_ref[0] = (da * up * _dsilu(g, s)).astype(BF16)
        o_ref[1] = (da * g * s).astype(BF16)

    blk = pl.BlockSpec((2, tm, FF_CHUNK), lambda j, i: (0, i, j))
    return pl.pallas_call(
        body, name="ffn_act_bwd", grid=(D_FF // FF_CHUNK, t // tm),
        out_shape=jax.ShapeDtypeStruct((2, t, D_FF), BF16),
        in_specs=[pl.BlockSpec((tm, D), lambda j, i: (i, 0)), pl.BlockSpec((FF_CHUNK, D), lambda j, i: (j, 0)), blk],
        out_specs=blk,
        compiler_params=_cparams(("parallel", "parallel")),
    )(dh, w_dn, gu)


def _hgrn_chunk_terms(q, fl, lbv, tril_f):
    sig = _sigmoid(fl)
    f = lbv + (1.0 - lbv) * sig
    lam = jnp.log(f)
    k = 1.0 - f
    sq = _sigmoid(q)
    qt = q * sq * Q_SCALE
    bc = _sel(_nn, lam, tril_f, 3, x_first=False)
    bmid = bc[CHUNK // 2 - 1:CHUNK // 2, :]
    bl = bc[CHUNK - 1:CHUNK, :]
    eq = jnp.exp(jnp.minimum(bc - bmid, EXP_CLIP))
    ek = jnp.exp(jnp.minimum(bmid - bc, EXP_CLIP))
    eb = jnp.exp(bc)
    ekl = jnp.exp(bl - bc)
    ebl = jnp.exp(bl)
    return sig, f, k, sq, qt, eq, ek, eb, ekl, ebl


def hgrn_fwd(proj, lb, gnorm):
    t = proj.shape[0]
    tb = _tile(t, TOKEN_BLOCK)
    ncb = tb // CHUNK

    hps = HGRN_HEADS_PER_STEP
    wide = hps * HK

    def body(q_ref, f_ref, i_ref, g_ref, lb_ref, gn_ref, oa_ref, oraw_ref, st_ref, state):
        @pl.when(pl.program_id(1) == 0)
        def _():
            state[...] = jnp.zeros_like(state)

        gn = gn_ref[...]
        mask = _tri(CHUNK)
        tril_f = mask.astype(BF16)

        def chunk(c, carry):
            sl = pl.ds(pl.multiple_of(c * CHUNK, CHUNK), CHUNK)
            for hh in range(hps):
                ln = slice(hh * HK, (hh + 1) * HK)
                q, fl, v, g = q_ref[sl, ln], f_ref[sl, ln], i_ref[sl, ln], g_ref[sl, ln]
                sig, f, k, sq, qt, eq, ek, eb, ekl, ebl = _hgrn_chunk_terms(q, fl, lb_ref[:, ln], tril_f)
                a = jnp.where(mask, _nt((qt * eq).astype(BF16), (k * ek).astype(BF16)), 0.0)
                st = state[hh]
                st_ref[hh, c] = st
                vb = v.astype(BF16)
                o = _nn(a.astype(BF16), vb) + _nt((qt * eb).astype(BF16), st.astype(BF16))
                state[hh] = st * ebl + _tn(vb, (k * ekl).astype(BF16))
                oraw_ref[sl, ln] = o
                rn = o * lax.rsqrt(jnp.mean(o * o, axis=-1, keepdims=True) + RMS_EPS)
                oa_ref[sl, ln] = (rn * gn * g * _sigmoid(g)).astype(BF16)
            return carry

        lax.fori_loop(0, ncb, chunk, 0, unroll=min(CHUNK_UNROLL, ncb))

    def col(block):
        return pl.BlockSpec((tb, wide), lambda h, j: (j, block * (N_HEADS_A // hps) + h))

    return pl.pallas_call(
        body, name="hgrn_fwd", grid=(N_HEADS_A // hps, t // tb),
        out_shape=[jax.ShapeDtypeStruct((t, D), BF16), jax.ShapeDtypeStruct((t, D), F32),
                   jax.ShapeDtypeStruct((N_HEADS_A, t // CHUNK, HK, HK), F32)],
        in_specs=[col(0), col(1), col(2), col(3), pl.BlockSpec((1, wide), lambda h, j: (0, h)),
                  pl.BlockSpec((1, HK), lambda h, j: (0, 0))],
        out_specs=[pl.BlockSpec((tb, wide), lambda h, j: (j, h)), pl.BlockSpec((tb, wide), lambda h, j: (j, h)),
                   pl.BlockSpec((hps, ncb, HK, HK), lambda h, j: (h, j, 0, 0))],
        scratch_shapes=[pltpu.VMEM((hps, HK, HK), F32)],
        compiler_params=_cparams(("parallel", "arbitrary")),
    )(proj, proj, proj, proj, lb, gnorm)


def hgrn_bwd(proj, lb, gnorm, o_raw, dya, w_ba, states, give, dproj):
    t = proj.shape[0]
    tb = _tile(t, TOKEN_BLOCK)
    ncb = tb // CHUNK
    nb = t // tb
    hps = HGRN_HEADS_PER_STEP
    wide = hps * HK

    def body(q_ref, f_ref, i_ref, g_ref, lb_ref, gn_ref, oraw_ref, dya_ref, wba_ref, st_ref, give_ref, dp_in_ref,
             dp_ref, dlb_ref, dgn_ref, got_ref, dstate, doa_ref, send_sem, recv_sem):
        h, j = pl.program_id(0), pl.program_id(1)
        swap_start, swap_wait = _sibling_exchange(give_ref, got_ref, send_sem, recv_sem)
        doa_ref[...] = _nt(dya_ref[...], wba_ref[...])

        @pl.when((h == 0) & (j == 0))
        def _():
            swap_start()

        @pl.when(j == 0)
        def _():
            dstate[...] = jnp.zeros_like(dstate)
            dlb_ref[...] = jnp.zeros_like(dlb_ref)

        @pl.when((j == 0) & (h == 0))
        def _():
            dgn_ref[...] = jnp.zeros_like(dgn_ref)

        gn = gn_ref[...]
        mask = _tri(CHUNK)
        mask_t = _tri(CHUNK, upper=True)
        tril_f = mask.astype(BF16)
        triu_f = mask_t.astype(BF16)

        def chunk(i, c0):
            c = ncb - 1 - i
            sl = pl.ds(pl.multiple_of(c * CHUNK, CHUNK), CHUNK)
            for hh in range(hps):
                ln = slice(hh * HK, (hh + 1) * HK)
                q, fl, v, g = q_ref[sl, ln], f_ref[sl, ln], i_ref[sl, ln], g_ref[sl, ln]
                lbv = lb_ref[:, ln]
                sig, f, k, sq, qt, eq, ek, eb, ekl, ebl = _hgrn_chunk_terms(q, fl, lbv, tril_f)
                qe = (qt * eq).astype(BF16)
                ke = (k * ek).astype(BF16)
                st32 = st_ref[hh, c]
                st = st32.astype(BF16)
                dst = dstate[hh]
                dstb = dst.astype(BF16)
                o = oraw_ref[sl, ln]
                rstd = lax.rsqrt(jnp.mean(o * o, axis=-1, keepdims=True) + RMS_EPS)
                rn = o * rstd
                sgm = _sigmoid(g)
                sg = g * sgm
                doa_v = doa_ref[sl, ln]
                drn = doa_v * gn * sg
                dgn_ref[...] += _colsum(doa_v * rn * sg)
                dp_ref[3, sl, ln] = (doa_v * rn * gn * _dsilu(g, sgm)).astype(BF16)
                do = rstd * (drn - rn * jnp.mean(drn * rn, axis=-1, keepdims=True))
                dob = do.astype(BF16)
                vb = v.astype(BF16)
                da = jnp.where(mask, _nt(dob, vb), 0.0).astype(BF16)
                da_t = jnp.where(mask_t, _nt(vb, dob), 0.0).astype(BF16)
                a_t = jnp.where(mask_t, _nt(ke, qe), 0.0).astype(BF16)
                kl = (k * ekl).astype(BF16)
                qb = (qt * eb).astype(BF16)
                dq_in = _nn(da, ke)
                dk_in = _nn(da_t, qe)
                dq_out = eb * _nn(dob, st)
                dk_out = ekl * _nn(vb, dstb)
                dqt = eq * dq_in + dq_out
                dk = ek * dk_in + dk_out
                dv = _nn(a_t, dob) + _nt(kl, dstb)
                dstate[hh] = dst * ebl + _tn(dob, qb)
                dbig = qe.astype(F32) * dq_in - ke.astype(F32) * dk_in + qt * dq_out - k * dk_out
                beyond = _colsum(k * dk_out) + ebl * _colsum(dst * st32)
                dlam = _sel(_nn, dbig, triu_f, 3, x_first=False) + beyond
                df = dlam / f - dk
                dp_ref[1, sl, ln] = (df * (1.0 - lbv) * sig * (1.0 - sig)).astype(BF16)
                dlb_ref[:, ln] += _colsum(df * (1.0 - sig))
                dp_ref[0, sl, ln] = (dqt * Q_SCALE * _dsilu(q, sq)).astype(BF16)
                dp_ref[2, sl, ln] = dv.astype(BF16)
            return c0

        lax.fori_loop(0, ncb, chunk, 0, unroll=min(CHUNK_UNROLL, ncb))

        @pl.when((h == N_HEADS_A // hps - 1) & (j == nb - 1))
        def _():
            swap_wait()

    def col(block):
        return pl.BlockSpec((tb, wide), lambda h, j: (nb - 1 - j, block * (N_HEADS_A // hps) + h))

    hcol = pl.BlockSpec((tb, wide), lambda h, j: (nb - 1 - j, h))
    hbm = pl.BlockSpec(memory_space=pl.ANY)
    return pl.pallas_call(
        body, name="hgrn_bwd", grid=(N_HEADS_A // hps, nb),
        out_shape=[jax.ShapeDtypeStruct(dproj.shape, dproj.dtype), jax.ShapeDtypeStruct((1, D), F32),
                   jax.ShapeDtypeStruct((1, HK), F32), jax.ShapeDtypeStruct(give.shape, give.dtype)],
        in_specs=[col(0), col(1), col(2), col(3), pl.BlockSpec((1, wide), lambda h, j: (0, h)),
                  pl.BlockSpec((1, HK), lambda h, j: (0, 0)), hcol,
                  pl.BlockSpec((tb, D), lambda h, j: (nb - 1 - j, 0)), pl.BlockSpec((wide, D), lambda h, j: (h, 0)),
                  pl.BlockSpec((hps, ncb, HK, HK), lambda h, j: (h, nb - 1 - j, 0, 0)), hbm, hbm],
        out_specs=[pl.BlockSpec((4, tb, wide), lambda h, j: (0, nb - 1 - j, h)),
                   pl.BlockSpec((1, wide), lambda h, j: (0, h)), pl.BlockSpec((1, HK), lambda h, j: (0, 0)), hbm],
        input_output_aliases={11: 0},
        scratch_shapes=[pltpu.VMEM((hps, HK, HK), F32), pltpu.VMEM((tb, wide), F32)] + SIBLING_SEMS,
        compiler_params=_cparams(("arbitrary", "arbitrary")),
    )(proj, proj, proj, proj, lb, gnorm, o_raw, dya, w_ba, states, give, dproj)


CONV_BLOCK0 = 6
CONV_TAPS = 4
HALO = 8


def conv_fwd(proj, conv_w, conv_b):
    t = proj.shape[0]
    tm = _tile(t, ROW_TILE)
    r = tm // HALO

    def body(x_ref, halo_ref, w_ref, b_ref, o_ref, ds_ref):
        i = pl.program_id(1)
        halo = jnp.where(i > 0, halo_ref[...], 0.0)
        ext = jnp.concatenate([halo, x_ref[...]], axis=0)
        pre = b_ref[...] + w_ref[CONV_TAPS - 1:CONV_TAPS, :] * ext[HALO:, :]
        for tap in range(CONV_TAPS - 1):
            pre = pre + w_ref[tap:tap + 1, :] * pltpu.roll(ext, CONV_TAPS - 1 - tap, axis=0)[HALO:, :]
        s = _sigmoid(pre)
        o_ref[...] = pre * s
        ds_ref[...] = _dsilu(pre, s).astype(BF16)

    blk = pl.BlockSpec((tm, D), lambda cb, i: (i, cb))
    return pl.pallas_call(
        body, name="conv_fwd", grid=(CONV_DIM // D, t // tm),
        out_shape=[jax.ShapeDtypeStruct((t, CONV_DIM), F32), jax.ShapeDtypeStruct((t, CONV_DIM), BF16)],
        in_specs=[pl.BlockSpec((tm, D), lambda cb, i: (i, CONV_BLOCK0 + cb)),
                  pl.BlockSpec((HALO, D), lambda cb, i: (jnp.maximum(i * r - 1, 0), CONV_BLOCK0 + cb)),
                  pl.BlockSpec((CONV_TAPS, D), lambda cb, i: (0, cb)), pl.BlockSpec((1, D), lambda cb, i: (0, cb))],
        out_specs=[blk, blk],
        compiler_params=_cparams(("parallel", "parallel")),
    )(proj, proj, conv_w, conv_b)


def conv_bwd(proj, dxc, dsilu, conv_w, dproj):
    t = proj.shape[0]
    tm = _tile(t, ROW_TILE)
    r = tm // HALO
    n = t // tm
    last_halo = t // HALO - 1

    def body(x_ref, prev_ref, d_ref, dnext_ref, s_ref, snext_ref, w_ref, dp_in_ref, dx_ref, dw_ref, db_ref):
        i = pl.program_id(1)

        @pl.when(i == 0)
        def _():
            dw_ref[...] = jnp.zeros_like(dw_ref)
            db_ref[...] = jnp.zeros_like(db_ref)

        dpre = jnp.concatenate([d_ref[...].astype(F32) * s_ref[...].astype(F32),
                                jnp.where(i < n - 1, dnext_ref[0:HALO, :].astype(F32) * snext_ref[0:HALO, :].astype(F32),
                                          0.0)], axis=0)
        dx = w_ref[CONV_TAPS - 1:CONV_TAPS, :] * dpre[:tm, :]
        for tap in range(CONV_TAPS - 1):
            back = CONV_TAPS - 1 - tap
            dx = dx + w_ref[tap:tap + 1, :] * pltpu.roll(dpre, tm + HALO - back, axis=0)[:tm, :]
        dx_ref[...] = dx.astype(BF16)
        dp = dpre[:tm, :]
        db_ref[...] += _colsum(dp)
        prev = jnp.where(i > 0, prev_ref[...], 0.0)
        ext = jnp.concatenate([prev, x_ref[...]], axis=0)
        dw_ref[CONV_TAPS - 1:CONV_TAPS, :] += _colsum(dp * ext[HALO:, :])
        for tap in range(CONV_TAPS - 1):
            dw_ref[tap:tap + 1, :] += _colsum(dp * pltpu.roll(ext, CONV_TAPS - 1 - tap, axis=0)[HALO:, :])

    blk = pl.BlockSpec((tm, D), lambda cb, i: (i, cb))
    nxt = pl.BlockSpec((2 * HALO, D), lambda cb, i: (jnp.minimum((i + 1) * (r // 2), last_halo // 2), cb))
    return pl.pallas_call(
        body, name="conv_bwd", grid=(CONV_DIM // D, n),
        out_shape=[jax.ShapeDtypeStruct(dproj.shape, dproj.dtype), jax.ShapeDtypeStruct((8, CONV_DIM), F32),
                   jax.ShapeDtypeStruct((1, CONV_DIM), F32)],
        in_specs=[pl.BlockSpec((tm, D), lambda cb, i: (i, CONV_BLOCK0 + cb)),
                  pl.BlockSpec((HALO, D), lambda cb, i: (jnp.maximum(i * r - 1, 0), CONV_BLOCK0 + cb)),
                  blk, nxt, blk, nxt,
                  pl.BlockSpec((CONV_TAPS, D), lambda cb, i: (0, cb)), pl.BlockSpec(memory_space=pl.ANY)],
        out_specs=[pl.BlockSpec((None, tm, D), lambda cb, i: (CONV_BLOCK0 + cb, i, 0)),
                   pl.BlockSpec((8, D), lambda cb, i: (0, cb)), pl.BlockSpec((1, D), lambda cb, i: (0, cb))],
        input_output_aliases={7: 0},
        compiler_params=_cparams(("parallel", "arbitrary")),
    )(proj, proj, dxc, dxc, dsilu, dsilu, conv_w, dproj)


def dt_fill(ddt, dproj):
    t = ddt.shape[0]
    tm = _tile(t, WIDE_ROW_TILE)
    w = ddt.shape[1]

    def body(d_ref, dp_in_ref, o_ref):
        o_ref[:, :w] = d_ref[...]
        o_ref[:, w:] = jnp.zeros((tm, D - w), o_ref.dtype)

    return pl.pallas_call(
        body, name="dt_fill", grid=(t // tm,),
        out_shape=jax.ShapeDtypeStruct(dproj.shape, dproj.dtype),
        in_specs=[pl.BlockSpec((tm, w), lambda i: (i, 0)), pl.BlockSpec(memory_space=pl.ANY)],
        out_specs=pl.BlockSpec((None, tm, D), lambda i: (DT_COL_BLOCK, i, 0)),
        input_output_aliases={1: 0},
        compiler_params=_cparams(("parallel",)),
    )(ddt, dproj)


Z_BLOCK0 = 8
DT_COL_BLOCK = 9
DT_BLOCK0 = 8 * DT_COL_BLOCK
GATE_BLOCK0 = 10
B_BLOCK0 = 16
C_BLOCK0 = 20


def _head_expand():
    e = np.zeros((N_STATE, GROUP_W), np.float32)
    for hh in range(HEADS_PER_GROUP):
        e[hh, hh * HEAD_P:(hh + 1) * HEAD_P] = 1.0
    return jnp.asarray(e, BF16)


def _ssd_chunk_terms(dt, bias, alog, expand, tril_f, eye):
    dtb = dt + bias
    delta = jnp.maximum(dtb, 0.0) + jnp.log(1.0 + jnp.exp(-jnp.abs(dtb)))
    ea = jnp.exp(alog)
    a = -ea * delta
    acum = _sel(_nn, a, tril_f, 3, x_first=False)
    delta_e = _sel(_nn, delta, expand, 2)
    acum_e = _sel(_nn, acum, expand, 2)
    acum_t = _sel(_nt, acum, eye, 3, x_first=False)
    return dtb, delta, ea, a, acum, delta_e, acum_e, acum_t


def ssd_fwd(proj, xc, alog4, bias4, dskip4, wnorm, expand):
    t = proj.shape[0]
    tb = _tile(t, TOKEN_BLOCK)
    ncb = tb // SSD_CHUNK

    def body(xs_ref, b_ref, c_ref, dt_ref, z_ref, alog_ref, bias_ref, dsk_ref, wn_ref, e_ref, ob_ref, st_ref, state):
        @pl.when(pl.program_id(1) == 0)
        def _():
            state[...] = jnp.zeros_like(state)

        expand = e_ref[...]
        mask = _tri(SSD_CHUNK)
        tril_f = mask.astype(BF16)
        eye = (lax.broadcasted_iota(jnp.int32, (N_STATE, N_STATE), 0) ==
               lax.broadcasted_iota(jnp.int32, (N_STATE, N_STATE), 1)).astype(BF16)
        alog, bias = alog_ref[0], bias_ref[0]
        d_e = _sel(_nn, jnp.broadcast_to(dsk_ref[0], (8, N_STATE)), expand, 3)[0:1, :]
        wn = wn_ref[...]

        def chunk(c, carry):
            sl = pl.ds(pl.multiple_of(c * SSD_CHUNK, SSD_CHUNK), SSD_CHUNK)
            xs, bm, cm, dt, z = xs_ref[sl, :], b_ref[sl, :], c_ref[sl, :], dt_ref[sl, :], z_ref[sl, :]
            dtb, delta, ea, a, acum, delta_e, acum_e, acum_t = _ssd_chunk_terms(dt, bias, alog, expand, tril_f, eye)
            alast_e = acum_e[SSD_CHUNK - 1:SSD_CHUNK, :]
            xd = xs * delta_e
            xdb = xd.astype(BF16)
            cb_, bb_ = cm.astype(BF16), bm.astype(BF16)
            cbm = _nt(cb_, bb_)
            ys = []
            for hh in range(HEADS_PER_GROUP):
                lh = jnp.where(mask, jnp.exp(jnp.minimum(acum[:, hh:hh + 1] - acum_t[hh:hh + 1, :], 0.0)), 0.0)
                ys.append(_nn((cbm * lh).astype(BF16), xdb[:, hh * HEAD_P:(hh + 1) * HEAD_P]))
            st = state[...]
            st_ref[0, c] = st
            y = jnp.concatenate(ys, axis=1) + _nn(cb_, st.astype(BF16)) * jnp.exp(acum_e) + xs * d_e
            state[...] = st * jnp.exp(alast_e) + _tn(bb_, (xd * jnp.exp(alast_e - acum_e)).astype(BF16))
            yg = y * z * _sigmoid(z)
            ob_ref[sl, :] = (yg * lax.rsqrt(jnp.mean(yg * yg, axis=-1, keepdims=True) + RMS_EPS) * wn).astype(BF16)
            return carry

        lax.fori_loop(0, ncb, chunk, 0, unroll=min(CHUNK_UNROLL, ncb))

    small = pl.BlockSpec((1, 1, N_STATE), lambda g, j: (g, 0, 0))
    return pl.pallas_call(
        body, name="ssd_fwd", grid=(N_GROUPS, t // tb),
        out_shape=[jax.ShapeDtypeStruct((t, B_INNER), BF16),
                   jax.ShapeDtypeStruct((N_GROUPS, t // SSD_CHUNK, N_STATE, GROUP_W), F32)],
        in_specs=[pl.BlockSpec((tb, GROUP_W), lambda g, j: (j, g)),
                  pl.BlockSpec((tb, N_STATE), lambda g, j: (j, B_BLOCK0 + g)),
                  pl.BlockSpec((tb, N_STATE), lambda g, j: (j, C_BLOCK0 + g)),
                  pl.BlockSpec((tb, N_STATE), lambda g, j: (j, DT_BLOCK0 + g)),
                  pl.BlockSpec((tb, GROUP_W), lambda g, j: (j, Z_BLOCK0 + g)),
                  small, small, small, pl.BlockSpec((1, GROUP_W), lambda g, j: (0, g)),
                  pl.BlockSpec((N_STATE, GROUP_W), lambda g, j: (0, 0))],
        out_specs=[pl.BlockSpec((tb, GROUP_W), lambda g, j: (j, g)),
                   pl.BlockSpec((1, ncb, N_STATE, GROUP_W), lambda g, j: (g, j, 0, 0))],
        scratch_shapes=[pltpu.VMEM((N_STATE, GROUP_W), F32)],
        compiler_params=_cparams(("parallel", "arbitrary")),
    )(xc, xc, xc, proj, proj, alog4, bias4, dskip4, wnorm, expand)


def ssd_bwd(proj, xc, alog4, bias4, dskip4, wnorm, expand, dyb, w_bb, states, part, dproj):
    t = proj.shape[0]
    tb = _tile(t, TOKEN_BLOCK)
    lc = min(SSD_CHUNK_BWD, tb)
    ncb = tb // lc
    nsaved = tb // SSD_CHUNK
    nb = t // tb

    def body(xs_ref, b_ref, c_ref, dt_ref, z_ref, alog_ref, bias_ref, dsk_ref, wn_ref, e_ref, dyb_ref, wbb_ref, st_ref,
             part_ref, dp_in_ref, dxs_ref, db_ref, dc_ref, dz_ref, ddt_ref, dwn_ref, dalog_ref, dbias_ref, ddsk_ref,
             parts_ref, dstate, dob_ref, send_sems, recv_sems, local_sem):
        xchg_start, xchg_wait = _chip_exchange(part_ref, parts_ref, send_sems, recv_sems, local_sem)
        dob_ref[...] = _nt(dyb_ref[...], wbb_ref[...])

        @pl.when((pl.program_id(0) == 0) & (pl.program_id(1) == 0))
        def _():
            xchg_start()

        @pl.when(pl.program_id(1) == 0)
        def _():
            dstate[...] = jnp.zeros_like(dstate)
            dwn_ref[...] = jnp.zeros_like(dwn_ref)
            dalog_ref[...] = jnp.zeros_like(dalog_ref)
            dbias_ref[...] = jnp.zeros_like(dbias_ref)
            ddsk_ref[...] = jnp.zeros_like(ddsk_ref)

        expand = e_ref[...]
        mask = _tri(lc)
        mask_t = _tri(lc, upper=True)
        tril_f = mask.astype(BF16)
        triu_f = mask_t.astype(BF16)
        eye = (lax.broadcasted_iota(jnp.int32, (N_STATE, N_STATE), 0) ==
               lax.broadcasted_iota(jnp.int32, (N_STATE, N_STATE), 1)).astype(BF16)
        alog, bias = alog_ref[0], bias_ref[0]
        d_e = _sel(_nn, jnp.broadcast_to(dsk_ref[0], (8, N_STATE)), expand, 3)[0:1, :]
        wn = wn_ref[...]

        def chunk(i, c0):
            c = ncb - 1 - i
            sl = pl.ds(pl.multiple_of(c * lc, lc), lc)
            xs, bm, cm, dt, z = xs_ref[sl, :], b_ref[sl, :], c_ref[sl, :], dt_ref[sl, :], z_ref[sl, :]
            dtb, delta, ea, a, acum, delta_e, acum_e, acum_t = _ssd_chunk_terms(dt, bias, alog, expand, tril_f, eye)
            alast_e = acum_e[lc - 1:lc, :]
            eacum = jnp.exp(acum_e)
            wl = jnp.exp(alast_e - acum_e)
            xd = xs * delta_e
            xdb = xd.astype(BF16)
            cb_, bb_ = cm.astype(BF16), bm.astype(BF16)
            cbm = _nt(cb_, bb_)
            st32 = st_ref[0, c * (lc // SSD_CHUNK)]
            stb = st32.astype(BF16)
            dst = dstate[...]
            dstb = dst.astype(BF16)
            lhs, mixes, ys = [], [], []
            for hh in range(HEADS_PER_GROUP):
                col, row = acum[:, hh:hh + 1], acum_t[hh:hh + 1, :]
                lh = jnp.where(mask, jnp.exp(jnp.minimum(col - row, 0.0)), 0.0)
                mix = (cbm * lh).astype(BF16)
                lhs.append(lh)
                mixes.append(mix)
                ys.append(_nn(mix, xdb[:, hh * HEAD_P:(hh + 1) * HEAD_P]))
            y_in = jnp.concatenate(ys, axis=1)
            y_out = _nn(cb_, stb) * eacum
            y = y_in + y_out + xs * d_e
            sgz = _sigmoid(z)
            sz = z * sgz
            yg = y * sz
            rstd = lax.rsqrt(jnp.mean(yg * yg, axis=-1, keepdims=True) + RMS_EPS)
            nrm = yg * rstd
            dob_v = dob_ref[sl, :]
            dn = dob_v * wn
            dwn_ref[...] += _colsum(dob_v * nrm)
            dyg = rstd * (dn - nrm * jnp.mean(dn * nrm, axis=-1, keepdims=True))
            dy = dyg * sz
            dz_ref[sl, :] = (dyg * y * _dsilu(z, sgz)).astype(BF16)
            dyb = dy.astype(BF16)
            dxds = []
            dcb = jnp.zeros((lc, lc), F32)
            for hh in range(HEADS_PER_GROUP):
                hs = slice(hh * HEAD_P, (hh + 1) * HEAD_P)
                dy_h, x_h = dyb[:, hs], xdb[:, hs]
                dxds.append(_tn(mixes[hh], dy_h))
                dcb = dcb + _nt(dy_h, x_h) * lhs[hh]
            dcbb = dcb.astype(BF16)
            dye = (dy * eacum).astype(BF16)
            xw = (xd * wl).astype(BF16)
            dxd_in = jnp.concatenate(dxds, axis=1)
            dxd_out = wl * _nn(bb_, dstb)
            dxd = dxd_in + dxd_out
            dc_ref[sl, :] = (_nn(dcbb, bb_) + _nt(dye, stb)).astype(dc_ref.dtype)
            db_ref[sl, :] = (_tn(dcbb, cb_) + _nt(xw, dstb)).astype(db_ref.dtype)
            dstate[...] = dst * jnp.exp(alast_e) + _tn(cb_, dye)
            col_out = xd * dxd_out
            dac = _sel(_nt, dyb.astype(F32) * y_in - xdb.astype(F32) * dxd_in + dy * y_out - col_out, expand, 2)
            beyond = _colsum(col_out) + jnp.exp(alast_e) * _colsum(dst * st32)
            da = (_sel(_nn, dac, triu_f, 3, x_first=False) +
                  _sel(_nt, jnp.broadcast_to(beyond, (8, GROUP_W)), expand, 3)[0:1, :])
            ddelta = _sel(_nt, dxd * xs, expand, 2) - da * ea
            dalog_ref[0] += _colsum(da * a)
            ddtb = ddelta * _sigmoid(dtb)
            dbias_ref[0] += _colsum(ddtb)
            ddt_ref[sl, :] = ddtb.astype(BF16)
            ddsk_ref[0] += _sel(_nt, jnp.broadcast_to(_colsum(dy * xs), (8, GROUP_W)), expand, 3)[0:1, :]
            dxs_ref[sl, :] = (dxd * delta_e + dy * d_e).astype(dxs_ref.dtype)
            return c0

        lax.fori_loop(0, ncb, chunk, 0, unroll=min(CHUNK_UNROLL, ncb))

        @pl.when((pl.program_id(0) == N_GROUPS - 1) & (pl.program_id(1) == nb - 1))
        def _():
            xchg_wait()

    small = pl.BlockSpec((1, 1, N_STATE), lambda g, j: (g, 0, 0))
    wide = pl.BlockSpec((tb, GROUP_W), lambda g, j: (nb - 1 - j, g))
    narrow = pl.BlockSpec((tb, N_STATE), lambda g, j: (nb - 1 - j, g))
    hbm = pl.BlockSpec(memory_space=pl.ANY)
    return pl.pallas_call(
        body, name="ssd_bwd", grid=(N_GROUPS, nb),
        out_shape=[jax.ShapeDtypeStruct((t, B_INNER), BF16), jax.ShapeDtypeStruct((t, GROUP_W), BF16),
                   jax.ShapeDtypeStruct((t, GROUP_W), BF16), jax.ShapeDtypeStruct(dproj.shape, dproj.dtype),
                   jax.ShapeDtypeStruct((t, GROUP_W), BF16), jax.ShapeDtypeStruct((1, B_INNER), F32),
                   jax.ShapeDtypeStruct((N_GROUPS, 1, N_STATE), F32), jax.ShapeDtypeStruct((N_GROUPS, 1, N_STATE), F32),
                   jax.ShapeDtypeStruct((N_GROUPS, 1, N_STATE), F32), jax.ShapeDtypeStruct(part.shape, part.dtype)],
        in_specs=[wide,
                  pl.BlockSpec((tb, N_STATE), lambda g, j: (nb - 1 - j, B_BLOCK0 + g)),
                  pl.BlockSpec((tb, N_STATE), lambda g, j: (nb - 1 - j, C_BLOCK0 + g)),
                  pl.BlockSpec((tb, N_STATE), lambda g, j: (nb - 1 - j, DT_BLOCK0 + g)),
                  pl.BlockSpec((tb, GROUP_W), lambda g, j: (nb - 1 - j, Z_BLOCK0 + g)),
                  small, small, small, pl.BlockSpec((1, GROUP_W), lambda g, j: (0, g)),
                  pl.BlockSpec((N_STATE, GROUP_W), lambda g, j: (0, 0)),
                  pl.BlockSpec((tb, D), lambda g, j: (nb - 1 - j, 0)), pl.BlockSpec((GROUP_W, D), lambda g, j: (g, 0)),
                  pl.BlockSpec((1, nsaved, N_STATE, GROUP_W), lambda g, j: (g, nb - 1 - j, 0, 0)), hbm, hbm],
        out_specs=[wide, narrow, narrow,
                   pl.BlockSpec((None, tb, GROUP_W), lambda g, j: (Z_BLOCK0 // 2 + g // 2, nb - 1 - j, g % 2)),
                   narrow, pl.BlockSpec((1, GROUP_W), lambda g, j: (0, g)), small, small, small, hbm],
        input_output_aliases={14: 3},
        scratch_shapes=[pltpu.VMEM((N_STATE, GROUP_W), F32), pltpu.VMEM((tb, GROUP_W), F32)] + CHIP_SEMS,
        compiler_params=_cparams(("arbitrary", "arbitrary")),
    )(xc, xc, xc, proj, proj, alog4, bias4, dskip4, wnorm, expand, dyb, w_bb, states, part, dproj)


def lower_bound_fwd(hgrn_lb):
    def body(a_ref, o_ref):
        a0, a1 = a_ref[0:1, :], a_ref[1:2, :]
        m = jnp.maximum(a0, a1)
        e0, e1 = jnp.exp(a0 - m), jnp.exp(a1 - m)
        o_ref[...] = e0 / (e0 + e1)

    return pl.pallas_call(body, name="lower_bound_fwd", out_shape=jax.ShapeDtypeStruct((1, D), F32))(hgrn_lb)


def ada_weight_grad(c_all, dmod_cols):
    def body(c_ref, d_ref, o_ref):
        cval = c_ref[...]
        o_ref[...] = _tn(cval * _sigmoid(cval), d_ref[...], HI)

    return pl.pallas_call(body, name="ada_weight_grad",
                          out_shape=jax.ShapeDtypeStruct((D, dmod_cols.shape[1]), F32))(c_all, dmod_cols)


def reduce_small(gathered, hgrn_lb, dlb_off):
    n = gathered.shape[2]

    def body(g_ref, a_ref, o_ref, glb_ref):
        s = g_ref[0]
        for d in range(1, N_DEV):
            s = s + g_ref[d]
        o_ref[...] = s
        a0, a1 = a_ref[0:1, :], a_ref[1:2, :]
        m = jnp.maximum(a0, a1)
        e0, e1 = jnp.exp(a0 - m), jnp.exp(a1 - m)
        p0 = e0 / (e0 + e1)
        tq = s[:, dlb_off:dlb_off + D] * p0 * (1.0 - p0)
        glb_ref[0:1, :] = tq
        glb_ref[1:2, :] = -tq

    return pl.pallas_call(body, name="reduce_small",
                          out_shape=[jax.ShapeDtypeStruct((1, n), F32), jax.ShapeDtypeStruct((2, D), F32)])(gathered, hgrn_lb)


def _adam_math(w, g, m, v):
    m2 = ADAM_B1 * m + (1.0 - ADAM_B1) * g
    v2 = ADAM_B2 * v + (1.0 - ADAM_B2) * (g * g)
    m_hat = m2 / (1.0 - ADAM_B1 ** ADAM_STEP)
    v_hat = v2 / (1.0 - ADAM_B2 ** ADAM_STEP)
    delta = -ADAM_LR * (m_hat / (jnp.sqrt(v_hat) + ADAM_EPS) + ADAM_WD * w)
    return delta, m2, v2


def _row_tile(rows, mult=8, cap=128):
    for cand in range(cap - cap % mult, 0, -mult):
        if rows % cand == 0:
            return cand
    return rows


def sum_parts(parts, name):
    n, rows, cols = parts.shape
    tr = _row_tile(rows, 16, 1024)

    def body(p_ref, o_ref):
        s = p_ref[0].astype(F32)
        for d in range(1, n):
            s = s + p_ref[d].astype(F32)
        o_ref[...] = s

    return pl.pallas_call(
        body, name=name, grid=(rows // tr,),
        out_shape=jax.ShapeDtypeStruct((rows, cols), F32),
        in_specs=[pl.BlockSpec((n, tr, cols), lambda i: (0, i, 0))],
        out_specs=pl.BlockSpec((tr, cols), lambda i: (i, 0)),
        compiler_params=_cparams(("parallel",)),
    )(parts)


def sum_pair(a, b, name):
    rows, cols = a.shape
    tr = _row_tile(rows, 16, 1024)

    def body(a_ref, b_ref, o_ref):
        o_ref[...] = (a_ref[...].astype(F32) + b_ref[...].astype(F32)).astype(o_ref.dtype)

    blk = pl.BlockSpec((tr, cols), lambda i: (i, 0))
    return pl.pallas_call(
        body, name=name, grid=(rows // tr,),
        out_shape=jax.ShapeDtypeStruct((rows, cols), a.dtype),
        in_specs=[blk, blk], out_specs=blk,
        compiler_params=_cparams(("parallel",)),
    )(a, b)


def adamw(w, g, m, v, name):
    rows, cols = w.shape
    tr = _row_tile(rows, 8, 256)

    def body(w_ref, g_ref, m_ref, v_ref, d_ref, m2_ref, v2_ref):
        delta, m2, v2 = _adam_math(w_ref[...], g_ref[...], m_ref[...], v_ref[...])
        d_ref[...] = delta
        m2_ref[...] = m2
        v2_ref[...] = v2

    blk = pl.BlockSpec((tr, cols), lambda i: (i, 0))
    return pl.pallas_call(
        body, name=name, grid=(rows // tr,),
        out_shape=[jax.ShapeDtypeStruct((rows, cols), F32)] * 3,
        in_specs=[blk] * 4, out_specs=[blk] * 3,
        compiler_params=_cparams(("parallel",)),
    )(w, g, m, v)


def _pad128(n):
    return -(-n // 128) * 128


def _pack(arrays):
    offs, parts, off = [], [], 0
    for a in arrays:
        flat = a.reshape(1, -1)
        n = flat.shape[1]
        offs.append(off)
        parts.append(jnp.pad(flat, ((0, 0), (0, _pad128(n) - n))))
        off += _pad128(n)
    return jnp.concatenate(parts, axis=1), offs


def _unpack(vec, offs, shapes):
    out = []
    for off, shp in zip(offs, shapes):
        n = int(np.prod(shp))
        out.append(vec[0, off:off + n].reshape(shp))
    return out


IN_ROWS = IN_DIM // N_DEV
DT_ROW0 = 9216
DT_DEV, DT_LO = divmod(DT_ROW0, IN_ROWS)


GATE_SHIFT = D - 32


def _in_row_pieces(tile):
    pieces = []
    if tile == DT_COL_BLOCK:
        for g in range(N_GROUPS):
            o = DT_ROW0 + HEADS_PER_GROUP * g
            pieces.append((N_STATE * g, o // IN_ROWS, o % IN_ROWS, HEADS_PER_GROUP))
        return pieces
    r, end = tile * D, (tile + 1) * D
    while r < end:
        o = r if r < DT_ROW0 else r - GATE_SHIFT
        dev, loc = divmod(o, IN_ROWS)
        n = min(end - r, IN_ROWS - loc)
        pieces.append((r - tile * D, dev, loc, n))
        r += n
    return pieces


def assemble_w_in(g_all):
    ntile = N_PROJ // D

    def body(g_ref, o_ref):
        j = pl.program_id(0)
        for tile in range(ntile):
            @pl.when(j == tile)
            def _(tile=tile):
                if tile == DT_COL_BLOCK:
                    o_ref[...] = jnp.zeros_like(o_ref)
                for dst, dev, loc, n in _in_row_pieces(tile):
                    o_ref[pl.ds(dst, n), :] = g_ref[dev, pl.ds(loc, n), :]

    return pl.pallas_call(
        body, name="assemble_w_in", grid=(ntile,),
        out_shape=jax.ShapeDtypeStruct((N_PROJ, D), g_all.dtype),
        in_specs=[pl.BlockSpec(memory_space=pltpu.VMEM)],
        out_specs=pl.BlockSpec((D, D), lambda j: (j, 0)),
        compiler_params=_cparams(("arbitrary",)),
    )(g_all)


def _grad_in_blocks(g_t, core, slot):
    dt0 = DT_COL_BLOCK * D
    dt = g_t[dt0:dt0 + N_GROUPS * N_STATE].reshape(N_GROUPS, N_STATE, D)[:, :HEADS_PER_GROUP].reshape(32, D)
    with_dt = jnp.concatenate([g_t[DT_DEV * IN_ROWS:DT_ROW0], dt,
                               g_t[DT_ROW0 + 32 + GATE_SHIFT:(DT_DEV + 1) * IN_ROWS + GATE_SHIFT]], axis=0)
    blocks = []
    for q in range(N_CHIP):
        if 2 * q + 1 < DT_DEV:
            blk = lax.dynamic_slice_in_dim(g_t, IN_ROWS * (2 * q + core), IN_ROWS, axis=0)
        else:
            assert 2 * q == DT_DEV
            after = g_t[(DT_DEV + 1) * IN_ROWS + GATE_SHIFT:(DT_DEV + 2) * IN_ROWS + GATE_SHIFT]
            blk = jnp.where(core == 0, with_dt, after)
        blocks.append(jnp.pad(blk, ((0, slot - IN_ROWS), (0, 0))))
    return jnp.stack(blocks)


def kernel(x, c, w_ada, b_ada, w_in, hgrn_lb, hgrn_gnorm, ssm_conv_w, ssm_conv_b, ssm_dt_bias, ssm_a_log, ssm_d, ssm_norm, w_branch_a, w_branch_b, w_o, ln1_g, ln1_b, w_ffn_gate, w_ffn_up, w_ffn_down, ln2_g, ln2_b, loss_target, m_w_ada, m_b_ada, m_w_in, m_hgrn_lb, m_hgrn_gnorm, m_ssm_conv_w, m_ssm_conv_b, m_ssm_dt_bias, m_ssm_a_log, m_ssm_d, m_ssm_norm, m_w_branch_a, m_w_branch_b, m_w_o, m_ln1_g, m_ln1_b, m_w_ffn_gate, m_w_ffn_up, m_w_ffn_down, m_ln2_g, m_ln2_b, v_w_ada, v_b_ada, v_w_in, v_hgrn_lb, v_hgrn_gnorm, v_ssm_conv_w, v_ssm_conv_b, v_ssm_dt_bias, v_ssm_a_log, v_ssm_d, v_ssm_norm, v_w_branch_a, v_w_branch_b, v_w_o, v_ln1_g, v_ln1_b, v_w_ffn_gate, v_w_ffn_up, v_w_ffn_down, v_ln2_g, v_ln2_b):
    me = 4 * lax.axis_index("x") + 2 * lax.axis_index("y") + lax.axis_index("c")
    xt = x[0]
    tgt = loss_target[0]
    t = xt.shape[0]
    ada_cols = w_ada.shape[2]
    conv_cols = ssm_conv_w.shape[2]

    small_in, _ = _pack([c, ssm_conv_w[0]])
    small_all = allgather_vmem(small_in, "allgather_small_inputs")
    c_all = small_all[:, 0, :D]
    conv_w = small_all[:, 0, D:D + CONV_TAPS * conv_cols].reshape(N_DEV, CONV_TAPS, conv_cols)
    conv_w = conv_w.transpose(1, 0, 2).reshape(CONV_TAPS, CONV_DIM)
    mod = ada_modulation(c_all, w_ada[0], b_ada.reshape(N_DEV, 1, ada_cols))
    mod6 = mod.reshape(6, D)

    shards = [w_in[0].T, w_branch_a[0], w_branch_b[0], w_o[0], w_ffn_gate[0].T, w_ffn_up[0].T, w_ffn_down[0]]
    shard_rows = [s.shape[0] for s in shards]
    slot_rows = [-(-r // 32) * 32 for r in shard_rows]
    row_offs = [sum(slot_rows[:i]) for i in range(len(shards))]
    padded = [jnp.pad(s.astype(BF16), ((0, p - r), (0, 0))) for s, r, p in zip(shards, shard_rows, slot_rows)]
    w_in_t = assemble_w_in(allgather_hbm(padded[0], "allgather_w_in"))

    lb = lower_bound_fwd(hgrn_lb)
    u1 = ln_modulate(xt, mod6, 0, 1, "ln_modulate_1")
    proj, g_rest = mm_nt_gather(u1, w_in_t, F32, jnp.concatenate(padded[1:], axis=0), "mm_in_proj")
    g_ba, g_bb, g_o, g_fg, g_fu, g_fd = (g_rest[:, o - slot_rows[0]:o - slot_rows[0] + r]
                                         for o, r in zip(row_offs[1:], shard_rows[1:]))
    w_ba = g_ba.reshape(D, D)
    w_bb = g_bb.reshape(B_INNER, D)
    w_oo = g_o.reshape(D, D)
    w_gu_t = jnp.concatenate([g_fg.reshape(D_FF, D), g_fu.reshape(D_FF, D)], axis=0)
    w_dn = g_fd.reshape(D_FF, D)
    o_a, o_raw, st_a = hgrn_fwd(proj, lb, hgrn_gnorm)
    xc, conv_slope = conv_fwd(proj, conv_w, ssm_conv_b)
    pad3 = ((0, 0), (0, 0), (0, N_STATE - HEADS_PER_GROUP))
    alog4 = jnp.pad(ssm_a_log.reshape(N_GROUPS, 1, HEADS_PER_GROUP), pad3)
    bias4 = jnp.pad(ssm_dt_bias.reshape(N_GROUPS, 1, HEADS_PER_GROUP), pad3)
    dskip4 = jnp.pad(ssm_d.reshape(N_GROUPS, 1, HEADS_PER_GROUP), pad3)
    expand = _head_expand()
    o_b, st_b = ssd_fwd(proj, xc, alog4, bias4, dskip4, ssm_norm, expand)
    ya, yb, merged, h1, x1, u2 = mixer_tail(o_a, o_b, w_ba, w_bb, proj, w_oo, xt, mod6, ln1_g, ln1_b)
    gu, act = ffn_in_act(u2, w_gu_t)

    dh2, dx1_part, acc4 = ffn_tail_loss_bwd(x1, act, w_dn, mod6, ln2_g, ln2_b, tgt)
    g_dn = mm_tn(act, dh2, "mm_grad_ffn_down")
    dgu = ffn_act_bwd(dh2, w_dn, gu)
    g_gu_t = mm_tn(dgu, u2, "mm_grad_ffn_in")
    dh1, dx_part, acc2 = mixer_tail_bwd(x1, dgu, w_gu_t, mod6, dx1_part, xt, h1, ln1_g, ln1_b)
    g_o = mm_tn(merged, dh1, "mm_grad_out_proj")
    dya, dyb, dproj = merge_gates_bwd(dh1, w_oo, ya, yb, proj)
    g_ba_full = mm_tn(o_a, dya, "mm_grad_branch_a")
    g_bb_full = mm_tn(o_b, dyb, "mm_grad_branch_b")
    my_core = lax.axis_index("c")

    def by_core(blocks, rows, slots):
        contrib = jnp.concatenate([jnp.pad(b.reshape(N_DEV, -1, D), ((0, 0), (0, p - r), (0, 0)))
                                   for b, r, p in zip(blocks, rows, slots)], axis=1)
        split = contrib.reshape(N_CHIP, 2, contrib.shape[1], D).transpose(1, 0, 2, 3)
        return (lax.dynamic_index_in_dim(split, my_core, 0, keepdims=False),
                lax.dynamic_index_in_dim(split, 1 - my_core, 0, keepdims=False))

    keep_e, give_e = by_core([g_ba_full, g_bb_full, g_o, g_gu_t[:D_FF], g_gu_t[D_FF:], g_dn],
                             shard_rows[1:], slot_rows[1:])
    dproj, dlb, dgn, got_e = hgrn_bwd(proj, lb, hgrn_gnorm, o_raw, dya, w_ba, st_a, give_e, dproj)
    chip_e = sum_pair(keep_e.reshape(-1, D), got_e.reshape(-1, D), "sum_grads_rest_chip").reshape(keep_e.shape)
    dxs, dbm, dcm, dproj, ddt, dwn, dalog, dbias, ddsk, parts_e = ssd_bwd(proj, xc, alog4, bias4, dskip4, ssm_norm,
                                                                          expand, dyb, w_bb, st_b, chip_e, dproj)
    dxc = jnp.concatenate([dxs, dbm, dcm], axis=1)
    dproj, dcw, dcb = conv_bwd(proj, dxc, conv_slope, conv_w, dproj)
    dproj = dt_fill(ddt, dproj)
    g_in_t = mm_tn(dproj, u1, "mm_grad_in_proj")
    keep_l = _grad_in_blocks(g_in_t, my_core, slot_rows[0])
    give_l = _grad_in_blocks(g_in_t, 1 - my_core, slot_rows[0])
    got_l = exchange_sibling(give_l, "exchange_grad_in_sibling")
    chip_l = sum_pair(keep_l.reshape(-1, D), got_l.reshape(-1, D), "sum_grad_in_chip").reshape(keep_l.shape)
    du1, parts_l = mm_nn_exchange(dproj, w_in_t, F32, chip_l, "mm_du1")
    dx, acc1 = ln_modulate_bwd(xt, du1, mod6, 1, dx_part, "ln_modulate_1_bwd")
    gw_in = sum_parts(parts_l, "sum_grad_in")[:shard_rows[0]].T
    g_rows = sum_parts(parts_e, "sum_grads_rest")
    gw_ba, gw_bb, gw_o, gw_fg, gw_fu, gw_fd = (g_rows[o - slot_rows[0]:o - slot_rows[0] + r]
                                               for o, r in zip(row_offs[1:], shard_rows[1:]))
    gw_fg, gw_fu = gw_fg.T, gw_fu.T

    dmod = jnp.concatenate([acc1[1:2], acc1[0:1], acc2[2:3], acc2[1:2], acc2[0:1], acc4[0:1]], axis=1)
    small_fields = [dmod, acc4[3:4, :128], dlb, dgn, dcw[:CONV_TAPS], dcb, dbias, dalog, ddsk, dwn,
                    acc2[3:4], acc2[4:5], acc4[1:2], acc4[2:3]]
    small_out, offs = _pack(small_fields)
    small_sum_in = allgather_vmem(small_out, "allgather_small_grads")
    gsum, g_lb = reduce_small(small_sum_in, hgrn_lb, offs[2])
    (g_bada, loss_row, _, g_gn, g_cw_full, g_cb, g_bias4, g_alog4, g_dsk4, g_wn, g_l1g, g_l1b, g_l2g, g_l2b) = _unpack(
        gsum, offs, [(1, 6 * D), (1, 128), (1, D), (1, HK), (CONV_TAPS, CONV_DIM), (1, CONV_DIM),
                     (N_GROUPS, N_STATE), (N_GROUPS, N_STATE), (N_GROUPS, N_STATE), (1, B_INNER),
                     (1, D), (1, D), (1, D), (1, D)])
    loss = loss_row[0, 0]
    g_cw = lax.dynamic_slice(g_cw_full, (0, me * conv_cols), (CONV_TAPS, conv_cols))[None]
    g_dtb = g_bias4[:, :HEADS_PER_GROUP].reshape(1, 32)
    g_alog = g_alog4[:, :HEADS_PER_GROUP].reshape(1, 32)
    g_dsk = g_dsk4[:, :HEADS_PER_GROUP].reshape(1, 32)

    dmod_all = small_sum_in[:, 0, offs[0]:offs[0] + 6 * D]
    dmod_cols = lax.dynamic_slice(dmod_all, (0, me * ada_cols), (N_DEV, ada_cols))
    gw_ada = ada_weight_grad(c_all, dmod_cols)

    big = [("ada", w_ada[0], gw_ada, m_w_ada[0], v_w_ada[0]), ("in", w_in[0], gw_in, m_w_in[0], v_w_in[0]),
           ("branch_a", w_branch_a[0], gw_ba, m_w_branch_a[0], v_w_branch_a[0]),
           ("branch_b", w_branch_b[0], gw_bb, m_w_branch_b[0], v_w_branch_b[0]),
           ("o", w_o[0], gw_o, m_w_o[0], v_w_o[0]),
           ("ffn_gate", w_ffn_gate[0], gw_fg, m_w_ffn_gate[0], v_w_ffn_gate[0]),
           ("ffn_up", w_ffn_up[0], gw_fu, m_w_ffn_up[0], v_w_ffn_up[0]),
           ("ffn_down", w_ffn_down[0], gw_fd, m_w_ffn_down[0], v_w_ffn_down[0])]
    big_out = {}
    for nm, w_, g_, m_, v_ in big:
        d_, m2_, v2_ = adamw(w_, g_, m_, v_, "adamw_" + nm)
        big_out[nm] = (g_[None], d_[None], m2_[None], v2_[None])

    small_w = [b_ada, hgrn_lb, hgrn_gnorm, ssm_conv_w, ssm_conv_b, ssm_dt_bias, ssm_a_log, ssm_d, ssm_norm,
               ln1_g, ln1_b, ln2_g, ln2_b]
    small_g = [g_bada, g_lb, g_gn, g_cw, g_cb, g_dtb, g_alog, g_dsk, g_wn, g_l1g, g_l1b, g_l2g, g_l2b]
    small_m = [m_b_ada, m_hgrn_lb, m_hgrn_gnorm, m_ssm_conv_w, m_ssm_conv_b, m_ssm_dt_bias, m_ssm_a_log, m_ssm_d,
               m_ssm_norm, m_ln1_g, m_ln1_b, m_ln2_g, m_ln2_b]
    small_v = [v_b_ada, v_hgrn_lb, v_hgrn_gnorm, v_ssm_conv_w, v_ssm_conv_b, v_ssm_dt_bias, v_ssm_a_log, v_ssm_d,
               v_ssm_norm, v_ln1_g, v_ln1_b, v_ln2_g, v_ln2_b]
    shapes = [a.shape for a in small_w]
    small_g = [g_.reshape(s) for g_, s in zip(small_g, shapes)]
    pw, poffs = _pack(small_w)
    pg, _ = _pack(small_g)
    pm, _ = _pack(small_m)
    pv, _ = _pack(small_v)
    pd, pm2, pv2 = adamw(pw, pg, pm, pv, "adamw_small")
    s_d, s_m, s_v = (_unpack(p, poffs, shapes) for p in (pd, pm2, pv2))
    (sn_bada, sn_lb, sn_gn, sn_cw, sn_cb, sn_dtb, sn_alog, sn_dsk, sn_wn, sn_l1g, sn_l1b, sn_l2g, sn_l2b) = range(13)

    def order(kind):
        sm = [small_g, s_d, s_m, s_v][kind]
        bg = lambda nm: big_out[nm][kind]
        return [bg("ada"), sm[sn_bada], bg("in"), sm[sn_lb], sm[sn_gn], sm[sn_cw], sm[sn_cb], sm[sn_dtb], sm[sn_alog],
                sm[sn_dsk], sm[sn_wn], bg("branch_a"), bg("branch_b"), bg("o"), sm[sn_l1g], sm[sn_l1b],
                bg("ffn_gate"), bg("ffn_up"), bg("ffn_down"), sm[sn_l2g], sm[sn_l2b]]

    return (loss, dx[None], *order(0), *order(1), *order(2), *order(3))
```

```python
import numpy as np
import jax
import jax.numpy as jnp
from jax import lax
from jax.experimental import pallas as pl
from jax.experimental.pallas import tpu as pltpu

F32 = jnp.float32
BF16 = jnp.bfloat16
HI = lax.Precision.HIGHEST

N_DEV = 8
D = 1024
N_HEADS_A = 8
HK = 128
CHUNK = 64
SSD_CHUNK = 128
SSD_CHUNK_BWD = 256
N_GROUPS = 4
HEADS_PER_GROUP = 8
HEAD_P = 64
N_STATE = 128
GROUP_W = HEADS_PER_GROUP * HEAD_P
B_INNER = 2048
CONV_DIM = 3072
D_FF = 2816
IN_DIM = 11296
N_PROJ = 12288
ALPHA = 2.0 ** 0.25
LN_EPS = 1e-5
RMS_EPS = 1e-6
Q_SCALE = 128 ** -0.5
EXP_CLIP = 80.0
ADAM_LR, ADAM_B1, ADAM_B2, ADAM_EPS, ADAM_WD, ADAM_STEP = 0.001, 0.9, 0.999, 1e-8, 0.01, 10
VMEM_LIMIT = 48 * 1024 * 1024
TOKEN_BLOCK = 1024
ROW_TILE = 512
WIDE_ROW_TILE = 1024
MM_ROW_TILE = 1024
MM_TOKEN_TILE = 4096
MM_K_TILE = 3072
MM_COL_TILE = 1408
HGRN_HEADS_PER_STEP = 4
CHUNK_UNROLL = 8
MESH_ID = pl.DeviceIdType.MESH

NT_DIMS = (((1,), (1,)), ((), ()))
TN_DIMS = (((0,), (0,)), ((), ()))


def _cparams(sem=None):
    return pltpu.CompilerParams(dimension_semantics=sem, vmem_limit_bytes=VMEM_LIMIT)


def _sigmoid(x):
    return 1.0 / (1.0 + jnp.exp(-x))


def _dsilu(x, s):
    return s * (1.0 + x * (1.0 - s))


def _nt(a, b, precision=None):
    return lax.dot_general(a, b, NT_DIMS, precision=precision, preferred_element_type=F32)


def _tn(a, b, precision=None):
    return lax.dot_general(a, b, TN_DIMS, precision=precision, preferred_element_type=F32)


def _nn(a, b, precision=None):
    return jnp.dot(a, b, precision=precision, preferred_element_type=F32)


def _split(x, pieces):
    out = []
    for i in range(pieces):
        p = x.astype(BF16)
        out.append(p)
        if i + 1 < pieces:
            x = x - p.astype(F32)
    return out


def _sel(dot, x, sel01, pieces, x_first=True):
    acc = None
    for p in _split(x, pieces):
        term = dot(p, sel01) if x_first else dot(sel01, p)
        acc = term if acc is None else acc + term
    return acc


def _ln(x):
    mu = jnp.mean(x, axis=-1, keepdims=True)
    xc = x - mu
    rstd = lax.rsqrt(jnp.mean(xc * xc, axis=-1, keepdims=True) + LN_EPS)
    return xc * rstd, rstd


def _ln_bwd(dxh, xh, rstd):
    return rstd * (dxh - jnp.mean(dxh, axis=-1, keepdims=True) - xh * jnp.mean(dxh * xh, axis=-1, keepdims=True))


def _colsum(x):
    return jnp.sum(x, axis=0, keepdims=True)


def _tri(n, upper=False):
    r = lax.broadcasted_iota(jnp.int32, (n, n), 0)
    c = lax.broadcasted_iota(jnp.int32, (n, n), 1)
    return (c >= r) if upper else (r >= c)


def _my_pos():
    return lax.axis_index("x"), lax.axis_index("y"), lax.axis_index("c")


def _peer(pos, k):
    x, y, c = pos
    return (x ^ ((k >> 2) & 1), y ^ ((k >> 1) & 1), c ^ (k & 1))


def _flat(pos):
    return 4 * pos[0] + 2 * pos[1] + pos[2]


def allgather_vmem(v, name):
    n = v.shape[1]

    def body(v_ref, o_ref, send_sems, recv_sems, local_sem):
        me = _my_pos()
        mine = pltpu.make_async_copy(v_ref, o_ref.at[_flat(me)], local_sem)
        mine.start()
        sends = []
        for k in range(1, N_DEV):
            peer = _peer(me, k)
            cp = pltpu.make_async_remote_copy(v_ref, o_ref.at[_flat(me)], send_sems.at[k - 1], recv_sems.at[k - 1],
                                              device_id=peer, device_id_type=MESH_ID)
            cp.start()
            sends.append(cp)
        for k in range(1, N_DEV):
            peer = _peer(me, k)
            pltpu.make_async_remote_copy(v_ref, o_ref.at[_flat(peer)], send_sems.at[k - 1], recv_sems.at[k - 1],
                                         device_id=peer, device_id_type=MESH_ID).wait_recv()
        for cp in sends:
            cp.wait_send()
        mine.wait()

    return pl.pallas_call(
        body, name=name,
        out_shape=jax.ShapeDtypeStruct((N_DEV, 1, n), F32),
        in_specs=[pl.BlockSpec(memory_space=pltpu.VMEM)],
        out_specs=pl.BlockSpec(memory_space=pltpu.VMEM),
        scratch_shapes=[pltpu.SemaphoreType.DMA((N_DEV - 1,)), pltpu.SemaphoreType.DMA((N_DEV - 1,)),
                        pltpu.SemaphoreType.DMA],
        compiler_params=_cparams(),
    )(v)


def ada_modulation(c_all, w_ada_s, b_ada_r):
    ncol = w_ada_s.shape[1]

    def body(c_ref, w_ref, b_ref, o_ref, part_ref, send_sems, recv_sems):
        me = _my_pos()
        cval = c_ref[...]
        cond = cval * _sigmoid(cval)
        part = _nn(cond, w_ref[...], HI)
        for r in range(N_DEV):
            part_ref[r] = part[r:r + 1, :]
        sends = []
        for k in range(1, N_DEV):
            peer = _peer(me, k)
            cp = pltpu.make_async_remote_copy(part_ref.at[_flat(peer)], o_ref.at[_flat(me)], send_sems.at[k - 1],
                                              recv_sems.at[k - 1], device_id=peer, device_id_type=MESH_ID)
            cp.start()
            sends.append(cp)
        o_ref[_flat(me)] = part_ref[_flat(me)]
        for k in range(1, N_DEV):
            peer = _peer(me, k)
            pltpu.make_async_remote_copy(part_ref.at[_flat(peer)], o_ref.at[_flat(peer)], send_sems.at[k - 1],
                                         recv_sems.at[k - 1], device_id=peer, device_id_type=MESH_ID).wait_recv()
        for cp in sends:
            cp.wait_send()
        o_ref[...] = o_ref[...] + b_ref[...]

    return pl.pallas_call(
        body, name="ada_modulation",
        out_shape=jax.ShapeDtypeStruct((N_DEV, 1, ncol), F32),
        in_specs=[pl.BlockSpec(memory_space=pltpu.VMEM)] * 3,
        out_specs=pl.BlockSpec(memory_space=pltpu.VMEM),
        scratch_shapes=[pltpu.VMEM((N_DEV, 1, ncol), F32), pltpu.SemaphoreType.DMA((N_DEV - 1,)),
                        pltpu.SemaphoreType.DMA((N_DEV - 1,))],
        compiler_params=_cparams(),
    )(c_all, w_ada_s, b_ada_r)


def allgather_hbm(shard, name):
    def body(x_ref, out_ref, send_sems, recv_sems, local_sem):
        x, y, c = _my_pos()
        me, sibling = (x, y, c), (x, y, 1 - c)
        chips = [(1 - x, y), (x, 1 - y), (1 - x, 1 - y)]

        def slot(pos):
            return out_ref.at[_flat(pos)]

        def copy(k, block, to, src=None):
            return pltpu.make_async_remote_copy(slot(block) if src is None else src, slot(block), send_sems.at[k],
                                                recv_sems.at[k], device_id=to, device_id_type=MESH_ID)

        mine = pltpu.make_async_copy(x_ref, slot(me), local_sem)
        mine.start()
        first = [copy(0, me, sibling, src=x_ref)]
        first += [copy(1 + j, me, (*chip, c), src=x_ref) for j, chip in enumerate(chips)]
        for cp in first:
            cp.start()
        passed = [copy(4 + j, (*chip, c), sibling) for j, chip in enumerate(chips)]
        for j, chip in enumerate(chips):
            copy(1 + j, (*chip, c), me).wait_recv()
            passed[j].start()
        copy(0, sibling, me).wait_recv()
        for j, chip in enumerate(chips):
            copy(4 + j, (*chip, 1 - c), me).wait_recv()
        for cp in first + passed:
            cp.wait_send()
        mine.wait()

    return pl.pallas_call(
        body, name=name,
        out_shape=jax.ShapeDtypeStruct((N_DEV,) + shard.shape, shard.dtype),
        in_specs=[pl.BlockSpec(memory_space=pl.ANY)],
        out_specs=pl.BlockSpec(memory_space=pl.ANY),
        scratch_shapes=[pltpu.SemaphoreType.DMA((N_DEV - 1,)), pltpu.SemaphoreType.DMA((N_DEV - 1,)),
                        pltpu.SemaphoreType.DMA],
        compiler_params=_cparams(),
    )(shard)


N_CHIP = N_DEV // 2
SIBLING_SEMS = [pltpu.SemaphoreType.DMA, pltpu.SemaphoreType.DMA]
CHIP_SEMS = [pltpu.SemaphoreType.DMA((N_CHIP - 1,)), pltpu.SemaphoreType.DMA((N_CHIP - 1,)), pltpu.SemaphoreType.DMA]


def _sibling_exchange(s_ref, o_ref, send_sem, recv_sem):
    x, y, c = _my_pos()
    cp = pltpu.make_async_remote_copy(s_ref, o_ref, send_sem, recv_sem, device_id=(x, y, 1 - c), device_id_type=MESH_ID)
    return cp.start, cp.wait


def _chip_exchange(p_ref, o_ref, send_sems, recv_sems, local_sem):
    x, y, c = _my_pos()
    my_chip = 2 * x + y
    mine = pltpu.make_async_copy(p_ref.at[my_chip], o_ref.at[my_chip], local_sem)
    peers = [(x ^ (k >> 1), y ^ (k & 1)) for k in range(1, N_CHIP)]
    sends = [pltpu.make_async_remote_copy(p_ref.at[2 * px + py], o_ref.at[my_chip], send_sems.at[k], recv_sems.at[k],
                                          device_id=(px, py, c), device_id_type=MESH_ID)
             for k, (px, py) in enumerate(peers)]
    recvs = [pltpu.make_async_remote_copy(p_ref.at[2 * px + py], o_ref.at[2 * px + py], send_sems.at[k], recv_sems.at[k],
                                          device_id=(px, py, c), device_id_type=MESH_ID)
             for k, (px, py) in enumerate(peers)]

    def start():
        mine.start()
        for cp in sends:
            cp.start()

    def wait():
        for cp in recvs:
            cp.wait_recv()
        for cp in sends:
            cp.wait_send()
        mine.wait()

    return start, wait


def exchange_sibling(send, name):
    def body(s_ref, o_ref, send_sem, recv_sem):
        start, wait = _sibling_exchange(s_ref, o_ref, send_sem, recv_sem)
        start()
        wait()

    return pl.pallas_call(
        body, name=name,
        out_shape=jax.ShapeDtypeStruct(send.shape, send.dtype),
        in_specs=[pl.BlockSpec(memory_space=pl.ANY)],
        out_specs=pl.BlockSpec(memory_space=pl.ANY),
        scratch_shapes=SIBLING_SEMS,
        compiler_params=_cparams(),
    )(send)


LANES = 128


def _k_tile(kdim, unit=LANES):
    for cand in range(MM_K_TILE - MM_K_TILE % unit, 0, -unit):
        if kdim % cand == 0:
            return cand
    return kdim


def _lane_tile(n, cap):
    for cand in range(cap - cap % LANES, 0, -LANES):
        if n % cand == 0:
            return cand
    return n


def _m_tile(m, kdim):
    return min(MM_ROW_TILE if kdim > D else 2 * MM_ROW_TILE, m)


def mm_nn_exchange(a, b, out_dtype, part, name):
    kblocks, m, kb = a.shape
    kdim = kblocks * kb
    n = b.shape[1]
    tm, tn, tk = min(MM_ROW_TILE, m), _lane_tile(n, MM_COL_TILE), _k_tile(kdim)
    gn, gm, nk = n // tn, m // tm, kdim // tk
    per_step = tk // kb

    def body(a_ref, b_ref, part_ref, o_ref, parts_ref, acc_ref, send_sems, recv_sems, local_sem):
        j, i, k = pl.program_id(0), pl.program_id(1), pl.program_id(2)
        xchg_start, xchg_wait = _chip_exchange(part_ref, parts_ref, send_sems, recv_sems, local_sem)

        @pl.when((j == 0) & (i == 0) & (k == 0))
        def _():
            xchg_start()

        p = _nn(a_ref[0], b_ref[0:kb, :])
        for c in range(1, per_step):
            p = p + _nn(a_ref[c], b_ref[c * kb:(c + 1) * kb, :])

        @pl.when(k == 0)
        def _():
            acc_ref[...] = p

        @pl.when(k > 0)
        def _():
            acc_ref[...] += p

        @pl.when(k == nk - 1)
        def _():
            o_ref[...] = acc_ref[...].astype(o_ref.dtype)

        @pl.when((j == gn - 1) & (i == gm - 1) & (k == nk - 1))
        def _():
            xchg_wait()

    hbm = pl.BlockSpec(memory_space=pl.ANY)
    return pl.pallas_call(
        body, name=name, grid=(gn, gm, nk),
        out_shape=[jax.ShapeDtypeStruct((m, n), out_dtype), jax.ShapeDtypeStruct(part.shape, part.dtype)],
        in_specs=[pl.BlockSpec((per_step, tm, kb), lambda j, i, k: (k, i, 0)),
                  pl.BlockSpec((tk, tn), lambda j, i, k: (k, j)), hbm],
        out_specs=[pl.BlockSpec((tm, tn), lambda j, i, k: (i, j)), hbm],
        scratch_shapes=[pltpu.VMEM((tm, tn), F32)] + CHIP_SEMS,
        compiler_params=_cparams(("arbitrary", "arbitrary", "arbitrary")),
    )(a, b, part)


def mm_nt_gather(a, b, out_dtype, shard, name):
    m, kdim = a.shape
    n = b.shape[0]
    tm, tn = _m_tile(m, kdim), 1024
    assert kdim == 1024
    gj = m // tm
    nsteps = (n // tn) * gj
    forward_step = max(nsteps - 2, 0)

    def body(a_ref, b_ref, x_ref, o_ref, g_ref, send_sems, recv_sems, local_sem):
        step = pl.program_id(0) * gj + pl.program_id(1)
        x, y, c = _my_pos()
        me, sibling = (x, y, c), (x, y, 1 - c)
        chips = [(1 - x, y), (x, 1 - y), (1 - x, 1 - y)]

        def slot(pos):
            return g_ref.at[_flat(pos)]

        def copy(k, block, to, src=None):
            return pltpu.make_async_remote_copy(slot(block) if src is None else src, slot(block), send_sems.at[k],
                                                recv_sems.at[k], device_id=to, device_id_type=MESH_ID)

        mine = pltpu.make_async_copy(x_ref, slot(me), local_sem)
        first = [copy(0, me, sibling, src=x_ref)]
        first += [copy(1 + j, me, (*chip, c), src=x_ref) for j, chip in enumerate(chips)]
        passed = [copy(4 + j, (*chip, c), sibling) for j, chip in enumerate(chips)]

        @pl.when(step == 0)
        def _():
            mine.start()
            for cp in first:
                cp.start()

        rows = pl.ds(pl.multiple_of(pl.program_id(1) * tm, tm), tm)
        o_ref[...] = _nt(a_ref[rows, :], b_ref[...]).astype(o_ref.dtype)

        @pl.when(step == forward_step)
        def _():
            for j, chip in enumerate(chips):
                copy(1 + j, (*chip, c), me).wait_recv()
                passed[j].start()

        @pl.when(step == nsteps - 1)
        def _():
            copy(0, sibling, me).wait_recv()
            for j, chip in enumerate(chips):
                copy(4 + j, (*chip, 1 - c), me).wait_recv()
            for cp in first + passed:
                cp.wait_send()
            mine.wait()

    return pl.pallas_call(
        body, name=name, grid=(n // tn, gj),
        out_shape=[jax.ShapeDtypeStruct((m, n), out_dtype), jax.ShapeDtypeStruct((N_DEV,) + shard.shape, shard.dtype)],
        in_specs=[pl.BlockSpec(memory_space=pltpu.VMEM), pl.BlockSpec((tn, kdim), lambda j, i: (j, 0)),
                  pl.BlockSpec(memory_space=pl.ANY)],
        out_specs=[pl.BlockSpec((tm, tn), lambda j, i: (i, j)), pl.BlockSpec(memory_space=pl.ANY)],
        scratch_shapes=[pltpu.SemaphoreType.DMA((N_DEV - 1,)), pltpu.SemaphoreType.DMA((N_DEV - 1,)),
                        pltpu.SemaphoreType.DMA],
        compiler_params=_cparams(("arbitrary", "arbitrary")),
    )(a, b, shard)


def mm_tn(a, b, name):
    tt, tn = min(MM_TOKEN_TILE, b.shape[0]), _lane_tile(b.shape[1], MM_COL_TILE)
    tka = _lane_tile(a.shape[-1], 1024)
    if a.ndim == 3:
        t, ka = a.shape[1], a.shape[0] * a.shape[2]
        per = a.shape[2] // tka
        a_spec = pl.BlockSpec((None, tt, tka), lambda i, j, s: (i // per, s, i % per))
    else:
        t, ka = a.shape
        a_spec = pl.BlockSpec((tt, tka), lambda i, j, s: (s, i))
    n = b.shape[1]
    nt = t // tt

    def body(a_ref, b_ref, o_ref, *acc):
        p = _tn(a_ref[...], b_ref[...])
        if nt == 1:
            o_ref[...] = p.astype(o_ref.dtype)
        else:
            acc_ref, s = acc[0], pl.program_id(2)

            @pl.when(s == 0)
            def _():
                acc_ref[...] = p

            @pl.when(s > 0)
            def _():
                acc_ref[...] += p

            @pl.when(s == nt - 1)
            def _():
                o_ref[...] = acc_ref[...].astype(o_ref.dtype)

    return pl.pallas_call(
        body, name=name, grid=(ka // tka, n // tn, nt),
        out_shape=jax.ShapeDtypeStruct((ka, n), BF16),
        in_specs=[a_spec, pl.BlockSpec((tt, tn), lambda i, j, s: (s, j))],
        out_specs=pl.BlockSpec((tka, tn), lambda i, j, s: (i, j)),
        scratch_shapes=[] if nt == 1 else [pltpu.VMEM((tka, tn), F32)],
        compiler_params=_cparams(("parallel", "parallel", "arbitrary")),
    )(a, b)


def _tile(t, cap):
    return min(cap, t)


def ln_modulate(x, mod6, shift_row, scale_row, name):
    t = x.shape[0]
    tm = _tile(t, WIDE_ROW_TILE)

    def body(x_ref, mod_ref, o_ref):
        xh, _ = _ln(x_ref[...])
        sc = mod_ref[scale_row:scale_row + 1, :]
        sh = mod_ref[shift_row:shift_row + 1, :]
        o_ref[...] = (xh * (1.0 + sc) + sh).astype(BF16)

    return pl.pallas_call(
        body, name=name, grid=(t // tm,),
        out_shape=jax.ShapeDtypeStruct((t, D), BF16),
        in_specs=[pl.BlockSpec((tm, D), lambda i: (i, 0)), pl.BlockSpec((6, D), lambda i: (0, 0))],
        out_specs=pl.BlockSpec((tm, D), lambda i: (i, 0)),
        compiler_params=_cparams(("parallel",)),
    )(x, mod6)


FF_SHARD_ROWS = D_FF // N_DEV


def _ffn_weight_spec(ndev, slot, index_map):
    return pl.BlockSpec((ndev, FF_SHARD_ROWS, D), lambda *ids: (index_map(*ids), slot, 0))


def ffn_tail_loss_bwd(x1, act, gathered, dn_slot, mod6, ln_g, ln_b, target):
    t = x1.shape[0]
    tm = _tile(t, ROW_TILE)
    kdim = act.shape[1]

    def body(x_ref, a_ref, w_ref, mod_ref, g_ref, b_ref, c_ref, dh_ref, dx_ref, acc_ref):
        @pl.when(pl.program_id(0) == 0)
        def _():
            acc_ref[...] = jnp.zeros_like(acc_ref)

        hv = _nn(a_ref[...], w_ref[...].reshape(kdim, D))
        gate = mod_ref[5:6, :]
        rh, rstd = _ln(ALPHA * x_ref[...] + gate * hv)
        lng = g_ref[...]
        diff = rh * lng + b_ref[...] - c_ref[...]
        dxo = diff * (1.0 / D)
        lsum = jnp.sum(_colsum(diff * diff), axis=-1, keepdims=True) * (0.5 / D)
        acc_ref[3:4, :] += jnp.broadcast_to(lsum, (1, D))
        acc_ref[1:2, :] += _colsum(dxo * rh)
        acc_ref[2:3, :] += _colsum(dxo)
        dr = _ln_bwd(dxo * lng, rh, rstd)
        acc_ref[0:1, :] += _colsum(dr * hv)
        dh_ref[...] = (gate * dr).astype(BF16)
        dx_ref[...] = ALPHA * dr

    row = pl.BlockSpec((tm, D), lambda i: (i, 0))
    vec = pl.BlockSpec((1, D), lambda i: (0, 0))
    return pl.pallas_call(
        body, name="ffn_tail_loss_bwd", grid=(t // tm,),
        out_shape=[jax.ShapeDtypeStruct((t, D), BF16), jax.ShapeDtypeStruct((t, D), F32),
                   jax.ShapeDtypeStruct((8, D), F32)],
        in_specs=[row, pl.BlockSpec((tm, kdim), lambda i: (i, 0)), _ffn_weight_spec(N_DEV, dn_slot, lambda i: 0),
                  pl.BlockSpec((6, D), lambda i: (0, 0)), vec, vec, row],
        out_specs=[row, row, pl.BlockSpec((8, D), lambda i: (0, 0))],
        compiler_params=_cparams(("arbitrary",)),
    )(x1, act, gathered, mod6, ln_g, ln_b, target)


def ln_modulate_bwd(x, du, mod6, scale_row, dx_part, name):
    t = x.shape[0]
    tm = _tile(t, ROW_TILE)

    def body(x_ref, du_ref, mod_ref, dp_ref, dx_ref, acc_ref):
        @pl.when(pl.program_id(0) == 0)
        def _():
            acc_ref[...] = jnp.zeros_like(acc_ref)

        xh, rstd = _ln(x_ref[...])
        du_v = du_ref[...]
        sc = mod_ref[scale_row:scale_row + 1, :]
        acc_ref[0:1, :] += _colsum(du_v * xh)
        acc_ref[1:2, :] += _colsum(du_v)
        dx_ref[...] = dp_ref[...] + _ln_bwd(du_v * (1.0 + sc), xh, rstd)

    row = pl.BlockSpec((tm, D), lambda i: (i, 0))
    return pl.pallas_call(
        body, name=name, grid=(t // tm,),
        out_shape=[jax.ShapeDtypeStruct((t, D), F32), jax.ShapeDtypeStruct((8, D), F32)],
        in_specs=[row, row, pl.BlockSpec((6, D), lambda i: (0, 0)), row],
        out_specs=[row, pl.BlockSpec((8, D), lambda i: (0, 0))],
        compiler_params=_cparams(("arbitrary",)),
    )(x, du, mod6, dx_part)


def mixer_tail_bwd(x1, dgu, gathered, gate_slot, up_slot, mod6, dx1_part, x, h, ln_g, ln_b):
    t = x.shape[0]
    tm = _tile(t, ROW_TILE // 2)
    _, _, kb = dgu.shape

    def body(x1_ref, a_ref, wg_ref, wu_ref, mod_ref, dp_ref, x_ref, h_ref, g_ref, b_ref, dh_ref, dx_ref, acc_ref):
        @pl.when(pl.program_id(0) == 0)
        def _():
            acc_ref[...] = jnp.zeros_like(acc_ref)

        du = _nn(a_ref[0], wg_ref[...].reshape(kb, D)) + _nn(a_ref[1], wu_ref[...].reshape(kb, D))
        xh, rstd1 = _ln(x1_ref[...])
        acc_ref[0:1, :] += _colsum(du * xh)
        acc_ref[1:2, :] += _colsum(du)
        dx1 = dp_ref[...] + _ln_bwd(du * (1.0 + mod_ref[4:5, :]), xh, rstd1)
        gate = mod_ref[2:3, :]
        hv = h_ref[...]
        rh, rstd = _ln(ALPHA * x_ref[...] + gate * hv)
        acc_ref[3:4, :] += _colsum(dx1 * rh)
        acc_ref[4:5, :] += _colsum(dx1)
        dr = _ln_bwd(dx1 * g_ref[...], rh, rstd)
        acc_ref[2:3, :] += _colsum(dr * hv)
        dh_ref[...] = (gate * dr).astype(BF16)
        dx_ref[...] = ALPHA * dr

    row = pl.BlockSpec((tm, D), lambda i: (i, 0))
    vec = pl.BlockSpec((1, D), lambda i: (0, 0))
    return pl.pallas_call(
        body, name="mixer_tail_bwd", grid=(t // tm,),
        out_shape=[jax.ShapeDtypeStruct((t, D), BF16), jax.ShapeDtypeStruct((t, D), F32),
                   jax.ShapeDtypeStruct((8, D), F32)],
        in_specs=[row, pl.BlockSpec((2, tm, kb), lambda i: (0, i, 0)), _ffn_weight_spec(N_DEV, gate_slot, lambda i: 0),
                  _ffn_weight_spec(N_DEV, up_slot, lambda i: 0),
                  pl.BlockSpec((6, D), lambda i: (0, 0)), row, row, row, vec, vec],
        out_specs=[row, row, pl.BlockSpec((8, D), lambda i: (0, 0))],
        compiler_params=_cparams(("arbitrary",)),
    )(x1, dgu, gathered, gathered, mod6, dx1_part, x, h, ln_g, ln_b)


def mixer_tail(o_a, o_b, w_ba, w_bb, proj, w_o, x, mod6, ln_g, ln_b):
    t = o_a.shape[0]
    tm = _tile(t, ROW_TILE // 2)

    def body(oa_ref, ob_ref, wa_ref, wb_ref, ga_ref, gb_ref, w_ref, x_ref, mod_ref, g_ref, b_ref,
             ya_ref, yb_ref, m_ref, h_ref, x1_ref, u2_ref):
        ya = _nn(oa_ref[...], wa_ref[...])
        yb = _nn(ob_ref[...], wb_ref[...])
        ya_ref[...] = ya.astype(BF16)
        yb_ref[...] = yb.astype(BF16)
        merged = (_sigmoid(ga_ref[...]) * ya + _sigmoid(gb_ref[...]) * yb).astype(BF16)
        m_ref[...] = merged
        hv = _nn(merged, w_ref[...])
        h_ref[...] = hv
        rh, _ = _ln(ALPHA * x_ref[...] + mod_ref[2:3, :] * hv)
        x1 = rh * g_ref[...] + b_ref[...]
        x1_ref[...] = x1
        xh, _ = _ln(x1)
        u2_ref[...] = (xh * (1.0 + mod_ref[4:5, :]) + mod_ref[3:4, :]).astype(BF16)

    row = pl.BlockSpec((tm, D), lambda i: (i, 0))
    vec = pl.BlockSpec((1, D), lambda i: (0, 0))
    whole = pl.BlockSpec(memory_space=pltpu.VMEM)
    return pl.pallas_call(
        body, name="mixer_tail", grid=(t // tm,),
        out_shape=[jax.ShapeDtypeStruct((t, D), BF16), jax.ShapeDtypeStruct((t, D), BF16),
                   jax.ShapeDtypeStruct((t, D), BF16), jax.ShapeDtypeStruct((t, D), F32),
                   jax.ShapeDtypeStruct((t, D), F32), jax.ShapeDtypeStruct((t, D), BF16)],
        in_specs=[row, pl.BlockSpec((tm, o_b.shape[1]), lambda i: (i, 0)), whole, whole,
                  pl.BlockSpec((tm, D), lambda i: (i, GATE_BLOCK0)),
                  pl.BlockSpec((tm, D), lambda i: (i, GATE_BLOCK0 + 1)), whole,
                  row, pl.BlockSpec((6, D), lambda i: (0, 0)), vec, vec],
        out_specs=[row] * 6,
        compiler_params=_cparams(("parallel",)),
    )(o_a, o_b, w_ba, w_bb, proj, proj, w_o, x, mod6, ln_g, ln_b)


def merge_gates_bwd(dh, w_o, ya, yb, proj):
    t = ya.shape[0]
    tm = _tile(t, ROW_TILE)

    def body(dh_ref, w_ref, ya_ref, yb_ref, ga_ref, gb_ref, dya_ref, dyb_ref, dp_ref):
        dmv = _nt(dh_ref[...], w_ref[...])
        sa = _sigmoid(ga_ref[...])
        sb = _sigmoid(gb_ref[...])
        dya_ref[...] = (dmv * sa).astype(BF16)
        dyb_ref[...] = (dmv * sb).astype(BF16)
        dp_ref[0] = (dmv * ya_ref[...].astype(F32) * sa * (1.0 - sa)).astype(BF16)
        dp_ref[1] = (dmv * yb_ref[...].astype(F32) * sb * (1.0 - sb)).astype(BF16)

    row = pl.BlockSpec((tm, D), lambda i: (i, 0))
    return pl.pallas_call(
        body, name="merge_gates_bwd", grid=(t // tm,),
        out_shape=[jax.ShapeDtypeStruct((t, D), BF16)] * 2 + [jax.ShapeDtypeStruct((N_PROJ // D, t, D), BF16)],
        in_specs=[row, pl.BlockSpec((D, D), lambda i: (0, 0)), row, row,
                  pl.BlockSpec((tm, D), lambda i: (i, GATE_BLOCK0)),
                  pl.BlockSpec((tm, D), lambda i: (i, GATE_BLOCK0 + 1))],
        out_specs=[row, row, pl.BlockSpec((2, tm, D), lambda i: (GATE_BLOCK0 // 2, i, 0))],
        compiler_params=_cparams(("parallel",)),
    )(dh, w_o, ya, yb, proj, proj)


FF_CHUNK = 1408


def ffn_in_act(u, gathered, gate_slot, up_slot):
    t = u.shape[0]
    tm = _tile(t, ROW_TILE)
    nj = D_FF // FF_CHUNK
    ndev = FF_CHUNK // FF_SHARD_ROWS

    def body(a_ref, bg_ref, bu_ref, gu_ref, act_ref):
        a = a_ref[...]
        g = _nt(a, bg_ref[...].reshape(FF_CHUNK, D))
        up = _nt(a, bu_ref[...].reshape(FF_CHUNK, D))
        gu_ref[0] = g.astype(BF16)
        gu_ref[1] = up.astype(BF16)
        act_ref[...] = (g * _sigmoid(g) * up).astype(BF16)

    return pl.pallas_call(
        body, name="ffn_in_act", grid=(nj, t // tm),
        out_shape=[jax.ShapeDtypeStruct((2, t, D_FF), BF16), jax.ShapeDtypeStruct((t, D_FF), BF16)],
        in_specs=[pl.BlockSpec((tm, D), lambda j, i: (i, 0)), _ffn_weight_spec(ndev, gate_slot, lambda j, i: j),
                  _ffn_weight_spec(ndev, up_slot, lambda j, i: j)],
        out_specs=[pl.BlockSpec((2, tm, FF_CHUNK), lambda j, i: (0, i, j)),
                   pl.BlockSpec((tm, FF_CHUNK), lambda j, i: (i, j))],
        compiler_params=_cparams(("parallel", "parallel")),
    )(u, gathered, gathered)


def ffn_act_bwd(dh, gathered, dn_slot, gu):
    t = dh.shape[0]
    tm = _tile(t, ROW_TILE)
    ndev = FF_CHUNK // FF_SHARD_ROWS

    def body(a_ref, b_ref, gu_ref, o_ref):
        da = _nt(a_ref[...], b_ref[...].reshape(FF_CHUNK, D))
        g = gu_ref[0].astype(F32)
        up = gu_ref[1].astype(F32)
        s = _sigmoid(g)
        o_ref[0] = (da * up * _dsilu(g, s)).astype(BF16)
        o_ref[1] = (da * g * s).astype(BF16)

    blk = pl.BlockSpec((2, tm, FF_CHUNK), lambda j, i: (0, i, j))
    return pl.pallas_call(
        body, name="ffn_act_bwd", grid=(D_FF // FF_CHUNK, t // tm),
        out_shape=jax.ShapeDtypeStruct((2, t, D_FF), BF16),
        in_specs=[pl.BlockSpec((tm, D), lambda j, i: (i, 0)), _ffn_weight_spec(ndev, dn_slot, lambda j, i: j), blk],
        out_specs=blk,
        compiler_params=_cparams(("parallel", "parallel")),
    )(dh, gathered, gu)


def _hgrn_chunk_terms(q, fl, lbv, tril_f):
    sig = _sigmoid(fl)
    f = lbv + (1.0 - lbv) * sig
    lam = jnp.log(f)
    k = 1.0 - f
    sq = _sigmoid(q)
    qt = q * sq * Q_SCALE
    bc = _sel(_nn, lam, tril_f, 3, x_first=False)
    bmid = bc[CHUNK // 2 - 1:CHUNK // 2, :]
    bl = bc[CHUNK - 1:CHUNK, :]
    eq = jnp.exp(jnp.minimum(bc - bmid, EXP_CLIP))
    ek = jnp.exp(jnp.minimum(bmid - bc, EXP_CLIP))
    eb = jnp.exp(bc)
    ekl = jnp.exp(bl - bc)
    ebl = jnp.exp(bl)
    return sig, f, k, sq, qt, eq, ek, eb, ekl, ebl


def hgrn_fwd(proj, lb, gnorm):
    t = proj.shape[0]
    tb = _tile(t, TOKEN_BLOCK)
    ncb = tb // CHUNK

    hps = HGRN_HEADS_PER_STEP
    wide = hps * HK

    def body(q_ref, f_ref, i_ref, g_ref, lb_ref, gn_ref, oa_ref, oraw_ref, st_ref, state):
        @pl.when(pl.program_id(1) == 0)
        def _():
            state[...] = jnp.zeros_like(state)

        gn = gn_ref[...]
        mask = _tri(CHUNK)
        tril_f = mask.astype(BF16)

        def chunk(c, carry):
            sl = pl.ds(pl.multiple_of(c * CHUNK, CHUNK), CHUNK)
            for hh in range(hps):
                ln = slice(hh * HK, (hh + 1) * HK)
                q, fl, v, g = q_ref[sl, ln], f_ref[sl, ln], i_ref[sl, ln], g_ref[sl, ln]
                sig, f, k, sq, qt, eq, ek, eb, ekl, ebl = _hgrn_chunk_terms(q, fl, lb_ref[:, ln], tril_f)
                a = jnp.where(mask, _nt((qt * eq).astype(BF16), (k * ek).astype(BF16)), 0.0)
                st = state[hh]
                st_ref[hh, c] = st
                vb = v.astype(BF16)
                o = _nn(a.astype(BF16), vb) + _nt((qt * eb).astype(BF16), st.astype(BF16))
                state[hh] = st * ebl + _tn(vb, (k * ekl).astype(BF16))
                oraw_ref[sl, ln] = o
                rn = o * lax.rsqrt(jnp.mean(o * o, axis=-1, keepdims=True) + RMS_EPS)
                oa_ref[sl, ln] = (rn * gn * g * _sigmoid(g)).astype(BF16)
            return carry

        lax.fori_loop(0, ncb, chunk, 0, unroll=min(CHUNK_UNROLL, ncb))

    def col(block):
        return pl.BlockSpec((tb, wide), lambda h, j: (j, block * (N_HEADS_A // hps) + h))

    return pl.pallas_call(
        body, name="hgrn_fwd", grid=(N_HEADS_A // hps, t // tb),
        out_shape=[jax.ShapeDtypeStruct((t, D), BF16), jax.ShapeDtypeStruct((t, D), F32),
                   jax.ShapeDtypeStruct((N_HEADS_A, t // CHUNK, HK, HK), F32)],
        in_specs=[col(0), col(1), col(2), col(3), pl.BlockSpec((1, wide), lambda h, j: (0, h)),
                  pl.BlockSpec((1, HK), lambda h, j: (0, 0))],
        out_specs=[pl.BlockSpec((tb, wide), lambda h, j: (j, h)), pl.BlockSpec((tb, wide), lambda h, j: (j, h)),
                   pl.BlockSpec((hps, ncb, HK, HK), lambda h, j: (h, j, 0, 0))],
        scratch_shapes=[pltpu.VMEM((hps, HK, HK), F32)],
        compiler_params=_cparams(("parallel", "arbitrary")),
    )(proj, proj, proj, proj, lb, gnorm)


def hgrn_bwd(proj, lb, gnorm, o_raw, dya, w_ba, states, give, dproj):
    t = proj.shape[0]
    tb = _tile(t, TOKEN_BLOCK)
    ncb = tb // CHUNK
    nb = t // tb
    hps = HGRN_HEADS_PER_STEP
    wide = hps * HK

    def body(q_ref, f_ref, i_ref, g_ref, lb_ref, gn_ref, oraw_ref, dya_ref, wba_ref, st_ref, give_ref, dp_in_ref,
             dp_ref, dlb_ref, dgn_ref, got_ref, dstate, doa_ref, send_sem, recv_sem):
        h, j = pl.program_id(0), pl.program_id(1)
        swap_start, swap_wait = _sibling_exchange(give_ref, got_ref, send_sem, recv_sem)
        doa_ref[...] = _nt(dya_ref[...], wba_ref[...])

        @pl.when((h == 0) & (j == 0))
        def _():
            swap_start()

        @pl.when(j == 0)
        def _():
            dstate[...] = jnp.zeros_like(dstate)
            dlb_ref[...] = jnp.zeros_like(dlb_ref)

        @pl.when((j == 0) & (h == 0))
        def _():
            dgn_ref[...] = jnp.zeros_like(dgn_ref)

        gn = gn_ref[...]
        mask = _tri(CHUNK)
        mask_t = _tri(CHUNK, upper=True)
        tril_f = mask.astype(BF16)
        triu_f = mask_t.astype(BF16)

        def chunk(i, c0):
            c = ncb - 1 - i
            sl = pl.ds(pl.multiple_of(c * CHUNK, CHUNK), CHUNK)
            for hh in range(hps):
                ln = slice(hh * HK, (hh + 1) * HK)
                q, fl, v, g = q_ref[sl, ln], f_ref[sl, ln], i_ref[sl, ln], g_ref[sl, ln]
                lbv = lb_ref[:, ln]
                sig, f, k, sq, qt, eq, ek, eb, ekl, ebl = _hgrn_chunk_terms(q, fl, lbv, tril_f)
                qe = (qt * eq).astype(BF16)
                ke = (k * ek).astype(BF16)
                st32 = st_ref[hh, c]
                st = st32.astype(BF16)
                dst = dstate[hh]
                dstb = dst.astype(BF16)
                o = oraw_ref[sl, ln]
                rstd = lax.rsqrt(jnp.mean(o * o, axis=-1, keepdims=True) + RMS_EPS)
                rn = o * rstd
                sgm = _sigmoid(g)
                sg = g * sgm
                doa_v = doa_ref[sl, ln]
                drn = doa_v * gn * sg
                dgn_ref[...] += _colsum(doa_v * rn * sg)
                dp_ref[3, sl, ln] = (doa_v * rn * gn * _dsilu(g, sgm)).astype(BF16)
                do = rstd * (drn - rn * jnp.mean(drn * rn, axis=-1, keepdims=True))
                dob = do.astype(BF16)
                vb = v.astype(BF16)
                da = jnp.where(mask, _nt(dob, vb), 0.0).astype(BF16)
                da_t = jnp.where(mask_t, _nt(vb, dob), 0.0).astype(BF16)
                a_t = jnp.where(mask_t, _nt(ke, qe), 0.0).astype(BF16)
                kl = (k * ekl).astype(BF16)
                qb = (qt * eb).astype(BF16)
                dq_in = _nn(da, ke)
                dk_in = _nn(da_t, qe)
                dq_out = eb * _nn(dob, st)
                dk_out = ekl * _nn(vb, dstb)
                dqt = eq * dq_in + dq_out
                dk = ek * dk_in + dk_out
                dv = _nn(a_t, dob) + _nt(kl, dstb)
                dstate[hh] = dst * ebl + _tn(dob, qb)
                dbig = qe.astype(F32) * dq_in - ke.astype(F32) * dk_in + qt * dq_out - k * dk_out
                beyond = _colsum(k * dk_out) + ebl * _colsum(dst * st32)
                dlam = _sel(_nn, dbig, triu_f, 3, x_first=False) + beyond
                df = dlam / f - dk
                dp_ref[1, sl, ln] = (df * (1.0 - lbv) * sig * (1.0 - sig)).astype(BF16)
                dlb_ref[:, ln] += _colsum(df * (1.0 - sig))
                dp_ref[0, sl, ln] = (dqt * Q_SCALE * _dsilu(q, sq)).astype(BF16)
                dp_ref[2, sl, ln] = dv.astype(BF16)
            return c0

        lax.fori_loop(0, ncb, chunk, 0, unroll=min(CHUNK_UNROLL, ncb))

        @pl.when((h == N_HEADS_A // hps - 1) & (j == nb - 1))
        def _():
            swap_wait()

    def col(block):
        return pl.BlockSpec((tb, wide), lambda h, j: (nb - 1 - j, block * (N_HEADS_A // hps) + h))

    hcol = pl.BlockSpec((tb, wide), lambda h, j: (nb - 1 - j, h))
    hbm = pl.BlockSpec(memory_space=pl.ANY)
    return pl.pallas_call(
        body, name="hgrn_bwd", grid=(N_HEADS_A // hps, nb),
        out_shape=[jax.ShapeDtypeStruct(dproj.shape, dproj.dtype), jax.ShapeDtypeStruct((1, D), F32),
                   jax.ShapeDtypeStruct((1, HK), F32), jax.ShapeDtypeStruct(give.shape, give.dtype)],
        in_specs=[col(0), col(1), col(2), col(3), pl.BlockSpec((1, wide), lambda h, j: (0, h)),
                  pl.BlockSpec((1, HK), lambda h, j: (0, 0)), hcol,
                  pl.BlockSpec((tb, D), lambda h, j: (nb - 1 - j, 0)), pl.BlockSpec((wide, D), lambda h, j: (h, 0)),
                  pl.BlockSpec((hps, ncb, HK, HK), lambda h, j: (h, nb - 1 - j, 0, 0)), hbm, hbm],
        out_specs=[pl.BlockSpec((4, tb, wide), lambda h, j: (0, nb - 1 - j, h)),
                   pl.BlockSpec((1, wide), lambda h, j: (0, h)), pl.BlockSpec((1, HK), lambda h, j: (0, 0)), hbm],
        input_output_aliases={11: 0},
        scratch_shapes=[pltpu.VMEM((hps, HK, HK), F32), pltpu.VMEM((tb, wide), F32)] + SIBLING_SEMS,
        compiler_params=_cparams(("arbitrary", "arbitrary")),
    )(proj, proj, proj, proj, lb, gnorm, o_raw, dya, w_ba, states, give, dproj)


CONV_BLOCK0 = 6
CONV_TAPS = 4
HALO = 8


def conv_fwd(proj, conv_w, conv_b):
    t = proj.shape[0]
    tm = _tile(t, ROW_TILE)
    r = tm // HALO

    def body(x_ref, halo_ref, w_ref, b_ref, o_ref, ds_ref):
        i = pl.program_id(1)
        halo = jnp.where(i > 0, halo_ref[...], 0.0)
        ext = jnp.concatenate([halo, x_ref[...]], axis=0)
        pre = b_ref[...] + w_ref[CONV_TAPS - 1:CONV_TAPS, :] * ext[HALO:, :]
        for tap in range(CONV_TAPS - 1):
            pre = pre + w_ref[tap:tap + 1, :] * pltpu.roll(ext, CONV_TAPS - 1 - tap, axis=0)[HALO:, :]
        s = _sigmoid(pre)
        o_ref[...] = pre * s
        ds_ref[...] = _dsilu(pre, s).astype(BF16)

    blk = pl.BlockSpec((tm, D), lambda cb, i: (i, cb))
    return pl.pallas_call(
        body, name="conv_fwd", grid=(CONV_DIM // D, t // tm),
        out_shape=[jax.ShapeDtypeStruct((t, CONV_DIM), F32), jax.ShapeDtypeStruct((t, CONV_DIM), BF16)],
        in_specs=[pl.BlockSpec((tm, D), lambda cb, i: (i, CONV_BLOCK0 + cb)),
                  pl.BlockSpec((HALO, D), lambda cb, i: (jnp.maximum(i * r - 1, 0), CONV_BLOCK0 + cb)),
                  pl.BlockSpec((CONV_TAPS, D), lambda cb, i: (0, cb)), pl.BlockSpec((1, D), lambda cb, i: (0, cb))],
        out_specs=[blk, blk],
        compiler_params=_cparams(("parallel", "parallel")),
    )(proj, proj, conv_w, conv_b)


def conv_bwd(proj, dxc, dsilu, conv_w, dproj):
    t = proj.shape[0]
    tm = _tile(t, ROW_TILE)
    r = tm // HALO
    n = t // tm
    last_halo = t // HALO - 1

    def body(x_ref, prev_ref, d_ref, dnext_ref, s_ref, snext_ref, w_ref, dp_in_ref, dx_ref, dw_ref, db_ref):
        i = pl.program_id(1)

        @pl.when(i == 0)
        def _():
            dw_ref[...] = jnp.zeros_like(dw_ref)
            db_ref[...] = jnp.zeros_like(db_ref)

        dpre = jnp.concatenate([d_ref[...].astype(F32) * s_ref[...].astype(F32),
                                jnp.where(i < n - 1, dnext_ref[0:HALO, :].astype(F32) * snext_ref[0:HALO, :].astype(F32),
                                          0.0)], axis=0)
        dx = w_ref[CONV_TAPS - 1:CONV_TAPS, :] * dpre[:tm, :]
        for tap in range(CONV_TAPS - 1):
            back = CONV_TAPS - 1 - tap
            dx = dx + w_ref[tap:tap + 1, :] * pltpu.roll(dpre, tm + HALO - back, axis=0)[:tm, :]
        dx_ref[...] = dx.astype(BF16)
        dp = dpre[:tm, :]
        db_ref[...] += _colsum(dp)
        prev = jnp.where(i > 0, prev_ref[...], 0.0)
        ext = jnp.concatenate([prev, x_ref[...]], axis=0)
        dw_ref[CONV_TAPS - 1:CONV_TAPS, :] += _colsum(dp * ext[HALO:, :])
        for tap in range(CONV_TAPS - 1):
            dw_ref[tap:tap + 1, :] += _colsum(dp * pltpu.roll(ext, CONV_TAPS - 1 - tap, axis=0)[HALO:, :])

    blk = pl.BlockSpec((tm, D), lambda cb, i: (i, cb))
    nxt = pl.BlockSpec((2 * HALO, D), lambda cb, i: (jnp.minimum((i + 1) * (r // 2), last_halo // 2), cb))
    return pl.pallas_call(
        body, name="conv_bwd", grid=(CONV_DIM // D, n),
        out_shape=[jax.ShapeDtypeStruct(dproj.shape, dproj.dtype), jax.ShapeDtypeStruct((8, CONV_DIM), F32),
                   jax.ShapeDtypeStruct((1, CONV_DIM), F32)],
        in_specs=[pl.BlockSpec((tm, D), lambda cb, i: (i, CONV_BLOCK0 + cb)),
                  pl.BlockSpec((HALO, D), lambda cb, i: (jnp.maximum(i * r - 1, 0), CONV_BLOCK0 + cb)),
                  blk, nxt, blk, nxt,
                  pl.BlockSpec((CONV_TAPS, D), lambda cb, i: (0, cb)), pl.BlockSpec(memory_space=pl.ANY)],
        out_specs=[pl.BlockSpec((None, tm, D), lambda cb, i: (CONV_BLOCK0 + cb, i, 0)),
                   pl.BlockSpec((8, D), lambda cb, i: (0, cb)), pl.BlockSpec((1, D), lambda cb, i: (0, cb))],
        input_output_aliases={7: 0},
        compiler_params=_cparams(("parallel", "arbitrary")),
    )(proj, proj, dxc, dxc, dsilu, dsilu, conv_w, dproj)


def dt_fill(ddt, dproj):
    t = ddt.shape[0]
    tm = _tile(t, WIDE_ROW_TILE)
    w = ddt.shape[1]

    def body(d_ref, dp_in_ref, o_ref):
        o_ref[:, :w] = d_ref[...]
        o_ref[:, w:] = jnp.zeros((tm, D - w), o_ref.dtype)

    return pl.pallas_call(
        body, name="dt_fill", grid=(t // tm,),
        out_shape=jax.ShapeDtypeStruct(dproj.shape, dproj.dtype),
        in_specs=[pl.BlockSpec((tm, w), lambda i: (i, 0)), pl.BlockSpec(memory_space=pl.ANY)],
        out_specs=pl.BlockSpec((None, tm, D), lambda i: (DT_COL_BLOCK, i, 0)),
        input_output_aliases={1: 0},
        compiler_params=_cparams(("parallel",)),
    )(ddt, dproj)


Z_BLOCK0 = 8
DT_COL_BLOCK = 9
DT_BLOCK0 = 8 * DT_COL_BLOCK
GATE_BLOCK0 = 10
B_BLOCK0 = 16
C_BLOCK0 = 20


def _head_expand():
    e = np.zeros((N_STATE, GROUP_W), np.float32)
    for hh in range(HEADS_PER_GROUP):
        e[hh, hh * HEAD_P:(hh + 1) * HEAD_P] = 1.0
    return jnp.asarray(e, BF16)


def _ssd_chunk_terms(dt, bias, alog, expand, tril_f, eye):
    dtb = dt + bias
    delta = jnp.maximum(dtb, 0.0) + jnp.log(1.0 + jnp.exp(-jnp.abs(dtb)))
    ea = jnp.exp(alog)
    a = -ea * delta
    acum = _sel(_nn, a, tril_f, 3, x_first=False)
    delta_e = _sel(_nn, delta, expand, 2)
    acum_e = _sel(_nn, acum, expand, 2)
    acum_t = _sel(_nt, acum, eye, 3, x_first=False)
    return dtb, delta, ea, a, acum, delta_e, acum_e, acum_t


def ssd_fwd(proj, xc, alog4, bias4, dskip4, wnorm, expand):
    t = proj.shape[0]
    tb = _tile(t, TOKEN_BLOCK)
    ncb = tb // SSD_CHUNK

    def body(xs_ref, b_ref, c_ref, dt_ref, z_ref, alog_ref, bias_ref, dsk_ref, wn_ref, e_ref, ob_ref, st_ref, state):
        @pl.when(pl.program_id(1) == 0)
        def _():
            state[...] = jnp.zeros_like(state)

        expand = e_ref[...]
        mask = _tri(SSD_CHUNK)
        tril_f = mask.astype(BF16)
        eye = (lax.broadcasted_iota(jnp.int32, (N_STATE, N_STATE), 0) ==
               lax.broadcasted_iota(jnp.int32, (N_STATE, N_STATE), 1)).astype(BF16)
        alog, bias = alog_ref[0], bias_ref[0]
        d_e = _sel(_nn, jnp.broadcast_to(dsk_ref[0], (8, N_STATE)), expand, 3)[0:1, :]
        wn = wn_ref[...]

        def chunk(c, carry):
            sl = pl.ds(pl.multiple_of(c * SSD_CHUNK, SSD_CHUNK), SSD_CHUNK)
            xs, bm, cm, dt, z = xs_ref[sl, :], b_ref[sl, :], c_ref[sl, :], dt_ref[sl, :], z_ref[sl, :]
            dtb, delta, ea, a, acum, delta_e, acum_e, acum_t = _ssd_chunk_terms(dt, bias, alog, expand, tril_f, eye)
            alast_e = acum_e[SSD_CHUNK - 1:SSD_CHUNK, :]
            xd = xs * delta_e
            xdb = xd.astype(BF16)
            cb_, bb_ = cm.astype(BF16), bm.astype(BF16)
            cbm = _nt(cb_, bb_)
            ys = []
            for hh in range(HEADS_PER_GROUP):
                lh = jnp.where(mask, jnp.exp(jnp.minimum(acum[:, hh:hh + 1] - acum_t[hh:hh + 1, :], 0.0)), 0.0)
                ys.append(_nn((cbm * lh).astype(BF16), xdb[:, hh * HEAD_P:(hh + 1) * HEAD_P]))
            st = state[...]
            st_ref[0, c] = st
            y = jnp.concatenate(ys, axis=1) + _nn(cb_, st.astype(BF16)) * jnp.exp(acum_e) + xs * d_e
            state[...] = st * jnp.exp(alast_e) + _tn(bb_, (xd * jnp.exp(alast_e - acum_e)).astype(BF16))
            yg = y * z * _sigmoid(z)
            ob_ref[sl, :] = (yg * lax.rsqrt(jnp.mean(yg * yg, axis=-1, keepdims=True) + RMS_EPS) * wn).astype(BF16)
            return carry

        lax.fori_loop(0, ncb, chunk, 0, unroll=min(CHUNK_UNROLL, ncb))

    small = pl.BlockSpec((1, 1, N_STATE), lambda g, j: (g, 0, 0))
    return pl.pallas_call(
        body, name="ssd_fwd", grid=(N_GROUPS, t // tb),
        out_shape=[jax.ShapeDtypeStruct((t, B_INNER), BF16),
                   jax.ShapeDtypeStruct((N_GROUPS, t // SSD_CHUNK, N_STATE, GROUP_W), F32)],
        in_specs=[pl.BlockSpec((tb, GROUP_W), lambda g, j: (j, g)),
                  pl.BlockSpec((tb, N_STATE), lambda g, j: (j, B_BLOCK0 + g)),
                  pl.BlockSpec((tb, N_STATE), lambda g, j: (j, C_BLOCK0 + g)),
                  pl.BlockSpec((tb, N_STATE), lambda g, j: (j, DT_BLOCK0 + g)),
                  pl.BlockSpec((tb, GROUP_W), lambda g, j: (j, Z_BLOCK0 + g)),
                  small, small, small, pl.BlockSpec((1, GROUP_W), lambda g, j: (0, g)),
                  pl.BlockSpec((N_STATE, GROUP_W), lambda g, j: (0, 0))],
        out_specs=[pl.BlockSpec((tb, GROUP_W), lambda g, j: (j, g)),
                   pl.BlockSpec((1, ncb, N_STATE, GROUP_W), lambda g, j: (g, j, 0, 0))],
        scratch_shapes=[pltpu.VMEM((N_STATE, GROUP_W), F32)],
        compiler_params=_cparams(("parallel", "arbitrary")),
    )(xc, xc, xc, proj, proj, alog4, bias4, dskip4, wnorm, expand)


def ssd_bwd(proj, xc, alog4, bias4, dskip4, wnorm, expand, dyb, w_bb, states, part, dproj):
    t = proj.shape[0]
    tb = _tile(t, TOKEN_BLOCK)
    lc = min(SSD_CHUNK_BWD, tb)
    ncb = tb // lc
    nsaved = tb // SSD_CHUNK
    nb = t // tb

    def body(xs_ref, b_ref, c_ref, dt_ref, z_ref, alog_ref, bias_ref, dsk_ref, wn_ref, e_ref, dyb_ref, wbb_ref, st_ref,
             part_ref, dp_in_ref, dxs_ref, db_ref, dc_ref, dz_ref, ddt_ref, dwn_ref, dalog_ref, dbias_ref, ddsk_ref,
             parts_ref, dstate, dob_ref, send_sems, recv_sems, local_sem):
        xchg_start, xchg_wait = _chip_exchange(part_ref, parts_ref, send_sems, recv_sems, local_sem)
        dob_ref[...] = _nt(dyb_ref[...], wbb_ref[...])

        @pl.when((pl.program_id(0) == 0) & (pl.program_id(1) == 0))
        def _():
            xchg_start()

        @pl.when(pl.program_id(1) == 0)
        def _():
            dstate[...] = jnp.zeros_like(dstate)
            dwn_ref[...] = jnp.zeros_like(dwn_ref)
            dalog_ref[...] = jnp.zeros_like(dalog_ref)
            dbias_ref[...] = jnp.zeros_like(dbias_ref)
            ddsk_ref[...] = jnp.zeros_like(ddsk_ref)

        expand = e_ref[...]
        mask = _tri(lc)
        mask_t = _tri(lc, upper=True)
        tril_f = mask.astype(BF16)
        triu_f = mask_t.astype(BF16)
        eye = (lax.broadcasted_iota(jnp.int32, (N_STATE, N_STATE), 0) ==
               lax.broadcasted_iota(jnp.int32, (N_STATE, N_STATE), 1)).astype(BF16)
        alog, bias = alog_ref[0], bias_ref[0]
        d_e = _sel(_nn, jnp.broadcast_to(dsk_ref[0], (8, N_STATE)), expand, 3)[0:1, :]
        wn = wn_ref[...]

        def chunk(i, c0):
            c = ncb - 1 - i
            sl = pl.ds(pl.multiple_of(c * lc, lc), lc)
            xs, bm, cm, dt, z = xs_ref[sl, :], b_ref[sl, :], c_ref[sl, :], dt_ref[sl, :], z_ref[sl, :]
            dtb, delta, ea, a, acum, delta_e, acum_e, acum_t = _ssd_chunk_terms(dt, bias, alog, expand, tril_f, eye)
            alast_e = acum_e[lc - 1:lc, :]
            eacum = jnp.exp(acum_e)
            wl = jnp.exp(alast_e - acum_e)
            xd = xs * delta_e
            xdb = xd.astype(BF16)
            cb_, bb_ = cm.astype(BF16), bm.astype(BF16)
            cbm = _nt(cb_, bb_)
            st32 = st_ref[0, c * (lc // SSD_CHUNK)]
            stb = st32.astype(BF16)
            dst = dstate[...]
            dstb = dst.astype(BF16)
            lhs, mixes, ys = [], [], []
            for hh in range(HEADS_PER_GROUP):
                col, row = acum[:, hh:hh + 1], acum_t[hh:hh + 1, :]
                lh = jnp.where(mask, jnp.exp(jnp.minimum(col - row, 0.0)), 0.0)
                mix = (cbm * lh).astype(BF16)
                lhs.append(lh)
                mixes.append(mix)
                ys.append(_nn(mix, xdb[:, hh * HEAD_P:(hh + 1) * HEAD_P]))
            y_in = jnp.concatenate(ys, axis=1)
            y_out = _nn(cb_, stb) * eacum
            y = y_in + y_out + xs * d_e
            sgz = _sigmoid(z)
            sz = z * sgz
            yg = y * sz
            rstd = lax.rsqrt(jnp.mean(yg * yg, axis=-1, keepdims=True) + RMS_EPS)
            nrm = yg * rstd
            dob_v = dob_ref[sl, :]
            dn = dob_v * wn
            dwn_ref[...] += _colsum(dob_v * nrm)
            dyg = rstd * (dn - nrm * jnp.mean(dn * nrm, axis=-1, keepdims=True))
            dy = dyg * sz
            dz_ref[sl, :] = (dyg * y * _dsilu(z, sgz)).astype(BF16)
            dyb = dy.astype(BF16)
            dxds = []
            dcb = jnp.zeros((lc, lc), F32)
            for hh in range(HEADS_PER_GROUP):
                hs = slice(hh * HEAD_P, (hh + 1) * HEAD_P)
                dy_h, x_h = dyb[:, hs], xdb[:, hs]
                dxds.append(_tn(mixes[hh], dy_h))
                dcb = dcb + _nt(dy_h, x_h) * lhs[hh]
            dcbb = dcb.astype(BF16)
            dye = (dy * eacum).astype(BF16)
            xw = (xd * wl).astype(BF16)
            dxd_in = jnp.concatenate(dxds, axis=1)
            dxd_out = wl * _nn(bb_, dstb)
            dxd = dxd_in + dxd_out
            dc_ref[sl, :] = (_nn(dcbb, bb_) + _nt(dye, stb)).astype(dc_ref.dtype)
            db_ref[sl, :] = (_tn(dcbb, cb_) + _nt(xw, dstb)).astype(db_ref.dtype)
            dstate[...] = dst * jnp.exp(alast_e) + _tn(cb_, dye)
            col_out = xd * dxd_out
            dac = _sel(_nt, dyb.astype(F32) * y_in - xdb.astype(F32) * dxd_in + dy * y_out - col_out, expand, 2)
            beyond = _colsum(col_out) + jnp.exp(alast_e) * _colsum(dst * st32)
            da = (_sel(_nn, dac, triu_f, 3, x_first=False) +
                  _sel(_nt, jnp.broadcast_to(beyond, (8, GROUP_W)), expand, 3)[0:1, :])
            ddelta = _sel(_nt, dxd * xs, expand, 2) - da * ea
            dalog_ref[0] += _colsum(da * a)
            ddtb = ddelta * _sigmoid(dtb)
            dbias_ref[0] += _colsum(ddtb)
            ddt_ref[sl, :] = ddtb.astype(BF16)
            ddsk_ref[0] += _sel(_nt, jnp.broadcast_to(_colsum(dy * xs), (8, GROUP_W)), expand, 3)[0:1, :]
            dxs_ref[sl, :] = (dxd * delta_e + dy * d_e).astype(dxs_ref.dtype)
            return c0

        lax.fori_loop(0, ncb, chunk, 0, unroll=min(CHUNK_UNROLL, ncb))

        @pl.when((pl.program_id(0) == N_GROUPS - 1) & (pl.program_id(1) == nb - 1))
        def _():
            xchg_wait()

    small = pl.BlockSpec((1, 1, N_STATE), lambda g, j: (g, 0, 0))
    wide = pl.BlockSpec((tb, GROUP_W), lambda g, j: (nb - 1 - j, g))
    narrow = pl.BlockSpec((tb, N_STATE), lambda g, j: (nb - 1 - j, g))
    hbm = pl.BlockSpec(memory_space=pl.ANY)
    return pl.pallas_call(
        body, name="ssd_bwd", grid=(N_GROUPS, nb),
        out_shape=[jax.ShapeDtypeStruct((t, B_INNER), BF16), jax.ShapeDtypeStruct((t, GROUP_W), BF16),
                   jax.ShapeDtypeStruct((t, GROUP_W), BF16), jax.ShapeDtypeStruct(dproj.shape, dproj.dtype),
                   jax.ShapeDtypeStruct((t, GROUP_W), BF16), jax.ShapeDtypeStruct((1, B_INNER), F32),
                   jax.ShapeDtypeStruct((N_GROUPS, 1, N_STATE), F32), jax.ShapeDtypeStruct((N_GROUPS, 1, N_STATE), F32),
                   jax.ShapeDtypeStruct((N_GROUPS, 1, N_STATE), F32), jax.ShapeDtypeStruct(part.shape, part.dtype)],
        in_specs=[wide,
                  pl.BlockSpec((tb, N_STATE), lambda g, j: (nb - 1 - j, B_BLOCK0 + g)),
                  pl.BlockSpec((tb, N_STATE), lambda g, j: (nb - 1 - j, C_BLOCK0 + g)),
                  pl.BlockSpec((tb, N_STATE), lambda g, j: (nb - 1 - j, DT_BLOCK0 + g)),
                  pl.BlockSpec((tb, GROUP_W), lambda g, j: (nb - 1 - j, Z_BLOCK0 + g)),
                  small, small, small, pl.BlockSpec((1, GROUP_W), lambda g, j: (0, g)),
                  pl.BlockSpec((N_STATE, GROUP_W), lambda g, j: (0, 0)),
                  pl.BlockSpec((tb, D), lambda g, j: (nb - 1 - j, 0)), pl.BlockSpec((GROUP_W, D), lambda g, j: (g, 0)),
                  pl.BlockSpec((1, nsaved, N_STATE, GROUP_W), lambda g, j: (g, nb - 1 - j, 0, 0)), hbm, hbm],
        out_specs=[wide, narrow, narrow,
                   pl.BlockSpec((None, tb, GROUP_W), lambda g, j: (Z_BLOCK0 // 2 + g // 2, nb - 1 - j, g % 2)),
                   narrow, pl.BlockSpec((1, GROUP_W), lambda g, j: (0, g)), small, small, small, hbm],
        input_output_aliases={14: 3},
        scratch_shapes=[pltpu.VMEM((N_STATE, GROUP_W), F32), pltpu.VMEM((tb, GROUP_W), F32)] + CHIP_SEMS,
        compiler_params=_cparams(("arbitrary", "arbitrary")),
    )(xc, xc, xc, proj, proj, alog4, bias4, dskip4, wnorm, expand, dyb, w_bb, states, part, dproj)


def lower_bound_fwd(hgrn_lb):
    def body(a_ref, o_ref):
        a0, a1 = a_ref[0:1, :], a_ref[1:2, :]
        m = jnp.maximum(a0, a1)
        e0, e1 = jnp.exp(a0 - m), jnp.exp(a1 - m)
        o_ref[...] = e0 / (e0 + e1)

    return pl.pallas_call(body, name="lower_bound_fwd", out_shape=jax.ShapeDtypeStruct((1, D), F32))(hgrn_lb)


def ada_weight_grad(c_all, dmod_cols):
    def body(c_ref, d_ref, o_ref):
        cval = c_ref[...]
        o_ref[...] = _tn(cval * _sigmoid(cval), d_ref[...], HI)

    return pl.pallas_call(body, name="ada_weight_grad",
                          out_shape=jax.ShapeDtypeStruct((D, dmod_cols.shape[1]), F32))(c_all, dmod_cols)


def reduce_small(gathered, hgrn_lb, dlb_off):
    n = gathered.shape[2]

    def body(g_ref, a_ref, o_ref, glb_ref):
        s = g_ref[0]
        for d in range(1, N_DEV):
            s = s + g_ref[d]
        o_ref[...] = s
        a0, a1 = a_ref[0:1, :], a_ref[1:2, :]
        m = jnp.maximum(a0, a1)
        e0, e1 = jnp.exp(a0 - m), jnp.exp(a1 - m)
        p0 = e0 / (e0 + e1)
        tq = s[:, dlb_off:dlb_off + D] * p0 * (1.0 - p0)
        glb_ref[0:1, :] = tq
        glb_ref[1:2, :] = -tq

    return pl.pallas_call(body, name="reduce_small",
                          out_shape=[jax.ShapeDtypeStruct((1, n), F32), jax.ShapeDtypeStruct((2, D), F32)])(gathered, hgrn_lb)


def _adam_math(w, g, m, v):
    m2 = ADAM_B1 * m + (1.0 - ADAM_B1) * g
    v2 = ADAM_B2 * v + (1.0 - ADAM_B2) * (g * g)
    m_hat = m2 / (1.0 - ADAM_B1 ** ADAM_STEP)
    v_hat = v2 / (1.0 - ADAM_B2 ** ADAM_STEP)
    delta = -ADAM_LR * (m_hat / (jnp.sqrt(v_hat) + ADAM_EPS) + ADAM_WD * w)
    return delta, m2, v2


def _row_tile(rows, mult=8, cap=128):
    for cand in range(cap - cap % mult, 0, -mult):
        if rows % cand == 0:
            return cand
    return rows


def sum_parts(parts, name):
    n, rows, cols = parts.shape
    tr = _row_tile(rows, 16, 1024)

    def body(p_ref, o_ref):
        s = p_ref[0].astype(F32)
        for d in range(1, n):
            s = s + p_ref[d].astype(F32)
        o_ref[...] = s

    return pl.pallas_call(
        body, name=name, grid=(rows // tr,),
        out_shape=jax.ShapeDtypeStruct((rows, cols), F32),
        in_specs=[pl.BlockSpec((n, tr, cols), lambda i: (0, i, 0))],
        out_specs=pl.BlockSpec((tr, cols), lambda i: (i, 0)),
        compiler_params=_cparams(("parallel",)),
    )(parts)


def sum_pair(a, b, name):
    rows, cols = a.shape
    tr = _row_tile(rows, 16, 1024)

    def body(a_ref, b_ref, o_ref):
        o_ref[...] = (a_ref[...].astype(F32) + b_ref[...].astype(F32)).astype(o_ref.dtype)

    blk = pl.BlockSpec((tr, cols), lambda i: (i, 0))
    return pl.pallas_call(
        body, name=name, grid=(rows // tr,),
        out_shape=jax.ShapeDtypeStruct((rows, cols), a.dtype),
        in_specs=[blk, blk], out_specs=blk,
        compiler_params=_cparams(("parallel",)),
    )(a, b)


def adamw(w, g, m, v, name):
    rows, cols = w.shape
    tr = _row_tile(rows, 8, 256)

    def body(w_ref, g_ref, m_ref, v_ref, d_ref, m2_ref, v2_ref):
        delta, m2, v2 = _adam_math(w_ref[...], g_ref[...], m_ref[...], v_ref[...])
        d_ref[...] = delta
        m2_ref[...] = m2
        v2_ref[...] = v2

    blk = pl.BlockSpec((tr, cols), lambda i: (i, 0))
    return pl.pallas_call(
        body, name=name, grid=(rows // tr,),
        out_shape=[jax.ShapeDtypeStruct((rows, cols), F32)] * 3,
        in_specs=[blk] * 4, out_specs=[blk] * 3,
        compiler_params=_cparams(("parallel",)),
    )(w, g, m, v)


def _pad128(n):
    return -(-n // 128) * 128


def _pack(arrays):
    offs, parts, off = [], [], 0
    for a in arrays:
        flat = a.reshape(1, -1)
        n = flat.shape[1]
        offs.append(off)
        parts.append(jnp.pad(flat, ((0, 0), (0, _pad128(n) - n))))
        off += _pad128(n)
    return jnp.concatenate(parts, axis=1), offs


def _unpack(vec, offs, shapes):
    out = []
    for off, shp in zip(offs, shapes):
        n = int(np.prod(shp))
        out.append(vec[0, off:off + n].reshape(shp))
    return out


IN_ROWS = IN_DIM // N_DEV
DT_ROW0 = 9216
DT_DEV, DT_LO = divmod(DT_ROW0, IN_ROWS)


GATE_SHIFT = D - 32


def _in_row_pieces(tile):
    pieces = []
    if tile == DT_COL_BLOCK:
        for g in range(N_GROUPS):
            o = DT_ROW0 + HEADS_PER_GROUP * g
            pieces.append((N_STATE * g, o // IN_ROWS, o % IN_ROWS, HEADS_PER_GROUP))
        return pieces
    r, end = tile * D, (tile + 1) * D
    while r < end:
        o = r if r < DT_ROW0 else r - GATE_SHIFT
        dev, loc = divmod(o, IN_ROWS)
        n = min(end - r, IN_ROWS - loc)
        pieces.append((r - tile * D, dev, loc, n))
        r += n
    return pieces


def assemble_w_in(g_all):
    ntile = N_PROJ // D

    def body(g_ref, o_ref):
        j = pl.program_id(0)
        for tile in range(ntile):
            @pl.when(j == tile)
            def _(tile=tile):
                if tile == DT_COL_BLOCK:
                    o_ref[...] = jnp.zeros_like(o_ref)
                for dst, dev, loc, n in _in_row_pieces(tile):
                    o_ref[pl.ds(dst, n), :] = g_ref[dev, pl.ds(loc, n), :]

    return pl.pallas_call(
        body, name="assemble_w_in", grid=(ntile,),
        out_shape=jax.ShapeDtypeStruct((N_PROJ, D), g_all.dtype),
        in_specs=[pl.BlockSpec(memory_space=pltpu.VMEM)],
        out_specs=pl.BlockSpec((D, D), lambda j: (j, 0)),
        compiler_params=_cparams(("arbitrary",)),
    )(g_all)


def _grad_in_blocks(g_t, core, slot):
    dt0 = DT_COL_BLOCK * D
    dt = g_t[dt0:dt0 + N_GROUPS * N_STATE].reshape(N_GROUPS, N_STATE, D)[:, :HEADS_PER_GROUP].reshape(32, D)
    with_dt = jnp.concatenate([g_t[DT_DEV * IN_ROWS:DT_ROW0], dt,
                               g_t[DT_ROW0 + 32 + GATE_SHIFT:(DT_DEV + 1) * IN_ROWS + GATE_SHIFT]], axis=0)
    blocks = []
    for q in range(N_CHIP):
        if 2 * q + 1 < DT_DEV:
            blk = lax.dynamic_slice_in_dim(g_t, IN_ROWS * (2 * q + core), IN_ROWS, axis=0)
        else:
            assert 2 * q == DT_DEV
            after = g_t[(DT_DEV + 1) * IN_ROWS + GATE_SHIFT:(DT_DEV + 2) * IN_ROWS + GATE_SHIFT]
            blk = jnp.where(core == 0, with_dt, after)
        blocks.append(jnp.pad(blk, ((0, slot - IN_ROWS), (0, 0))))
    return jnp.stack(blocks)


def kernel(x, c, w_ada, b_ada, w_in, hgrn_lb, hgrn_gnorm, ssm_conv_w, ssm_conv_b, ssm_dt_bias, ssm_a_log, ssm_d, ssm_norm, w_branch_a, w_branch_b, w_o, ln1_g, ln1_b, w_ffn_gate, w_ffn_up, w_ffn_down, ln2_g, ln2_b, loss_target, m_w_ada, m_b_ada, m_w_in, m_hgrn_lb, m_hgrn_gnorm, m_ssm_conv_w, m_ssm_conv_b, m_ssm_dt_bias, m_ssm_a_log, m_ssm_d, m_ssm_norm, m_w_branch_a, m_w_branch_b, m_w_o, m_ln1_g, m_ln1_b, m_w_ffn_gate, m_w_ffn_up, m_w_ffn_down, m_ln2_g, m_ln2_b, v_w_ada, v_b_ada, v_w_in, v_hgrn_lb, v_hgrn_gnorm, v_ssm_conv_w, v_ssm_conv_b, v_ssm_dt_bias, v_ssm_a_log, v_ssm_d, v_ssm_norm, v_w_branch_a, v_w_branch_b, v_w_o, v_ln1_g, v_ln1_b, v_w_ffn_gate, v_w_ffn_up, v_w_ffn_down, v_ln2_g, v_ln2_b):
    me = 4 * lax.axis_index("x") + 2 * lax.axis_index("y") + lax.axis_index("c")
    xt = x[0]
    tgt = loss_target[0]
    t = xt.shape[0]
    ada_cols = w_ada.shape[2]
    conv_cols = ssm_conv_w.shape[2]

    small_in, _ = _pack([c, ssm_conv_w[0]])
    small_all = allgather_vmem(small_in, "allgather_small_inputs")
    c_all = small_all[:, 0, :D]
    conv_w = small_all[:, 0, D:D + CONV_TAPS * conv_cols].reshape(N_DEV, CONV_TAPS, conv_cols)
    conv_w = conv_w.transpose(1, 0, 2).reshape(CONV_TAPS, CONV_DIM)
    mod = ada_modulation(c_all, w_ada[0], b_ada.reshape(N_DEV, 1, ada_cols))
    mod6 = mod.reshape(6, D)

    shards = [w_in[0].T, w_ffn_gate[0].T, w_ffn_up[0].T, w_ffn_down[0], w_branch_a[0], w_branch_b[0], w_o[0]]
    shard_rows = [s.shape[0] for s in shards]
    slot_rows = [-(-r // 32) * 32 for r in shard_rows]
    row_offs = [sum(slot_rows[:i]) for i in range(len(shards))]
    padded = [jnp.pad(s.astype(BF16), ((0, p - r), (0, 0))) for s, r, p in zip(shards, shard_rows, slot_rows)]
    w_in_t = assemble_w_in(allgather_hbm(padded[0], "allgather_w_in"))

    lb = lower_bound_fwd(hgrn_lb)
    u1 = ln_modulate(xt, mod6, 0, 1, "ln_modulate_1")
    proj, g_rest = mm_nt_gather(u1, w_in_t, F32, jnp.concatenate(padded[1:], axis=0), "mm_in_proj")
    assert shard_rows[1:4] == slot_rows[1:4] == [FF_SHARD_ROWS] * 3
    gate_slot, up_slot, dn_slot = 0, 1, 2
    g_ba, g_bb, g_o = (g_rest[:, o - slot_rows[0]:o - slot_rows[0] + r] for o, r in zip(row_offs[4:], shard_rows[4:]))
    w_ba = g_ba.reshape(D, D)
    w_bb = g_bb.reshape(B_INNER, D)
    w_oo = g_o.reshape(D, D)
    o_a, o_raw, st_a = hgrn_fwd(proj, lb, hgrn_gnorm)
    xc, conv_slope = conv_fwd(proj, conv_w, ssm_conv_b)
    pad3 = ((0, 0), (0, 0), (0, N_STATE - HEADS_PER_GROUP))
    alog4 = jnp.pad(ssm_a_log.reshape(N_GROUPS, 1, HEADS_PER_GROUP), pad3)
    bias4 = jnp.pad(ssm_dt_bias.reshape(N_GROUPS, 1, HEADS_PER_GROUP), pad3)
    dskip4 = jnp.pad(ssm_d.reshape(N_GROUPS, 1, HEADS_PER_GROUP), pad3)
    expand = _head_expand()
    o_b, st_b = ssd_fwd(proj, xc, alog4, bias4, dskip4, ssm_norm, expand)
    ya, yb, merged, h1, x1, u2 = mixer_tail(o_a, o_b, w_ba, w_bb, proj, w_oo, xt, mod6, ln1_g, ln1_b)
    gu, act = ffn_in_act(u2, g_rest, gate_slot, up_slot)

    dh2, dx1_part, acc4 = ffn_tail_loss_bwd(x1, act, g_rest, dn_slot, mod6, ln2_g, ln2_b, tgt)
    g_dn = mm_tn(act, dh2, "mm_grad_ffn_down")
    dgu = ffn_act_bwd(dh2, g_rest, dn_slot, gu)
    g_gu_t = mm_tn(dgu, u2, "mm_grad_ffn_in")
    dh1, dx_part, acc2 = mixer_tail_bwd(x1, dgu, g_rest, gate_slot, up_slot, mod6, dx1_part, xt, h1, ln1_g, ln1_b)
    g_o = mm_tn(merged, dh1, "mm_grad_out_proj")
    dya, dyb, dproj = merge_gates_bwd(dh1, w_oo, ya, yb, proj)
    g_ba_full = mm_tn(o_a, dya, "mm_grad_branch_a")
    g_bb_full = mm_tn(o_b, dyb, "mm_grad_branch_b")
    my_core = lax.axis_index("c")

    def by_core(blocks, rows, slots):
        contrib = jnp.concatenate([jnp.pad(b.reshape(N_DEV, -1, D), ((0, 0), (0, p - r), (0, 0)))
                                   for b, r, p in zip(blocks, rows, slots)], axis=1)
        split = contrib.reshape(N_CHIP, 2, contrib.shape[1], D).transpose(1, 0, 2, 3)
        return (lax.dynamic_index_in_dim(split, my_core, 0, keepdims=False),
                lax.dynamic_index_in_dim(split, 1 - my_core, 0, keepdims=False))

    keep_e, give_e = by_core([g_gu_t[:D_FF], g_gu_t[D_FF:], g_dn, g_ba_full, g_bb_full, g_o],
                             shard_rows[1:], slot_rows[1:])
    dproj, dlb, dgn, got_e = hgrn_bwd(proj, lb, hgrn_gnorm, o_raw, dya, w_ba, st_a, give_e, dproj)
    chip_e = sum_pair(keep_e.reshape(-1, D), got_e.reshape(-1, D), "sum_grads_rest_chip").reshape(keep_e.shape)
    dxs, dbm, dcm, dproj, ddt, dwn, dalog, dbias, ddsk, parts_e = ssd_bwd(proj, xc, alog4, bias4, dskip4, ssm_norm,
                                                                          expand, dyb, w_bb, st_b, chip_e, dproj)
    dxc = jnp.concatenate([dxs, dbm, dcm], axis=1)
    dproj, dcw, dcb = conv_bwd(proj, dxc, conv_slope, conv_w, dproj)
    dproj = dt_fill(ddt, dproj)
    g_in_t = mm_tn(dproj, u1, "mm_grad_in_proj")
    keep_l = _grad_in_blocks(g_in_t, my_core, slot_rows[0])
    give_l = _grad_in_blocks(g_in_t, 1 - my_core, slot_rows[0])
    got_l = exchange_sibling(give_l, "exchange_grad_in_sibling")
    chip_l = sum_pair(keep_l.reshape(-1, D), got_l.reshape(-1, D), "sum_grad_in_chip").reshape(keep_l.shape)
    du1, parts_l = mm_nn_exchange(dproj, w_in_t, F32, chip_l, "mm_du1")
    dx, acc1 = ln_modulate_bwd(xt, du1, mod6, 1, dx_part, "ln_modulate_1_bwd")
    gw_in = sum_parts(parts_l, "sum_grad_in")[:shard_rows[0]].T
    g_rows = sum_parts(parts_e, "sum_grads_rest")
    gw_fg, gw_fu, gw_fd, gw_ba, gw_bb, gw_o = (g_rows[o - slot_rows[0]:o - slot_rows[0] + r]
                                               for o, r in zip(row_offs[1:], shard_rows[1:]))
    gw_fg, gw_fu = gw_fg.T, gw_fu.T

    dmod = jnp.concatenate([acc1[1:2], acc1[0:1], acc2[2:3], acc2[1:2], acc2[0:1], acc4[0:1]], axis=1)
    small_fields = [dmod, acc4[3:4, :128], dlb, dgn, dcw[:CONV_TAPS], dcb, dbias, dalog, ddsk, dwn,
                    acc2[3:4], acc2[4:5], acc4[1:2], acc4[2:3]]
    small_out, offs = _pack(small_fields)
    small_sum_in = allgather_vmem(small_out, "allgather_small_grads")
    gsum, g_lb = reduce_small(small_sum_in, hgrn_lb, offs[2])
    (g_bada, loss_row, _, g_gn, g_cw_full, g_cb, g_bias4, g_alog4, g_dsk4, g_wn, g_l1g, g_l1b, g_l2g, g_l2b) = _unpack(
        gsum, offs, [(1, 6 * D), (1, 128), (1, D), (1, HK), (CONV_TAPS, CONV_DIM), (1, CONV_DIM),
                     (N_GROUPS, N_STATE), (N_GROUPS, N_STATE), (N_GROUPS, N_STATE), (1, B_INNER),
                     (1, D), (1, D), (1, D), (1, D)])
    loss = loss_row[0, 0]
    g_cw = lax.dynamic_slice(g_cw_full, (0, me * conv_cols), (CONV_TAPS, conv_cols))[None]
    g_dtb = g_bias4[:, :HEADS_PER_GROUP].reshape(1, 32)
    g_alog = g_alog4[:, :HEADS_PER_GROUP].reshape(1, 32)
    g_dsk = g_dsk4[:, :HEADS_PER_GROUP].reshape(1, 32)

    dmod_all = small_sum_in[:, 0, offs[0]:offs[0] + 6 * D]
    dmod_cols = lax.dynamic_slice(dmod_all, (0, me * ada_cols), (N_DEV, ada_cols))
    gw_ada = ada_weight_grad(c_all, dmod_cols)

    big = [("ada", w_ada[0], gw_ada, m_w_ada[0], v_w_ada[0]), ("in", w_in[0], gw_in, m_w_in[0], v_w_in[0]),
           ("branch_a", w_branch_a[0], gw_ba, m_w_branch_a[0], v_w_branch_a[0]),
           ("branch_b", w_branch_b[0], gw_bb, m_w_branch_b[0], v_w_branch_b[0]),
           ("o", w_o[0], gw_o, m_w_o[0], v_w_o[0]),
           ("ffn_gate", w_ffn_gate[0], gw_fg, m_w_ffn_gate[0], v_w_ffn_gate[0]),
           ("ffn_up", w_ffn_up[0], gw_fu, m_w_ffn_up[0], v_w_ffn_up[0]),
           ("ffn_down", w_ffn_down[0], gw_fd, m_w_ffn_down[0], v_w_ffn_down[0])]
    big_out = {}
    for nm, w_, g_, m_, v_ in big:
        d_, m2_, v2_ = adamw(w_, g_, m_, v_, "adamw_" + nm)
        big_out[nm] = (g_[None], d_[None], m2_[None], v2_[None])

    small_w = [b_ada, hgrn_lb, hgrn_gnorm, ssm_conv_w, ssm_conv_b, ssm_dt_bias, ssm_a_log, ssm_d, ssm_norm,
               ln1_g, ln1_b, ln2_g, ln2_b]
    small_g = [g_bada, g_lb, g_gn, g_cw, g_cb, g_dtb, g_alog, g_dsk, g_wn, g_l1g, g_l1b, g_l2g, g_l2b]
    small_m = [m_b_ada, m_hgrn_lb, m_hgrn_gnorm, m_ssm_conv_w, m_ssm_conv_b, m_ssm_dt_bias, m_ssm_a_log, m_ssm_d,
               m_ssm_norm, m_ln1_g, m_ln1_b, m_ln2_g, m_ln2_b]
    small_v = [v_b_ada, v_hgrn_lb, v_hgrn_gnorm, v_ssm_conv_w, v_ssm_conv_b, v_ssm_dt_bias, v_ssm_a_log, v_ssm_d,
               v_ssm_norm, v_ln1_g, v_ln1_b, v_ln2_g, v_ln2_b]
    shapes = [a.shape for a in small_w]
    small_g = [g_.reshape(s) for g_, s in zip(small_g, shapes)]
    pw, poffs = _pack(small_w)
    pg, _ = _pack(small_g)
    pm, _ = _pack(small_m)
    pv, _ = _pack(small_v)
    pd, pm2, pv2 = adamw(pw, pg, pm, pv, "adamw_small")
    s_d, s_m, s_v = (_unpack(p, poffs, shapes) for p in (pd, pm2, pv2))
    (sn_bada, sn_lb, sn_gn, sn_cw, sn_cb, sn_dtb, sn_alog, sn_dsk, sn_wn, sn_l1g, sn_l1b, sn_l2g, sn_l2b) = range(13)

    def order(kind):
        sm = [small_g, s_d, s_m, s_v][kind]
        bg = lambda nm: big_out[nm][kind]
        return [bg("ada"), sm[sn_bada], bg("in"), sm[sn_lb], sm[sn_gn], sm[sn_cw], sm[sn_cb], sm[sn_dtb], sm[sn_alog],
                sm[sn_dsk], sm[sn_wn], bg("branch_a"), bg("branch_b"), bg("o"), sm[sn_l1g], sm[sn_l1b],
                bg("ffn_gate"), bg("ffn_up"), bg("ffn_down"), sm[sn_l2g], sm[sn_l2b]]

    return (loss, dx[None], *order(0), *order(1), *order(2), *order(3))
```

```python
import numpy as np
import jax
import jax.numpy as jnp
from jax import lax
from jax.experimental import pallas as pl
from jax.experimental.pallas import tpu as pltpu

F32 = jnp.float32
BF16 = jnp.bfloat16
HI = lax.Precision.HIGHEST

N_DEV = 8
D = 1024
N_HEADS_A = 8
HK = 128
CHUNK = 64
SSD_CHUNK = 128
SSD_CHUNK_BWD = 256
N_GROUPS = 4
HEADS_PER_GROUP = 8
HEAD_P = 64
N_STATE = 128
GROUP_W = HEADS_PER_GROUP * HEAD_P
B_INNER = 2048
CONV_DIM = 3072
D_FF = 2816
IN_DIM = 11296
N_PROJ = 12288
ALPHA = 2.0 ** 0.25
LN_EPS = 1e-5
RMS_EPS = 1e-6
Q_SCALE = 128 ** -0.5
EXP_CLIP = 80.0
ADAM_LR, ADAM_B1, ADAM_B2, ADAM_EPS, ADAM_WD, ADAM_STEP = 0.001, 0.9, 0.999, 1e-8, 0.01, 10
VMEM_LIMIT = 48 * 1024 * 1024
TOKEN_BLOCK = 1024
ROW_TILE = 512
WIDE_ROW_TILE = 1024
MM_ROW_TILE = 1024
MM_TOKEN_TILE = 4096
MM_K_TILE = 3072
MM_COL_TILE = 1408
HGRN_HEADS_PER_STEP = 4
CHUNK_UNROLL = 8
MESH_ID = pl.DeviceIdType.MESH

NT_DIMS = (((1,), (1,)), ((), ()))
TN_DIMS = (((0,), (0,)), ((), ()))


def _cparams(sem=None):
    return pltpu.CompilerParams(dimension_semantics=sem, vmem_limit_bytes=VMEM_LIMIT)


def _sigmoid(x):
    return 1.0 / (1.0 + jnp.exp(-x))


def _dsilu(x, s):
    return s * (1.0 + x * (1.0 - s))


def _nt(a, b, precision=None):
    return lax.dot_general(a, b, NT_DIMS, precision=precision, preferred_element_type=F32)


def _tn(a, b, precision=None):
    return lax.dot_general(a, b, TN_DIMS, precision=precision, preferred_element_type=F32)


def _nn(a, b, precision=None):
    return jnp.dot(a, b, precision=precision, preferred_element_type=F32)


def _split(x, pieces):
    out = []
    for i in range(pieces):
        p = x.astype(BF16)
        out.append(p)
        if i + 1 < pieces:
            x = x - p.astype(F32)
    return out


def _sel(dot, x, sel01, pieces, x_first=True):
    acc = None
    for p in _split(x, pieces):
        term = dot(p, sel01) if x_first else dot(sel01, p)
        acc = term if acc is None else acc + term
    return acc


def _ln(x):
    mu = jnp.mean(x, axis=-1, keepdims=True)
    xc = x - mu
    rstd = lax.rsqrt(jnp.mean(xc * xc, axis=-1, keepdims=True) + LN_EPS)
    return xc * rstd, rstd


def _ln_bwd(dxh, xh, rstd):
    return rstd * (dxh - jnp.mean(dxh, axis=-1, keepdims=True) - xh * jnp.mean(dxh * xh, axis=-1, keepdims=True))


def _colsum(x):
    return jnp.sum(x, axis=0, keepdims=True)


def _tri(n, upper=False):
    r = lax.broadcasted_iota(jnp.int32, (n, n), 0)
    c = lax.broadcasted_iota(jnp.int32, (n, n), 1)
    return (c >= r) if upper else (r >= c)


def _my_pos():
    return lax.axis_index("x"), lax.axis_index("y"), lax.axis_index("c")


def _peer(pos, k):
    x, y, c = pos
    return (x ^ ((k >> 2) & 1), y ^ ((k >> 1) & 1), c ^ (k & 1))


def _flat(pos):
    return 4 * pos[0] + 2 * pos[1] + pos[2]


def allgather_vmem(v, name):
    n = v.shape[1]

    def body(v_ref, o_ref, send_sems, recv_sems, local_sem):
        me = _my_pos()
        mine = pltpu.make_async_copy(v_ref, o_ref.at[_flat(me)], local_sem)
        mine.start()
        sends = []
        for k in range(1, N_DEV):
            peer = _peer(me, k)
            cp = pltpu.make_async_remote_copy(v_ref, o_ref.at[_flat(me)], send_sems.at[k - 1], recv_sems.at[k - 1],
                                              device_id=peer, device_id_type=MESH_ID)
            cp.start()
            sends.append(cp)
        for k in range(1, N_DEV):
            peer = _peer(me, k)
            pltpu.make_async_remote_copy(v_ref, o_ref.at[_flat(peer)], send_sems.at[k - 1], recv_sems.at[k - 1],
                                         device_id=peer, device_id_type=MESH_ID).wait_recv()
        for cp in sends:
            cp.wait_send()
        mine.wait()

    return pl.pallas_call(
        body, name=name,
        out_shape=jax.ShapeDtypeStruct((N_DEV, 1, n), F32),
        in_specs=[pl.BlockSpec(memory_space=pltpu.VMEM)],
        out_specs=pl.BlockSpec(memory_space=pltpu.VMEM),
        scratch_shapes=[pltpu.SemaphoreType.DMA((N_DEV - 1,)), pltpu.SemaphoreType.DMA((N_DEV - 1,)),
                        pltpu.SemaphoreType.DMA],
        compiler_params=_cparams(),
    )(v)


def ada_modulation(c_all, w_ada_s, b_ada_r):
    ncol = w_ada_s.shape[1]

    def body(c_ref, w_ref, b_ref, o_ref, part_ref, send_sems, recv_sems):
        me = _my_pos()
        cval = c_ref[...]
        cond = cval * _sigmoid(cval)
        part = _nn(cond, w_ref[...], HI)
        for r in range(N_DEV):
            part_ref[r] = part[r:r + 1, :]
        sends = []
        for k in range(1, N_DEV):
            peer = _peer(me, k)
            cp = pltpu.make_async_remote_copy(part_ref.at[_flat(peer)], o_ref.at[_flat(me)], send_sems.at[k - 1],
                                              recv_sems.at[k - 1], device_id=peer, device_id_type=MESH_ID)
            cp.start()
            sends.append(cp)
        o_ref[_flat(me)] = part_ref[_flat(me)]
        for k in range(1, N_DEV):
            peer = _peer(me, k)
            pltpu.make_async_remote_copy(part_ref.at[_flat(peer)], o_ref.at[_flat(peer)], send_sems.at[k - 1],
                                         recv_sems.at[k - 1], device_id=peer, device_id_type=MESH_ID).wait_recv()
        for cp in sends:
            cp.wait_send()
        o_ref[...] = o_ref[...] + b_ref[...]

    return pl.pallas_call(
        body, name="ada_modulation",
        out_shape=jax.ShapeDtypeStruct((N_DEV, 1, ncol), F32),
        in_specs=[pl.BlockSpec(memory_space=pltpu.VMEM)] * 3,
        out_specs=pl.BlockSpec(memory_space=pltpu.VMEM),
        scratch_shapes=[pltpu.VMEM((N_DEV, 1, ncol), F32), pltpu.SemaphoreType.DMA((N_DEV - 1,)),
                        pltpu.SemaphoreType.DMA((N_DEV - 1,))],
        compiler_params=_cparams(),
    )(c_all, w_ada_s, b_ada_r)


def allgather_hbm(shard, name):
    def body(x_ref, out_ref, send_sems, recv_sems, local_sem):
        x, y, c = _my_pos()
        me, sibling = (x, y, c), (x, y, 1 - c)
        chips = [(1 - x, y), (x, 1 - y), (1 - x, 1 - y)]

        def slot(pos):
            return out_ref.at[_flat(pos)]

        def copy(k, block, to, src=None):
            return pltpu.make_async_remote_copy(slot(block) if src is None else src, slot(block), send_sems.at[k],
                                                recv_sems.at[k], device_id=to, device_id_type=MESH_ID)

        mine = pltpu.make_async_copy(x_ref, slot(me), local_sem)
        mine.start()
        first = [copy(0, me, sibling, src=x_ref)]
        first += [copy(1 + j, me, (*chip, c), src=x_ref) for j, chip in enumerate(chips)]
        for cp in first:
            cp.start()
        passed = [copy(4 + j, (*chip, c), sibling) for j, chip in enumerate(chips)]
        for j, chip in enumerate(chips):
            copy(1 + j, (*chip, c), me).wait_recv()
            passed[j].start()
        copy(0, sibling, me).wait_recv()
        for j, chip in enumerate(chips):
            copy(4 + j, (*chip, 1 - c), me).wait_recv()
        for cp in first + passed:
            cp.wait_send()
        mine.wait()

    return pl.pallas_call(
        body, name=name,
        out_shape=jax.ShapeDtypeStruct((N_DEV,) + shard.shape, shard.dtype),
        in_specs=[pl.BlockSpec(memory_space=pl.ANY)],
        out_specs=pl.BlockSpec(memory_space=pl.ANY),
        scratch_shapes=[pltpu.SemaphoreType.DMA((N_DEV - 1,)), pltpu.SemaphoreType.DMA((N_DEV - 1,)),
                        pltpu.SemaphoreType.DMA],
        compiler_params=_cparams(),
    )(shard)


N_CHIP = N_DEV // 2
SIBLING_SEMS = [pltpu.SemaphoreType.DMA, pltpu.SemaphoreType.DMA]
CHIP_SEMS = [pltpu.SemaphoreType.DMA((N_CHIP - 1,)), pltpu.SemaphoreType.DMA((N_CHIP - 1,)), pltpu.SemaphoreType.DMA]


def _sibling_exchange(s_ref, o_ref, send_sem, recv_sem):
    x, y, c = _my_pos()
    cp = pltpu.make_async_remote_copy(s_ref, o_ref, send_sem, recv_sem, device_id=(x, y, 1 - c), device_id_type=MESH_ID)
    return cp.start, cp.wait


def _chip_exchange(p_ref, o_ref, send_sems, recv_sems, local_sem):
    x, y, c = _my_pos()
    my_chip = 2 * x + y
    mine = pltpu.make_async_copy(p_ref.at[my_chip], o_ref.at[my_chip], local_sem)
    peers = [(x ^ (k >> 1), y ^ (k & 1)) for k in range(1, N_CHIP)]
    sends = [pltpu.make_async_remote_copy(p_ref.at[2 * px + py], o_ref.at[my_chip], send_sems.at[k], recv_sems.at[k],
                                          device_id=(px, py, c), device_id_type=MESH_ID)
             for k, (px, py) in enumerate(peers)]
    recvs = [pltpu.make_async_remote_copy(p_ref.at[2 * px + py], o_ref.at[2 * px + py], send_sems.at[k], recv_sems.at[k],
                                          device_id=(px, py, c), device_id_type=MESH_ID)
             for k, (px, py) in enumerate(peers)]

    def start():
        mine.start()
        for cp in sends:
            cp.start()

    def wait():
        for cp in recvs:
            cp.wait_recv()
        for cp in sends:
            cp.wait_send()
        mine.wait()

    return start, wait


def exchange_sibling(send, name):
    def body(s_ref, o_ref, send_sem, recv_sem):
        start, wait = _sibling_exchange(s_ref, o_ref, send_sem, recv_sem)
        start()
        wait()

    return pl.pallas_call(
        body, name=name,
        out_shape=jax.ShapeDtypeStruct(send.shape, send.dtype),
        in_specs=[pl.BlockSpec(memory_space=pl.ANY)],
        out_specs=pl.BlockSpec(memory_space=pl.ANY),
        scratch_shapes=SIBLING_SEMS,
        compiler_params=_cparams(),
    )(send)


LANES = 128


def _k_tile(kdim, unit=LANES):
    for cand in range(MM_K_TILE - MM_K_TILE % unit, 0, -unit):
        if kdim % cand == 0:
            return cand
    return kdim


def _lane_tile(n, cap):
    for cand in range(cap - cap % LANES, 0, -LANES):
        if n % cand == 0:
            return cand
    return n


def _m_tile(m, kdim):
    return min(MM_ROW_TILE if kdim > D else 2 * MM_ROW_TILE, m)


def mm_nn_exchange(a, b, out_dtype, part, name):
    kblocks, m, kb = a.shape
    kdim = kblocks * kb
    n = b.shape[1]
    tm, tn, tk = min(MM_ROW_TILE, m), _lane_tile(n, MM_COL_TILE), _k_tile(kdim)
    gn, gm, nk = n // tn, m // tm, kdim // tk
    per_step = tk // kb

    def body(a_ref, b_ref, part_ref, o_ref, parts_ref, acc_ref, send_sems, recv_sems, local_sem):
        j, i, k = pl.program_id(0), pl.program_id(1), pl.program_id(2)
        xchg_start, xchg_wait = _chip_exchange(part_ref, parts_ref, send_sems, recv_sems, local_sem)

        @pl.when((j == 0) & (i == 0) & (k == 0))
        def _():
            xchg_start()

        p = _nn(a_ref[0], b_ref[0:kb, :])
        for c in range(1, per_step):
            p = p + _nn(a_ref[c], b_ref[c * kb:(c + 1) * kb, :])

        @pl.when(k == 0)
        def _():
            acc_ref[...] = p

        @pl.when(k > 0)
        def _():
            acc_ref[...] += p

        @pl.when(k == nk - 1)
        def _():
            o_ref[...] = acc_ref[...].astype(o_ref.dtype)

        @pl.when((j == gn - 1) & (i == gm - 1) & (k == nk - 1))
        def _():
            xchg_wait()

    hbm = pl.BlockSpec(memory_space=pl.ANY)
    return pl.pallas_call(
        body, name=name, grid=(gn, gm, nk),
        out_shape=[jax.ShapeDtypeStruct((m, n), out_dtype), jax.ShapeDtypeStruct(part.shape, part.dtype)],
        in_specs=[pl.BlockSpec((per_step, tm, kb), lambda j, i, k: (k, i, 0)),
                  pl.BlockSpec((tk, tn), lambda j, i, k: (k, j)), hbm],
        out_specs=[pl.BlockSpec((tm, tn), lambda j, i, k: (i, j)), hbm],
        scratch_shapes=[pltpu.VMEM((tm, tn), F32)] + CHIP_SEMS,
        compiler_params=_cparams(("arbitrary", "arbitrary", "arbitrary")),
    )(a, b, part)


def mm_nt_gather(a, b, out_dtype, shard, name):
    m, kdim = a.shape
    n = b.shape[0]
    tm, tn = _m_tile(m, kdim), 1024
    assert kdim == 1024
    gj = m // tm
    nsteps = (n // tn) * gj
    forward_step = max(nsteps - 2, 0)

    def body(a_ref, b_ref, x_ref, o_ref, g_ref, send_sems, recv_sems, local_sem):
        step = pl.program_id(0) * gj + pl.program_id(1)
        x, y, c = _my_pos()
        me, sibling = (x, y, c), (x, y, 1 - c)
        chips = [(1 - x, y), (x, 1 - y), (1 - x, 1 - y)]

        def slot(pos):
            return g_ref.at[_flat(pos)]

        def copy(k, block, to, src=None):
            return pltpu.make_async_remote_copy(slot(block) if src is None else src, slot(block), send_sems.at[k],
                                                recv_sems.at[k], device_id=to, device_id_type=MESH_ID)

        mine = pltpu.make_async_copy(x_ref, slot(me), local_sem)
        first = [copy(0, me, sibling, src=x_ref)]
        first += [copy(1 + j, me, (*chip, c), src=x_ref) for j, chip in enumerate(chips)]
        passed = [copy(4 + j, (*chip, c), sibling) for j, chip in enumerate(chips)]

        @pl.when(step == 0)
        def _():
            mine.start()
            for cp in first:
                cp.start()

        rows = pl.ds(pl.multiple_of(pl.program_id(1) * tm, tm), tm)
        o_ref[...] = _nt(a_ref[rows, :], b_ref[...]).astype(o_ref.dtype)

        @pl.when(step == forward_step)
        def _():
            for j, chip in enumerate(chips):
                copy(1 + j, (*chip, c), me).wait_recv()
                passed[j].start()

        @pl.when(step == nsteps - 1)
        def _():
            copy(0, sibling, me).wait_recv()
            for j, chip in enumerate(chips):
                copy(4 + j, (*chip, 1 - c), me).wait_recv()
            for cp in first + passed:
                cp.wait_send()
            mine.wait()

    return pl.pallas_call(
        body, name=name, grid=(n // tn, gj),
        out_shape=[jax.ShapeDtypeStruct((m, n), out_dtype), jax.ShapeDtypeStruct((N_DEV,) + shard.shape, shard.dtype)],
        in_specs=[pl.BlockSpec(memory_space=pltpu.VMEM), pl.BlockSpec((tn, kdim), lambda j, i: (j, 0)),
                  pl.BlockSpec(memory_space=pl.ANY)],
        out_specs=[pl.BlockSpec((tm, tn), lambda j, i: (i, j)), pl.BlockSpec(memory_space=pl.ANY)],
        scratch_shapes=[pltpu.SemaphoreType.DMA((N_DEV - 1,)), pltpu.SemaphoreType.DMA((N_DEV - 1,)),
                        pltpu.SemaphoreType.DMA],
        compiler_params=_cparams(("arbitrary", "arbitrary")),
    )(a, b, shard)


def mm_tn(a, b, name):
    tt, tn = min(MM_TOKEN_TILE, b.shape[0]), _lane_tile(b.shape[1], MM_COL_TILE)
    tka = _lane_tile(a.shape[-1], 1024)
    if a.ndim == 3:
        t, ka = a.shape[1], a.shape[0] * a.shape[2]
        per = a.shape[2] // tka
        a_spec = pl.BlockSpec((None, tt, tka), lambda i, j, s: (i // per, s, i % per))
    else:
        t, ka = a.shape
        a_spec = pl.BlockSpec((tt, tka), lambda i, j, s: (s, i))
    n = b.shape[1]
    nt = t // tt

    def body(a_ref, b_ref, o_ref, *acc):
        p = _tn(a_ref[...], b_ref[...])
        if nt == 1:
            o_ref[...] = p.astype(o_ref.dtype)
        else:
            acc_ref, s = acc[0], pl.program_id(2)

            @pl.when(s == 0)
            def _():
                acc_ref[...] = p

            @pl.when(s > 0)
            def _():
                acc_ref[...] += p

            @pl.when(s == nt - 1)
            def _():
                o_ref[...] = acc_ref[...].astype(o_ref.dtype)

    return pl.pallas_call(
        body, name=name, grid=(ka // tka, n // tn, nt),
        out_shape=jax.ShapeDtypeStruct((ka, n), BF16),
        in_specs=[a_spec, pl.BlockSpec((tt, tn), lambda i, j, s: (s, j))],
        out_specs=pl.BlockSpec((tka, tn), lambda i, j, s: (i, j)),
        scratch_shapes=[] if nt == 1 else [pltpu.VMEM((tka, tn), F32)],
        compiler_params=_cparams(("parallel", "parallel", "arbitrary")),
    )(a, b)


def _tile(t, cap):
    return min(cap, t)


def ln_modulate(x, mod6, shift_row, scale_row, name):
    t = x.shape[0]
    tm = _tile(t, WIDE_ROW_TILE)

    def body(x_ref, mod_ref, o_ref):
        xh, _ = _ln(x_ref[...])
        sc = mod_ref[scale_row:scale_row + 1, :]
        sh = mod_ref[shift_row:shift_row + 1, :]
        o_ref[...] = (xh * (1.0 + sc) + sh).astype(BF16)

    return pl.pallas_call(
        body, name=name, grid=(t // tm,),
        out_shape=jax.ShapeDtypeStruct((t, D), BF16),
        in_specs=[pl.BlockSpec((tm, D), lambda i: (i, 0)), pl.BlockSpec((6, D), lambda i: (0, 0))],
        out_specs=pl.BlockSpec((tm, D), lambda i: (i, 0)),
        compiler_params=_cparams(("parallel",)),
    )(x, mod6)


FF_SHARD_ROWS = D_FF // N_DEV


def _ffn_weight_spec(ndev, slot, index_map):
    return pl.BlockSpec((ndev, FF_SHARD_ROWS, D), lambda *ids: (index_map(*ids), slot, 0))


def ffn_tail_loss_bwd(x1, act, gathered, dn_slot, mod6, ln_g, ln_b, target):
    t = x1.shape[0]
    tm = _tile(t, ROW_TILE)
    kdim = act.shape[1]

    def body(x_ref, a_ref, w_ref, mod_ref, g_ref, b_ref, c_ref, dh_ref, dx_ref, acc_ref):
        @pl.when(pl.program_id(0) == 0)
        def _():
            acc_ref[...] = jnp.zeros_like(acc_ref)

        hv = _nn(a_ref[...], w_ref[...].reshape(kdim, D))
        gate = mod_ref[5:6, :]
        rh, rstd = _ln(ALPHA * x_ref[...] + gate * hv)
        lng = g_ref[...]
        diff = rh * lng + b_ref[...] - c_ref[...]
        dxo = diff * (1.0 / D)
        lsum = jnp.sum(_colsum(diff * diff), axis=-1, keepdims=True) * (0.5 / D)
        acc_ref[3:4, :] += jnp.broadcast_to(lsum, (1, D))
        acc_ref[1:2, :] += _colsum(dxo * rh)
        acc_ref[2:3, :] += _colsum(dxo)
        dr = _ln_bwd(dxo * lng, rh, rstd)
        acc_ref[0:1, :] += _colsum(dr * hv)
        dh_ref[...] = (gate * dr).astype(BF16)
        dx_ref[...] = ALPHA * dr

    row = pl.BlockSpec((tm, D), lambda i: (i, 0))
    vec = pl.BlockSpec((1, D), lambda i: (0, 0))
    return pl.pallas_call(
        body, name="ffn_tail_loss_bwd", grid=(t // tm,),
        out_shape=[jax.ShapeDtypeStruct((t, D), BF16), jax.ShapeDtypeStruct((t, D), F32),
                   jax.ShapeDtypeStruct((8, D), F32)],
        in_specs=[row, pl.BlockSpec((tm, kdim), lambda i: (i, 0)), _ffn_weight_spec(N_DEV, dn_slot, lambda i: 0),
                  pl.BlockSpec((6, D), lambda i: (0, 0)), vec, vec, row],
        out_specs=[row, row, pl.BlockSpec((8, D), lambda i: (0, 0))],
        compiler_params=_cparams(("arbitrary",)),
    )(x1, act, gathered, mod6, ln_g, ln_b, target)


def ln_modulate_bwd(x, du, mod6, scale_row, dx_part, name):
    t = x.shape[0]
    tm = _tile(t, ROW_TILE)

    def body(x_ref, du_ref, mod_ref, dp_ref, dx_ref, acc_ref):
        @pl.when(pl.program_id(0) == 0)
        def _():
            acc_ref[...] = jnp.zeros_like(acc_ref)

        xh, rstd = _ln(x_ref[...])
        du_v = du_ref[...]
        sc = mod_ref[scale_row:scale_row + 1, :]
        acc_ref[0:1, :] += _colsum(du_v * xh)
        acc_ref[1:2, :] += _colsum(du_v)
        dx_ref[...] = dp_ref[...] + _ln_bwd(du_v * (1.0 + sc), xh, rstd)

    row = pl.BlockSpec((tm, D), lambda i: (i, 0))
    return pl.pallas_call(
        body, name=name, grid=(t // tm,),
        out_shape=[jax.ShapeDtypeStruct((t, D), F32), jax.ShapeDtypeStruct((8, D), F32)],
        in_specs=[row, row, pl.BlockSpec((6, D), lambda i: (0, 0)), row],
        out_specs=[row, pl.BlockSpec((8, D), lambda i: (0, 0))],
        compiler_params=_cparams(("arbitrary",)),
    )(x, du, mod6, dx_part)


def mixer_tail_bwd(x1, dgu, gathered, gate_slot, up_slot, mod6, dx1_part, x, h, ln_g, ln_b):
    t = x.shape[0]
    tm = _tile(t, ROW_TILE // 2)
    _, _, kb = dgu.shape

    def body(x1_ref, a_ref, wg_ref, wu_ref, mod_ref, dp_ref, x_ref, h_ref, g_ref, b_ref, dh_ref, dx_ref, acc_ref):
        @pl.when(pl.program_id(0) == 0)
        def _():
            acc_ref[...] = jnp.zeros_like(acc_ref)

        du = _nn(a_ref[0], wg_ref[...].reshape(kb, D)) + _nn(a_ref[1], wu_ref[...].reshape(kb, D))
        xh, rstd1 = _ln(x1_ref[...])
        acc_ref[0:1, :] += _colsum(du * xh)
        acc_ref[1:2, :] += _colsum(du)
        dx1 = dp_ref[...] + _ln_bwd(du * (1.0 + mod_ref[4:5, :]), xh, rstd1)
        gate = mod_ref[2:3, :]
        hv = h_ref[...]
        rh, rstd = _ln(ALPHA * x_ref[...] + gate * hv)
        acc_ref[3:4, :] += _colsum(dx1 * rh)
        acc_ref[4:5, :] += _colsum(dx1)
        dr = _ln_bwd(dx1 * g_ref[...], rh, rstd)
        acc_ref[2:3, :] += _colsum(dr * hv)
        dh_ref[...] = (gate * dr).astype(BF16)
        dx_ref[...] = ALPHA * dr

    row = pl.BlockSpec((tm, D), lambda i: (i, 0))
    vec = pl.BlockSpec((1, D), lambda i: (0, 0))
    return pl.pallas_call(
        body, name="mixer_tail_bwd", grid=(t // tm,),
        out_shape=[jax.ShapeDtypeStruct((t, D), BF16), jax.ShapeDtypeStruct((t, D), F32),
                   jax.ShapeDtypeStruct((8, D), F32)],
        in_specs=[row, pl.BlockSpec((2, tm, kb), lambda i: (0, i, 0)), _ffn_weight_spec(N_DEV, gate_slot, lambda i: 0),
                  _ffn_weight_spec(N_DEV, up_slot, lambda i: 0),
                  pl.BlockSpec((6, D), lambda i: (0, 0)), row, row, row, vec, vec],
        out_specs=[row, row, pl.BlockSpec((8, D), lambda i: (0, 0))],
        compiler_params=_cparams(("arbitrary",)),
    )(x1, dgu, gathered, gathered, mod6, dx1_part, x, h, ln_g, ln_b)


def mixer_tail(o_a, o_b, w_ba, w_bb, proj, w_o, x, mod6, ln_g, ln_b):
    t = o_a.shape[0]
    tm = _tile(t, ROW_TILE // 2)

    def body(oa_ref, ob_ref, wa_ref, wb_ref, ga_ref, gb_ref, w_ref, x_ref, mod_ref, g_ref, b_ref,
             ya_ref, yb_ref, m_ref, h_ref, x1_ref, u2_ref):
        ya = _nn(oa_ref[...], wa_ref[...])
        yb = _nn(ob_ref[...], wb_ref[...])
        ya_ref[...] = ya.astype(BF16)
        yb_ref[...] = yb.astype(BF16)
        merged = (_sigmoid(ga_ref[...]) * ya + _sigmoid(gb_ref[...]) * yb).astype(BF16)
        m_ref[...] = merged
        hv = _nn(merged, w_ref[...])
        h_ref[...] = hv
        rh, _ = _ln(ALPHA * x_ref[...] + mod_ref[2:3, :] * hv)
        x1 = rh * g_ref[...] + b_ref[...]
        x1_ref[...] = x1
        xh, _ = _ln(x1)
        u2_ref[...] = (xh * (1.0 + mod_ref[4:5, :]) + mod_ref[3:4, :]).astype(BF16)

    row = pl.BlockSpec((tm, D), lambda i: (i, 0))
    vec = pl.BlockSpec((1, D), lambda i: (0, 0))
    whole = pl.BlockSpec(memory_space=pltpu.VMEM)
    return pl.pallas_call(
        body, name="mixer_tail", grid=(t // tm,),
        out_shape=[jax.ShapeDtypeStruct((t, D), BF16), jax.ShapeDtypeStruct((t, D), BF16),
                   jax.ShapeDtypeStruct((t, D), BF16), jax.ShapeDtypeStruct((t, D), F32),
                   jax.ShapeDtypeStruct((t, D), F32), jax.ShapeDtypeStruct((t, D), BF16)],
        in_specs=[row, pl.BlockSpec((tm, o_b.shape[1]), lambda i: (i, 0)), whole, whole,
                  pl.BlockSpec((tm, D), lambda i: (i, GATE_BLOCK0)),
                  pl.BlockSpec((tm, D), lambda i: (i, GATE_BLOCK0 + 1)), whole,
                  row, pl.BlockSpec((6, D), lambda i: (0, 0)), vec, vec],
        out_specs=[row] * 6,
        compiler_params=_cparams(("parallel",)),
    )(o_a, o_b, w_ba, w_bb, proj, proj, w_o, x, mod6, ln_g, ln_b)


def merge_gates_bwd(dh, w_o, ya, yb, proj):
    t = ya.shape[0]
    tm = _tile(t, ROW_TILE)

    def body(dh_ref, w_ref, ya_ref, yb_ref, ga_ref, gb_ref, dya_ref, dyb_ref, dp_ref):
        dmv = _nt(dh_ref[...], w_ref[...])
        sa = _sigmoid(ga_ref[...])
        sb = _sigmoid(gb_ref[...])
        dya_ref[...] = (dmv * sa).astype(BF16)
        dyb_ref[...] = (dmv * sb).astype(BF16)
        dp_ref[0] = (dmv * ya_ref[...].astype(F32) * sa * (1.0 - sa)).astype(BF16)
        dp_ref[1] = (dmv * yb_ref[...].astype(F32) * sb * (1.0 - sb)).astype(BF16)

    row = pl.BlockSpec((tm, D), lambda i: (i, 0))
    return pl.pallas_call(
        body, name="merge_gates_bwd", grid=(t // tm,),
        out_shape=[jax.ShapeDtypeStruct((t, D), BF16)] * 2 + [jax.ShapeDtypeStruct((N_PROJ // D, t, D), BF16)],
        in_specs=[row, pl.BlockSpec((D, D), lambda i: (0, 0)), row, row,
                  pl.BlockSpec((tm, D), lambda i: (i, GATE_BLOCK0)),
                  pl.BlockSpec((tm, D), lambda i: (i, GATE_BLOCK0 + 1))],
        out_specs=[row, row, pl.BlockSpec((2, tm, D), lambda i: (GATE_BLOCK0 // 2, i, 0))],
        compiler_params=_cparams(("parallel",)),
    )(dh, w_o, ya, yb, proj, proj)


FF_CHUNK = 1408


def ffn_in_act(u, gathered, gate_slot, up_slot):
    t = u.shape[0]
    tm = _tile(t, ROW_TILE)
    nj = D_FF // FF_CHUNK
    ndev = FF_CHUNK // FF_SHARD_ROWS

    def body(a_ref, bg_ref, bu_ref, gu_ref, act_ref):
        a = a_ref[...]
        g = _nt(a, bg_ref[...].reshape(FF_CHUNK, D))
        up = _nt(a, bu_ref[...].reshape(FF_CHUNK, D))
        gu_ref[0] = g.astype(BF16)
        gu_ref[1] = up.astype(BF16)
        act_ref[...] = (g * _sigmoid(g) * up).astype(BF16)

    return pl.pallas_call(
        body, name="ffn_in_act", grid=(nj, t // tm),
        out_shape=[jax.ShapeDtypeStruct((2, t, D_FF), BF16), jax.ShapeDtypeStruct((t, D_FF), BF16)],
        in_specs=[pl.BlockSpec((tm, D), lambda j, i: (i, 0)), _ffn_weight_spec(ndev, gate_slot, lambda j, i: j),
                  _ffn_weight_spec(ndev, up_slot, lambda j, i: j)],
        out_specs=[pl.BlockSpec((2, tm, FF_CHUNK), lambda j, i: (0, i, j)),
                   pl.BlockSpec((tm, FF_CHUNK), lambda j, i: (i, j))],
        compiler_params=_cparams(("parallel", "parallel")),
    )(u, gathered, gathered)


def ffn_act_bwd(dh, gathered, dn_slot, gu):
    t = dh.shape[0]
    tm = _tile(t, ROW_TILE)
    ndev = FF_CHUNK // FF_SHARD_ROWS

    def body(a_ref, b_ref, gu_ref, o_ref):
        da = _nt(a_ref[...], b_ref[...].reshape(FF_CHUNK, D))
        g = gu_ref[0].astype(F32)
        up = gu_ref[1].astype(F32)
        s = _sigmoid(g)
        o_ref[0] = (da * up * _dsilu(g, s)).astype(BF16)
        o_ref[1] = (da * g * s).astype(BF16)

    blk = pl.BlockSpec((2, tm, FF_CHUNK), lambda j, i: (0, i, j))
    return pl.pallas_call(
        body, name="ffn_act_bwd", grid=(D_FF // FF_CHUNK, t // tm),
        out_shape=jax.ShapeDtypeStruct((2, t, D_FF), BF16),
        in_specs=[pl.BlockSpec((tm, D), lambda j, i: (i, 0)), _ffn_weight_spec(ndev, dn_slot, lambda j, i: j), blk],
        out_specs=blk,
        compiler_params=_cparams(("parallel", "parallel")),
    )(dh, gathered, gu)


def _hgrn_chunk_terms(q, fl, lbv, tril_f):
    sig = _sigmoid(fl)
    f = lbv + (1.0 - lbv) * sig
    lam = jnp.log(f)
    k = 1.0 - f
    sq = _sigmoid(q)
    qt = q * sq * Q_SCALE
    bc = _sel(_nn, lam, tril_f, 3, x_first=False)
    bmid = bc[CHUNK // 2 - 1:CHUNK // 2, :]
    bl = bc[CHUNK - 1:CHUNK, :]
    eq = jnp.exp(jnp.minimum(bc - bmid, EXP_CLIP))
    ek = jnp.exp(jnp.minimum(bmid - bc, EXP_CLIP))
    eb = jnp.exp(bc)
    ekl = jnp.exp(bl - bc)
    ebl = jnp.exp(bl)
    return sig, f, k, sq, qt, eq, ek, eb, ekl, ebl


def hgrn_fwd(proj, lb, gnorm):
    t = proj.shape[0]
    tb = _tile(t, TOKEN_BLOCK)
    ncb = tb // CHUNK

    hps = HGRN_HEADS_PER_STEP
    wide = hps * HK

    def body(q_ref, f_ref, i_ref, g_ref, lb_ref, gn_ref, oa_ref, oraw_ref, st_ref, state):
        @pl.when(pl.program_id(1) == 0)
        def _():
            state[...] = jnp.zeros_like(state)

        gn = gn_ref[...]
        mask = _tri(CHUNK)
        tril_f = mask.astype(BF16)

        def chunk(c, carry):
            sl = pl.ds(pl.multiple_of(c * CHUNK, CHUNK), CHUNK)
            for hh in range(hps):
                ln = slice(hh * HK, (hh + 1) * HK)
                q, fl, v, g = q_ref[sl, ln], f_ref[sl, ln], i_ref[sl, ln], g_ref[sl, ln]
                sig, f, k, sq, qt, eq, ek, eb, ekl, ebl = _hgrn_chunk_terms(q, fl, lb_ref[:, ln], tril_f)
                a = jnp.where(mask, _nt((qt * eq).astype(BF16), (k * ek).astype(BF16)), 0.0)
                st = state[hh]
                st_ref[hh, c] = st
                vb = v.astype(BF16)
                o = _nn(a.astype(BF16), vb) + _nt((qt * eb).astype(BF16), st.astype(BF16))
                state[hh] = st * ebl + _tn(vb, (k * ekl).astype(BF16))
                oraw_ref[sl, ln] = o
                rn = o * lax.rsqrt(jnp.mean(o * o, axis=-1, keepdims=True) + RMS_EPS)
                oa_ref[sl, ln] = (rn * gn * g * _sigmoid(g)).astype(BF16)
            return carry

        lax.fori_loop(0, ncb, chunk, 0, unroll=min(CHUNK_UNROLL, ncb))

    def col(block):
        return pl.BlockSpec((tb, wide), lambda h, j: (j, block * (N_HEADS_A // hps) + h))

    return pl.pallas_call(
        body, name="hgrn_fwd", grid=(N_HEADS_A // hps, t // tb),
        out_shape=[jax.ShapeDtypeStruct((t, D), BF16), jax.ShapeDtypeStruct((t, D), F32),
                   jax.ShapeDtypeStruct((N_HEADS_A, t // CHUNK, HK, HK), F32)],
        in_specs=[col(0), col(1), col(2), col(3), pl.BlockSpec((1, wide), lambda h, j: (0, h)),
                  pl.BlockSpec((1, HK), lambda h, j: (0, 0))],
        out_specs=[pl.BlockSpec((tb, wide), lambda h, j: (j, h)), pl.BlockSpec((tb, wide), lambda h, j: (j, h)),
                   pl.BlockSpec((hps, ncb, HK, HK), lambda h, j: (h, j, 0, 0))],
        scratch_shapes=[pltpu.VMEM((hps, HK, HK), F32)],
        compiler_params=_cparams(("parallel", "arbitrary")),
    )(proj, proj, proj, proj, lb, gnorm)


def hgrn_bwd(proj, lb, gnorm, o_raw, dya, w_ba, states, give, dproj):
    t = proj.shape[0]
    tb = _tile(t, TOKEN_BLOCK)
    ncb = tb // CHUNK
    nb = t // tb
    hps = HGRN_HEADS_PER_STEP
    wide = hps * HK

    def body(q_ref, f_ref, i_ref, g_ref, lb_ref, gn_ref, oraw_ref, dya_ref, wba_ref, st_ref, give_ref, dp_in_ref,
             dp_ref, dlb_ref, dgn_ref, got_ref, dstate, doa_ref, send_sem, recv_sem):
        h, j = pl.program_id(0), pl.program_id(1)
        swap_start, swap_wait = _sibling_exchange(give_ref, got_ref, send_sem, recv_sem)
        doa_ref[...] = _nt(dya_ref[...], wba_ref[...])

        @pl.when((h == 0) & (j == 0))
        def _():
            swap_start()

        @pl.when(j == 0)
        def _():
            dstate[...] = jnp.zeros_like(dstate)
            dlb_ref[...] = jnp.zeros_like(dlb_ref)

        @pl.when((j == 0) & (h == 0))
        def _():
            dgn_ref[...] = jnp.zeros_like(dgn_ref)

        gn = gn_ref[...]
        mask = _tri(CHUNK)
        mask_t = _tri(CHUNK, upper=True)
        tril_f = mask.astype(BF16)
        triu_f = mask_t.astype(BF16)

        def chunk(i, c0):
            c = ncb - 1 - i
            sl = pl.ds(pl.multiple_of(c * CHUNK, CHUNK), CHUNK)
            for hh in range(hps):
                ln = slice(hh * HK, (hh + 1) * HK)
                q, fl, v, g = q_ref[sl, ln], f_ref[sl, ln], i_ref[sl, ln], g_ref[sl, ln]
                lbv = lb_ref[:, ln]
                sig, f, k, sq, qt, eq, ek, eb, ekl, ebl = _hgrn_chunk_terms(q, fl, lbv, tril_f)
                qe = (qt * eq).astype(BF16)
                ke = (k * ek).astype(BF16)
                st32 = st_ref[hh, c]
                st = st32.astype(BF16)
                dst = dstate[hh]
                dstb = dst.astype(BF16)
                o = oraw_ref[sl, ln]
                rstd = lax.rsqrt(jnp.mean(o * o, axis=-1, keepdims=True) + RMS_EPS)
                rn = o * rstd
                sgm = _sigmoid(g)
                sg = g * sgm
                doa_v = doa_ref[sl, ln]
                drn = doa_v * gn * sg
                dgn_ref[...] += _colsum(doa_v * rn * sg)
                dp_ref[3, sl, ln] = (doa_v * rn * gn * _dsilu(g, sgm)).astype(BF16)
                do = rstd * (drn - rn * jnp.mean(drn * rn, axis=-1, keepdims=True))
                dob = do.astype(BF16)
                vb = v.astype(BF16)
                da = jnp.where(mask, _nt(dob, vb), 0.0).astype(BF16)
                da_t = jnp.where(mask_t, _nt(vb, dob), 0.0).astype(BF16)
                a_t = jnp.where(mask_t, _nt(ke, qe), 0.0).astype(BF16)
                kl = (k * ekl).astype(BF16)
                qb = (qt * eb).astype(BF16)
                dq_in = _nn(da, ke)
                dk_in = _nn(da_t, qe)
                dq_out = eb * _nn(dob, st)
                dk_out = ekl * _nn(vb, dstb)
                dqt = eq * dq_in + dq_out
                dk = ek * dk_in + dk_out
                dv = _nn(a_t, dob) + _nt(kl, dstb)
                dstate[hh] = dst * ebl + _tn(dob, qb)
                dbig = qe.astype(F32) * dq_in - ke.astype(F32) * dk_in + qt * dq_out - k * dk_out
                beyond = _colsum(k * dk_out) + ebl * _colsum(dst * st32)
                dlam = _sel(_nn, dbig, triu_f, 3, x_first=False) + beyond
                df = dlam / f - dk
                dp_ref[1, sl, ln] = (df * (1.0 - lbv) * sig * (1.0 - sig)).astype(BF16)
                dlb_ref[:, ln] += _colsum(df * (1.0 - sig))
                dp_ref[0, sl, ln] = (dqt * Q_SCALE * _dsilu(q, sq)).astype(BF16)
                dp_ref[2, sl, ln] = dv.astype(BF16)
            return c0

        lax.fori_loop(0, ncb, chunk, 0, unroll=min(CHUNK_UNROLL, ncb))

        @pl.when((h == N_HEADS_A // hps - 1) & (j == nb - 1))
        def _():
            swap_wait()

    def col(block):
        return pl.BlockSpec((tb, wide), lambda h, j: (nb - 1 - j, block * (N_HEADS_A // hps) + h))

    hcol = pl.BlockSpec((tb, wide), lambda h, j: (nb - 1 - j, h))
    hbm = pl.BlockSpec(memory_space=pl.ANY)
    return pl.pallas_call(
        body, name="hgrn_bwd", grid=(N_HEADS_A // hps, nb),
        out_shape=[jax.ShapeDtypeStruct(dproj.shape, dproj.dtype), jax.ShapeDtypeStruct((1, D), F32),
                   jax.ShapeDtypeStruct((1, HK), F32), jax.ShapeDtypeStruct(give.shape, give.dtype)],
        in_specs=[col(0), col(1), col(2), col(3), pl.BlockSpec((1, wide), lambda h, j: (0, h)),
                  pl.BlockSpec((1, HK), lambda h, j: (0, 0)), hcol,
                  pl.BlockSpec((tb, D), lambda h, j: (nb - 1 - j, 0)), pl.BlockSpec((wide, D), lambda h, j: (h, 0)),
                  pl.BlockSpec((hps, ncb, HK, HK), lambda h, j: (h, nb - 1 - j, 0, 0)), hbm, hbm],
        out_specs=[pl.BlockSpec((4, tb, wide), lambda h, j: (0, nb - 1 - j, h)),
                   pl.BlockSpec((1, wide), lambda h, j: (0, h)), pl.BlockSpec((1, HK), lambda h, j: (0, 0)), hbm],
        input_output_aliases={11: 0},
        scratch_shapes=[pltpu.VMEM((hps, HK, HK), F32), pltpu.VMEM((tb, wide), F32)] + SIBLING_SEMS,
        compiler_params=_cparams(("arbitrary", "arbitrary")),
    )(proj, proj, proj, proj, lb, gnorm, o_raw, dya, w_ba, states, give, dproj)


CONV_BLOCK0 = 6
CONV_TAPS = 4
HALO = 8


def conv_fwd(proj, conv_w, conv_b):
    t = proj.shape[0]
    tm = _tile(t, ROW_TILE)
    r = tm // HALO

    def body(x_ref, halo_ref, w_ref, b_ref, o_ref, ds_ref):
        i = pl.program_id(1)
        halo = jnp.where(i > 0, halo_ref[...], 0.0)
        ext = jnp.concatenate([halo, x_ref[...]], axis=0)
        pre = b_ref[...] + w_ref[CONV_TAPS - 1:CONV_TAPS, :] * ext[HALO:, :]
        for tap in range(CONV_TAPS - 1):
            pre = pre + w_ref[tap:tap + 1, :] * pltpu.roll(ext, CONV_TAPS - 1 - tap, axis=0)[HALO:, :]
        s = _sigmoid(pre)
        o_ref[...] = pre * s
        ds_ref[...] = _dsilu(pre, s).astype(BF16)

    blk = pl.BlockSpec((tm, D), lambda cb, i: (i, cb))
    return pl.pallas_call(
        body, name="conv_fwd", grid=(CONV_DIM // D, t // tm),
        out_shape=[jax.ShapeDtypeStruct((t, CONV_DIM), F32), jax.ShapeDtypeStruct((t, CONV_DIM), BF16)],
        in_specs=[pl.BlockSpec((tm, D), lambda cb, i: (i, CONV_BLOCK0 + cb)),
                  pl.BlockSpec((HALO, D), lambda cb, i: (jnp.maximum(i * r - 1, 0), CONV_BLOCK0 + cb)),
                  pl.BlockSpec((CONV_TAPS, D), lambda cb, i: (0, cb)), pl.BlockSpec((1, D), lambda cb, i: (0, cb))],
        out_specs=[blk, blk],
        compiler_params=_cparams(("parallel", "parallel")),
    )(proj, proj, conv_w, conv_b)


def conv_bwd(proj, dxs, dbm, dcm, dsilu, conv_w, dproj):
    t = proj.shape[0]
    tm = _tile(t, ROW_TILE)
    r = tm // HALO
    n = t // tm
    last_halo = t // HALO - 1
    x_blocks = B_INNER // D
    assert 2 * GROUP_W == D and CONV_DIM == B_INNER + D

    def body(x_ref, prev_ref, gx_ref, gxn_ref, gb_ref, gbn_ref, gc_ref, gcn_ref, s_ref, snext_ref, w_ref, dp_in_ref,
             dx_ref, dw_ref, db_ref):
        i = pl.program_id(1)
        is_x = pl.program_id(0) < x_blocks

        @pl.when(i == 0)
        def _():
            dw_ref[...] = jnp.zeros_like(dw_ref)
            db_ref[...] = jnp.zeros_like(db_ref)

        d = jnp.where(is_x, gx_ref[...], jnp.concatenate([gb_ref[...], gc_ref[...]], axis=1))
        dnext = jnp.where(is_x, gxn_ref[0:HALO, :], jnp.concatenate([gbn_ref[0:HALO, :], gcn_ref[0:HALO, :]], axis=1))
        dpre = jnp.concatenate([d.astype(F32) * s_ref[...].astype(F32),
                                jnp.where(i < n - 1, dnext.astype(F32) * snext_ref[0:HALO, :].astype(F32),
                                          0.0)], axis=0)
        dx = w_ref[CONV_TAPS - 1:CONV_TAPS, :] * dpre[:tm, :]
        for tap in range(CONV_TAPS - 1):
            back = CONV_TAPS - 1 - tap
            dx = dx + w_ref[tap:tap + 1, :] * pltpu.roll(dpre, tm + HALO - back, axis=0)[:tm, :]
        dx_ref[...] = dx.astype(BF16)
        dp = dpre[:tm, :]
        db_ref[...] += _colsum(dp)
        prev = jnp.where(i > 0, prev_ref[...], 0.0)
        ext = jnp.concatenate([prev, x_ref[...]], axis=0)
        dw_ref[CONV_TAPS - 1:CONV_TAPS, :] += _colsum(dp * ext[HALO:, :])
        for tap in range(CONV_TAPS - 1):
            dw_ref[tap:tap + 1, :] += _colsum(dp * pltpu.roll(ext, CONV_TAPS - 1 - tap, axis=0)[HALO:, :])

    def next_rows(i):
        return jnp.minimum((i + 1) * (r // 2), last_halo // 2)

    blk = pl.BlockSpec((tm, D), lambda cb, i: (i, cb))
    nxt = pl.BlockSpec((2 * HALO, D), lambda cb, i: (next_rows(i), cb))
    xblk = pl.BlockSpec((tm, D), lambda cb, i: (i, jnp.minimum(cb, x_blocks - 1)))
    xnxt = pl.BlockSpec((2 * HALO, D), lambda cb, i: (next_rows(i), jnp.minimum(cb, x_blocks - 1)))
    gblk = pl.BlockSpec((tm, GROUP_W), lambda cb, i: (i, 0))
    gnxt = pl.BlockSpec((2 * HALO, GROUP_W), lambda cb, i: (next_rows(i), 0))
    return pl.pallas_call(
        body, name="conv_bwd", grid=(CONV_DIM // D, n),
        out_shape=[jax.ShapeDtypeStruct(dproj.shape, dproj.dtype), jax.ShapeDtypeStruct((8, CONV_DIM), F32),
                   jax.ShapeDtypeStruct((1, CONV_DIM), F32)],
        in_specs=[pl.BlockSpec((tm, D), lambda cb, i: (i, CONV_BLOCK0 + cb)),
                  pl.BlockSpec((HALO, D), lambda cb, i: (jnp.maximum(i * r - 1, 0), CONV_BLOCK0 + cb)),
                  xblk, xnxt, gblk, gnxt, gblk, gnxt, blk, nxt,
                  pl.BlockSpec((CONV_TAPS, D), lambda cb, i: (0, cb)), pl.BlockSpec(memory_space=pl.ANY)],
        out_specs=[pl.BlockSpec((None, tm, D), lambda cb, i: (CONV_BLOCK0 + cb, i, 0)),
                   pl.BlockSpec((8, D), lambda cb, i: (0, cb)), pl.BlockSpec((1, D), lambda cb, i: (0, cb))],
        input_output_aliases={11: 0},
        compiler_params=_cparams(("parallel", "arbitrary")),
    )(proj, proj, dxs, dxs, dbm, dbm, dcm, dcm, dsilu, dsilu, conv_w, dproj)


def dt_fill(ddt, dproj):
    t = ddt.shape[0]
    tm = _tile(t, WIDE_ROW_TILE)
    w = ddt.shape[1]

    def body(d_ref, dp_in_ref, o_ref):
        o_ref[:, :w] = d_ref[...]
        o_ref[:, w:] = jnp.zeros((tm, D - w), o_ref.dtype)

    return pl.pallas_call(
        body, name="dt_fill", grid=(t // tm,),
        out_shape=jax.ShapeDtypeStruct(dproj.shape, dproj.dtype),
        in_specs=[pl.BlockSpec((tm, w), lambda i: (i, 0)), pl.BlockSpec(memory_space=pl.ANY)],
        out_specs=pl.BlockSpec((None, tm, D), lambda i: (DT_COL_BLOCK, i, 0)),
        input_output_aliases={1: 0},
        compiler_params=_cparams(("parallel",)),
    )(ddt, dproj)


Z_BLOCK0 = 8
DT_COL_BLOCK = 9
DT_BLOCK0 = 8 * DT_COL_BLOCK
GATE_BLOCK0 = 10
B_BLOCK0 = 16
C_BLOCK0 = 20


def _head_expand():
    e = np.zeros((N_STATE, GROUP_W), np.float32)
    for hh in range(HEADS_PER_GROUP):
        e[hh, hh * HEAD_P:(hh + 1) * HEAD_P] = 1.0
    return jnp.asarray(e, BF16)


def _ssd_chunk_terms(dt, bias, alog, expand, tril_f, eye):
    dtb = dt + bias
    delta = jnp.maximum(dtb, 0.0) + jnp.log(1.0 + jnp.exp(-jnp.abs(dtb)))
    ea = jnp.exp(alog)
    a = -ea * delta
    acum = _sel(_nn, a, tril_f, 3, x_first=False)
    delta_e = _sel(_nn, delta, expand, 2)
    acum_e = _sel(_nn, acum, expand, 2)
    acum_t = _sel(_nt, acum, eye, 3, x_first=False)
    return dtb, delta, ea, a, acum, delta_e, acum_e, acum_t


def ssd_fwd(proj, xc, alog4, bias4, dskip4, wnorm, expand):
    t = proj.shape[0]
    tb = _tile(t, TOKEN_BLOCK)
    ncb = tb // SSD_CHUNK

    def body(xs_ref, b_ref, c_ref, dt_ref, z_ref, alog_ref, bias_ref, dsk_ref, wn_ref, e_ref, ob_ref, st_ref, state):
        @pl.when(pl.program_id(1) == 0)
        def _():
            state[...] = jnp.zeros_like(state)

        expand = e_ref[...]
        mask = _tri(SSD_CHUNK)
        tril_f = mask.astype(BF16)
        eye = (lax.broadcasted_iota(jnp.int32, (N_STATE, N_STATE), 0) ==
               lax.broadcasted_iota(jnp.int32, (N_STATE, N_STATE), 1)).astype(BF16)
        alog, bias = alog_ref[0], bias_ref[0]
        d_e = _sel(_nn, jnp.broadcast_to(dsk_ref[0], (8, N_STATE)), expand, 3)[0:1, :]
        wn = wn_ref[...]

        def chunk(c, carry):
            sl = pl.ds(pl.multiple_of(c * SSD_CHUNK, SSD_CHUNK), SSD_CHUNK)
            xs, bm, cm, dt, z = xs_ref[sl, :], b_ref[sl, :], c_ref[sl, :], dt_ref[sl, :], z_ref[sl, :]
            dtb, delta, ea, a, acum, delta_e, acum_e, acum_t = _ssd_chunk_terms(dt, bias, alog, expand, tril_f, eye)
            alast_e = acum_e[SSD_CHUNK - 1:SSD_CHUNK, :]
            xd = xs * delta_e
            xdb = xd.astype(BF16)
            cb_, bb_ = cm.astype(BF16), bm.astype(BF16)
            cbm = _nt(cb_, bb_)
            ys = []
            for hh in range(HEADS_PER_GROUP):
                lh = jnp.where(mask, jnp.exp(jnp.minimum(acum[:, hh:hh + 1] - acum_t[hh:hh + 1, :], 0.0)), 0.0)
                ys.append(_nn((cbm * lh).astype(BF16), xdb[:, hh * HEAD_P:(hh + 1) * HEAD_P]))
            st = state[...]
            st_ref[0, c] = st
            y = jnp.concatenate(ys, axis=1) + _nn(cb_, st.astype(BF16)) * jnp.exp(acum_e) + xs * d_e
            state[...] = st * jnp.exp(alast_e) + _tn(bb_, (xd * jnp.exp(alast_e - acum_e)).astype(BF16))
            yg = y * z * _sigmoid(z)
            ob_ref[sl, :] = (yg * lax.rsqrt(jnp.mean(yg * yg, axis=-1, keepdims=True) + RMS_EPS) * wn).astype(BF16)
            return carry

        lax.fori_loop(0, ncb, chunk, 0, unroll=min(CHUNK_UNROLL, ncb))

    small = pl.BlockSpec((1, 1, N_STATE), lambda g, j: (g, 0, 0))
    return pl.pallas_call(
        body, name="ssd_fwd", grid=(N_GROUPS, t // tb),
        out_shape=[jax.ShapeDtypeStruct((t, B_INNER), BF16),
                   jax.ShapeDtypeStruct((N_GROUPS, t // SSD_CHUNK, N_STATE, GROUP_W), F32)],
        in_specs=[pl.BlockSpec((tb, GROUP_W), lambda g, j: (j, g)),
                  pl.BlockSpec((tb, N_STATE), lambda g, j: (j, B_BLOCK0 + g)),
                  pl.BlockSpec((tb, N_STATE), lambda g, j: (j, C_BLOCK0 + g)),
                  pl.BlockSpec((tb, N_STATE), lambda g, j: (j, DT_BLOCK0 + g)),
                  pl.BlockSpec((tb, GROUP_W), lambda g, j: (j, Z_BLOCK0 + g)),
                  small, small, small, pl.BlockSpec((1, GROUP_W), lambda g, j: (0, g)),
                  pl.BlockSpec((N_STATE, GROUP_W), lambda g, j: (0, 0))],
        out_specs=[pl.BlockSpec((tb, GROUP_W), lambda g, j: (j, g)),
                   pl.BlockSpec((1, ncb, N_STATE, GROUP_W), lambda g, j: (g, j, 0, 0))],
        scratch_shapes=[pltpu.VMEM((N_STATE, GROUP_W), F32)],
        compiler_params=_cparams(("parallel", "arbitrary")),
    )(xc, xc, xc, proj, proj, alog4, bias4, dskip4, wnorm, expand)


def ssd_bwd(proj, xc, alog4, bias4, dskip4, wnorm, expand, dyb, w_bb, states, part, dproj):
    t = proj.shape[0]
    tb = _tile(t, TOKEN_BLOCK)
    lc = min(SSD_CHUNK_BWD, tb)
    ncb = tb // lc
    nsaved = tb // SSD_CHUNK
    nb = t // tb

    def body(xs_ref, b_ref, c_ref, dt_ref, z_ref, alog_ref, bias_ref, dsk_ref, wn_ref, e_ref, dyb_ref, wbb_ref, st_ref,
             part_ref, dp_in_ref, dxs_ref, db_ref, dc_ref, dz_ref, ddt_ref, dwn_ref, dalog_ref, dbias_ref, ddsk_ref,
             parts_ref, dstate, dob_ref, send_sems, recv_sems, local_sem):
        xchg_start, xchg_wait = _chip_exchange(part_ref, parts_ref, send_sems, recv_sems, local_sem)
        dob_ref[...] = _nt(dyb_ref[...], wbb_ref[...])

        @pl.when((pl.program_id(0) == 0) & (pl.program_id(1) == 0))
        def _():
            xchg_start()

        @pl.when(pl.program_id(1) == 0)
        def _():
            dstate[...] = jnp.zeros_like(dstate)
            dwn_ref[...] = jnp.zeros_like(dwn_ref)
            dalog_ref[...] = jnp.zeros_like(dalog_ref)
            dbias_ref[...] = jnp.zeros_like(dbias_ref)
            ddsk_ref[...] = jnp.zeros_like(ddsk_ref)

        expand = e_ref[...]
        mask = _tri(lc)
        mask_t = _tri(lc, upper=True)
        tril_f = mask.astype(BF16)
        triu_f = mask_t.astype(BF16)
        eye = (lax.broadcasted_iota(jnp.int32, (N_STATE, N_STATE), 0) ==
               lax.broadcasted_iota(jnp.int32, (N_STATE, N_STATE), 1)).astype(BF16)
        alog, bias = alog_ref[0], bias_ref[0]
        d_e = _sel(_nn, jnp.broadcast_to(dsk_ref[0], (8, N_STATE)), expand, 3)[0:1, :]
        wn = wn_ref[...]

        def chunk(i, c0):
            c = ncb - 1 - i
            sl = pl.ds(pl.multiple_of(c * lc, lc), lc)
            xs, bm, cm, dt, z = xs_ref[sl, :], b_ref[sl, :], c_ref[sl, :], dt_ref[sl, :], z_ref[sl, :]
            dtb, delta, ea, a, acum, delta_e, acum_e, acum_t = _ssd_chunk_terms(dt, bias, alog, expand, tril_f, eye)
            alast_e = acum_e[lc - 1:lc, :]
            eacum = jnp.exp(acum_e)
            wl = jnp.exp(alast_e - acum_e)
            xd = xs * delta_e
            xdb = xd.astype(BF16)
            cb_, bb_ = cm.astype(BF16), bm.astype(BF16)
            cbm = _nt(cb_, bb_)
            st32 = st_ref[0, c * (lc // SSD_CHUNK)]
            stb = st32.astype(BF16)
            dst = dstate[...]
            dstb = dst.astype(BF16)
            lhs, mixes, ys = [], [], []
            for hh in range(HEADS_PER_GROUP):
                col, row = acum[:, hh:hh + 1], acum_t[hh:hh + 1, :]
                lh = jnp.where(mask, jnp.exp(jnp.minimum(col - row, 0.0)), 0.0)
                mix = (cbm * lh).astype(BF16)
                lhs.append(lh)
                mixes.append(mix)
                ys.append(_nn(mix, xdb[:, hh * HEAD_P:(hh + 1) * HEAD_P]))
            y_in = jnp.concatenate(ys, axis=1)
            y_out = _nn(cb_, stb) * eacum
            y = y_in + y_out + xs * d_e
            sgz = _sigmoid(z)
            sz = z * sgz
            yg = y * sz
            rstd = lax.rsqrt(jnp.mean(yg * yg, axis=-1, keepdims=True) + RMS_EPS)
            nrm = yg * rstd
            dob_v = dob_ref[sl, :]
            dn = dob_v * wn
            dwn_ref[...] += _colsum(dob_v * nrm)
            dyg = rstd * (dn - nrm * jnp.mean(dn * nrm, axis=-1, keepdims=True))
            dy = dyg * sz
            dz_ref[sl, :] = (dyg * y * _dsilu(z, sgz)).astype(BF16)
            dyb = dy.astype(BF16)
            dxds = []
            dcb = jnp.zeros((lc, lc), F32)
            for hh in range(HEADS_PER_GROUP):
                hs = slice(hh * HEAD_P, (hh + 1) * HEAD_P)
                dy_h, x_h = dyb[:, hs], xdb[:, hs]
                dxds.append(_tn(mixes[hh], dy_h))
                dcb = dcb + _nt(dy_h, x_h) * lhs[hh]
            dcbb = dcb.astype(BF16)
            dye = (dy * eacum).astype(BF16)
            xw = (xd * wl).astype(BF16)
            dxd_in = jnp.concatenate(dxds, axis=1)
            dxd_out = wl * _nn(bb_, dstb)
            dxd = dxd_in + dxd_out
            dc_ref[sl, :] = (_nn(dcbb, bb_) + _nt(dye, stb)).astype(dc_ref.dtype)
            db_ref[sl, :] = (_tn(dcbb, cb_) + _nt(xw, dstb)).astype(db_ref.dtype)
            dstate[...] = dst * jnp.exp(alast_e) + _tn(cb_, dye)
            col_out = xd * dxd_out
            dac = _sel(_nt, dyb.astype(F32) * y_in - xdb.astype(F32) * dxd_in + dy * y_out - col_out, expand, 2)
            beyond = _colsum(col_out) + jnp.exp(alast_e) * _colsum(dst * st32)
            da = (_sel(_nn, dac, triu_f, 3, x_first=False) +
                  _sel(_nt, jnp.broadcast_to(beyond, (8, GROUP_W)), expand, 3)[0:1, :])
            ddelta = _sel(_nt, dxd * xs, expand, 2) - da * ea
            dalog_ref[0] += _colsum(da * a)
            ddtb = ddelta * _sigmoid(dtb)
            dbias_ref[0] += _colsum(ddtb)
            ddt_ref[sl, :] = ddtb.astype(BF16)
            ddsk_ref[0] += _sel(_nt, jnp.broadcast_to(_colsum(dy * xs), (8, GROUP_W)), expand, 3)[0:1, :]
            dxs_ref[sl, :] = (dxd * delta_e + dy * d_e).astype(dxs_ref.dtype)
            return c0

        lax.fori_loop(0, ncb, chunk, 0, unroll=min(CHUNK_UNROLL, ncb))

        @pl.when((pl.program_id(0) == N_GROUPS - 1) & (pl.program_id(1) == nb - 1))
        def _():
            xchg_wait()

    small = pl.BlockSpec((1, 1, N_STATE), lambda g, j: (g, 0, 0))
    wide = pl.BlockSpec((tb, GROUP_W), lambda g, j: (nb - 1 - j, g))
    narrow = pl.BlockSpec((tb, N_STATE), lambda g, j: (nb - 1 - j, g))
    hbm = pl.BlockSpec(memory_space=pl.ANY)
    return pl.pallas_call(
        body, name="ssd_bwd", grid=(N_GROUPS, nb),
        out_shape=[jax.ShapeDtypeStruct((t, B_INNER), BF16), jax.ShapeDtypeStruct((t, GROUP_W), BF16),
                   jax.ShapeDtypeStruct((t, GROUP_W), BF16), jax.ShapeDtypeStruct(dproj.shape, dproj.dtype),
                   jax.ShapeDtypeStruct((t, GROUP_W), BF16), jax.ShapeDtypeStruct((1, B_INNER), F32),
                   jax.ShapeDtypeStruct((N_GROUPS, 1, N_STATE), F32), jax.ShapeDtypeStruct((N_GROUPS, 1, N_STATE), F32),
                   jax.ShapeDtypeStruct((N_GROUPS, 1, N_STATE), F32), jax.ShapeDtypeStruct(part.shape, part.dtype)],
        in_specs=[wide,
                  pl.BlockSpec((tb, N_STATE), lambda g, j: (nb - 1 - j, B_BLOCK0 + g)),
                  pl.BlockSpec((tb, N_STATE), lambda g, j: (nb - 1 - j, C_BLOCK0 + g)),
                  pl.BlockSpec((tb, N_STATE), lambda g, j: (nb - 1 - j, DT_BLOCK0 + g)),
                  pl.BlockSpec((tb, GROUP_W), lambda g, j: (nb - 1 - j, Z_BLOCK0 + g)),
                  small, small, small, pl.BlockSpec((1, GROUP_W), lambda g, j: (0, g)),
                  pl.BlockSpec((N_STATE, GROUP_W), lambda g, j: (0, 0)),
                  pl.BlockSpec((tb, D), lambda g, j: (nb - 1 - j, 0)), pl.BlockSpec((GROUP_W, D), lambda g, j: (g, 0)),
                  pl.BlockSpec((1, nsaved, N_STATE, GROUP_W), lambda g, j: (g, nb - 1 - j, 0, 0)), hbm, hbm],
        out_specs=[wide, narrow, narrow,
                   pl.BlockSpec((None, tb, GROUP_W), lambda g, j: (Z_BLOCK0 // 2 + g // 2, nb - 1 - j, g % 2)),
                   narrow, pl.BlockSpec((1, GROUP_W), lambda g, j: (0, g)), small, small, small, hbm],
        input_output_aliases={14: 3},
        scratch_shapes=[pltpu.VMEM((N_STATE, GROUP_W), F32), pltpu.VMEM((tb, GROUP_W), F32)] + CHIP_SEMS,
        compiler_params=_cparams(("arbitrary", "arbitrary")),
    )(xc, xc, xc, proj, proj, alog4, bias4, dskip4, wnorm, expand, dyb, w_bb, states, part, dproj)


def lower_bound_fwd(hgrn_lb):
    def body(a_ref, o_ref):
        a0, a1 = a_ref[0:1, :], a_ref[1:2, :]
        m = jnp.maximum(a0, a1)
        e0, e1 = jnp.exp(a0 - m), jnp.exp(a1 - m)
        o_ref[...] = e0 / (e0 + e1)

    return pl.pallas_call(body, name="lower_bound_fwd", out_shape=jax.ShapeDtypeStruct((1, D), F32))(hgrn_lb)


def ada_weight_grad(c_all, dmod_cols):
    def body(c_ref, d_ref, o_ref):
        cval = c_ref[...]
        o_ref[...] = _tn(cval * _sigmoid(cval), d_ref[...], HI)

    return pl.pallas_call(body, name="ada_weight_grad",
                          out_shape=jax.ShapeDtypeStruct((D, dmod_cols.shape[1]), F32))(c_all, dmod_cols)


def reduce_small(gathered, hgrn_lb, dlb_off):
    n = gathered.shape[2]

    def body(g_ref, a_ref, o_ref, glb_ref):
        s = g_ref[0]
        for d in range(1, N_DEV):
            s = s + g_ref[d]
        o_ref[...] = s
        a0, a1 = a_ref[0:1, :], a_ref[1:2, :]
        m = jnp.maximum(a0, a1)
        e0, e1 = jnp.exp(a0 - m), jnp.exp(a1 - m)
        p0 = e0 / (e0 + e1)
        tq = s[:, dlb_off:dlb_off + D] * p0 * (1.0 - p0)
        glb_ref[0:1, :] = tq
        glb_ref[1:2, :] = -tq

    return pl.pallas_call(body, name="reduce_small",
                          out_shape=[jax.ShapeDtypeStruct((1, n), F32), jax.ShapeDtypeStruct((2, D), F32)])(gathered, hgrn_lb)


def _adam_math(w, g, m, v):
    m2 = ADAM_B1 * m + (1.0 - ADAM_B1) * g
    v2 = ADAM_B2 * v + (1.0 - ADAM_B2) * (g * g)
    m_hat = m2 / (1.0 - ADAM_B1 ** ADAM_STEP)
    v_hat = v2 / (1.0 - ADAM_B2 ** ADAM_STEP)
    delta = -ADAM_LR * (m_hat / (jnp.sqrt(v_hat) + ADAM_EPS) + ADAM_WD * w)
    return delta, m2, v2


def _row_tile(rows, mult=8, cap=128):
    for cand in range(cap - cap % mult, 0, -mult):
        if rows % cand == 0:
            return cand
    return rows


def sum_parts(parts, name):
    n, rows, cols = parts.shape
    tr = _row_tile(rows, 16, 1024)

    def body(p_ref, o_ref):
        s = p_ref[0].astype(F32)
        for d in range(1, n):
            s = s + p_ref[d].astype(F32)
        o_ref[...] = s

    return pl.pallas_call(
        body, name=name, grid=(rows // tr,),
        out_shape=jax.ShapeDtypeStruct((rows, cols), F32),
        in_specs=[pl.BlockSpec((n, tr, cols), lambda i: (0, i, 0))],
        out_specs=pl.BlockSpec((tr, cols), lambda i: (i, 0)),
        compiler_params=_cparams(("parallel",)),
    )(parts)


def sum_pair(a, b, name):
    rows, cols = a.shape
    tr = _row_tile(rows, 16, 1024)

    def body(a_ref, b_ref, o_ref):
        o_ref[...] = (a_ref[...].astype(F32) + b_ref[...].astype(F32)).astype(o_ref.dtype)

    blk = pl.BlockSpec((tr, cols), lambda i: (i, 0))
    return pl.pallas_call(
        body, name=name, grid=(rows // tr,),
        out_shape=jax.ShapeDtypeStruct((rows, cols), a.dtype),
        in_specs=[blk, blk], out_specs=blk,
        compiler_params=_cparams(("parallel",)),
    )(a, b)


def adamw(w, g, m, v, name):
    rows, cols = w.shape
    tr = _row_tile(rows, 8, 256)

    def body(w_ref, g_ref, m_ref, v_ref, d_ref, m2_ref, v2_ref):
        delta, m2, v2 = _adam_math(w_ref[...], g_ref[...], m_ref[...], v_ref[...])
        d_ref[...] = delta
        m2_ref[...] = m2
        v2_ref[...] = v2

    blk = pl.BlockSpec((tr, cols), lambda i: (i, 0))
    return pl.pallas_call(
        body, name=name, grid=(rows // tr,),
        out_shape=[jax.ShapeDtypeStruct((rows, cols), F32)] * 3,
        in_specs=[blk] * 4, out_specs=[blk] * 3,
        compiler_params=_cparams(("parallel",)),
    )(w, g, m, v)


def _pad128(n):
    return -(-n // 128) * 128


def _pack(arrays):
    offs, parts, off = [], [], 0
    for a in arrays:
        flat = a.reshape(1, -1)
        n = flat.shape[1]
        offs.append(off)
        parts.append(jnp.pad(flat, ((0, 0), (0, _pad128(n) - n))))
        off += _pad128(n)
    return jnp.concatenate(parts, axis=1), offs


def _unpack(vec, offs, shapes):
    out = []
    for off, shp in zip(offs, shapes):
        n = int(np.prod(shp))
        out.append(vec[0, off:off + n].reshape(shp))
    return out


IN_ROWS = IN_DIM // N_DEV
DT_ROW0 = 9216
DT_DEV, DT_LO = divmod(DT_ROW0, IN_ROWS)


GATE_SHIFT = D - 32


def _in_row_pieces(tile):
    pieces = []
    if tile == DT_COL_BLOCK:
        for g in range(N_GROUPS):
            o = DT_ROW0 + HEADS_PER_GROUP * g
            pieces.append((N_STATE * g, o // IN_ROWS, o % IN_ROWS, HEADS_PER_GROUP))
        return pieces
    r, end = tile * D, (tile + 1) * D
    while r < end:
        o = r if r < DT_ROW0 else r - GATE_SHIFT
        dev, loc = divmod(o, IN_ROWS)
        n = min(end - r, IN_ROWS - loc)
        pieces.append((r - tile * D, dev, loc, n))
        r += n
    return pieces


def assemble_w_in(g_all):
    ntile = N_PROJ // D

    def body(g_ref, o_ref):
        j = pl.program_id(0)
        for tile in range(ntile):
            @pl.when(j == tile)
            def _(tile=tile):
                if tile == DT_COL_BLOCK:
                    o_ref[...] = jnp.zeros_like(o_ref)
                for dst, dev, loc, n in _in_row_pieces(tile):
                    o_ref[pl.ds(dst, n), :] = g_ref[dev, pl.ds(loc, n), :]

    return pl.pallas_call(
        body, name="assemble_w_in", grid=(ntile,),
        out_shape=jax.ShapeDtypeStruct((N_PROJ, D), g_all.dtype),
        in_specs=[pl.BlockSpec(memory_space=pltpu.VMEM)],
        out_specs=pl.BlockSpec((D, D), lambda j: (j, 0)),
        compiler_params=_cparams(("arbitrary",)),
    )(g_all)


def _grad_in_blocks(g_t, core, slot):
    dt0 = DT_COL_BLOCK * D
    dt = g_t[dt0:dt0 + N_GROUPS * N_STATE].reshape(N_GROUPS, N_STATE, D)[:, :HEADS_PER_GROUP].reshape(32, D)
    with_dt = jnp.concatenate([g_t[DT_DEV * IN_ROWS:DT_ROW0], dt,
                               g_t[DT_ROW0 + 32 + GATE_SHIFT:(DT_DEV + 1) * IN_ROWS + GATE_SHIFT]], axis=0)
    blocks = []
    for q in range(N_CHIP):
        if 2 * q + 1 < DT_DEV:
            blk = lax.dynamic_slice_in_dim(g_t, IN_ROWS * (2 * q + core), IN_ROWS, axis=0)
        else:
            assert 2 * q == DT_DEV
            after = g_t[(DT_DEV + 1) * IN_ROWS + GATE_SHIFT:(DT_DEV + 2) * IN_ROWS + GATE_SHIFT]
            blk = jnp.where(core == 0, with_dt, after)
        blocks.append(jnp.pad(blk, ((0, slot - IN_ROWS), (0, 0))))
    return jnp.stack(blocks)


def kernel(x, c, w_ada, b_ada, w_in, hgrn_lb, hgrn_gnorm, ssm_conv_w, ssm_conv_b, ssm_dt_bias, ssm_a_log, ssm_d, ssm_norm, w_branch_a, w_branch_b, w_o, ln1_g, ln1_b, w_ffn_gate, w_ffn_up, w_ffn_down, ln2_g, ln2_b, loss_target, m_w_ada, m_b_ada, m_w_in, m_hgrn_lb, m_hgrn_gnorm, m_ssm_conv_w, m_ssm_conv_b, m_ssm_dt_bias, m_ssm_a_log, m_ssm_d, m_ssm_norm, m_w_branch_a, m_w_branch_b, m_w_o, m_ln1_g, m_ln1_b, m_w_ffn_gate, m_w_ffn_up, m_w_ffn_down, m_ln2_g, m_ln2_b, v_w_ada, v_b_ada, v_w_in, v_hgrn_lb, v_hgrn_gnorm, v_ssm_conv_w, v_ssm_conv_b, v_ssm_dt_bias, v_ssm_a_log, v_ssm_d, v_ssm_norm, v_w_branch_a, v_w_branch_b, v_w_o, v_ln1_g, v_ln1_b, v_w_ffn_gate, v_w_ffn_up, v_w_ffn_down, v_ln2_g, v_ln2_b):
    me = 4 * lax.axis_index("x") + 2 * lax.axis_index("y") + lax.axis_index("c")
    xt = x[0]
    tgt = loss_target[0]
    t = xt.shape[0]
    ada_cols = w_ada.shape[2]
    conv_cols = ssm_conv_w.shape[2]

    small_in, _ = _pack([c, ssm_conv_w[0]])
    small_all = allgather_vmem(small_in, "allgather_small_inputs")
    c_all = small_all[:, 0, :D]
    conv_w = small_all[:, 0, D:D + CONV_TAPS * conv_cols].reshape(N_DEV, CONV_TAPS, conv_cols)
    conv_w = conv_w.transpose(1, 0, 2).reshape(CONV_TAPS, CONV_DIM)
    mod = ada_modulation(c_all, w_ada[0], b_ada.reshape(N_DEV, 1, ada_cols))
    mod6 = mod.reshape(6, D)

    shards = [w_in[0].T, w_ffn_gate[0].T, w_ffn_up[0].T, w_ffn_down[0], w_branch_a[0], w_branch_b[0], w_o[0]]
    shard_rows = [s.shape[0] for s in shards]
    slot_rows = [-(-r // 32) * 32 for r in shard_rows]
    row_offs = [sum(slot_rows[:i]) for i in range(len(shards))]
    padded = [jnp.pad(s.astype(BF16), ((0, p - r), (0, 0))) for s, r, p in zip(shards, shard_rows, slot_rows)]
    w_in_t = assemble_w_in(allgather_hbm(padded[0], "allgather_w_in"))

    lb = lower_bound_fwd(hgrn_lb)
    u1 = ln_modulate(xt, mod6, 0, 1, "ln_modulate_1")
    proj, g_rest = mm_nt_gather(u1, w_in_t, F32, jnp.concatenate(padded[1:], axis=0), "mm_in_proj")
    assert shard_rows[1:4] == slot_rows[1:4] == [FF_SHARD_ROWS] * 3
    gate_slot, up_slot, dn_slot = 0, 1, 2
    g_ba, g_bb, g_o = (g_rest[:, o - slot_rows[0]:o - slot_rows[0] + r] for o, r in zip(row_offs[4:], shard_rows[4:]))
    w_ba = g_ba.reshape(D, D)
    w_bb = g_bb.reshape(B_INNER, D)
    w_oo = g_o.reshape(D, D)
    o_a, o_raw, st_a = hgrn_fwd(proj, lb, hgrn_gnorm)
    xc, conv_slope = conv_fwd(proj, conv_w, ssm_conv_b)
    pad3 = ((0, 0), (0, 0), (0, N_STATE - HEADS_PER_GROUP))
    alog4 = jnp.pad(ssm_a_log.reshape(N_GROUPS, 1, HEADS_PER_GROUP), pad3)
    bias4 = jnp.pad(ssm_dt_bias.reshape(N_GROUPS, 1, HEADS_PER_GROUP), pad3)
    dskip4 = jnp.pad(ssm_d.reshape(N_GROUPS, 1, HEADS_PER_GROUP), pad3)
    expand = _head_expand()
    o_b, st_b = ssd_fwd(proj, xc, alog4, bias4, dskip4, ssm_norm, expand)
    ya, yb, merged, h1, x1, u2 = mixer_tail(o_a, o_b, w_ba, w_bb, proj, w_oo, xt, mod6, ln1_g, ln1_b)
    gu, act = ffn_in_act(u2, g_rest, gate_slot, up_slot)

    dh2, dx1_part, acc4 = ffn_tail_loss_bwd(x1, act, g_rest, dn_slot, mod6, ln2_g, ln2_b, tgt)
    g_dn = mm_tn(act, dh2, "mm_grad_ffn_down")
    dgu = ffn_act_bwd(dh2, g_rest, dn_slot, gu)
    g_gu_t = mm_tn(dgu, u2, "mm_grad_ffn_in")
    dh1, dx_part, acc2 = mixer_tail_bwd(x1, dgu, g_rest, gate_slot, up_slot, mod6, dx1_part, xt, h1, ln1_g, ln1_b)
    g_o = mm_tn(merged, dh1, "mm_grad_out_proj")
    dya, dyb, dproj = merge_gates_bwd(dh1, w_oo, ya, yb, proj)
    g_ba_full = mm_tn(o_a, dya, "mm_grad_branch_a")
    g_bb_full = mm_tn(o_b, dyb, "mm_grad_branch_b")
    my_core = lax.axis_index("c")

    def by_core(blocks, rows, slots):
        contrib = jnp.concatenate([jnp.pad(b.reshape(N_DEV, -1, D), ((0, 0), (0, p - r), (0, 0)))
                                   for b, r, p in zip(blocks, rows, slots)], axis=1)
        split = contrib.reshape(N_CHIP, 2, contrib.shape[1], D).transpose(1, 0, 2, 3)
        return (lax.dynamic_index_in_dim(split, my_core, 0, keepdims=False),
                lax.dynamic_index_in_dim(split, 1 - my_core, 0, keepdims=False))

    keep_e, give_e = by_core([g_gu_t[:D_FF], g_gu_t[D_FF:], g_dn, g_ba_full, g_bb_full, g_o],
                             shard_rows[1:], slot_rows[1:])
    dproj, dlb, dgn, got_e = hgrn_bwd(proj, lb, hgrn_gnorm, o_raw, dya, w_ba, st_a, give_e, dproj)
    chip_e = sum_pair(keep_e.reshape(-1, D), got_e.reshape(-1, D), "sum_grads_rest_chip").reshape(keep_e.shape)
    dxs, dbm, dcm, dproj, ddt, dwn, dalog, dbias, ddsk, parts_e = ssd_bwd(proj, xc, alog4, bias4, dskip4, ssm_norm,
                                                                          expand, dyb, w_bb, st_b, chip_e, dproj)
    dproj, dcw, dcb = conv_bwd(proj, dxs, dbm, dcm, conv_slope, conv_w, dproj)
    dproj = dt_fill(ddt, dproj)
    g_in_t = mm_tn(dproj, u1, "mm_grad_in_proj")
    keep_l = _grad_in_blocks(g_in_t, my_core, slot_rows[0])
    give_l = _grad_in_blocks(g_in_t, 1 - my_core, slot_rows[0])
    got_l = exchange_sibling(give_l, "exchange_grad_in_sibling")
    chip_l = sum_pair(keep_l.reshape(-1, D), got_l.reshape(-1, D), "sum_grad_in_chip").reshape(keep_l.shape)
    du1, parts_l = mm_nn_exchange(dproj, w_in_t, F32, chip_l, "mm_du1")
    dx, acc1 = ln_modulate_bwd(xt, du1, mod6, 1, dx_part, "ln_modulate_1_bwd")
    gw_in = sum_parts(parts_l, "sum_grad_in")[:shard_rows[0]].T
    g_rows = sum_parts(parts_e, "sum_grads_rest")
    gw_fg, gw_fu, gw_fd, gw_ba, gw_bb, gw_o = (g_rows[o - slot_rows[0]:o - slot_rows[0] + r]
                                               for o, r in zip(row_offs[1:], shard_rows[1:]))
    gw_fg, gw_fu = gw_fg.T, gw_fu.T

    dmod = jnp.concatenate([acc1[1:2], acc1[0:1], acc2[2:3], acc2[1:2], acc2[0:1], acc4[0:1]], axis=1)
    small_fields = [dmod, acc4[3:4, :128], dlb, dgn, dcw[:CONV_TAPS], dcb, dbias, dalog, ddsk, dwn,
                    acc2[3:4], acc2[4:5], acc4[1:2], acc4[2:3]]
    small_out, offs = _pack(small_fields)
    small_sum_in = allgather_vmem(small_out, "allgather_small_grads")
    gsum, g_lb = reduce_small(small_sum_in, hgrn_lb, offs[2])
    (g_bada, loss_row, _, g_gn, g_cw_full, g_cb, g_bias4, g_alog4, g_dsk4, g_wn, g_l1g, g_l1b, g_l2g, g_l2b) = _unpack(
        gsum, offs, [(1, 6 * D), (1, 128), (1, D), (1, HK), (CONV_TAPS, CONV_DIM), (1, CONV_DIM),
                     (N_GROUPS, N_STATE), (N_GROUPS, N_STATE), (N_GROUPS, N_STATE), (1, B_INNER),
                     (1, D), (1, D), (1, D), (1, D)])
    loss = loss_row[0, 0]
    g_cw = lax.dynamic_slice(g_cw_full, (0, me * conv_cols), (CONV_TAPS, conv_cols))[None]
    g_dtb = g_bias4[:, :HEADS_PER_GROUP].reshape(1, 32)
    g_alog = g_alog4[:, :HEADS_PER_GROUP].reshape(1, 32)
    g_dsk = g_dsk4[:, :HEADS_PER_GROUP].reshape(1, 32)

    dmod_all = small_sum_in[:, 0, offs[0]:offs[0] + 6 * D]
    dmod_cols = lax.dynamic_slice(dmod_all, (0, me * ada_cols), (N_DEV, ada_cols))
    gw_ada = ada_weight_grad(c_all, dmod_cols)

    big = [("ada", w_ada[0], gw_ada, m_w_ada[0], v_w_ada[0]), ("in", w_in[0], gw_in, m_w_in[0], v_w_in[0]),
           ("branch_a", w_branch_a[0], gw_ba, m_w_branch_a[0], v_w_branch_a[0]),
           ("branch_b", w_branch_b[0], gw_bb, m_w_branch_b[0], v_w_branch_b[0]),
           ("o", w_o[0], gw_o, m_w_o[0], v_w_o[0]),
           ("ffn_gate", w_ffn_gate[0], gw_fg, m_w_ffn_gate[0], v_w_ffn_gate[0]),
           ("ffn_up", w_ffn_up[0], gw_fu, m_w_ffn_up[0], v_w_ffn_up[0]),
           ("ffn_down", w_ffn_down[0], gw_fd, m_w_ffn_down[0], v_w_ffn_down[0])]
    big_out = {}
    for nm, w_, g_, m_, v_ in big:
        d_, m2_, v2_ = adamw(w_, g_, m_, v_, "adamw_" + nm)
        big_out[nm] = (g_[None], d_[None], m2_[None], v2_[None])

    small_w = [b_ada, hgrn_lb, hgrn_gnorm, ssm_conv_w, ssm_conv_b, ssm_dt_bias, ssm_a_log, ssm_d, ssm_norm,
               ln1_g, ln1_b, ln2_g, ln2_b]
    small_g = [g_bada, g_lb, g_gn, g_cw, g_cb, g_dtb, g_alog, g_dsk, g_wn, g_l1g, g_l1b, g_l2g, g_l2b]
    small_m = [m_b_ada, m_hgrn_lb, m_hgrn_gnorm, m_ssm_conv_w, m_ssm_conv_b, m_ssm_dt_bias, m_ssm_a_log, m_ssm_d,
               m_ssm_norm, m_ln1_g, m_ln1_b, m_ln2_g, m_ln2_b]
    small_v = [v_b_ada, v_hgrn_lb, v_hgrn_gnorm, v_ssm_conv_w, v_ssm_conv_b, v_ssm_dt_bias, v_ssm_a_log, v_ssm_d,
               v_ssm_norm, v_ln1_g, v_ln1_b, v_ln2_g, v_ln2_b]
    shapes = [a.shape for a in small_w]
    small_g = [g_.reshape(s) for g_, s in zip(small_g, shapes)]
    pw, poffs = _pack(small_w)
    pg, _ = _pack(small_g)
    pm, _ = _pack(small_m)
    pv, _ = _pack(small_v)
    pd, pm2, pv2 = adamw(pw, pg, pm, pv, "adamw_small")
    s_d, s_m, s_v = (_unpack(p, poffs, shapes) for p in (pd, pm2, pv2))
    (sn_bada, sn_lb, sn_gn, sn_cw, sn_cb, sn_dtb, sn_alog, sn_dsk, sn_wn, sn_l1g, sn_l1b, sn_l2g, sn_l2b) = range(13)

    def order(kind):
        sm = [small_g, s_d, s_m, s_v][kind]
        bg = lambda nm: big_out[nm][kind]
        return [bg("ada"), sm[sn_bada], bg("in"), sm[sn_lb], sm[sn_gn], sm[sn_cw], sm[sn_cb], sm[sn_dtb], sm[sn_alog],
                sm[sn_dsk], sm[sn_wn], bg("branch_a"), bg("branch_b"), bg("o"), sm[sn_l1g], sm[sn_l1b],
                bg("ffn_gate"), bg("ffn_up"), bg("ffn_down"), sm[sn_l2g], sm[sn_l2b]]

    return (loss, dx[None], *order(0), *order(1), *order(2), *order(3))
```

```python
import numpy as np
import jax
import jax.numpy as jnp
from jax import lax
from jax.experimental import pallas as pl
from jax.experimental.pallas import tpu as pltpu

F32 = jnp.float32
BF16 = jnp.bfloat16
HI = lax.Precision.HIGHEST

N_DEV = 8
D = 1024
N_HEADS_A = 8
HK = 128
CHUNK = 64
SSD_CHUNK = 128
SSD_CHUNK_BWD = 256
N_GROUPS = 4
HEADS_PER_GROUP = 8
HEAD_P = 64
N_STATE = 128
GROUP_W = HEADS_PER_GROUP * HEAD_P
B_INNER = 2048
CONV_DIM = 3072
D_FF = 2816
IN_DIM = 11296
N_PROJ = 12288
ALPHA = 2.0 ** 0.25
LN_EPS = 1e-5
RMS_EPS = 1e-6
Q_SCALE = 128 ** -0.5
EXP_CLIP = 80.0
ADAM_LR, ADAM_B1, ADAM_B2, ADAM_EPS, ADAM_WD, ADAM_STEP = 0.001, 0.9, 0.999, 1e-8, 0.01, 10
VMEM_LIMIT = 48 * 1024 * 1024
TOKEN_BLOCK = 1024
ROW_TILE = 512
WIDE_ROW_TILE = 1024
MM_ROW_TILE = 1024
MM_TOKEN_TILE = 4096
MM_K_TILE = 3072
MM_COL_TILE = 1408
HGRN_HEADS_PER_STEP = 4
CHUNK_UNROLL = 8
MESH_ID = pl.DeviceIdType.MESH

NT_DIMS = (((1,), (1,)), ((), ()))
TN_DIMS = (((0,), (0,)), ((), ()))


def _cparams(sem=None):
    return pltpu.CompilerParams(dimension_semantics=sem, vmem_limit_bytes=VMEM_LIMIT)


def _sigmoid(x):
    return 1.0 / (1.0 + jnp.exp(-x))


def _dsilu(x, s):
    return s * (1.0 + x * (1.0 - s))


def _nt(a, b, precision=None):
    return lax.dot_general(a, b, NT_DIMS, precision=precision, preferred_element_type=F32)


def _tn(a, b, precision=None):
    return lax.dot_general(a, b, TN_DIMS, precision=precision, preferred_element_type=F32)


def _nn(a, b, precision=None):
    return jnp.dot(a, b, precision=precision, preferred_element_type=F32)


def _split(x, pieces):
    out = []
    for i in range(pieces):
        p = x.astype(BF16)
        out.append(p)
        if i + 1 < pieces:
            x = x - p.astype(F32)
    return out


def _sel(dot, x, sel01, pieces, x_first=True):
    acc = None
    for p in _split(x, pieces):
        term = dot(p, sel01) if x_first else dot(sel01, p)
        acc = term if acc is None else acc + term
    return acc


def _ln(x):
    mu = jnp.mean(x, axis=-1, keepdims=True)
    xc = x - mu
    rstd = lax.rsqrt(jnp.mean(xc * xc, axis=-1, keepdims=True) + LN_EPS)
    return xc * rstd, rstd


def _ln_bwd(dxh, xh, rstd):
    return rstd * (dxh - jnp.mean(dxh, axis=-1, keepdims=True) - xh * jnp.mean(dxh * xh, axis=-1, keepdims=True))


def _colsum(x):
    return jnp.sum(x, axis=0, keepdims=True)


def _tri(n, upper=False):
    r = lax.broadcasted_iota(jnp.int32, (n, n), 0)
    c = lax.broadcasted_iota(jnp.int32, (n, n), 1)
    return (c >= r) if upper else (r >= c)


def _my_pos():
    return lax.axis_index("x"), lax.axis_index("y"), lax.axis_index("c")


def _peer(pos, k):
    x, y, c = pos
    return (x ^ ((k >> 2) & 1), y ^ ((k >> 1) & 1), c ^ (k & 1))


def _flat(pos):
    return 4 * pos[0] + 2 * pos[1] + pos[2]


def allgather_vmem(v, name):
    n = v.shape[1]

    def body(v_ref, o_ref, send_sems, recv_sems, local_sem):
        me = _my_pos()
        mine = pltpu.make_async_copy(v_ref, o_ref.at[_flat(me)], local_sem)
        mine.start()
        sends = []
        for k in range(1, N_DEV):
            peer = _peer(me, k)
            cp = pltpu.make_async_remote_copy(v_ref, o_ref.at[_flat(me)], send_sems.at[k - 1], recv_sems.at[k - 1],
                                              device_id=peer, device_id_type=MESH_ID)
            cp.start()
            sends.append(cp)
        for k in range(1, N_DEV):
            peer = _peer(me, k)
            pltpu.make_async_remote_copy(v_ref, o_ref.at[_flat(peer)], send_sems.at[k - 1], recv_sems.at[k - 1],
                                         device_id=peer, device_id_type=MESH_ID).wait_recv()
        for cp in sends:
            cp.wait_send()
        mine.wait()

    return pl.pallas_call(
        body, name=name,
        out_shape=jax.ShapeDtypeStruct((N_DEV, 1, n), F32),
        in_specs=[pl.BlockSpec(memory_space=pltpu.VMEM)],
        out_specs=pl.BlockSpec(memory_space=pltpu.VMEM),
        scratch_shapes=[pltpu.SemaphoreType.DMA((N_DEV - 1,)), pltpu.SemaphoreType.DMA((N_DEV - 1,)),
                        pltpu.SemaphoreType.DMA],
        compiler_params=_cparams(),
    )(v)


def ada_modulation(c_all, w_ada_s, b_ada_r):
    ncol = w_ada_s.shape[1]

    def body(c_ref, w_ref, b_ref, o_ref, part_ref, send_sems, recv_sems):
        me = _my_pos()
        cval = c_ref[...]
        cond = cval * _sigmoid(cval)
        part = _nn(cond, w_ref[...], HI)
        for r in range(N_DEV):
            part_ref[r] = part[r:r + 1, :]
        sends = []
        for k in range(1, N_DEV):
            peer = _peer(me, k)
            cp = pltpu.make_async_remote_copy(part_ref.at[_flat(peer)], o_ref.at[_flat(me)], send_sems.at[k - 1],
                                              recv_sems.at[k - 1], device_id=peer, device_id_type=MESH_ID)
            cp.start()
            sends.append(cp)
        o_ref[_flat(me)] = part_ref[_flat(me)]
        for k in range(1, N_DEV):
            peer = _peer(me, k)
            pltpu.make_async_remote_copy(part_ref.at[_flat(peer)], o_ref.at[_flat(peer)], send_sems.at[k - 1],
                                         recv_sems.at[k - 1], device_id=peer, device_id_type=MESH_ID).wait_recv()
        for cp in sends:
            cp.wait_send()
        o_ref[...] = o_ref[...] + b_ref[...]

    return pl.pallas_call(
        body, name="ada_modulation",
        out_shape=jax.ShapeDtypeStruct((N_DEV, 1, ncol), F32),
        in_specs=[pl.BlockSpec(memory_space=pltpu.VMEM)] * 3,
        out_specs=pl.BlockSpec(memory_space=pltpu.VMEM),
        scratch_shapes=[pltpu.VMEM((N_DEV, 1, ncol), F32), pltpu.SemaphoreType.DMA((N_DEV - 1,)),
                        pltpu.SemaphoreType.DMA((N_DEV - 1,))],
        compiler_params=_cparams(),
    )(c_all, w_ada_s, b_ada_r)


def allgather_hbm(shard, name):
    def body(x_ref, out_ref, send_sems, recv_sems, local_sem):
        x, y, c = _my_pos()
        me, sibling = (x, y, c), (x, y, 1 - c)
        chips = [(1 - x, y), (x, 1 - y), (1 - x, 1 - y)]

        def slot(pos):
            return out_ref.at[_flat(pos)]

        def copy(k, block, to, src=None):
            return pltpu.make_async_remote_copy(slot(block) if src is None else src, slot(block), send_sems.at[k],
                                                recv_sems.at[k], device_id=to, device_id_type=MESH_ID)

        mine = pltpu.make_async_copy(x_ref, slot(me), local_sem)
        mine.start()
        first = [copy(0, me, sibling, src=x_ref)]
        first += [copy(1 + j, me, (*chip, c), src=x_ref) for j, chip in enumerate(chips)]
        for cp in first:
            cp.start()
        passed = [copy(4 + j, (*chip, c), sibling) for j, chip in enumerate(chips)]
        for j, chip in enumerate(chips):
            copy(1 + j, (*chip, c), me).wait_recv()
            passed[j].start()
        copy(0, sibling, me).wait_recv()
        for j, chip in enumerate(chips):
            copy(4 + j, (*chip, 1 - c), me).wait_recv()
        for cp in first + passed:
            cp.wait_send()
        mine.wait()

    return pl.pallas_call(
        body, name=name,
        out_shape=jax.ShapeDtypeStruct((N_DEV,) + shard.shape, shard.dtype),
        in_specs=[pl.BlockSpec(memory_space=pl.ANY)],
        out_specs=pl.BlockSpec(memory_space=pl.ANY),
        scratch_shapes=[pltpu.SemaphoreType.DMA((N_DEV - 1,)), pltpu.SemaphoreType.DMA((N_DEV - 1,)),
                        pltpu.SemaphoreType.DMA],
        compiler_params=_cparams(),
    )(shard)


N_CHIP = N_DEV // 2
SIBLING_SEMS = [pltpu.SemaphoreType.DMA, pltpu.SemaphoreType.DMA]
CHIP_SEMS = [pltpu.SemaphoreType.DMA((N_CHIP - 1,)), pltpu.SemaphoreType.DMA((N_CHIP - 1,)), pltpu.SemaphoreType.DMA]


def _sibling_exchange(s_ref, o_ref, send_sem, recv_sem):
    x, y, c = _my_pos()
    cp = pltpu.make_async_remote_copy(s_ref, o_ref, send_sem, recv_sem, device_id=(x, y, 1 - c), device_id_type=MESH_ID)
    return cp.start, cp.wait


def _chip_exchange(p_ref, o_ref, send_sems, recv_sems, local_sem):
    x, y, c = _my_pos()
    my_chip = 2 * x + y
    mine = pltpu.make_async_copy(p_ref.at[my_chip], o_ref.at[my_chip], local_sem)
    peers = [(x ^ (k >> 1), y ^ (k & 1)) for k in range(1, N_CHIP)]
    sends = [pltpu.make_async_remote_copy(p_ref.at[2 * px + py], o_ref.at[my_chip], send_sems.at[k], recv_sems.at[k],
                                          device_id=(px, py, c), device_id_type=MESH_ID)
             for k, (px, py) in enumerate(peers)]
    recvs = [pltpu.make_async_remote_copy(p_ref.at[2 * px + py], o_ref.at[2 * px + py], send_sems.at[k], recv_sems.at[k],
                                          device_id=(px, py, c), device_id_type=MESH_ID)
             for k, (px, py) in enumerate(peers)]

    def start():
        mine.start()
        for cp in sends:
            cp.start()

    def wait():
        for cp in recvs:
            cp.wait_recv()
        for cp in sends:
            cp.wait_send()
        mine.wait()

    return start, wait


def exchange_sibling(send, name):
    def body(s_ref, o_ref, send_sem, recv_sem):
        start, wait = _sibling_exchange(s_ref, o_ref, send_sem, recv_sem)
        start()
        wait()

    return pl.pallas_call(
        body, name=name,
        out_shape=jax.ShapeDtypeStruct(send.shape, send.dtype),
        in_specs=[pl.BlockSpec(memory_space=pl.ANY)],
        out_specs=pl.BlockSpec(memory_space=pl.ANY),
        scratch_shapes=SIBLING_SEMS,
        compiler_params=_cparams(),
    )(send)


LANES = 128


def _k_tile(kdim, unit=LANES):
    for cand in range(MM_K_TILE - MM_K_TILE % unit, 0, -unit):
        if kdim % cand == 0:
            return cand
    return kdim


def _lane_tile(n, cap):
    for cand in range(cap - cap % LANES, 0, -LANES):
        if n % cand == 0:
            return cand
    return n


def _m_tile(m, kdim):
    return min(MM_ROW_TILE if kdim > D else 2 * MM_ROW_TILE, m)


def mm_nn_exchange(a, b, out_dtype, part, name):
    kblocks, m, kb = a.shape
    kdim = kblocks * kb
    n = b.shape[1]
    tm, tn, tk = min(MM_ROW_TILE, m), _lane_tile(n, MM_COL_TILE), _k_tile(kdim)
    gn, gm, nk = n // tn, m // tm, kdim // tk
    per_step = tk // kb

    def body(a_ref, b_ref, part_ref, o_ref, parts_ref, acc_ref, send_sems, recv_sems, local_sem):
        j, i, k = pl.program_id(0), pl.program_id(1), pl.program_id(2)
        xchg_start, xchg_wait = _chip_exchange(part_ref, parts_ref, send_sems, recv_sems, local_sem)

        @pl.when((j == 0) & (i == 0) & (k == 0))
        def _():
            xchg_start()

        p = _nn(a_ref[0], b_ref[0:kb, :])
        for c in range(1, per_step):
            p = p + _nn(a_ref[c], b_ref[c * kb:(c + 1) * kb, :])

        @pl.when(k == 0)
        def _():
            acc_ref[...] = p

        @pl.when(k > 0)
        def _():
            acc_ref[...] += p

        @pl.when(k == nk - 1)
        def _():
            o_ref[...] = acc_ref[...].astype(o_ref.dtype)

        @pl.when((j == gn - 1) & (i == gm - 1) & (k == nk - 1))
        def _():
            xchg_wait()

    hbm = pl.BlockSpec(memory_space=pl.ANY)
    return pl.pallas_call(
        body, name=name, grid=(gn, gm, nk),
        out_shape=[jax.ShapeDtypeStruct((m, n), out_dtype), jax.ShapeDtypeStruct(part.shape, part.dtype)],
        in_specs=[pl.BlockSpec((per_step, tm, kb), lambda j, i, k: (k, i, 0)),
                  pl.BlockSpec((tk, tn), lambda j, i, k: (k, j)), hbm],
        out_specs=[pl.BlockSpec((tm, tn), lambda j, i, k: (i, j)), hbm],
        scratch_shapes=[pltpu.VMEM((tm, tn), F32)] + CHIP_SEMS,
        compiler_params=_cparams(("arbitrary", "arbitrary", "arbitrary")),
    )(a, b, part)


def mm_nt_gather(a, b, out_dtype, shard, name):
    m, kdim = a.shape
    n = b.shape[0]
    tm, tn = _m_tile(m, kdim), 1024
    assert kdim == 1024
    gj = m // tm
    nsteps = (n // tn) * gj
    forward_step = max(nsteps - 2, 0)

    def body(a_ref, b_ref, x_ref, o_ref, g_ref, send_sems, recv_sems, local_sem):
        step = pl.program_id(0) * gj + pl.program_id(1)
        x, y, c = _my_pos()
        me, sibling = (x, y, c), (x, y, 1 - c)
        chips = [(1 - x, y), (x, 1 - y), (1 - x, 1 - y)]

        def slot(pos):
            return g_ref.at[_flat(pos)]

        def copy(k, block, to, src=None):
            return pltpu.make_async_remote_copy(slot(block) if src is None else src, slot(block), send_sems.at[k],
                                                recv_sems.at[k], device_id=to, device_id_type=MESH_ID)

        mine = pltpu.make_async_copy(x_ref, slot(me), local_sem)
        first = [copy(0, me, sibling, src=x_ref)]
        first += [copy(1 + j, me, (*chip, c), src=x_ref) for j, chip in enumerate(chips)]
        passed = [copy(4 + j, (*chip, c), sibling) for j, chip in enumerate(chips)]

        @pl.when(step == 0)
        def _():
            mine.start()
            for cp in first:
                cp.start()

        rows = pl.ds(pl.multiple_of(pl.program_id(1) * tm, tm), tm)
        o_ref[...] = _nt(a_ref[rows, :], b_ref[...]).astype(o_ref.dtype)

        @pl.when(step == forward_step)
        def _():
            for j, chip in enumerate(chips):
                copy(1 + j, (*chip, c), me).wait_recv()
                passed[j].start()

        @pl.when(step == nsteps - 1)
        def _():
            copy(0, sibling, me).wait_recv()
            for j, chip in enumerate(chips):
                copy(4 + j, (*chip, 1 - c), me).wait_recv()
            for cp in first + passed:
                cp.wait_send()
            mine.wait()

    return pl.pallas_call(
        body, name=name, grid=(n // tn, gj),
        out_shape=[jax.ShapeDtypeStruct((m, n), out_dtype), jax.ShapeDtypeStruct((N_DEV,) + shard.shape, shard.dtype)],
        in_specs=[pl.BlockSpec(memory_space=pltpu.VMEM), pl.BlockSpec((tn, kdim), lambda j, i: (j, 0)),
                  pl.BlockSpec(memory_space=pl.ANY)],
        out_specs=[pl.BlockSpec((tm, tn), lambda j, i: (i, j)), pl.BlockSpec(memory_space=pl.ANY)],
        scratch_shapes=[pltpu.SemaphoreType.DMA((N_DEV - 1,)), pltpu.SemaphoreType.DMA((N_DEV - 1,)),
                        pltpu.SemaphoreType.DMA],
        compiler_params=_cparams(("arbitrary", "arbitrary")),
    )(a, b, shard)


def mm_tn(a, b, name):
    tt, tn = min(MM_TOKEN_TILE, b.shape[0]), _lane_tile(b.shape[1], MM_COL_TILE)
    tka = _lane_tile(a.shape[-1], 1024)
    if a.ndim == 3:
        t, ka = a.shape[1], a.shape[0] * a.shape[2]
        per = a.shape[2] // tka
        a_spec = pl.BlockSpec((None, tt, tka), lambda i, j, s: (i // per, s, i % per))
    else:
        t, ka = a.shape
        a_spec = pl.BlockSpec((tt, tka), lambda i, j, s: (s, i))
    n = b.shape[1]
    nt = t // tt

    def body(a_ref, b_ref, o_ref, *acc):
        p = _tn(a_ref[...], b_ref[...])
        if nt == 1:
            o_ref[...] = p.astype(o_ref.dtype)
        else:
            acc_ref, s = acc[0], pl.program_id(2)

            @pl.when(s == 0)
            def _():
                acc_ref[...] = p

            @pl.when(s > 0)
            def _():
                acc_ref[...] += p

            @pl.when(s == nt - 1)
            def _():
                o_ref[...] = acc_ref[...].astype(o_ref.dtype)

    return pl.pallas_call(
        body, name=name, grid=(ka // tka, n // tn, nt),
        out_shape=jax.ShapeDtypeStruct((ka, n), BF16),
        in_specs=[a_spec, pl.BlockSpec((tt, tn), lambda i, j, s: (s, j))],
        out_specs=pl.BlockSpec((tka, tn), lambda i, j, s: (i, j)),
        scratch_shapes=[] if nt == 1 else [pltpu.VMEM((tka, tn), F32)],
        compiler_params=_cparams(("parallel", "parallel", "arbitrary")),
    )(a, b)


def _tile(t, cap):
    return min(cap, t)


def ln_modulate(x, mod6, shift_row, scale_row, name):
    t = x.shape[0]
    tm = _tile(t, WIDE_ROW_TILE)

    def body(x_ref, mod_ref, o_ref):
        xh, _ = _ln(x_ref[...])
        sc = mod_ref[scale_row:scale_row + 1, :]
        sh = mod_ref[shift_row:shift_row + 1, :]
        o_ref[...] = (xh * (1.0 + sc) + sh).astype(BF16)

    return pl.pallas_call(
        body, name=name, grid=(t // tm,),
        out_shape=jax.ShapeDtypeStruct((t, D), BF16),
        in_specs=[pl.BlockSpec((tm, D), lambda i: (i, 0)), pl.BlockSpec((6, D), lambda i: (0, 0))],
        out_specs=pl.BlockSpec((tm, D), lambda i: (i, 0)),
        compiler_params=_cparams(("parallel",)),
    )(x, mod6)


FF_SHARD_ROWS = D_FF // N_DEV


def _ffn_weight_spec(ndev, slot, index_map):
    return pl.BlockSpec((ndev, FF_SHARD_ROWS, D), lambda *ids: (index_map(*ids), slot, 0))


def ffn_tail_loss_bwd(x1, act, gathered, dn_slot, mod6, ln_g, ln_b, target):
    t = x1.shape[0]
    tm = _tile(t, ROW_TILE)
    kdim = act.shape[1]

    def body(x_ref, a_ref, w_ref, mod_ref, g_ref, b_ref, c_ref, dh_ref, dx_ref, acc_ref):
        @pl.when(pl.program_id(0) == 0)
        def _():
            acc_ref[...] = jnp.zeros_like(acc_ref)

        hv = _nn(a_ref[...], w_ref[...].reshape(kdim, D))
        gate = mod_ref[5:6, :]
        rh, rstd = _ln(ALPHA * x_ref[...] + gate * hv)
        lng = g_ref[...]
        diff = rh * lng + b_ref[...] - c_ref[...]
        dxo = diff * (1.0 / D)
        lsum = jnp.sum(_colsum(diff * diff), axis=-1, keepdims=True) * (0.5 / D)
        acc_ref[3:4, :] += jnp.broadcast_to(lsum, (1, D))
        acc_ref[1:2, :] += _colsum(dxo * rh)
        acc_ref[2:3, :] += _colsum(dxo)
        dr = _ln_bwd(dxo * lng, rh, rstd)
        acc_ref[0:1, :] += _colsum(dr * hv)
        dh_ref[...] = (gate * dr).astype(BF16)
        dx_ref[...] = ALPHA * dr

    row = pl.BlockSpec((tm, D), lambda i: (i, 0))
    vec = pl.BlockSpec((1, D), lambda i: (0, 0))
    return pl.pallas_call(
        body, name="ffn_tail_loss_bwd", grid=(t // tm,),
        out_shape=[jax.ShapeDtypeStruct((t, D), BF16), jax.ShapeDtypeStruct((t, D), F32),
                   jax.ShapeDtypeStruct((8, D), F32)],
        in_specs=[row, pl.BlockSpec((tm, kdim), lambda i: (i, 0)), _ffn_weight_spec(N_DEV, dn_slot, lambda i: 0),
                  pl.BlockSpec((6, D), lambda i: (0, 0)), vec, vec, row],
        out_specs=[row, row, pl.BlockSpec((8, D), lambda i: (0, 0))],
        compiler_params=_cparams(("arbitrary",)),
    )(x1, act, gathered, mod6, ln_g, ln_b, target)


def ln_modulate_bwd(x, du, mod6, scale_row, dx_part, name):
    t = x.shape[0]
    tm = _tile(t, ROW_TILE)

    def body(x_ref, du_ref, mod_ref, dp_ref, dx_ref, acc_ref):
        @pl.when(pl.program_id(0) == 0)
        def _():
            acc_ref[...] = jnp.zeros_like(acc_ref)

        xh, rstd = _ln(x_ref[...])
        du_v = du_ref[...]
        sc = mod_ref[scale_row:scale_row + 1, :]
        acc_ref[0:1, :] += _colsum(du_v * xh)
        acc_ref[1:2, :] += _colsum(du_v)
        dx_ref[...] = dp_ref[...] + _ln_bwd(du_v * (1.0 + sc), xh, rstd)

    row = pl.BlockSpec((tm, D), lambda i: (i, 0))
    return pl.pallas_call(
        body, name=name, grid=(t // tm,),
        out_shape=[jax.ShapeDtypeStruct((t, D), F32), jax.ShapeDtypeStruct((8, D), F32)],
        in_specs=[row, row, pl.BlockSpec((6, D), lambda i: (0, 0)), row],
        out_specs=[row, pl.BlockSpec((8, D), lambda i: (0, 0))],
        compiler_params=_cparams(("arbitrary",)),
    )(x, du, mod6, dx_part)


def mixer_tail_bwd(x1, dgu, gathered, gate_slot, up_slot, mod6, dx1_part, x, h, ln_g, ln_b, merged):
    t = x.shape[0]
    tm = _tile(t, ROW_TILE // 2)
    _, _, kb = dgu.shape
    last = t // tm - 1

    def body(x1_ref, a_ref, wg_ref, wu_ref, mod_ref, dp_ref, x_ref, h_ref, g_ref, b_ref, m_ref,
             dh_ref, dx_ref, acc_ref, gw_ref, gw_acc):
        @pl.when(pl.program_id(0) == 0)
        def _():
            acc_ref[...] = jnp.zeros_like(acc_ref)
            gw_acc[...] = jnp.zeros_like(gw_acc)

        du = _nn(a_ref[0], wg_ref[...].reshape(kb, D)) + _nn(a_ref[1], wu_ref[...].reshape(kb, D))
        xh, rstd1 = _ln(x1_ref[...])
        acc_ref[0:1, :] += _colsum(du * xh)
        acc_ref[1:2, :] += _colsum(du)
        dx1 = dp_ref[...] + _ln_bwd(du * (1.0 + mod_ref[4:5, :]), xh, rstd1)
        gate = mod_ref[2:3, :]
        hv = h_ref[...]
        rh, rstd = _ln(ALPHA * x_ref[...] + gate * hv)
        acc_ref[3:4, :] += _colsum(dx1 * rh)
        acc_ref[4:5, :] += _colsum(dx1)
        dr = _ln_bwd(dx1 * g_ref[...], rh, rstd)
        acc_ref[2:3, :] += _colsum(dr * hv)
        dh = (gate * dr).astype(BF16)
        dh_ref[...] = dh
        dx_ref[...] = ALPHA * dr
        gw_acc[...] += _tn(m_ref[...], dh)

        @pl.when(pl.program_id(0) == last)
        def _():
            gw_ref[...] = gw_acc[...].astype(BF16)

    row = pl.BlockSpec((tm, D), lambda i: (i, 0))
    vec = pl.BlockSpec((1, D), lambda i: (0, 0))
    return pl.pallas_call(
        body, name="mixer_tail_bwd", grid=(t // tm,),
        out_shape=[jax.ShapeDtypeStruct((t, D), BF16), jax.ShapeDtypeStruct((t, D), F32),
                   jax.ShapeDtypeStruct((8, D), F32), jax.ShapeDtypeStruct((D, D), BF16)],
        in_specs=[row, pl.BlockSpec((2, tm, kb), lambda i: (0, i, 0)), _ffn_weight_spec(N_DEV, gate_slot, lambda i: 0),
                  _ffn_weight_spec(N_DEV, up_slot, lambda i: 0),
                  pl.BlockSpec((6, D), lambda i: (0, 0)), row, row, row, vec, vec, row],
        out_specs=[row, row, pl.BlockSpec((8, D), lambda i: (0, 0)), pl.BlockSpec((D, D), lambda i: (0, 0))],
        scratch_shapes=[pltpu.VMEM((D, D), F32)],
        compiler_params=_cparams(("arbitrary",)),
    )(x1, dgu, gathered, gathered, mod6, dx1_part, x, h, ln_g, ln_b, merged)


def mixer_tail(o_a, o_b, w_ba, w_bb, proj, w_o, x, mod6, ln_g, ln_b):
    t = o_a.shape[0]
    tm = _tile(t, ROW_TILE // 2)

    def body(oa_ref, ob_ref, wa_ref, wb_ref, ga_ref, gb_ref, w_ref, x_ref, mod_ref, g_ref, b_ref,
             ya_ref, yb_ref, m_ref, h_ref, x1_ref, u2_ref):
        ya = _nn(oa_ref[...], wa_ref[...])
        yb = _nn(ob_ref[...], wb_ref[...])
        ya_ref[...] = ya.astype(BF16)
        yb_ref[...] = yb.astype(BF16)
        merged = (_sigmoid(ga_ref[...]) * ya + _sigmoid(gb_ref[...]) * yb).astype(BF16)
        m_ref[...] = merged
        hv = _nn(merged, w_ref[...])
        h_ref[...] = hv
        rh, _ = _ln(ALPHA * x_ref[...] + mod_ref[2:3, :] * hv)
        x1 = rh * g_ref[...] + b_ref[...]
        x1_ref[...] = x1
        xh, _ = _ln(x1)
        u2_ref[...] = (xh * (1.0 + mod_ref[4:5, :]) + mod_ref[3:4, :]).astype(BF16)

    row = pl.BlockSpec((tm, D), lambda i: (i, 0))
    vec = pl.BlockSpec((1, D), lambda i: (0, 0))
    whole = pl.BlockSpec(memory_space=pltpu.VMEM)
    return pl.pallas_call(
        body, name="mixer_tail", grid=(t // tm,),
        out_shape=[jax.ShapeDtypeStruct((t, D), BF16), jax.ShapeDtypeStruct((t, D), BF16),
                   jax.ShapeDtypeStruct((t, D), BF16), jax.ShapeDtypeStruct((t, D), F32),
                   jax.ShapeDtypeStruct((t, D), F32), jax.ShapeDtypeStruct((t, D), BF16)],
        in_specs=[row, pl.BlockSpec((tm, o_b.shape[1]), lambda i: (i, 0)), whole, whole,
                  pl.BlockSpec((tm, D), lambda i: (i, GATE_BLOCK0)),
                  pl.BlockSpec((tm, D), lambda i: (i, GATE_BLOCK0 + 1)), whole,
                  row, pl.BlockSpec((6, D), lambda i: (0, 0)), vec, vec],
        out_specs=[row] * 6,
        compiler_params=_cparams(("parallel",)),
    )(o_a, o_b, w_ba, w_bb, proj, proj, w_o, x, mod6, ln_g, ln_b)


def merge_gates_bwd(dh, w_o, ya, yb, proj):
    t = ya.shape[0]
    tm = _tile(t, ROW_TILE)

    def body(dh_ref, w_ref, ya_ref, yb_ref, ga_ref, gb_ref, dya_ref, dyb_ref, dp_ref):
        dmv = _nt(dh_ref[...], w_ref[...])
        sa = _sigmoid(ga_ref[...])
        sb = _sigmoid(gb_ref[...])
        dya_ref[...] = (dmv * sa).astype(BF16)
        dyb_ref[...] = (dmv * sb).astype(BF16)
        dp_ref[0] = (dmv * ya_ref[...].astype(F32) * sa * (1.0 - sa)).astype(BF16)
        dp_ref[1] = (dmv * yb_ref[...].astype(F32) * sb * (1.0 - sb)).astype(BF16)

    row = pl.BlockSpec((tm, D), lambda i: (i, 0))
    return pl.pallas_call(
        body, name="merge_gates_bwd", grid=(t // tm,),
        out_shape=[jax.ShapeDtypeStruct((t, D), BF16)] * 2 + [jax.ShapeDtypeStruct((N_PROJ // D, t, D), BF16)],
        in_specs=[row, pl.BlockSpec((D, D), lambda i: (0, 0)), row, row,
                  pl.BlockSpec((tm, D), lambda i: (i, GATE_BLOCK0)),
                  pl.BlockSpec((tm, D), lambda i: (i, GATE_BLOCK0 + 1))],
        out_specs=[row, row, pl.BlockSpec((2, tm, D), lambda i: (GATE_BLOCK0 // 2, i, 0))],
        compiler_params=_cparams(("parallel",)),
    )(dh, w_o, ya, yb, proj, proj)


FF_CHUNK = 1408


def ffn_in_act(u, gathered, gate_slot, up_slot):
    t = u.shape[0]
    tm = _tile(t, ROW_TILE)
    nj = D_FF // FF_CHUNK
    ndev = FF_CHUNK // FF_SHARD_ROWS

    def body(a_ref, bg_ref, bu_ref, gu_ref, act_ref):
        a = a_ref[...]
        g = _nt(a, bg_ref[...].reshape(FF_CHUNK, D))
        up = _nt(a, bu_ref[...].reshape(FF_CHUNK, D))
        gu_ref[0] = g.astype(BF16)
        gu_ref[1] = up.astype(BF16)
        act_ref[...] = (g * _sigmoid(g) * up).astype(BF16)

    return pl.pallas_call(
        body, name="ffn_in_act", grid=(nj, t // tm),
        out_shape=[jax.ShapeDtypeStruct((2, t, D_FF), BF16), jax.ShapeDtypeStruct((t, D_FF), BF16)],
        in_specs=[pl.BlockSpec((tm, D), lambda j, i: (i, 0)), _ffn_weight_spec(ndev, gate_slot, lambda j, i: j),
                  _ffn_weight_spec(ndev, up_slot, lambda j, i: j)],
        out_specs=[pl.BlockSpec((2, tm, FF_CHUNK), lambda j, i: (0, i, j)),
                   pl.BlockSpec((tm, FF_CHUNK), lambda j, i: (i, j))],
        compiler_params=_cparams(("parallel", "parallel")),
    )(u, gathered, gathered)


def ffn_act_bwd(dh, gathered, dn_slot, gu):
    t = dh.shape[0]
    tm = _tile(t, ROW_TILE)
    ndev = FF_CHUNK // FF_SHARD_ROWS

    def body(a_ref, b_ref, gu_ref, o_ref):
        da = _nt(a_ref[...], b_ref[...].reshape(FF_CHUNK, D))
        g = gu_ref[0].astype(F32)
        up = gu_ref[1].astype(F32)
        s = _sigmoid(g)
        o_ref[0] = (da * up * _dsilu(g, s)).astype(BF16)
        o_ref[1] = (da * g * s).astype(BF16)

    blk = pl.BlockSpec((2, tm, FF_CHUNK), lambda j, i: (0, i, j))
    return pl.pallas_call(
        body, name="ffn_act_bwd", grid=(D_FF // FF_CHUNK, t // tm),
        out_shape=jax.ShapeDtypeStruct((2, t, D_FF), BF16),
        in_specs=[pl.BlockSpec((tm, D), lambda j, i: (i, 0)), _ffn_weight_spec(ndev, dn_slot, lambda j, i: j), blk],
        out_specs=blk,
        compiler_params=_cparams(("parallel", "parallel")),
    )(dh, gathered, gu)


def _hgrn_chunk_terms(q, fl, lbv, tril_f):
    sig = _sigmoid(fl)
    f = lbv + (1.0 - lbv) * sig
    lam = jnp.log(f)
    k = 1.0 - f
    sq = _sigmoid(q)
    qt = q * sq * Q_SCALE
    bc = _sel(_nn, lam, tril_f, 3, x_first=False)
    bmid = bc[CHUNK // 2 - 1:CHUNK // 2, :]
    bl = bc[CHUNK - 1:CHUNK, :]
    eq = jnp.exp(jnp.minimum(bc - bmid, EXP_CLIP))
    ek = jnp.exp(jnp.minimum(bmid - bc, EXP_CLIP))
    eb = jnp.exp(bc)
    ekl = jnp.exp(bl - bc)
    ebl = jnp.exp(bl)
    return sig, f, k, sq, qt, eq, ek, eb, ekl, ebl


def hgrn_fwd(proj, lb, gnorm):
    t = proj.shape[0]
    tb = _tile(t, TOKEN_BLOCK)
    ncb = tb // CHUNK

    hps = HGRN_HEADS_PER_STEP
    wide = hps * HK

    def body(q_ref, f_ref, i_ref, g_ref, lb_ref, gn_ref, oa_ref, oraw_ref, st_ref, state):
        @pl.when(pl.program_id(1) == 0)
        def _():
            state[...] = jnp.zeros_like(state)

        gn = gn_ref[...]
        mask = _tri(CHUNK)
        tril_f = mask.astype(BF16)

        def chunk(c, carry):
            sl = pl.ds(pl.multiple_of(c * CHUNK, CHUNK), CHUNK)
            for hh in range(hps):
                ln = slice(hh * HK, (hh + 1) * HK)
                q, fl, v, g = q_ref[sl, ln], f_ref[sl, ln], i_ref[sl, ln], g_ref[sl, ln]
                sig, f, k, sq, qt, eq, ek, eb, ekl, ebl = _hgrn_chunk_terms(q, fl, lb_ref[:, ln], tril_f)
                a = jnp.where(mask, _nt((qt * eq).astype(BF16), (k * ek).astype(BF16)), 0.0)
                st = state[hh]
                st_ref[hh, c] = st
                vb = v.astype(BF16)
                o = _nn(a.astype(BF16), vb) + _nt((qt * eb).astype(BF16), st.astype(BF16))
                state[hh] = st * ebl + _tn(vb, (k * ekl).astype(BF16))
                oraw_ref[sl, ln] = o
                rn = o * lax.rsqrt(jnp.mean(o * o, axis=-1, keepdims=True) + RMS_EPS)
                oa_ref[sl, ln] = (rn * gn * g * _sigmoid(g)).astype(BF16)
            return carry

        lax.fori_loop(0, ncb, chunk, 0, unroll=min(CHUNK_UNROLL, ncb))

    def col(block):
        return pl.BlockSpec((tb, wide), lambda h, j: (j, block * (N_HEADS_A // hps) + h))

    return pl.pallas_call(
        body, name="hgrn_fwd", grid=(N_HEADS_A // hps, t // tb),
        out_shape=[jax.ShapeDtypeStruct((t, D), BF16), jax.ShapeDtypeStruct((t, D), F32),
                   jax.ShapeDtypeStruct((N_HEADS_A, t // CHUNK, HK, HK), F32)],
        in_specs=[col(0), col(1), col(2), col(3), pl.BlockSpec((1, wide), lambda h, j: (0, h)),
                  pl.BlockSpec((1, HK), lambda h, j: (0, 0))],
        out_specs=[pl.BlockSpec((tb, wide), lambda h, j: (j, h)), pl.BlockSpec((tb, wide), lambda h, j: (j, h)),
                   pl.BlockSpec((hps, ncb, HK, HK), lambda h, j: (h, j, 0, 0))],
        scratch_shapes=[pltpu.VMEM((hps, HK, HK), F32)],
        compiler_params=_cparams(("parallel", "arbitrary")),
    )(proj, proj, proj, proj, lb, gnorm)


def hgrn_bwd(proj, lb, gnorm, o_raw, dya, w_ba, states, give, dproj):
    t = proj.shape[0]
    tb = _tile(t, TOKEN_BLOCK)
    ncb = tb // CHUNK
    nb = t // tb
    hps = HGRN_HEADS_PER_STEP
    wide = hps * HK

    def body(q_ref, f_ref, i_ref, g_ref, lb_ref, gn_ref, oraw_ref, dya_ref, wba_ref, st_ref, give_ref, dp_in_ref,
             dp_ref, dlb_ref, dgn_ref, got_ref, dstate, doa_ref, send_sem, recv_sem):
        h, j = pl.program_id(0), pl.program_id(1)
        swap_start, swap_wait = _sibling_exchange(give_ref, got_ref, send_sem, recv_sem)
        doa_ref[...] = _nt(dya_ref[...], wba_ref[...])

        @pl.when((h == 0) & (j == 0))
        def _():
            swap_start()

        @pl.when(j == 0)
        def _():
            dstate[...] = jnp.zeros_like(dstate)
            dlb_ref[...] = jnp.zeros_like(dlb_ref)

        @pl.when((j == 0) & (h == 0))
        def _():
            dgn_ref[...] = jnp.zeros_like(dgn_ref)

        gn = gn_ref[...]
        mask = _tri(CHUNK)
        mask_t = _tri(CHUNK, upper=True)
        tril_f = mask.astype(BF16)
        triu_f = mask_t.astype(BF16)

        def chunk(i, c0):
            c = ncb - 1 - i
            sl = pl.ds(pl.multiple_of(c * CHUNK, CHUNK), CHUNK)
            for hh in range(hps):
                ln = slice(hh * HK, (hh + 1) * HK)
                q, fl, v, g = q_ref[sl, ln], f_ref[sl, ln], i_ref[sl, ln], g_ref[sl, ln]
                lbv = lb_ref[:, ln]
                sig, f, k, sq, qt, eq, ek, eb, ekl, ebl = _hgrn_chunk_terms(q, fl, lbv, tril_f)
                qe = (qt * eq).astype(BF16)
                ke = (k * ek).astype(BF16)
                st32 = st_ref[hh, c]
                st = st32.astype(BF16)
                dst = dstate[hh]
                dstb = dst.astype(BF16)
                o = oraw_ref[sl, ln]
                rstd = lax.rsqrt(jnp.mean(o * o, axis=-1, keepdims=True) + RMS_EPS)
                rn = o * rstd
                sgm = _sigmoid(g)
                sg = g * sgm
                doa_v = doa_ref[sl, ln]
                drn = doa_v * gn * sg
                dgn_ref[...] += _colsum(doa_v * rn * sg)
                dp_ref[3, sl, ln] = (doa_v * rn * gn * _dsilu(g, sgm)).astype(BF16)
                do = rstd * (drn - rn * jnp.mean(drn * rn, axis=-1, keepdims=True))
                dob = do.astype(BF16)
                vb = v.astype(BF16)
                da = jnp.where(mask, _nt(dob, vb), 0.0).astype(BF16)
                da_t = jnp.where(mask_t, _nt(vb, dob), 0.0).astype(BF16)
                a_t = jnp.where(mask_t, _nt(ke, qe), 0.0).astype(BF16)
                kl = (k * ekl).astype(BF16)
                qb = (qt * eb).astype(BF16)
                dq_in = _nn(da, ke)
                dk_in = _nn(da_t, qe)
                dq_out = eb * _nn(dob, st)
                dk_out = ekl * _nn(vb, dstb)
                dqt = eq * dq_in + dq_out
                dk = ek * dk_in + dk_out
                dv = _nn(a_t, dob) + _nt(kl, dstb)
                dstate[hh] = dst * ebl + _tn(dob, qb)
                dbig = qe.astype(F32) * dq_in - ke.astype(F32) * dk_in + qt * dq_out - k * dk_out
                beyond = _colsum(k * dk_out) + ebl * _colsum(dst * st32)
                dlam = _sel(_nn, dbig, triu_f, 3, x_first=False) + beyond
                df = dlam / f - dk
                dp_ref[1, sl, ln] = (df * (1.0 - lbv) * sig * (1.0 - sig)).astype(BF16)
                dlb_ref[:, ln] += _colsum(df * (1.0 - sig))
                dp_ref[0, sl, ln] = (dqt * Q_SCALE * _dsilu(q, sq)).astype(BF16)
                dp_ref[2, sl, ln] = dv.astype(BF16)
            return c0

        lax.fori_loop(0, ncb, chunk, 0, unroll=min(CHUNK_UNROLL, ncb))

        @pl.when((h == N_HEADS_A // hps - 1) & (j == nb - 1))
        def _():
            swap_wait()

    def col(block):
        return pl.BlockSpec((tb, wide), lambda h, j: (nb - 1 - j, block * (N_HEADS_A // hps) + h))

    hcol = pl.BlockSpec((tb, wide), lambda h, j: (nb - 1 - j, h))
    hbm = pl.BlockSpec(memory_space=pl.ANY)
    return pl.pallas_call(
        body, name="hgrn_bwd", grid=(N_HEADS_A // hps, nb),
        out_shape=[jax.ShapeDtypeStruct(dproj.shape, dproj.dtype), jax.ShapeDtypeStruct((1, D), F32),
                   jax.ShapeDtypeStruct((1, HK), F32), jax.ShapeDtypeStruct(give.shape, give.dtype)],
        in_specs=[col(0), col(1), col(2), col(3), pl.BlockSpec((1, wide), lambda h, j: (0, h)),
                  pl.BlockSpec((1, HK), lambda h, j: (0, 0)), hcol,
                  pl.BlockSpec((tb, D), lambda h, j: (nb - 1 - j, 0)), pl.BlockSpec((wide, D), lambda h, j: (h, 0)),
                  pl.BlockSpec((hps, ncb, HK, HK), lambda h, j: (h, nb - 1 - j, 0, 0)), hbm, hbm],
        out_specs=[pl.BlockSpec((4, tb, wide), lambda h, j: (0, nb - 1 - j, h)),
                   pl.BlockSpec((1, wide), lambda h, j: (0, h)), pl.BlockSpec((1, HK), lambda h, j: (0, 0)), hbm],
        input_output_aliases={11: 0},
        scratch_shapes=[pltpu.VMEM((hps, HK, HK), F32), pltpu.VMEM((tb, wide), F32)] + SIBLING_SEMS,
        compiler_params=_cparams(("arbitrary", "arbitrary")),
    )(proj, proj, proj, proj, lb, gnorm, o_raw, dya, w_ba, states, give, dproj)


CONV_BLOCK0 = 6
CONV_TAPS = 4
HALO = 8


def conv_fwd(proj, conv_w, conv_b):
    t = proj.shape[0]
    tm = _tile(t, ROW_TILE)
    r = tm // HALO

    def body(x_ref, halo_ref, w_ref, b_ref, o_ref, ds_ref):
        i = pl.program_id(1)
        halo = jnp.where(i > 0, halo_ref[...], 0.0)
        ext = jnp.concatenate([halo, x_ref[...]], axis=0)
        pre = b_ref[...] + w_ref[CONV_TAPS - 1:CONV_TAPS, :] * ext[HALO:, :]
        for tap in range(CONV_TAPS - 1):
            pre = pre + w_ref[tap:tap + 1, :] * pltpu.roll(ext, CONV_TAPS - 1 - tap, axis=0)[HALO:, :]
        s = _sigmoid(pre)
        o_ref[...] = pre * s
        ds_ref[...] = _dsilu(pre, s).astype(BF16)

    blk = pl.BlockSpec((tm, D), lambda cb, i: (i, cb))
    return pl.pallas_call(
        body, name="conv_fwd", grid=(CONV_DIM // D, t // tm),
        out_shape=[jax.ShapeDtypeStruct((t, CONV_DIM), F32), jax.ShapeDtypeStruct((t, CONV_DIM), BF16)],
        in_specs=[pl.BlockSpec((tm, D), lambda cb, i: (i, CONV_BLOCK0 + cb)),
                  pl.BlockSpec((HALO, D), lambda cb, i: (jnp.maximum(i * r - 1, 0), CONV_BLOCK0 + cb)),
                  pl.BlockSpec((CONV_TAPS, D), lambda cb, i: (0, cb)), pl.BlockSpec((1, D), lambda cb, i: (0, cb))],
        out_specs=[blk, blk],
        compiler_params=_cparams(("parallel", "parallel")),
    )(proj, proj, conv_w, conv_b)


def conv_bwd(proj, dxs, dbm, dcm, dsilu, conv_w, dproj):
    t = proj.shape[0]
    tm = _tile(t, ROW_TILE)
    r = tm // HALO
    n = t // tm
    last_halo = t // HALO - 1
    x_blocks = B_INNER // D
    assert 2 * GROUP_W == D and CONV_DIM == B_INNER + D

    def body(x_ref, prev_ref, gx_ref, gxn_ref, gb_ref, gbn_ref, gc_ref, gcn_ref, s_ref, snext_ref, w_ref, dp_in_ref,
             dx_ref, dw_ref, db_ref):
        i = pl.program_id(1)
        is_x = pl.program_id(0) < x_blocks

        @pl.when(i == 0)
        def _():
            dw_ref[...] = jnp.zeros_like(dw_ref)
            db_ref[...] = jnp.zeros_like(db_ref)

        d = jnp.where(is_x, gx_ref[...], jnp.concatenate([gb_ref[...], gc_ref[...]], axis=1))
        dnext = jnp.where(is_x, gxn_ref[0:HALO, :], jnp.concatenate([gbn_ref[0:HALO, :], gcn_ref[0:HALO, :]], axis=1))
        dpre = jnp.concatenate([d.astype(F32) * s_ref[...].astype(F32),
                                jnp.where(i < n - 1, dnext.astype(F32) * snext_ref[0:HALO, :].astype(F32),
                                          0.0)], axis=0)
        dx = w_ref[CONV_TAPS - 1:CONV_TAPS, :] * dpre[:tm, :]
        for tap in range(CONV_TAPS - 1):
            back = CONV_TAPS - 1 - tap
            dx = dx + w_ref[tap:tap + 1, :] * pltpu.roll(dpre, tm + HALO - back, axis=0)[:tm, :]
        dx_ref[...] = dx.astype(BF16)
        dp = dpre[:tm, :]
        db_ref[...] += _colsum(dp)
        prev = jnp.where(i > 0, prev_ref[...], 0.0)
        ext = jnp.concatenate([prev, x_ref[...]], axis=0)
        dw_ref[CONV_TAPS - 1:CONV_TAPS, :] += _colsum(dp * ext[HALO:, :])
        for tap in range(CONV_TAPS - 1):
            dw_ref[tap:tap + 1, :] += _colsum(dp * pltpu.roll(ext, CONV_TAPS - 1 - tap, axis=0)[HALO:, :])

    def next_rows(i):
        return jnp.minimum((i + 1) * (r // 2), last_halo // 2)

    blk = pl.BlockSpec((tm, D), lambda cb, i: (i, cb))
    nxt = pl.BlockSpec((2 * HALO, D), lambda cb, i: (next_rows(i), cb))
    xblk = pl.BlockSpec((tm, D), lambda cb, i: (i, jnp.minimum(cb, x_blocks - 1)))
    xnxt = pl.BlockSpec((2 * HALO, D), lambda cb, i: (next_rows(i), jnp.minimum(cb, x_blocks - 1)))
    gblk = pl.BlockSpec((tm, GROUP_W), lambda cb, i: (i, 0))
    gnxt = pl.BlockSpec((2 * HALO, GROUP_W), lambda cb, i: (next_rows(i), 0))
    return pl.pallas_call(
        body, name="conv_bwd", grid=(CONV_DIM // D, n),
        out_shape=[jax.ShapeDtypeStruct(dproj.shape, dproj.dtype), jax.ShapeDtypeStruct((8, CONV_DIM), F32),
                   jax.ShapeDtypeStruct((1, CONV_DIM), F32)],
        in_specs=[pl.BlockSpec((tm, D), lambda cb, i: (i, CONV_BLOCK0 + cb)),
                  pl.BlockSpec((HALO, D), lambda cb, i: (jnp.maximum(i * r - 1, 0), CONV_BLOCK0 + cb)),
                  xblk, xnxt, gblk, gnxt, gblk, gnxt, blk, nxt,
                  pl.BlockSpec((CONV_TAPS, D), lambda cb, i: (0, cb)), pl.BlockSpec(memory_space=pl.ANY)],
        out_specs=[pl.BlockSpec((None, tm, D), lambda cb, i: (CONV_BLOCK0 + cb, i, 0)),
                   pl.BlockSpec((8, D), lambda cb, i: (0, cb)), pl.BlockSpec((1, D), lambda cb, i: (0, cb))],
        input_output_aliases={11: 0},
        compiler_params=_cparams(("parallel", "arbitrary")),
    )(proj, proj, dxs, dxs, dbm, dbm, dcm, dcm, dsilu, dsilu, conv_w, dproj)


def dt_fill(ddt, dproj):
    t = ddt.shape[0]
    tm = _tile(t, WIDE_ROW_TILE)
    w = ddt.shape[1]

    def body(d_ref, dp_in_ref, o_ref):
        o_ref[:, :w] = d_ref[...]
        o_ref[:, w:] = jnp.zeros((tm, D - w), o_ref.dtype)

    return pl.pallas_call(
        body, name="dt_fill", grid=(t // tm,),
        out_shape=jax.ShapeDtypeStruct(dproj.shape, dproj.dtype),
        in_specs=[pl.BlockSpec((tm, w), lambda i: (i, 0)), pl.BlockSpec(memory_space=pl.ANY)],
        out_specs=pl.BlockSpec((None, tm, D), lambda i: (DT_COL_BLOCK, i, 0)),
        input_output_aliases={1: 0},
        compiler_params=_cparams(("parallel",)),
    )(ddt, dproj)


Z_BLOCK0 = 8
DT_COL_BLOCK = 9
DT_BLOCK0 = 8 * DT_COL_BLOCK
GATE_BLOCK0 = 10
B_BLOCK0 = 16
C_BLOCK0 = 20


def _head_expand():
    e = np.zeros((N_STATE, GROUP_W), np.float32)
    for hh in range(HEADS_PER_GROUP):
        e[hh, hh * HEAD_P:(hh + 1) * HEAD_P] = 1.0
    return jnp.asarray(e, BF16)


def _ssd_chunk_terms(dt, bias, alog, expand, tril_f, eye):
    dtb = dt + bias
    delta = jnp.maximum(dtb, 0.0) + jnp.log(1.0 + jnp.exp(-jnp.abs(dtb)))
    ea = jnp.exp(alog)
    a = -ea * delta
    acum = _sel(_nn, a, tril_f, 3, x_first=False)
    delta_e = _sel(_nn, delta, expand, 2)
    acum_e = _sel(_nn, acum, expand, 2)
    acum_t = _sel(_nt, acum, eye, 3, x_first=False)
    return dtb, delta, ea, a, acum, delta_e, acum_e, acum_t


def ssd_fwd(proj, xc, alog4, bias4, dskip4, wnorm, expand):
    t = proj.shape[0]
    tb = _tile(t, TOKEN_BLOCK)
    ncb = tb // SSD_CHUNK

    def body(xs_ref, b_ref, c_ref, dt_ref, z_ref, alog_ref, bias_ref, dsk_ref, wn_ref, e_ref, ob_ref, st_ref, state):
        @pl.when(pl.program_id(1) == 0)
        def _():
            state[...] = jnp.zeros_like(state)

        expand = e_ref[...]
        mask = _tri(SSD_CHUNK)
        tril_f = mask.astype(BF16)
        eye = (lax.broadcasted_iota(jnp.int32, (N_STATE, N_STATE), 0) ==
               lax.broadcasted_iota(jnp.int32, (N_STATE, N_STATE), 1)).astype(BF16)
        alog, bias = alog_ref[0], bias_ref[0]
        d_e = _sel(_nn, jnp.broadcast_to(dsk_ref[0], (8, N_STATE)), expand, 3)[0:1, :]
        wn = wn_ref[...]

        def chunk(c, carry):
            sl = pl.ds(pl.multiple_of(c * SSD_CHUNK, SSD_CHUNK), SSD_CHUNK)
            xs, bm, cm, dt, z = xs_ref[sl, :], b_ref[sl, :], c_ref[sl, :], dt_ref[sl, :], z_ref[sl, :]
            dtb, delta, ea, a, acum, delta_e, acum_e, acum_t = _ssd_chunk_terms(dt, bias, alog, expand, tril_f, eye)
            alast_e = acum_e[SSD_CHUNK - 1:SSD_CHUNK, :]
            xd = xs * delta_e
            xdb = xd.astype(BF16)
            cb_, bb_ = cm.astype(BF16), bm.astype(BF16)
            cbm = _nt(cb_, bb_)
            ys = []
            for hh in range(HEADS_PER_GROUP):
                lh = jnp.where(mask, jnp.exp(jnp.minimum(acum[:, hh:hh + 1] - acum_t[hh:hh + 1, :], 0.0)), 0.0)
                ys.append(_nn((cbm * lh).astype(BF16), xdb[:, hh * HEAD_P:(hh + 1) * HEAD_P]))
            st = state[...]
            st_ref[0, c] = st
            y = jnp.concatenate(ys, axis=1) + _nn(cb_, st.astype(BF16)) * jnp.exp(acum_e) + xs * d_e
            state[...] = st * jnp.exp(alast_e) + _tn(bb_, (xd * jnp.exp(alast_e - acum_e)).astype(BF16))
            yg = y * z * _sigmoid(z)
            ob_ref[sl, :] = (yg * lax.rsqrt(jnp.mean(yg * yg, axis=-1, keepdims=True) + RMS_EPS) * wn).astype(BF16)
            return carry

        lax.fori_loop(0, ncb, chunk, 0, unroll=min(CHUNK_UNROLL, ncb))

    small = pl.BlockSpec((1, 1, N_STATE), lambda g, j: (g, 0, 0))
    return pl.pallas_call(
        body, name="ssd_fwd", grid=(N_GROUPS, t // tb),
        out_shape=[jax.ShapeDtypeStruct((t, B_INNER), BF16),
                   jax.ShapeDtypeStruct((N_GROUPS, t // SSD_CHUNK, N_STATE, GROUP_W), F32)],
        in_specs=[pl.BlockSpec((tb, GROUP_W), lambda g, j: (j, g)),
                  pl.BlockSpec((tb, N_STATE), lambda g, j: (j, B_BLOCK0 + g)),
                  pl.BlockSpec((tb, N_STATE), lambda g, j: (j, C_BLOCK0 + g)),
                  pl.BlockSpec((tb, N_STATE), lambda g, j: (j, DT_BLOCK0 + g)),
                  pl.BlockSpec((tb, GROUP_W), lambda g, j: (j, Z_BLOCK0 + g)),
                  small, small, small, pl.BlockSpec((1, GROUP_W), lambda g, j: (0, g)),
                  pl.BlockSpec((N_STATE, GROUP_W), lambda g, j: (0, 0))],
        out_specs=[pl.BlockSpec((tb, GROUP_W), lambda g, j: (j, g)),
                   pl.BlockSpec((1, ncb, N_STATE, GROUP_W), lambda g, j: (g, j, 0, 0))],
        scratch_shapes=[pltpu.VMEM((N_STATE, GROUP_W), F32)],
        compiler_params=_cparams(("parallel", "arbitrary")),
    )(xc, xc, xc, proj, proj, alog4, bias4, dskip4, wnorm, expand)


def ssd_bwd(proj, xc, alog4, bias4, dskip4, wnorm, expand, dyb, w_bb, states, part, dproj):
    t = proj.shape[0]
    tb = _tile(t, TOKEN_BLOCK)
    lc = min(SSD_CHUNK_BWD, tb)
    ncb = tb // lc
    nsaved = tb // SSD_CHUNK
    nb = t // tb

    def body(xs_ref, b_ref, c_ref, dt_ref, z_ref, alog_ref, bias_ref, dsk_ref, wn_ref, e_ref, dyb_ref, wbb_ref, st_ref,
             part_ref, dp_in_ref, dxs_ref, db_ref, dc_ref, dz_ref, ddt_ref, dwn_ref, dalog_ref, dbias_ref, ddsk_ref,
             parts_ref, dstate, dob_ref, send_sems, recv_sems, local_sem):
        xchg_start, xchg_wait = _chip_exchange(part_ref, parts_ref, send_sems, recv_sems, local_sem)
        dob_ref[...] = _nt(dyb_ref[...], wbb_ref[...])

        @pl.when((pl.program_id(0) == 0) & (pl.program_id(1) == 0))
        def _():
            xchg_start()

        @pl.when(pl.program_id(1) == 0)
        def _():
            dstate[...] = jnp.zeros_like(dstate)
            dwn_ref[...] = jnp.zeros_like(dwn_ref)
            dalog_ref[...] = jnp.zeros_like(dalog_ref)
            dbias_ref[...] = jnp.zeros_like(dbias_ref)
            ddsk_ref[...] = jnp.zeros_like(ddsk_ref)

        expand = e_ref[...]
        mask = _tri(lc)
        mask_t = _tri(lc, upper=True)
        tril_f = mask.astype(BF16)
        triu_f = mask_t.astype(BF16)
        eye = (lax.broadcasted_iota(jnp.int32, (N_STATE, N_STATE), 0) ==
               lax.broadcasted_iota(jnp.int32, (N_STATE, N_STATE), 1)).astype(BF16)
        alog, bias = alog_ref[0], bias_ref[0]
        d_e = _sel(_nn, jnp.broadcast_to(dsk_ref[0], (8, N_STATE)), expand, 3)[0:1, :]
        wn = wn_ref[...]

        def chunk(i, c0):
            c = ncb - 1 - i
            sl = pl.ds(pl.multiple_of(c * lc, lc), lc)
            xs, bm, cm, dt, z = xs_ref[sl, :], b_ref[sl, :], c_ref[sl, :], dt_ref[sl, :], z_ref[sl, :]
            dtb, delta, ea, a, acum, delta_e, acum_e, acum_t = _ssd_chunk_terms(dt, bias, alog, expand, tril_f, eye)
            alast_e = acum_e[lc - 1:lc, :]
            eacum = jnp.exp(acum_e)
            wl = jnp.exp(alast_e - acum_e)
            xd = xs * delta_e
            xdb = xd.astype(BF16)
            cb_, bb_ = cm.astype(BF16), bm.astype(BF16)
            cbm = _nt(cb_, bb_)
            st32 = st_ref[0, c * (lc // SSD_CHUNK)]
            stb = st32.astype(BF16)
            dst = dstate[...]
            dstb = dst.astype(BF16)
            lhs, mixes, ys = [], [], []
            for hh in range(HEADS_PER_GROUP):
                col, row = acum[:, hh:hh + 1], acum_t[hh:hh + 1, :]
                lh = jnp.where(mask, jnp.exp(jnp.minimum(col - row, 0.0)), 0.0)
                mix = (cbm * lh).astype(BF16)
                lhs.append(lh)
                mixes.append(mix)
                ys.append(_nn(mix, xdb[:, hh * HEAD_P:(hh + 1) * HEAD_P]))
            y_in = jnp.concatenate(ys, axis=1)
            y_out = _nn(cb_, stb) * eacum
            y = y_in + y_out + xs * d_e
            sgz = _sigmoid(z)
            sz = z * sgz
            yg = y * sz
            rstd = lax.rsqrt(jnp.mean(yg * yg, axis=-1, keepdims=True) + RMS_EPS)
            nrm = yg * rstd
            dob_v = dob_ref[sl, :]
            dn = dob_v * wn
            dwn_ref[...] += _colsum(dob_v * nrm)
            dyg = rstd * (dn - nrm * jnp.mean(dn * nrm, axis=-1, keepdims=True))
            dy = dyg * sz
            dz_ref[sl, :] = (dyg * y * _dsilu(z, sgz)).astype(BF16)
            dyb = dy.astype(BF16)
            dxds = []
            dcb = jnp.zeros((lc, lc), F32)
            for hh in range(HEADS_PER_GROUP):
                hs = slice(hh * HEAD_P, (hh + 1) * HEAD_P)
                dy_h, x_h = dyb[:, hs], xdb[:, hs]
                dxds.append(_tn(mixes[hh], dy_h))
                dcb = dcb + _nt(dy_h, x_h) * lhs[hh]
            dcbb = dcb.astype(BF16)
            dye = (dy * eacum).astype(BF16)
            xw = (xd * wl).astype(BF16)
            dxd_in = jnp.concatenate(dxds, axis=1)
            dxd_out = wl * _nn(bb_, dstb)
            dxd = dxd_in + dxd_out
            dc_ref[sl, :] = (_nn(dcbb, bb_) + _nt(dye, stb)).astype(dc_ref.dtype)
            db_ref[sl, :] = (_tn(dcbb, cb_) + _nt(xw, dstb)).astype(db_ref.dtype)
            dstate[...] = dst * jnp.exp(alast_e) + _tn(cb_, dye)
            col_out = xd * dxd_out
            dac = _sel(_nt, dyb.astype(F32) * y_in - xdb.astype(F32) * dxd_in + dy * y_out - col_out, expand, 2)
            beyond = _colsum(col_out) + jnp.exp(alast_e) * _colsum(dst * st32)
            da = (_sel(_nn, dac, triu_f, 3, x_first=False) +
                  _sel(_nt, jnp.broadcast_to(beyond, (8, GROUP_W)), expand, 3)[0:1, :])
            ddelta = _sel(_nt, dxd * xs, expand, 2) - da * ea
            dalog_ref[0] += _colsum(da * a)
            ddtb = ddelta * _sigmoid(dtb)
            dbias_ref[0] += _colsum(ddtb)
            ddt_ref[sl, :] = ddtb.astype(BF16)
            ddsk_ref[0] += _sel(_nt, jnp.broadcast_to(_colsum(dy * xs), (8, GROUP_W)), expand, 3)[0:1, :]
            dxs_ref[sl, :] = (dxd * delta_e + dy * d_e).astype(dxs_ref.dtype)
            return c0

        lax.fori_loop(0, ncb, chunk, 0, unroll=min(CHUNK_UNROLL, ncb))

        @pl.when((pl.program_id(0) == N_GROUPS - 1) & (pl.program_id(1) == nb - 1))
        def _():
            xchg_wait()

    small = pl.BlockSpec((1, 1, N_STATE), lambda g, j: (g, 0, 0))
    wide = pl.BlockSpec((tb, GROUP_W), lambda g, j: (nb - 1 - j, g))
    narrow = pl.BlockSpec((tb, N_STATE), lambda g, j: (nb - 1 - j, g))
    hbm = pl.BlockSpec(memory_space=pl.ANY)
    return pl.pallas_call(
        body, name="ssd_bwd", grid=(N_GROUPS, nb),
        out_shape=[jax.ShapeDtypeStruct((t, B_INNER), BF16), jax.ShapeDtypeStruct((t, GROUP_W), BF16),
                   jax.ShapeDtypeStruct((t, GROUP_W), BF16), jax.ShapeDtypeStruct(dproj.shape, dproj.dtype),
                   jax.ShapeDtypeStruct((t, GROUP_W), BF16), jax.ShapeDtypeStruct((1, B_INNER), F32),
                   jax.ShapeDtypeStruct((N_GROUPS, 1, N_STATE), F32), jax.ShapeDtypeStruct((N_GROUPS, 1, N_STATE), F32),
                   jax.ShapeDtypeStruct((N_GROUPS, 1, N_STATE), F32), jax.ShapeDtypeStruct(part.shape, part.dtype)],
        in_specs=[wide,
                  pl.BlockSpec((tb, N_STATE), lambda g, j: (nb - 1 - j, B_BLOCK0 + g)),
                  pl.BlockSpec((tb, N_STATE), lambda g, j: (nb - 1 - j, C_BLOCK0 + g)),
                  pl.BlockSpec((tb, N_STATE), lambda g, j: (nb - 1 - j, DT_BLOCK0 + g)),
                  pl.BlockSpec((tb, GROUP_W), lambda g, j: (nb - 1 - j, Z_BLOCK0 + g)),
                  small, small, small, pl.BlockSpec((1, GROUP_W), lambda g, j: (0, g)),
                  pl.BlockSpec((N_STATE, GROUP_W), lambda g, j: (0, 0)),
                  pl.BlockSpec((tb, D), lambda g, j: (nb - 1 - j, 0)), pl.BlockSpec((GROUP_W, D), lambda g, j: (g, 0)),
                  pl.BlockSpec((1, nsaved, N_STATE, GROUP_W), lambda g, j: (g, nb - 1 - j, 0, 0)), hbm, hbm],
        out_specs=[wide, narrow, narrow,
                   pl.BlockSpec((None, tb, GROUP_W), lambda g, j: (Z_BLOCK0 // 2 + g // 2, nb - 1 - j, g % 2)),
                   narrow, pl.BlockSpec((1, GROUP_W), lambda g, j: (0, g)), small, small, small, hbm],
        input_output_aliases={14: 3},
        scratch_shapes=[pltpu.VMEM((N_STATE, GROUP_W), F32), pltpu.VMEM((tb, GROUP_W), F32)] + CHIP_SEMS,
        compiler_params=_cparams(("arbitrary", "arbitrary")),
    )(xc, xc, xc, proj, proj, alog4, bias4, dskip4, wnorm, expand, dyb, w_bb, states, part, dproj)


def lower_bound_fwd(hgrn_lb):
    def body(a_ref, o_ref):
        a0, a1 = a_ref[0:1, :], a_ref[1:2, :]
        m = jnp.maximum(a0, a1)
        e0, e1 = jnp.exp(a0 - m), jnp.exp(a1 - m)
        o_ref[...] = e0 / (e0 + e1)

    return pl.pallas_call(body, name="lower_bound_fwd", out_shape=jax.ShapeDtypeStruct((1, D), F32))(hgrn_lb)


def ada_weight_grad(c_all, dmod_cols):
    def body(c_ref, d_ref, o_ref):
        cval = c_ref[...]
        o_ref[...] = _tn(cval * _sigmoid(cval), d_ref[...], HI)

    return pl.pallas_call(body, name="ada_weight_grad",
                          out_shape=jax.ShapeDtypeStruct((D, dmod_cols.shape[1]), F32))(c_all, dmod_cols)


def reduce_small(gathered, hgrn_lb, dlb_off):
    n = gathered.shape[2]

    def body(g_ref, a_ref, o_ref, glb_ref):
        s = g_ref[0]
        for d in range(1, N_DEV):
            s = s + g_ref[d]
        o_ref[...] = s
        a0, a1 = a_ref[0:1, :], a_ref[1:2, :]
        m = jnp.maximum(a0, a1)
        e0, e1 = jnp.exp(a0 - m), jnp.exp(a1 - m)
        p0 = e0 / (e0 + e1)
        tq = s[:, dlb_off:dlb_off + D] * p0 * (1.0 - p0)
        glb_ref[0:1, :] = tq
        glb_ref[1:2, :] = -tq

    return pl.pallas_call(body, name="reduce_small",
                          out_shape=[jax.ShapeDtypeStruct((1, n), F32), jax.ShapeDtypeStruct((2, D), F32)])(gathered, hgrn_lb)


def _adam_math(w, g, m, v):
    m2 = ADAM_B1 * m + (1.0 - ADAM_B1) * g
    v2 = ADAM_B2 * v + (1.0 - ADAM_B2) * (g * g)
    m_hat = m2 / (1.0 - ADAM_B1 ** ADAM_STEP)
    v_hat = v2 / (1.0 - ADAM_B2 ** ADAM_STEP)
    delta = -ADAM_LR * (m_hat / (jnp.sqrt(v_hat) + ADAM_EPS) + ADAM_WD * w)
    return delta, m2, v2


def _row_tile(rows, mult=8, cap=128):
    for cand in range(cap - cap % mult, 0, -mult):
        if rows % cand == 0:
            return cand
    return rows


def sum_parts(parts, name):
    n, rows, cols = parts.shape
    tr = _row_tile(rows, 16, 1024)

    def body(p_ref, o_ref):
        s = p_ref[0].astype(F32)
        for d in range(1, n):
            s = s + p_ref[d].astype(F32)
        o_ref[...] = s

    return pl.pallas_call(
        body, name=name, grid=(rows // tr,),
        out_shape=jax.ShapeDtypeStruct((rows, cols), F32),
        in_specs=[pl.BlockSpec((n, tr, cols), lambda i: (0, i, 0))],
        out_specs=pl.BlockSpec((tr, cols), lambda i: (i, 0)),
        compiler_params=_cparams(("parallel",)),
    )(parts)


def sum_pair(a, b, name):
    rows, cols = a.shape
    tr = _row_tile(rows, 16, 1024)

    def body(a_ref, b_ref, o_ref):
        o_ref[...] = (a_ref[...].astype(F32) + b_ref[...].astype(F32)).astype(o_ref.dtype)

    blk = pl.BlockSpec((tr, cols), lambda i: (i, 0))
    return pl.pallas_call(
        body, name=name, grid=(rows // tr,),
        out_shape=jax.ShapeDtypeStruct((rows, cols), a.dtype),
        in_specs=[blk, blk], out_specs=blk,
        compiler_params=_cparams(("parallel",)),
    )(a, b)


def adamw(w, g, m, v, name):
    rows, cols = w.shape
    tr = _row_tile(rows, 8, 256)

    def body(w_ref, g_ref, m_ref, v_ref, d_ref, m2_ref, v2_ref):
        delta, m2, v2 = _adam_math(w_ref[...], g_ref[...], m_ref[...], v_ref[...])
        d_ref[...] = delta
        m2_ref[...] = m2
        v2_ref[...] = v2

    blk = pl.BlockSpec((tr, cols), lambda i: (i, 0))
    return pl.pallas_call(
        body, name=name, grid=(rows // tr,),
        out_shape=[jax.ShapeDtypeStruct((rows, cols), F32)] * 3,
        in_specs=[blk] * 4, out_specs=[blk] * 3,
        compiler_params=_cparams(("parallel",)),
    )(w, g, m, v)


def _pad128(n):
    return -(-n // 128) * 128


def _pack(arrays):
    offs, parts, off = [], [], 0
    for a in arrays:
        flat = a.reshape(1, -1)
        n = flat.shape[1]
        offs.append(off)
        parts.append(jnp.pad(flat, ((0, 0), (0, _pad128(n) - n))))
        off += _pad128(n)
    return jnp.concatenate(parts, axis=1), offs


def _unpack(vec, offs, shapes):
    out = []
    for off, shp in zip(offs, shapes):
        n = int(np.prod(shp))
        out.append(vec[0, off:off + n].reshape(shp))
    return out


IN_ROWS = IN_DIM // N_DEV
DT_ROW0 = 9216
DT_DEV, DT_LO = divmod(DT_ROW0, IN_ROWS)


GATE_SHIFT = D - 32


def _in_row_pieces(tile):
    pieces = []
    if tile == DT_COL_BLOCK:
        for g in range(N_GROUPS):
            o = DT_ROW0 + HEADS_PER_GROUP * g
            pieces.append((N_STATE * g, o // IN_ROWS, o % IN_ROWS, HEADS_PER_GROUP))
        return pieces
    r, end = tile * D, (tile + 1) * D
    while r < end:
        o = r if r < DT_ROW0 else r - GATE_SHIFT
        dev, loc = divmod(o, IN_ROWS)
        n = min(end - r, IN_ROWS - loc)
        pieces.append((r - tile * D, dev, loc, n))
        r += n
    return pieces


def assemble_w_in(g_all):
    ntile = N_PROJ // D

    def body(g_ref, o_ref):
        j = pl.program_id(0)
        for tile in range(ntile):
            @pl.when(j == tile)
            def _(tile=tile):
                if tile == DT_COL_BLOCK:
                    o_ref[...] = jnp.zeros_like(o_ref)
                for dst, dev, loc, n in _in_row_pieces(tile):
                    o_ref[pl.ds(dst, n), :] = g_ref[dev, pl.ds(loc, n), :]

    return pl.pallas_call(
        body, name="assemble_w_in", grid=(ntile,),
        out_shape=jax.ShapeDtypeStruct((N_PROJ, D), g_all.dtype),
        in_specs=[pl.BlockSpec(memory_space=pltpu.VMEM)],
        out_specs=pl.BlockSpec((D, D), lambda j: (j, 0)),
        compiler_params=_cparams(("arbitrary",)),
    )(g_all)


def _grad_in_blocks(g_t, core, slot):
    dt0 = DT_COL_BLOCK * D
    dt = g_t[dt0:dt0 + N_GROUPS * N_STATE].reshape(N_GROUPS, N_STATE, D)[:, :HEADS_PER_GROUP].reshape(32, D)
    with_dt = jnp.concatenate([g_t[DT_DEV * IN_ROWS:DT_ROW0], dt,
                               g_t[DT_ROW0 + 32 + GATE_SHIFT:(DT_DEV + 1) * IN_ROWS + GATE_SHIFT]], axis=0)
    blocks = []
    for q in range(N_CHIP):
        if 2 * q + 1 < DT_DEV:
            blk = lax.dynamic_slice_in_dim(g_t, IN_ROWS * (2 * q + core), IN_ROWS, axis=0)
        else:
            assert 2 * q == DT_DEV
            after = g_t[(DT_DEV + 1) * IN_ROWS + GATE_SHIFT:(DT_DEV + 2) * IN_ROWS + GATE_SHIFT]
            blk = jnp.where(core == 0, with_dt, after)
        blocks.append(jnp.pad(blk, ((0, slot - IN_ROWS), (0, 0))))
    return jnp.stack(blocks)


def kernel(x, c, w_ada, b_ada, w_in, hgrn_lb, hgrn_gnorm, ssm_conv_w, ssm_conv_b, ssm_dt_bias, ssm_a_log, ssm_d, ssm_norm, w_branch_a, w_branch_b, w_o, ln1_g, ln1_b, w_ffn_gate, w_ffn_up, w_ffn_down, ln2_g, ln2_b, loss_target, m_w_ada, m_b_ada, m_w_in, m_hgrn_lb, m_hgrn_gnorm, m_ssm_conv_w, m_ssm_conv_b, m_ssm_dt_bias, m_ssm_a_log, m_ssm_d, m_ssm_norm, m_w_branch_a, m_w_branch_b, m_w_o, m_ln1_g, m_ln1_b, m_w_ffn_gate, m_w_ffn_up, m_w_ffn_down, m_ln2_g, m_ln2_b, v_w_ada, v_b_ada, v_w_in, v_hgrn_lb, v_hgrn_gnorm, v_ssm_conv_w, v_ssm_conv_b, v_ssm_dt_bias, v_ssm_a_log, v_ssm_d, v_ssm_norm, v_w_branch_a, v_w_branch_b, v_w_o, v_ln1_g, v_ln1_b, v_w_ffn_gate, v_w_ffn_up, v_w_ffn_down, v_ln2_g, v_ln2_b):
    me = 4 * lax.axis_index("x") + 2 * lax.axis_index("y") + lax.axis_index("c")
    xt = x[0]
    tgt = loss_target[0]
    t = xt.shape[0]
    ada_cols = w_ada.shape[2]
    conv_cols = ssm_conv_w.shape[2]

    small_in, _ = _pack([c, ssm_conv_w[0]])
    small_all = allgather_vmem(small_in, "allgather_small_inputs")
    c_all = small_all[:, 0, :D]
    conv_w = small_all[:, 0, D:D + CONV_TAPS * conv_cols].reshape(N_DEV, CONV_TAPS, conv_cols)
    conv_w = conv_w.transpose(1, 0, 2).reshape(CONV_TAPS, CONV_DIM)
    mod = ada_modulation(c_all, w_ada[0], b_ada.reshape(N_DEV, 1, ada_cols))
    mod6 = mod.reshape(6, D)

    shards = [w_in[0].T, w_ffn_gate[0].T, w_ffn_up[0].T, w_ffn_down[0], w_branch_a[0], w_branch_b[0], w_o[0]]
    shard_rows = [s.shape[0] for s in shards]
    slot_rows = [-(-r // 32) * 32 for r in shard_rows]
    row_offs = [sum(slot_rows[:i]) for i in range(len(shards))]
    padded = [jnp.pad(s.astype(BF16), ((0, p - r), (0, 0))) for s, r, p in zip(shards, shard_rows, slot_rows)]
    w_in_t = assemble_w_in(allgather_hbm(padded[0], "allgather_w_in"))

    lb = lower_bound_fwd(hgrn_lb)
    u1 = ln_modulate(xt, mod6, 0, 1, "ln_modulate_1")
    proj, g_rest = mm_nt_gather(u1, w_in_t, F32, jnp.concatenate(padded[1:], axis=0), "mm_in_proj")
    assert shard_rows[1:4] == slot_rows[1:4] == [FF_SHARD_ROWS] * 3
    gate_slot, up_slot, dn_slot = 0, 1, 2
    g_ba, g_bb, g_o = (g_rest[:, o - slot_rows[0]:o - slot_rows[0] + r] for o, r in zip(row_offs[4:], shard_rows[4:]))
    w_ba = g_ba.reshape(D, D)
    w_bb = g_bb.reshape(B_INNER, D)
    w_oo = g_o.reshape(D, D)
    o_a, o_raw, st_a = hgrn_fwd(proj, lb, hgrn_gnorm)
    xc, conv_slope = conv_fwd(proj, conv_w, ssm_conv_b)
    pad3 = ((0, 0), (0, 0), (0, N_STATE - HEADS_PER_GROUP))
    alog4 = jnp.pad(ssm_a_log.reshape(N_GROUPS, 1, HEADS_PER_GROUP), pad3)
    bias4 = jnp.pad(ssm_dt_bias.reshape(N_GROUPS, 1, HEADS_PER_GROUP), pad3)
    dskip4 = jnp.pad(ssm_d.reshape(N_GROUPS, 1, HEADS_PER_GROUP), pad3)
    expand = _head_expand()
    o_b, st_b = ssd_fwd(proj, xc, alog4, bias4, dskip4, ssm_norm, expand)
    ya, yb, merged, h1, x1, u2 = mixer_tail(o_a, o_b, w_ba, w_bb, proj, w_oo, xt, mod6, ln1_g, ln1_b)
    gu, act = ffn_in_act(u2, g_rest, gate_slot, up_slot)

    dh2, dx1_part, acc4 = ffn_tail_loss_bwd(x1, act, g_rest, dn_slot, mod6, ln2_g, ln2_b, tgt)
    g_dn = mm_tn(act, dh2, "mm_grad_ffn_down")
    dgu = ffn_act_bwd(dh2, g_rest, dn_slot, gu)
    g_gu_t = mm_tn(dgu, u2, "mm_grad_ffn_in")
    dh1, dx_part, acc2, g_o = mixer_tail_bwd(x1, dgu, g_rest, gate_slot, up_slot, mod6, dx1_part, xt, h1, ln1_g, ln1_b,
                                             merged)
    dya, dyb, dproj = merge_gates_bwd(dh1, w_oo, ya, yb, proj)
    g_ba_full = mm_tn(o_a, dya, "mm_grad_branch_a")
    g_bb_full = mm_tn(o_b, dyb, "mm_grad_branch_b")
    my_core = lax.axis_index("c")

    def by_core(blocks, rows, slots):
        contrib = jnp.concatenate([jnp.pad(b.reshape(N_DEV, -1, D), ((0, 0), (0, p - r), (0, 0)))
                                   for b, r, p in zip(blocks, rows, slots)], axis=1)
        split = contrib.reshape(N_CHIP, 2, contrib.shape[1], D).transpose(1, 0, 2, 3)
        return (lax.dynamic_index_in_dim(split, my_core, 0, keepdims=False),
                lax.dynamic_index_in_dim(split, 1 - my_core, 0, keepdims=False))

    keep_e, give_e = by_core([g_gu_t[:D_FF], g_gu_t[D_FF:], g_dn, g_ba_full, g_bb_full, g_o],
                             shard_rows[1:], slot_rows[1:])
    dproj, dlb, dgn, got_e = hgrn_bwd(proj, lb, hgrn_gnorm, o_raw, dya, w_ba, st_a, give_e, dproj)
    chip_e = sum_pair(keep_e.reshape(-1, D), got_e.reshape(-1, D), "sum_grads_rest_chip").reshape(keep_e.shape)
    dxs, dbm, dcm, dproj, ddt, dwn, dalog, dbias, ddsk, parts_e = ssd_bwd(proj, xc, alog4, bias4, dskip4, ssm_norm,
                                                                          expand, dyb, w_bb, st_b, chip_e, dproj)
    dproj, dcw, dcb = conv_bwd(proj, dxs, dbm, dcm, conv_slope, conv_w, dproj)
    dproj = dt_fill(ddt, dproj)
    g_in_t = mm_tn(dproj, u1, "mm_grad_in_proj")
    keep_l = _grad_in_blocks(g_in_t, my_core, slot_rows[0])
    give_l = _grad_in_blocks(g_in_t, 1 - my_core, slot_rows[0])
    got_l = exchange_sibling(give_l, "exchange_grad_in_sibling")
    chip_l = sum_pair(keep_l.reshape(-1, D), got_l.reshape(-1, D), "sum_grad_in_chip").reshape(keep_l.shape)
    du1, parts_l = mm_nn_exchange(dproj, w_in_t, F32, chip_l, "mm_du1")
    dx, acc1 = ln_modulate_bwd(xt, du1, mod6, 1, dx_part, "ln_modulate_1_bwd")
    gw_in = sum_parts(parts_l, "sum_grad_in")[:shard_rows[0]].T
    g_rows = sum_parts(parts_e, "sum_grads_rest")
    gw_fg, gw_fu, gw_fd, gw_ba, gw_bb, gw_o = (g_rows[o - slot_rows[0]:o - slot_rows[0] + r]
                                               for o, r in zip(row_offs[1:], shard_rows[1:]))
    gw_fg, gw_fu = gw_fg.T, gw_fu.T

    dmod = jnp.concatenate([acc1[1:2], acc1[0:1], acc2[2:3], acc2[1:2], acc2[0:1], acc4[0:1]], axis=1)
    small_fields = [dmod, acc4[3:4, :128], dlb, dgn, dcw[:CONV_TAPS], dcb, dbias, dalog, ddsk, dwn,
                    acc2[3:4], acc2[4:5], acc4[1:2], acc4[2:3]]
    small_out, offs = _pack(small_fields)
    small_sum_in = allgather_vmem(small_out, "allgather_small_grads")
    gsum, g_lb = reduce_small(small_sum_in, hgrn_lb, offs[2])
    (g_bada, loss_row, _, g_gn, g_cw_full, g_cb, g_bias4, g_alog4, g_dsk4, g_wn, g_l1g, g_l1b, g_l2g, g_l2b) = _unpack(
        gsum, offs, [(1, 6 * D), (1, 128), (1, D), (1, HK), (CONV_TAPS, CONV_DIM), (1, CONV_DIM),
                     (N_GROUPS, N_STATE), (N_GROUPS, N_STATE), (N_GROUPS, N_STATE), (1, B_INNER),
                     (1, D), (1, D), (1, D), (1, D)])
    loss = loss_row[0, 0]
    g_cw = lax.dynamic_slice(g_cw_full, (0, me * conv_cols), (CONV_TAPS, conv_cols))[None]
    g_dtb = g_bias4[:, :HEADS_PER_GROUP].reshape(1, 32)
    g_alog = g_alog4[:, :HEADS_PER_GROUP].reshape(1, 32)
    g_dsk = g_dsk4[:, :HEADS_PER_GROUP].reshape(1, 32)

    dmod_all = small_sum_in[:, 0, offs[0]:offs[0] + 6 * D]
    dmod_cols = lax.dynamic_slice(dmod_all, (0, me * ada_cols), (N_DEV, ada_cols))
    gw_ada = ada_weight_grad(c_all, dmod_cols)

    big = [("ada", w_ada[0], gw_ada, m_w_ada[0], v_w_ada[0]), ("in", w_in[0], gw_in, m_w_in[0], v_w_in[0]),
           ("branch_a", w_branch_a[0], gw_ba, m_w_branch_a[0], v_w_branch_a[0]),
           ("branch_b", w_branch_b[0], gw_bb, m_w_branch_b[0], v_w_branch_b[0]),
           ("o", w_o[0], gw_o, m_w_o[0], v_w_o[0]),
           ("ffn_gate", w_ffn_gate[0], gw_fg, m_w_ffn_gate[0], v_w_ffn_gate[0]),
           ("ffn_up", w_ffn_up[0], gw_fu, m_w_ffn_up[0], v_w_ffn_up[0]),
           ("ffn_down", w_ffn_down[0], gw_fd, m_w_ffn_down[0], v_w_ffn_down[0])]
    big_out = {}
    for nm, w_, g_, m_, v_ in big:
        d_, m2_, v2_ = adamw(w_, g_, m_, v_, "adamw_" + nm)
        big_out[nm] = (g_[None], d_[None], m2_[None], v2_[None])

    small_w = [b_ada, hgrn_lb, hgrn_gnorm, ssm_conv_w, ssm_conv_b, ssm_dt_bias, ssm_a_log, ssm_d, ssm_norm,
               ln1_g, ln1_b, ln2_g, ln2_b]
    small_g = [g_bada, g_lb, g_gn, g_cw, g_cb, g_dtb, g_alog, g_dsk, g_wn, g_l1g, g_l1b, g_l2g, g_l2b]
    small_m = [m_b_ada, m_hgrn_lb, m_hgrn_gnorm, m_ssm_conv_w, m_ssm_conv_b, m_ssm_dt_bias, m_ssm_a_log, m_ssm_d,
               m_ssm_norm, m_ln1_g, m_ln1_b, m_ln2_g, m_ln2_b]
    small_v = [v_b_ada, v_hgrn_lb, v_hgrn_gnorm, v_ssm_conv_w, v_ssm_conv_b, v_ssm_dt_bias, v_ssm_a_log, v_ssm_d,
               v_ssm_norm, v_ln1_g, v_ln1_b, v_ln2_g, v_ln2_b]
    shapes = [a.shape for a in small_w]
    small_g = [g_.reshape(s) for g_, s in zip(small_g, shapes)]
    pw, poffs = _pack(small_w)
    pg, _ = _pack(small_g)
    pm, _ = _pack(small_m)
    pv, _ = _pack(small_v)
    pd, pm2, pv2 = adamw(pw, pg, pm, pv, "adamw_small")
    s_d, s_m, s_v = (_unpack(p, poffs, shapes) for p in (pd, pm2, pv2))
    (sn_bada, sn_lb, sn_gn, sn_cw, sn_cb, sn_dtb, sn_alog, sn_dsk, sn_wn, sn_l1g, sn_l1b, sn_l2g, sn_l2b) = range(13)

    def order(kind):
        sm = [small_g, s_d, s_m, s_v][kind]
        bg = lambda nm: big_out[nm][kind]
        return [bg("ada"), sm[sn_bada], bg("in"), sm[sn_lb], sm[sn_gn], sm[sn_cw], sm[sn_cb], sm[sn_dtb], sm[sn_alog],
                sm[sn_dsk], sm[sn_wn], bg("branch_a"), bg("branch_b"), bg("o"), sm[sn_l1g], sm[sn_l1b],
                bg("ffn_gate"), bg("ffn_up"), bg("ffn_down"), sm[sn_l2g], sm[sn_l2b]]

    return (loss, dx[None], *order(0), *order(1), *order(2), *order(3))
```

```python
import numpy as np
import jax
import jax.numpy as jnp
from jax import lax
from jax.experimental import pallas as pl
from jax.experimental.pallas import tpu as pltpu

F32 = jnp.float32
BF16 = jnp.bfloat16
HI = lax.Precision.HIGHEST

N_DEV = 8
D = 1024
N_HEADS_A = 8
HK = 128
CHUNK = 64
SSD_CHUNK = 128
SSD_CHUNK_BWD = 256
N_GROUPS = 4
HEADS_PER_GROUP = 8
HEAD_P = 64
N_STATE = 128
GROUP_W = HEADS_PER_GROUP * HEAD_P
B_INNER = 2048
CONV_DIM = 3072
D_FF = 2816
IN_DIM = 11296
N_PROJ = 12288
ALPHA = 2.0 ** 0.25
LN_EPS = 1e-5
RMS_EPS = 1e-6
Q_SCALE = 128 ** -0.5
EXP_CLIP = 80.0
ADAM_LR, ADAM_B1, ADAM_B2, ADAM_EPS, ADAM_WD, ADAM_STEP = 0.001, 0.9, 0.999, 1e-8, 0.01, 10
VMEM_LIMIT = 48 * 1024 * 1024
TOKEN_BLOCK = 1024
ROW_TILE = 512
WIDE_ROW_TILE = 1024
MM_ROW_TILE = 1024
MM_TOKEN_TILE = 4096
MM_K_TILE = 3072
MM_COL_TILE = 1408
HGRN_HEADS_PER_STEP = 4
CHUNK_UNROLL = 8
MESH_ID = pl.DeviceIdType.MESH

NT_DIMS = (((1,), (1,)), ((), ()))
TN_DIMS = (((0,), (0,)), ((), ()))


def _cparams(sem=None):
    return pltpu.CompilerParams(dimension_semantics=sem, vmem_limit_bytes=VMEM_LIMIT)


def _sigmoid(x):
    return 1.0 / (1.0 + jnp.exp(-x))


def _dsilu(x, s):
    return s * (1.0 + x * (1.0 - s))


def _nt(a, b, precision=None):
    return lax.dot_general(a, b, NT_DIMS, precision=precision, preferred_element_type=F32)


def _tn(a, b, precision=None):
    return lax.dot_general(a, b, TN_DIMS, precision=precision, preferred_element_type=F32)


def _nn(a, b, precision=None):
    return jnp.dot(a, b, precision=precision, preferred_element_type=F32)


def _split(x, pieces):
    out = []
    for i in range(pieces):
        p = x.astype(BF16)
        out.append(p)
        if i + 1 < pieces:
            x = x - p.astype(F32)
    return out


def _sel(dot, x, sel01, pieces, x_first=True):
    acc = None
    for p in _split(x, pieces):
        term = dot(p, sel01) if x_first else dot(sel01, p)
        acc = term if acc is None else acc + term
    return acc


def _ln(x):
    mu = jnp.mean(x, axis=-1, keepdims=True)
    xc = x - mu
    rstd = lax.rsqrt(jnp.mean(xc * xc, axis=-1, keepdims=True) + LN_EPS)
    return xc * rstd, rstd


def _ln_bwd(dxh, xh, rstd):
    return rstd * (dxh - jnp.mean(dxh, axis=-1, keepdims=True) - xh * jnp.mean(dxh * xh, axis=-1, keepdims=True))


def _colsum(x):
    return jnp.sum(x, axis=0, keepdims=True)


def _tri(n, upper=False):
    r = lax.broadcasted_iota(jnp.int32, (n, n), 0)
    c = lax.broadcasted_iota(jnp.int32, (n, n), 1)
    return (c >= r) if upper else (r >= c)


def _my_pos():
    return lax.axis_index("x"), lax.axis_index("y"), lax.axis_index("c")


def _peer(pos, k):
    x, y, c = pos
    return (x ^ ((k >> 2) & 1), y ^ ((k >> 1) & 1), c ^ (k & 1))


def _flat(pos):
    return 4 * pos[0] + 2 * pos[1] + pos[2]


def allgather_vmem(v, name):
    n = v.shape[1]

    def body(v_ref, o_ref, send_sems, recv_sems, local_sem):
        me = _my_pos()
        mine = pltpu.make_async_copy(v_ref, o_ref.at[_flat(me)], local_sem)
        mine.start()
        sends = []
        for k in range(1, N_DEV):
            peer = _peer(me, k)
            cp = pltpu.make_async_remote_copy(v_ref, o_ref.at[_flat(me)], send_sems.at[k - 1], recv_sems.at[k - 1],
                                              device_id=peer, device_id_type=MESH_ID)
            cp.start()
            sends.append(cp)
        for k in range(1, N_DEV):
            peer = _peer(me, k)
            pltpu.make_async_remote_copy(v_ref, o_ref.at[_flat(peer)], send_sems.at[k - 1], recv_sems.at[k - 1],
                                         device_id=peer, device_id_type=MESH_ID).wait_recv()
        for cp in sends:
            cp.wait_send()
        mine.wait()

    return pl.pallas_call(
        body, name=name,
        out_shape=jax.ShapeDtypeStruct((N_DEV, 1, n), F32),
        in_specs=[pl.BlockSpec(memory_space=pltpu.VMEM)],
        out_specs=pl.BlockSpec(memory_space=pltpu.VMEM),
        scratch_shapes=[pltpu.SemaphoreType.DMA((N_DEV - 1,)), pltpu.SemaphoreType.DMA((N_DEV - 1,)),
                        pltpu.SemaphoreType.DMA],
        compiler_params=_cparams(),
    )(v)


def ada_modulation(c_all, w_ada_s, b_ada_r):
    ncol = w_ada_s.shape[1]

    def body(c_ref, w_ref, b_ref, o_ref, part_ref, send_sems, recv_sems):
        me = _my_pos()
        cval = c_ref[...]
        cond = cval * _sigmoid(cval)
        part = _nn(cond, w_ref[...], HI)
        for r in range(N_DEV):
            part_ref[r] = part[r:r + 1, :]
        sends = []
        for k in range(1, N_DEV):
            peer = _peer(me, k)
            cp = pltpu.make_async_remote_copy(part_ref.at[_flat(peer)], o_ref.at[_flat(me)], send_sems.at[k - 1],
                                              recv_sems.at[k - 1], device_id=peer, device_id_type=MESH_ID)
            cp.start()
            sends.append(cp)
        o_ref[_flat(me)] = part_ref[_flat(me)]
        for k in range(1, N_DEV):
            peer = _peer(me, k)
            pltpu.make_async_remote_copy(part_ref.at[_flat(peer)], o_ref.at[_flat(peer)], send_sems.at[k - 1],
                                         recv_sems.at[k - 1], device_id=peer, device_id_type=MESH_ID).wait_recv()
        for cp in sends:
            cp.wait_send()
        o_ref[...] = o_ref[...] + b_ref[...]

    return pl.pallas_call(
        body, name="ada_modulation",
        out_shape=jax.ShapeDtypeStruct((N_DEV, 1, ncol), F32),
        in_specs=[pl.BlockSpec(memory_space=pltpu.VMEM)] * 3,
        out_specs=pl.BlockSpec(memory_space=pltpu.VMEM),
        scratch_shapes=[pltpu.VMEM((N_DEV, 1, ncol), F32), pltpu.SemaphoreType.DMA((N_DEV - 1,)),
                        pltpu.SemaphoreType.DMA((N_DEV - 1,))],
        compiler_params=_cparams(),
    )(c_all, w_ada_s, b_ada_r)


def allgather_hbm(shard, name):
    def body(x_ref, out_ref, send_sems, recv_sems, local_sem):
        x, y, c = _my_pos()
        me, sibling = (x, y, c), (x, y, 1 - c)
        chips = [(1 - x, y), (x, 1 - y), (1 - x, 1 - y)]

        def slot(pos):
            return out_ref.at[_flat(pos)]

        def copy(k, block, to, src=None):
            return pltpu.make_async_remote_copy(slot(block) if src is None else src, slot(block), send_sems.at[k],
                                                recv_sems.at[k], device_id=to, device_id_type=MESH_ID)

        mine = pltpu.make_async_copy(x_ref, slot(me), local_sem)
        mine.start()
        first = [copy(0, me, sibling, src=x_ref)]
        first += [copy(1 + j, me, (*chip, c), src=x_ref) for j, chip in enumerate(chips)]
        for cp in first:
            cp.start()
        passed = [copy(4 + j, (*chip, c), sibling) for j, chip in enumerate(chips)]
        for j, chip in enumerate(chips):
            copy(1 + j, (*chip, c), me).wait_recv()
            passed[j].start()
        copy(0, sibling, me).wait_recv()
        for j, chip in enumerate(chips):
            copy(4 + j, (*chip, 1 - c), me).wait_recv()
        for cp in first + passed:
            cp.wait_send()
        mine.wait()

    return pl.pallas_call(
        body, name=name,
        out_shape=jax.ShapeDtypeStruct((N_DEV,) + shard.shape, shard.dtype),
        in_specs=[pl.BlockSpec(memory_space=pl.ANY)],
        out_specs=pl.BlockSpec(memory_space=pl.ANY),
        scratch_shapes=[pltpu.SemaphoreType.DMA((N_DEV - 1,)), pltpu.SemaphoreType.DMA((N_DEV - 1,)),
                        pltpu.SemaphoreType.DMA],
        compiler_params=_cparams(),
    )(shard)


N_CHIP = N_DEV // 2
SIBLING_SEMS = [pltpu.SemaphoreType.DMA, pltpu.SemaphoreType.DMA]
CHIP_SEMS = [pltpu.SemaphoreType.DMA((N_CHIP - 1,)), pltpu.SemaphoreType.DMA((N_CHIP - 1,)), pltpu.SemaphoreType.DMA]


def _sibling_exchange(s_ref, o_ref, send_sem, recv_sem):
    x, y, c = _my_pos()
    cp = pltpu.make_async_remote_copy(s_ref, o_ref, send_sem, recv_sem, device_id=(x, y, 1 - c), device_id_type=MESH_ID)
    return cp.start, cp.wait


def _chip_exchange(p_ref, o_ref, send_sems, recv_sems, local_sem):
    x, y, c = _my_pos()
    my_chip = 2 * x + y
    mine = pltpu.make_async_copy(p_ref.at[my_chip], o_ref.at[my_chip], local_sem)
    peers = [(x ^ (k >> 1), y ^ (k & 1)) for k in range(1, N_CHIP)]
    sends = [pltpu.make_async_remote_copy(p_ref.at[2 * px + py], o_ref.at[my_chip], send_sems.at[k], recv_sems.at[k],
                                          device_id=(px, py, c), device_id_type=MESH_ID)
             for k, (px, py) in enumerate(peers)]
    recvs = [pltpu.make_async_remote_copy(p_ref.at[2 * px + py], o_ref.at[2 * px + py], send_sems.at[k], recv_sems.at[k],
                                          device_id=(px, py, c), device_id_type=MESH_ID)
             for k, (px, py) in enumerate(peers)]

    def start():
        mine.start()
        for cp in sends:
            cp.start()

    def wait():
        for cp in recvs:
            cp.wait_recv()
        for cp in sends:
            cp.wait_send()
        mine.wait()

    return start, wait


def exchange_sibling(send, name):
    def body(s_ref, o_ref, send_sem, recv_sem):
        start, wait = _sibling_exchange(s_ref, o_ref, send_sem, recv_sem)
        start()
        wait()

    return pl.pallas_call(
        body, name=name,
        out_shape=jax.ShapeDtypeStruct(send.shape, send.dtype),
        in_specs=[pl.BlockSpec(memory_space=pl.ANY)],
        out_specs=pl.BlockSpec(memory_space=pl.ANY),
        scratch_shapes=SIBLING_SEMS,
        compiler_params=_cparams(),
    )(send)


LANES = 128


def _k_tile(kdim, unit=LANES):
    for cand in range(MM_K_TILE - MM_K_TILE % unit, 0, -unit):
        if kdim % cand == 0:
            return cand
    return kdim


def _lane_tile(n, cap):
    for cand in range(cap - cap % LANES, 0, -LANES):
        if n % cand == 0:
            return cand
    return n


def _m_tile(m, kdim):
    return min(MM_ROW_TILE if kdim > D else 2 * MM_ROW_TILE, m)


def mm_nn_exchange(a, b, out_dtype, part, name):
    kblocks, m, kb = a.shape
    kdim = kblocks * kb
    n = b.shape[1]
    tm, tn, tk = min(MM_ROW_TILE, m), _lane_tile(n, MM_COL_TILE), _k_tile(kdim)
    gn, gm, nk = n // tn, m // tm, kdim // tk
    per_step = tk // kb

    def body(a_ref, b_ref, part_ref, o_ref, parts_ref, acc_ref, send_sems, recv_sems, local_sem):
        j, i, k = pl.program_id(0), pl.program_id(1), pl.program_id(2)
        xchg_start, xchg_wait = _chip_exchange(part_ref, parts_ref, send_sems, recv_sems, local_sem)

        @pl.when((j == 0) & (i == 0) & (k == 0))
        def _():
            xchg_start()

        p = _nn(a_ref[0], b_ref[0:kb, :])
        for c in range(1, per_step):
            p = p + _nn(a_ref[c], b_ref[c * kb:(c + 1) * kb, :])

        @pl.when(k == 0)
        def _():
            acc_ref[...] = p

        @pl.when(k > 0)
        def _():
            acc_ref[...] += p

        @pl.when(k == nk - 1)
        def _():
            o_ref[...] = acc_ref[...].astype(o_ref.dtype)

        @pl.when((j == gn - 1) & (i == gm - 1) & (k == nk - 1))
        def _():
            xchg_wait()

    hbm = pl.BlockSpec(memory_space=pl.ANY)
    return pl.pallas_call(
        body, name=name, grid=(gn, gm, nk),
        out_shape=[jax.ShapeDtypeStruct((m, n), out_dtype), jax.ShapeDtypeStruct(part.shape, part.dtype)],
        in_specs=[pl.BlockSpec((per_step, tm, kb), lambda j, i, k: (k, i, 0)),
                  pl.BlockSpec((tk, tn), lambda j, i, k: (k, j)), hbm],
        out_specs=[pl.BlockSpec((tm, tn), lambda j, i, k: (i, j)), hbm],
        scratch_shapes=[pltpu.VMEM((tm, tn), F32)] + CHIP_SEMS,
        compiler_params=_cparams(("arbitrary", "arbitrary", "arbitrary")),
    )(a, b, part)


def mm_nt_gather(a, b, out_dtype, shard, name):
    m, kdim = a.shape
    n = b.shape[0]
    tm, tn = _m_tile(m, kdim), 1024
    assert kdim == 1024
    gj = m // tm
    nsteps = (n // tn) * gj
    forward_step = max(nsteps - 2, 0)

    def body(a_ref, b_ref, x_ref, o_ref, g_ref, send_sems, recv_sems, local_sem):
        step = pl.program_id(0) * gj + pl.program_id(1)
        x, y, c = _my_pos()
        me, sibling = (x, y, c), (x, y, 1 - c)
        chips = [(1 - x, y), (x, 1 - y), (1 - x, 1 - y)]

        def slot(pos):
            return g_ref.at[_flat(pos)]

        def copy(k, block, to, src=None):
            return pltpu.make_async_remote_copy(slot(block) if src is None else src, slot(block), send_sems.at[k],
                                                recv_sems.at[k], device_id=to, device_id_type=MESH_ID)

        mine = pltpu.make_async_copy(x_ref, slot(me), local_sem)
        first = [copy(0, me, sibling, src=x_ref)]
        first += [copy(1 + j, me, (*chip, c), src=x_ref) for j, chip in enumerate(chips)]
        passed = [copy(4 + j, (*chip, c), sibling) for j, chip in enumerate(chips)]

        @pl.when(step == 0)
        def _():
            mine.start()
            for cp in first:
                cp.start()

        rows = pl.ds(pl.multiple_of(pl.program_id(1) * tm, tm), tm)
        o_ref[...] = _nt(a_ref[rows, :], b_ref[...]).astype(o_ref.dtype)

        @pl.when(step == forward_step)
        def _():
            for j, chip in enumerate(chips):
                copy(1 + j, (*chip, c), me).wait_recv()
                passed[j].start()

        @pl.when(step == nsteps - 1)
        def _():
            copy(0, sibling, me).wait_recv()
            for j, chip in enumerate(chips):
                copy(4 + j, (*chip, 1 - c), me).wait_recv()
            for cp in first + passed:
                cp.wait_send()
            mine.wait()

    return pl.pallas_call(
        body, name=name, grid=(n // tn, gj),
        out_shape=[jax.ShapeDtypeStruct((m, n), out_dtype), jax.ShapeDtypeStruct((N_DEV,) + shard.shape, shard.dtype)],
        in_specs=[pl.BlockSpec(memory_space=pltpu.VMEM), pl.BlockSpec((tn, kdim), lambda j, i: (j, 0)),
                  pl.BlockSpec(memory_space=pl.ANY)],
        out_specs=[pl.BlockSpec((tm, tn), lambda j, i: (i, j)), pl.BlockSpec(memory_space=pl.ANY)],
        scratch_shapes=[pltpu.SemaphoreType.DMA((N_DEV - 1,)), pltpu.SemaphoreType.DMA((N_DEV - 1,)),
                        pltpu.SemaphoreType.DMA],
        compiler_params=_cparams(("arbitrary", "arbitrary")),
    )(a, b, shard)


def mm_tn(a, b, name):
    tt, tn = min(MM_TOKEN_TILE, b.shape[0]), _lane_tile(b.shape[1], MM_COL_TILE)
    tka = _lane_tile(a.shape[-1], 1024)
    if a.ndim == 3:
        t, ka = a.shape[1], a.shape[0] * a.shape[2]
        per = a.shape[2] // tka
        a_spec = pl.BlockSpec((None, tt, tka), lambda i, j, s: (i // per, s, i % per))
    else:
        t, ka = a.shape
        a_spec = pl.BlockSpec((tt, tka), lambda i, j, s: (s, i))
    n = b.shape[1]
    nt = t // tt

    def body(a_ref, b_ref, o_ref, *acc):
        p = _tn(a_ref[...], b_ref[...])
        if nt == 1:
            o_ref[...] = p.astype(o_ref.dtype)
        else:
            acc_ref, s = acc[0], pl.program_id(2)

            @pl.when(s == 0)
            def _():
                acc_ref[...] = p

            @pl.when(s > 0)
            def _():
                acc_ref[...] += p

            @pl.when(s == nt - 1)
            def _():
                o_ref[...] = acc_ref[...].astype(o_ref.dtype)

    return pl.pallas_call(
        body, name=name, grid=(ka // tka, n // tn, nt),
        out_shape=jax.ShapeDtypeStruct((ka, n), BF16),
        in_specs=[a_spec, pl.BlockSpec((tt, tn), lambda i, j, s: (s, j))],
        out_specs=pl.BlockSpec((tka, tn), lambda i, j, s: (i, j)),
        scratch_shapes=[] if nt == 1 else [pltpu.VMEM((tka, tn), F32)],
        compiler_params=_cparams(("parallel", "parallel", "arbitrary")),
    )(a, b)


def _tile(t, cap):
    return min(cap, t)


def ln_modulate(x, mod6, shift_row, scale_row, name):
    t = x.shape[0]
    tm = _tile(t, WIDE_ROW_TILE)

    def body(x_ref, mod_ref, o_ref):
        xh, _ = _ln(x_ref[...])
        sc = mod_ref[scale_row:scale_row + 1, :]
        sh = mod_ref[shift_row:shift_row + 1, :]
        o_ref[...] = (xh * (1.0 + sc) + sh).astype(BF16)

    return pl.pallas_call(
        body, name=name, grid=(t // tm,),
        out_shape=jax.ShapeDtypeStruct((t, D), BF16),
        in_specs=[pl.BlockSpec((tm, D), lambda i: (i, 0)), pl.BlockSpec((6, D), lambda i: (0, 0))],
        out_specs=pl.BlockSpec((tm, D), lambda i: (i, 0)),
        compiler_params=_cparams(("parallel",)),
    )(x, mod6)


FF_SHARD_ROWS = D_FF // N_DEV


def _ffn_weight_spec(ndev, slot, index_map):
    return pl.BlockSpec((ndev, FF_SHARD_ROWS, D), lambda *ids: (index_map(*ids), slot, 0))


def ffn_tail_loss_bwd(x1, act, gathered, dn_slot, mod6, ln_g, ln_b, target):
    t = x1.shape[0]
    tm = _tile(t, ROW_TILE)
    kdim = act.shape[1]

    def body(x_ref, a_ref, w_ref, mod_ref, g_ref, b_ref, c_ref, dh_ref, dx_ref, acc_ref):
        @pl.when(pl.program_id(0) == 0)
        def _():
            acc_ref[...] = jnp.zeros_like(acc_ref)

        hv = _nn(a_ref[...], w_ref[...].reshape(kdim, D))
        gate = mod_ref[5:6, :]
        rh, rstd = _ln(ALPHA * x_ref[...] + gate * hv)
        lng = g_ref[...]
        diff = rh * lng + b_ref[...] - c_ref[...]
        dxo = diff * (1.0 / D)
        lsum = jnp.sum(_colsum(diff * diff), axis=-1, keepdims=True) * (0.5 / D)
        acc_ref[3:4, :] += jnp.broadcast_to(lsum, (1, D))
        acc_ref[1:2, :] += _colsum(dxo * rh)
        acc_ref[2:3, :] += _colsum(dxo)
        dr = _ln_bwd(dxo * lng, rh, rstd)
        acc_ref[0:1, :] += _colsum(dr * hv)
        dh_ref[...] = (gate * dr).astype(BF16)
        dx_ref[...] = ALPHA * dr

    row = pl.BlockSpec((tm, D), lambda i: (i, 0))
    vec = pl.BlockSpec((1, D), lambda i: (0, 0))
    return pl.pallas_call(
        body, name="ffn_tail_loss_bwd", grid=(t // tm,),
        out_shape=[jax.ShapeDtypeStruct((t, D), BF16), jax.ShapeDtypeStruct((t, D), F32),
                   jax.ShapeDtypeStruct((8, D), F32)],
        in_specs=[row, pl.BlockSpec((tm, kdim), lambda i: (i, 0)), _ffn_weight_spec(N_DEV, dn_slot, lambda i: 0),
                  pl.BlockSpec((6, D), lambda i: (0, 0)), vec, vec, row],
        out_specs=[row, row, pl.BlockSpec((8, D), lambda i: (0, 0))],
        compiler_params=_cparams(("arbitrary",)),
    )(x1, act, gathered, mod6, ln_g, ln_b, target)


def ln_modulate_bwd(x, du, mod6, scale_row, dx_part, name):
    t = x.shape[0]
    tm = _tile(t, ROW_TILE)

    def body(x_ref, du_ref, mod_ref, dp_ref, dx_ref, acc_ref):
        @pl.when(pl.program_id(0) == 0)
        def _():
            acc_ref[...] = jnp.zeros_like(acc_ref)

        xh, rstd = _ln(x_ref[...])
        du_v = du_ref[...]
        sc = mod_ref[scale_row:scale_row + 1, :]
        acc_ref[0:1, :] += _colsum(du_v * xh)
        acc_ref[1:2, :] += _colsum(du_v)
        dx_ref[...] = dp_ref[...] + _ln_bwd(du_v * (1.0 + sc), xh, rstd)

    row = pl.BlockSpec((tm, D), lambda i: (i, 0))
    return pl.pallas_call(
        body, name=name, grid=(t // tm,),
        out_shape=[jax.ShapeDtypeStruct((t, D), F32), jax.ShapeDtypeStruct((8, D), F32)],
        in_specs=[row, row, pl.BlockSpec((6, D), lambda i: (0, 0)), row],
        out_specs=[row, pl.BlockSpec((8, D), lambda i: (0, 0))],
        compiler_params=_cparams(("arbitrary",)),
    )(x, du, mod6, dx_part)


def mixer_tail_bwd(x1, dgu, gathered, gate_slot, up_slot, mod6, dx1_part, x, h, ln_g, ln_b, merged):
    t = x.shape[0]
    tm = _tile(t, ROW_TILE // 2)
    _, _, kb = dgu.shape
    last = t // tm - 1

    def body(x1_ref, a_ref, wg_ref, wu_ref, mod_ref, dp_ref, x_ref, h_ref, g_ref, b_ref, m_ref,
             dh_ref, dx_ref, acc_ref, gw_ref, gw_acc):
        @pl.when(pl.program_id(0) == 0)
        def _():
            acc_ref[...] = jnp.zeros_like(acc_ref)
            gw_acc[...] = jnp.zeros_like(gw_acc)

        du = _nn(a_ref[0], wg_ref[...].reshape(kb, D)) + _nn(a_ref[1], wu_ref[...].reshape(kb, D))
        xh, rstd1 = _ln(x1_ref[...])
        acc_ref[0:1, :] += _colsum(du * xh)
        acc_ref[1:2, :] += _colsum(du)
        dx1 = dp_ref[...] + _ln_bwd(du * (1.0 + mod_ref[4:5, :]), xh, rstd1)
        gate = mod_ref[2:3, :]
        hv = h_ref[...]
        rh, rstd = _ln(ALPHA * x_ref[...] + gate * hv)
        acc_ref[3:4, :] += _colsum(dx1 * rh)
        acc_ref[4:5, :] += _colsum(dx1)
        dr = _ln_bwd(dx1 * g_ref[...], rh, rstd)
        acc_ref[2:3, :] += _colsum(dr * hv)
        dh = (gate * dr).astype(BF16)
        dh_ref[...] = dh
        dx_ref[...] = ALPHA * dr
        gw_acc[...] += _tn(m_ref[...], dh)

        @pl.when(pl.program_id(0) == last)
        def _():
            gw_ref[...] = gw_acc[...].astype(BF16)

    row = pl.BlockSpec((tm, D), lambda i: (i, 0))
    vec = pl.BlockSpec((1, D), lambda i: (0, 0))
    return pl.pallas_call(
        body, name="mixer_tail_bwd", grid=(t // tm,),
        out_shape=[jax.ShapeDtypeStruct((t, D), BF16), jax.ShapeDtypeStruct((t, D), F32),
                   jax.ShapeDtypeStruct((8, D), F32), jax.ShapeDtypeStruct((D, D), BF16)],
        in_specs=[row, pl.BlockSpec((2, tm, kb), lambda i: (0, i, 0)), _ffn_weight_spec(N_DEV, gate_slot, lambda i: 0),
                  _ffn_weight_spec(N_DEV, up_slot, lambda i: 0),
                  pl.BlockSpec((6, D), lambda i: (0, 0)), row, row, row, vec, vec, row],
        out_specs=[row, row, pl.BlockSpec((8, D), lambda i: (0, 0)), pl.BlockSpec((D, D), lambda i: (0, 0))],
        scratch_shapes=[pltpu.VMEM((D, D), F32)],
        compiler_params=_cparams(("arbitrary",)),
    )(x1, dgu, gathered, gathered, mod6, dx1_part, x, h, ln_g, ln_b, merged)


def mixer_tail(o_a, o_b, w_ba, w_bb, proj, w_o, x, mod6, ln_g, ln_b):
    t = o_a.shape[0]
    tm = _tile(t, ROW_TILE // 2)

    def body(oa_ref, ob_ref, wa_ref, wb_ref, ga_ref, gb_ref, w_ref, x_ref, mod_ref, g_ref, b_ref,
             ya_ref, yb_ref, m_ref, h_ref, x1_ref, u2_ref):
        ya = _nn(oa_ref[...], wa_ref[...])
        yb = _nn(ob_ref[...], wb_ref[...])
        ya_ref[...] = ya.astype(BF16)
        yb_ref[...] = yb.astype(BF16)
        merged = (_sigmoid(ga_ref[...]) * ya + _sigmoid(gb_ref[...]) * yb).astype(BF16)
        m_ref[...] = merged
        hv = _nn(merged, w_ref[...])
        h_ref[...] = hv
        rh, _ = _ln(ALPHA * x_ref[...] + mod_ref[2:3, :] * hv)
        x1 = rh * g_ref[...] + b_ref[...]
        x1_ref[...] = x1
        xh, _ = _ln(x1)
        u2_ref[...] = (xh * (1.0 + mod_ref[4:5, :]) + mod_ref[3:4, :]).astype(BF16)

    row = pl.BlockSpec((tm, D), lambda i: (i, 0))
    vec = pl.BlockSpec((1, D), lambda i: (0, 0))
    whole = pl.BlockSpec(memory_space=pltpu.VMEM)
    return pl.pallas_call(
        body, name="mixer_tail", grid=(t // tm,),
        out_shape=[jax.ShapeDtypeStruct((t, D), BF16), jax.ShapeDtypeStruct((t, D), BF16),
                   jax.ShapeDtypeStruct((t, D), BF16), jax.ShapeDtypeStruct((t, D), F32),
                   jax.ShapeDtypeStruct((t, D), F32), jax.ShapeDtypeStruct((t, D), BF16)],
        in_specs=[row, pl.BlockSpec((tm, o_b.shape[1]), lambda i: (i, 0)), whole, whole,
                  pl.BlockSpec((tm, D), lambda i: (i, GATE_BLOCK0)),
                  pl.BlockSpec((tm, D), lambda i: (i, GATE_BLOCK0 + 1)), whole,
                  row, pl.BlockSpec((6, D), lambda i: (0, 0)), vec, vec],
        out_specs=[row] * 6,
        compiler_params=_cparams(("parallel",)),
    )(o_a, o_b, w_ba, w_bb, proj, proj, w_o, x, mod6, ln_g, ln_b)


def merge_gates_bwd(dh, w_o, ya, yb, proj, o_a, o_b):
    t = ya.shape[0]
    tm = _tile(t, ROW_TILE // 2)
    last = t // tm - 1
    ka, kb = o_a.shape[1], o_b.shape[1]

    def body(dh_ref, w_ref, ya_ref, yb_ref, ga_ref, gb_ref, oa_ref, ob_ref, dya_ref, dyb_ref, dp_ref, gwa_ref, gwb_ref,
             gwa_acc, gwb_acc):
        @pl.when(pl.program_id(0) == 0)
        def _():
            gwa_acc[...] = jnp.zeros_like(gwa_acc)
            gwb_acc[...] = jnp.zeros_like(gwb_acc)

        dmv = _nt(dh_ref[...], w_ref[...])
        sa = _sigmoid(ga_ref[...])
        sb = _sigmoid(gb_ref[...])
        dya = (dmv * sa).astype(BF16)
        dyb = (dmv * sb).astype(BF16)
        dya_ref[...] = dya
        dyb_ref[...] = dyb
        dp_ref[0] = (dmv * ya_ref[...].astype(F32) * sa * (1.0 - sa)).astype(BF16)
        dp_ref[1] = (dmv * yb_ref[...].astype(F32) * sb * (1.0 - sb)).astype(BF16)
        gwa_acc[...] += _tn(oa_ref[...], dya)
        gwb_acc[...] += _tn(ob_ref[...], dyb)

        @pl.when(pl.program_id(0) == last)
        def _():
            gwa_ref[...] = gwa_acc[...].astype(BF16)
            gwb_ref[...] = gwb_acc[...].astype(BF16)

    row = pl.BlockSpec((tm, D), lambda i: (i, 0))
    return pl.pallas_call(
        body, name="merge_gates_bwd", grid=(t // tm,),
        out_shape=[jax.ShapeDtypeStruct((t, D), BF16)] * 2 + [jax.ShapeDtypeStruct((N_PROJ // D, t, D), BF16),
                                                              jax.ShapeDtypeStruct((ka, D), BF16),
                                                              jax.ShapeDtypeStruct((kb, D), BF16)],
        in_specs=[row, pl.BlockSpec((D, D), lambda i: (0, 0)), row, row,
                  pl.BlockSpec((tm, D), lambda i: (i, GATE_BLOCK0)),
                  pl.BlockSpec((tm, D), lambda i: (i, GATE_BLOCK0 + 1)),
                  pl.BlockSpec((tm, ka), lambda i: (i, 0)), pl.BlockSpec((tm, kb), lambda i: (i, 0))],
        out_specs=[row, row, pl.BlockSpec((2, tm, D), lambda i: (GATE_BLOCK0 // 2, i, 0)),
                   pl.BlockSpec((ka, D), lambda i: (0, 0)), pl.BlockSpec((kb, D), lambda i: (0, 0))],
        scratch_shapes=[pltpu.VMEM((ka, D), F32), pltpu.VMEM((kb, D), F32)],
        compiler_params=_cparams(("arbitrary",)),
    )(dh, w_o, ya, yb, proj, proj, o_a, o_b)


FF_CHUNK = 1408


def ffn_in_act(u, gathered, gate_slot, up_slot):
    t = u.shape[0]
    tm = _tile(t, ROW_TILE)
    nj = D_FF // FF_CHUNK
    ndev = FF_CHUNK // FF_SHARD_ROWS

    def body(a_ref, bg_ref, bu_ref, gu_ref, act_ref):
        a = a_ref[...]
        g = _nt(a, bg_ref[...].reshape(FF_CHUNK, D))
        up = _nt(a, bu_ref[...].reshape(FF_CHUNK, D))
        gu_ref[0] = g.astype(BF16)
        gu_ref[1] = up.astype(BF16)
        act_ref[...] = (g * _sigmoid(g) * up).astype(BF16)

    return pl.pallas_call(
        body, name="ffn_in_act", grid=(nj, t // tm),
        out_shape=[jax.ShapeDtypeStruct((2, t, D_FF), BF16), jax.ShapeDtypeStruct((t, D_FF), BF16)],
        in_specs=[pl.BlockSpec((tm, D), lambda j, i: (i, 0)), _ffn_weight_spec(ndev, gate_slot, lambda j, i: j),
                  _ffn_weight_spec(ndev, up_slot, lambda j, i: j)],
        out_specs=[pl.BlockSpec((2, tm, FF_CHUNK), lambda j, i: (0, i, j)),
                   pl.BlockSpec((tm, FF_CHUNK), lambda j, i: (i, j))],
        compiler_params=_cparams(("parallel", "parallel")),
    )(u, gathered, gathered)


def ffn_act_bwd(dh, gathered, dn_slot, gu):
    t = dh.shape[0]
    tm = _tile(t, ROW_TILE)
    ndev = FF_CHUNK // FF_SHARD_ROWS

    def body(a_ref, b_ref, gu_ref, o_ref):
        da = _nt(a_ref[...], b_ref[...].reshape(FF_CHUNK, D))
        g = gu_ref[0].astype(F32)
        up = gu_ref[1].astype(F32)
        s = _sigmoid(g)
        o_ref[0] = (da * up * _dsilu(g, s)).astype(BF16)
        o_ref[1] = (da * g * s).astype(BF16)

    blk = pl.BlockSpec((2, tm, FF_CHUNK), lambda j, i: (0, i, j))
    return pl.pallas_call(
        body, name="ffn_act_bwd", grid=(D_FF // FF_CHUNK, t // tm),
        out_shape=jax.ShapeDtypeStruct((2, t, D_FF), BF16),
        in_specs=[pl.BlockSpec((tm, D), lambda j, i: (i, 0)), _ffn_weight_spec(ndev, dn_slot, lambda j, i: j), blk],
        out_specs=blk,
        compiler_params=_cparams(("parallel", "parallel")),
    )(dh, gathered, gu)


def _hgrn_chunk_terms(q, fl, lbv, tril_f):
    sig = _sigmoid(fl)
    f = lbv + (1.0 - lbv) * sig
    lam = jnp.log(f)
    k = 1.0 - f
    sq = _sigmoid(q)
    qt = q * sq * Q_SCALE
    bc = _sel(_nn, lam, tril_f, 3, x_first=False)
    bmid = bc[CHUNK // 2 - 1:CHUNK // 2, :]
    bl = bc[CHUNK - 1:CHUNK, :]
    eq = jnp.exp(jnp.minimum(bc - bmid, EXP_CLIP))
    ek = jnp.exp(jnp.minimum(bmid - bc, EXP_CLIP))
    eb = jnp.exp(bc)
    ekl = jnp.exp(bl - bc)
    ebl = jnp.exp(bl)
    return sig, f, k, sq, qt, eq, ek, eb, ekl, ebl


def hgrn_fwd(proj, lb, gnorm):
    t = proj.shape[0]
    tb = _tile(t, TOKEN_BLOCK)
    ncb = tb // CHUNK

    hps = HGRN_HEADS_PER_STEP
    wide = hps * HK

    def body(q_ref, f_ref, i_ref, g_ref, lb_ref, gn_ref, oa_ref, oraw_ref, st_ref, state):
        @pl.when(pl.program_id(1) == 0)
        def _():
            state[...] = jnp.zeros_like(state)

        gn = gn_ref[...]
        mask = _tri(CHUNK)
        tril_f = mask.astype(BF16)

        def chunk(c, carry):
            sl = pl.ds(pl.multiple_of(c * CHUNK, CHUNK), CHUNK)
            for hh in range(hps):
                ln = slice(hh * HK, (hh + 1) * HK)
                q, fl, v, g = q_ref[sl, ln], f_ref[sl, ln], i_ref[sl, ln], g_ref[sl, ln]
                sig, f, k, sq, qt, eq, ek, eb, ekl, ebl = _hgrn_chunk_terms(q, fl, lb_ref[:, ln], tril_f)
                a = jnp.where(mask, _nt((qt * eq).astype(BF16), (k * ek).astype(BF16)), 0.0)
                st = state[hh]
                st_ref[hh, c] = st
                vb = v.astype(BF16)
                o = _nn(a.astype(BF16), vb) + _nt((qt * eb).astype(BF16), st.astype(BF16))
                state[hh] = st * ebl + _tn(vb, (k * ekl).astype(BF16))
                oraw_ref[sl, ln] = o
                rn = o * lax.rsqrt(jnp.mean(o * o, axis=-1, keepdims=True) + RMS_EPS)
                oa_ref[sl, ln] = (rn * gn * g * _sigmoid(g)).astype(BF16)
            return carry

        lax.fori_loop(0, ncb, chunk, 0, unroll=min(CHUNK_UNROLL, ncb))

    def col(block):
        return pl.BlockSpec((tb, wide), lambda h, j: (j, block * (N_HEADS_A // hps) + h))

    return pl.pallas_call(
        body, name="hgrn_fwd", grid=(N_HEADS_A // hps, t // tb),
        out_shape=[jax.ShapeDtypeStruct((t, D), BF16), jax.ShapeDtypeStruct((t, D), F32),
                   jax.ShapeDtypeStruct((N_HEADS_A, t // CHUNK, HK, HK), F32)],
        in_specs=[col(0), col(1), col(2), col(3), pl.BlockSpec((1, wide), lambda h, j: (0, h)),
                  pl.BlockSpec((1, HK), lambda h, j: (0, 0))],
        out_specs=[pl.BlockSpec((tb, wide), lambda h, j: (j, h)), pl.BlockSpec((tb, wide), lambda h, j: (j, h)),
                   pl.BlockSpec((hps, ncb, HK, HK), lambda h, j: (h, j, 0, 0))],
        scratch_shapes=[pltpu.VMEM((hps, HK, HK), F32)],
        compiler_params=_cparams(("parallel", "arbitrary")),
    )(proj, proj, proj, proj, lb, gnorm)


def hgrn_bwd(proj, lb, gnorm, o_raw, dya, w_ba, states, give, dproj):
    t = proj.shape[0]
    tb = _tile(t, TOKEN_BLOCK)
    ncb = tb // CHUNK
    nb = t // tb
    hps = HGRN_HEADS_PER_STEP
    wide = hps * HK

    def body(q_ref, f_ref, i_ref, g_ref, lb_ref, gn_ref, oraw_ref, dya_ref, wba_ref, st_ref, give_ref, dp_in_ref,
             dp_ref, dlb_ref, dgn_ref, got_ref, dstate, doa_ref, send_sem, recv_sem):
        h, j = pl.program_id(0), pl.program_id(1)
        swap_start, swap_wait = _sibling_exchange(give_ref, got_ref, send_sem, recv_sem)
        doa_ref[...] = _nt(dya_ref[...], wba_ref[...])

        @pl.when((h == 0) & (j == 0))
        def _():
            swap_start()

        @pl.when(j == 0)
        def _():
            dstate[...] = jnp.zeros_like(dstate)
            dlb_ref[...] = jnp.zeros_like(dlb_ref)

        @pl.when((j == 0) & (h == 0))
        def _():
            dgn_ref[...] = jnp.zeros_like(dgn_ref)

        gn = gn_ref[...]
        mask = _tri(CHUNK)
        mask_t = _tri(CHUNK, upper=True)
        tril_f = mask.astype(BF16)
        triu_f = mask_t.astype(BF16)

        def chunk(i, c0):
            c = ncb - 1 - i
            sl = pl.ds(pl.multiple_of(c * CHUNK, CHUNK), CHUNK)
            for hh in range(hps):
                ln = slice(hh * HK, (hh + 1) * HK)
                q, fl, v, g = q_ref[sl, ln], f_ref[sl, ln], i_ref[sl, ln], g_ref[sl, ln]
                lbv = lb_ref[:, ln]
                sig, f, k, sq, qt, eq, ek, eb, ekl, ebl = _hgrn_chunk_terms(q, fl, lbv, tril_f)
                qe = (qt * eq).astype(BF16)
                ke = (k * ek).astype(BF16)
                st32 = st_ref[hh, c]
                st = st32.astype(BF16)
                dst = dstate[hh]
                dstb = dst.astype(BF16)
                o = oraw_ref[sl, ln]
                rstd = lax.rsqrt(jnp.mean(o * o, axis=-1, keepdims=True) + RMS_EPS)
                rn = o * rstd
                sgm = _sigmoid(g)
                sg = g * sgm
                doa_v = doa_ref[sl, ln]
                drn = doa_v * gn * sg
                dgn_ref[...] += _colsum(doa_v * rn * sg)
                dp_ref[3, sl, ln] = (doa_v * rn * gn * _dsilu(g, sgm)).astype(BF16)
                do = rstd * (drn - rn * jnp.mean(drn * rn, axis=-1, keepdims=True))
                dob = do.astype(BF16)
                vb = v.astype(BF16)
                da = jnp.where(mask, _nt(dob, vb), 0.0).astype(BF16)
                da_t = jnp.where(mask_t, _nt(vb, dob), 0.0).astype(BF16)
                a_t = jnp.where(mask_t, _nt(ke, qe), 0.0).astype(BF16)
                kl = (k * ekl).astype(BF16)
                qb = (qt * eb).astype(BF16)
                dq_in = _nn(da, ke)
                dk_in = _nn(da_t, qe)
                dq_out = eb * _nn(dob, st)
                dk_out = ekl * _nn(vb, dstb)
                dqt = eq * dq_in + dq_out
                dk = ek * dk_in + dk_out
                dv = _nn(a_t, dob) + _nt(kl, dstb)
                dstate[hh] = dst * ebl + _tn(dob, qb)
                dbig = qe.astype(F32) * dq_in - ke.astype(F32) * dk_in + qt * dq_out - k * dk_out
                beyond = _colsum(k * dk_out) + ebl * _colsum(dst * st32)
                dlam = _sel(_nn, dbig, triu_f, 3, x_first=False) + beyond
                df = dlam / f - dk
                dp_ref[1, sl, ln] = (df * (1.0 - lbv) * sig * (1.0 - sig)).astype(BF16)
                dlb_ref[:, ln] += _colsum(df * (1.0 - sig))
                dp_ref[0, sl, ln] = (dqt * Q_SCALE * _dsilu(q, sq)).astype(BF16)
                dp_ref[2, sl, ln] = dv.astype(BF16)
            return c0

        lax.fori_loop(0, ncb, chunk, 0, unroll=min(CHUNK_UNROLL, ncb))

        @pl.when((h == N_HEADS_A // hps - 1) & (j == nb - 1))
        def _():
            swap_wait()

    def col(block):
        return pl.BlockSpec((tb, wide), lambda h, j: (nb - 1 - j, block * (N_HEADS_A // hps) + h))

    hcol = pl.BlockSpec((tb, wide), lambda h, j: (nb - 1 - j, h))
    hbm = pl.BlockSpec(memory_space=pl.ANY)
    return pl.pallas_call(
        body, name="hgrn_bwd", grid=(N_HEADS_A // hps, nb),
        out_shape=[jax.ShapeDtypeStruct(dproj.shape, dproj.dtype), jax.ShapeDtypeStruct((1, D), F32),
                   jax.ShapeDtypeStruct((1, HK), F32), jax.ShapeDtypeStruct(give.shape, give.dtype)],
        in_specs=[col(0), col(1), col(2), col(3), pl.BlockSpec((1, wide), lambda h, j: (0, h)),
                  pl.BlockSpec((1, HK), lambda h, j: (0, 0)), hcol,
                  pl.BlockSpec((tb, D), lambda h, j: (nb - 1 - j, 0)), pl.BlockSpec((wide, D), lambda h, j: (h, 0)),
                  pl.BlockSpec((hps, ncb, HK, HK), lambda h, j: (h, nb - 1 - j, 0, 0)), hbm, hbm],
        out_specs=[pl.BlockSpec((4, tb, wide), lambda h, j: (0, nb - 1 - j, h)),
                   pl.BlockSpec((1, wide), lambda h, j: (0, h)), pl.BlockSpec((1, HK), lambda h, j: (0, 0)), hbm],
        input_output_aliases={11: 0},
        scratch_shapes=[pltpu.VMEM((hps, HK, HK), F32), pltpu.VMEM((tb, wide), F32)] + SIBLING_SEMS,
        compiler_params=_cparams(("arbitrary", "arbitrary")),
    )(proj, proj, proj, proj, lb, gnorm, o_raw, dya, w_ba, states, give, dproj)


CONV_BLOCK0 = 6
CONV_TAPS = 4
HALO = 8


def conv_fwd(proj, conv_w, conv_b):
    t = proj.shape[0]
    tm = _tile(t, ROW_TILE)
    r = tm // HALO

    def body(x_ref, halo_ref, w_ref, b_ref, o_ref, ds_ref):
        i = pl.program_id(1)
        halo = jnp.where(i > 0, halo_ref[...], 0.0)
        ext = jnp.concatenate([halo, x_ref[...]], axis=0)
        pre = b_ref[...] + w_ref[CONV_TAPS - 1:CONV_TAPS, :] * ext[HALO:, :]
        for tap in range(CONV_TAPS - 1):
            pre = pre + w_ref[tap:tap + 1, :] * pltpu.roll(ext, CONV_TAPS - 1 - tap, axis=0)[HALO:, :]
        s = _sigmoid(pre)
        o_ref[...] = pre * s
        ds_ref[...] = _dsilu(pre, s).astype(BF16)

    blk = pl.BlockSpec((tm, D), lambda cb, i: (i, cb))
    return pl.pallas_call(
        body, name="conv_fwd", grid=(CONV_DIM // D, t // tm),
        out_shape=[jax.ShapeDtypeStruct((t, CONV_DIM), F32), jax.ShapeDtypeStruct((t, CONV_DIM), BF16)],
        in_specs=[pl.BlockSpec((tm, D), lambda cb, i: (i, CONV_BLOCK0 + cb)),
                  pl.BlockSpec((HALO, D), lambda cb, i: (jnp.maximum(i * r - 1, 0), CONV_BLOCK0 + cb)),
                  pl.BlockSpec((CONV_TAPS, D), lambda cb, i: (0, cb)), pl.BlockSpec((1, D), lambda cb, i: (0, cb))],
        out_specs=[blk, blk],
        compiler_params=_cparams(("parallel", "parallel")),
    )(proj, proj, conv_w, conv_b)


def conv_bwd(proj, dxs, dbm, dcm, dsilu, conv_w, dproj):
    t = proj.shape[0]
    tm = _tile(t, ROW_TILE)
    r = tm // HALO
    n = t // tm
    last_halo = t // HALO - 1
    x_blocks = B_INNER // D
    assert 2 * GROUP_W == D and CONV_DIM == B_INNER + D

    def body(x_ref, prev_ref, gx_ref, gxn_ref, gb_ref, gbn_ref, gc_ref, gcn_ref, s_ref, snext_ref, w_ref, dp_in_ref,
             dx_ref, dw_ref, db_ref):
        i = pl.program_id(1)
        is_x = pl.program_id(0) < x_blocks

        @pl.when(i == 0)
        def _():
            dw_ref[...] = jnp.zeros_like(dw_ref)
            db_ref[...] = jnp.zeros_like(db_ref)

        d = jnp.where(is_x, gx_ref[...], jnp.concatenate([gb_ref[...], gc_ref[...]], axis=1))
        dnext = jnp.where(is_x, gxn_ref[0:HALO, :], jnp.concatenate([gbn_ref[0:HALO, :], gcn_ref[0:HALO, :]], axis=1))
        dpre = jnp.concatenate([d.astype(F32) * s_ref[...].astype(F32),
                                jnp.where(i < n - 1, dnext.astype(F32) * snext_ref[0:HALO, :].astype(F32),
                                          0.0)], axis=0)
        dx = w_ref[CONV_TAPS - 1:CONV_TAPS, :] * dpre[:tm, :]
        for tap in range(CONV_TAPS - 1):
            back = CONV_TAPS - 1 - tap
            dx = dx + w_ref[tap:tap + 1, :] * pltpu.roll(dpre, tm + HALO - back, axis=0)[:tm, :]
        dx_ref[...] = dx.astype(BF16)
        dp = dpre[:tm, :]
        db_ref[...] += _colsum(dp)
        prev = jnp.where(i > 0, prev_ref[...], 0.0)
        ext = jnp.concatenate([prev, x_ref[...]], axis=0)
        dw_ref[CONV_TAPS - 1:CONV_TAPS, :] += _colsum(dp * ext[HALO:, :])
        for tap in range(CONV_TAPS - 1):
            dw_ref[tap:tap + 1, :] += _colsum(dp * pltpu.roll(ext, CONV_TAPS - 1 - tap, axis=0)[HALO:, :])

    def next_rows(i):
        return jnp.minimum((i + 1) * (r // 2), last_halo // 2)

    blk = pl.BlockSpec((tm, D), lambda cb, i: (i, cb))
    nxt = pl.BlockSpec((2 * HALO, D), lambda cb, i: (next_rows(i), cb))
    xblk = pl.BlockSpec((tm, D), lambda cb, i: (i, jnp.minimum(cb, x_blocks - 1)))
    xnxt = pl.BlockSpec((2 * HALO, D), lambda cb, i: (next_rows(i), jnp.minimum(cb, x_blocks - 1)))
    gblk = pl.BlockSpec((tm, GROUP_W), lambda cb, i: (i, 0))
    gnxt = pl.BlockSpec((2 * HALO, GROUP_W), lambda cb, i: (next_rows(i), 0))
    return pl.pallas_call(
        body, name="conv_bwd", grid=(CONV_DIM // D, n),
        out_shape=[jax.ShapeDtypeStruct(dproj.shape, dproj.dtype), jax.ShapeDtypeStruct((8, CONV_DIM), F32),
                   jax.ShapeDtypeStruct((1, CONV_DIM), F32)],
        in_specs=[pl.BlockSpec((tm, D), lambda cb, i: (i, CONV_BLOCK0 + cb)),
                  pl.BlockSpec((HALO, D), lambda cb, i: (jnp.maximum(i * r - 1, 0), CONV_BLOCK0 + cb)),
                  xblk, xnxt, gblk, gnxt, gblk, gnxt, blk, nxt,
                  pl.BlockSpec((CONV_TAPS, D), lambda cb, i: (0, cb)), pl.BlockSpec(memory_space=pl.ANY)],
        out_specs=[pl.BlockSpec((None, tm, D), lambda cb, i: (CONV_BLOCK0 + cb, i, 0)),
                   pl.BlockSpec((8, D), lambda cb, i: (0, cb)), pl.BlockSpec((1, D), lambda cb, i: (0, cb))],
        input_output_aliases={11: 0},
        compiler_params=_cparams(("parallel", "arbitrary")),
    )(proj, proj, dxs, dxs, dbm, dbm, dcm, dcm, dsilu, dsilu, conv_w, dproj)


def dt_fill(ddt, dproj):
    t = ddt.shape[0]
    tm = _tile(t, WIDE_ROW_TILE)
    w = ddt.shape[1]

    def body(d_ref, dp_in_ref, o_ref):
        o_ref[:, :w] = d_ref[...]
        o_ref[:, w:] = jnp.zeros((tm, D - w), o_ref.dtype)

    return pl.pallas_call(
        body, name="dt_fill", grid=(t // tm,),
        out_shape=jax.ShapeDtypeStruct(dproj.shape, dproj.dtype),
        in_specs=[pl.BlockSpec((tm, w), lambda i: (i, 0)), pl.BlockSpec(memory_space=pl.ANY)],
        out_specs=pl.BlockSpec((None, tm, D), lambda i: (DT_COL_BLOCK, i, 0)),
        input_output_aliases={1: 0},
        compiler_params=_cparams(("parallel",)),
    )(ddt, dproj)


Z_BLOCK0 = 8
DT_COL_BLOCK = 9
DT_BLOCK0 = 8 * DT_COL_BLOCK
GATE_BLOCK0 = 10
B_BLOCK0 = 16
C_BLOCK0 = 20


def _head_expand():
    e = np.zeros((N_STATE, GROUP_W), np.float32)
    for hh in range(HEADS_PER_GROUP):
        e[hh, hh * HEAD_P:(hh + 1) * HEAD_P] = 1.0
    return jnp.asarray(e, BF16)


def _ssd_chunk_terms(dt, bias, alog, expand, tril_f, eye):
    dtb = dt + bias
    delta = jnp.maximum(dtb, 0.0) + jnp.log(1.0 + jnp.exp(-jnp.abs(dtb)))
    ea = jnp.exp(alog)
    a = -ea * delta
    acum = _sel(_nn, a, tril_f, 3, x_first=False)
    delta_e = _sel(_nn, delta, expand, 2)
    acum_e = _sel(_nn, acum, expand, 2)
    acum_t = _sel(_nt, acum, eye, 3, x_first=False)
    return dtb, delta, ea, a, acum, delta_e, acum_e, acum_t


def ssd_fwd(proj, xc, alog4, bias4, dskip4, wnorm, expand):
    t = proj.shape[0]
    tb = _tile(t, TOKEN_BLOCK)
    ncb = tb // SSD_CHUNK

    def body(xs_ref, b_ref, c_ref, dt_ref, z_ref, alog_ref, bias_ref, dsk_ref, wn_ref, e_ref, ob_ref, st_ref, state):
        @pl.when(pl.program_id(1) == 0)
        def _():
            state[...] = jnp.zeros_like(state)

        expand = e_ref[...]
        mask = _tri(SSD_CHUNK)
        tril_f = mask.astype(BF16)
        eye = (lax.broadcasted_iota(jnp.int32, (N_STATE, N_STATE), 0) ==
               lax.broadcasted_iota(jnp.int32, (N_STATE, N_STATE), 1)).astype(BF16)
        alog, bias = alog_ref[0], bias_ref[0]
        d_e = _sel(_nn, jnp.broadcast_to(dsk_ref[0], (8, N_STATE)), expand, 3)[0:1, :]
        wn = wn_ref[...]

        def chunk(c, carry):
            sl = pl.ds(pl.multiple_of(c * SSD_CHUNK, SSD_CHUNK), SSD_CHUNK)
            xs, bm, cm, dt, z = xs_ref[sl, :], b_ref[sl, :], c_ref[sl, :], dt_ref[sl, :], z_ref[sl, :]
            dtb, delta, ea, a, acum, delta_e, acum_e, acum_t = _ssd_chunk_terms(dt, bias, alog, expand, tril_f, eye)
            alast_e = acum_e[SSD_CHUNK - 1:SSD_CHUNK, :]
            xd = xs * delta_e
            xdb = xd.astype(BF16)
            cb_, bb_ = cm.astype(BF16), bm.astype(BF16)
            cbm = _nt(cb_, bb_)
            ys = []
            for hh in range(HEADS_PER_GROUP):
                lh = jnp.where(mask, jnp.exp(jnp.minimum(acum[:, hh:hh + 1] - acum_t[hh:hh + 1, :], 0.0)), 0.0)
                ys.append(_nn((cbm * lh).astype(BF16), xdb[:, hh * HEAD_P:(hh + 1) * HEAD_P]))
            st = state[...]
            st_ref[0, c] = st
            y = jnp.concatenate(ys, axis=1) + _nn(cb_, st.astype(BF16)) * jnp.exp(acum_e) + xs * d_e
            state[...] = st * jnp.exp(alast_e) + _tn(bb_, (xd * jnp.exp(alast_e - acum_e)).astype(BF16))
            yg = y * z * _sigmoid(z)
            ob_ref[sl, :] = (yg * lax.rsqrt(jnp.mean(yg * yg, axis=-1, keepdims=True) + RMS_EPS) * wn).astype(BF16)
            return carry

        lax.fori_loop(0, ncb, chunk, 0, unroll=min(CHUNK_UNROLL, ncb))

    small = pl.BlockSpec((1, 1, N_STATE), lambda g, j: (g, 0, 0))
    return pl.pallas_call(
        body, name="ssd_fwd", grid=(N_GROUPS, t // tb),
        out_shape=[jax.ShapeDtypeStruct((t, B_INNER), BF16),
                   jax.ShapeDtypeStruct((N_GROUPS, t // SSD_CHUNK, N_STATE, GROUP_W), F32)],
        in_specs=[pl.BlockSpec((tb, GROUP_W), lambda g, j: (j, g)),
                  pl.BlockSpec((tb, N_STATE), lambda g, j: (j, B_BLOCK0 + g)),
                  pl.BlockSpec((tb, N_STATE), lambda g, j: (j, C_BLOCK0 + g)),
                  pl.BlockSpec((tb, N_STATE), lambda g, j: (j, DT_BLOCK0 + g)),
                  pl.BlockSpec((tb, GROUP_W), lambda g, j: (j, Z_BLOCK0 + g)),
                  small, small, small, pl.BlockSpec((1, GROUP_W), lambda g, j: (0, g)),
                  pl.BlockSpec((N_STATE, GROUP_W), lambda g, j: (0, 0))],
        out_specs=[pl.BlockSpec((tb, GROUP_W), lambda g, j: (j, g)),
                   pl.BlockSpec((1, ncb, N_STATE, GROUP_W), lambda g, j: (g, j, 0, 0))],
        scratch_shapes=[pltpu.VMEM((N_STATE, GROUP_W), F32)],
        compiler_params=_cparams(("parallel", "arbitrary")),
    )(xc, xc, xc, proj, proj, alog4, bias4, dskip4, wnorm, expand)


def ssd_bwd(proj, xc, alog4, bias4, dskip4, wnorm, expand, dyb, w_bb, states, part, dproj):
    t = proj.shape[0]
    tb = _tile(t, TOKEN_BLOCK)
    lc = min(SSD_CHUNK_BWD, tb)
    ncb = tb // lc
    nsaved = tb // SSD_CHUNK
    nb = t // tb

    def body(xs_ref, b_ref, c_ref, dt_ref, z_ref, alog_ref, bias_ref, dsk_ref, wn_ref, e_ref, dyb_ref, wbb_ref, st_ref,
             part_ref, dp_in_ref, dxs_ref, db_ref, dc_ref, dz_ref, ddt_ref, dwn_ref, dalog_ref, dbias_ref, ddsk_ref,
             parts_ref, dstate, dob_ref, send_sems, recv_sems, local_sem):
        xchg_start, xchg_wait = _chip_exchange(part_ref, parts_ref, send_sems, recv_sems, local_sem)
        dob_ref[...] = _nt(dyb_ref[...], wbb_ref[...])

        @pl.when((pl.program_id(0) == 0) & (pl.program_id(1) == 0))
        def _():
            xchg_start()

        @pl.when(pl.program_id(1) == 0)
        def _():
            dstate[...] = jnp.zeros_like(dstate)
            dwn_ref[...] = jnp.zeros_like(dwn_ref)
            dalog_ref[...] = jnp.zeros_like(dalog_ref)
            dbias_ref[...] = jnp.zeros_like(dbias_ref)
            ddsk_ref[...] = jnp.zeros_like(ddsk_ref)

        expand = e_ref[...]
        mask = _tri(lc)
        mask_t = _tri(lc, upper=True)
        tril_f = mask.astype(BF16)
        triu_f = mask_t.astype(BF16)
        eye = (lax.broadcasted_iota(jnp.int32, (N_STATE, N_STATE), 0) ==
               lax.broadcasted_iota(jnp.int32, (N_STATE, N_STATE), 1)).astype(BF16)
        alog, bias = alog_ref[0], bias_ref[0]
        d_e = _sel(_nn, jnp.broadcast_to(dsk_ref[0], (8, N_STATE)), expand, 3)[0:1, :]
        wn = wn_ref[...]

        def chunk(i, c0):
            c = ncb - 1 - i
            sl = pl.ds(pl.multiple_of(c * lc, lc), lc)
            xs, bm, cm, dt, z = xs_ref[sl, :], b_ref[sl, :], c_ref[sl, :], dt_ref[sl, :], z_ref[sl, :]
            dtb, delta, ea, a, acum, delta_e, acum_e, acum_t = _ssd_chunk_terms(dt, bias, alog, expand, tril_f, eye)
            alast_e = acum_e[lc - 1:lc, :]
            eacum = jnp.exp(acum_e)
            wl = jnp.exp(alast_e - acum_e)
            xd = xs * delta_e
            xdb = xd.astype(BF16)
            cb_, bb_ = cm.astype(BF16), bm.astype(BF16)
            cbm = _nt(cb_, bb_)
            st32 = st_ref[0, c * (lc // SSD_CHUNK)]
            stb = st32.astype(BF16)
            dst = dstate[...]
            dstb = dst.astype(BF16)
            lhs, mixes, ys = [], [], []
            for hh in range(HEADS_PER_GROUP):
                col, row = acum[:, hh:hh + 1], acum_t[hh:hh + 1, :]
                lh = jnp.where(mask, jnp.exp(jnp.minimum(col - row, 0.0)), 0.0)
                mix = (cbm * lh).astype(BF16)
                lhs.append(lh)
                mixes.append(mix)
                ys.append(_nn(mix, xdb[:, hh * HEAD_P:(hh + 1) * HEAD_P]))
            y_in = jnp.concatenate(ys, axis=1)
            y_out = _nn(cb_, stb) * eacum
            y = y_in + y_out + xs * d_e
            sgz = _sigmoid(z)
            sz = z * sgz
            yg = y * sz
            rstd = lax.rsqrt(jnp.mean(yg * yg, axis=-1, keepdims=True) + RMS_EPS)
            nrm = yg * rstd
            dob_v = dob_ref[sl, :]
            dn = dob_v * wn
            dwn_ref[...] += _colsum(dob_v * nrm)
            dyg = rstd * (dn - nrm * jnp.mean(dn * nrm, axis=-1, keepdims=True))
            dy = dyg * sz
            dz_ref[sl, :] = (dyg * y * _dsilu(z, sgz)).astype(BF16)
            dyb = dy.astype(BF16)
            dxds = []
            dcb = jnp.zeros((lc, lc), F32)
            for hh in range(HEADS_PER_GROUP):
                hs = slice(hh * HEAD_P, (hh + 1) * HEAD_P)
                dy_h, x_h = dyb[:, hs], xdb[:, hs]
                dxds.append(_tn(mixes[hh], dy_h))
                dcb = dcb + _nt(dy_h, x_h) * lhs[hh]
            dcbb = dcb.astype(BF16)
            dye = (dy * eacum).astype(BF16)
            xw = (xd * wl).astype(BF16)
            dxd_in = jnp.concatenate(dxds, axis=1)
            dxd_out = wl * _nn(bb_, dstb)
            dxd = dxd_in + dxd_out
            dc_ref[sl, :] = (_nn(dcbb, bb_) + _nt(dye, stb)).astype(dc_ref.dtype)
            db_ref[sl, :] = (_tn(dcbb, cb_) + _nt(xw, dstb)).astype(db_ref.dtype)
            dstate[...] = dst * jnp.exp(alast_e) + _tn(cb_, dye)
            col_out = xd * dxd_out
            dac = _sel(_nt, dyb.astype(F32) * y_in - xdb.astype(F32) * dxd_in + dy * y_out - col_out, expand, 2)
            beyond = _colsum(col_out) + jnp.exp(alast_e) * _colsum(dst * st32)
            da = (_sel(_nn, dac, triu_f, 3, x_first=False) +
                  _sel(_nt, jnp.broadcast_to(beyond, (8, GROUP_W)), expand, 3)[0:1, :])
            ddelta = _sel(_nt, dxd * xs, expand, 2) - da * ea
            dalog_ref[0] += _colsum(da * a)
            ddtb = ddelta * _sigmoid(dtb)
            dbias_ref[0] += _colsum(ddtb)
            ddt_ref[sl, :] = ddtb.astype(BF16)
            ddsk_ref[0] += _sel(_nt, jnp.broadcast_to(_colsum(dy * xs), (8, GROUP_W)), expand, 3)[0:1, :]
            dxs_ref[sl, :] = (dxd * delta_e + dy * d_e).astype(dxs_ref.dtype)
            return c0

        lax.fori_loop(0, ncb, chunk, 0, unroll=min(CHUNK_UNROLL, ncb))

        @pl.when((pl.program_id(0) == N_GROUPS - 1) & (pl.program_id(1) == nb - 1))
        def _():
            xchg_wait()

    small = pl.BlockSpec((1, 1, N_STATE), lambda g, j: (g, 0, 0))
    wide = pl.BlockSpec((tb, GROUP_W), lambda g, j: (nb - 1 - j, g))
    narrow = pl.BlockSpec((tb, N_STATE), lambda g, j: (nb - 1 - j, g))
    hbm = pl.BlockSpec(memory_space=pl.ANY)
    return pl.pallas_call(
        body, name="ssd_bwd", grid=(N_GROUPS, nb),
        out_shape=[jax.ShapeDtypeStruct((t, B_INNER), BF16), jax.ShapeDtypeStruct((t, GROUP_W), BF16),
                   jax.ShapeDtypeStruct((t, GROUP_W), BF16), jax.ShapeDtypeStruct(dproj.shape, dproj.dtype),
                   jax.ShapeDtypeStruct((t, GROUP_W), BF16), jax.ShapeDtypeStruct((1, B_INNER), F32),
                   jax.ShapeDtypeStruct((N_GROUPS, 1, N_STATE), F32), jax.ShapeDtypeStruct((N_GROUPS, 1, N_STATE), F32),
                   jax.ShapeDtypeStruct((N_GROUPS, 1, N_STATE), F32), jax.ShapeDtypeStruct(part.shape, part.dtype)],
        in_specs=[wide,
                  pl.BlockSpec((tb, N_STATE), lambda g, j: (nb - 1 - j, B_BLOCK0 + g)),
                  pl.BlockSpec((tb, N_STATE), lambda g, j: (nb - 1 - j, C_BLOCK0 + g)),
                  pl.BlockSpec((tb, N_STATE), lambda g, j: (nb - 1 - j, DT_BLOCK0 + g)),
                  pl.BlockSpec((tb, GROUP_W), lambda g, j: (nb - 1 - j, Z_BLOCK0 + g)),
                  small, small, small, pl.BlockSpec((1, GROUP_W), lambda g, j: (0, g)),
                  pl.BlockSpec((N_STATE, GROUP_W), lambda g, j: (0, 0)),
                  pl.BlockSpec((tb, D), lambda g, j: (nb - 1 - j, 0)), pl.BlockSpec((GROUP_W, D), lambda g, j: (g, 0)),
                  pl.BlockSpec((1, nsaved, N_STATE, GROUP_W), lambda g, j: (g, nb - 1 - j, 0, 0)), hbm, hbm],
        out_specs=[wide, narrow, narrow,
                   pl.BlockSpec((None, tb, GROUP_W), lambda g, j: (Z_BLOCK0 // 2 + g // 2, nb - 1 - j, g % 2)),
                   narrow, pl.BlockSpec((1, GROUP_W), lambda g, j: (0, g)), small, small, small, hbm],
        input_output_aliases={14: 3},
        scratch_shapes=[pltpu.VMEM((N_STATE, GROUP_W), F32), pltpu.VMEM((tb, GROUP_W), F32)] + CHIP_SEMS,
        compiler_params=_cparams(("arbitrary", "arbitrary")),
    )(xc, xc, xc, proj, proj, alog4, bias4, dskip4, wnorm, expand, dyb, w_bb, states, part, dproj)


def lower_bound_fwd(hgrn_lb):
    def body(a_ref, o_ref):
        a0, a1 = a_ref[0:1, :], a_ref[1:2, :]
        m = jnp.maximum(a0, a1)
        e0, e1 = jnp.exp(a0 - m), jnp.exp(a1 - m)
        o_ref[...] = e0 / (e0 + e1)

    return pl.pallas_call(body, name="lower_bound_fwd", out_shape=jax.ShapeDtypeStruct((1, D), F32))(hgrn_lb)


def ada_weight_grad(c_all, dmod_cols):
    def body(c_ref, d_ref, o_ref):
        cval = c_ref[...]
        o_ref[...] = _tn(cval * _sigmoid(cval), d_ref[...], HI)

    return pl.pallas_call(body, name="ada_weight_grad",
                          out_shape=jax.ShapeDtypeStruct((D, dmod_cols.shape[1]), F32))(c_all, dmod_cols)


def reduce_small(gathered, hgrn_lb, dlb_off):
    n = gathered.shape[2]

    def body(g_ref, a_ref, o_ref, glb_ref):
        s = g_ref[0]
        for d in range(1, N_DEV):
            s = s + g_ref[d]
        o_ref[...] = s
        a0, a1 = a_ref[0:1, :], a_ref[1:2, :]
        m = jnp.maximum(a0, a1)
        e0, e1 = jnp.exp(a0 - m), jnp.exp(a1 - m)
        p0 = e0 / (e0 + e1)
        tq = s[:, dlb_off:dlb_off + D] * p0 * (1.0 - p0)
        glb_ref[0:1, :] = tq
        glb_ref[1:2, :] = -tq

    return pl.pallas_call(body, name="reduce_small",
                          out_shape=[jax.ShapeDtypeStruct((1, n), F32), jax.ShapeDtypeStruct((2, D), F32)])(gathered, hgrn_lb)


def _adam_math(w, g, m, v):
    m2 = ADAM_B1 * m + (1.0 - ADAM_B1) * g
    v2 = ADAM_B2 * v + (1.0 - ADAM_B2) * (g * g)
    m_hat = m2 / (1.0 - ADAM_B1 ** ADAM_STEP)
    v_hat = v2 / (1.0 - ADAM_B2 ** ADAM_STEP)
    delta = -ADAM_LR * (m_hat / (jnp.sqrt(v_hat) + ADAM_EPS) + ADAM_WD * w)
    return delta, m2, v2


def _row_tile(rows, mult=8, cap=128):
    for cand in range(cap - cap % mult, 0, -mult):
        if rows % cand == 0:
            return cand
    return rows


def sum_parts(parts, name):
    n, rows, cols = parts.shape
    tr = _row_tile(rows, 16, 1024)

    def body(p_ref, o_ref):
        s = p_ref[0].astype(F32)
        for d in range(1, n):
            s = s + p_ref[d].astype(F32)
        o_ref[...] = s

    return pl.pallas_call(
        body, name=name, grid=(rows // tr,),
        out_shape=jax.ShapeDtypeStruct((rows, cols), F32),
        in_specs=[pl.BlockSpec((n, tr, cols), lambda i: (0, i, 0))],
        out_specs=pl.BlockSpec((tr, cols), lambda i: (i, 0)),
        compiler_params=_cparams(("parallel",)),
    )(parts)


def sum_pair(a, b, name):
    rows, cols = a.shape
    tr = _row_tile(rows, 16, 1024)

    def body(a_ref, b_ref, o_ref):
        o_ref[...] = (a_ref[...].astype(F32) + b_ref[...].astype(F32)).astype(o_ref.dtype)

    blk = pl.BlockSpec((tr, cols), lambda i: (i, 0))
    return pl.pallas_call(
        body, name=name, grid=(rows // tr,),
        out_shape=jax.ShapeDtypeStruct((rows, cols), a.dtype),
        in_specs=[blk, blk], out_specs=blk,
        compiler_params=_cparams(("parallel",)),
    )(a, b)


def adamw(w, g, m, v, name):
    rows, cols = w.shape
    tr = _row_tile(rows, 8, 256)

    def body(w_ref, g_ref, m_ref, v_ref, d_ref, m2_ref, v2_ref):
        delta, m2, v2 = _adam_math(w_ref[...], g_ref[...], m_ref[...], v_ref[...])
        d_ref[...] = delta
        m2_ref[...] = m2
        v2_ref[...] = v2

    blk = pl.BlockSpec((tr, cols), lambda i: (i, 0))
    return pl.pallas_call(
        body, name=name, grid=(rows // tr,),
        out_shape=[jax.ShapeDtypeStruct((rows, cols), F32)] * 3,
        in_specs=[blk] * 4, out_specs=[blk] * 3,
        compiler_params=_cparams(("parallel",)),
    )(w, g, m, v)


def _pad128(n):
    return -(-n // 128) * 128


def _pack(arrays):
    offs, parts, off = [], [], 0
    for a in arrays:
        flat = a.reshape(1, -1)
        n = flat.shape[1]
        offs.append(off)
        parts.append(jnp.pad(flat, ((0, 0), (0, _pad128(n) - n))))
        off += _pad128(n)
    return jnp.concatenate(parts, axis=1), offs


def _unpack(vec, offs, shapes):
    out = []
    for off, shp in zip(offs, shapes):
        n = int(np.prod(shp))
        out.append(vec[0, off:off + n].reshape(shp))
    return out


IN_ROWS = IN_DIM // N_DEV
DT_ROW0 = 9216
DT_DEV, DT_LO = divmod(DT_ROW0, IN_ROWS)


GATE_SHIFT = D - 32


def _in_row_pieces(tile):
    pieces = []
    if tile == DT_COL_BLOCK:
        for g in range(N_GROUPS):
            o = DT_ROW0 + HEADS_PER_GROUP * g
            pieces.append((N_STATE * g, o // IN_ROWS, o % IN_ROWS, HEADS_PER_GROUP))
        return pieces
    r, end = tile * D, (tile + 1) * D
    while r < end:
        o = r if r < DT_ROW0 else r - GATE_SHIFT
        dev, loc = divmod(o, IN_ROWS)
        n = min(end - r, IN_ROWS - loc)
        pieces.append((r - tile * D, dev, loc, n))
        r += n
    return pieces


def assemble_w_in(g_all):
    ntile = N_PROJ // D

    def body(g_ref, o_ref):
        j = pl.program_id(0)
        for tile in range(ntile):
            @pl.when(j == tile)
            def _(tile=tile):
                if tile == DT_COL_BLOCK:
                    o_ref[...] = jnp.zeros_like(o_ref)
                for dst, dev, loc, n in _in_row_pieces(tile):
                    o_ref[pl.ds(dst, n), :] = g_ref[dev, pl.ds(loc, n), :]

    return pl.pallas_call(
        body, name="assemble_w_in", grid=(ntile,),
        out_shape=jax.ShapeDtypeStruct((N_PROJ, D), g_all.dtype),
        in_specs=[pl.BlockSpec(memory_space=pltpu.VMEM)],
        out_specs=pl.BlockSpec((D, D), lambda j: (j, 0)),
        compiler_params=_cparams(("arbitrary",)),
    )(g_all)


def _grad_in_blocks(g_t, core, slot):
    dt0 = DT_COL_BLOCK * D
    dt = g_t[dt0:dt0 + N_GROUPS * N_STATE].reshape(N_GROUPS, N_STATE, D)[:, :HEADS_PER_GROUP].reshape(32, D)
    with_dt = jnp.concatenate([g_t[DT_DEV * IN_ROWS:DT_ROW0], dt,
                               g_t[DT_ROW0 + 32 + GATE_SHIFT:(DT_DEV + 1) * IN_ROWS + GATE_SHIFT]], axis=0)
    blocks = []
    for q in range(N_CHIP):
        if 2 * q + 1 < DT_DEV:
            blk = lax.dynamic_slice_in_dim(g_t, IN_ROWS * (2 * q + core), IN_ROWS, axis=0)
        else:
            assert 2 * q == DT_DEV
            after = g_t[(DT_DEV + 1) * IN_ROWS + GATE_SHIFT:(DT_DEV + 2) * IN_ROWS + GATE_SHIFT]
            blk = jnp.where(core == 0, with_dt, after)
        blocks.append(jnp.pad(blk, ((0, slot - IN_ROWS), (0, 0))))
    return jnp.stack(blocks)


def kernel(x, c, w_ada, b_ada, w_in, hgrn_lb, hgrn_gnorm, ssm_conv_w, ssm_conv_b, ssm_dt_bias, ssm_a_log, ssm_d, ssm_norm, w_branch_a, w_branch_b, w_o, ln1_g, ln1_b, w_ffn_gate, w_ffn_up, w_ffn_down, ln2_g, ln2_b, loss_target, m_w_ada, m_b_ada, m_w_in, m_hgrn_lb, m_hgrn_gnorm, m_ssm_conv_w, m_ssm_conv_b, m_ssm_dt_bias, m_ssm_a_log, m_ssm_d, m_ssm_norm, m_w_branch_a, m_w_branch_b, m_w_o, m_ln1_g, m_ln1_b, m_w_ffn_gate, m_w_ffn_up, m_w_ffn_down, m_ln2_g, m_ln2_b, v_w_ada, v_b_ada, v_w_in, v_hgrn_lb, v_hgrn_gnorm, v_ssm_conv_w, v_ssm_conv_b, v_ssm_dt_bias, v_ssm_a_log, v_ssm_d, v_ssm_norm, v_w_branch_a, v_w_branch_b, v_w_o, v_ln1_g, v_ln1_b, v_w_ffn_gate, v_w_ffn_up, v_w_ffn_down, v_ln2_g, v_ln2_b):
    me = 4 * lax.axis_index("x") + 2 * lax.axis_index("y") + lax.axis_index("c")
    xt = x[0]
    tgt = loss_target[0]
    t = xt.shape[0]
    ada_cols = w_ada.shape[2]
    conv_cols = ssm_conv_w.shape[2]

    small_in, _ = _pack([c, ssm_conv_w[0]])
    small_all = allgather_vmem(small_in, "allgather_small_inputs")
    c_all = small_all[:, 0, :D]
    conv_w = small_all[:, 0, D:D + CONV_TAPS * conv_cols].reshape(N_DEV, CONV_TAPS, conv_cols)
    conv_w = conv_w.transpose(1, 0, 2).reshape(CONV_TAPS, CONV_DIM)
    mod = ada_modulation(c_all, w_ada[0], b_ada.reshape(N_DEV, 1, ada_cols))
    mod6 = mod.reshape(6, D)

    shards = [w_in[0].T, w_ffn_gate[0].T, w_ffn_up[0].T, w_ffn_down[0], w_branch_a[0], w_branch_b[0], w_o[0]]
    shard_rows = [s.shape[0] for s in shards]
    slot_rows = [-(-r // 32) * 32 for r in shard_rows]
    row_offs = [sum(slot_rows[:i]) for i in range(len(shards))]
    padded = [jnp.pad(s.astype(BF16), ((0, p - r), (0, 0))) for s, r, p in zip(shards, shard_rows, slot_rows)]
    w_in_t = assemble_w_in(allgather_hbm(padded[0], "allgather_w_in"))

    lb = lower_bound_fwd(hgrn_lb)
    u1 = ln_modulate(xt, mod6, 0, 1, "ln_modulate_1")
    proj, g_rest = mm_nt_gather(u1, w_in_t, F32, jnp.concatenate(padded[1:], axis=0), "mm_in_proj")
    assert shard_rows[1:4] == slot_rows[1:4] == [FF_SHARD_ROWS] * 3
    gate_slot, up_slot, dn_slot = 0, 1, 2
    g_ba, g_bb, g_o = (g_rest[:, o - slot_rows[0]:o - slot_rows[0] + r] for o, r in zip(row_offs[4:], shard_rows[4:]))
    w_ba = g_ba.reshape(D, D)
    w_bb = g_bb.reshape(B_INNER, D)
    w_oo = g_o.reshape(D, D)
    o_a, o_raw, st_a = hgrn_fwd(proj, lb, hgrn_gnorm)
    xc, conv_slope = conv_fwd(proj, conv_w, ssm_conv_b)
    pad3 = ((0, 0), (0, 0), (0, N_STATE - HEADS_PER_GROUP))
    alog4 = jnp.pad(ssm_a_log.reshape(N_GROUPS, 1, HEADS_PER_GROUP), pad3)
    bias4 = jnp.pad(ssm_dt_bias.reshape(N_GROUPS, 1, HEADS_PER_GROUP), pad3)
    dskip4 = jnp.pad(ssm_d.reshape(N_GROUPS, 1, HEADS_PER_GROUP), pad3)
    expand = _head_expand()
    o_b, st_b = ssd_fwd(proj, xc, alog4, bias4, dskip4, ssm_norm, expand)
    ya, yb, merged, h1, x1, u2 = mixer_tail(o_a, o_b, w_ba, w_bb, proj, w_oo, xt, mod6, ln1_g, ln1_b)
    gu, act = ffn_in_act(u2, g_rest, gate_slot, up_slot)

    dh2, dx1_part, acc4 = ffn_tail_loss_bwd(x1, act, g_rest, dn_slot, mod6, ln2_g, ln2_b, tgt)
    g_dn = mm_tn(act, dh2, "mm_grad_ffn_down")
    dgu = ffn_act_bwd(dh2, g_rest, dn_slot, gu)
    g_gu_t = mm_tn(dgu, u2, "mm_grad_ffn_in")
    dh1, dx_part, acc2, g_o = mixer_tail_bwd(x1, dgu, g_rest, gate_slot, up_slot, mod6, dx1_part, xt, h1, ln1_g, ln1_b,
                                             merged)
    dya, dyb, dproj, g_ba_full, g_bb_full = merge_gates_bwd(dh1, w_oo, ya, yb, proj, o_a, o_b)
    my_core = lax.axis_index("c")

    def by_core(blocks, rows, slots):
        contrib = jnp.concatenate([jnp.pad(b.reshape(N_DEV, -1, D), ((0, 0), (0, p - r), (0, 0)))
                                   for b, r, p in zip(blocks, rows, slots)], axis=1)
        split = contrib.reshape(N_CHIP, 2, contrib.shape[1], D).transpose(1, 0, 2, 3)
        return (lax.dynamic_index_in_dim(split, my_core, 0, keepdims=False),
                lax.dynamic_index_in_dim(split, 1 - my_core, 0, keepdims=False))

    keep_e, give_e = by_core([g_gu_t[:D_FF], g_gu_t[D_FF:], g_dn, g_ba_full, g_bb_full, g_o],
                             shard_rows[1:], slot_rows[1:])
    dproj, dlb, dgn, got_e = hgrn_bwd(proj, lb, hgrn_gnorm, o_raw, dya, w_ba, st_a, give_e, dproj)
    chip_e = sum_pair(keep_e.reshape(-1, D), got_e.reshape(-1, D), "sum_grads_rest_chip").reshape(keep_e.shape)
    dxs, dbm, dcm, dproj, ddt, dwn, dalog, dbias, ddsk, parts_e = ssd_bwd(proj, xc, alog4, bias4, dskip4, ssm_norm,
                                                                          expand, dyb, w_bb, st_b, chip_e, dproj)
    dproj, dcw, dcb = conv_bwd(proj, dxs, dbm, dcm, conv_slope, conv_w, dproj)
    dproj = dt_fill(ddt, dproj)
    g_in_t = mm_tn(dproj, u1, "mm_grad_in_proj")
    keep_l = _grad_in_blocks(g_in_t, my_core, slot_rows[0])
    give_l = _grad_in_blocks(g_in_t, 1 - my_core, slot_rows[0])
    got_l = exchange_sibling(give_l, "exchange_grad_in_sibling")
    chip_l = sum_pair(keep_l.reshape(-1, D), got_l.reshape(-1, D), "sum_grad_in_chip").reshape(keep_l.shape)
    du1, parts_l = mm_nn_exchange(dproj, w_in_t, F32, chip_l, "mm_du1")
    dx, acc1 = ln_modulate_bwd(xt, du1, mod6, 1, dx_part, "ln_modulate_1_bwd")
    gw_in = sum_parts(parts_l, "sum_grad_in")[:shard_rows[0]].T
    g_rows = sum_parts(parts_e, "sum_grads_rest")
    gw_fg, gw_fu, gw_fd, gw_ba, gw_bb, gw_o = (g_rows[o - slot_rows[0]:o - slot_rows[0] + r]
                                               for o, r in zip(row_offs[1:], shard_rows[1:]))
    gw_fg, gw_fu = gw_fg.T, gw_fu.T

    dmod = jnp.concatenate([acc1[1:2], acc1[0:1], acc2[2:3], acc2[1:2], acc2[0:1], acc4[0:1]], axis=1)
    small_fields = [dmod, acc4[3:4, :128], dlb, dgn, dcw[:CONV_TAPS], dcb, dbias, dalog, ddsk, dwn,
                    acc2[3:4], acc2[4:5], acc4[1:2], acc4[2:3]]
    small_out, offs = _pack(small_fields)
    small_sum_in = allgather_vmem(small_out, "allgather_small_grads")
    gsum, g_lb = reduce_small(small_sum_in, hgrn_lb, offs[2])
    (g_bada, loss_row, _, g_gn, g_cw_full, g_cb, g_bias4, g_alog4, g_dsk4, g_wn, g_l1g, g_l1b, g_l2g, g_l2b) = _unpack(
        gsum, offs, [(1, 6 * D), (1, 128), (1, D), (1, HK), (CONV_TAPS, CONV_DIM), (1, CONV_DIM),
                     (N_GROUPS, N_STATE), (N_GROUPS, N_STATE), (N_GROUPS, N_STATE), (1, B_INNER),
                     (1, D), (1, D), (1, D), (1, D)])
    loss = loss_row[0, 0]
    g_cw = lax.dynamic_slice(g_cw_full, (0, me * conv_cols), (CONV_TAPS, conv_cols))[None]
    g_dtb = g_bias4[:, :HEADS_PER_GROUP].reshape(1, 32)
    g_alog = g_alog4[:, :HEADS_PER_GROUP].reshape(1, 32)
    g_dsk = g_dsk4[:, :HEADS_PER_GROUP].reshape(1, 32)

    dmod_all = small_sum_in[:, 0, offs[0]:offs[0] + 6 * D]
    dmod_cols = lax.dynamic_slice(dmod_all, (0, me * ada_cols), (N_DEV, ada_cols))
    gw_ada = ada_weight_grad(c_all, dmod_cols)

    big = [("ada", w_ada[0], gw_ada, m_w_ada[0], v_w_ada[0]), ("in", w_in[0], gw_in, m_w_in[0], v_w_in[0]),
           ("branch_a", w_branch_a[0], gw_ba, m_w_branch_a[0], v_w_branch_a[0]),
           ("branch_b", w_branch_b[0], gw_bb, m_w_branch_b[0], v_w_branch_b[0]),
           ("o", w_o[0], gw_o, m_w_o[0], v_w_o[0]),
           ("ffn_gate", w_ffn_gate[0], gw_fg, m_w_ffn_gate[0], v_w_ffn_gate[0]),
           ("ffn_up", w_ffn_up[0], gw_fu, m_w_ffn_up[0], v_w_ffn_up[0]),
           ("ffn_down", w_ffn_down[0], gw_fd, m_w_ffn_down[0], v_w_ffn_down[0])]
    big_out = {}
    for nm, w_, g_, m_, v_ in big:
        d_, m2_, v2_ = adamw(w_, g_, m_, v_, "adamw_" + nm)
        big_out[nm] = (g_[None], d_[None], m2_[None], v2_[None])

    small_w = [b_ada, hgrn_lb, hgrn_gnorm, ssm_conv_w, ssm_conv_b, ssm_dt_bias, ssm_a_log, ssm_d, ssm_norm,
               ln1_g, ln1_b, ln2_g, ln2_b]
    small_g = [g_bada, g_lb, g_gn, g_cw, g_cb, g_dtb, g_alog, g_dsk, g_wn, g_l1g, g_l1b, g_l2g, g_l2b]
    small_m = [m_b_ada, m_hgrn_lb, m_hgrn_gnorm, m_ssm_conv_w, m_ssm_conv_b, m_ssm_dt_bias, m_ssm_a_log, m_ssm_d,
               m_ssm_norm, m_ln1_g, m_ln1_b, m_ln2_g, m_ln2_b]
    small_v = [v_b_ada, v_hgrn_lb, v_hgrn_gnorm, v_ssm_conv_w, v_ssm_conv_b, v_ssm_dt_bias, v_ssm_a_log, v_ssm_d,
               v_ssm_norm, v_ln1_g, v_ln1_b, v_ln2_g, v_ln2_b]
    shapes = [a.shape for a in small_w]
    small_g = [g_.reshape(s) for g_, s in zip(small_g, shapes)]
    pw, poffs = _pack(small_w)
    pg, _ = _pack(small_g)
    pm, _ = _pack(small_m)
    pv, _ = _pack(small_v)
    pd, pm2, pv2 = adamw(pw, pg, pm, pv, "adamw_small")
    s_d, s_m, s_v = (_unpack(p, poffs, shapes) for p in (pd, pm2, pv2))
    (sn_bada, sn_lb, sn_gn, sn_cw, sn_cb, sn_dtb, sn_alog, sn_dsk, sn_wn, sn_l1g, sn_l1b, sn_l2g, sn_l2b) = range(13)

    def order(kind):
        sm = [small_g, s_d, s_m, s_v][kind]
        bg = lambda nm: big_out[nm][kind]
        return [bg("ada"), sm[sn_bada], bg("in"), sm[sn_lb], sm[sn_gn], sm[sn_cw], sm[sn_cb], sm[sn_dtb], sm[sn_alog],
                sm[sn_dsk], sm[sn_wn], bg("branch_a"), bg("branch_b"), bg("o"), sm[sn_l1g], sm[sn_l1b],
                bg("ffn_gate"), bg("ffn_up"), bg("ffn_down"), sm[sn_l2g], sm[sn_l2b]]

    return (loss, dx[None], *order(0), *order(1), *order(2), *order(3))
```
